```python
import jax, jax.numpy as jnp
from jax import lax
import numpy as np

D_MODEL = 1024
BATCH = 8
SEQ = 4096
DEPTH = 1

GRID_W = 64
CTX_LEN = 256
D_MIX = D_MODEL
D_ATTN = D_MIX // 2
D_POOL = D_MIX - D_ATTN
MLA_HEADS = 4
QK_NOPE_DIM = 128
QK_ROPE_DIM = 64
QK_HEAD_DIM = QK_NOPE_DIM + QK_ROPE_DIM
V_HEAD_DIM = D_ATTN // MLA_HEADS
Q_LORA_RANK = 256
KV_LORA_RANK = 128
ROPE_AXIS_DIM = QK_ROPE_DIM // 2
ROPE_BASE = 10000.0
POOL_WINDOWS = (2, 4, 8, 16)
POOL_GROUPS = len(POOL_WINDOWS)
POOL_GROUP_DIM = D_POOL // POOL_GROUPS
Q_BLOCK = 128
NORM_EPS = 1e-6

OFF_CQ = 0
OFF_CKV = OFF_CQ + Q_LORA_RANK
OFF_KR = OFF_CKV + KV_LORA_RANK
OFF_GA = OFF_KR + QK_ROPE_DIM
OFF_PIN = OFF_GA + D_ATTN
OFF_GP = OFF_PIN + D_POOL
D_IN_PROJ = OFF_GP + D_POOL

kernel_name = "hymba_mla_pool_adaln_prefix"


def _rms(x, g):
    xf = x.astype(jnp.float32)
    y = xf * lax.rsqrt(jnp.mean(xf * xf, axis=-1, keepdims=True) + NORM_EPS)
    return (y * g.astype(jnp.float32)).astype(x.dtype)


def _rotate_half(x):
    x1, x2 = jnp.split(x, 2, axis=-1)
    return jnp.concatenate([-x2, x1], axis=-1)


def _axial_rope_tables(L):
    rows = L // GRID_W
    row = jnp.repeat(jnp.arange(rows, dtype=jnp.float32), GRID_W)
    col = jnp.tile(jnp.arange(GRID_W, dtype=jnp.float32), rows)
    n_freq = ROPE_AXIS_DIM // 2
    inv = ROPE_BASE ** (-jnp.arange(n_freq, dtype=jnp.float32) / n_freq)
    ang_r = row[:, None] * inv
    ang_c = col[:, None] * inv
    ang = jnp.concatenate([ang_r, ang_r, ang_c, ang_c], axis=-1)
    return jnp.cos(ang), jnp.sin(ang)


def _apply_axial_rope(x, cos, sin):
    xr, xc = jnp.split(x, 2, axis=-1)
    rot = jnp.concatenate([_rotate_half(xr), _rotate_half(xc)], axis=-1)
    c = cos[:, None, :]
    s = sin[:, None, :]
    return (x.astype(jnp.float32) * c + rot.astype(jnp.float32) * s).astype(x.dtype)


def _mla_qkv(u, q_lora_g, w_uq, kv_lora_g, w_ukv, q_norm_g, k_norm_g, rope):
    B, L = u.shape[0], u.shape[1]
    cq = u[..., OFF_CQ:OFF_CKV]
    ckv = u[..., OFF_CKV:OFF_KR]
    k_rope = u[..., OFF_KR:OFF_GA]
    q = (_rms(cq, q_lora_g) @ w_uq).reshape(B, L, MLA_HEADS, QK_HEAD_DIM)
    kv = (_rms(ckv, kv_lora_g) @ w_ukv).reshape(B, L, MLA_HEADS, QK_NOPE_DIM + V_HEAD_DIM)
    k_nope, v = kv[..., :QK_NOPE_DIM], kv[..., QK_NOPE_DIM:]
    k = jnp.concatenate(
        [k_nope, jnp.broadcast_to(k_rope[:, :, None, :], (B, L, MLA_HEADS, QK_ROPE_DIM))], axis=-1)
    q = _rms(q, q_norm_g)
    k = _rms(k, k_norm_g)
    if rope is not None:
        cos, sin = rope
        q = jnp.concatenate([q[..., :QK_NOPE_DIM], _apply_axial_rope(q[..., QK_NOPE_DIM:], cos, sin)], axis=-1)
        k = jnp.concatenate([k[..., :QK_NOPE_DIM], _apply_axial_rope(k[..., QK_NOPE_DIM:], cos, sin)], axis=-1)
    tr = lambda t: jnp.transpose(t, (0, 2, 1, 3))
    return tr(q), tr(k), tr(v)


def _attend(q, k, v):
    B, H, Lq, dk = q.shape
    nb = Lq // Q_BLOCK
    scale = QK_HEAD_DIM ** -0.5
    qb = jnp.transpose(q.reshape(B, H, nb, Q_BLOCK, dk), (2, 0, 1, 3, 4))

    def one(qblk):
        s = jnp.einsum('bhqd,bhkd->bhqk', qblk, k).astype(jnp.float32) * scale
        p = jax.nn.softmax(s, axis=-1)
        return jnp.einsum('bhqk,bhkd->bhqd', p.astype(v.dtype), v)

    o = lax.map(one, qb)
    o = jnp.transpose(o, (1, 3, 0, 2, 4)).reshape(B, Lq, H * v.shape[-1])
    return o


def _multiscale_pool(u, w_pool, pool_scale):
    B, L, _ = u.shape
    ug = u.reshape(B, L, POOL_GROUPS, POOL_GROUP_DIM)
    cs = jnp.concatenate(
        [jnp.zeros((B, 1, POOL_GROUPS, POOL_GROUP_DIM), jnp.float32),
         jnp.cumsum(ug.astype(jnp.float32), axis=1)], axis=1)
    t = jnp.arange(L, dtype=jnp.int32)[:, None]
    w = jnp.array(POOL_WINDOWS, dtype=jnp.int32)[None, :]
    lo = jnp.clip(t - w // 2, 0, L)
    hi = jnp.clip(t - w // 2 + w, 0, L)
    g = jnp.arange(POOL_GROUPS, dtype=jnp.int32)[None, :]
    win_sum = cs[:, hi, g, :] - cs[:, lo, g, :]
    cnt = (hi - lo).astype(jnp.float32)[None, :, :, None]
    pooled = (win_sum / cnt - ug.astype(jnp.float32)).astype(u.dtype)
    y = jnp.einsum('blgc,gcd->blgd', pooled, w_pool).reshape(B, L, D_POOL)
    return y * pool_scale


def _branches(u, mla_out, w_pool, pool_scale):
    gate_a = u[..., OFF_GA:OFF_PIN]
    pool_in = u[..., OFF_PIN:OFF_GP]
    gate_p = u[..., OFF_GP:D_IN_PROJ]
    br_a = jax.nn.silu(gate_a) * mla_out
    br_p = jax.nn.silu(gate_p) * _multiscale_pool(pool_in, w_pool, pool_scale)
    return jnp.concatenate([br_a, br_p], axis=-1)


def _fwd_setup_inputs(seed: int = 0) -> dict:
    key = jax.random.key(seed)
    ks = jax.random.split(key, 20)
    f32 = jnp.float32
    nrm = lambda k, shape, s: jax.random.normal(k, shape, f32) * s
    return {
        "x": nrm(ks[0], (BATCH, SEQ, D_MODEL), 1.0),
        "c": nrm(ks[1], (BATCH, D_MODEL), 1.0),
        "ctx": nrm(ks[2], (BATCH, CTX_LEN, D_MODEL), 1.0),
        "c_ctx": nrm(ks[3], (D_MODEL,), 1.0),
        "w_mod": nrm(ks[4], (DEPTH, D_MODEL, 3 * D_MODEL), 0.5 * D_MODEL ** -0.5),
        "b_mod": nrm(ks[5], (DEPTH, 3 * D_MODEL), 0.02),
        "norm_g": 1.0 + nrm(ks[6], (DEPTH, D_MODEL), 0.1),
        "w_in": nrm(ks[7], (DEPTH, D_MODEL, D_IN_PROJ), D_MODEL ** -0.5),
        "q_lora_g": 1.0 + nrm(ks[8], (DEPTH, Q_LORA_RANK), 0.1),
        "w_uq": nrm(ks[9], (DEPTH, Q_LORA_RANK, MLA_HEADS * QK_HEAD_DIM), Q_LORA_RANK ** -0.5),
        "kv_lora_g": 1.0 + nrm(ks[10], (DEPTH, KV_LORA_RANK), 0.1),
        "w_ukv": nrm(ks[11], (DEPTH, KV_LORA_RANK, MLA_HEADS * (QK_NOPE_DIM + V_HEAD_DIM)), KV_LORA_RANK ** -0.5),
        "q_norm_g": 1.0 + nrm(ks[12], (DEPTH, QK_HEAD_DIM), 0.1),
        "k_norm_g": 1.0 + nrm(ks[13], (DEPTH, QK_HEAD_DIM), 0.1),
        "w_pool": nrm(ks[14], (DEPTH, POOL_GROUPS, POOL_GROUP_DIM, POOL_GROUP_DIM), POOL_GROUP_DIM ** -0.5),
        "pool_scale": 1.0 + nrm(ks[15], (DEPTH, D_POOL), 0.1),
        "w_out": nrm(ks[16], (DEPTH, D_MIX, D_MODEL), D_MIX ** -0.5),
    }


def _fwd_reference(x, c, ctx, c_ctx, w_mod, b_mod, norm_g, w_in, q_lora_g, w_uq, kv_lora_g, w_ukv,
              q_norm_g, k_norm_g, w_pool, pool_scale, w_out):
    L = x.shape[1]
    rope = _axial_rope_tables(L)
    for l in range(DEPTH):
        mod = jax.nn.silu(c) @ w_mod[l] + b_mod[l]
        shift, scale, gate = jnp.split(mod, 3, axis=-1)
        mod_c = jax.nn.silu(c_ctx) @ w_mod[l] + b_mod[l]
        shift_c, scale_c, gate_c = jnp.split(mod_c, 3, axis=-1)

        h = _rms(x, norm_g[l]) * (1.0 + scale[:, None, :]) + shift[:, None, :]
        hc = _rms(ctx, norm_g[l]) * (1.0 + scale_c) + shift_c
        u = h @ w_in[l]
        uc = hc @ w_in[l]

        q, k, v = _mla_qkv(u, q_lora_g[l], w_uq[l], kv_lora_g[l], w_ukv[l],
                           q_norm_g[l], k_norm_g[l], rope)
        qc, kc, vc = _mla_qkv(uc, q_lora_g[l], w_uq[l], kv_lora_g[l], w_ukv[l],
                              q_norm_g[l], k_norm_g[l], None)
        k_all = jnp.concatenate([kc, k], axis=2)
        v_all = jnp.concatenate([vc, v], axis=2)
        attn = _attend(q, k_all, v_all)
        y = _branches(u, attn, w_pool[l], pool_scale[l]) @ w_out[l]
        x_new = x + gate[:, None, :] * y

        if l < DEPTH - 1:
            attn_c = _attend(qc, kc, vc)
            yc = _branches(uc, attn_c, w_pool[l], pool_scale[l]) @ w_out[l]
            ctx = ctx + gate_c * yc
        x = x_new
    return x


import jax as _jax
import jax.numpy as _jnp

TWIN_FORMAT = 'train_step'
FWD_PARAMS = ['x', 'c', 'ctx', 'c_ctx', 'w_mod', 'b_mod', 'norm_g', 'w_in', 'q_lora_g', 'w_uq', 'kv_lora_g', 'w_ukv', 'q_norm_g', 'k_norm_g', 'w_pool', 'pool_scale', 'w_out']
TWIN_WEIGHTS = ['c_ctx', 'w_mod', 'b_mod', 'norm_g', 'w_in', 'q_lora_g', 'w_uq', 'kv_lora_g', 'w_ukv', 'q_norm_g', 'k_norm_g', 'w_pool', 'pool_scale', 'w_out']
TWIN_DIFF_INPUT = 'x'
TWIN_INPUTS = ['x', 'c', 'ctx', 'c_ctx', 'w_mod', 'b_mod', 'norm_g', 'w_in', 'q_lora_g', 'w_uq', 'kv_lora_g', 'w_ukv', 'q_norm_g', 'k_norm_g', 'w_pool', 'pool_scale', 'w_out', 'loss_target', 'm_c_ctx', 'm_w_mod', 'm_b_mod', 'm_norm_g', 'm_w_in', 'm_q_lora_g', 'm_w_uq', 'm_kv_lora_g', 'm_w_ukv', 'm_q_norm_g', 'm_k_norm_g', 'm_w_pool', 'm_pool_scale', 'm_w_out', 'v_c_ctx', 'v_w_mod', 'v_b_mod', 'v_norm_g', 'v_w_in', 'v_q_lora_g', 'v_w_uq', 'v_kv_lora_g', 'v_w_ukv', 'v_q_norm_g', 'v_k_norm_g', 'v_w_pool', 'v_pool_scale', 'v_w_out']
TWIN_OUTPUTS = ['loss', 'grad_x', 'grad_c_ctx', 'grad_w_mod', 'grad_b_mod', 'grad_norm_g', 'grad_w_in', 'grad_q_lora_g', 'grad_w_uq', 'grad_kv_lora_g', 'grad_w_ukv', 'grad_q_norm_g', 'grad_k_norm_g', 'grad_w_pool', 'grad_pool_scale', 'grad_w_out', 'delta_c_ctx', 'delta_w_mod', 'delta_b_mod', 'delta_norm_g', 'delta_w_in', 'delta_q_lora_g', 'delta_w_uq', 'delta_kv_lora_g', 'delta_w_ukv', 'delta_q_norm_g', 'delta_k_norm_g', 'delta_w_pool', 'delta_pool_scale', 'delta_w_out', 'new_m_c_ctx', 'new_m_w_mod', 'new_m_b_mod', 'new_m_norm_g', 'new_m_w_in', 'new_m_q_lora_g', 'new_m_w_uq', 'new_m_kv_lora_g', 'new_m_w_ukv', 'new_m_q_norm_g', 'new_m_k_norm_g', 'new_m_w_pool', 'new_m_pool_scale', 'new_m_w_out', 'new_v_c_ctx', 'new_v_w_mod', 'new_v_b_mod', 'new_v_norm_g', 'new_v_w_in', 'new_v_q_lora_g', 'new_v_w_uq', 'new_v_kv_lora_g', 'new_v_w_ukv', 'new_v_q_norm_g', 'new_v_k_norm_g', 'new_v_w_pool', 'new_v_pool_scale', 'new_v_w_out']
TWIN_LEAF_KINDS = {'loss': 'loss', 'grad_x': 'grad_x', 'grad_c_ctx': 'grad_w', 'grad_w_mod': 'grad_w', 'grad_b_mod': 'grad_w', 'grad_norm_g': 'grad_w', 'grad_w_in': 'grad_w', 'grad_q_lora_g': 'grad_w', 'grad_w_uq': 'grad_w', 'grad_kv_lora_g': 'grad_w', 'grad_w_ukv': 'grad_w', 'grad_q_norm_g': 'grad_w', 'grad_k_norm_g': 'grad_w', 'grad_w_pool': 'grad_w', 'grad_pool_scale': 'grad_w', 'grad_w_out': 'grad_w', 'delta_c_ctx': 'delta_w', 'delta_w_mod': 'delta_w', 'delta_b_mod': 'delta_w', 'delta_norm_g': 'delta_w', 'delta_w_in': 'delta_w', 'delta_q_lora_g': 'delta_w', 'delta_w_uq': 'delta_w', 'delta_kv_lora_g': 'delta_w', 'delta_w_ukv': 'delta_w', 'delta_q_norm_g': 'delta_w', 'delta_k_norm_g': 'delta_w', 'delta_w_pool': 'delta_w', 'delta_pool_scale': 'delta_w', 'delta_w_out': 'delta_w', 'new_m_c_ctx': 'new_m', 'new_m_w_mod': 'new_m', 'new_m_b_mod': 'new_m', 'new_m_norm_g': 'new_m', 'new_m_w_in': 'new_m', 'new_m_q_lora_g': 'new_m', 'new_m_w_uq': 'new_m', 'new_m_kv_lora_g': 'new_m', 'new_m_w_ukv': 'new_m', 'new_m_q_norm_g': 'new_m', 'new_m_k_norm_g': 'new_m', 'new_m_w_pool': 'new_m', 'new_m_pool_scale': 'new_m', 'new_m_w_out': 'new_m', 'new_v_c_ctx': 'new_v', 'new_v_w_mod': 'new_v', 'new_v_b_mod': 'new_v', 'new_v_norm_g': 'new_v', 'new_v_w_in': 'new_v', 'new_v_q_lora_g': 'new_v', 'new_v_w_uq': 'new_v', 'new_v_kv_lora_g': 'new_v', 'new_v_w_ukv': 'new_v', 'new_v_q_norm_g': 'new_v', 'new_v_k_norm_g': 'new_v', 'new_v_w_pool': 'new_v', 'new_v_pool_scale': 'new_v', 'new_v_w_out': 'new_v'}


def _forward(args):
    return _fwd_reference(*[args[k] for k in FWD_PARAMS])


def _output_shape():
    out = _jax.eval_shape(lambda: _forward(_fwd_setup_inputs(0)))
    return out.shape, out.dtype

N_MICROBATCH = 1
ADAM_LR = 0.001
ADAM_B1 = 0.9
ADAM_B2 = 0.999
ADAM_EPS = 1e-08
ADAM_WD = 0.01
ADAM_STEP = 10
PER_EXAMPLE_BATCH_AXIS = {'x': 0, 'c': 0, 'ctx': 0, 'loss_target': 0}
SHARED_INPUTS = []
_WEIGHT_DTYPES = {'c_ctx': _jnp.float32, 'w_mod': _jnp.float32, 'b_mod': _jnp.float32, 'norm_g': _jnp.float32, 'w_in': _jnp.float32, 'q_lora_g': _jnp.float32, 'w_uq': _jnp.float32, 'kv_lora_g': _jnp.float32, 'w_ukv': _jnp.float32, 'q_norm_g': _jnp.float32, 'k_norm_g': _jnp.float32, 'w_pool': _jnp.float32, 'pool_scale': _jnp.float32, 'w_out': _jnp.float32}
MOMENT_SCALE = {'c_ctx': 1.024069e-02, 'w_mod': 4.221215e-01, 'b_mod': 9.366542e-01, 'norm_g': 1.208513e+00, 'w_in': 7.674466e-02, 'q_lora_g': 8.215538e-03, 'w_uq': 4.667472e-03, 'kv_lora_g': 4.328014e-01, 'w_ukv': 2.824430e-02, 'q_norm_g': 1.280476e-02, 'k_norm_g': 1.211437e-02, 'w_pool': 9.043938e-02, 'pool_scale': 1.170099e+00, 'w_out': 4.387642e-02}


def _to_microbatches(a, axis):
    t = _jnp.moveaxis(a, axis, 0)
    t = t.reshape((N_MICROBATCH, t.shape[0] // N_MICROBATCH) + t.shape[1:])
    return _jnp.moveaxis(t, 1, axis + 1)


def setup_inputs(seed: int = 0) -> dict:
    inp = _fwd_setup_inputs(seed)
    key = _jax.random.fold_in(_jax.random.key(seed), 7919)
    shape, _ = _output_shape()
    out = dict(inp)
    out["loss_target"] = _jax.random.normal(_jax.random.fold_in(key, 0), shape, _jnp.float32)
    for i, name in enumerate(TWIN_WEIGHTS):
        w = inp[name].astype(_jnp.float32)
        if MOMENT_SCALE is None:
            s = _jnp.sqrt(_jnp.mean(_jnp.square(w)) + 1e-30)
        else:
            s = MOMENT_SCALE[name]
        km, kv = _jax.random.split(_jax.random.fold_in(key, i + 1))
        out[name] = w
        out["m_" + name] = s * _jax.random.normal(km, w.shape, _jnp.float32)
        out["v_" + name] = (s * s) * _jax.random.uniform(kv, w.shape, _jnp.float32, 0.5, 1.5)
    if N_MICROBATCH > 1:
        for name, axis in PER_EXAMPLE_BATCH_AXIS.items():
            out[name] = _to_microbatches(out[name], axis)
    return {'x': out['x'], 'c': out['c'], 'ctx': out['ctx'], 'c_ctx': out['c_ctx'], 'w_mod': out['w_mod'], 'b_mod': out['b_mod'], 'norm_g': out['norm_g'], 'w_in': out['w_in'], 'q_lora_g': out['q_lora_g'], 'w_uq': out['w_uq'], 'kv_lora_g': out['kv_lora_g'], 'w_ukv': out['w_ukv'], 'q_norm_g': out['q_norm_g'], 'k_norm_g': out['k_norm_g'], 'w_pool': out['w_pool'], 'pool_scale': out['pool_scale'], 'w_out': out['w_out'], 'loss_target': out['loss_target'], 'm_c_ctx': out['m_c_ctx'], 'm_w_mod': out['m_w_mod'], 'm_b_mod': out['m_b_mod'], 'm_norm_g': out['m_norm_g'], 'm_w_in': out['m_w_in'], 'm_q_lora_g': out['m_q_lora_g'], 'm_w_uq': out['m_w_uq'], 'm_kv_lora_g': out['m_kv_lora_g'], 'm_w_ukv': out['m_w_ukv'], 'm_q_norm_g': out['m_q_norm_g'], 'm_k_norm_g': out['m_k_norm_g'], 'm_w_pool': out['m_w_pool'], 'm_pool_scale': out['m_pool_scale'], 'm_w_out': out['m_w_out'], 'v_c_ctx': out['v_c_ctx'], 'v_w_mod': out['v_w_mod'], 'v_b_mod': out['v_b_mod'], 'v_norm_g': out['v_norm_g'], 'v_w_in': out['v_w_in'], 'v_q_lora_g': out['v_q_lora_g'], 'v_w_uq': out['v_w_uq'], 'v_kv_lora_g': out['v_kv_lora_g'], 'v_w_ukv': out['v_w_ukv'], 'v_q_norm_g': out['v_q_norm_g'], 'v_k_norm_g': out['v_k_norm_g'], 'v_w_pool': out['v_w_pool'], 'v_pool_scale': out['v_pool_scale'], 'v_w_out': out['v_w_out']}


def _loss(weights, diff, rest, loss_target):
    with _jax.named_scope("forward"):
        args = {**rest, TWIN_DIFF_INPUT: diff, **{k: w.astype(_WEIGHT_DTYPES[k]) for k, w in weights.items()}}
        y = _forward(args)
    with _jax.named_scope("loss_head"):
        err = _jnp.square(y.astype(_jnp.float32) - loss_target)
        return 0.5 * _jnp.sum(_jnp.mean(err, axis=-1)) if err.ndim else 0.5 * err


def _adamw(w, g, m, v):
    m = ADAM_B1 * m + (1.0 - ADAM_B1) * g
    v = ADAM_B2 * v + (1.0 - ADAM_B2) * _jnp.square(g)
    m_hat = m / (1.0 - ADAM_B1 ** ADAM_STEP)
    v_hat = v / (1.0 - ADAM_B2 ** ADAM_STEP)
    delta = -ADAM_LR * (m_hat / (_jnp.sqrt(v_hat) + ADAM_EPS) + ADAM_WD * w)
    return delta, m, v


def reference(x, c, ctx, c_ctx, w_mod, b_mod, norm_g, w_in, q_lora_g, w_uq, kv_lora_g, w_ukv, q_norm_g, k_norm_g, w_pool, pool_scale, w_out, loss_target, m_c_ctx, m_w_mod, m_b_mod, m_norm_g, m_w_in, m_q_lora_g, m_w_uq, m_kv_lora_g, m_w_ukv, m_q_norm_g, m_k_norm_g, m_w_pool, m_pool_scale, m_w_out, v_c_ctx, v_w_mod, v_b_mod, v_norm_g, v_w_in, v_q_lora_g, v_w_uq, v_kv_lora_g, v_w_ukv, v_q_norm_g, v_k_norm_g, v_w_pool, v_pool_scale, v_w_out):
    given = dict(x=x, c=c, ctx=ctx, c_ctx=c_ctx, w_mod=w_mod, b_mod=b_mod, norm_g=norm_g, w_in=w_in, q_lora_g=q_lora_g, w_uq=w_uq, kv_lora_g=kv_lora_g, w_ukv=w_ukv, q_norm_g=q_norm_g, k_norm_g=k_norm_g, w_pool=w_pool, pool_scale=pool_scale, w_out=w_out, loss_target=loss_target, m_c_ctx=m_c_ctx, m_w_mod=m_w_mod, m_b_mod=m_b_mod, m_norm_g=m_norm_g, m_w_in=m_w_in, m_q_lora_g=m_q_lora_g, m_w_uq=m_w_uq, m_kv_lora_g=m_kv_lora_g, m_w_ukv=m_w_ukv, m_q_norm_g=m_q_norm_g, m_k_norm_g=m_k_norm_g, m_w_pool=m_w_pool, m_pool_scale=m_pool_scale, m_w_out=m_w_out, v_c_ctx=v_c_ctx, v_w_mod=v_w_mod, v_b_mod=v_b_mod, v_norm_g=v_norm_g, v_w_in=v_w_in, v_q_lora_g=v_q_lora_g, v_w_uq=v_w_uq, v_kv_lora_g=v_kv_lora_g, v_w_ukv=v_w_ukv, v_q_norm_g=v_q_norm_g, v_k_norm_g=v_k_norm_g, v_w_pool=v_w_pool, v_pool_scale=v_pool_scale, v_w_out=v_w_out)
    weights = {n: given[n] for n in TWIN_WEIGHTS}
    shared = {n: given[n] for n in SHARED_INPUTS}
    per_example = {n: given[n] for n in ['x', 'c', 'ctx']}
    grad_fn = _jax.value_and_grad(_loss, argnums=(0, 1))

    def one_microbatch(ex, loss_target):
        ex = dict(ex)
        diff = ex.pop(TWIN_DIFF_INPUT)
        return grad_fn(weights, diff, {**shared, **ex}, loss_target)

    if N_MICROBATCH == 1:
        loss, (grad_w, grad_x) = one_microbatch(per_example, given["loss_target"])
    else:
        def body(carry, xs):
            loss_sum, grad_sum = carry
            l_k, (gw_k, gx_k) = one_microbatch(xs[0], xs[1])
            with _jax.named_scope("update"):
                return (loss_sum + l_k, _jax.tree.map(_jnp.add, grad_sum, gw_k)), gx_k

        init = (_jnp.zeros((), _jnp.float32), _jax.tree.map(_jnp.zeros_like, weights))
        (loss, grad_w), grad_x = _jax.lax.scan(body, init, (per_example, given["loss_target"]))
    with _jax.named_scope("update"):
        delta_w, new_m, new_v = {}, {}, {}
        for n in TWIN_WEIGHTS:
            delta_w[n], new_m[n], new_v[n] = _adamw(weights[n], grad_w[n], given["m_" + n], given["v_" + n])
    return (loss, grad_x, *[grad_w[n] for n in TWIN_WEIGHTS], *[delta_w[n] for n in TWIN_WEIGHTS],
            *[new_m[n] for n in TWIN_WEIGHTS], *[new_v[n] for n in TWIN_WEIGHTS])
```

```python
import functools

import jax
import jax.numpy as jnp
from jax import lax
from jax.experimental import pallas as pl
from jax.experimental.pallas import tpu as pltpu

F32 = jnp.float32
BF16 = jnp.bfloat16
MESH = pl.DeviceIdType.MESH

D = 1024
NH = 4
DK = 192
DKP = 256
DV = 128
QL = 256
KVL = 128
DU = 2048
U_KR_END = 448
POOL_WINDOWS = (2, 4, 8, 16)
HALO = 8
EPS = 1e-6
ROPE_BASE = 10000.0
GRID_W = 64
Q_BLOCK = 128
TB = 256
SCALE = DK ** -0.5
VMEM_LIMIT = 56 * 1024 * 1024

ADAM_LR = 0.001
ADAM_B1 = 0.9
ADAM_B2 = 0.999
ADAM_EPS = 1e-08
ADAM_WD = 0.01
ADAM_STEP = 10

CHIPS3 = ((1, 0), (0, 1), (1, 1))
PEERS7 = tuple((dx, dy, dc) for dx in (0, 1) for dy in (0, 1) for dc in (0, 1) if (dx, dy, dc) != (0, 0, 0))


def _nn(a, b):
    return jnp.dot(a, b, preferred_element_type=F32)


def _nt(a, b):
    return lax.dot_general(a, b, (((1,), (1,)), ((), ())), preferred_element_type=F32)


def _tn(a, b):
    return lax.dot_general(a, b, (((0,), (0,)), ((), ())), preferred_element_type=F32)


def _split3(a):
    a0 = a.astype(BF16)
    r = a - a0.astype(F32)
    a1 = r.astype(BF16)
    a2 = (r - a1.astype(F32)).astype(BF16)
    return a0, a1, a2


def _dot3(dot, a, b):
    sa = _split3(a)
    sb = _split3(b)
    out = None
    for i in range(3):
        for j in range(3 - i):
            t = dot(sa[i], sb[j])
            out = t if out is None else out + t
    return out


def _sig(x):
    return 1.0 / (1.0 + jnp.exp(-x))


def _rot(t):
    lane = lax.broadcasted_iota(jnp.int32, t.shape, 1)
    return jnp.where((lane % 32) < 16, -pltpu.roll(t, 112, 1), pltpu.roll(t, 16, 1))


def _rope(t, cos, sin):
    return t * cos + _rot(t) * sin


def _rope_t(t, cos, sin):
    return t * cos - _rot(t * sin)


def _shift_rows(z, k):
    n = z.shape[0]
    return pltpu.roll(z, (n - k) % n, 0)


def _colsum(a):
    return jnp.sum(a, axis=0, keepdims=True)


def _rowsum(a):
    return jnp.sum(a, axis=-1, keepdims=True)


def _row_layout(col):
    return jnp.transpose(jnp.broadcast_to(col, (col.shape[0], 128)))[0:8, :]


def _params(sem=None):
    return pltpu.CompilerParams(dimension_semantics=sem, vmem_limit_bytes=VMEM_LIMIT)


def _full(shape):
    nd = len(shape)
    return pl.BlockSpec(shape, lambda *_: (0,) * nd)


def _peer(x, y, c, off):
    dx, dy, dc = off
    return ((x + dx) % 2, (y + dy) % 2, (c + dc) % 2)


def _modulated(x, mod_ref, ng):
    shift = mod_ref[0, 0:1, :]
    scale = mod_ref[0, 1:2, :]
    r = lax.rsqrt(jnp.mean(x * x, axis=-1, keepdims=True) + EPS)
    xh = x * r
    xg = xh * ng
    return r, xh, xg, xg * (1.0 + scale) + shift, scale


def _fwd_in(xall, modsel, norm_g, w_in, q_lora_g, w_uq, kv_lora_g, w_ukv, qn_g, kn_g, cos, sin, s_len):
    t_all = xall.shape[0]
    nb = t_all // TB
    off = (t_all - s_len) // TB

    def body(x_ref, mod_ref, ng_ref, win_ref, qlg_ref, wuq_ref, kvlg_ref, wukv_ref, qng_ref, kng_ref, cos_ref, sin_ref,
             u_ref, q_ref, k_ref, v_ref):
        _, _, _, h, _ = _modulated(x_ref[...], mod_ref, ng_ref[...])
        u = _nn(h.astype(BF16), win_ref[...])
        u_ref[...] = u
        cos = cos_ref[...]
        sin = sin_ref[...]
        cq = u[:, 0:QL]
        cqn = cq * lax.rsqrt(jnp.mean(cq * cq, axis=-1, keepdims=True) + EPS) * qlg_ref[...]
        qraw = _nn(cqn.astype(BF16), wuq_ref[...])
        qng = qng_ref[...]
        for hd in range(NH):
            qh = qraw[:, hd * DKP:(hd + 1) * DKP]
            qn = qh * lax.rsqrt(_rowsum(qh * qh) / DK + EPS) * qng
            q_ref[hd] = (jnp.concatenate([qn[:, :128], _rope(qn[:, 128:], cos, sin)], axis=1) * SCALE).astype(BF16)
        ckv = u[:, QL:QL + KVL]
        ckvn = ckv * lax.rsqrt(jnp.mean(ckv * ckv, axis=-1, keepdims=True) + EPS) * kvlg_ref[...]
        kv = _nn(ckvn.astype(BF16), wukv_ref[...])
        kr = u[:, 384:512]
        skr = _rowsum(kr * kr)
        kng = kng_ref[...]
        for hd in range(NH):
            kn = kv[:, hd * 256:hd * 256 + 128]
            rk = lax.rsqrt((_rowsum(kn * kn) + skr) / DK + EPS)
            k_ref[hd] = jnp.concatenate([kn * rk * kng[:, :128], _rope(kr * rk * kng[:, 128:], cos, sin)], axis=1).astype(BF16)
            v_ref[hd] = kv[:, hd * 256 + 128:(hd + 1) * 256].astype(BF16)

    row = lambda w: pl.BlockSpec((TB, w), lambda i: (i, 0))
    heads = lambda w: pl.BlockSpec((NH, TB, w), lambda i: (0, i, 0))
    return pl.pallas_call(
        body, name="fwd_in", grid=(nb,),
        in_specs=[row(D), pl.BlockSpec((1, 3, D), lambda i: (jnp.minimum(i // off, 1), 0, 0)), _full((1, D)), _full((D, DU)),
                  _full((1, QL)), _full((QL, NH * DKP)), _full((1, KVL)), _full((KVL, NH * 256)), _full((1, DKP)),
                  _full((1, DKP)), row(128), row(128)],
        out_specs=[row(DU), heads(DKP), heads(DKP), heads(DV)],
        out_shape=[jax.ShapeDtypeStruct((t_all, DU), F32), jax.ShapeDtypeStruct((NH, t_all, DKP), BF16),
                   jax.ShapeDtypeStruct((NH, t_all, DKP), BF16), jax.ShapeDtypeStruct((NH, t_all, DV), BF16)],
        compiler_params=_params(("arbitrary",)),
    )(xall, modsel, norm_g, w_in, q_lora_g, w_uq, kv_lora_g, w_ukv, qn_g, kn_g, cos, sin)


def _attn_fwd(q, k, v, s_len):
    t_all = q.shape[1]
    off = (t_all - s_len) // TB
    nq = s_len // TB

    def body(q_ref, k_ref, v_ref, o_ref, lse_ref):
        s = _nt(q_ref[0], k_ref[0])
        m = jnp.max(s, axis=-1, keepdims=True)
        e = jnp.exp(s - m)
        l = _rowsum(e)
        o_ref[...] = _nn(e.astype(BF16), v_ref[0]) / l
        lse_ref[0, 0] = _row_layout(m + jnp.log(l))

    return pl.pallas_call(
        body, name="attn_fwd", grid=(NH, nq),
        in_specs=[pl.BlockSpec((1, TB, DKP), lambda h, i: (h, i + off, 0)), pl.BlockSpec((1, t_all, DKP), lambda h, i: (h, 0, 0)),
                  pl.BlockSpec((1, t_all, DV), lambda h, i: (h, 0, 0))],
        out_specs=[pl.BlockSpec((TB, DV), lambda h, i: (i, h)), pl.BlockSpec((1, 1, 8, TB), lambda h, i: (h, i, 0, 0))],
        out_shape=[jax.ShapeDtypeStruct((s_len, NH * DV), F32), jax.ShapeDtypeStruct((NH, nq, 8, TB), F32)],
        compiler_params=_params(("arbitrary", "arbitrary")),
    )(q, k, v)


def _out_stage(attn, u, xall, target, gate, w_pool, pool_scale, w_out, s_len):
    t_all = xall.shape[0]
    off = (t_all - s_len) // TB
    nq = s_len // TB
    hb = TB // HALO
    n = TB + 2 * HALO

    def body(attn_ref, ga_ref, pin_ref, pprev_ref, pnext_ref, gp_ref, x_ref, tgt_ref, gate_ref, wp_ref, ps_ref, wo_ref,
             dxn_ref, dattn_ref, dga_ref, dgp_ref, dpool_ref, dwo_ref, dgate_ref, dps_ref, dwp_ref, loss_ref):
        i = pl.program_id(0)

        @pl.when(i == 0)
        def _():
            dwo_ref[...] = jnp.zeros_like(dwo_ref)
            dgate_ref[...] = jnp.zeros_like(dgate_ref)
            dps_ref[...] = jnp.zeros_like(dps_ref)
            dwp_ref[...] = jnp.zeros_like(dwp_ref)
            loss_ref[...] = jnp.zeros_like(loss_ref)

        attn = attn_ref[...]
        ga = ga_ref[...]
        gp = gp_ref[...]
        pin = pin_ref[...]
        prev = jnp.where(i == 0, 0.0, pprev_ref[...])
        nxt = jnp.where(i == nq - 1, 0.0, pnext_ref[...])
        win = jnp.concatenate([prev, pin, nxt], axis=0)
        tg = i * TB + lax.broadcasted_iota(jnp.int32, (TB, 1), 0)
        pooled = []
        for g, w in enumerate(POOL_WINDOWS):
            a = win[:, g * 128:(g + 1) * 128]
            p = _shift_rows(a, -1) + a
            for step in (1, 2, 4):
                if w >= 4 * step:
                    p = _shift_rows(p, -step) + _shift_rows(p, step)
            cnt = (jnp.minimum(tg + w // 2, s_len) - jnp.maximum(tg - w // 2, 0)).astype(F32)
            pooled.append(p[HALO:HALO + TB] / cnt - a[HALO:HALO + TB])
        pooled_b = [p.astype(BF16) for p in pooled]
        z = jnp.concatenate([_nn(pooled_b[g], wp_ref[g]) for g in range(4)], axis=1)
        ps = ps_ref[...]
        yp = z * ps
        sga = _sig(ga)
        sila = ga * sga
        sgp = _sig(gp)
        silp = gp * sgp
        br = jnp.concatenate([sila * attn, silp * yp], axis=1).astype(BF16)
        y = _nn(br, wo_ref[...])
        gate = gate_ref[...]
        err = x_ref[...] + gate * y - tgt_ref[...]
        loss_ref[...] += _colsum(_rowsum(err * err)) * (0.5 / D)
        dxn = err * (1.0 / D)
        dxn_ref[...] = dxn
        dgate_ref[...] += _colsum(dxn * y)
        dy = (dxn * gate).astype(BF16)
        dwo_ref[...] += _tn(br, dy)
        dbr = _nt(dy, wo_ref[...])
        dbra = dbr[:, :512]
        dbrp = dbr[:, 512:]
        dattn_ref[...] = (dbra * sila).astype(BF16)
        dga_ref[...] = (dbra * attn * (sga * (1.0 + ga * (1.0 - sga)))).astype(BF16)
        dgp_ref[...] = (dbrp * yp * (sgp * (1.0 + gp * (1.0 - sgp)))).astype(BF16)
        dyp = dbrp * silp
        dps_ref[...] += _colsum(dyp * z)
        dz = (dyp * ps).astype(BF16)
        dpool = []
        for g in range(4):
            dzg = dz[:, g * 128:(g + 1) * 128]
            dwp_ref[g] += _tn(pooled_b[g], dzg)
            dpool.append(_nt(dzg, wp_ref[g]))
        dpool_ref[...] = jnp.concatenate(dpool, axis=1)

    lat = lambda w: pl.BlockSpec((TB, w), lambda i: (i, 0))
    ucol = lambda j: pl.BlockSpec((TB, 512), lambda i: (i + off, j))
    last8 = t_all // HALO - 1
    return pl.pallas_call(
        body, name="out_stage", grid=(nq,),
        in_specs=[lat(512), ucol(1), ucol(2),
                  pl.BlockSpec((HALO, 512), lambda i: ((i + off) * hb - 1, 2)),
                  pl.BlockSpec((HALO, 512), lambda i: (jnp.minimum((i + off + 1) * hb, last8), 2)),
                  ucol(3), pl.BlockSpec((TB, D), lambda i: (i + off, 0)), lat(D), _full((1, D)), _full((4, 128, 128)),
                  _full((1, 512)), _full((D, D))],
        out_specs=[lat(D), lat(512), lat(512), lat(512), lat(512),
                   _full((D, D)), _full((1, D)), _full((1, 512)), _full((4, 128, 128)), _full((1, 1))],
        out_shape=[jax.ShapeDtypeStruct((s_len, D), F32), jax.ShapeDtypeStruct((s_len, 512), BF16),
                   jax.ShapeDtypeStruct((s_len, 512), BF16), jax.ShapeDtypeStruct((s_len, 512), BF16),
                   jax.ShapeDtypeStruct((s_len, 512), F32),
                   jax.ShapeDtypeStruct((D, D), F32), jax.ShapeDtypeStruct((1, D), F32), jax.ShapeDtypeStruct((1, 512), F32),
                   jax.ShapeDtypeStruct((4, 128, 128), F32), jax.ShapeDtypeStruct((1, 1), F32)],
        compiler_params=_params(("arbitrary",)),
    )(attn, u, u, u, u, u, xall, target, gate, w_pool, pool_scale, w_out)


def _attn_bwd(q, k, v, dattn, attn, lse, s_len):
    t_all = q.shape[1]
    off = (t_all - s_len) // TB
    nq = s_len // TB
    nch = 4
    ch = t_all // nch

    def body(q_ref, k_ref, v_ref, do_ref, o_ref, lse_ref, dq_ref, dk_ref, dv_ref):
        i = pl.program_id(1)

        @pl.when(i == 0)
        def _():
            dk_ref[...] = jnp.zeros_like(dk_ref)
            dv_ref[...] = jnp.zeros_like(dv_ref)

        qb = q_ref[0]
        do = do_ref[...]
        lse_r = lse_ref[0, 0][0:1, :]
        delta_r = _row_layout(_rowsum(do.astype(F32) * o_ref[...]))[0:1, :]
        dq = jnp.zeros((TB, DKP), F32)
        for c in range(nch):
            rows = pl.ds(c * ch, ch)
            kc = k_ref[0, rows, :]
            p_t = jnp.exp(_nt(kc, qb) - lse_r)
            ds_t = (p_t * (_nt(v_ref[0, rows, :], do) - delta_r)).astype(BF16)
            dv_ref[0, rows, :] += _nn(p_t.astype(BF16), do)
            dk_ref[0, rows, :] += _nn(ds_t, qb)
            dq += _tn(ds_t, kc)
        dq_ref[0] = dq * SCALE

    kvspec = lambda w: pl.BlockSpec((1, t_all, w), lambda h, i: (h, 0, 0))
    rowspec = pl.BlockSpec((1, 1, 8, TB), lambda h, i: (h, i, 0, 0))
    return pl.pallas_call(
        body, name="attn_bwd", grid=(NH, nq),
        in_specs=[pl.BlockSpec((1, TB, DKP), lambda h, i: (h, i + off, 0)), kvspec(DKP), kvspec(DV),
                  pl.BlockSpec((TB, DV), lambda h, i: (i, h)), pl.BlockSpec((TB, DV), lambda h, i: (i, h)), rowspec],
        out_specs=[pl.BlockSpec((1, TB, DKP), lambda h, i: (h, i, 0)), kvspec(DKP), kvspec(DV)],
        out_shape=[jax.ShapeDtypeStruct((NH, s_len, DKP), F32), jax.ShapeDtypeStruct((NH, t_all, DKP), F32),
                   jax.ShapeDtypeStruct((NH, t_all, DV), F32)],
        compiler_params=_params(("arbitrary", "arbitrary")),
    )(q, k, v, dattn, attn, lse)


def _qkv_bwd(u, dq, dk, dv, cos, sin, q_lora_g, w_uq, kv_lora_g, w_ukv, qn_g, kn_g, s_len):
    t_all = u.shape[0]
    off = (t_all - s_len) // TB
    nb = t_all // TB

    def body(ulo_ref, dq_ref, dk_ref, dv_ref, cos_ref, sin_ref, qlg_ref, wuq_ref, kvlg_ref, wukv_ref, qng_ref, kng_ref,
             dlo_ref, dwuq_ref, dwukv_ref, dqlg_ref, dkvlg_ref, dqng_ref, dkng_ref):
        i = pl.program_id(0)

        @pl.when(i == 0)
        def _():
            for r in (dwuq_ref, dwukv_ref, dqlg_ref, dkvlg_ref, dqng_ref, dkng_ref):
                r[...] = jnp.zeros_like(r)

        latent = i >= off
        ulo = ulo_ref[...]
        cos = cos_ref[...]
        sin = sin_ref[...]
        cq = ulo[:, 0:QL]
        rc = lax.rsqrt(jnp.mean(cq * cq, axis=-1, keepdims=True) + EPS)
        cqh = cq * rc
        qlg = qlg_ref[...]
        cqn_b = (cqh * qlg).astype(BF16)
        qraw = _nn(cqn_b, wuq_ref[...])
        qng = qng_ref[...]
        dqng = jnp.zeros((1, DKP), F32)
        parts = []
        for hd in range(NH):
            qh = qraw[:, hd * DKP:(hd + 1) * DKP]
            rq = lax.rsqrt(_rowsum(qh * qh) / DK + EPS)
            xh = qh * rq
            dqh = jnp.where(latent, dq_ref[hd], 0.0)
            dyq = jnp.concatenate([dqh[:, :128], _rope_t(dqh[:, 128:], cos, sin)], axis=1)
            dqng += _colsum(dyq * xh)
            dxh = dyq * qng
            parts.append(rq * (dxh - xh * (_rowsum(dxh * xh) / DK)))
        dqng_ref[...] += dqng
        dqraw = jnp.concatenate(parts, axis=1).astype(BF16)
        dwuq_ref[...] += _tn(cqn_b, dqraw)
        dcqn = _nt(dqraw, wuq_ref[...])
        dqlg_ref[...] += _colsum(dcqn * cqh)
        dxh = dcqn * qlg
        dcq = rc * (dxh - cqh * jnp.mean(dxh * cqh, axis=-1, keepdims=True))

        ckv = ulo[:, QL:QL + KVL]
        r0 = lax.rsqrt(jnp.mean(ckv * ckv, axis=-1, keepdims=True) + EPS)
        ckvh = ckv * r0
        kvlg = kvlg_ref[...]
        ckvn_b = (ckvh * kvlg).astype(BF16)
        kv = _nn(ckvn_b, wukv_ref[...])
        kr = ulo[:, 384:512]
        skr = _rowsum(kr * kr)
        kng = kng_ref[...]
        dkr = jnp.zeros((TB, 128), F32)
        dkng = jnp.zeros((1, DKP), F32)
        parts = []
        for hd in range(NH):
            kn = kv[:, hd * 256:hd * 256 + 128]
            rk = lax.rsqrt((_rowsum(kn * kn) + skr) / DK + EPS)
            xh1 = kn * rk
            xh2 = kr * rk
            dkh = dk_ref[hd]
            d1 = dkh[:, :128]
            d2 = _rope_t(dkh[:, 128:], cos, sin)
            dkng += jnp.concatenate([_colsum(d1 * xh1), _colsum(d2 * xh2)], axis=1)
            dx1 = d1 * kng[:, :128]
            dx2 = d2 * kng[:, 128:]
            dot = (_rowsum(dx1 * xh1) + _rowsum(dx2 * xh2)) / DK
            parts.append(rk * (dx1 - xh1 * dot))
            parts.append(dv_ref[hd])
            dkr += rk * (dx2 - xh2 * dot)
        dkng_ref[...] += dkng
        dkv = jnp.concatenate(parts, axis=1).astype(BF16)
        dwukv_ref[...] += _tn(ckvn_b, dkv)
        dckvn = _nt(dkv, wukv_ref[...])
        dkvlg_ref[...] += _colsum(dckvn * ckvh)
        dxh = dckvn * kvlg
        dckv = r0 * (dxh - ckvh * jnp.mean(dxh * ckvh, axis=-1, keepdims=True))
        dlo_ref[...] = jnp.concatenate([dcq, dckv, dkr], axis=1).astype(BF16)

    row = lambda w: pl.BlockSpec((TB, w), lambda i: (i, 0))
    heads = lambda w: pl.BlockSpec((NH, TB, w), lambda i: (0, i, 0))
    return pl.pallas_call(
        body, name="qkv_bwd", grid=(nb,),
        in_specs=[row(512), pl.BlockSpec((NH, TB, DKP), lambda i: (0, jnp.maximum(i - off, 0), 0)), heads(DKP), heads(DV),
                  row(128), row(128), _full((1, QL)), _full((QL, NH * DKP)), _full((1, KVL)), _full((KVL, NH * 256)),
                  _full((1, DKP)), _full((1, DKP))],
        out_specs=[row(512), _full((QL, NH * DKP)), _full((KVL, NH * 256)), _full((1, QL)), _full((1, KVL)),
                   _full((1, DKP)), _full((1, DKP))],
        out_shape=[jax.ShapeDtypeStruct((t_all, 512), BF16), jax.ShapeDtypeStruct((QL, NH * DKP), F32),
                   jax.ShapeDtypeStruct((KVL, NH * 256), F32), jax.ShapeDtypeStruct((1, QL), F32),
                   jax.ShapeDtypeStruct((1, KVL), F32), jax.ShapeDtypeStruct((1, DKP), F32), jax.ShapeDtypeStruct((1, DKP), F32)],
        compiler_params=_params(("arbitrary",)),
    )(u, dq, dk, dv, cos, sin, q_lora_g, w_uq, kv_lora_g, w_ukv, qn_g, kn_g)


def _in_bwd(xall, modsel, norm_g, dlo, dga, dgp, dpool, dxn, w_in, s_len):
    t_all = xall.shape[0]
    off = (t_all - s_len) // TB
    nb = t_all // TB
    nq = s_len // TB
    hb = TB // HALO
    n = TB + 2 * HALO

    def body(x_ref, mod_ref, ng_ref, dlo_ref, dga_ref, dgp_ref, dp_ref, dpprev_ref, dpnext_ref, dxn_ref, win_ref,
             gx_ref, dwin_ref, dmod_ref, dng_ref):
        i = pl.program_id(0)
        j = i - off

        @pl.when(i == 0)
        def _():
            dwin_ref[...] = jnp.zeros_like(dwin_ref)
            dmod_ref[...] = jnp.zeros_like(dmod_ref)
            dng_ref[...] = jnp.zeros_like(dng_ref)

        latent = i >= off
        dp = dp_ref[...]
        prev = jnp.where(j <= 0, 0.0, dpprev_ref[...])
        nxt = jnp.where(j >= nq - 1, 0.0, dpnext_ref[...])
        win = jnp.concatenate([prev, dp, nxt], axis=0)
        tg = j * TB - HALO + lax.broadcasted_iota(jnp.int32, (n, 1), 0)
        dpin = []
        for g, w in enumerate(POOL_WINDOWS):
            cnt = jnp.maximum(jnp.minimum(tg + w // 2, s_len) - jnp.maximum(tg - w // 2, 0), 1).astype(F32)
            zq = win[:, g * 128:(g + 1) * 128] / cnt
            zq = zq + _shift_rows(zq, 1)
            for step in (1, 2, 4):
                if w >= 4 * step:
                    zq = _shift_rows(zq, -step) + _shift_rows(zq, step)
            dpin.append(zq[HALO:HALO + TB] - dp[:, g * 128:(g + 1) * 128])
        dhi = jnp.concatenate([dga_ref[...], jnp.concatenate(dpin, axis=1).astype(BF16), dgp_ref[...]], axis=1)
        dhi = jnp.where(latent, dhi, jnp.zeros_like(dhi))
        du = jnp.concatenate([dlo_ref[...], dhi], axis=1)

        ng = ng_ref[...]
        r, xh, xg, h, scale = _modulated(x_ref[...], mod_ref, ng)
        dwin_ref[...] += _tn(h.astype(BF16), du)
        dh = _nt(du, win_ref[...])
        is_lat = latent.astype(F32)
        dsh = _colsum(dh)
        dsc = _colsum(dh * xg)
        dmod_ref[0, 0:1, :] += dsh * (1.0 - is_lat)
        dmod_ref[0, 1:2, :] += dsc * (1.0 - is_lat)
        dmod_ref[1, 0:1, :] += dsh * is_lat
        dmod_ref[1, 1:2, :] += dsc * is_lat
        dxg = dh * (1.0 + scale)
        dng_ref[...] += _colsum(dxg * xh)
        dxh = dxg * ng
        gx_ref[...] = r * (dxh - xh * jnp.mean(dxh * xh, axis=-1, keepdims=True)) + dxn_ref[...]

    row = lambda w: pl.BlockSpec((TB, w), lambda i: (i, 0))
    lat = lambda w: pl.BlockSpec((TB, w), lambda i: (jnp.maximum(i - off, 0), 0))
    last8 = s_len // HALO - 1
    return pl.pallas_call(
        body, name="in_bwd", grid=(nb,),
        in_specs=[row(D), pl.BlockSpec((1, 3, D), lambda i: (jnp.minimum(i // off, 1), 0, 0)), _full((1, D)), row(512), lat(512),
                  lat(512), lat(512),
                  pl.BlockSpec((HALO, 512), lambda i: (jnp.maximum(jnp.maximum(i - off, 0) * hb - 1, 0), 0)),
                  pl.BlockSpec((HALO, 512), lambda i: (jnp.minimum((jnp.maximum(i - off, 0) + 1) * hb, last8), 0)),
                  lat(D), _full((D, DU))],
        out_specs=[lat(D), _full((D, DU)), _full((2, 2, D)), _full((1, D))],
        out_shape=[jax.ShapeDtypeStruct((s_len, D), F32), jax.ShapeDtypeStruct((D, DU), F32),
                   jax.ShapeDtypeStruct((2, 2, D), F32), jax.ShapeDtypeStruct((1, D), F32)],
        compiler_params=_params(("arbitrary",)),
    )(xall, modsel, norm_g, dlo, dga, dgp, dpool, dpool, dpool, dxn, w_in)


def _adamw(w, g, m, v, name):
    rows, cols = w.shape
    rb = 256 if rows % 256 == 0 else rows

    def body(w_ref, g_ref, m_ref, v_ref, d_ref, mo_ref, vo_ref):
        gv = g_ref[...]
        mn = ADAM_B1 * m_ref[...] + (1.0 - ADAM_B1) * gv
        vn = ADAM_B2 * v_ref[...] + (1.0 - ADAM_B2) * (gv * gv)
        m_hat = mn / (1.0 - ADAM_B1 ** ADAM_STEP)
        v_hat = vn / (1.0 - ADAM_B2 ** ADAM_STEP)
        d_ref[...] = -ADAM_LR * (m_hat / (jnp.sqrt(v_hat) + ADAM_EPS) + ADAM_WD * w_ref[...])
        mo_ref[...] = mn
        vo_ref[...] = vn

    spec = pl.BlockSpec((rb, cols), lambda i: (i, 0))
    shp = jax.ShapeDtypeStruct((rows, cols), F32)
    return pl.pallas_call(
        body, name=name, grid=(rows // rb,), in_specs=[spec] * 4, out_specs=[spec] * 3, out_shape=[shp] * 3,
        compiler_params=_params(("arbitrary",)),
    )(w, g, m, v)


VM = pl.BlockSpec(memory_space=pltpu.VMEM)
ANY = pl.BlockSpec(memory_space=pl.ANY)


def _select_rows(slots_ref, n_slots, row=0):
    sub = lax.broadcasted_iota(jnp.int32, (8, 1), 0)
    out = None
    for d in range(n_slots):
        r = jnp.where(sub == d, jnp.broadcast_to(slots_ref[d][row:row + 1, :], (8, slots_ref.shape[-1])), 0.0)
        out = r if out is None else out + r
    return out


def _mod_exchange(c, c_ctx, w_mod, b_mod_k):
    kw = w_mod.shape[1]

    def body(c_ref, cc_ref, w_ref, b_ref, a16_ref, mod_ref, a_ref, send_sems, recv_sems):
        x, y, cc = lax.axis_index("x"), lax.axis_index("y"), lax.axis_index("c")
        me = 4 * x + 2 * y + cc
        k = 2 * x + y
        cv = c_ref[...]
        a_ref[me] = jnp.broadcast_to(cv * _sig(cv), (8, D))
        sends = []
        for j, off in enumerate(PEERS7):
            cp = pltpu.make_async_remote_copy(a_ref.at[me], a_ref.at[me], send_sems.at[j], recv_sems.at[j],
                                              device_id=_peer(x, y, cc, off), device_id_type=MESH)
            cp.start()
            sends.append(cp)
        for j, off in enumerate(PEERS7):
            px, py, pc = _peer(x, y, cc, off)
            src = a_ref.at[4 * px + 2 * py + pc]
            pltpu.make_async_remote_copy(src, src, send_sems.at[j], recv_sems.at[j], device_id=(px, py, pc),
                                         device_id_type=MESH).wait_recv()
        ccv = cc_ref[...]
        sub = lax.broadcasted_iota(jnp.int32, (8, 1), 0)
        a16 = jnp.concatenate([_select_rows(a_ref, 8), jnp.where(sub == 0, jnp.broadcast_to(ccv * _sig(ccv), (8, D)), 0.0)], axis=0)
        a16_ref[...] = a16
        mod_ref[k] = _dot3(_nn, a16, w_ref[...]) + b_ref[...]
        for j, off in enumerate(CHIPS3):
            cp = pltpu.make_async_remote_copy(mod_ref.at[k], mod_ref.at[k], send_sems.at[7 + j], recv_sems.at[7 + j],
                                              device_id=_peer(x, y, cc, off + (0,)), device_id_type=MESH)
            cp.start()
            sends.append(cp)
        for j, off in enumerate(CHIPS3):
            px, py, pc = _peer(x, y, cc, off + (0,))
            src = mod_ref.at[2 * px + py]
            pltpu.make_async_remote_copy(src, src, send_sems.at[7 + j], recv_sems.at[7 + j], device_id=(px, py, pc),
                                         device_id_type=MESH).wait_recv()
        for cp in sends:
            cp.wait_send()

    return pl.pallas_call(
        body, name="mod_exchange", in_specs=[VM] * 4, out_specs=[VM, VM],
        out_shape=[jax.ShapeDtypeStruct((16, D), F32), jax.ShapeDtypeStruct((4, 16, kw), F32)],
        scratch_shapes=[pltpu.VMEM((8, 8, D), F32), pltpu.SemaphoreType.DMA((10,)), pltpu.SemaphoreType.DMA((10,))],
        compiler_params=pltpu.CompilerParams(vmem_limit_bytes=VMEM_LIMIT),
    )(c, c_ctx, w_mod, b_mod_k)


def _gather_weights(shards):
    nw = len(shards)

    def body(*refs):
        w_refs = refs[:nw]
        g_refs = refs[nw:2 * nw]
        send_sems, recv_sems, local_sems = refs[2 * nw:]
        x, y, cc = lax.axis_index("x"), lax.axis_index("y"), lax.axis_index("c")
        k = 2 * x + y
        sibling = (x, y, 1 - cc)
        waits = []
        for wi in range(nw):
            lc = pltpu.make_async_copy(w_refs[wi], g_refs[wi].at[k], local_sems.at[wi])
            lc.start()
            waits.append(lc)
        sends = []
        for wi in range(nw):
            hr = w_refs[wi].shape[0] // 2
            mine = pl.ds(pl.multiple_of(cc * hr, 16), hr)
            for j, off in enumerate(CHIPS3):
                cp = pltpu.make_async_remote_copy(w_refs[wi].at[mine, :], g_refs[wi].at[k, mine, :], send_sems.at[wi * 6 + j],
                                                  recv_sems.at[wi * 6 + j], device_id=_peer(x, y, cc, off + (0,)),
                                                  device_id_type=MESH)
                cp.start()
                sends.append(cp)
        for wi in range(nw):
            hr = w_refs[wi].shape[0] // 2
            mine = pl.ds(pl.multiple_of(cc * hr, 16), hr)
            for j, off in enumerate(CHIPS3):
                px, py, pc = _peer(x, y, cc, off + (0,))
                blk = g_refs[wi].at[2 * px + py, mine, :]
                pltpu.make_async_remote_copy(blk, blk, send_sems.at[wi * 6 + j], recv_sems.at[wi * 6 + j],
                                             device_id=(px, py, pc), device_id_type=MESH).wait_recv()
                fw = pltpu.make_async_remote_copy(blk, blk, send_sems.at[wi * 6 + 3 + j], recv_sems.at[wi * 6 + 3 + j],
                                                  device_id=sibling, device_id_type=MESH)
                fw.start()
                sends.append(fw)
        for wi in range(nw):
            hr = w_refs[wi].shape[0] // 2
            theirs = pl.ds(pl.multiple_of((1 - cc) * hr, 16), hr)
            for j, off in enumerate(CHIPS3):
                px, py, _ = _peer(x, y, cc, off + (0,))
                blk = g_refs[wi].at[2 * px + py, theirs, :]
                pltpu.make_async_remote_copy(blk, blk, send_sems.at[wi * 6 + 3 + j], recv_sems.at[wi * 6 + 3 + j],
                                             device_id=sibling, device_id_type=MESH).wait_recv()
        for cp in sends:
            cp.wait_send()
        for lc in waits:
            lc.wait()

    return pl.pallas_call(
        body, name="gather_weights", in_specs=[ANY] * nw, out_specs=[ANY] * nw,
        out_shape=[jax.ShapeDtypeStruct((4,) + s.shape, s.dtype) for s in shards],
        scratch_shapes=[pltpu.SemaphoreType.DMA((6 * nw,)), pltpu.SemaphoreType.DMA((6 * nw,)), pltpu.SemaphoreType.DMA((nw,))],
    )(*shards)


def _reduce_grads(grads, small):
    nw = len(grads)
    halves = [(g.shape[1] // 2, g.shape[2]) for g in grads]

    def body(*refs):
        g_refs = refs[:nw]
        sm_ref = refs[nw]
        r_refs = refs[nw + 1:2 * nw + 1]
        rsm_ref = refs[2 * nw + 1]
        bufs = refs[2 * nw + 2:3 * nw + 2]
        smbuf = refs[3 * nw + 2]
        send_sems, recv_sems, local_sems = refs[3 * nw + 3:]
        x, y, cc = lax.axis_index("x"), lax.axis_index("y"), lax.axis_index("c")
        me = 4 * x + 2 * y + cc
        k = 2 * x + y
        sibling = (x, y, 1 - cc)
        locals_ = []
        sends = []
        for wi in range(nw + 1):
            if wi < nw:
                hr = halves[wi][0]
                src = g_refs[wi].at[k, pl.ds(pl.multiple_of(cc * hr, 8), hr), :]
                buf = bufs[wi]
            else:
                src = sm_ref
                buf = smbuf
            lc = pltpu.make_async_copy(src, buf.at[me], local_sems.at[wi])
            lc.start()
            locals_.append(lc)
            for j, off in enumerate(PEERS7):
                px, py, pc = _peer(x, y, cc, off)
                if wi < nw:
                    src = g_refs[wi].at[2 * px + py, pl.ds(pl.multiple_of(pc * hr, 8), hr), :]
                cp = pltpu.make_async_remote_copy(src, buf.at[me], send_sems.at[wi * 8 + j], recv_sems.at[wi * 8 + j],
                                                  device_id=(px, py, pc), device_id_type=MESH)
                cp.start()
                sends.append(cp)
        for wi in range(nw + 1):
            buf = bufs[wi] if wi < nw else smbuf
            for j, off in enumerate(PEERS7):
                px, py, pc = _peer(x, y, cc, off)
                dst = buf.at[4 * px + 2 * py + pc]
                pltpu.make_async_remote_copy(dst, dst, send_sems.at[wi * 8 + j], recv_sems.at[wi * 8 + j],
                                             device_id=(px, py, pc), device_id_type=MESH).wait_recv()
            locals_[wi].wait()
            total = buf[0]
            for d in range(1, 8):
                total = total + buf[d]
            if wi < nw:
                r_refs[wi][cc] = total
                cp = pltpu.make_async_remote_copy(r_refs[wi].at[cc], r_refs[wi].at[cc], send_sems.at[wi * 8 + 7],
                                                  recv_sems.at[wi * 8 + 7], device_id=sibling, device_id_type=MESH)
                cp.start()
                sends.append(cp)
            else:
                rsm_ref[...] = total
        for wi in range(nw):
            blk = r_refs[wi].at[1 - cc]
            pltpu.make_async_remote_copy(blk, blk, send_sems.at[wi * 8 + 7], recv_sems.at[wi * 8 + 7], device_id=sibling,
                                         device_id_type=MESH).wait_recv()
        for cp in sends:
            cp.wait_send()

    return pl.pallas_call(
        body, name="reduce_grads", in_specs=[ANY] * (nw + 1), out_specs=[VM] * (nw + 1),
        out_shape=[jax.ShapeDtypeStruct((2,) + h, F32) for h in halves] + [jax.ShapeDtypeStruct(small.shape, F32)],
        scratch_shapes=[pltpu.VMEM((8,) + h, F32) for h in halves] + [pltpu.VMEM((8,) + small.shape, F32)]
        + [pltpu.SemaphoreType.DMA((8 * (nw + 1),)), pltpu.SemaphoreType.DMA((8 * (nw + 1),)), pltpu.SemaphoreType.DMA((nw + 1,))],
        compiler_params=pltpu.CompilerParams(vmem_limit_bytes=VMEM_LIMIT),
    )(*grads, small)


def _mod_bwd(dmod8, a16, w_mod, c_ctx):
    kw = w_mod.shape[1]

    def body(dm_ref, a16_ref, w_ref, cc_ref, gw_ref, gb_ref, gc_ref, dm_all, pc_all, send_sems, recv_sems):
        x, y, cc = lax.axis_index("x"), lax.axis_index("y"), lax.axis_index("c")
        me = 4 * x + 2 * y + cc
        k = 2 * x + y
        dm_all[me] = dm_ref[...]
        sends = []
        for j, off in enumerate(PEERS7):
            cp = pltpu.make_async_remote_copy(dm_all.at[me], dm_all.at[me], send_sems.at[j], recv_sems.at[j],
                                              device_id=_peer(x, y, cc, off), device_id_type=MESH)
            cp.start()
            sends.append(cp)
        for j, off in enumerate(PEERS7):
            px, py, pc = _peer(x, y, cc, off)
            src = dm_all.at[4 * px + 2 * py + pc]
            pltpu.make_async_remote_copy(src, src, send_sems.at[j], recv_sems.at[j], device_id=(px, py, pc),
                                         device_id_type=MESH).wait_recv()
        dmc = dm_all[0][1:2, :]
        dml = dm_all[0][0:1, :]
        for d in range(1, 8):
            dmc = dmc + dm_all[d][1:2, :]
            dml = dml + dm_all[d][0:1, :]
        gb_ref[...] = dml + dmc
        sub = lax.broadcasted_iota(jnp.int32, (8, 1), 0)
        b16 = jnp.concatenate([_select_rows(dm_all, 8), jnp.where(sub == 0, jnp.broadcast_to(dmc, (8, 3 * D)), 0.0)], axis=0)
        bk = jnp.zeros((16, kw), F32)
        for kk in range(4):
            bk = bk + jnp.where(k == kk, b16[:, kk * kw:(kk + 1) * kw], 0.0)
        gw_ref[...] = _dot3(_tn, a16_ref[...], bk)
        pc_all[k] = _dot3(_nt, jnp.broadcast_to(bk[8:9, :], (8, kw)), w_ref[...])
        for j, off in enumerate(CHIPS3):
            cp = pltpu.make_async_remote_copy(pc_all.at[k], pc_all.at[k], send_sems.at[7 + j], recv_sems.at[7 + j],
                                              device_id=_peer(x, y, cc, off + (0,)), device_id_type=MESH)
            cp.start()
            sends.append(cp)
        for j, off in enumerate(CHIPS3):
            px, py, pc = _peer(x, y, cc, off + (0,))
            src = pc_all.at[2 * px + py]
            pltpu.make_async_remote_copy(src, src, send_sems.at[7 + j], recv_sems.at[7 + j], device_id=(px, py, pc),
                                         device_id_type=MESH).wait_recv()
        tot = pc_all[0][0:1, :] + pc_all[1][0:1, :] + pc_all[2][0:1, :] + pc_all[3][0:1, :]
        ccv = cc_ref[...]
        sg = _sig(ccv)
        gc_ref[...] = tot * (sg * (1.0 + ccv * (1.0 - sg)))
        for cp in sends:
            cp.wait_send()

    return pl.pallas_call(
        body, name="mod_bwd", in_specs=[VM] * 4, out_specs=[VM] * 3,
        out_shape=[jax.ShapeDtypeStruct((D, kw), F32), jax.ShapeDtypeStruct((1, 3 * D), F32), jax.ShapeDtypeStruct((1, D), F32)],
        scratch_shapes=[pltpu.VMEM((8, 8, 3 * D), F32), pltpu.VMEM((4, 8, D), F32), pltpu.SemaphoreType.DMA((10,)),
                        pltpu.SemaphoreType.DMA((10,))],
        compiler_params=pltpu.CompilerParams(vmem_limit_bytes=VMEM_LIMIT),
    )(dmod8, a16, w_mod, c_ctx)


def _rope_tables(s_len, lc):
    rows = s_len // GRID_W
    row = jnp.repeat(jnp.arange(rows, dtype=F32), GRID_W)
    col = jnp.tile(jnp.arange(GRID_W, dtype=F32), rows)
    n_freq = 16
    inv = ROPE_BASE ** (-jnp.arange(n_freq, dtype=F32) / n_freq)
    ang_r = row[:, None] * inv
    ang_c = col[:, None] * inv
    ang = jnp.concatenate([ang_r, ang_r, ang_c, ang_c], axis=-1)
    cos = jnp.concatenate([jnp.cos(ang), jnp.ones((s_len, 64), F32)], axis=1)
    sin = jnp.concatenate([jnp.sin(ang), jnp.zeros((s_len, 64), F32)], axis=1)
    cos = jnp.concatenate([jnp.ones((lc, 128), F32), cos], axis=0)
    sin = jnp.concatenate([jnp.zeros((lc, 128), F32), sin], axis=0)
    return cos, sin


SMALL = (("norm_g", D), ("q_lora_g", QL), ("kv_lora_g", KVL), ("q_norm_g", DK), ("k_norm_g", DK), ("w_pool", 4 * 128 * 128),
         ("pool_scale", 512), ("c_ctx", D), ("b_mod", 3 * D))


def _pack_small(parts, names):
    flat = jnp.concatenate([parts[n].reshape(-1).astype(F32) for n in names])
    rows = -(-flat.shape[0] // (8 * 128)) * 8
    return jnp.pad(flat, (0, rows * 128 - flat.shape[0])).reshape(rows, 128)


def _unpack_small(packed, names, shapes):
    flat = packed.reshape(-1)
    out = {}
    o = 0
    sizes = dict(SMALL)
    for n in names:
        out[n] = flat[o:o + sizes[n]].reshape(shapes[n])
        o += sizes[n]
    return out


def kernel(x, c, ctx, c_ctx, w_mod, b_mod, norm_g, w_in, q_lora_g, w_uq, kv_lora_g, w_ukv, q_norm_g, k_norm_g, w_pool, pool_scale, w_out, loss_target, m_c_ctx, m_w_mod, m_b_mod, m_norm_g, m_w_in, m_q_lora_g, m_w_uq, m_kv_lora_g, m_w_ukv, m_q_norm_g, m_k_norm_g, m_w_pool, m_pool_scale, m_w_out, v_c_ctx, v_w_mod, v_b_mod, v_norm_g, v_w_in, v_q_lora_g, v_w_uq, v_kv_lora_g, v_w_ukv, v_q_norm_g, v_k_norm_g, v_w_pool, v_pool_scale, v_w_out):
    xi, yi, ci = lax.axis_index("x"), lax.axis_index("y"), lax.axis_index("c")
    me = 4 * xi + 2 * yi + ci
    k = 2 * xi + yi
    s_len = x.shape[1]
    lc = ctx.shape[1]
    kw = w_mod.shape[2]
    weights = dict(c_ctx=c_ctx, w_mod=w_mod, b_mod=b_mod, norm_g=norm_g, w_in=w_in, q_lora_g=q_lora_g, w_uq=w_uq,
                   kv_lora_g=kv_lora_g, w_ukv=w_ukv, q_norm_g=q_norm_g, k_norm_g=k_norm_g, w_pool=w_pool,
                   pool_scale=pool_scale, w_out=w_out)
    m_in = dict(c_ctx=m_c_ctx, w_mod=m_w_mod, b_mod=m_b_mod, norm_g=m_norm_g, w_in=m_w_in, q_lora_g=m_q_lora_g, w_uq=m_w_uq,
                kv_lora_g=m_kv_lora_g, w_ukv=m_w_ukv, q_norm_g=m_q_norm_g, k_norm_g=m_k_norm_g, w_pool=m_w_pool,
                pool_scale=m_pool_scale, w_out=m_w_out)
    v_in = dict(c_ctx=v_c_ctx, w_mod=v_w_mod, b_mod=v_b_mod, norm_g=v_norm_g, w_in=v_w_in, q_lora_g=v_q_lora_g, w_uq=v_w_uq,
                kv_lora_g=v_kv_lora_g, w_ukv=v_w_ukv, q_norm_g=v_q_norm_g, k_norm_g=v_k_norm_g, w_pool=v_w_pool,
                pool_scale=v_pool_scale, w_out=v_w_out)
    order = ["c_ctx", "w_mod", "b_mod", "norm_g", "w_in", "q_lora_g", "w_uq", "kv_lora_g", "w_ukv", "q_norm_g", "k_norm_g",
             "w_pool", "pool_scale", "w_out"]

    c_ctx2 = c_ctx.reshape(1, D)
    b_mod_k = lax.dynamic_slice(b_mod, (0, k * kw), (1, kw))
    a16, mod_all = _mod_exchange(c, c_ctx2, w_mod[0], b_mod_k)
    mod_me = lax.dynamic_index_in_dim(mod_all, me, axis=1, keepdims=False).reshape(3, D)
    mod_c = mod_all[:, 8, :].reshape(3, D)
    modsel = jnp.stack([mod_c, mod_me])
    g_in, g_uq, g_ukv, g_out = _gather_weights([w_in[0].astype(BF16), w_uq[0].astype(BF16), w_ukv[0].astype(BF16),
                                                w_out[0].astype(BF16)])
    w_in_f = jnp.concatenate([g_in[j] for j in range(4)], axis=1)
    w_in_p = jnp.concatenate([w_in_f[:, :U_KR_END], jnp.zeros((D, 64), BF16), w_in_f[:, U_KR_END:]], axis=1)
    w_uq_f = jnp.concatenate([g_uq[j] for j in range(4)], axis=1)
    w_uq_p = jnp.pad(w_uq_f.reshape(QL, NH, DK), ((0, 0), (0, 0), (0, DKP - DK))).reshape(QL, NH * DKP)
    w_ukv_f = jnp.concatenate([g_ukv[j] for j in range(4)], axis=1)
    w_out_f = g_out.reshape(D, D)
    qn_g = jnp.pad(q_norm_g, ((0, 0), (0, DKP - DK)))
    kn_g = jnp.pad(k_norm_g, ((0, 0), (0, DKP - DK)))
    w_pool_b = w_pool[0].astype(BF16)
    cos, sin = _rope_tables(s_len, lc)

    xall = jnp.concatenate([ctx[0], x[0]], axis=0)
    u, q, kk, v = _fwd_in(xall, modsel, norm_g, w_in_p, q_lora_g, w_uq_p, kv_lora_g, w_ukv_f, qn_g, kn_g, cos, sin, s_len)
    attn, lse = _attn_fwd(q, kk, v, s_len)
    nqb = s_len // Q_BLOCK
    attn_t = jnp.transpose(attn.reshape(nqb, Q_BLOCK, NH * DV), (1, 0, 2)).reshape(s_len, NH * DV)
    (dxn, dattn_t, dga, dgp, dpool, dw_out, dgate, dps, dw_pool, loss) = _out_stage(
        attn_t, u, xall, loss_target[0], modsel[1, 2:3, :], w_pool_b, pool_scale, w_out_f, s_len)
    dattn = jnp.transpose(dattn_t.reshape(Q_BLOCK, nqb, NH * DV), (1, 0, 2)).reshape(s_len, NH * DV)
    dq, dk, dv = _attn_bwd(q, kk, v, dattn, attn, lse, s_len)
    dlo, dw_uq_p, dw_ukv, dqlg, dkvlg, dqng, dkng = _qkv_bwd(u, dq, dk, dv, cos, sin, q_lora_g, w_uq_p, kv_lora_g, w_ukv_f,
                                                            qn_g, kn_g, s_len)
    gx, dw_in_p, dmod, dng = _in_bwd(xall, modsel, norm_g, dlo, dga, dgp, dpool, dxn, w_in_p, s_len)

    dmod_l = jnp.concatenate([dmod[1, 0], dmod[1, 1], dgate[0]]).reshape(1, 3 * D)
    dmod_c = jnp.concatenate([dmod[0, 0], dmod[0, 1], jnp.zeros((D,), F32)]).reshape(1, 3 * D)
    dmod8 = jnp.concatenate([dmod_l, dmod_c, jnp.zeros((6, 3 * D), F32)], axis=0)
    g_w_mod, g_b_mod, g_c_ctx = _mod_bwd(dmod8, a16, w_mod[0], c_ctx2)

    dw_in = jnp.concatenate([dw_in_p[:, :U_KR_END], dw_in_p[:, U_KR_END + 64:]], axis=1)
    dw_uq = dw_uq_p.reshape(QL, NH, DKP)[:, :, :DK].reshape(QL, NH * DK)
    by_chip = lambda a: jnp.transpose(a.reshape(a.shape[0], 4, a.shape[1] // 4), (1, 0, 2))
    small_names = ["norm_g", "q_lora_g", "kv_lora_g", "q_norm_g", "k_norm_g", "w_pool", "pool_scale"]
    small_g = dict(norm_g=dng, q_lora_g=dqlg, kv_lora_g=dkvlg, q_norm_g=dqng[:, :DK], k_norm_g=dkng[:, :DK], w_pool=dw_pool,
                   pool_scale=dps)
    r_in, r_uq, r_ukv, r_out, r_small = _reduce_grads(
        [by_chip(dw_in), by_chip(dw_uq), by_chip(dw_ukv), dw_out.reshape(4, D // 4, D)], _pack_small(small_g, small_names))
    shapes = {n: weights[n].shape for n in order}
    grads = _unpack_small(r_small, small_names, shapes)
    grads.update(c_ctx=g_c_ctx.reshape(D), b_mod=g_b_mod, w_mod=g_w_mod[None], w_in=r_in.reshape(1, D, -1),
                 w_uq=r_uq.reshape(1, QL, -1), w_ukv=r_ukv.reshape(1, KVL, -1), w_out=r_out.reshape(1, -1, D))

    delta_w, new_m, new_v = {}, {}, {}
    for n in ("w_mod", "w_in", "w_uq", "w_ukv", "w_out"):
        shp = weights[n].shape
        d_, m_, v_ = _adamw(weights[n][0], grads[n][0], m_in[n][0], v_in[n][0], "adamw_" + n)
        delta_w[n], new_m[n], new_v[n] = d_.reshape(shp), m_.reshape(shp), v_.reshape(shp)
    names = [n for n, _ in SMALL]
    d_, m_, v_ = _adamw(_pack_small(weights, names), _pack_small(grads, names), _pack_small(m_in, names),
                        _pack_small(v_in, names), "adamw_small")
    for dst, packed in ((delta_w, d_), (new_m, m_), (new_v, v_)):
        dst.update(_unpack_small(packed, names, shapes))

    loss_all = lax.psum(loss[0, 0], ("x", "y", "c"))
    return (loss_all, gx[None], *[grads[n] for n in order], *[delta_w[n] for n in order], *[new_m[n] for n in order],
            *[new_v[n] for n in order])
```

```python
import functools

import jax
import jax.numpy as jnp
from jax import lax
from jax.experimental import pallas as pl
from jax.experimental.pallas import tpu as pltpu

F32 = jnp.float32
BF16 = jnp.bfloat16
MESH = pl.DeviceIdType.MESH

D = 1024
NH = 4
DK = 192
DKP = 256
DV = 128
QL = 256
KVL = 128
DU = 2048
U_KR_END = 448
POOL_WINDOWS = (2, 4, 8, 16)
HALO = 8
EPS = 1e-6
ROPE_BASE = 10000.0
GRID_W = 64
Q_BLOCK = 128
TB = 256
SCALE = DK ** -0.5
VMEM_LIMIT = 56 * 1024 * 1024

ADAM_LR = 0.001
ADAM_B1 = 0.9
ADAM_B2 = 0.999
ADAM_EPS = 1e-08
ADAM_WD = 0.01
ADAM_STEP = 10

CHIPS3 = ((1, 0), (0, 1), (1, 1))
PEERS7 = tuple((dx, dy, dc) for dx in (0, 1) for dy in (0, 1) for dc in (0, 1) if (dx, dy, dc) != (0, 0, 0))


def _nn(a, b):
    return jnp.dot(a, b, preferred_element_type=F32)


def _nt(a, b):
    return lax.dot_general(a, b, (((1,), (1,)), ((), ())), preferred_element_type=F32)


def _tn(a, b):
    return lax.dot_general(a, b, (((0,), (0,)), ((), ())), preferred_element_type=F32)


def _split3(a):
    a0 = a.astype(BF16)
    r = a - a0.astype(F32)
    a1 = r.astype(BF16)
    a2 = (r - a1.astype(F32)).astype(BF16)
    return a0, a1, a2


def _dot3(dot, a, b):
    sa = _split3(a)
    sb = _split3(b)
    out = None
    for i in range(3):
        for j in range(3 - i):
            t = dot(sa[i], sb[j])
            out = t if out is None else out + t
    return out


def _sig(x):
    return 1.0 / (1.0 + jnp.exp(-x))


def _rot(t):
    lane = lax.broadcasted_iota(jnp.int32, t.shape, 1)
    return jnp.where((lane % 32) < 16, -pltpu.roll(t, 112, 1), pltpu.roll(t, 16, 1))


def _rope(t, cos, sin):
    return t * cos + _rot(t) * sin


def _rope_t(t, cos, sin):
    return t * cos - _rot(t * sin)


def _shift_rows(z, k):
    n = z.shape[0]
    return pltpu.roll(z, (n - k) % n, 0)


def _colsum(a):
    return jnp.sum(a, axis=0, keepdims=True)


def _rowsum(a):
    return jnp.sum(a, axis=-1, keepdims=True)


def _row_layout(col):
    return jnp.transpose(jnp.broadcast_to(col, (col.shape[0], 128)))[0:8, :]


def _params(sem=None):
    return pltpu.CompilerParams(dimension_semantics=sem, vmem_limit_bytes=VMEM_LIMIT)


def _full(shape):
    nd = len(shape)
    return pl.BlockSpec(shape, lambda *_: (0,) * nd)


def _peer(x, y, c, off):
    dx, dy, dc = off
    return ((x + dx) % 2, (y + dy) % 2, (c + dc) % 2)


def _modulated(x, mod_ref, ng):
    shift = mod_ref[0, 0:1, :]
    scale = mod_ref[0, 1:2, :]
    r = lax.rsqrt(jnp.mean(x * x, axis=-1, keepdims=True) + EPS)
    xh = x * r
    xg = xh * ng
    return r, xh, xg, xg * (1.0 + scale) + shift, scale


def _fwd_in(xall, modsel, norm_g, w_in, q_lora_g, w_uq, kv_lora_g, w_ukv, qn_g, kn_g, cos, sin, s_len):
    t_all = xall.shape[0]
    nb = t_all // TB
    off = (t_all - s_len) // TB

    def body(x_ref, mod_ref, ng_ref, win_ref, qlg_ref, wuq_ref, kvlg_ref, wukv_ref, qng_ref, kng_ref, cos_ref, sin_ref,
             u_ref, q_ref, k_ref, v_ref):
        _, _, _, h, _ = _modulated(x_ref[...], mod_ref, ng_ref[...])
        u = _nn(h.astype(BF16), win_ref[...])
        u_ref[...] = u
        cos = cos_ref[...]
        sin = sin_ref[...]
        cq = u[:, 0:QL]
        cqn = cq * lax.rsqrt(jnp.mean(cq * cq, axis=-1, keepdims=True) + EPS) * qlg_ref[...]
        qraw = _nn(cqn.astype(BF16), wuq_ref[...])
        qng = qng_ref[...]
        for hd in range(NH):
            qh = qraw[:, hd * DKP:(hd + 1) * DKP]
            qn = qh * lax.rsqrt(_rowsum(qh * qh) / DK + EPS) * qng
            q_ref[hd] = (jnp.concatenate([qn[:, :128], _rope(qn[:, 128:], cos, sin)], axis=1) * SCALE).astype(BF16)
        ckv = u[:, QL:QL + KVL]
        ckvn = ckv * lax.rsqrt(jnp.mean(ckv * ckv, axis=-1, keepdims=True) + EPS) * kvlg_ref[...]
        kv = _nn(ckvn.astype(BF16), wukv_ref[...])
        kr = u[:, 384:512]
        skr = _rowsum(kr * kr)
        kng = kng_ref[...]
        for hd in range(NH):
            kn = kv[:, hd * 256:hd * 256 + 128]
            rk = lax.rsqrt((_rowsum(kn * kn) + skr) / DK + EPS)
            k_ref[hd] = jnp.concatenate([kn * rk * kng[:, :128], _rope(kr * rk * kng[:, 128:], cos, sin)], axis=1).astype(BF16)
            v_ref[hd] = kv[:, hd * 256 + 128:(hd + 1) * 256].astype(BF16)

    row = lambda w: pl.BlockSpec((TB, w), lambda i: (i, 0))
    heads = lambda w: pl.BlockSpec((NH, TB, w), lambda i: (0, i, 0))
    return pl.pallas_call(
        body, name="fwd_in", grid=(nb,),
        in_specs=[row(D), pl.BlockSpec((1, 3, D), lambda i: (jnp.minimum(i // off, 1), 0, 0)), _full((1, D)), _full((D, DU)),
                  _full((1, QL)), _full((QL, NH * DKP)), _full((1, KVL)), _full((KVL, NH * 256)), _full((1, DKP)),
                  _full((1, DKP)), row(128), row(128)],
        out_specs=[row(DU), heads(DKP), heads(DKP), heads(DV)],
        out_shape=[jax.ShapeDtypeStruct((t_all, DU), F32), jax.ShapeDtypeStruct((NH, t_all, DKP), BF16),
                   jax.ShapeDtypeStruct((NH, t_all, DKP), BF16), jax.ShapeDtypeStruct((NH, t_all, DV), BF16)],
        compiler_params=_params(("arbitrary",)),
    )(xall, modsel, norm_g, w_in, q_lora_g, w_uq, kv_lora_g, w_ukv, qn_g, kn_g, cos, sin)


def _attn_fwd(q, k, v, s_len):
    t_all = q.shape[1]
    off = (t_all - s_len) // TB
    nq = s_len // TB

    def body(q_ref, k_ref, v_ref, o_ref, lse_ref):
        s = _nt(q_ref[0], k_ref[0])
        m = jnp.max(s, axis=-1, keepdims=True)
        e = jnp.exp(s - m)
        l = _rowsum(e)
        o_ref[...] = _nn(e.astype(BF16), v_ref[0]) / l
        lse_ref[0, 0] = _row_layout(m + jnp.log(l))

    return pl.pallas_call(
        body, name="attn_fwd", grid=(NH, nq),
        in_specs=[pl.BlockSpec((1, TB, DKP), lambda h, i: (h, i + off, 0)), pl.BlockSpec((1, t_all, DKP), lambda h, i: (h, 0, 0)),
                  pl.BlockSpec((1, t_all, DV), lambda h, i: (h, 0, 0))],
        out_specs=[pl.BlockSpec((TB, DV), lambda h, i: (i, h)), pl.BlockSpec((1, 1, 8, TB), lambda h, i: (h, i, 0, 0))],
        out_shape=[jax.ShapeDtypeStruct((s_len, NH * DV), F32), jax.ShapeDtypeStruct((NH, nq, 8, TB), F32)],
        compiler_params=_params(("arbitrary", "arbitrary")),
    )(q, k, v)


def _out_stage(attn, u, xall, target, gate, w_pool, pool_scale, w_out, s_len):
    t_all = xall.shape[0]
    off = (t_all - s_len) // TB
    nq = s_len // TB
    hb = TB // HALO
    n = TB + 2 * HALO

    def body(attn_ref, ga_ref, pin_ref, pprev_ref, pnext_ref, gp_ref, x_ref, tgt_ref, gate_ref, wp_ref, ps_ref, wo_ref,
             dxn_ref, dattn_ref, dga_ref, dgp_ref, dpool_ref, dwo_ref, dgate_ref, dps_ref, dwp_ref, loss_ref):
        i = pl.program_id(0)

        @pl.when(i == 0)
        def _():
            dwo_ref[...] = jnp.zeros_like(dwo_ref)
            dgate_ref[...] = jnp.zeros_like(dgate_ref)
            dps_ref[...] = jnp.zeros_like(dps_ref)
            dwp_ref[...] = jnp.zeros_like(dwp_ref)
            loss_ref[...] = jnp.zeros_like(loss_ref)

        attn = attn_ref[...]
        ga = ga_ref[...]
        gp = gp_ref[...]
        pin = pin_ref[...]
        prev = jnp.where(i == 0, 0.0, pprev_ref[...])
        nxt = jnp.where(i == nq - 1, 0.0, pnext_ref[...])
        win = jnp.concatenate([prev, pin, nxt], axis=0)
        tg = i * TB + lax.broadcasted_iota(jnp.int32, (TB, 1), 0)
        pooled = []
        for g, w in enumerate(POOL_WINDOWS):
            a = win[:, g * 128:(g + 1) * 128]
            p = _shift_rows(a, -1) + a
            for step in (1, 2, 4):
                if w >= 4 * step:
                    p = _shift_rows(p, -step) + _shift_rows(p, step)
            cnt = (jnp.minimum(tg + w // 2, s_len) - jnp.maximum(tg - w // 2, 0)).astype(F32)
            pooled.append(p[HALO:HALO + TB] / cnt - a[HALO:HALO + TB])
        pooled_b = [p.astype(BF16) for p in pooled]
        z = jnp.concatenate([_nn(pooled_b[g], wp_ref[g]) for g in range(4)], axis=1)
        ps = ps_ref[...]
        yp = z * ps
        sga = _sig(ga)
        sila = ga * sga
        sgp = _sig(gp)
        silp = gp * sgp
        br = jnp.concatenate([sila * attn, silp * yp], axis=1).astype(BF16)
        y = _nn(br, wo_ref[...])
        gate = gate_ref[...]
        err = x_ref[...] + gate * y - tgt_ref[...]
        loss_ref[...] += _colsum(_rowsum(err * err)) * (0.5 / D)
        dxn = err * (1.0 / D)
        dxn_ref[...] = dxn
        dgate_ref[...] += _colsum(dxn * y)
        dy = (dxn * gate).astype(BF16)
        dwo_ref[...] += _tn(br, dy)
        dbr = _nt(dy, wo_ref[...])
        dbra = dbr[:, :512]
        dbrp = dbr[:, 512:]
        dattn_ref[...] = (dbra * sila).astype(BF16)
        dga_ref[...] = (dbra * attn * (sga * (1.0 + ga * (1.0 - sga)))).astype(BF16)
        dgp_ref[...] = (dbrp * yp * (sgp * (1.0 + gp * (1.0 - sgp)))).astype(BF16)
        dyp = dbrp * silp
        dps_ref[...] += _colsum(dyp * z)
        dz = (dyp * ps).astype(BF16)
        dpool = []
        for g in range(4):
            dzg = dz[:, g * 128:(g + 1) * 128]
            dwp_ref[g] += _tn(pooled_b[g], dzg)
            dpool.append(_nt(dzg, wp_ref[g]))
        dpool_ref[...] = jnp.concatenate(dpool, axis=1)

    lat = lambda w: pl.BlockSpec((TB, w), lambda i: (i, 0))
    ucol = lambda j: pl.BlockSpec((TB, 512), lambda i: (i + off, j))
    last8 = t_all // HALO - 1
    return pl.pallas_call(
        body, name="out_stage", grid=(nq,),
        in_specs=[lat(512), ucol(1), ucol(2),
                  pl.BlockSpec((HALO, 512), lambda i: ((i + off) * hb - 1, 2)),
                  pl.BlockSpec((HALO, 512), lambda i: (jnp.minimum((i + off + 1) * hb, last8), 2)),
                  ucol(3), pl.BlockSpec((TB, D), lambda i: (i + off, 0)), lat(D), _full((1, D)), _full((4, 128, 128)),
                  _full((1, 512)), _full((D, D))],
        out_specs=[lat(D), lat(512), lat(512), lat(512), lat(512),
                   _full((D, D)), _full((1, D)), _full((1, 512)), _full((4, 128, 128)), _full((1, 1))],
        out_shape=[jax.ShapeDtypeStruct((s_len, D), F32), jax.ShapeDtypeStruct((s_len, 512), BF16),
                   jax.ShapeDtypeStruct((s_len, 512), BF16), jax.ShapeDtypeStruct((s_len, 512), BF16),
                   jax.ShapeDtypeStruct((s_len, 512), F32),
                   jax.ShapeDtypeStruct((D, D), F32), jax.ShapeDtypeStruct((1, D), F32), jax.ShapeDtypeStruct((1, 512), F32),
                   jax.ShapeDtypeStruct((4, 128, 128), F32), jax.ShapeDtypeStruct((1, 1), F32)],
        compiler_params=_params(("arbitrary",)),
    )(attn, u, u, u, u, u, xall, target, gate, w_pool, pool_scale, w_out)


def _attn_bwd(q, k, v, dattn, attn, lse, s_len):
    t_all = q.shape[1]
    off = (t_all - s_len) // TB
    nq = s_len // TB
    nch = 4
    ch = t_all // nch

    def body(q_ref, k_ref, v_ref, do_ref, o_ref, lse_ref, dq_ref, dk_ref, dv_ref):
        i = pl.program_id(1)

        @pl.when(i == 0)
        def _():
            dk_ref[...] = jnp.zeros_like(dk_ref)
            dv_ref[...] = jnp.zeros_like(dv_ref)

        qb = q_ref[0]
        do = do_ref[...]
        lse_r = lse_ref[0, 0][0:1, :]
        delta_r = _row_layout(_rowsum(do.astype(F32) * o_ref[...]))[0:1, :]
        dq = jnp.zeros((TB, DKP), F32)
        for c in range(nch):
            rows = pl.ds(c * ch, ch)
            kc = k_ref[0, rows, :]
            p_t = jnp.exp(_nt(kc, qb) - lse_r)
            ds_t = (p_t * (_nt(v_ref[0, rows, :], do) - delta_r)).astype(BF16)
            dv_ref[0, rows, :] += _nn(p_t.astype(BF16), do)
            dk_ref[0, rows, :] += _nn(ds_t, qb)
            dq += _tn(ds_t, kc)
        dq_ref[0] = dq * SCALE

    kvspec = lambda w: pl.BlockSpec((1, t_all, w), lambda h, i: (h, 0, 0))
    rowspec = pl.BlockSpec((1, 1, 8, TB), lambda h, i: (h, i, 0, 0))
    return pl.pallas_call(
        body, name="attn_bwd", grid=(NH, nq),
        in_specs=[pl.BlockSpec((1, TB, DKP), lambda h, i: (h, i + off, 0)), kvspec(DKP), kvspec(DV),
                  pl.BlockSpec((TB, DV), lambda h, i: (i, h)), pl.BlockSpec((TB, DV), lambda h, i: (i, h)), rowspec],
        out_specs=[pl.BlockSpec((1, TB, DKP), lambda h, i: (h, i, 0)), kvspec(DKP), kvspec(DV)],
        out_shape=[jax.ShapeDtypeStruct((NH, s_len, DKP), F32), jax.ShapeDtypeStruct((NH, t_all, DKP), F32),
                   jax.ShapeDtypeStruct((NH, t_all, DV), F32)],
        compiler_params=_params(("arbitrary", "arbitrary")),
    )(q, k, v, dattn, attn, lse)


def _qkv_bwd(u, dq, dk, dv, cos, sin, q_lora_g, w_uq, kv_lora_g, w_ukv, qn_g, kn_g, s_len):
    t_all = u.shape[0]
    off = (t_all - s_len) // TB
    nb = t_all // TB

    def body(ulo_ref, dq_ref, dk_ref, dv_ref, cos_ref, sin_ref, qlg_ref, wuq_ref, kvlg_ref, wukv_ref, qng_ref, kng_ref,
             dlo_ref, dwuq_ref, dwukv_ref, dqlg_ref, dkvlg_ref, dqng_ref, dkng_ref):
        i = pl.program_id(0)

        @pl.when(i == 0)
        def _():
            for r in (dwuq_ref, dwukv_ref, dqlg_ref, dkvlg_ref, dqng_ref, dkng_ref):
                r[...] = jnp.zeros_like(r)

        latent = i >= off
        ulo = ulo_ref[...]
        cos = cos_ref[...]
        sin = sin_ref[...]
        cq = ulo[:, 0:QL]
        rc = lax.rsqrt(jnp.mean(cq * cq, axis=-1, keepdims=True) + EPS)
        cqh = cq * rc
        qlg = qlg_ref[...]
        cqn_b = (cqh * qlg).astype(BF16)
        qraw = _nn(cqn_b, wuq_ref[...])
        qng = qng_ref[...]
        dqng = jnp.zeros((1, DKP), F32)
        parts = []
        for hd in range(NH):
            qh = qraw[:, hd * DKP:(hd + 1) * DKP]
            rq = lax.rsqrt(_rowsum(qh * qh) / DK + EPS)
            xh = qh * rq
            dqh = jnp.where(latent, dq_ref[hd], 0.0)
            dyq = jnp.concatenate([dqh[:, :128], _rope_t(dqh[:, 128:], cos, sin)], axis=1)
            dqng += _colsum(dyq * xh)
            dxh = dyq * qng
            parts.append(rq * (dxh - xh * (_rowsum(dxh * xh) / DK)))
        dqng_ref[...] += dqng
        dqraw = jnp.concatenate(parts, axis=1).astype(BF16)
        dwuq_ref[...] += _tn(cqn_b, dqraw)
        dcqn = _nt(dqraw, wuq_ref[...])
        dqlg_ref[...] += _colsum(dcqn * cqh)
        dxh = dcqn * qlg
        dcq = rc * (dxh - cqh * jnp.mean(dxh * cqh, axis=-1, keepdims=True))

        ckv = ulo[:, QL:QL + KVL]
        r0 = lax.rsqrt(jnp.mean(ckv * ckv, axis=-1, keepdims=True) + EPS)
        ckvh = ckv * r0
        kvlg = kvlg_ref[...]
        ckvn_b = (ckvh * kvlg).astype(BF16)
        kv = _nn(ckvn_b, wukv_ref[...])
        kr = ulo[:, 384:512]
        skr = _rowsum(kr * kr)
        kng = kng_ref[...]
        dkr = jnp.zeros((TB, 128), F32)
        dkng = jnp.zeros((1, DKP), F32)
        parts = []
        for hd in range(NH):
            kn = kv[:, hd * 256:hd * 256 + 128]
            rk = lax.rsqrt((_rowsum(kn * kn) + skr) / DK + EPS)
            xh1 = kn * rk
            xh2 = kr * rk
            dkh = dk_ref[hd]
            d1 = dkh[:, :128]
            d2 = _rope_t(dkh[:, 128:], cos, sin)
            dkng += jnp.concatenate([_colsum(d1 * xh1), _colsum(d2 * xh2)], axis=1)
            dx1 = d1 * kng[:, :128]
            dx2 = d2 * kng[:, 128:]
            dot = (_rowsum(dx1 * xh1) + _rowsum(dx2 * xh2)) / DK
            parts.append(rk * (dx1 - xh1 * dot))
            parts.append(dv_ref[hd])
            dkr += rk * (dx2 - xh2 * dot)
        dkng_ref[...] += dkng
        dkv = jnp.concatenate(parts, axis=1).astype(BF16)
        dwukv_ref[...] += _tn(ckvn_b, dkv)
        dckvn = _nt(dkv, wukv_ref[...])
        dkvlg_ref[...] += _colsum(dckvn * ckvh)
        dxh = dckvn * kvlg
        dckv = r0 * (dxh - ckvh * jnp.mean(dxh * ckvh, axis=-1, keepdims=True))
        dlo_ref[...] = jnp.concatenate([dcq, dckv, dkr], axis=1).astype(BF16)

    row = lambda w: pl.BlockSpec((TB, w), lambda i: (i, 0))
    heads = lambda w: pl.BlockSpec((NH, TB, w), lambda i: (0, i, 0))
    return pl.pallas_call(
        body, name="qkv_bwd", grid=(nb,),
        in_specs=[row(512), pl.BlockSpec((NH, TB, DKP), lambda i: (0, jnp.maximum(i - off, 0), 0)), heads(DKP), heads(DV),
                  row(128), row(128), _full((1, QL)), _full((QL, NH * DKP)), _full((1, KVL)), _full((KVL, NH * 256)),
                  _full((1, DKP)), _full((1, DKP))],
        out_specs=[row(512), _full((QL, NH * DKP)), _full((KVL, NH * 256)), _full((1, QL)), _full((1, KVL)),
                   _full((1, DKP)), _full((1, DKP))],
        out_shape=[jax.ShapeDtypeStruct((t_all, 512), BF16), jax.ShapeDtypeStruct((QL, NH * DKP), F32),
                   jax.ShapeDtypeStruct((KVL, NH * 256), F32), jax.ShapeDtypeStruct((1, QL), F32),
                   jax.ShapeDtypeStruct((1, KVL), F32), jax.ShapeDtypeStruct((1, DKP), F32), jax.ShapeDtypeStruct((1, DKP), F32)],
        compiler_params=_params(("arbitrary",)),
    )(u, dq, dk, dv, cos, sin, q_lora_g, w_uq, kv_lora_g, w_ukv, qn_g, kn_g)


def _in_bwd(xall, modsel, norm_g, dlo, dga, dgp, dpool, dxn, w_in, s_len):
    t_all = xall.shape[0]
    off = (t_all - s_len) // TB
    nb = t_all // TB
    nq = s_len // TB
    hb = TB // HALO
    n = TB + 2 * HALO

    def body(x_ref, mod_ref, ng_ref, dlo_ref, dga_ref, dgp_ref, dp_ref, dpprev_ref, dpnext_ref, dxn_ref, win_ref,
             gx_ref, dwin_ref, dmod_ref, dng_ref):
        i = pl.program_id(0)
        j = i - off

        @pl.when(i == 0)
        def _():
            dwin_ref[...] = jnp.zeros_like(dwin_ref)
            dmod_ref[...] = jnp.zeros_like(dmod_ref)
            dng_ref[...] = jnp.zeros_like(dng_ref)

        latent = i >= off
        dp = dp_ref[...]
        prev = jnp.where(j <= 0, 0.0, dpprev_ref[...])
        nxt = jnp.where(j >= nq - 1, 0.0, dpnext_ref[...])
        win = jnp.concatenate([prev, dp, nxt], axis=0)
        tg = j * TB - HALO + lax.broadcasted_iota(jnp.int32, (n, 1), 0)
        dpin = []
        for g, w in enumerate(POOL_WINDOWS):
            cnt = jnp.maximum(jnp.minimum(tg + w // 2, s_len) - jnp.maximum(tg - w // 2, 0), 1).astype(F32)
            zq = win[:, g * 128:(g + 1) * 128] / cnt
            zq = zq + _shift_rows(zq, 1)
            for step in (1, 2, 4):
                if w >= 4 * step:
                    zq = _shift_rows(zq, -step) + _shift_rows(zq, step)
            dpin.append(zq[HALO:HALO + TB] - dp[:, g * 128:(g + 1) * 128])
        dhi = jnp.concatenate([dga_ref[...], jnp.concatenate(dpin, axis=1).astype(BF16), dgp_ref[...]], axis=1)
        dhi = jnp.where(latent, dhi, jnp.zeros_like(dhi))
        du = jnp.concatenate([dlo_ref[...], dhi], axis=1)

        ng = ng_ref[...]
        r, xh, xg, h, scale = _modulated(x_ref[...], mod_ref, ng)
        dwin_ref[...] += _tn(h.astype(BF16), du)
        dh = _nt(du, win_ref[...])
        is_lat = latent.astype(F32)
        dsh = _colsum(dh)
        dsc = _colsum(dh * xg)
        dmod_ref[0, 0:1, :] += dsh * (1.0 - is_lat)
        dmod_ref[0, 1:2, :] += dsc * (1.0 - is_lat)
        dmod_ref[1, 0:1, :] += dsh * is_lat
        dmod_ref[1, 1:2, :] += dsc * is_lat
        dxg = dh * (1.0 + scale)
        dng_ref[...] += _colsum(dxg * xh)
        dxh = dxg * ng
        gx_ref[...] = r * (dxh - xh * jnp.mean(dxh * xh, axis=-1, keepdims=True)) + dxn_ref[...]

    row = lambda w: pl.BlockSpec((TB, w), lambda i: (i, 0))
    lat = lambda w: pl.BlockSpec((TB, w), lambda i: (jnp.maximum(i - off, 0), 0))
    last8 = s_len // HALO - 1
    return pl.pallas_call(
        body, name="in_bwd", grid=(nb,),
        in_specs=[row(D), pl.BlockSpec((1, 3, D), lambda i: (jnp.minimum(i // off, 1), 0, 0)), _full((1, D)), row(512), lat(512),
                  lat(512), lat(512),
                  pl.BlockSpec((HALO, 512), lambda i: (jnp.maximum(jnp.maximum(i - off, 0) * hb - 1, 0), 0)),
                  pl.BlockSpec((HALO, 512), lambda i: (jnp.minimum((jnp.maximum(i - off, 0) + 1) * hb, last8), 0)),
                  lat(D), _full((D, DU))],
        out_specs=[lat(D), _full((D, DU)), _full((2, 2, D)), _full((1, D))],
        out_shape=[jax.ShapeDtypeStruct((s_len, D), F32), jax.ShapeDtypeStruct((D, DU), F32),
                   jax.ShapeDtypeStruct((2, 2, D), F32), jax.ShapeDtypeStruct((1, D), F32)],
        compiler_params=_params(("arbitrary",)),
    )(xall, modsel, norm_g, dlo, dga, dgp, dpool, dpool, dpool, dxn, w_in)


def _adamw_update(w_ref, g_ref, m_ref, v_ref, d_ref, mo_ref, vo_ref):
    gv = g_ref[...]
    mn = ADAM_B1 * m_ref[...] + (1.0 - ADAM_B1) * gv
    vn = ADAM_B2 * v_ref[...] + (1.0 - ADAM_B2) * (gv * gv)
    m_hat = mn / (1.0 - ADAM_B1 ** ADAM_STEP)
    v_hat = vn / (1.0 - ADAM_B2 ** ADAM_STEP)
    d_ref[...] = -ADAM_LR * (m_hat / (jnp.sqrt(v_hat) + ADAM_EPS) + ADAM_WD * w_ref[...])
    mo_ref[...] = mn
    vo_ref[...] = vn


def _adamw_small(ws, gs, ms, vs):
    n = len(ws)

    def body(*refs):
        for i in range(n):
            _adamw_update(refs[i], refs[n + i], refs[2 * n + i], refs[3 * n + i], refs[4 * n + i], refs[5 * n + i], refs[6 * n + i])

    shp = [jax.ShapeDtypeStruct(w.shape, F32) for w in ws]
    out = pl.pallas_call(body, name="adamw_small", in_specs=[VM] * (4 * n), out_specs=[VM] * (3 * n), out_shape=shp * 3,
                         compiler_params=pltpu.CompilerParams(vmem_limit_bytes=VMEM_LIMIT))(*ws, *gs, *ms, *vs)
    return out[:n], out[n:2 * n], out[2 * n:]


def _adamw(w, g, m, v, name):
    rows, cols = w.shape
    rb = 256 if rows % 256 == 0 else rows

    def body(w_ref, g_ref, m_ref, v_ref, d_ref, mo_ref, vo_ref):
        _adamw_update(w_ref, g_ref, m_ref, v_ref, d_ref, mo_ref, vo_ref)

    spec = pl.BlockSpec((rb, cols), lambda i: (i, 0))
    shp = jax.ShapeDtypeStruct((rows, cols), F32)
    return pl.pallas_call(
        body, name=name, grid=(rows // rb,), in_specs=[spec] * 4, out_specs=[spec] * 3, out_shape=[shp] * 3,
        compiler_params=_params(("arbitrary",)),
    )(w, g, m, v)


VM = pl.BlockSpec(memory_space=pltpu.VMEM)
ANY = pl.BlockSpec(memory_space=pl.ANY)


def _select_rows(slots_ref, n_slots, row=0):
    sub = lax.broadcasted_iota(jnp.int32, (8, 1), 0)
    out = None
    for d in range(n_slots):
        r = jnp.where(sub == d, jnp.broadcast_to(slots_ref[d][row:row + 1, :], (8, slots_ref.shape[-1])), 0.0)
        out = r if out is None else out + r
    return out


def _mod_exchange(c, c_ctx, w_mod, b_mod_k):
    kw = w_mod.shape[1]

    def body(c_ref, cc_ref, w_ref, b_ref, a16_ref, mod_ref, a_ref, send_sems, recv_sems):
        x, y, cc = lax.axis_index("x"), lax.axis_index("y"), lax.axis_index("c")
        me = 4 * x + 2 * y + cc
        k = 2 * x + y
        cv = c_ref[...]
        a_ref[me] = jnp.broadcast_to(cv * _sig(cv), (8, D))
        sends = []
        for j, off in enumerate(PEERS7):
            cp = pltpu.make_async_remote_copy(a_ref.at[me], a_ref.at[me], send_sems.at[j], recv_sems.at[j],
                                              device_id=_peer(x, y, cc, off), device_id_type=MESH)
            cp.start()
            sends.append(cp)
        for j, off in enumerate(PEERS7):
            px, py, pc = _peer(x, y, cc, off)
            src = a_ref.at[4 * px + 2 * py + pc]
            pltpu.make_async_remote_copy(src, src, send_sems.at[j], recv_sems.at[j], device_id=(px, py, pc),
                                         device_id_type=MESH).wait_recv()
        ccv = cc_ref[...]
        sub = lax.broadcasted_iota(jnp.int32, (8, 1), 0)
        a16 = jnp.concatenate([_select_rows(a_ref, 8), jnp.where(sub == 0, jnp.broadcast_to(ccv * _sig(ccv), (8, D)), 0.0)], axis=0)
        a16_ref[...] = a16
        mod_ref[k] = _dot3(_nn, a16, w_ref[...]) + b_ref[...]
        for j, off in enumerate(CHIPS3):
            cp = pltpu.make_async_remote_copy(mod_ref.at[k], mod_ref.at[k], send_sems.at[7 + j], recv_sems.at[7 + j],
                                              device_id=_peer(x, y, cc, off + (0,)), device_id_type=MESH)
            cp.start()
            sends.append(cp)
        for j, off in enumerate(CHIPS3):
            px, py, pc = _peer(x, y, cc, off + (0,))
            src = mod_ref.at[2 * px + py]
            pltpu.make_async_remote_copy(src, src, send_sems.at[7 + j], recv_sems.at[7 + j], device_id=(px, py, pc),
                                         device_id_type=MESH).wait_recv()
        for cp in sends:
            cp.wait_send()

    return pl.pallas_call(
        body, name="mod_exchange", in_specs=[VM] * 4, out_specs=[VM, VM],
        out_shape=[jax.ShapeDtypeStruct((16, D), F32), jax.ShapeDtypeStruct((4, 16, kw), F32)],
        scratch_shapes=[pltpu.VMEM((8, 8, D), F32), pltpu.SemaphoreType.DMA((10,)), pltpu.SemaphoreType.DMA((10,))],
        compiler_params=pltpu.CompilerParams(vmem_limit_bytes=VMEM_LIMIT),
    )(c, c_ctx, w_mod, b_mod_k)


def _gather_weights(shards):
    nw = len(shards)

    def body(*refs):
        w_refs = refs[:nw]
        g_refs = refs[nw:2 * nw]
        send_sems, recv_sems, local_sems = refs[2 * nw:]
        x, y, cc = lax.axis_index("x"), lax.axis_index("y"), lax.axis_index("c")
        k = 2 * x + y
        sibling = (x, y, 1 - cc)
        waits = []
        for wi in range(nw):
            lc = pltpu.make_async_copy(w_refs[wi], g_refs[wi].at[k], local_sems.at[wi])
            lc.start()
            waits.append(lc)
        sends = []
        for wi in range(nw):
            hr = w_refs[wi].shape[0] // 2
            mine = pl.ds(pl.multiple_of(cc * hr, 16), hr)
            for j, off in enumerate(CHIPS3):
                cp = pltpu.make_async_remote_copy(w_refs[wi].at[mine, :], g_refs[wi].at[k, mine, :], send_sems.at[wi * 6 + j],
                                                  recv_sems.at[wi * 6 + j], device_id=_peer(x, y, cc, off + (0,)),
                                                  device_id_type=MESH)
                cp.start()
                sends.append(cp)
        for wi in range(nw):
            hr = w_refs[wi].shape[0] // 2
            mine = pl.ds(pl.multiple_of(cc * hr, 16), hr)
            for j, off in enumerate(CHIPS3):
                px, py, pc = _peer(x, y, cc, off + (0,))
                blk = g_refs[wi].at[2 * px + py, mine, :]
                pltpu.make_async_remote_copy(blk, blk, send_sems.at[wi * 6 + j], recv_sems.at[wi * 6 + j],
                                             device_id=(px, py, pc), device_id_type=MESH).wait_recv()
                fw = pltpu.make_async_remote_copy(blk, blk, send_sems.at[wi * 6 + 3 + j], recv_sems.at[wi * 6 + 3 + j],
                                                  device_id=sibling, device_id_type=MESH)
                fw.start()
                sends.append(fw)
        for wi in range(nw):
            hr = w_refs[wi].shape[0] // 2
            theirs = pl.ds(pl.multiple_of((1 - cc) * hr, 16), hr)
            for j, off in enumerate(CHIPS3):
                px, py, _ = _peer(x, y, cc, off + (0,))
                blk = g_refs[wi].at[2 * px + py, theirs, :]
                pltpu.make_async_remote_copy(blk, blk, send_sems.at[wi * 6 + 3 + j], recv_sems.at[wi * 6 + 3 + j],
                                             device_id=sibling, device_id_type=MESH).wait_recv()
        for cp in sends:
            cp.wait_send()
        for lc in waits:
            lc.wait()

    return pl.pallas_call(
        body, name="gather_weights", in_specs=[ANY] * nw, out_specs=[ANY] * nw,
        out_shape=[jax.ShapeDtypeStruct((4,) + s.shape, s.dtype) for s in shards],
        scratch_shapes=[pltpu.SemaphoreType.DMA((6 * nw,)), pltpu.SemaphoreType.DMA((6 * nw,)), pltpu.SemaphoreType.DMA((nw,))],
    )(*shards)


SMALL_ROW_WIDTHS = (D, QL, KVL, DKP, DKP, 512, 128)
SMALL_OUT_WIDTHS = (D, QL, KVL, DK, DK, 512, 1)


def _reduce_grads(grads, smalls, w_pool_g):
    nw = len(grads)
    ns = len(smalls)
    halves = [(g.shape[1] // 2, g.shape[2]) for g in grads]

    def body(*refs):
        g_refs = refs[:nw]
        small_refs = refs[nw:nw + ns]
        wp_ref = refs[nw + ns]
        o = nw + ns + 1
        r_refs = refs[o:o + nw]
        small_outs = refs[o + nw:o + nw + ns]
        rwp_ref = refs[o + nw + ns]
        o = o + nw + ns + 1
        own, sib, part, got = (refs[o + i * nw:o + (i + 1) * nw] for i in range(4))
        smbuf, wpbuf, send_sems, recv_sems, local_sems = refs[o + 4 * nw:]
        x, y, cc = lax.axis_index("x"), lax.axis_index("y"), lax.axis_index("c")
        me = 4 * x + 2 * y + cc
        k = 2 * x + y
        sibling = (x, y, 1 - cc)
        sends = []

        def remote(src, dst, sem, to):
            cp = pltpu.make_async_remote_copy(src, dst, send_sems.at[sem], recv_sems.at[sem], device_id=to, device_id_type=MESH)
            sends.append(cp)
            return cp

        def arrived(dst, sem, frm):
            pltpu.make_async_remote_copy(dst, dst, send_sems.at[sem], recv_sems.at[sem], device_id=frm,
                                         device_id_type=MESH).wait_recv()

        locals_ = []
        for wi in range(nw):
            hr = halves[wi][0]
            lc = pltpu.make_async_copy(g_refs[wi].at[:, pl.ds(pl.multiple_of(cc * hr, 8), hr), :], own[wi], local_sems.at[wi])
            lc.start()
            locals_.append(lc)
            remote(g_refs[wi].at[:, pl.ds(pl.multiple_of((1 - cc) * hr, 8), hr), :], sib[wi], wi * 5, sibling).start()
        slot = smbuf.at[me]
        slot[...] = jnp.zeros((8, D), F32)
        for r, (ref, w) in enumerate(zip(small_refs, SMALL_ROW_WIDTHS)):
            slot[r:r + 1, 0:w] = jnp.broadcast_to(ref[...], (1, w))
        wpbuf[me] = wp_ref[...]
        for j, off in enumerate(PEERS7):
            peer = _peer(x, y, cc, off)
            remote(smbuf.at[me], smbuf.at[me], 5 * nw + j, peer).start()
            remote(wpbuf.at[me], wpbuf.at[me], 5 * nw + 7 + j, peer).start()
        for wi in range(nw):
            locals_[wi].wait()
            arrived(sib[wi], wi * 5, sibling)
            part[wi][...] = (own[wi][...] + sib[wi][...]).astype(BF16)
            got[wi][k] = part[wi][k]
            for j, off in enumerate(CHIPS3):
                px, py, pc = _peer(x, y, cc, off + (0,))
                remote(part[wi].at[2 * px + py], got[wi].at[k], wi * 5 + 1 + j, (px, py, pc)).start()
        for j, off in enumerate(PEERS7):
            px, py, pc = _peer(x, y, cc, off)
            arrived(smbuf.at[4 * px + 2 * py + pc], 5 * nw + j, (px, py, pc))
            arrived(wpbuf.at[4 * px + 2 * py + pc], 5 * nw + 7 + j, (px, py, pc))
        tot = smbuf[0]
        wpt = wpbuf[0]
        for d in range(1, 8):
            tot = tot + smbuf[d]
            wpt = wpt + wpbuf[d]
        for r, (ref, w) in enumerate(zip(small_outs, SMALL_OUT_WIDTHS)):
            ref[...] = tot[r:r + 1, 0:w]
        rwp_ref[...] = wpt
        for wi in range(nw):
            for j, off in enumerate(CHIPS3):
                px, py, pc = _peer(x, y, cc, off + (0,))
                arrived(got[wi].at[2 * px + py], wi * 5 + 1 + j, (px, py, pc))
            total = got[wi][0].astype(F32)
            for kk in range(1, 4):
                total = total + got[wi][kk].astype(F32)
            r_refs[wi][cc] = total
            remote(r_refs[wi].at[cc], r_refs[wi].at[cc], wi * 5 + 4, sibling).start()
        for wi in range(nw):
            arrived(r_refs[wi].at[1 - cc], wi * 5 + 4, sibling)
        for cp in sends:
            cp.wait_send()

    nsem = 5 * nw + 14
    quads = [(4,) + h for h in halves]
    small_shapes = [jax.ShapeDtypeStruct((1, w), F32) for w in SMALL_OUT_WIDTHS]
    return pl.pallas_call(
        body, name="reduce_grads", in_specs=[ANY] * nw + [VM] * (len(smalls) + 1), out_specs=[VM] * (nw + len(smalls) + 1),
        out_shape=[jax.ShapeDtypeStruct((2,) + h, F32) for h in halves] + small_shapes + [jax.ShapeDtypeStruct(w_pool_g.shape, F32)],
        scratch_shapes=[pltpu.VMEM(q, F32) for q in quads] + [pltpu.VMEM(q, F32) for q in quads]
        + [pltpu.VMEM(q, BF16) for q in quads] + [pltpu.VMEM(q, BF16) for q in quads]
        + [pltpu.VMEM((8, 8, D), F32), pltpu.VMEM((8,) + w_pool_g.shape, F32)]
        + [pltpu.SemaphoreType.DMA((nsem,)), pltpu.SemaphoreType.DMA((nsem,)), pltpu.SemaphoreType.DMA((nw,))],
        compiler_params=pltpu.CompilerParams(vmem_limit_bytes=VMEM_LIMIT),
    )(*grads, *smalls, w_pool_g)


def _mod_bwd(dmod8, a16, w_mod, c_ctx):
    kw = w_mod.shape[1]

    def body(dm_ref, a16_ref, w_ref, cc_ref, gw_ref, gb_ref, gc_ref, dm_all, pc_all, send_sems, recv_sems):
        x, y, cc = lax.axis_index("x"), lax.axis_index("y"), lax.axis_index("c")
        me = 4 * x + 2 * y + cc
        k = 2 * x + y
        dm_all[me] = dm_ref[...]
        sends = []
        for j, off in enumerate(PEERS7):
            cp = pltpu.make_async_remote_copy(dm_all.at[me], dm_all.at[me], send_sems.at[j], recv_sems.at[j],
                                              device_id=_peer(x, y, cc, off), device_id_type=MESH)
            cp.start()
            sends.append(cp)
        for j, off in enumerate(PEERS7):
            px, py, pc = _peer(x, y, cc, off)
            src = dm_all.at[4 * px + 2 * py + pc]
            pltpu.make_async_remote_copy(src, src, send_sems.at[j], recv_sems.at[j], device_id=(px, py, pc),
                                         device_id_type=MESH).wait_recv()
        dmc = dm_all[0][1:2, :]
        dml = dm_all[0][0:1, :]
        for d in range(1, 8):
            dmc = dmc + dm_all[d][1:2, :]
            dml = dml + dm_all[d][0:1, :]
        gb_ref[...] = dml + dmc
        sub = lax.broadcasted_iota(jnp.int32, (8, 1), 0)
        b16 = jnp.concatenate([_select_rows(dm_all, 8), jnp.where(sub == 0, jnp.broadcast_to(dmc, (8, 3 * D)), 0.0)], axis=0)
        bk = jnp.zeros((16, kw), F32)
        for kk in range(4):
            bk = bk + jnp.where(k == kk, b16[:, kk * kw:(kk + 1) * kw], 0.0)
        gw_ref[...] = _dot3(_tn, a16_ref[...], bk)
        pc_all[k] = _dot3(_nt, jnp.broadcast_to(bk[8:9, :], (8, kw)), w_ref[...])
        for j, off in enumerate(CHIPS3):
            cp = pltpu.make_async_remote_copy(pc_all.at[k], pc_all.at[k], send_sems.at[7 + j], recv_sems.at[7 + j],
                                              device_id=_peer(x, y, cc, off + (0,)), device_id_type=MESH)
            cp.start()
            sends.append(cp)
        for j, off in enumerate(CHIPS3):
            px, py, pc = _peer(x, y, cc, off + (0,))
            src = pc_all.at[2 * px + py]
            pltpu.make_async_remote_copy(src, src, send_sems.at[7 + j], recv_sems.at[7 + j], device_id=(px, py, pc),
                                         device_id_type=MESH).wait_recv()
        tot = pc_all[0][0:1, :] + pc_all[1][0:1, :] + pc_all[2][0:1, :] + pc_all[3][0:1, :]
        ccv = cc_ref[...]
        sg = _sig(ccv)
        gc_ref[...] = tot * (sg * (1.0 + ccv * (1.0 - sg)))
        for cp in sends:
            cp.wait_send()

    return pl.pallas_call(
        body, name="mod_bwd", in_specs=[VM] * 4, out_specs=[VM] * 3,
        out_shape=[jax.ShapeDtypeStruct((D, kw), F32), jax.ShapeDtypeStruct((1, 3 * D), F32), jax.ShapeDtypeStruct((1, D), F32)],
        scratch_shapes=[pltpu.VMEM((8, 8, 3 * D), F32), pltpu.VMEM((4, 8, D), F32), pltpu.SemaphoreType.DMA((10,)),
                        pltpu.SemaphoreType.DMA((10,))],
        compiler_params=pltpu.CompilerParams(vmem_limit_bytes=VMEM_LIMIT),
    )(dmod8, a16, w_mod, c_ctx)


def _rope_tables(s_len, lc):
    rows = s_len // GRID_W
    row = jnp.repeat(jnp.arange(rows, dtype=F32), GRID_W)
    col = jnp.tile(jnp.arange(GRID_W, dtype=F32), rows)
    n_freq = 16
    inv = ROPE_BASE ** (-jnp.arange(n_freq, dtype=F32) / n_freq)
    ang_r = row[:, None] * inv
    ang_c = col[:, None] * inv
    ang = jnp.concatenate([ang_r, ang_r, ang_c, ang_c], axis=-1)
    cos = jnp.concatenate([jnp.cos(ang), jnp.ones((s_len, 64), F32)], axis=1)
    sin = jnp.concatenate([jnp.sin(ang), jnp.zeros((s_len, 64), F32)], axis=1)
    cos = jnp.concatenate([jnp.ones((lc, 128), F32), cos], axis=0)
    sin = jnp.concatenate([jnp.zeros((lc, 128), F32), sin], axis=0)
    return cos, sin


def kernel(x, c, ctx, c_ctx, w_mod, b_mod, norm_g, w_in, q_lora_g, w_uq, kv_lora_g, w_ukv, q_norm_g, k_norm_g, w_pool, pool_scale, w_out, loss_target, m_c_ctx, m_w_mod, m_b_mod, m_norm_g, m_w_in, m_q_lora_g, m_w_uq, m_kv_lora_g, m_w_ukv, m_q_norm_g, m_k_norm_g, m_w_pool, m_pool_scale, m_w_out, v_c_ctx, v_w_mod, v_b_mod, v_norm_g, v_w_in, v_q_lora_g, v_w_uq, v_kv_lora_g, v_w_ukv, v_q_norm_g, v_k_norm_g, v_w_pool, v_pool_scale, v_w_out):
    xi, yi, ci = lax.axis_index("x"), lax.axis_index("y"), lax.axis_index("c")
    me = 4 * xi + 2 * yi + ci
    k = 2 * xi + yi
    s_len = x.shape[1]
    lc = ctx.shape[1]
    kw = w_mod.shape[2]
    weights = dict(c_ctx=c_ctx, w_mod=w_mod, b_mod=b_mod, norm_g=norm_g, w_in=w_in, q_lora_g=q_lora_g, w_uq=w_uq,
                   kv_lora_g=kv_lora_g, w_ukv=w_ukv, q_norm_g=q_norm_g, k_norm_g=k_norm_g, w_pool=w_pool,
                   pool_scale=pool_scale, w_out=w_out)
    m_in = dict(c_ctx=m_c_ctx, w_mod=m_w_mod, b_mod=m_b_mod, norm_g=m_norm_g, w_in=m_w_in, q_lora_g=m_q_lora_g, w_uq=m_w_uq,
                kv_lora_g=m_kv_lora_g, w_ukv=m_w_ukv, q_norm_g=m_q_norm_g, k_norm_g=m_k_norm_g, w_pool=m_w_pool,
                pool_scale=m_pool_scale, w_out=m_w_out)
    v_in = dict(c_ctx=v_c_ctx, w_mod=v_w_mod, b_mod=v_b_mod, norm_g=v_norm_g, w_in=v_w_in, q_lora_g=v_q_lora_g, w_uq=v_w_uq,
                kv_lora_g=v_kv_lora_g, w_ukv=v_w_ukv, q_norm_g=v_q_norm_g, k_norm_g=v_k_norm_g, w_pool=v_w_pool,
                pool_scale=v_pool_scale, w_out=v_w_out)
    order = ["c_ctx", "w_mod", "b_mod", "norm_g", "w_in", "q_lora_g", "w_uq", "kv_lora_g", "w_ukv", "q_norm_g", "k_norm_g",
             "w_pool", "pool_scale", "w_out"]

    c_ctx2 = c_ctx.reshape(1, D)
    b_mod_k = lax.dynamic_slice(b_mod, (0, k * kw), (1, kw))
    a16, mod_all = _mod_exchange(c, c_ctx2, w_mod[0], b_mod_k)
    mod_me = lax.dynamic_index_in_dim(mod_all, me, axis=1, keepdims=False).reshape(3, D)
    mod_c = mod_all[:, 8, :].reshape(3, D)
    modsel = jnp.stack([mod_c, mod_me])
    g_in, g_uq, g_ukv, g_out = _gather_weights([w_in[0].astype(BF16), w_uq[0].astype(BF16), w_ukv[0].astype(BF16),
                                                w_out[0].astype(BF16)])
    w_in_p = jnp.concatenate([g_in[0][:, :U_KR_END], jnp.zeros((D, 64), BF16), g_in[0][:, U_KR_END:], g_in[1], g_in[2], g_in[3]],
                             axis=1)
    w_uq_p = jnp.concatenate([p for j in range(4) for p in (g_uq[j], jnp.zeros((QL, DKP - DK), BF16))], axis=1)
    w_ukv_f = jnp.concatenate([g_ukv[j] for j in range(4)], axis=1)
    w_out_f = g_out.reshape(D, D)
    qn_g = jnp.pad(q_norm_g, ((0, 0), (0, DKP - DK)))
    kn_g = jnp.pad(k_norm_g, ((0, 0), (0, DKP - DK)))
    w_pool_b = w_pool[0].astype(BF16)
    cos, sin = _rope_tables(s_len, lc)

    xall = jnp.concatenate([ctx[0], x[0]], axis=0)
    u, q, kk, v = _fwd_in(xall, modsel, norm_g, w_in_p, q_lora_g, w_uq_p, kv_lora_g, w_ukv_f, qn_g, kn_g, cos, sin, s_len)
    attn, lse = _attn_fwd(q, kk, v, s_len)
    nqb = s_len // Q_BLOCK
    attn_t = jnp.transpose(attn.reshape(nqb, Q_BLOCK, NH * DV), (1, 0, 2)).reshape(s_len, NH * DV)
    (dxn, dattn_t, dga, dgp, dpool, dw_out, dgate, dps, dw_pool, loss) = _out_stage(
        attn_t, u, xall, loss_target[0], modsel[1, 2:3, :], w_pool_b, pool_scale, w_out_f, s_len)
    dattn = jnp.transpose(dattn_t.reshape(Q_BLOCK, nqb, NH * DV), (1, 0, 2)).reshape(s_len, NH * DV)
    dq, dk, dv = _attn_bwd(q, kk, v, dattn, attn, lse, s_len)
    dlo, dw_uq_p, dw_ukv, dqlg, dkvlg, dqng, dkng = _qkv_bwd(u, dq, dk, dv, cos, sin, q_lora_g, w_uq_p, kv_lora_g, w_ukv_f,
                                                            qn_g, kn_g, s_len)
    gx, dw_in_p, dmod, dng = _in_bwd(xall, modsel, norm_g, dlo, dga, dgp, dpool, dxn, w_in_p, s_len)

    dmod_l = jnp.concatenate([dmod[1, 0], dmod[1, 1], dgate[0]]).reshape(1, 3 * D)
    dmod_c = jnp.concatenate([dmod[0, 0], dmod[0, 1], jnp.zeros((D,), F32)]).reshape(1, 3 * D)
    dmod8 = jnp.concatenate([dmod_l, dmod_c, jnp.zeros((6, 3 * D), F32)], axis=0)
    g_w_mod, g_b_mod, g_c_ctx = _mod_bwd(dmod8, a16, w_mod[0], c_ctx2)

    dw_in = jnp.concatenate([dw_in_p[:, :U_KR_END], dw_in_p[:, U_KR_END + 64:]], axis=1)
    dw_uq = dw_uq_p.reshape(QL, NH, DKP)[:, :, :DK].reshape(QL, NH * DK)
    by_chip = lambda a: jnp.transpose(a.reshape(a.shape[0], 4, a.shape[1] // 4), (1, 0, 2))
    (r_in, r_uq, r_ukv, r_out, g_ng, g_qlg, g_kvlg, g_qng, g_kng, g_ps, loss_all, g_wp) = _reduce_grads(
        [by_chip(dw_in), by_chip(dw_uq), by_chip(dw_ukv), dw_out.reshape(4, D // 4, D)],
        [dng, dqlg, dkvlg, dqng, dkng, dps, loss], dw_pool)
    grads = dict(c_ctx=g_c_ctx.reshape(D), b_mod=g_b_mod, w_mod=g_w_mod[None], w_in=r_in.reshape(1, D, -1),
                 w_uq=r_uq.reshape(1, QL, -1), w_ukv=r_ukv.reshape(1, KVL, -1), w_out=r_out.reshape(1, -1, D), norm_g=g_ng,
                 q_lora_g=g_qlg, kv_lora_g=g_kvlg, q_norm_g=g_qng, k_norm_g=g_kng, pool_scale=g_ps, w_pool=g_wp[None])

    delta_w, new_m, new_v = {}, {}, {}
    for n in ("w_mod", "w_in", "w_uq", "w_ukv", "w_out"):
        shp = weights[n].shape
        d_, m_, v_ = _adamw(weights[n][0], grads[n][0], m_in[n][0], v_in[n][0], "adamw_" + n)
        delta_w[n], new_m[n], new_v[n] = d_.reshape(shp), m_.reshape(shp), v_.reshape(shp)
    small = [n for n in order if n not in delta_w]
    flat2d = lambda a: a.reshape(-1, a.shape[-1])
    outs = _adamw_small(*[[flat2d(src[n]) for n in small] for src in (weights, grads, m_in, v_in)])
    for dst, arrs in zip((delta_w, new_m, new_v), outs):
        dst.update({n: a.reshape(weights[n].shape) for n, a in zip(small, arrs)})

    return (loss_all[0, 0], gx[None], *[grads[n] for n in order], *[delta_w[n] for n in order], *[new_m[n] for n in order],
            *[new_v[n] for n in order])
```

```python
import jax
import jax.numpy as jnp
from jax import lax
from jax.experimental import pallas as pl
from jax.experimental.pallas import tpu as pltpu

F32 = jnp.float32
BF16 = jnp.bfloat16
MESH = pl.DeviceIdType.MESH

D = 1024
NH = 4
DK = 192
DKP = 256
DV = 128
QL = 256
KVL = 128
DIN = 1984
U_LO = 448
SEG = ((0, 512), (448, 960), (960, 1472), (1472, 1984))
DU = 2048
POOL_WINDOWS = (2, 4, 8, 16)
HALO = 8
EPS = 1e-6
ROPE_BASE = 10000.0
GRID_W = 64
Q_BLOCK = 128
TB = 256
SCALE = DK ** -0.5
VMEM_LIMIT = 56 * 1024 * 1024

ADAM_LR = 0.001
ADAM_B1 = 0.9
ADAM_B2 = 0.999
ADAM_EPS = 1e-08
ADAM_WD = 0.01
ADAM_STEP = 10

CHIPS3 = ((1, 0), (0, 1), (1, 1))
PEERS7 = tuple((dx, dy, dc) for dx in (0, 1) for dy in (0, 1) for dc in (0, 1) if (dx, dy, dc) != (0, 0, 0))

VM = pl.BlockSpec(memory_space=pltpu.VMEM)
ANY = pl.BlockSpec(memory_space=pl.ANY)


def _nn(a, b):
    return jnp.dot(a, b, preferred_element_type=F32)


def _nt(a, b):
    return lax.dot_general(a, b, (((1,), (1,)), ((), ())), preferred_element_type=F32)


def _tn(a, b):
    return lax.dot_general(a, b, (((0,), (0,)), ((), ())), preferred_element_type=F32)


def _split3(a):
    a0 = a.astype(BF16)
    r = a - a0.astype(F32)
    a1 = r.astype(BF16)
    a2 = (r - a1.astype(F32)).astype(BF16)
    return a0, a1, a2


def _dot3(dot, a, b):
    sa = _split3(a)
    sb = _split3(b)
    out = None
    for i in range(3):
        for j in range(3 - i):
            t = dot(sa[i], sb[j])
            out = t if out is None else out + t
    return out


def _sig(x):
    return 1.0 / (1.0 + jnp.exp(-x))


def _rot(t):
    lane = lax.broadcasted_iota(jnp.int32, t.shape, 1)
    return jnp.where((lane % 32) < 16, -pltpu.roll(t, 112, 1), pltpu.roll(t, 16, 1))


def _rope(t, cos, sin):
    return t * cos + _rot(t) * sin


def _rope_t(t, cos, sin):
    return t * cos - _rot(t * sin)


def _shift_rows(z, k):
    n = z.shape[0]
    return pltpu.roll(z, (n - k) % n, 0)


def _colsum(a):
    return jnp.sum(a, axis=0, keepdims=True)


def _rowsum(a):
    return jnp.sum(a, axis=-1, keepdims=True)


def _row_layout(col):
    return jnp.transpose(jnp.broadcast_to(col, (col.shape[0], 128)))[0:8, :]


def _params(sem=None):
    return pltpu.CompilerParams(dimension_semantics=sem, vmem_limit_bytes=VMEM_LIMIT)


def _full(shape):
    nd = len(shape)
    return pl.BlockSpec(shape, lambda *_: (0,) * nd)


def _peer(x, y, c, off):
    dx, dy, dc = off
    return ((x + dx) % 2, (y + dy) % 2, (c + dc) % 2)


def _token_specs(off):
    ctx = pl.BlockSpec((TB, D), lambda i: (jnp.minimum(i, off - 1), 0))
    lat = pl.BlockSpec((TB, D), lambda i: (jnp.maximum(i - off, 0), 0))
    mod = pl.BlockSpec((1, 3, D), lambda i: (jnp.minimum(i // off, 1), 0, 0))
    return ctx, lat, mod


def _modulated(x, mod_ref, ng):
    shift = mod_ref[0, 0:1, :]
    scale = mod_ref[0, 1:2, :]
    r = lax.rsqrt(jnp.mean(x * x, axis=-1, keepdims=True) + EPS)
    xh = x * r
    xg = xh * ng
    return r, xh, xg, xg * (1.0 + scale) + shift, scale


def _fwd_in(ctx, x, modsel, norm_g, w_in_t, q_lora_g, w_uq_t, kv_lora_g, w_ukv, qn_g, kn_g, cos, sin):
    s_len, lc = x.shape[0], ctx.shape[0]
    t_all = s_len + lc
    nb = t_all // TB
    off = lc // TB

    def body(ctx_ref, x_ref, mod_ref, ng_ref, win_ref, qlg_ref, wuq_ref, kvlg_ref, wukv_ref, qng_ref, kng_ref, cos_ref, sin_ref,
             u_ref, q_ref, k_ref, v_ref):
        i = pl.program_id(0)
        xb = jnp.where(i < off, ctx_ref[...], x_ref[...])
        _, _, _, h, _ = _modulated(xb, mod_ref, ng_ref[...])
        hb = h.astype(BF16)
        lane = lax.broadcasted_iota(jnp.int32, (TB, 512), 1)
        ulo = jnp.where(lane < U_LO, _nt(hb, win_ref[SEG[0][0]:SEG[0][1], :]), 0.0)
        u_ref[:, 0:512] = ulo
        for j in range(1, 4):
            u_ref[:, j * 512:(j + 1) * 512] = _nt(hb, win_ref[SEG[j][0]:SEG[j][1], :])
        cos = cos_ref[...]
        sin = sin_ref[...]
        cq = ulo[:, 0:QL]
        cqn = (cq * lax.rsqrt(jnp.mean(cq * cq, axis=-1, keepdims=True) + EPS) * qlg_ref[...]).astype(BF16)
        qng = qng_ref[...]
        for hd in range(NH):
            qh = _nt(cqn, wuq_ref[hd])
            qn = qh * lax.rsqrt(_rowsum(qh * qh) / DK + EPS) * qng
            q_ref[hd] = (jnp.concatenate([qn[:, :128], _rope(qn[:, 128:], cos, sin)], axis=1) * SCALE).astype(BF16)
        ckv = ulo[:, QL:QL + KVL]
        ckvn = (ckv * lax.rsqrt(jnp.mean(ckv * ckv, axis=-1, keepdims=True) + EPS) * kvlg_ref[...]).astype(BF16)
        kr = ulo[:, 384:512]
        skr = _rowsum(kr * kr)
        kng = kng_ref[...]
        for hd in range(NH):
            kv = _nn(ckvn, wukv_ref[hd])
            kn = kv[:, :128]
            rk = lax.rsqrt((_rowsum(kn * kn) + skr) / DK + EPS)
            k_ref[hd] = jnp.concatenate([kn * rk * kng[:, :128], _rope(kr * rk * kng[:, 128:], cos, sin)], axis=1).astype(BF16)
            v_ref[hd] = kv[:, 128:].astype(BF16)

    row = lambda w: pl.BlockSpec((TB, w), lambda i: (i, 0))
    heads = lambda w: pl.BlockSpec((NH, TB, w), lambda i: (0, i, 0))
    cspec, xspec, mspec = _token_specs(off)
    return pl.pallas_call(
        body, name="fwd_in", grid=(nb,),
        in_specs=[cspec, xspec, mspec, _full((1, D)), _full((DIN, D)), _full((1, QL)), _full((NH, DKP, QL)), _full((1, KVL)),
                  _full((NH, KVL, 256)), _full((1, DKP)), _full((1, DKP)), row(128), row(128)],
        out_specs=[row(DU), heads(DKP), heads(DKP), heads(DV)],
        out_shape=[jax.ShapeDtypeStruct((t_all, DU), F32), jax.ShapeDtypeStruct((NH, t_all, DKP), BF16),
                   jax.ShapeDtypeStruct((NH, t_all, DKP), BF16), jax.ShapeDtypeStruct((NH, t_all, DV), BF16)],
        compiler_params=_params(("arbitrary",)),
    )(ctx, x, modsel, norm_g, w_in_t, q_lora_g, w_uq_t, kv_lora_g, w_ukv, qn_g, kn_g, cos, sin)


def _attn_fwd(q, k, v, s_len):
    t_all = q.shape[1]
    off = (t_all - s_len) // TB
    nq = s_len // TB

    def body(q_ref, k_ref, v_ref, o_ref, lse_ref):
        s = _nt(q_ref[0], k_ref[0])
        m = jnp.max(s, axis=-1, keepdims=True)
        e = jnp.exp(s - m)
        l = _rowsum(e)
        o_ref[...] = _nn(e.astype(BF16), v_ref[0]) / l
        lse_ref[0, 0] = _row_layout(m + jnp.log(l))

    return pl.pallas_call(
        body, name="attn_fwd", grid=(NH, nq),
        in_specs=[pl.BlockSpec((1, TB, DKP), lambda h, i: (h, i + off, 0)), pl.BlockSpec((1, t_all, DKP), lambda h, i: (h, 0, 0)),
                  pl.BlockSpec((1, t_all, DV), lambda h, i: (h, 0, 0))],
        out_specs=[pl.BlockSpec((TB, DV), lambda h, i: (i, h)), pl.BlockSpec((1, 1, 8, TB), lambda h, i: (h, i, 0, 0))],
        out_shape=[jax.ShapeDtypeStruct((s_len, NH * DV), F32), jax.ShapeDtypeStruct((NH, nq, 8, TB), F32)],
        compiler_params=_params(("arbitrary", "arbitrary")),
    )(q, k, v)


def _out_stage(attn, u, x, target, gate, w_pool, pool_scale, w_out, lc):
    s_len = x.shape[0]
    t_all = s_len + lc
    off = lc // TB
    nq = s_len // TB
    hb = TB // HALO

    def body(attn_ref, ga_ref, pin_ref, pprev_ref, pnext_ref, gp_ref, x_ref, tgt_ref, gate_ref, wp_ref, ps_ref, wo_ref,
             dxn_ref, dattn_ref, dga_ref, dgp_ref, dpool_ref, dwo_ref, dgate_ref, dps_ref, dwp_ref, loss_ref):
        i = pl.program_id(0)

        @pl.when(i == 0)
        def _():
            dwo_ref[...] = jnp.zeros_like(dwo_ref)
            dgate_ref[...] = jnp.zeros_like(dgate_ref)
            dps_ref[...] = jnp.zeros_like(dps_ref)
            dwp_ref[...] = jnp.zeros_like(dwp_ref)
            loss_ref[...] = jnp.zeros_like(loss_ref)

        attn = attn_ref[...]
        ga = ga_ref[...]
        gp = gp_ref[...]
        pin = pin_ref[...]
        prev = jnp.where(i == 0, 0.0, pprev_ref[...])
        nxt = jnp.where(i == nq - 1, 0.0, pnext_ref[...])
        win = jnp.concatenate([prev, pin, nxt], axis=0)
        tg = i * TB + lax.broadcasted_iota(jnp.int32, (TB, 1), 0)
        pooled = []
        for g, w in enumerate(POOL_WINDOWS):
            a = win[:, g * 128:(g + 1) * 128]
            p = _shift_rows(a, -1) + a
            for step in (1, 2, 4):
                if w >= 4 * step:
                    p = _shift_rows(p, -step) + _shift_rows(p, step)
            cnt = (jnp.minimum(tg + w // 2, s_len) - jnp.maximum(tg - w // 2, 0)).astype(F32)
            pooled.append(p[HALO:HALO + TB] / cnt - a[HALO:HALO + TB])
        pooled_b = [p.astype(BF16) for p in pooled]
        z = jnp.concatenate([_nn(pooled_b[g], wp_ref[g]) for g in range(4)], axis=1)
        ps = ps_ref[...]
        yp = z * ps
        sga = _sig(ga)
        sila = ga * sga
        sgp = _sig(gp)
        silp = gp * sgp
        br = jnp.concatenate([sila * attn, silp * yp], axis=1).astype(BF16)
        y = _nn(br, wo_ref[...])
        gate = gate_ref[...]
        err = x_ref[...] + gate * y - tgt_ref[...]
        loss_ref[...] += _colsum(_rowsum(err * err)) * (0.5 / D)
        dxn = err * (1.0 / D)
        dxn_ref[...] = dxn
        dgate_ref[...] += _colsum(dxn * y)
        dy = (dxn * gate).astype(BF16)
        dwo_ref[...] += _tn(br, dy)
        dbr = _nt(dy, wo_ref[...])
        dbra = dbr[:, :512]
        dbrp = dbr[:, 512:]
        dattn_ref[...] = (dbra * sila).astype(BF16)
        dga_ref[...] = (dbra * attn * (sga * (1.0 + ga * (1.0 - sga)))).astype(BF16)
        dgp_ref[...] = (dbrp * yp * (sgp * (1.0 + gp * (1.0 - sgp)))).astype(BF16)
        dyp = dbrp * silp
        dps_ref[...] += _colsum(dyp * z)
        dz = (dyp * ps).astype(BF16)
        dpool = []
        for g in range(4):
            dzg = dz[:, g * 128:(g + 1) * 128]
            dwp_ref[g] += _tn(pooled_b[g], dzg)
            dpool.append(_nt(dzg, wp_ref[g]))
        dpool_ref[...] = jnp.concatenate(dpool, axis=1)

    lat = lambda w: pl.BlockSpec((TB, w), lambda i: (i, 0))
    ucol = lambda j: pl.BlockSpec((TB, 512), lambda i: (i + off, j))
    last8 = t_all // HALO - 1
    return pl.pallas_call(
        body, name="out_stage", grid=(nq,),
        in_specs=[lat(512), ucol(1), ucol(2),
                  pl.BlockSpec((HALO, 512), lambda i: ((i + off) * hb - 1, 2)),
                  pl.BlockSpec((HALO, 512), lambda i: (jnp.minimum((i + off + 1) * hb, last8), 2)),
                  ucol(3), lat(D), lat(D), _full((1, D)), _full((4, 128, 128)), _full((1, 512)), _full((D, D))],
        out_specs=[lat(D), lat(512), lat(512), lat(512), lat(512),
                   _full((D, D)), _full((1, D)), _full((1, 512)), _full((4, 128, 128)), _full((1, 1))],
        out_shape=[jax.ShapeDtypeStruct((s_len, D), F32), jax.ShapeDtypeStruct((s_len, 512), BF16),
                   jax.ShapeDtypeStruct((s_len, 512), BF16), jax.ShapeDtypeStruct((s_len, 512), BF16),
                   jax.ShapeDtypeStruct((s_len, 512), F32),
                   jax.ShapeDtypeStruct((D, D), F32), jax.ShapeDtypeStruct((1, D), F32), jax.ShapeDtypeStruct((1, 512), F32),
                   jax.ShapeDtypeStruct((4, 128, 128), F32), jax.ShapeDtypeStruct((1, 1), F32)],
        compiler_params=_params(("arbitrary",)),
    )(attn, u, u, u, u, u, x, target, gate, w_pool, pool_scale, w_out)


def _attn_bwd(q, k, v, dattn, attn, lse, s_len):
    t_all = q.shape[1]
    off = (t_all - s_len) // TB
    nq = s_len // TB
    nch = 4
    ch = t_all // nch

    def body(q_ref, k_ref, v_ref, do_ref, o_ref, lse_ref, dq_ref, dk_ref, dv_ref):
        i = pl.program_id(1)

        @pl.when(i == 0)
        def _():
            dk_ref[...] = jnp.zeros_like(dk_ref)
            dv_ref[...] = jnp.zeros_like(dv_ref)

        qb = q_ref[0]
        do = do_ref[...]
        lse_r = lse_ref[0, 0][0:1, :]
        delta_r = _row_layout(_rowsum(do.astype(F32) * o_ref[...]))[0:1, :]
        dq = jnp.zeros((TB, DKP), F32)
        for c in range(nch):
            rows = pl.ds(c * ch, ch)
            kc = k_ref[0, rows, :]
            p_t = jnp.exp(_nt(kc, qb) - lse_r)
            ds_t = (p_t * (_nt(v_ref[0, rows, :], do) - delta_r)).astype(BF16)
            dv_ref[0, rows, :] += _nn(p_t.astype(BF16), do)
            dk_ref[0, rows, :] += _nn(ds_t, qb)
            dq += _tn(ds_t, kc)
        dq_ref[0] = dq * SCALE

    kvspec = lambda w: pl.BlockSpec((1, t_all, w), lambda h, i: (h, 0, 0))
    rowspec = pl.BlockSpec((1, 1, 8, TB), lambda h, i: (h, i, 0, 0))
    return pl.pallas_call(
        body, name="attn_bwd", grid=(NH, nq),
        in_specs=[pl.BlockSpec((1, TB, DKP), lambda h, i: (h, i + off, 0)), kvspec(DKP), kvspec(DV),
                  pl.BlockSpec((TB, DV), lambda h, i: (i, h)), pl.BlockSpec((TB, DV), lambda h, i: (i, h)), rowspec],
        out_specs=[pl.BlockSpec((1, TB, DKP), lambda h, i: (h, i, 0)), kvspec(DKP), kvspec(DV)],
        out_shape=[jax.ShapeDtypeStruct((NH, s_len, DKP), F32), jax.ShapeDtypeStruct((NH, t_all, DKP), F32),
                   jax.ShapeDtypeStruct((NH, t_all, DV), F32)],
        compiler_params=_params(("arbitrary", "arbitrary")),
    )(q, k, v, dattn, attn, lse)


def _qkv_bwd(u, dq, dk, dv, cos, sin, q_lora_g, w_uq_t, kv_lora_g, w_ukv, qn_g, kn_g, s_len):
    t_all = u.shape[0]
    off = (t_all - s_len) // TB
    nb = t_all // TB

    def body(ulo_ref, dq_ref, dk_ref, dv_ref, cos_ref, sin_ref, qlg_ref, wuq_ref, kvlg_ref, wukv_ref, qng_ref, kng_ref,
             dlo_ref, dwuq_ref, dwukv_ref, dqlg_ref, dkvlg_ref, dqng_ref, dkng_ref):
        i = pl.program_id(0)

        @pl.when(i == 0)
        def _():
            for r in (dwuq_ref, dwukv_ref, dqlg_ref, dkvlg_ref, dqng_ref, dkng_ref):
                r[...] = jnp.zeros_like(r)

        latent = i >= off
        ulo = ulo_ref[...]
        cos = cos_ref[...]
        sin = sin_ref[...]
        cq = ulo[:, 0:QL]
        rc = lax.rsqrt(jnp.mean(cq * cq, axis=-1, keepdims=True) + EPS)
        cqh = cq * rc
        qlg = qlg_ref[...]
        cqn_b = (cqh * qlg).astype(BF16)
        qng = qng_ref[...]
        dqng = jnp.zeros((1, DKP), F32)
        dcqn = jnp.zeros((TB, QL), F32)
        for hd in range(NH):
            qh = _nt(cqn_b, wuq_ref[hd])
            rq = lax.rsqrt(_rowsum(qh * qh) / DK + EPS)
            xh = qh * rq
            dqh = jnp.where(latent, dq_ref[hd], 0.0)
            dyq = jnp.concatenate([dqh[:, :128], _rope_t(dqh[:, 128:], cos, sin)], axis=1)
            dqng += _colsum(dyq * xh)
            dxh = dyq * qng
            dqraw = (rq * (dxh - xh * (_rowsum(dxh * xh) / DK))).astype(BF16)
            dwuq_ref[hd] += _tn(dqraw, cqn_b)[:DK]
            dcqn += _nn(dqraw, wuq_ref[hd])
        dqng_ref[...] += dqng
        dqlg_ref[...] += _colsum(dcqn * cqh)
        dxh = dcqn * qlg
        dcq = rc * (dxh - cqh * jnp.mean(dxh * cqh, axis=-1, keepdims=True))

        ckv = ulo[:, QL:QL + KVL]
        r0 = lax.rsqrt(jnp.mean(ckv * ckv, axis=-1, keepdims=True) + EPS)
        ckvh = ckv * r0
        kvlg = kvlg_ref[...]
        ckvn_b = (ckvh * kvlg).astype(BF16)
        kr = ulo[:, 384:512]
        skr = _rowsum(kr * kr)
        kng = kng_ref[...]
        dkr = jnp.zeros((TB, 128), F32)
        dkng = jnp.zeros((1, DKP), F32)
        dckvn = jnp.zeros((TB, KVL), F32)
        for hd in range(NH):
            kn = _nn(ckvn_b, wukv_ref[hd])[:, :128]
            rk = lax.rsqrt((_rowsum(kn * kn) + skr) / DK + EPS)
            xh1 = kn * rk
            xh2 = kr * rk
            dkh = dk_ref[hd]
            d1 = dkh[:, :128]
            d2 = _rope_t(dkh[:, 128:], cos, sin)
            dkng += jnp.concatenate([_colsum(d1 * xh1), _colsum(d2 * xh2)], axis=1)
            dx1 = d1 * kng[:, :128]
            dx2 = d2 * kng[:, 128:]
            dot = (_rowsum(dx1 * xh1) + _rowsum(dx2 * xh2)) / DK
            dkv = jnp.concatenate([rk * (dx1 - xh1 * dot), dv_ref[hd]], axis=1).astype(BF16)
            dkr += rk * (dx2 - xh2 * dot)
            dwukv_ref[hd] += _tn(ckvn_b, dkv)
            dckvn += _nt(dkv, wukv_ref[hd])
        dkng_ref[...] += dkng
        dkvlg_ref[...] += _colsum(dckvn * ckvh)
        dxh = dckvn * kvlg
        dckv = r0 * (dxh - ckvh * jnp.mean(dxh * ckvh, axis=-1, keepdims=True))
        dlo_ref[...] = jnp.concatenate([dcq, dckv, dkr], axis=1).astype(BF16)

    row = lambda w: pl.BlockSpec((TB, w), lambda i: (i, 0))
    heads = lambda w: pl.BlockSpec((NH, TB, w), lambda i: (0, i, 0))
    return pl.pallas_call(
        body, name="qkv_bwd", grid=(nb,),
        in_specs=[row(512), pl.BlockSpec((NH, TB, DKP), lambda i: (0, jnp.maximum(i - off, 0), 0)), heads(DKP), heads(DV),
                  row(128), row(128), _full((1, QL)), _full((NH, DKP, QL)), _full((1, KVL)), _full((NH, KVL, 256)),
                  _full((1, DKP)), _full((1, DKP))],
        out_specs=[row(512), _full((NH, DK, QL)), _full((NH, KVL, 256)), _full((1, QL)), _full((1, KVL)),
                   _full((1, DKP)), _full((1, DKP))],
        out_shape=[jax.ShapeDtypeStruct((t_all, 512), BF16), jax.ShapeDtypeStruct((NH, DK, QL), F32),
                   jax.ShapeDtypeStruct((NH, KVL, 256), F32), jax.ShapeDtypeStruct((1, QL), F32),
                   jax.ShapeDtypeStruct((1, KVL), F32), jax.ShapeDtypeStruct((1, DKP), F32), jax.ShapeDtypeStruct((1, DKP), F32)],
        compiler_params=_params(("arbitrary",)),
    )(u, dq, dk, dv, cos, sin, q_lora_g, w_uq_t, kv_lora_g, w_ukv, qn_g, kn_g)


def _in_bwd(ctx, x, modsel, norm_g, dlo, dga, dgp, dpool, dxn, w_in_t):
    s_len, lc = x.shape[0], ctx.shape[0]
    t_all = s_len + lc
    off = lc // TB
    nb = t_all // TB
    nq = s_len // TB
    hb = TB // HALO
    n = TB + 2 * HALO

    def body(ctx_ref, x_ref, mod_ref, ng_ref, dlo_ref, dga_ref, dgp_ref, dp_ref, dpprev_ref, dpnext_ref, dxn_ref, win_ref,
             gx_ref, dwin_ref, dmod_ref, dng_ref):
        i = pl.program_id(0)
        j = i - off

        @pl.when(i == 0)
        def _():
            dwin_ref[...] = jnp.zeros_like(dwin_ref)
            dmod_ref[...] = jnp.zeros_like(dmod_ref)
            dng_ref[...] = jnp.zeros_like(dng_ref)

        latent = i >= off
        dp = dp_ref[...]
        prev = jnp.where(j <= 0, 0.0, dpprev_ref[...])
        nxt = jnp.where(j >= nq - 1, 0.0, dpnext_ref[...])
        win = jnp.concatenate([prev, dp, nxt], axis=0)
        tg = j * TB - HALO + lax.broadcasted_iota(jnp.int32, (n, 1), 0)
        dpin = []
        for g, w in enumerate(POOL_WINDOWS):
            cnt = jnp.maximum(jnp.minimum(tg + w // 2, s_len) - jnp.maximum(tg - w // 2, 0), 1).astype(F32)
            zq = win[:, g * 128:(g + 1) * 128] / cnt
            zq = zq + _shift_rows(zq, 1)
            for step in (1, 2, 4):
                if w >= 4 * step:
                    zq = _shift_rows(zq, -step) + _shift_rows(zq, step)
            dpin.append(zq[HALO:HALO + TB] - dp[:, g * 128:(g + 1) * 128])
        zero = jnp.zeros((TB, 512), BF16)
        du = [dlo_ref[...], jnp.where(latent, dga_ref[...], zero),
              jnp.where(latent, jnp.concatenate(dpin, axis=1).astype(BF16), zero), jnp.where(latent, dgp_ref[...], zero)]

        ng = ng_ref[...]
        xb = jnp.where(i < off, ctx_ref[...], x_ref[...])
        r, xh, xg, h, scale = _modulated(xb, mod_ref, ng)
        hb_ = h.astype(BF16)
        dh = jnp.zeros((TB, D), F32)
        for s, (lo, hi) in enumerate(SEG):
            dwin_ref[lo:hi, :] += _tn(du[s], hb_)
            dh += _nn(du[s], win_ref[lo:hi, :])
        is_lat = latent.astype(F32)
        dsh = _colsum(dh)
        dsc = _colsum(dh * xg)
        dmod_ref[0, 0:1, :] += dsh * (1.0 - is_lat)
        dmod_ref[0, 1:2, :] += dsc * (1.0 - is_lat)
        dmod_ref[1, 0:1, :] += dsh * is_lat
        dmod_ref[1, 1:2, :] += dsc * is_lat
        dxg = dh * (1.0 + scale)
        dng_ref[...] += _colsum(dxg * xh)
        dxh = dxg * ng
        gx_ref[...] = r * (dxh - xh * jnp.mean(dxh * xh, axis=-1, keepdims=True)) + dxn_ref[...]

    row = lambda w: pl.BlockSpec((TB, w), lambda i: (i, 0))
    lat = lambda w: pl.BlockSpec((TB, w), lambda i: (jnp.maximum(i - off, 0), 0))
    last8 = s_len // HALO - 1
    cspec, xspec, mspec = _token_specs(off)
    return pl.pallas_call(
        body, name="in_bwd", grid=(nb,),
        in_specs=[cspec, xspec, mspec, _full((1, D)), row(512), lat(512), lat(512), lat(512),
                  pl.BlockSpec((HALO, 512), lambda i: (jnp.maximum(jnp.maximum(i - off, 0) * hb - 1, 0), 0)),
                  pl.BlockSpec((HALO, 512), lambda i: (jnp.minimum((jnp.maximum(i - off, 0) + 1) * hb, last8), 0)),
                  lat(D), _full((DIN, D))],
        out_specs=[lat(D), _full((DIN, D)), _full((2, 2, D)), _full((1, D))],
        out_shape=[jax.ShapeDtypeStruct((s_len, D), F32), jax.ShapeDtypeStruct((DIN, D), F32),
                   jax.ShapeDtypeStruct((2, 2, D), F32), jax.ShapeDtypeStruct((1, D), F32)],
        compiler_params=_params(("arbitrary",)),
    )(ctx, x, modsel, norm_g, dlo, dga, dgp, dpool, dpool, dpool, dxn, w_in_t)


def _adamw_update(w_ref, g_ref, m_ref, v_ref, d_ref, mo_ref, vo_ref):
    gv = g_ref[...]
    mn = ADAM_B1 * m_ref[...] + (1.0 - ADAM_B1) * gv
    vn = ADAM_B2 * v_ref[...] + (1.0 - ADAM_B2) * (gv * gv)
    m_hat = mn / (1.0 - ADAM_B1 ** ADAM_STEP)
    v_hat = vn / (1.0 - ADAM_B2 ** ADAM_STEP)
    d_ref[...] = -ADAM_LR * (m_hat / (jnp.sqrt(v_hat) + ADAM_EPS) + ADAM_WD * w_ref[...])
    mo_ref[...] = mn
    vo_ref[...] = vn


def _adamw_small(ws, gs, ms, vs):
    n = len(ws)

    def body(*refs):
        for i in range(n):
            _adamw_update(refs[i], refs[n + i], refs[2 * n + i], refs[3 * n + i], refs[4 * n + i], refs[5 * n + i], refs[6 * n + i])

    shp = [jax.ShapeDtypeStruct(w.shape, F32) for w in ws]
    out = pl.pallas_call(body, name="adamw_small", in_specs=[VM] * (4 * n), out_specs=[VM] * (3 * n), out_shape=shp * 3,
                         compiler_params=pltpu.CompilerParams(vmem_limit_bytes=VMEM_LIMIT))(*ws, *gs, *ms, *vs)
    return out[:n], out[n:2 * n], out[2 * n:]


def _adamw(w, g, m, v, name):
    rows, cols = w.shape
    rb = next(r for r in range(min(rows, 256), 0, -8) if rows % r == 0)

    def body(w_ref, g_ref, m_ref, v_ref, d_ref, mo_ref, vo_ref):
        _adamw_update(w_ref, g_ref, m_ref, v_ref, d_ref, mo_ref, vo_ref)

    spec = pl.BlockSpec((rb, cols), lambda i: (i, 0))
    shp = jax.ShapeDtypeStruct((rows, cols), F32)
    return pl.pallas_call(
        body, name=name, grid=(rows // rb,), in_specs=[spec] * 4, out_specs=[spec] * 3, out_shape=[shp] * 3,
        compiler_params=_params(("arbitrary",)),
    )(w, g, m, v)


class _Links:
    def __init__(self, send_sems, recv_sems):
        self.send_sems, self.recv_sems, self.sends = send_sems, recv_sems, []

    def send(self, src, dst, sem, to):
        cp = pltpu.make_async_remote_copy(src, dst, self.send_sems.at[sem], self.recv_sems.at[sem], device_id=to,
                                          device_id_type=MESH)
        cp.start()
        self.sends.append(cp)

    def arrived(self, dst, sem, frm):
        pltpu.make_async_remote_copy(dst, dst, self.send_sems.at[sem], self.recv_sems.at[sem], device_id=frm,
                                     device_id_type=MESH).wait_recv()

    def drain(self):
        for cp in self.sends:
            cp.wait_send()


def _half(ref, c, axis):
    size = ref.shape[axis - 2] // 2
    win = pl.ds(pl.multiple_of(c * size, 16 if axis == 0 else 128), size)
    idx = (win, slice(None)) if axis == 0 else (slice(None), win)
    return ref.at[(slice(None),) * (len(ref.shape) - 2) + idx]


def _select_rows(slots_ref, n_slots, row=0):
    sub = lax.broadcasted_iota(jnp.int32, (8, 1), 0)
    out = None
    for d in range(n_slots):
        r = jnp.where(sub == d, jnp.broadcast_to(slots_ref[d][row:row + 1, :], (8, slots_ref.shape[-1])), 0.0)
        out = r if out is None else out + r
    return out


def _gather(c, c_ctx, w_mod, b_mod_k, shards, axes):
    nw = len(shards)
    kw = w_mod.shape[1]

    def body(*refs):
        c_ref, cc_ref, wm_ref, b_ref = refs[:4]
        w_refs = refs[4:4 + nw]
        a16_ref, mod_ref = refs[4 + nw:6 + nw]
        g_refs = refs[6 + nw:6 + 2 * nw]
        a_ref, send_sems, recv_sems, local_sems = refs[6 + 2 * nw:]
        x, y, cc = lax.axis_index("x"), lax.axis_index("y"), lax.axis_index("c")
        me = 4 * x + 2 * y + cc
        k = 2 * x + y
        sibling = (x, y, 1 - cc)
        links = _Links(send_sems, recv_sems)
        chips = [_peer(x, y, cc, off + (0,)) for off in CHIPS3]
        locals_ = []
        for wi in range(nw):
            lc = pltpu.make_async_copy(w_refs[wi], g_refs[wi].at[k], local_sems.at[wi])
            lc.start()
            locals_.append(lc)
            for j, to in enumerate(chips):
                links.send(_half(w_refs[wi], cc, axes[wi]), _half(g_refs[wi].at[k], cc, axes[wi]), 10 + wi * 6 + j, to)
        cv = c_ref[...]
        a_ref[me] = jnp.broadcast_to(cv * _sig(cv), (8, D))
        for j, off in enumerate(PEERS7):
            links.send(a_ref.at[me], a_ref.at[me], j, _peer(x, y, cc, off))
        for j, off in enumerate(PEERS7):
            px, py, pc = _peer(x, y, cc, off)
            links.arrived(a_ref.at[4 * px + 2 * py + pc], j, (px, py, pc))
        ccv = cc_ref[...]
        sub = lax.broadcasted_iota(jnp.int32, (8, 1), 0)
        a16 = jnp.concatenate([_select_rows(a_ref, 8), jnp.where(sub == 0, jnp.broadcast_to(ccv * _sig(ccv), (8, D)), 0.0)], axis=0)
        a16_ref[...] = a16
        mod_ref[k] = _dot3(_nn, a16, wm_ref[...]) + b_ref[...]
        for j, to in enumerate(chips):
            links.send(mod_ref.at[k], mod_ref.at[k], 7 + j, to)
        for wi in range(nw):
            for j, (px, py, pc) in enumerate(chips):
                blk = _half(g_refs[wi].at[2 * px + py], cc, axes[wi])
                links.arrived(blk, 10 + wi * 6 + j, (px, py, pc))
                links.send(blk, blk, 10 + wi * 6 + 3 + j, sibling)
        for j, (px, py, pc) in enumerate(chips):
            links.arrived(mod_ref.at[2 * px + py], 7 + j, (px, py, pc))
        for wi in range(nw):
            for j, (px, py, pc) in enumerate(chips):
                links.arrived(_half(g_refs[wi].at[2 * px + py], 1 - cc, axes[wi]), 10 + wi * 6 + 3 + j, sibling)
        links.drain()
        for lc in locals_:
            lc.wait()

    nsem = 10 + 6 * nw
    return pl.pallas_call(
        body, name="gather", in_specs=[VM] * 4 + [ANY] * nw, out_specs=[VM, VM] + [ANY] * nw,
        out_shape=[jax.ShapeDtypeStruct((16, D), F32), jax.ShapeDtypeStruct((4, 16, kw), F32)]
        + [jax.ShapeDtypeStruct((4,) + s.shape, s.dtype) for s in shards],
        scratch_shapes=[pltpu.VMEM((8, 8, D), F32), pltpu.SemaphoreType.DMA((nsem,)), pltpu.SemaphoreType.DMA((nsem,)),
                        pltpu.SemaphoreType.DMA((nw,))],
        compiler_params=pltpu.CompilerParams(vmem_limit_bytes=VMEM_LIMIT),
    )(c, c_ctx, w_mod, b_mod_k, *shards)


SMALL_ROW_WIDTHS = (D, QL, KVL, DKP, DKP, 512, 128)
SMALL_OUT_WIDTHS = (D, QL, KVL, DK, DK, 512, 1)


def _reduce(grads, axes, smalls, w_pool_g, dmod8, a16, w_mod, c_ctx):
    nw = len(grads)
    ns = len(smalls)
    kw = w_mod.shape[1]
    halves = []
    for g, ax in zip(grads, axes):
        halves.append((g.shape[1] // 2, g.shape[2]) if ax == 0 else (g.shape[1], g.shape[2] // 2))

    def body(*refs):
        g_refs = refs[:nw]
        small_refs = refs[nw:nw + ns]
        wp_ref, dm_ref, a16_ref, wm_ref, cc_ref = refs[nw + ns:nw + ns + 5]
        o = nw + ns + 5
        r_refs = refs[o:o + nw]
        small_outs = refs[o + nw:o + nw + ns]
        rwp_ref, gw_ref, gb_ref, gc_ref = refs[o + nw + ns:o + nw + ns + 4]
        o = o + nw + ns + 4
        own, sib, part, got = (refs[o + i * nw:o + (i + 1) * nw] for i in range(4))
        smbuf, wpbuf, dm_all, pc_all, send_sems, recv_sems, local_sems = refs[o + 4 * nw:]
        x, y, cc = lax.axis_index("x"), lax.axis_index("y"), lax.axis_index("c")
        me = 4 * x + 2 * y + cc
        k = 2 * x + y
        sibling = (x, y, 1 - cc)
        links = _Links(send_sems, recv_sems)
        chips = [_peer(x, y, cc, off + (0,)) for off in CHIPS3]
        peers = [_peer(x, y, cc, off) for off in PEERS7]
        big, sm0, wp0, dm0, pc0 = 0, 5 * nw, 5 * nw + 7, 5 * nw + 14, 5 * nw + 21

        locals_ = []
        for wi in range(nw):
            lc = pltpu.make_async_copy(_half(g_refs[wi], cc, axes[wi]), own[wi], local_sems.at[wi])
            lc.start()
            locals_.append(lc)
            links.send(_half(g_refs[wi], 1 - cc, axes[wi]), sib[wi], big + wi * 5, sibling)
        slot = smbuf.at[me]
        slot[...] = jnp.zeros((8, D), F32)
        for r, (ref, w) in enumerate(zip(small_refs, SMALL_ROW_WIDTHS)):
            slot[r:r + 1, 0:w] = jnp.broadcast_to(ref[...], (1, w))
        wpbuf[me] = wp_ref[...]
        dm_all[me] = dm_ref[...]
        for j, peer in enumerate(peers):
            links.send(dm_all.at[me], dm_all.at[me], dm0 + j, peer)
            links.send(smbuf.at[me], smbuf.at[me], sm0 + j, peer)
            links.send(wpbuf.at[me], wpbuf.at[me], wp0 + j, peer)
        for wi in range(nw):
            locals_[wi].wait()
            links.arrived(sib[wi], big + wi * 5, sibling)
            part[wi][...] = (own[wi][...] + sib[wi][...]).astype(BF16)
            got[wi][k] = part[wi][k]
            for j, (px, py, pc) in enumerate(chips):
                links.send(part[wi].at[2 * px + py], got[wi].at[k], big + wi * 5 + 1 + j, (px, py, pc))
        for j, (px, py, pc) in enumerate(peers):
            links.arrived(dm_all.at[4 * px + 2 * py + pc], dm0 + j, (px, py, pc))
        dmc = dm_all[0][1:2, :]
        dml = dm_all[0][0:1, :]
        for d in range(1, 8):
            dmc = dmc + dm_all[d][1:2, :]
            dml = dml + dm_all[d][0:1, :]
        gb_ref[...] = dml + dmc
        sub = lax.broadcasted_iota(jnp.int32, (8, 1), 0)
        b16 = jnp.concatenate([_select_rows(dm_all, 8), jnp.where(sub == 0, jnp.broadcast_to(dmc, (8, 3 * D)), 0.0)], axis=0)
        bk = jnp.zeros((16, kw), F32)
        for kk in range(4):
            bk = bk + jnp.where(k == kk, b16[:, kk * kw:(kk + 1) * kw], 0.0)
        gw_ref[...] = _dot3(_tn, a16_ref[...], bk)
        pc_all[k] = _dot3(_nt, jnp.broadcast_to(bk[8:9, :], (8, kw)), wm_ref[...])
        for j, to in enumerate(chips):
            links.send(pc_all.at[k], pc_all.at[k], pc0 + j, to)
        for j, (px, py, pc) in enumerate(peers):
            links.arrived(smbuf.at[4 * px + 2 * py + pc], sm0 + j, (px, py, pc))
            links.arrived(wpbuf.at[4 * px + 2 * py + pc], wp0 + j, (px, py, pc))
        tot = smbuf[0]
        wpt = wpbuf[0]
        for d in range(1, 8):
            tot = tot + smbuf[d]
            wpt = wpt + wpbuf[d]
        for r, (ref, w) in enumerate(zip(small_outs, SMALL_OUT_WIDTHS)):
            ref[...] = tot[r:r + 1, 0:w]
        rwp_ref[...] = wpt
        for wi in range(nw):
            for j, (px, py, pc) in enumerate(chips):
                links.arrived(got[wi].at[2 * px + py], big + wi * 5 + 1 + j, (px, py, pc))
            total = got[wi][0].astype(F32)
            for kk in range(1, 4):
                total = total + got[wi][kk].astype(F32)
            mine = _half(r_refs[wi], cc, axes[wi])
            mine[...] = total
            links.send(mine, mine, big + wi * 5 + 4, sibling)
        for j, (px, py, pc) in enumerate(chips):
            links.arrived(pc_all.at[2 * px + py], pc0 + j, (px, py, pc))
        ccv = cc_ref[...]
        sg = _sig(ccv)
        gc_ref[...] = (pc_all[0][0:1, :] + pc_all[1][0:1, :] + pc_all[2][0:1, :] + pc_all[3][0:1, :]) * (sg * (1.0 + ccv * (1.0 - sg)))
        for wi in range(nw):
            links.arrived(_half(r_refs[wi], 1 - cc, axes[wi]), big + wi * 5 + 4, sibling)
        links.drain()

    nsem = 5 * nw + 24
    quads = [(4,) + h for h in halves]
    return pl.pallas_call(
        body, name="reduce", in_specs=[ANY] * nw + [VM] * (ns + 5), out_specs=[VM] * (nw + ns + 4),
        out_shape=[jax.ShapeDtypeStruct(g.shape[1:], F32) for g in grads]
        + [jax.ShapeDtypeStruct((1, w), F32) for w in SMALL_OUT_WIDTHS]
        + [jax.ShapeDtypeStruct(w_pool_g.shape, F32), jax.ShapeDtypeStruct((D, kw), F32), jax.ShapeDtypeStruct((1, 3 * D), F32),
           jax.ShapeDtypeStruct((1, D), F32)],
        scratch_shapes=[pltpu.VMEM(q, F32) for q in quads] + [pltpu.VMEM(q, F32) for q in quads]
        + [pltpu.VMEM(q, BF16) for q in quads] + [pltpu.VMEM(q, BF16) for q in quads]
        + [pltpu.VMEM((8, 8, D), F32), pltpu.VMEM((8,) + w_pool_g.shape, F32), pltpu.VMEM((8, 8, 3 * D), F32),
           pltpu.VMEM((4, 8, D), F32)]
        + [pltpu.SemaphoreType.DMA((nsem,)), pltpu.SemaphoreType.DMA((nsem,)), pltpu.SemaphoreType.DMA((nw,))],
        compiler_params=pltpu.CompilerParams(vmem_limit_bytes=VMEM_LIMIT),
    )(*grads, *smalls, w_pool_g, dmod8, a16, w_mod, c_ctx)


def _rope_tables(s_len, lc):
    rows = s_len // GRID_W
    n_freq = 16
    inv = ROPE_BASE ** (-jnp.arange(n_freq, dtype=F32) / n_freq)
    ang_r = jnp.arange(rows, dtype=F32)[:, None] * inv
    ang_c = jnp.arange(GRID_W, dtype=F32)[:, None] * inv
    out = []
    for fn, pad in ((jnp.cos, 1.0), (jnp.sin, 0.0)):
        tr = jnp.broadcast_to(fn(ang_r)[:, None, :], (rows, GRID_W, n_freq))
        tc = jnp.broadcast_to(fn(ang_c)[None, :, :], (rows, GRID_W, n_freq))
        t = jnp.concatenate([tr, tr, tc, tc, jnp.full((rows, GRID_W, 64), pad, F32)], axis=-1).reshape(s_len, 128)
        out.append(jnp.concatenate([jnp.full((lc, 128), pad, F32), t], axis=0))
    return out


def kernel(x, c, ctx, c_ctx, w_mod, b_mod, norm_g, w_in, q_lora_g, w_uq, kv_lora_g, w_ukv, q_norm_g, k_norm_g, w_pool, pool_scale, w_out, loss_target, m_c_ctx, m_w_mod, m_b_mod, m_norm_g, m_w_in, m_q_lora_g, m_w_uq, m_kv_lora_g, m_w_ukv, m_q_norm_g, m_k_norm_g, m_w_pool, m_pool_scale, m_w_out, v_c_ctx, v_w_mod, v_b_mod, v_norm_g, v_w_in, v_q_lora_g, v_w_uq, v_kv_lora_g, v_w_ukv, v_q_norm_g, v_k_norm_g, v_w_pool, v_pool_scale, v_w_out):
    xi, yi, ci = lax.axis_index("x"), lax.axis_index("y"), lax.axis_index("c")
    me = 4 * xi + 2 * yi + ci
    k = 2 * xi + yi
    s_len = x.shape[1]
    lc = ctx.shape[1]
    kw = w_mod.shape[2]
    weights = dict(c_ctx=c_ctx, w_mod=w_mod, b_mod=b_mod, norm_g=norm_g, w_in=w_in, q_lora_g=q_lora_g, w_uq=w_uq,
                   kv_lora_g=kv_lora_g, w_ukv=w_ukv, q_norm_g=q_norm_g, k_norm_g=k_norm_g, w_pool=w_pool,
                   pool_scale=pool_scale, w_out=w_out)
    m_in = dict(c_ctx=m_c_ctx, w_mod=m_w_mod, b_mod=m_b_mod, norm_g=m_norm_g, w_in=m_w_in, q_lora_g=m_q_lora_g, w_uq=m_w_uq,
                kv_lora_g=m_kv_lora_g, w_ukv=m_w_ukv, q_norm_g=m_q_norm_g, k_norm_g=m_k_norm_g, w_pool=m_w_pool,
                pool_scale=m_pool_scale, w_out=m_w_out)
    v_in = dict(c_ctx=v_c_ctx, w_mod=v_w_mod, b_mod=v_b_mod, norm_g=v_norm_g, w_in=v_w_in, q_lora_g=v_q_lora_g, w_uq=v_w_uq,
                kv_lora_g=v_kv_lora_g, w_ukv=v_w_ukv, q_norm_g=v_q_norm_g, k_norm_g=v_k_norm_g, w_pool=v_w_pool,
                pool_scale=v_pool_scale, w_out=v_w_out)
    order = ["c_ctx", "w_mod", "b_mod", "norm_g", "w_in", "q_lora_g", "w_uq", "kv_lora_g", "w_ukv", "q_norm_g", "k_norm_g",
             "w_pool", "pool_scale", "w_out"]
    transposed = ("w_in", "w_uq")
    as2d = lambda n, a: jnp.transpose(a[0]) if n in transposed else a.reshape(-1, a.shape[-1])
    back = lambda n, a: jnp.transpose(a)[None] if n in transposed else a.reshape(weights[n].shape)

    c_ctx2 = c_ctx.reshape(1, D)
    b_mod_k = lax.dynamic_slice(b_mod, (0, k * kw), (1, kw))
    split = (1, 0, 0, 0)
    a16, mod_all, g_in, g_uq, g_ukv, g_out = _gather(
        c, c_ctx2, w_mod[0], b_mod_k,
        [as2d("w_in", w_in).astype(BF16), as2d("w_uq", w_uq).astype(BF16), w_ukv[0].astype(BF16), w_out[0].astype(BF16)], split)
    mod_me = lax.dynamic_index_in_dim(mod_all, me, axis=1, keepdims=False).reshape(3, D)
    mod_c = mod_all[:, 8, :].reshape(3, D)
    modsel = jnp.stack([mod_c, mod_me])
    w_in_t = g_in.reshape(DIN, D)
    w_uq_t = jnp.pad(g_uq, ((0, 0), (0, DKP - DK), (0, 0)))
    w_out_f = g_out.reshape(D, D)
    qn_g = jnp.pad(q_norm_g, ((0, 0), (0, DKP - DK)))
    kn_g = jnp.pad(k_norm_g, ((0, 0), (0, DKP - DK)))
    w_pool_b = w_pool[0].astype(BF16)
    cos, sin = _rope_tables(s_len, lc)

    u, q, kk, v = _fwd_in(ctx[0], x[0], modsel, norm_g, w_in_t, q_lora_g, w_uq_t, kv_lora_g, g_ukv, qn_g, kn_g, cos, sin)
    attn, lse = _attn_fwd(q, kk, v, s_len)
    nqb = s_len // Q_BLOCK
    attn_t = jnp.transpose(attn.reshape(nqb, Q_BLOCK, NH * DV), (1, 0, 2)).reshape(s_len, NH * DV)
    (dxn, dattn_t, dga, dgp, dpool, dw_out, dgate, dps, dw_pool, loss) = _out_stage(
        attn_t, u, x[0], loss_target[0], modsel[1, 2:3, :], w_pool_b, pool_scale, w_out_f, lc)
    dattn = jnp.transpose(dattn_t.reshape(Q_BLOCK, nqb, NH * DV), (1, 0, 2)).reshape(s_len, NH * DV)
    dq, dk, dv = _attn_bwd(q, kk, v, dattn, attn, lse, s_len)
    dlo, dw_uq_t, dw_ukv, dqlg, dkvlg, dqng, dkng = _qkv_bwd(u, dq, dk, dv, cos, sin, q_lora_g, w_uq_t, kv_lora_g, g_ukv,
                                                            qn_g, kn_g, s_len)
    gx, dw_in_t, dmod, dng = _in_bwd(ctx[0], x[0], modsel, norm_g, dlo, dga, dgp, dpool, dxn, w_in_t)

    dmod_l = jnp.concatenate([dmod[1, 0], dmod[1, 1], dgate[0]]).reshape(1, 3 * D)
    dmod_c = jnp.concatenate([dmod[0, 0], dmod[0, 1], jnp.zeros((D,), F32)]).reshape(1, 3 * D)
    dmod8 = jnp.concatenate([dmod_l, dmod_c, jnp.zeros((6, 3 * D), F32)], axis=0)
    (r_in, r_uq, r_ukv, r_out, g_ng, g_qlg, g_kvlg, g_qng, g_kng, g_ps, loss_all, g_wp, g_w_mod, g_b_mod, g_c_ctx) = _reduce(
        [dw_in_t.reshape(4, DIN // 4, D), dw_uq_t, dw_ukv, dw_out.reshape(4, D // 4, D)], split,
        [dng, dqlg, dkvlg, dqng, dkng, dps, loss], dw_pool, dmod8, a16, w_mod[0], c_ctx2)
    g2d = dict(c_ctx=g_c_ctx, b_mod=g_b_mod, w_mod=g_w_mod, w_in=r_in, w_uq=r_uq, w_ukv=r_ukv, w_out=r_out, norm_g=g_ng,
               q_lora_g=g_qlg, kv_lora_g=g_kvlg, q_norm_g=g_qng, k_norm_g=g_kng, pool_scale=g_ps, w_pool=g_wp.reshape(512, 128))

    d2d, m2d, v2d = {}, {}, {}
    for n in ("w_mod", "w_in", "w_uq", "w_ukv", "w_out"):
        d2d[n], m2d[n], v2d[n] = _adamw(as2d(n, weights[n]), g2d[n], as2d(n, m_in[n]), as2d(n, v_in[n]), "adamw_" + n)
    small = [n for n in order if n not in d2d]
    outs = _adamw_small([as2d(n, weights[n]) for n in small], [g2d[n] for n in small], [as2d(n, m_in[n]) for n in small],
                        [as2d(n, v_in[n]) for n in small])
    for dst, arrs in zip((d2d, m2d, v2d), outs):
        dst.update(dict(zip(small, arrs)))

    return (loss_all[0, 0], gx[None], *[back(n, g2d[n]) for n in order], *[back(n, d2d[n]) for n in order],
            *[back(n, m2d[n]) for n in order], *[back(n, v2d[n]) for n in order])
```

```python
import jax
import jax.numpy as jnp
from jax import lax
from jax.experimental import pallas as pl
from jax.experimental.pallas import tpu as pltpu

F32 = jnp.float32
BF16 = jnp.bfloat16
MESH = pl.DeviceIdType.MESH

D = 1024
NH = 4
DK = 192
DKP = 256
DV = 128
QL = 256
KVL = 128
DIN = 1984
U_LO = 448
SEG = ((0, 512), (448, 960), (960, 1472), (1472, 1984))
DU = 2048
POOL_WINDOWS = (2, 4, 8, 16)
HALO = 8
EPS = 1e-6
ROPE_BASE = 10000.0
GRID_W = 64
Q_BLOCK = 128
TB = 256
SCALE = DK ** -0.5
VMEM_LIMIT = 56 * 1024 * 1024

ADAM_LR = 0.001
ADAM_B1 = 0.9
ADAM_B2 = 0.999
ADAM_EPS = 1e-08
ADAM_WD = 0.01
ADAM_STEP = 10

CHIPS3 = ((1, 0), (0, 1), (1, 1))
PEERS7 = tuple((dx, dy, dc) for dx in (0, 1) for dy in (0, 1) for dc in (0, 1) if (dx, dy, dc) != (0, 0, 0))

VM = pl.BlockSpec(memory_space=pltpu.VMEM)
ANY = pl.BlockSpec(memory_space=pl.ANY)


def _nn(a, b):
    return jnp.dot(a, b, preferred_element_type=F32)


def _nt(a, b):
    return lax.dot_general(a, b, (((1,), (1,)), ((), ())), preferred_element_type=F32)


def _tn(a, b):
    return lax.dot_general(a, b, (((0,), (0,)), ((), ())), preferred_element_type=F32)


def _split3(a):
    a0 = a.astype(BF16)
    r = a - a0.astype(F32)
    a1 = r.astype(BF16)
    a2 = (r - a1.astype(F32)).astype(BF16)
    return a0, a1, a2


def _dot3(dot, a, b):
    sa = _split3(a)
    sb = _split3(b)
    out = None
    for i in range(3):
        for j in range(3 - i):
            t = dot(sa[i], sb[j])
            out = t if out is None else out + t
    return out


def _sig(x):
    return 1.0 / (1.0 + jnp.exp(-x))


def _rot(t):
    lane = lax.broadcasted_iota(jnp.int32, t.shape, 1)
    return jnp.where((lane % 32) < 16, -pltpu.roll(t, 112, 1), pltpu.roll(t, 16, 1))


def _rope(t, cos, sin):
    return t * cos + _rot(t) * sin


def _rope_t(t, cos, sin):
    return t * cos - _rot(t * sin)


def _rope_block(rows_ref, cols_ref, is_ctx):
    lane = lax.broadcasted_iota(jnp.int32, (TB, 256), 1) % 128
    rows = jnp.concatenate([jnp.broadcast_to(rows_ref[0, r:r + 1, :], (GRID_W, 256)) for r in range(TB // GRID_W)], axis=0)
    cs = jnp.where(lane < 32, rows, cols_ref[...])
    return jnp.where(is_ctx, 1.0, cs[:, :128]), jnp.where(is_ctx, 0.0, cs[:, 128:])


def _shift_rows(z, k):
    n = z.shape[0]
    return pltpu.roll(z, (n - k) % n, 0)


def _colsum(a):
    return jnp.sum(a, axis=0, keepdims=True)


def _rowsum(a):
    return jnp.sum(a, axis=-1, keepdims=True)


def _row_layout(col):
    return jnp.transpose(jnp.broadcast_to(col, (col.shape[0], 128)))[0:8, :]


def _params(sem=None):
    return pltpu.CompilerParams(dimension_semantics=sem, vmem_limit_bytes=VMEM_LIMIT)


def _full(shape):
    nd = len(shape)
    return pl.BlockSpec(shape, lambda *_: (0,) * nd)


def _peer(x, y, c, off):
    dx, dy, dc = off
    return ((x + dx) % 2, (y + dy) % 2, (c + dc) % 2)


def _token_specs(off):
    ctx = pl.BlockSpec((TB, D), lambda i: (jnp.minimum(i, off - 1), 0))
    lat = pl.BlockSpec((TB, D), lambda i: (jnp.maximum(i - off, 0), 0))
    mod = pl.BlockSpec((1, 3, D), lambda i: (jnp.minimum(i // off, 1), 0, 0))
    return ctx, lat, mod


def _modulated(x, mod_ref, ng):
    shift = mod_ref[0, 0:1, :]
    scale = mod_ref[0, 1:2, :]
    r = lax.rsqrt(jnp.mean(x * x, axis=-1, keepdims=True) + EPS)
    xh = x * r
    xg = xh * ng
    return r, xh, xg, xg * (1.0 + scale) + shift, scale


def _fwd_in(ctx, x, modsel, norm_g, w_in_t, q_lora_g, w_uq_t, kv_lora_g, w_ukv, qn_g, kn_g, cos, sin):
    s_len, lc = x.shape[0], ctx.shape[0]
    t_all = s_len + lc
    nb = t_all // TB
    off = lc // TB

    def body(ctx_ref, x_ref, mod_ref, ng_ref, win_ref, qlg_ref, wuq_ref, kvlg_ref, wukv_ref, qng_ref, kng_ref, cos_ref, sin_ref,
             u_ref, q_ref, k_ref, v_ref):
        i = pl.program_id(0)
        xb = jnp.where(i < off, ctx_ref[...], x_ref[...])
        _, _, _, h, _ = _modulated(xb, mod_ref, ng_ref[...])
        hb = h.astype(BF16)
        lane = lax.broadcasted_iota(jnp.int32, (TB, 512), 1)
        ulo = jnp.where(lane < U_LO, _nt(hb, win_ref[SEG[0][0]:SEG[0][1], :]), 0.0)
        u_ref[:, 0:512] = ulo
        for j in range(1, 4):
            u_ref[:, j * 512:(j + 1) * 512] = _nt(hb, win_ref[SEG[j][0]:SEG[j][1], :])
        cos, sin = _rope_block(cos_ref, sin_ref, pl.program_id(0) < off)
        cq = ulo[:, 0:QL]
        cqn = (cq * lax.rsqrt(jnp.mean(cq * cq, axis=-1, keepdims=True) + EPS) * qlg_ref[...]).astype(BF16)
        qng = qng_ref[...]
        for hd in range(NH):
            qh = _nt(cqn, wuq_ref[hd])
            qn = qh * lax.rsqrt(_rowsum(qh * qh) / DK + EPS) * qng
            q_ref[hd] = (jnp.concatenate([qn[:, :128], _rope(qn[:, 128:], cos, sin)], axis=1) * SCALE).astype(BF16)
        ckv = ulo[:, QL:QL + KVL]
        ckvn = (ckv * lax.rsqrt(jnp.mean(ckv * ckv, axis=-1, keepdims=True) + EPS) * kvlg_ref[...]).astype(BF16)
        kr = ulo[:, 384:512]
        skr = _rowsum(kr * kr)
        kng = kng_ref[...]
        for hd in range(NH):
            kv = _nn(ckvn, wukv_ref[hd])
            kn = kv[:, :128]
            rk = lax.rsqrt((_rowsum(kn * kn) + skr) / DK + EPS)
            k_ref[hd] = jnp.concatenate([kn * rk * kng[:, :128], _rope(kr * rk * kng[:, 128:], cos, sin)], axis=1).astype(BF16)
            v_ref[hd] = kv[:, 128:].astype(BF16)

    row = lambda w: pl.BlockSpec((TB, w), lambda i: (i, 0))
    heads = lambda w: pl.BlockSpec((NH, TB, w), lambda i: (0, i, 0))
    cspec, xspec, mspec = _token_specs(off)
    return pl.pallas_call(
        body, name="fwd_in", grid=(nb,),
        in_specs=[cspec, xspec, mspec, _full((1, D)), _full((DIN, D)), _full((1, QL)), _full((NH, DKP, QL)), _full((1, KVL)),
                  _full((NH, KVL, 256)), _full((1, DKP)), _full((1, DKP)), pl.BlockSpec((1, 8, 256), lambda i: (jnp.maximum(i - off, 0), 0, 0)), _full((TB, 256))],
        out_specs=[row(DU), heads(DKP), heads(DKP), heads(DV)],
        out_shape=[jax.ShapeDtypeStruct((t_all, DU), F32), jax.ShapeDtypeStruct((NH, t_all, DKP), BF16),
                   jax.ShapeDtypeStruct((NH, t_all, DKP), BF16), jax.ShapeDtypeStruct((NH, t_all, DV), BF16)],
        compiler_params=_params(("arbitrary",)),
    )(ctx, x, modsel, norm_g, w_in_t, q_lora_g, w_uq_t, kv_lora_g, w_ukv, qn_g, kn_g, cos, sin)


def _attn_fwd(q, k, v, s_len):
    t_all = q.shape[1]
    off = (t_all - s_len) // TB
    nq = s_len // TB

    def body(q_ref, k_ref, v_ref, o_ref, lse_ref):
        s = _nt(q_ref[0], k_ref[0])
        m = jnp.max(s, axis=-1, keepdims=True)
        e = jnp.exp(s - m)
        l = _rowsum(e)
        o_ref[...] = _nn(e.astype(BF16), v_ref[0]) / l
        lse_ref[0, 0] = _row_layout(m + jnp.log(l))

    return pl.pallas_call(
        body, name="attn_fwd", grid=(NH, nq),
        in_specs=[pl.BlockSpec((1, TB, DKP), lambda h, i: (h, i + off, 0)), pl.BlockSpec((1, t_all, DKP), lambda h, i: (h, 0, 0)),
                  pl.BlockSpec((1, t_all, DV), lambda h, i: (h, 0, 0))],
        out_specs=[pl.BlockSpec((TB, DV), lambda h, i: (i, h)), pl.BlockSpec((1, 1, 8, TB), lambda h, i: (h, i, 0, 0))],
        out_shape=[jax.ShapeDtypeStruct((s_len, NH * DV), F32), jax.ShapeDtypeStruct((NH, nq, 8, TB), F32)],
        compiler_params=_params(("arbitrary", "arbitrary")),
    )(q, k, v)


def _out_stage(attn, u, x, target, gate, w_pool, pool_scale, w_out, lc):
    s_len = x.shape[0]
    t_all = s_len + lc
    off = lc // TB
    nq = s_len // TB
    hb = TB // HALO
    nqb = s_len // Q_BLOCK
    jb = TB // nqb

    def body(attn_ref, ga_ref, pin_ref, pprev_ref, pnext_ref, gp_ref, x_ref, tgt_ref, gate_ref, wp_ref, ps_ref, wo_ref,
             dxn_ref, dattn_ref, dga_ref, dgp_ref, dpool_ref, dwo_ref, dgate_ref, dps_ref, dwp_ref, loss_ref):
        i = pl.program_id(0)

        @pl.when(i == 0)
        def _():
            dwo_ref[...] = jnp.zeros_like(dwo_ref)
            dgate_ref[...] = jnp.zeros_like(dgate_ref)
            dps_ref[...] = jnp.zeros_like(dps_ref)
            dwp_ref[...] = jnp.zeros_like(dwp_ref)
            loss_ref[...] = jnp.zeros_like(loss_ref)

        attn = jnp.concatenate([attn_ref[:, jj, :] for jj in range(jb)], axis=0)
        ga = ga_ref[...]
        gp = gp_ref[...]
        pin = pin_ref[...]
        prev = jnp.where(i == 0, 0.0, pprev_ref[...])
        nxt = jnp.where(i == nq - 1, 0.0, pnext_ref[...])
        win = jnp.concatenate([prev, pin, nxt], axis=0)
        tg = i * TB + lax.broadcasted_iota(jnp.int32, (TB, 1), 0)
        pooled = []
        for g, w in enumerate(POOL_WINDOWS):
            a = win[:, g * 128:(g + 1) * 128]
            p = _shift_rows(a, -1) + a
            for step in (1, 2, 4):
                if w >= 4 * step:
                    p = _shift_rows(p, -step) + _shift_rows(p, step)
            cnt = (jnp.minimum(tg + w // 2, s_len) - jnp.maximum(tg - w // 2, 0)).astype(F32)
            pooled.append(p[HALO:HALO + TB] / cnt - a[HALO:HALO + TB])
        pooled_b = [p.astype(BF16) for p in pooled]
        z = jnp.concatenate([_nn(pooled_b[g], wp_ref[g]) for g in range(4)], axis=1)
        ps = ps_ref[...]
        yp = z * ps
        sga = _sig(ga)
        sila = ga * sga
        sgp = _sig(gp)
        silp = gp * sgp
        br = jnp.concatenate([sila * attn, silp * yp], axis=1).astype(BF16)
        y = _nn(br, wo_ref[...])
        gate = gate_ref[...]
        err = x_ref[...] + gate * y - tgt_ref[...]
        loss_ref[...] += _colsum(_rowsum(err * err)) * (0.5 / D)
        dxn = err * (1.0 / D)
        dxn_ref[...] = dxn
        dgate_ref[...] += _colsum(dxn * y)
        dy = (dxn * gate).astype(BF16)
        dwo_ref[...] += _tn(br, dy)
        dbr = _nt(dy, wo_ref[...])
        dbra = dbr[:, :512]
        dbrp = dbr[:, 512:]
        dattn = dbra * sila
        for jj in range(jb):
            dattn_ref[:, jj, :] = dattn[jj * nqb:(jj + 1) * nqb]
        dga_ref[...] = (dbra * attn * (sga * (1.0 + ga * (1.0 - sga)))).astype(BF16)
        dgp_ref[...] = (dbrp * yp * (sgp * (1.0 + gp * (1.0 - sgp)))).astype(BF16)
        dyp = dbrp * silp
        dps_ref[...] += _colsum(dyp * z)
        dz = (dyp * ps).astype(BF16)
        dpool = []
        for g in range(4):
            dzg = dz[:, g * 128:(g + 1) * 128]
            dwp_ref[g] += _tn(pooled_b[g], dzg)
            dpool.append(_nt(dzg, wp_ref[g]))
        dpool_ref[...] = jnp.concatenate(dpool, axis=1)

    lat = lambda w: pl.BlockSpec((TB, w), lambda i: (i, 0))
    perm = pl.BlockSpec((nqb, jb, 512), lambda i: (0, i, 0))
    ucol = lambda j: pl.BlockSpec((TB, 512), lambda i: (i + off, j))
    last8 = t_all // HALO - 1
    return pl.pallas_call(
        body, name="out_stage", grid=(nq,),
        in_specs=[perm, ucol(1), ucol(2),
                  pl.BlockSpec((HALO, 512), lambda i: ((i + off) * hb - 1, 2)),
                  pl.BlockSpec((HALO, 512), lambda i: (jnp.minimum((i + off + 1) * hb, last8), 2)),
                  ucol(3), lat(D), lat(D), _full((1, D)), _full((4, 128, 128)), _full((1, 512)), _full((D, D))],
        out_specs=[lat(D), perm, lat(512), lat(512), lat(512),
                   _full((D, D)), _full((1, D)), _full((1, 512)), _full((4, 128, 128)), _full((1, 1))],
        out_shape=[jax.ShapeDtypeStruct((s_len, D), F32), jax.ShapeDtypeStruct((nqb, Q_BLOCK, 512), F32),
                   jax.ShapeDtypeStruct((s_len, 512), BF16), jax.ShapeDtypeStruct((s_len, 512), BF16),
                   jax.ShapeDtypeStruct((s_len, 512), F32),
                   jax.ShapeDtypeStruct((D, D), F32), jax.ShapeDtypeStruct((1, D), F32), jax.ShapeDtypeStruct((1, 512), F32),
                   jax.ShapeDtypeStruct((4, 128, 128), F32), jax.ShapeDtypeStruct((1, 1), F32)],
        compiler_params=_params(("arbitrary",)),
    )(attn, u, u, u, u, u, x, target, gate, w_pool, pool_scale, w_out)


def _attn_bwd(q, k, v, dattn, attn, lse, s_len):
    t_all = q.shape[1]
    off = (t_all - s_len) // TB
    nq = s_len // TB
    nch = 4
    ch = t_all // nch

    def body(q_ref, k_ref, v_ref, do_ref, o_ref, lse_ref, dq_ref, dk_ref, dv_ref):
        i = pl.program_id(1)

        @pl.when(i == 0)
        def _():
            dk_ref[...] = jnp.zeros_like(dk_ref)
            dv_ref[...] = jnp.zeros_like(dv_ref)

        qb = q_ref[0]
        delta_r = _row_layout(_rowsum(do_ref[...] * o_ref[...]))[0:1, :]
        do = do_ref[...].astype(BF16)
        lse_r = lse_ref[0, 0][0:1, :]
        dq = jnp.zeros((TB, DKP), F32)
        for c in range(nch):
            rows = pl.ds(c * ch, ch)
            kc = k_ref[0, rows, :]
            p_t = jnp.exp(_nt(kc, qb) - lse_r)
            ds_t = (p_t * (_nt(v_ref[0, rows, :], do) - delta_r)).astype(BF16)
            dv_ref[0, rows, :] += _nn(p_t.astype(BF16), do)
            dk_ref[0, rows, :] += _nn(ds_t, qb)
            dq += _tn(ds_t, kc)
        dq_ref[0] = dq * SCALE

    kvspec = lambda w: pl.BlockSpec((1, t_all, w), lambda h, i: (h, 0, 0))
    rowspec = pl.BlockSpec((1, 1, 8, TB), lambda h, i: (h, i, 0, 0))
    return pl.pallas_call(
        body, name="attn_bwd", grid=(NH, nq),
        in_specs=[pl.BlockSpec((1, TB, DKP), lambda h, i: (h, i + off, 0)), kvspec(DKP), kvspec(DV),
                  pl.BlockSpec((TB, DV), lambda h, i: (i, h)), pl.BlockSpec((TB, DV), lambda h, i: (i, h)), rowspec],
        out_specs=[pl.BlockSpec((1, TB, DKP), lambda h, i: (h, i, 0)), kvspec(DKP), kvspec(DV)],
        out_shape=[jax.ShapeDtypeStruct((NH, s_len, DKP), F32), jax.ShapeDtypeStruct((NH, t_all, DKP), F32),
                   jax.ShapeDtypeStruct((NH, t_all, DV), F32)],
        compiler_params=_params(("arbitrary", "arbitrary")),
    )(q, k, v, dattn, attn, lse)


def _qkv_bwd(u, dq, dk, dv, cos, sin, q_lora_g, w_uq_t, kv_lora_g, w_ukv, qn_g, kn_g, s_len):
    t_all = u.shape[0]
    off = (t_all - s_len) // TB
    nb = t_all // TB

    def body(ulo_ref, dq_ref, dk_ref, dv_ref, cos_ref, sin_ref, qlg_ref, wuq_ref, kvlg_ref, wukv_ref, qng_ref, kng_ref,
             dlo_ref, dwuq_ref, dwukv_ref, dqlg_ref, dkvlg_ref, dqng_ref, dkng_ref):
        i = pl.program_id(0)

        @pl.when(i == 0)
        def _():
            for r in (dwuq_ref, dwukv_ref, dqlg_ref, dkvlg_ref, dqng_ref, dkng_ref):
                r[...] = jnp.zeros_like(r)

        latent = i >= off
        ulo = ulo_ref[...]
        cos, sin = _rope_block(cos_ref, sin_ref, pl.program_id(0) < off)
        cq = ulo[:, 0:QL]
        rc = lax.rsqrt(jnp.mean(cq * cq, axis=-1, keepdims=True) + EPS)
        cqh = cq * rc
        qlg = qlg_ref[...]
        cqn_b = (cqh * qlg).astype(BF16)
        qng = qng_ref[...]
        dqng = jnp.zeros((1, DKP), F32)
        dcqn = jnp.zeros((TB, QL), F32)
        for hd in range(NH):
            qh = _nt(cqn_b, wuq_ref[hd])
            rq = lax.rsqrt(_rowsum(qh * qh) / DK + EPS)
            xh = qh * rq
            dqh = jnp.where(latent, dq_ref[hd], 0.0)
            dyq = jnp.concatenate([dqh[:, :128], _rope_t(dqh[:, 128:], cos, sin)], axis=1)
            dqng += _colsum(dyq * xh)
            dxh = dyq * qng
            dqraw = (rq * (dxh - xh * (_rowsum(dxh * xh) / DK))).astype(BF16)
            dwuq_ref[hd] += _tn(dqraw, cqn_b)[:DK]
            dcqn += _nn(dqraw, wuq_ref[hd])
        dqng_ref[...] += dqng
        dqlg_ref[...] += _colsum(dcqn * cqh)
        dxh = dcqn * qlg
        dcq = rc * (dxh - cqh * jnp.mean(dxh * cqh, axis=-1, keepdims=True))

        ckv = ulo[:, QL:QL + KVL]
        r0 = lax.rsqrt(jnp.mean(ckv * ckv, axis=-1, keepdims=True) + EPS)
        ckvh = ckv * r0
        kvlg = kvlg_ref[...]
        ckvn_b = (ckvh * kvlg).astype(BF16)
        kr = ulo[:, 384:512]
        skr = _rowsum(kr * kr)
        kng = kng_ref[...]
        dkr = jnp.zeros((TB, 128), F32)
        dkng = jnp.zeros((1, DKP), F32)
        dckvn = jnp.zeros((TB, KVL), F32)
        for hd in range(NH):
            kn = _nn(ckvn_b, wukv_ref[hd])[:, :128]
            rk = lax.rsqrt((_rowsum(kn * kn) + skr) / DK + EPS)
            xh1 = kn * rk
            xh2 = kr * rk
            dkh = dk_ref[hd]
            d1 = dkh[:, :128]
            d2 = _rope_t(dkh[:, 128:], cos, sin)
            dkng += jnp.concatenate([_colsum(d1 * xh1), _colsum(d2 * xh2)], axis=1)
            dx1 = d1 * kng[:, :128]
            dx2 = d2 * kng[:, 128:]
            dot = (_rowsum(dx1 * xh1) + _rowsum(dx2 * xh2)) / DK
            dkv = jnp.concatenate([rk * (dx1 - xh1 * dot), dv_ref[hd]], axis=1).astype(BF16)
            dkr += rk * (dx2 - xh2 * dot)
            dwukv_ref[hd] += _tn(ckvn_b, dkv)
            dckvn += _nt(dkv, wukv_ref[hd])
        dkng_ref[...] += dkng
        dkvlg_ref[...] += _colsum(dckvn * ckvh)
        dxh = dckvn * kvlg
        dckv = r0 * (dxh - ckvh * jnp.mean(dxh * ckvh, axis=-1, keepdims=True))
        dlo_ref[...] = jnp.concatenate([dcq, dckv, dkr], axis=1).astype(BF16)

    row = lambda w: pl.BlockSpec((TB, w), lambda i: (i, 0))
    heads = lambda w: pl.BlockSpec((NH, TB, w), lambda i: (0, i, 0))
    return pl.pallas_call(
        body, name="qkv_bwd", grid=(nb,),
        in_specs=[row(512), pl.BlockSpec((NH, TB, DKP), lambda i: (0, jnp.maximum(i - off, 0), 0)), heads(DKP), heads(DV),
                  pl.BlockSpec((1, 8, 256), lambda i: (jnp.maximum(i - off, 0), 0, 0)), _full((TB, 256)), _full((1, QL)), _full((NH, DKP, QL)), _full((1, KVL)), _full((NH, KVL, 256)),
                  _full((1, DKP)), _full((1, DKP))],
        out_specs=[row(512), _full((NH, DK, QL)), _full((NH, KVL, 256)), _full((1, QL)), _full((1, KVL)),
                   _full((1, DKP)), _full((1, DKP))],
        out_shape=[jax.ShapeDtypeStruct((t_all, 512), BF16), jax.ShapeDtypeStruct((NH, DK, QL), F32),
                   jax.ShapeDtypeStruct((NH, KVL, 256), F32), jax.ShapeDtypeStruct((1, QL), F32),
                   jax.ShapeDtypeStruct((1, KVL), F32), jax.ShapeDtypeStruct((1, DKP), F32), jax.ShapeDtypeStruct((1, DKP), F32)],
        compiler_params=_params(("arbitrary",)),
    )(u, dq, dk, dv, cos, sin, q_lora_g, w_uq_t, kv_lora_g, w_ukv, qn_g, kn_g)


def _in_bwd(ctx, x, modsel, norm_g, dlo, dga, dgp, dpool, dxn, w_in_t):
    s_len, lc = x.shape[0], ctx.shape[0]
    t_all = s_len + lc
    off = lc // TB
    nb = t_all // TB
    nq = s_len // TB
    hb = TB // HALO
    n = TB + 2 * HALO

    def body(ctx_ref, x_ref, mod_ref, ng_ref, dlo_ref, dga_ref, dgp_ref, dp_ref, dpprev_ref, dpnext_ref, dxn_ref, win_ref,
             gx_ref, dwin_ref, dmod_ref, dng_ref):
        i = pl.program_id(0)
        j = i - off

        @pl.when(i == 0)
        def _():
            dwin_ref[...] = jnp.zeros_like(dwin_ref)
            dmod_ref[...] = jnp.zeros_like(dmod_ref)
            dng_ref[...] = jnp.zeros_like(dng_ref)

        latent = i >= off
        dp = dp_ref[...]
        prev = jnp.where(j <= 0, 0.0, dpprev_ref[...])
        nxt = jnp.where(j >= nq - 1, 0.0, dpnext_ref[...])
        win = jnp.concatenate([prev, dp, nxt], axis=0)
        tg = j * TB - HALO + lax.broadcasted_iota(jnp.int32, (n, 1), 0)
        dpin = []
        for g, w in enumerate(POOL_WINDOWS):
            cnt = jnp.maximum(jnp.minimum(tg + w // 2, s_len) - jnp.maximum(tg - w // 2, 0), 1).astype(F32)
            zq = win[:, g * 128:(g + 1) * 128] / cnt
            zq = zq + _shift_rows(zq, 1)
            for step in (1, 2, 4):
                if w >= 4 * step:
                    zq = _shift_rows(zq, -step) + _shift_rows(zq, step)
            dpin.append(zq[HALO:HALO + TB] - dp[:, g * 128:(g + 1) * 128])
        zero = jnp.zeros((TB, 512), BF16)
        du = [dlo_ref[...], jnp.where(latent, dga_ref[...], zero),
              jnp.where(latent, jnp.concatenate(dpin, axis=1).astype(BF16), zero), jnp.where(latent, dgp_ref[...], zero)]

        ng = ng_ref[...]
        xb = jnp.where(i < off, ctx_ref[...], x_ref[...])
        r, xh, xg, h, scale = _modulated(xb, mod_ref, ng)
        hb_ = h.astype(BF16)
        dh = jnp.zeros((TB, D), F32)
        for s, (lo, hi) in enumerate(SEG):
            dwin_ref[lo:hi, :] += _tn(du[s], hb_)
            dh += _nn(du[s], win_ref[lo:hi, :])
        is_lat = latent.astype(F32)
        dsh = _colsum(dh)
        dsc = _colsum(dh * xg)
        dmod_ref[0, 0:1, :] += dsh * (1.0 - is_lat)
        dmod_ref[0, 1:2, :] += dsc * (1.0 - is_lat)
        dmod_ref[1, 0:1, :] += dsh * is_lat
        dmod_ref[1, 1:2, :] += dsc * is_lat
        dxg = dh * (1.0 + scale)
        dng_ref[...] += _colsum(dxg * xh)
        dxh = dxg * ng
        gx_ref[...] = r * (dxh - xh * jnp.mean(dxh * xh, axis=-1, keepdims=True)) + dxn_ref[...]

    row = lambda w: pl.BlockSpec((TB, w), lambda i: (i, 0))
    lat = lambda w: pl.BlockSpec((TB, w), lambda i: (jnp.maximum(i - off, 0), 0))
    last8 = s_len // HALO - 1
    cspec, xspec, mspec = _token_specs(off)
    return pl.pallas_call(
        body, name="in_bwd", grid=(nb,),
        in_specs=[cspec, xspec, mspec, _full((1, D)), row(512), lat(512), lat(512), lat(512),
                  pl.BlockSpec((HALO, 512), lambda i: (jnp.maximum(jnp.maximum(i - off, 0) * hb - 1, 0), 0)),
                  pl.BlockSpec((HALO, 512), lambda i: (jnp.minimum((jnp.maximum(i - off, 0) + 1) * hb, last8), 0)),
                  lat(D), _full((DIN, D))],
        out_specs=[lat(D), _full((DIN, D)), _full((2, 2, D)), _full((1, D))],
        out_shape=[jax.ShapeDtypeStruct((s_len, D), F32), jax.ShapeDtypeStruct((DIN, D), F32),
                   jax.ShapeDtypeStruct((2, 2, D), F32), jax.ShapeDtypeStruct((1, D), F32)],
        compiler_params=_params(("arbitrary",)),
    )(ctx, x, modsel, norm_g, dlo, dga, dgp, dpool, dpool, dpool, dxn, w_in_t)


def _adamw_update(w_ref, g_ref, m_ref, v_ref, d_ref, mo_ref, vo_ref):
    gv = g_ref[...]
    mn = ADAM_B1 * m_ref[...] + (1.0 - ADAM_B1) * gv
    vn = ADAM_B2 * v_ref[...] + (1.0 - ADAM_B2) * (gv * gv)
    m_hat = mn / (1.0 - ADAM_B1 ** ADAM_STEP)
    v_hat = vn / (1.0 - ADAM_B2 ** ADAM_STEP)
    d_ref[...] = -ADAM_LR * (m_hat / (jnp.sqrt(v_hat) + ADAM_EPS) + ADAM_WD * w_ref[...])
    mo_ref[...] = mn
    vo_ref[...] = vn


def _adamw_small(ws, gs, ms, vs):
    n = len(ws)

    def body(*refs):
        for i in range(n):
            _adamw_update(refs[i], refs[n + i], refs[2 * n + i], refs[3 * n + i], refs[4 * n + i], refs[5 * n + i], refs[6 * n + i])

    shp = [jax.ShapeDtypeStruct(w.shape, F32) for w in ws]
    out = pl.pallas_call(body, name="adamw_small", in_specs=[VM] * (4 * n), out_specs=[VM] * (3 * n), out_shape=shp * 3,
                         compiler_params=pltpu.CompilerParams(vmem_limit_bytes=VMEM_LIMIT))(*ws, *gs, *ms, *vs)
    return out[:n], out[n:2 * n], out[2 * n:]


def _adamw(w, g, m, v, name):
    rows, cols = w.shape
    rb = next(r for r in range(min(rows, 256), 0, -8) if rows % r == 0)

    def body(w_ref, g_ref, m_ref, v_ref, d_ref, mo_ref, vo_ref):
        _adamw_update(w_ref, g_ref, m_ref, v_ref, d_ref, mo_ref, vo_ref)

    spec = pl.BlockSpec((rb, cols), lambda i: (i, 0))
    shp = jax.ShapeDtypeStruct((rows, cols), F32)
    return pl.pallas_call(
        body, name=name, grid=(rows // rb,), in_specs=[spec] * 4, out_specs=[spec] * 3, out_shape=[shp] * 3,
        compiler_params=_params(("arbitrary",)),
    )(w, g, m, v)


class _Links:
    def __init__(self, send_sems, recv_sems):
        self.send_sems, self.recv_sems, self.sends = send_sems, recv_sems, []

    def send(self, src, dst, sem, to):
        cp = pltpu.make_async_remote_copy(src, dst, self.send_sems.at[sem], self.recv_sems.at[sem], device_id=to,
                                          device_id_type=MESH)
        cp.start()
        self.sends.append(cp)

    def arrived(self, dst, sem, frm):
        pltpu.make_async_remote_copy(dst, dst, self.send_sems.at[sem], self.recv_sems.at[sem], device_id=frm,
                                     device_id_type=MESH).wait_recv()

    def drain(self):
        for cp in self.sends:
            cp.wait_send()


def _half(ref, c, axis):
    size = ref.shape[axis - 2] // 2
    win = pl.ds(pl.multiple_of(c * size, 16 if axis == 0 else 128), size)
    idx = (win, slice(None)) if axis == 0 else (slice(None), win)
    return ref.at[(slice(None),) * (len(ref.shape) - 2) + idx]


def _select_rows(slots_ref, n_slots, row=0):
    sub = lax.broadcasted_iota(jnp.int32, (8, 1), 0)
    out = None
    for d in range(n_slots):
        r = jnp.where(sub == d, jnp.broadcast_to(slots_ref[d][row:row + 1, :], (8, slots_ref.shape[-1])), 0.0)
        out = r if out is None else out + r
    return out


def _gather(c, c_ctx, w_mod, b_mod_k, shards, axes):
    nw = len(shards)
    kw = w_mod.shape[1]

    def body(*refs):
        c_ref, cc_ref, wm_ref, b_ref = refs[:4]
        w_refs = refs[4:4 + nw]
        a16_ref, mod_ref = refs[4 + nw:6 + nw]
        g_refs = refs[6 + nw:6 + 2 * nw]
        a_ref, send_sems, recv_sems, local_sems = refs[6 + 2 * nw:]
        x, y, cc = lax.axis_index("x"), lax.axis_index("y"), lax.axis_index("c")
        me = 4 * x + 2 * y + cc
        k = 2 * x + y
        sibling = (x, y, 1 - cc)
        links = _Links(send_sems, recv_sems)
        chips = [_peer(x, y, cc, off + (0,)) for off in CHIPS3]
        locals_ = []
        for wi in range(nw):
            lc = pltpu.make_async_copy(w_refs[wi], g_refs[wi].at[k], local_sems.at[wi])
            lc.start()
            locals_.append(lc)
            for j, to in enumerate(chips):
                links.send(_half(w_refs[wi], cc, axes[wi]), _half(g_refs[wi].at[k], cc, axes[wi]), 10 + wi * 6 + j, to)
        cv = c_ref[...]
        a_ref[me] = jnp.broadcast_to(cv * _sig(cv), (8, D))
        for j, off in enumerate(PEERS7):
            links.send(a_ref.at[me], a_ref.at[me], j, _peer(x, y, cc, off))
        for j, off in enumerate(PEERS7):
            px, py, pc = _peer(x, y, cc, off)
            links.arrived(a_ref.at[4 * px + 2 * py + pc], j, (px, py, pc))
        ccv = cc_ref[...]
        sub = lax.broadcasted_iota(jnp.int32, (8, 1), 0)
        a16 = jnp.concatenate([_select_rows(a_ref, 8), jnp.where(sub == 0, jnp.broadcast_to(ccv * _sig(ccv), (8, D)), 0.0)], axis=0)
        a16_ref[...] = a16
        mod_ref[k] = _dot3(_nn, a16, wm_ref[...]) + b_ref[...]
        for j, to in enumerate(chips):
            links.send(mod_ref.at[k], mod_ref.at[k], 7 + j, to)
        for wi in range(nw):
            for j, (px, py, pc) in enumerate(chips):
                blk = _half(g_refs[wi].at[2 * px + py], cc, axes[wi])
                links.arrived(blk, 10 + wi * 6 + j, (px, py, pc))
                links.send(blk, blk, 10 + wi * 6 + 3 + j, sibling)
        for j, (px, py, pc) in enumerate(chips):
            links.arrived(mod_ref.at[2 * px + py], 7 + j, (px, py, pc))
        for wi in range(nw):
            for j, (px, py, pc) in enumerate(chips):
                links.arrived(_half(g_refs[wi].at[2 * px + py], 1 - cc, axes[wi]), 10 + wi * 6 + 3 + j, sibling)
        links.drain()
        for lc in locals_:
            lc.wait()

    nsem = 10 + 6 * nw
    return pl.pallas_call(
        body, name="gather", in_specs=[VM] * 4 + [ANY] * nw, out_specs=[VM, VM] + [ANY] * nw,
        out_shape=[jax.ShapeDtypeStruct((16, D), F32), jax.ShapeDtypeStruct((4, 16, kw), F32)]
        + [jax.ShapeDtypeStruct((4,) + s.shape, s.dtype) for s in shards],
        scratch_shapes=[pltpu.VMEM((8, 8, D), F32), pltpu.SemaphoreType.DMA((nsem,)), pltpu.SemaphoreType.DMA((nsem,)),
                        pltpu.SemaphoreType.DMA((nw,))],
        compiler_params=pltpu.CompilerParams(vmem_limit_bytes=VMEM_LIMIT),
    )(c, c_ctx, w_mod, b_mod_k, *shards)


SMALL_ROW_WIDTHS = (D, QL, KVL, DKP, DKP, 512, 128)
SMALL_OUT_WIDTHS = (D, QL, KVL, DK, DK, 512, 1)


def _reduce(grads, axes, smalls, w_pool_g, dmod8, a16, w_mod, c_ctx):
    nw = len(grads)
    ns = len(smalls)
    kw = w_mod.shape[1]
    halves = []
    for g, ax in zip(grads, axes):
        halves.append((g.shape[1] // 2, g.shape[2]) if ax == 0 else (g.shape[1], g.shape[2] // 2))

    def body(*refs):
        g_refs = refs[:nw]
        small_refs = refs[nw:nw + ns]
        wp_ref, dm_ref, a16_ref, wm_ref, cc_ref = refs[nw + ns:nw + ns + 5]
        o = nw + ns + 5
        r_refs = refs[o:o + nw]
        small_outs = refs[o + nw:o + nw + ns]
        rwp_ref, gw_ref, gb_ref, gc_ref = refs[o + nw + ns:o + nw + ns + 4]
        o = o + nw + ns + 4
        own, sib, part, got = (refs[o + i * nw:o + (i + 1) * nw] for i in range(4))
        smbuf, wpbuf, dm_all, pc_all, send_sems, recv_sems, local_sems = refs[o + 4 * nw:]
        x, y, cc = lax.axis_index("x"), lax.axis_index("y"), lax.axis_index("c")
        me = 4 * x + 2 * y + cc
        k = 2 * x + y
        sibling = (x, y, 1 - cc)
        links = _Links(send_sems, recv_sems)
        chips = [_peer(x, y, cc, off + (0,)) for off in CHIPS3]
        peers = [_peer(x, y, cc, off) for off in PEERS7]
        big, sm0, wp0, dm0, pc0 = 0, 5 * nw, 5 * nw + 7, 5 * nw + 14, 5 * nw + 21

        locals_ = []
        for wi in range(nw):
            lc = pltpu.make_async_copy(_half(g_refs[wi], cc, axes[wi]), own[wi], local_sems.at[wi])
            lc.start()
            locals_.append(lc)
            links.send(_half(g_refs[wi], 1 - cc, axes[wi]), sib[wi], big + wi * 5, sibling)
        slot = smbuf.at[me]
        slot[...] = jnp.zeros((8, D), F32)
        for r, (ref, w) in enumerate(zip(small_refs, SMALL_ROW_WIDTHS)):
            slot[r:r + 1, 0:w] = jnp.broadcast_to(ref[...], (1, w))
        wpbuf[me] = wp_ref[...]
        dm_all[me] = dm_ref[...]
        for j, peer in enumerate(peers):
            links.send(dm_all.at[me], dm_all.at[me], dm0 + j, peer)
            links.send(smbuf.at[me], smbuf.at[me], sm0 + j, peer)
            links.send(wpbuf.at[me], wpbuf.at[me], wp0 + j, peer)
        for wi in range(nw):
            locals_[wi].wait()
            links.arrived(sib[wi], big + wi * 5, sibling)
            part[wi][...] = (own[wi][...] + sib[wi][...]).astype(BF16)
            got[wi][k] = part[wi][k]
            for j, (px, py, pc) in enumerate(chips):
                links.send(part[wi].at[2 * px + py], got[wi].at[k], big + wi * 5 + 1 + j, (px, py, pc))
        for j, (px, py, pc) in enumerate(peers):
            links.arrived(dm_all.at[4 * px + 2 * py + pc], dm0 + j, (px, py, pc))
        dmc = dm_all[0][1:2, :]
        dml = dm_all[0][0:1, :]
        for d in range(1, 8):
            dmc = dmc + dm_all[d][1:2, :]
            dml = dml + dm_all[d][0:1, :]
        gb_ref[...] = dml + dmc
        sub = lax.broadcasted_iota(jnp.int32, (8, 1), 0)
        b16 = jnp.concatenate([_select_rows(dm_all, 8), jnp.where(sub == 0, jnp.broadcast_to(dmc, (8, 3 * D)), 0.0)], axis=0)
        bk = jnp.zeros((16, kw), F32)
        for kk in range(4):
            bk = bk + jnp.where(k == kk, b16[:, kk * kw:(kk + 1) * kw], 0.0)
        gw_ref[...] = _dot3(_tn, a16_ref[...], bk)
        pc_all[k] = _dot3(_nt, jnp.broadcast_to(bk[8:9, :], (8, kw)), wm_ref[...])
        for j, to in enumerate(chips):
            links.send(pc_all.at[k], pc_all.at[k], pc0 + j, to)
        for j, (px, py, pc) in enumerate(peers):
            links.arrived(smbuf.at[4 * px + 2 * py + pc], sm0 + j, (px, py, pc))
            links.arrived(wpbuf.at[4 * px + 2 * py + pc], wp0 + j, (px, py, pc))
        tot = smbuf[0]
        wpt = wpbuf[0]
        for d in range(1, 8):
            tot = tot + smbuf[d]
            wpt = wpt + wpbuf[d]
        for r, (ref, w) in enumerate(zip(small_outs, SMALL_OUT_WIDTHS)):
            ref[...] = tot[r:r + 1, 0:w]
        rwp_ref[...] = wpt
        for wi in range(nw):
            for j, (px, py, pc) in enumerate(chips):
                links.arrived(got[wi].at[2 * px + py], big + wi * 5 + 1 + j, (px, py, pc))
            total = got[wi][0].astype(F32)
            for kk in range(1, 4):
                total = total + got[wi][kk].astype(F32)
            mine = _half(r_refs[wi], cc, axes[wi])
            mine[...] = total
            links.send(mine, mine, big + wi * 5 + 4, sibling)
        for j, (px, py, pc) in enumerate(chips):
            links.arrived(pc_all.at[2 * px + py], pc0 + j, (px, py, pc))
        ccv = cc_ref[...]
        sg = _sig(ccv)
        gc_ref[...] = (pc_all[0][0:1, :] + pc_all[1][0:1, :] + pc_all[2][0:1, :] + pc_all[3][0:1, :]) * (sg * (1.0 + ccv * (1.0 - sg)))
        for wi in range(nw):
            links.arrived(_half(r_refs[wi], 1 - cc, axes[wi]), big + wi * 5 + 4, sibling)
        links.drain()

    nsem = 5 * nw + 24
    quads = [(4,) + h for h in halves]
    return pl.pallas_call(
        body, name="reduce", in_specs=[ANY] * nw + [VM] * (ns + 5), out_specs=[VM] * (nw + ns + 4),
        out_shape=[jax.ShapeDtypeStruct(g.shape[1:], F32) for g in grads]
        + [jax.ShapeDtypeStruct((1, w), F32) for w in SMALL_OUT_WIDTHS]
        + [jax.ShapeDtypeStruct(w_pool_g.shape, F32), jax.ShapeDtypeStruct((D, kw), F32), jax.ShapeDtypeStruct((1, 3 * D), F32),
           jax.ShapeDtypeStruct((1, D), F32)],
        scratch_shapes=[pltpu.VMEM(q, F32) for q in quads] + [pltpu.VMEM(q, F32) for q in quads]
        + [pltpu.VMEM(q, BF16) for q in quads] + [pltpu.VMEM(q, BF16) for q in quads]
        + [pltpu.VMEM((8, 8, D), F32), pltpu.VMEM((8,) + w_pool_g.shape, F32), pltpu.VMEM((8, 8, 3 * D), F32),
           pltpu.VMEM((4, 8, D), F32)]
        + [pltpu.SemaphoreType.DMA((nsem,)), pltpu.SemaphoreType.DMA((nsem,)), pltpu.SemaphoreType.DMA((nw,))],
        compiler_params=pltpu.CompilerParams(vmem_limit_bytes=VMEM_LIMIT),
    )(*grads, *smalls, w_pool_g, dmod8, a16, w_mod, c_ctx)


def _rope_tables(s_len):
    rows = s_len // GRID_W
    per = TB // GRID_W
    n_freq = 16
    inv = ROPE_BASE ** (-jnp.arange(n_freq, dtype=F32) / n_freq)
    ang_r = jnp.arange(rows, dtype=F32)[:, None] * inv
    ang_c = jnp.arange(GRID_W, dtype=F32)[:, None] * inv
    by_row, by_col = [], []
    for fn, pad in ((jnp.cos, 1.0), (jnp.sin, 0.0)):
        r = jnp.concatenate([fn(ang_r), fn(ang_r), jnp.zeros((rows, 96), F32)], axis=1).reshape(rows // per, per, 128)
        by_row.append(jnp.pad(r, ((0, 0), (0, 8 - per), (0, 0))))
        cpart = jnp.concatenate([jnp.zeros((GRID_W, 32), F32), fn(ang_c), fn(ang_c), jnp.full((GRID_W, 64), pad, F32)], axis=1)
        by_col.append(jnp.tile(cpart, (per, 1)))
    return jnp.concatenate(by_row, axis=-1), jnp.concatenate(by_col, axis=-1)


def kernel(x, c, ctx, c_ctx, w_mod, b_mod, norm_g, w_in, q_lora_g, w_uq, kv_lora_g, w_ukv, q_norm_g, k_norm_g, w_pool, pool_scale, w_out, loss_target, m_c_ctx, m_w_mod, m_b_mod, m_norm_g, m_w_in, m_q_lora_g, m_w_uq, m_kv_lora_g, m_w_ukv, m_q_norm_g, m_k_norm_g, m_w_pool, m_pool_scale, m_w_out, v_c_ctx, v_w_mod, v_b_mod, v_norm_g, v_w_in, v_q_lora_g, v_w_uq, v_kv_lora_g, v_w_ukv, v_q_norm_g, v_k_norm_g, v_w_pool, v_pool_scale, v_w_out):
    xi, yi, ci = lax.axis_index("x"), lax.axis_index("y"), lax.axis_index("c")
    me = 4 * xi + 2 * yi + ci
    k = 2 * xi + yi
    s_len = x.shape[1]
    lc = ctx.shape[1]
    kw = w_mod.shape[2]
    weights = dict(c_ctx=c_ctx, w_mod=w_mod, b_mod=b_mod, norm_g=norm_g, w_in=w_in, q_lora_g=q_lora_g, w_uq=w_uq,
                   kv_lora_g=kv_lora_g, w_ukv=w_ukv, q_norm_g=q_norm_g, k_norm_g=k_norm_g, w_pool=w_pool,
                   pool_scale=pool_scale, w_out=w_out)
    m_in = dict(c_ctx=m_c_ctx, w_mod=m_w_mod, b_mod=m_b_mod, norm_g=m_norm_g, w_in=m_w_in, q_lora_g=m_q_lora_g, w_uq=m_w_uq,
                kv_lora_g=m_kv_lora_g, w_ukv=m_w_ukv, q_norm_g=m_q_norm_g, k_norm_g=m_k_norm_g, w_pool=m_w_pool,
                pool_scale=m_pool_scale, w_out=m_w_out)
    v_in = dict(c_ctx=v_c_ctx, w_mod=v_w_mod, b_mod=v_b_mod, norm_g=v_norm_g, w_in=v_w_in, q_lora_g=v_q_lora_g, w_uq=v_w_uq,
                kv_lora_g=v_kv_lora_g, w_ukv=v_w_ukv, q_norm_g=v_q_norm_g, k_norm_g=v_k_norm_g, w_pool=v_w_pool,
                pool_scale=v_pool_scale, w_out=v_w_out)
    order = ["c_ctx", "w_mod", "b_mod", "norm_g", "w_in", "q_lora_g", "w_uq", "kv_lora_g", "w_ukv", "q_norm_g", "k_norm_g",
             "w_pool", "pool_scale", "w_out"]
    transposed = ("w_in", "w_uq")
    as2d = lambda n, a: jnp.transpose(a[0]) if n in transposed else a.reshape(-1, a.shape[-1])
    back = lambda n, a: jnp.transpose(a)[None] if n in transposed else a.reshape(weights[n].shape)

    c_ctx2 = c_ctx.reshape(1, D)
    b_mod_k = lax.dynamic_slice(b_mod, (0, k * kw), (1, kw))
    split = (1, 0, 0, 0)
    a16, mod_all, g_in, g_uq, g_ukv, g_out = _gather(
        c, c_ctx2, w_mod[0], b_mod_k,
        [as2d("w_in", w_in).astype(BF16), as2d("w_uq", w_uq).astype(BF16), w_ukv[0].astype(BF16), w_out[0].astype(BF16)], split)
    mod_me = lax.dynamic_index_in_dim(mod_all, me, axis=1, keepdims=False).reshape(3, D)
    mod_c = mod_all[:, 8, :].reshape(3, D)
    modsel = jnp.stack([mod_c, mod_me])
    w_in_t = g_in.reshape(DIN, D)
    w_uq_t = jnp.pad(g_uq, ((0, 0), (0, DKP - DK), (0, 0)))
    w_out_f = g_out.reshape(D, D)
    qn_g = jnp.pad(q_norm_g, ((0, 0), (0, DKP - DK)))
    kn_g = jnp.pad(k_norm_g, ((0, 0), (0, DKP - DK)))
    w_pool_b = w_pool[0].astype(BF16)
    cos, sin = _rope_tables(s_len)

    u, q, kk, v = _fwd_in(ctx[0], x[0], modsel, norm_g, w_in_t, q_lora_g, w_uq_t, kv_lora_g, g_ukv, qn_g, kn_g, cos, sin)
    attn, lse = _attn_fwd(q, kk, v, s_len)
    (dxn, dattn, dga, dgp, dpool, dw_out, dgate, dps, dw_pool, loss) = _out_stage(
        attn.reshape(s_len // Q_BLOCK, Q_BLOCK, NH * DV), u, x[0], loss_target[0], modsel[1, 2:3, :], w_pool_b, pool_scale,
        w_out_f, lc)
    dattn = dattn.reshape(s_len, NH * DV)
    dq, dk, dv = _attn_bwd(q, kk, v, dattn, attn, lse, s_len)
    dlo, dw_uq_t, dw_ukv, dqlg, dkvlg, dqng, dkng = _qkv_bwd(u, dq, dk, dv, cos, sin, q_lora_g, w_uq_t, kv_lora_g, g_ukv,
                                                            qn_g, kn_g, s_len)
    gx, dw_in_t, dmod, dng = _in_bwd(ctx[0], x[0], modsel, norm_g, dlo, dga, dgp, dpool, dxn, w_in_t)

    dmod_l = jnp.concatenate([dmod[1, 0], dmod[1, 1], dgate[0]]).reshape(1, 3 * D)
    dmod_c = jnp.concatenate([dmod[0, 0], dmod[0, 1], jnp.zeros((D,), F32)]).reshape(1, 3 * D)
    dmod8 = jnp.concatenate([dmod_l, dmod_c, jnp.zeros((6, 3 * D), F32)], axis=0)
    (r_in, r_uq, r_ukv, r_out, g_ng, g_qlg, g_kvlg, g_qng, g_kng, g_ps, loss_all, g_wp, g_w_mod, g_b_mod, g_c_ctx) = _reduce(
        [dw_in_t.reshape(4, DIN // 4, D), dw_uq_t, dw_ukv, dw_out.reshape(4, D // 4, D)], split,
        [dng, dqlg, dkvlg, dqng, dkng, dps, loss], dw_pool, dmod8, a16, w_mod[0], c_ctx2)
    g2d = dict(c_ctx=g_c_ctx, b_mod=g_b_mod, w_mod=g_w_mod, w_in=r_in, w_uq=r_uq, w_ukv=r_ukv, w_out=r_out, norm_g=g_ng,
               q_lora_g=g_qlg, kv_lora_g=g_kvlg, q_norm_g=g_qng, k_norm_g=g_kng, pool_scale=g_ps, w_pool=g_wp.reshape(512, 128))

    d2d, m2d, v2d = {}, {}, {}
    for n in ("w_mod", "w_in", "w_uq", "w_ukv", "w_out"):
        d2d[n], m2d[n], v2d[n] = _adamw(as2d(n, weights[n]), g2d[n], as2d(n, m_in[n]), as2d(n, v_in[n]), "adamw_" + n)
    small = [n for n in order if n not in d2d]
    outs = _adamw_small([as2d(n, weights[n]) for n in small], [g2d[n] for n in small], [as2d(n, m_in[n]) for n in small],
                        [as2d(n, v_in[n]) for n in small])
    for dst, arrs in zip((d2d, m2d, v2d), outs):
        dst.update(dict(zip(small, arrs)))

    return (loss_all[0, 0], gx[None], *[back(n, g2d[n]) for n in order], *[back(n, d2d[n]) for n in order],
            *[back(n, m2d[n]) for n in order], *[back(n, v2d[n]) for n in order])
```

```python
import jax
import jax.numpy as jnp
from jax import lax
from jax.experimental import pallas as pl
from jax.experimental.pallas import tpu as pltpu

F32 = jnp.float32
BF16 = jnp.bfloat16
MESH = pl.DeviceIdType.MESH

D = 1024
NH = 4
DK = 192
DKP = 256
DV = 128
QL = 256
KVL = 128
DIN = 1984
U_LO = 448
SEG = ((0, 512), (448, 960), (960, 1472), (1472, 1984))
DU = 2048
POOL_WINDOWS = (2, 4, 8, 16)
HALO = 8
EPS = 1e-6
ROPE_BASE = 10000.0
GRID_W = 64
Q_BLOCK = 128
TB = 256
BWD_QBLOCKS = 1
SCALE = DK ** -0.5
LOG2E = 1.4426950408889634
LN2 = 0.6931471805599453
VMEM_LIMIT = 56 * 1024 * 1024

ADAM_LR = 0.001
ADAM_B1 = 0.9
ADAM_B2 = 0.999
ADAM_EPS = 1e-08
ADAM_WD = 0.01
ADAM_STEP = 10

CHIPS3 = ((1, 0), (0, 1), (1, 1))
PEERS7 = tuple((dx, dy, dc) for dx in (0, 1) for dy in (0, 1) for dc in (0, 1) if (dx, dy, dc) != (0, 0, 0))

VM = pl.BlockSpec(memory_space=pltpu.VMEM)
ANY = pl.BlockSpec(memory_space=pl.ANY)


def _nn(a, b):
    return jnp.dot(a, b, preferred_element_type=F32)


def _nt(a, b):
    return lax.dot_general(a, b, (((1,), (1,)), ((), ())), preferred_element_type=F32)


def _tn(a, b):
    return lax.dot_general(a, b, (((0,), (0,)), ((), ())), preferred_element_type=F32)


def _split3(a):
    a0 = a.astype(BF16)
    r = a - a0.astype(F32)
    a1 = r.astype(BF16)
    a2 = (r - a1.astype(F32)).astype(BF16)
    return a0, a1, a2


def _dot3(dot, a, b):
    sa = _split3(a)
    sb = _split3(b)
    out = None
    for i in range(3):
        for j in range(3 - i):
            t = dot(sa[i], sb[j])
            out = t if out is None else out + t
    return out


def _sig(x):
    return 1.0 / (1.0 + jnp.exp(-x))


def _rot(t):
    lane = lax.broadcasted_iota(jnp.int32, t.shape, 1)
    return jnp.where((lane % 32) < 16, -pltpu.roll(t, 112, 1), pltpu.roll(t, 16, 1))


def _rope(t, cos, sin):
    return t * cos + _rot(t) * sin


def _rope_t(t, cos, sin):
    return t * cos - _rot(t * sin)


def _rope_block(rows_ref, cols_ref, is_ctx):
    lane = lax.broadcasted_iota(jnp.int32, (TB, 256), 1) % 128
    rows = jnp.concatenate([jnp.broadcast_to(rows_ref[0, r:r + 1, :], (GRID_W, 256)) for r in range(TB // GRID_W)], axis=0)
    cs = jnp.where(lane < 32, rows, cols_ref[...])
    return jnp.where(is_ctx, 1.0, cs[:, :128]), jnp.where(is_ctx, 0.0, cs[:, 128:])


def _shift_rows(z, k):
    n = z.shape[0]
    return pltpu.roll(z, (n - k) % n, 0)


def _colsum(a):
    return jnp.sum(a, axis=0, keepdims=True)


def _rowsum(a):
    return jnp.sum(a, axis=-1, keepdims=True)


def _row_layout(col):
    return jnp.transpose(jnp.broadcast_to(col, (col.shape[0], 128)))[0:8, :]


def _params(sem=None):
    return pltpu.CompilerParams(dimension_semantics=sem, vmem_limit_bytes=VMEM_LIMIT)


def _full(shape):
    nd = len(shape)
    return pl.BlockSpec(shape, lambda *_: (0,) * nd)


def _peer(x, y, c, off):
    dx, dy, dc = off
    return ((x + dx) % 2, (y + dy) % 2, (c + dc) % 2)


def _token_specs(off):
    ctx = pl.BlockSpec((TB, D), lambda i: (jnp.minimum(i, off - 1), 0))
    lat = pl.BlockSpec((TB, D), lambda i: (jnp.maximum(i - off, 0), 0))
    mod = pl.BlockSpec((1, 3, D), lambda i: (jnp.minimum(i // off, 1), 0, 0))
    return ctx, lat, mod


def _modulated(x, mod_ref, ng):
    shift = mod_ref[0, 0:1, :]
    scale = mod_ref[0, 1:2, :]
    r = lax.rsqrt(jnp.mean(x * x, axis=-1, keepdims=True) + EPS)
    xh = x * r
    xg = xh * ng
    return r, xh, xg, xg * (1.0 + scale) + shift, scale


def _fwd_in(ctx, x, modsel, norm_g, w_in_t, q_lora_g, w_uq_t, kv_lora_g, w_ukv, qn_g, kn_g, cos, sin):
    s_len, lc = x.shape[0], ctx.shape[0]
    t_all = s_len + lc
    nb = t_all // TB
    off = lc // TB

    def body(ctx_ref, x_ref, mod_ref, ng_ref, win_ref, qlg_ref, wuq_ref, kvlg_ref, wukv_ref, qng_ref, kng_ref, cos_ref, sin_ref,
             u_ref, q_ref, k_ref, v_ref):
        i = pl.program_id(0)
        xb = jnp.where(i < off, ctx_ref[...], x_ref[...])
        _, _, _, h, _ = _modulated(xb, mod_ref, ng_ref[...])
        hb = h.astype(BF16)
        lane = lax.broadcasted_iota(jnp.int32, (TB, 512), 1)
        ulo = jnp.where(lane < U_LO, _nt(hb, win_ref[SEG[0][0]:SEG[0][1], :]), 0.0)
        u_ref[:, 0:512] = ulo
        for j in range(1, 4):
            u_ref[:, j * 512:(j + 1) * 512] = _nt(hb, win_ref[SEG[j][0]:SEG[j][1], :])
        cos, sin = _rope_block(cos_ref, sin_ref, pl.program_id(0) < off)
        cq = ulo[:, 0:QL]
        cqn = (cq * lax.rsqrt(jnp.mean(cq * cq, axis=-1, keepdims=True) + EPS) * qlg_ref[...]).astype(BF16)
        qng = qng_ref[...]
        for hd in range(NH):
            qh = _nt(cqn, wuq_ref[hd])
            qn = qh * lax.rsqrt(_rowsum(qh * qh) / DK + EPS) * qng
            q_ref[hd] = (jnp.concatenate([qn[:, :128], _rope(qn[:, 128:], cos, sin)], axis=1) * (SCALE * LOG2E)).astype(BF16)
        ckv = ulo[:, QL:QL + KVL]
        ckvn = (ckv * lax.rsqrt(jnp.mean(ckv * ckv, axis=-1, keepdims=True) + EPS) * kvlg_ref[...]).astype(BF16)
        kr = ulo[:, 384:512]
        skr = _rowsum(kr * kr)
        kng = kng_ref[...]
        for hd in range(NH):
            kv = _nn(ckvn, wukv_ref[hd])
            kn = kv[:, :128]
            rk = lax.rsqrt((_rowsum(kn * kn) + skr) / DK + EPS)
            k_ref[hd] = jnp.concatenate([kn * rk * kng[:, :128], _rope(kr * rk * kng[:, 128:], cos, sin)], axis=1).astype(BF16)
            v_ref[hd] = kv[:, 128:].astype(BF16)

    row = lambda w: pl.BlockSpec((TB, w), lambda i: (i, 0))
    heads = lambda w: pl.BlockSpec((NH, TB, w), lambda i: (0, i, 0))
    cspec, xspec, mspec = _token_specs(off)
    return pl.pallas_call(
        body, name="fwd_in", grid=(nb,),
        in_specs=[cspec, xspec, mspec, _full((1, D)), _full((DIN, D)), _full((1, QL)), _full((NH, DKP, QL)), _full((1, KVL)),
                  _full((NH, KVL, 256)), _full((1, DKP)), _full((1, DKP)), pl.BlockSpec((1, 8, 256), lambda i: (jnp.maximum(i - off, 0), 0, 0)), _full((TB, 256))],
        out_specs=[row(DU), heads(DKP), heads(DKP), heads(DV)],
        out_shape=[jax.ShapeDtypeStruct((t_all, DU), F32), jax.ShapeDtypeStruct((NH, t_all, DKP), BF16),
                   jax.ShapeDtypeStruct((NH, t_all, DKP), BF16), jax.ShapeDtypeStruct((NH, t_all, DV), BF16)],
        compiler_params=_params(("arbitrary",)),
    )(ctx, x, modsel, norm_g, w_in_t, q_lora_g, w_uq_t, kv_lora_g, w_ukv, qn_g, kn_g, cos, sin)


def _attn_fwd(q, k, v, s_len):
    t_all = q.shape[1]
    off = (t_all - s_len) // TB
    nq = s_len // TB
    nsub = next(n for n in (4, 2, 1) if nq % n == 0)

    def body(*refs):
        q_refs = refs[:nsub]
        k_ref, v_ref, o_ref, lse_ref = refs[nsub:]
        for sb in range(nsub):
            s = _nt(q_refs[sb][0], k_ref[0])
            m = jnp.max(s, axis=-1, keepdims=True)
            e = jnp.exp2(s - m)
            l = _rowsum(e)
            o_ref[sb * TB:(sb + 1) * TB, :] = _nn(e.astype(BF16), v_ref[0]) / l
            lse_ref[0, sb] = _row_layout(m + jnp.log2(l))

    qspec = lambda sb: pl.BlockSpec((1, TB, DKP), lambda h, i: (h, i * nsub + sb + off, 0))
    return pl.pallas_call(
        body, name="attn_fwd", grid=(NH, nq // nsub),
        in_specs=[qspec(sb) for sb in range(nsub)]
        + [pl.BlockSpec((1, t_all, DKP), lambda h, i: (h, 0, 0)), pl.BlockSpec((1, t_all, DV), lambda h, i: (h, 0, 0))],
        out_specs=[pl.BlockSpec((nsub * TB, DV), lambda h, i: (i, h)), pl.BlockSpec((1, nsub, 8, TB), lambda h, i: (h, i, 0, 0))],
        out_shape=[jax.ShapeDtypeStruct((s_len, NH * DV), F32), jax.ShapeDtypeStruct((NH, nq, 8, TB), F32)],
        compiler_params=_params(("arbitrary", "arbitrary")),
    )(*([q] * nsub), k, v)


def _out_stage(attn, u, x, target, gate, w_pool, pool_scale, w_out, lc):
    s_len = x.shape[0]
    t_all = s_len + lc
    off = lc // TB
    nq = s_len // TB
    hb = TB // HALO
    nqb = s_len // Q_BLOCK
    jb = TB // nqb

    def body(attn_ref, ga_ref, pin_ref, pprev_ref, pnext_ref, gp_ref, x_ref, tgt_ref, gate_ref, wp_ref, ps_ref, wo_ref,
             dxn_ref, dattn_ref, dga_ref, dgp_ref, dpool_ref, dwo_ref, dgate_ref, dps_ref, dwp_ref, loss_ref):
        i = pl.program_id(0)

        @pl.when(i == 0)
        def _():
            dwo_ref[...] = jnp.zeros_like(dwo_ref)
            dgate_ref[...] = jnp.zeros_like(dgate_ref)
            dps_ref[...] = jnp.zeros_like(dps_ref)
            dwp_ref[...] = jnp.zeros_like(dwp_ref)
            loss_ref[...] = jnp.zeros_like(loss_ref)

        attn = jnp.concatenate([attn_ref[:, jj, :] for jj in range(jb)], axis=0)
        ga = ga_ref[...]
        gp = gp_ref[...]
        pin = pin_ref[...]
        prev = jnp.where(i == 0, 0.0, pprev_ref[...])
        nxt = jnp.where(i == nq - 1, 0.0, pnext_ref[...])
        win = jnp.concatenate([prev, pin, nxt], axis=0)
        tg = i * TB + lax.broadcasted_iota(jnp.int32, (TB, 1), 0)
        pooled = []
        for g, w in enumerate(POOL_WINDOWS):
            a = win[:, g * 128:(g + 1) * 128]
            p = _shift_rows(a, -1) + a
            for step in (1, 2, 4):
                if w >= 4 * step:
                    p = _shift_rows(p, -step) + _shift_rows(p, step)
            cnt = (jnp.minimum(tg + w // 2, s_len) - jnp.maximum(tg - w // 2, 0)).astype(F32)
            pooled.append(p[HALO:HALO + TB] / cnt - a[HALO:HALO + TB])
        pooled_b = [p.astype(BF16) for p in pooled]
        z = jnp.concatenate([_nn(pooled_b[g], wp_ref[g]) for g in range(4)], axis=1)
        ps = ps_ref[...]
        yp = z * ps
        sga = _sig(ga)
        sila = ga * sga
        sgp = _sig(gp)
        silp = gp * sgp
        br = jnp.concatenate([sila * attn, silp * yp], axis=1).astype(BF16)
        y = _nn(br, wo_ref[...])
        gate = gate_ref[...]
        err = x_ref[...] + gate * y - tgt_ref[...]
        loss_ref[...] += _colsum(_rowsum(err * err)) * (0.5 / D)
        dxn = err * (1.0 / D)
        dxn_ref[...] = dxn
        dgate_ref[...] += _colsum(dxn * y)
        dy = (dxn * gate).astype(BF16)
        dwo_ref[...] += _tn(br, dy)
        dbr = _nt(dy, wo_ref[...])
        dbra = dbr[:, :512]
        dbrp = dbr[:, 512:]
        dattn = dbra * sila
        for jj in range(jb):
            dattn_ref[:, jj, :] = dattn[jj * nqb:(jj + 1) * nqb]
        dga_ref[...] = (dbra * attn * (sga * (1.0 + ga * (1.0 - sga)))).astype(BF16)
        dgp_ref[...] = (dbrp * yp * (sgp * (1.0 + gp * (1.0 - sgp)))).astype(BF16)
        dyp = dbrp * silp
        dps_ref[...] += _colsum(dyp * z)
        dz = (dyp * ps).astype(BF16)
        dpool = []
        for g in range(4):
            dzg = dz[:, g * 128:(g + 1) * 128]
            dwp_ref[g] += _tn(pooled_b[g], dzg)
            dpool.append(_nt(dzg, wp_ref[g]))
        dpool_ref[...] = jnp.concatenate(dpool, axis=1)

    lat = lambda w: pl.BlockSpec((TB, w), lambda i: (i, 0))
    perm = pl.BlockSpec((nqb, jb, 512), lambda i: (0, i, 0))
    ucol = lambda j: pl.BlockSpec((TB, 512), lambda i: (i + off, j))
    last8 = t_all // HALO - 1
    return pl.pallas_call(
        body, name="out_stage", grid=(nq,),
        in_specs=[perm, ucol(1), ucol(2),
                  pl.BlockSpec((HALO, 512), lambda i: ((i + off) * hb - 1, 2)),
                  pl.BlockSpec((HALO, 512), lambda i: (jnp.minimum((i + off + 1) * hb, last8), 2)),
                  ucol(3), lat(D), lat(D), _full((1, D)), _full((4, 128, 128)), _full((1, 512)), _full((D, D))],
        out_specs=[lat(D), perm, lat(512), lat(512), lat(512),
                   _full((D, D)), _full((1, D)), _full((1, 512)), _full((4, 128, 128)), _full((1, 1))],
        out_shape=[jax.ShapeDtypeStruct((s_len, D), F32), jax.ShapeDtypeStruct((nqb, Q_BLOCK, 512), F32),
                   jax.ShapeDtypeStruct((s_len, 512), BF16), jax.ShapeDtypeStruct((s_len, 512), BF16),
                   jax.ShapeDtypeStruct((s_len, 512), F32),
                   jax.ShapeDtypeStruct((D, D), F32), jax.ShapeDtypeStruct((1, D), F32), jax.ShapeDtypeStruct((1, 512), F32),
                   jax.ShapeDtypeStruct((4, 128, 128), F32), jax.ShapeDtypeStruct((1, 1), F32)],
        compiler_params=_params(("arbitrary",)),
    )(attn, u, u, u, u, u, x, target, gate, w_pool, pool_scale, w_out)


def _attn_bwd(q, k, v, dattn, attn, lse, s_len):
    t_all = q.shape[1]
    off = (t_all - s_len) // TB
    nq = s_len // TB
    nch = 4
    ch = t_all // nch
    nsub = next(n for n in (BWD_QBLOCKS, 2, 1) if nq % n == 0)
    tq = nsub * TB

    def body(*refs):
        q_refs = refs[:nsub]
        k_ref, v_ref, do_ref, o_ref, lse_ref, dq_ref, dk_ref, dv_ref = refs[nsub:]
        i = pl.program_id(1)

        @pl.when(i == 0)
        def _():
            dk_ref[...] = jnp.zeros_like(dk_ref)
            dv_ref[...] = jnp.zeros_like(dv_ref)

        qb = jnp.concatenate([r[0] for r in q_refs], axis=0)
        delta_r = _row_layout(_rowsum(do_ref[...] * o_ref[...]))[0:1, :]
        do = do_ref[...].astype(BF16)
        lse_r = jnp.concatenate([lse_ref[0, sb][0:1, :] for sb in range(nsub)], axis=1)
        dq = jnp.zeros((tq, DKP), F32)
        for c in range(nch):
            rows = pl.ds(c * ch, ch)
            kc = k_ref[0, rows, :]
            p_t = jnp.exp2(_nt(kc, qb) - lse_r)
            ds_t = (p_t * (_nt(v_ref[0, rows, :], do) - delta_r)).astype(BF16)
            dv_ref[0, rows, :] += _nn(p_t.astype(BF16), do)
            dk_ref[0, rows, :] += _nn(ds_t, qb)
            dq += _tn(ds_t, kc)
        dq_ref[0] = dq * SCALE

    kvspec = lambda w: pl.BlockSpec((1, t_all, w), lambda h, i: (h, 0, 0))
    rowspec = pl.BlockSpec((1, nsub, 8, TB), lambda h, i: (h, i, 0, 0))
    qspec = lambda sb: pl.BlockSpec((1, TB, DKP), lambda h, i: (h, i * nsub + sb + off, 0))
    return pl.pallas_call(
        body, name="attn_bwd", grid=(NH, nq // nsub),
        in_specs=[qspec(sb) for sb in range(nsub)]
        + [kvspec(DKP), kvspec(DV), pl.BlockSpec((tq, DV), lambda h, i: (i, h)), pl.BlockSpec((tq, DV), lambda h, i: (i, h)),
           rowspec],
        out_specs=[pl.BlockSpec((1, tq, DKP), lambda h, i: (h, i, 0)), kvspec(DKP), kvspec(DV)],
        out_shape=[jax.ShapeDtypeStruct((NH, s_len, DKP), F32), jax.ShapeDtypeStruct((NH, t_all, DKP), F32),
                   jax.ShapeDtypeStruct((NH, t_all, DV), F32)],
        compiler_params=_params(("arbitrary", "arbitrary")),
    )(*([q] * nsub), k, v, dattn, attn, lse)


def _qkv_bwd(u, dq, dk, dv, cos, sin, q_lora_g, w_uq_t, kv_lora_g, w_ukv, qn_g, kn_g, s_len):
    t_all = u.shape[0]
    off = (t_all - s_len) // TB
    nb = t_all // TB

    def body(ulo_ref, dq_ref, dk_ref, dv_ref, cos_ref, sin_ref, qlg_ref, wuq_ref, kvlg_ref, wukv_ref, qng_ref, kng_ref,
             dlo_ref, dwuq_ref, dwukv_ref, dqlg_ref, dkvlg_ref, dqng_ref, dkng_ref):
        i = pl.program_id(0)

        @pl.when(i == 0)
        def _():
            for r in (dwuq_ref, dwukv_ref, dqlg_ref, dkvlg_ref, dqng_ref, dkng_ref):
                r[...] = jnp.zeros_like(r)

        latent = i >= off
        ulo = ulo_ref[...]
        cos, sin = _rope_block(cos_ref, sin_ref, pl.program_id(0) < off)
        cq = ulo[:, 0:QL]
        rc = lax.rsqrt(jnp.mean(cq * cq, axis=-1, keepdims=True) + EPS)
        cqh = cq * rc
        qlg = qlg_ref[...]
        cqn_b = (cqh * qlg).astype(BF16)
        qng = qng_ref[...]
        dqng = jnp.zeros((1, DKP), F32)
        dcqn = jnp.zeros((TB, QL), F32)
        for hd in range(NH):
            qh = _nt(cqn_b, wuq_ref[hd])
            rq = lax.rsqrt(_rowsum(qh * qh) / DK + EPS)
            xh = qh * rq
            dqh = jnp.where(latent, dq_ref[hd], 0.0)
            dyq = jnp.concatenate([dqh[:, :128], _rope_t(dqh[:, 128:], cos, sin)], axis=1)
            dqng += _colsum(dyq * xh)
            dxh = dyq * qng
            dqraw = (rq * (dxh - xh * (_rowsum(dxh * xh) / DK))).astype(BF16)
            dwuq_ref[hd] += _tn(dqraw, cqn_b)[:DK]
            dcqn += _nn(dqraw, wuq_ref[hd])
        dqng_ref[...] += dqng
        dqlg_ref[...] += _colsum(dcqn * cqh)
        dxh = dcqn * qlg
        dcq = rc * (dxh - cqh * jnp.mean(dxh * cqh, axis=-1, keepdims=True))

        ckv = ulo[:, QL:QL + KVL]
        r0 = lax.rsqrt(jnp.mean(ckv * ckv, axis=-1, keepdims=True) + EPS)
        ckvh = ckv * r0
        kvlg = kvlg_ref[...]
        ckvn_b = (ckvh * kvlg).astype(BF16)
        kr = ulo[:, 384:512]
        skr = _rowsum(kr * kr)
        kng = kng_ref[...]
        dkr = jnp.zeros((TB, 128), F32)
        dkng = jnp.zeros((1, DKP), F32)
        dckvn = jnp.zeros((TB, KVL), F32)
        for hd in range(NH):
            kn = _nn(ckvn_b, wukv_ref[hd])[:, :128]
            rk = lax.rsqrt((_rowsum(kn * kn) + skr) / DK + EPS)
            xh1 = kn * rk
            xh2 = kr * rk
            dkh = dk_ref[hd] * LN2
            d1 = dkh[:, :128]
            d2 = _rope_t(dkh[:, 128:], cos, sin)
            dkng += jnp.concatenate([_colsum(d1 * xh1), _colsum(d2 * xh2)], axis=1)
            dx1 = d1 * kng[:, :128]
            dx2 = d2 * kng[:, 128:]
            dot = (_rowsum(dx1 * xh1) + _rowsum(dx2 * xh2)) / DK
            dkv = jnp.concatenate([rk * (dx1 - xh1 * dot), dv_ref[hd]], axis=1).astype(BF16)
            dkr += rk * (dx2 - xh2 * dot)
            dwukv_ref[hd] += _tn(ckvn_b, dkv)
            dckvn += _nt(dkv, wukv_ref[hd])
        dkng_ref[...] += dkng
        dkvlg_ref[...] += _colsum(dckvn * ckvh)
        dxh = dckvn * kvlg
        dckv = r0 * (dxh - ckvh * jnp.mean(dxh * ckvh, axis=-1, keepdims=True))
        dlo_ref[...] = jnp.concatenate([dcq, dckv, dkr], axis=1).astype(BF16)

    row = lambda w: pl.BlockSpec((TB, w), lambda i: (i, 0))
    heads = lambda w: pl.BlockSpec((NH, TB, w), lambda i: (0, i, 0))
    return pl.pallas_call(
        body, name="qkv_bwd", grid=(nb,),
        in_specs=[row(512), pl.BlockSpec((NH, TB, DKP), lambda i: (0, jnp.maximum(i - off, 0), 0)), heads(DKP), heads(DV),
                  pl.BlockSpec((1, 8, 256), lambda i: (jnp.maximum(i - off, 0), 0, 0)), _full((TB, 256)), _full((1, QL)), _full((NH, DKP, QL)), _full((1, KVL)), _full((NH, KVL, 256)),
                  _full((1, DKP)), _full((1, DKP))],
        out_specs=[row(512), _full((NH, DK, QL)), _full((NH, KVL, 256)), _full((1, QL)), _full((1, KVL)),
                   _full((1, DKP)), _full((1, DKP))],
        out_shape=[jax.ShapeDtypeStruct((t_all, 512), BF16), jax.ShapeDtypeStruct((NH, DK, QL), F32),
                   jax.ShapeDtypeStruct((NH, KVL, 256), F32), jax.ShapeDtypeStruct((1, QL), F32),
                   jax.ShapeDtypeStruct((1, KVL), F32), jax.ShapeDtypeStruct((1, DKP), F32), jax.ShapeDtypeStruct((1, DKP), F32)],
        compiler_params=_params(("arbitrary",)),
    )(u, dq, dk, dv, cos, sin, q_lora_g, w_uq_t, kv_lora_g, w_ukv, qn_g, kn_g)


def _in_bwd(ctx, x, modsel, norm_g, dlo, dga, dgp, dpool, dxn, w_in_t):
    s_len, lc = x.shape[0], ctx.shape[0]
    t_all = s_len + lc
    off = lc // TB
    nb = t_all // TB
    nq = s_len // TB
    hb = TB // HALO
    n = TB + 2 * HALO

    def body(ctx_ref, x_ref, mod_ref, ng_ref, dlo_ref, dga_ref, dgp_ref, dp_ref, dpprev_ref, dpnext_ref, dxn_ref, win_ref,
             gx_ref, dwin_ref, dmod_ref, dng_ref):
        i = pl.program_id(0)
        j = i - off

        @pl.when(i == 0)
        def _():
            dwin_ref[...] = jnp.zeros_like(dwin_ref)
            dmod_ref[...] = jnp.zeros_like(dmod_ref)
            dng_ref[...] = jnp.zeros_like(dng_ref)

        latent = i >= off
        dp = dp_ref[...]
        prev = jnp.where(j <= 0, 0.0, dpprev_ref[...])
        nxt = jnp.where(j >= nq - 1, 0.0, dpnext_ref[...])
        win = jnp.concatenate([prev, dp, nxt], axis=0)
        tg = j * TB - HALO + lax.broadcasted_iota(jnp.int32, (n, 1), 0)
        dpin = []
        for g, w in enumerate(POOL_WINDOWS):
            cnt = jnp.maximum(jnp.minimum(tg + w // 2, s_len) - jnp.maximum(tg - w // 2, 0), 1).astype(F32)
            zq = win[:, g * 128:(g + 1) * 128] / cnt
            zq = zq + _shift_rows(zq, 1)
            for step in (1, 2, 4):
                if w >= 4 * step:
                    zq = _shift_rows(zq, -step) + _shift_rows(zq, step)
            dpin.append(zq[HALO:HALO + TB] - dp[:, g * 128:(g + 1) * 128])
        zero = jnp.zeros((TB, 512), BF16)
        du = [dlo_ref[...], jnp.where(latent, dga_ref[...], zero),
              jnp.where(latent, jnp.concatenate(dpin, axis=1).astype(BF16), zero), jnp.where(latent, dgp_ref[...], zero)]

        ng = ng_ref[...]
        xb = jnp.where(i < off, ctx_ref[...], x_ref[...])
        r, xh, xg, h, scale = _modulated(xb, mod_ref, ng)
        hb_ = h.astype(BF16)
        dh = jnp.zeros((TB, D), F32)
        for s, (lo, hi) in enumerate(SEG):
            dwin_ref[lo:hi, :] += _tn(du[s], hb_)
            dh += _nn(du[s], win_ref[lo:hi, :])
        is_lat = latent.astype(F32)
        dsh = _colsum(dh)
        dsc = _colsum(dh * xg)
        dmod_ref[0, 0:1, :] += dsh * (1.0 - is_lat)
        dmod_ref[0, 1:2, :] += dsc * (1.0 - is_lat)
        dmod_ref[1, 0:1, :] += dsh * is_lat
        dmod_ref[1, 1:2, :] += dsc * is_lat
        dxg = dh * (1.0 + scale)
        dng_ref[...] += _colsum(dxg * xh)
        dxh = dxg * ng
        gx_ref[...] = r * (dxh - xh * jnp.mean(dxh * xh, axis=-1, keepdims=True)) + dxn_ref[...]

    row = lambda w: pl.BlockSpec((TB, w), lambda i: (i, 0))
    lat = lambda w: pl.BlockSpec((TB, w), lambda i: (jnp.maximum(i - off, 0), 0))
    last8 = s_len // HALO - 1
    cspec, xspec, mspec = _token_specs(off)
    return pl.pallas_call(
        body, name="in_bwd", grid=(nb,),
        in_specs=[cspec, xspec, mspec, _full((1, D)), row(512), lat(512), lat(512), lat(512),
                  pl.BlockSpec((HALO, 512), lambda i: (jnp.maximum(jnp.maximum(i - off, 0) * hb - 1, 0), 0)),
                  pl.BlockSpec((HALO, 512), lambda i: (jnp.minimum((jnp.maximum(i - off, 0) + 1) * hb, last8), 0)),
                  lat(D), _full((DIN, D))],
        out_specs=[lat(D), _full((DIN, D)), _full((2, 2, D)), _full((1, D))],
        out_shape=[jax.ShapeDtypeStruct((s_len, D), F32), jax.ShapeDtypeStruct((DIN, D), F32),
                   jax.ShapeDtypeStruct((2, 2, D), F32), jax.ShapeDtypeStruct((1, D), F32)],
        compiler_params=_params(("arbitrary",)),
    )(ctx, x, modsel, norm_g, dlo, dga, dgp, dpool, dpool, dpool, dxn, w_in_t)


def _adamw_update(w_ref, g_ref, m_ref, v_ref, d_ref, mo_ref, vo_ref):
    gv = g_ref[...]
    mn = ADAM_B1 * m_ref[...] + (1.0 - ADAM_B1) * gv
    vn = ADAM_B2 * v_ref[...] + (1.0 - ADAM_B2) * (gv * gv)
    m_hat = mn / (1.0 - ADAM_B1 ** ADAM_STEP)
    v_hat = vn / (1.0 - ADAM_B2 ** ADAM_STEP)
    d_ref[...] = -ADAM_LR * (m_hat / (jnp.sqrt(v_hat) + ADAM_EPS) + ADAM_WD * w_ref[...])
    mo_ref[...] = mn
    vo_ref[...] = vn


def _adamw_small(ws, gs, ms, vs):
    n = len(ws)

    def body(*refs):
        for i in range(n):
            _adamw_update(refs[i], refs[n + i], refs[2 * n + i], refs[3 * n + i], refs[4 * n + i], refs[5 * n + i], refs[6 * n + i])

    shp = [jax.ShapeDtypeStruct(w.shape, F32) for w in ws]
    out = pl.pallas_call(body, name="adamw_small", in_specs=[VM] * (4 * n), out_specs=[VM] * (3 * n), out_shape=shp * 3,
                         compiler_params=pltpu.CompilerParams(vmem_limit_bytes=VMEM_LIMIT))(*ws, *gs, *ms, *vs)
    return out[:n], out[n:2 * n], out[2 * n:]


def _adamw(w, g, m, v, name):
    rows, cols = w.shape
    rb = next(r for r in range(min(rows, 256), 0, -8) if rows % r == 0)

    def body(w_ref, g_ref, m_ref, v_ref, d_ref, mo_ref, vo_ref):
        _adamw_update(w_ref, g_ref, m_ref, v_ref, d_ref, mo_ref, vo_ref)

    spec = pl.BlockSpec((rb, cols), lambda i: (i, 0))
    shp = jax.ShapeDtypeStruct((rows, cols), F32)
    return pl.pallas_call(
        body, name=name, grid=(rows // rb,), in_specs=[spec] * 4, out_specs=[spec] * 3, out_shape=[shp] * 3,
        compiler_params=_params(("arbitrary",)),
    )(w, g, m, v)


class _Links:
    def __init__(self, send_sems, recv_sems):
        self.send_sems, self.recv_sems, self.sends = send_sems, recv_sems, []

    def send(self, src, dst, sem, to):
        cp = pltpu.make_async_remote_copy(src, dst, self.send_sems.at[sem], self.recv_sems.at[sem], device_id=to,
                                          device_id_type=MESH)
        cp.start()
        self.sends.append(cp)

    def arrived(self, dst, sem, frm):
        pltpu.make_async_remote_copy(dst, dst, self.send_sems.at[sem], self.recv_sems.at[sem], device_id=frm,
                                     device_id_type=MESH).wait_recv()

    def drain(self):
        for cp in self.sends:
            cp.wait_send()


def _half(ref, c, axis):
    size = ref.shape[axis - 2] // 2
    win = pl.ds(pl.multiple_of(c * size, 16 if axis == 0 else 128), size)
    idx = (win, slice(None)) if axis == 0 else (slice(None), win)
    return ref.at[(slice(None),) * (len(ref.shape) - 2) + idx]


def _select_rows(slots_ref, n_slots, row=0):
    sub = lax.broadcasted_iota(jnp.int32, (8, 1), 0)
    out = None
    for d in range(n_slots):
        r = jnp.where(sub == d, jnp.broadcast_to(slots_ref[d][row:row + 1, :], (8, slots_ref.shape[-1])), 0.0)
        out = r if out is None else out + r
    return out


def _gather(c, c_ctx, w_mod, b_mod_k, shards, axes):
    nw = len(shards)
    kw = w_mod.shape[1]

    def body(*refs):
        c_ref, cc_ref, wm_ref, b_ref = refs[:4]
        w_refs = refs[4:4 + nw]
        a16_ref, mod_ref = refs[4 + nw:6 + nw]
        g_refs = refs[6 + nw:6 + 2 * nw]
        a_ref, send_sems, recv_sems, local_sems = refs[6 + 2 * nw:]
        x, y, cc = lax.axis_index("x"), lax.axis_index("y"), lax.axis_index("c")
        me = 4 * x + 2 * y + cc
        k = 2 * x + y
        sibling = (x, y, 1 - cc)
        links = _Links(send_sems, recv_sems)
        chips = [_peer(x, y, cc, off + (0,)) for off in CHIPS3]
        locals_ = []
        for wi in range(nw):
            lc = pltpu.make_async_copy(w_refs[wi], g_refs[wi].at[k], local_sems.at[wi])
            lc.start()
            locals_.append(lc)
            for j, to in enumerate(chips):
                links.send(_half(w_refs[wi], cc, axes[wi]), _half(g_refs[wi].at[k], cc, axes[wi]), 10 + wi * 6 + j, to)
        cv = c_ref[...]
        a_ref[me] = jnp.broadcast_to(cv * _sig(cv), (8, D))
        for j, off in enumerate(PEERS7):
            links.send(a_ref.at[me], a_ref.at[me], j, _peer(x, y, cc, off))
        for j, off in enumerate(PEERS7):
            px, py, pc = _peer(x, y, cc, off)
            links.arrived(a_ref.at[4 * px + 2 * py + pc], j, (px, py, pc))
        ccv = cc_ref[...]
        sub = lax.broadcasted_iota(jnp.int32, (8, 1), 0)
        a16 = jnp.concatenate([_select_rows(a_ref, 8), jnp.where(sub == 0, jnp.broadcast_to(ccv * _sig(ccv), (8, D)), 0.0)], axis=0)
        a16_ref[...] = a16
        mod_ref[k] = _dot3(_nn, a16, wm_ref[...]) + b_ref[...]
        for j, to in enumerate(chips):
            links.send(mod_ref.at[k], mod_ref.at[k], 7 + j, to)
        for wi in range(nw):
            for j, (px, py, pc) in enumerate(chips):
                blk = _half(g_refs[wi].at[2 * px + py], cc, axes[wi])
                links.arrived(blk, 10 + wi * 6 + j, (px, py, pc))
                links.send(blk, blk, 10 + wi * 6 + 3 + j, sibling)
        for j, (px, py, pc) in enumerate(chips):
            links.arrived(mod_ref.at[2 * px + py], 7 + j, (px, py, pc))
        for wi in range(nw):
            for j, (px, py, pc) in enumerate(chips):
                links.arrived(_half(g_refs[wi].at[2 * px + py], 1 - cc, axes[wi]), 10 + wi * 6 + 3 + j, sibling)
        links.drain()
        for lc in locals_:
            lc.wait()

    nsem = 10 + 6 * nw
    return pl.pallas_call(
        body, name="gather", in_specs=[VM] * 4 + [ANY] * nw, out_specs=[VM, VM] + [ANY] * nw,
        out_shape=[jax.ShapeDtypeStruct((16, D), F32), jax.ShapeDtypeStruct((4, 16, kw), F32)]
        + [jax.ShapeDtypeStruct((4,) + s.shape, s.dtype) for s in shards],
        scratch_shapes=[pltpu.VMEM((8, 8, D), F32), pltpu.SemaphoreType.DMA((nsem,)), pltpu.SemaphoreType.DMA((nsem,)),
                        pltpu.SemaphoreType.DMA((nw,))],
        compiler_params=pltpu.CompilerParams(vmem_limit_bytes=VMEM_LIMIT),
    )(c, c_ctx, w_mod, b_mod_k, *shards)


SMALL_ROW_WIDTHS = (D, QL, KVL, DKP, DKP, 512, 128)
SMALL_OUT_WIDTHS = (D, QL, KVL, DK, DK, 512, 1)


def _reduce(grads, axes, smalls, w_pool_g, dmod8, a16, w_mod, c_ctx):
    nw = len(grads)
    ns = len(smalls)
    kw = w_mod.shape[1]
    halves = []
    for g, ax in zip(grads, axes):
        halves.append((g.shape[1] // 2, g.shape[2]) if ax == 0 else (g.shape[1], g.shape[2] // 2))

    def body(*refs):
        g_refs = refs[:nw]
        small_refs = refs[nw:nw + ns]
        wp_ref, dm_ref, a16_ref, wm_ref, cc_ref = refs[nw + ns:nw + ns + 5]
        o = nw + ns + 5
        r_refs = refs[o:o + nw]
        small_outs = refs[o + nw:o + nw + ns]
        rwp_ref, gw_ref, gb_ref, gc_ref = refs[o + nw + ns:o + nw + ns + 4]
        o = o + nw + ns + 4
        own, sib, part, got = (refs[o + i * nw:o + (i + 1) * nw] for i in range(4))
        smbuf, wpbuf, dm_all, pc_all, send_sems, recv_sems, local_sems = refs[o + 4 * nw:]
        x, y, cc = lax.axis_index("x"), lax.axis_index("y"), lax.axis_index("c")
        me = 4 * x + 2 * y + cc
        k = 2 * x + y
        sibling = (x, y, 1 - cc)
        links = _Links(send_sems, recv_sems)
        chips = [_peer(x, y, cc, off + (0,)) for off in CHIPS3]
        peers = [_peer(x, y, cc, off) for off in PEERS7]
        big, sm0, wp0, dm0, pc0 = 0, 5 * nw, 5 * nw + 7, 5 * nw + 14, 5 * nw + 21

        locals_ = []
        for wi in range(nw):
            lc = pltpu.make_async_copy(_half(g_refs[wi], cc, axes[wi]), own[wi], local_sems.at[wi])
            lc.start()
            locals_.append(lc)
            links.send(_half(g_refs[wi], 1 - cc, axes[wi]), sib[wi], big + wi * 5, sibling)
        slot = smbuf.at[me]
        slot[...] = jnp.zeros((8, D), F32)
        for r, (ref, w) in enumerate(zip(small_refs, SMALL_ROW_WIDTHS)):
            slot[r:r + 1, 0:w] = jnp.broadcast_to(ref[...], (1, w))
        wpbuf[me] = wp_ref[...]
        dm_all[me] = dm_ref[...]
        for j, peer in enumerate(peers):
            links.send(dm_all.at[me], dm_all.at[me], dm0 + j, peer)
            links.send(smbuf.at[me], smbuf.at[me], sm0 + j, peer)
            links.send(wpbuf.at[me], wpbuf.at[me], wp0 + j, peer)
        for wi in range(nw):
            locals_[wi].wait()
            links.arrived(sib[wi], big + wi * 5, sibling)
            part[wi][...] = (own[wi][...] + sib[wi][...]).astype(BF16)
            got[wi][k] = part[wi][k]
            for j, (px, py, pc) in enumerate(chips):
                links.send(part[wi].at[2 * px + py], got[wi].at[k], big + wi * 5 + 1 + j, (px, py, pc))
        for j, (px, py, pc) in enumerate(peers):
            links.arrived(dm_all.at[4 * px + 2 * py + pc], dm0 + j, (px, py, pc))
        dmc = dm_all[0][1:2, :]
        dml = dm_all[0][0:1, :]
        for d in range(1, 8):
            dmc = dmc + dm_all[d][1:2, :]
            dml = dml + dm_all[d][0:1, :]
        gb_ref[...] = dml + dmc
        sub = lax.broadcasted_iota(jnp.int32, (8, 1), 0)
        b16 = jnp.concatenate([_select_rows(dm_all, 8), jnp.where(sub == 0, jnp.broadcast_to(dmc, (8, 3 * D)), 0.0)], axis=0)
        bk = jnp.zeros((16, kw), F32)
        for kk in range(4):
            bk = bk + jnp.where(k == kk, b16[:, kk * kw:(kk + 1) * kw], 0.0)
        gw_ref[...] = _dot3(_tn, a16_ref[...], bk)
        pc_all[k] = _dot3(_nt, jnp.broadcast_to(bk[8:9, :], (8, kw)), wm_ref[...])
        for j, to in enumerate(chips):
            links.send(pc_all.at[k], pc_all.at[k], pc0 + j, to)
        for j, (px, py, pc) in enumerate(peers):
            links.arrived(smbuf.at[4 * px + 2 * py + pc], sm0 + j, (px, py, pc))
            links.arrived(wpbuf.at[4 * px + 2 * py + pc], wp0 + j, (px, py, pc))
        tot = smbuf[0]
        wpt = wpbuf[0]
        for d in range(1, 8):
            tot = tot + smbuf[d]
            wpt = wpt + wpbuf[d]
        for r, (ref, w) in enumerate(zip(small_outs, SMALL_OUT_WIDTHS)):
            ref[...] = tot[r:r + 1, 0:w]
        rwp_ref[...] = wpt
        for wi in range(nw):
            for j, (px, py, pc) in enumerate(chips):
                links.arrived(got[wi].at[2 * px + py], big + wi * 5 + 1 + j, (px, py, pc))
            total = got[wi][0].astype(F32)
            for kk in range(1, 4):
                total = total + got[wi][kk].astype(F32)
            mine = _half(r_refs[wi], cc, axes[wi])
            mine[...] = total
            links.send(mine, mine, big + wi * 5 + 4, sibling)
        for j, (px, py, pc) in enumerate(chips):
            links.arrived(pc_all.at[2 * px + py], pc0 + j, (px, py, pc))
        ccv = cc_ref[...]
        sg = _sig(ccv)
        gc_ref[...] = (pc_all[0][0:1, :] + pc_all[1][0:1, :] + pc_all[2][0:1, :] + pc_all[3][0:1, :]) * (sg * (1.0 + ccv * (1.0 - sg)))
        for wi in range(nw):
            links.arrived(_half(r_refs[wi], 1 - cc, axes[wi]), big + wi * 5 + 4, sibling)
        links.drain()

    nsem = 5 * nw + 24
    quads = [(4,) + h for h in halves]
    return pl.pallas_call(
        body, name="reduce", in_specs=[ANY] * nw + [VM] * (ns + 5), out_specs=[VM] * (nw + ns + 4),
        out_shape=[jax.ShapeDtypeStruct(g.shape[1:], F32) for g in grads]
        + [jax.ShapeDtypeStruct((1, w), F32) for w in SMALL_OUT_WIDTHS]
        + [jax.ShapeDtypeStruct(w_pool_g.shape, F32), jax.ShapeDtypeStruct((D, kw), F32), jax.ShapeDtypeStruct((1, 3 * D), F32),
           jax.ShapeDtypeStruct((1, D), F32)],
        scratch_shapes=[pltpu.VMEM(q, F32) for q in quads] + [pltpu.VMEM(q, F32) for q in quads]
        + [pltpu.VMEM(q, BF16) for q in quads] + [pltpu.VMEM(q, BF16) for q in quads]
        + [pltpu.VMEM((8, 8, D), F32), pltpu.VMEM((8,) + w_pool_g.shape, F32), pltpu.VMEM((8, 8, 3 * D), F32),
           pltpu.VMEM((4, 8, D), F32)]
        + [pltpu.SemaphoreType.DMA((nsem,)), pltpu.SemaphoreType.DMA((nsem,)), pltpu.SemaphoreType.DMA((nw,))],
        compiler_params=pltpu.CompilerParams(vmem_limit_bytes=VMEM_LIMIT),
    )(*grads, *smalls, w_pool_g, dmod8, a16, w_mod, c_ctx)


def _rope_tables(s_len):
    rows = s_len // GRID_W
    per = TB // GRID_W
    n_freq = 16
    inv = ROPE_BASE ** (-jnp.arange(n_freq, dtype=F32) / n_freq)
    ang_r = jnp.arange(rows, dtype=F32)[:, None] * inv
    ang_c = jnp.arange(GRID_W, dtype=F32)[:, None] * inv
    by_row, by_col = [], []
    for fn, pad in ((jnp.cos, 1.0), (jnp.sin, 0.0)):
        r = jnp.concatenate([fn(ang_r), fn(ang_r), jnp.zeros((rows, 96), F32)], axis=1).reshape(rows // per, per, 128)
        by_row.append(jnp.pad(r, ((0, 0), (0, 8 - per), (0, 0))))
        cpart = jnp.concatenate([jnp.zeros((GRID_W, 32), F32), fn(ang_c), fn(ang_c), jnp.full((GRID_W, 64), pad, F32)], axis=1)
        by_col.append(jnp.tile(cpart, (per, 1)))
    return jnp.concatenate(by_row, axis=-1), jnp.concatenate(by_col, axis=-1)


def kernel(x, c, ctx, c_ctx, w_mod, b_mod, norm_g, w_in, q_lora_g, w_uq, kv_lora_g, w_ukv, q_norm_g, k_norm_g, w_pool, pool_scale, w_out, loss_target, m_c_ctx, m_w_mod, m_b_mod, m_norm_g, m_w_in, m_q_lora_g, m_w_uq, m_kv_lora_g, m_w_ukv, m_q_norm_g, m_k_norm_g, m_w_pool, m_pool_scale, m_w_out, v_c_ctx, v_w_mod, v_b_mod, v_norm_g, v_w_in, v_q_lora_g, v_w_uq, v_kv_lora_g, v_w_ukv, v_q_norm_g, v_k_norm_g, v_w_pool, v_pool_scale, v_w_out):
    xi, yi, ci = lax.axis_index("x"), lax.axis_index("y"), lax.axis_index("c")
    me = 4 * xi + 2 * yi + ci
    k = 2 * xi + yi
    s_len = x.shape[1]
    lc = ctx.shape[1]
    kw = w_mod.shape[2]
    weights = dict(c_ctx=c_ctx, w_mod=w_mod, b_mod=b_mod, norm_g=norm_g, w_in=w_in, q_lora_g=q_lora_g, w_uq=w_uq,
                   kv_lora_g=kv_lora_g, w_ukv=w_ukv, q_norm_g=q_norm_g, k_norm_g=k_norm_g, w_pool=w_pool,
                   pool_scale=pool_scale, w_out=w_out)
    m_in = dict(c_ctx=m_c_ctx, w_mod=m_w_mod, b_mod=m_b_mod, norm_g=m_norm_g, w_in=m_w_in, q_lora_g=m_q_lora_g, w_uq=m_w_uq,
                kv_lora_g=m_kv_lora_g, w_ukv=m_w_ukv, q_norm_g=m_q_norm_g, k_norm_g=m_k_norm_g, w_pool=m_w_pool,
                pool_scale=m_pool_scale, w_out=m_w_out)
    v_in = dict(c_ctx=v_c_ctx, w_mod=v_w_mod, b_mod=v_b_mod, norm_g=v_norm_g, w_in=v_w_in, q_lora_g=v_q_lora_g, w_uq=v_w_uq,
                kv_lora_g=v_kv_lora_g, w_ukv=v_w_ukv, q_norm_g=v_q_norm_g, k_norm_g=v_k_norm_g, w_pool=v_w_pool,
                pool_scale=v_pool_scale, w_out=v_w_out)
    order = ["c_ctx", "w_mod", "b_mod", "norm_g", "w_in", "q_lora_g", "w_uq", "kv_lora_g", "w_ukv", "q_norm_g", "k_norm_g",
             "w_pool", "pool_scale", "w_out"]
    transposed = ("w_in", "w_uq")
    as2d = lambda n, a: jnp.transpose(a[0]) if n in transposed else a.reshape(-1, a.shape[-1])
    back = lambda n, a: jnp.transpose(a)[None] if n in transposed else a.reshape(weights[n].shape)

    c_ctx2 = c_ctx.reshape(1, D)
    b_mod_k = lax.dynamic_slice(b_mod, (0, k * kw), (1, kw))
    split = (1, 0, 0, 0)
    a16, mod_all, g_in, g_uq, g_ukv, g_out = _gather(
        c, c_ctx2, w_mod[0], b_mod_k,
        [as2d("w_in", w_in).astype(BF16), as2d("w_uq", w_uq).astype(BF16), w_ukv[0].astype(BF16), w_out[0].astype(BF16)], split)
    mod_me = lax.dynamic_index_in_dim(mod_all, me, axis=1, keepdims=False).reshape(3, D)
    mod_c = mod_all[:, 8, :].reshape(3, D)
    modsel = jnp.stack([mod_c, mod_me])
    w_in_t = g_in.reshape(DIN, D)
    w_uq_t = jnp.pad(g_uq, ((0, 0), (0, DKP - DK), (0, 0)))
    w_out_f = g_out.reshape(D, D)
    qn_g = jnp.pad(q_norm_g, ((0, 0), (0, DKP - DK)))
    kn_g = jnp.pad(k_norm_g, ((0, 0), (0, DKP - DK)))
    w_pool_b = w_pool[0].astype(BF16)
    cos, sin = _rope_tables(s_len)

    u, q, kk, v = _fwd_in(ctx[0], x[0], modsel, norm_g, w_in_t, q_lora_g, w_uq_t, kv_lora_g, g_ukv, qn_g, kn_g, cos, sin)
    attn, lse = _attn_fwd(q, kk, v, s_len)
    (dxn, dattn, dga, dgp, dpool, dw_out, dgate, dps, dw_pool, loss) = _out_stage(
        attn.reshape(s_len // Q_BLOCK, Q_BLOCK, NH * DV), u, x[0], loss_target[0], modsel[1, 2:3, :], w_pool_b, pool_scale,
        w_out_f, lc)
    dattn = dattn.reshape(s_len, NH * DV)
    dq, dk, dv = _attn_bwd(q, kk, v, dattn, attn, lse, s_len)
    dlo, dw_uq_t, dw_ukv, dqlg, dkvlg, dqng, dkng = _qkv_bwd(u, dq, dk, dv, cos, sin, q_lora_g, w_uq_t, kv_lora_g, g_ukv,
                                                            qn_g, kn_g, s_len)
    gx, dw_in_t, dmod, dng = _in_bwd(ctx[0], x[0], modsel, norm_g, dlo, dga, dgp, dpool, dxn, w_in_t)

    dmod_l = jnp.concatenate([dmod[1, 0], dmod[1, 1], dgate[0]]).reshape(1, 3 * D)
    dmod_c = jnp.concatenate([dmod[0, 0], dmod[0, 1], jnp.zeros((D,), F32)]).reshape(1, 3 * D)
    dmod8 = jnp.concatenate([dmod_l, dmod_c, jnp.zeros((6, 3 * D), F32)], axis=0)
    (r_in, r_uq, r_ukv, r_out, g_ng, g_qlg, g_kvlg, g_qng, g_kng, g_ps, loss_all, g_wp, g_w_mod, g_b_mod, g_c_ctx) = _reduce(
        [dw_in_t.reshape(4, DIN // 4, D), dw_uq_t, dw_ukv, dw_out.reshape(4, D // 4, D)], split,
        [dng, dqlg, dkvlg, dqng, dkng, dps, loss], dw_pool, dmod8, a16, w_mod[0], c_ctx2)
    g2d = dict(c_ctx=g_c_ctx, b_mod=g_b_mod, w_mod=g_w_mod, w_in=r_in, w_uq=r_uq, w_ukv=r_ukv, w_out=r_out, norm_g=g_ng,
               q_lora_g=g_qlg, kv_lora_g=g_kvlg, q_norm_g=g_qng, k_norm_g=g_kng, pool_scale=g_ps, w_pool=g_wp.reshape(512, 128))

    d2d, m2d, v2d = {}, {}, {}
    for n in ("w_mod", "w_in", "w_uq", "w_ukv", "w_out"):
        d2d[n], m2d[n], v2d[n] = _adamw(as2d(n, weights[n]), g2d[n], as2d(n, m_in[n]), as2d(n, v_in[n]), "adamw_" + n)
    small = [n for n in order if n not in d2d]
    outs = _adamw_small([as2d(n, weights[n]) for n in small], [g2d[n] for n in small], [as2d(n, m_in[n]) for n in small],
                        [as2d(n, v_in[n]) for n in small])
    for dst, arrs in zip((d2d, m2d, v2d), outs):
        dst.update(dict(zip(small, arrs)))

    return (loss_all[0, 0], gx[None], *[back(n, g2d[n]) for n in order], *[back(n, d2d[n]) for n in order],
            *[back(n, m2d[n]) for n in order], *[back(n, v2d[n]) for n in order])
```

```python
import jax
import jax.numpy as jnp
from jax import lax
from jax.experimental import pallas as pl
from jax.experimental.pallas import tpu as pltpu

F32 = jnp.float32
BF16 = jnp.bfloat16
MESH = pl.DeviceIdType.MESH

D = 1024
NH = 4
DK = 192
DKP = 256
DV = 128
QL = 256
KVL = 128
DIN = 1984
U_LO = 448
SEG = ((0, 512), (448, 960), (960, 1472), (1472, 1984))
DU = 2048
POOL_WINDOWS = (2, 4, 8, 16)
HALO = 8
EPS = 1e-6
ROPE_BASE = 10000.0
GRID_W = 64
Q_BLOCK = 128
TB = 256
BWD_QBLOCKS = 1
SCALE = DK ** -0.5
LOG2E = 1.4426950408889634
LN2 = 0.6931471805599453
VMEM_LIMIT = 56 * 1024 * 1024

ADAM_LR = 0.001
ADAM_B1 = 0.9
ADAM_B2 = 0.999
ADAM_EPS = 1e-08
ADAM_WD = 0.01
ADAM_STEP = 10

CHIPS3 = ((1, 0), (0, 1), (1, 1))
PEERS7 = tuple((dx, dy, dc) for dx in (0, 1) for dy in (0, 1) for dc in (0, 1) if (dx, dy, dc) != (0, 0, 0))

VM = pl.BlockSpec(memory_space=pltpu.VMEM)
ANY = pl.BlockSpec(memory_space=pl.ANY)


def _nn(a, b):
    return jnp.dot(a, b, preferred_element_type=F32)


def _nt(a, b):
    return lax.dot_general(a, b, (((1,), (1,)), ((), ())), preferred_element_type=F32)


def _tn(a, b):
    return lax.dot_general(a, b, (((0,), (0,)), ((), ())), preferred_element_type=F32)


def _split3(a):
    a0 = a.astype(BF16)
    r = a - a0.astype(F32)
    a1 = r.astype(BF16)
    a2 = (r - a1.astype(F32)).astype(BF16)
    return a0, a1, a2


def _dot3(dot, a, b):
    sa = _split3(a)
    sb = _split3(b)
    out = None
    for i in range(3):
        for j in range(3 - i):
            t = dot(sa[i], sb[j])
            out = t if out is None else out + t
    return out


def _sig(x):
    return 1.0 / (1.0 + jnp.exp(-x))


def _rot(t):
    lane = lax.broadcasted_iota(jnp.int32, t.shape, 1)
    return jnp.where((lane % 32) < 16, -pltpu.roll(t, 112, 1), pltpu.roll(t, 16, 1))


def _rope(t, cos, sin):
    return t * cos + _rot(t) * sin


def _rope_t(t, cos, sin):
    return t * cos - _rot(t * sin)


def _rope_block(rows_ref, cols_ref, is_ctx):
    lane = lax.broadcasted_iota(jnp.int32, (TB, 256), 1) % 128
    rows = jnp.concatenate([jnp.broadcast_to(rows_ref[0, r:r + 1, :], (GRID_W, 256)) for r in range(TB // GRID_W)], axis=0)
    cs = jnp.where(lane < 32, rows, cols_ref[...])
    return jnp.where(is_ctx, 1.0, cs[:, :128]), jnp.where(is_ctx, 0.0, cs[:, 128:])


def _shift_rows(z, k):
    n = z.shape[0]
    return pltpu.roll(z, (n - k) % n, 0)


def _colsum(a):
    return jnp.sum(a, axis=0, keepdims=True)


def _rowsum(a):
    return jnp.sum(a, axis=-1, keepdims=True)


def _bsum(a):
    ones = jnp.ones((a.shape[1], 128), BF16)
    hi = a.astype(BF16)
    lo = (a - hi.astype(F32)).astype(BF16)
    return _nn(hi, ones) + _nn(lo, ones)


def _wide(r):
    return jnp.concatenate([r, r], axis=1)


def _row_layout(col):
    return jnp.transpose(jnp.broadcast_to(col, (col.shape[0], 128)))[0:8, :]


def _params(sem=None):
    return pltpu.CompilerParams(dimension_semantics=sem, vmem_limit_bytes=VMEM_LIMIT)


def _full(shape):
    nd = len(shape)
    return pl.BlockSpec(shape, lambda *_: (0,) * nd)


def _peer(x, y, c, off):
    dx, dy, dc = off
    return ((x + dx) % 2, (y + dy) % 2, (c + dc) % 2)


def _token_specs(off):
    ctx = pl.BlockSpec((TB, D), lambda i: (jnp.minimum(i, off - 1), 0))
    lat = pl.BlockSpec((TB, D), lambda i: (jnp.maximum(i - off, 0), 0))
    mod = pl.BlockSpec((1, 3, D), lambda i: (jnp.minimum(i // off, 1), 0, 0))
    return ctx, lat, mod


def _modulated(x, mod_ref, ng):
    shift = mod_ref[0, 0:1, :]
    scale = mod_ref[0, 1:2, :]
    r = lax.rsqrt(jnp.mean(x * x, axis=-1, keepdims=True) + EPS)
    xh = x * r
    xg = xh * ng
    return r, xh, xg, xg * (1.0 + scale) + shift, scale


def _fwd_in(ctx, x, modsel, norm_g, w_in_t, q_lora_g, w_uq_t, kv_lora_g, w_ukv, qn_g, kn_g, cos, sin):
    s_len, lc = x.shape[0], ctx.shape[0]
    t_all = s_len + lc
    nb = t_all // TB
    off = lc // TB

    def body(ctx_ref, x_ref, mod_ref, ng_ref, win_ref, qlg_ref, wuq_ref, kvlg_ref, wukv_ref, qng_ref, kng_ref, cos_ref, sin_ref,
             u_ref, q_ref, k_ref, v_ref):
        is_ctx = pl.program_id(0) < off
        xb = jnp.where(is_ctx, ctx_ref[...], x_ref[...])
        _, _, _, h, _ = _modulated(xb, mod_ref, ng_ref[...])
        hb = h.astype(BF16)
        lane = lax.broadcasted_iota(jnp.int32, (TB, 512), 1)
        ulo = jnp.where(lane < U_LO, _nt(hb, win_ref[SEG[0][0]:SEG[0][1], :]), 0.0)
        u_ref[:, 0:512] = ulo

        def project(j):
            u_ref[:, j * 512:(j + 1) * 512] = _nt(hb, win_ref[SEG[j][0]:SEG[j][1], :])

        cos, sin = _rope_block(cos_ref, sin_ref, is_ctx)
        cq = ulo[:, 0:QL]
        cqn = (cq * _wide(lax.rsqrt(_bsum(cq * cq) / QL + EPS)) * qlg_ref[...]).astype(BF16)
        qng = qng_ref[...]
        for hd in range(NH):
            if hd < 3:
                project(hd + 1)
            qh = _nt(cqn, wuq_ref[hd])
            qn = qh * _wide(lax.rsqrt(_bsum(qh * qh) / DK + EPS)) * qng
            q_ref[hd] = (jnp.concatenate([qn[:, :128], _rope(qn[:, 128:], cos, sin)], axis=1) * (SCALE * LOG2E)).astype(BF16)
        ckv = ulo[:, QL:QL + KVL]
        ckvn = (ckv * lax.rsqrt(_bsum(ckv * ckv) / KVL + EPS) * kvlg_ref[...]).astype(BF16)
        kr = ulo[:, 384:512]
        skr = _bsum(kr * kr)
        kng = kng_ref[...]
        for hd in range(NH):
            kv = _nn(ckvn, wukv_ref[hd])
            kn = kv[:, :128]
            rk = lax.rsqrt((_bsum(kn * kn) + skr) / DK + EPS)
            k_ref[hd] = jnp.concatenate([kn * rk * kng[:, :128], _rope(kr * rk * kng[:, 128:], cos, sin)], axis=1).astype(BF16)
            v_ref[hd] = kv[:, 128:].astype(BF16)

    row = lambda w: pl.BlockSpec((TB, w), lambda i: (i, 0))
    heads = lambda w: pl.BlockSpec((NH, TB, w), lambda i: (0, i, 0))
    cspec, xspec, mspec = _token_specs(off)
    return pl.pallas_call(
        body, name="fwd_in", grid=(nb,),
        in_specs=[cspec, xspec, mspec, _full((1, D)), _full((DIN, D)), _full((1, QL)), _full((NH, DKP, QL)), _full((1, KVL)),
                  _full((NH, KVL, 256)), _full((1, DKP)), _full((1, DKP)),
                  pl.BlockSpec((1, 8, 256), lambda i: (jnp.maximum(i - off, 0), 0, 0)), _full((TB, 256))],
        out_specs=[row(DU), heads(DKP), heads(DKP), heads(DV)],
        out_shape=[jax.ShapeDtypeStruct((t_all, DU), F32), jax.ShapeDtypeStruct((NH, t_all, DKP), BF16),
                   jax.ShapeDtypeStruct((NH, t_all, DKP), BF16), jax.ShapeDtypeStruct((NH, t_all, DV), BF16)],
        compiler_params=_params(("arbitrary",)),
    )(ctx, x, modsel, norm_g, w_in_t, q_lora_g, w_uq_t, kv_lora_g, w_ukv, qn_g, kn_g, cos, sin)


def _attn_fwd(q, k, v, s_len):
    t_all = q.shape[1]
    off = (t_all - s_len) // TB
    nq = s_len // TB
    nsub = next(n for n in (4, 2, 1) if nq % n == 0)

    def body(*refs):
        q_refs = refs[:nsub]
        k_ref, v_ref, o_ref, lse_ref = refs[nsub:]
        for sb in range(nsub):
            s = _nt(q_refs[sb][0], k_ref[0])
            m = jnp.max(s, axis=-1, keepdims=True)
            e = jnp.exp2(s - m)
            l = _rowsum(e)
            o_ref[sb * TB:(sb + 1) * TB, :] = _nn(e.astype(BF16), v_ref[0]) / l
            lse_ref[0, sb] = _row_layout(m + jnp.log2(l))

    qspec = lambda sb: pl.BlockSpec((1, TB, DKP), lambda h, i: (h, i * nsub + sb + off, 0))
    return pl.pallas_call(
        body, name="attn_fwd", grid=(NH, nq // nsub),
        in_specs=[qspec(sb) for sb in range(nsub)]
        + [pl.BlockSpec((1, t_all, DKP), lambda h, i: (h, 0, 0)), pl.BlockSpec((1, t_all, DV), lambda h, i: (h, 0, 0))],
        out_specs=[pl.BlockSpec((nsub * TB, DV), lambda h, i: (i, h)), pl.BlockSpec((1, nsub, 8, TB), lambda h, i: (h, i, 0, 0))],
        out_shape=[jax.ShapeDtypeStruct((s_len, NH * DV), F32), jax.ShapeDtypeStruct((NH, nq, 8, TB), F32)],
        compiler_params=_params(("arbitrary", "arbitrary")),
    )(*([q] * nsub), k, v)


def _out_stage(attn, u, x, target, gate, w_pool, pool_scale, w_out, lc):
    s_len = x.shape[0]
    t_all = s_len + lc
    off = lc // TB
    nq = s_len // TB
    hb = TB // HALO
    nqb = s_len // Q_BLOCK
    jb = TB // nqb

    def body(attn_ref, ga_ref, pin_ref, pprev_ref, pnext_ref, gp_ref, x_ref, tgt_ref, gate_ref, wp_ref, ps_ref, wo_ref,
             dxn_ref, dattn_ref, dga_ref, dgp_ref, dpool_ref, dwo_ref, dgate_ref, dps_ref, dwp_ref, loss_ref):
        i = pl.program_id(0)

        @pl.when(i == 0)
        def _():
            dwo_ref[...] = jnp.zeros_like(dwo_ref)
            dgate_ref[...] = jnp.zeros_like(dgate_ref)
            dps_ref[...] = jnp.zeros_like(dps_ref)
            dwp_ref[...] = jnp.zeros_like(dwp_ref)
            loss_ref[...] = jnp.zeros_like(loss_ref)

        attn = jnp.concatenate([attn_ref[:, jj, :] for jj in range(jb)], axis=0)
        ga = ga_ref[...]
        gp = gp_ref[...]
        pin = pin_ref[...]
        prev = jnp.where(i == 0, 0.0, pprev_ref[...])
        nxt = jnp.where(i == nq - 1, 0.0, pnext_ref[...])
        win = jnp.concatenate([prev, pin, nxt], axis=0)
        tg = i * TB + lax.broadcasted_iota(jnp.int32, (TB, 1), 0)
        pooled = []
        for g, w in enumerate(POOL_WINDOWS):
            a = win[:, g * 128:(g + 1) * 128]
            p = _shift_rows(a, -1) + a
            for step in (1, 2, 4):
                if w >= 4 * step:
                    p = _shift_rows(p, -step) + _shift_rows(p, step)
            cnt = (jnp.minimum(tg + w // 2, s_len) - jnp.maximum(tg - w // 2, 0)).astype(F32)
            pooled.append(p[HALO:HALO + TB] / cnt - a[HALO:HALO + TB])
        pooled_b = [p.astype(BF16) for p in pooled]
        z = jnp.concatenate([_nn(pooled_b[g], wp_ref[g]) for g in range(4)], axis=1)
        ps = ps_ref[...]
        yp = z * ps
        sga = _sig(ga)
        sila = ga * sga
        sgp = _sig(gp)
        silp = gp * sgp
        br = jnp.concatenate([sila * attn, silp * yp], axis=1).astype(BF16)
        y = _nn(br, wo_ref[...])
        gate = gate_ref[...]
        err = x_ref[...] + gate * y - tgt_ref[...]
        loss_ref[...] += _colsum(_rowsum(err * err)) * (0.5 / D)
        dxn = err * (1.0 / D)
        dxn_ref[...] = dxn
        dgate_ref[...] += _colsum(dxn * y)
        dy = (dxn * gate).astype(BF16)
        dwo_ref[...] += _tn(br, dy)
        dbr = _nt(dy, wo_ref[...])
        dbra = dbr[:, :512]
        dbrp = dbr[:, 512:]
        dattn = dbra * sila
        for jj in range(jb):
            dattn_ref[:, jj, :] = dattn[jj * nqb:(jj + 1) * nqb]
        dga_ref[...] = (dbra * attn * (sga * (1.0 + ga * (1.0 - sga)))).astype(BF16)
        dgp_ref[...] = (dbrp * yp * (sgp * (1.0 + gp * (1.0 - sgp)))).astype(BF16)
        dyp = dbrp * silp
        dps_ref[...] += _colsum(dyp * z)
        dz = (dyp * ps).astype(BF16)
        dpool = []
        for g in range(4):
            dzg = dz[:, g * 128:(g + 1) * 128]
            dwp_ref[g] += _tn(pooled_b[g], dzg)
            dpool.append(_nt(dzg, wp_ref[g]))
        dpool_ref[...] = jnp.concatenate(dpool, axis=1)

    lat = lambda w: pl.BlockSpec((TB, w), lambda i: (i, 0))
    perm = pl.BlockSpec((nqb, jb, 512), lambda i: (0, i, 0))
    ucol = lambda j: pl.BlockSpec((TB, 512), lambda i: (i + off, j))
    last8 = t_all // HALO - 1
    return pl.pallas_call(
        body, name="out_stage", grid=(nq,),
        in_specs=[perm, ucol(1), ucol(2),
                  pl.BlockSpec((HALO, 512), lambda i: ((i + off) * hb - 1, 2)),
                  pl.BlockSpec((HALO, 512), lambda i: (jnp.minimum((i + off + 1) * hb, last8), 2)),
                  ucol(3), lat(D), lat(D), _full((1, D)), _full((4, 128, 128)), _full((1, 512)), _full((D, D))],
        out_specs=[lat(D), perm, lat(512), lat(512), lat(512),
                   _full((D, D)), _full((1, D)), _full((1, 512)), _full((4, 128, 128)), _full((1, 1))],
        out_shape=[jax.ShapeDtypeStruct((s_len, D), F32), jax.ShapeDtypeStruct((nqb, Q_BLOCK, 512), F32),
                   jax.ShapeDtypeStruct((s_len, 512), BF16), jax.ShapeDtypeStruct((s_len, 512), BF16),
                   jax.ShapeDtypeStruct((s_len, 512), F32),
                   jax.ShapeDtypeStruct((D, D), F32), jax.ShapeDtypeStruct((1, D), F32), jax.ShapeDtypeStruct((1, 512), F32),
                   jax.ShapeDtypeStruct((4, 128, 128), F32), jax.ShapeDtypeStruct((1, 1), F32)],
        compiler_params=_params(("arbitrary",)),
    )(attn, u, u, u, u, u, x, target, gate, w_pool, pool_scale, w_out)


def _attn_bwd(q, k, v, dattn, attn, lse, s_len):
    t_all = q.shape[1]
    off = (t_all - s_len) // TB
    nq = s_len // TB
    nch = 4
    ch = t_all // nch
    nsub = next(n for n in (BWD_QBLOCKS, 2, 1) if nq % n == 0)
    tq = nsub * TB

    def body(*refs):
        q_refs = refs[:nsub]
        k_ref, v_ref, do_ref, o_ref, lse_ref, dq_ref, dk_ref, dv_ref = refs[nsub:]
        i = pl.program_id(1)

        @pl.when(i == 0)
        def _():
            dk_ref[...] = jnp.zeros_like(dk_ref)
            dv_ref[...] = jnp.zeros_like(dv_ref)

        qb = jnp.concatenate([r[0] for r in q_refs], axis=0)
        delta_r = _row_layout(_rowsum(do_ref[...] * o_ref[...]))[0:1, :]
        do = do_ref[...].astype(BF16)
        lse_r = jnp.concatenate([lse_ref[0, sb][0:1, :] for sb in range(nsub)], axis=1)
        dq = jnp.zeros((tq, DKP), F32)
        for c in range(nch):
            rows = pl.ds(c * ch, ch)
            kc = k_ref[0, rows, :]
            p_t = jnp.exp2(_nt(kc, qb) - lse_r)
            ds_t = (p_t * (_nt(v_ref[0, rows, :], do) - delta_r)).astype(BF16)
            dv_ref[0, rows, :] += _nn(p_t.astype(BF16), do)
            dk_ref[0, rows, :] += _nn(ds_t, qb)
            dq += _tn(ds_t, kc)
        dq_ref[0] = dq * SCALE

    kvspec = lambda w: pl.BlockSpec((1, t_all, w), lambda h, i: (h, 0, 0))
    rowspec = pl.BlockSpec((1, nsub, 8, TB), lambda h, i: (h, i, 0, 0))
    qspec = lambda sb: pl.BlockSpec((1, TB, DKP), lambda h, i: (h, i * nsub + sb + off, 0))
    return pl.pallas_call(
        body, name="attn_bwd", grid=(NH, nq // nsub),
        in_specs=[qspec(sb) for sb in range(nsub)]
        + [kvspec(DKP), kvspec(DV), pl.BlockSpec((tq, DV), lambda h, i: (i, h)), pl.BlockSpec((tq, DV), lambda h, i: (i, h)),
           rowspec],
        out_specs=[pl.BlockSpec((1, tq, DKP), lambda h, i: (h, i, 0)), kvspec(DKP), kvspec(DV)],
        out_shape=[jax.ShapeDtypeStruct((NH, s_len, DKP), F32), jax.ShapeDtypeStruct((NH, t_all, DKP), F32),
                   jax.ShapeDtypeStruct((NH, t_all, DV), F32)],
        compiler_params=_params(("arbitrary", "arbitrary")),
    )(*([q] * nsub), k, v, dattn, attn, lse)


def _qkv_bwd(u, dq, dk, dv, cos, sin, q_lora_g, w_uq_t, kv_lora_g, w_ukv, qn_g, kn_g, s_len):
    t_all = u.shape[0]
    off = (t_all - s_len) // TB
    nb = t_all // TB

    def body(ulo_ref, dq_ref, dk_ref, dv_ref, cos_ref, sin_ref, qlg_ref, wuq_ref, kvlg_ref, wukv_ref, qng_ref, kng_ref,
             dlo_ref, dwuq_ref, dwukv_ref, dqlg_ref, dkvlg_ref, dqng_ref, dkng_ref):
        i = pl.program_id(0)

        @pl.when(i == 0)
        def _():
            for r in (dwuq_ref, dwukv_ref, dqlg_ref, dkvlg_ref, dqng_ref, dkng_ref):
                r[...] = jnp.zeros_like(r)

        latent = i >= off
        ulo = ulo_ref[...]
        cos, sin = _rope_block(cos_ref, sin_ref, pl.program_id(0) < off)
        cq = ulo[:, 0:QL]
        rc = lax.rsqrt(jnp.mean(cq * cq, axis=-1, keepdims=True) + EPS)
        cqh = cq * rc
        qlg = qlg_ref[...]
        cqn_b = (cqh * qlg).astype(BF16)
        qng = qng_ref[...]
        dqng = jnp.zeros((1, DKP), F32)
        dcqn = jnp.zeros((TB, QL), F32)
        for hd in range(NH):
            qh = _nt(cqn_b, wuq_ref[hd])
            rq = lax.rsqrt(_rowsum(qh * qh) / DK + EPS)
            xh = qh * rq
            dqh = jnp.where(latent, dq_ref[hd], 0.0)
            dyq = jnp.concatenate([dqh[:, :128], _rope_t(dqh[:, 128:], cos, sin)], axis=1)
            dqng += _colsum(dyq * xh)
            dxh = dyq * qng
            dqraw = (rq * (dxh - xh * (_rowsum(dxh * xh) / DK))).astype(BF16)
            dwuq_ref[hd] += _tn(dqraw, cqn_b)[:DK]
            dcqn += _nn(dqraw, wuq_ref[hd])
        dqng_ref[...] += dqng
        dqlg_ref[...] += _colsum(dcqn * cqh)
        dxh = dcqn * qlg
        dcq = rc * (dxh - cqh * jnp.mean(dxh * cqh, axis=-1, keepdims=True))

        ckv = ulo[:, QL:QL + KVL]
        r0 = lax.rsqrt(jnp.mean(ckv * ckv, axis=-1, keepdims=True) + EPS)
        ckvh = ckv * r0
        kvlg = kvlg_ref[...]
        ckvn_b = (ckvh * kvlg).astype(BF16)
        kr = ulo[:, 384:512]
        skr = _rowsum(kr * kr)
        kng = kng_ref[...]
        dkr = jnp.zeros((TB, 128), F32)
        dkng = jnp.zeros((1, DKP), F32)
        dckvn = jnp.zeros((TB, KVL), F32)
        for hd in range(NH):
            kn = _nn(ckvn_b, wukv_ref[hd])[:, :128]
            rk = lax.rsqrt((_rowsum(kn * kn) + skr) / DK + EPS)
            xh1 = kn * rk
            xh2 = kr * rk
            dkh = dk_ref[hd] * LN2
            d1 = dkh[:, :128]
            d2 = _rope_t(dkh[:, 128:], cos, sin)
            dkng += jnp.concatenate([_colsum(d1 * xh1), _colsum(d2 * xh2)], axis=1)
            dx1 = d1 * kng[:, :128]
            dx2 = d2 * kng[:, 128:]
            dot = _rowsum(dx1 * xh1 + dx2 * xh2) / DK
            dkv = jnp.concatenate([rk * (dx1 - xh1 * dot), dv_ref[hd]], axis=1).astype(BF16)
            dkr += rk * (dx2 - xh2 * dot)
            dwukv_ref[hd] += _tn(ckvn_b, dkv)
            dckvn += _nt(dkv, wukv_ref[hd])
        dkng_ref[...] += dkng
        dkvlg_ref[...] += _colsum(dckvn * ckvh)
        dxh = dckvn * kvlg
        dckv = r0 * (dxh - ckvh * jnp.mean(dxh * ckvh, axis=-1, keepdims=True))
        dlo_ref[...] = jnp.concatenate([dcq, dckv, dkr], axis=1).astype(BF16)

    row = lambda w: pl.BlockSpec((TB, w), lambda i: (i, 0))
    heads = lambda w: pl.BlockSpec((NH, TB, w), lambda i: (0, i, 0))
    return pl.pallas_call(
        body, name="qkv_bwd", grid=(nb,),
        in_specs=[row(512), pl.BlockSpec((NH, TB, DKP), lambda i: (0, jnp.maximum(i - off, 0), 0)), heads(DKP), heads(DV),
                  pl.BlockSpec((1, 8, 256), lambda i: (jnp.maximum(i - off, 0), 0, 0)), _full((TB, 256)), _full((1, QL)), _full((NH, DKP, QL)), _full((1, KVL)), _full((NH, KVL, 256)),
                  _full((1, DKP)), _full((1, DKP))],
        out_specs=[row(512), _full((NH, DK, QL)), _full((NH, KVL, 256)), _full((1, QL)), _full((1, KVL)),
                   _full((1, DKP)), _full((1, DKP))],
        out_shape=[jax.ShapeDtypeStruct((t_all, 512), BF16), jax.ShapeDtypeStruct((NH, DK, QL), F32),
                   jax.ShapeDtypeStruct((NH, KVL, 256), F32), jax.ShapeDtypeStruct((1, QL), F32),
                   jax.ShapeDtypeStruct((1, KVL), F32), jax.ShapeDtypeStruct((1, DKP), F32), jax.ShapeDtypeStruct((1, DKP), F32)],
        compiler_params=_params(("arbitrary",)),
    )(u, dq, dk, dv, cos, sin, q_lora_g, w_uq_t, kv_lora_g, w_ukv, qn_g, kn_g)


def _in_bwd(ctx, x, modsel, norm_g, dlo, dga, dgp, dpool, dxn, w_in_t):
    s_len, lc = x.shape[0], ctx.shape[0]
    t_all = s_len + lc
    off = lc // TB
    nb = t_all // TB
    nq = s_len // TB
    hb = TB // HALO
    n = TB + 2 * HALO

    def body(ctx_ref, x_ref, mod_ref, ng_ref, dlo_ref, dga_ref, dgp_ref, dp_ref, dpprev_ref, dpnext_ref, dxn_ref, win_ref,
             gx_ref, dwin_ref, dmod_ref, dng_ref):
        i = pl.program_id(0)
        j = i - off

        @pl.when(i == 0)
        def _():
            dwin_ref[...] = jnp.zeros_like(dwin_ref)
            dmod_ref[...] = jnp.zeros_like(dmod_ref)
            dng_ref[...] = jnp.zeros_like(dng_ref)

        latent = i >= off
        dp = dp_ref[...]
        prev = jnp.where(j <= 0, 0.0, dpprev_ref[...])
        nxt = jnp.where(j >= nq - 1, 0.0, dpnext_ref[...])
        win = jnp.concatenate([prev, dp, nxt], axis=0)
        tg = j * TB - HALO + lax.broadcasted_iota(jnp.int32, (n, 1), 0)
        dpin = []
        for g, w in enumerate(POOL_WINDOWS):
            cnt = jnp.maximum(jnp.minimum(tg + w // 2, s_len) - jnp.maximum(tg - w // 2, 0), 1).astype(F32)
            zq = win[:, g * 128:(g + 1) * 128] / cnt
            zq = zq + _shift_rows(zq, 1)
            for step in (1, 2, 4):
                if w >= 4 * step:
                    zq = _shift_rows(zq, -step) + _shift_rows(zq, step)
            dpin.append(zq[HALO:HALO + TB] - dp[:, g * 128:(g + 1) * 128])
        zero = jnp.zeros((TB, 512), BF16)
        du = [dlo_ref[...], jnp.where(latent, dga_ref[...], zero),
              jnp.where(latent, jnp.concatenate(dpin, axis=1).astype(BF16), zero), jnp.where(latent, dgp_ref[...], zero)]

        ng = ng_ref[...]
        xb = jnp.where(i < off, ctx_ref[...], x_ref[...])
        r, xh, xg, h, scale = _modulated(xb, mod_ref, ng)
        hb_ = h.astype(BF16)
        dh = jnp.zeros((TB, D), F32)
        for s, (lo, hi) in enumerate(SEG):
            dwin_ref[lo:hi, :] += _tn(du[s], hb_)
            dh += _nn(du[s], win_ref[lo:hi, :])
        is_lat = latent.astype(F32)
        dsh = _colsum(dh)
        dsc = _colsum(dh * xg)
        dmod_ref[0, 0:1, :] += dsh * (1.0 - is_lat)
        dmod_ref[0, 1:2, :] += dsc * (1.0 - is_lat)
        dmod_ref[1, 0:1, :] += dsh * is_lat
        dmod_ref[1, 1:2, :] += dsc * is_lat
        dxg = dh * (1.0 + scale)
        dng_ref[...] += _colsum(dxg * xh)
        dxh = dxg * ng
        gx_ref[...] = r * (dxh - xh * jnp.mean(dxh * xh, axis=-1, keepdims=True)) + dxn_ref[...]

    row = lambda w: pl.BlockSpec((TB, w), lambda i: (i, 0))
    lat = lambda w: pl.BlockSpec((TB, w), lambda i: (jnp.maximum(i - off, 0), 0))
    last8 = s_len // HALO - 1
    cspec, xspec, mspec = _token_specs(off)
    return pl.pallas_call(
        body, name="in_bwd", grid=(nb,),
        in_specs=[cspec, xspec, mspec, _full((1, D)), row(512), lat(512), lat(512), lat(512),
                  pl.BlockSpec((HALO, 512), lambda i: (jnp.maximum(jnp.maximum(i - off, 0) * hb - 1, 0), 0)),
                  pl.BlockSpec((HALO, 512), lambda i: (jnp.minimum((jnp.maximum(i - off, 0) + 1) * hb, last8), 0)),
                  lat(D), _full((DIN, D))],
        out_specs=[lat(D), _full((DIN, D)), _full((2, 2, D)), _full((1, D))],
        out_shape=[jax.ShapeDtypeStruct((s_len, D), F32), jax.ShapeDtypeStruct((DIN, D), F32),
                   jax.ShapeDtypeStruct((2, 2, D), F32), jax.ShapeDtypeStruct((1, D), F32)],
        compiler_params=_params(("arbitrary",)),
    )(ctx, x, modsel, norm_g, dlo, dga, dgp, dpool, dpool, dpool, dxn, w_in_t)


def _adamw_update(w_ref, g_ref, m_ref, v_ref, d_ref, mo_ref, vo_ref):
    gv = g_ref[...]
    mn = ADAM_B1 * m_ref[...] + (1.0 - ADAM_B1) * gv
    vn = ADAM_B2 * v_ref[...] + (1.0 - ADAM_B2) * (gv * gv)
    m_hat = mn / (1.0 - ADAM_B1 ** ADAM_STEP)
    v_hat = vn / (1.0 - ADAM_B2 ** ADAM_STEP)
    d_ref[...] = -ADAM_LR * (m_hat / (jnp.sqrt(v_hat) + ADAM_EPS) + ADAM_WD * w_ref[...])
    mo_ref[...] = mn
    vo_ref[...] = vn


def _adamw_small(ws, gs, ms, vs):
    n = len(ws)

    def body(*refs):
        for i in range(n):
            _adamw_update(refs[i], refs[n + i], refs[2 * n + i], refs[3 * n + i], refs[4 * n + i], refs[5 * n + i], refs[6 * n + i])

    shp = [jax.ShapeDtypeStruct(w.shape, F32) for w in ws]
    out = pl.pallas_call(body, name="adamw_small", in_specs=[VM] * (4 * n), out_specs=[VM] * (3 * n), out_shape=shp * 3,
                         compiler_params=pltpu.CompilerParams(vmem_limit_bytes=VMEM_LIMIT))(*ws, *gs, *ms, *vs)
    return out[:n], out[n:2 * n], out[2 * n:]


def _adamw(w, g, m, v, name):
    rows, cols = w.shape
    rb = next(r for r in range(min(rows, 256), 0, -8) if rows % r == 0)

    def body(w_ref, g_ref, m_ref, v_ref, d_ref, mo_ref, vo_ref):
        _adamw_update(w_ref, g_ref, m_ref, v_ref, d_ref, mo_ref, vo_ref)

    spec = pl.BlockSpec((rb, cols), lambda i: (i, 0))
    shp = jax.ShapeDtypeStruct((rows, cols), F32)
    return pl.pallas_call(
        body, name=name, grid=(rows // rb,), in_specs=[spec] * 4, out_specs=[spec] * 3, out_shape=[shp] * 3,
        compiler_params=_params(("arbitrary",)),
    )(w, g, m, v)


class _Links:
    def __init__(self, send_sems, recv_sems):
        self.send_sems, self.recv_sems, self.sends = send_sems, recv_sems, []

    def send(self, src, dst, sem, to):
        cp = pltpu.make_async_remote_copy(src, dst, self.send_sems.at[sem], self.recv_sems.at[sem], device_id=to,
                                          device_id_type=MESH)
        cp.start()
        self.sends.append(cp)

    def arrived(self, dst, sem, frm):
        pltpu.make_async_remote_copy(dst, dst, self.send_sems.at[sem], self.recv_sems.at[sem], device_id=frm,
                                     device_id_type=MESH).wait_recv()

    def drain(self):
        for cp in self.sends:
            cp.wait_send()


def _half(ref, c, axis):
    size = ref.shape[axis - 2] // 2
    win = pl.ds(pl.multiple_of(c * size, 16 if axis == 0 else 128), size)
    idx = (win, slice(None)) if axis == 0 else (slice(None), win)
    return ref.at[(slice(None),) * (len(ref.shape) - 2) + idx]


def _select_rows(slots_ref, n_slots, row=0):
    sub = lax.broadcasted_iota(jnp.int32, (8, 1), 0)
    out = None
    for d in range(n_slots):
        r = jnp.where(sub == d, jnp.broadcast_to(slots_ref[d][row:row + 1, :], (8, slots_ref.shape[-1])), 0.0)
        out = r if out is None else out + r
    return out


def _gather(c, c_ctx, w_mod, b_mod_k, shards, axes):
    nw = len(shards)
    kw = w_mod.shape[1]

    def body(*refs):
        c_ref, cc_ref, wm_ref, b_ref = refs[:4]
        w_refs = refs[4:4 + nw]
        a16_ref, mod_ref = refs[4 + nw:6 + nw]
        g_refs = refs[6 + nw:6 + 2 * nw]
        a_ref, send_sems, recv_sems, local_sems = refs[6 + 2 * nw:]
        x, y, cc = lax.axis_index("x"), lax.axis_index("y"), lax.axis_index("c")
        me = 4 * x + 2 * y + cc
        k = 2 * x + y
        sibling = (x, y, 1 - cc)
        links = _Links(send_sems, recv_sems)
        chips = [_peer(x, y, cc, off + (0,)) for off in CHIPS3]
        locals_ = []
        for wi in range(nw):
            lc = pltpu.make_async_copy(w_refs[wi], g_refs[wi].at[k], local_sems.at[wi])
            lc.start()
            locals_.append(lc)
            for j, to in enumerate(chips):
                links.send(_half(w_refs[wi], cc, axes[wi]), _half(g_refs[wi].at[k], cc, axes[wi]), 10 + wi * 6 + j, to)
        cv = c_ref[...]
        a_ref[me] = jnp.broadcast_to(cv * _sig(cv), (8, D))
        for j, off in enumerate(PEERS7):
            links.send(a_ref.at[me], a_ref.at[me], j, _peer(x, y, cc, off))
        for j, off in enumerate(PEERS7):
            px, py, pc = _peer(x, y, cc, off)
            links.arrived(a_ref.at[4 * px + 2 * py + pc], j, (px, py, pc))
        ccv = cc_ref[...]
        sub = lax.broadcasted_iota(jnp.int32, (8, 1), 0)
        a16 = jnp.concatenate([_select_rows(a_ref, 8), jnp.where(sub == 0, jnp.broadcast_to(ccv * _sig(ccv), (8, D)), 0.0)], axis=0)
        a16_ref[...] = a16
        mod_ref[k] = _dot3(_nn, a16, wm_ref[...]) + b_ref[...]
        for j, to in enumerate(chips):
            links.send(mod_ref.at[k], mod_ref.at[k], 7 + j, to)
        for wi in range(nw):
            for j, (px, py, pc) in enumerate(chips):
                blk = _half(g_refs[wi].at[2 * px + py], cc, axes[wi])
                links.arrived(blk, 10 + wi * 6 + j, (px, py, pc))
                links.send(blk, blk, 10 + wi * 6 + 3 + j, sibling)
        for j, (px, py, pc) in enumerate(chips):
            links.arrived(mod_ref.at[2 * px + py], 7 + j, (px, py, pc))
        for wi in range(nw):
            for j, (px, py, pc) in enumerate(chips):
                links.arrived(_half(g_refs[wi].at[2 * px + py], 1 - cc, axes[wi]), 10 + wi * 6 + 3 + j, sibling)
        links.drain()
        for lc in locals_:
            lc.wait()

    nsem = 10 + 6 * nw
    return pl.pallas_call(
        body, name="gather", in_specs=[VM] * 4 + [ANY] * nw, out_specs=[VM, VM] + [ANY] * nw,
        out_shape=[jax.ShapeDtypeStruct((16, D), F32), jax.ShapeDtypeStruct((4, 16, kw), F32)]
        + [jax.ShapeDtypeStruct((4,) + s.shape, s.dtype) for s in shards],
        scratch_shapes=[pltpu.VMEM((8, 8, D), F32), pltpu.SemaphoreType.DMA((nsem,)), pltpu.SemaphoreType.DMA((nsem,)),
                        pltpu.SemaphoreType.DMA((nw,))],
        compiler_params=pltpu.CompilerParams(vmem_limit_bytes=VMEM_LIMIT),
    )(c, c_ctx, w_mod, b_mod_k, *shards)


SMALL_ROW_WIDTHS = (D, QL, KVL, DKP, DKP, 512, 128)
SMALL_OUT_WIDTHS = (D, QL, KVL, DK, DK, 512, 1)


def _reduce(grads, axes, smalls, w_pool_g, dmod8, a16, w_mod, c_ctx):
    nw = len(grads)
    ns = len(smalls)
    kw = w_mod.shape[1]
    halves = []
    for g, ax in zip(grads, axes):
        halves.append((g.shape[1] // 2, g.shape[2]) if ax == 0 else (g.shape[1], g.shape[2] // 2))

    def body(*refs):
        g_refs = refs[:nw]
        small_refs = refs[nw:nw + ns]
        wp_ref, dm_ref, a16_ref, wm_ref, cc_ref = refs[nw + ns:nw + ns + 5]
        o = nw + ns + 5
        r_refs = refs[o:o + nw]
        small_outs = refs[o + nw:o + nw + ns]
        rwp_ref, gw_ref, gb_ref, gc_ref = refs[o + nw + ns:o + nw + ns + 4]
        o = o + nw + ns + 4
        own, sib, part, got = (refs[o + i * nw:o + (i + 1) * nw] for i in range(4))
        smbuf, wps, wpg, dm_all, pc_all, send_sems, recv_sems, local_sems = refs[o + 4 * nw:]
        x, y, cc = lax.axis_index("x"), lax.axis_index("y"), lax.axis_index("c")
        me = 4 * x + 2 * y + cc
        k = 2 * x + y
        sibling = (x, y, 1 - cc)
        links = _Links(send_sems, recv_sems)
        chips = [_peer(x, y, cc, off + (0,)) for off in CHIPS3]
        peers = [_peer(x, y, cc, off) for off in PEERS7]
        big, sm0, wp0, dm0, pc0 = 0, 5 * nw, 5 * nw + 7, 5 * nw + 14, 5 * nw + 21

        locals_ = []
        for wi in range(nw):
            lc = pltpu.make_async_copy(_half(g_refs[wi], cc, axes[wi]), own[wi], local_sems.at[wi])
            lc.start()
            locals_.append(lc)
            links.send(_half(g_refs[wi], 1 - cc, axes[wi]), sib[wi], big + wi * 5, sibling)
        slot = smbuf.at[me]
        slot[...] = jnp.zeros((8, D), F32)
        for r, (ref, w) in enumerate(zip(small_refs, SMALL_ROW_WIDTHS)):
            slot[r:r + 1, 0:w] = jnp.broadcast_to(ref[...], (1, w))
        links.send(wp_ref, wps, wp0, sibling)
        dm_all[me] = dm_ref[...]
        for j, peer in enumerate(peers):
            links.send(dm_all.at[me], dm_all.at[me], dm0 + j, peer)
            links.send(smbuf.at[me], smbuf.at[me], sm0 + j, peer)
        links.arrived(wps, wp0, sibling)
        wpg[k] = (wp_ref[...] + wps[...]).astype(BF16)
        for j, to in enumerate(chips):
            links.send(wpg.at[k], wpg.at[k], wp0 + 1 + j, to)
        for wi in range(nw):
            locals_[wi].wait()
            links.arrived(sib[wi], big + wi * 5, sibling)
            part[wi][...] = (own[wi][...] + sib[wi][...]).astype(BF16)
            got[wi][k] = part[wi][k]
            for j, (px, py, pc) in enumerate(chips):
                links.send(part[wi].at[2 * px + py], got[wi].at[k], big + wi * 5 + 1 + j, (px, py, pc))
        for j, (px, py, pc) in enumerate(peers):
            links.arrived(dm_all.at[4 * px + 2 * py + pc], dm0 + j, (px, py, pc))
        dmc = dm_all[0][1:2, :]
        dml = dm_all[0][0:1, :]
        for d in range(1, 8):
            dmc = dmc + dm_all[d][1:2, :]
            dml = dml + dm_all[d][0:1, :]
        gb_ref[...] = dml + dmc
        sub = lax.broadcasted_iota(jnp.int32, (8, 1), 0)
        b16 = jnp.concatenate([_select_rows(dm_all, 8), jnp.where(sub == 0, jnp.broadcast_to(dmc, (8, 3 * D)), 0.0)], axis=0)
        bk = jnp.zeros((16, kw), F32)
        for kk in range(4):
            bk = bk + jnp.where(k == kk, b16[:, kk * kw:(kk + 1) * kw], 0.0)
        gw_ref[...] = _dot3(_tn, a16_ref[...], bk)
        pc_all[k] = _dot3(_nt, jnp.broadcast_to(bk[8:9, :], (8, kw)), wm_ref[...])
        for j, to in enumerate(chips):
            links.send(pc_all.at[k], pc_all.at[k], pc0 + j, to)
        for j, (px, py, pc) in enumerate(peers):
            links.arrived(smbuf.at[4 * px + 2 * py + pc], sm0 + j, (px, py, pc))
        tot = smbuf[0]
        for d in range(1, 8):
            tot = tot + smbuf[d]
        for r, (ref, w) in enumerate(zip(small_outs, SMALL_OUT_WIDTHS)):
            ref[...] = tot[r:r + 1, 0:w]
        for j, (px, py, pc) in enumerate(chips):
            links.arrived(wpg.at[2 * px + py], wp0 + 1 + j, (px, py, pc))
        wpt = wpg[0].astype(F32)
        for kk in range(1, 4):
            wpt = wpt + wpg[kk].astype(F32)
        rwp_ref[...] = wpt
        for wi in range(nw):
            for j, (px, py, pc) in enumerate(chips):
                links.arrived(got[wi].at[2 * px + py], big + wi * 5 + 1 + j, (px, py, pc))
            total = got[wi][0].astype(F32)
            for kk in range(1, 4):
                total = total + got[wi][kk].astype(F32)
            mine = _half(r_refs[wi], cc, axes[wi])
            mine[...] = total
            links.send(mine, mine, big + wi * 5 + 4, sibling)
        for j, (px, py, pc) in enumerate(chips):
            links.arrived(pc_all.at[2 * px + py], pc0 + j, (px, py, pc))
        ccv = cc_ref[...]
        sg = _sig(ccv)
        gc_ref[...] = (pc_all[0][0:1, :] + pc_all[1][0:1, :] + pc_all[2][0:1, :] + pc_all[3][0:1, :]) * (sg * (1.0 + ccv * (1.0 - sg)))
        for wi in range(nw):
            links.arrived(_half(r_refs[wi], 1 - cc, axes[wi]), big + wi * 5 + 4, sibling)
        links.drain()

    nsem = 5 * nw + 24
    quads = [(4,) + h for h in halves]
    return pl.pallas_call(
        body, name="reduce", in_specs=[ANY] * nw + [VM] * (ns + 5), out_specs=[VM] * (nw + ns + 4),
        out_shape=[jax.ShapeDtypeStruct(g.shape[1:], F32) for g in grads]
        + [jax.ShapeDtypeStruct((1, w), F32) for w in SMALL_OUT_WIDTHS]
        + [jax.ShapeDtypeStruct(w_pool_g.shape, F32), jax.ShapeDtypeStruct((D, kw), F32), jax.ShapeDtypeStruct((1, 3 * D), F32),
           jax.ShapeDtypeStruct((1, D), F32)],
        scratch_shapes=[pltpu.VMEM(q, F32) for q in quads] + [pltpu.VMEM(q, F32) for q in quads]
        + [pltpu.VMEM(q, BF16) for q in quads] + [pltpu.VMEM(q, BF16) for q in quads]
        + [pltpu.VMEM((8, 8, D), F32), pltpu.VMEM(w_pool_g.shape, F32), pltpu.VMEM((4,) + w_pool_g.shape, BF16),
           pltpu.VMEM((8, 8, 3 * D), F32),
           pltpu.VMEM((4, 8, D), F32)]
        + [pltpu.SemaphoreType.DMA((nsem,)), pltpu.SemaphoreType.DMA((nsem,)), pltpu.SemaphoreType.DMA((nw,))],
        compiler_params=pltpu.CompilerParams(vmem_limit_bytes=VMEM_LIMIT),
    )(*grads, *smalls, w_pool_g, dmod8, a16, w_mod, c_ctx)


def _rope_tables(s_len):
    rows = s_len // GRID_W
    per = TB // GRID_W
    n_freq = 16
    inv = ROPE_BASE ** (-jnp.arange(n_freq, dtype=F32) / n_freq)
    ang_r = jnp.arange(rows, dtype=F32)[:, None] * inv
    ang_c = jnp.arange(GRID_W, dtype=F32)[:, None] * inv
    by_row, by_col = [], []
    for fn, pad in ((jnp.cos, 1.0), (jnp.sin, 0.0)):
        r = jnp.concatenate([fn(ang_r), fn(ang_r), jnp.zeros((rows, 96), F32)], axis=1).reshape(rows // per, per, 128)
        by_row.append(jnp.pad(r, ((0, 0), (0, 8 - per), (0, 0))))
        cpart = jnp.concatenate([jnp.zeros((GRID_W, 32), F32), fn(ang_c), fn(ang_c), jnp.full((GRID_W, 64), pad, F32)], axis=1)
        by_col.append(jnp.tile(cpart, (per, 1)))
    return jnp.concatenate(by_row, axis=-1), jnp.concatenate(by_col, axis=-1)


def kernel(x, c, ctx, c_ctx, w_mod, b_mod, norm_g, w_in, q_lora_g, w_uq, kv_lora_g, w_ukv, q_norm_g, k_norm_g, w_pool, pool_scale, w_out, loss_target, m_c_ctx, m_w_mod, m_b_mod, m_norm_g, m_w_in, m_q_lora_g, m_w_uq, m_kv_lora_g, m_w_ukv, m_q_norm_g, m_k_norm_g, m_w_pool, m_pool_scale, m_w_out, v_c_ctx, v_w_mod, v_b_mod, v_norm_g, v_w_in, v_q_lora_g, v_w_uq, v_kv_lora_g, v_w_ukv, v_q_norm_g, v_k_norm_g, v_w_pool, v_pool_scale, v_w_out):
    xi, yi, ci = lax.axis_index("x"), lax.axis_index("y"), lax.axis_index("c")
    me = 4 * xi + 2 * yi + ci
    k = 2 * xi + yi
    s_len = x.shape[1]
    lc = ctx.shape[1]
    kw = w_mod.shape[2]
    weights = dict(c_ctx=c_ctx, w_mod=w_mod, b_mod=b_mod, norm_g=norm_g, w_in=w_in, q_lora_g=q_lora_g, w_uq=w_uq,
                   kv_lora_g=kv_lora_g, w_ukv=w_ukv, q_norm_g=q_norm_g, k_norm_g=k_norm_g, w_pool=w_pool,
                   pool_scale=pool_scale, w_out=w_out)
    m_in = dict(c_ctx=m_c_ctx, w_mod=m_w_mod, b_mod=m_b_mod, norm_g=m_norm_g, w_in=m_w_in, q_lora_g=m_q_lora_g, w_uq=m_w_uq,
                kv_lora_g=m_kv_lora_g, w_ukv=m_w_ukv, q_norm_g=m_q_norm_g, k_norm_g=m_k_norm_g, w_pool=m_w_pool,
                pool_scale=m_pool_scale, w_out=m_w_out)
    v_in = dict(c_ctx=v_c_ctx, w_mod=v_w_mod, b_mod=v_b_mod, norm_g=v_norm_g, w_in=v_w_in, q_lora_g=v_q_lora_g, w_uq=v_w_uq,
                kv_lora_g=v_kv_lora_g, w_ukv=v_w_ukv, q_norm_g=v_q_norm_g, k_norm_g=v_k_norm_g, w_pool=v_w_pool,
                pool_scale=v_pool_scale, w_out=v_w_out)
    order = ["c_ctx", "w_mod", "b_mod", "norm_g", "w_in", "q_lora_g", "w_uq", "kv_lora_g", "w_ukv", "q_norm_g", "k_norm_g",
             "w_pool", "pool_scale", "w_out"]
    transposed = ("w_in", "w_uq")
    as2d = lambda n, a: jnp.transpose(a[0]) if n in transposed else a.reshape(-1, a.shape[-1])
    back = lambda n, a: jnp.transpose(a)[None] if n in transposed else a.reshape(weights[n].shape)

    c_ctx2 = c_ctx.reshape(1, D)
    b_mod_k = lax.dynamic_slice(b_mod, (0, k * kw), (1, kw))
    split = (1, 0, 0, 0)
    a16, mod_all, g_in, g_uq, g_ukv, g_out = _gather(
        c, c_ctx2, w_mod[0], b_mod_k,
        [as2d("w_in", w_in).astype(BF16), as2d("w_uq", w_uq).astype(BF16), w_ukv[0].astype(BF16), w_out[0].astype(BF16)], split)
    mod_me = lax.dynamic_index_in_dim(mod_all, me, axis=1, keepdims=False).reshape(3, D)
    mod_c = mod_all[:, 8, :].reshape(3, D)
    modsel = jnp.stack([mod_c, mod_me])
    w_in_t = g_in.reshape(DIN, D)
    w_uq_t = jnp.pad(g_uq, ((0, 0), (0, DKP - DK), (0, 0)))
    w_out_f = g_out.reshape(D, D)
    qn_g = jnp.pad(q_norm_g, ((0, 0), (0, DKP - DK)))
    kn_g = jnp.pad(k_norm_g, ((0, 0), (0, DKP - DK)))
    w_pool_b = w_pool[0].astype(BF16)
    cos, sin = _rope_tables(s_len)

    u, q, kk, v = _fwd_in(ctx[0], x[0], modsel, norm_g, w_in_t, q_lora_g, w_uq_t, kv_lora_g, g_ukv, qn_g, kn_g, cos, sin)
    attn, lse = _attn_fwd(q, kk, v, s_len)
    (dxn, dattn, dga, dgp, dpool, dw_out, dgate, dps, dw_pool, loss) = _out_stage(
        attn.reshape(s_len // Q_BLOCK, Q_BLOCK, NH * DV), u, x[0], loss_target[0], modsel[1, 2:3, :], w_pool_b, pool_scale,
        w_out_f, lc)
    dattn = dattn.reshape(s_len, NH * DV)
    dq, dk, dv = _attn_bwd(q, kk, v, dattn, attn, lse, s_len)
    dlo, dw_uq_t, dw_ukv, dqlg, dkvlg, dqng, dkng = _qkv_bwd(u, dq, dk, dv, cos, sin, q_lora_g, w_uq_t, kv_lora_g, g_ukv,
                                                            qn_g, kn_g, s_len)
    gx, dw_in_t, dmod, dng = _in_bwd(ctx[0], x[0], modsel, norm_g, dlo, dga, dgp, dpool, dxn, w_in_t)

    dmod_l = jnp.concatenate([dmod[1, 0], dmod[1, 1], dgate[0]]).reshape(1, 3 * D)
    dmod_c = jnp.concatenate([dmod[0, 0], dmod[0, 1], jnp.zeros((D,), F32)]).reshape(1, 3 * D)
    dmod8 = jnp.concatenate([dmod_l, dmod_c, jnp.zeros((6, 3 * D), F32)], axis=0)
    (r_in, r_uq, r_ukv, r_out, g_ng, g_qlg, g_kvlg, g_qng, g_kng, g_ps, loss_all, g_wp, g_w_mod, g_b_mod, g_c_ctx) = _reduce(
        [dw_in_t.reshape(4, DIN // 4, D), dw_uq_t, dw_ukv, dw_out.reshape(4, D // 4, D)], split,
        [dng, dqlg, dkvlg, dqng, dkng, dps, loss], dw_pool, dmod8, a16, w_mod[0], c_ctx2)
    g2d = dict(c_ctx=g_c_ctx, b_mod=g_b_mod, w_mod=g_w_mod, w_in=r_in, w_uq=r_uq, w_ukv=r_ukv, w_out=r_out, norm_g=g_ng,
               q_lora_g=g_qlg, kv_lora_g=g_kvlg, q_norm_g=g_qng, k_norm_g=g_kng, pool_scale=g_ps, w_pool=g_wp.reshape(512, 128))

    d2d, m2d, v2d = {}, {}, {}
    for n in ("w_mod", "w_in", "w_uq", "w_ukv", "w_out"):
        d2d[n], m2d[n], v2d[n] = _adamw(as2d(n, weights[n]), g2d[n], as2d(n, m_in[n]), as2d(n, v_in[n]), "adamw_" + n)
    small = [n for n in order if n not in d2d]
    outs = _adamw_small([as2d(n, weights[n]) for n in small], [g2d[n] for n in small], [as2d(n, m_in[n]) for n in small],
                        [as2d(n, v_in[n]) for n in small])
    for dst, arrs in zip((d2d, m2d, v2d), outs):
        dst.update(dict(zip(small, arrs)))

    return (loss_all[0, 0], gx[None], *[back(n, g2d[n]) for n in order], *[back(n, d2d[n]) for n in order],
            *[back(n, m2d[n]) for n in order], *[back(n, v2d[n]) for n in order])
```

```python
import jax
import jax.numpy as jnp
from jax import lax
from jax.experimental import pallas as pl
from jax.experimental.pallas import tpu as pltpu

F32 = jnp.float32
BF16 = jnp.bfloat16
MESH = pl.DeviceIdType.MESH

D = 1024
NH = 4
DK = 192
DKP = 256
DV = 128
QL = 256
KVL = 128
DIN = 1984
U_LO = 448
SEG = ((0, 512), (448, 960), (960, 1472), (1472, 1984))
DU = 2048
POOL_WINDOWS = (2, 4, 8, 16)
HALO = 8
EPS = 1e-6
ROPE_BASE = 10000.0
GRID_W = 64
Q_BLOCK = 128
TB = 256
BWD_QBLOCKS = 1
SCALE = DK ** -0.5
LOG2E = 1.4426950408889634
LN2 = 0.6931471805599453
VMEM_LIMIT = 56 * 1024 * 1024

ADAM_LR = 0.001
ADAM_B1 = 0.9
ADAM_B2 = 0.999
ADAM_EPS = 1e-08
ADAM_WD = 0.01
ADAM_STEP = 10

CHIPS3 = ((1, 0), (0, 1), (1, 1))
PEERS7 = tuple((dx, dy, dc) for dx in (0, 1) for dy in (0, 1) for dc in (0, 1) if (dx, dy, dc) != (0, 0, 0))

VM = pl.BlockSpec(memory_space=pltpu.VMEM)
ANY = pl.BlockSpec(memory_space=pl.ANY)


def _nn(a, b):
    return jnp.dot(a, b, preferred_element_type=F32)


def _nt(a, b):
    return lax.dot_general(a, b, (((1,), (1,)), ((), ())), preferred_element_type=F32)


def _tn(a, b):
    return lax.dot_general(a, b, (((0,), (0,)), ((), ())), preferred_element_type=F32)


def _split3(a):
    a0 = a.astype(BF16)
    r = a - a0.astype(F32)
    a1 = r.astype(BF16)
    a2 = (r - a1.astype(F32)).astype(BF16)
    return a0, a1, a2


def _dot3(dot, a, b):
    sa = _split3(a)
    sb = _split3(b)
    out = None
    for i in range(3):
        for j in range(3 - i):
            t = dot(sa[i], sb[j])
            out = t if out is None else out + t
    return out


def _sig(x):
    return 1.0 / (1.0 + jnp.exp(-x))


def _rot(t):
    lane = lax.broadcasted_iota(jnp.int32, t.shape, 1)
    return jnp.where((lane % 32) < 16, -pltpu.roll(t, 112, 1), pltpu.roll(t, 16, 1))


def _rope(t, cos, sin):
    return t * cos + _rot(t) * sin


def _rope_t(t, cos, sin):
    return t * cos - _rot(t * sin)


def _rope_block(rows_ref, cols_ref, is_ctx):
    lane = lax.broadcasted_iota(jnp.int32, (TB, 256), 1) % 128
    rows = jnp.concatenate([jnp.broadcast_to(rows_ref[0, r:r + 1, :], (GRID_W, 256)) for r in range(TB // GRID_W)], axis=0)
    cs = jnp.where(lane < 32, rows, cols_ref[...])
    return jnp.where(is_ctx, 1.0, cs[:, :128]), jnp.where(is_ctx, 0.0, cs[:, 128:])


def _shift_rows(z, k):
    n = z.shape[0]
    return pltpu.roll(z, (n - k) % n, 0)


def _colsum(a):
    return jnp.sum(a, axis=0, keepdims=True)


def _rowsum(a):
    return jnp.sum(a, axis=-1, keepdims=True)


def _row_layout(col):
    return jnp.transpose(jnp.broadcast_to(col, (col.shape[0], 128)))[0:8, :]


def _params(sem=None):
    return pltpu.CompilerParams(dimension_semantics=sem, vmem_limit_bytes=VMEM_LIMIT)


def _full(shape):
    nd = len(shape)
    return pl.BlockSpec(shape, lambda *_: (0,) * nd)


def _peer(x, y, c, off):
    dx, dy, dc = off
    return ((x + dx) % 2, (y + dy) % 2, (c + dc) % 2)


def _token_specs(off):
    ctx = pl.BlockSpec((TB, D), lambda i: (jnp.minimum(i, off - 1), 0))
    lat = pl.BlockSpec((TB, D), lambda i: (jnp.maximum(i - off, 0), 0))
    mod = pl.BlockSpec((1, 3, D), lambda i: (jnp.minimum(i // off, 1), 0, 0))
    return ctx, lat, mod


def _modulated(x, mod_ref, ng):
    shift = mod_ref[0, 0:1, :]
    scale = mod_ref[0, 1:2, :]
    r = lax.rsqrt(jnp.mean(x * x, axis=-1, keepdims=True) + EPS)
    xh = x * r
    xg = xh * ng
    return r, xh, xg, xg * (1.0 + scale) + shift, scale


def _fwd_in(ctx, x, modsel, norm_g, w_in_t, q_lora_g, w_uq_t, kv_lora_g, w_ukv, qn_g, kn_g, cos, sin):
    s_len, lc = x.shape[0], ctx.shape[0]
    t_all = s_len + lc
    nb = t_all // TB
    off = lc // TB

    def body(ctx_ref, x_ref, mod_ref, ng_ref, win_ref, qlg_ref, wuq_ref, kvlg_ref, wukv_ref, qng_ref, kng_ref, cos_ref, sin_ref,
             u_ref, q_ref, k_ref, v_ref):
        is_ctx = pl.program_id(0) < off
        xb = jnp.where(is_ctx, ctx_ref[...], x_ref[...])
        _, _, _, h, _ = _modulated(xb, mod_ref, ng_ref[...])
        hb = h.astype(BF16)
        lane = lax.broadcasted_iota(jnp.int32, (TB, 512), 1)
        ulo = jnp.where(lane < U_LO, _nt(hb, win_ref[SEG[0][0]:SEG[0][1], :]), 0.0)
        u_ref[:, 0:512] = ulo
        for j in range(1, 4):
            u_ref[:, j * 512:(j + 1) * 512] = _nt(hb, win_ref[SEG[j][0]:SEG[j][1], :])
        cos, sin = _rope_block(cos_ref, sin_ref, is_ctx)
        cq = ulo[:, 0:QL]
        cqn = (cq * lax.rsqrt(jnp.mean(cq * cq, axis=-1, keepdims=True) + EPS) * qlg_ref[...]).astype(BF16)
        qng = qng_ref[...]
        for hd in range(NH):
            qh = _nt(cqn, wuq_ref[hd])
            qn = qh * lax.rsqrt(_rowsum(qh * qh) / DK + EPS) * qng
            q_ref[hd] = (jnp.concatenate([qn[:, :128], _rope(qn[:, 128:], cos, sin)], axis=1) * (SCALE * LOG2E)).astype(BF16)
        ckv = ulo[:, QL:QL + KVL]
        ckvn = (ckv * lax.rsqrt(jnp.mean(ckv * ckv, axis=-1, keepdims=True) + EPS) * kvlg_ref[...]).astype(BF16)
        kr = ulo[:, 384:512]
        skr = _rowsum(kr * kr)
        kng = kng_ref[...]
        for hd in range(NH):
            kv = _nn(ckvn, wukv_ref[hd])
            kn = kv[:, :128]
            rk = lax.rsqrt((_rowsum(kn * kn) + skr) / DK + EPS)
            k_ref[hd] = jnp.concatenate([kn * rk * kng[:, :128], _rope(kr * rk * kng[:, 128:], cos, sin)], axis=1).astype(BF16)
            v_ref[hd] = kv[:, 128:].astype(BF16)

    row = lambda w: pl.BlockSpec((TB, w), lambda i: (i, 0))
    heads = lambda w: pl.BlockSpec((NH, TB, w), lambda i: (0, i, 0))
    cspec, xspec, mspec = _token_specs(off)
    return pl.pallas_call(
        body, name="fwd_in", grid=(nb,),
        in_specs=[cspec, xspec, mspec, _full((1, D)), _full((DIN, D)), _full((1, QL)), _full((NH, DKP, QL)), _full((1, KVL)),
                  _full((NH, KVL, 256)), _full((1, DKP)), _full((1, DKP)),
                  pl.BlockSpec((1, 8, 256), lambda i: (jnp.maximum(i - off, 0), 0, 0)), _full((TB, 256))],
        out_specs=[row(DU), heads(DKP), heads(DKP), heads(DV)],
        out_shape=[jax.ShapeDtypeStruct((t_all, DU), F32), jax.ShapeDtypeStruct((NH, t_all, DKP), BF16),
                   jax.ShapeDtypeStruct((NH, t_all, DKP), BF16), jax.ShapeDtypeStruct((NH, t_all, DV), BF16)],
        compiler_params=_params(("arbitrary",)),
    )(ctx, x, modsel, norm_g, w_in_t, q_lora_g, w_uq_t, kv_lora_g, w_ukv, qn_g, kn_g, cos, sin)


def _attn_fwd(q, k, v, s_len):
    t_all = q.shape[1]
    off = (t_all - s_len) // TB
    nq = s_len // TB
    nsub = next(n for n in (4, 2, 1) if nq % n == 0)

    def body(*refs):
        q_refs = refs[:nsub]
        k_ref, v_ref, o_ref, lse_ref = refs[nsub:]
        for sb in range(nsub):
            s = _nt(q_refs[sb][0], k_ref[0])
            m = jnp.max(s, axis=-1, keepdims=True)
            e = jnp.exp2(s - m)
            l = _rowsum(e)
            o_ref[sb * TB:(sb + 1) * TB, :] = _nn(e.astype(BF16), v_ref[0]) / l
            lse_ref[0, sb] = _row_layout(m + jnp.log2(l))

    qspec = lambda sb: pl.BlockSpec((1, TB, DKP), lambda h, i: (h, i * nsub + sb + off, 0))
    return pl.pallas_call(
        body, name="attn_fwd", grid=(NH, nq // nsub),
        in_specs=[qspec(sb) for sb in range(nsub)]
        + [pl.BlockSpec((1, t_all, DKP), lambda h, i: (h, 0, 0)), pl.BlockSpec((1, t_all, DV), lambda h, i: (h, 0, 0))],
        out_specs=[pl.BlockSpec((nsub * TB, DV), lambda h, i: (i, h)), pl.BlockSpec((1, nsub, 8, TB), lambda h, i: (h, i, 0, 0))],
        out_shape=[jax.ShapeDtypeStruct((s_len, NH * DV), F32), jax.ShapeDtypeStruct((NH, nq, 8, TB), F32)],
        compiler_params=_params(("arbitrary", "arbitrary")),
    )(*([q] * nsub), k, v)


def _out_stage(attn, u, x, target, gate, w_pool, pool_scale, w_out, lc):
    s_len = x.shape[0]
    t_all = s_len + lc
    off = lc // TB
    nq = s_len // TB
    hb = TB // HALO
    nqb = s_len // Q_BLOCK
    jb = TB // nqb

    def body(attn_ref, ga_ref, pin_ref, pprev_ref, pnext_ref, gp_ref, x_ref, tgt_ref, gate_ref, wp_ref, ps_ref, wo_ref,
             dxn_ref, dattn_ref, dga_ref, dgp_ref, dpool_ref, dwo_ref, dgate_ref, dps_ref, dwp_ref, loss_ref):
        i = pl.program_id(0)

        @pl.when(i == 0)
        def _():
            dwo_ref[...] = jnp.zeros_like(dwo_ref)
            dgate_ref[...] = jnp.zeros_like(dgate_ref)
            dps_ref[...] = jnp.zeros_like(dps_ref)
            dwp_ref[...] = jnp.zeros_like(dwp_ref)
            loss_ref[...] = jnp.zeros_like(loss_ref)

        attn = jnp.concatenate([attn_ref[:, jj, :] for jj in range(jb)], axis=0)
        ga = ga_ref[...]
        gp = gp_ref[...]
        pin = pin_ref[...]
        prev = jnp.where(i == 0, 0.0, pprev_ref[...])
        nxt = jnp.where(i == nq - 1, 0.0, pnext_ref[...])
        win = jnp.concatenate([prev, pin, nxt], axis=0)
        tg = i * TB + lax.broadcasted_iota(jnp.int32, (TB, 1), 0)
        pooled = []
        for g, w in enumerate(POOL_WINDOWS):
            a = win[:, g * 128:(g + 1) * 128]
            p = _shift_rows(a, -1) + a
            for step in (1, 2, 4):
                if w >= 4 * step:
                    p = _shift_rows(p, -step) + _shift_rows(p, step)
            cnt = (jnp.minimum(tg + w // 2, s_len) - jnp.maximum(tg - w // 2, 0)).astype(F32)
            pooled.append(p[HALO:HALO + TB] / cnt - a[HALO:HALO + TB])
        pooled_b = [p.astype(BF16) for p in pooled]
        z = jnp.concatenate([_nn(pooled_b[g], wp_ref[g]) for g in range(4)], axis=1)
        ps = ps_ref[...]
        yp = z * ps
        sga = _sig(ga)
        sila = ga * sga
        sgp = _sig(gp)
        silp = gp * sgp
        br = jnp.concatenate([sila * attn, silp * yp], axis=1).astype(BF16)
        y = _nn(br, wo_ref[...])
        gate = gate_ref[...]
        err = x_ref[...] + gate * y - tgt_ref[...]
        loss_ref[...] += _colsum(_rowsum(err * err)) * (0.5 / D)
        dxn = err * (1.0 / D)
        dxn_ref[...] = dxn
        dgate_ref[...] += _colsum(dxn * y)
        dy = (dxn * gate).astype(BF16)
        dwo_ref[...] += _tn(br, dy)
        dbr = _nt(dy, wo_ref[...])
        dbra = dbr[:, :512]
        dbrp = dbr[:, 512:]
        dattn = dbra * sila
        for jj in range(jb):
            dattn_ref[:, jj, :] = dattn[jj * nqb:(jj + 1) * nqb]
        dga_ref[...] = (dbra * attn * (sga * (1.0 + ga * (1.0 - sga)))).astype(BF16)
        dgp_ref[...] = (dbrp * yp * (sgp * (1.0 + gp * (1.0 - sgp)))).astype(BF16)
        dyp = dbrp * silp
        dps_ref[...] += _colsum(dyp * z)
        dz = (dyp * ps).astype(BF16)
        dpool = []
        for g in range(4):
            dzg = dz[:, g * 128:(g + 1) * 128]
            dwp_ref[g] += _tn(pooled_b[g], dzg)
            dpool.append(_nt(dzg, wp_ref[g]))
        dpool_ref[...] = jnp.concatenate(dpool, axis=1)

    lat = lambda w: pl.BlockSpec((TB, w), lambda i: (i, 0))
    perm = pl.BlockSpec((nqb, jb, 512), lambda i: (0, i, 0))
    ucol = lambda j: pl.BlockSpec((TB, 512), lambda i: (i + off, j))
    last8 = t_all // HALO - 1
    return pl.pallas_call(
        body, name="out_stage", grid=(nq,),
        in_specs=[perm, ucol(1), ucol(2),
                  pl.BlockSpec((HALO, 512), lambda i: ((i + off) * hb - 1, 2)),
                  pl.BlockSpec((HALO, 512), lambda i: (jnp.minimum((i + off + 1) * hb, last8), 2)),
                  ucol(3), lat(D), lat(D), _full((1, D)), _full((4, 128, 128)), _full((1, 512)), _full((D, D))],
        out_specs=[lat(D), perm, lat(512), lat(512), lat(512),
                   _full((D, D)), _full((1, D)), _full((1, 512)), _full((4, 128, 128)), _full((1, 1))],
        out_shape=[jax.ShapeDtypeStruct((s_len, D), F32), jax.ShapeDtypeStruct((nqb, Q_BLOCK, 512), F32),
                   jax.ShapeDtypeStruct((s_len, 512), BF16), jax.ShapeDtypeStruct((s_len, 512), BF16),
                   jax.ShapeDtypeStruct((s_len, 512), F32),
                   jax.ShapeDtypeStruct((D, D), F32), jax.ShapeDtypeStruct((1, D), F32), jax.ShapeDtypeStruct((1, 512), F32),
                   jax.ShapeDtypeStruct((4, 128, 128), F32), jax.ShapeDtypeStruct((1, 1), F32)],
        compiler_params=_params(("arbitrary",)),
    )(attn, u, u, u, u, u, x, target, gate, w_pool, pool_scale, w_out)


def _attn_bwd(q, k, v, dattn, attn, lse, s_len):
    t_all = q.shape[1]
    off = (t_all - s_len) // TB
    nq = s_len // TB
    nch = 4
    chunks = [(c * (t_all // nch), t_all // nch) for c in range(nch)]
    nsub = next(n for n in (BWD_QBLOCKS, 2, 1) if nq % n == 0)
    tq = nsub * TB

    def body(*refs):
        q_refs = refs[:nsub]
        k_ref, v_ref, do_ref, o_ref, lse_ref, dq_ref, dk_ref, dv_ref = refs[nsub:]
        i = pl.program_id(1)

        @pl.when(i == 0)
        def _():
            dk_ref[...] = jnp.zeros_like(dk_ref)
            dv_ref[...] = jnp.zeros_like(dv_ref)

        qb = jnp.concatenate([r[0] for r in q_refs], axis=0)
        delta_r = _row_layout(_rowsum(do_ref[...] * o_ref[...]))[0:1, :]
        do = do_ref[...].astype(BF16)
        lse_r = jnp.concatenate([lse_ref[0, sb][0:1, :] for sb in range(nsub)], axis=1)
        dq = jnp.zeros((tq, DKP), F32)
        for start, size in chunks:
            rows = pl.ds(start, size)
            kc = k_ref[0, rows, :]
            p_t = jnp.exp2(_nt(kc, qb) - lse_r)
            ds_t = (p_t * (_nt(v_ref[0, rows, :], do) - delta_r)).astype(BF16)
            dv_ref[0, rows, :] += _nn(p_t.astype(BF16), do)
            dk_ref[0, rows, :] += _nn(ds_t, qb)
            dq += _tn(ds_t, kc)
        dq_ref[0] = dq * SCALE

    kvspec = lambda w: pl.BlockSpec((1, t_all, w), lambda h, i: (h, 0, 0))
    rowspec = pl.BlockSpec((1, nsub, 8, TB), lambda h, i: (h, i, 0, 0))
    qspec = lambda sb: pl.BlockSpec((1, TB, DKP), lambda h, i: (h, i * nsub + sb + off, 0))
    return pl.pallas_call(
        body, name="attn_bwd", grid=(NH, nq // nsub),
        in_specs=[qspec(sb) for sb in range(nsub)]
        + [kvspec(DKP), kvspec(DV), pl.BlockSpec((tq, DV), lambda h, i: (i, h)), pl.BlockSpec((tq, DV), lambda h, i: (i, h)),
           rowspec],
        out_specs=[pl.BlockSpec((1, tq, DKP), lambda h, i: (h, i, 0)), kvspec(DKP), kvspec(DV)],
        out_shape=[jax.ShapeDtypeStruct((NH, s_len, DKP), F32), jax.ShapeDtypeStruct((NH, t_all, DKP), F32),
                   jax.ShapeDtypeStruct((NH, t_all, DV), F32)],
        compiler_params=_params(("arbitrary", "arbitrary")),
    )(*([q] * nsub), k, v, dattn, attn, lse)


def _qkv_bwd(u, dq, dk, dv, cos, sin, q_lora_g, w_uq_t, kv_lora_g, w_ukv, qn_g, kn_g, s_len):
    t_all = u.shape[0]
    off = (t_all - s_len) // TB
    nb = t_all // TB

    def body(ulo_ref, dq_ref, dk_ref, dv_ref, cos_ref, sin_ref, qlg_ref, wuq_ref, kvlg_ref, wukv_ref, qng_ref, kng_ref,
             dlo_ref, dwuq_ref, dwukv_ref, dqlg_ref, dkvlg_ref, dqng_ref, dkng_ref):
        i = pl.program_id(0)

        @pl.when(i == 0)
        def _():
            for r in (dwuq_ref, dwukv_ref, dqlg_ref, dkvlg_ref, dqng_ref, dkng_ref):
                r[...] = jnp.zeros_like(r)

        latent = i >= off
        ulo = ulo_ref[...]
        cos, sin = _rope_block(cos_ref, sin_ref, pl.program_id(0) < off)
        cq = ulo[:, 0:QL]
        rc = lax.rsqrt(jnp.mean(cq * cq, axis=-1, keepdims=True) + EPS)
        cqh = cq * rc
        qlg = qlg_ref[...]
        cqn_b = (cqh * qlg).astype(BF16)
        qng = qng_ref[...]
        dqng = jnp.zeros((1, DKP), F32)
        dcqn = jnp.zeros((TB, QL), F32)
        for hd in range(NH):
            qh = _nt(cqn_b, wuq_ref[hd])
            rq = lax.rsqrt(_rowsum(qh * qh) / DK + EPS)
            xh = qh * rq
            dqh = jnp.where(latent, dq_ref[hd], 0.0)
            dyq = jnp.concatenate([dqh[:, :128], _rope_t(dqh[:, 128:], cos, sin)], axis=1)
            dqng += _colsum(dyq * xh)
            dxh = dyq * qng
            dqraw = (rq * (dxh - xh * (_rowsum(dxh * xh) / DK))).astype(BF16)
            dwuq_ref[hd] += _tn(dqraw, cqn_b)[:DK]
            dcqn += _nn(dqraw, wuq_ref[hd])
        dqng_ref[...] += dqng
        dqlg_ref[...] += _colsum(dcqn * cqh)
        dxh = dcqn * qlg
        dcq = rc * (dxh - cqh * jnp.mean(dxh * cqh, axis=-1, keepdims=True))

        ckv = ulo[:, QL:QL + KVL]
        r0 = lax.rsqrt(jnp.mean(ckv * ckv, axis=-1, keepdims=True) + EPS)
        ckvh = ckv * r0
        kvlg = kvlg_ref[...]
        ckvn_b = (ckvh * kvlg).astype(BF16)
        kr = ulo[:, 384:512]
        skr = _rowsum(kr * kr)
        kng = kng_ref[...]
        dkr = jnp.zeros((TB, 128), F32)
        dkng = jnp.zeros((1, DKP), F32)
        dckvn = jnp.zeros((TB, KVL), F32)
        for hd in range(NH):
            kn = _nn(ckvn_b, wukv_ref[hd])[:, :128]
            rk = lax.rsqrt((_rowsum(kn * kn) + skr) / DK + EPS)
            xh1 = kn * rk
            xh2 = kr * rk
            dkh = dk_ref[hd] * LN2
            d1 = dkh[:, :128]
            d2 = _rope_t(dkh[:, 128:], cos, sin)
            dkng += jnp.concatenate([_colsum(d1 * xh1), _colsum(d2 * xh2)], axis=1)
            dx1 = d1 * kng[:, :128]
            dx2 = d2 * kng[:, 128:]
            dot = (_rowsum(dx1 * xh1) + _rowsum(dx2 * xh2)) / DK
            dkv = jnp.concatenate([rk * (dx1 - xh1 * dot), dv_ref[hd]], axis=1).astype(BF16)
            dkr += rk * (dx2 - xh2 * dot)
            dwukv_ref[hd] += _tn(ckvn_b, dkv)
            dckvn += _nt(dkv, wukv_ref[hd])
        dkng_ref[...] += dkng
        dkvlg_ref[...] += _colsum(dckvn * ckvh)
        dxh = dckvn * kvlg
        dckv = r0 * (dxh - ckvh * jnp.mean(dxh * ckvh, axis=-1, keepdims=True))
        dlo_ref[...] = jnp.concatenate([dcq, dckv, dkr], axis=1).astype(BF16)

    row = lambda w: pl.BlockSpec((TB, w), lambda i: (i, 0))
    heads = lambda w: pl.BlockSpec((NH, TB, w), lambda i: (0, i, 0))
    return pl.pallas_call(
        body, name="qkv_bwd", grid=(nb,),
        in_specs=[row(512), pl.BlockSpec((NH, TB, DKP), lambda i: (0, jnp.maximum(i - off, 0), 0)), heads(DKP), heads(DV),
                  pl.BlockSpec((1, 8, 256), lambda i: (jnp.maximum(i - off, 0), 0, 0)), _full((TB, 256)), _full((1, QL)), _full((NH, DKP, QL)), _full((1, KVL)), _full((NH, KVL, 256)),
                  _full((1, DKP)), _full((1, DKP))],
        out_specs=[row(512), _full((NH, DK, QL)), _full((NH, KVL, 256)), _full((1, QL)), _full((1, KVL)),
                   _full((1, DKP)), _full((1, DKP))],
        out_shape=[jax.ShapeDtypeStruct((t_all, 512), BF16), jax.ShapeDtypeStruct((NH, DK, QL), F32),
                   jax.ShapeDtypeStruct((NH, KVL, 256), F32), jax.ShapeDtypeStruct((1, QL), F32),
                   jax.ShapeDtypeStruct((1, KVL), F32), jax.ShapeDtypeStruct((1, DKP), F32), jax.ShapeDtypeStruct((1, DKP), F32)],
        compiler_params=_params(("arbitrary",)),
    )(u, dq, dk, dv, cos, sin, q_lora_g, w_uq_t, kv_lora_g, w_ukv, qn_g, kn_g)


def _in_bwd(ctx, x, modsel, norm_g, dlo, dga, dgp, dpool, dxn, w_in_t):
    s_len, lc = x.shape[0], ctx.shape[0]
    t_all = s_len + lc
    off = lc // TB
    nb = t_all // TB
    nq = s_len // TB
    hb = TB // HALO
    n = TB + 2 * HALO

    def body(ctx_ref, x_ref, mod_ref, ng_ref, dlo_ref, dga_ref, dgp_ref, dp_ref, dpprev_ref, dpnext_ref, dxn_ref, win_ref,
             gx_ref, dwin_ref, dmod_ref, dng_ref):
        i = pl.program_id(0)
        j = i - off

        @pl.when(i == 0)
        def _():
            dwin_ref[...] = jnp.zeros_like(dwin_ref)
            dmod_ref[...] = jnp.zeros_like(dmod_ref)
            dng_ref[...] = jnp.zeros_like(dng_ref)

        latent = i >= off
        dp = dp_ref[...]
        prev = jnp.where(j <= 0, 0.0, dpprev_ref[...])
        nxt = jnp.where(j >= nq - 1, 0.0, dpnext_ref[...])
        win = jnp.concatenate([prev, dp, nxt], axis=0)
        tg = j * TB - HALO + lax.broadcasted_iota(jnp.int32, (n, 1), 0)
        dpin = []
        for g, w in enumerate(POOL_WINDOWS):
            cnt = jnp.maximum(jnp.minimum(tg + w // 2, s_len) - jnp.maximum(tg - w // 2, 0), 1).astype(F32)
            zq = win[:, g * 128:(g + 1) * 128] / cnt
            zq = zq + _shift_rows(zq, 1)
            for step in (1, 2, 4):
                if w >= 4 * step:
                    zq = _shift_rows(zq, -step) + _shift_rows(zq, step)
            dpin.append(zq[HALO:HALO + TB] - dp[:, g * 128:(g + 1) * 128])
        zero = jnp.zeros((TB, 512), BF16)
        du = [dlo_ref[...], jnp.where(latent, dga_ref[...], zero),
              jnp.where(latent, jnp.concatenate(dpin, axis=1).astype(BF16), zero), jnp.where(latent, dgp_ref[...], zero)]

        ng = ng_ref[...]
        xb = jnp.where(i < off, ctx_ref[...], x_ref[...])
        r, xh, xg, h, scale = _modulated(xb, mod_ref, ng)
        hb_ = h.astype(BF16)
        dh = jnp.zeros((TB, D), F32)
        for s, (lo, hi) in enumerate(SEG):
            dwin_ref[lo:hi, :] += _tn(du[s], hb_)
            dh += _nn(du[s], win_ref[lo:hi, :])
        is_lat = latent.astype(F32)
        dsh = _colsum(dh)
        dsc = _colsum(dh * xg)
        dmod_ref[0, 0:1, :] += dsh * (1.0 - is_lat)
        dmod_ref[0, 1:2, :] += dsc * (1.0 - is_lat)
        dmod_ref[1, 0:1, :] += dsh * is_lat
        dmod_ref[1, 1:2, :] += dsc * is_lat
        dxg = dh * (1.0 + scale)
        dng_ref[...] += _colsum(dxg * xh)
        dxh = dxg * ng
        gx_ref[...] = r * (dxh - xh * jnp.mean(dxh * xh, axis=-1, keepdims=True)) + dxn_ref[...]

    row = lambda w: pl.BlockSpec((TB, w), lambda i: (i, 0))
    lat = lambda w: pl.BlockSpec((TB, w), lambda i: (jnp.maximum(i - off, 0), 0))
    last8 = s_len // HALO - 1
    cspec, xspec, mspec = _token_specs(off)
    return pl.pallas_call(
        body, name="in_bwd", grid=(nb,),
        in_specs=[cspec, xspec, mspec, _full((1, D)), row(512), lat(512), lat(512), lat(512),
                  pl.BlockSpec((HALO, 512), lambda i: (jnp.maximum(jnp.maximum(i - off, 0) * hb - 1, 0), 0)),
                  pl.BlockSpec((HALO, 512), lambda i: (jnp.minimum((jnp.maximum(i - off, 0) + 1) * hb, last8), 0)),
                  lat(D), _full((DIN, D))],
        out_specs=[lat(D), _full((DIN, D)), _full((2, 2, D)), _full((1, D))],
        out_shape=[jax.ShapeDtypeStruct((s_len, D), F32), jax.ShapeDtypeStruct((DIN, D), F32),
                   jax.ShapeDtypeStruct((2, 2, D), F32), jax.ShapeDtypeStruct((1, D), F32)],
        compiler_params=_params(("arbitrary",)),
    )(ctx, x, modsel, norm_g, dlo, dga, dgp, dpool, dpool, dpool, dxn, w_in_t)


def _adamw_update(w_ref, g_ref, m_ref, v_ref, d_ref, mo_ref, vo_ref):
    gv = g_ref[...]
    mn = ADAM_B1 * m_ref[...] + (1.0 - ADAM_B1) * gv
    vn = ADAM_B2 * v_ref[...] + (1.0 - ADAM_B2) * (gv * gv)
    m_hat = mn / (1.0 - ADAM_B1 ** ADAM_STEP)
    v_hat = vn / (1.0 - ADAM_B2 ** ADAM_STEP)
    d_ref[...] = -ADAM_LR * (m_hat / (jnp.sqrt(v_hat) + ADAM_EPS) + ADAM_WD * w_ref[...])
    mo_ref[...] = mn
    vo_ref[...] = vn


def _adamw_many(ws, gs, ms, vs):
    n = len(ws)

    def body(*refs):
        for i in range(n):
            _adamw_update(refs[i], refs[n + i], refs[2 * n + i], refs[3 * n + i], refs[4 * n + i], refs[5 * n + i], refs[6 * n + i])

    def spec(w):
        rows, cols = w.shape
        return pl.BlockSpec((rows // 2, cols), lambda i: (i, 0)) if rows % 16 == 0 else _full((rows, cols))

    specs = [spec(w) for w in ws]
    shp = [jax.ShapeDtypeStruct(w.shape, F32) for w in ws]
    out = pl.pallas_call(body, name="adamw_many", grid=(2,), in_specs=specs * 4, out_specs=specs * 3, out_shape=shp * 3,
                         compiler_params=_params(("arbitrary",)))(*ws, *gs, *ms, *vs)
    return out[:n], out[n:2 * n], out[2 * n:]


def _adamw(w, g, m, v, name):
    rows, cols = w.shape
    rb = next(r for r in range(min(rows, 256), 0, -8) if rows % r == 0)

    def body(w_ref, g_ref, m_ref, v_ref, d_ref, mo_ref, vo_ref):
        _adamw_update(w_ref, g_ref, m_ref, v_ref, d_ref, mo_ref, vo_ref)

    spec = pl.BlockSpec((rb, cols), lambda i: (i, 0))
    shp = jax.ShapeDtypeStruct((rows, cols), F32)
    return pl.pallas_call(
        body, name=name, grid=(rows // rb,), in_specs=[spec] * 4, out_specs=[spec] * 3, out_shape=[shp] * 3,
        compiler_params=_params(("arbitrary",)),
    )(w, g, m, v)


class _Links:
    def __init__(self, send_sems, recv_sems):
        self.send_sems, self.recv_sems, self.sends = send_sems, recv_sems, []

    def send(self, src, dst, sem, to):
        cp = pltpu.make_async_remote_copy(src, dst, self.send_sems.at[sem], self.recv_sems.at[sem], device_id=to,
                                          device_id_type=MESH)
        cp.start()
        self.sends.append(cp)

    def arrived(self, dst, sem, frm):
        pltpu.make_async_remote_copy(dst, dst, self.send_sems.at[sem], self.recv_sems.at[sem], device_id=frm,
                                     device_id_type=MESH).wait_recv()

    def drain(self):
        for cp in self.sends:
            cp.wait_send()


def _half(ref, c, axis):
    size = ref.shape[axis - 2] // 2
    win = pl.ds(pl.multiple_of(c * size, 16 if axis == 0 else 128), size)
    idx = (win, slice(None)) if axis == 0 else (slice(None), win)
    return ref.at[(slice(None),) * (len(ref.shape) - 2) + idx]


def _select_rows(slots_ref, n_slots, row=0):
    sub = lax.broadcasted_iota(jnp.int32, (8, 1), 0)
    out = None
    for d in range(n_slots):
        r = jnp.where(sub == d, jnp.broadcast_to(slots_ref[d][row:row + 1, :], (8, slots_ref.shape[-1])), 0.0)
        out = r if out is None else out + r
    return out


def _gather(c, c_ctx, w_mod, b_mod_k, shards, axes):
    nw = len(shards)
    kw = w_mod.shape[1]

    def body(*refs):
        c_ref, cc_ref, wm_ref, b_ref = refs[:4]
        w_refs = refs[4:4 + nw]
        a16_ref, mod_ref = refs[4 + nw:6 + nw]
        g_refs = refs[6 + nw:6 + 2 * nw]
        a_ref, send_sems, recv_sems, local_sems = refs[6 + 2 * nw:]
        x, y, cc = lax.axis_index("x"), lax.axis_index("y"), lax.axis_index("c")
        me = 4 * x + 2 * y + cc
        k = 2 * x + y
        sibling = (x, y, 1 - cc)
        links = _Links(send_sems, recv_sems)
        chips = [_peer(x, y, cc, off + (0,)) for off in CHIPS3]
        locals_ = []
        for wi in range(nw):
            lc = pltpu.make_async_copy(w_refs[wi], g_refs[wi].at[k], local_sems.at[wi])
            lc.start()
            locals_.append(lc)
            for j, to in enumerate(chips):
                links.send(_half(w_refs[wi], cc, axes[wi]), _half(g_refs[wi].at[k], cc, axes[wi]), 10 + wi * 6 + j, to)
        cv = c_ref[...]
        a_ref[me] = jnp.broadcast_to(cv * _sig(cv), (8, D))
        for j, off in enumerate(PEERS7):
            links.send(a_ref.at[me], a_ref.at[me], j, _peer(x, y, cc, off))
        for j, off in enumerate(PEERS7):
            px, py, pc = _peer(x, y, cc, off)
            links.arrived(a_ref.at[4 * px + 2 * py + pc], j, (px, py, pc))
        ccv = cc_ref[...]
        sub = lax.broadcasted_iota(jnp.int32, (8, 1), 0)
        a16 = jnp.concatenate([_select_rows(a_ref, 8), jnp.where(sub == 0, jnp.broadcast_to(ccv * _sig(ccv), (8, D)), 0.0)], axis=0)
        a16_ref[...] = a16
        mod_ref[k] = _dot3(_nn, a16, wm_ref[...]) + b_ref[...]
        for j, to in enumerate(chips):
            links.send(mod_ref.at[k], mod_ref.at[k], 7 + j, to)
        for wi in range(nw):
            for j, (px, py, pc) in enumerate(chips):
                blk = _half(g_refs[wi].at[2 * px + py], cc, axes[wi])
                links.arrived(blk, 10 + wi * 6 + j, (px, py, pc))
                links.send(blk, blk, 10 + wi * 6 + 3 + j, sibling)
        for j, (px, py, pc) in enumerate(chips):
            links.arrived(mod_ref.at[2 * px + py], 7 + j, (px, py, pc))
        for wi in range(nw):
            for j, (px, py, pc) in enumerate(chips):
                links.arrived(_half(g_refs[wi].at[2 * px + py], 1 - cc, axes[wi]), 10 + wi * 6 + 3 + j, sibling)
        links.drain()
        for lc in locals_:
            lc.wait()

    nsem = 10 + 6 * nw
    return pl.pallas_call(
        body, name="gather", in_specs=[VM] * 4 + [ANY] * nw, out_specs=[VM, VM] + [ANY] * nw,
        out_shape=[jax.ShapeDtypeStruct((16, D), F32), jax.ShapeDtypeStruct((4, 16, kw), F32)]
        + [jax.ShapeDtypeStruct((4,) + s.shape, s.dtype) for s in shards],
        scratch_shapes=[pltpu.VMEM((8, 8, D), F32), pltpu.SemaphoreType.DMA((nsem,)), pltpu.SemaphoreType.DMA((nsem,)),
                        pltpu.SemaphoreType.DMA((nw,))],
        compiler_params=pltpu.CompilerParams(vmem_limit_bytes=VMEM_LIMIT),
    )(c, c_ctx, w_mod, b_mod_k, *shards)


SMALL_ROW_WIDTHS = (D, QL, KVL, DKP, DKP, 512, 128)
SMALL_OUT_WIDTHS = (D, QL, KVL, DK, DK, 512, 1)


def _reduce(grads, axes, smalls, w_pool_g, dmod8, a16, w_mod, c_ctx):
    nw = len(grads)
    ns = len(smalls)
    kw = w_mod.shape[1]
    halves = []
    for g, ax in zip(grads, axes):
        halves.append((g.shape[1] // 2, g.shape[2]) if ax == 0 else (g.shape[1], g.shape[2] // 2))

    def body(*refs):
        g_refs = refs[:nw]
        small_refs = refs[nw:nw + ns]
        wp_ref, dm_ref, a16_ref, wm_ref, cc_ref = refs[nw + ns:nw + ns + 5]
        o = nw + ns + 5
        r_refs = refs[o:o + nw]
        small_outs = refs[o + nw:o + nw + ns]
        rwp_ref, gw_ref, gb_ref, gc_ref = refs[o + nw + ns:o + nw + ns + 4]
        o = o + nw + ns + 4
        own, sib, part, got = (refs[o + i * nw:o + (i + 1) * nw] for i in range(4))
        smbuf, wps, wpg, dm_all, pc_all, send_sems, recv_sems, local_sems = refs[o + 4 * nw:]
        x, y, cc = lax.axis_index("x"), lax.axis_index("y"), lax.axis_index("c")
        me = 4 * x + 2 * y + cc
        k = 2 * x + y
        sibling = (x, y, 1 - cc)
        links = _Links(send_sems, recv_sems)
        chips = [_peer(x, y, cc, off + (0,)) for off in CHIPS3]
        peers = [_peer(x, y, cc, off) for off in PEERS7]
        big, sm0, wp0, dm0, pc0 = 0, 5 * nw, 5 * nw + 7, 5 * nw + 14, 5 * nw + 21

        locals_ = []
        for wi in range(nw):
            lc = pltpu.make_async_copy(_half(g_refs[wi], cc, axes[wi]), own[wi], local_sems.at[wi])
            lc.start()
            locals_.append(lc)
            links.send(_half(g_refs[wi], 1 - cc, axes[wi]), sib[wi], big + wi * 5, sibling)
        slot = smbuf.at[me]
        slot[...] = jnp.zeros((8, D), F32)
        for r, (ref, w) in enumerate(zip(small_refs, SMALL_ROW_WIDTHS)):
            slot[r:r + 1, 0:w] = jnp.broadcast_to(ref[...], (1, w))
        links.send(wp_ref, wps, wp0, sibling)
        dm_all[me] = dm_ref[...]
        for j, peer in enumerate(peers):
            links.send(dm_all.at[me], dm_all.at[me], dm0 + j, peer)
            links.send(smbuf.at[me], smbuf.at[me], sm0 + j, peer)
        links.arrived(wps, wp0, sibling)
        wpg[k] = (wp_ref[...] + wps[...]).astype(BF16)
        for j, to in enumerate(chips):
            links.send(wpg.at[k], wpg.at[k], wp0 + 1 + j, to)
        for wi in range(nw):
            locals_[wi].wait()
            links.arrived(sib[wi], big + wi * 5, sibling)
            part[wi][...] = (own[wi][...] + sib[wi][...]).astype(BF16)
            got[wi][k] = part[wi][k]
            for j, (px, py, pc) in enumerate(chips):
                links.send(part[wi].at[2 * px + py], got[wi].at[k], big + wi * 5 + 1 + j, (px, py, pc))
        for j, (px, py, pc) in enumerate(peers):
            links.arrived(dm_all.at[4 * px + 2 * py + pc], dm0 + j, (px, py, pc))
        dmc = dm_all[0][1:2, :]
        dml = dm_all[0][0:1, :]
        for d in range(1, 8):
            dmc = dmc + dm_all[d][1:2, :]
            dml = dml + dm_all[d][0:1, :]
        gb_ref[...] = dml + dmc
        sub = lax.broadcasted_iota(jnp.int32, (8, 1), 0)
        b16 = jnp.concatenate([_select_rows(dm_all, 8), jnp.where(sub == 0, jnp.broadcast_to(dmc, (8, 3 * D)), 0.0)], axis=0)
        bk = jnp.zeros((16, kw), F32)
        for kk in range(4):
            bk = bk + jnp.where(k == kk, b16[:, kk * kw:(kk + 1) * kw], 0.0)
        gw_ref[...] = _dot3(_tn, a16_ref[...], bk)
        pc_all[k] = _dot3(_nt, jnp.broadcast_to(bk[8:9, :], (8, kw)), wm_ref[...])
        for j, to in enumerate(chips):
            links.send(pc_all.at[k], pc_all.at[k], pc0 + j, to)
        for j, (px, py, pc) in enumerate(peers):
            links.arrived(smbuf.at[4 * px + 2 * py + pc], sm0 + j, (px, py, pc))
        tot = smbuf[0]
        for d in range(1, 8):
            tot = tot + smbuf[d]
        for r, (ref, w) in enumerate(zip(small_outs, SMALL_OUT_WIDTHS)):
            ref[...] = tot[r:r + 1, 0:w]
        for j, (px, py, pc) in enumerate(chips):
            links.arrived(wpg.at[2 * px + py], wp0 + 1 + j, (px, py, pc))
        wpt = wpg[0].astype(F32)
        for kk in range(1, 4):
            wpt = wpt + wpg[kk].astype(F32)
        rwp_ref[...] = wpt
        for wi in range(nw):
            for j, (px, py, pc) in enumerate(chips):
                links.arrived(got[wi].at[2 * px + py], big + wi * 5 + 1 + j, (px, py, pc))
            total = got[wi][0].astype(F32)
            for kk in range(1, 4):
                total = total + got[wi][kk].astype(F32)
            mine = _half(r_refs[wi], cc, axes[wi])
            mine[...] = total
            links.send(mine, mine, big + wi * 5 + 4, sibling)
        for j, (px, py, pc) in enumerate(chips):
            links.arrived(pc_all.at[2 * px + py], pc0 + j, (px, py, pc))
        ccv = cc_ref[...]
        sg = _sig(ccv)
        gc_ref[...] = (pc_all[0][0:1, :] + pc_all[1][0:1, :] + pc_all[2][0:1, :] + pc_all[3][0:1, :]) * (sg * (1.0 + ccv * (1.0 - sg)))
        for wi in range(nw):
            links.arrived(_half(r_refs[wi], 1 - cc, axes[wi]), big + wi * 5 + 4, sibling)
        links.drain()

    nsem = 5 * nw + 24
    quads = [(4,) + h for h in halves]
    return pl.pallas_call(
        body, name="reduce", in_specs=[ANY] * nw + [VM] * (ns + 5), out_specs=[VM] * (nw + ns + 4),
        out_shape=[jax.ShapeDtypeStruct(g.shape[1:], F32) for g in grads]
        + [jax.ShapeDtypeStruct((1, w), F32) for w in SMALL_OUT_WIDTHS]
        + [jax.ShapeDtypeStruct(w_pool_g.shape, F32), jax.ShapeDtypeStruct((D, kw), F32), jax.ShapeDtypeStruct((1, 3 * D), F32),
           jax.ShapeDtypeStruct((1, D), F32)],
        scratch_shapes=[pltpu.VMEM(q, F32) for q in quads] + [pltpu.VMEM(q, F32) for q in quads]
        + [pltpu.VMEM(q, BF16) for q in quads] + [pltpu.VMEM(q, BF16) for q in quads]
        + [pltpu.VMEM((8, 8, D), F32), pltpu.VMEM(w_pool_g.shape, F32), pltpu.VMEM((4,) + w_pool_g.shape, BF16),
           pltpu.VMEM((8, 8, 3 * D), F32),
           pltpu.VMEM((4, 8, D), F32)]
        + [pltpu.SemaphoreType.DMA((nsem,)), pltpu.SemaphoreType.DMA((nsem,)), pltpu.SemaphoreType.DMA((nw,))],
        compiler_params=pltpu.CompilerParams(vmem_limit_bytes=VMEM_LIMIT),
    )(*grads, *smalls, w_pool_g, dmod8, a16, w_mod, c_ctx)


def _rope_tables(s_len):
    rows = s_len // GRID_W
    per = TB // GRID_W
    n_freq = 16
    inv = ROPE_BASE ** (-jnp.arange(n_freq, dtype=F32) / n_freq)
    ang_r = jnp.arange(rows, dtype=F32)[:, None] * inv
    ang_c = jnp.arange(GRID_W, dtype=F32)[:, None] * inv
    by_row, by_col = [], []
    for fn, pad in ((jnp.cos, 1.0), (jnp.sin, 0.0)):
        r = jnp.concatenate([fn(ang_r), fn(ang_r), jnp.zeros((rows, 96), F32)], axis=1).reshape(rows // per, per, 128)
        by_row.append(jnp.pad(r, ((0, 0), (0, 8 - per), (0, 0))))
        cpart = jnp.concatenate([jnp.zeros((GRID_W, 32), F32), fn(ang_c), fn(ang_c), jnp.full((GRID_W, 64), pad, F32)], axis=1)
        by_col.append(jnp.tile(cpart, (per, 1)))
    return jnp.concatenate(by_row, axis=-1), jnp.concatenate(by_col, axis=-1)


def kernel(x, c, ctx, c_ctx, w_mod, b_mod, norm_g, w_in, q_lora_g, w_uq, kv_lora_g, w_ukv, q_norm_g, k_norm_g, w_pool, pool_scale, w_out, loss_target, m_c_ctx, m_w_mod, m_b_mod, m_norm_g, m_w_in, m_q_lora_g, m_w_uq, m_kv_lora_g, m_w_ukv, m_q_norm_g, m_k_norm_g, m_w_pool, m_pool_scale, m_w_out, v_c_ctx, v_w_mod, v_b_mod, v_norm_g, v_w_in, v_q_lora_g, v_w_uq, v_kv_lora_g, v_w_ukv, v_q_norm_g, v_k_norm_g, v_w_pool, v_pool_scale, v_w_out):
    xi, yi, ci = lax.axis_index("x"), lax.axis_index("y"), lax.axis_index("c")
    me = 4 * xi + 2 * yi + ci
    k = 2 * xi + yi
    s_len = x.shape[1]
    lc = ctx.shape[1]
    kw = w_mod.shape[2]
    weights = dict(c_ctx=c_ctx, w_mod=w_mod, b_mod=b_mod, norm_g=norm_g, w_in=w_in, q_lora_g=q_lora_g, w_uq=w_uq,
                   kv_lora_g=kv_lora_g, w_ukv=w_ukv, q_norm_g=q_norm_g, k_norm_g=k_norm_g, w_pool=w_pool,
                   pool_scale=pool_scale, w_out=w_out)
    m_in = dict(c_ctx=m_c_ctx, w_mod=m_w_mod, b_mod=m_b_mod, norm_g=m_norm_g, w_in=m_w_in, q_lora_g=m_q_lora_g, w_uq=m_w_uq,
                kv_lora_g=m_kv_lora_g, w_ukv=m_w_ukv, q_norm_g=m_q_norm_g, k_norm_g=m_k_norm_g, w_pool=m_w_pool,
                pool_scale=m_pool_scale, w_out=m_w_out)
    v_in = dict(c_ctx=v_c_ctx, w_mod=v_w_mod, b_mod=v_b_mod, norm_g=v_norm_g, w_in=v_w_in, q_lora_g=v_q_lora_g, w_uq=v_w_uq,
                kv_lora_g=v_kv_lora_g, w_ukv=v_w_ukv, q_norm_g=v_q_norm_g, k_norm_g=v_k_norm_g, w_pool=v_w_pool,
                pool_scale=v_pool_scale, w_out=v_w_out)
    order = ["c_ctx", "w_mod", "b_mod", "norm_g", "w_in", "q_lora_g", "w_uq", "kv_lora_g", "w_ukv", "q_norm_g", "k_norm_g",
             "w_pool", "pool_scale", "w_out"]
    transposed = ("w_in", "w_uq")
    as2d = lambda n, a: jnp.transpose(a[0]) if n in transposed else a.reshape(-1, a.shape[-1])
    back = lambda n, a: jnp.transpose(a)[None] if n in transposed else a.reshape(weights[n].shape)

    c_ctx2 = c_ctx.reshape(1, D)
    b_mod_k = lax.dynamic_slice(b_mod, (0, k * kw), (1, kw))
    split = (1, 0, 0, 0)
    a16, mod_all, g_in, g_uq, g_ukv, g_out = _gather(
        c, c_ctx2, w_mod[0], b_mod_k,
        [as2d("w_in", w_in).astype(BF16), as2d("w_uq", w_uq).astype(BF16), w_ukv[0].astype(BF16), w_out[0].astype(BF16)], split)
    mod_me = lax.dynamic_index_in_dim(mod_all, me, axis=1, keepdims=False).reshape(3, D)
    mod_c = mod_all[:, 8, :].reshape(3, D)
    modsel = jnp.stack([mod_c, mod_me])
    w_in_t = g_in.reshape(DIN, D)
    w_uq_t = jnp.pad(g_uq, ((0, 0), (0, DKP - DK), (0, 0)))
    w_out_f = g_out.reshape(D, D)
    qn_g = jnp.pad(q_norm_g, ((0, 0), (0, DKP - DK)))
    kn_g = jnp.pad(k_norm_g, ((0, 0), (0, DKP - DK)))
    w_pool_b = w_pool[0].astype(BF16)
    cos, sin = _rope_tables(s_len)

    u, q, kk, v = _fwd_in(ctx[0], x[0], modsel, norm_g, w_in_t, q_lora_g, w_uq_t, kv_lora_g, g_ukv, qn_g, kn_g, cos, sin)
    attn, lse = _attn_fwd(q, kk, v, s_len)
    (dxn, dattn, dga, dgp, dpool, dw_out, dgate, dps, dw_pool, loss) = _out_stage(
        attn.reshape(s_len // Q_BLOCK, Q_BLOCK, NH * DV), u, x[0], loss_target[0], modsel[1, 2:3, :], w_pool_b, pool_scale,
        w_out_f, lc)
    dattn = dattn.reshape(s_len, NH * DV)
    dq, dk, dv = _attn_bwd(q, kk, v, dattn, attn, lse, s_len)
    dlo, dw_uq_t, dw_ukv, dqlg, dkvlg, dqng, dkng = _qkv_bwd(u, dq, dk, dv, cos, sin, q_lora_g, w_uq_t, kv_lora_g, g_ukv,
                                                            qn_g, kn_g, s_len)
    gx, dw_in_t, dmod, dng = _in_bwd(ctx[0], x[0], modsel, norm_g, dlo, dga, dgp, dpool, dxn, w_in_t)

    dmod_l = jnp.concatenate([dmod[1, 0], dmod[1, 1], dgate[0]]).reshape(1, 3 * D)
    dmod_c = jnp.concatenate([dmod[0, 0], dmod[0, 1], jnp.zeros((D,), F32)]).reshape(1, 3 * D)
    dmod8 = jnp.concatenate([dmod_l, dmod_c, jnp.zeros((6, 3 * D), F32)], axis=0)
    (r_in, r_uq, r_ukv, r_out, g_ng, g_qlg, g_kvlg, g_qng, g_kng, g_ps, loss_all, g_wp, g_w_mod, g_b_mod, g_c_ctx) = _reduce(
        [dw_in_t.reshape(4, DIN // 4, D), dw_uq_t, dw_ukv, dw_out.reshape(4, D // 4, D)], split,
        [dng, dqlg, dkvlg, dqng, dkng, dps, loss], dw_pool, dmod8, a16, w_mod[0], c_ctx2)
    g2d = dict(c_ctx=g_c_ctx, b_mod=g_b_mod, w_mod=g_w_mod, w_in=r_in, w_uq=r_uq, w_ukv=r_ukv, w_out=r_out, norm_g=g_ng,
               q_lora_g=g_qlg, kv_lora_g=g_kvlg, q_norm_g=g_qng, k_norm_g=g_kng, pool_scale=g_ps, w_pool=g_wp.reshape(512, 128))

    d2d, m2d, v2d = {}, {}, {}
    d2d["w_mod"], m2d["w_mod"], v2d["w_mod"] = _adamw(as2d("w_mod", w_mod), g2d["w_mod"], as2d("w_mod", m_w_mod),
                                                      as2d("w_mod", v_w_mod), "adamw_w_mod")
    rest = [n for n in order if n != "w_mod"]
    outs = _adamw_many([as2d(n, weights[n]) for n in rest], [g2d[n] for n in rest], [as2d(n, m_in[n]) for n in rest],
                       [as2d(n, v_in[n]) for n in rest])
    for dst, arrs in zip((d2d, m2d, v2d), outs):
        dst.update(dict(zip(rest, arrs)))

    return (loss_all[0, 0], gx[None], *[back(n, g2d[n]) for n in order], *[back(n, d2d[n]) for n in order],
            *[back(n, m2d[n]) for n in order], *[back(n, v2d[n]) for n in order])
```

```python
import jax
import jax.numpy as jnp
from jax import lax
from jax.experimental import pallas as pl
from jax.experimental.pallas import tpu as pltpu

F32 = jnp.float32
BF16 = jnp.bfloat16
MESH = pl.DeviceIdType.MESH

D = 1024
NH = 4
DK = 192
DKP = 256
DV = 128
QL = 256
KVL = 128
DIN = 1984
U_LO = 448
SEG = ((0, 512), (448, 960), (960, 1472), (1472, 1984))
DU = 2048
POOL_WINDOWS = (2, 4, 8, 16)
HALO = 8
EPS = 1e-6
ROPE_BASE = 10000.0
GRID_W = 64
Q_BLOCK = 128
TB = 256
BWD_QBLOCKS = 1
SCALE = DK ** -0.5
LOG2E = 1.4426950408889634
LN2 = 0.6931471805599453
VMEM_LIMIT = 56 * 1024 * 1024

ADAM_LR = 0.001
ADAM_B1 = 0.9
ADAM_B2 = 0.999
ADAM_EPS = 1e-08
ADAM_WD = 0.01
ADAM_STEP = 10

CHIPS3 = ((1, 0), (0, 1), (1, 1))
PEERS7 = tuple((dx, dy, dc) for dx in (0, 1) for dy in (0, 1) for dc in (0, 1) if (dx, dy, dc) != (0, 0, 0))

VM = pl.BlockSpec(memory_space=pltpu.VMEM)
ANY = pl.BlockSpec(memory_space=pl.ANY)


def _nn(a, b):
    return jnp.dot(a, b, preferred_element_type=F32)


def _nt(a, b):
    return lax.dot_general(a, b, (((1,), (1,)), ((), ())), preferred_element_type=F32)


def _tn(a, b):
    return lax.dot_general(a, b, (((0,), (0,)), ((), ())), preferred_element_type=F32)


def _split3(a):
    a0 = a.astype(BF16)
    r = a - a0.astype(F32)
    a1 = r.astype(BF16)
    a2 = (r - a1.astype(F32)).astype(BF16)
    return a0, a1, a2


def _dot3(dot, a, b):
    sa = _split3(a)
    sb = _split3(b)
    out = None
    for i in range(3):
        for j in range(3 - i):
            t = dot(sa[i], sb[j])
            out = t if out is None else out + t
    return out


def _sig(x):
    return 1.0 / (1.0 + jnp.exp(-x))


def _rot(t):
    lane = lax.broadcasted_iota(jnp.int32, t.shape, 1)
    return jnp.where((lane % 32) < 16, -pltpu.roll(t, 112, 1), pltpu.roll(t, 16, 1))


def _rope(t, cos, sin):
    return t * cos + _rot(t) * sin


def _rope_t(t, cos, sin):
    return t * cos - _rot(t * sin)


def _rope_block(rows_ref, cols_ref, is_ctx):
    lane = lax.broadcasted_iota(jnp.int32, (TB, 256), 1) % 128
    rows = jnp.concatenate([jnp.broadcast_to(rows_ref[0, r:r + 1, :], (GRID_W, 256)) for r in range(TB // GRID_W)], axis=0)
    cs = jnp.where(lane < 32, rows, cols_ref[...])
    return jnp.where(is_ctx, 1.0, cs[:, :128]), jnp.where(is_ctx, 0.0, cs[:, 128:])


def _shift_rows(z, k):
    n = z.shape[0]
    return pltpu.roll(z, (n - k) % n, 0)


def _colsum(a):
    return jnp.sum(a, axis=0, keepdims=True)


def _rowsum(a):
    return jnp.sum(a, axis=-1, keepdims=True)


def _row_layout(col):
    return jnp.transpose(jnp.broadcast_to(col, (col.shape[0], 128)))[0:8, :]


def _params(sem=None):
    return pltpu.CompilerParams(dimension_semantics=sem, vmem_limit_bytes=VMEM_LIMIT)


def _full(shape):
    nd = len(shape)
    return pl.BlockSpec(shape, lambda *_: (0,) * nd)


def _peer(x, y, c, off):
    dx, dy, dc = off
    return ((x + dx) % 2, (y + dy) % 2, (c + dc) % 2)


def _token_specs(off):
    ctx = pl.BlockSpec((TB, D), lambda i: (jnp.minimum(i, off - 1), 0))
    lat = pl.BlockSpec((TB, D), lambda i: (jnp.maximum(i - off, 0), 0))
    mod = pl.BlockSpec((1, 3, D), lambda i: (jnp.minimum(i // off, 1), 0, 0))
    return ctx, lat, mod


def _modulated(x, mod_ref, ng):
    shift = mod_ref[0, 0:1, :]
    scale = mod_ref[0, 1:2, :]
    r = lax.rsqrt(jnp.mean(x * x, axis=-1, keepdims=True) + EPS)
    xh = x * r
    xg = xh * ng
    return r, xh, xg, xg * (1.0 + scale) + shift, scale


def _fwd_in(ctx, x, modsel, norm_g, w_in_t, q_lora_g, w_uq_t, kv_lora_g, w_ukv, qn_g, kn_g, cos, sin):
    s_len, lc = x.shape[0], ctx.shape[0]
    t_all = s_len + lc
    nb = t_all // TB
    off = lc // TB

    def body(ctx_ref, x_ref, mod_ref, ng_ref, win_ref, qlg_ref, wuq_ref, kvlg_ref, wukv_ref, qng_ref, kng_ref, cos_ref, sin_ref,
             u_ref, q_ref, k_ref, v_ref):
        is_ctx = pl.program_id(0) < off
        xb = jnp.where(is_ctx, ctx_ref[...], x_ref[...])
        _, _, _, h, _ = _modulated(xb, mod_ref, ng_ref[...])
        hb = h.astype(BF16)
        lane = lax.broadcasted_iota(jnp.int32, (TB, 512), 1)
        ulo = jnp.where(lane < U_LO, _nt(hb, win_ref[SEG[0][0]:SEG[0][1], :]), 0.0)
        u_ref[:, 0:512] = ulo
        for j in range(1, 4):
            u_ref[:, j * 512:(j + 1) * 512] = _nt(hb, win_ref[SEG[j][0]:SEG[j][1], :])
        cos, sin = _rope_block(cos_ref, sin_ref, is_ctx)
        cq = ulo[:, 0:QL]
        cqn = (cq * lax.rsqrt(jnp.mean(cq * cq, axis=-1, keepdims=True) + EPS) * qlg_ref[...]).astype(BF16)
        qng = qng_ref[...]
        for hd in range(NH):
            qh = _nt(cqn, wuq_ref[hd])
            qn = qh * lax.rsqrt(_rowsum(qh * qh) / DK + EPS) * qng
            q_ref[hd] = (jnp.concatenate([qn[:, :128], _rope(qn[:, 128:], cos, sin)], axis=1) * (SCALE * LOG2E)).astype(BF16)
        ckv = ulo[:, QL:QL + KVL]
        ckvn = (ckv * lax.rsqrt(jnp.mean(ckv * ckv, axis=-1, keepdims=True) + EPS) * kvlg_ref[...]).astype(BF16)
        kr = ulo[:, 384:512]
        skr = _rowsum(kr * kr)
        kng = kng_ref[...]
        kr_roped = _rope(kr * kng[:, 128:], cos, sin)
        for hd in range(NH):
            kv = _nn(ckvn, wukv_ref[hd])
            kn = kv[:, :128]
            rk = lax.rsqrt((_rowsum(kn * kn) + skr) / DK + EPS)
            k_ref[hd] = jnp.concatenate([kn * rk * kng[:, :128], kr_roped * rk], axis=1).astype(BF16)
            v_ref[hd] = kv[:, 128:].astype(BF16)

    row = lambda w: pl.BlockSpec((TB, w), lambda i: (i, 0))
    heads = lambda w: pl.BlockSpec((NH, TB, w), lambda i: (0, i, 0))
    cspec, xspec, mspec = _token_specs(off)
    return pl.pallas_call(
        body, name="fwd_in", grid=(nb,),
        in_specs=[cspec, xspec, mspec, _full((1, D)), _full((DIN, D)), _full((1, QL)), _full((NH, DKP, QL)), _full((1, KVL)),
                  _full((NH, KVL, 256)), _full((1, DKP)), _full((1, DKP)),
                  pl.BlockSpec((1, 8, 256), lambda i: (jnp.maximum(i - off, 0), 0, 0)), _full((TB, 256))],
        out_specs=[row(DU), heads(DKP), heads(DKP), heads(DV)],
        out_shape=[jax.ShapeDtypeStruct((t_all, DU), F32), jax.ShapeDtypeStruct((NH, t_all, DKP), BF16),
                   jax.ShapeDtypeStruct((NH, t_all, DKP), BF16), jax.ShapeDtypeStruct((NH, t_all, DV), BF16)],
        compiler_params=_params(("arbitrary",)),
    )(ctx, x, modsel, norm_g, w_in_t, q_lora_g, w_uq_t, kv_lora_g, w_ukv, qn_g, kn_g, cos, sin)


def _attn_fwd(q, k, v, s_len):
    t_all = q.shape[1]
    off = (t_all - s_len) // TB
    nq = s_len // TB
    nsub = next(n for n in (4, 2, 1) if nq % n == 0)

    def body(*refs):
        q_refs = refs[:nsub]
        k_ref, v_ref, o_ref, lse_ref = refs[nsub:]
        for sb in range(nsub):
            s = _nt(q_refs[sb][0], k_ref[0])
            m = jnp.max(s, axis=-1, keepdims=True)
            e = jnp.exp2(s - m)
            l = _rowsum(e)
            o_ref[sb * TB:(sb + 1) * TB, :] = _nn(e.astype(BF16), v_ref[0]) / l
            lse_ref[0, sb] = _row_layout(m + jnp.log2(l))

    qspec = lambda sb: pl.BlockSpec((1, TB, DKP), lambda h, i: (h, i * nsub + sb + off, 0))
    return pl.pallas_call(
        body, name="attn_fwd", grid=(NH, nq // nsub),
        in_specs=[qspec(sb) for sb in range(nsub)]
        + [pl.BlockSpec((1, t_all, DKP), lambda h, i: (h, 0, 0)), pl.BlockSpec((1, t_all, DV), lambda h, i: (h, 0, 0))],
        out_specs=[pl.BlockSpec((nsub * TB, DV), lambda h, i: (i, h)), pl.BlockSpec((1, nsub, 8, TB), lambda h, i: (h, i, 0, 0))],
        out_shape=[jax.ShapeDtypeStruct((s_len, NH * DV), F32), jax.ShapeDtypeStruct((NH, nq, 8, TB), F32)],
        compiler_params=_params(("arbitrary", "arbitrary")),
    )(*([q] * nsub), k, v)


def _out_stage(attn, u, x, target, gate, w_pool, pool_scale, w_out, lc):
    s_len = x.shape[0]
    t_all = s_len + lc
    off = lc // TB
    nq = s_len // TB
    hb = TB // HALO
    nqb = s_len // Q_BLOCK
    jb = TB // nqb

    def body(attn_ref, ga_ref, pin_ref, pprev_ref, pnext_ref, gp_ref, x_ref, tgt_ref, gate_ref, wp_ref, ps_ref, wo_ref,
             dxn_ref, dattn_ref, dga_ref, dgp_ref, dpool_ref, dwo_ref, dgate_ref, dps_ref, dwp_ref, loss_ref):
        i = pl.program_id(0)

        @pl.when(i == 0)
        def _():
            dwo_ref[...] = jnp.zeros_like(dwo_ref)
            dgate_ref[...] = jnp.zeros_like(dgate_ref)
            dps_ref[...] = jnp.zeros_like(dps_ref)
            dwp_ref[...] = jnp.zeros_like(dwp_ref)
            loss_ref[...] = jnp.zeros_like(loss_ref)

        attn = jnp.concatenate([attn_ref[:, jj, :] for jj in range(jb)], axis=0)
        ga = ga_ref[...]
        gp = gp_ref[...]
        pin = pin_ref[...]
        prev = jnp.where(i == 0, 0.0, pprev_ref[...])
        nxt = jnp.where(i == nq - 1, 0.0, pnext_ref[...])
        win = jnp.concatenate([prev, pin, nxt], axis=0)
        tg = i * TB + lax.broadcasted_iota(jnp.int32, (TB, 1), 0)
        pooled = []
        for g, w in enumerate(POOL_WINDOWS):
            a = win[:, g * 128:(g + 1) * 128]
            p = _shift_rows(a, -1) + a
            for step in (1, 2, 4):
                if w >= 4 * step:
                    p = _shift_rows(p, -step) + _shift_rows(p, step)
            cnt = (jnp.minimum(tg + w // 2, s_len) - jnp.maximum(tg - w // 2, 0)).astype(F32)
            pooled.append(p[HALO:HALO + TB] / cnt - a[HALO:HALO + TB])
        pooled_b = [p.astype(BF16) for p in pooled]
        z = jnp.concatenate([_nn(pooled_b[g], wp_ref[g]) for g in range(4)], axis=1)
        ps = ps_ref[...]
        yp = z * ps
        sga = _sig(ga)
        sila = ga * sga
        sgp = _sig(gp)
        silp = gp * sgp
        br = jnp.concatenate([sila * attn, silp * yp], axis=1).astype(BF16)
        y = _nn(br, wo_ref[...])
        gate = gate_ref[...]
        err = x_ref[...] + gate * y - tgt_ref[...]
        loss_ref[...] += _colsum(_rowsum(err * err)) * (0.5 / D)
        dxn = err * (1.0 / D)
        dxn_ref[...] = dxn
        dgate_ref[...] += _colsum(dxn * y)
        dy = (dxn * gate).astype(BF16)
        dwo_ref[...] += _tn(br, dy)
        dbr = _nt(dy, wo_ref[...])
        dbra = dbr[:, :512]
        dbrp = dbr[:, 512:]
        dattn = dbra * sila
        for jj in range(jb):
            dattn_ref[:, jj, :] = dattn[jj * nqb:(jj + 1) * nqb]
        dga_ref[...] = (dbra * attn * (sga * (1.0 + ga * (1.0 - sga)))).astype(BF16)
        dgp_ref[...] = (dbrp * yp * (sgp * (1.0 + gp * (1.0 - sgp)))).astype(BF16)
        dyp = dbrp * silp
        dps_ref[...] += _colsum(dyp * z)
        dz = (dyp * ps).astype(BF16)
        dpool = []
        for g in range(4):
            dzg = dz[:, g * 128:(g + 1) * 128]
            dwp_ref[g] += _tn(pooled_b[g], dzg)
            dpool.append(_nt(dzg, wp_ref[g]))
        dpool_ref[...] = jnp.concatenate(dpool, axis=1)

    lat = lambda w: pl.BlockSpec((TB, w), lambda i: (i, 0))
    perm = pl.BlockSpec((nqb, jb, 512), lambda i: (0, i, 0))
    ucol = lambda j: pl.BlockSpec((TB, 512), lambda i: (i + off, j))
    last8 = t_all // HALO - 1
    return pl.pallas_call(
        body, name="out_stage", grid=(nq,),
        in_specs=[perm, ucol(1), ucol(2),
                  pl.BlockSpec((HALO, 512), lambda i: ((i + off) * hb - 1, 2)),
                  pl.BlockSpec((HALO, 512), lambda i: (jnp.minimum((i + off + 1) * hb, last8), 2)),
                  ucol(3), lat(D), lat(D), _full((1, D)), _full((4, 128, 128)), _full((1, 512)), _full((D, D))],
        out_specs=[lat(D), perm, lat(512), lat(512), lat(512),
                   _full((D, D)), _full((1, D)), _full((1, 512)), _full((4, 128, 128)), _full((1, 1))],
        out_shape=[jax.ShapeDtypeStruct((s_len, D), F32), jax.ShapeDtypeStruct((nqb, Q_BLOCK, 512), F32),
                   jax.ShapeDtypeStruct((s_len, 512), BF16), jax.ShapeDtypeStruct((s_len, 512), BF16),
                   jax.ShapeDtypeStruct((s_len, 512), F32),
                   jax.ShapeDtypeStruct((D, D), F32), jax.ShapeDtypeStruct((1, D), F32), jax.ShapeDtypeStruct((1, 512), F32),
                   jax.ShapeDtypeStruct((4, 128, 128), F32), jax.ShapeDtypeStruct((1, 1), F32)],
        compiler_params=_params(("arbitrary",)),
    )(attn, u, u, u, u, u, x, target, gate, w_pool, pool_scale, w_out)


def _attn_bwd(q, k, v, dattn, attn, lse, s_len):
    t_all = q.shape[1]
    off = (t_all - s_len) // TB
    nq = s_len // TB
    nch = 4
    chunks = [(c * (t_all // nch), t_all // nch) for c in range(nch)]
    nsub = next(n for n in (BWD_QBLOCKS, 2, 1) if nq % n == 0)
    tq = nsub * TB

    def body(*refs):
        q_refs = refs[:nsub]
        k_ref, v_ref, do_ref, o_ref, lse_ref, dq_ref, dk_ref, dv_ref = refs[nsub:]
        i = pl.program_id(1)

        @pl.when(i == 0)
        def _():
            dk_ref[...] = jnp.zeros_like(dk_ref)
            dv_ref[...] = jnp.zeros_like(dv_ref)

        qb = jnp.concatenate([r[0] for r in q_refs], axis=0)
        delta_r = _row_layout(_rowsum(do_ref[...] * o_ref[...]))[0:1, :]
        do = do_ref[...].astype(BF16)
        lse_r = jnp.concatenate([lse_ref[0, sb][0:1, :] for sb in range(nsub)], axis=1)
        dq = jnp.zeros((tq, DKP), F32)
        for start, size in chunks:
            rows = pl.ds(start, size)
            kc = k_ref[0, rows, :]
            p_t = jnp.exp2(_nt(kc, qb) - lse_r)
            ds_t = (p_t * (_nt(v_ref[0, rows, :], do) - delta_r)).astype(BF16)
            dv_ref[0, rows, :] += _nn(p_t.astype(BF16), do)
            dk_ref[0, rows, :] += _nn(ds_t, qb)
            dq += _tn(ds_t, kc)
        dq_ref[0] = dq * SCALE

    kvspec = lambda w: pl.BlockSpec((1, t_all, w), lambda h, i: (h, 0, 0))
    rowspec = pl.BlockSpec((1, nsub, 8, TB), lambda h, i: (h, i, 0, 0))
    qspec = lambda sb: pl.BlockSpec((1, TB, DKP), lambda h, i: (h, i * nsub + sb + off, 0))
    return pl.pallas_call(
        body, name="attn_bwd", grid=(NH, nq // nsub),
        in_specs=[qspec(sb) for sb in range(nsub)]
        + [kvspec(DKP), kvspec(DV), pl.BlockSpec((tq, DV), lambda h, i: (i, h)), pl.BlockSpec((tq, DV), lambda h, i: (i, h)),
           rowspec],
        out_specs=[pl.BlockSpec((1, tq, DKP), lambda h, i: (h, i, 0)), kvspec(DKP), kvspec(DV)],
        out_shape=[jax.ShapeDtypeStruct((NH, s_len, DKP), F32), jax.ShapeDtypeStruct((NH, t_all, DKP), F32),
                   jax.ShapeDtypeStruct((NH, t_all, DV), F32)],
        compiler_params=_params(("arbitrary", "arbitrary")),
    )(*([q] * nsub), k, v, dattn, attn, lse)


def _qkv_bwd(u, dq, dk, dv, cos, sin, q_lora_g, w_uq_t, kv_lora_g, w_ukv, qn_g, kn_g, s_len):
    t_all = u.shape[0]
    off = (t_all - s_len) // TB
    nb = t_all // TB

    def body(ulo_ref, dq_ref, dk_ref, dv_ref, cos_ref, sin_ref, qlg_ref, wuq_ref, kvlg_ref, wukv_ref, qng_ref, kng_ref,
             dlo_ref, dwuq_ref, dwukv_ref, dqlg_ref, dkvlg_ref, dqng_ref, dkng_ref):
        i = pl.program_id(0)

        @pl.when(i == 0)
        def _():
            for r in (dwuq_ref, dwukv_ref, dqlg_ref, dkvlg_ref, dqng_ref, dkng_ref):
                r[...] = jnp.zeros_like(r)

        latent = i >= off
        ulo = ulo_ref[...]
        cos, sin = _rope_block(cos_ref, sin_ref, pl.program_id(0) < off)
        cq = ulo[:, 0:QL]
        rc = lax.rsqrt(jnp.mean(cq * cq, axis=-1, keepdims=True) + EPS)
        cqh = cq * rc
        qlg = qlg_ref[...]
        cqn_b = (cqh * qlg).astype(BF16)
        qng = qng_ref[...]
        dqng = jnp.zeros((1, DKP), F32)
        dcqn = jnp.zeros((TB, QL), F32)
        for hd in range(NH):
            qh = _nt(cqn_b, wuq_ref[hd])
            rq = lax.rsqrt(_rowsum(qh * qh) / DK + EPS)
            xh = qh * rq
            dqh = jnp.where(latent, dq_ref[hd], 0.0)
            dyq = jnp.concatenate([dqh[:, :128], _rope_t(dqh[:, 128:], cos, sin)], axis=1)
            dqng += _colsum(dyq * xh)
            dxh = dyq * qng
            dqraw = (rq * (dxh - xh * (_rowsum(dxh * xh) / DK))).astype(BF16)
            dwuq_ref[hd] += _tn(dqraw, cqn_b)[:DK]
            dcqn += _nn(dqraw, wuq_ref[hd])
        dqng_ref[...] += dqng
        dqlg_ref[...] += _colsum(dcqn * cqh)
        dxh = dcqn * qlg
        dcq = rc * (dxh - cqh * jnp.mean(dxh * cqh, axis=-1, keepdims=True))

        ckv = ulo[:, QL:QL + KVL]
        r0 = lax.rsqrt(jnp.mean(ckv * ckv, axis=-1, keepdims=True) + EPS)
        ckvh = ckv * r0
        kvlg = kvlg_ref[...]
        ckvn_b = (ckvh * kvlg).astype(BF16)
        kr = ulo[:, 384:512]
        skr = _rowsum(kr * kr)
        kng = kng_ref[...]
        kr_roped = _rope(kr * kng[:, 128:], cos, sin)
        e_sum = jnp.zeros((TB, 128), F32)
        pull = jnp.zeros((TB, 1), F32)
        dkng1 = jnp.zeros((1, 128), F32)
        dckvn = jnp.zeros((TB, KVL), F32)
        for hd in range(NH):
            kn = _nn(ckvn_b, wukv_ref[hd])[:, :128]
            rk = lax.rsqrt((_rowsum(kn * kn) + skr) / DK + EPS)
            xh1 = kn * rk
            dkh = dk_ref[hd] * LN2
            d1 = dkh[:, :128]
            e = dkh[:, 128:]
            dkng1 += _colsum(d1 * xh1)
            dx1 = d1 * kng[:, :128]
            dot = (_rowsum(dx1 * xh1) + rk * _rowsum(e * kr_roped)) / DK
            dkv = jnp.concatenate([rk * (dx1 - xh1 * dot), dv_ref[hd]], axis=1).astype(BF16)
            e_sum += e * rk
            pull += rk * rk * dot
            dwukv_ref[hd] += _tn(ckvn_b, dkv)
            dckvn += _nt(dkv, wukv_ref[hd])
        d2 = _rope_t(e_sum, cos, sin)
        dkr = kng[:, 128:] * d2 - kr * pull
        dkng_ref[...] += jnp.concatenate([dkng1, _colsum(d2 * kr)], axis=1)
        dkvlg_ref[...] += _colsum(dckvn * ckvh)
        dxh = dckvn * kvlg
        dckv = r0 * (dxh - ckvh * jnp.mean(dxh * ckvh, axis=-1, keepdims=True))
        dlo_ref[...] = jnp.concatenate([dcq, dckv, dkr], axis=1).astype(BF16)

    row = lambda w: pl.BlockSpec((TB, w), lambda i: (i, 0))
    heads = lambda w: pl.BlockSpec((NH, TB, w), lambda i: (0, i, 0))
    return pl.pallas_call(
        body, name="qkv_bwd", grid=(nb,),
        in_specs=[row(512), pl.BlockSpec((NH, TB, DKP), lambda i: (0, jnp.maximum(i - off, 0), 0)), heads(DKP), heads(DV),
                  pl.BlockSpec((1, 8, 256), lambda i: (jnp.maximum(i - off, 0), 0, 0)), _full((TB, 256)), _full((1, QL)), _full((NH, DKP, QL)), _full((1, KVL)), _full((NH, KVL, 256)),
                  _full((1, DKP)), _full((1, DKP))],
        out_specs=[row(512), _full((NH, DK, QL)), _full((NH, KVL, 256)), _full((1, QL)), _full((1, KVL)),
                   _full((1, DKP)), _full((1, DKP))],
        out_shape=[jax.ShapeDtypeStruct((t_all, 512), BF16), jax.ShapeDtypeStruct((NH, DK, QL), F32),
                   jax.ShapeDtypeStruct((NH, KVL, 256), F32), jax.ShapeDtypeStruct((1, QL), F32),
                   jax.ShapeDtypeStruct((1, KVL), F32), jax.ShapeDtypeStruct((1, DKP), F32), jax.ShapeDtypeStruct((1, DKP), F32)],
        compiler_params=_params(("arbitrary",)),
    )(u, dq, dk, dv, cos, sin, q_lora_g, w_uq_t, kv_lora_g, w_ukv, qn_g, kn_g)


def _in_bwd(ctx, x, modsel, norm_g, dlo, dga, dgp, dpool, dxn, w_in_t):
    s_len, lc = x.shape[0], ctx.shape[0]
    t_all = s_len + lc
    off = lc // TB
    nb = t_all // TB
    nq = s_len // TB
    hb = TB // HALO
    n = TB + 2 * HALO

    def body(ctx_ref, x_ref, mod_ref, ng_ref, dlo_ref, dga_ref, dgp_ref, dp_ref, dpprev_ref, dpnext_ref, dxn_ref, win_ref,
             gx_ref, dwin_ref, dmod_ref, dng_ref):
        i = pl.program_id(0)
        j = i - off

        @pl.when(i == 0)
        def _():
            dwin_ref[...] = jnp.zeros_like(dwin_ref)
            dmod_ref[...] = jnp.zeros_like(dmod_ref)
            dng_ref[...] = jnp.zeros_like(dng_ref)

        latent = i >= off
        dp = dp_ref[...]
        prev = jnp.where(j <= 0, 0.0, dpprev_ref[...])
        nxt = jnp.where(j >= nq - 1, 0.0, dpnext_ref[...])
        win = jnp.concatenate([prev, dp, nxt], axis=0)
        tg = j * TB - HALO + lax.broadcasted_iota(jnp.int32, (n, 1), 0)
        dpin = []
        for g, w in enumerate(POOL_WINDOWS):
            cnt = jnp.maximum(jnp.minimum(tg + w // 2, s_len) - jnp.maximum(tg - w // 2, 0), 1).astype(F32)
            zq = win[:, g * 128:(g + 1) * 128] / cnt
            zq = zq + _shift_rows(zq, 1)
            for step in (1, 2, 4):
                if w >= 4 * step:
                    zq = _shift_rows(zq, -step) + _shift_rows(zq, step)
            dpin.append(zq[HALO:HALO + TB] - dp[:, g * 128:(g + 1) * 128])
        zero = jnp.zeros((TB, 512), BF16)
        du = [dlo_ref[...], jnp.where(latent, dga_ref[...], zero),
              jnp.where(latent, jnp.concatenate(dpin, axis=1).astype(BF16), zero), jnp.where(latent, dgp_ref[...], zero)]

        ng = ng_ref[...]
        xb = jnp.where(i < off, ctx_ref[...], x_ref[...])
        r, xh, xg, h, scale = _modulated(xb, mod_ref, ng)
        hb_ = h.astype(BF16)
        dh = jnp.zeros((TB, D), F32)
        for s, (lo, hi) in enumerate(SEG):
            dwin_ref[lo:hi, :] += _tn(du[s], hb_)
            dh += _nn(du[s], win_ref[lo:hi, :])
        is_lat = latent.astype(F32)
        dsh = _colsum(dh)
        dsc = _colsum(dh * xg)
        dmod_ref[0, 0:1, :] += dsh * (1.0 - is_lat)
        dmod_ref[0, 1:2, :] += dsc * (1.0 - is_lat)
        dmod_ref[1, 0:1, :] += dsh * is_lat
        dmod_ref[1, 1:2, :] += dsc * is_lat
        dxg = dh * (1.0 + scale)
        dng_ref[...] += _colsum(dxg * xh)
        dxh = dxg * ng
        gx_ref[...] = r * (dxh - xh * jnp.mean(dxh * xh, axis=-1, keepdims=True)) + dxn_ref[...]

    row = lambda w: pl.BlockSpec((TB, w), lambda i: (i, 0))
    lat = lambda w: pl.BlockSpec((TB, w), lambda i: (jnp.maximum(i - off, 0), 0))
    last8 = s_len // HALO - 1
    cspec, xspec, mspec = _token_specs(off)
    return pl.pallas_call(
        body, name="in_bwd", grid=(nb,),
        in_specs=[cspec, xspec, mspec, _full((1, D)), row(512), lat(512), lat(512), lat(512),
                  pl.BlockSpec((HALO, 512), lambda i: (jnp.maximum(jnp.maximum(i - off, 0) * hb - 1, 0), 0)),
                  pl.BlockSpec((HALO, 512), lambda i: (jnp.minimum((jnp.maximum(i - off, 0) + 1) * hb, last8), 0)),
                  lat(D), _full((DIN, D))],
        out_specs=[lat(D), _full((DIN, D)), _full((2, 2, D)), _full((1, D))],
        out_shape=[jax.ShapeDtypeStruct((s_len, D), F32), jax.ShapeDtypeStruct((DIN, D), F32),
                   jax.ShapeDtypeStruct((2, 2, D), F32), jax.ShapeDtypeStruct((1, D), F32)],
        compiler_params=_params(("arbitrary",)),
    )(ctx, x, modsel, norm_g, dlo, dga, dgp, dpool, dpool, dpool, dxn, w_in_t)


def _adamw_update(w_ref, g_ref, m_ref, v_ref, d_ref, mo_ref, vo_ref):
    gv = g_ref[...]
    mn = ADAM_B1 * m_ref[...] + (1.0 - ADAM_B1) * gv
    vn = ADAM_B2 * v_ref[...] + (1.0 - ADAM_B2) * (gv * gv)
    m_hat = mn / (1.0 - ADAM_B1 ** ADAM_STEP)
    v_hat = vn / (1.0 - ADAM_B2 ** ADAM_STEP)
    d_ref[...] = -ADAM_LR * (m_hat / (jnp.sqrt(v_hat) + ADAM_EPS) + ADAM_WD * w_ref[...])
    mo_ref[...] = mn
    vo_ref[...] = vn


def _adamw_many(ws, gs, ms, vs):
    n = len(ws)

    def body(*refs):
        for i in range(n):
            _adamw_update(refs[i], refs[n + i], refs[2 * n + i], refs[3 * n + i], refs[4 * n + i], refs[5 * n + i], refs[6 * n + i])

    def spec(w):
        rows, cols = w.shape
        return pl.BlockSpec((rows // 2, cols), lambda i: (i, 0)) if rows % 16 == 0 else _full((rows, cols))

    specs = [spec(w) for w in ws]
    shp = [jax.ShapeDtypeStruct(w.shape, F32) for w in ws]
    out = pl.pallas_call(body, name="adamw_many", grid=(2,), in_specs=specs * 4, out_specs=specs * 3, out_shape=shp * 3,
                         compiler_params=_params(("arbitrary",)))(*ws, *gs, *ms, *vs)
    return out[:n], out[n:2 * n], out[2 * n:]


def _adamw(w, g, m, v, name):
    rows, cols = w.shape
    rb = next(r for r in range(min(rows, 256), 0, -8) if rows % r == 0)

    def body(w_ref, g_ref, m_ref, v_ref, d_ref, mo_ref, vo_ref):
        _adamw_update(w_ref, g_ref, m_ref, v_ref, d_ref, mo_ref, vo_ref)

    spec = pl.BlockSpec((rb, cols), lambda i: (i, 0))
    shp = jax.ShapeDtypeStruct((rows, cols), F32)
    return pl.pallas_call(
        body, name=name, grid=(rows // rb,), in_specs=[spec] * 4, out_specs=[spec] * 3, out_shape=[shp] * 3,
        compiler_params=_params(("arbitrary",)),
    )(w, g, m, v)


class _Links:
    def __init__(self, send_sems, recv_sems):
        self.send_sems, self.recv_sems, self.sends = send_sems, recv_sems, []

    def send(self, src, dst, sem, to):
        cp = pltpu.make_async_remote_copy(src, dst, self.send_sems.at[sem], self.recv_sems.at[sem], device_id=to,
                                          device_id_type=MESH)
        cp.start()
        self.sends.append(cp)

    def arrived(self, dst, sem, frm):
        pltpu.make_async_remote_copy(dst, dst, self.send_sems.at[sem], self.recv_sems.at[sem], device_id=frm,
                                     device_id_type=MESH).wait_recv()

    def drain(self):
        for cp in self.sends:
            cp.wait_send()


def _half(ref, c, axis):
    size = ref.shape[axis - 2] // 2
    win = pl.ds(pl.multiple_of(c * size, 16 if axis == 0 else 128), size)
    idx = (win, slice(None)) if axis == 0 else (slice(None), win)
    return ref.at[(slice(None),) * (len(ref.shape) - 2) + idx]


def _select_rows(slots_ref, n_slots, row=0):
    sub = lax.broadcasted_iota(jnp.int32, (8, 1), 0)
    out = None
    for d in range(n_slots):
        r = jnp.where(sub == d, jnp.broadcast_to(slots_ref[d][row:row + 1, :], (8, slots_ref.shape[-1])), 0.0)
        out = r if out is None else out + r
    return out


def _gather(c, c_ctx, w_mod, b_mod_k, shards, axes):
    nw = len(shards)
    kw = w_mod.shape[1]

    def body(*refs):
        c_ref, cc_ref, wm_ref, b_ref = refs[:4]
        w_refs = refs[4:4 + nw]
        a16_ref, mod_ref = refs[4 + nw:6 + nw]
        g_refs = refs[6 + nw:6 + 2 * nw]
        a_ref, send_sems, recv_sems, local_sems = refs[6 + 2 * nw:]
        x, y, cc = lax.axis_index("x"), lax.axis_index("y"), lax.axis_index("c")
        me = 4 * x + 2 * y + cc
        k = 2 * x + y
        sibling = (x, y, 1 - cc)
        links = _Links(send_sems, recv_sems)
        chips = [_peer(x, y, cc, off + (0,)) for off in CHIPS3]
        locals_ = []
        for wi in range(nw):
            lc = pltpu.make_async_copy(w_refs[wi], g_refs[wi].at[k], local_sems.at[wi])
            lc.start()
            locals_.append(lc)
            for j, to in enumerate(chips):
                links.send(_half(w_refs[wi], cc, axes[wi]), _half(g_refs[wi].at[k], cc, axes[wi]), 10 + wi * 6 + j, to)
        cv = c_ref[...]
        a_ref[me] = jnp.broadcast_to(cv * _sig(cv), (8, D))
        for j, off in enumerate(PEERS7):
            links.send(a_ref.at[me], a_ref.at[me], j, _peer(x, y, cc, off))
        for j, off in enumerate(PEERS7):
            px, py, pc = _peer(x, y, cc, off)
            links.arrived(a_ref.at[4 * px + 2 * py + pc], j, (px, py, pc))
        ccv = cc_ref[...]
        sub = lax.broadcasted_iota(jnp.int32, (8, 1), 0)
        a16 = jnp.concatenate([_select_rows(a_ref, 8), jnp.where(sub == 0, jnp.broadcast_to(ccv * _sig(ccv), (8, D)), 0.0)], axis=0)
        a16_ref[...] = a16
        mod_ref[k] = _dot3(_nn, a16, wm_ref[...]) + b_ref[...]
        for j, to in enumerate(chips):
            links.send(mod_ref.at[k], mod_ref.at[k], 7 + j, to)
        for wi in range(nw):
            for j, (px, py, pc) in enumerate(chips):
                blk = _half(g_refs[wi].at[2 * px + py], cc, axes[wi])
                links.arrived(blk, 10 + wi * 6 + j, (px, py, pc))
                links.send(blk, blk, 10 + wi * 6 + 3 + j, sibling)
        for j, (px, py, pc) in enumerate(chips):
            links.arrived(mod_ref.at[2 * px + py], 7 + j, (px, py, pc))
        for wi in range(nw):
            for j, (px, py, pc) in enumerate(chips):
                links.arrived(_half(g_refs[wi].at[2 * px + py], 1 - cc, axes[wi]), 10 + wi * 6 + 3 + j, sibling)
        links.drain()
        for lc in locals_:
            lc.wait()

    nsem = 10 + 6 * nw
    return pl.pallas_call(
        body, name="gather", in_specs=[VM] * 4 + [ANY] * nw, out_specs=[VM, VM] + [ANY] * nw,
        out_shape=[jax.ShapeDtypeStruct((16, D), F32), jax.ShapeDtypeStruct((4, 16, kw), F32)]
        + [jax.ShapeDtypeStruct((4,) + s.shape, s.dtype) for s in shards],
        scratch_shapes=[pltpu.VMEM((8, 8, D), F32), pltpu.SemaphoreType.DMA((nsem,)), pltpu.SemaphoreType.DMA((nsem,)),
                        pltpu.SemaphoreType.DMA((nw,))],
        compiler_params=pltpu.CompilerParams(vmem_limit_bytes=VMEM_LIMIT),
    )(c, c_ctx, w_mod, b_mod_k, *shards)


SMALL_ROW_WIDTHS = (D, QL, KVL, DKP, DKP, 512, 128)
SMALL_OUT_WIDTHS = (D, QL, KVL, DK, DK, 512, 1)


def _reduce(grads, axes, smalls, w_pool_g, dmod8, a16, w_mod, c_ctx):
    nw = len(grads)
    ns = len(smalls)
    kw = w_mod.shape[1]
    halves = []
    for g, ax in zip(grads, axes):
        halves.append((g.shape[1] // 2, g.shape[2]) if ax == 0 else (g.shape[1], g.shape[2] // 2))

    def body(*refs):
        g_refs = refs[:nw]
        small_refs = refs[nw:nw + ns]
        wp_ref, dm_ref, a16_ref, wm_ref, cc_ref = refs[nw + ns:nw + ns + 5]
        o = nw + ns + 5
        r_refs = refs[o:o + nw]
        small_outs = refs[o + nw:o + nw + ns]
        rwp_ref, gw_ref, gb_ref, gc_ref = refs[o + nw + ns:o + nw + ns + 4]
        o = o + nw + ns + 4
        own, sib, part, got = (refs[o + i * nw:o + (i + 1) * nw] for i in range(4))
        smbuf, wps, wpg, dm_all, pc_all, send_sems, recv_sems, local_sems = refs[o + 4 * nw:]
        x, y, cc = lax.axis_index("x"), lax.axis_index("y"), lax.axis_index("c")
        me = 4 * x + 2 * y + cc
        k = 2 * x + y
        sibling = (x, y, 1 - cc)
        links = _Links(send_sems, recv_sems)
        chips = [_peer(x, y, cc, off + (0,)) for off in CHIPS3]
        peers = [_peer(x, y, cc, off) for off in PEERS7]
        big, sm0, wp0, dm0, pc0 = 0, 5 * nw, 5 * nw + 7, 5 * nw + 14, 5 * nw + 21

        locals_ = []
        for wi in range(nw):
            lc = pltpu.make_async_copy(_half(g_refs[wi], cc, axes[wi]), own[wi], local_sems.at[wi])
            lc.start()
            locals_.append(lc)
            links.send(_half(g_refs[wi], 1 - cc, axes[wi]), sib[wi], big + wi * 5, sibling)
        slot = smbuf.at[me]
        slot[...] = jnp.zeros((8, D), F32)
        for r, (ref, w) in enumerate(zip(small_refs, SMALL_ROW_WIDTHS)):
            slot[r:r + 1, 0:w] = jnp.broadcast_to(ref[...], (1, w))
        links.send(wp_ref, wps, wp0, sibling)
        dm_all[me] = dm_ref[...]
        for j, peer in enumerate(peers):
            links.send(dm_all.at[me], dm_all.at[me], dm0 + j, peer)
            links.send(smbuf.at[me], smbuf.at[me], sm0 + j, peer)
        links.arrived(wps, wp0, sibling)
        wpg[k] = (wp_ref[...] + wps[...]).astype(BF16)
        for j, to in enumerate(chips):
            links.send(wpg.at[k], wpg.at[k], wp0 + 1 + j, to)
        for wi in range(nw):
            locals_[wi].wait()
            links.arrived(sib[wi], big + wi * 5, sibling)
            part[wi][...] = (own[wi][...] + sib[wi][...]).astype(BF16)
            got[wi][k] = part[wi][k]
            for j, (px, py, pc) in enumerate(chips):
                links.send(part[wi].at[2 * px + py], got[wi].at[k], big + wi * 5 + 1 + j, (px, py, pc))
        for j, (px, py, pc) in enumerate(peers):
            links.arrived(dm_all.at[4 * px + 2 * py + pc], dm0 + j, (px, py, pc))
        dmc = dm_all[0][1:2, :]
        dml = dm_all[0][0:1, :]
        for d in range(1, 8):
            dmc = dmc + dm_all[d][1:2, :]
            dml = dml + dm_all[d][0:1, :]
        gb_ref[...] = dml + dmc
        sub = lax.broadcasted_iota(jnp.int32, (8, 1), 0)
        b16 = jnp.concatenate([_select_rows(dm_all, 8), jnp.where(sub == 0, jnp.broadcast_to(dmc, (8, 3 * D)), 0.0)], axis=0)
        bk = jnp.zeros((16, kw), F32)
        for kk in range(4):
            bk = bk + jnp.where(k == kk, b16[:, kk * kw:(kk + 1) * kw], 0.0)
        gw_ref[...] = _dot3(_tn, a16_ref[...], bk)
        pc_all[k] = _dot3(_nt, jnp.broadcast_to(bk[8:9, :], (8, kw)), wm_ref[...])
        for j, to in enumerate(chips):
            links.send(pc_all.at[k], pc_all.at[k], pc0 + j, to)
        for j, (px, py, pc) in enumerate(peers):
            links.arrived(smbuf.at[4 * px + 2 * py + pc], sm0 + j, (px, py, pc))
        tot = smbuf[0]
        for d in range(1, 8):
            tot = tot + smbuf[d]
        for r, (ref, w) in enumerate(zip(small_outs, SMALL_OUT_WIDTHS)):
            ref[...] = tot[r:r + 1, 0:w]
        for j, (px, py, pc) in enumerate(chips):
            links.arrived(wpg.at[2 * px + py], wp0 + 1 + j, (px, py, pc))
        wpt = wpg[0].astype(F32)
        for kk in range(1, 4):
            wpt = wpt + wpg[kk].astype(F32)
        rwp_ref[...] = wpt
        for wi in range(nw):
            for j, (px, py, pc) in enumerate(chips):
                links.arrived(got[wi].at[2 * px + py], big + wi * 5 + 1 + j, (px, py, pc))
            total = got[wi][0].astype(F32)
            for kk in range(1, 4):
                total = total + got[wi][kk].astype(F32)
            mine = _half(r_refs[wi], cc, axes[wi])
            mine[...] = total
            links.send(mine, mine, big + wi * 5 + 4, sibling)
        for j, (px, py, pc) in enumerate(chips):
            links.arrived(pc_all.at[2 * px + py], pc0 + j, (px, py, pc))
        ccv = cc_ref[...]
        sg = _sig(ccv)
        gc_ref[...] = (pc_all[0][0:1, :] + pc_all[1][0:1, :] + pc_all[2][0:1, :] + pc_all[3][0:1, :]) * (sg * (1.0 + ccv * (1.0 - sg)))
        for wi in range(nw):
            links.arrived(_half(r_refs[wi], 1 - cc, axes[wi]), big + wi * 5 + 4, sibling)
        links.drain()

    nsem = 5 * nw + 24
    quads = [(4,) + h for h in halves]
    return pl.pallas_call(
        body, name="reduce", in_specs=[ANY] * nw + [VM] * (ns + 5), out_specs=[VM] * (nw + ns + 4),
        out_shape=[jax.ShapeDtypeStruct(g.shape[1:], F32) for g in grads]
        + [jax.ShapeDtypeStruct((1, w), F32) for w in SMALL_OUT_WIDTHS]
        + [jax.ShapeDtypeStruct(w_pool_g.shape, F32), jax.ShapeDtypeStruct((D, kw), F32), jax.ShapeDtypeStruct((1, 3 * D), F32),
           jax.ShapeDtypeStruct((1, D), F32)],
        scratch_shapes=[pltpu.VMEM(q, F32) for q in quads] + [pltpu.VMEM(q, F32) for q in quads]
        + [pltpu.VMEM(q, BF16) for q in quads] + [pltpu.VMEM(q, BF16) for q in quads]
        + [pltpu.VMEM((8, 8, D), F32), pltpu.VMEM(w_pool_g.shape, F32), pltpu.VMEM((4,) + w_pool_g.shape, BF16),
           pltpu.VMEM((8, 8, 3 * D), F32),
           pltpu.VMEM((4, 8, D), F32)]
        + [pltpu.SemaphoreType.DMA((nsem,)), pltpu.SemaphoreType.DMA((nsem,)), pltpu.SemaphoreType.DMA((nw,))],
        compiler_params=pltpu.CompilerParams(vmem_limit_bytes=VMEM_LIMIT),
    )(*grads, *smalls, w_pool_g, dmod8, a16, w_mod, c_ctx)


def _rope_tables(s_len):
    rows = s_len // GRID_W
    per = TB // GRID_W
    n_freq = 16
    inv = ROPE_BASE ** (-jnp.arange(n_freq, dtype=F32) / n_freq)
    ang_r = jnp.arange(rows, dtype=F32)[:, None] * inv
    ang_c = jnp.arange(GRID_W, dtype=F32)[:, None] * inv
    by_row, by_col = [], []
    for fn, pad in ((jnp.cos, 1.0), (jnp.sin, 0.0)):
        r = jnp.concatenate([fn(ang_r), fn(ang_r), jnp.zeros((rows, 96), F32)], axis=1).reshape(rows // per, per, 128)
        by_row.append(jnp.pad(r, ((0, 0), (0, 8 - per), (0, 0))))
        cpart = jnp.concatenate([jnp.zeros((GRID_W, 32), F32), fn(ang_c), fn(ang_c), jnp.full((GRID_W, 64), pad, F32)], axis=1)
        by_col.append(jnp.tile(cpart, (per, 1)))
    return jnp.concatenate(by_row, axis=-1), jnp.concatenate(by_col, axis=-1)


def kernel(x, c, ctx, c_ctx, w_mod, b_mod, norm_g, w_in, q_lora_g, w_uq, kv_lora_g, w_ukv, q_norm_g, k_norm_g, w_pool, pool_scale, w_out, loss_target, m_c_ctx, m_w_mod, m_b_mod, m_norm_g, m_w_in, m_q_lora_g, m_w_uq, m_kv_lora_g, m_w_ukv, m_q_norm_g, m_k_norm_g, m_w_pool, m_pool_scale, m_w_out, v_c_ctx, v_w_mod, v_b_mod, v_norm_g, v_w_in, v_q_lora_g, v_w_uq, v_kv_lora_g, v_w_ukv, v_q_norm_g, v_k_norm_g, v_w_pool, v_pool_scale, v_w_out):
    xi, yi, ci = lax.axis_index("x"), lax.axis_index("y"), lax.axis_index("c")
    me = 4 * xi + 2 * yi + ci
    k = 2 * xi + yi
    s_len = x.shape[1]
    lc = ctx.shape[1]
    kw = w_mod.shape[2]
    weights = dict(c_ctx=c_ctx, w_mod=w_mod, b_mod=b_mod, norm_g=norm_g, w_in=w_in, q_lora_g=q_lora_g, w_uq=w_uq,
                   kv_lora_g=kv_lora_g, w_ukv=w_ukv, q_norm_g=q_norm_g, k_norm_g=k_norm_g, w_pool=w_pool,
                   pool_scale=pool_scale, w_out=w_out)
    m_in = dict(c_ctx=m_c_ctx, w_mod=m_w_mod, b_mod=m_b_mod, norm_g=m_norm_g, w_in=m_w_in, q_lora_g=m_q_lora_g, w_uq=m_w_uq,
                kv_lora_g=m_kv_lora_g, w_ukv=m_w_ukv, q_norm_g=m_q_norm_g, k_norm_g=m_k_norm_g, w_pool=m_w_pool,
                pool_scale=m_pool_scale, w_out=m_w_out)
    v_in = dict(c_ctx=v_c_ctx, w_mod=v_w_mod, b_mod=v_b_mod, norm_g=v_norm_g, w_in=v_w_in, q_lora_g=v_q_lora_g, w_uq=v_w_uq,
                kv_lora_g=v_kv_lora_g, w_ukv=v_w_ukv, q_norm_g=v_q_norm_g, k_norm_g=v_k_norm_g, w_pool=v_w_pool,
                pool_scale=v_pool_scale, w_out=v_w_out)
    order = ["c_ctx", "w_mod", "b_mod", "norm_g", "w_in", "q_lora_g", "w_uq", "kv_lora_g", "w_ukv", "q_norm_g", "k_norm_g",
             "w_pool", "pool_scale", "w_out"]
    transposed = ("w_in", "w_uq")
    as2d = lambda n, a: jnp.transpose(a[0]) if n in transposed else a.reshape(-1, a.shape[-1])
    back = lambda n, a: jnp.transpose(a)[None] if n in transposed else a.reshape(weights[n].shape)

    c_ctx2 = c_ctx.reshape(1, D)
    b_mod_k = lax.dynamic_slice(b_mod, (0, k * kw), (1, kw))
    split = (1, 0, 0, 0)
    a16, mod_all, g_in, g_uq, g_ukv, g_out = _gather(
        c, c_ctx2, w_mod[0], b_mod_k,
        [as2d("w_in", w_in).astype(BF16), as2d("w_uq", w_uq).astype(BF16), w_ukv[0].astype(BF16), w_out[0].astype(BF16)], split)
    mod_me = lax.dynamic_index_in_dim(mod_all, me, axis=1, keepdims=False).reshape(3, D)
    mod_c = mod_all[:, 8, :].reshape(3, D)
    modsel = jnp.stack([mod_c, mod_me])
    w_in_t = g_in.reshape(DIN, D)
    w_uq_t = jnp.pad(g_uq, ((0, 0), (0, DKP - DK), (0, 0)))
    w_out_f = g_out.reshape(D, D)
    qn_g = jnp.pad(q_norm_g, ((0, 0), (0, DKP - DK)))
    kn_g = jnp.pad(k_norm_g, ((0, 0), (0, DKP - DK)))
    w_pool_b = w_pool[0].astype(BF16)
    cos, sin = _rope_tables(s_len)

    u, q, kk, v = _fwd_in(ctx[0], x[0], modsel, norm_g, w_in_t, q_lora_g, w_uq_t, kv_lora_g, g_ukv, qn_g, kn_g, cos, sin)
    attn, lse = _attn_fwd(q, kk, v, s_len)
    (dxn, dattn, dga, dgp, dpool, dw_out, dgate, dps, dw_pool, loss) = _out_stage(
        attn.reshape(s_len // Q_BLOCK, Q_BLOCK, NH * DV), u, x[0], loss_target[0], modsel[1, 2:3, :], w_pool_b, pool_scale,
        w_out_f, lc)
    dattn = dattn.reshape(s_len, NH * DV)
    dq, dk, dv = _attn_bwd(q, kk, v, dattn, attn, lse, s_len)
    dlo, dw_uq_t, dw_ukv, dqlg, dkvlg, dqng, dkng = _qkv_bwd(u, dq, dk, dv, cos, sin, q_lora_g, w_uq_t, kv_lora_g, g_ukv,
                                                            qn_g, kn_g, s_len)
    gx, dw_in_t, dmod, dng = _in_bwd(ctx[0], x[0], modsel, norm_g, dlo, dga, dgp, dpool, dxn, w_in_t)

    dmod_l = jnp.concatenate([dmod[1, 0], dmod[1, 1], dgate[0]]).reshape(1, 3 * D)
    dmod_c = jnp.concatenate([dmod[0, 0], dmod[0, 1], jnp.zeros((D,), F32)]).reshape(1, 3 * D)
    dmod8 = jnp.concatenate([dmod_l, dmod_c, jnp.zeros((6, 3 * D), F32)], axis=0)
    (r_in, r_uq, r_ukv, r_out, g_ng, g_qlg, g_kvlg, g_qng, g_kng, g_ps, loss_all, g_wp, g_w_mod, g_b_mod, g_c_ctx) = _reduce(
        [dw_in_t.reshape(4, DIN // 4, D), dw_uq_t, dw_ukv, dw_out.reshape(4, D // 4, D)], split,
        [dng, dqlg, dkvlg, dqng, dkng, dps, loss], dw_pool, dmod8, a16, w_mod[0], c_ctx2)
    g2d = dict(c_ctx=g_c_ctx, b_mod=g_b_mod, w_mod=g_w_mod, w_in=r_in, w_uq=r_uq, w_ukv=r_ukv, w_out=r_out, norm_g=g_ng,
               q_lora_g=g_qlg, kv_lora_g=g_kvlg, q_norm_g=g_qng, k_norm_g=g_kng, pool_scale=g_ps, w_pool=g_wp.reshape(512, 128))

    d2d, m2d, v2d = {}, {}, {}
    d2d["w_mod"], m2d["w_mod"], v2d["w_mod"] = _adamw(as2d("w_mod", w_mod), g2d["w_mod"], as2d("w_mod", m_w_mod),
                                                      as2d("w_mod", v_w_mod), "adamw_w_mod")
    rest = [n for n in order if n != "w_mod"]
    outs = _adamw_many([as2d(n, weights[n]) for n in rest], [g2d[n] for n in rest], [as2d(n, m_in[n]) for n in rest],
                       [as2d(n, v_in[n]) for n in rest])
    for dst, arrs in zip((d2d, m2d, v2d), outs):
        dst.update(dict(zip(rest, arrs)))

    return (loss_all[0, 0], gx[None], *[back(n, g2d[n]) for n in order], *[back(n, d2d[n]) for n in order],
            *[back(n, m2d[n]) for n in order], *[back(n, v2d[n]) for n in order])
```

```python
import jax
import jax.numpy as jnp
from jax import lax
from jax.experimental import pallas as pl
from jax.experimental.pallas import tpu as pltpu

F32 = jnp.float32
BF16 = jnp.bfloat16
MESH = pl.DeviceIdType.MESH

D = 1024
NH = 4
DK = 192
DKP = 256
DV = 128
QL = 256
KVL = 128
DIN = 1984
U_LO = 448
SEG = ((0, 512), (448, 960), (960, 1472), (1472, 1984))
DU = 2048
POOL_WINDOWS = (2, 4, 8, 16)
HALO = 8
EPS = 1e-6
ROPE_BASE = 10000.0
GRID_W = 64
Q_BLOCK = 128
TB = 256
BWD_QBLOCKS = 1
SCALE = DK ** -0.5
LOG2E = 1.4426950408889634
LN2 = 0.6931471805599453
VMEM_LIMIT = 56 * 1024 * 1024

ADAM_LR = 0.001
ADAM_B1 = 0.9
ADAM_B2 = 0.999
ADAM_EPS = 1e-08
ADAM_WD = 0.01
ADAM_STEP = 10

CHIPS3 = ((1, 0), (0, 1), (1, 1))
PEERS7 = tuple((dx, dy, dc) for dx in (0, 1) for dy in (0, 1) for dc in (0, 1) if (dx, dy, dc) != (0, 0, 0))

VM = pl.BlockSpec(memory_space=pltpu.VMEM)
ANY = pl.BlockSpec(memory_space=pl.ANY)


def _nn(a, b):
    return jnp.dot(a, b, preferred_element_type=F32)


def _nt(a, b):
    return lax.dot_general(a, b, (((1,), (1,)), ((), ())), preferred_element_type=F32)


def _tn(a, b):
    return lax.dot_general(a, b, (((0,), (0,)), ((), ())), preferred_element_type=F32)


def _split3(a):
    a0 = a.astype(BF16)
    r = a - a0.astype(F32)
    a1 = r.astype(BF16)
    a2 = (r - a1.astype(F32)).astype(BF16)
    return a0, a1, a2


def _dot3(dot, a, b):
    sa = _split3(a)
    sb = _split3(b)
    out = None
    for i in range(3):
        for j in range(3 - i):
            t = dot(sa[i], sb[j])
            out = t if out is None else out + t
    return out


def _sig(x):
    return 1.0 / (1.0 + jnp.exp(-x))


def _rot(t):
    lane = lax.broadcasted_iota(jnp.int32, t.shape, 1)
    return jnp.where((lane % 32) < 16, -pltpu.roll(t, 112, 1), pltpu.roll(t, 16, 1))


def _rot_mxu(t):
    src = lax.broadcasted_iota(jnp.int32, (128, 128), 0)
    dst = lax.broadcasted_iota(jnp.int32, (128, 128), 1)
    first = (dst % 32) < 16
    perm = jnp.where(first & (src == dst + 16), -1.0, jnp.where(~first & (src == dst - 16), 1.0, 0.0)).astype(BF16)
    hi = t.astype(BF16)
    lo = (t - hi.astype(F32)).astype(BF16)
    return _nn(hi, perm) + _nn(lo, perm)


def _rope(t, cos, sin):
    return t * cos + _rot_mxu(t) * sin


def _rope_t(t, cos, sin):
    return t * cos - _rot(t * sin)


def _rope_block(rows_ref, cols_ref, is_ctx):
    lane = lax.broadcasted_iota(jnp.int32, (TB, 256), 1) % 128
    rows = jnp.concatenate([jnp.broadcast_to(rows_ref[0, r:r + 1, :], (GRID_W, 256)) for r in range(TB // GRID_W)], axis=0)
    cs = jnp.where(lane < 32, rows, cols_ref[...])
    return jnp.where(is_ctx, 1.0, cs[:, :128]), jnp.where(is_ctx, 0.0, cs[:, 128:])


def _shift_rows(z, k):
    n = z.shape[0]
    return pltpu.roll(z, (n - k) % n, 0)


def _colsum(a):
    return jnp.sum(a, axis=0, keepdims=True)


def _rowsum(a):
    return jnp.sum(a, axis=-1, keepdims=True)


def _row_layout(col):
    return jnp.transpose(jnp.broadcast_to(col, (col.shape[0], 128)))[0:8, :]


def _params(sem=None):
    return pltpu.CompilerParams(dimension_semantics=sem, vmem_limit_bytes=VMEM_LIMIT)


def _full(shape):
    nd = len(shape)
    return pl.BlockSpec(shape, lambda *_: (0,) * nd)


def _peer(x, y, c, off):
    dx, dy, dc = off
    return ((x + dx) % 2, (y + dy) % 2, (c + dc) % 2)


def _token_specs(off):
    ctx = pl.BlockSpec((TB, D), lambda i: (jnp.minimum(i, off - 1), 0))
    lat = pl.BlockSpec((TB, D), lambda i: (jnp.maximum(i - off, 0), 0))
    mod = pl.BlockSpec((1, 3, D), lambda i: (jnp.minimum(i // off, 1), 0, 0))
    return ctx, lat, mod


def _modulated(x, mod_ref, ng):
    shift = mod_ref[0, 0:1, :]
    scale = mod_ref[0, 1:2, :]
    r = lax.rsqrt(jnp.mean(x * x, axis=-1, keepdims=True) + EPS)
    xh = x * r
    xg = xh * ng
    return r, xh, xg, xg * (1.0 + scale) + shift, scale


def _fwd_in(ctx, x, modsel, norm_g, w_in_t, q_lora_g, w_uq_t, kv_lora_g, w_ukv, qn_g, kn_g, cos, sin):
    s_len, lc = x.shape[0], ctx.shape[0]
    t_all = s_len + lc
    nb = t_all // TB
    off = lc // TB

    def body(ctx_ref, x_ref, mod_ref, ng_ref, win_ref, qlg_ref, wuq_ref, kvlg_ref, wukv_ref, qng_ref, kng_ref, cos_ref, sin_ref,
             u_ref, q_ref, k_ref, v_ref):
        is_ctx = pl.program_id(0) < off
        xb = jnp.where(is_ctx, ctx_ref[...], x_ref[...])
        _, _, _, h, _ = _modulated(xb, mod_ref, ng_ref[...])
        hb = h.astype(BF16)
        lane = lax.broadcasted_iota(jnp.int32, (TB, 512), 1)
        ulo = jnp.where(lane < U_LO, _nt(hb, win_ref[SEG[0][0]:SEG[0][1], :]), 0.0)
        u_ref[:, 0:512] = ulo
        for j in range(1, 4):
            u_ref[:, j * 512:(j + 1) * 512] = _nt(hb, win_ref[SEG[j][0]:SEG[j][1], :])
        cos, sin = _rope_block(cos_ref, sin_ref, is_ctx)
        cq = ulo[:, 0:QL]
        cqn = (cq * lax.rsqrt(jnp.mean(cq * cq, axis=-1, keepdims=True) + EPS) * qlg_ref[...]).astype(BF16)
        qng = qng_ref[...]
        for hd in range(NH):
            qh = _nt(cqn, wuq_ref[hd])
            qn = qh * lax.rsqrt(_rowsum(qh * qh) / DK + EPS) * qng
            q_ref[hd] = (jnp.concatenate([qn[:, :128], _rope(qn[:, 128:], cos, sin)], axis=1) * (SCALE * LOG2E)).astype(BF16)
        ckv = ulo[:, QL:QL + KVL]
        ckvn = (ckv * lax.rsqrt(jnp.mean(ckv * ckv, axis=-1, keepdims=True) + EPS) * kvlg_ref[...]).astype(BF16)
        kr = ulo[:, 384:512]
        skr = _rowsum(kr * kr)
        kng = kng_ref[...]
        kr_roped = _rope(kr * kng[:, 128:], cos, sin)
        for hd in range(NH):
            kv = _nn(ckvn, wukv_ref[hd])
            kn = kv[:, :128]
            rk = lax.rsqrt((_rowsum(kn * kn) + skr) / DK + EPS)
            k_ref[hd] = jnp.concatenate([kn * rk * kng[:, :128], kr_roped * rk], axis=1).astype(BF16)
            v_ref[hd] = kv[:, 128:].astype(BF16)

    row = lambda w: pl.BlockSpec((TB, w), lambda i: (i, 0))
    heads = lambda w: pl.BlockSpec((NH, TB, w), lambda i: (0, i, 0))
    cspec, xspec, mspec = _token_specs(off)
    return pl.pallas_call(
        body, name="fwd_in", grid=(nb,),
        in_specs=[cspec, xspec, mspec, _full((1, D)), _full((DIN, D)), _full((1, QL)), _full((NH, DKP, QL)), _full((1, KVL)),
                  _full((NH, KVL, 256)), _full((1, DKP)), _full((1, DKP)),
                  pl.BlockSpec((1, 8, 256), lambda i: (jnp.maximum(i - off, 0), 0, 0)), _full((TB, 256))],
        out_specs=[row(DU), heads(DKP), heads(DKP), heads(DV)],
        out_shape=[jax.ShapeDtypeStruct((t_all, DU), F32), jax.ShapeDtypeStruct((NH, t_all, DKP), BF16),
                   jax.ShapeDtypeStruct((NH, t_all, DKP), BF16), jax.ShapeDtypeStruct((NH, t_all, DV), BF16)],
        compiler_params=_params(("arbitrary",)),
    )(ctx, x, modsel, norm_g, w_in_t, q_lora_g, w_uq_t, kv_lora_g, w_ukv, qn_g, kn_g, cos, sin)


def _attn_fwd(q, k, v, s_len):
    t_all = q.shape[1]
    off = (t_all - s_len) // TB
    nq = s_len // TB
    nsub = next(n for n in (4, 2, 1) if nq % n == 0)

    def body(*refs):
        q_refs = refs[:nsub]
        k_ref, v_ref, o_ref, lse_ref = refs[nsub:]
        for sb in range(nsub):
            s = _nt(q_refs[sb][0], k_ref[0])
            m = jnp.max(s, axis=-1, keepdims=True)
            e = jnp.exp2(s - m)
            l = _rowsum(e)
            o_ref[sb * TB:(sb + 1) * TB, :] = _nn(e.astype(BF16), v_ref[0]) / l
            lse_ref[0, sb] = _row_layout(m + jnp.log2(l))

    qspec = lambda sb: pl.BlockSpec((1, TB, DKP), lambda h, i: (h, i * nsub + sb + off, 0))
    return pl.pallas_call(
        body, name="attn_fwd", grid=(NH, nq // nsub),
        in_specs=[qspec(sb) for sb in range(nsub)]
        + [pl.BlockSpec((1, t_all, DKP), lambda h, i: (h, 0, 0)), pl.BlockSpec((1, t_all, DV), lambda h, i: (h, 0, 0))],
        out_specs=[pl.BlockSpec((nsub * TB, DV), lambda h, i: (i, h)), pl.BlockSpec((1, nsub, 8, TB), lambda h, i: (h, i, 0, 0))],
        out_shape=[jax.ShapeDtypeStruct((s_len, NH * DV), F32), jax.ShapeDtypeStruct((NH, nq, 8, TB), F32)],
        compiler_params=_params(("arbitrary", "arbitrary")),
    )(*([q] * nsub), k, v)


def _out_stage(attn, u, x, target, gate, w_pool, pool_scale, w_out, lc):
    s_len = x.shape[0]
    t_all = s_len + lc
    off = lc // TB
    nq = s_len // TB
    hb = TB // HALO
    nqb = s_len // Q_BLOCK
    jb = TB // nqb

    def body(attn_ref, ga_ref, pin_ref, pprev_ref, pnext_ref, gp_ref, x_ref, tgt_ref, gate_ref, wp_ref, ps_ref, wo_ref,
             dxn_ref, dattn_ref, dga_ref, dgp_ref, dpool_ref, dwo_ref, dgate_ref, dps_ref, dwp_ref, loss_ref):
        i = pl.program_id(0)

        @pl.when(i == 0)
        def _():
            dwo_ref[...] = jnp.zeros_like(dwo_ref)
            dgate_ref[...] = jnp.zeros_like(dgate_ref)
            dps_ref[...] = jnp.zeros_like(dps_ref)
            dwp_ref[...] = jnp.zeros_like(dwp_ref)
            loss_ref[...] = jnp.zeros_like(loss_ref)

        attn = jnp.concatenate([attn_ref[:, jj, :] for jj in range(jb)], axis=0)
        ga = ga_ref[...]
        gp = gp_ref[...]
        pin = pin_ref[...]
        prev = jnp.where(i == 0, 0.0, pprev_ref[...])
        nxt = jnp.where(i == nq - 1, 0.0, pnext_ref[...])
        win = jnp.concatenate([prev, pin, nxt], axis=0)
        tg = i * TB + lax.broadcasted_iota(jnp.int32, (TB, 1), 0)
        pooled = []
        for g, w in enumerate(POOL_WINDOWS):
            a = win[:, g * 128:(g + 1) * 128]
            p = _shift_rows(a, -1) + a
            for step in (1, 2, 4):
                if w >= 4 * step:
                    p = _shift_rows(p, -step) + _shift_rows(p, step)
            cnt = (jnp.minimum(tg + w // 2, s_len) - jnp.maximum(tg - w // 2, 0)).astype(F32)
            pooled.append(p[HALO:HALO + TB] / cnt - a[HALO:HALO + TB])
        pooled_b = [p.astype(BF16) for p in pooled]
        z = jnp.concatenate([_nn(pooled_b[g], wp_ref[g]) for g in range(4)], axis=1)
        ps = ps_ref[...]
        yp = z * ps
        sga = _sig(ga)
        sila = ga * sga
        sgp = _sig(gp)
        silp = gp * sgp
        br = jnp.concatenate([sila * attn, silp * yp], axis=1).astype(BF16)
        y = _nn(br, wo_ref[...])
        gate = gate_ref[...]
        err = x_ref[...] + gate * y - tgt_ref[...]
        loss_ref[...] += _colsum(_rowsum(err * err)) * (0.5 / D)
        dxn = err * (1.0 / D)
        dxn_ref[...] = dxn
        dgate_ref[...] += _colsum(dxn * y)
        dy = (dxn * gate).astype(BF16)
        dwo_ref[...] += _tn(br, dy)
        dbr = _nt(dy, wo_ref[...])
        dbra = dbr[:, :512]
        dbrp = dbr[:, 512:]
        dattn = dbra * sila
        for jj in range(jb):
            dattn_ref[:, jj, :] = dattn[jj * nqb:(jj + 1) * nqb]
        dga_ref[...] = (dbra * attn * (sga * (1.0 + ga * (1.0 - sga)))).astype(BF16)
        dgp_ref[...] = (dbrp * yp * (sgp * (1.0 + gp * (1.0 - sgp)))).astype(BF16)
        dyp = dbrp * silp
        dps_ref[...] += _colsum(dyp * z)
        dz = (dyp * ps).astype(BF16)
        dpool = []
        for g in range(4):
            dzg = dz[:, g * 128:(g + 1) * 128]
            dwp_ref[g] += _tn(pooled_b[g], dzg)
            dpool.append(_nt(dzg, wp_ref[g]))
        dpool_ref[...] = jnp.concatenate(dpool, axis=1)

    lat = lambda w: pl.BlockSpec((TB, w), lambda i: (i, 0))
    perm = pl.BlockSpec((nqb, jb, 512), lambda i: (0, i, 0))
    ucol = lambda j: pl.BlockSpec((TB, 512), lambda i: (i + off, j))
    last8 = t_all // HALO - 1
    return pl.pallas_call(
        body, name="out_stage", grid=(nq,),
        in_specs=[perm, ucol(1), ucol(2),
                  pl.BlockSpec((HALO, 512), lambda i: ((i + off) * hb - 1, 2)),
                  pl.BlockSpec((HALO, 512), lambda i: (jnp.minimum((i + off + 1) * hb, last8), 2)),
                  ucol(3), lat(D), lat(D), _full((1, D)), _full((4, 128, 128)), _full((1, 512)), _full((D, D))],
        out_specs=[lat(D), perm, lat(512), lat(512), lat(512),
                   _full((D, D)), _full((1, D)), _full((1, 512)), _full((4, 128, 128)), _full((1, 1))],
        out_shape=[jax.ShapeDtypeStruct((s_len, D), F32), jax.ShapeDtypeStruct((nqb, Q_BLOCK, 512), F32),
                   jax.ShapeDtypeStruct((s_len, 512), BF16), jax.ShapeDtypeStruct((s_len, 512), BF16),
                   jax.ShapeDtypeStruct((s_len, 512), F32),
                   jax.ShapeDtypeStruct((D, D), F32), jax.ShapeDtypeStruct((1, D), F32), jax.ShapeDtypeStruct((1, 512), F32),
                   jax.ShapeDtypeStruct((4, 128, 128), F32), jax.ShapeDtypeStruct((1, 1), F32)],
        compiler_params=_params(("arbitrary",)),
    )(attn, u, u, u, u, u, x, target, gate, w_pool, pool_scale, w_out)


def _attn_bwd(q, k, v, dattn, attn, lse, s_len):
    t_all = q.shape[1]
    off = (t_all - s_len) // TB
    nq = s_len // TB
    nch = 4
    chunks = [(c * (t_all // nch), t_all // nch) for c in range(nch)]
    nsub = next(n for n in (BWD_QBLOCKS, 2, 1) if nq % n == 0)
    tq = nsub * TB

    def body(*refs):
        q_refs = refs[:nsub]
        k_ref, v_ref, do_ref, o_ref, lse_ref, dq_ref, dk_ref, dv_ref = refs[nsub:]
        i = pl.program_id(1)

        @pl.when(i == 0)
        def _():
            dk_ref[...] = jnp.zeros_like(dk_ref)
            dv_ref[...] = jnp.zeros_like(dv_ref)

        qb = jnp.concatenate([r[0] for r in q_refs], axis=0)
        delta_r = _row_layout(_rowsum(do_ref[...] * o_ref[...]))[0:1, :]
        do = do_ref[...].astype(BF16)
        lse_r = jnp.concatenate([lse_ref[0, sb][0:1, :] for sb in range(nsub)], axis=1)
        dq = jnp.zeros((tq, DKP), F32)
        for start, size in chunks:
            rows = pl.ds(start, size)
            kc = k_ref[0, rows, :]
            p_t = jnp.exp2(_nt(kc, qb) - lse_r)
            ds_t = (p_t * (_nt(v_ref[0, rows, :], do) - delta_r)).astype(BF16)
            dv_ref[0, rows, :] += _nn(p_t.astype(BF16), do)
            dk_ref[0, rows, :] += _nn(ds_t, qb)
            dq += _tn(ds_t, kc)
        dq_ref[0] = dq * SCALE

    kvspec = lambda w: pl.BlockSpec((1, t_all, w), lambda h, i: (h, 0, 0))
    rowspec = pl.BlockSpec((1, nsub, 8, TB), lambda h, i: (h, i, 0, 0))
    qspec = lambda sb: pl.BlockSpec((1, TB, DKP), lambda h, i: (h, i * nsub + sb + off, 0))
    return pl.pallas_call(
        body, name="attn_bwd", grid=(NH, nq // nsub),
        in_specs=[qspec(sb) for sb in range(nsub)]
        + [kvspec(DKP), kvspec(DV), pl.BlockSpec((tq, DV), lambda h, i: (i, h)), pl.BlockSpec((tq, DV), lambda h, i: (i, h)),
           rowspec],
        out_specs=[pl.BlockSpec((1, tq, DKP), lambda h, i: (h, i, 0)), kvspec(DKP), kvspec(DV)],
        out_shape=[jax.ShapeDtypeStruct((NH, s_len, DKP), F32), jax.ShapeDtypeStruct((NH, t_all, DKP), F32),
                   jax.ShapeDtypeStruct((NH, t_all, DV), F32)],
        compiler_params=_params(("arbitrary", "arbitrary")),
    )(*([q] * nsub), k, v, dattn, attn, lse)


def _qkv_bwd(u, dq, dk, dv, cos, sin, q_lora_g, w_uq_t, kv_lora_g, w_ukv, qn_g, kn_g, s_len):
    t_all = u.shape[0]
    off = (t_all - s_len) // TB
    nb = t_all // TB

    def body(ulo_ref, dq_ref, dk_ref, dv_ref, cos_ref, sin_ref, qlg_ref, wuq_ref, kvlg_ref, wukv_ref, qng_ref, kng_ref,
             dlo_ref, dwuq_ref, dwukv_ref, dqlg_ref, dkvlg_ref, dqng_ref, dkng_ref):
        i = pl.program_id(0)

        @pl.when(i == 0)
        def _():
            for r in (dwuq_ref, dwukv_ref, dqlg_ref, dkvlg_ref, dqng_ref, dkng_ref):
                r[...] = jnp.zeros_like(r)

        latent = i >= off
        ulo = ulo_ref[...]
        cos, sin = _rope_block(cos_ref, sin_ref, pl.program_id(0) < off)
        cq = ulo[:, 0:QL]
        rc = lax.rsqrt(jnp.mean(cq * cq, axis=-1, keepdims=True) + EPS)
        cqh = cq * rc
        qlg = qlg_ref[...]
        cqn_b = (cqh * qlg).astype(BF16)
        qng = qng_ref[...]
        dqng = jnp.zeros((1, DKP), F32)
        dcqn = jnp.zeros((TB, QL), F32)
        for hd in range(NH):
            qh = _nt(cqn_b, wuq_ref[hd])
            rq = lax.rsqrt(_rowsum(qh * qh) / DK + EPS)
            xh = qh * rq
            dqh = jnp.where(latent, dq_ref[hd], 0.0)
            dyq = jnp.concatenate([dqh[:, :128], _rope_t(dqh[:, 128:], cos, sin)], axis=1)
            dqng += _colsum(dyq * xh)
            dxh = dyq * qng
            dqraw = (rq * (dxh - xh * (_rowsum(dxh * xh) / DK))).astype(BF16)
            dwuq_ref[hd] += _tn(dqraw, cqn_b)[:DK]
            dcqn += _nn(dqraw, wuq_ref[hd])
        dqng_ref[...] += dqng
        dqlg_ref[...] += _colsum(dcqn * cqh)
        dxh = dcqn * qlg
        dcq = rc * (dxh - cqh * jnp.mean(dxh * cqh, axis=-1, keepdims=True))

        ckv = ulo[:, QL:QL + KVL]
        r0 = lax.rsqrt(jnp.mean(ckv * ckv, axis=-1, keepdims=True) + EPS)
        ckvh = ckv * r0
        kvlg = kvlg_ref[...]
        ckvn_b = (ckvh * kvlg).astype(BF16)
        kr = ulo[:, 384:512]
        skr = _rowsum(kr * kr)
        kng = kng_ref[...]
        dkr = jnp.zeros((TB, 128), F32)
        dkng = jnp.zeros((1, DKP), F32)
        dckvn = jnp.zeros((TB, KVL), F32)
        for hd in range(NH):
            kn = _nn(ckvn_b, wukv_ref[hd])[:, :128]
            rk = lax.rsqrt((_rowsum(kn * kn) + skr) / DK + EPS)
            xh1 = kn * rk
            xh2 = kr * rk
            dkh = dk_ref[hd] * LN2
            d1 = dkh[:, :128]
            d2 = _rope_t(dkh[:, 128:], cos, sin)
            dkng += jnp.concatenate([_colsum(d1 * xh1), _colsum(d2 * xh2)], axis=1)
            dx1 = d1 * kng[:, :128]
            dx2 = d2 * kng[:, 128:]
            dot = (_rowsum(dx1 * xh1) + _rowsum(dx2 * xh2)) / DK
            dkv = jnp.concatenate([rk * (dx1 - xh1 * dot), dv_ref[hd]], axis=1).astype(BF16)
            dkr += rk * (dx2 - xh2 * dot)
            dwukv_ref[hd] += _tn(ckvn_b, dkv)
            dckvn += _nt(dkv, wukv_ref[hd])
        dkng_ref[...] += dkng
        dkvlg_ref[...] += _colsum(dckvn * ckvh)
        dxh = dckvn * kvlg
        dckv = r0 * (dxh - ckvh * jnp.mean(dxh * ckvh, axis=-1, keepdims=True))
        dlo_ref[...] = jnp.concatenate([dcq, dckv, dkr], axis=1).astype(BF16)

    row = lambda w: pl.BlockSpec((TB, w), lambda i: (i, 0))
    heads = lambda w: pl.BlockSpec((NH, TB, w), lambda i: (0, i, 0))
    return pl.pallas_call(
        body, name="qkv_bwd", grid=(nb,),
        in_specs=[row(512), pl.BlockSpec((NH, TB, DKP), lambda i: (0, jnp.maximum(i - off, 0), 0)), heads(DKP), heads(DV),
                  pl.BlockSpec((1, 8, 256), lambda i: (jnp.maximum(i - off, 0), 0, 0)), _full((TB, 256)), _full((1, QL)), _full((NH, DKP, QL)), _full((1, KVL)), _full((NH, KVL, 256)),
                  _full((1, DKP)), _full((1, DKP))],
        out_specs=[row(512), _full((NH, DK, QL)), _full((NH, KVL, 256)), _full((1, QL)), _full((1, KVL)),
                   _full((1, DKP)), _full((1, DKP))],
        out_shape=[jax.ShapeDtypeStruct((t_all, 512), BF16), jax.ShapeDtypeStruct((NH, DK, QL), F32),
                   jax.ShapeDtypeStruct((NH, KVL, 256), F32), jax.ShapeDtypeStruct((1, QL), F32),
                   jax.ShapeDtypeStruct((1, KVL), F32), jax.ShapeDtypeStruct((1, DKP), F32), jax.ShapeDtypeStruct((1, DKP), F32)],
        compiler_params=_params(("arbitrary",)),
    )(u, dq, dk, dv, cos, sin, q_lora_g, w_uq_t, kv_lora_g, w_ukv, qn_g, kn_g)


def _in_bwd(ctx, x, modsel, norm_g, dlo, dga, dgp, dpool, dxn, w_in_t):
    s_len, lc = x.shape[0], ctx.shape[0]
    t_all = s_len + lc
    off = lc // TB
    nb = t_all // TB
    nq = s_len // TB
    hb = TB // HALO
    n = TB + 2 * HALO

    def body(ctx_ref, x_ref, mod_ref, ng_ref, dlo_ref, dga_ref, dgp_ref, dp_ref, dpprev_ref, dpnext_ref, dxn_ref, win_ref,
             gx_ref, dwin_ref, dmod_ref, dng_ref):
        i = pl.program_id(0)
        j = i - off

        @pl.when(i == 0)
        def _():
            dwin_ref[...] = jnp.zeros_like(dwin_ref)
            dmod_ref[...] = jnp.zeros_like(dmod_ref)
            dng_ref[...] = jnp.zeros_like(dng_ref)

        latent = i >= off
        dp = dp_ref[...]
        prev = jnp.where(j <= 0, 0.0, dpprev_ref[...])
        nxt = jnp.where(j >= nq - 1, 0.0, dpnext_ref[...])
        win = jnp.concatenate([prev, dp, nxt], axis=0)
        tg = j * TB - HALO + lax.broadcasted_iota(jnp.int32, (n, 1), 0)
        dpin = []
        for g, w in enumerate(POOL_WINDOWS):
            cnt = jnp.maximum(jnp.minimum(tg + w // 2, s_len) - jnp.maximum(tg - w // 2, 0), 1).astype(F32)
            zq = win[:, g * 128:(g + 1) * 128] / cnt
            zq = zq + _shift_rows(zq, 1)
            for step in (1, 2, 4):
                if w >= 4 * step:
                    zq = _shift_rows(zq, -step) + _shift_rows(zq, step)
            dpin.append(zq[HALO:HALO + TB] - dp[:, g * 128:(g + 1) * 128])
        zero = jnp.zeros((TB, 512), BF16)
        du = [dlo_ref[...], jnp.where(latent, dga_ref[...], zero),
              jnp.where(latent, jnp.concatenate(dpin, axis=1).astype(BF16), zero), jnp.where(latent, dgp_ref[...], zero)]

        ng = ng_ref[...]
        xb = jnp.where(i < off, ctx_ref[...], x_ref[...])
        r, xh, xg, h, scale = _modulated(xb, mod_ref, ng)
        hb_ = h.astype(BF16)
        dh = jnp.zeros((TB, D), F32)
        for s, (lo, hi) in enumerate(SEG):
            dwin_ref[lo:hi, :] += _tn(du[s], hb_)
            dh += _nn(du[s], win_ref[lo:hi, :])
        is_lat = latent.astype(F32)
        dsh = _colsum(dh)
        dsc = _colsum(dh * xg)
        dmod_ref[0, 0:1, :] += dsh * (1.0 - is_lat)
        dmod_ref[0, 1:2, :] += dsc * (1.0 - is_lat)
        dmod_ref[1, 0:1, :] += dsh * is_lat
        dmod_ref[1, 1:2, :] += dsc * is_lat
        dxg = dh * (1.0 + scale)
        dng_ref[...] += _colsum(dxg * xh)
        dxh = dxg * ng
        gx_ref[...] = r * (dxh - xh * jnp.mean(dxh * xh, axis=-1, keepdims=True)) + dxn_ref[...]

    row = lambda w: pl.BlockSpec((TB, w), lambda i: (i, 0))
    lat = lambda w: pl.BlockSpec((TB, w), lambda i: (jnp.maximum(i - off, 0), 0))
    last8 = s_len // HALO - 1
    cspec, xspec, mspec = _token_specs(off)
    return pl.pallas_call(
        body, name="in_bwd", grid=(nb,),
        in_specs=[cspec, xspec, mspec, _full((1, D)), row(512), lat(512), lat(512), lat(512),
                  pl.BlockSpec((HALO, 512), lambda i: (jnp.maximum(jnp.maximum(i - off, 0) * hb - 1, 0), 0)),
                  pl.BlockSpec((HALO, 512), lambda i: (jnp.minimum((jnp.maximum(i - off, 0) + 1) * hb, last8), 0)),
                  lat(D), _full((DIN, D))],
        out_specs=[lat(D), _full((DIN, D)), _full((2, 2, D)), _full((1, D))],
        out_shape=[jax.ShapeDtypeStruct((s_len, D), F32), jax.ShapeDtypeStruct((DIN, D), F32),
                   jax.ShapeDtypeStruct((2, 2, D), F32), jax.ShapeDtypeStruct((1, D), F32)],
        compiler_params=_params(("arbitrary",)),
    )(ctx, x, modsel, norm_g, dlo, dga, dgp, dpool, dpool, dpool, dxn, w_in_t)


def _adamw_update(w_ref, g_ref, m_ref, v_ref, d_ref, mo_ref, vo_ref):
    gv = g_ref[...]
    mn = ADAM_B1 * m_ref[...] + (1.0 - ADAM_B1) * gv
    vn = ADAM_B2 * v_ref[...] + (1.0 - ADAM_B2) * (gv * gv)
    m_hat = mn / (1.0 - ADAM_B1 ** ADAM_STEP)
    v_hat = vn / (1.0 - ADAM_B2 ** ADAM_STEP)
    d_ref[...] = -ADAM_LR * (m_hat / (jnp.sqrt(v_hat) + ADAM_EPS) + ADAM_WD * w_ref[...])
    mo_ref[...] = mn
    vo_ref[...] = vn


def _adamw_many(ws, gs, ms, vs):
    n = len(ws)

    def body(*refs):
        for i in range(n):
            _adamw_update(refs[i], refs[n + i], refs[2 * n + i], refs[3 * n + i], refs[4 * n + i], refs[5 * n + i], refs[6 * n + i])

    def spec(w):
        rows, cols = w.shape
        return pl.BlockSpec((rows // 2, cols), lambda i: (i, 0)) if rows % 16 == 0 else _full((rows, cols))

    specs = [spec(w) for w in ws]
    shp = [jax.ShapeDtypeStruct(w.shape, F32) for w in ws]
    out = pl.pallas_call(body, name="adamw_many", grid=(2,), in_specs=specs * 4, out_specs=specs * 3, out_shape=shp * 3,
                         compiler_params=_params(("arbitrary",)))(*ws, *gs, *ms, *vs)
    return out[:n], out[n:2 * n], out[2 * n:]


def _adamw(w, g, m, v, name):
    rows, cols = w.shape
    rb = next(r for r in range(min(rows, 256), 0, -8) if rows % r == 0)

    def body(w_ref, g_ref, m_ref, v_ref, d_ref, mo_ref, vo_ref):
        _adamw_update(w_ref, g_ref, m_ref, v_ref, d_ref, mo_ref, vo_ref)

    spec = pl.BlockSpec((rb, cols), lambda i: (i, 0))
    shp = jax.ShapeDtypeStruct((rows, cols), F32)
    return pl.pallas_call(
        body, name=name, grid=(rows // rb,), in_specs=[spec] * 4, out_specs=[spec] * 3, out_shape=[shp] * 3,
        compiler_params=_params(("arbitrary",)),
    )(w, g, m, v)


class _Links:
    def __init__(self, send_sems, recv_sems):
        self.send_sems, self.recv_sems, self.sends = send_sems, recv_sems, []

    def send(self, src, dst, sem, to):
        cp = pltpu.make_async_remote_copy(src, dst, self.send_sems.at[sem], self.recv_sems.at[sem], device_id=to,
                                          device_id_type=MESH)
        cp.start()
        self.sends.append(cp)

    def arrived(self, dst, sem, frm):
        pltpu.make_async_remote_copy(dst, dst, self.send_sems.at[sem], self.recv_sems.at[sem], device_id=frm,
                                     device_id_type=MESH).wait_recv()

    def drain(self):
        for cp in self.sends:
            cp.wait_send()


def _half(ref, c, axis):
    size = ref.shape[axis - 2] // 2
    win = pl.ds(pl.multiple_of(c * size, 16 if axis == 0 else 128), size)
    idx = (win, slice(None)) if axis == 0 else (slice(None), win)
    return ref.at[(slice(None),) * (len(ref.shape) - 2) + idx]


def _select_rows(slots_ref, n_slots, row=0):
    sub = lax.broadcasted_iota(jnp.int32, (8, 1), 0)
    out = None
    for d in range(n_slots):
        r = jnp.where(sub == d, jnp.broadcast_to(slots_ref[d][row:row + 1, :], (8, slots_ref.shape[-1])), 0.0)
        out = r if out is None else out + r
    return out


def _gather(c, c_ctx, w_mod, b_mod_k, shards, axes):
    nw = len(shards)
    kw = w_mod.shape[1]

    def body(*refs):
        c_ref, cc_ref, wm_ref, b_ref = refs[:4]
        w_refs = refs[4:4 + nw]
        a16_ref, mod_ref = refs[4 + nw:6 + nw]
        g_refs = refs[6 + nw:6 + 2 * nw]
        a_ref, send_sems, recv_sems, local_sems = refs[6 + 2 * nw:]
        x, y, cc = lax.axis_index("x"), lax.axis_index("y"), lax.axis_index("c")
        me = 4 * x + 2 * y + cc
        k = 2 * x + y
        sibling = (x, y, 1 - cc)
        links = _Links(send_sems, recv_sems)
        chips = [_peer(x, y, cc, off + (0,)) for off in CHIPS3]
        locals_ = []
        for wi in range(nw):
            lc = pltpu.make_async_copy(w_refs[wi], g_refs[wi].at[k], local_sems.at[wi])
            lc.start()
            locals_.append(lc)
            for j, to in enumerate(chips):
                links.send(_half(w_refs[wi], cc, axes[wi]), _half(g_refs[wi].at[k], cc, axes[wi]), 10 + wi * 6 + j, to)
        cv = c_ref[...]
        a_ref[me] = jnp.broadcast_to(cv * _sig(cv), (8, D))
        for j, off in enumerate(PEERS7):
            links.send(a_ref.at[me], a_ref.at[me], j, _peer(x, y, cc, off))
        for j, off in enumerate(PEERS7):
            px, py, pc = _peer(x, y, cc, off)
            links.arrived(a_ref.at[4 * px + 2 * py + pc], j, (px, py, pc))
        ccv = cc_ref[...]
        sub = lax.broadcasted_iota(jnp.int32, (8, 1), 0)
        a16 = jnp.concatenate([_select_rows(a_ref, 8), jnp.where(sub == 0, jnp.broadcast_to(ccv * _sig(ccv), (8, D)), 0.0)], axis=0)
        a16_ref[...] = a16
        mod_ref[k] = _dot3(_nn, a16, wm_ref[...]) + b_ref[...]
        for j, to in enumerate(chips):
            links.send(mod_ref.at[k], mod_ref.at[k], 7 + j, to)
        for wi in range(nw):
            for j, (px, py, pc) in enumerate(chips):
                blk = _half(g_refs[wi].at[2 * px + py], cc, axes[wi])
                links.arrived(blk, 10 + wi * 6 + j, (px, py, pc))
                links.send(blk, blk, 10 + wi * 6 + 3 + j, sibling)
        for j, (px, py, pc) in enumerate(chips):
            links.arrived(mod_ref.at[2 * px + py], 7 + j, (px, py, pc))
        for wi in range(nw):
            for j, (px, py, pc) in enumerate(chips):
                links.arrived(_half(g_refs[wi].at[2 * px + py], 1 - cc, axes[wi]), 10 + wi * 6 + 3 + j, sibling)
        links.drain()
        for lc in locals_:
            lc.wait()

    nsem = 10 + 6 * nw
    return pl.pallas_call(
        body, name="gather", in_specs=[VM] * 4 + [ANY] * nw, out_specs=[VM, VM] + [ANY] * nw,
        out_shape=[jax.ShapeDtypeStruct((16, D), F32), jax.ShapeDtypeStruct((4, 16, kw), F32)]
        + [jax.ShapeDtypeStruct((4,) + s.shape, s.dtype) for s in shards],
        scratch_shapes=[pltpu.VMEM((8, 8, D), F32), pltpu.SemaphoreType.DMA((nsem,)), pltpu.SemaphoreType.DMA((nsem,)),
                        pltpu.SemaphoreType.DMA((nw,))],
        compiler_params=pltpu.CompilerParams(vmem_limit_bytes=VMEM_LIMIT),
    )(c, c_ctx, w_mod, b_mod_k, *shards)


SMALL_ROW_WIDTHS = (D, QL, KVL, DKP, DKP, 512, 128)
SMALL_OUT_WIDTHS = (D, QL, KVL, DK, DK, 512, 1)


def _reduce(grads, axes, smalls, w_pool_g, dmod8, a16, w_mod, c_ctx):
    nw = len(grads)
    ns = len(smalls)
    kw = w_mod.shape[1]
    halves = []
    for g, ax in zip(grads, axes):
        halves.append((g.shape[1] // 2, g.shape[2]) if ax == 0 else (g.shape[1], g.shape[2] // 2))

    def body(*refs):
        g_refs = refs[:nw]
        small_refs = refs[nw:nw + ns]
        wp_ref, dm_ref, a16_ref, wm_ref, cc_ref = refs[nw + ns:nw + ns + 5]
        o = nw + ns + 5
        r_refs = refs[o:o + nw]
        small_outs = refs[o + nw:o + nw + ns]
        rwp_ref, gw_ref, gb_ref, gc_ref = refs[o + nw + ns:o + nw + ns + 4]
        o = o + nw + ns + 4
        own, sib, part, got = (refs[o + i * nw:o + (i + 1) * nw] for i in range(4))
        smbuf, wps, wpg, dm_all, pc_all, send_sems, recv_sems, local_sems = refs[o + 4 * nw:]
        x, y, cc = lax.axis_index("x"), lax.axis_index("y"), lax.axis_index("c")
        me = 4 * x + 2 * y + cc
        k = 2 * x + y
        sibling = (x, y, 1 - cc)
        links = _Links(send_sems, recv_sems)
        chips = [_peer(x, y, cc, off + (0,)) for off in CHIPS3]
        peers = [_peer(x, y, cc, off) for off in PEERS7]
        big, sm0, wp0, dm0, pc0 = 0, 5 * nw, 5 * nw + 7, 5 * nw + 14, 5 * nw + 21

        locals_ = []
        for wi in range(nw):
            lc = pltpu.make_async_copy(_half(g_refs[wi], cc, axes[wi]), own[wi], local_sems.at[wi])
            lc.start()
            locals_.append(lc)
            links.send(_half(g_refs[wi], 1 - cc, axes[wi]), sib[wi], big + wi * 5, sibling)
        slot = smbuf.at[me]
        slot[...] = jnp.zeros((8, D), F32)
        for r, (ref, w) in enumerate(zip(small_refs, SMALL_ROW_WIDTHS)):
            slot[r:r + 1, 0:w] = jnp.broadcast_to(ref[...], (1, w))
        links.send(wp_ref, wps, wp0, sibling)
        dm_all[me] = dm_ref[...]
        for j, peer in enumerate(peers):
            links.send(dm_all.at[me], dm_all.at[me], dm0 + j, peer)
            links.send(smbuf.at[me], smbuf.at[me], sm0 + j, peer)
        links.arrived(wps, wp0, sibling)
        wpg[k] = (wp_ref[...] + wps[...]).astype(BF16)
        for j, to in enumerate(chips):
            links.send(wpg.at[k], wpg.at[k], wp0 + 1 + j, to)
        for wi in range(nw):
            locals_[wi].wait()
            links.arrived(sib[wi], big + wi * 5, sibling)
            part[wi][...] = (own[wi][...] + sib[wi][...]).astype(BF16)
            got[wi][k] = part[wi][k]
            for j, (px, py, pc) in enumerate(chips):
                links.send(part[wi].at[2 * px + py], got[wi].at[k], big + wi * 5 + 1 + j, (px, py, pc))
        for j, (px, py, pc) in enumerate(peers):
            links.arrived(dm_all.at[4 * px + 2 * py + pc], dm0 + j, (px, py, pc))
        dmc = dm_all[0][1:2, :]
        dml = dm_all[0][0:1, :]
        for d in range(1, 8):
            dmc = dmc + dm_all[d][1:2, :]
            dml = dml + dm_all[d][0:1, :]
        gb_ref[...] = dml + dmc
        sub = lax.broadcasted_iota(jnp.int32, (8, 1), 0)
        b16 = jnp.concatenate([_select_rows(dm_all, 8), jnp.where(sub == 0, jnp.broadcast_to(dmc, (8, 3 * D)), 0.0)], axis=0)
        bk = jnp.zeros((16, kw), F32)
        for kk in range(4):
            bk = bk + jnp.where(k == kk, b16[:, kk * kw:(kk + 1) * kw], 0.0)
        gw_ref[...] = _dot3(_tn, a16_ref[...], bk)
        pc_all[k] = _dot3(_nt, jnp.broadcast_to(bk[8:9, :], (8, kw)), wm_ref[...])
        for j, to in enumerate(chips):
            links.send(pc_all.at[k], pc_all.at[k], pc0 + j, to)
        for j, (px, py, pc) in enumerate(peers):
            links.arrived(smbuf.at[4 * px + 2 * py + pc], sm0 + j, (px, py, pc))
        tot = smbuf[0]
        for d in range(1, 8):
            tot = tot + smbuf[d]
        for r, (ref, w) in enumerate(zip(small_outs, SMALL_OUT_WIDTHS)):
            ref[...] = tot[r:r + 1, 0:w]
        for j, (px, py, pc) in enumerate(chips):
            links.arrived(wpg.at[2 * px + py], wp0 + 1 + j, (px, py, pc))
        wpt = wpg[0].astype(F32)
        for kk in range(1, 4):
            wpt = wpt + wpg[kk].astype(F32)
        rwp_ref[...] = wpt
        for wi in range(nw):
            for j, (px, py, pc) in enumerate(chips):
                links.arrived(got[wi].at[2 * px + py], big + wi * 5 + 1 + j, (px, py, pc))
            total = got[wi][0].astype(F32)
            for kk in range(1, 4):
                total = total + got[wi][kk].astype(F32)
            mine = _half(r_refs[wi], cc, axes[wi])
            mine[...] = total
            links.send(mine, mine, big + wi * 5 + 4, sibling)
        for j, (px, py, pc) in enumerate(chips):
            links.arrived(pc_all.at[2 * px + py], pc0 + j, (px, py, pc))
        ccv = cc_ref[...]
        sg = _sig(ccv)
        gc_ref[...] = (pc_all[0][0:1, :] + pc_all[1][0:1, :] + pc_all[2][0:1, :] + pc_all[3][0:1, :]) * (sg * (1.0 + ccv * (1.0 - sg)))
        for wi in range(nw):
            links.arrived(_half(r_refs[wi], 1 - cc, axes[wi]), big + wi * 5 + 4, sibling)
        links.drain()

    nsem = 5 * nw + 24
    quads = [(4,) + h for h in halves]
    return pl.pallas_call(
        body, name="reduce", in_specs=[ANY] * nw + [VM] * (ns + 5), out_specs=[VM] * (nw + ns + 4),
        out_shape=[jax.ShapeDtypeStruct(g.shape[1:], F32) for g in grads]
        + [jax.ShapeDtypeStruct((1, w), F32) for w in SMALL_OUT_WIDTHS]
        + [jax.ShapeDtypeStruct(w_pool_g.shape, F32), jax.ShapeDtypeStruct((D, kw), F32), jax.ShapeDtypeStruct((1, 3 * D), F32),
           jax.ShapeDtypeStruct((1, D), F32)],
        scratch_shapes=[pltpu.VMEM(q, F32) for q in quads] + [pltpu.VMEM(q, F32) for q in quads]
        + [pltpu.VMEM(q, BF16) for q in quads] + [pltpu.VMEM(q, BF16) for q in quads]
        + [pltpu.VMEM((8, 8, D), F32), pltpu.VMEM(w_pool_g.shape, F32), pltpu.VMEM((4,) + w_pool_g.shape, BF16),
           pltpu.VMEM((8, 8, 3 * D), F32),
           pltpu.VMEM((4, 8, D), F32)]
        + [pltpu.SemaphoreType.DMA((nsem,)), pltpu.SemaphoreType.DMA((nsem,)), pltpu.SemaphoreType.DMA((nw,))],
        compiler_params=pltpu.CompilerParams(vmem_limit_bytes=VMEM_LIMIT),
    )(*grads, *smalls, w_pool_g, dmod8, a16, w_mod, c_ctx)


def _rope_tables(s_len):
    rows = s_len // GRID_W
    per = TB // GRID_W
    n_freq = 16
    inv = ROPE_BASE ** (-jnp.arange(n_freq, dtype=F32) / n_freq)
    ang_r = jnp.arange(rows, dtype=F32)[:, None] * inv
    ang_c = jnp.arange(GRID_W, dtype=F32)[:, None] * inv
    by_row, by_col = [], []
    for fn, pad in ((jnp.cos, 1.0), (jnp.sin, 0.0)):
        r = jnp.concatenate([fn(ang_r), fn(ang_r), jnp.zeros((rows, 96), F32)], axis=1).reshape(rows // per, per, 128)
        by_row.append(jnp.pad(r, ((0, 0), (0, 8 - per), (0, 0))))
        cpart = jnp.concatenate([jnp.zeros((GRID_W, 32), F32), fn(ang_c), fn(ang_c), jnp.full((GRID_W, 64), pad, F32)], axis=1)
        by_col.append(jnp.tile(cpart, (per, 1)))
    return jnp.concatenate(by_row, axis=-1), jnp.concatenate(by_col, axis=-1)


def kernel(x, c, ctx, c_ctx, w_mod, b_mod, norm_g, w_in, q_lora_g, w_uq, kv_lora_g, w_ukv, q_norm_g, k_norm_g, w_pool, pool_scale, w_out, loss_target, m_c_ctx, m_w_mod, m_b_mod, m_norm_g, m_w_in, m_q_lora_g, m_w_uq, m_kv_lora_g, m_w_ukv, m_q_norm_g, m_k_norm_g, m_w_pool, m_pool_scale, m_w_out, v_c_ctx, v_w_mod, v_b_mod, v_norm_g, v_w_in, v_q_lora_g, v_w_uq, v_kv_lora_g, v_w_ukv, v_q_norm_g, v_k_norm_g, v_w_pool, v_pool_scale, v_w_out):
    xi, yi, ci = lax.axis_index("x"), lax.axis_index("y"), lax.axis_index("c")
    me = 4 * xi + 2 * yi + ci
    k = 2 * xi + yi
    s_len = x.shape[1]
    lc = ctx.shape[1]
    kw = w_mod.shape[2]
    weights = dict(c_ctx=c_ctx, w_mod=w_mod, b_mod=b_mod, norm_g=norm_g, w_in=w_in, q_lora_g=q_lora_g, w_uq=w_uq,
                   kv_lora_g=kv_lora_g, w_ukv=w_ukv, q_norm_g=q_norm_g, k_norm_g=k_norm_g, w_pool=w_pool,
                   pool_scale=pool_scale, w_out=w_out)
    m_in = dict(c_ctx=m_c_ctx, w_mod=m_w_mod, b_mod=m_b_mod, norm_g=m_norm_g, w_in=m_w_in, q_lora_g=m_q_lora_g, w_uq=m_w_uq,
                kv_lora_g=m_kv_lora_g, w_ukv=m_w_ukv, q_norm_g=m_q_norm_g, k_norm_g=m_k_norm_g, w_pool=m_w_pool,
                pool_scale=m_pool_scale, w_out=m_w_out)
    v_in = dict(c_ctx=v_c_ctx, w_mod=v_w_mod, b_mod=v_b_mod, norm_g=v_norm_g, w_in=v_w_in, q_lora_g=v_q_lora_g, w_uq=v_w_uq,
                kv_lora_g=v_kv_lora_g, w_ukv=v_w_ukv, q_norm_g=v_q_norm_g, k_norm_g=v_k_norm_g, w_pool=v_w_pool,
                pool_scale=v_pool_scale, w_out=v_w_out)
    order = ["c_ctx", "w_mod", "b_mod", "norm_g", "w_in", "q_lora_g", "w_uq", "kv_lora_g", "w_ukv", "q_norm_g", "k_norm_g",
             "w_pool", "pool_scale", "w_out"]
    transposed = ("w_in", "w_uq")
    as2d = lambda n, a: jnp.transpose(a[0]) if n in transposed else a.reshape(-1, a.shape[-1])
    back = lambda n, a: jnp.transpose(a)[None] if n in transposed else a.reshape(weights[n].shape)

    c_ctx2 = c_ctx.reshape(1, D)
    b_mod_k = lax.dynamic_slice(b_mod, (0, k * kw), (1, kw))
    split = (1, 0, 0, 0)
    a16, mod_all, g_in, g_uq, g_ukv, g_out = _gather(
        c, c_ctx2, w_mod[0], b_mod_k,
        [as2d("w_in", w_in).astype(BF16), as2d("w_uq", w_uq).astype(BF16), w_ukv[0].astype(BF16), w_out[0].astype(BF16)], split)
    mod_me = lax.dynamic_index_in_dim(mod_all, me, axis=1, keepdims=False).reshape(3, D)
    mod_c = mod_all[:, 8, :].reshape(3, D)
    modsel = jnp.stack([mod_c, mod_me])
    w_in_t = g_in.reshape(DIN, D)
    w_uq_t = jnp.pad(g_uq, ((0, 0), (0, DKP - DK), (0, 0)))
    w_out_f = g_out.reshape(D, D)
    qn_g = jnp.pad(q_norm_g, ((0, 0), (0, DKP - DK)))
    kn_g = jnp.pad(k_norm_g, ((0, 0), (0, DKP - DK)))
    w_pool_b = w_pool[0].astype(BF16)
    cos, sin = _rope_tables(s_len)

    u, q, kk, v = _fwd_in(ctx[0], x[0], modsel, norm_g, w_in_t, q_lora_g, w_uq_t, kv_lora_g, g_ukv, qn_g, kn_g, cos, sin)
    attn, lse = _attn_fwd(q, kk, v, s_len)
    (dxn, dattn, dga, dgp, dpool, dw_out, dgate, dps, dw_pool, loss) = _out_stage(
        attn.reshape(s_len // Q_BLOCK, Q_BLOCK, NH * DV), u, x[0], loss_target[0], modsel[1, 2:3, :], w_pool_b, pool_scale,
        w_out_f, lc)
    dattn = dattn.reshape(s_len, NH * DV)
    dq, dk, dv = _attn_bwd(q, kk, v, dattn, attn, lse, s_len)
    dlo, dw_uq_t, dw_ukv, dqlg, dkvlg, dqng, dkng = _qkv_bwd(u, dq, dk, dv, cos, sin, q_lora_g, w_uq_t, kv_lora_g, g_ukv,
                                                            qn_g, kn_g, s_len)
    gx, dw_in_t, dmod, dng = _in_bwd(ctx[0], x[0], modsel, norm_g, dlo, dga, dgp, dpool, dxn, w_in_t)

    dmod_l = jnp.concatenate([dmod[1, 0], dmod[1, 1], dgate[0]]).reshape(1, 3 * D)
    dmod_c = jnp.concatenate([dmod[0, 0], dmod[0, 1], jnp.zeros((D,), F32)]).reshape(1, 3 * D)
    dmod8 = jnp.concatenate([dmod_l, dmod_c, jnp.zeros((6, 3 * D), F32)], axis=0)
    (r_in, r_uq, r_ukv, r_out, g_ng, g_qlg, g_kvlg, g_qng, g_kng, g_ps, loss_all, g_wp, g_w_mod, g_b_mod, g_c_ctx) = _reduce(
        [dw_in_t.reshape(4, DIN // 4, D), dw_uq_t, dw_ukv, dw_out.reshape(4, D // 4, D)], split,
        [dng, dqlg, dkvlg, dqng, dkng, dps, loss], dw_pool, dmod8, a16, w_mod[0], c_ctx2)
    g2d = dict(c_ctx=g_c_ctx, b_mod=g_b_mod, w_mod=g_w_mod, w_in=r_in, w_uq=r_uq, w_ukv=r_ukv, w_out=r_out, norm_g=g_ng,
               q_lora_g=g_qlg, kv_lora_g=g_kvlg, q_norm_g=g_qng, k_norm_g=g_kng, pool_scale=g_ps, w_pool=g_wp.reshape(512, 128))

    d2d, m2d, v2d = {}, {}, {}
    d2d["w_mod"], m2d["w_mod"], v2d["w_mod"] = _adamw(as2d("w_mod", w_mod), g2d["w_mod"], as2d("w_mod", m_w_mod),
                                                      as2d("w_mod", v_w_mod), "adamw_w_mod")
    rest = [n for n in order if n != "w_mod"]
    outs = _adamw_many([as2d(n, weights[n]) for n in rest], [g2d[n] for n in rest], [as2d(n, m_in[n]) for n in rest],
                       [as2d(n, v_in[n]) for n in rest])
    for dst, arrs in zip((d2d, m2d, v2d), outs):
        dst.update(dict(zip(rest, arrs)))

    return (loss_all[0, 0], gx[None], *[back(n, g2d[n]) for n in order], *[back(n, d2d[n]) for n in order],
            *[back(n, m2d[n]) for n in order], *[back(n, v2d[n]) for n in order])
```

```python
import jax
import jax.numpy as jnp
from jax import lax
from jax.experimental import pallas as pl
from jax.experimental.pallas import tpu as pltpu

F32 = jnp.float32
BF16 = jnp.bfloat16
MESH = pl.DeviceIdType.MESH

D = 1024
NH = 4
DK = 192
DKP = 256
DV = 128
QL = 256
KVL = 128
DIN = 1984
U_LO = 448
SEG = ((0, 512), (448, 960), (960, 1472), (1472, 1984))
DU = 2048
POOL_WINDOWS = (2, 4, 8, 16)
HALO = 8
EPS = 1e-6
ROPE_BASE = 10000.0
GRID_W = 64
Q_BLOCK = 128
TB = 256
BWD_QBLOCKS = 1
SCALE = DK ** -0.5
LOG2E = 1.4426950408889634
LN2 = 0.6931471805599453
VMEM_LIMIT = 56 * 1024 * 1024

ADAM_LR = 0.001
ADAM_B1 = 0.9
ADAM_B2 = 0.999
ADAM_EPS = 1e-08
ADAM_WD = 0.01
ADAM_STEP = 10

CHIPS3 = ((1, 0), (0, 1), (1, 1))
PEERS7 = tuple((dx, dy, dc) for dx in (0, 1) for dy in (0, 1) for dc in (0, 1) if (dx, dy, dc) != (0, 0, 0))

VM = pl.BlockSpec(memory_space=pltpu.VMEM)
ANY = pl.BlockSpec(memory_space=pl.ANY)


def _nn(a, b):
    return jnp.dot(a, b, preferred_element_type=F32)


def _nt(a, b):
    return lax.dot_general(a, b, (((1,), (1,)), ((), ())), preferred_element_type=F32)


def _tn(a, b):
    return lax.dot_general(a, b, (((0,), (0,)), ((), ())), preferred_element_type=F32)


def _split3(a):
    a0 = a.astype(BF16)
    r = a - a0.astype(F32)
    a1 = r.astype(BF16)
    a2 = (r - a1.astype(F32)).astype(BF16)
    return a0, a1, a2


def _dot3(dot, a, b):
    sa = _split3(a)
    sb = _split3(b)
    out = None
    for i in range(3):
        for j in range(3 - i):
            t = dot(sa[i], sb[j])
            out = t if out is None else out + t
    return out


def _sig(x):
    return 1.0 / (1.0 + jnp.exp(-x))


def _rot(t):
    src = lax.broadcasted_iota(jnp.int32, (128, 128), 0)
    dst = lax.broadcasted_iota(jnp.int32, (128, 128), 1)
    first = (dst % 32) < 16
    perm = jnp.where(first & (src == dst + 16), -1.0, jnp.where(~first & (src == dst - 16), 1.0, 0.0)).astype(BF16)
    hi = t.astype(BF16)
    lo = (t - hi.astype(F32)).astype(BF16)
    return _nn(hi, perm) + _nn(lo, perm)


def _rope(t, cos, sin):
    return t * cos + _rot(t) * sin


def _rope_t(t, cos, sin):
    return t * cos - _rot(t * sin)


def _rope_block(rows_ref, cols_ref, is_ctx):
    lane = lax.broadcasted_iota(jnp.int32, (TB, 256), 1) % 128
    rows = jnp.concatenate([jnp.broadcast_to(rows_ref[0, r:r + 1, :], (GRID_W, 256)) for r in range(TB // GRID_W)], axis=0)
    cs = jnp.where(lane < 32, rows, cols_ref[...])
    return jnp.where(is_ctx, 1.0, cs[:, :128]), jnp.where(is_ctx, 0.0, cs[:, 128:])


def _shift_rows(z, k):
    n = z.shape[0]
    return pltpu.roll(z, (n - k) % n, 0)


def _colsum(a):
    return jnp.sum(a, axis=0, keepdims=True)


def _rowsum(a):
    return jnp.sum(a, axis=-1, keepdims=True)


def _row_layout(col):
    return jnp.transpose(jnp.broadcast_to(col, (col.shape[0], 128)))[0:8, :]


def _params(sem=None):
    return pltpu.CompilerParams(dimension_semantics=sem, vmem_limit_bytes=VMEM_LIMIT)


def _full(shape):
    nd = len(shape)
    return pl.BlockSpec(shape, lambda *_: (0,) * nd)


def _peer(x, y, c, off):
    dx, dy, dc = off
    return ((x + dx) % 2, (y + dy) % 2, (c + dc) % 2)


def _token_specs(off):
    ctx = pl.BlockSpec((TB, D), lambda i: (jnp.minimum(i, off - 1), 0))
    lat = pl.BlockSpec((TB, D), lambda i: (jnp.maximum(i - off, 0), 0))
    mod = pl.BlockSpec((1, 3, D), lambda i: (jnp.minimum(i // off, 1), 0, 0))
    return ctx, lat, mod


def _modulated(x, mod_ref, ng):
    shift = mod_ref[0, 0:1, :]
    scale = mod_ref[0, 1:2, :]
    r = lax.rsqrt(jnp.mean(x * x, axis=-1, keepdims=True) + EPS)
    xh = x * r
    xg = xh * ng
    return r, xh, xg, xg * (1.0 + scale) + shift, scale


def _fwd_in(ctx, x, modsel, norm_g, w_in_t, q_lora_g, w_uq_t, kv_lora_g, w_ukv, qn_g, kn_g, cos, sin):
    s_len, lc = x.shape[0], ctx.shape[0]
    t_all = s_len + lc
    nb = t_all // TB
    off = lc // TB

    def body(ctx_ref, x_ref, mod_ref, ng_ref, win_ref, qlg_ref, wuq_ref, kvlg_ref, wukv_ref, qng_ref, kng_ref, cos_ref, sin_ref,
             u_ref, q_ref, k_ref, v_ref):
        is_ctx = pl.program_id(0) < off
        xb = jnp.where(is_ctx, ctx_ref[...], x_ref[...])
        _, _, _, h, _ = _modulated(xb, mod_ref, ng_ref[...])
        hb = h.astype(BF16)
        lane = lax.broadcasted_iota(jnp.int32, (TB, 512), 1)
        ulo = jnp.where(lane < U_LO, _nt(hb, win_ref[SEG[0][0]:SEG[0][1], :]), 0.0)
        u_ref[:, 0:512] = ulo
        for j in range(1, 4):
            u_ref[:, j * 512:(j + 1) * 512] = _nt(hb, win_ref[SEG[j][0]:SEG[j][1], :])
        cos, sin = _rope_block(cos_ref, sin_ref, is_ctx)
        cq = ulo[:, 0:QL]
        cqn = (cq * lax.rsqrt(jnp.mean(cq * cq, axis=-1, keepdims=True) + EPS) * qlg_ref[...]).astype(BF16)
        qng = qng_ref[...]
        ckv = ulo[:, QL:QL + KVL]
        ckvn = (ckv * lax.rsqrt(jnp.mean(ckv * ckv, axis=-1, keepdims=True) + EPS) * kvlg_ref[...]).astype(BF16)
        qhs = [_nt(cqn, wuq_ref[hd]) for hd in range(NH)]
        kvs = [_nn(ckvn, wukv_ref[hd]) for hd in range(NH)]
        for hd in range(NH):
            qh = qhs[hd]
            qn = qh * lax.rsqrt(_rowsum(qh * qh) / DK + EPS) * qng
            q_ref[hd] = (jnp.concatenate([qn[:, :128], _rope(qn[:, 128:], cos, sin)], axis=1) * (SCALE * LOG2E)).astype(BF16)
        kr = ulo[:, 384:512]
        skr = _rowsum(kr * kr)
        kng = kng_ref[...]
        kr_roped = _rope(kr * kng[:, 128:], cos, sin)
        for hd in range(NH):
            kv = kvs[hd]
            kn = kv[:, :128]
            rk = lax.rsqrt((_rowsum(kn * kn) + skr) / DK + EPS)
            k_ref[hd] = jnp.concatenate([kn * rk * kng[:, :128], kr_roped * rk], axis=1).astype(BF16)
            v_ref[hd] = kv[:, 128:].astype(BF16)

    row = lambda w: pl.BlockSpec((TB, w), lambda i: (i, 0))
    heads = lambda w: pl.BlockSpec((NH, TB, w), lambda i: (0, i, 0))
    cspec, xspec, mspec = _token_specs(off)
    return pl.pallas_call(
        body, name="fwd_in", grid=(nb,),
        in_specs=[cspec, xspec, mspec, _full((1, D)), _full((DIN, D)), _full((1, QL)), _full((NH, DKP, QL)), _full((1, KVL)),
                  _full((NH, KVL, 256)), _full((1, DKP)), _full((1, DKP)),
                  pl.BlockSpec((1, 8, 256), lambda i: (jnp.maximum(i - off, 0), 0, 0)), _full((TB, 256))],
        out_specs=[row(DU), heads(DKP), heads(DKP), heads(DV)],
        out_shape=[jax.ShapeDtypeStruct((t_all, DU), F32), jax.ShapeDtypeStruct((NH, t_all, DKP), BF16),
                   jax.ShapeDtypeStruct((NH, t_all, DKP), BF16), jax.ShapeDtypeStruct((NH, t_all, DV), BF16)],
        compiler_params=_params(("arbitrary",)),
    )(ctx, x, modsel, norm_g, w_in_t, q_lora_g, w_uq_t, kv_lora_g, w_ukv, qn_g, kn_g, cos, sin)


def _attn_fwd(q, k, v, s_len):
    t_all = q.shape[1]
    off = (t_all - s_len) // TB
    nq = s_len // TB
    nsub = next(n for n in (4, 2, 1) if nq % n == 0)

    def body(*refs):
        q_refs = refs[:nsub]
        k_ref, v_ref, o_ref, lse_ref = refs[nsub:]
        for sb in range(nsub):
            s = _nt(q_refs[sb][0], k_ref[0])
            m = jnp.max(s, axis=-1, keepdims=True)
            e = jnp.exp2(s - m)
            l = _rowsum(e)
            o_ref[sb * TB:(sb + 1) * TB, :] = _nn(e.astype(BF16), v_ref[0]) / l
            lse_ref[0, sb] = _row_layout(m + jnp.log2(l))

    qspec = lambda sb: pl.BlockSpec((1, TB, DKP), lambda h, i: (h, i * nsub + sb + off, 0))
    return pl.pallas_call(
        body, name="attn_fwd", grid=(NH, nq // nsub),
        in_specs=[qspec(sb) for sb in range(nsub)]
        + [pl.BlockSpec((1, t_all, DKP), lambda h, i: (h, 0, 0)), pl.BlockSpec((1, t_all, DV), lambda h, i: (h, 0, 0))],
        out_specs=[pl.BlockSpec((nsub * TB, DV), lambda h, i: (i, h)), pl.BlockSpec((1, nsub, 8, TB), lambda h, i: (h, i, 0, 0))],
        out_shape=[jax.ShapeDtypeStruct((s_len, NH * DV), F32), jax.ShapeDtypeStruct((NH, nq, 8, TB), F32)],
        compiler_params=_params(("arbitrary", "arbitrary")),
    )(*([q] * nsub), k, v)


def _out_stage(attn, u, x, target, gate, w_pool, pool_scale, w_out, lc):
    s_len = x.shape[0]
    t_all = s_len + lc
    off = lc // TB
    nq = s_len // TB
    hb = TB // HALO
    nqb = s_len // Q_BLOCK
    jb = TB // nqb

    def body(attn_ref, ga_ref, pin_ref, pprev_ref, pnext_ref, gp_ref, x_ref, tgt_ref, gate_ref, wp_ref, ps_ref, wo_ref,
             dxn_ref, dattn_ref, dga_ref, dgp_ref, dpool_ref, dwo_ref, dgate_ref, dps_ref, dwp_ref, loss_ref):
        i = pl.program_id(0)

        @pl.when(i == 0)
        def _():
            dwo_ref[...] = jnp.zeros_like(dwo_ref)
            dgate_ref[...] = jnp.zeros_like(dgate_ref)
            dps_ref[...] = jnp.zeros_like(dps_ref)
            dwp_ref[...] = jnp.zeros_like(dwp_ref)
            loss_ref[...] = jnp.zeros_like(loss_ref)

        attn = jnp.concatenate([attn_ref[:, jj, :] for jj in range(jb)], axis=0)
        ga = ga_ref[...]
        gp = gp_ref[...]
        pin = pin_ref[...]
        prev = jnp.where(i == 0, 0.0, pprev_ref[...])
        nxt = jnp.where(i == nq - 1, 0.0, pnext_ref[...])
        win = jnp.concatenate([prev, pin, nxt], axis=0)
        tg = i * TB + lax.broadcasted_iota(jnp.int32, (TB, 1), 0)
        pooled = []
        for g, w in enumerate(POOL_WINDOWS):
            a = win[:, g * 128:(g + 1) * 128]
            p = _shift_rows(a, -1) + a
            for step in (1, 2, 4):
                if w >= 4 * step:
                    p = _shift_rows(p, -step) + _shift_rows(p, step)
            cnt = (jnp.minimum(tg + w // 2, s_len) - jnp.maximum(tg - w // 2, 0)).astype(F32)
            pooled.append(p[HALO:HALO + TB] / cnt - a[HALO:HALO + TB])
        pooled_b = [p.astype(BF16) for p in pooled]
        z = jnp.concatenate([_nn(pooled_b[g], wp_ref[g]) for g in range(4)], axis=1)
        ps = ps_ref[...]
        yp = z * ps
        sga = _sig(ga)
        sila = ga * sga
        sgp = _sig(gp)
        silp = gp * sgp
        br = jnp.concatenate([sila * attn, silp * yp], axis=1).astype(BF16)
        y = _nn(br, wo_ref[...])
        gate = gate_ref[...]
        err = x_ref[...] + gate * y - tgt_ref[...]
        loss_ref[...] += _colsum(_rowsum(err * err)) * (0.5 / D)
        dxn = err * (1.0 / D)
        dxn_ref[...] = dxn
        dgate_ref[...] += _colsum(dxn * y)
        dy = (dxn * gate).astype(BF16)
        dwo_ref[...] += _tn(br, dy)
        dbr = _nt(dy, wo_ref[...])
        dbra = dbr[:, :512]
        dbrp = dbr[:, 512:]
        dattn = dbra * sila
        for jj in range(jb):
            dattn_ref[:, jj, :] = dattn[jj * nqb:(jj + 1) * nqb]
        dga_ref[...] = (dbra * attn * (sga * (1.0 + ga * (1.0 - sga)))).astype(BF16)
        dgp_ref[...] = (dbrp * yp * (sgp * (1.0 + gp * (1.0 - sgp)))).astype(BF16)
        dyp = dbrp * silp
        dps_ref[...] += _colsum(dyp * z)
        dz = (dyp * ps).astype(BF16)
        dpool = []
        for g in range(4):
            dzg = dz[:, g * 128:(g + 1) * 128]
            dwp_ref[g] += _tn(pooled_b[g], dzg)
            dpool.append(_nt(dzg, wp_ref[g]))
        dpool_ref[...] = jnp.concatenate(dpool, axis=1)

    lat = lambda w: pl.BlockSpec((TB, w), lambda i: (i, 0))
    perm = pl.BlockSpec((nqb, jb, 512), lambda i: (0, i, 0))
    ucol = lambda j: pl.BlockSpec((TB, 512), lambda i: (i + off, j))
    last8 = t_all // HALO - 1
    return pl.pallas_call(
        body, name="out_stage", grid=(nq,),
        in_specs=[perm, ucol(1), ucol(2),
                  pl.BlockSpec((HALO, 512), lambda i: ((i + off) * hb - 1, 2)),
                  pl.BlockSpec((HALO, 512), lambda i: (jnp.minimum((i + off + 1) * hb, last8), 2)),
                  ucol(3), lat(D), lat(D), _full((1, D)), _full((4, 128, 128)), _full((1, 512)), _full((D, D))],
        out_specs=[lat(D), perm, lat(512), lat(512), lat(512),
                   _full((D, D)), _full((1, D)), _full((1, 512)), _full((4, 128, 128)), _full((1, 1))],
        out_shape=[jax.ShapeDtypeStruct((s_len, D), F32), jax.ShapeDtypeStruct((nqb, Q_BLOCK, 512), F32),
                   jax.ShapeDtypeStruct((s_len, 512), BF16), jax.ShapeDtypeStruct((s_len, 512), BF16),
                   jax.ShapeDtypeStruct((s_len, 512), F32),
                   jax.ShapeDtypeStruct((D, D), F32), jax.ShapeDtypeStruct((1, D), F32), jax.ShapeDtypeStruct((1, 512), F32),
                   jax.ShapeDtypeStruct((4, 128, 128), F32), jax.ShapeDtypeStruct((1, 1), F32)],
        compiler_params=_params(("arbitrary",)),
    )(attn, u, u, u, u, u, x, target, gate, w_pool, pool_scale, w_out)


def _attn_bwd(q, k, v, dattn, attn, lse, s_len):
    t_all = q.shape[1]
    off = (t_all - s_len) // TB
    nq = s_len // TB
    nch = 4
    chunks = [(c * (t_all // nch), t_all // nch) for c in range(nch)]
    nsub = next(n for n in (BWD_QBLOCKS, 2, 1) if nq % n == 0)
    tq = nsub * TB

    def body(*refs):
        q_refs = refs[:nsub]
        k_ref, v_ref, do_ref, o_ref, lse_ref, dq_ref, dk_ref, dv_ref = refs[nsub:]
        i = pl.program_id(1)

        @pl.when(i == 0)
        def _():
            dk_ref[...] = jnp.zeros_like(dk_ref)
            dv_ref[...] = jnp.zeros_like(dv_ref)

        qb = jnp.concatenate([r[0] for r in q_refs], axis=0)
        delta_r = _row_layout(_rowsum(do_ref[...] * o_ref[...]))[0:1, :]
        do = do_ref[...].astype(BF16)
        lse_r = jnp.concatenate([lse_ref[0, sb][0:1, :] for sb in range(nsub)], axis=1)
        dq = jnp.zeros((tq, DKP), F32)
        for start, size in chunks:
            rows = pl.ds(start, size)
            kc = k_ref[0, rows, :]
            p_t = jnp.exp2(_nt(kc, qb) - lse_r)
            ds_t = (p_t * (_nt(v_ref[0, rows, :], do) - delta_r)).astype(BF16)
            dv_ref[0, rows, :] += _nn(p_t.astype(BF16), do)
            dk_ref[0, rows, :] += _nn(ds_t, qb)
            dq += _tn(ds_t, kc)
        dq_ref[0] = dq * SCALE

    kvspec = lambda w: pl.BlockSpec((1, t_all, w), lambda h, i: (h, 0, 0))
    rowspec = pl.BlockSpec((1, nsub, 8, TB), lambda h, i: (h, i, 0, 0))
    qspec = lambda sb: pl.BlockSpec((1, TB, DKP), lambda h, i: (h, i * nsub + sb + off, 0))
    return pl.pallas_call(
        body, name="attn_bwd", grid=(NH, nq // nsub),
        in_specs=[qspec(sb) for sb in range(nsub)]
        + [kvspec(DKP), kvspec(DV), pl.BlockSpec((tq, DV), lambda h, i: (i, h)), pl.BlockSpec((tq, DV), lambda h, i: (i, h)),
           rowspec],
        out_specs=[pl.BlockSpec((1, tq, DKP), lambda h, i: (h, i, 0)), kvspec(DKP), kvspec(DV)],
        out_shape=[jax.ShapeDtypeStruct((NH, s_len, DKP), F32), jax.ShapeDtypeStruct((NH, t_all, DKP), F32),
                   jax.ShapeDtypeStruct((NH, t_all, DV), F32)],
        compiler_params=_params(("arbitrary", "arbitrary")),
    )(*([q] * nsub), k, v, dattn, attn, lse)


def _qkv_bwd(u, dq, dk, dv, cos, sin, q_lora_g, w_uq_t, kv_lora_g, w_ukv, qn_g, kn_g, s_len):
    t_all = u.shape[0]
    off = (t_all - s_len) // TB
    nb = t_all // TB

    def body(ulo_ref, dq_ref, dk_ref, dv_ref, cos_ref, sin_ref, qlg_ref, wuq_ref, kvlg_ref, wukv_ref, qng_ref, kng_ref,
             dlo_ref, dwuq_ref, dwukv_ref, dqlg_ref, dkvlg_ref, dqng_ref, dkng_ref):
        i = pl.program_id(0)

        @pl.when(i == 0)
        def _():
            for r in (dwuq_ref, dwukv_ref, dqlg_ref, dkvlg_ref, dqng_ref, dkng_ref):
                r[...] = jnp.zeros_like(r)

        latent = i >= off
        ulo = ulo_ref[...]
        cos, sin = _rope_block(cos_ref, sin_ref, pl.program_id(0) < off)
        cq = ulo[:, 0:QL]
        rc = lax.rsqrt(jnp.mean(cq * cq, axis=-1, keepdims=True) + EPS)
        cqh = cq * rc
        qlg = qlg_ref[...]
        cqn_b = (cqh * qlg).astype(BF16)
        qng = qng_ref[...]
        ckv = ulo[:, QL:QL + KVL]
        r0 = lax.rsqrt(jnp.mean(ckv * ckv, axis=-1, keepdims=True) + EPS)
        ckvh = ckv * r0
        kvlg = kvlg_ref[...]
        ckvn_b = (ckvh * kvlg).astype(BF16)
        qhs = [_nt(cqn_b, wuq_ref[hd]) for hd in range(NH)]
        kns = [_nn(ckvn_b, wukv_ref[hd])[:, :128] for hd in range(NH)]
        dqng = jnp.zeros((1, DKP), F32)
        dqraws = []
        for hd in range(NH):
            qh = qhs[hd]
            rq = lax.rsqrt(_rowsum(qh * qh) / DK + EPS)
            xh = qh * rq
            dqh = jnp.where(latent, dq_ref[hd], 0.0)
            dyq = jnp.concatenate([dqh[:, :128], _rope_t(dqh[:, 128:], cos, sin)], axis=1)
            dqng += _colsum(dyq * xh)
            dxh = dyq * qng
            dqraws.append((rq * (dxh - xh * (_rowsum(dxh * xh) / DK))).astype(BF16))
        dqng_ref[...] += dqng

        kr = ulo[:, 384:512]
        skr = _rowsum(kr * kr)
        kng = kng_ref[...]
        dkr = jnp.zeros((TB, 128), F32)
        dkng = jnp.zeros((1, DKP), F32)
        dkvs = []
        for hd in range(NH):
            kn = kns[hd]
            rk = lax.rsqrt((_rowsum(kn * kn) + skr) / DK + EPS)
            xh1 = kn * rk
            xh2 = kr * rk
            dkh = dk_ref[hd] * LN2
            d1 = dkh[:, :128]
            d2 = _rope_t(dkh[:, 128:], cos, sin)
            dkng += jnp.concatenate([_colsum(d1 * xh1), _colsum(d2 * xh2)], axis=1)
            dx1 = d1 * kng[:, :128]
            dx2 = d2 * kng[:, 128:]
            dot = (_rowsum(dx1 * xh1) + _rowsum(dx2 * xh2)) / DK
            dkvs.append(jnp.concatenate([rk * (dx1 - xh1 * dot), dv_ref[hd]], axis=1).astype(BF16))
            dkr += rk * (dx2 - xh2 * dot)
        dkng_ref[...] += dkng

        dcqn = jnp.zeros((TB, QL), F32)
        dckvn = jnp.zeros((TB, KVL), F32)
        for hd in range(NH):
            dwuq_ref[hd] += _tn(dqraws[hd], cqn_b)[:DK]
            dcqn += _nn(dqraws[hd], wuq_ref[hd])
            dwukv_ref[hd] += _tn(ckvn_b, dkvs[hd])
            dckvn += _nt(dkvs[hd], wukv_ref[hd])
        dqlg_ref[...] += _colsum(dcqn * cqh)
        dxh = dcqn * qlg
        dcq = rc * (dxh - cqh * jnp.mean(dxh * cqh, axis=-1, keepdims=True))
        dkvlg_ref[...] += _colsum(dckvn * ckvh)
        dxh = dckvn * kvlg
        dckv = r0 * (dxh - ckvh * jnp.mean(dxh * ckvh, axis=-1, keepdims=True))
        dlo_ref[...] = jnp.concatenate([dcq, dckv, dkr], axis=1).astype(BF16)

    row = lambda w: pl.BlockSpec((TB, w), lambda i: (i, 0))
    heads = lambda w: pl.BlockSpec((NH, TB, w), lambda i: (0, i, 0))
    return pl.pallas_call(
        body, name="qkv_bwd", grid=(nb,),
        in_specs=[row(512), pl.BlockSpec((NH, TB, DKP), lambda i: (0, jnp.maximum(i - off, 0), 0)), heads(DKP), heads(DV),
                  pl.BlockSpec((1, 8, 256), lambda i: (jnp.maximum(i - off, 0), 0, 0)), _full((TB, 256)), _full((1, QL)), _full((NH, DKP, QL)), _full((1, KVL)), _full((NH, KVL, 256)),
                  _full((1, DKP)), _full((1, DKP))],
        out_specs=[row(512), _full((NH, DK, QL)), _full((NH, KVL, 256)), _full((1, QL)), _full((1, KVL)),
                   _full((1, DKP)), _full((1, DKP))],
        out_shape=[jax.ShapeDtypeStruct((t_all, 512), BF16), jax.ShapeDtypeStruct((NH, DK, QL), F32),
                   jax.ShapeDtypeStruct((NH, KVL, 256), F32), jax.ShapeDtypeStruct((1, QL), F32),
                   jax.ShapeDtypeStruct((1, KVL), F32), jax.ShapeDtypeStruct((1, DKP), F32), jax.ShapeDtypeStruct((1, DKP), F32)],
        compiler_params=_params(("arbitrary",)),
    )(u, dq, dk, dv, cos, sin, q_lora_g, w_uq_t, kv_lora_g, w_ukv, qn_g, kn_g)


def _in_bwd(ctx, x, modsel, norm_g, dlo, dga, dgp, dpool, dxn, w_in_t):
    s_len, lc = x.shape[0], ctx.shape[0]
    t_all = s_len + lc
    off = lc // TB
    nb = t_all // TB
    nq = s_len // TB
    hb = TB // HALO
    n = TB + 2 * HALO

    def body(ctx_ref, x_ref, mod_ref, ng_ref, dlo_ref, dga_ref, dgp_ref, dp_ref, dpprev_ref, dpnext_ref, dxn_ref, win_ref,
             gx_ref, dwin_ref, dmod_ref, dng_ref):
        i = pl.program_id(0)
        j = i - off

        @pl.when(i == 0)
        def _():
            dwin_ref[...] = jnp.zeros_like(dwin_ref)
            dmod_ref[...] = jnp.zeros_like(dmod_ref)
            dng_ref[...] = jnp.zeros_like(dng_ref)

        latent = i >= off
        dp = dp_ref[...]
        prev = jnp.where(j <= 0, 0.0, dpprev_ref[...])
        nxt = jnp.where(j >= nq - 1, 0.0, dpnext_ref[...])
        win = jnp.concatenate([prev, dp, nxt], axis=0)
        tg = j * TB - HALO + lax.broadcasted_iota(jnp.int32, (n, 1), 0)
        dpin = []
        for g, w in enumerate(POOL_WINDOWS):
            cnt = jnp.maximum(jnp.minimum(tg + w // 2, s_len) - jnp.maximum(tg - w // 2, 0), 1).astype(F32)
            zq = win[:, g * 128:(g + 1) * 128] / cnt
            zq = zq + _shift_rows(zq, 1)
            for step in (1, 2, 4):
                if w >= 4 * step:
                    zq = _shift_rows(zq, -step) + _shift_rows(zq, step)
            dpin.append(zq[HALO:HALO + TB] - dp[:, g * 128:(g + 1) * 128])
        zero = jnp.zeros((TB, 512), BF16)
        du = [dlo_ref[...], jnp.where(latent, dga_ref[...], zero),
              jnp.where(latent, jnp.concatenate(dpin, axis=1).astype(BF16), zero), jnp.where(latent, dgp_ref[...], zero)]

        ng = ng_ref[...]
        xb = jnp.where(i < off, ctx_ref[...], x_ref[...])
        r, xh, xg, h, scale = _modulated(xb, mod_ref, ng)
        hb_ = h.astype(BF16)
        dh = jnp.zeros((TB, D), F32)
        for s, (lo, hi) in enumerate(SEG):
            dwin_ref[lo:hi, :] += _tn(du[s], hb_)
            dh += _nn(du[s], win_ref[lo:hi, :])
        is_lat = latent.astype(F32)
        dsh = _colsum(dh)
        dsc = _colsum(dh * xg)
        dmod_ref[0, 0:1, :] += dsh * (1.0 - is_lat)
        dmod_ref[0, 1:2, :] += dsc * (1.0 - is_lat)
        dmod_ref[1, 0:1, :] += dsh * is_lat
        dmod_ref[1, 1:2, :] += dsc * is_lat
        dxg = dh * (1.0 + scale)
        dng_ref[...] += _colsum(dxg * xh)
        dxh = dxg * ng
        gx_ref[...] = r * (dxh - xh * jnp.mean(dxh * xh, axis=-1, keepdims=True)) + dxn_ref[...]

    row = lambda w: pl.BlockSpec((TB, w), lambda i: (i, 0))
    lat = lambda w: pl.BlockSpec((TB, w), lambda i: (jnp.maximum(i - off, 0), 0))
    last8 = s_len // HALO - 1
    cspec, xspec, mspec = _token_specs(off)
    return pl.pallas_call(
        body, name="in_bwd", grid=(nb,),
        in_specs=[cspec, xspec, mspec, _full((1, D)), row(512), lat(512), lat(512), lat(512),
                  pl.BlockSpec((HALO, 512), lambda i: (jnp.maximum(jnp.maximum(i - off, 0) * hb - 1, 0), 0)),
                  pl.BlockSpec((HALO, 512), lambda i: (jnp.minimum((jnp.maximum(i - off, 0) + 1) * hb, last8), 0)),
                  lat(D), _full((DIN, D))],
        out_specs=[lat(D), _full((DIN, D)), _full((2, 2, D)), _full((1, D))],
        out_shape=[jax.ShapeDtypeStruct((s_len, D), F32), jax.ShapeDtypeStruct((DIN, D), F32),
                   jax.ShapeDtypeStruct((2, 2, D), F32), jax.ShapeDtypeStruct((1, D), F32)],
        compiler_params=_params(("arbitrary",)),
    )(ctx, x, modsel, norm_g, dlo, dga, dgp, dpool, dpool, dpool, dxn, w_in_t)


def _adamw_update(w_ref, g_ref, m_ref, v_ref, d_ref, mo_ref, vo_ref):
    gv = g_ref[...]
    mn = ADAM_B1 * m_ref[...] + (1.0 - ADAM_B1) * gv
    vn = ADAM_B2 * v_ref[...] + (1.0 - ADAM_B2) * (gv * gv)
    m_hat = mn / (1.0 - ADAM_B1 ** ADAM_STEP)
    v_hat = vn / (1.0 - ADAM_B2 ** ADAM_STEP)
    d_ref[...] = -ADAM_LR * (m_hat / (jnp.sqrt(v_hat) + ADAM_EPS) + ADAM_WD * w_ref[...])
    mo_ref[...] = mn
    vo_ref[...] = vn


def _adamw_many(ws, gs, ms, vs):
    n = len(ws)

    def body(*refs):
        for i in range(n):
            _adamw_update(refs[i], refs[n + i], refs[2 * n + i], refs[3 * n + i], refs[4 * n + i], refs[5 * n + i], refs[6 * n + i])

    def spec(w):
        rows, cols = w.shape
        return pl.BlockSpec((rows // 2, cols), lambda i: (i, 0)) if rows % 16 == 0 else _full((rows, cols))

    specs = [spec(w) for w in ws]
    shp = [jax.ShapeDtypeStruct(w.shape, F32) for w in ws]
    out = pl.pallas_call(body, name="adamw_many", grid=(2,), in_specs=specs * 4, out_specs=specs * 3, out_shape=shp * 3,
                         compiler_params=_params(("arbitrary",)))(*ws, *gs, *ms, *vs)
    return out[:n], out[n:2 * n], out[2 * n:]


def _adamw(w, g, m, v, name):
    rows, cols = w.shape
    rb = next(r for r in range(min(rows, 256), 0, -8) if rows % r == 0)

    def body(w_ref, g_ref, m_ref, v_ref, d_ref, mo_ref, vo_ref):
        _adamw_update(w_ref, g_ref, m_ref, v_ref, d_ref, mo_ref, vo_ref)

    spec = pl.BlockSpec((rb, cols), lambda i: (i, 0))
    shp = jax.ShapeDtypeStruct((rows, cols), F32)
    return pl.pallas_call(
        body, name=name, grid=(rows // rb,), in_specs=[spec] * 4, out_specs=[spec] * 3, out_shape=[shp] * 3,
        compiler_params=_params(("arbitrary",)),
    )(w, g, m, v)


class _Links:
    def __init__(self, send_sems, recv_sems):
        self.send_sems, self.recv_sems, self.sends = send_sems, recv_sems, []

    def send(self, src, dst, sem, to):
        cp = pltpu.make_async_remote_copy(src, dst, self.send_sems.at[sem], self.recv_sems.at[sem], device_id=to,
                                          device_id_type=MESH)
        cp.start()
        self.sends.append(cp)

    def arrived(self, dst, sem, frm):
        pltpu.make_async_remote_copy(dst, dst, self.send_sems.at[sem], self.recv_sems.at[sem], device_id=frm,
                                     device_id_type=MESH).wait_recv()

    def drain(self):
        for cp in self.sends:
            cp.wait_send()


def _half(ref, c, axis):
    size = ref.shape[axis - 2] // 2
    win = pl.ds(pl.multiple_of(c * size, 16 if axis == 0 else 128), size)
    idx = (win, slice(None)) if axis == 0 else (slice(None), win)
    return ref.at[(slice(None),) * (len(ref.shape) - 2) + idx]


def _select_rows(slots_ref, n_slots, row=0):
    sub = lax.broadcasted_iota(jnp.int32, (8, 1), 0)
    out = None
    for d in range(n_slots):
        r = jnp.where(sub == d, jnp.broadcast_to(slots_ref[d][row:row + 1, :], (8, slots_ref.shape[-1])), 0.0)
        out = r if out is None else out + r
    return out


def _gather(c, c_ctx, w_mod, b_mod_k, shards, axes):
    nw = len(shards)
    kw = w_mod.shape[1]

    def body(*refs):
        c_ref, cc_ref, wm_ref, b_ref = refs[:4]
        w_refs = refs[4:4 + nw]
        a16_ref, mod_ref = refs[4 + nw:6 + nw]
        g_refs = refs[6 + nw:6 + 2 * nw]
        a_ref, send_sems, recv_sems, local_sems = refs[6 + 2 * nw:]
        x, y, cc = lax.axis_index("x"), lax.axis_index("y"), lax.axis_index("c")
        me = 4 * x + 2 * y + cc
        k = 2 * x + y
        sibling = (x, y, 1 - cc)
        links = _Links(send_sems, recv_sems)
        chips = [_peer(x, y, cc, off + (0,)) for off in CHIPS3]
        locals_ = []
        for wi in range(nw):
            lc = pltpu.make_async_copy(w_refs[wi], g_refs[wi].at[k], local_sems.at[wi])
            lc.start()
            locals_.append(lc)
            for j, to in enumerate(chips):
                links.send(_half(w_refs[wi], cc, axes[wi]), _half(g_refs[wi].at[k], cc, axes[wi]), 10 + wi * 6 + j, to)
        cv = c_ref[...]
        a_ref[me] = jnp.broadcast_to(cv * _sig(cv), (8, D))
        for j, off in enumerate(PEERS7):
            links.send(a_ref.at[me], a_ref.at[me], j, _peer(x, y, cc, off))
        for j, off in enumerate(PEERS7):
            px, py, pc = _peer(x, y, cc, off)
            links.arrived(a_ref.at[4 * px + 2 * py + pc], j, (px, py, pc))
        ccv = cc_ref[...]
        sub = lax.broadcasted_iota(jnp.int32, (8, 1), 0)
        a16 = jnp.concatenate([_select_rows(a_ref, 8), jnp.where(sub == 0, jnp.broadcast_to(ccv * _sig(ccv), (8, D)), 0.0)], axis=0)
        a16_ref[...] = a16
        mod_ref[k] = _dot3(_nn, a16, wm_ref[...]) + b_ref[...]
        for j, to in enumerate(chips):
            links.send(mod_ref.at[k], mod_ref.at[k], 7 + j, to)
        for wi in range(nw):
            for j, (px, py, pc) in enumerate(chips):
                blk = _half(g_refs[wi].at[2 * px + py], cc, axes[wi])
                links.arrived(blk, 10 + wi * 6 + j, (px, py, pc))
                links.send(blk, blk, 10 + wi * 6 + 3 + j, sibling)
        for j, (px, py, pc) in enumerate(chips):
            links.arrived(mod_ref.at[2 * px + py], 7 + j, (px, py, pc))
        for wi in range(nw):
            for j, (px, py, pc) in enumerate(chips):
                links.arrived(_half(g_refs[wi].at[2 * px + py], 1 - cc, axes[wi]), 10 + wi * 6 + 3 + j, sibling)
        links.drain()
        for lc in locals_:
            lc.wait()

    nsem = 10 + 6 * nw
    return pl.pallas_call(
        body, name="gather", in_specs=[VM] * 4 + [ANY] * nw, out_specs=[VM, VM] + [ANY] * nw,
        out_shape=[jax.ShapeDtypeStruct((16, D), F32), jax.ShapeDtypeStruct((4, 16, kw), F32)]
        + [jax.ShapeDtypeStruct((4,) + s.shape, s.dtype) for s in shards],
        scratch_shapes=[pltpu.VMEM((8, 8, D), F32), pltpu.SemaphoreType.DMA((nsem,)), pltpu.SemaphoreType.DMA((nsem,)),
                        pltpu.SemaphoreType.DMA((nw,))],
        compiler_params=pltpu.CompilerParams(vmem_limit_bytes=VMEM_LIMIT),
    )(c, c_ctx, w_mod, b_mod_k, *shards)


SMALL_ROW_WIDTHS = (D, QL, KVL, DKP, DKP, 512, 128)
SMALL_OUT_WIDTHS = (D, QL, KVL, DK, DK, 512, 1)


def _reduce(grads, axes, smalls, w_pool_g, dmod8, a16, w_mod, c_ctx):
    nw = len(grads)
    ns = len(smalls)
    kw = w_mod.shape[1]
    halves = []
    for g, ax in zip(grads, axes):
        halves.append((g.shape[1] // 2, g.shape[2]) if ax == 0 else (g.shape[1], g.shape[2] // 2))

    def body(*refs):
        g_refs = refs[:nw]
        small_refs = refs[nw:nw + ns]
        wp_ref, dm_ref, a16_ref, wm_ref, cc_ref = refs[nw + ns:nw + ns + 5]
        o = nw + ns + 5
        r_refs = refs[o:o + nw]
        small_outs = refs[o + nw:o + nw + ns]
        rwp_ref, gw_ref, gb_ref, gc_ref = refs[o + nw + ns:o + nw + ns + 4]
        o = o + nw + ns + 4
        own, sib, part, got = (refs[o + i * nw:o + (i + 1) * nw] for i in range(4))
        smbuf, wps, wpg, dm_all, pc_all, send_sems, recv_sems, local_sems = refs[o + 4 * nw:]
        x, y, cc = lax.axis_index("x"), lax.axis_index("y"), lax.axis_index("c")
        me = 4 * x + 2 * y + cc
        k = 2 * x + y
        sibling = (x, y, 1 - cc)
        links = _Links(send_sems, recv_sems)
        chips = [_peer(x, y, cc, off + (0,)) for off in CHIPS3]
        peers = [_peer(x, y, cc, off) for off in PEERS7]
        big, sm0, wp0, dm0, pc0 = 0, 5 * nw, 5 * nw + 7, 5 * nw + 14, 5 * nw + 21

        locals_ = []
        for wi in range(nw):
            lc = pltpu.make_async_copy(_half(g_refs[wi], cc, axes[wi]), own[wi], local_sems.at[wi])
            lc.start()
            locals_.append(lc)
            links.send(_half(g_refs[wi], 1 - cc, axes[wi]), sib[wi], big + wi * 5, sibling)
        slot = smbuf.at[me]
        slot[...] = jnp.zeros((8, D), F32)
        for r, (ref, w) in enumerate(zip(small_refs, SMALL_ROW_WIDTHS)):
            slot[r:r + 1, 0:w] = jnp.broadcast_to(ref[...], (1, w))
        links.send(wp_ref, wps, wp0, sibling)
        dm_all[me] = dm_ref[...]
        for j, peer in enumerate(peers):
            links.send(dm_all.at[me], dm_all.at[me], dm0 + j, peer)
            links.send(smbuf.at[me], smbuf.at[me], sm0 + j, peer)
        links.arrived(wps, wp0, sibling)
        wpg[k] = (wp_ref[...] + wps[...]).astype(BF16)
        for j, to in enumerate(chips):
            links.send(wpg.at[k], wpg.at[k], wp0 + 1 + j, to)
        for wi in range(nw):
            locals_[wi].wait()
            links.arrived(sib[wi], big + wi * 5, sibling)
            part[wi][...] = (own[wi][...] + sib[wi][...]).astype(BF16)
            got[wi][k] = part[wi][k]
            for j, (px, py, pc) in enumerate(chips):
                links.send(part[wi].at[2 * px + py], got[wi].at[k], big + wi * 5 + 1 + j, (px, py, pc))
        for j, (px, py, pc) in enumerate(peers):
            links.arrived(dm_all.at[4 * px + 2 * py + pc], dm0 + j, (px, py, pc))
        dmc = dm_all[0][1:2, :]
        dml = dm_all[0][0:1, :]
        for d in range(1, 8):
            dmc = dmc + dm_all[d][1:2, :]
            dml = dml + dm_all[d][0:1, :]
        gb_ref[...] = dml + dmc
        sub = lax.broadcasted_iota(jnp.int32, (8, 1), 0)
        b16 = jnp.concatenate([_select_rows(dm_all, 8), jnp.where(sub == 0, jnp.broadcast_to(dmc, (8, 3 * D)), 0.0)], axis=0)
        bk = jnp.zeros((16, kw), F32)
        for kk in range(4):
            bk = bk + jnp.where(k == kk, b16[:, kk * kw:(kk + 1) * kw], 0.0)
        gw_ref[...] = _dot3(_tn, a16_ref[...], bk)
        pc_all[k] = _dot3(_nt, jnp.broadcast_to(bk[8:9, :], (8, kw)), wm_ref[...])
        for j, to in enumerate(chips):
            links.send(pc_all.at[k], pc_all.at[k], pc0 + j, to)
        for j, (px, py, pc) in enumerate(peers):
            links.arrived(smbuf.at[4 * px + 2 * py + pc], sm0 + j, (px, py, pc))
        tot = smbuf[0]
        for d in range(1, 8):
            tot = tot + smbuf[d]
        for r, (ref, w) in enumerate(zip(small_outs, SMALL_OUT_WIDTHS)):
            ref[...] = tot[r:r + 1, 0:w]
        for j, (px, py, pc) in enumerate(chips):
            links.arrived(wpg.at[2 * px + py], wp0 + 1 + j, (px, py, pc))
        wpt = wpg[0].astype(F32)
        for kk in range(1, 4):
            wpt = wpt + wpg[kk].astype(F32)
        rwp_ref[...] = wpt
        for wi in range(nw):
            for j, (px, py, pc) in enumerate(chips):
                links.arrived(got[wi].at[2 * px + py], big + wi * 5 + 1 + j, (px, py, pc))
            total = got[wi][0].astype(F32)
            for kk in range(1, 4):
                total = total + got[wi][kk].astype(F32)
            mine = _half(r_refs[wi], cc, axes[wi])
            mine[...] = total
            links.send(mine, mine, big + wi * 5 + 4, sibling)
        for j, (px, py, pc) in enumerate(chips):
            links.arrived(pc_all.at[2 * px + py], pc0 + j, (px, py, pc))
        ccv = cc_ref[...]
        sg = _sig(ccv)
        gc_ref[...] = (pc_all[0][0:1, :] + pc_all[1][0:1, :] + pc_all[2][0:1, :] + pc_all[3][0:1, :]) * (sg * (1.0 + ccv * (1.0 - sg)))
        for wi in range(nw):
            links.arrived(_half(r_refs[wi], 1 - cc, axes[wi]), big + wi * 5 + 4, sibling)
        links.drain()

    nsem = 5 * nw + 24
    quads = [(4,) + h for h in halves]
    return pl.pallas_call(
        body, name="reduce", in_specs=[ANY] * nw + [VM] * (ns + 5), out_specs=[VM] * (nw + ns + 4),
        out_shape=[jax.ShapeDtypeStruct(g.shape[1:], F32) for g in grads]
        + [jax.ShapeDtypeStruct((1, w), F32) for w in SMALL_OUT_WIDTHS]
        + [jax.ShapeDtypeStruct(w_pool_g.shape, F32), jax.ShapeDtypeStruct((D, kw), F32), jax.ShapeDtypeStruct((1, 3 * D), F32),
           jax.ShapeDtypeStruct((1, D), F32)],
        scratch_shapes=[pltpu.VMEM(q, F32) for q in quads] + [pltpu.VMEM(q, F32) for q in quads]
        + [pltpu.VMEM(q, BF16) for q in quads] + [pltpu.VMEM(q, BF16) for q in quads]
        + [pltpu.VMEM((8, 8, D), F32), pltpu.VMEM(w_pool_g.shape, F32), pltpu.VMEM((4,) + w_pool_g.shape, BF16),
           pltpu.VMEM((8, 8, 3 * D), F32),
           pltpu.VMEM((4, 8, D), F32)]
        + [pltpu.SemaphoreType.DMA((nsem,)), pltpu.SemaphoreType.DMA((nsem,)), pltpu.SemaphoreType.DMA((nw,))],
        compiler_params=pltpu.CompilerParams(vmem_limit_bytes=VMEM_LIMIT),
    )(*grads, *smalls, w_pool_g, dmod8, a16, w_mod, c_ctx)


def _rope_tables(s_len):
    rows = s_len // GRID_W
    per = TB // GRID_W
    n_freq = 16
    inv = ROPE_BASE ** (-jnp.arange(n_freq, dtype=F32) / n_freq)
    ang_r = jnp.arange(rows, dtype=F32)[:, None] * inv
    ang_c = jnp.arange(GRID_W, dtype=F32)[:, None] * inv
    by_row, by_col = [], []
    for fn, pad in ((jnp.cos, 1.0), (jnp.sin, 0.0)):
        r = jnp.concatenate([fn(ang_r), fn(ang_r), jnp.zeros((rows, 96), F32)], axis=1).reshape(rows // per, per, 128)
        by_row.append(jnp.pad(r, ((0, 0), (0, 8 - per), (0, 0))))
        cpart = jnp.concatenate([jnp.zeros((GRID_W, 32), F32), fn(ang_c), fn(ang_c), jnp.full((GRID_W, 64), pad, F32)], axis=1)
        by_col.append(jnp.tile(cpart, (per, 1)))
    return jnp.concatenate(by_row, axis=-1), jnp.concatenate(by_col, axis=-1)


def kernel(x, c, ctx, c_ctx, w_mod, b_mod, norm_g, w_in, q_lora_g, w_uq, kv_lora_g, w_ukv, q_norm_g, k_norm_g, w_pool, pool_scale, w_out, loss_target, m_c_ctx, m_w_mod, m_b_mod, m_norm_g, m_w_in, m_q_lora_g, m_w_uq, m_kv_lora_g, m_w_ukv, m_q_norm_g, m_k_norm_g, m_w_pool, m_pool_scale, m_w_out, v_c_ctx, v_w_mod, v_b_mod, v_norm_g, v_w_in, v_q_lora_g, v_w_uq, v_kv_lora_g, v_w_ukv, v_q_norm_g, v_k_norm_g, v_w_pool, v_pool_scale, v_w_out):
    xi, yi, ci = lax.axis_index("x"), lax.axis_index("y"), lax.axis_index("c")
    me = 4 * xi + 2 * yi + ci
    k = 2 * xi + yi
    s_len = x.shape[1]
    lc = ctx.shape[1]
    kw = w_mod.shape[2]
    weights = dict(c_ctx=c_ctx, w_mod=w_mod, b_mod=b_mod, norm_g=norm_g, w_in=w_in, q_lora_g=q_lora_g, w_uq=w_uq,
                   kv_lora_g=kv_lora_g, w_ukv=w_ukv, q_norm_g=q_norm_g, k_norm_g=k_norm_g, w_pool=w_pool,
                   pool_scale=pool_scale, w_out=w_out)
    m_in = dict(c_ctx=m_c_ctx, w_mod=m_w_mod, b_mod=m_b_mod, norm_g=m_norm_g, w_in=m_w_in, q_lora_g=m_q_lora_g, w_uq=m_w_uq,
                kv_lora_g=m_kv_lora_g, w_ukv=m_w_ukv, q_norm_g=m_q_norm_g, k_norm_g=m_k_norm_g, w_pool=m_w_pool,
                pool_scale=m_pool_scale, w_out=m_w_out)
    v_in = dict(c_ctx=v_c_ctx, w_mod=v_w_mod, b_mod=v_b_mod, norm_g=v_norm_g, w_in=v_w_in, q_lora_g=v_q_lora_g, w_uq=v_w_uq,
                kv_lora_g=v_kv_lora_g, w_ukv=v_w_ukv, q_norm_g=v_q_norm_g, k_norm_g=v_k_norm_g, w_pool=v_w_pool,
                pool_scale=v_pool_scale, w_out=v_w_out)
    order = ["c_ctx", "w_mod", "b_mod", "norm_g", "w_in", "q_lora_g", "w_uq", "kv_lora_g", "w_ukv", "q_norm_g", "k_norm_g",
             "w_pool", "pool_scale", "w_out"]
    transposed = ("w_in", "w_uq")
    as2d = lambda n, a: jnp.transpose(a[0]) if n in transposed else a.reshape(-1, a.shape[-1])
    back = lambda n, a: jnp.transpose(a)[None] if n in transposed else a.reshape(weights[n].shape)

    c_ctx2 = c_ctx.reshape(1, D)
    b_mod_k = lax.dynamic_slice(b_mod, (0, k * kw), (1, kw))
    split = (1, 0, 0, 0)
    a16, mod_all, g_in, g_uq, g_ukv, g_out = _gather(
        c, c_ctx2, w_mod[0], b_mod_k,
        [as2d("w_in", w_in).astype(BF16), as2d("w_uq", w_uq).astype(BF16), w_ukv[0].astype(BF16), w_out[0].astype(BF16)], split)
    mod_me = lax.dynamic_index_in_dim(mod_all, me, axis=1, keepdims=False).reshape(3, D)
    mod_c = mod_all[:, 8, :].reshape(3, D)
    modsel = jnp.stack([mod_c, mod_me])
    w_in_t = g_in.reshape(DIN, D)
    w_uq_t = jnp.pad(g_uq, ((0, 0), (0, DKP - DK), (0, 0)))
    w_out_f = g_out.reshape(D, D)
    qn_g = jnp.pad(q_norm_g, ((0, 0), (0, DKP - DK)))
    kn_g = jnp.pad(k_norm_g, ((0, 0), (0, DKP - DK)))
    w_pool_b = w_pool[0].astype(BF16)
    cos, sin = _rope_tables(s_len)

    u, q, kk, v = _fwd_in(ctx[0], x[0], modsel, norm_g, w_in_t, q_lora_g, w_uq_t, kv_lora_g, g_ukv, qn_g, kn_g, cos, sin)
    attn, lse = _attn_fwd(q, kk, v, s_len)
    (dxn, dattn, dga, dgp, dpool, dw_out, dgate, dps, dw_pool, loss) = _out_stage(
        attn.reshape(s_len // Q_BLOCK, Q_BLOCK, NH * DV), u, x[0], loss_target[0], modsel[1, 2:3, :], w_pool_b, pool_scale,
        w_out_f, lc)
    dattn = dattn.reshape(s_len, NH * DV)
    dq, dk, dv = _attn_bwd(q, kk, v, dattn, attn, lse, s_len)
    dlo, dw_uq_t, dw_ukv, dqlg, dkvlg, dqng, dkng = _qkv_bwd(u, dq, dk, dv, cos, sin, q_lora_g, w_uq_t, kv_lora_g, g_ukv,
                                                            qn_g, kn_g, s_len)
    gx, dw_in_t, dmod, dng = _in_bwd(ctx[0], x[0], modsel, norm_g, dlo, dga, dgp, dpool, dxn, w_in_t)

    dmod_l = jnp.concatenate([dmod[1, 0], dmod[1, 1], dgate[0]]).reshape(1, 3 * D)
    dmod_c = jnp.concatenate([dmod[0, 0], dmod[0, 1], jnp.zeros((D,), F32)]).reshape(1, 3 * D)
    dmod8 = jnp.concatenate([dmod_l, dmod_c, jnp.zeros((6, 3 * D), F32)], axis=0)
    (r_in, r_uq, r_ukv, r_out, g_ng, g_qlg, g_kvlg, g_qng, g_kng, g_ps, loss_all, g_wp, g_w_mod, g_b_mod, g_c_ctx) = _reduce(
        [dw_in_t.reshape(4, DIN // 4, D), dw_uq_t, dw_ukv, dw_out.reshape(4, D // 4, D)], split,
        [dng, dqlg, dkvlg, dqng, dkng, dps, loss], dw_pool, dmod8, a16, w_mod[0], c_ctx2)
    g2d = dict(c_ctx=g_c_ctx, b_mod=g_b_mod, w_mod=g_w_mod, w_in=r_in, w_uq=r_uq, w_ukv=r_ukv, w_out=r_out, norm_g=g_ng,
               q_lora_g=g_qlg, kv_lora_g=g_kvlg, q_norm_g=g_qng, k_norm_g=g_kng, pool_scale=g_ps, w_pool=g_wp.reshape(512, 128))

    d2d, m2d, v2d = {}, {}, {}
    d2d["w_mod"], m2d["w_mod"], v2d["w_mod"] = _adamw(as2d("w_mod", w_mod), g2d["w_mod"], as2d("w_mod", m_w_mod),
                                                      as2d("w_mod", v_w_mod), "adamw_w_mod")
    rest = [n for n in order if n != "w_mod"]
    outs = _adamw_many([as2d(n, weights[n]) for n in rest], [g2d[n] for n in rest], [as2d(n, m_in[n]) for n in rest],
                       [as2d(n, v_in[n]) for n in rest])
    for dst, arrs in zip((d2d, m2d, v2d), outs):
        dst.update(dict(zip(rest, arrs)))

    return (loss_all[0, 0], gx[None], *[back(n, g2d[n]) for n in order], *[back(n, d2d[n]) for n in order],
            *[back(n, m2d[n]) for n in order], *[back(n, v2d[n]) for n in order])
```

```python
import jax
import jax.numpy as jnp
from jax import lax
from jax.experimental import pallas as pl
from jax.experimental.pallas import tpu as pltpu

F32 = jnp.float32
BF16 = jnp.bfloat16
MESH = pl.DeviceIdType.MESH

D = 1024
NH = 4
DK = 192
DKP = 256
DV = 128
QL = 256
KVL = 128
DIN = 1984
U_LO = 448
SEG = ((0, 512), (448, 960), (960, 1472), (1472, 1984))
DU = 2048
POOL_WINDOWS = (2, 4, 8, 16)
HALO = 8
EPS = 1e-6
ROPE_BASE = 10000.0
GRID_W = 64
Q_BLOCK = 128
TB = 256
BWD_QBLOCKS = 1
SCALE = DK ** -0.5
LOG2E = 1.4426950408889634
LN2 = 0.6931471805599453
VMEM_LIMIT = 56 * 1024 * 1024

ADAM_LR = 0.001
ADAM_B1 = 0.9
ADAM_B2 = 0.999
ADAM_EPS = 1e-08
ADAM_WD = 0.01
ADAM_STEP = 10

CHIPS3 = ((1, 0), (0, 1), (1, 1))
PEERS7 = tuple((dx, dy, dc) for dx in (0, 1) for dy in (0, 1) for dc in (0, 1) if (dx, dy, dc) != (0, 0, 0))

VM = pl.BlockSpec(memory_space=pltpu.VMEM)
ANY = pl.BlockSpec(memory_space=pl.ANY)


def _nn(a, b):
    return jnp.dot(a, b, preferred_element_type=F32)


def _nt(a, b):
    return lax.dot_general(a, b, (((1,), (1,)), ((), ())), preferred_element_type=F32)


def _tn(a, b):
    return lax.dot_general(a, b, (((0,), (0,)), ((), ())), preferred_element_type=F32)


def _split3(a):
    a0 = a.astype(BF16)
    r = a - a0.astype(F32)
    a1 = r.astype(BF16)
    a2 = (r - a1.astype(F32)).astype(BF16)
    return a0, a1, a2


def _dot3(dot, a, b):
    sa = _split3(a)
    sb = _split3(b)
    out = None
    for i in range(3):
        for j in range(3 - i):
            t = dot(sa[i], sb[j])
            out = t if out is None else out + t
    return out


def _sig(x):
    return 1.0 / (1.0 + jnp.exp(-x))


def _rot(t):
    src = lax.broadcasted_iota(jnp.int32, (128, 128), 0)
    dst = lax.broadcasted_iota(jnp.int32, (128, 128), 1)
    first = (dst % 32) < 16
    perm = jnp.where(first & (src == dst + 16), -1.0, jnp.where(~first & (src == dst - 16), 1.0, 0.0)).astype(BF16)
    hi = t.astype(BF16)
    lo = (t - hi.astype(F32)).astype(BF16)
    return _nn(hi, perm) + _nn(lo, perm)


def _rope(t, cos, sin):
    return t * cos + _rot(t) * sin


def _rope_t(t, cos, sin):
    return t * cos - _rot(t * sin)


def _rope_block(rows_ref, cols_ref, is_ctx):
    lane = lax.broadcasted_iota(jnp.int32, (TB, 256), 1) % 128
    rows = jnp.concatenate([jnp.broadcast_to(rows_ref[0, r:r + 1, :], (GRID_W, 256)) for r in range(TB // GRID_W)], axis=0)
    cs = jnp.where(lane < 32, rows, cols_ref[...])
    return jnp.where(is_ctx, 1.0, cs[:, :128]), jnp.where(is_ctx, 0.0, cs[:, 128:])


def _shift_rows(z, k):
    n = z.shape[0]
    return pltpu.roll(z, (n - k) % n, 0)


def _colsum(a):
    return jnp.sum(a, axis=0, keepdims=True)


def _rowsum(a):
    return jnp.sum(a, axis=-1, keepdims=True)


def _row_layout(col):
    return jnp.transpose(jnp.broadcast_to(col, (col.shape[0], 128)))[0:8, :]


def _params(sem=None):
    return pltpu.CompilerParams(dimension_semantics=sem, vmem_limit_bytes=VMEM_LIMIT)


def _full(shape):
    nd = len(shape)
    return pl.BlockSpec(shape, lambda *_: (0,) * nd)


def _peer(x, y, c, off):
    dx, dy, dc = off
    return ((x + dx) % 2, (y + dy) % 2, (c + dc) % 2)


def _token_specs(off):
    ctx = pl.BlockSpec((TB, D), lambda i: (jnp.minimum(i, off - 1), 0))
    lat = pl.BlockSpec((TB, D), lambda i: (jnp.maximum(i - off, 0), 0))
    mod = pl.BlockSpec((1, 3, D), lambda i: (jnp.minimum(i // off, 1), 0, 0))
    return ctx, lat, mod


def _modulated(x, mod_ref, ng):
    shift = mod_ref[0, 0:1, :]
    scale = mod_ref[0, 1:2, :]
    r = lax.rsqrt(jnp.mean(x * x, axis=-1, keepdims=True) + EPS)
    xh = x * r
    xg = xh * ng
    return r, xh, xg, xg * (1.0 + scale) + shift, scale


def _fwd_in(ctx, x, modsel, norm_g, w_in_t, q_lora_g, w_uq_t, kv_lora_g, w_ukv, qn_g, kn_g, cos, sin):
    s_len, lc = x.shape[0], ctx.shape[0]
    t_all = s_len + lc
    nb = t_all // TB
    off = lc // TB

    def body(ctx_ref, x_ref, mod_ref, ng_ref, win_ref, qlg_ref, wuq_ref, kvlg_ref, wukv_ref, qng_ref, kng_ref, cos_ref, sin_ref,
             u_ref, q_ref, k_ref, v_ref):
        is_ctx = pl.program_id(0) < off
        xb = jnp.where(is_ctx, ctx_ref[...], x_ref[...])
        _, _, _, h, _ = _modulated(xb, mod_ref, ng_ref[...])
        hb = h.astype(BF16)
        lane = lax.broadcasted_iota(jnp.int32, (TB, 512), 1)
        ulo = jnp.where(lane < U_LO, _nt(hb, win_ref[SEG[0][0]:SEG[0][1], :]), 0.0)
        u_ref[:, 0:512] = ulo
        for j in range(1, 4):
            u_ref[:, j * 512:(j + 1) * 512] = _nt(hb, win_ref[SEG[j][0]:SEG[j][1], :])
        cos, sin = _rope_block(cos_ref, sin_ref, is_ctx)
        cq = ulo[:, 0:QL]
        cqn = (cq * lax.rsqrt(jnp.mean(cq * cq, axis=-1, keepdims=True) + EPS) * qlg_ref[...]).astype(BF16)
        qng = qng_ref[...]
        ckv = ulo[:, QL:QL + KVL]
        ckvn = (ckv * lax.rsqrt(jnp.mean(ckv * ckv, axis=-1, keepdims=True) + EPS) * kvlg_ref[...]).astype(BF16)
        qhs = [_nt(cqn, wuq_ref[hd]) for hd in range(NH)]
        kvs = [_nn(ckvn, wukv_ref[hd]) for hd in range(NH)]
        for hd in range(NH):
            qh = qhs[hd]
            qn = qh * lax.rsqrt(_rowsum(qh * qh) / DK + EPS) * qng
            q_ref[hd] = (jnp.concatenate([qn[:, :128], _rope(qn[:, 128:], cos, sin)], axis=1) * (SCALE * LOG2E)).astype(BF16)
        kr = ulo[:, 384:512]
        skr = _rowsum(kr * kr)
        kng = kng_ref[...]
        kr_roped = _rope(kr * kng[:, 128:], cos, sin)
        for hd in range(NH):
            kv = kvs[hd]
            kn = kv[:, :128]
            rk = lax.rsqrt((_rowsum(kn * kn) + skr) / DK + EPS)
            k_ref[hd] = jnp.concatenate([kn * rk * kng[:, :128], kr_roped * rk], axis=1).astype(BF16)
            v_ref[hd] = kv[:, 128:].astype(BF16)

    row = lambda w: pl.BlockSpec((TB, w), lambda i: (i, 0))
    heads = lambda w: pl.BlockSpec((NH, TB, w), lambda i: (0, i, 0))
    cspec, xspec, mspec = _token_specs(off)
    return pl.pallas_call(
        body, name="fwd_in", grid=(nb,),
        in_specs=[cspec, xspec, mspec, _full((1, D)), _full((DIN, D)), _full((1, QL)), _full((NH, DKP, QL)), _full((1, KVL)),
                  _full((NH, KVL, 256)), _full((1, DKP)), _full((1, DKP)),
                  pl.BlockSpec((1, 8, 256), lambda i: (jnp.maximum(i - off, 0), 0, 0)), _full((TB, 256))],
        out_specs=[row(DU), heads(DKP), heads(DKP), heads(DV)],
        out_shape=[jax.ShapeDtypeStruct((t_all, DU), F32), jax.ShapeDtypeStruct((NH, t_all, DKP), BF16),
                   jax.ShapeDtypeStruct((NH, t_all, DKP), BF16), jax.ShapeDtypeStruct((NH, t_all, DV), BF16)],
        compiler_params=_params(("arbitrary",)),
    )(ctx, x, modsel, norm_g, w_in_t, q_lora_g, w_uq_t, kv_lora_g, w_ukv, qn_g, kn_g, cos, sin)


def _attn_fwd(q, k, v, s_len):
    t_all = q.shape[1]
    off = (t_all - s_len) // TB
    nq = s_len // TB
    nsub = next(n for n in (4, 2, 1) if nq % n == 0)

    def body(*refs):
        q_refs = refs[:nsub]
        k_ref, v_ref, o_ref, lse_ref = refs[nsub:]
        for sb in range(nsub):
            s = _nt(q_refs[sb][0], k_ref[0])
            m = jnp.max(s, axis=-1, keepdims=True)
            e = jnp.exp2(s - m)
            l = _rowsum(e)
            o_ref[sb * TB:(sb + 1) * TB, :] = _nn(e.astype(BF16), v_ref[0]) / l
            lse_ref[0, sb] = _row_layout(m + jnp.log2(l))

    qspec = lambda sb: pl.BlockSpec((1, TB, DKP), lambda h, i: (h, i * nsub + sb + off, 0))
    return pl.pallas_call(
        body, name="attn_fwd", grid=(NH, nq // nsub),
        in_specs=[qspec(sb) for sb in range(nsub)]
        + [pl.BlockSpec((1, t_all, DKP), lambda h, i: (h, 0, 0)), pl.BlockSpec((1, t_all, DV), lambda h, i: (h, 0, 0))],
        out_specs=[pl.BlockSpec((nsub * TB, DV), lambda h, i: (i, h)), pl.BlockSpec((1, nsub, 8, TB), lambda h, i: (h, i, 0, 0))],
        out_shape=[jax.ShapeDtypeStruct((s_len, NH * DV), F32), jax.ShapeDtypeStruct((NH, nq, 8, TB), F32)],
        compiler_params=_params(("arbitrary", "arbitrary")),
    )(*([q] * nsub), k, v)


def _out_stage(attn, u, x, target, gate, w_pool, pool_scale, w_out, lc):
    s_len = x.shape[0]
    t_all = s_len + lc
    off = lc // TB
    nq = s_len // TB
    hb = TB // HALO
    nqb = s_len // Q_BLOCK
    jb = TB // nqb

    def body(attn_ref, ga_ref, pin_ref, pprev_ref, pnext_ref, gp_ref, x_ref, tgt_ref, gate_ref, wp_ref, ps_ref, wo_ref,
             dxn_ref, dattn_ref, dga_ref, dgp_ref, dpool_ref, dwo_ref, dgate_ref, dps_ref, dwp_ref, loss_ref):
        i = pl.program_id(0)

        @pl.when(i == 0)
        def _():
            dwo_ref[...] = jnp.zeros_like(dwo_ref)
            dgate_ref[...] = jnp.zeros_like(dgate_ref)
            dps_ref[...] = jnp.zeros_like(dps_ref)
            dwp_ref[...] = jnp.zeros_like(dwp_ref)
            loss_ref[...] = jnp.zeros_like(loss_ref)

        attn = jnp.concatenate([attn_ref[:, jj, :] for jj in range(jb)], axis=0)
        ga = ga_ref[...]
        gp = gp_ref[...]
        pin = pin_ref[...]
        prev = jnp.where(i == 0, 0.0, pprev_ref[...])
        nxt = jnp.where(i == nq - 1, 0.0, pnext_ref[...])
        win = jnp.concatenate([prev, pin, nxt], axis=0)
        tg = i * TB + lax.broadcasted_iota(jnp.int32, (TB, 1), 0)
        pooled = []
        for g, w in enumerate(POOL_WINDOWS):
            a = win[:, g * 128:(g + 1) * 128]
            p = _shift_rows(a, -1) + a
            for step in (1, 2, 4):
                if w >= 4 * step:
                    p = _shift_rows(p, -step) + _shift_rows(p, step)
            cnt = (jnp.minimum(tg + w // 2, s_len) - jnp.maximum(tg - w // 2, 0)).astype(F32)
            pooled.append(p[HALO:HALO + TB] / cnt - a[HALO:HALO + TB])
        pooled_b = [p.astype(BF16) for p in pooled]
        z = jnp.concatenate([_nn(pooled_b[g], wp_ref[g]) for g in range(4)], axis=1)
        ps = ps_ref[...]
        yp = z * ps
        sga = _sig(ga)
        sila = ga * sga
        sgp = _sig(gp)
        silp = gp * sgp
        br = jnp.concatenate([sila * attn, silp * yp], axis=1).astype(BF16)
        y = _nn(br, wo_ref[...])
        gate = gate_ref[...]
        err = x_ref[...] + gate * y - tgt_ref[...]
        loss_ref[...] += _colsum(_rowsum(err * err)) * (0.5 / D)
        dxn = err * (1.0 / D)
        dxn_ref[...] = dxn
        dgate_ref[...] += _colsum(dxn * y)
        dy = (dxn * gate).astype(BF16)
        dbr = _nt(dy, wo_ref[...])
        dbra = dbr[:, :512]
        dbrp = dbr[:, 512:]
        dattn = dbra * sila
        for jj in range(jb):
            dattn_ref[:, jj, :] = dattn[jj * nqb:(jj + 1) * nqb]
        dga_ref[...] = (dbra * attn * (sga * (1.0 + ga * (1.0 - sga)))).astype(BF16)
        dgp_ref[...] = (dbrp * yp * (sgp * (1.0 + gp * (1.0 - sgp)))).astype(BF16)
        dyp = dbrp * silp
        dps_ref[...] += _colsum(dyp * z)
        dz = (dyp * ps).astype(BF16)
        dpool_ref[...] = jnp.concatenate([_nt(dz[:, g * 128:(g + 1) * 128], wp_ref[g]) for g in range(4)], axis=1)
        dwo_ref[...] += _tn(br, dy)
        for g in range(4):
            dwp_ref[g] += _tn(pooled_b[g], dz[:, g * 128:(g + 1) * 128])

    lat = lambda w: pl.BlockSpec((TB, w), lambda i: (i, 0))
    perm = pl.BlockSpec((nqb, jb, 512), lambda i: (0, i, 0))
    ucol = lambda j: pl.BlockSpec((TB, 512), lambda i: (i + off, j))
    last8 = t_all // HALO - 1
    return pl.pallas_call(
        body, name="out_stage", grid=(nq,),
        in_specs=[perm, ucol(1), ucol(2),
                  pl.BlockSpec((HALO, 512), lambda i: ((i + off) * hb - 1, 2)),
                  pl.BlockSpec((HALO, 512), lambda i: (jnp.minimum((i + off + 1) * hb, last8), 2)),
                  ucol(3), lat(D), lat(D), _full((1, D)), _full((4, 128, 128)), _full((1, 512)), _full((D, D))],
        out_specs=[lat(D), perm, lat(512), lat(512), lat(512),
                   _full((D, D)), _full((1, D)), _full((1, 512)), _full((4, 128, 128)), _full((1, 1))],
        out_shape=[jax.ShapeDtypeStruct((s_len, D), F32), jax.ShapeDtypeStruct((nqb, Q_BLOCK, 512), F32),
                   jax.ShapeDtypeStruct((s_len, 512), BF16), jax.ShapeDtypeStruct((s_len, 512), BF16),
                   jax.ShapeDtypeStruct((s_len, 512), F32),
                   jax.ShapeDtypeStruct((D, D), F32), jax.ShapeDtypeStruct((1, D), F32), jax.ShapeDtypeStruct((1, 512), F32),
                   jax.ShapeDtypeStruct((4, 128, 128), F32), jax.ShapeDtypeStruct((1, 1), F32)],
        compiler_params=_params(("arbitrary",)),
    )(attn, u, u, u, u, u, x, target, gate, w_pool, pool_scale, w_out)


def _attn_bwd(q, k, v, dattn, attn, lse, s_len):
    t_all = q.shape[1]
    off = (t_all - s_len) // TB
    nq = s_len // TB
    nch = 4
    chunks = [(c * (t_all // nch), t_all // nch) for c in range(nch)]
    nsub = next(n for n in (BWD_QBLOCKS, 2, 1) if nq % n == 0)
    tq = nsub * TB

    def body(*refs):
        q_refs = refs[:nsub]
        k_ref, v_ref, do_ref, o_ref, lse_ref, dq_ref, dk_ref, dv_ref = refs[nsub:]
        i = pl.program_id(1)

        @pl.when(i == 0)
        def _():
            dk_ref[...] = jnp.zeros_like(dk_ref)
            dv_ref[...] = jnp.zeros_like(dv_ref)

        qb = jnp.concatenate([r[0] for r in q_refs], axis=0)
        delta_r = _row_layout(_rowsum(do_ref[...] * o_ref[...]))[0:1, :]
        do = do_ref[...].astype(BF16)
        lse_r = jnp.concatenate([lse_ref[0, sb][0:1, :] for sb in range(nsub)], axis=1)
        dq = jnp.zeros((tq, DKP), F32)
        for start, size in chunks:
            rows = pl.ds(start, size)
            kc = k_ref[0, rows, :]
            p_t = jnp.exp2(_nt(kc, qb) - lse_r)
            ds_t = (p_t * (_nt(v_ref[0, rows, :], do) - delta_r)).astype(BF16)
            dv_ref[0, rows, :] += _nn(p_t.astype(BF16), do)
            dk_ref[0, rows, :] += _nn(ds_t, qb)
            dq += _tn(ds_t, kc)
        dq_ref[0] = dq * SCALE

    kvspec = lambda w: pl.BlockSpec((1, t_all, w), lambda h, i: (h, 0, 0))
    rowspec = pl.BlockSpec((1, nsub, 8, TB), lambda h, i: (h, i, 0, 0))
    qspec = lambda sb: pl.BlockSpec((1, TB, DKP), lambda h, i: (h, i * nsub + sb + off, 0))
    return pl.pallas_call(
        body, name="attn_bwd", grid=(NH, nq // nsub),
        in_specs=[qspec(sb) for sb in range(nsub)]
        + [kvspec(DKP), kvspec(DV), pl.BlockSpec((tq, DV), lambda h, i: (i, h)), pl.BlockSpec((tq, DV), lambda h, i: (i, h)),
           rowspec],
        out_specs=[pl.BlockSpec((1, tq, DKP), lambda h, i: (h, i, 0)), kvspec(DKP), kvspec(DV)],
        out_shape=[jax.ShapeDtypeStruct((NH, s_len, DKP), F32), jax.ShapeDtypeStruct((NH, t_all, DKP), F32),
                   jax.ShapeDtypeStruct((NH, t_all, DV), F32)],
        compiler_params=_params(("arbitrary", "arbitrary")),
    )(*([q] * nsub), k, v, dattn, attn, lse)


def _qkv_bwd(u, dq, dk, dv, cos, sin, q_lora_g, w_uq_t, kv_lora_g, w_ukv, qn_g, kn_g, s_len):
    t_all = u.shape[0]
    off = (t_all - s_len) // TB
    nb = t_all // TB

    def body(ulo_ref, dq_ref, dk_ref, dv_ref, cos_ref, sin_ref, qlg_ref, wuq_ref, kvlg_ref, wukv_ref, qng_ref, kng_ref,
             dlo_ref, dwuq_ref, dwukv_ref, dqlg_ref, dkvlg_ref, dqng_ref, dkng_ref):
        i = pl.program_id(0)

        @pl.when(i == 0)
        def _():
            for r in (dwuq_ref, dwukv_ref, dqlg_ref, dkvlg_ref, dqng_ref, dkng_ref):
                r[...] = jnp.zeros_like(r)

        latent = i >= off
        ulo = ulo_ref[...]
        cos, sin = _rope_block(cos_ref, sin_ref, pl.program_id(0) < off)
        cq = ulo[:, 0:QL]
        rc = lax.rsqrt(jnp.mean(cq * cq, axis=-1, keepdims=True) + EPS)
        cqh = cq * rc
        qlg = qlg_ref[...]
        cqn_b = (cqh * qlg).astype(BF16)
        qng = qng_ref[...]
        ckv = ulo[:, QL:QL + KVL]
        r0 = lax.rsqrt(jnp.mean(ckv * ckv, axis=-1, keepdims=True) + EPS)
        ckvh = ckv * r0
        kvlg = kvlg_ref[...]
        ckvn_b = (ckvh * kvlg).astype(BF16)
        qhs = [_nt(cqn_b, wuq_ref[hd]) for hd in range(NH)]
        kns = [_nn(ckvn_b, wukv_ref[hd])[:, :128] for hd in range(NH)]
        dqng = jnp.zeros((1, DKP), F32)
        dqraws = []
        for hd in range(NH):
            qh = qhs[hd]
            rq = lax.rsqrt(_rowsum(qh * qh) / DK + EPS)
            xh = qh * rq
            dqh = jnp.where(latent, dq_ref[hd], 0.0)
            dyq = jnp.concatenate([dqh[:, :128], _rope_t(dqh[:, 128:], cos, sin)], axis=1)
            dqng += _colsum(dyq * xh)
            dxh = dyq * qng
            dqraws.append((rq * (dxh - xh * (_rowsum(dxh * xh) / DK))).astype(BF16))
        dqng_ref[...] += dqng

        kr = ulo[:, 384:512]
        skr = _rowsum(kr * kr)
        kng = kng_ref[...]
        dkr = jnp.zeros((TB, 128), F32)
        dkng = jnp.zeros((1, DKP), F32)
        dkvs = []
        for hd in range(NH):
            kn = kns[hd]
            rk = lax.rsqrt((_rowsum(kn * kn) + skr) / DK + EPS)
            xh1 = kn * rk
            xh2 = kr * rk
            dkh = dk_ref[hd] * LN2
            d1 = dkh[:, :128]
            d2 = _rope_t(dkh[:, 128:], cos, sin)
            dkng += jnp.concatenate([_colsum(d1 * xh1), _colsum(d2 * xh2)], axis=1)
            dx1 = d1 * kng[:, :128]
            dx2 = d2 * kng[:, 128:]
            dot = (_rowsum(dx1 * xh1) + _rowsum(dx2 * xh2)) / DK
            dkvs.append(jnp.concatenate([rk * (dx1 - xh1 * dot), dv_ref[hd]], axis=1).astype(BF16))
            dkr += rk * (dx2 - xh2 * dot)
        dkng_ref[...] += dkng

        dcqn = jnp.zeros((TB, QL), F32)
        dckvn = jnp.zeros((TB, KVL), F32)
        for hd in range(NH):
            dwuq_ref[hd] += _tn(dqraws[hd], cqn_b)[:DK]
            dcqn += _nn(dqraws[hd], wuq_ref[hd])
            dwukv_ref[hd] += _tn(ckvn_b, dkvs[hd])
            dckvn += _nt(dkvs[hd], wukv_ref[hd])
        dqlg_ref[...] += _colsum(dcqn * cqh)
        dxh = dcqn * qlg
        dcq = rc * (dxh - cqh * jnp.mean(dxh * cqh, axis=-1, keepdims=True))
        dkvlg_ref[...] += _colsum(dckvn * ckvh)
        dxh = dckvn * kvlg
        dckv = r0 * (dxh - ckvh * jnp.mean(dxh * ckvh, axis=-1, keepdims=True))
        dlo_ref[...] = jnp.concatenate([dcq, dckv, dkr], axis=1).astype(BF16)

    row = lambda w: pl.BlockSpec((TB, w), lambda i: (i, 0))
    heads = lambda w: pl.BlockSpec((NH, TB, w), lambda i: (0, i, 0))
    return pl.pallas_call(
        body, name="qkv_bwd", grid=(nb,),
        in_specs=[row(512), pl.BlockSpec((NH, TB, DKP), lambda i: (0, jnp.maximum(i - off, 0), 0)), heads(DKP), heads(DV),
                  pl.BlockSpec((1, 8, 256), lambda i: (jnp.maximum(i - off, 0), 0, 0)), _full((TB, 256)), _full((1, QL)), _full((NH, DKP, QL)), _full((1, KVL)), _full((NH, KVL, 256)),
                  _full((1, DKP)), _full((1, DKP))],
        out_specs=[row(512), _full((NH, DK, QL)), _full((NH, KVL, 256)), _full((1, QL)), _full((1, KVL)),
                   _full((1, DKP)), _full((1, DKP))],
        out_shape=[jax.ShapeDtypeStruct((t_all, 512), BF16), jax.ShapeDtypeStruct((NH, DK, QL), F32),
                   jax.ShapeDtypeStruct((NH, KVL, 256), F32), jax.ShapeDtypeStruct((1, QL), F32),
                   jax.ShapeDtypeStruct((1, KVL), F32), jax.ShapeDtypeStruct((1, DKP), F32), jax.ShapeDtypeStruct((1, DKP), F32)],
        compiler_params=_params(("arbitrary",)),
    )(u, dq, dk, dv, cos, sin, q_lora_g, w_uq_t, kv_lora_g, w_ukv, qn_g, kn_g)


def _in_bwd(ctx, x, modsel, norm_g, dlo, dga, dgp, dpool, dxn, w_in_t):
    s_len, lc = x.shape[0], ctx.shape[0]
    t_all = s_len + lc
    off = lc // TB
    nb = t_all // TB
    nq = s_len // TB
    hb = TB // HALO
    n = TB + 2 * HALO

    def body(ctx_ref, x_ref, mod_ref, ng_ref, dlo_ref, dga_ref, dgp_ref, dp_ref, dpprev_ref, dpnext_ref, dxn_ref, win_ref,
             gx_ref, dwin_ref, dmod_ref, dng_ref):
        i = pl.program_id(0)
        j = i - off

        @pl.when(i == 0)
        def _():
            dwin_ref[...] = jnp.zeros_like(dwin_ref)
            dmod_ref[...] = jnp.zeros_like(dmod_ref)
            dng_ref[...] = jnp.zeros_like(dng_ref)

        latent = i >= off
        dp = dp_ref[...]
        prev = jnp.where(j <= 0, 0.0, dpprev_ref[...])
        nxt = jnp.where(j >= nq - 1, 0.0, dpnext_ref[...])
        win = jnp.concatenate([prev, dp, nxt], axis=0)
        tg = j * TB - HALO + lax.broadcasted_iota(jnp.int32, (n, 1), 0)
        dpin = []
        for g, w in enumerate(POOL_WINDOWS):
            cnt = jnp.maximum(jnp.minimum(tg + w // 2, s_len) - jnp.maximum(tg - w // 2, 0), 1).astype(F32)
            zq = win[:, g * 128:(g + 1) * 128] / cnt
            zq = zq + _shift_rows(zq, 1)
            for step in (1, 2, 4):
                if w >= 4 * step:
                    zq = _shift_rows(zq, -step) + _shift_rows(zq, step)
            dpin.append(zq[HALO:HALO + TB] - dp[:, g * 128:(g + 1) * 128])
        zero = jnp.zeros((TB, 512), BF16)
        du = [dlo_ref[...], jnp.where(latent, dga_ref[...], zero),
              jnp.where(latent, jnp.concatenate(dpin, axis=1).astype(BF16), zero), jnp.where(latent, dgp_ref[...], zero)]

        ng = ng_ref[...]
        xb = jnp.where(i < off, ctx_ref[...], x_ref[...])
        r, xh, xg, h, scale = _modulated(xb, mod_ref, ng)
        hb_ = h.astype(BF16)
        dh = jnp.zeros((TB, D), F32)
        for s, (lo, hi) in enumerate(SEG):
            dh += _nn(du[s], win_ref[lo:hi, :])
        is_lat = latent.astype(F32)
        dsh = _colsum(dh)
        dsc = _colsum(dh * xg)
        dmod_ref[0, 0:1, :] += dsh * (1.0 - is_lat)
        dmod_ref[0, 1:2, :] += dsc * (1.0 - is_lat)
        dmod_ref[1, 0:1, :] += dsh * is_lat
        dmod_ref[1, 1:2, :] += dsc * is_lat
        dxg = dh * (1.0 + scale)
        dng_ref[...] += _colsum(dxg * xh)
        dxh = dxg * ng
        gx_ref[...] = r * (dxh - xh * jnp.mean(dxh * xh, axis=-1, keepdims=True)) + dxn_ref[...]
        for s, (lo, hi) in enumerate(SEG):
            dwin_ref[lo:hi, :] += _tn(du[s], hb_)

    row = lambda w: pl.BlockSpec((TB, w), lambda i: (i, 0))
    lat = lambda w: pl.BlockSpec((TB, w), lambda i: (jnp.maximum(i - off, 0), 0))
    last8 = s_len // HALO - 1
    cspec, xspec, mspec = _token_specs(off)
    return pl.pallas_call(
        body, name="in_bwd", grid=(nb,),
        in_specs=[cspec, xspec, mspec, _full((1, D)), row(512), lat(512), lat(512), lat(512),
                  pl.BlockSpec((HALO, 512), lambda i: (jnp.maximum(jnp.maximum(i - off, 0) * hb - 1, 0), 0)),
                  pl.BlockSpec((HALO, 512), lambda i: (jnp.minimum((jnp.maximum(i - off, 0) + 1) * hb, last8), 0)),
                  lat(D), _full((DIN, D))],
        out_specs=[lat(D), _full((DIN, D)), _full((2, 2, D)), _full((1, D))],
        out_shape=[jax.ShapeDtypeStruct((s_len, D), F32), jax.ShapeDtypeStruct((DIN, D), F32),
                   jax.ShapeDtypeStruct((2, 2, D), F32), jax.ShapeDtypeStruct((1, D), F32)],
        compiler_params=_params(("arbitrary",)),
    )(ctx, x, modsel, norm_g, dlo, dga, dgp, dpool, dpool, dpool, dxn, w_in_t)


def _adamw_update(w_ref, g_ref, m_ref, v_ref, d_ref, mo_ref, vo_ref):
    gv = g_ref[...]
    mn = ADAM_B1 * m_ref[...] + (1.0 - ADAM_B1) * gv
    vn = ADAM_B2 * v_ref[...] + (1.0 - ADAM_B2) * (gv * gv)
    m_hat = mn / (1.0 - ADAM_B1 ** ADAM_STEP)
    v_hat = vn / (1.0 - ADAM_B2 ** ADAM_STEP)
    d_ref[...] = -ADAM_LR * (m_hat / (jnp.sqrt(v_hat) + ADAM_EPS) + ADAM_WD * w_ref[...])
    mo_ref[...] = mn
    vo_ref[...] = vn


def _adamw_many(ws, gs, ms, vs):
    n = len(ws)

    def body(*refs):
        for i in range(n):
            _adamw_update(refs[i], refs[n + i], refs[2 * n + i], refs[3 * n + i], refs[4 * n + i], refs[5 * n + i], refs[6 * n + i])

    def spec(w):
        rows, cols = w.shape
        return pl.BlockSpec((rows // 2, cols), lambda i: (i, 0)) if rows % 16 == 0 else _full((rows, cols))

    specs = [spec(w) for w in ws]
    shp = [jax.ShapeDtypeStruct(w.shape, F32) for w in ws]
    out = pl.pallas_call(body, name="adamw_many", grid=(2,), in_specs=specs * 4, out_specs=specs * 3, out_shape=shp * 3,
                         compiler_params=_params(("arbitrary",)))(*ws, *gs, *ms, *vs)
    return out[:n], out[n:2 * n], out[2 * n:]


def _adamw(w, g, m, v, name):
    rows, cols = w.shape
    rb = next(r for r in range(min(rows, 256), 0, -8) if rows % r == 0)

    def body(w_ref, g_ref, m_ref, v_ref, d_ref, mo_ref, vo_ref):
        _adamw_update(w_ref, g_ref, m_ref, v_ref, d_ref, mo_ref, vo_ref)

    spec = pl.BlockSpec((rb, cols), lambda i: (i, 0))
    shp = jax.ShapeDtypeStruct((rows, cols), F32)
    return pl.pallas_call(
        body, name=name, grid=(rows // rb,), in_specs=[spec] * 4, out_specs=[spec] * 3, out_shape=[shp] * 3,
        compiler_params=_params(("arbitrary",)),
    )(w, g, m, v)


class _Links:
    def __init__(self, send_sems, recv_sems):
        self.send_sems, self.recv_sems, self.sends = send_sems, recv_sems, []

    def send(self, src, dst, sem, to):
        cp = pltpu.make_async_remote_copy(src, dst, self.send_sems.at[sem], self.recv_sems.at[sem], device_id=to,
                                          device_id_type=MESH)
        cp.start()
        self.sends.append(cp)

    def arrived(self, dst, sem, frm):
        pltpu.make_async_remote_copy(dst, dst, self.send_sems.at[sem], self.recv_sems.at[sem], device_id=frm,
                                     device_id_type=MESH).wait_recv()

    def drain(self):
        for cp in self.sends:
            cp.wait_send()


def _half(ref, c, axis):
    size = ref.shape[axis - 2] // 2
    win = pl.ds(pl.multiple_of(c * size, 16 if axis == 0 else 128), size)
    idx = (win, slice(None)) if axis == 0 else (slice(None), win)
    return ref.at[(slice(None),) * (len(ref.shape) - 2) + idx]


def _select_rows(slots_ref, n_slots, row=0):
    sub = lax.broadcasted_iota(jnp.int32, (8, 1), 0)
    out = None
    for d in range(n_slots):
        r = jnp.where(sub == d, jnp.broadcast_to(slots_ref[d][row:row + 1, :], (8, slots_ref.shape[-1])), 0.0)
        out = r if out is None else out + r
    return out


def _gather(c, c_ctx, w_mod, b_mod_k, shards, axes):
    nw = len(shards)
    kw = w_mod.shape[1]

    def body(*refs):
        c_ref, cc_ref, wm_ref, b_ref = refs[:4]
        w_refs = refs[4:4 + nw]
        a16_ref, mod_ref = refs[4 + nw:6 + nw]
        g_refs = refs[6 + nw:6 + 2 * nw]
        a_ref, send_sems, recv_sems, local_sems = refs[6 + 2 * nw:]
        x, y, cc = lax.axis_index("x"), lax.axis_index("y"), lax.axis_index("c")
        me = 4 * x + 2 * y + cc
        k = 2 * x + y
        sibling = (x, y, 1 - cc)
        links = _Links(send_sems, recv_sems)
        chips = [_peer(x, y, cc, off + (0,)) for off in CHIPS3]
        locals_ = []
        for wi in range(nw):
            lc = pltpu.make_async_copy(w_refs[wi], g_refs[wi].at[k], local_sems.at[wi])
            lc.start()
            locals_.append(lc)
            for j, to in enumerate(chips):
                links.send(_half(w_refs[wi], cc, axes[wi]), _half(g_refs[wi].at[k], cc, axes[wi]), 10 + wi * 6 + j, to)
        cv = c_ref[...]
        a_ref[me] = jnp.broadcast_to(cv * _sig(cv), (8, D))
        for j, off in enumerate(PEERS7):
            links.send(a_ref.at[me], a_ref.at[me], j, _peer(x, y, cc, off))
        for j, off in enumerate(PEERS7):
            px, py, pc = _peer(x, y, cc, off)
            links.arrived(a_ref.at[4 * px + 2 * py + pc], j, (px, py, pc))
        ccv = cc_ref[...]
        sub = lax.broadcasted_iota(jnp.int32, (8, 1), 0)
        a16 = jnp.concatenate([_select_rows(a_ref, 8), jnp.where(sub == 0, jnp.broadcast_to(ccv * _sig(ccv), (8, D)), 0.0)], axis=0)
        a16_ref[...] = a16
        mod_ref[k] = _dot3(_nn, a16, wm_ref[...]) + b_ref[...]
        for j, to in enumerate(chips):
            links.send(mod_ref.at[k], mod_ref.at[k], 7 + j, to)
        for wi in range(nw):
            for j, (px, py, pc) in enumerate(chips):
                blk = _half(g_refs[wi].at[2 * px + py], cc, axes[wi])
                links.arrived(blk, 10 + wi * 6 + j, (px, py, pc))
                links.send(blk, blk, 10 + wi * 6 + 3 + j, sibling)
        for j, (px, py, pc) in enumerate(chips):
            links.arrived(mod_ref.at[2 * px + py], 7 + j, (px, py, pc))
        for wi in range(nw):
            for j, (px, py, pc) in enumerate(chips):
                links.arrived(_half(g_refs[wi].at[2 * px + py], 1 - cc, axes[wi]), 10 + wi * 6 + 3 + j, sibling)
        links.drain()
        for lc in locals_:
            lc.wait()

    nsem = 10 + 6 * nw
    return pl.pallas_call(
        body, name="gather", in_specs=[VM] * 4 + [ANY] * nw, out_specs=[VM, VM] + [ANY] * nw,
        out_shape=[jax.ShapeDtypeStruct((16, D), F32), jax.ShapeDtypeStruct((4, 16, kw), F32)]
        + [jax.ShapeDtypeStruct((4,) + s.shape, s.dtype) for s in shards],
        scratch_shapes=[pltpu.VMEM((8, 8, D), F32), pltpu.SemaphoreType.DMA((nsem,)), pltpu.SemaphoreType.DMA((nsem,)),
                        pltpu.SemaphoreType.DMA((nw,))],
        compiler_params=pltpu.CompilerParams(vmem_limit_bytes=VMEM_LIMIT),
    )(c, c_ctx, w_mod, b_mod_k, *shards)


SMALL_ROW_WIDTHS = (D, QL, KVL, DKP, DKP, 512, 128)
SMALL_OUT_WIDTHS = (D, QL, KVL, DK, DK, 512, 1)


def _reduce(grads, axes, smalls, w_pool_g, dmod8, a16, w_mod, c_ctx):
    nw = len(grads)
    ns = len(smalls)
    kw = w_mod.shape[1]
    halves = []
    for g, ax in zip(grads, axes):
        halves.append((g.shape[1] // 2, g.shape[2]) if ax == 0 else (g.shape[1], g.shape[2] // 2))

    def body(*refs):
        g_refs = refs[:nw]
        small_refs = refs[nw:nw + ns]
        wp_ref, dm_ref, a16_ref, wm_ref, cc_ref = refs[nw + ns:nw + ns + 5]
        o = nw + ns + 5
        r_refs = refs[o:o + nw]
        small_outs = refs[o + nw:o + nw + ns]
        rwp_ref, gw_ref, gb_ref, gc_ref = refs[o + nw + ns:o + nw + ns + 4]
        o = o + nw + ns + 4
        own, sib, part, got = (refs[o + i * nw:o + (i + 1) * nw] for i in range(4))
        smbuf, wps, wpg, dm_all, pc_all, send_sems, recv_sems, local_sems = refs[o + 4 * nw:]
        x, y, cc = lax.axis_index("x"), lax.axis_index("y"), lax.axis_index("c")
        me = 4 * x + 2 * y + cc
        k = 2 * x + y
        sibling = (x, y, 1 - cc)
        links = _Links(send_sems, recv_sems)
        chips = [_peer(x, y, cc, off + (0,)) for off in CHIPS3]
        peers = [_peer(x, y, cc, off) for off in PEERS7]
        big, sm0, wp0, dm0, pc0 = 0, 5 * nw, 5 * nw + 7, 5 * nw + 14, 5 * nw + 21

        locals_ = []
        for wi in range(nw):
            lc = pltpu.make_async_copy(_half(g_refs[wi], cc, axes[wi]), own[wi], local_sems.at[wi])
            lc.start()
            locals_.append(lc)
            links.send(_half(g_refs[wi], 1 - cc, axes[wi]), sib[wi], big + wi * 5, sibling)
        slot = smbuf.at[me]
        slot[...] = jnp.zeros((8, D), F32)
        for r, (ref, w) in enumerate(zip(small_refs, SMALL_ROW_WIDTHS)):
            slot[r:r + 1, 0:w] = jnp.broadcast_to(ref[...], (1, w))
        links.send(wp_ref, wps, wp0, sibling)
        dm_all[me] = dm_ref[...]
        for j, peer in enumerate(peers):
            links.send(dm_all.at[me], dm_all.at[me], dm0 + j, peer)
            links.send(smbuf.at[me], smbuf.at[me], sm0 + j, peer)
        links.arrived(wps, wp0, sibling)
        wpg[k] = (wp_ref[...] + wps[...]).astype(BF16)
        for j, to in enumerate(chips):
            links.send(wpg.at[k], wpg.at[k], wp0 + 1 + j, to)
        for wi in range(nw):
            locals_[wi].wait()
            links.arrived(sib[wi], big + wi * 5, sibling)
            part[wi][...] = (own[wi][...] + sib[wi][...]).astype(BF16)
            got[wi][k] = part[wi][k]
            for j, (px, py, pc) in enumerate(chips):
                links.send(part[wi].at[2 * px + py], got[wi].at[k], big + wi * 5 + 1 + j, (px, py, pc))
        for j, (px, py, pc) in enumerate(peers):
            links.arrived(dm_all.at[4 * px + 2 * py + pc], dm0 + j, (px, py, pc))
        dmc = dm_all[0][1:2, :]
        dml = dm_all[0][0:1, :]
        for d in range(1, 8):
            dmc = dmc + dm_all[d][1:2, :]
            dml = dml + dm_all[d][0:1, :]
        gb_ref[...] = dml + dmc
        sub = lax.broadcasted_iota(jnp.int32, (8, 1), 0)
        b16 = jnp.concatenate([_select_rows(dm_all, 8), jnp.where(sub == 0, jnp.broadcast_to(dmc, (8, 3 * D)), 0.0)], axis=0)
        bk = jnp.zeros((16, kw), F32)
        for kk in range(4):
            bk = bk + jnp.where(k == kk, b16[:, kk * kw:(kk + 1) * kw], 0.0)
        gw_ref[...] = _dot3(_tn, a16_ref[...], bk)
        pc_all[k] = _dot3(_nt, jnp.broadcast_to(bk[8:9, :], (8, kw)), wm_ref[...])
        for j, to in enumerate(chips):
            links.send(pc_all.at[k], pc_all.at[k], pc0 + j, to)
        for j, (px, py, pc) in enumerate(peers):
            links.arrived(smbuf.at[4 * px + 2 * py + pc], sm0 + j, (px, py, pc))
        tot = smbuf[0]
        for d in range(1, 8):
            tot = tot + smbuf[d]
        for r, (ref, w) in enumerate(zip(small_outs, SMALL_OUT_WIDTHS)):
            ref[...] = tot[r:r + 1, 0:w]
        for j, (px, py, pc) in enumerate(chips):
            links.arrived(wpg.at[2 * px + py], wp0 + 1 + j, (px, py, pc))
        wpt = wpg[0].astype(F32)
        for kk in range(1, 4):
            wpt = wpt + wpg[kk].astype(F32)
        rwp_ref[...] = wpt
        for wi in range(nw):
            for j, (px, py, pc) in enumerate(chips):
                links.arrived(got[wi].at[2 * px + py], big + wi * 5 + 1 + j, (px, py, pc))
            total = got[wi][0].astype(F32)
            for kk in range(1, 4):
                total = total + got[wi][kk].astype(F32)
            mine = _half(r_refs[wi], cc, axes[wi])
            mine[...] = total
            links.send(mine, mine, big + wi * 5 + 4, sibling)
        for j, (px, py, pc) in enumerate(chips):
            links.arrived(pc_all.at[2 * px + py], pc0 + j, (px, py, pc))
        ccv = cc_ref[...]
        sg = _sig(ccv)
        gc_ref[...] = (pc_all[0][0:1, :] + pc_all[1][0:1, :] + pc_all[2][0:1, :] + pc_all[3][0:1, :]) * (sg * (1.0 + ccv * (1.0 - sg)))
        for wi in range(nw):
            links.arrived(_half(r_refs[wi], 1 - cc, axes[wi]), big + wi * 5 + 4, sibling)
        links.drain()

    nsem = 5 * nw + 24
    quads = [(4,) + h for h in halves]
    return pl.pallas_call(
        body, name="reduce", in_specs=[ANY] * nw + [VM] * (ns + 5), out_specs=[VM] * (nw + ns + 4),
        out_shape=[jax.ShapeDtypeStruct(g.shape[1:], F32) for g in grads]
        + [jax.ShapeDtypeStruct((1, w), F32) for w in SMALL_OUT_WIDTHS]
        + [jax.ShapeDtypeStruct(w_pool_g.shape, F32), jax.ShapeDtypeStruct((D, kw), F32), jax.ShapeDtypeStruct((1, 3 * D), F32),
           jax.ShapeDtypeStruct((1, D), F32)],
        scratch_shapes=[pltpu.VMEM(q, F32) for q in quads] + [pltpu.VMEM(q, F32) for q in quads]
        + [pltpu.VMEM(q, BF16) for q in quads] + [pltpu.VMEM(q, BF16) for q in quads]
        + [pltpu.VMEM((8, 8, D), F32), pltpu.VMEM(w_pool_g.shape, F32), pltpu.VMEM((4,) + w_pool_g.shape, BF16),
           pltpu.VMEM((8, 8, 3 * D), F32),
           pltpu.VMEM((4, 8, D), F32)]
        + [pltpu.SemaphoreType.DMA((nsem,)), pltpu.SemaphoreType.DMA((nsem,)), pltpu.SemaphoreType.DMA((nw,))],
        compiler_params=pltpu.CompilerParams(vmem_limit_bytes=VMEM_LIMIT),
    )(*grads, *smalls, w_pool_g, dmod8, a16, w_mod, c_ctx)


def _rope_tables(s_len):
    rows = s_len // GRID_W
    per = TB // GRID_W
    n_freq = 16
    inv = ROPE_BASE ** (-jnp.arange(n_freq, dtype=F32) / n_freq)
    ang_r = jnp.arange(rows, dtype=F32)[:, None] * inv
    ang_c = jnp.arange(GRID_W, dtype=F32)[:, None] * inv
    by_row, by_col = [], []
    for fn, pad in ((jnp.cos, 1.0), (jnp.sin, 0.0)):
        r = jnp.concatenate([fn(ang_r), fn(ang_r), jnp.zeros((rows, 96), F32)], axis=1).reshape(rows // per, per, 128)
        by_row.append(jnp.pad(r, ((0, 0), (0, 8 - per), (0, 0))))
        cpart = jnp.concatenate([jnp.zeros((GRID_W, 32), F32), fn(ang_c), fn(ang_c), jnp.full((GRID_W, 64), pad, F32)], axis=1)
        by_col.append(jnp.tile(cpart, (per, 1)))
    return jnp.concatenate(by_row, axis=-1), jnp.concatenate(by_col, axis=-1)


def kernel(x, c, ctx, c_ctx, w_mod, b_mod, norm_g, w_in, q_lora_g, w_uq, kv_lora_g, w_ukv, q_norm_g, k_norm_g, w_pool, pool_scale, w_out, loss_target, m_c_ctx, m_w_mod, m_b_mod, m_norm_g, m_w_in, m_q_lora_g, m_w_uq, m_kv_lora_g, m_w_ukv, m_q_norm_g, m_k_norm_g, m_w_pool, m_pool_scale, m_w_out, v_c_ctx, v_w_mod, v_b_mod, v_norm_g, v_w_in, v_q_lora_g, v_w_uq, v_kv_lora_g, v_w_ukv, v_q_norm_g, v_k_norm_g, v_w_pool, v_pool_scale, v_w_out):
    xi, yi, ci = lax.axis_index("x"), lax.axis_index("y"), lax.axis_index("c")
    me = 4 * xi + 2 * yi + ci
    k = 2 * xi + yi
    s_len = x.shape[1]
    lc = ctx.shape[1]
    kw = w_mod.shape[2]
    weights = dict(c_ctx=c_ctx, w_mod=w_mod, b_mod=b_mod, norm_g=norm_g, w_in=w_in, q_lora_g=q_lora_g, w_uq=w_uq,
                   kv_lora_g=kv_lora_g, w_ukv=w_ukv, q_norm_g=q_norm_g, k_norm_g=k_norm_g, w_pool=w_pool,
                   pool_scale=pool_scale, w_out=w_out)
    m_in = dict(c_ctx=m_c_ctx, w_mod=m_w_mod, b_mod=m_b_mod, norm_g=m_norm_g, w_in=m_w_in, q_lora_g=m_q_lora_g, w_uq=m_w_uq,
                kv_lora_g=m_kv_lora_g, w_ukv=m_w_ukv, q_norm_g=m_q_norm_g, k_norm_g=m_k_norm_g, w_pool=m_w_pool,
                pool_scale=m_pool_scale, w_out=m_w_out)
    v_in = dict(c_ctx=v_c_ctx, w_mod=v_w_mod, b_mod=v_b_mod, norm_g=v_norm_g, w_in=v_w_in, q_lora_g=v_q_lora_g, w_uq=v_w_uq,
                kv_lora_g=v_kv_lora_g, w_ukv=v_w_ukv, q_norm_g=v_q_norm_g, k_norm_g=v_k_norm_g, w_pool=v_w_pool,
                pool_scale=v_pool_scale, w_out=v_w_out)
    order = ["c_ctx", "w_mod", "b_mod", "norm_g", "w_in", "q_lora_g", "w_uq", "kv_lora_g", "w_ukv", "q_norm_g", "k_norm_g",
             "w_pool", "pool_scale", "w_out"]
    transposed = ("w_in", "w_uq")
    as2d = lambda n, a: jnp.transpose(a[0]) if n in transposed else a.reshape(-1, a.shape[-1])
    back = lambda n, a: jnp.transpose(a)[None] if n in transposed else a.reshape(weights[n].shape)

    c_ctx2 = c_ctx.reshape(1, D)
    b_mod_k = lax.dynamic_slice(b_mod, (0, k * kw), (1, kw))
    split = (1, 0, 0, 0)
    a16, mod_all, g_in, g_uq, g_ukv, g_out = _gather(
        c, c_ctx2, w_mod[0], b_mod_k,
        [as2d("w_in", w_in).astype(BF16), as2d("w_uq", w_uq).astype(BF16), w_ukv[0].astype(BF16), w_out[0].astype(BF16)], split)
    mod_me = lax.dynamic_index_in_dim(mod_all, me, axis=1, keepdims=False).reshape(3, D)
    mod_c = mod_all[:, 8, :].reshape(3, D)
    modsel = jnp.stack([mod_c, mod_me])
    w_in_t = g_in.reshape(DIN, D)
    w_uq_t = jnp.pad(g_uq, ((0, 0), (0, DKP - DK), (0, 0)))
    w_out_f = g_out.reshape(D, D)
    qn_g = jnp.pad(q_norm_g, ((0, 0), (0, DKP - DK)))
    kn_g = jnp.pad(k_norm_g, ((0, 0), (0, DKP - DK)))
    w_pool_b = w_pool[0].astype(BF16)
    cos, sin = _rope_tables(s_len)

    u, q, kk, v = _fwd_in(ctx[0], x[0], modsel, norm_g, w_in_t, q_lora_g, w_uq_t, kv_lora_g, g_ukv, qn_g, kn_g, cos, sin)
    attn, lse = _attn_fwd(q, kk, v, s_len)
    (dxn, dattn, dga, dgp, dpool, dw_out, dgate, dps, dw_pool, loss) = _out_stage(
        attn.reshape(s_len // Q_BLOCK, Q_BLOCK, NH * DV), u, x[0], loss_target[0], modsel[1, 2:3, :], w_pool_b, pool_scale,
        w_out_f, lc)
    dattn = dattn.reshape(s_len, NH * DV)
    dq, dk, dv = _attn_bwd(q, kk, v, dattn, attn, lse, s_len)
    dlo, dw_uq_t, dw_ukv, dqlg, dkvlg, dqng, dkng = _qkv_bwd(u, dq, dk, dv, cos, sin, q_lora_g, w_uq_t, kv_lora_g, g_ukv,
                                                            qn_g, kn_g, s_len)
    gx, dw_in_t, dmod, dng = _in_bwd(ctx[0], x[0], modsel, norm_g, dlo, dga, dgp, dpool, dxn, w_in_t)

    dmod_l = jnp.concatenate([dmod[1, 0], dmod[1, 1], dgate[0]]).reshape(1, 3 * D)
    dmod_c = jnp.concatenate([dmod[0, 0], dmod[0, 1], jnp.zeros((D,), F32)]).reshape(1, 3 * D)
    dmod8 = jnp.concatenate([dmod_l, dmod_c, jnp.zeros((6, 3 * D), F32)], axis=0)
    (r_in, r_uq, r_ukv, r_out, g_ng, g_qlg, g_kvlg, g_qng, g_kng, g_ps, loss_all, g_wp, g_w_mod, g_b_mod, g_c_ctx) = _reduce(
        [dw_in_t.reshape(4, DIN // 4, D), dw_uq_t, dw_ukv, dw_out.reshape(4, D // 4, D)], split,
        [dng, dqlg, dkvlg, dqng, dkng, dps, loss], dw_pool, dmod8, a16, w_mod[0], c_ctx2)
    g2d = dict(c_ctx=g_c_ctx, b_mod=g_b_mod, w_mod=g_w_mod, w_in=r_in, w_uq=r_uq, w_ukv=r_ukv, w_out=r_out, norm_g=g_ng,
               q_lora_g=g_qlg, kv_lora_g=g_kvlg, q_norm_g=g_qng, k_norm_g=g_kng, pool_scale=g_ps, w_pool=g_wp.reshape(512, 128))

    d2d, m2d, v2d = {}, {}, {}
    d2d["w_mod"], m2d["w_mod"], v2d["w_mod"] = _adamw(as2d("w_mod", w_mod), g2d["w_mod"], as2d("w_mod", m_w_mod),
                                                      as2d("w_mod", v_w_mod), "adamw_w_mod")
    rest = [n for n in order if n != "w_mod"]
    outs = _adamw_many([as2d(n, weights[n]) for n in rest], [g2d[n] for n in rest], [as2d(n, m_in[n]) for n in rest],
                       [as2d(n, v_in[n]) for n in rest])
    for dst, arrs in zip((d2d, m2d, v2d), outs):
        dst.update(dict(zip(rest, arrs)))

    return (loss_all[0, 0], gx[None], *[back(n, g2d[n]) for n in order], *[back(n, d2d[n]) for n in order],
            *[back(n, m2d[n]) for n in order], *[back(n, v2d[n]) for n in order])
```

```python
import jax
import jax.numpy as jnp
from jax import lax
from jax.experimental import pallas as pl
from jax.experimental.pallas import tpu as pltpu

F32 = jnp.float32
BF16 = jnp.bfloat16
MESH = pl.DeviceIdType.MESH

D = 1024
NH = 4
DK = 192
DKP = 256
DV = 128
QL = 256
KVL = 128
DIN = 1984
U_LO = 448
SEG = ((0, 512), (448, 960), (960, 1472), (1472, 1984))
DU = 2048
POOL_WINDOWS = (2, 4, 8, 16)
HALO = 8
EPS = 1e-6
ROPE_BASE = 10000.0
GRID_W = 64
Q_BLOCK = 128
TB = 256
BWD_QBLOCKS = 1
SCALE = DK ** -0.5
LOG2E = 1.4426950408889634
LN2 = 0.6931471805599453
VMEM_LIMIT = 56 * 1024 * 1024

ADAM_LR = 0.001
ADAM_B1 = 0.9
ADAM_B2 = 0.999
ADAM_EPS = 1e-08
ADAM_WD = 0.01
ADAM_STEP = 10

CHIPS3 = ((1, 0), (0, 1), (1, 1))
PEERS7 = tuple((dx, dy, dc) for dx in (0, 1) for dy in (0, 1) for dc in (0, 1) if (dx, dy, dc) != (0, 0, 0))

VM = pl.BlockSpec(memory_space=pltpu.VMEM)
ANY = pl.BlockSpec(memory_space=pl.ANY)


def _nn(a, b):
    return jnp.dot(a, b, preferred_element_type=F32)


def _nt(a, b):
    return lax.dot_general(a, b, (((1,), (1,)), ((), ())), preferred_element_type=F32)


def _tn(a, b):
    return lax.dot_general(a, b, (((0,), (0,)), ((), ())), preferred_element_type=F32)


def _split3(a):
    a0 = a.astype(BF16)
    r = a - a0.astype(F32)
    a1 = r.astype(BF16)
    a2 = (r - a1.astype(F32)).astype(BF16)
    return a0, a1, a2


def _dot3(dot, a, b):
    sa = _split3(a)
    sb = _split3(b)
    out = None
    for i in range(3):
        for j in range(3 - i):
            t = dot(sa[i], sb[j])
            out = t if out is None else out + t
    return out


def _sig(x):
    return 1.0 / (1.0 + jnp.exp(-x))


def _rot(t):
    src = lax.broadcasted_iota(jnp.int32, (128, 128), 0)
    dst = lax.broadcasted_iota(jnp.int32, (128, 128), 1)
    first = (dst % 32) < 16
    perm = jnp.where(first & (src == dst + 16), -1.0, jnp.where(~first & (src == dst - 16), 1.0, 0.0)).astype(BF16)
    hi = t.astype(BF16)
    lo = (t - hi.astype(F32)).astype(BF16)
    return _nn(hi, perm) + _nn(lo, perm)


def _rope(t, cos, sin):
    return t * cos + _rot(t) * sin


def _rope_t(t, cos, sin):
    return t * cos - _rot(t * sin)


def _rope_block(rows_ref, cols_ref, is_ctx):
    lane = lax.broadcasted_iota(jnp.int32, (TB, 256), 1) % 128
    rows = jnp.concatenate([jnp.broadcast_to(rows_ref[0, r:r + 1, :], (GRID_W, 256)) for r in range(TB // GRID_W)], axis=0)
    cs = jnp.where(lane < 32, rows, cols_ref[...])
    return jnp.where(is_ctx, 1.0, cs[:, :128]), jnp.where(is_ctx, 0.0, cs[:, 128:])


def _shift_rows(z, k):
    n = z.shape[0]
    return pltpu.roll(z, (n - k) % n, 0)


def _colsum(a):
    return jnp.sum(a, axis=0, keepdims=True)


def _rowsum(a):
    return jnp.sum(a, axis=-1, keepdims=True)


def _row_layout(col):
    return jnp.transpose(jnp.broadcast_to(col, (col.shape[0], 128)))[0:8, :]


def _params(sem=None):
    return pltpu.CompilerParams(dimension_semantics=sem, vmem_limit_bytes=VMEM_LIMIT)


def _full(shape):
    nd = len(shape)
    return pl.BlockSpec(shape, lambda *_: (0,) * nd)


def _peer(x, y, c, off):
    dx, dy, dc = off
    return ((x + dx) % 2, (y + dy) % 2, (c + dc) % 2)


def _token_specs(off):
    ctx = pl.BlockSpec((TB, D), lambda i: (jnp.minimum(i, off - 1), 0))
    lat = pl.BlockSpec((TB, D), lambda i: (jnp.maximum(i - off, 0), 0))
    mod = pl.BlockSpec((1, 3, D), lambda i: (jnp.minimum(i // off, 1), 0, 0))
    return ctx, lat, mod


def _modulated(x, mod_ref, ng):
    shift = mod_ref[0, 0:1, :]
    scale = mod_ref[0, 1:2, :]
    r = lax.rsqrt(jnp.mean(x * x, axis=-1, keepdims=True) + EPS)
    xh = x * r
    xg = xh * ng
    return r, xh, xg, xg * (1.0 + scale) + shift, scale


def _fwd_in(ctx, x, modsel, norm_g, w_in_t, q_lora_g, w_uq_t, kv_lora_g, w_ukv, qn_g, kn_g, cos, sin):
    s_len, lc = x.shape[0], ctx.shape[0]
    t_all = s_len + lc
    nb = t_all // TB
    off = lc // TB

    def body(ctx_ref, x_ref, mod_ref, ng_ref, win_ref, qlg_ref, wuq_ref, kvlg_ref, wukv_ref, qng_ref, kng_ref, cos_ref, sin_ref,
             u_ref, q_ref, k_ref, v_ref):
        is_ctx = pl.program_id(0) < off
        xb = jnp.where(is_ctx, ctx_ref[...], x_ref[...])
        _, _, _, h, _ = _modulated(xb, mod_ref, ng_ref[...])
        hb = h.astype(BF16)
        lane = lax.broadcasted_iota(jnp.int32, (TB, 512), 1)
        ulo = jnp.where(lane < U_LO, _nt(hb, win_ref[SEG[0][0]:SEG[0][1], :]), 0.0)
        u_ref[:, 0:512] = ulo
        for j in range(1, 4):
            u_ref[:, j * 512:(j + 1) * 512] = _nt(hb, win_ref[SEG[j][0]:SEG[j][1], :])
        cos, sin = _rope_block(cos_ref, sin_ref, is_ctx)
        cq = ulo[:, 0:QL]
        cqn = (cq * lax.rsqrt(jnp.mean(cq * cq, axis=-1, keepdims=True) + EPS) * qlg_ref[...]).astype(BF16)
        qng = qng_ref[...]
        ckv = ulo[:, QL:QL + KVL]
        ckvn = (ckv * lax.rsqrt(jnp.mean(ckv * ckv, axis=-1, keepdims=True) + EPS) * kvlg_ref[...]).astype(BF16)
        qhs = [_nt(cqn, wuq_ref[hd]) for hd in range(NH)]
        kvs = [_nn(ckvn, wukv_ref[hd]) for hd in range(NH)]
        for hd in range(NH):
            qh = qhs[hd]
            qn = qh * lax.rsqrt(_rowsum(qh * qh) / DK + EPS) * qng
            q_ref[hd] = (jnp.concatenate([qn[:, :128], _rope(qn[:, 128:], cos, sin)], axis=1) * (SCALE * LOG2E)).astype(BF16)
        kr = ulo[:, 384:512]
        skr = _rowsum(kr * kr)
        kng = kng_ref[...]
        kr_roped = _rope(kr * kng[:, 128:], cos, sin)
        for hd in range(NH):
            kv = kvs[hd]
            kn = kv[:, :128]
            rk = lax.rsqrt((_rowsum(kn * kn) + skr) / DK + EPS)
            k_ref[hd] = jnp.concatenate([kn * rk * kng[:, :128], kr_roped * rk], axis=1).astype(BF16)
            v_ref[hd] = kv[:, 128:].astype(BF16)

    row = lambda w: pl.BlockSpec((TB, w), lambda i: (i, 0))
    heads = lambda w: pl.BlockSpec((NH, TB, w), lambda i: (0, i, 0))
    cspec, xspec, mspec = _token_specs(off)
    return pl.pallas_call(
        body, name="fwd_in", grid=(nb,),
        in_specs=[cspec, xspec, mspec, _full((1, D)), _full((DIN, D)), _full((1, QL)), _full((NH, DKP, QL)), _full((1, KVL)),
                  _full((NH, KVL, 256)), _full((1, DKP)), _full((1, DKP)),
                  pl.BlockSpec((1, 8, 256), lambda i: (jnp.maximum(i - off, 0), 0, 0)), _full((TB, 256))],
        out_specs=[row(DU), heads(DKP), heads(DKP), heads(DV)],
        out_shape=[jax.ShapeDtypeStruct((t_all, DU), F32), jax.ShapeDtypeStruct((NH, t_all, DKP), BF16),
                   jax.ShapeDtypeStruct((NH, t_all, DKP), BF16), jax.ShapeDtypeStruct((NH, t_all, DV), BF16)],
        compiler_params=_params(("arbitrary",)),
    )(ctx, x, modsel, norm_g, w_in_t, q_lora_g, w_uq_t, kv_lora_g, w_ukv, qn_g, kn_g, cos, sin)


def _attn_fwd(q, k, v, s_len):
    t_all = q.shape[1]
    off = (t_all - s_len) // TB
    nq = s_len // TB
    nsub = next(n for n in (4, 2, 1) if nq % n == 0)

    def body(*refs):
        q_refs = refs[:nsub]
        k_ref, v_ref, o_ref, lse_ref = refs[nsub:]
        for sb in range(nsub):
            s = _nt(q_refs[sb][0], k_ref[0])
            m = jnp.max(s, axis=-1, keepdims=True)
            e = jnp.exp2(s - m)
            l = _rowsum(e)
            o_ref[sb * TB:(sb + 1) * TB, :] = _nn(e.astype(BF16), v_ref[0]) / l
            lse_ref[0, sb] = _row_layout(m + jnp.log2(l))

    qspec = lambda sb: pl.BlockSpec((1, TB, DKP), lambda h, i: (h, i * nsub + sb + off, 0))
    return pl.pallas_call(
        body, name="attn_fwd", grid=(NH, nq // nsub),
        in_specs=[qspec(sb) for sb in range(nsub)]
        + [pl.BlockSpec((1, t_all, DKP), lambda h, i: (h, 0, 0)), pl.BlockSpec((1, t_all, DV), lambda h, i: (h, 0, 0))],
        out_specs=[pl.BlockSpec((nsub * TB, DV), lambda h, i: (i, h)), pl.BlockSpec((1, nsub, 8, TB), lambda h, i: (h, i, 0, 0))],
        out_shape=[jax.ShapeDtypeStruct((s_len, NH * DV), F32), jax.ShapeDtypeStruct((NH, nq, 8, TB), F32)],
        compiler_params=_params(("arbitrary", "arbitrary")),
    )(*([q] * nsub), k, v)


def _out_stage(attn, u, x, target, gate, w_pool, pool_scale, w_out, lc):
    s_len = x.shape[0]
    t_all = s_len + lc
    off = lc // TB
    nq = s_len // TB
    hb = TB // HALO
    nqb = s_len // Q_BLOCK
    jb = TB // nqb

    def body(attn_ref, ga_ref, pin_ref, pprev_ref, pnext_ref, gp_ref, x_ref, tgt_ref, gate_ref, wp_ref, ps_ref, wo_ref,
             dxn_ref, dattn_ref, dga_ref, dgp_ref, dpool_ref, dwo_ref, dgate_ref, dps_ref, dwp_ref, loss_ref):
        i = pl.program_id(0)

        @pl.when(i == 0)
        def _():
            dwo_ref[...] = jnp.zeros_like(dwo_ref)
            dgate_ref[...] = jnp.zeros_like(dgate_ref)
            dps_ref[...] = jnp.zeros_like(dps_ref)
            dwp_ref[...] = jnp.zeros_like(dwp_ref)
            loss_ref[...] = jnp.zeros_like(loss_ref)

        attn = jnp.concatenate([attn_ref[:, jj, :] for jj in range(jb)], axis=0)
        ga = ga_ref[...]
        gp = gp_ref[...]
        pin = pin_ref[...]
        prev = jnp.where(i == 0, 0.0, pprev_ref[...])
        nxt = jnp.where(i == nq - 1, 0.0, pnext_ref[...])
        win = jnp.concatenate([prev, pin, nxt], axis=0)
        tg = i * TB + lax.broadcasted_iota(jnp.int32, (TB, 1), 0)
        pooled = []
        for g, w in enumerate(POOL_WINDOWS):
            a = win[:, g * 128:(g + 1) * 128]
            p = _shift_rows(a, -1) + a
            for step in (1, 2, 4):
                if w >= 4 * step:
                    p = _shift_rows(p, -step) + _shift_rows(p, step)
            cnt = (jnp.minimum(tg + w // 2, s_len) - jnp.maximum(tg - w // 2, 0)).astype(F32)
            pooled.append(p[HALO:HALO + TB] / cnt - a[HALO:HALO + TB])
        pooled_b = [p.astype(BF16) for p in pooled]
        z = jnp.concatenate([_nn(pooled_b[g], wp_ref[g]) for g in range(4)], axis=1)
        ps = ps_ref[...]
        yp = z * ps
        sga = _sig(ga)
        sila = ga * sga
        sgp = _sig(gp)
        silp = gp * sgp
        br = jnp.concatenate([sila * attn, silp * yp], axis=1).astype(BF16)
        y = _nn(br, wo_ref[...])
        gate = gate_ref[...]
        err = x_ref[...] + gate * y - tgt_ref[...]
        loss_ref[...] += _colsum(_rowsum(err * err)) * (0.5 / D)
        dxn = err * (1.0 / D)
        dxn_ref[...] = dxn
        dgate_ref[...] += _colsum(dxn * y)
        dy = (dxn * gate).astype(BF16)
        dwo_ref[...] += _tn(br, dy)
        dbr = _nt(dy, wo_ref[...])
        dbra = dbr[:, :512]
        dbrp = dbr[:, 512:]
        dattn = dbra * sila
        for jj in range(jb):
            dattn_ref[:, jj, :] = dattn[jj * nqb:(jj + 1) * nqb]
        dga_ref[...] = (dbra * attn * (sga * (1.0 + ga * (1.0 - sga)))).astype(BF16)
        dgp_ref[...] = (dbrp * yp * (sgp * (1.0 + gp * (1.0 - sgp)))).astype(BF16)
        dyp = dbrp * silp
        dps_ref[...] += _colsum(dyp * z)
        dz = (dyp * ps).astype(BF16)
        dpool = []
        for g in range(4):
            dzg = dz[:, g * 128:(g + 1) * 128]
            dwp_ref[g] += _tn(pooled_b[g], dzg)
            dpool.append(_nt(dzg, wp_ref[g]))
        dpool_ref[...] = jnp.concatenate(dpool, axis=1)

    lat = lambda w: pl.BlockSpec((TB, w), lambda i: (i, 0))
    perm = pl.BlockSpec((nqb, jb, 512), lambda i: (0, i, 0))
    ucol = lambda j: pl.BlockSpec((TB, 512), lambda i: (i + off, j))
    last8 = t_all // HALO - 1
    return pl.pallas_call(
        body, name="out_stage", grid=(nq,),
        in_specs=[perm, ucol(1), ucol(2),
                  pl.BlockSpec((HALO, 512), lambda i: ((i + off) * hb - 1, 2)),
                  pl.BlockSpec((HALO, 512), lambda i: (jnp.minimum((i + off + 1) * hb, last8), 2)),
                  ucol(3), lat(D), lat(D), _full((1, D)), _full((4, 128, 128)), _full((1, 512)), _full((D, D))],
        out_specs=[lat(D), perm, lat(512), lat(512), lat(512),
                   _full((D, D)), _full((1, D)), _full((1, 512)), _full((4, 128, 128)), _full((1, 1))],
        out_shape=[jax.ShapeDtypeStruct((s_len, D), F32), jax.ShapeDtypeStruct((nqb, Q_BLOCK, 512), F32),
                   jax.ShapeDtypeStruct((s_len, 512), BF16), jax.ShapeDtypeStruct((s_len, 512), BF16),
                   jax.ShapeDtypeStruct((s_len, 512), F32),
                   jax.ShapeDtypeStruct((D, D), F32), jax.ShapeDtypeStruct((1, D), F32), jax.ShapeDtypeStruct((1, 512), F32),
                   jax.ShapeDtypeStruct((4, 128, 128), F32), jax.ShapeDtypeStruct((1, 1), F32)],
        compiler_params=_params(("arbitrary",)),
    )(attn, u, u, u, u, u, x, target, gate, w_pool, pool_scale, w_out)


def _attn_bwd(q, k, v, dattn, attn, lse, s_len):
    t_all = q.shape[1]
    off = (t_all - s_len) // TB
    nq = s_len // TB
    nch = 4
    chunks = [(c * (t_all // nch), t_all // nch) for c in range(nch)]
    nsub = next(n for n in (BWD_QBLOCKS, 2, 1) if nq % n == 0)
    tq = nsub * TB

    def body(*refs):
        q_refs = refs[:nsub]
        k_ref, v_ref, do_ref, o_ref, lse_ref, dq_ref, dk_ref, dv_ref = refs[nsub:]
        i = pl.program_id(1)

        @pl.when(i == 0)
        def _():
            dk_ref[...] = jnp.zeros_like(dk_ref)
            dv_ref[...] = jnp.zeros_like(dv_ref)

        qb = jnp.concatenate([r[0] for r in q_refs], axis=0)
        delta_r = _row_layout(_rowsum(do_ref[...] * o_ref[...]))[0:1, :]
        do = do_ref[...].astype(BF16)
        lse_r = jnp.concatenate([lse_ref[0, sb][0:1, :] for sb in range(nsub)], axis=1)
        dq = jnp.zeros((tq, DKP), F32)
        for start, size in chunks:
            rows = pl.ds(start, size)
            kc = k_ref[0, rows, :]
            p_t = jnp.exp2(_nt(kc, qb) - lse_r)
            ds_t = (p_t * (_nt(v_ref[0, rows, :], do) - delta_r)).astype(BF16)
            dv_ref[0, rows, :] += _nn(p_t.astype(BF16), do)
            dk_ref[0, rows, :] += _nn(ds_t, qb)
            dq += _tn(ds_t, kc)
        dq_ref[0] = dq * SCALE

    kvspec = lambda w: pl.BlockSpec((1, t_all, w), lambda h, i: (h, 0, 0))
    rowspec = pl.BlockSpec((1, nsub, 8, TB), lambda h, i: (h, i, 0, 0))
    qspec = lambda sb: pl.BlockSpec((1, TB, DKP), lambda h, i: (h, i * nsub + sb + off, 0))
    return pl.pallas_call(
        body, name="attn_bwd", grid=(NH, nq // nsub),
        in_specs=[qspec(sb) for sb in range(nsub)]
        + [kvspec(DKP), kvspec(DV), pl.BlockSpec((tq, DV), lambda h, i: (i, h)), pl.BlockSpec((tq, DV), lambda h, i: (i, h)),
           rowspec],
        out_specs=[pl.BlockSpec((1, tq, DKP), lambda h, i: (h, i, 0)), kvspec(DKP), kvspec(DV)],
        out_shape=[jax.ShapeDtypeStruct((NH, s_len, DKP), F32), jax.ShapeDtypeStruct((NH, t_all, DKP), F32),
                   jax.ShapeDtypeStruct((NH, t_all, DV), F32)],
        compiler_params=_params(("arbitrary", "arbitrary")),
    )(*([q] * nsub), k, v, dattn, attn, lse)


def _qkv_bwd(u, dq, dk, dv, cos, sin, q_lora_g, w_uq_t, kv_lora_g, w_ukv, qn_g, kn_g, s_len):
    t_all = u.shape[0]
    off = (t_all - s_len) // TB
    nb = t_all // TB

    def body(ulo_ref, dq_ref, dk_ref, dv_ref, cos_ref, sin_ref, qlg_ref, wuq_ref, kvlg_ref, wukv_ref, qng_ref, kng_ref,
             dlo_ref, dwuq_ref, dwukv_ref, dqlg_ref, dkvlg_ref, dqng_ref, dkng_ref):
        i = pl.program_id(0)

        @pl.when(i == 0)
        def _():
            for r in (dwuq_ref, dwukv_ref, dqlg_ref, dkvlg_ref, dqng_ref, dkng_ref):
                r[...] = jnp.zeros_like(r)

        latent = i >= off
        ulo = ulo_ref[...]
        cos, sin = _rope_block(cos_ref, sin_ref, pl.program_id(0) < off)
        cq = ulo[:, 0:QL]
        rc = lax.rsqrt(jnp.mean(cq * cq, axis=-1, keepdims=True) + EPS)
        cqh = cq * rc
        qlg = qlg_ref[...]
        cqn_b = (cqh * qlg).astype(BF16)
        qng = qng_ref[...]
        ckv = ulo[:, QL:QL + KVL]
        r0 = lax.rsqrt(jnp.mean(ckv * ckv, axis=-1, keepdims=True) + EPS)
        ckvh = ckv * r0
        kvlg = kvlg_ref[...]
        ckvn_b = (ckvh * kvlg).astype(BF16)
        qhs = [_nt(cqn_b, wuq_ref[hd]) for hd in range(NH)]
        kns = [_nn(ckvn_b, wukv_ref[hd])[:, :128] for hd in range(NH)]
        dqng = jnp.zeros((1, DKP), F32)
        dqraws = []
        for hd in range(NH):
            qh = qhs[hd]
            rq = lax.rsqrt(_rowsum(qh * qh) / DK + EPS)
            xh = qh * rq
            dqh = jnp.where(latent, dq_ref[hd], 0.0)
            dyq = jnp.concatenate([dqh[:, :128], _rope_t(dqh[:, 128:], cos, sin)], axis=1)
            dqng += _colsum(dyq * xh)
            dxh = dyq * qng
            dqraws.append((rq * (dxh - xh * (_rowsum(dxh * xh) / DK))).astype(BF16))
        dqng_ref[...] += dqng

        kr = ulo[:, 384:512]
        skr = _rowsum(kr * kr)
        kng = kng_ref[...]
        dkr = jnp.zeros((TB, 128), F32)
        dkng = jnp.zeros((1, DKP), F32)
        dkvs = []
        for hd in range(NH):
            kn = kns[hd]
            rk = lax.rsqrt((_rowsum(kn * kn) + skr) / DK + EPS)
            xh1 = kn * rk
            xh2 = kr * rk
            dkh = dk_ref[hd] * LN2
            d1 = dkh[:, :128]
            d2 = _rope_t(dkh[:, 128:], cos, sin)
            dkng += jnp.concatenate([_colsum(d1 * xh1), _colsum(d2 * xh2)], axis=1)
            dx1 = d1 * kng[:, :128]
            dx2 = d2 * kng[:, 128:]
            dot = (_rowsum(dx1 * xh1) + _rowsum(dx2 * xh2)) / DK
            dkvs.append(jnp.concatenate([rk * (dx1 - xh1 * dot), dv_ref[hd]], axis=1).astype(BF16))
            dkr += rk * (dx2 - xh2 * dot)
        dkng_ref[...] += dkng

        dcqn = jnp.zeros((TB, QL), F32)
        dckvn = jnp.zeros((TB, KVL), F32)
        for hd in range(NH):
            dwuq_ref[hd] += _tn(dqraws[hd], cqn_b)[:DK]
            dcqn += _nn(dqraws[hd], wuq_ref[hd])
            dwukv_ref[hd] += _tn(ckvn_b, dkvs[hd])
            dckvn += _nt(dkvs[hd], wukv_ref[hd])
        dqlg_ref[...] += _colsum(dcqn * cqh)
        dxh = dcqn * qlg
        dcq = rc * (dxh - cqh * jnp.mean(dxh * cqh, axis=-1, keepdims=True))
        dkvlg_ref[...] += _colsum(dckvn * ckvh)
        dxh = dckvn * kvlg
        dckv = r0 * (dxh - ckvh * jnp.mean(dxh * ckvh, axis=-1, keepdims=True))
        dlo_ref[...] = jnp.concatenate([dcq, dckv, dkr], axis=1).astype(BF16)

    row = lambda w: pl.BlockSpec((TB, w), lambda i: (i, 0))
    heads = lambda w: pl.BlockSpec((NH, TB, w), lambda i: (0, i, 0))
    return pl.pallas_call(
        body, name="qkv_bwd", grid=(nb,),
        in_specs=[row(512), pl.BlockSpec((NH, TB, DKP), lambda i: (0, jnp.maximum(i - off, 0), 0)), heads(DKP), heads(DV),
                  pl.BlockSpec((1, 8, 256), lambda i: (jnp.maximum(i - off, 0), 0, 0)), _full((TB, 256)), _full((1, QL)), _full((NH, DKP, QL)), _full((1, KVL)), _full((NH, KVL, 256)),
                  _full((1, DKP)), _full((1, DKP))],
        out_specs=[row(512), _full((NH, DK, QL)), _full((NH, KVL, 256)), _full((1, QL)), _full((1, KVL)),
                   _full((1, DKP)), _full((1, DKP))],
        out_shape=[jax.ShapeDtypeStruct((t_all, 512), BF16), jax.ShapeDtypeStruct((NH, DK, QL), F32),
                   jax.ShapeDtypeStruct((NH, KVL, 256), F32), jax.ShapeDtypeStruct((1, QL), F32),
                   jax.ShapeDtypeStruct((1, KVL), F32), jax.ShapeDtypeStruct((1, DKP), F32), jax.ShapeDtypeStruct((1, DKP), F32)],
        compiler_params=_params(("arbitrary",)),
    )(u, dq, dk, dv, cos, sin, q_lora_g, w_uq_t, kv_lora_g, w_ukv, qn_g, kn_g)


def _in_bwd(ctx, x, modsel, norm_g, dlo, dga, dgp, dpool, dxn, w_in_t):
    s_len, lc = x.shape[0], ctx.shape[0]
    t_all = s_len + lc
    off = lc // TB
    nb = t_all // TB
    nq = s_len // TB
    hb = TB // HALO
    n = TB + 2 * HALO

    def body(ctx_ref, x_ref, mod_ref, ng_ref, dlo_ref, dga_ref, dgp_ref, dp_ref, dpprev_ref, dpnext_ref, dxn_ref, win_ref,
             gx_ref, dwin_ref, dmod_ref, dng_ref):
        i = pl.program_id(0)
        j = i - off

        @pl.when(i == 0)
        def _():
            dwin_ref[...] = jnp.zeros_like(dwin_ref)
            dmod_ref[...] = jnp.zeros_like(dmod_ref)
            dng_ref[...] = jnp.zeros_like(dng_ref)

        latent = i >= off
        dp = dp_ref[...]
        prev = jnp.where(j <= 0, 0.0, dpprev_ref[...])
        nxt = jnp.where(j >= nq - 1, 0.0, dpnext_ref[...])
        win = jnp.concatenate([prev, dp, nxt], axis=0)
        tg = j * TB - HALO + lax.broadcasted_iota(jnp.int32, (n, 1), 0)
        dpin = []
        for g, w in enumerate(POOL_WINDOWS):
            cnt = jnp.maximum(jnp.minimum(tg + w // 2, s_len) - jnp.maximum(tg - w // 2, 0), 1).astype(F32)
            zq = win[:, g * 128:(g + 1) * 128] / cnt
            zq = zq + _shift_rows(zq, 1)
            for step in (1, 2, 4):
                if w >= 4 * step:
                    zq = _shift_rows(zq, -step) + _shift_rows(zq, step)
            dpin.append(zq[HALO:HALO + TB] - dp[:, g * 128:(g + 1) * 128])
        zero = jnp.zeros((TB, 512), BF16)
        du = [dlo_ref[...], jnp.where(latent, dga_ref[...], zero),
              jnp.where(latent, jnp.concatenate(dpin, axis=1).astype(BF16), zero), jnp.where(latent, dgp_ref[...], zero)]

        ng = ng_ref[...]
        xb = jnp.where(i < off, ctx_ref[...], x_ref[...])
        r, xh, xg, h, scale = _modulated(xb, mod_ref, ng)
        hb_ = h.astype(BF16)
        dh = jnp.zeros((TB, D), F32)
        for s, (lo, hi) in enumerate(SEG):
            dwin_ref[lo:hi, :] += _tn(du[s], hb_)
            dh += _nn(du[s], win_ref[lo:hi, :])
        is_lat = latent.astype(F32)
        dsh = _colsum(dh)
        dsc = _colsum(dh * xg)
        dmod_ref[0, 0:1, :] += dsh * (1.0 - is_lat)
        dmod_ref[0, 1:2, :] += dsc * (1.0 - is_lat)
        dmod_ref[1, 0:1, :] += dsh * is_lat
        dmod_ref[1, 1:2, :] += dsc * is_lat
        dxg = dh * (1.0 + scale)
        dng_ref[...] += _colsum(dxg * xh)
        dxh = dxg * ng
        gx_ref[...] = r * (dxh - xh * jnp.mean(dxh * xh, axis=-1, keepdims=True)) + dxn_ref[...]

    row = lambda w: pl.BlockSpec((TB, w), lambda i: (i, 0))
    lat = lambda w: pl.BlockSpec((TB, w), lambda i: (jnp.maximum(i - off, 0), 0))
    last8 = s_len // HALO - 1
    cspec, xspec, mspec = _token_specs(off)
    return pl.pallas_call(
        body, name="in_bwd", grid=(nb,),
        in_specs=[cspec, xspec, mspec, _full((1, D)), row(512), lat(512), lat(512), lat(512),
                  pl.BlockSpec((HALO, 512), lambda i: (jnp.maximum(jnp.maximum(i - off, 0) * hb - 1, 0), 0)),
                  pl.BlockSpec((HALO, 512), lambda i: (jnp.minimum((jnp.maximum(i - off, 0) + 1) * hb, last8), 0)),
                  lat(D), _full((DIN, D))],
        out_specs=[lat(D), _full((DIN, D)), _full((2, 2, D)), _full((1, D))],
        out_shape=[jax.ShapeDtypeStruct((s_len, D), F32), jax.ShapeDtypeStruct((DIN, D), F32),
                   jax.ShapeDtypeStruct((2, 2, D), F32), jax.ShapeDtypeStruct((1, D), F32)],
        compiler_params=_params(("arbitrary",)),
    )(ctx, x, modsel, norm_g, dlo, dga, dgp, dpool, dpool, dpool, dxn, w_in_t)


def _adamw_update(w_ref, g_ref, m_ref, v_ref, d_ref, mo_ref, vo_ref):
    gv = g_ref[...]
    mn = ADAM_B1 * m_ref[...] + (1.0 - ADAM_B1) * gv
    vn = ADAM_B2 * v_ref[...] + (1.0 - ADAM_B2) * (gv * gv)
    m_hat = mn / (1.0 - ADAM_B1 ** ADAM_STEP)
    v_hat = vn / (1.0 - ADAM_B2 ** ADAM_STEP)
    d_ref[...] = -ADAM_LR * (m_hat / (jnp.sqrt(v_hat) + ADAM_EPS) + ADAM_WD * w_ref[...])
    mo_ref[...] = mn
    vo_ref[...] = vn


def _adamw_many(ws, gs, ms, vs):
    n = len(ws)

    def body(*refs):
        for i in range(n):
            _adamw_update(refs[i], refs[n + i], refs[2 * n + i], refs[3 * n + i], refs[4 * n + i], refs[5 * n + i], refs[6 * n + i])

    def spec(w):
        rows, cols = w.shape
        return pl.BlockSpec((rows // 2, cols), lambda i: (i, 0)) if rows % 16 == 0 else _full((rows, cols))

    specs = [spec(w) for w in ws]
    shp = [jax.ShapeDtypeStruct(w.shape, F32) for w in ws]
    out = pl.pallas_call(body, name="adamw_many", grid=(2,), in_specs=specs * 4, out_specs=specs * 3, out_shape=shp * 3,
                         compiler_params=_params(("arbitrary",)))(*ws, *gs, *ms, *vs)
    return out[:n], out[n:2 * n], out[2 * n:]


def _adamw(w, g, m, v, name):
    rows, cols = w.shape
    rb = next(r for r in range(min(rows, 256), 0, -8) if rows % r == 0)

    def body(w_ref, g_ref, m_ref, v_ref, d_ref, mo_ref, vo_ref):
        _adamw_update(w_ref, g_ref, m_ref, v_ref, d_ref, mo_ref, vo_ref)

    spec = pl.BlockSpec((rb, cols), lambda i: (i, 0))
    shp = jax.ShapeDtypeStruct((rows, cols), F32)
    return pl.pallas_call(
        body, name=name, grid=(rows // rb,), in_specs=[spec] * 4, out_specs=[spec] * 3, out_shape=[shp] * 3,
        compiler_params=_params(("arbitrary",)),
    )(w, g, m, v)


class _Links:
    def __init__(self, send_sems, recv_sems):
        self.send_sems, self.recv_sems, self.sends = send_sems, recv_sems, []

    def send(self, src, dst, sem, to):
        cp = pltpu.make_async_remote_copy(src, dst, self.send_sems.at[sem], self.recv_sems.at[sem], device_id=to,
                                          device_id_type=MESH)
        cp.start()
        self.sends.append(cp)

    def arrived(self, dst, sem, frm):
        pltpu.make_async_remote_copy(dst, dst, self.send_sems.at[sem], self.recv_sems.at[sem], device_id=frm,
                                     device_id_type=MESH).wait_recv()

    def drain(self):
        for cp in self.sends:
            cp.wait_send()


def _half(ref, c, axis):
    size = ref.shape[axis - 2] // 2
    win = pl.ds(pl.multiple_of(c * size, 16 if axis == 0 else 128), size)
    idx = (win, slice(None)) if axis == 0 else (slice(None), win)
    return ref.at[(slice(None),) * (len(ref.shape) - 2) + idx]


def _select_rows(slots_ref, n_slots, row=0):
    sub = lax.broadcasted_iota(jnp.int32, (8, 1), 0)
    out = None
    for d in range(n_slots):
        r = jnp.where(sub == d, jnp.broadcast_to(slots_ref[d][row:row + 1, :], (8, slots_ref.shape[-1])), 0.0)
        out = r if out is None else out + r
    return out


def _gather(c, c_ctx, w_mod, b_mod_k, shards, axes):
    nw = len(shards)
    kw = w_mod.shape[1]

    def body(*refs):
        c_ref, cc_ref, wm_ref, b_ref = refs[:4]
        w_refs = refs[4:4 + nw]
        a16_ref, mod_ref = refs[4 + nw:6 + nw]
        g_refs = refs[6 + nw:6 + 2 * nw]
        a_ref, send_sems, recv_sems, local_sems = refs[6 + 2 * nw:]
        x, y, cc = lax.axis_index("x"), lax.axis_index("y"), lax.axis_index("c")
        me = 4 * x + 2 * y + cc
        k = 2 * x + y
        sibling = (x, y, 1 - cc)
        links = _Links(send_sems, recv_sems)
        chips = [_peer(x, y, cc, off + (0,)) for off in CHIPS3]
        chip_a = ((x + 1 - cc) % 2, (y + cc) % 2, cc)
        chip_b = ((x + cc) % 2, (y + 1 - cc) % 2, cc)
        chip_d = (1 - x, 1 - y, cc)
        locals_ = []
        for wi in range(nw):
            lc = pltpu.make_async_copy(w_refs[wi], g_refs[wi].at[k], local_sems.at[wi])
            lc.start()
            locals_.append(lc)
            for j, to in enumerate((chip_a, chip_b)):
                links.send(_half(w_refs[wi], cc, axes[wi]), _half(g_refs[wi].at[k], cc, axes[wi]), 10 + wi * 6 + j, to)
        cv = c_ref[...]
        a_ref[me] = jnp.broadcast_to(cv * _sig(cv), (8, D))
        for j, off in enumerate(PEERS7):
            links.send(a_ref.at[me], a_ref.at[me], j, _peer(x, y, cc, off))
        for j, off in enumerate(PEERS7):
            px, py, pc = _peer(x, y, cc, off)
            links.arrived(a_ref.at[4 * px + 2 * py + pc], j, (px, py, pc))
        ccv = cc_ref[...]
        sub = lax.broadcasted_iota(jnp.int32, (8, 1), 0)
        a16 = jnp.concatenate([_select_rows(a_ref, 8), jnp.where(sub == 0, jnp.broadcast_to(ccv * _sig(ccv), (8, D)), 0.0)], axis=0)
        a16_ref[...] = a16
        mod_ref[k] = _dot3(_nn, a16, wm_ref[...]) + b_ref[...]
        for j, to in enumerate(chips):
            links.send(mod_ref.at[k], mod_ref.at[k], 7 + j, to)
        for j, (frm, origin) in enumerate(((chip_a, chip_a), (chip_b, chip_b), (chip_b, chip_d))):
            for wi in range(nw):
                blk = _half(g_refs[wi].at[2 * origin[0] + origin[1]], cc, axes[wi])
                links.arrived(blk, 10 + wi * 6 + j, frm)
                if j == 0:
                    links.send(blk, blk, 10 + wi * 6 + 2, chip_b)
                links.send(blk, blk, 10 + wi * 6 + 3 + j, sibling)
        for j, (px, py, pc) in enumerate(chips):
            links.arrived(mod_ref.at[2 * px + py], 7 + j, (px, py, pc))
        for j, origin in enumerate((chip_b, chip_a, chip_d)):
            for wi in range(nw):
                links.arrived(_half(g_refs[wi].at[2 * origin[0] + origin[1]], 1 - cc, axes[wi]), 10 + wi * 6 + 3 + j, sibling)
        links.drain()
        for lc in locals_:
            lc.wait()

    nsem = 10 + 6 * nw
    return pl.pallas_call(
        body, name="gather", in_specs=[VM] * 4 + [ANY] * nw, out_specs=[VM, VM] + [ANY] * nw,
        out_shape=[jax.ShapeDtypeStruct((16, D), F32), jax.ShapeDtypeStruct((4, 16, kw), F32)]
        + [jax.ShapeDtypeStruct((4,) + s.shape, s.dtype) for s in shards],
        scratch_shapes=[pltpu.VMEM((8, 8, D), F32), pltpu.SemaphoreType.DMA((nsem,)), pltpu.SemaphoreType.DMA((nsem,)),
                        pltpu.SemaphoreType.DMA((nw,))],
        compiler_params=pltpu.CompilerParams(vmem_limit_bytes=VMEM_LIMIT),
    )(c, c_ctx, w_mod, b_mod_k, *shards)


SMALL_ROW_WIDTHS = (D, QL, KVL, DKP, DKP, 512, 128)
SMALL_OUT_WIDTHS = (D, QL, KVL, DK, DK, 512, 1)


def _reduce(grads, axes, smalls, w_pool_g, dmod8, a16, w_mod, c_ctx):
    nw = len(grads)
    ns = len(smalls)
    kw = w_mod.shape[1]
    halves = []
    for g, ax in zip(grads, axes):
        halves.append((g.shape[1] // 2, g.shape[2]) if ax == 0 else (g.shape[1], g.shape[2] // 2))

    def body(*refs):
        g_refs = refs[:nw]
        small_refs = refs[nw:nw + ns]
        wp_ref, dm_ref, a16_ref, wm_ref, cc_ref = refs[nw + ns:nw + ns + 5]
        o = nw + ns + 5
        r_refs = refs[o:o + nw]
        small_outs = refs[o + nw:o + nw + ns]
        rwp_ref, gw_ref, gb_ref, gc_ref = refs[o + nw + ns:o + nw + ns + 4]
        o = o + nw + ns + 4
        own, sib, part, got = (refs[o + i * nw:o + (i + 1) * nw] for i in range(4))
        smbuf, wps, wpg, dm_all, pc_all, send_sems, recv_sems, local_sems = refs[o + 4 * nw:]
        x, y, cc = lax.axis_index("x"), lax.axis_index("y"), lax.axis_index("c")
        me = 4 * x + 2 * y + cc
        k = 2 * x + y
        sibling = (x, y, 1 - cc)
        links = _Links(send_sems, recv_sems)
        chips = [_peer(x, y, cc, off + (0,)) for off in CHIPS3]
        peers = [_peer(x, y, cc, off) for off in PEERS7]
        big, sm0, wp0, dm0, pc0 = 0, 5 * nw, 5 * nw + 7, 5 * nw + 14, 5 * nw + 21

        locals_ = []
        for wi in range(nw):
            lc = pltpu.make_async_copy(_half(g_refs[wi], cc, axes[wi]), own[wi], local_sems.at[wi])
            lc.start()
            locals_.append(lc)
            links.send(_half(g_refs[wi], 1 - cc, axes[wi]), sib[wi], big + wi * 5, sibling)
        slot = smbuf.at[me]
        slot[...] = jnp.zeros((8, D), F32)
        for r, (ref, w) in enumerate(zip(small_refs, SMALL_ROW_WIDTHS)):
            slot[r:r + 1, 0:w] = jnp.broadcast_to(ref[...], (1, w))
        links.send(wp_ref, wps, wp0, sibling)
        dm_all[me] = dm_ref[...]
        for j, peer in enumerate(peers):
            links.send(dm_all.at[me], dm_all.at[me], dm0 + j, peer)
            links.send(smbuf.at[me], smbuf.at[me], sm0 + j, peer)
        links.arrived(wps, wp0, sibling)
        wpg[k] = (wp_ref[...] + wps[...]).astype(BF16)
        for j, to in enumerate(chips):
            links.send(wpg.at[k], wpg.at[k], wp0 + 1 + j, to)
        for wi in range(nw):
            locals_[wi].wait()
            links.arrived(sib[wi], big + wi * 5, sibling)
            part[wi][...] = (own[wi][...] + sib[wi][...]).astype(BF16)
            got[wi][k] = part[wi][k]
            for j, (px, py, pc) in enumerate(chips):
                links.send(part[wi].at[2 * px + py], got[wi].at[k], big + wi * 5 + 1 + j, (px, py, pc))
        for j, (px, py, pc) in enumerate(peers):
            links.arrived(dm_all.at[4 * px + 2 * py + pc], dm0 + j, (px, py, pc))
        dmc = dm_all[0][1:2, :]
        dml = dm_all[0][0:1, :]
        for d in range(1, 8):
            dmc = dmc + dm_all[d][1:2, :]
            dml = dml + dm_all[d][0:1, :]
        gb_ref[...] = dml + dmc
        sub = lax.broadcasted_iota(jnp.int32, (8, 1), 0)
        b16 = jnp.concatenate([_select_rows(dm_all, 8), jnp.where(sub == 0, jnp.broadcast_to(dmc, (8, 3 * D)), 0.0)], axis=0)
        bk = jnp.zeros((16, kw), F32)
        for kk in range(4):
            bk = bk + jnp.where(k == kk, b16[:, kk * kw:(kk + 1) * kw], 0.0)
        gw_ref[...] = _dot3(_tn, a16_ref[...], bk)
        pc_all[k] = _dot3(_nt, jnp.broadcast_to(bk[8:9, :], (8, kw)), wm_ref[...])
        for j, to in enumerate(chips):
            links.send(pc_all.at[k], pc_all.at[k], pc0 + j, to)
        for j, (px, py, pc) in enumerate(peers):
            links.arrived(smbuf.at[4 * px + 2 * py + pc], sm0 + j, (px, py, pc))
        tot = smbuf[0]
        for d in range(1, 8):
            tot = tot + smbuf[d]
        for r, (ref, w) in enumerate(zip(small_outs, SMALL_OUT_WIDTHS)):
            ref[...] = tot[r:r + 1, 0:w]
        for j, (px, py, pc) in enumerate(chips):
            links.arrived(wpg.at[2 * px + py], wp0 + 1 + j, (px, py, pc))
        wpt = wpg[0].astype(F32)
        for kk in range(1, 4):
            wpt = wpt + wpg[kk].astype(F32)
        rwp_ref[...] = wpt
        for wi in range(nw):
            for j, (px, py, pc) in enumerate(chips):
                links.arrived(got[wi].at[2 * px + py], big + wi * 5 + 1 + j, (px, py, pc))
            total = got[wi][0].astype(F32)
            for kk in range(1, 4):
                total = total + got[wi][kk].astype(F32)
            mine = _half(r_refs[wi], cc, axes[wi])
            mine[...] = total
            links.send(mine, mine, big + wi * 5 + 4, sibling)
        for j, (px, py, pc) in enumerate(chips):
            links.arrived(pc_all.at[2 * px + py], pc0 + j, (px, py, pc))
        ccv = cc_ref[...]
        sg = _sig(ccv)
        gc_ref[...] = (pc_all[0][0:1, :] + pc_all[1][0:1, :] + pc_all[2][0:1, :] + pc_all[3][0:1, :]) * (sg * (1.0 + ccv * (1.0 - sg)))
        for wi in range(nw):
            links.arrived(_half(r_refs[wi], 1 - cc, axes[wi]), big + wi * 5 + 4, sibling)
        links.drain()

    nsem = 5 * nw + 24
    quads = [(4,) + h for h in halves]
    return pl.pallas_call(
        body, name="reduce", in_specs=[ANY] * nw + [VM] * (ns + 5), out_specs=[VM] * (nw + ns + 4),
        out_shape=[jax.ShapeDtypeStruct(g.shape[1:], F32) for g in grads]
        + [jax.ShapeDtypeStruct((1, w), F32) for w in SMALL_OUT_WIDTHS]
        + [jax.ShapeDtypeStruct(w_pool_g.shape, F32), jax.ShapeDtypeStruct((D, kw), F32), jax.ShapeDtypeStruct((1, 3 * D), F32),
           jax.ShapeDtypeStruct((1, D), F32)],
        scratch_shapes=[pltpu.VMEM(q, F32) for q in quads] + [pltpu.VMEM(q, F32) for q in quads]
        + [pltpu.VMEM(q, BF16) for q in quads] + [pltpu.VMEM(q, BF16) for q in quads]
        + [pltpu.VMEM((8, 8, D), F32), pltpu.VMEM(w_pool_g.shape, F32), pltpu.VMEM((4,) + w_pool_g.shape, BF16),
           pltpu.VMEM((8, 8, 3 * D), F32),
           pltpu.VMEM((4, 8, D), F32)]
        + [pltpu.SemaphoreType.DMA((nsem,)), pltpu.SemaphoreType.DMA((nsem,)), pltpu.SemaphoreType.DMA((nw,))],
        compiler_params=pltpu.CompilerParams(vmem_limit_bytes=VMEM_LIMIT),
    )(*grads, *smalls, w_pool_g, dmod8, a16, w_mod, c_ctx)


def _rope_tables(s_len):
    rows = s_len // GRID_W
    per = TB // GRID_W
    n_freq = 16
    inv = ROPE_BASE ** (-jnp.arange(n_freq, dtype=F32) / n_freq)
    ang_r = jnp.arange(rows, dtype=F32)[:, None] * inv
    ang_c = jnp.arange(GRID_W, dtype=F32)[:, None] * inv
    by_row, by_col = [], []
    for fn, pad in ((jnp.cos, 1.0), (jnp.sin, 0.0)):
        r = jnp.concatenate([fn(ang_r), fn(ang_r), jnp.zeros((rows, 96), F32)], axis=1).reshape(rows // per, per, 128)
        by_row.append(jnp.pad(r, ((0, 0), (0, 8 - per), (0, 0))))
        cpart = jnp.concatenate([jnp.zeros((GRID_W, 32), F32), fn(ang_c), fn(ang_c), jnp.full((GRID_W, 64), pad, F32)], axis=1)
        by_col.append(jnp.tile(cpart, (per, 1)))
    return jnp.concatenate(by_row, axis=-1), jnp.concatenate(by_col, axis=-1)


def kernel(x, c, ctx, c_ctx, w_mod, b_mod, norm_g, w_in, q_lora_g, w_uq, kv_lora_g, w_ukv, q_norm_g, k_norm_g, w_pool, pool_scale, w_out, loss_target, m_c_ctx, m_w_mod, m_b_mod, m_norm_g, m_w_in, m_q_lora_g, m_w_uq, m_kv_lora_g, m_w_ukv, m_q_norm_g, m_k_norm_g, m_w_pool, m_pool_scale, m_w_out, v_c_ctx, v_w_mod, v_b_mod, v_norm_g, v_w_in, v_q_lora_g, v_w_uq, v_kv_lora_g, v_w_ukv, v_q_norm_g, v_k_norm_g, v_w_pool, v_pool_scale, v_w_out):
    xi, yi, ci = lax.axis_index("x"), lax.axis_index("y"), lax.axis_index("c")
    me = 4 * xi + 2 * yi + ci
    k = 2 * xi + yi
    s_len = x.shape[1]
    lc = ctx.shape[1]
    kw = w_mod.shape[2]
    weights = dict(c_ctx=c_ctx, w_mod=w_mod, b_mod=b_mod, norm_g=norm_g, w_in=w_in, q_lora_g=q_lora_g, w_uq=w_uq,
                   kv_lora_g=kv_lora_g, w_ukv=w_ukv, q_norm_g=q_norm_g, k_norm_g=k_norm_g, w_pool=w_pool,
                   pool_scale=pool_scale, w_out=w_out)
    m_in = dict(c_ctx=m_c_ctx, w_mod=m_w_mod, b_mod=m_b_mod, norm_g=m_norm_g, w_in=m_w_in, q_lora_g=m_q_lora_g, w_uq=m_w_uq,
                kv_lora_g=m_kv_lora_g, w_ukv=m_w_ukv, q_norm_g=m_q_norm_g, k_norm_g=m_k_norm_g, w_pool=m_w_pool,
                pool_scale=m_pool_scale, w_out=m_w_out)
    v_in = dict(c_ctx=v_c_ctx, w_mod=v_w_mod, b_mod=v_b_mod, norm_g=v_norm_g, w_in=v_w_in, q_lora_g=v_q_lora_g, w_uq=v_w_uq,
                kv_lora_g=v_kv_lora_g, w_ukv=v_w_ukv, q_norm_g=v_q_norm_g, k_norm_g=v_k_norm_g, w_pool=v_w_pool,
                pool_scale=v_pool_scale, w_out=v_w_out)
    order = ["c_ctx", "w_mod", "b_mod", "norm_g", "w_in", "q_lora_g", "w_uq", "kv_lora_g", "w_ukv", "q_norm_g", "k_norm_g",
             "w_pool", "pool_scale", "w_out"]
    transposed = ("w_in", "w_uq")
    as2d = lambda n, a: jnp.transpose(a[0]) if n in transposed else a.reshape(-1, a.shape[-1])
    back = lambda n, a: jnp.transpose(a)[None] if n in transposed else a.reshape(weights[n].shape)

    c_ctx2 = c_ctx.reshape(1, D)
    b_mod_k = lax.dynamic_slice(b_mod, (0, k * kw), (1, kw))
    split = (1, 0, 0, 0)
    a16, mod_all, g_in, g_uq, g_ukv, g_out = _gather(
        c, c_ctx2, w_mod[0], b_mod_k,
        [as2d("w_in", w_in).astype(BF16), as2d("w_uq", w_uq).astype(BF16), w_ukv[0].astype(BF16), w_out[0].astype(BF16)], split)
    mod_me = lax.dynamic_index_in_dim(mod_all, me, axis=1, keepdims=False).reshape(3, D)
    mod_c = mod_all[:, 8, :].reshape(3, D)
    modsel = jnp.stack([mod_c, mod_me])
    w_in_t = g_in.reshape(DIN, D)
    w_uq_t = jnp.pad(g_uq, ((0, 0), (0, DKP - DK), (0, 0)))
    w_out_f = g_out.reshape(D, D)
    qn_g = jnp.pad(q_norm_g, ((0, 0), (0, DKP - DK)))
    kn_g = jnp.pad(k_norm_g, ((0, 0), (0, DKP - DK)))
    w_pool_b = w_pool[0].astype(BF16)
    cos, sin = _rope_tables(s_len)

    u, q, kk, v = _fwd_in(ctx[0], x[0], modsel, norm_g, w_in_t, q_lora_g, w_uq_t, kv_lora_g, g_ukv, qn_g, kn_g, cos, sin)
    attn, lse = _attn_fwd(q, kk, v, s_len)
    (dxn, dattn, dga, dgp, dpool, dw_out, dgate, dps, dw_pool, loss) = _out_stage(
        attn.reshape(s_len // Q_BLOCK, Q_BLOCK, NH * DV), u, x[0], loss_target[0], modsel[1, 2:3, :], w_pool_b, pool_scale,
        w_out_f, lc)
    dattn = dattn.reshape(s_len, NH * DV)
    dq, dk, dv = _attn_bwd(q, kk, v, dattn, attn, lse, s_len)
    dlo, dw_uq_t, dw_ukv, dqlg, dkvlg, dqng, dkng = _qkv_bwd(u, dq, dk, dv, cos, sin, q_lora_g, w_uq_t, kv_lora_g, g_ukv,
                                                            qn_g, kn_g, s_len)
    gx, dw_in_t, dmod, dng = _in_bwd(ctx[0], x[0], modsel, norm_g, dlo, dga, dgp, dpool, dxn, w_in_t)

    dmod_l = jnp.concatenate([dmod[1, 0], dmod[1, 1], dgate[0]]).reshape(1, 3 * D)
    dmod_c = jnp.concatenate([dmod[0, 0], dmod[0, 1], jnp.zeros((D,), F32)]).reshape(1, 3 * D)
    dmod8 = jnp.concatenate([dmod_l, dmod_c, jnp.zeros((6, 3 * D), F32)], axis=0)
    (r_in, r_uq, r_ukv, r_out, g_ng, g_qlg, g_kvlg, g_qng, g_kng, g_ps, loss_all, g_wp, g_w_mod, g_b_mod, g_c_ctx) = _reduce(
        [dw_in_t.reshape(4, DIN // 4, D), dw_uq_t, dw_ukv, dw_out.reshape(4, D // 4, D)], split,
        [dng, dqlg, dkvlg, dqng, dkng, dps, loss], dw_pool, dmod8, a16, w_mod[0], c_ctx2)
    g2d = dict(c_ctx=g_c_ctx, b_mod=g_b_mod, w_mod=g_w_mod, w_in=r_in, w_uq=r_uq, w_ukv=r_ukv, w_out=r_out, norm_g=g_ng,
               q_lora_g=g_qlg, kv_lora_g=g_kvlg, q_norm_g=g_qng, k_norm_g=g_kng, pool_scale=g_ps, w_pool=g_wp.reshape(512, 128))

    d2d, m2d, v2d = {}, {}, {}
    d2d["w_mod"], m2d["w_mod"], v2d["w_mod"] = _adamw(as2d("w_mod", w_mod), g2d["w_mod"], as2d("w_mod", m_w_mod),
                                                      as2d("w_mod", v_w_mod), "adamw_w_mod")
    rest = [n for n in order if n != "w_mod"]
    outs = _adamw_many([as2d(n, weights[n]) for n in rest], [g2d[n] for n in rest], [as2d(n, m_in[n]) for n in rest],
                       [as2d(n, v_in[n]) for n in rest])
    for dst, arrs in zip((d2d, m2d, v2d), outs):
        dst.update(dict(zip(rest, arrs)))

    return (loss_all[0, 0], gx[None], *[back(n, g2d[n]) for n in order], *[back(n, d2d[n]) for n in order],
            *[back(n, m2d[n]) for n in order], *[back(n, v2d[n]) for n in order])
```

```python
import jax
import jax.numpy as jnp
from jax import lax
from jax.experimental import pallas as pl
from jax.experimental.pallas import tpu as pltpu

F32 = jnp.float32
BF16 = jnp.bfloat16
MESH = pl.DeviceIdType.MESH

D = 1024
NH = 4
DK = 192
DKP = 256
DV = 128
QL = 256
KVL = 128
DIN = 1984
U_LO = 448
SEG = ((0, 512), (448, 960), (960, 1472), (1472, 1984))
DU = 2048
POOL_WINDOWS = (2, 4, 8, 16)
HALO = 8
EPS = 1e-6
ROPE_BASE = 10000.0
GRID_W = 64
Q_BLOCK = 128
TB = 256
BWD_QBLOCKS = 1
LOCAL_PARTS = 4
SCALE = DK ** -0.5
LOG2E = 1.4426950408889634
LN2 = 0.6931471805599453
VMEM_LIMIT = 56 * 1024 * 1024

ADAM_LR = 0.001
ADAM_B1 = 0.9
ADAM_B2 = 0.999
ADAM_EPS = 1e-08
ADAM_WD = 0.01
ADAM_STEP = 10

CHIPS3 = ((1, 0), (0, 1), (1, 1))
PEERS7 = tuple((dx, dy, dc) for dx in (0, 1) for dy in (0, 1) for dc in (0, 1) if (dx, dy, dc) != (0, 0, 0))

VM = pl.BlockSpec(memory_space=pltpu.VMEM)
ANY = pl.BlockSpec(memory_space=pl.ANY)


def _nn(a, b):
    return jnp.dot(a, b, preferred_element_type=F32)


def _nt(a, b):
    return lax.dot_general(a, b, (((1,), (1,)), ((), ())), preferred_element_type=F32)


def _tn(a, b):
    return lax.dot_general(a, b, (((0,), (0,)), ((), ())), preferred_element_type=F32)


def _split3(a):
    a0 = a.astype(BF16)
    r = a - a0.astype(F32)
    a1 = r.astype(BF16)
    a2 = (r - a1.astype(F32)).astype(BF16)
    return a0, a1, a2


def _dot3(dot, a, b):
    sa = _split3(a)
    sb = _split3(b)
    out = None
    for i in range(3):
        for j in range(3 - i):
            t = dot(sa[i], sb[j])
            out = t if out is None else out + t
    return out


def _sig(x):
    return 1.0 / (1.0 + jnp.exp(-x))


def _rot(t):
    src = lax.broadcasted_iota(jnp.int32, (128, 128), 0)
    dst = lax.broadcasted_iota(jnp.int32, (128, 128), 1)
    first = (dst % 32) < 16
    perm = jnp.where(first & (src == dst + 16), -1.0, jnp.where(~first & (src == dst - 16), 1.0, 0.0)).astype(BF16)
    hi = t.astype(BF16)
    lo = (t - hi.astype(F32)).astype(BF16)
    return _nn(hi, perm) + _nn(lo, perm)


def _rope(t, cos, sin):
    return t * cos + _rot(t) * sin


def _rope_t(t, cos, sin):
    return t * cos - _rot(t * sin)


def _rope_block(rows_ref, cols_ref, is_ctx):
    lane = lax.broadcasted_iota(jnp.int32, (TB, 256), 1) % 128
    rows = jnp.concatenate([jnp.broadcast_to(rows_ref[0, r:r + 1, :], (GRID_W, 256)) for r in range(TB // GRID_W)], axis=0)
    cs = jnp.where(lane < 32, rows, cols_ref[...])
    return jnp.where(is_ctx, 1.0, cs[:, :128]), jnp.where(is_ctx, 0.0, cs[:, 128:])


def _shift_rows(z, k):
    n = z.shape[0]
    return pltpu.roll(z, (n - k) % n, 0)


def _colsum(a):
    return jnp.sum(a, axis=0, keepdims=True)


def _rowsum(a):
    return jnp.sum(a, axis=-1, keepdims=True)


def _row_layout(col):
    return jnp.transpose(jnp.broadcast_to(col, (col.shape[0], 128)))[0:8, :]


def _params(sem=None):
    return pltpu.CompilerParams(dimension_semantics=sem, vmem_limit_bytes=VMEM_LIMIT)


def _full(shape):
    nd = len(shape)
    return pl.BlockSpec(shape, lambda *_: (0,) * nd)


def _peer(x, y, c, off):
    dx, dy, dc = off
    return ((x + dx) % 2, (y + dy) % 2, (c + dc) % 2)


def _token_specs(off):
    ctx = pl.BlockSpec((TB, D), lambda i: (jnp.minimum(i, off - 1), 0))
    lat = pl.BlockSpec((TB, D), lambda i: (jnp.maximum(i - off, 0), 0))
    mod = pl.BlockSpec((1, 3, D), lambda i: (jnp.minimum(i // off, 1), 0, 0))
    return ctx, lat, mod


def _modulated(x, mod_ref, ng):
    shift = mod_ref[0, 0:1, :]
    scale = mod_ref[0, 1:2, :]
    r = lax.rsqrt(jnp.mean(x * x, axis=-1, keepdims=True) + EPS)
    xh = x * r
    xg = xh * ng
    return r, xh, xg, xg * (1.0 + scale) + shift, scale


def _fwd_in(ctx, x, modsel, norm_g, w_in_t, q_lora_g, w_uq_t, kv_lora_g, w_ukv, qn_g, kn_g, cos, sin):
    s_len, lc = x.shape[0], ctx.shape[0]
    t_all = s_len + lc
    nb = t_all // TB
    off = lc // TB

    def body(ctx_ref, x_ref, mod_ref, ng_ref, win_ref, qlg_ref, wuq_ref, kvlg_ref, wukv_ref, qng_ref, kng_ref, cos_ref, sin_ref,
             u_ref, q_ref, k_ref, v_ref):
        is_ctx = pl.program_id(0) < off
        xb = jnp.where(is_ctx, ctx_ref[...], x_ref[...])
        _, _, _, h, _ = _modulated(xb, mod_ref, ng_ref[...])
        hb = h.astype(BF16)
        lane = lax.broadcasted_iota(jnp.int32, (TB, 512), 1)
        ulo = jnp.where(lane < U_LO, _nt(hb, win_ref[SEG[0][0]:SEG[0][1], :]), 0.0)
        u_ref[:, 0:512] = ulo
        for j in range(1, 4):
            u_ref[:, j * 512:(j + 1) * 512] = _nt(hb, win_ref[SEG[j][0]:SEG[j][1], :])
        cos, sin = _rope_block(cos_ref, sin_ref, is_ctx)
        cq = ulo[:, 0:QL]
        cqn = (cq * lax.rsqrt(jnp.mean(cq * cq, axis=-1, keepdims=True) + EPS) * qlg_ref[...]).astype(BF16)
        qng = qng_ref[...]
        ckv = ulo[:, QL:QL + KVL]
        ckvn = (ckv * lax.rsqrt(jnp.mean(ckv * ckv, axis=-1, keepdims=True) + EPS) * kvlg_ref[...]).astype(BF16)
        qhs = [_nt(cqn, wuq_ref[hd]) for hd in range(NH)]
        kvs = [_nn(ckvn, wukv_ref[hd]) for hd in range(NH)]
        for hd in range(NH):
            qh = qhs[hd]
            qn = qh * lax.rsqrt(_rowsum(qh * qh) / DK + EPS) * qng
            q_ref[hd] = (jnp.concatenate([qn[:, :128], _rope(qn[:, 128:], cos, sin)], axis=1) * (SCALE * LOG2E)).astype(BF16)
        kr = ulo[:, 384:512]
        skr = _rowsum(kr * kr)
        kng = kng_ref[...]
        kr_roped = _rope(kr * kng[:, 128:], cos, sin)
        for hd in range(NH):
            kv = kvs[hd]
            kn = kv[:, :128]
            rk = lax.rsqrt((_rowsum(kn * kn) + skr) / DK + EPS)
            k_ref[hd] = jnp.concatenate([kn * rk * kng[:, :128], kr_roped * rk], axis=1).astype(BF16)
            v_ref[hd] = kv[:, 128:].astype(BF16)

    row = lambda w: pl.BlockSpec((TB, w), lambda i: (i, 0))
    heads = lambda w: pl.BlockSpec((NH, TB, w), lambda i: (0, i, 0))
    cspec, xspec, mspec = _token_specs(off)
    return pl.pallas_call(
        body, name="fwd_in", grid=(nb,),
        in_specs=[cspec, xspec, mspec, _full((1, D)), _full((DIN, D)), _full((1, QL)), _full((NH, DKP, QL)), _full((1, KVL)),
                  _full((NH, KVL, 256)), _full((1, DKP)), _full((1, DKP)),
                  pl.BlockSpec((1, 8, 256), lambda i: (jnp.maximum(i - off, 0), 0, 0)), _full((TB, 256))],
        out_specs=[row(DU), heads(DKP), heads(DKP), heads(DV)],
        out_shape=[jax.ShapeDtypeStruct((t_all, DU), F32), jax.ShapeDtypeStruct((NH, t_all, DKP), BF16),
                   jax.ShapeDtypeStruct((NH, t_all, DKP), BF16), jax.ShapeDtypeStruct((NH, t_all, DV), BF16)],
        compiler_params=_params(("arbitrary",)),
    )(ctx, x, modsel, norm_g, w_in_t, q_lora_g, w_uq_t, kv_lora_g, w_ukv, qn_g, kn_g, cos, sin)


def _attn_fwd(q, k, v, s_len):
    t_all = q.shape[1]
    off = (t_all - s_len) // TB
    nq = s_len // TB
    nsub = next(n for n in (4, 2, 1) if nq % n == 0)

    def body(*refs):
        q_refs = refs[:nsub]
        k_ref, v_ref, o_ref, lse_ref = refs[nsub:]
        for sb in range(nsub):
            s = _nt(q_refs[sb][0], k_ref[0])
            m = jnp.max(s, axis=-1, keepdims=True)
            e = jnp.exp2(s - m)
            l = _rowsum(e)
            o_ref[sb * TB:(sb + 1) * TB, :] = _nn(e.astype(BF16), v_ref[0]) / l
            lse_ref[0, sb] = _row_layout(m + jnp.log2(l))

    qspec = lambda sb: pl.BlockSpec((1, TB, DKP), lambda h, i: (h, i * nsub + sb + off, 0))
    return pl.pallas_call(
        body, name="attn_fwd", grid=(NH, nq // nsub),
        in_specs=[qspec(sb) for sb in range(nsub)]
        + [pl.BlockSpec((1, t_all, DKP), lambda h, i: (h, 0, 0)), pl.BlockSpec((1, t_all, DV), lambda h, i: (h, 0, 0))],
        out_specs=[pl.BlockSpec((nsub * TB, DV), lambda h, i: (i, h)), pl.BlockSpec((1, nsub, 8, TB), lambda h, i: (h, i, 0, 0))],
        out_shape=[jax.ShapeDtypeStruct((s_len, NH * DV), F32), jax.ShapeDtypeStruct((NH, nq, 8, TB), F32)],
        compiler_params=_params(("arbitrary", "arbitrary")),
    )(*([q] * nsub), k, v)


def _out_stage(attn, u, x, target, gate, w_pool, pool_scale, w_out, lc):
    s_len = x.shape[0]
    t_all = s_len + lc
    off = lc // TB
    nq = s_len // TB
    hb = TB // HALO
    nqb = s_len // Q_BLOCK
    jb = TB // nqb

    def body(attn_ref, ga_ref, pin_ref, pprev_ref, pnext_ref, gp_ref, x_ref, tgt_ref, gate_ref, wp_ref, ps_ref, wo_ref,
             dxn_ref, dattn_ref, dga_ref, dgp_ref, dpool_ref, dwo_ref, dgate_ref, dps_ref, dwp_ref, loss_ref):
        i = pl.program_id(0)

        @pl.when(i == 0)
        def _():
            dwo_ref[...] = jnp.zeros_like(dwo_ref)
            dgate_ref[...] = jnp.zeros_like(dgate_ref)
            dps_ref[...] = jnp.zeros_like(dps_ref)
            dwp_ref[...] = jnp.zeros_like(dwp_ref)
            loss_ref[...] = jnp.zeros_like(loss_ref)

        attn = jnp.concatenate([attn_ref[:, jj, :] for jj in range(jb)], axis=0)
        ga = ga_ref[...]
        gp = gp_ref[...]
        pin = pin_ref[...]
        prev = jnp.where(i == 0, 0.0, pprev_ref[...])
        nxt = jnp.where(i == nq - 1, 0.0, pnext_ref[...])
        win = jnp.concatenate([prev, pin, nxt], axis=0)
        tg = i * TB + lax.broadcasted_iota(jnp.int32, (TB, 1), 0)
        pooled = []
        for g, w in enumerate(POOL_WINDOWS):
            a = win[:, g * 128:(g + 1) * 128]
            p = _shift_rows(a, -1) + a
            for step in (1, 2, 4):
                if w >= 4 * step:
                    p = _shift_rows(p, -step) + _shift_rows(p, step)
            cnt = (jnp.minimum(tg + w // 2, s_len) - jnp.maximum(tg - w // 2, 0)).astype(F32)
            pooled.append(p[HALO:HALO + TB] / cnt - a[HALO:HALO + TB])
        pooled_b = [p.astype(BF16) for p in pooled]
        z = jnp.concatenate([_nn(pooled_b[g], wp_ref[g]) for g in range(4)], axis=1)
        ps = ps_ref[...]
        yp = z * ps
        sga = _sig(ga)
        sila = ga * sga
        sgp = _sig(gp)
        silp = gp * sgp
        br = jnp.concatenate([sila * attn, silp * yp], axis=1).astype(BF16)
        y = _nn(br, wo_ref[...])
        gate = gate_ref[...]
        err = x_ref[...] + gate * y - tgt_ref[...]
        loss_ref[...] += _colsum(_rowsum(err * err)) * (0.5 / D)
        dxn = err * (1.0 / D)
        dxn_ref[...] = dxn
        dgate_ref[...] += _colsum(dxn * y)
        dy = (dxn * gate).astype(BF16)
        dwo_ref[...] += _tn(br, dy)
        dbr = _nt(dy, wo_ref[...])
        dbra = dbr[:, :512]
        dbrp = dbr[:, 512:]
        dattn = dbra * sila
        for jj in range(jb):
            dattn_ref[:, jj, :] = dattn[jj * nqb:(jj + 1) * nqb]
        dga_ref[...] = (dbra * attn * (sga * (1.0 + ga * (1.0 - sga)))).astype(BF16)
        dgp_ref[...] = (dbrp * yp * (sgp * (1.0 + gp * (1.0 - sgp)))).astype(BF16)
        dyp = dbrp * silp
        dps_ref[...] += _colsum(dyp * z)
        dz = (dyp * ps).astype(BF16)
        dpool = []
        for g in range(4):
            dzg = dz[:, g * 128:(g + 1) * 128]
            dwp_ref[g] += _tn(pooled_b[g], dzg)
            dpool.append(_nt(dzg, wp_ref[g]))
        dpool_ref[...] = jnp.concatenate(dpool, axis=1)

    lat = lambda w: pl.BlockSpec((TB, w), lambda i: (i, 0))
    perm = pl.BlockSpec((nqb, jb, 512), lambda i: (0, i, 0))
    ucol = lambda j: pl.BlockSpec((TB, 512), lambda i: (i + off, j))
    last8 = t_all // HALO - 1
    return pl.pallas_call(
        body, name="out_stage", grid=(nq,),
        in_specs=[perm, ucol(1), ucol(2),
                  pl.BlockSpec((HALO, 512), lambda i: ((i + off) * hb - 1, 2)),
                  pl.BlockSpec((HALO, 512), lambda i: (jnp.minimum((i + off + 1) * hb, last8), 2)),
                  ucol(3), lat(D), lat(D), _full((1, D)), _full((4, 128, 128)), _full((1, 512)), _full((D, D))],
        out_specs=[lat(D), perm, lat(512), lat(512), lat(512),
                   _full((D, D)), _full((1, D)), _full((1, 512)), _full((4, 128, 128)), _full((1, 1))],
        out_shape=[jax.ShapeDtypeStruct((s_len, D), F32), jax.ShapeDtypeStruct((nqb, Q_BLOCK, 512), F32),
                   jax.ShapeDtypeStruct((s_len, 512), BF16), jax.ShapeDtypeStruct((s_len, 512), BF16),
                   jax.ShapeDtypeStruct((s_len, 512), F32),
                   jax.ShapeDtypeStruct((D, D), F32), jax.ShapeDtypeStruct((1, D), F32), jax.ShapeDtypeStruct((1, 512), F32),
                   jax.ShapeDtypeStruct((4, 128, 128), F32), jax.ShapeDtypeStruct((1, 1), F32)],
        compiler_params=_params(("arbitrary",)),
    )(attn, u, u, u, u, u, x, target, gate, w_pool, pool_scale, w_out)


def _attn_bwd(q, k, v, dattn, attn, lse, s_len):
    t_all = q.shape[1]
    off = (t_all - s_len) // TB
    nq = s_len // TB
    nch = 4
    chunks = [(c * (t_all // nch), t_all // nch) for c in range(nch)]
    nsub = next(n for n in (BWD_QBLOCKS, 2, 1) if nq % n == 0)
    tq = nsub * TB

    def body(*refs):
        q_refs = refs[:nsub]
        k_ref, v_ref, do_ref, o_ref, lse_ref, dq_ref, dk_ref, dv_ref = refs[nsub:]
        i = pl.program_id(1)

        @pl.when(i == 0)
        def _():
            dk_ref[...] = jnp.zeros_like(dk_ref)
            dv_ref[...] = jnp.zeros_like(dv_ref)

        qb = jnp.concatenate([r[0] for r in q_refs], axis=0)
        delta_r = _row_layout(_rowsum(do_ref[...] * o_ref[...]))[0:1, :]
        do = do_ref[...].astype(BF16)
        lse_r = jnp.concatenate([lse_ref[0, sb][0:1, :] for sb in range(nsub)], axis=1)
        dq = jnp.zeros((tq, DKP), F32)
        for start, size in chunks:
            rows = pl.ds(start, size)
            kc = k_ref[0, rows, :]
            p_t = jnp.exp2(_nt(kc, qb) - lse_r)
            ds_t = (p_t * (_nt(v_ref[0, rows, :], do) - delta_r)).astype(BF16)
            dv_ref[0, rows, :] += _nn(p_t.astype(BF16), do)
            dk_ref[0, rows, :] += _nn(ds_t, qb)
            dq += _tn(ds_t, kc)
        dq_ref[0] = dq * SCALE

    kvspec = lambda w: pl.BlockSpec((1, t_all, w), lambda h, i: (h, 0, 0))
    rowspec = pl.BlockSpec((1, nsub, 8, TB), lambda h, i: (h, i, 0, 0))
    qspec = lambda sb: pl.BlockSpec((1, TB, DKP), lambda h, i: (h, i * nsub + sb + off, 0))
    return pl.pallas_call(
        body, name="attn_bwd", grid=(NH, nq // nsub),
        in_specs=[qspec(sb) for sb in range(nsub)]
        + [kvspec(DKP), kvspec(DV), pl.BlockSpec((tq, DV), lambda h, i: (i, h)), pl.BlockSpec((tq, DV), lambda h, i: (i, h)),
           rowspec],
        out_specs=[pl.BlockSpec((1, tq, DKP), lambda h, i: (h, i, 0)), kvspec(DKP), kvspec(DV)],
        out_shape=[jax.ShapeDtypeStruct((NH, s_len, DKP), F32), jax.ShapeDtypeStruct((NH, t_all, DKP), F32),
                   jax.ShapeDtypeStruct((NH, t_all, DV), F32)],
        compiler_params=_params(("arbitrary", "arbitrary")),
    )(*([q] * nsub), k, v, dattn, attn, lse)


def _qkv_bwd(u, dq, dk, dv, cos, sin, q_lora_g, w_uq_t, kv_lora_g, w_ukv, qn_g, kn_g, s_len):
    t_all = u.shape[0]
    off = (t_all - s_len) // TB
    nb = t_all // TB

    def body(ulo_ref, dq_ref, dk_ref, dv_ref, cos_ref, sin_ref, qlg_ref, wuq_ref, kvlg_ref, wukv_ref, qng_ref, kng_ref,
             dlo_ref, dwuq_ref, dwukv_ref, dqlg_ref, dkvlg_ref, dqng_ref, dkng_ref):
        i = pl.program_id(0)

        @pl.when(i == 0)
        def _():
            for r in (dwuq_ref, dwukv_ref, dqlg_ref, dkvlg_ref, dqng_ref, dkng_ref):
                r[...] = jnp.zeros_like(r)

        latent = i >= off
        ulo = ulo_ref[...]
        cos, sin = _rope_block(cos_ref, sin_ref, pl.program_id(0) < off)
        cq = ulo[:, 0:QL]
        rc = lax.rsqrt(jnp.mean(cq * cq, axis=-1, keepdims=True) + EPS)
        cqh = cq * rc
        qlg = qlg_ref[...]
        cqn_b = (cqh * qlg).astype(BF16)
        qng = qng_ref[...]
        ckv = ulo[:, QL:QL + KVL]
        r0 = lax.rsqrt(jnp.mean(ckv * ckv, axis=-1, keepdims=True) + EPS)
        ckvh = ckv * r0
        kvlg = kvlg_ref[...]
        ckvn_b = (ckvh * kvlg).astype(BF16)
        qhs = [_nt(cqn_b, wuq_ref[hd]) for hd in range(NH)]
        kns = [_nn(ckvn_b, wukv_ref[hd])[:, :128] for hd in range(NH)]
        dqng = jnp.zeros((1, DKP), F32)
        dqraws = []
        for hd in range(NH):
            qh = qhs[hd]
            rq = lax.rsqrt(_rowsum(qh * qh) / DK + EPS)
            xh = qh * rq
            dqh = jnp.where(latent, dq_ref[hd], 0.0)
            dyq = jnp.concatenate([dqh[:, :128], _rope_t(dqh[:, 128:], cos, sin)], axis=1)
            dqng += _colsum(dyq * xh)
            dxh = dyq * qng
            dqraws.append((rq * (dxh - xh * (_rowsum(dxh * xh) / DK))).astype(BF16))
        dqng_ref[...] += dqng

        kr = ulo[:, 384:512]
        skr = _rowsum(kr * kr)
        kng = kng_ref[...]
        dkr = jnp.zeros((TB, 128), F32)
        dkng = jnp.zeros((1, DKP), F32)
        dkvs = []
        for hd in range(NH):
            kn = kns[hd]
            rk = lax.rsqrt((_rowsum(kn * kn) + skr) / DK + EPS)
            xh1 = kn * rk
            xh2 = kr * rk
            dkh = dk_ref[hd] * LN2
            d1 = dkh[:, :128]
            d2 = _rope_t(dkh[:, 128:], cos, sin)
            dkng += jnp.concatenate([_colsum(d1 * xh1), _colsum(d2 * xh2)], axis=1)
            dx1 = d1 * kng[:, :128]
            dx2 = d2 * kng[:, 128:]
            dot = (_rowsum(dx1 * xh1) + _rowsum(dx2 * xh2)) / DK
            dkvs.append(jnp.concatenate([rk * (dx1 - xh1 * dot), dv_ref[hd]], axis=1).astype(BF16))
            dkr += rk * (dx2 - xh2 * dot)
        dkng_ref[...] += dkng

        dcqn = jnp.zeros((TB, QL), F32)
        dckvn = jnp.zeros((TB, KVL), F32)
        for hd in range(NH):
            dwuq_ref[hd] += _tn(dqraws[hd], cqn_b)[:DK]
            dcqn += _nn(dqraws[hd], wuq_ref[hd])
            dwukv_ref[hd] += _tn(ckvn_b, dkvs[hd])
            dckvn += _nt(dkvs[hd], wukv_ref[hd])
        dqlg_ref[...] += _colsum(dcqn * cqh)
        dxh = dcqn * qlg
        dcq = rc * (dxh - cqh * jnp.mean(dxh * cqh, axis=-1, keepdims=True))
        dkvlg_ref[...] += _colsum(dckvn * ckvh)
        dxh = dckvn * kvlg
        dckv = r0 * (dxh - ckvh * jnp.mean(dxh * ckvh, axis=-1, keepdims=True))
        dlo_ref[...] = jnp.concatenate([dcq, dckv, dkr], axis=1).astype(BF16)

    row = lambda w: pl.BlockSpec((TB, w), lambda i: (i, 0))
    heads = lambda w: pl.BlockSpec((NH, TB, w), lambda i: (0, i, 0))
    return pl.pallas_call(
        body, name="qkv_bwd", grid=(nb,),
        in_specs=[row(512), pl.BlockSpec((NH, TB, DKP), lambda i: (0, jnp.maximum(i - off, 0), 0)), heads(DKP), heads(DV),
                  pl.BlockSpec((1, 8, 256), lambda i: (jnp.maximum(i - off, 0), 0, 0)), _full((TB, 256)), _full((1, QL)), _full((NH, DKP, QL)), _full((1, KVL)), _full((NH, KVL, 256)),
                  _full((1, DKP)), _full((1, DKP))],
        out_specs=[row(512), _full((NH, DK, QL)), _full((NH, KVL, 256)), _full((1, QL)), _full((1, KVL)),
                   _full((1, DKP)), _full((1, DKP))],
        out_shape=[jax.ShapeDtypeStruct((t_all, 512), BF16), jax.ShapeDtypeStruct((NH, DK, QL), F32),
                   jax.ShapeDtypeStruct((NH, KVL, 256), F32), jax.ShapeDtypeStruct((1, QL), F32),
                   jax.ShapeDtypeStruct((1, KVL), F32), jax.ShapeDtypeStruct((1, DKP), F32), jax.ShapeDtypeStruct((1, DKP), F32)],
        compiler_params=_params(("arbitrary",)),
    )(u, dq, dk, dv, cos, sin, q_lora_g, w_uq_t, kv_lora_g, w_ukv, qn_g, kn_g)


def _in_bwd(ctx, x, modsel, norm_g, dlo, dga, dgp, dpool, dxn, w_in_t):
    s_len, lc = x.shape[0], ctx.shape[0]
    t_all = s_len + lc
    off = lc // TB
    nb = t_all // TB
    nq = s_len // TB
    hb = TB // HALO
    n = TB + 2 * HALO

    def body(ctx_ref, x_ref, mod_ref, ng_ref, dlo_ref, dga_ref, dgp_ref, dp_ref, dpprev_ref, dpnext_ref, dxn_ref, win_ref,
             gx_ref, dwin_ref, dmod_ref, dng_ref):
        i = pl.program_id(0)
        j = i - off

        @pl.when(i == 0)
        def _():
            dwin_ref[...] = jnp.zeros_like(dwin_ref)
            dmod_ref[...] = jnp.zeros_like(dmod_ref)
            dng_ref[...] = jnp.zeros_like(dng_ref)

        latent = i >= off
        dp = dp_ref[...]
        prev = jnp.where(j <= 0, 0.0, dpprev_ref[...])
        nxt = jnp.where(j >= nq - 1, 0.0, dpnext_ref[...])
        win = jnp.concatenate([prev, dp, nxt], axis=0)
        tg = j * TB - HALO + lax.broadcasted_iota(jnp.int32, (n, 1), 0)
        dpin = []
        for g, w in enumerate(POOL_WINDOWS):
            cnt = jnp.maximum(jnp.minimum(tg + w // 2, s_len) - jnp.maximum(tg - w // 2, 0), 1).astype(F32)
            zq = win[:, g * 128:(g + 1) * 128] / cnt
            zq = zq + _shift_rows(zq, 1)
            for step in (1, 2, 4):
                if w >= 4 * step:
                    zq = _shift_rows(zq, -step) + _shift_rows(zq, step)
            dpin.append(zq[HALO:HALO + TB] - dp[:, g * 128:(g + 1) * 128])
        zero = jnp.zeros((TB, 512), BF16)
        du = [dlo_ref[...], jnp.where(latent, dga_ref[...], zero),
              jnp.where(latent, jnp.concatenate(dpin, axis=1).astype(BF16), zero), jnp.where(latent, dgp_ref[...], zero)]

        ng = ng_ref[...]
        xb = jnp.where(i < off, ctx_ref[...], x_ref[...])
        r, xh, xg, h, scale = _modulated(xb, mod_ref, ng)
        hb_ = h.astype(BF16)
        dh = jnp.zeros((TB, D), F32)
        for s, (lo, hi) in enumerate(SEG):
            dwin_ref[lo:hi, :] += _tn(du[s], hb_)
            dh += _nn(du[s], win_ref[lo:hi, :])
        is_lat = latent.astype(F32)
        dsh = _colsum(dh)
        dsc = _colsum(dh * xg)
        dmod_ref[0, 0:1, :] += dsh * (1.0 - is_lat)
        dmod_ref[0, 1:2, :] += dsc * (1.0 - is_lat)
        dmod_ref[1, 0:1, :] += dsh * is_lat
        dmod_ref[1, 1:2, :] += dsc * is_lat
        dxg = dh * (1.0 + scale)
        dng_ref[...] += _colsum(dxg * xh)
        dxh = dxg * ng
        gx_ref[...] = r * (dxh - xh * jnp.mean(dxh * xh, axis=-1, keepdims=True)) + dxn_ref[...]

    row = lambda w: pl.BlockSpec((TB, w), lambda i: (i, 0))
    lat = lambda w: pl.BlockSpec((TB, w), lambda i: (jnp.maximum(i - off, 0), 0))
    last8 = s_len // HALO - 1
    cspec, xspec, mspec = _token_specs(off)
    return pl.pallas_call(
        body, name="in_bwd", grid=(nb,),
        in_specs=[cspec, xspec, mspec, _full((1, D)), row(512), lat(512), lat(512), lat(512),
                  pl.BlockSpec((HALO, 512), lambda i: (jnp.maximum(jnp.maximum(i - off, 0) * hb - 1, 0), 0)),
                  pl.BlockSpec((HALO, 512), lambda i: (jnp.minimum((jnp.maximum(i - off, 0) + 1) * hb, last8), 0)),
                  lat(D), _full((DIN, D))],
        out_specs=[lat(D), _full((DIN, D)), _full((2, 2, D)), _full((1, D))],
        out_shape=[jax.ShapeDtypeStruct((s_len, D), F32), jax.ShapeDtypeStruct((DIN, D), F32),
                   jax.ShapeDtypeStruct((2, 2, D), F32), jax.ShapeDtypeStruct((1, D), F32)],
        compiler_params=_params(("arbitrary",)),
    )(ctx, x, modsel, norm_g, dlo, dga, dgp, dpool, dpool, dpool, dxn, w_in_t)


def _adamw_update(w_ref, g_ref, m_ref, v_ref, d_ref, mo_ref, vo_ref):
    gv = g_ref[...]
    mn = ADAM_B1 * m_ref[...] + (1.0 - ADAM_B1) * gv
    vn = ADAM_B2 * v_ref[...] + (1.0 - ADAM_B2) * (gv * gv)
    m_hat = mn / (1.0 - ADAM_B1 ** ADAM_STEP)
    v_hat = vn / (1.0 - ADAM_B2 ** ADAM_STEP)
    d_ref[...] = -ADAM_LR * (m_hat / (jnp.sqrt(v_hat) + ADAM_EPS) + ADAM_WD * w_ref[...])
    mo_ref[...] = mn
    vo_ref[...] = vn


def _adamw_many(ws, gs, ms, vs):
    n = len(ws)

    def body(*refs):
        for i in range(n):
            _adamw_update(refs[i], refs[n + i], refs[2 * n + i], refs[3 * n + i], refs[4 * n + i], refs[5 * n + i], refs[6 * n + i])

    def spec(w):
        rows, cols = w.shape
        return pl.BlockSpec((rows // 2, cols), lambda i: (i, 0)) if rows % 16 == 0 else _full((rows, cols))

    specs = [spec(w) for w in ws]
    shp = [jax.ShapeDtypeStruct(w.shape, F32) for w in ws]
    out = pl.pallas_call(body, name="adamw_many", grid=(2,), in_specs=specs * 4, out_specs=specs * 3, out_shape=shp * 3,
                         compiler_params=_params(("arbitrary",)))(*ws, *gs, *ms, *vs)
    return out[:n], out[n:2 * n], out[2 * n:]


def _adamw(w, g, m, v, name):
    rows, cols = w.shape
    rb = next(r for r in range(min(rows, 256), 0, -8) if rows % r == 0)

    def body(w_ref, g_ref, m_ref, v_ref, d_ref, mo_ref, vo_ref):
        _adamw_update(w_ref, g_ref, m_ref, v_ref, d_ref, mo_ref, vo_ref)

    spec = pl.BlockSpec((rb, cols), lambda i: (i, 0))
    shp = jax.ShapeDtypeStruct((rows, cols), F32)
    return pl.pallas_call(
        body, name=name, grid=(rows // rb,), in_specs=[spec] * 4, out_specs=[spec] * 3, out_shape=[shp] * 3,
        compiler_params=_params(("arbitrary",)),
    )(w, g, m, v)


class _Links:
    def __init__(self, send_sems, recv_sems):
        self.send_sems, self.recv_sems, self.sends = send_sems, recv_sems, []

    def send(self, src, dst, sem, to):
        cp = pltpu.make_async_remote_copy(src, dst, self.send_sems.at[sem], self.recv_sems.at[sem], device_id=to,
                                          device_id_type=MESH)
        cp.start()
        self.sends.append(cp)

    def arrived(self, dst, sem, frm):
        pltpu.make_async_remote_copy(dst, dst, self.send_sems.at[sem], self.recv_sems.at[sem], device_id=frm,
                                     device_id_type=MESH).wait_recv()

    def drain(self):
        for cp in self.sends:
            cp.wait_send()


def _half(ref, c, axis):
    size = ref.shape[axis - 2] // 2
    win = pl.ds(pl.multiple_of(c * size, 16 if axis == 0 else 128), size)
    idx = (win, slice(None)) if axis == 0 else (slice(None), win)
    return ref.at[(slice(None),) * (len(ref.shape) - 2) + idx]


def _select_rows(slots_ref, n_slots, row=0):
    sub = lax.broadcasted_iota(jnp.int32, (8, 1), 0)
    out = None
    for d in range(n_slots):
        r = jnp.where(sub == d, jnp.broadcast_to(slots_ref[d][row:row + 1, :], (8, slots_ref.shape[-1])), 0.0)
        out = r if out is None else out + r
    return out


def _gather(c, c_ctx, w_mod, b_mod_k, shards, axes):
    nw = len(shards)
    kw = w_mod.shape[1]

    def body(*refs):
        c_ref, cc_ref, wm_ref, b_ref = refs[:4]
        w_refs = refs[4:4 + nw]
        a16_ref, mod_ref = refs[4 + nw:6 + nw]
        g_refs = refs[6 + nw:6 + 2 * nw]
        a_ref, send_sems, recv_sems, local_sems = refs[6 + 2 * nw:]
        x, y, cc = lax.axis_index("x"), lax.axis_index("y"), lax.axis_index("c")
        me = 4 * x + 2 * y + cc
        k = 2 * x + y
        sibling = (x, y, 1 - cc)
        links = _Links(send_sems, recv_sems)
        chips = [_peer(x, y, cc, off + (0,)) for off in CHIPS3]
        chip_a = ((x + 1 - cc) % 2, (y + cc) % 2, cc)
        chip_b = ((x + cc) % 2, (y + 1 - cc) % 2, cc)
        chip_d = (1 - x, 1 - y, cc)
        locals_ = []
        for wi in range(nw):
            rows, cols = w_refs[wi].shape
            for p in range(LOCAL_PARTS):
                if axes[wi] == 1:
                    part = (slice(None), slice(p * (cols // LOCAL_PARTS), (p + 1) * (cols // LOCAL_PARTS)))
                else:
                    part = (slice(p * (rows // LOCAL_PARTS), (p + 1) * (rows // LOCAL_PARTS)), slice(None))
                lc = pltpu.make_async_copy(w_refs[wi].at[part], g_refs[wi].at[k].at[part], local_sems.at[wi * LOCAL_PARTS + p])
                lc.start()
                locals_.append(lc)
            for j, to in enumerate((chip_a, chip_b)):
                links.send(_half(w_refs[wi], cc, axes[wi]), _half(g_refs[wi].at[k], cc, axes[wi]), 10 + wi * 6 + j, to)
        cv = c_ref[...]
        a_ref[me] = jnp.broadcast_to(cv * _sig(cv), (8, D))
        for j, off in enumerate(PEERS7):
            links.send(a_ref.at[me], a_ref.at[me], j, _peer(x, y, cc, off))
        for j, off in enumerate(PEERS7):
            px, py, pc = _peer(x, y, cc, off)
            links.arrived(a_ref.at[4 * px + 2 * py + pc], j, (px, py, pc))
        ccv = cc_ref[...]
        sub = lax.broadcasted_iota(jnp.int32, (8, 1), 0)
        a16 = jnp.concatenate([_select_rows(a_ref, 8), jnp.where(sub == 0, jnp.broadcast_to(ccv * _sig(ccv), (8, D)), 0.0)], axis=0)
        a16_ref[...] = a16
        mod_ref[k] = _dot3(_nn, a16, wm_ref[...]) + b_ref[...]
        for j, to in enumerate(chips):
            links.send(mod_ref.at[k], mod_ref.at[k], 7 + j, to)
        for j, (frm, origin) in enumerate(((chip_a, chip_a), (chip_b, chip_b), (chip_b, chip_d))):
            for wi in range(nw):
                blk = _half(g_refs[wi].at[2 * origin[0] + origin[1]], cc, axes[wi])
                links.arrived(blk, 10 + wi * 6 + j, frm)
                if j == 0:
                    links.send(blk, blk, 10 + wi * 6 + 2, chip_b)
                links.send(blk, blk, 10 + wi * 6 + 3 + j, sibling)
        for j, (px, py, pc) in enumerate(chips):
            links.arrived(mod_ref.at[2 * px + py], 7 + j, (px, py, pc))
        for j, origin in enumerate((chip_b, chip_a, chip_d)):
            for wi in range(nw):
                links.arrived(_half(g_refs[wi].at[2 * origin[0] + origin[1]], 1 - cc, axes[wi]), 10 + wi * 6 + 3 + j, sibling)
        links.drain()
        for lc in locals_:
            lc.wait()

    nsem = 10 + 6 * nw
    return pl.pallas_call(
        body, name="gather", in_specs=[VM] * 4 + [ANY] * nw, out_specs=[VM, VM] + [ANY] * nw,
        out_shape=[jax.ShapeDtypeStruct((16, D), F32), jax.ShapeDtypeStruct((4, 16, kw), F32)]
        + [jax.ShapeDtypeStruct((4,) + s.shape, s.dtype) for s in shards],
        scratch_shapes=[pltpu.VMEM((8, 8, D), F32), pltpu.SemaphoreType.DMA((nsem,)), pltpu.SemaphoreType.DMA((nsem,)),
                        pltpu.SemaphoreType.DMA((nw * LOCAL_PARTS,))],
        compiler_params=pltpu.CompilerParams(vmem_limit_bytes=VMEM_LIMIT),
    )(c, c_ctx, w_mod, b_mod_k, *shards)


SMALL_ROW_WIDTHS = (D, QL, KVL, DKP, DKP, 512, 128)
SMALL_OUT_WIDTHS = (D, QL, KVL, DK, DK, 512, 1)


def _reduce(grads, axes, smalls, w_pool_g, dmod8, a16, w_mod, c_ctx):
    nw = len(grads)
    ns = len(smalls)
    kw = w_mod.shape[1]
    halves = []
    for g, ax in zip(grads, axes):
        halves.append((g.shape[1] // 2, g.shape[2]) if ax == 0 else (g.shape[1], g.shape[2] // 2))

    def body(*refs):
        g_refs = refs[:nw]
        small_refs = refs[nw:nw + ns]
        wp_ref, dm_ref, a16_ref, wm_ref, cc_ref = refs[nw + ns:nw + ns + 5]
        o = nw + ns + 5
        r_refs = refs[o:o + nw]
        small_outs = refs[o + nw:o + nw + ns]
        rwp_ref, gw_ref, gb_ref, gc_ref = refs[o + nw + ns:o + nw + ns + 4]
        o = o + nw + ns + 4
        own, sib, part, got = (refs[o + i * nw:o + (i + 1) * nw] for i in range(4))
        smbuf, wps, wpg, dm_all, pc_all, send_sems, recv_sems, local_sems = refs[o + 4 * nw:]
        x, y, cc = lax.axis_index("x"), lax.axis_index("y"), lax.axis_index("c")
        me = 4 * x + 2 * y + cc
        k = 2 * x + y
        sibling = (x, y, 1 - cc)
        links = _Links(send_sems, recv_sems)
        chips = [_peer(x, y, cc, off + (0,)) for off in CHIPS3]
        peers = [_peer(x, y, cc, off) for off in PEERS7]
        big, sm0, wp0, dm0, pc0 = 0, 5 * nw, 5 * nw + 7, 5 * nw + 14, 5 * nw + 21

        locals_ = []
        for wi in range(nw):
            lc = pltpu.make_async_copy(_half(g_refs[wi], cc, axes[wi]), own[wi], local_sems.at[wi])
            lc.start()
            locals_.append(lc)
            links.send(_half(g_refs[wi], 1 - cc, axes[wi]), sib[wi], big + wi * 5, sibling)
        slot = smbuf.at[me]
        slot[...] = jnp.zeros((8, D), F32)
        for r, (ref, w) in enumerate(zip(small_refs, SMALL_ROW_WIDTHS)):
            slot[r:r + 1, 0:w] = jnp.broadcast_to(ref[...], (1, w))
        links.send(wp_ref, wps, wp0, sibling)
        dm_all[me] = dm_ref[...]
        for j, peer in enumerate(peers):
            links.send(dm_all.at[me], dm_all.at[me], dm0 + j, peer)
            links.send(smbuf.at[me], smbuf.at[me], sm0 + j, peer)
        links.arrived(wps, wp0, sibling)
        wpg[k] = (wp_ref[...] + wps[...]).astype(BF16)
        for j, to in enumerate(chips):
            links.send(wpg.at[k], wpg.at[k], wp0 + 1 + j, to)
        for wi in range(nw):
            locals_[wi].wait()
            links.arrived(sib[wi], big + wi * 5, sibling)
            part[wi][...] = (own[wi][...] + sib[wi][...]).astype(BF16)
            got[wi][k] = part[wi][k]
            for j, (px, py, pc) in enumerate(chips):
                links.send(part[wi].at[2 * px + py], got[wi].at[k], big + wi * 5 + 1 + j, (px, py, pc))
        for j, (px, py, pc) in enumerate(peers):
            links.arrived(dm_all.at[4 * px + 2 * py + pc], dm0 + j, (px, py, pc))
        dmc = dm_all[0][1:2, :]
        dml = dm_all[0][0:1, :]
        for d in range(1, 8):
            dmc = dmc + dm_all[d][1:2, :]
            dml = dml + dm_all[d][0:1, :]
        gb_ref[...] = dml + dmc
        sub = lax.broadcasted_iota(jnp.int32, (8, 1), 0)
        b16 = jnp.concatenate([_select_rows(dm_all, 8), jnp.where(sub == 0, jnp.broadcast_to(dmc, (8, 3 * D)), 0.0)], axis=0)
        bk = jnp.zeros((16, kw), F32)
        for kk in range(4):
            bk = bk + jnp.where(k == kk, b16[:, kk * kw:(kk + 1) * kw], 0.0)
        gw_ref[...] = _dot3(_tn, a16_ref[...], bk)
        pc_all[k] = _dot3(_nt, jnp.broadcast_to(bk[8:9, :], (8, kw)), wm_ref[...])
        for j, to in enumerate(chips):
            links.send(pc_all.at[k], pc_all.at[k], pc0 + j, to)
        for j, (px, py, pc) in enumerate(peers):
            links.arrived(smbuf.at[4 * px + 2 * py + pc], sm0 + j, (px, py, pc))
        tot = smbuf[0]
        for d in range(1, 8):
            tot = tot + smbuf[d]
        for r, (ref, w) in enumerate(zip(small_outs, SMALL_OUT_WIDTHS)):
            ref[...] = tot[r:r + 1, 0:w]
        for j, (px, py, pc) in enumerate(chips):
            links.arrived(wpg.at[2 * px + py], wp0 + 1 + j, (px, py, pc))
        wpt = wpg[0].astype(F32)
        for kk in range(1, 4):
            wpt = wpt + wpg[kk].astype(F32)
        rwp_ref[...] = wpt
        for wi in range(nw):
            for j, (px, py, pc) in enumerate(chips):
                links.arrived(got[wi].at[2 * px + py], big + wi * 5 + 1 + j, (px, py, pc))
            total = got[wi][0].astype(F32)
            for kk in range(1, 4):
                total = total + got[wi][kk].astype(F32)
            mine = _half(r_refs[wi], cc, axes[wi])
            mine[...] = total
            links.send(mine, mine, big + wi * 5 + 4, sibling)
        for j, (px, py, pc) in enumerate(chips):
            links.arrived(pc_all.at[2 * px + py], pc0 + j, (px, py, pc))
        ccv = cc_ref[...]
        sg = _sig(ccv)
        gc_ref[...] = (pc_all[0][0:1, :] + pc_all[1][0:1, :] + pc_all[2][0:1, :] + pc_all[3][0:1, :]) * (sg * (1.0 + ccv * (1.0 - sg)))
        for wi in range(nw):
            links.arrived(_half(r_refs[wi], 1 - cc, axes[wi]), big + wi * 5 + 4, sibling)
        links.drain()

    nsem = 5 * nw + 24
    quads = [(4,) + h for h in halves]
    return pl.pallas_call(
        body, name="reduce", in_specs=[ANY] * nw + [VM] * (ns + 5), out_specs=[VM] * (nw + ns + 4),
        out_shape=[jax.ShapeDtypeStruct(g.shape[1:], F32) for g in grads]
        + [jax.ShapeDtypeStruct((1, w), F32) for w in SMALL_OUT_WIDTHS]
        + [jax.ShapeDtypeStruct(w_pool_g.shape, F32), jax.ShapeDtypeStruct((D, kw), F32), jax.ShapeDtypeStruct((1, 3 * D), F32),
           jax.ShapeDtypeStruct((1, D), F32)],
        scratch_shapes=[pltpu.VMEM(q, F32) for q in quads] + [pltpu.VMEM(q, F32) for q in quads]
        + [pltpu.VMEM(q, BF16) for q in quads] + [pltpu.VMEM(q, BF16) for q in quads]
        + [pltpu.VMEM((8, 8, D), F32), pltpu.VMEM(w_pool_g.shape, F32), pltpu.VMEM((4,) + w_pool_g.shape, BF16),
           pltpu.VMEM((8, 8, 3 * D), F32),
           pltpu.VMEM((4, 8, D), F32)]
        + [pltpu.SemaphoreType.DMA((nsem,)), pltpu.SemaphoreType.DMA((nsem,)), pltpu.SemaphoreType.DMA((nw,))],
        compiler_params=pltpu.CompilerParams(vmem_limit_bytes=VMEM_LIMIT),
    )(*grads, *smalls, w_pool_g, dmod8, a16, w_mod, c_ctx)


def _rope_tables(s_len):
    rows = s_len // GRID_W
    per = TB // GRID_W
    n_freq = 16
    inv = ROPE_BASE ** (-jnp.arange(n_freq, dtype=F32) / n_freq)
    ang_r = jnp.arange(rows, dtype=F32)[:, None] * inv
    ang_c = jnp.arange(GRID_W, dtype=F32)[:, None] * inv
    by_row, by_col = [], []
    for fn, pad in ((jnp.cos, 1.0), (jnp.sin, 0.0)):
        r = jnp.concatenate([fn(ang_r), fn(ang_r), jnp.zeros((rows, 96), F32)], axis=1).reshape(rows // per, per, 128)
        by_row.append(jnp.pad(r, ((0, 0), (0, 8 - per), (0, 0))))
        cpart = jnp.concatenate([jnp.zeros((GRID_W, 32), F32), fn(ang_c), fn(ang_c), jnp.full((GRID_W, 64), pad, F32)], axis=1)
        by_col.append(jnp.tile(cpart, (per, 1)))
    return jnp.concatenate(by_row, axis=-1), jnp.concatenate(by_col, axis=-1)


def kernel(x, c, ctx, c_ctx, w_mod, b_mod, norm_g, w_in, q_lora_g, w_uq, kv_lora_g, w_ukv, q_norm_g, k_norm_g, w_pool, pool_scale, w_out, loss_target, m_c_ctx, m_w_mod, m_b_mod, m_norm_g, m_w_in, m_q_lora_g, m_w_uq, m_kv_lora_g, m_w_ukv, m_q_norm_g, m_k_norm_g, m_w_pool, m_pool_scale, m_w_out, v_c_ctx, v_w_mod, v_b_mod, v_norm_g, v_w_in, v_q_lora_g, v_w_uq, v_kv_lora_g, v_w_ukv, v_q_norm_g, v_k_norm_g, v_w_pool, v_pool_scale, v_w_out):
    xi, yi, ci = lax.axis_index("x"), lax.axis_index("y"), lax.axis_index("c")
    me = 4 * xi + 2 * yi + ci
    k = 2 * xi + yi
    s_len = x.shape[1]
    lc = ctx.shape[1]
    kw = w_mod.shape[2]
    weights = dict(c_ctx=c_ctx, w_mod=w_mod, b_mod=b_mod, norm_g=norm_g, w_in=w_in, q_lora_g=q_lora_g, w_uq=w_uq,
                   kv_lora_g=kv_lora_g, w_ukv=w_ukv, q_norm_g=q_norm_g, k_norm_g=k_norm_g, w_pool=w_pool,
                   pool_scale=pool_scale, w_out=w_out)
    m_in = dict(c_ctx=m_c_ctx, w_mod=m_w_mod, b_mod=m_b_mod, norm_g=m_norm_g, w_in=m_w_in, q_lora_g=m_q_lora_g, w_uq=m_w_uq,
                kv_lora_g=m_kv_lora_g, w_ukv=m_w_ukv, q_norm_g=m_q_norm_g, k_norm_g=m_k_norm_g, w_pool=m_w_pool,
                pool_scale=m_pool_scale, w_out=m_w_out)
    v_in = dict(c_ctx=v_c_ctx, w_mod=v_w_mod, b_mod=v_b_mod, norm_g=v_norm_g, w_in=v_w_in, q_lora_g=v_q_lora_g, w_uq=v_w_uq,
                kv_lora_g=v_kv_lora_g, w_ukv=v_w_ukv, q_norm_g=v_q_norm_g, k_norm_g=v_k_norm_g, w_pool=v_w_pool,
                pool_scale=v_pool_scale, w_out=v_w_out)
    order = ["c_ctx", "w_mod", "b_mod", "norm_g", "w_in", "q_lora_g", "w_uq", "kv_lora_g", "w_ukv", "q_norm_g", "k_norm_g",
             "w_pool", "pool_scale", "w_out"]
    transposed = ("w_in", "w_uq")
    as2d = lambda n, a: jnp.transpose(a[0]) if n in transposed else a.reshape(-1, a.shape[-1])
    back = lambda n, a: jnp.transpose(a)[None] if n in transposed else a.reshape(weights[n].shape)

    c_ctx2 = c_ctx.reshape(1, D)
    b_mod_k = lax.dynamic_slice(b_mod, (0, k * kw), (1, kw))
    split = (1, 0, 0, 0)
    a16, mod_all, g_in, g_uq, g_ukv, g_out = _gather(
        c, c_ctx2, w_mod[0], b_mod_k,
        [as2d("w_in", w_in).astype(BF16), as2d("w_uq", w_uq).astype(BF16), w_ukv[0].astype(BF16), w_out[0].astype(BF16)], split)
    mod_me = lax.dynamic_index_in_dim(mod_all, me, axis=1, keepdims=False).reshape(3, D)
    mod_c = mod_all[:, 8, :].reshape(3, D)
    modsel = jnp.stack([mod_c, mod_me])
    w_in_t = g_in.reshape(DIN, D)
    w_uq_t = jnp.pad(g_uq, ((0, 0), (0, DKP - DK), (0, 0)))
    w_out_f = g_out.reshape(D, D)
    qn_g = jnp.pad(q_norm_g, ((0, 0), (0, DKP - DK)))
    kn_g = jnp.pad(k_norm_g, ((0, 0), (0, DKP - DK)))
    w_pool_b = w_pool[0].astype(BF16)
    cos, sin = _rope_tables(s_len)

    u, q, kk, v = _fwd_in(ctx[0], x[0], modsel, norm_g, w_in_t, q_lora_g, w_uq_t, kv_lora_g, g_ukv, qn_g, kn_g, cos, sin)
    attn, lse = _attn_fwd(q, kk, v, s_len)
    (dxn, dattn, dga, dgp, dpool, dw_out, dgate, dps, dw_pool, loss) = _out_stage(
        attn.reshape(s_len // Q_BLOCK, Q_BLOCK, NH * DV), u, x[0], loss_target[0], modsel[1, 2:3, :], w_pool_b, pool_scale,
        w_out_f, lc)
    dattn = dattn.reshape(s_len, NH * DV)
    dq, dk, dv = _attn_bwd(q, kk, v, dattn, attn, lse, s_len)
    dlo, dw_uq_t, dw_ukv, dqlg, dkvlg, dqng, dkng = _qkv_bwd(u, dq, dk, dv, cos, sin, q_lora_g, w_uq_t, kv_lora_g, g_ukv,
                                                            qn_g, kn_g, s_len)
    gx, dw_in_t, dmod, dng = _in_bwd(ctx[0], x[0], modsel, norm_g, dlo, dga, dgp, dpool, dxn, w_in_t)

    dmod_l = jnp.concatenate([dmod[1, 0], dmod[1, 1], dgate[0]]).reshape(1, 3 * D)
    dmod_c = jnp.concatenate([dmod[0, 0], dmod[0, 1], jnp.zeros((D,), F32)]).reshape(1, 3 * D)
    dmod8 = jnp.concatenate([dmod_l, dmod_c, jnp.zeros((6, 3 * D), F32)], axis=0)
    (r_in, r_uq, r_ukv, r_out, g_ng, g_qlg, g_kvlg, g_qng, g_kng, g_ps, loss_all, g_wp, g_w_mod, g_b_mod, g_c_ctx) = _reduce(
        [dw_in_t.reshape(4, DIN // 4, D), dw_uq_t, dw_ukv, dw_out.reshape(4, D // 4, D)], split,
        [dng, dqlg, dkvlg, dqng, dkng, dps, loss], dw_pool, dmod8, a16, w_mod[0], c_ctx2)
    g2d = dict(c_ctx=g_c_ctx, b_mod=g_b_mod, w_mod=g_w_mod, w_in=r_in, w_uq=r_uq, w_ukv=r_ukv, w_out=r_out, norm_g=g_ng,
               q_lora_g=g_qlg, kv_lora_g=g_kvlg, q_norm_g=g_qng, k_norm_g=g_kng, pool_scale=g_ps, w_pool=g_wp.reshape(512, 128))

    d2d, m2d, v2d = {}, {}, {}
    d2d["w_mod"], m2d["w_mod"], v2d["w_mod"] = _adamw(as2d("w_mod", w_mod), g2d["w_mod"], as2d("w_mod", m_w_mod),
                                                      as2d("w_mod", v_w_mod), "adamw_w_mod")
    rest = [n for n in order if n != "w_mod"]
    outs = _adamw_many([as2d(n, weights[n]) for n in rest], [g2d[n] for n in rest], [as2d(n, m_in[n]) for n in rest],
                       [as2d(n, v_in[n]) for n in rest])
    for dst, arrs in zip((d2d, m2d, v2d), outs):
        dst.update(dict(zip(rest, arrs)))

    return (loss_all[0, 0], gx[None], *[back(n, g2d[n]) for n in order], *[back(n, d2d[n]) for n in order],
            *[back(n, m2d[n]) for n in order], *[back(n, v2d[n]) for n in order])
```

```python
import jax
import jax.numpy as jnp
from jax import lax
from jax.experimental import pallas as pl
from jax.experimental.pallas import tpu as pltpu

F32 = jnp.float32
BF16 = jnp.bfloat16
MESH = pl.DeviceIdType.MESH

D = 1024
NH = 4
DK = 192
DKP = 256
DV = 128
QL = 256
KVL = 128
DIN = 1984
U_LO = 448
SEG = ((0, 512), (448, 960), (960, 1472), (1472, 1984))
DU = 2048
POOL_WINDOWS = (2, 4, 8, 16)
HALO = 8
EPS = 1e-6
ROPE_BASE = 10000.0
GRID_W = 64
Q_BLOCK = 128
TB = 256
BWD_QBLOCKS = 1
SCALE = DK ** -0.5
LOG2E = 1.4426950408889634
LN2 = 0.6931471805599453
VMEM_LIMIT = 56 * 1024 * 1024

ADAM_LR = 0.001
ADAM_B1 = 0.9
ADAM_B2 = 0.999
ADAM_EPS = 1e-08
ADAM_WD = 0.01
ADAM_STEP = 10

CHIPS3 = ((1, 0), (0, 1), (1, 1))
PEERS7 = tuple((dx, dy, dc) for dx in (0, 1) for dy in (0, 1) for dc in (0, 1) if (dx, dy, dc) != (0, 0, 0))

VM = pl.BlockSpec(memory_space=pltpu.VMEM)
ANY = pl.BlockSpec(memory_space=pl.ANY)


def _nn(a, b):
    return jnp.dot(a, b, preferred_element_type=F32)


def _nt(a, b):
    return lax.dot_general(a, b, (((1,), (1,)), ((), ())), preferred_element_type=F32)


def _tn(a, b):
    return lax.dot_general(a, b, (((0,), (0,)), ((), ())), preferred_element_type=F32)


def _split3(a):
    a0 = a.astype(BF16)
    r = a - a0.astype(F32)
    a1 = r.astype(BF16)
    a2 = (r - a1.astype(F32)).astype(BF16)
    return a0, a1, a2


def _dot3(dot, a, b):
    sa = _split3(a)
    sb = _split3(b)
    out = None
    for i in range(3):
        for j in range(3 - i):
            t = dot(sa[i], sb[j])
            out = t if out is None else out + t
    return out


def _sig(x):
    return 1.0 / (1.0 + jnp.exp(-x))


def _rot(t):
    src = lax.broadcasted_iota(jnp.int32, (128, 128), 0)
    dst = lax.broadcasted_iota(jnp.int32, (128, 128), 1)
    first = (dst % 32) < 16
    perm = jnp.where(first & (src == dst + 16), -1.0, jnp.where(~first & (src == dst - 16), 1.0, 0.0)).astype(BF16)
    hi = t.astype(BF16)
    lo = (t - hi.astype(F32)).astype(BF16)
    return _nn(hi, perm) + _nn(lo, perm)


def _rope(t, cos, sin):
    return t * cos + _rot(t) * sin


def _rope_t(t, cos, sin):
    return t * cos - _rot(t * sin)


def _rope_block(rows_ref, cols_ref, is_ctx):
    lane = lax.broadcasted_iota(jnp.int32, (TB, 256), 1) % 128
    rows = jnp.concatenate([jnp.broadcast_to(rows_ref[0, r:r + 1, :], (GRID_W, 256)) for r in range(TB // GRID_W)], axis=0)
    cs = jnp.where(lane < 32, rows, cols_ref[...])
    return jnp.where(is_ctx, 1.0, cs[:, :128]), jnp.where(is_ctx, 0.0, cs[:, 128:])


def _shift_rows(z, k):
    n = z.shape[0]
    return pltpu.roll(z, (n - k) % n, 0)


def _colsum(a):
    return jnp.sum(a, axis=0, keepdims=True)


def _rowsum(a):
    return jnp.sum(a, axis=-1, keepdims=True)


def _row_layout(col):
    return jnp.transpose(jnp.broadcast_to(col, (col.shape[0], 128)))[0:8, :]


def _params(sem=None):
    return pltpu.CompilerParams(dimension_semantics=sem, vmem_limit_bytes=VMEM_LIMIT)


def _full(shape):
    nd = len(shape)
    return pl.BlockSpec(shape, lambda *_: (0,) * nd)


def _peer(x, y, c, off):
    dx, dy, dc = off
    return ((x + dx) % 2, (y + dy) % 2, (c + dc) % 2)


def _token_specs(off):
    ctx = pl.BlockSpec((TB, D), lambda i: (jnp.minimum(i, off - 1), 0))
    lat = pl.BlockSpec((TB, D), lambda i: (jnp.maximum(i - off, 0), 0))
    mod = pl.BlockSpec((1, 3, D), lambda i: (jnp.minimum(i // off, 1), 0, 0))
    return ctx, lat, mod


def _modulated(x, mod_ref, ng):
    shift = mod_ref[0, 0:1, :]
    scale = mod_ref[0, 1:2, :]
    r = lax.rsqrt(jnp.mean(x * x, axis=-1, keepdims=True) + EPS)
    xh = x * r
    xg = xh * ng
    return r, xh, xg, xg * (1.0 + scale) + shift, scale


def _fwd_in(ctx, x, modsel, norm_g, w_in_t, q_lora_g, w_uq_t, kv_lora_g, w_ukv, qn_g, kn_g, cos, sin):
    s_len, lc = x.shape[0], ctx.shape[0]
    t_all = s_len + lc
    nb = t_all // TB
    off = lc // TB

    def body(ctx_ref, x_ref, mod_ref, ng_ref, win_ref, qlg_ref, wuq_ref, kvlg_ref, wukv_ref, qng_ref, kng_ref, cos_ref, sin_ref,
             u_ref, q_ref, k_ref, v_ref):
        is_ctx = pl.program_id(0) < off
        xb = jnp.where(is_ctx, ctx_ref[...], x_ref[...])
        _, _, _, h, _ = _modulated(xb, mod_ref, ng_ref[...])
        hb = h.astype(BF16)
        lane = lax.broadcasted_iota(jnp.int32, (TB, 512), 1)
        ulo = jnp.where(lane < U_LO, _nt(hb, win_ref[SEG[0][0]:SEG[0][1], :]), 0.0)
        u_ref[:, 0:512] = ulo
        for j in range(1, 4):
            u_ref[:, j * 512:(j + 1) * 512] = _nt(hb, win_ref[SEG[j][0]:SEG[j][1], :])
        cos, sin = _rope_block(cos_ref, sin_ref, is_ctx)
        cq = ulo[:, 0:QL]
        cqn = (cq * lax.rsqrt(jnp.mean(cq * cq, axis=-1, keepdims=True) + EPS) * qlg_ref[...]).astype(BF16)
        qng = qng_ref[...]
        ckv = ulo[:, QL:QL + KVL]
        ckvn = (ckv * lax.rsqrt(jnp.mean(ckv * ckv, axis=-1, keepdims=True) + EPS) * kvlg_ref[...]).astype(BF16)
        qhs = [_nt(cqn, wuq_ref[hd]) for hd in range(NH)]
        kvs = [_nn(ckvn, wukv_ref[hd]) for hd in range(NH)]
        for hd in range(NH):
            qh = qhs[hd]
            qn = qh * lax.rsqrt(_rowsum(qh * qh) / DK + EPS) * qng
            q_ref[hd] = (jnp.concatenate([qn[:, :128], _rope(qn[:, 128:], cos, sin)], axis=1) * (SCALE * LOG2E)).astype(BF16)
        kr = ulo[:, 384:512]
        skr = _rowsum(kr * kr)
        kng = kng_ref[...]
        kr_roped = _rope(kr * kng[:, 128:], cos, sin)
        for hd in range(NH):
            kv = kvs[hd]
            kn = kv[:, :128]
            rk = lax.rsqrt((_rowsum(kn * kn) + skr) / DK + EPS)
            k_ref[hd] = jnp.concatenate([kn * rk * kng[:, :128], kr_roped * rk], axis=1).astype(BF16)
            v_ref[hd] = kv[:, 128:].astype(BF16)

    row = lambda w: pl.BlockSpec((TB, w), lambda i: (i, 0))
    heads = lambda w: pl.BlockSpec((NH, TB, w), lambda i: (0, i, 0))
    cspec, xspec, mspec = _token_specs(off)
    return pl.pallas_call(
        body, name="fwd_in", grid=(nb,),
        in_specs=[cspec, xspec, mspec, _full((1, D)), _full((DIN, D)), _full((1, QL)), _full((NH, DKP, QL)), _full((1, KVL)),
                  _full((NH, KVL, 256)), _full((1, DKP)), _full((1, DKP)),
                  pl.BlockSpec((1, 8, 256), lambda i: (jnp.maximum(i - off, 0), 0, 0)), _full((TB, 256))],
        out_specs=[row(DU), heads(DKP), heads(DKP), heads(DV)],
        out_shape=[jax.ShapeDtypeStruct((t_all, DU), F32), jax.ShapeDtypeStruct((NH, t_all, DKP), BF16),
                   jax.ShapeDtypeStruct((NH, t_all, DKP), BF16), jax.ShapeDtypeStruct((NH, t_all, DV), BF16)],
        compiler_params=_params(("arbitrary",)),
    )(ctx, x, modsel, norm_g, w_in_t, q_lora_g, w_uq_t, kv_lora_g, w_ukv, qn_g, kn_g, cos, sin)


def _attn_fwd(q, k, v, s_len):
    t_all = q.shape[1]
    off = (t_all - s_len) // TB
    nq = s_len // TB
    nsub = next(n for n in (4, 2, 1) if nq % n == 0)

    def body(*refs):
        q_refs = refs[:nsub]
        k_ref, v_ref, o_ref, lse_ref = refs[nsub:]
        for sb in range(nsub):
            s = _nt(q_refs[sb][0], k_ref[0])
            m = jnp.max(s, axis=-1, keepdims=True)
            e = jnp.exp2(s - m)
            l = _rowsum(e)
            o_ref[sb * TB:(sb + 1) * TB, :] = _nn(e.astype(BF16), v_ref[0]) / l
            lse_ref[0, sb] = _row_layout(m + jnp.log2(l))

    qspec = lambda sb: pl.BlockSpec((1, TB, DKP), lambda h, i: (h, i * nsub + sb + off, 0))
    return pl.pallas_call(
        body, name="attn_fwd", grid=(NH, nq // nsub),
        in_specs=[qspec(sb) for sb in range(nsub)]
        + [pl.BlockSpec((1, t_all, DKP), lambda h, i: (h, 0, 0)), pl.BlockSpec((1, t_all, DV), lambda h, i: (h, 0, 0))],
        out_specs=[pl.BlockSpec((nsub * TB, DV), lambda h, i: (i, h)), pl.BlockSpec((1, nsub, 8, TB), lambda h, i: (h, i, 0, 0))],
        out_shape=[jax.ShapeDtypeStruct((s_len, NH * DV), F32), jax.ShapeDtypeStruct((NH, nq, 8, TB), F32)],
        compiler_params=_params(("arbitrary", "arbitrary")),
    )(*([q] * nsub), k, v)


def _out_stage(attn, u, x, target, gate, w_pool, pool_scale, w_out, lc):
    s_len = x.shape[0]
    t_all = s_len + lc
    off = lc // TB
    nq = s_len // TB
    hb = TB // HALO
    nqb = s_len // Q_BLOCK
    jb = TB // nqb

    def body(attn_ref, ga_ref, pin_ref, pprev_ref, pnext_ref, gp_ref, x_ref, tgt_ref, gate_ref, wp_ref, ps_ref, wo_ref,
             dxn_ref, dattn_ref, dga_ref, dgp_ref, dpool_ref, dwo_ref, dgate_ref, dps_ref, dwp_ref, loss_ref):
        i = pl.program_id(0)

        @pl.when(i == 0)
        def _():
            dwo_ref[...] = jnp.zeros_like(dwo_ref)
            dgate_ref[...] = jnp.zeros_like(dgate_ref)
            dps_ref[...] = jnp.zeros_like(dps_ref)
            dwp_ref[...] = jnp.zeros_like(dwp_ref)
            loss_ref[...] = jnp.zeros_like(loss_ref)

        attn = jnp.concatenate([attn_ref[:, jj, :] for jj in range(jb)], axis=0)
        ga = ga_ref[...]
        gp = gp_ref[...]
        pin = pin_ref[...]
        prev = jnp.where(i == 0, 0.0, pprev_ref[...])
        nxt = jnp.where(i == nq - 1, 0.0, pnext_ref[...])
        win = jnp.concatenate([prev, pin, nxt], axis=0)
        tg = i * TB + lax.broadcasted_iota(jnp.int32, (TB, 1), 0)
        pooled = []
        for g, w in enumerate(POOL_WINDOWS):
            a = win[:, g * 128:(g + 1) * 128]
            p = _shift_rows(a, -1) + a
            for step in (1, 2, 4):
                if w >= 4 * step:
                    p = _shift_rows(p, -step) + _shift_rows(p, step)
            cnt = (jnp.minimum(tg + w // 2, s_len) - jnp.maximum(tg - w // 2, 0)).astype(F32)
            pooled.append(p[HALO:HALO + TB] / cnt - a[HALO:HALO + TB])
        pooled_b = [p.astype(BF16) for p in pooled]
        z = jnp.concatenate([_nn(pooled_b[g], wp_ref[g]) for g in range(4)], axis=1)
        ps = ps_ref[...]
        yp = z * ps
        sga = _sig(ga)
        sila = ga * sga
        sgp = _sig(gp)
        silp = gp * sgp
        br = jnp.concatenate([sila * attn, silp * yp], axis=1).astype(BF16)
        y = _nn(br, wo_ref[...])
        gate = gate_ref[...]
        err = x_ref[...] + gate * y - tgt_ref[...]
        loss_ref[...] += _colsum(_rowsum(err * err)) * (0.5 / D)
        dxn = err * (1.0 / D)
        dxn_ref[...] = dxn
        dgate_ref[...] += _colsum(dxn * y)
        dy = (dxn * gate).astype(BF16)
        dwo_ref[...] += _tn(br, dy)
        dbr = _nt(dy, wo_ref[...])
        dbra = dbr[:, :512]
        dbrp = dbr[:, 512:]
        dattn = dbra * sila
        for jj in range(jb):
            dattn_ref[:, jj, :] = dattn[jj * nqb:(jj + 1) * nqb]
        dga_ref[...] = (dbra * attn * (sga * (1.0 + ga * (1.0 - sga)))).astype(BF16)
        dgp_ref[...] = (dbrp * yp * (sgp * (1.0 + gp * (1.0 - sgp)))).astype(BF16)
        dyp = dbrp * silp
        dps_ref[...] += _colsum(dyp * z)
        dz = (dyp * ps).astype(BF16)
        dpool = []
        for g in range(4):
            dzg = dz[:, g * 128:(g + 1) * 128]
            dwp_ref[g] += _tn(pooled_b[g], dzg)
            dpool.append(_nt(dzg, wp_ref[g]))
        dpool_ref[...] = jnp.concatenate(dpool, axis=1)

    lat = lambda w: pl.BlockSpec((TB, w), lambda i: (i, 0))
    perm = pl.BlockSpec((nqb, jb, 512), lambda i: (0, i, 0))
    ucol = lambda j: pl.BlockSpec((TB, 512), lambda i: (i + off, j))
    last8 = t_all // HALO - 1
    return pl.pallas_call(
        body, name="out_stage", grid=(nq,),
        in_specs=[perm, ucol(1), ucol(2),
                  pl.BlockSpec((HALO, 512), lambda i: ((i + off) * hb - 1, 2)),
                  pl.BlockSpec((HALO, 512), lambda i: (jnp.minimum((i + off + 1) * hb, last8), 2)),
                  ucol(3), lat(D), lat(D), _full((1, D)), _full((4, 128, 128)), _full((1, 512)), _full((D, D))],
        out_specs=[lat(D), perm, lat(512), lat(512), lat(512),
                   _full((D, D)), _full((1, D)), _full((1, 512)), _full((4, 128, 128)), _full((1, 1))],
        out_shape=[jax.ShapeDtypeStruct((s_len, D), F32), jax.ShapeDtypeStruct((nqb, Q_BLOCK, 512), F32),
                   jax.ShapeDtypeStruct((s_len, 512), BF16), jax.ShapeDtypeStruct((s_len, 512), BF16),
                   jax.ShapeDtypeStruct((s_len, 512), F32),
                   jax.ShapeDtypeStruct((D, D), F32), jax.ShapeDtypeStruct((1, D), F32), jax.ShapeDtypeStruct((1, 512), F32),
                   jax.ShapeDtypeStruct((4, 128, 128), F32), jax.ShapeDtypeStruct((1, 1), F32)],
        compiler_params=_params(("arbitrary",)),
    )(attn, u, u, u, u, u, x, target, gate, w_pool, pool_scale, w_out)


def _attn_bwd(q, k, v, dattn, attn, lse, s_len):
    t_all = q.shape[1]
    off = (t_all - s_len) // TB
    nq = s_len // TB
    nch = 4
    chunks = [(c * (t_all // nch), t_all // nch) for c in range(nch)]
    nsub = next(n for n in (BWD_QBLOCKS, 2, 1) if nq % n == 0)
    tq = nsub * TB

    def body(*refs):
        q_refs = refs[:nsub]
        k_ref, v_ref, do_ref, o_ref, lse_ref, dq_ref, dk_ref, dv_ref = refs[nsub:]
        i = pl.program_id(1)

        @pl.when(i == 0)
        def _():
            dk_ref[...] = jnp.zeros_like(dk_ref)
            dv_ref[...] = jnp.zeros_like(dv_ref)

        qb = jnp.concatenate([r[0] for r in q_refs], axis=0)
        delta_r = _row_layout(_rowsum(do_ref[...] * o_ref[...]))[0:1, :]
        do = do_ref[...].astype(BF16)
        lse_r = jnp.concatenate([lse_ref[0, sb][0:1, :] for sb in range(nsub)], axis=1)
        dq = jnp.zeros((tq, DKP), F32)
        for start, size in chunks:
            rows = pl.ds(start, size)
            kc = k_ref[0, rows, :]
            p_t = jnp.exp2(_nt(kc, qb) - lse_r)
            ds_t = (p_t * (_nt(v_ref[0, rows, :], do) - delta_r)).astype(BF16)
            dv_ref[0, rows, :] += _nn(p_t.astype(BF16), do)
            dk_ref[0, rows, :] += _nn(ds_t, qb)
            dq += _tn(ds_t, kc)
        dq_ref[0] = dq * SCALE

    kvspec = lambda w: pl.BlockSpec((1, t_all, w), lambda h, i: (h, 0, 0))
    rowspec = pl.BlockSpec((1, nsub, 8, TB), lambda h, i: (h, i, 0, 0))
    qspec = lambda sb: pl.BlockSpec((1, TB, DKP), lambda h, i: (h, i * nsub + sb + off, 0))
    return pl.pallas_call(
        body, name="attn_bwd", grid=(NH, nq // nsub),
        in_specs=[qspec(sb) for sb in range(nsub)]
        + [kvspec(DKP), kvspec(DV), pl.BlockSpec((tq, DV), lambda h, i: (i, h)), pl.BlockSpec((tq, DV), lambda h, i: (i, h)),
           rowspec],
        out_specs=[pl.BlockSpec((1, tq, DKP), lambda h, i: (h, i, 0)), kvspec(DKP), kvspec(DV)],
        out_shape=[jax.ShapeDtypeStruct((NH, s_len, DKP), F32), jax.ShapeDtypeStruct((NH, t_all, DKP), F32),
                   jax.ShapeDtypeStruct((NH, t_all, DV), F32)],
        compiler_params=_params(("arbitrary", "arbitrary")),
    )(*([q] * nsub), k, v, dattn, attn, lse)


def _qkv_bwd(u, dq, dk, dv, cos, sin, q_lora_g, w_uq_t, kv_lora_g, w_ukv, qn_g, kn_g, s_len):
    t_all = u.shape[0]
    off = (t_all - s_len) // TB
    nb = t_all // TB

    def body(ulo_ref, dq_ref, dk_ref, dv_ref, cos_ref, sin_ref, qlg_ref, wuq_ref, kvlg_ref, wukv_ref, qng_ref, kng_ref,
             dlo_ref, dwuq_ref, dwukv_ref, dqlg_ref, dkvlg_ref, dqng_ref, dkng_ref):
        i = pl.program_id(0)

        @pl.when(i == 0)
        def _():
            for r in (dwuq_ref, dwukv_ref, dqlg_ref, dkvlg_ref, dqng_ref, dkng_ref):
                r[...] = jnp.zeros_like(r)

        latent = i >= off
        ulo = ulo_ref[...]
        cos, sin = _rope_block(cos_ref, sin_ref, pl.program_id(0) < off)
        cq = ulo[:, 0:QL]
        rc = lax.rsqrt(jnp.mean(cq * cq, axis=-1, keepdims=True) + EPS)
        cqh = cq * rc
        qlg = qlg_ref[...]
        cqn_b = (cqh * qlg).astype(BF16)
        qng = qng_ref[...]
        ckv = ulo[:, QL:QL + KVL]
        r0 = lax.rsqrt(jnp.mean(ckv * ckv, axis=-1, keepdims=True) + EPS)
        ckvh = ckv * r0
        kvlg = kvlg_ref[...]
        ckvn_b = (ckvh * kvlg).astype(BF16)
        qhs = [_nt(cqn_b, wuq_ref[hd]) for hd in range(NH)]
        kns = [_nn(ckvn_b, wukv_ref[hd])[:, :128] for hd in range(NH)]
        dqng = jnp.zeros((1, DKP), F32)
        dqraws = []
        for hd in range(NH):
            qh = qhs[hd]
            rq = lax.rsqrt(_rowsum(qh * qh) / DK + EPS)
            xh = qh * rq
            dqh = jnp.where(latent, dq_ref[hd], 0.0)
            dyq = jnp.concatenate([dqh[:, :128], _rope_t(dqh[:, 128:], cos, sin)], axis=1)
            dqng += _colsum(dyq * xh)
            dxh = dyq * qng
            dqraws.append((rq * (dxh - xh * (_rowsum(dxh * xh) / DK))).astype(BF16))
        dqng_ref[...] += dqng

        kr = ulo[:, 384:512]
        skr = _rowsum(kr * kr)
        kng = kng_ref[...]
        dkr = jnp.zeros((TB, 128), F32)
        dkng = jnp.zeros((1, DKP), F32)
        dkvs = []
        for hd in range(NH):
            kn = kns[hd]
            rk = lax.rsqrt((_rowsum(kn * kn) + skr) / DK + EPS)
            xh1 = kn * rk
            xh2 = kr * rk
            dkh = dk_ref[hd] * LN2
            d1 = dkh[:, :128]
            d2 = _rope_t(dkh[:, 128:], cos, sin)
            dkng += jnp.concatenate([_colsum(d1 * xh1), _colsum(d2 * xh2)], axis=1)
            dx1 = d1 * kng[:, :128]
            dx2 = d2 * kng[:, 128:]
            dot = (_rowsum(dx1 * xh1) + _rowsum(dx2 * xh2)) / DK
            dkvs.append(jnp.concatenate([rk * (dx1 - xh1 * dot), dv_ref[hd]], axis=1).astype(BF16))
            dkr += rk * (dx2 - xh2 * dot)
        dkng_ref[...] += dkng

        dcqn = jnp.zeros((TB, QL), F32)
        dckvn = jnp.zeros((TB, KVL), F32)
        for hd in range(NH):
            dwuq_ref[hd] += _tn(dqraws[hd], cqn_b)[:DK]
            dcqn += _nn(dqraws[hd], wuq_ref[hd])
            dwukv_ref[hd] += _tn(ckvn_b, dkvs[hd])
            dckvn += _nt(dkvs[hd], wukv_ref[hd])
        dqlg_ref[...] += _colsum(dcqn * cqh)
        dxh = dcqn * qlg
        dcq = rc * (dxh - cqh * jnp.mean(dxh * cqh, axis=-1, keepdims=True))
        dkvlg_ref[...] += _colsum(dckvn * ckvh)
        dxh = dckvn * kvlg
        dckv = r0 * (dxh - ckvh * jnp.mean(dxh * ckvh, axis=-1, keepdims=True))
        dlo_ref[...] = jnp.concatenate([dcq, dckv, dkr], axis=1).astype(BF16)

    row = lambda w: pl.BlockSpec((TB, w), lambda i: (i, 0))
    heads = lambda w: pl.BlockSpec((NH, TB, w), lambda i: (0, i, 0))
    return pl.pallas_call(
        body, name="qkv_bwd", grid=(nb,),
        in_specs=[row(512), pl.BlockSpec((NH, TB, DKP), lambda i: (0, jnp.maximum(i - off, 0), 0)), heads(DKP), heads(DV),
                  pl.BlockSpec((1, 8, 256), lambda i: (jnp.maximum(i - off, 0), 0, 0)), _full((TB, 256)), _full((1, QL)), _full((NH, DKP, QL)), _full((1, KVL)), _full((NH, KVL, 256)),
                  _full((1, DKP)), _full((1, DKP))],
        out_specs=[row(512), _full((NH, DK, QL)), _full((NH, KVL, 256)), _full((1, QL)), _full((1, KVL)),
                   _full((1, DKP)), _full((1, DKP))],
        out_shape=[jax.ShapeDtypeStruct((t_all, 512), BF16), jax.ShapeDtypeStruct((NH, DK, QL), F32),
                   jax.ShapeDtypeStruct((NH, KVL, 256), F32), jax.ShapeDtypeStruct((1, QL), F32),
                   jax.ShapeDtypeStruct((1, KVL), F32), jax.ShapeDtypeStruct((1, DKP), F32), jax.ShapeDtypeStruct((1, DKP), F32)],
        compiler_params=_params(("arbitrary",)),
    )(u, dq, dk, dv, cos, sin, q_lora_g, w_uq_t, kv_lora_g, w_ukv, qn_g, kn_g)


def _in_bwd(ctx, x, modsel, norm_g, dlo, dga, dgp, dpool, dxn, w_in_t):
    s_len, lc = x.shape[0], ctx.shape[0]
    t_all = s_len + lc
    off = lc // TB
    nb = t_all // TB
    nq = s_len // TB
    hb = TB // HALO
    n = TB + 2 * HALO

    def body(ctx_ref, x_ref, mod_ref, ng_ref, dlo_ref, dga_ref, dgp_ref, dp_ref, dpprev_ref, dpnext_ref, dxn_ref, win_ref,
             gx_ref, dwin_ref, dmod_ref, dng_ref):
        i = pl.program_id(0)
        j = i - off

        @pl.when(i == 0)
        def _():
            dwin_ref[...] = jnp.zeros_like(dwin_ref)
            dmod_ref[...] = jnp.zeros_like(dmod_ref)
            dng_ref[...] = jnp.zeros_like(dng_ref)

        latent = i >= off
        dp = dp_ref[...]
        prev = jnp.where(j <= 0, 0.0, dpprev_ref[...])
        nxt = jnp.where(j >= nq - 1, 0.0, dpnext_ref[...])
        win = jnp.concatenate([prev, dp, nxt], axis=0)
        tg = j * TB - HALO + lax.broadcasted_iota(jnp.int32, (n, 1), 0)
        dpin = []
        for g, w in enumerate(POOL_WINDOWS):
            cnt = jnp.maximum(jnp.minimum(tg + w // 2, s_len) - jnp.maximum(tg - w // 2, 0), 1).astype(F32)
            zq = win[:, g * 128:(g + 1) * 128] / cnt
            zq = zq + _shift_rows(zq, 1)
            for step in (1, 2, 4):
                if w >= 4 * step:
                    zq = _shift_rows(zq, -step) + _shift_rows(zq, step)
            dpin.append(zq[HALO:HALO + TB] - dp[:, g * 128:(g + 1) * 128])
        zero = jnp.zeros((TB, 512), BF16)
        du = [dlo_ref[...], jnp.where(latent, dga_ref[...], zero),
              jnp.where(latent, jnp.concatenate(dpin, axis=1).astype(BF16), zero), jnp.where(latent, dgp_ref[...], zero)]

        ng = ng_ref[...]
        xb = jnp.where(i < off, ctx_ref[...], x_ref[...])
        r, xh, xg, h, scale = _modulated(xb, mod_ref, ng)
        hb_ = h.astype(BF16)
        dh = jnp.zeros((TB, D), F32)
        for s, (lo, hi) in enumerate(SEG):
            dwin_ref[lo:hi, :] += _tn(du[s], hb_)
            dh += _nn(du[s], win_ref[lo:hi, :])
        is_lat = latent.astype(F32)
        dsh = _colsum(dh)
        dsc = _colsum(dh * xg)
        dmod_ref[0, 0:1, :] += dsh * (1.0 - is_lat)
        dmod_ref[0, 1:2, :] += dsc * (1.0 - is_lat)
        dmod_ref[1, 0:1, :] += dsh * is_lat
        dmod_ref[1, 1:2, :] += dsc * is_lat
        dxg = dh * (1.0 + scale)
        dng_ref[...] += _colsum(dxg * xh)
        dxh = dxg * ng
        gx_ref[...] = r * (dxh - xh * jnp.mean(dxh * xh, axis=-1, keepdims=True)) + dxn_ref[...]

    row = lambda w: pl.BlockSpec((TB, w), lambda i: (i, 0))
    lat = lambda w: pl.BlockSpec((TB, w), lambda i: (jnp.maximum(i - off, 0), 0))
    last8 = s_len // HALO - 1
    cspec, xspec, mspec = _token_specs(off)
    return pl.pallas_call(
        body, name="in_bwd", grid=(nb,),
        in_specs=[cspec, xspec, mspec, _full((1, D)), row(512), lat(512), lat(512), lat(512),
                  pl.BlockSpec((HALO, 512), lambda i: (jnp.maximum(jnp.maximum(i - off, 0) * hb - 1, 0), 0)),
                  pl.BlockSpec((HALO, 512), lambda i: (jnp.minimum((jnp.maximum(i - off, 0) + 1) * hb, last8), 0)),
                  lat(D), _full((DIN, D))],
        out_specs=[lat(D), _full((DIN, D)), _full((2, 2, D)), _full((1, D))],
        out_shape=[jax.ShapeDtypeStruct((s_len, D), F32), jax.ShapeDtypeStruct((DIN, D), F32),
                   jax.ShapeDtypeStruct((2, 2, D), F32), jax.ShapeDtypeStruct((1, D), F32)],
        compiler_params=_params(("arbitrary",)),
    )(ctx, x, modsel, norm_g, dlo, dga, dgp, dpool, dpool, dpool, dxn, w_in_t)


def _adamw_update(w_ref, g_ref, m_ref, v_ref, d_ref, mo_ref, vo_ref):
    gv = g_ref[...]
    mn = ADAM_B1 * m_ref[...] + (1.0 - ADAM_B1) * gv
    vn = ADAM_B2 * v_ref[...] + (1.0 - ADAM_B2) * (gv * gv)
    m_hat = mn / (1.0 - ADAM_B1 ** ADAM_STEP)
    v_hat = vn / (1.0 - ADAM_B2 ** ADAM_STEP)
    d_ref[...] = -ADAM_LR * (m_hat / (jnp.sqrt(v_hat) + ADAM_EPS) + ADAM_WD * w_ref[...])
    mo_ref[...] = mn
    vo_ref[...] = vn


def _adamw_many(ws, gs, ms, vs):
    n = len(ws)

    def body(*refs):
        for i in range(n):
            _adamw_update(refs[i], refs[n + i], refs[2 * n + i], refs[3 * n + i], refs[4 * n + i], refs[5 * n + i], refs[6 * n + i])

    def spec(w):
        rows, cols = w.shape
        return pl.BlockSpec((rows // 2, cols), lambda i: (i, 0)) if rows % 16 == 0 else _full((rows, cols))

    specs = [spec(w) for w in ws]
    shp = [jax.ShapeDtypeStruct(w.shape, F32) for w in ws]
    out = pl.pallas_call(body, name="adamw_many", grid=(2,), in_specs=specs * 4, out_specs=specs * 3, out_shape=shp * 3,
                         compiler_params=_params(("arbitrary",)))(*ws, *gs, *ms, *vs)
    return out[:n], out[n:2 * n], out[2 * n:]


def _adamw(w, g, m, v, name):
    rows, cols = w.shape
    rb = next(r for r in range(min(rows, 256), 0, -8) if rows % r == 0)

    def body(w_ref, g_ref, m_ref, v_ref, d_ref, mo_ref, vo_ref):
        _adamw_update(w_ref, g_ref, m_ref, v_ref, d_ref, mo_ref, vo_ref)

    spec = pl.BlockSpec((rb, cols), lambda i: (i, 0))
    shp = jax.ShapeDtypeStruct((rows, cols), F32)
    return pl.pallas_call(
        body, name=name, grid=(rows // rb,), in_specs=[spec] * 4, out_specs=[spec] * 3, out_shape=[shp] * 3,
        compiler_params=_params(("arbitrary",)),
    )(w, g, m, v)


class _Links:
    def __init__(self, send_sems, recv_sems):
        self.send_sems, self.recv_sems, self.sends = send_sems, recv_sems, []

    def send(self, src, dst, sem, to):
        cp = pltpu.make_async_remote_copy(src, dst, self.send_sems.at[sem], self.recv_sems.at[sem], device_id=to,
                                          device_id_type=MESH)
        cp.start()
        self.sends.append(cp)

    def arrived(self, dst, sem, frm):
        pltpu.make_async_remote_copy(dst, dst, self.send_sems.at[sem], self.recv_sems.at[sem], device_id=frm,
                                     device_id_type=MESH).wait_recv()

    def drain(self):
        for cp in self.sends:
            cp.wait_send()


def _half(ref, c, axis):
    size = ref.shape[axis - 2] // 2
    win = pl.ds(pl.multiple_of(c * size, 16 if axis == 0 else 128), size)
    idx = (win, slice(None)) if axis == 0 else (slice(None), win)
    return ref.at[(slice(None),) * (len(ref.shape) - 2) + idx]


def _select_rows(slots_ref, n_slots, row=0):
    sub = lax.broadcasted_iota(jnp.int32, (8, 1), 0)
    out = None
    for d in range(n_slots):
        r = jnp.where(sub == d, jnp.broadcast_to(slots_ref[d][row:row + 1, :], (8, slots_ref.shape[-1])), 0.0)
        out = r if out is None else out + r
    return out


def _gather(c, c_ctx, w_mod, b_mod_k, shards, axes):
    nw = len(shards)
    kw = w_mod.shape[1]

    def body(*refs):
        c_ref, cc_ref, wm_ref, b_ref = refs[:4]
        w_refs = refs[4:4 + nw]
        a16_ref, mod_ref = refs[4 + nw:6 + nw]
        g_refs = refs[6 + nw:6 + 2 * nw]
        a_ref, send_sems, recv_sems, local_sems = refs[6 + 2 * nw:]
        x, y, cc = lax.axis_index("x"), lax.axis_index("y"), lax.axis_index("c")
        me = 4 * x + 2 * y + cc
        k = 2 * x + y
        sibling = (x, y, 1 - cc)
        links = _Links(send_sems, recv_sems)
        chips = [_peer(x, y, cc, off + (0,)) for off in CHIPS3]
        chip_a = ((x + 1 - cc) % 2, (y + cc) % 2, cc)
        chip_b = ((x + cc) % 2, (y + 1 - cc) % 2, cc)
        chip_d = (1 - x, 1 - y, cc)
        locals_ = []
        for wi in range(nw):
            lc = pltpu.make_async_copy(w_refs[wi], g_refs[wi].at[k], local_sems.at[wi])
            lc.start()
            locals_.append(lc)
            for j, to in enumerate((chip_a, chip_b)):
                links.send(_half(w_refs[wi], cc, axes[wi]), _half(g_refs[wi].at[k], cc, axes[wi]), 10 + wi * 6 + j, to)
        cv = c_ref[...]
        a_ref[me] = jnp.broadcast_to(cv * _sig(cv), (8, D))
        for j, off in enumerate(PEERS7):
            links.send(a_ref.at[me], a_ref.at[me], j, _peer(x, y, cc, off))
        for j, off in enumerate(PEERS7):
            px, py, pc = _peer(x, y, cc, off)
            links.arrived(a_ref.at[4 * px + 2 * py + pc], j, (px, py, pc))
        ccv = cc_ref[...]
        sub = lax.broadcasted_iota(jnp.int32, (8, 1), 0)
        a16 = jnp.concatenate([_select_rows(a_ref, 8), jnp.where(sub == 0, jnp.broadcast_to(ccv * _sig(ccv), (8, D)), 0.0)], axis=0)
        a16_ref[...] = a16
        mod_ref[k] = _dot3(_nn, a16, wm_ref[...]) + b_ref[...]
        for j, to in enumerate(chips):
            links.send(mod_ref.at[k], mod_ref.at[k], 7 + j, to)
        for j, (frm, origin) in enumerate(((chip_a, chip_a), (chip_b, chip_b), (chip_b, chip_d))):
            for wi in range(nw):
                blk = _half(g_refs[wi].at[2 * origin[0] + origin[1]], cc, axes[wi])
                links.arrived(blk, 10 + wi * 6 + j, frm)
                if j == 0:
                    links.send(blk, blk, 10 + wi * 6 + 2, chip_b)
                links.send(blk, blk, 10 + wi * 6 + 3 + j, sibling)
        for j, (px, py, pc) in enumerate(chips):
            links.arrived(mod_ref.at[2 * px + py], 7 + j, (px, py, pc))
        for j, origin in enumerate((chip_b, chip_a, chip_d)):
            for wi in range(nw):
                links.arrived(_half(g_refs[wi].at[2 * origin[0] + origin[1]], 1 - cc, axes[wi]), 10 + wi * 6 + 3 + j, sibling)
        links.drain()
        for lc in locals_:
            lc.wait()

    nsem = 10 + 6 * nw
    return pl.pallas_call(
        body, name="gather", in_specs=[VM] * (4 + nw), out_specs=[VM] * (2 + nw),
        out_shape=[jax.ShapeDtypeStruct((16, D), F32), jax.ShapeDtypeStruct((4, 16, kw), F32)]
        + [jax.ShapeDtypeStruct((4,) + s.shape, s.dtype) for s in shards],
        scratch_shapes=[pltpu.VMEM((8, 8, D), F32), pltpu.SemaphoreType.DMA((nsem,)), pltpu.SemaphoreType.DMA((nsem,)),
                        pltpu.SemaphoreType.DMA((nw,))],
        compiler_params=pltpu.CompilerParams(vmem_limit_bytes=VMEM_LIMIT),
    )(c, c_ctx, w_mod, b_mod_k, *shards)


SMALL_ROW_WIDTHS = (D, QL, KVL, DKP, DKP, 512, 128)
SMALL_OUT_WIDTHS = (D, QL, KVL, DK, DK, 512, 1)


def _reduce(grads, axes, smalls, w_pool_g, dmod8, a16, w_mod, c_ctx):
    nw = len(grads)
    ns = len(smalls)
    kw = w_mod.shape[1]
    halves = []
    for g, ax in zip(grads, axes):
        halves.append((g.shape[1] // 2, g.shape[2]) if ax == 0 else (g.shape[1], g.shape[2] // 2))

    def body(*refs):
        g_refs = refs[:nw]
        small_refs = refs[nw:nw + ns]
        wp_ref, dm_ref, a16_ref, wm_ref, cc_ref = refs[nw + ns:nw + ns + 5]
        o = nw + ns + 5
        r_refs = refs[o:o + nw]
        small_outs = refs[o + nw:o + nw + ns]
        rwp_ref, gw_ref, gb_ref, gc_ref = refs[o + nw + ns:o + nw + ns + 4]
        o = o + nw + ns + 4
        own, sib, part, got = (refs[o + i * nw:o + (i + 1) * nw] for i in range(4))
        smbuf, wps, wpg, dm_all, pc_all, send_sems, recv_sems, local_sems = refs[o + 4 * nw:]
        x, y, cc = lax.axis_index("x"), lax.axis_index("y"), lax.axis_index("c")
        me = 4 * x + 2 * y + cc
        k = 2 * x + y
        sibling = (x, y, 1 - cc)
        links = _Links(send_sems, recv_sems)
        chips = [_peer(x, y, cc, off + (0,)) for off in CHIPS3]
        peers = [_peer(x, y, cc, off) for off in PEERS7]
        big, sm0, wp0, dm0, pc0 = 0, 5 * nw, 5 * nw + 7, 5 * nw + 14, 5 * nw + 21

        locals_ = []
        for wi in range(nw):
            lc = pltpu.make_async_copy(_half(g_refs[wi], cc, axes[wi]), own[wi], local_sems.at[wi])
            lc.start()
            locals_.append(lc)
            links.send(_half(g_refs[wi], 1 - cc, axes[wi]), sib[wi], big + wi * 5, sibling)
        slot = smbuf.at[me]
        slot[...] = jnp.zeros((8, D), F32)
        for r, (ref, w) in enumerate(zip(small_refs, SMALL_ROW_WIDTHS)):
            slot[r:r + 1, 0:w] = jnp.broadcast_to(ref[...], (1, w))
        links.send(wp_ref, wps, wp0, sibling)
        dm_all[me] = dm_ref[...]
        for j, peer in enumerate(peers):
            links.send(dm_all.at[me], dm_all.at[me], dm0 + j, peer)
            links.send(smbuf.at[me], smbuf.at[me], sm0 + j, peer)
        links.arrived(wps, wp0, sibling)
        wpg[k] = (wp_ref[...] + wps[...]).astype(BF16)
        for j, to in enumerate(chips):
            links.send(wpg.at[k], wpg.at[k], wp0 + 1 + j, to)
        for wi in range(nw):
            locals_[wi].wait()
            links.arrived(sib[wi], big + wi * 5, sibling)
            part[wi][...] = (own[wi][...] + sib[wi][...]).astype(BF16)
            got[wi][k] = part[wi][k]
            for j, (px, py, pc) in enumerate(chips):
                links.send(part[wi].at[2 * px + py], got[wi].at[k], big + wi * 5 + 1 + j, (px, py, pc))
        for j, (px, py, pc) in enumerate(peers):
            links.arrived(dm_all.at[4 * px + 2 * py + pc], dm0 + j, (px, py, pc))
        dmc = dm_all[0][1:2, :]
        dml = dm_all[0][0:1, :]
        for d in range(1, 8):
            dmc = dmc + dm_all[d][1:2, :]
            dml = dml + dm_all[d][0:1, :]
        gb_ref[...] = dml + dmc
        sub = lax.broadcasted_iota(jnp.int32, (8, 1), 0)
        b16 = jnp.concatenate([_select_rows(dm_all, 8), jnp.where(sub == 0, jnp.broadcast_to(dmc, (8, 3 * D)), 0.0)], axis=0)
        bk = jnp.zeros((16, kw), F32)
        for kk in range(4):
            bk = bk + jnp.where(k == kk, b16[:, kk * kw:(kk + 1) * kw], 0.0)
        gw_ref[...] = _dot3(_tn, a16_ref[...], bk)
        pc_all[k] = _dot3(_nt, jnp.broadcast_to(bk[8:9, :], (8, kw)), wm_ref[...])
        for j, to in enumerate(chips):
            links.send(pc_all.at[k], pc_all.at[k], pc0 + j, to)
        for j, (px, py, pc) in enumerate(peers):
            links.arrived(smbuf.at[4 * px + 2 * py + pc], sm0 + j, (px, py, pc))
        tot = smbuf[0]
        for d in range(1, 8):
            tot = tot + smbuf[d]
        for r, (ref, w) in enumerate(zip(small_outs, SMALL_OUT_WIDTHS)):
            ref[...] = tot[r:r + 1, 0:w]
        for j, (px, py, pc) in enumerate(chips):
            links.arrived(wpg.at[2 * px + py], wp0 + 1 + j, (px, py, pc))
        wpt = wpg[0].astype(F32)
        for kk in range(1, 4):
            wpt = wpt + wpg[kk].astype(F32)
        rwp_ref[...] = wpt
        for wi in range(nw):
            for j, (px, py, pc) in enumerate(chips):
                links.arrived(got[wi].at[2 * px + py], big + wi * 5 + 1 + j, (px, py, pc))
            total = got[wi][0].astype(F32)
            for kk in range(1, 4):
                total = total + got[wi][kk].astype(F32)
            mine = _half(r_refs[wi], cc, axes[wi])
            mine[...] = total
            links.send(mine, mine, big + wi * 5 + 4, sibling)
        for j, (px, py, pc) in enumerate(chips):
            links.arrived(pc_all.at[2 * px + py], pc0 + j, (px, py, pc))
        ccv = cc_ref[...]
        sg = _sig(ccv)
        gc_ref[...] = (pc_all[0][0:1, :] + pc_all[1][0:1, :] + pc_all[2][0:1, :] + pc_all[3][0:1, :]) * (sg * (1.0 + ccv * (1.0 - sg)))
        for wi in range(nw):
            links.arrived(_half(r_refs[wi], 1 - cc, axes[wi]), big + wi * 5 + 4, sibling)
        links.drain()

    nsem = 5 * nw + 24
    quads = [(4,) + h for h in halves]
    return pl.pallas_call(
        body, name="reduce", in_specs=[ANY] * nw + [VM] * (ns + 5), out_specs=[VM] * (nw + ns + 4),
        out_shape=[jax.ShapeDtypeStruct(g.shape[1:], F32) for g in grads]
        + [jax.ShapeDtypeStruct((1, w), F32) for w in SMALL_OUT_WIDTHS]
        + [jax.ShapeDtypeStruct(w_pool_g.shape, F32), jax.ShapeDtypeStruct((D, kw), F32), jax.ShapeDtypeStruct((1, 3 * D), F32),
           jax.ShapeDtypeStruct((1, D), F32)],
        scratch_shapes=[pltpu.VMEM(q, F32) for q in quads] + [pltpu.VMEM(q, F32) for q in quads]
        + [pltpu.VMEM(q, BF16) for q in quads] + [pltpu.VMEM(q, BF16) for q in quads]
        + [pltpu.VMEM((8, 8, D), F32), pltpu.VMEM(w_pool_g.shape, F32), pltpu.VMEM((4,) + w_pool_g.shape, BF16),
           pltpu.VMEM((8, 8, 3 * D), F32),
           pltpu.VMEM((4, 8, D), F32)]
        + [pltpu.SemaphoreType.DMA((nsem,)), pltpu.SemaphoreType.DMA((nsem,)), pltpu.SemaphoreType.DMA((nw,))],
        compiler_params=pltpu.CompilerParams(vmem_limit_bytes=VMEM_LIMIT),
    )(*grads, *smalls, w_pool_g, dmod8, a16, w_mod, c_ctx)


def _rope_tables(s_len):
    rows = s_len // GRID_W
    per = TB // GRID_W
    n_freq = 16
    inv = ROPE_BASE ** (-jnp.arange(n_freq, dtype=F32) / n_freq)
    ang_r = jnp.arange(rows, dtype=F32)[:, None] * inv
    ang_c = jnp.arange(GRID_W, dtype=F32)[:, None] * inv
    by_row, by_col = [], []
    for fn, pad in ((jnp.cos, 1.0), (jnp.sin, 0.0)):
        r = jnp.concatenate([fn(ang_r), fn(ang_r), jnp.zeros((rows, 96), F32)], axis=1).reshape(rows // per, per, 128)
        by_row.append(jnp.pad(r, ((0, 0), (0, 8 - per), (0, 0))))
        cpart = jnp.concatenate([jnp.zeros((GRID_W, 32), F32), fn(ang_c), fn(ang_c), jnp.full((GRID_W, 64), pad, F32)], axis=1)
        by_col.append(jnp.tile(cpart, (per, 1)))
    return jnp.concatenate(by_row, axis=-1), jnp.concatenate(by_col, axis=-1)


def kernel(x, c, ctx, c_ctx, w_mod, b_mod, norm_g, w_in, q_lora_g, w_uq, kv_lora_g, w_ukv, q_norm_g, k_norm_g, w_pool, pool_scale, w_out, loss_target, m_c_ctx, m_w_mod, m_b_mod, m_norm_g, m_w_in, m_q_lora_g, m_w_uq, m_kv_lora_g, m_w_ukv, m_q_norm_g, m_k_norm_g, m_w_pool, m_pool_scale, m_w_out, v_c_ctx, v_w_mod, v_b_mod, v_norm_g, v_w_in, v_q_lora_g, v_w_uq, v_kv_lora_g, v_w_ukv, v_q_norm_g, v_k_norm_g, v_w_pool, v_pool_scale, v_w_out):
    xi, yi, ci = lax.axis_index("x"), lax.axis_index("y"), lax.axis_index("c")
    me = 4 * xi + 2 * yi + ci
    k = 2 * xi + yi
    s_len = x.shape[1]
    lc = ctx.shape[1]
    kw = w_mod.shape[2]
    weights = dict(c_ctx=c_ctx, w_mod=w_mod, b_mod=b_mod, norm_g=norm_g, w_in=w_in, q_lora_g=q_lora_g, w_uq=w_uq,
                   kv_lora_g=kv_lora_g, w_ukv=w_ukv, q_norm_g=q_norm_g, k_norm_g=k_norm_g, w_pool=w_pool,
                   pool_scale=pool_scale, w_out=w_out)
    m_in = dict(c_ctx=m_c_ctx, w_mod=m_w_mod, b_mod=m_b_mod, norm_g=m_norm_g, w_in=m_w_in, q_lora_g=m_q_lora_g, w_uq=m_w_uq,
                kv_lora_g=m_kv_lora_g, w_ukv=m_w_ukv, q_norm_g=m_q_norm_g, k_norm_g=m_k_norm_g, w_pool=m_w_pool,
                pool_scale=m_pool_scale, w_out=m_w_out)
    v_in = dict(c_ctx=v_c_ctx, w_mod=v_w_mod, b_mod=v_b_mod, norm_g=v_norm_g, w_in=v_w_in, q_lora_g=v_q_lora_g, w_uq=v_w_uq,
                kv_lora_g=v_kv_lora_g, w_ukv=v_w_ukv, q_norm_g=v_q_norm_g, k_norm_g=v_k_norm_g, w_pool=v_w_pool,
                pool_scale=v_pool_scale, w_out=v_w_out)
    order = ["c_ctx", "w_mod", "b_mod", "norm_g", "w_in", "q_lora_g", "w_uq", "kv_lora_g", "w_ukv", "q_norm_g", "k_norm_g",
             "w_pool", "pool_scale", "w_out"]
    transposed = ("w_in", "w_uq")
    as2d = lambda n, a: jnp.transpose(a[0]) if n in transposed else a.reshape(-1, a.shape[-1])
    back = lambda n, a: jnp.transpose(a)[None] if n in transposed else a.reshape(weights[n].shape)

    c_ctx2 = c_ctx.reshape(1, D)
    b_mod_k = lax.dynamic_slice(b_mod, (0, k * kw), (1, kw))
    split = (1, 0, 0, 0)
    a16, mod_all, g_in, g_uq, g_ukv, g_out = _gather(
        c, c_ctx2, w_mod[0], b_mod_k,
        [as2d("w_in", w_in).astype(BF16), as2d("w_uq", w_uq).astype(BF16), w_ukv[0].astype(BF16), w_out[0].astype(BF16)], split)
    mod_me = lax.dynamic_index_in_dim(mod_all, me, axis=1, keepdims=False).reshape(3, D)
    mod_c = mod_all[:, 8, :].reshape(3, D)
    modsel = jnp.stack([mod_c, mod_me])
    w_in_t = g_in.reshape(DIN, D)
    w_uq_t = jnp.pad(g_uq, ((0, 0), (0, DKP - DK), (0, 0)))
    w_out_f = g_out.reshape(D, D)
    qn_g = jnp.pad(q_norm_g, ((0, 0), (0, DKP - DK)))
    kn_g = jnp.pad(k_norm_g, ((0, 0), (0, DKP - DK)))
    w_pool_b = w_pool[0].astype(BF16)
    cos, sin = _rope_tables(s_len)

    u, q, kk, v = _fwd_in(ctx[0], x[0], modsel, norm_g, w_in_t, q_lora_g, w_uq_t, kv_lora_g, g_ukv, qn_g, kn_g, cos, sin)
    attn, lse = _attn_fwd(q, kk, v, s_len)
    (dxn, dattn, dga, dgp, dpool, dw_out, dgate, dps, dw_pool, loss) = _out_stage(
        attn.reshape(s_len // Q_BLOCK, Q_BLOCK, NH * DV), u, x[0], loss_target[0], modsel[1, 2:3, :], w_pool_b, pool_scale,
        w_out_f, lc)
    dattn = dattn.reshape(s_len, NH * DV)
    dq, dk, dv = _attn_bwd(q, kk, v, dattn, attn, lse, s_len)
    dlo, dw_uq_t, dw_ukv, dqlg, dkvlg, dqng, dkng = _qkv_bwd(u, dq, dk, dv, cos, sin, q_lora_g, w_uq_t, kv_lora_g, g_ukv,
                                                            qn_g, kn_g, s_len)
    gx, dw_in_t, dmod, dng = _in_bwd(ctx[0], x[0], modsel, norm_g, dlo, dga, dgp, dpool, dxn, w_in_t)

    dmod_l = jnp.concatenate([dmod[1, 0], dmod[1, 1], dgate[0]]).reshape(1, 3 * D)
    dmod_c = jnp.concatenate([dmod[0, 0], dmod[0, 1], jnp.zeros((D,), F32)]).reshape(1, 3 * D)
    dmod8 = jnp.concatenate([dmod_l, dmod_c, jnp.zeros((6, 3 * D), F32)], axis=0)
    (r_in, r_uq, r_ukv, r_out, g_ng, g_qlg, g_kvlg, g_qng, g_kng, g_ps, loss_all, g_wp, g_w_mod, g_b_mod, g_c_ctx) = _reduce(
        [dw_in_t.reshape(4, DIN // 4, D), dw_uq_t, dw_ukv, dw_out.reshape(4, D // 4, D)], split,
        [dng, dqlg, dkvlg, dqng, dkng, dps, loss], dw_pool, dmod8, a16, w_mod[0], c_ctx2)
    g2d = dict(c_ctx=g_c_ctx, b_mod=g_b_mod, w_mod=g_w_mod, w_in=r_in, w_uq=r_uq, w_ukv=r_ukv, w_out=r_out, norm_g=g_ng,
               q_lora_g=g_qlg, kv_lora_g=g_kvlg, q_norm_g=g_qng, k_norm_g=g_kng, pool_scale=g_ps, w_pool=g_wp.reshape(512, 128))

    d2d, m2d, v2d = {}, {}, {}
    d2d["w_mod"], m2d["w_mod"], v2d["w_mod"] = _adamw(as2d("w_mod", w_mod), g2d["w_mod"], as2d("w_mod", m_w_mod),
                                                      as2d("w_mod", v_w_mod), "adamw_w_mod")
    rest = [n for n in order if n != "w_mod"]
    outs = _adamw_many([as2d(n, weights[n]) for n in rest], [g2d[n] for n in rest], [as2d(n, m_in[n]) for n in rest],
                       [as2d(n, v_in[n]) for n in rest])
    for dst, arrs in zip((d2d, m2d, v2d), outs):
        dst.update(dict(zip(rest, arrs)))

    return (loss_all[0, 0], gx[None], *[back(n, g2d[n]) for n in order], *[back(n, d2d[n]) for n in order],
            *[back(n, m2d[n]) for n in order], *[back(n, v2d[n]) for n in order])
```

```python
import jax
import jax.numpy as jnp
from jax import lax
from jax.experimental import pallas as pl
from jax.experimental.pallas import tpu as pltpu

F32 = jnp.float32
BF16 = jnp.bfloat16
MESH = pl.DeviceIdType.MESH

D = 1024
NH = 4
DK = 192
DKP = 256
DV = 128
QL = 256
KVL = 128
DIN = 1984
U_LO = 448
SEG = ((0, 512), (448, 960), (960, 1472), (1472, 1984))
DU = 2048
POOL_WINDOWS = (2, 4, 8, 16)
HALO = 8
EPS = 1e-6
ROPE_BASE = 10000.0
GRID_W = 64
Q_BLOCK = 128
TB = 256
BWD_QBLOCKS = 1
SCALE = DK ** -0.5
LOG2E = 1.4426950408889634
LN2 = 0.6931471805599453
VMEM_LIMIT = 56 * 1024 * 1024

ADAM_LR = 0.001
ADAM_B1 = 0.9
ADAM_B2 = 0.999
ADAM_EPS = 1e-08
ADAM_WD = 0.01
ADAM_STEP = 10

CHIPS3 = ((1, 0), (0, 1), (1, 1))
PEERS7 = tuple((dx, dy, dc) for dx in (0, 1) for dy in (0, 1) for dc in (0, 1) if (dx, dy, dc) != (0, 0, 0))

VM = pl.BlockSpec(memory_space=pltpu.VMEM)
ANY = pl.BlockSpec(memory_space=pl.ANY)


def _nn(a, b):
    return jnp.dot(a, b, preferred_element_type=F32)


def _nt(a, b):
    return lax.dot_general(a, b, (((1,), (1,)), ((), ())), preferred_element_type=F32)


def _tn(a, b):
    return lax.dot_general(a, b, (((0,), (0,)), ((), ())), preferred_element_type=F32)


def _split3(a):
    a0 = a.astype(BF16)
    r = a - a0.astype(F32)
    a1 = r.astype(BF16)
    a2 = (r - a1.astype(F32)).astype(BF16)
    return a0, a1, a2


def _dot3(dot, a, b):
    sa = _split3(a)
    sb = _split3(b)
    out = None
    for i in range(3):
        for j in range(3 - i):
            t = dot(sa[i], sb[j])
            out = t if out is None else out + t
    return out


def _sig(x):
    return 1.0 / (1.0 + jnp.exp(-x))


def _rot(t):
    src = lax.broadcasted_iota(jnp.int32, (128, 128), 0)
    dst = lax.broadcasted_iota(jnp.int32, (128, 128), 1)
    first = (dst % 32) < 16
    perm = jnp.where(first & (src == dst + 16), -1.0, jnp.where(~first & (src == dst - 16), 1.0, 0.0)).astype(BF16)
    hi = t.astype(BF16)
    lo = (t - hi.astype(F32)).astype(BF16)
    return _nn(hi, perm) + _nn(lo, perm)


def _rope(t, cos, sin):
    return t * cos + _rot(t) * sin


def _rope_t(t, cos, sin):
    return t * cos - _rot(t * sin)


def _rope_block(rows_ref, cols_ref, is_ctx):
    lane = lax.broadcasted_iota(jnp.int32, (TB, 256), 1) % 128
    rows = jnp.concatenate([jnp.broadcast_to(rows_ref[0, r:r + 1, :], (GRID_W, 256)) for r in range(TB // GRID_W)], axis=0)
    cs = jnp.where(lane < 32, rows, cols_ref[...])
    return jnp.where(is_ctx, 1.0, cs[:, :128]), jnp.where(is_ctx, 0.0, cs[:, 128:])


def _shift_rows(z, k):
    n = z.shape[0]
    return pltpu.roll(z, (n - k) % n, 0)


def _colsum(a):
    return jnp.sum(a, axis=0, keepdims=True)


def _rowsum(a):
    return jnp.sum(a, axis=-1, keepdims=True)


def _row_layout(col):
    return jnp.transpose(jnp.broadcast_to(col, (col.shape[0], 128)))[0:8, :]


def _params(sem=None):
    return pltpu.CompilerParams(dimension_semantics=sem, vmem_limit_bytes=VMEM_LIMIT)


def _full(shape):
    nd = len(shape)
    return pl.BlockSpec(shape, lambda *_: (0,) * nd)


def _peer(x, y, c, off):
    dx, dy, dc = off
    return ((x + dx) % 2, (y + dy) % 2, (c + dc) % 2)


def _token_specs(off):
    ctx = pl.BlockSpec((TB, D), lambda i: (jnp.minimum(i, off - 1), 0))
    lat = pl.BlockSpec((TB, D), lambda i: (jnp.maximum(i - off, 0), 0))
    mod = pl.BlockSpec((1, 3, D), lambda i: (jnp.minimum(i // off, 1), 0, 0))
    return ctx, lat, mod


def _modulated(x, mod_ref, ng):
    shift = mod_ref[0, 0:1, :]
    scale = mod_ref[0, 1:2, :]
    r = lax.rsqrt(jnp.mean(x * x, axis=-1, keepdims=True) + EPS)
    xh = x * r
    xg = xh * ng
    return r, xh, xg, xg * (1.0 + scale) + shift, scale


def _fwd_in(ctx, x, modsel, norm_g, w_in_t, q_lora_g, w_uq_t, kv_lora_g, w_ukv, qn_g, kn_g, cos, sin):
    s_len, lc = x.shape[0], ctx.shape[0]
    t_all = s_len + lc
    nb = t_all // TB
    off = lc // TB

    def body(ctx_ref, x_ref, mod_ref, ng_ref, win_ref, qlg_ref, wuq_ref, kvlg_ref, wukv_ref, qng_ref, kng_ref, cos_ref, sin_ref,
             u_ref, q_ref, k_ref, v_ref):
        is_ctx = pl.program_id(0) < off
        xb = jnp.where(is_ctx, ctx_ref[...], x_ref[...])
        _, _, _, h, _ = _modulated(xb, mod_ref, ng_ref[...])
        hb = h.astype(BF16)
        lane = lax.broadcasted_iota(jnp.int32, (TB, 512), 1)
        ulo = jnp.where(lane < U_LO, _nt(hb, win_ref[SEG[0][0]:SEG[0][1], :]), 0.0)
        u_ref[:, 0:512] = ulo
        for j in range(1, 4):
            u_ref[:, j * 512:(j + 1) * 512] = _nt(hb, win_ref[SEG[j][0]:SEG[j][1], :])
        cos, sin = _rope_block(cos_ref, sin_ref, is_ctx)
        cq = ulo[:, 0:QL]
        cqn = (cq * lax.rsqrt(jnp.mean(cq * cq, axis=-1, keepdims=True) + EPS) * qlg_ref[...]).astype(BF16)
        qng = qng_ref[...]
        ckv = ulo[:, QL:QL + KVL]
        ckvn = (ckv * lax.rsqrt(jnp.mean(ckv * ckv, axis=-1, keepdims=True) + EPS) * kvlg_ref[...]).astype(BF16)
        qhs = [_nt(cqn, wuq_ref[hd]) for hd in range(NH)]
        kvs = [_nn(ckvn, wukv_ref[hd]) for hd in range(NH)]
        for hd in range(NH):
            qh = qhs[hd]
            qn = qh * lax.rsqrt(_rowsum(qh * qh) / DK + EPS) * qng
            q_ref[hd] = (jnp.concatenate([qn[:, :128], _rope(qn[:, 128:], cos, sin)], axis=1) * (SCALE * LOG2E)).astype(BF16)
        kr = ulo[:, 384:512]
        skr = _rowsum(kr * kr)
        kng = kng_ref[...]
        kr_roped = _rope(kr * kng[:, 128:], cos, sin)
        for hd in range(NH):
            kv = kvs[hd]
            kn = kv[:, :128]
            rk = lax.rsqrt((_rowsum(kn * kn) + skr) / DK + EPS)
            k_ref[hd] = jnp.concatenate([kn * rk * kng[:, :128], kr_roped * rk], axis=1).astype(BF16)
            v_ref[hd] = kv[:, 128:].astype(BF16)

    row = lambda w: pl.BlockSpec((TB, w), lambda i: (i, 0))
    heads = lambda w: pl.BlockSpec((NH, TB, w), lambda i: (0, i, 0))
    cspec, xspec, mspec = _token_specs(off)
    return pl.pallas_call(
        body, name="fwd_in", grid=(nb,),
        in_specs=[cspec, xspec, mspec, _full((1, D)), _full((DIN, D)), _full((1, QL)), _full((NH, DKP, QL)), _full((1, KVL)),
                  _full((NH, KVL, 256)), _full((1, DKP)), _full((1, DKP)),
                  pl.BlockSpec((1, 8, 256), lambda i: (jnp.maximum(i - off, 0), 0, 0)), _full((TB, 256))],
        out_specs=[row(DU), heads(DKP), heads(DKP), heads(DV)],
        out_shape=[jax.ShapeDtypeStruct((t_all, DU), F32), jax.ShapeDtypeStruct((NH, t_all, DKP), BF16),
                   jax.ShapeDtypeStruct((NH, t_all, DKP), BF16), jax.ShapeDtypeStruct((NH, t_all, DV), BF16)],
        compiler_params=_params(("arbitrary",)),
    )(ctx, x, modsel, norm_g, w_in_t, q_lora_g, w_uq_t, kv_lora_g, w_ukv, qn_g, kn_g, cos, sin)


def _attn_fwd(q, k, v, s_len):
    t_all = q.shape[1]
    off = (t_all - s_len) // TB
    nq = s_len // TB
    nsub = next(n for n in (4, 2, 1) if nq % n == 0)

    def body(*refs):
        q_refs = refs[:nsub]
        k_ref, v_ref, o_ref, lse_ref = refs[nsub:]
        for sb in range(nsub):
            s = _nt(q_refs[sb][0], k_ref[0])
            m = jnp.max(s, axis=-1, keepdims=True)
            e = jnp.exp2(s - m)
            l = _rowsum(e)
            o_ref[sb * TB:(sb + 1) * TB, :] = _nn(e.astype(BF16), v_ref[0]) / l
            lse_ref[0, sb] = _row_layout(m + jnp.log2(l))

    qspec = lambda sb: pl.BlockSpec((1, TB, DKP), lambda h, i: (h, i * nsub + sb + off, 0))
    return pl.pallas_call(
        body, name="attn_fwd", grid=(NH, nq // nsub),
        in_specs=[qspec(sb) for sb in range(nsub)]
        + [pl.BlockSpec((1, t_all, DKP), lambda h, i: (h, 0, 0)), pl.BlockSpec((1, t_all, DV), lambda h, i: (h, 0, 0))],
        out_specs=[pl.BlockSpec((nsub * TB, DV), lambda h, i: (i, h)), pl.BlockSpec((1, nsub, 8, TB), lambda h, i: (h, i, 0, 0))],
        out_shape=[jax.ShapeDtypeStruct((s_len, NH * DV), F32), jax.ShapeDtypeStruct((NH, nq, 8, TB), F32)],
        compiler_params=_params(("arbitrary", "arbitrary")),
    )(*([q] * nsub), k, v)


def _out_stage(attn, u, x, target, gate, w_pool, pool_scale, w_out, lc):
    s_len = x.shape[0]
    t_all = s_len + lc
    off = lc // TB
    nq = s_len // TB
    hb = TB // HALO
    nqb = s_len // Q_BLOCK
    jb = TB // nqb

    def body(attn_ref, ga_ref, pin_ref, pprev_ref, pnext_ref, gp_ref, x_ref, tgt_ref, gate_ref, wp_ref, ps_ref, wo_ref,
             dxn_ref, dattn_ref, dga_ref, dgp_ref, dpool_ref, dwo_ref, dgate_ref, dps_ref, dwp_ref, loss_ref):
        i = pl.program_id(0)

        @pl.when(i == 0)
        def _():
            dwo_ref[...] = jnp.zeros_like(dwo_ref)
            dgate_ref[...] = jnp.zeros_like(dgate_ref)
            dps_ref[...] = jnp.zeros_like(dps_ref)
            dwp_ref[...] = jnp.zeros_like(dwp_ref)
            loss_ref[...] = jnp.zeros_like(loss_ref)

        attn = jnp.concatenate([attn_ref[:, jj, :] for jj in range(jb)], axis=0)
        ga = ga_ref[...]
        gp = gp_ref[...]
        pin = pin_ref[...]
        prev = jnp.where(i == 0, 0.0, pprev_ref[...])
        nxt = jnp.where(i == nq - 1, 0.0, pnext_ref[...])
        win = jnp.concatenate([prev, pin, nxt], axis=0)
        tg = i * TB + lax.broadcasted_iota(jnp.int32, (TB, 1), 0)
        pooled = []
        for g, w in enumerate(POOL_WINDOWS):
            a = win[:, g * 128:(g + 1) * 128]
            p = _shift_rows(a, -1) + a
            for step in (1, 2, 4):
                if w >= 4 * step:
                    p = _shift_rows(p, -step) + _shift_rows(p, step)
            cnt = (jnp.minimum(tg + w // 2, s_len) - jnp.maximum(tg - w // 2, 0)).astype(F32)
            pooled.append(p[HALO:HALO + TB] / cnt - a[HALO:HALO + TB])
        pooled_b = [p.astype(BF16) for p in pooled]
        z = jnp.concatenate([_nn(pooled_b[g], wp_ref[g]) for g in range(4)], axis=1)
        ps = ps_ref[...]
        yp = z * ps
        sga = _sig(ga)
        sila = ga * sga
        sgp = _sig(gp)
        silp = gp * sgp
        br = jnp.concatenate([sila * attn, silp * yp], axis=1).astype(BF16)
        y = _nn(br, wo_ref[...])
        gate = gate_ref[...]
        err = x_ref[...] + gate * y - tgt_ref[...]
        loss_ref[...] += _colsum(_rowsum(err * err)) * (0.5 / D)
        dxn = err * (1.0 / D)
        dxn_ref[...] = dxn
        dgate_ref[...] += _colsum(dxn * y)
        dy = (dxn * gate).astype(BF16)
        dwo_ref[...] += _tn(br, dy)
        dbr = _nt(dy, wo_ref[...])
        dbra = dbr[:, :512]
        dbrp = dbr[:, 512:]
        dattn = dbra * sila
        for jj in range(jb):
            dattn_ref[:, jj, :] = dattn[jj * nqb:(jj + 1) * nqb]
        dga_ref[...] = (dbra * attn * (sga * (1.0 + ga * (1.0 - sga)))).astype(BF16)
        dgp_ref[...] = (dbrp * yp * (sgp * (1.0 + gp * (1.0 - sgp)))).astype(BF16)
        dyp = dbrp * silp
        dps_ref[...] += _colsum(dyp * z)
        dz = (dyp * ps).astype(BF16)
        dpool = []
        for g in range(4):
            dzg = dz[:, g * 128:(g + 1) * 128]
            dwp_ref[g] += _tn(pooled_b[g], dzg)
            dpool.append(_nt(dzg, wp_ref[g]))
        dpool_ref[...] = jnp.concatenate(dpool, axis=1)

    lat = lambda w: pl.BlockSpec((TB, w), lambda i: (i, 0))
    perm = pl.BlockSpec((nqb, jb, 512), lambda i: (0, i, 0))
    ucol = lambda j: pl.BlockSpec((TB, 512), lambda i: (i + off, j))
    last8 = t_all // HALO - 1
    return pl.pallas_call(
        body, name="out_stage", grid=(nq,),
        in_specs=[perm, ucol(1), ucol(2),
                  pl.BlockSpec((HALO, 512), lambda i: ((i + off) * hb - 1, 2)),
                  pl.BlockSpec((HALO, 512), lambda i: (jnp.minimum((i + off + 1) * hb, last8), 2)),
                  ucol(3), lat(D), lat(D), _full((1, D)), _full((4, 128, 128)), _full((1, 512)), _full((D, D))],
        out_specs=[lat(D), perm, lat(512), lat(512), lat(512),
                   _full((D, D)), _full((1, D)), _full((1, 512)), _full((4, 128, 128)), _full((1, 1))],
        out_shape=[jax.ShapeDtypeStruct((s_len, D), F32), jax.ShapeDtypeStruct((nqb, Q_BLOCK, 512), F32),
                   jax.ShapeDtypeStruct((s_len, 512), BF16), jax.ShapeDtypeStruct((s_len, 512), BF16),
                   jax.ShapeDtypeStruct((s_len, 512), F32),
                   jax.ShapeDtypeStruct((D, D), F32), jax.ShapeDtypeStruct((1, D), F32), jax.ShapeDtypeStruct((1, 512), F32),
                   jax.ShapeDtypeStruct((4, 128, 128), F32), jax.ShapeDtypeStruct((1, 1), F32)],
        compiler_params=_params(("arbitrary",)),
    )(attn, u, u, u, u, u, x, target, gate, w_pool, pool_scale, w_out)


def _attn_bwd(q, k, v, dattn, attn, lse, s_len):
    t_all = q.shape[1]
    off = (t_all - s_len) // TB
    nq = s_len // TB
    nch = 4
    chunks = [(c * (t_all // nch), t_all // nch) for c in range(nch)]
    nsub = next(n for n in (BWD_QBLOCKS, 2, 1) if nq % n == 0)
    tq = nsub * TB

    def body(*refs):
        q_refs = refs[:nsub]
        k_ref, v_ref, do_ref, o_ref, lse_ref, dq_ref, dk_ref, dv_ref = refs[nsub:]
        i = pl.program_id(1)

        @pl.when(i == 0)
        def _():
            dk_ref[...] = jnp.zeros_like(dk_ref)
            dv_ref[...] = jnp.zeros_like(dv_ref)

        qb = jnp.concatenate([r[0] for r in q_refs], axis=0)
        delta_r = _row_layout(_rowsum(do_ref[...] * o_ref[...]))[0:1, :]
        do = do_ref[...].astype(BF16)
        lse_r = jnp.concatenate([lse_ref[0, sb][0:1, :] for sb in range(nsub)], axis=1)
        dq = jnp.zeros((tq, DKP), F32)
        for start, size in chunks:
            rows = pl.ds(start, size)
            kc = k_ref[0, rows, :]
            p_t = jnp.exp2(_nt(kc, qb) - lse_r)
            ds_t = (p_t * (_nt(v_ref[0, rows, :], do) - delta_r)).astype(BF16)
            dv_ref[0, rows, :] += _nn(p_t.astype(BF16), do)
            dk_ref[0, rows, :] += _nn(ds_t, qb)
            dq += _tn(ds_t, kc)
        dq_ref[0] = dq * SCALE

    kvspec = lambda w: pl.BlockSpec((1, t_all, w), lambda h, i: (h, 0, 0))
    rowspec = pl.BlockSpec((1, nsub, 8, TB), lambda h, i: (h, i, 0, 0))
    qspec = lambda sb: pl.BlockSpec((1, TB, DKP), lambda h, i: (h, i * nsub + sb + off, 0))
    return pl.pallas_call(
        body, name="attn_bwd", grid=(NH, nq // nsub),
        in_specs=[qspec(sb) for sb in range(nsub)]
        + [kvspec(DKP), kvspec(DV), pl.BlockSpec((tq, DV), lambda h, i: (i, h)), pl.BlockSpec((tq, DV), lambda h, i: (i, h)),
           rowspec],
        out_specs=[pl.BlockSpec((1, tq, DKP), lambda h, i: (h, i, 0)), kvspec(DKP), kvspec(DV)],
        out_shape=[jax.ShapeDtypeStruct((NH, s_len, DKP), F32), jax.ShapeDtypeStruct((NH, t_all, DKP), F32),
                   jax.ShapeDtypeStruct((NH, t_all, DV), F32)],
        compiler_params=_params(("arbitrary", "arbitrary")),
    )(*([q] * nsub), k, v, dattn, attn, lse)


def _qkv_bwd(u, dq, dk, dv, cos, sin, q_lora_g, w_uq_t, kv_lora_g, w_ukv, qn_g, kn_g, s_len):
    t_all = u.shape[0]
    off = (t_all - s_len) // TB
    nb = t_all // TB

    def body(ulo_ref, dq_ref, dk_ref, dv_ref, cos_ref, sin_ref, qlg_ref, wuq_ref, kvlg_ref, wukv_ref, qng_ref, kng_ref,
             dlo_ref, dwuq_ref, dwukv_ref, dqlg_ref, dkvlg_ref, dqng_ref, dkng_ref):
        i = pl.program_id(0)

        @pl.when(i == 0)
        def _():
            for r in (dwuq_ref, dwukv_ref, dqlg_ref, dkvlg_ref, dqng_ref, dkng_ref):
                r[...] = jnp.zeros_like(r)

        latent = i >= off
        ulo = ulo_ref[...]
        cos, sin = _rope_block(cos_ref, sin_ref, pl.program_id(0) < off)
        cq = ulo[:, 0:QL]
        rc = lax.rsqrt(jnp.mean(cq * cq, axis=-1, keepdims=True) + EPS)
        cqh = cq * rc
        qlg = qlg_ref[...]
        cqn_b = (cqh * qlg).astype(BF16)
        qng = qng_ref[...]
        ckv = ulo[:, QL:QL + KVL]
        r0 = lax.rsqrt(jnp.mean(ckv * ckv, axis=-1, keepdims=True) + EPS)
        ckvh = ckv * r0
        kvlg = kvlg_ref[...]
        ckvn_b = (ckvh * kvlg).astype(BF16)
        qhs = [_nt(cqn_b, wuq_ref[hd]) for hd in range(NH)]
        kns = [_nn(ckvn_b, wukv_ref[hd])[:, :128] for hd in range(NH)]
        dqng = jnp.zeros((1, DKP), F32)
        dqraws = []
        for hd in range(NH):
            qh = qhs[hd]
            rq = lax.rsqrt(_rowsum(qh * qh) / DK + EPS)
            xh = qh * rq
            dqh = jnp.where(latent, dq_ref[hd], 0.0)
            dyq = jnp.concatenate([dqh[:, :128], _rope_t(dqh[:, 128:], cos, sin)], axis=1)
            dqng += _colsum(dyq * xh)
            dxh = dyq * qng
            dqraws.append((rq * (dxh - xh * (_rowsum(dxh * xh) / DK))).astype(BF16))
        dqng_ref[...] += dqng

        kr = ulo[:, 384:512]
        skr = _rowsum(kr * kr)
        kng = kng_ref[...]
        dkr = jnp.zeros((TB, 128), F32)
        dkng = jnp.zeros((1, DKP), F32)
        dkvs = []
        for hd in range(NH):
            kn = kns[hd]
            rk = lax.rsqrt((_rowsum(kn * kn) + skr) / DK + EPS)
            xh1 = kn * rk
            xh2 = kr * rk
            dkh = dk_ref[hd] * LN2
            d1 = dkh[:, :128]
            d2 = _rope_t(dkh[:, 128:], cos, sin)
            dkng += jnp.concatenate([_colsum(d1 * xh1), _colsum(d2 * xh2)], axis=1)
            dx1 = d1 * kng[:, :128]
            dx2 = d2 * kng[:, 128:]
            dot = (_rowsum(dx1 * xh1) + _rowsum(dx2 * xh2)) / DK
            dkvs.append(jnp.concatenate([rk * (dx1 - xh1 * dot), dv_ref[hd]], axis=1).astype(BF16))
            dkr += rk * (dx2 - xh2 * dot)
        dkng_ref[...] += dkng

        dcqn = jnp.zeros((TB, QL), F32)
        dckvn = jnp.zeros((TB, KVL), F32)
        for hd in range(NH):
            dwuq_ref[hd] += _tn(dqraws[hd], cqn_b)[:DK]
            dcqn += _nn(dqraws[hd], wuq_ref[hd])
            dwukv_ref[hd] += _tn(ckvn_b, dkvs[hd])
            dckvn += _nt(dkvs[hd], wukv_ref[hd])
        dqlg_ref[...] += _colsum(dcqn * cqh)
        dxh = dcqn * qlg
        dcq = rc * (dxh - cqh * jnp.mean(dxh * cqh, axis=-1, keepdims=True))
        dkvlg_ref[...] += _colsum(dckvn * ckvh)
        dxh = dckvn * kvlg
        dckv = r0 * (dxh - ckvh * jnp.mean(dxh * ckvh, axis=-1, keepdims=True))
        dlo_ref[...] = jnp.concatenate([dcq, dckv, dkr], axis=1).astype(BF16)

    row = lambda w: pl.BlockSpec((TB, w), lambda i: (i, 0))
    heads = lambda w: pl.BlockSpec((NH, TB, w), lambda i: (0, i, 0))
    return pl.pallas_call(
        body, name="qkv_bwd", grid=(nb,),
        in_specs=[row(512), pl.BlockSpec((NH, TB, DKP), lambda i: (0, jnp.maximum(i - off, 0), 0)), heads(DKP), heads(DV),
                  pl.BlockSpec((1, 8, 256), lambda i: (jnp.maximum(i - off, 0), 0, 0)), _full((TB, 256)), _full((1, QL)), _full((NH, DKP, QL)), _full((1, KVL)), _full((NH, KVL, 256)),
                  _full((1, DKP)), _full((1, DKP))],
        out_specs=[row(512), _full((NH, DK, QL)), _full((NH, KVL, 256)), _full((1, QL)), _full((1, KVL)),
                   _full((1, DKP)), _full((1, DKP))],
        out_shape=[jax.ShapeDtypeStruct((t_all, 512), BF16), jax.ShapeDtypeStruct((NH, DK, QL), F32),
                   jax.ShapeDtypeStruct((NH, KVL, 256), F32), jax.ShapeDtypeStruct((1, QL), F32),
                   jax.ShapeDtypeStruct((1, KVL), F32), jax.ShapeDtypeStruct((1, DKP), F32), jax.ShapeDtypeStruct((1, DKP), F32)],
        compiler_params=_params(("arbitrary",)),
    )(u, dq, dk, dv, cos, sin, q_lora_g, w_uq_t, kv_lora_g, w_ukv, qn_g, kn_g)


def _in_bwd(ctx, x, modsel, norm_g, dlo, dga, dgp, dpool, dxn, w_in_t):
    s_len, lc = x.shape[0], ctx.shape[0]
    t_all = s_len + lc
    off = lc // TB
    nb = t_all // TB
    nq = s_len // TB
    hb = TB // HALO
    n = TB + 2 * HALO

    def body(ctx_ref, x_ref, mod_ref, ng_ref, dlo_ref, dga_ref, dgp_ref, dp_ref, dpprev_ref, dpnext_ref, dxn_ref, win_ref,
             gx_ref, dwin_ref, dmod_ref, dng_ref):
        i = pl.program_id(0)
        j = i - off

        @pl.when(i == 0)
        def _():
            dwin_ref[...] = jnp.zeros_like(dwin_ref)
            dmod_ref[...] = jnp.zeros_like(dmod_ref)
            dng_ref[...] = jnp.zeros_like(dng_ref)

        latent = i >= off
        dp = dp_ref[...]
        prev = jnp.where(j <= 0, 0.0, dpprev_ref[...])
        nxt = jnp.where(j >= nq - 1, 0.0, dpnext_ref[...])
        win = jnp.concatenate([prev, dp, nxt], axis=0)
        tg = j * TB - HALO + lax.broadcasted_iota(jnp.int32, (n, 1), 0)
        dpin = []
        for g, w in enumerate(POOL_WINDOWS):
            cnt = jnp.maximum(jnp.minimum(tg + w // 2, s_len) - jnp.maximum(tg - w // 2, 0), 1).astype(F32)
            zq = win[:, g * 128:(g + 1) * 128] / cnt
            zq = zq + _shift_rows(zq, 1)
            for step in (1, 2, 4):
                if w >= 4 * step:
                    zq = _shift_rows(zq, -step) + _shift_rows(zq, step)
            dpin.append(zq[HALO:HALO + TB] - dp[:, g * 128:(g + 1) * 128])
        zero = jnp.zeros((TB, 512), BF16)
        du = [dlo_ref[...], jnp.where(latent, dga_ref[...], zero),
              jnp.where(latent, jnp.concatenate(dpin, axis=1).astype(BF16), zero), jnp.where(latent, dgp_ref[...], zero)]

        ng = ng_ref[...]
        xb = jnp.where(i < off, ctx_ref[...], x_ref[...])
        r, xh, xg, h, scale = _modulated(xb, mod_ref, ng)
        hb_ = h.astype(BF16)
        dh = jnp.zeros((TB, D), F32)
        for s, (lo, hi) in enumerate(SEG):
            dwin_ref[lo:hi, :] += _tn(du[s], hb_)
            dh += _nn(du[s], win_ref[lo:hi, :])
        is_lat = latent.astype(F32)
        dsh = _colsum(dh)
        dsc = _colsum(dh * xg)
        dmod_ref[0, 0:1, :] += dsh * (1.0 - is_lat)
        dmod_ref[0, 1:2, :] += dsc * (1.0 - is_lat)
        dmod_ref[1, 0:1, :] += dsh * is_lat
        dmod_ref[1, 1:2, :] += dsc * is_lat
        dxg = dh * (1.0 + scale)
        dng_ref[...] += _colsum(dxg * xh)
        dxh = dxg * ng
        gx_ref[...] = r * (dxh - xh * jnp.mean(dxh * xh, axis=-1, keepdims=True)) + dxn_ref[...]

    row = lambda w: pl.BlockSpec((TB, w), lambda i: (i, 0))
    lat = lambda w: pl.BlockSpec((TB, w), lambda i: (jnp.maximum(i - off, 0), 0))
    last8 = s_len // HALO - 1
    cspec, xspec, mspec = _token_specs(off)
    return pl.pallas_call(
        body, name="in_bwd", grid=(nb,),
        in_specs=[cspec, xspec, mspec, _full((1, D)), row(512), lat(512), lat(512), lat(512),
                  pl.BlockSpec((HALO, 512), lambda i: (jnp.maximum(jnp.maximum(i - off, 0) * hb - 1, 0), 0)),
                  pl.BlockSpec((HALO, 512), lambda i: (jnp.minimum((jnp.maximum(i - off, 0) + 1) * hb, last8), 0)),
                  lat(D), _full((DIN, D))],
        out_specs=[lat(D), _full((DIN, D)), _full((2, 2, D)), _full((1, D))],
        out_shape=[jax.ShapeDtypeStruct((s_len, D), F32), jax.ShapeDtypeStruct((DIN, D), F32),
                   jax.ShapeDtypeStruct((2, 2, D), F32), jax.ShapeDtypeStruct((1, D), F32)],
        compiler_params=_params(("arbitrary",)),
    )(ctx, x, modsel, norm_g, dlo, dga, dgp, dpool, dpool, dpool, dxn, w_in_t)


def _adamw_update(w_ref, g_ref, m_ref, v_ref, d_ref, mo_ref, vo_ref):
    gv = g_ref[...]
    mn = ADAM_B1 * m_ref[...] + (1.0 - ADAM_B1) * gv
    vn = ADAM_B2 * v_ref[...] + (1.0 - ADAM_B2) * (gv * gv)
    m_hat = mn / (1.0 - ADAM_B1 ** ADAM_STEP)
    v_hat = vn / (1.0 - ADAM_B2 ** ADAM_STEP)
    d_ref[...] = -ADAM_LR * (m_hat / (jnp.sqrt(v_hat) + ADAM_EPS) + ADAM_WD * w_ref[...])
    mo_ref[...] = mn
    vo_ref[...] = vn


def _adamw_many(ws, gs, ms, vs):
    n = len(ws)

    def body(*refs):
        for i in range(n):
            _adamw_update(refs[i], refs[n + i], refs[2 * n + i], refs[3 * n + i], refs[4 * n + i], refs[5 * n + i], refs[6 * n + i])

    def spec(w):
        rows, cols = w.shape
        return pl.BlockSpec((rows // 2, cols), lambda i: (i, 0)) if rows % 16 == 0 else _full((rows, cols))

    specs = [spec(w) for w in ws]
    shp = [jax.ShapeDtypeStruct(w.shape, F32) for w in ws]
    out = pl.pallas_call(body, name="adamw_many", grid=(2,), in_specs=specs * 4, out_specs=specs * 3, out_shape=shp * 3,
                         compiler_params=_params(("arbitrary",)))(*ws, *gs, *ms, *vs)
    return out[:n], out[n:2 * n], out[2 * n:]


def _adamw(w, g, m, v, name):
    rows, cols = w.shape
    rb = next(r for r in range(min(rows, 256), 0, -8) if rows % r == 0)

    def body(w_ref, g_ref, m_ref, v_ref, d_ref, mo_ref, vo_ref):
        _adamw_update(w_ref, g_ref, m_ref, v_ref, d_ref, mo_ref, vo_ref)

    spec = pl.BlockSpec((rb, cols), lambda i: (i, 0))
    shp = jax.ShapeDtypeStruct((rows, cols), F32)
    return pl.pallas_call(
        body, name=name, grid=(rows // rb,), in_specs=[spec] * 4, out_specs=[spec] * 3, out_shape=[shp] * 3,
        compiler_params=_params(("arbitrary",)),
    )(w, g, m, v)


class _Links:
    def __init__(self, send_sems, recv_sems):
        self.send_sems, self.recv_sems, self.sends = send_sems, recv_sems, []

    def send(self, src, dst, sem, to):
        cp = pltpu.make_async_remote_copy(src, dst, self.send_sems.at[sem], self.recv_sems.at[sem], device_id=to,
                                          device_id_type=MESH)
        cp.start()
        self.sends.append(cp)

    def arrived(self, dst, sem, frm):
        pltpu.make_async_remote_copy(dst, dst, self.send_sems.at[sem], self.recv_sems.at[sem], device_id=frm,
                                     device_id_type=MESH).wait_recv()

    def drain(self):
        for cp in self.sends:
            cp.wait_send()


def _half(ref, c, axis):
    size = ref.shape[axis - 2] // 2
    win = pl.ds(pl.multiple_of(c * size, 16 if axis == 0 else 128), size)
    idx = (win, slice(None)) if axis == 0 else (slice(None), win)
    return ref.at[(slice(None),) * (len(ref.shape) - 2) + idx]


def _select_rows(slots_ref, n_slots, row=0):
    sub = lax.broadcasted_iota(jnp.int32, (8, 1), 0)
    out = None
    for d in range(n_slots):
        r = jnp.where(sub == d, jnp.broadcast_to(slots_ref[d][row:row + 1, :], (8, slots_ref.shape[-1])), 0.0)
        out = r if out is None else out + r
    return out


def _gather(c, c_ctx, w_mod, b_mod_k, shards, axes):
    nw = len(shards)
    kw = w_mod.shape[1]

    def body(*refs):
        c_ref, cc_ref, wm_ref, b_ref = refs[:4]
        w_refs = refs[4:4 + nw]
        a16_ref, mod_ref = refs[4 + nw:6 + nw]
        g_refs = refs[6 + nw:6 + 2 * nw]
        a_ref, send_sems, recv_sems, local_sems = refs[6 + 2 * nw:]
        x, y, cc = lax.axis_index("x"), lax.axis_index("y"), lax.axis_index("c")
        me = 4 * x + 2 * y + cc
        k = 2 * x + y
        sibling = (x, y, 1 - cc)
        links = _Links(send_sems, recv_sems)
        chips = [_peer(x, y, cc, off + (0,)) for off in CHIPS3]
        chip_a = ((x + 1 - cc) % 2, (y + cc) % 2, cc)
        chip_b = ((x + cc) % 2, (y + 1 - cc) % 2, cc)
        chip_d = (1 - x, 1 - y, cc)
        locals_ = []
        for wi in range(nw):
            lc = pltpu.make_async_copy(w_refs[wi], g_refs[wi].at[k], local_sems.at[wi])
            lc.start()
            locals_.append(lc)
            for j, to in enumerate((chip_a, chip_b)):
                links.send(_half(w_refs[wi], cc, axes[wi]), _half(g_refs[wi].at[k], cc, axes[wi]), 10 + wi * 6 + j, to)
        cv = c_ref[...]
        a_ref[me] = jnp.broadcast_to(cv * _sig(cv), (8, D))
        for j, off in enumerate(PEERS7):
            links.send(a_ref.at[me], a_ref.at[me], j, _peer(x, y, cc, off))
        for j, off in enumerate(PEERS7):
            px, py, pc = _peer(x, y, cc, off)
            links.arrived(a_ref.at[4 * px + 2 * py + pc], j, (px, py, pc))
        ccv = cc_ref[...]
        sub = lax.broadcasted_iota(jnp.int32, (8, 1), 0)
        a16 = jnp.concatenate([_select_rows(a_ref, 8), jnp.where(sub == 0, jnp.broadcast_to(ccv * _sig(ccv), (8, D)), 0.0)], axis=0)
        a16_ref[...] = a16
        mod_ref[k] = _dot3(_nn, a16, wm_ref[...]) + b_ref[...]
        for j, to in enumerate(chips):
            links.send(mod_ref.at[k], mod_ref.at[k], 7 + j, to)
        for j, (frm, origin) in enumerate(((chip_a, chip_a), (chip_b, chip_b), (chip_b, chip_d))):
            for wi in range(nw):
                blk = _half(g_refs[wi].at[2 * origin[0] + origin[1]], cc, axes[wi])
                links.arrived(blk, 10 + wi * 6 + j, frm)
                if j == 0:
                    links.send(blk, blk, 10 + wi * 6 + 2, chip_b)
                links.send(blk, blk, 10 + wi * 6 + 3 + j, sibling)
        for j, (px, py, pc) in enumerate(chips):
            links.arrived(mod_ref.at[2 * px + py], 7 + j, (px, py, pc))
        for j, origin in enumerate((chip_b, chip_a, chip_d)):
            for wi in range(nw):
                links.arrived(_half(g_refs[wi].at[2 * origin[0] + origin[1]], 1 - cc, axes[wi]), 10 + wi * 6 + 3 + j, sibling)
        links.drain()
        for lc in locals_:
            lc.wait()

    nsem = 10 + 6 * nw
    return pl.pallas_call(
        body, name="gather", in_specs=[VM] * (4 + nw), out_specs=[VM] * (2 + nw),
        out_shape=[jax.ShapeDtypeStruct((16, D), F32), jax.ShapeDtypeStruct((4, 16, kw), F32)]
        + [jax.ShapeDtypeStruct((4,) + s.shape, s.dtype) for s in shards],
        scratch_shapes=[pltpu.VMEM((8, 8, D), F32), pltpu.SemaphoreType.DMA((nsem,)), pltpu.SemaphoreType.DMA((nsem,)),
                        pltpu.SemaphoreType.DMA((nw,))],
        compiler_params=pltpu.CompilerParams(vmem_limit_bytes=VMEM_LIMIT),
    )(c, c_ctx, w_mod, b_mod_k, *shards)


SMALL_ROW_WIDTHS = (D, QL, KVL, DKP, DKP, 512, 128)
SMALL_OUT_WIDTHS = (D, QL, KVL, DK, DK, 512, 1)


def _reduce(grads, axes, smalls, w_pool_g, dmod8, a16, w_mod, c_ctx):
    nw = len(grads)
    ns = len(smalls)
    kw = w_mod.shape[1]
    halves = []
    for g, ax in zip(grads, axes):
        halves.append((g.shape[1] // 2, g.shape[2]) if ax == 0 else (g.shape[1], g.shape[2] // 2))

    def body(*refs):
        g_refs = refs[:nw]
        small_refs = refs[nw:nw + ns]
        wp_ref, dm_ref, a16_ref, wm_ref, cc_ref = refs[nw + ns:nw + ns + 5]
        o = nw + ns + 5
        r_refs = refs[o:o + nw]
        small_outs = refs[o + nw:o + nw + ns]
        rwp_ref, gw_ref, gb_ref, gc_ref = refs[o + nw + ns:o + nw + ns + 4]
        o = o + nw + ns + 4
        sib, part, got = (refs[o + i * nw:o + (i + 1) * nw] for i in range(3))
        smbuf, wps, wpg, dm_all, pc_all, send_sems, recv_sems = refs[o + 3 * nw:]
        x, y, cc = lax.axis_index("x"), lax.axis_index("y"), lax.axis_index("c")
        me = 4 * x + 2 * y + cc
        k = 2 * x + y
        sibling = (x, y, 1 - cc)
        links = _Links(send_sems, recv_sems)
        chips = [_peer(x, y, cc, off + (0,)) for off in CHIPS3]
        peers = [_peer(x, y, cc, off) for off in PEERS7]
        big, sm0, wp0, dm0, pc0 = 0, 5 * nw, 5 * nw + 7, 5 * nw + 14, 5 * nw + 21

        for wi in range(nw):
            links.send(_half(g_refs[wi], 1 - cc, axes[wi]), sib[wi], big + wi * 5, sibling)
        slot = smbuf.at[me]
        slot[...] = jnp.zeros((8, D), F32)
        for r, (ref, w) in enumerate(zip(small_refs, SMALL_ROW_WIDTHS)):
            slot[r:r + 1, 0:w] = jnp.broadcast_to(ref[...], (1, w))
        links.send(wp_ref, wps, wp0, sibling)
        dm_all[me] = dm_ref[...]
        for j, peer in enumerate(peers):
            links.send(dm_all.at[me], dm_all.at[me], dm0 + j, peer)
            links.send(smbuf.at[me], smbuf.at[me], sm0 + j, peer)
        links.arrived(wps, wp0, sibling)
        wpg[k] = (wp_ref[...] + wps[...]).astype(BF16)
        for j, to in enumerate(chips):
            links.send(wpg.at[k], wpg.at[k], wp0 + 1 + j, to)
        for wi in range(nw):
            links.arrived(sib[wi], big + wi * 5, sibling)
            part[wi][...] = (_half(g_refs[wi], cc, axes[wi])[...] + sib[wi][...]).astype(BF16)
            got[wi][k] = part[wi][k]
            for j, (px, py, pc) in enumerate(chips):
                links.send(part[wi].at[2 * px + py], got[wi].at[k], big + wi * 5 + 1 + j, (px, py, pc))
        for j, (px, py, pc) in enumerate(peers):
            links.arrived(dm_all.at[4 * px + 2 * py + pc], dm0 + j, (px, py, pc))
        dmc = dm_all[0][1:2, :]
        dml = dm_all[0][0:1, :]
        for d in range(1, 8):
            dmc = dmc + dm_all[d][1:2, :]
            dml = dml + dm_all[d][0:1, :]
        gb_ref[...] = dml + dmc
        sub = lax.broadcasted_iota(jnp.int32, (8, 1), 0)
        b16 = jnp.concatenate([_select_rows(dm_all, 8), jnp.where(sub == 0, jnp.broadcast_to(dmc, (8, 3 * D)), 0.0)], axis=0)
        bk = jnp.zeros((16, kw), F32)
        for kk in range(4):
            bk = bk + jnp.where(k == kk, b16[:, kk * kw:(kk + 1) * kw], 0.0)
        gw_ref[...] = _dot3(_tn, a16_ref[...], bk)
        pc_all[k] = _dot3(_nt, jnp.broadcast_to(bk[8:9, :], (8, kw)), wm_ref[...])
        for j, to in enumerate(chips):
            links.send(pc_all.at[k], pc_all.at[k], pc0 + j, to)
        for j, (px, py, pc) in enumerate(peers):
            links.arrived(smbuf.at[4 * px + 2 * py + pc], sm0 + j, (px, py, pc))
        tot = smbuf[0]
        for d in range(1, 8):
            tot = tot + smbuf[d]
        for r, (ref, w) in enumerate(zip(small_outs, SMALL_OUT_WIDTHS)):
            ref[...] = tot[r:r + 1, 0:w]
        for j, (px, py, pc) in enumerate(chips):
            links.arrived(wpg.at[2 * px + py], wp0 + 1 + j, (px, py, pc))
        wpt = wpg[0].astype(F32)
        for kk in range(1, 4):
            wpt = wpt + wpg[kk].astype(F32)
        rwp_ref[...] = wpt
        for wi in range(nw):
            for j, (px, py, pc) in enumerate(chips):
                links.arrived(got[wi].at[2 * px + py], big + wi * 5 + 1 + j, (px, py, pc))
            total = got[wi][0].astype(F32)
            for kk in range(1, 4):
                total = total + got[wi][kk].astype(F32)
            mine = _half(r_refs[wi], cc, axes[wi])
            mine[...] = total
            links.send(mine, mine, big + wi * 5 + 4, sibling)
        for j, (px, py, pc) in enumerate(chips):
            links.arrived(pc_all.at[2 * px + py], pc0 + j, (px, py, pc))
        ccv = cc_ref[...]
        sg = _sig(ccv)
        gc_ref[...] = (pc_all[0][0:1, :] + pc_all[1][0:1, :] + pc_all[2][0:1, :] + pc_all[3][0:1, :]) * (sg * (1.0 + ccv * (1.0 - sg)))
        for wi in range(nw):
            links.arrived(_half(r_refs[wi], 1 - cc, axes[wi]), big + wi * 5 + 4, sibling)
        links.drain()

    nsem = 5 * nw + 24
    quads = [(4,) + h for h in halves]
    return pl.pallas_call(
        body, name="reduce", in_specs=[VM] * (nw + ns + 5), out_specs=[VM] * (nw + ns + 4),
        out_shape=[jax.ShapeDtypeStruct(g.shape[1:], F32) for g in grads]
        + [jax.ShapeDtypeStruct((1, w), F32) for w in SMALL_OUT_WIDTHS]
        + [jax.ShapeDtypeStruct(w_pool_g.shape, F32), jax.ShapeDtypeStruct((D, kw), F32), jax.ShapeDtypeStruct((1, 3 * D), F32),
           jax.ShapeDtypeStruct((1, D), F32)],
        scratch_shapes=[pltpu.VMEM(q, F32) for q in quads] + [pltpu.VMEM(q, BF16) for q in quads] + [pltpu.VMEM(q, BF16) for q in quads]
        + [pltpu.VMEM((8, 8, D), F32), pltpu.VMEM(w_pool_g.shape, F32), pltpu.VMEM((4,) + w_pool_g.shape, BF16),
           pltpu.VMEM((8, 8, 3 * D), F32),
           pltpu.VMEM((4, 8, D), F32)]
        + [pltpu.SemaphoreType.DMA((nsem,)), pltpu.SemaphoreType.DMA((nsem,))],
        compiler_params=pltpu.CompilerParams(vmem_limit_bytes=VMEM_LIMIT),
    )(*grads, *smalls, w_pool_g, dmod8, a16, w_mod, c_ctx)


def _rope_tables(s_len):
    rows = s_len // GRID_W
    per = TB // GRID_W
    n_freq = 16
    inv = ROPE_BASE ** (-jnp.arange(n_freq, dtype=F32) / n_freq)
    ang_r = jnp.arange(rows, dtype=F32)[:, None] * inv
    ang_c = jnp.arange(GRID_W, dtype=F32)[:, None] * inv
    by_row, by_col = [], []
    for fn, pad in ((jnp.cos, 1.0), (jnp.sin, 0.0)):
        r = jnp.concatenate([fn(ang_r), fn(ang_r), jnp.zeros((rows, 96), F32)], axis=1).reshape(rows // per, per, 128)
        by_row.append(jnp.pad(r, ((0, 0), (0, 8 - per), (0, 0))))
        cpart = jnp.concatenate([jnp.zeros((GRID_W, 32), F32), fn(ang_c), fn(ang_c), jnp.full((GRID_W, 64), pad, F32)], axis=1)
        by_col.append(jnp.tile(cpart, (per, 1)))
    return jnp.concatenate(by_row, axis=-1), jnp.concatenate(by_col, axis=-1)


def kernel(x, c, ctx, c_ctx, w_mod, b_mod, norm_g, w_in, q_lora_g, w_uq, kv_lora_g, w_ukv, q_norm_g, k_norm_g, w_pool, pool_scale, w_out, loss_target, m_c_ctx, m_w_mod, m_b_mod, m_norm_g, m_w_in, m_q_lora_g, m_w_uq, m_kv_lora_g, m_w_ukv, m_q_norm_g, m_k_norm_g, m_w_pool, m_pool_scale, m_w_out, v_c_ctx, v_w_mod, v_b_mod, v_norm_g, v_w_in, v_q_lora_g, v_w_uq, v_kv_lora_g, v_w_ukv, v_q_norm_g, v_k_norm_g, v_w_pool, v_pool_scale, v_w_out):
    xi, yi, ci = lax.axis_index("x"), lax.axis_index("y"), lax.axis_index("c")
    me = 4 * xi + 2 * yi + ci
    k = 2 * xi + yi
    s_len = x.shape[1]
    lc = ctx.shape[1]
    kw = w_mod.shape[2]
    weights = dict(c_ctx=c_ctx, w_mod=w_mod, b_mod=b_mod, norm_g=norm_g, w_in=w_in, q_lora_g=q_lora_g, w_uq=w_uq,
                   kv_lora_g=kv_lora_g, w_ukv=w_ukv, q_norm_g=q_norm_g, k_norm_g=k_norm_g, w_pool=w_pool,
                   pool_scale=pool_scale, w_out=w_out)
    m_in = dict(c_ctx=m_c_ctx, w_mod=m_w_mod, b_mod=m_b_mod, norm_g=m_norm_g, w_in=m_w_in, q_lora_g=m_q_lora_g, w_uq=m_w_uq,
                kv_lora_g=m_kv_lora_g, w_ukv=m_w_ukv, q_norm_g=m_q_norm_g, k_norm_g=m_k_norm_g, w_pool=m_w_pool,
                pool_scale=m_pool_scale, w_out=m_w_out)
    v_in = dict(c_ctx=v_c_ctx, w_mod=v_w_mod, b_mod=v_b_mod, norm_g=v_norm_g, w_in=v_w_in, q_lora_g=v_q_lora_g, w_uq=v_w_uq,
                kv_lora_g=v_kv_lora_g, w_ukv=v_w_ukv, q_norm_g=v_q_norm_g, k_norm_g=v_k_norm_g, w_pool=v_w_pool,
                pool_scale=v_pool_scale, w_out=v_w_out)
    order = ["c_ctx", "w_mod", "b_mod", "norm_g", "w_in", "q_lora_g", "w_uq", "kv_lora_g", "w_ukv", "q_norm_g", "k_norm_g",
             "w_pool", "pool_scale", "w_out"]
    transposed = ("w_in", "w_uq")
    as2d = lambda n, a: jnp.transpose(a[0]) if n in transposed else a.reshape(-1, a.shape[-1])
    back = lambda n, a: jnp.transpose(a)[None] if n in transposed else a.reshape(weights[n].shape)

    c_ctx2 = c_ctx.reshape(1, D)
    b_mod_k = lax.dynamic_slice(b_mod, (0, k * kw), (1, kw))
    split = (1, 0, 0, 0)
    a16, mod_all, g_in, g_uq, g_ukv, g_out = _gather(
        c, c_ctx2, w_mod[0], b_mod_k,
        [as2d("w_in", w_in).astype(BF16), as2d("w_uq", w_uq).astype(BF16), w_ukv[0].astype(BF16), w_out[0].astype(BF16)], split)
    mod_me = lax.dynamic_index_in_dim(mod_all, me, axis=1, keepdims=False).reshape(3, D)
    mod_c = mod_all[:, 8, :].reshape(3, D)
    modsel = jnp.stack([mod_c, mod_me])
    w_in_t = g_in.reshape(DIN, D)
    w_uq_t = jnp.pad(g_uq, ((0, 0), (0, DKP - DK), (0, 0)))
    w_out_f = g_out.reshape(D, D)
    qn_g = jnp.pad(q_norm_g, ((0, 0), (0, DKP - DK)))
    kn_g = jnp.pad(k_norm_g, ((0, 0), (0, DKP - DK)))
    w_pool_b = w_pool[0].astype(BF16)
    cos, sin = _rope_tables(s_len)

    u, q, kk, v = _fwd_in(ctx[0], x[0], modsel, norm_g, w_in_t, q_lora_g, w_uq_t, kv_lora_g, g_ukv, qn_g, kn_g, cos, sin)
    attn, lse = _attn_fwd(q, kk, v, s_len)
    (dxn, dattn, dga, dgp, dpool, dw_out, dgate, dps, dw_pool, loss) = _out_stage(
        attn.reshape(s_len // Q_BLOCK, Q_BLOCK, NH * DV), u, x[0], loss_target[0], modsel[1, 2:3, :], w_pool_b, pool_scale,
        w_out_f, lc)
    dattn = dattn.reshape(s_len, NH * DV)
    dq, dk, dv = _attn_bwd(q, kk, v, dattn, attn, lse, s_len)
    dlo, dw_uq_t, dw_ukv, dqlg, dkvlg, dqng, dkng = _qkv_bwd(u, dq, dk, dv, cos, sin, q_lora_g, w_uq_t, kv_lora_g, g_ukv,
                                                            qn_g, kn_g, s_len)
    gx, dw_in_t, dmod, dng = _in_bwd(ctx[0], x[0], modsel, norm_g, dlo, dga, dgp, dpool, dxn, w_in_t)

    dmod_l = jnp.concatenate([dmod[1, 0], dmod[1, 1], dgate[0]]).reshape(1, 3 * D)
    dmod_c = jnp.concatenate([dmod[0, 0], dmod[0, 1], jnp.zeros((D,), F32)]).reshape(1, 3 * D)
    dmod8 = jnp.concatenate([dmod_l, dmod_c, jnp.zeros((6, 3 * D), F32)], axis=0)
    (r_in, r_uq, r_ukv, r_out, g_ng, g_qlg, g_kvlg, g_qng, g_kng, g_ps, loss_all, g_wp, g_w_mod, g_b_mod, g_c_ctx) = _reduce(
        [dw_in_t.reshape(4, DIN // 4, D), dw_uq_t, dw_ukv, dw_out.reshape(4, D // 4, D)], split,
        [dng, dqlg, dkvlg, dqng, dkng, dps, loss], dw_pool, dmod8, a16, w_mod[0], c_ctx2)
    g2d = dict(c_ctx=g_c_ctx, b_mod=g_b_mod, w_mod=g_w_mod, w_in=r_in, w_uq=r_uq, w_ukv=r_ukv, w_out=r_out, norm_g=g_ng,
               q_lora_g=g_qlg, kv_lora_g=g_kvlg, q_norm_g=g_qng, k_norm_g=g_kng, pool_scale=g_ps, w_pool=g_wp.reshape(512, 128))

    d2d, m2d, v2d = {}, {}, {}
    d2d["w_mod"], m2d["w_mod"], v2d["w_mod"] = _adamw(as2d("w_mod", w_mod), g2d["w_mod"], as2d("w_mod", m_w_mod),
                                                      as2d("w_mod", v_w_mod), "adamw_w_mod")
    rest = [n for n in order if n != "w_mod"]
    outs = _adamw_many([as2d(n, weights[n]) for n in rest], [g2d[n] for n in rest], [as2d(n, m_in[n]) for n in rest],
                       [as2d(n, v_in[n]) for n in rest])
    for dst, arrs in zip((d2d, m2d, v2d), outs):
        dst.update(dict(zip(rest, arrs)))

    return (loss_all[0, 0], gx[None], *[back(n, g2d[n]) for n in order], *[back(n, d2d[n]) for n in order],
            *[back(n, m2d[n]) for n in order], *[back(n, v2d[n]) for n in order])
```

```python
import jax
import jax.numpy as jnp
from jax import lax
from jax.experimental import pallas as pl
from jax.experimental.pallas import tpu as pltpu

F32 = jnp.float32
BF16 = jnp.bfloat16
MESH = pl.DeviceIdType.MESH

D = 1024
NH = 4
DK = 192
DKP = 256
DV = 128
QL = 256
KVL = 128
DIN = 1984
U_LO = 448
SEG = ((0, 512), (448, 960), (960, 1472), (1472, 1984))
DU = 2048
POOL_WINDOWS = (2, 4, 8, 16)
HALO = 8
EPS = 1e-6
ROPE_BASE = 10000.0
GRID_W = 64
Q_BLOCK = 128
TB = 256
BWD_QBLOCKS = 1
SCALE = DK ** -0.5
LOG2E = 1.4426950408889634
LN2 = 0.6931471805599453
VMEM_LIMIT = 56 * 1024 * 1024

ADAM_LR = 0.001
ADAM_B1 = 0.9
ADAM_B2 = 0.999
ADAM_EPS = 1e-08
ADAM_WD = 0.01
ADAM_STEP = 10

CHIPS3 = ((1, 0), (0, 1), (1, 1))
PEERS7 = tuple((dx, dy, dc) for dx in (0, 1) for dy in (0, 1) for dc in (0, 1) if (dx, dy, dc) != (0, 0, 0))

VM = pl.BlockSpec(memory_space=pltpu.VMEM)
ANY = pl.BlockSpec(memory_space=pl.ANY)


def _nn(a, b):
    return jnp.dot(a, b, preferred_element_type=F32)


def _nt(a, b):
    return lax.dot_general(a, b, (((1,), (1,)), ((), ())), preferred_element_type=F32)


def _tn(a, b):
    return lax.dot_general(a, b, (((0,), (0,)), ((), ())), preferred_element_type=F32)


def _split3(a):
    a0 = a.astype(BF16)
    r = a - a0.astype(F32)
    a1 = r.astype(BF16)
    a2 = (r - a1.astype(F32)).astype(BF16)
    return a0, a1, a2


def _dot3(dot, a, b):
    sa = _split3(a)
    sb = _split3(b)
    out = None
    for i in range(3):
        for j in range(3 - i):
            t = dot(sa[i], sb[j])
            out = t if out is None else out + t
    return out


def _sig(x):
    return 1.0 / (1.0 + jnp.exp(-x))


def _rot(t):
    src = lax.broadcasted_iota(jnp.int32, (128, 128), 0)
    dst = lax.broadcasted_iota(jnp.int32, (128, 128), 1)
    first = (dst % 32) < 16
    perm = jnp.where(first & (src == dst + 16), -1.0, jnp.where(~first & (src == dst - 16), 1.0, 0.0)).astype(BF16)
    hi = t.astype(BF16)
    lo = (t - hi.astype(F32)).astype(BF16)
    return _nn(hi, perm) + _nn(lo, perm)


def _rope(t, cos, sin):
    return t * cos + _rot(t) * sin


def _rope_t(t, cos, sin):
    return t * cos - _rot(t * sin)


def _rope_block(rows_ref, cols_ref, is_ctx):
    lane = lax.broadcasted_iota(jnp.int32, (TB, 256), 1) % 128
    rows = jnp.concatenate([jnp.broadcast_to(rows_ref[0, r:r + 1, :], (GRID_W, 256)) for r in range(TB // GRID_W)], axis=0)
    cs = jnp.where(lane < 32, rows, cols_ref[...])
    return jnp.where(is_ctx, 1.0, cs[:, :128]), jnp.where(is_ctx, 0.0, cs[:, 128:])


def _shift_rows(z, k):
    n = z.shape[0]
    return pltpu.roll(z, (n - k) % n, 0)


def _colsum(a):
    return jnp.sum(a, axis=0, keepdims=True)


def _rowsum(a):
    return jnp.sum(a, axis=-1, keepdims=True)


def _row_layout(col):
    return jnp.transpose(jnp.broadcast_to(col, (col.shape[0], 128)))[0:8, :]


def _params(sem=None):
    return pltpu.CompilerParams(dimension_semantics=sem, vmem_limit_bytes=VMEM_LIMIT)


def _full(shape):
    nd = len(shape)
    return pl.BlockSpec(shape, lambda *_: (0,) * nd)


def _peer(x, y, c, off):
    dx, dy, dc = off
    return ((x + dx) % 2, (y + dy) % 2, (c + dc) % 2)


def _token_specs(off):
    ctx = pl.BlockSpec((TB, D), lambda i: (jnp.minimum(i, off - 1), 0))
    lat = pl.BlockSpec((TB, D), lambda i: (jnp.maximum(i - off, 0), 0))
    mod = pl.BlockSpec((1, 3, D), lambda i: (jnp.minimum(i // off, 1), 0, 0))
    return ctx, lat, mod


def _modulated(x, mod_ref, ng):
    shift = mod_ref[0, 0:1, :]
    scale = mod_ref[0, 1:2, :]
    r = lax.rsqrt(jnp.mean(x * x, axis=-1, keepdims=True) + EPS)
    xh = x * r
    xg = xh * ng
    return r, xh, xg, xg * (1.0 + scale) + shift, scale


def _fwd_in(ctx, x, modsel, norm_g, w_in_t, q_lora_g, w_uq_t, kv_lora_g, w_ukv, qn_g, kn_g, cos, sin):
    s_len, lc = x.shape[0], ctx.shape[0]
    t_all = s_len + lc
    nb = t_all // TB
    off = lc // TB

    def body(ctx_ref, x_ref, mod_ref, ng_ref, win_ref, qlg_ref, wuq_ref, kvlg_ref, wukv_ref, qng_ref, kng_ref, cos_ref, sin_ref,
             u_ref, q_ref, k_ref, v_ref):
        is_ctx = pl.program_id(0) < off
        xb = jnp.where(is_ctx, ctx_ref[...], x_ref[...])
        _, _, _, h, _ = _modulated(xb, mod_ref, ng_ref[...])
        hb = h.astype(BF16)
        lane = lax.broadcasted_iota(jnp.int32, (TB, 512), 1)
        ulo = jnp.where(lane < U_LO, _nt(hb, win_ref[SEG[0][0]:SEG[0][1], :]), 0.0)
        u_ref[:, 0:512] = ulo
        for j in range(1, 4):
            u_ref[:, j * 512:(j + 1) * 512] = _nt(hb, win_ref[SEG[j][0]:SEG[j][1], :])
        cos, sin = _rope_block(cos_ref, sin_ref, is_ctx)
        cq = ulo[:, 0:QL]
        cqn = (cq * lax.rsqrt(jnp.mean(cq * cq, axis=-1, keepdims=True) + EPS) * qlg_ref[...]).astype(BF16)
        qng = qng_ref[...]
        ckv = ulo[:, QL:QL + KVL]
        ckvn = (ckv * lax.rsqrt(jnp.mean(ckv * ckv, axis=-1, keepdims=True) + EPS) * kvlg_ref[...]).astype(BF16)
        qhs = [_nt(cqn, wuq_ref[hd]) for hd in range(NH)]
        kvs = [_nn(ckvn, wukv_ref[hd]) for hd in range(NH)]
        for hd in range(NH):
            qh = qhs[hd]
            qn = qh * lax.rsqrt(_rowsum(qh * qh) / DK + EPS) * qng
            q_ref[hd] = (jnp.concatenate([qn[:, :128], _rope(qn[:, 128:], cos, sin)], axis=1) * (SCALE * LOG2E)).astype(BF16)
        kr = ulo[:, 384:512]
        skr = _rowsum(kr * kr)
        kng = kng_ref[...]
        kr_roped = _rope(kr * kng[:, 128:], cos, sin)
        for hd in range(NH):
            kv = kvs[hd]
            kn = kv[:, :128]
            rk = lax.rsqrt((_rowsum(kn * kn) + skr) / DK + EPS)
            k_ref[hd] = jnp.concatenate([kn * rk * kng[:, :128], kr_roped * rk], axis=1).astype(BF16)
            v_ref[hd] = kv[:, 128:].astype(BF16)

    row = lambda w: pl.BlockSpec((TB, w), lambda i: (i, 0))
    heads = lambda w: pl.BlockSpec((NH, TB, w), lambda i: (0, i, 0))
    cspec, xspec, mspec = _token_specs(off)
    return pl.pallas_call(
        body, name="fwd_in", grid=(nb,),
        in_specs=[cspec, xspec, mspec, _full((1, D)), _full((DIN, D)), _full((1, QL)), _full((NH, DKP, QL)), _full((1, KVL)),
                  _full((NH, KVL, 256)), _full((1, DKP)), _full((1, DKP)),
                  pl.BlockSpec((1, 8, 256), lambda i: (jnp.maximum(i - off, 0), 0, 0)), _full((TB, 256))],
        out_specs=[row(DU), heads(DKP), heads(DKP), heads(DV)],
        out_shape=[jax.ShapeDtypeStruct((t_all, DU), F32), jax.ShapeDtypeStruct((NH, t_all, DKP), BF16),
                   jax.ShapeDtypeStruct((NH, t_all, DKP), BF16), jax.ShapeDtypeStruct((NH, t_all, DV), BF16)],
        compiler_params=_params(("arbitrary",)),
    )(ctx, x, modsel, norm_g, w_in_t, q_lora_g, w_uq_t, kv_lora_g, w_ukv, qn_g, kn_g, cos, sin)


def _attn_fwd(q, k, v, s_len):
    t_all = q.shape[1]
    off = (t_all - s_len) // TB
    nq = s_len // TB
    nsub = next(n for n in (4, 2, 1) if nq % n == 0)

    def body(*refs):
        q_refs = refs[:nsub]
        k_ref, v_ref, o_ref, lse_ref = refs[nsub:]
        for sb in range(nsub):
            s = _nt(q_refs[sb][0], k_ref[0])
            m = jnp.max(s, axis=-1, keepdims=True)
            e = jnp.exp2(s - m)
            l = _rowsum(e)
            o_ref[sb * TB:(sb + 1) * TB, :] = _nn(e.astype(BF16), v_ref[0]) / l
            lse_ref[0, sb] = _row_layout(m + jnp.log2(l))

    qspec = lambda sb: pl.BlockSpec((1, TB, DKP), lambda h, i: (h, i * nsub + sb + off, 0))
    return pl.pallas_call(
        body, name="attn_fwd", grid=(NH, nq // nsub),
        in_specs=[qspec(sb) for sb in range(nsub)]
        + [pl.BlockSpec((1, t_all, DKP), lambda h, i: (h, 0, 0)), pl.BlockSpec((1, t_all, DV), lambda h, i: (h, 0, 0))],
        out_specs=[pl.BlockSpec((nsub * TB, DV), lambda h, i: (i, h)), pl.BlockSpec((1, nsub, 8, TB), lambda h, i: (h, i, 0, 0))],
        out_shape=[jax.ShapeDtypeStruct((s_len, NH * DV), F32), jax.ShapeDtypeStruct((NH, nq, 8, TB), F32)],
        compiler_params=_params(("arbitrary", "arbitrary")),
    )(*([q] * nsub), k, v)


def _out_stage(attn, u, x, target, gate, w_pool, pool_scale, w_out, lc):
    s_len = x.shape[0]
    t_all = s_len + lc
    off = lc // TB
    nq = s_len // TB
    hb = TB // HALO
    nqb = s_len // Q_BLOCK
    jb = TB // nqb

    def body(attn_ref, ga_ref, pin_ref, pprev_ref, pnext_ref, gp_ref, x_ref, tgt_ref, gate_ref, wp_ref, ps_ref, wo_ref,
             dxn_ref, dattn_ref, dga_ref, dgp_ref, dpool_ref, dwo_ref, dgate_ref, dps_ref, dwp_ref, loss_ref):
        i = pl.program_id(0)

        @pl.when(i == 0)
        def _():
            dwo_ref[...] = jnp.zeros_like(dwo_ref)
            dgate_ref[...] = jnp.zeros_like(dgate_ref)
            dps_ref[...] = jnp.zeros_like(dps_ref)
            dwp_ref[...] = jnp.zeros_like(dwp_ref)
            loss_ref[...] = jnp.zeros_like(loss_ref)

        attn = jnp.concatenate([attn_ref[:, jj, :] for jj in range(jb)], axis=0)
        ga = ga_ref[...]
        gp = gp_ref[...]
        pin = pin_ref[...]
        prev = jnp.where(i == 0, 0.0, pprev_ref[...])
        nxt = jnp.where(i == nq - 1, 0.0, pnext_ref[...])
        win = jnp.concatenate([prev, pin, nxt], axis=0)
        tg = i * TB + lax.broadcasted_iota(jnp.int32, (TB, 1), 0)
        pooled = []
        for g, w in enumerate(POOL_WINDOWS):
            a = win[:, g * 128:(g + 1) * 128]
            p = _shift_rows(a, -1) + a
            for step in (1, 2, 4):
                if w >= 4 * step:
                    p = _shift_rows(p, -step) + _shift_rows(p, step)
            cnt = (jnp.minimum(tg + w // 2, s_len) - jnp.maximum(tg - w // 2, 0)).astype(F32)
            pooled.append(p[HALO:HALO + TB] / cnt - a[HALO:HALO + TB])
        pooled_b = [p.astype(BF16) for p in pooled]
        z = jnp.concatenate([_nn(pooled_b[g], wp_ref[g]) for g in range(4)], axis=1)
        ps = ps_ref[...]
        yp = z * ps
        sga = _sig(ga)
        sila = ga * sga
        sgp = _sig(gp)
        silp = gp * sgp
        br = jnp.concatenate([sila * attn, silp * yp], axis=1).astype(BF16)
        y = _nn(br, wo_ref[...])
        gate = gate_ref[...]
        err = x_ref[...] + gate * y - tgt_ref[...]
        loss_ref[...] += _colsum(_rowsum(err * err)) * (0.5 / D)
        dxn = err * (1.0 / D)
        dxn_ref[...] = dxn
        dgate_ref[...] += _colsum(dxn * y)
        dy = (dxn * gate).astype(BF16)
        dwo_ref[...] += _tn(br, dy)
        dbr = _nt(dy, wo_ref[...])
        dbra = dbr[:, :512]
        dbrp = dbr[:, 512:]
        dattn = dbra * sila
        for jj in range(jb):
            dattn_ref[:, jj, :] = dattn[jj * nqb:(jj + 1) * nqb]
        dga_ref[...] = (dbra * attn * (sga * (1.0 + ga * (1.0 - sga)))).astype(BF16)
        dgp_ref[...] = (dbrp * yp * (sgp * (1.0 + gp * (1.0 - sgp)))).astype(BF16)
        dyp = dbrp * silp
        dps_ref[...] += _colsum(dyp * z)
        dz = (dyp * ps).astype(BF16)
        dpool = []
        for g in range(4):
            dzg = dz[:, g * 128:(g + 1) * 128]
            dwp_ref[g] += _tn(pooled_b[g], dzg)
            dpool.append(_nt(dzg, wp_ref[g]))
        dpool_ref[...] = jnp.concatenate(dpool, axis=1)

    lat = lambda w: pl.BlockSpec((TB, w), lambda i: (i, 0))
    perm = pl.BlockSpec((nqb, jb, 512), lambda i: (0, i, 0))
    ucol = lambda j: pl.BlockSpec((TB, 512), lambda i: (i + off, j))
    last8 = t_all // HALO - 1
    return pl.pallas_call(
        body, name="out_stage", grid=(nq,),
        in_specs=[perm, ucol(1), ucol(2),
                  pl.BlockSpec((HALO, 512), lambda i: ((i + off) * hb - 1, 2)),
                  pl.BlockSpec((HALO, 512), lambda i: (jnp.minimum((i + off + 1) * hb, last8), 2)),
                  ucol(3), lat(D), lat(D), _full((1, D)), _full((4, 128, 128)), _full((1, 512)), _full((D, D))],
        out_specs=[lat(D), perm, lat(512), lat(512), lat(512),
                   _full((D, D)), _full((1, D)), _full((1, 512)), _full((4, 128, 128)), _full((1, 1))],
        out_shape=[jax.ShapeDtypeStruct((s_len, D), F32), jax.ShapeDtypeStruct((nqb, Q_BLOCK, 512), F32),
                   jax.ShapeDtypeStruct((s_len, 512), BF16), jax.ShapeDtypeStruct((s_len, 512), BF16),
                   jax.ShapeDtypeStruct((s_len, 512), F32),
                   jax.ShapeDtypeStruct((D, D), F32), jax.ShapeDtypeStruct((1, D), F32), jax.ShapeDtypeStruct((1, 512), F32),
                   jax.ShapeDtypeStruct((4, 128, 128), F32), jax.ShapeDtypeStruct((1, 1), F32)],
        compiler_params=_params(("arbitrary",)),
    )(attn, u, u, u, u, u, x, target, gate, w_pool, pool_scale, w_out)


def _attn_bwd(q, k, v, dattn, attn, lse, s_len):
    t_all = q.shape[1]
    off = (t_all - s_len) // TB
    nq = s_len // TB
    nch = 4
    chunks = [(c * (t_all // nch), t_all // nch) for c in range(nch)]
    nsub = next(n for n in (BWD_QBLOCKS, 2, 1) if nq % n == 0)
    tq = nsub * TB

    def body(*refs):
        q_refs = refs[:nsub]
        k_ref, v_ref, do_ref, o_ref, lse_ref, dq_ref, dk_ref, dv_ref = refs[nsub:]
        i = pl.program_id(1)

        @pl.when(i == 0)
        def _():
            dk_ref[...] = jnp.zeros_like(dk_ref)
            dv_ref[...] = jnp.zeros_like(dv_ref)

        qb = jnp.concatenate([r[0] for r in q_refs], axis=0)
        delta_r = _row_layout(_rowsum(do_ref[...] * o_ref[...]))[0:1, :]
        do = do_ref[...].astype(BF16)
        lse_r = jnp.concatenate([lse_ref[0, sb][0:1, :] for sb in range(nsub)], axis=1)
        dq = jnp.zeros((tq, DKP), F32)
        for start, size in chunks:
            rows = pl.ds(start, size)
            kc = k_ref[0, rows, :]
            p_t = jnp.exp2(_nt(kc, qb) - lse_r)
            ds_t = (p_t * (_nt(v_ref[0, rows, :], do) - delta_r)).astype(BF16)
            dv_ref[0, rows, :] += _nn(p_t.astype(BF16), do)
            dk_ref[0, rows, :] += _nn(ds_t, qb)
            dq += _tn(ds_t, kc)
        dq_ref[0] = dq * SCALE

    kvspec = lambda w: pl.BlockSpec((1, t_all, w), lambda h, i: (h, 0, 0))
    rowspec = pl.BlockSpec((1, nsub, 8, TB), lambda h, i: (h, i, 0, 0))
    qspec = lambda sb: pl.BlockSpec((1, TB, DKP), lambda h, i: (h, i * nsub + sb + off, 0))
    return pl.pallas_call(
        body, name="attn_bwd", grid=(NH, nq // nsub),
        in_specs=[qspec(sb) for sb in range(nsub)]
        + [kvspec(DKP), kvspec(DV), pl.BlockSpec((tq, DV), lambda h, i: (i, h)), pl.BlockSpec((tq, DV), lambda h, i: (i, h)),
           rowspec],
        out_specs=[pl.BlockSpec((1, tq, DKP), lambda h, i: (h, i, 0)), kvspec(DKP), kvspec(DV)],
        out_shape=[jax.ShapeDtypeStruct((NH, s_len, DKP), F32), jax.ShapeDtypeStruct((NH, t_all, DKP), F32),
                   jax.ShapeDtypeStruct((NH, t_all, DV), F32)],
        compiler_params=_params(("arbitrary", "arbitrary")),
    )(*([q] * nsub), k, v, dattn, attn, lse)


def _qkv_bwd(u, dq, dk, dv, cos, sin, q_lora_g, w_uq_t, kv_lora_g, w_ukv, qn_g, kn_g, s_len):
    t_all = u.shape[0]
    off = (t_all - s_len) // TB
    nb = t_all // TB

    def body(ulo_ref, dq_ref, dk_ref, dv_ref, cos_ref, sin_ref, qlg_ref, wuq_ref, kvlg_ref, wukv_ref, qng_ref, kng_ref,
             dlo_ref, dwuq_ref, dwukv_ref, dqlg_ref, dkvlg_ref, dqng_ref, dkng_ref):
        i = pl.program_id(0)

        @pl.when(i == 0)
        def _():
            for r in (dwuq_ref, dwukv_ref, dqlg_ref, dkvlg_ref, dqng_ref, dkng_ref):
                r[...] = jnp.zeros_like(r)

        latent = i >= off
        ulo = ulo_ref[...]
        cos, sin = _rope_block(cos_ref, sin_ref, pl.program_id(0) < off)
        cq = ulo[:, 0:QL]
        rc = lax.rsqrt(jnp.mean(cq * cq, axis=-1, keepdims=True) + EPS)
        cqh = cq * rc
        qlg = qlg_ref[...]
        cqn_b = (cqh * qlg).astype(BF16)
        qng = qng_ref[...]
        ckv = ulo[:, QL:QL + KVL]
        r0 = lax.rsqrt(jnp.mean(ckv * ckv, axis=-1, keepdims=True) + EPS)
        ckvh = ckv * r0
        kvlg = kvlg_ref[...]
        ckvn_b = (ckvh * kvlg).astype(BF16)
        qhs = [_nt(cqn_b, wuq_ref[hd]) for hd in range(NH)]
        kns = [_nn(ckvn_b, wukv_ref[hd])[:, :128] for hd in range(NH)]
        dqng = jnp.zeros((1, DKP), F32)
        dqraws = []
        for hd in range(NH):
            qh = qhs[hd]
            rq = lax.rsqrt(_rowsum(qh * qh) / DK + EPS)
            xh = qh * rq
            dqh = jnp.where(latent, dq_ref[hd], 0.0)
            dyq = jnp.concatenate([dqh[:, :128], _rope_t(dqh[:, 128:], cos, sin)], axis=1)
            dqng += _colsum(dyq * xh)
            dxh = dyq * qng
            dqraws.append((rq * (dxh - xh * (_rowsum(dxh * xh) / DK))).astype(BF16))
        dqng_ref[...] += dqng

        kr = ulo[:, 384:512]
        skr = _rowsum(kr * kr)
        kng = kng_ref[...]
        dkr = jnp.zeros((TB, 128), F32)
        dkng = jnp.zeros((1, DKP), F32)
        dkvs = []
        for hd in range(NH):
            kn = kns[hd]
            rk = lax.rsqrt((_rowsum(kn * kn) + skr) / DK + EPS)
            xh1 = kn * rk
            xh2 = kr * rk
            dkh = dk_ref[hd] * LN2
            d1 = dkh[:, :128]
            d2 = _rope_t(dkh[:, 128:], cos, sin)
            dkng += jnp.concatenate([_colsum(d1 * xh1), _colsum(d2 * xh2)], axis=1)
            dx1 = d1 * kng[:, :128]
            dx2 = d2 * kng[:, 128:]
            dot = (_rowsum(dx1 * xh1) + _rowsum(dx2 * xh2)) / DK
            dkvs.append(jnp.concatenate([rk * (dx1 - xh1 * dot), dv_ref[hd]], axis=1).astype(BF16))
            dkr += rk * (dx2 - xh2 * dot)
        dkng_ref[...] += dkng

        dcqn = jnp.zeros((TB, QL), F32)
        dckvn = jnp.zeros((TB, KVL), F32)
        for hd in range(NH):
            dwuq_ref[hd] += _tn(dqraws[hd], cqn_b)[:DK]
            dcqn += _nn(dqraws[hd], wuq_ref[hd])
            dwukv_ref[hd] += _tn(ckvn_b, dkvs[hd])
            dckvn += _nt(dkvs[hd], wukv_ref[hd])
        dqlg_ref[...] += _colsum(dcqn * cqh)
        dxh = dcqn * qlg
        dcq = rc * (dxh - cqh * jnp.mean(dxh * cqh, axis=-1, keepdims=True))
        dkvlg_ref[...] += _colsum(dckvn * ckvh)
        dxh = dckvn * kvlg
        dckv = r0 * (dxh - ckvh * jnp.mean(dxh * ckvh, axis=-1, keepdims=True))
        dlo_ref[...] = jnp.concatenate([dcq, dckv, dkr], axis=1).astype(BF16)

    row = lambda w: pl.BlockSpec((TB, w), lambda i: (i, 0))
    heads = lambda w: pl.BlockSpec((NH, TB, w), lambda i: (0, i, 0))
    return pl.pallas_call(
        body, name="qkv_bwd", grid=(nb,),
        in_specs=[row(512), pl.BlockSpec((NH, TB, DKP), lambda i: (0, jnp.maximum(i - off, 0), 0)), heads(DKP), heads(DV),
                  pl.BlockSpec((1, 8, 256), lambda i: (jnp.maximum(i - off, 0), 0, 0)), _full((TB, 256)), _full((1, QL)), _full((NH, DKP, QL)), _full((1, KVL)), _full((NH, KVL, 256)),
                  _full((1, DKP)), _full((1, DKP))],
        out_specs=[row(512), _full((NH, DK, QL)), _full((NH, KVL, 256)), _full((1, QL)), _full((1, KVL)),
                   _full((1, DKP)), _full((1, DKP))],
        out_shape=[jax.ShapeDtypeStruct((t_all, 512), BF16), jax.ShapeDtypeStruct((NH, DK, QL), F32),
                   jax.ShapeDtypeStruct((NH, KVL, 256), F32), jax.ShapeDtypeStruct((1, QL), F32),
                   jax.ShapeDtypeStruct((1, KVL), F32), jax.ShapeDtypeStruct((1, DKP), F32), jax.ShapeDtypeStruct((1, DKP), F32)],
        compiler_params=_params(("arbitrary",)),
    )(u, dq, dk, dv, cos, sin, q_lora_g, w_uq_t, kv_lora_g, w_ukv, qn_g, kn_g)


def _in_bwd(ctx, x, modsel, norm_g, dlo, dga, dgp, dpool, dxn, w_in_t):
    s_len, lc = x.shape[0], ctx.shape[0]
    t_all = s_len + lc
    off = lc // TB
    nb = t_all // TB
    nq = s_len // TB
    hb = TB // HALO
    n = TB + 2 * HALO

    def body(ctx_ref, x_ref, mod_ref, ng_ref, dlo_ref, dga_ref, dgp_ref, dp_ref, dpprev_ref, dpnext_ref, dxn_ref, win_ref,
             gx_ref, dwin_ref, dmod_ref, dng_ref):
        i = pl.program_id(0)
        j = i - off

        @pl.when(i == 0)
        def _():
            dwin_ref[...] = jnp.zeros_like(dwin_ref)
            dmod_ref[...] = jnp.zeros_like(dmod_ref)
            dng_ref[...] = jnp.zeros_like(dng_ref)

        latent = i >= off
        dp = dp_ref[...]
        prev = jnp.where(j <= 0, 0.0, dpprev_ref[...])
        nxt = jnp.where(j >= nq - 1, 0.0, dpnext_ref[...])
        win = jnp.concatenate([prev, dp, nxt], axis=0)
        tg = j * TB - HALO + lax.broadcasted_iota(jnp.int32, (n, 1), 0)
        dpin = []
        for g, w in enumerate(POOL_WINDOWS):
            cnt = jnp.maximum(jnp.minimum(tg + w // 2, s_len) - jnp.maximum(tg - w // 2, 0), 1).astype(F32)
            zq = win[:, g * 128:(g + 1) * 128] / cnt
            zq = zq + _shift_rows(zq, 1)
            for step in (1, 2, 4):
                if w >= 4 * step:
                    zq = _shift_rows(zq, -step) + _shift_rows(zq, step)
            dpin.append(zq[HALO:HALO + TB] - dp[:, g * 128:(g + 1) * 128])
        zero = jnp.zeros((TB, 512), BF16)
        du = [dlo_ref[...], jnp.where(latent, dga_ref[...], zero),
              jnp.where(latent, jnp.concatenate(dpin, axis=1).astype(BF16), zero), jnp.where(latent, dgp_ref[...], zero)]

        ng = ng_ref[...]
        xb = jnp.where(i < off, ctx_ref[...], x_ref[...])
        r, xh, xg, h, scale = _modulated(xb, mod_ref, ng)
        hb_ = h.astype(BF16)
        dh = jnp.zeros((TB, D), F32)
        for s, (lo, hi) in enumerate(SEG):
            dwin_ref[lo:hi, :] += _tn(du[s], hb_)
            dh += _nn(du[s], win_ref[lo:hi, :])
        is_lat = latent.astype(F32)
        dsh = _colsum(dh)
        dsc = _colsum(dh * xg)
        dmod_ref[0, 0:1, :] += dsh * (1.0 - is_lat)
        dmod_ref[0, 1:2, :] += dsc * (1.0 - is_lat)
        dmod_ref[1, 0:1, :] += dsh * is_lat
        dmod_ref[1, 1:2, :] += dsc * is_lat
        dxg = dh * (1.0 + scale)
        dng_ref[...] += _colsum(dxg * xh)
        dxh = dxg * ng
        gx_ref[...] = r * (dxh - xh * jnp.mean(dxh * xh, axis=-1, keepdims=True)) + dxn_ref[...]

    row = lambda w: pl.BlockSpec((TB, w), lambda i: (i, 0))
    lat = lambda w: pl.BlockSpec((TB, w), lambda i: (jnp.maximum(i - off, 0), 0))
    last8 = s_len // HALO - 1
    cspec, xspec, mspec = _token_specs(off)
    return pl.pallas_call(
        body, name="in_bwd", grid=(nb,),
        in_specs=[cspec, xspec, mspec, _full((1, D)), row(512), lat(512), lat(512), lat(512),
                  pl.BlockSpec((HALO, 512), lambda i: (jnp.maximum(jnp.maximum(i - off, 0) * hb - 1, 0), 0)),
                  pl.BlockSpec((HALO, 512), lambda i: (jnp.minimum((jnp.maximum(i - off, 0) + 1) * hb, last8), 0)),
                  lat(D), _full((DIN, D))],
        out_specs=[lat(D), _full((DIN, D)), _full((2, 2, D)), _full((1, D))],
        out_shape=[jax.ShapeDtypeStruct((s_len, D), F32), jax.ShapeDtypeStruct((DIN, D), F32),
                   jax.ShapeDtypeStruct((2, 2, D), F32), jax.ShapeDtypeStruct((1, D), F32)],
        compiler_params=_params(("arbitrary",)),
    )(ctx, x, modsel, norm_g, dlo, dga, dgp, dpool, dpool, dpool, dxn, w_in_t)


def _adamw_update(w_ref, g_ref, m_ref, v_ref, d_ref, mo_ref, vo_ref):
    gv = g_ref[...]
    mn = ADAM_B1 * m_ref[...] + (1.0 - ADAM_B1) * gv
    vn = ADAM_B2 * v_ref[...] + (1.0 - ADAM_B2) * (gv * gv)
    m_hat = mn / (1.0 - ADAM_B1 ** ADAM_STEP)
    v_hat = vn / (1.0 - ADAM_B2 ** ADAM_STEP)
    d_ref[...] = -ADAM_LR * (m_hat / (jnp.sqrt(v_hat) + ADAM_EPS) + ADAM_WD * w_ref[...])
    mo_ref[...] = mn
    vo_ref[...] = vn


def _adamw_many(ws, gs, ms, vs):
    n = len(ws)

    def body(*refs):
        for i in range(n):
            _adamw_update(refs[i], refs[n + i], refs[2 * n + i], refs[3 * n + i], refs[4 * n + i], refs[5 * n + i], refs[6 * n + i])

    def spec(w):
        rows, cols = w.shape
        return pl.BlockSpec((rows // 2, cols), lambda i: (i, 0)) if rows % 16 == 0 else _full((rows, cols))

    specs = [spec(w) for w in ws]
    shp = [jax.ShapeDtypeStruct(w.shape, F32) for w in ws]
    out = pl.pallas_call(body, name="adamw_many", grid=(2,), in_specs=specs * 4, out_specs=specs * 3, out_shape=shp * 3,
                         compiler_params=_params(("arbitrary",)))(*ws, *gs, *ms, *vs)
    return out[:n], out[n:2 * n], out[2 * n:]


def _adamw(w, g, m, v, name):
    rows, cols = w.shape
    rb = next(r for r in range(min(rows, 256), 0, -8) if rows % r == 0)

    def body(w_ref, g_ref, m_ref, v_ref, d_ref, mo_ref, vo_ref):
        _adamw_update(w_ref, g_ref, m_ref, v_ref, d_ref, mo_ref, vo_ref)

    spec = pl.BlockSpec((rb, cols), lambda i: (i, 0))
    shp = jax.ShapeDtypeStruct((rows, cols), F32)
    return pl.pallas_call(
        body, name=name, grid=(rows // rb,), in_specs=[spec] * 4, out_specs=[spec] * 3, out_shape=[shp] * 3,
        compiler_params=_params(("arbitrary",)),
    )(w, g, m, v)


class _Links:
    def __init__(self, send_sems, recv_sems):
        self.send_sems, self.recv_sems, self.sends = send_sems, recv_sems, []

    def send(self, src, dst, sem, to):
        cp = pltpu.make_async_remote_copy(src, dst, self.send_sems.at[sem], self.recv_sems.at[sem], device_id=to,
                                          device_id_type=MESH)
        cp.start()
        self.sends.append(cp)

    def arrived(self, dst, sem, frm):
        pltpu.make_async_remote_copy(dst, dst, self.send_sems.at[sem], self.recv_sems.at[sem], device_id=frm,
                                     device_id_type=MESH).wait_recv()

    def drain(self):
        for cp in self.sends:
            cp.wait_send()


def _half(ref, c, axis):
    size = ref.shape[axis - 2] // 2
    win = pl.ds(pl.multiple_of(c * size, 16 if axis == 0 else 128), size)
    idx = (win, slice(None)) if axis == 0 else (slice(None), win)
    return ref.at[(slice(None),) * (len(ref.shape) - 2) + idx]


def _select_rows(slots_ref, n_slots, row=0):
    sub = lax.broadcasted_iota(jnp.int32, (8, 1), 0)
    out = None
    for d in range(n_slots):
        r = jnp.where(sub == d, jnp.broadcast_to(slots_ref[d][row:row + 1, :], (8, slots_ref.shape[-1])), 0.0)
        out = r if out is None else out + r
    return out


def _gather(c, c_ctx, w_mod, b_mod_k, shards, axes):
    nw = len(shards)
    kw = w_mod.shape[1]

    def body(*refs):
        c_ref, cc_ref, wm_ref, b_ref = refs[:4]
        w_refs = refs[4:4 + nw]
        a16_ref, mod_ref = refs[4 + nw:6 + nw]
        g_refs = refs[6 + nw:6 + 2 * nw]
        a_ref, send_sems, recv_sems, local_sems = refs[6 + 2 * nw:]
        x, y, cc = lax.axis_index("x"), lax.axis_index("y"), lax.axis_index("c")
        me = 4 * x + 2 * y + cc
        k = 2 * x + y
        sibling = (x, y, 1 - cc)
        links = _Links(send_sems, recv_sems)
        chips = [_peer(x, y, cc, off + (0,)) for off in CHIPS3]
        chip_a = ((x + 1 - cc) % 2, (y + cc) % 2, cc)
        chip_b = ((x + cc) % 2, (y + 1 - cc) % 2, cc)
        chip_d = (1 - x, 1 - y, cc)
        cv = c_ref[...]
        a_ref[me] = jnp.broadcast_to(cv * _sig(cv), (8, D))
        for j, off in enumerate(PEERS7):
            links.send(a_ref.at[me], a_ref.at[me], j, _peer(x, y, cc, off))
        locals_ = []
        for wi in range(nw):
            lc = pltpu.make_async_copy(w_refs[wi], g_refs[wi].at[k], local_sems.at[wi])
            lc.start()
            locals_.append(lc)
            for j, to in enumerate((chip_a, chip_b)):
                links.send(_half(w_refs[wi], cc, axes[wi]), _half(g_refs[wi].at[k], cc, axes[wi]), 10 + wi * 6 + j, to)
        for j, off in enumerate(PEERS7):
            px, py, pc = _peer(x, y, cc, off)
            links.arrived(a_ref.at[4 * px + 2 * py + pc], j, (px, py, pc))
        ccv = cc_ref[...]
        sub = lax.broadcasted_iota(jnp.int32, (8, 1), 0)
        a16 = jnp.concatenate([_select_rows(a_ref, 8), jnp.where(sub == 0, jnp.broadcast_to(ccv * _sig(ccv), (8, D)), 0.0)], axis=0)
        a16_ref[...] = a16
        mod_ref[k] = _dot3(_nn, a16, wm_ref[...]) + b_ref[...]
        for j, to in enumerate(chips):
            links.send(mod_ref.at[k], mod_ref.at[k], 7 + j, to)
        for j, (frm, origin) in enumerate(((chip_a, chip_a), (chip_b, chip_b), (chip_b, chip_d))):
            for wi in range(nw):
                blk = _half(g_refs[wi].at[2 * origin[0] + origin[1]], cc, axes[wi])
                links.arrived(blk, 10 + wi * 6 + j, frm)
                if j == 0:
                    links.send(blk, blk, 10 + wi * 6 + 2, chip_b)
                links.send(blk, blk, 10 + wi * 6 + 3 + j, sibling)
        for j, (px, py, pc) in enumerate(chips):
            links.arrived(mod_ref.at[2 * px + py], 7 + j, (px, py, pc))
        for j, origin in enumerate((chip_b, chip_a, chip_d)):
            for wi in range(nw):
                links.arrived(_half(g_refs[wi].at[2 * origin[0] + origin[1]], 1 - cc, axes[wi]), 10 + wi * 6 + 3 + j, sibling)
        links.drain()
        for lc in locals_:
            lc.wait()

    nsem = 10 + 6 * nw
    return pl.pallas_call(
        body, name="gather", in_specs=[VM] * (4 + nw), out_specs=[VM] * (2 + nw),
        out_shape=[jax.ShapeDtypeStruct((16, D), F32), jax.ShapeDtypeStruct((4, 16, kw), F32)]
        + [jax.ShapeDtypeStruct((4,) + s.shape, s.dtype) for s in shards],
        scratch_shapes=[pltpu.VMEM((8, 8, D), F32), pltpu.SemaphoreType.DMA((nsem,)), pltpu.SemaphoreType.DMA((nsem,)),
                        pltpu.SemaphoreType.DMA((nw,))],
        compiler_params=pltpu.CompilerParams(vmem_limit_bytes=VMEM_LIMIT),
    )(c, c_ctx, w_mod, b_mod_k, *shards)


SMALL_ROW_WIDTHS = (D, QL, KVL, DKP, DKP, 512, 128)
SMALL_OUT_WIDTHS = (D, QL, KVL, DK, DK, 512, 1)


def _reduce(grads, axes, smalls, w_pool_g, dmod8, a16, w_mod, c_ctx):
    nw = len(grads)
    ns = len(smalls)
    kw = w_mod.shape[1]
    halves = []
    for g, ax in zip(grads, axes):
        halves.append((g.shape[1] // 2, g.shape[2]) if ax == 0 else (g.shape[1], g.shape[2] // 2))

    def body(*refs):
        g_refs = refs[:nw]
        small_refs = refs[nw:nw + ns]
        wp_ref, dm_ref, a16_ref, wm_ref, cc_ref = refs[nw + ns:nw + ns + 5]
        o = nw + ns + 5
        r_refs = refs[o:o + nw]
        small_outs = refs[o + nw:o + nw + ns]
        rwp_ref, gw_ref, gb_ref, gc_ref = refs[o + nw + ns:o + nw + ns + 4]
        o = o + nw + ns + 4
        own, sib, part, got = (refs[o + i * nw:o + (i + 1) * nw] for i in range(4))
        smbuf, wps, wpg, dm_all, pc_all, send_sems, recv_sems, local_sems = refs[o + 4 * nw:]
        x, y, cc = lax.axis_index("x"), lax.axis_index("y"), lax.axis_index("c")
        me = 4 * x + 2 * y + cc
        k = 2 * x + y
        sibling = (x, y, 1 - cc)
        links = _Links(send_sems, recv_sems)
        chips = [_peer(x, y, cc, off + (0,)) for off in CHIPS3]
        peers = [_peer(x, y, cc, off) for off in PEERS7]
        big, sm0, wp0, dm0, pc0 = 0, 5 * nw, 5 * nw + 7, 5 * nw + 14, 5 * nw + 21

        locals_ = []
        for wi in range(nw):
            lc = pltpu.make_async_copy(_half(g_refs[wi], cc, axes[wi]), own[wi], local_sems.at[wi])
            lc.start()
            locals_.append(lc)
            links.send(_half(g_refs[wi], 1 - cc, axes[wi]), sib[wi], big + wi * 5, sibling)
        slot = smbuf.at[me]
        slot[...] = jnp.zeros((8, D), F32)
        for r, (ref, w) in enumerate(zip(small_refs, SMALL_ROW_WIDTHS)):
            slot[r:r + 1, 0:w] = jnp.broadcast_to(ref[...], (1, w))
        links.send(wp_ref, wps, wp0, sibling)
        dm_all[me] = dm_ref[...]
        for j, peer in enumerate(peers):
            links.send(dm_all.at[me], dm_all.at[me], dm0 + j, peer)
            links.send(smbuf.at[me], smbuf.at[me], sm0 + j, peer)
        links.arrived(wps, wp0, sibling)
        wpg[k] = (wp_ref[...] + wps[...]).astype(BF16)
        for j, to in enumerate(chips):
            links.send(wpg.at[k], wpg.at[k], wp0 + 1 + j, to)
        for wi in range(nw):
            locals_[wi].wait()
            links.arrived(sib[wi], big + wi * 5, sibling)
            part[wi][...] = (own[wi][...] + sib[wi][...]).astype(BF16)
            got[wi][k] = part[wi][k]
            for j, (px, py, pc) in enumerate(chips):
                links.send(part[wi].at[2 * px + py], got[wi].at[k], big + wi * 5 + 1 + j, (px, py, pc))
        for j, (px, py, pc) in enumerate(peers):
            links.arrived(dm_all.at[4 * px + 2 * py + pc], dm0 + j, (px, py, pc))
        dmc = dm_all[0][1:2, :]
        dml = dm_all[0][0:1, :]
        for d in range(1, 8):
            dmc = dmc + dm_all[d][1:2, :]
            dml = dml + dm_all[d][0:1, :]
        gb_ref[...] = dml + dmc
        sub = lax.broadcasted_iota(jnp.int32, (8, 1), 0)
        b16 = jnp.concatenate([_select_rows(dm_all, 8), jnp.where(sub == 0, jnp.broadcast_to(dmc, (8, 3 * D)), 0.0)], axis=0)
        bk = jnp.zeros((16, kw), F32)
        for kk in range(4):
            bk = bk + jnp.where(k == kk, b16[:, kk * kw:(kk + 1) * kw], 0.0)
        gw_ref[...] = _dot3(_tn, a16_ref[...], bk)
        pc_all[k] = _dot3(_nt, jnp.broadcast_to(bk[8:9, :], (8, kw)), wm_ref[...])
        for j, to in enumerate(chips):
            links.send(pc_all.at[k], pc_all.at[k], pc0 + j, to)
        for j, (px, py, pc) in enumerate(peers):
            links.arrived(smbuf.at[4 * px + 2 * py + pc], sm0 + j, (px, py, pc))
        tot = smbuf[0]
        for d in range(1, 8):
            tot = tot + smbuf[d]
        for r, (ref, w) in enumerate(zip(small_outs, SMALL_OUT_WIDTHS)):
            ref[...] = tot[r:r + 1, 0:w]
        for j, (px, py, pc) in enumerate(chips):
            links.arrived(wpg.at[2 * px + py], wp0 + 1 + j, (px, py, pc))
        wpt = wpg[0].astype(F32)
        for kk in range(1, 4):
            wpt = wpt + wpg[kk].astype(F32)
        rwp_ref[...] = wpt
        for wi in range(nw):
            for j, (px, py, pc) in enumerate(chips):
                links.arrived(got[wi].at[2 * px + py], big + wi * 5 + 1 + j, (px, py, pc))
            total = got[wi][0].astype(F32)
            for kk in range(1, 4):
                total = total + got[wi][kk].astype(F32)
            mine = _half(r_refs[wi], cc, axes[wi])
            mine[...] = total
            links.send(mine, mine, big + wi * 5 + 4, sibling)
        for j, (px, py, pc) in enumerate(chips):
            links.arrived(pc_all.at[2 * px + py], pc0 + j, (px, py, pc))
        ccv = cc_ref[...]
        sg = _sig(ccv)
        gc_ref[...] = (pc_all[0][0:1, :] + pc_all[1][0:1, :] + pc_all[2][0:1, :] + pc_all[3][0:1, :]) * (sg * (1.0 + ccv * (1.0 - sg)))
        for wi in range(nw):
            links.arrived(_half(r_refs[wi], 1 - cc, axes[wi]), big + wi * 5 + 4, sibling)
        links.drain()

    nsem = 5 * nw + 24
    quads = [(4,) + h for h in halves]
    return pl.pallas_call(
        body, name="reduce", in_specs=[ANY] * nw + [VM] * (ns + 5), out_specs=[VM] * (nw + ns + 4),
        out_shape=[jax.ShapeDtypeStruct(g.shape[1:], F32) for g in grads]
        + [jax.ShapeDtypeStruct((1, w), F32) for w in SMALL_OUT_WIDTHS]
        + [jax.ShapeDtypeStruct(w_pool_g.shape, F32), jax.ShapeDtypeStruct((D, kw), F32), jax.ShapeDtypeStruct((1, 3 * D), F32),
           jax.ShapeDtypeStruct((1, D), F32)],
        scratch_shapes=[pltpu.VMEM(q, F32) for q in quads] + [pltpu.VMEM(q, F32) for q in quads]
        + [pltpu.VMEM(q, BF16) for q in quads] + [pltpu.VMEM(q, BF16) for q in quads]
        + [pltpu.VMEM((8, 8, D), F32), pltpu.VMEM(w_pool_g.shape, F32), pltpu.VMEM((4,) + w_pool_g.shape, BF16),
           pltpu.VMEM((8, 8, 3 * D), F32),
           pltpu.VMEM((4, 8, D), F32)]
        + [pltpu.SemaphoreType.DMA((nsem,)), pltpu.SemaphoreType.DMA((nsem,)), pltpu.SemaphoreType.DMA((nw,))],
        compiler_params=pltpu.CompilerParams(vmem_limit_bytes=VMEM_LIMIT),
    )(*grads, *smalls, w_pool_g, dmod8, a16, w_mod, c_ctx)


def _rope_tables(s_len):
    rows = s_len // GRID_W
    per = TB // GRID_W
    n_freq = 16
    inv = ROPE_BASE ** (-jnp.arange(n_freq, dtype=F32) / n_freq)
    ang_r = jnp.arange(rows, dtype=F32)[:, None] * inv
    ang_c = jnp.arange(GRID_W, dtype=F32)[:, None] * inv
    by_row, by_col = [], []
    for fn, pad in ((jnp.cos, 1.0), (jnp.sin, 0.0)):
        r = jnp.concatenate([fn(ang_r), fn(ang_r), jnp.zeros((rows, 96), F32)], axis=1).reshape(rows // per, per, 128)
        by_row.append(jnp.pad(r, ((0, 0), (0, 8 - per), (0, 0))))
        cpart = jnp.concatenate([jnp.zeros((GRID_W, 32), F32), fn(ang_c), fn(ang_c), jnp.full((GRID_W, 64), pad, F32)], axis=1)
        by_col.append(jnp.tile(cpart, (per, 1)))
    return jnp.concatenate(by_row, axis=-1), jnp.concatenate(by_col, axis=-1)


def kernel(x, c, ctx, c_ctx, w_mod, b_mod, norm_g, w_in, q_lora_g, w_uq, kv_lora_g, w_ukv, q_norm_g, k_norm_g, w_pool, pool_scale, w_out, loss_target, m_c_ctx, m_w_mod, m_b_mod, m_norm_g, m_w_in, m_q_lora_g, m_w_uq, m_kv_lora_g, m_w_ukv, m_q_norm_g, m_k_norm_g, m_w_pool, m_pool_scale, m_w_out, v_c_ctx, v_w_mod, v_b_mod, v_norm_g, v_w_in, v_q_lora_g, v_w_uq, v_kv_lora_g, v_w_ukv, v_q_norm_g, v_k_norm_g, v_w_pool, v_pool_scale, v_w_out):
    xi, yi, ci = lax.axis_index("x"), lax.axis_index("y"), lax.axis_index("c")
    me = 4 * xi + 2 * yi + ci
    k = 2 * xi + yi
    s_len = x.shape[1]
    lc = ctx.shape[1]
    kw = w_mod.shape[2]
    weights = dict(c_ctx=c_ctx, w_mod=w_mod, b_mod=b_mod, norm_g=norm_g, w_in=w_in, q_lora_g=q_lora_g, w_uq=w_uq,
                   kv_lora_g=kv_lora_g, w_ukv=w_ukv, q_norm_g=q_norm_g, k_norm_g=k_norm_g, w_pool=w_pool,
                   pool_scale=pool_scale, w_out=w_out)
    m_in = dict(c_ctx=m_c_ctx, w_mod=m_w_mod, b_mod=m_b_mod, norm_g=m_norm_g, w_in=m_w_in, q_lora_g=m_q_lora_g, w_uq=m_w_uq,
                kv_lora_g=m_kv_lora_g, w_ukv=m_w_ukv, q_norm_g=m_q_norm_g, k_norm_g=m_k_norm_g, w_pool=m_w_pool,
                pool_scale=m_pool_scale, w_out=m_w_out)
    v_in = dict(c_ctx=v_c_ctx, w_mod=v_w_mod, b_mod=v_b_mod, norm_g=v_norm_g, w_in=v_w_in, q_lora_g=v_q_lora_g, w_uq=v_w_uq,
                kv_lora_g=v_kv_lora_g, w_ukv=v_w_ukv, q_norm_g=v_q_norm_g, k_norm_g=v_k_norm_g, w_pool=v_w_pool,
                pool_scale=v_pool_scale, w_out=v_w_out)
    order = ["c_ctx", "w_mod", "b_mod", "norm_g", "w_in", "q_lora_g", "w_uq", "kv_lora_g", "w_ukv", "q_norm_g", "k_norm_g",
             "w_pool", "pool_scale", "w_out"]
    transposed = ("w_in", "w_uq")
    as2d = lambda n, a: jnp.transpose(a[0]) if n in transposed else a.reshape(-1, a.shape[-1])
    back = lambda n, a: jnp.transpose(a)[None] if n in transposed else a.reshape(weights[n].shape)

    c_ctx2 = c_ctx.reshape(1, D)
    b_mod_k = lax.dynamic_slice(b_mod, (0, k * kw), (1, kw))
    split = (1, 0, 0, 0)
    a16, mod_all, g_in, g_uq, g_ukv, g_out = _gather(
        c, c_ctx2, w_mod[0], b_mod_k,
        [as2d("w_in", w_in).astype(BF16), as2d("w_uq", w_uq).astype(BF16), w_ukv[0].astype(BF16), w_out[0].astype(BF16)], split)
    mod_me = lax.dynamic_index_in_dim(mod_all, me, axis=1, keepdims=False).reshape(3, D)
    mod_c = mod_all[:, 8, :].reshape(3, D)
    modsel = jnp.stack([mod_c, mod_me])
    w_in_t = g_in.reshape(DIN, D)
    w_uq_t = jnp.pad(g_uq, ((0, 0), (0, DKP - DK), (0, 0)))
    w_out_f = g_out.reshape(D, D)
    qn_g = jnp.pad(q_norm_g, ((0, 0), (0, DKP - DK)))
    kn_g = jnp.pad(k_norm_g, ((0, 0), (0, DKP - DK)))
    w_pool_b = w_pool[0].astype(BF16)
    cos, sin = _rope_tables(s_len)

    u, q, kk, v = _fwd_in(ctx[0], x[0], modsel, norm_g, w_in_t, q_lora_g, w_uq_t, kv_lora_g, g_ukv, qn_g, kn_g, cos, sin)
    attn, lse = _attn_fwd(q, kk, v, s_len)
    (dxn, dattn, dga, dgp, dpool, dw_out, dgate, dps, dw_pool, loss) = _out_stage(
        attn.reshape(s_len // Q_BLOCK, Q_BLOCK, NH * DV), u, x[0], loss_target[0], modsel[1, 2:3, :], w_pool_b, pool_scale,
        w_out_f, lc)
    dattn = dattn.reshape(s_len, NH * DV)
    dq, dk, dv = _attn_bwd(q, kk, v, dattn, attn, lse, s_len)
    dlo, dw_uq_t, dw_ukv, dqlg, dkvlg, dqng, dkng = _qkv_bwd(u, dq, dk, dv, cos, sin, q_lora_g, w_uq_t, kv_lora_g, g_ukv,
                                                            qn_g, kn_g, s_len)
    gx, dw_in_t, dmod, dng = _in_bwd(ctx[0], x[0], modsel, norm_g, dlo, dga, dgp, dpool, dxn, w_in_t)

    dmod_l = jnp.concatenate([dmod[1, 0], dmod[1, 1], dgate[0]]).reshape(1, 3 * D)
    dmod_c = jnp.concatenate([dmod[0, 0], dmod[0, 1], jnp.zeros((D,), F32)]).reshape(1, 3 * D)
    dmod8 = jnp.concatenate([dmod_l, dmod_c, jnp.zeros((6, 3 * D), F32)], axis=0)
    (r_in, r_uq, r_ukv, r_out, g_ng, g_qlg, g_kvlg, g_qng, g_kng, g_ps, loss_all, g_wp, g_w_mod, g_b_mod, g_c_ctx) = _reduce(
        [dw_in_t.reshape(4, DIN // 4, D), dw_uq_t, dw_ukv, dw_out.reshape(4, D // 4, D)], split,
        [dng, dqlg, dkvlg, dqng, dkng, dps, loss], dw_pool, dmod8, a16, w_mod[0], c_ctx2)
    g2d = dict(c_ctx=g_c_ctx, b_mod=g_b_mod, w_mod=g_w_mod, w_in=r_in, w_uq=r_uq, w_ukv=r_ukv, w_out=r_out, norm_g=g_ng,
               q_lora_g=g_qlg, kv_lora_g=g_kvlg, q_norm_g=g_qng, k_norm_g=g_kng, pool_scale=g_ps, w_pool=g_wp.reshape(512, 128))

    d2d, m2d, v2d = {}, {}, {}
    d2d["w_mod"], m2d["w_mod"], v2d["w_mod"] = _adamw(as2d("w_mod", w_mod), g2d["w_mod"], as2d("w_mod", m_w_mod),
                                                      as2d("w_mod", v_w_mod), "adamw_w_mod")
    rest = [n for n in order if n != "w_mod"]
    outs = _adamw_many([as2d(n, weights[n]) for n in rest], [g2d[n] for n in rest], [as2d(n, m_in[n]) for n in rest],
                       [as2d(n, v_in[n]) for n in rest])
    for dst, arrs in zip((d2d, m2d, v2d), outs):
        dst.update(dict(zip(rest, arrs)))

    return (loss_all[0, 0], gx[None], *[back(n, g2d[n]) for n in order], *[back(n, d2d[n]) for n in order],
            *[back(n, m2d[n]) for n in order], *[back(n, v2d[n]) for n in order])
```

```python
import jax
import jax.numpy as jnp
from jax import lax
from jax.experimental import pallas as pl
from jax.experimental.pallas import tpu as pltpu

F32 = jnp.float32
BF16 = jnp.bfloat16
MESH = pl.DeviceIdType.MESH

D = 1024
NH = 4
DK = 192
DKP = 256
DV = 128
QL = 256
KVL = 128
DIN = 1984
U_LO = 448
SEG = ((0, 512), (448, 960), (960, 1472), (1472, 1984))
DU = 2048
POOL_WINDOWS = (2, 4, 8, 16)
HALO = 8
EPS = 1e-6
ROPE_BASE = 10000.0
GRID_W = 64
Q_BLOCK = 128
TB = 256
BWD_QBLOCKS = 1
SCALE = DK ** -0.5
LOG2E = 1.4426950408889634
LN2 = 0.6931471805599453
VMEM_LIMIT = 56 * 1024 * 1024

ADAM_LR = 0.001
ADAM_B1 = 0.9
ADAM_B2 = 0.999
ADAM_EPS = 1e-08
ADAM_WD = 0.01
ADAM_STEP = 10

CHIPS3 = ((1, 0), (0, 1), (1, 1))
PEERS7 = tuple((dx, dy, dc) for dx in (0, 1) for dy in (0, 1) for dc in (0, 1) if (dx, dy, dc) != (0, 0, 0))

VM = pl.BlockSpec(memory_space=pltpu.VMEM)
ANY = pl.BlockSpec(memory_space=pl.ANY)


def _nn(a, b):
    return jnp.dot(a, b, preferred_element_type=F32)


def _nt(a, b):
    return lax.dot_general(a, b, (((1,), (1,)), ((), ())), preferred_element_type=F32)


def _tn(a, b):
    return lax.dot_general(a, b, (((0,), (0,)), ((), ())), preferred_element_type=F32)


def _split3(a):
    a0 = a.astype(BF16)
    r = a - a0.astype(F32)
    a1 = r.astype(BF16)
    a2 = (r - a1.astype(F32)).astype(BF16)
    return a0, a1, a2


def _dot3(dot, a, b):
    sa = _split3(a)
    sb = _split3(b)
    out = None
    for i in range(3):
        for j in range(3 - i):
            t = dot(sa[i], sb[j])
            out = t if out is None else out + t
    return out


def _sig(x):
    return 1.0 / (1.0 + jnp.exp(-x))


def _rot(t):
    src = lax.broadcasted_iota(jnp.int32, (128, 128), 0)
    dst = lax.broadcasted_iota(jnp.int32, (128, 128), 1)
    first = (dst % 32) < 16
    perm = jnp.where(first & (src == dst + 16), -1.0, jnp.where(~first & (src == dst - 16), 1.0, 0.0)).astype(BF16)
    hi = t.astype(BF16)
    lo = (t - hi.astype(F32)).astype(BF16)
    return _nn(hi, perm) + _nn(lo, perm)


def _rope(t, cos, sin):
    return t * cos + _rot(t) * sin


def _rope_t(t, cos, sin):
    return t * cos - _rot(t * sin)


def _rope_block(rows_ref, cols_ref, is_ctx):
    lane = lax.broadcasted_iota(jnp.int32, (TB, 256), 1) % 128
    rows = jnp.concatenate([jnp.broadcast_to(rows_ref[0, r:r + 1, :], (GRID_W, 256)) for r in range(TB // GRID_W)], axis=0)
    cs = jnp.where(lane < 32, rows, cols_ref[...])
    return jnp.where(is_ctx, 1.0, cs[:, :128]), jnp.where(is_ctx, 0.0, cs[:, 128:])


def _shift_rows(z, k):
    n = z.shape[0]
    return pltpu.roll(z, (n - k) % n, 0)


def _colsum(a):
    return jnp.sum(a, axis=0, keepdims=True)


def _rowsum(a):
    return jnp.sum(a, axis=-1, keepdims=True)


def _row_layout(col):
    return jnp.transpose(jnp.broadcast_to(col, (col.shape[0], 128)))[0:8, :]


def _params(sem=None):
    return pltpu.CompilerParams(dimension_semantics=sem, vmem_limit_bytes=VMEM_LIMIT)


def _full(shape):
    nd = len(shape)
    return pl.BlockSpec(shape, lambda *_: (0,) * nd)


def _peer(x, y, c, off):
    dx, dy, dc = off
    return ((x + dx) % 2, (y + dy) % 2, (c + dc) % 2)


def _token_specs(off):
    ctx = pl.BlockSpec((TB, D), lambda i: (jnp.minimum(i, off - 1), 0))
    lat = pl.BlockSpec((TB, D), lambda i: (jnp.maximum(i - off, 0), 0))
    mod = pl.BlockSpec((1, 3, D), lambda i: (jnp.minimum(i // off, 1), 0, 0))
    return ctx, lat, mod


def _modulated(x, mod_ref, ng):
    shift = mod_ref[0, 0:1, :]
    scale = mod_ref[0, 1:2, :]
    r = lax.rsqrt(jnp.mean(x * x, axis=-1, keepdims=True) + EPS)
    xh = x * r
    xg = xh * ng
    return r, xh, xg, xg * (1.0 + scale) + shift, scale


def _fwd_in(ctx, x, modsel, norm_g, w_in_t, q_lora_g, w_uq_t, kv_lora_g, w_ukv, qn_g, kn_g, cos, sin):
    s_len, lc = x.shape[0], ctx.shape[0]
    t_all = s_len + lc
    nb = t_all // TB
    off = lc // TB

    def body(ctx_ref, x_ref, mod_ref, ng_ref, win_ref, qlg_ref, wuq_ref, kvlg_ref, wukv_ref, qng_ref, kng_ref, cos_ref, sin_ref,
             u_ref, q_ref, k_ref, v_ref):
        is_ctx = pl.program_id(0) < off
        xb = jnp.where(is_ctx, ctx_ref[...], x_ref[...])
        _, _, _, h, _ = _modulated(xb, mod_ref, ng_ref[...])
        hb = h.astype(BF16)
        lane = lax.broadcasted_iota(jnp.int32, (TB, 512), 1)
        ulo = jnp.where(lane < U_LO, _nt(hb, win_ref[SEG[0][0]:SEG[0][1], :]), 0.0)
        u_ref[:, 0:512] = ulo
        for j in range(1, 4):
            u_ref[:, j * 512:(j + 1) * 512] = _nt(hb, win_ref[SEG[j][0]:SEG[j][1], :])
        cos, sin = _rope_block(cos_ref, sin_ref, is_ctx)
        cq = ulo[:, 0:QL]
        cqn = (cq * lax.rsqrt(jnp.mean(cq * cq, axis=-1, keepdims=True) + EPS) * qlg_ref[...]).astype(BF16)
        qng = qng_ref[...]
        ckv = ulo[:, QL:QL + KVL]
        ckvn = (ckv * lax.rsqrt(jnp.mean(ckv * ckv, axis=-1, keepdims=True) + EPS) * kvlg_ref[...]).astype(BF16)
        qhs = [_nt(cqn, wuq_ref[hd]) for hd in range(NH)]
        kvs = [_nn(ckvn, wukv_ref[hd]) for hd in range(NH)]
        for hd in range(NH):
            qh = qhs[hd]
            qn = qh * lax.rsqrt(_rowsum(qh * qh) / DK + EPS) * qng
            q_ref[hd] = (jnp.concatenate([qn[:, :128], _rope(qn[:, 128:], cos, sin)], axis=1) * (SCALE * LOG2E)).astype(BF16)
        kr = ulo[:, 384:512]
        skr = _rowsum(kr * kr)
        kng = kng_ref[...]
        kr_roped = _rope(kr * kng[:, 128:], cos, sin)
        for hd in range(NH):
            kv = kvs[hd]
            kn = kv[:, :128]
            rk = lax.rsqrt((_rowsum(kn * kn) + skr) / DK + EPS)
            k_ref[hd] = jnp.concatenate([kn * rk * kng[:, :128], kr_roped * rk], axis=1).astype(BF16)
            v_ref[hd] = kv[:, 128:].astype(BF16)

    row = lambda w: pl.BlockSpec((TB, w), lambda i: (i, 0))
    heads = lambda w: pl.BlockSpec((NH, TB, w), lambda i: (0, i, 0))
    cspec, xspec, mspec = _token_specs(off)
    return pl.pallas_call(
        body, name="fwd_in", grid=(nb,),
        in_specs=[cspec, xspec, mspec, _full((1, D)), _full((DIN, D)), _full((1, QL)), _full((NH, DKP, QL)), _full((1, KVL)),
                  _full((NH, KVL, 256)), _full((1, DKP)), _full((1, DKP)),
                  pl.BlockSpec((1, 8, 256), lambda i: (jnp.maximum(i - off, 0), 0, 0)), _full((TB, 256))],
        out_specs=[row(DU), heads(DKP), heads(DKP), heads(DV)],
        out_shape=[jax.ShapeDtypeStruct((t_all, DU), F32), jax.ShapeDtypeStruct((NH, t_all, DKP), BF16),
                   jax.ShapeDtypeStruct((NH, t_all, DKP), BF16), jax.ShapeDtypeStruct((NH, t_all, DV), BF16)],
        compiler_params=_params(("arbitrary",)),
    )(ctx, x, modsel, norm_g, w_in_t, q_lora_g, w_uq_t, kv_lora_g, w_ukv, qn_g, kn_g, cos, sin)


def _attn_fwd(q, k, v, s_len):
    t_all = q.shape[1]
    off = (t_all - s_len) // TB
    nq = s_len // TB
    nsub = next(n for n in (4, 2, 1) if nq % n == 0)

    def body(*refs):
        q_refs = refs[:nsub]
        k_ref, v_ref, o_ref, lse_ref = refs[nsub:]
        for sb in range(nsub):
            s = _nt(q_refs[sb][0], k_ref[0])
            m = jnp.max(s, axis=-1, keepdims=True)
            e = jnp.exp2(s - m)
            l = _rowsum(e)
            o_ref[sb * TB:(sb + 1) * TB, :] = _nn(e.astype(BF16), v_ref[0]) / l
            lse_ref[0, sb] = _row_layout(m + jnp.log2(l))

    qspec = lambda sb: pl.BlockSpec((1, TB, DKP), lambda h, i: (h, i * nsub + sb + off, 0))
    return pl.pallas_call(
        body, name="attn_fwd", grid=(NH, nq // nsub),
        in_specs=[qspec(sb) for sb in range(nsub)]
        + [pl.BlockSpec((1, t_all, DKP), lambda h, i: (h, 0, 0)), pl.BlockSpec((1, t_all, DV), lambda h, i: (h, 0, 0))],
        out_specs=[pl.BlockSpec((nsub * TB, DV), lambda h, i: (i, h)), pl.BlockSpec((1, nsub, 8, TB), lambda h, i: (h, i, 0, 0))],
        out_shape=[jax.ShapeDtypeStruct((s_len, NH * DV), F32), jax.ShapeDtypeStruct((NH, nq, 8, TB), F32)],
        compiler_params=_params(("arbitrary", "arbitrary")),
    )(*([q] * nsub), k, v)


def _out_stage(attn, u, x, target, gate, w_pool, pool_scale, w_out, lc):
    s_len = x.shape[0]
    t_all = s_len + lc
    off = lc // TB
    nq = s_len // TB
    hb = TB // HALO
    nqb = s_len // Q_BLOCK
    jb = TB // nqb

    def body(attn_ref, ga_ref, pin_ref, pprev_ref, pnext_ref, gp_ref, x_ref, tgt_ref, gate_ref, wp_ref, ps_ref, wo_ref,
             dxn_ref, dattn_ref, dga_ref, dgp_ref, dpool_ref, dwo_ref, dgate_ref, dps_ref, dwp_ref, loss_ref):
        i = pl.program_id(0)

        @pl.when(i == 0)
        def _():
            dwo_ref[...] = jnp.zeros_like(dwo_ref)
            dgate_ref[...] = jnp.zeros_like(dgate_ref)
            dps_ref[...] = jnp.zeros_like(dps_ref)
            dwp_ref[...] = jnp.zeros_like(dwp_ref)
            loss_ref[...] = jnp.zeros_like(loss_ref)

        attn = jnp.concatenate([attn_ref[:, jj, :] for jj in range(jb)], axis=0)
        ga = ga_ref[...]
        gp = gp_ref[...]
        pin = pin_ref[...]
        prev = jnp.where(i == 0, 0.0, pprev_ref[...])
        nxt = jnp.where(i == nq - 1, 0.0, pnext_ref[...])
        win = jnp.concatenate([prev, pin, nxt], axis=0)
        tg = i * TB + lax.broadcasted_iota(jnp.int32, (TB, 1), 0)
        pooled = []
        for g, w in enumerate(POOL_WINDOWS):
            a = win[:, g * 128:(g + 1) * 128]
            p = _shift_rows(a, -1) + a
            for step in (1, 2, 4):
                if w >= 4 * step:
                    p = _shift_rows(p, -step) + _shift_rows(p, step)
            cnt = (jnp.minimum(tg + w // 2, s_len) - jnp.maximum(tg - w // 2, 0)).astype(F32)
            pooled.append(p[HALO:HALO + TB] / cnt - a[HALO:HALO + TB])
        pooled_b = [p.astype(BF16) for p in pooled]
        z = jnp.concatenate([_nn(pooled_b[g], wp_ref[g]) for g in range(4)], axis=1)
        ps = ps_ref[...]
        yp = z * ps
        sga = _sig(ga)
        sila = ga * sga
        sgp = _sig(gp)
        silp = gp * sgp
        br = jnp.concatenate([sila * attn, silp * yp], axis=1).astype(BF16)
        y = _nn(br, wo_ref[...])
        gate = gate_ref[...]
        err = x_ref[...] + gate * y - tgt_ref[...]
        loss_ref[...] += _colsum(_rowsum(err * err)) * (0.5 / D)
        dxn = err * (1.0 / D)
        dxn_ref[...] = dxn
        dgate_ref[...] += _colsum(dxn * y)
        dy = (dxn * gate).astype(BF16)
        dwo_ref[...] += _tn(br, dy)
        dbr = _nt(dy, wo_ref[...])
        dbra = dbr[:, :512]
        dbrp = dbr[:, 512:]
        dattn = dbra * sila
        for jj in range(jb):
            dattn_ref[:, jj, :] = dattn[jj * nqb:(jj + 1) * nqb]
        dga_ref[...] = (dbra * attn * (sga * (1.0 + ga * (1.0 - sga)))).astype(BF16)
        dgp_ref[...] = (dbrp * yp * (sgp * (1.0 + gp * (1.0 - sgp)))).astype(BF16)
        dyp = dbrp * silp
        dps_ref[...] += _colsum(dyp * z)
        dz = (dyp * ps).astype(BF16)
        dpool = []
        for g in range(4):
            dzg = dz[:, g * 128:(g + 1) * 128]
            dwp_ref[g] += _tn(pooled_b[g], dzg)
            dpool.append(_nt(dzg, wp_ref[g]))
        dpool_ref[...] = jnp.concatenate(dpool, axis=1)

    lat = lambda w: pl.BlockSpec((TB, w), lambda i: (i, 0))
    perm = pl.BlockSpec((nqb, jb, 512), lambda i: (0, i, 0))
    ucol = lambda j: pl.BlockSpec((TB, 512), lambda i: (i + off, j))
    last8 = t_all // HALO - 1
    return pl.pallas_call(
        body, name="out_stage", grid=(nq,),
        in_specs=[perm, ucol(1), ucol(2),
                  pl.BlockSpec((HALO, 512), lambda i: ((i + off) * hb - 1, 2)),
                  pl.BlockSpec((HALO, 512), lambda i: (jnp.minimum((i + off + 1) * hb, last8), 2)),
                  ucol(3), lat(D), lat(D), _full((1, D)), _full((4, 128, 128)), _full((1, 512)), _full((D, D))],
        out_specs=[lat(D), perm, lat(512), lat(512), lat(512),
                   _full((D, D)), _full((1, D)), _full((1, 512)), _full((4, 128, 128)), _full((1, 1))],
        out_shape=[jax.ShapeDtypeStruct((s_len, D), F32), jax.ShapeDtypeStruct((nqb, Q_BLOCK, 512), F32),
                   jax.ShapeDtypeStruct((s_len, 512), BF16), jax.ShapeDtypeStruct((s_len, 512), BF16),
                   jax.ShapeDtypeStruct((s_len, 512), F32),
                   jax.ShapeDtypeStruct((D, D), F32), jax.ShapeDtypeStruct((1, D), F32), jax.ShapeDtypeStruct((1, 512), F32),
                   jax.ShapeDtypeStruct((4, 128, 128), F32), jax.ShapeDtypeStruct((1, 1), F32)],
        compiler_params=_params(("arbitrary",)),
    )(attn, u, u, u, u, u, x, target, gate, w_pool, pool_scale, w_out)


def _attn_bwd(q, k, v, dattn, attn, lse, s_len):
    t_all = q.shape[1]
    off = (t_all - s_len) // TB
    nq = s_len // TB
    nch = 4
    chunks = [(c * (t_all // nch), t_all // nch) for c in range(nch)]
    nsub = next(n for n in (BWD_QBLOCKS, 2, 1) if nq % n == 0)
    tq = nsub * TB

    def body(*refs):
        q_refs = refs[:nsub]
        k_ref, v_ref, do_ref, o_ref, lse_ref, dq_ref, dk_ref, dv_ref = refs[nsub:]
        i = pl.program_id(1)

        @pl.when(i == 0)
        def _():
            dk_ref[...] = jnp.zeros_like(dk_ref)
            dv_ref[...] = jnp.zeros_like(dv_ref)

        qb = jnp.concatenate([r[0] for r in q_refs], axis=0)
        delta_r = _row_layout(_rowsum(do_ref[...] * o_ref[...]))[0:1, :]
        do = do_ref[...].astype(BF16)
        lse_r = jnp.concatenate([lse_ref[0, sb][0:1, :] for sb in range(nsub)], axis=1)
        dq = jnp.zeros((tq, DKP), F32)
        for start, size in chunks:
            rows = pl.ds(start, size)
            kc = k_ref[0, rows, :]
            p_t = jnp.exp2(_nt(kc, qb) - lse_r)
            ds_t = (p_t * (_nt(v_ref[0, rows, :], do) - delta_r)).astype(BF16)
            dv_ref[0, rows, :] += _nn(p_t.astype(BF16), do)
            dk_ref[0, rows, :] += _nn(ds_t, qb)
            dq += _tn(ds_t, kc)
        dq_ref[0] = dq * SCALE

    kvspec = lambda w: pl.BlockSpec((1, t_all, w), lambda h, i: (h, 0, 0))
    rowspec = pl.BlockSpec((1, nsub, 8, TB), lambda h, i: (h, i, 0, 0))
    qspec = lambda sb: pl.BlockSpec((1, TB, DKP), lambda h, i: (h, i * nsub + sb + off, 0))
    return pl.pallas_call(
        body, name="attn_bwd", grid=(NH, nq // nsub),
        in_specs=[qspec(sb) for sb in range(nsub)]
        + [kvspec(DKP), kvspec(DV), pl.BlockSpec((tq, DV), lambda h, i: (i, h)), pl.BlockSpec((tq, DV), lambda h, i: (i, h)),
           rowspec],
        out_specs=[pl.BlockSpec((1, tq, DKP), lambda h, i: (h, i, 0)), kvspec(DKP), kvspec(DV)],
        out_shape=[jax.ShapeDtypeStruct((NH, s_len, DKP), F32), jax.ShapeDtypeStruct((NH, t_all, DKP), F32),
                   jax.ShapeDtypeStruct((NH, t_all, DV), F32)],
        compiler_params=_params(("arbitrary", "arbitrary")),
    )(*([q] * nsub), k, v, dattn, attn, lse)


def _qkv_bwd(u, dq, dk, dv, cos, sin, q_lora_g, w_uq_t, kv_lora_g, w_ukv, qn_g, kn_g, s_len):
    t_all = u.shape[0]
    off = (t_all - s_len) // TB
    nb = t_all // TB

    def body(ulo_ref, dq_ref, dk_ref, dv_ref, cos_ref, sin_ref, qlg_ref, wuq_ref, kvlg_ref, wukv_ref, qng_ref, kng_ref,
             dlo_ref, dwuq_ref, dwukv_ref, dqlg_ref, dkvlg_ref, dqng_ref, dkng_ref):
        i = pl.program_id(0)

        @pl.when(i == 0)
        def _():
            for r in (dwuq_ref, dwukv_ref, dqlg_ref, dkvlg_ref, dqng_ref, dkng_ref):
                r[...] = jnp.zeros_like(r)

        latent = i >= off
        ulo = ulo_ref[...]
        cos, sin = _rope_block(cos_ref, sin_ref, pl.program_id(0) < off)
        cq = ulo[:, 0:QL]
        rc = lax.rsqrt(jnp.mean(cq * cq, axis=-1, keepdims=True) + EPS)
        cqh = cq * rc
        qlg = qlg_ref[...]
        cqn_b = (cqh * qlg).astype(BF16)
        qng = qng_ref[...]
        ckv = ulo[:, QL:QL + KVL]
        r0 = lax.rsqrt(jnp.mean(ckv * ckv, axis=-1, keepdims=True) + EPS)
        ckvh = ckv * r0
        kvlg = kvlg_ref[...]
        ckvn_b = (ckvh * kvlg).astype(BF16)
        qhs = [_nt(cqn_b, wuq_ref[hd]) for hd in range(NH)]
        kns = [_nn(ckvn_b, wukv_ref[hd])[:, :128] for hd in range(NH)]
        dqng = jnp.zeros((1, DKP), F32)
        dqraws = []
        for hd in range(NH):
            qh = qhs[hd]
            rq = lax.rsqrt(_rowsum(qh * qh) / DK + EPS)
            xh = qh * rq
            dqh = jnp.where(latent, dq_ref[hd], 0.0)
            dyq = jnp.concatenate([dqh[:, :128], _rope_t(dqh[:, 128:], cos, sin)], axis=1)
            dqng += _colsum(dyq * xh)
            dxh = dyq * qng
            dqraws.append((rq * (dxh - xh * (_rowsum(dxh * xh) / DK))).astype(BF16))
        dqng_ref[...] += dqng

        kr = ulo[:, 384:512]
        skr = _rowsum(kr * kr)
        kng = kng_ref[...]
        dkr = jnp.zeros((TB, 128), F32)
        dkng = jnp.zeros((1, DKP), F32)
        dkvs = []
        for hd in range(NH):
            kn = kns[hd]
            rk = lax.rsqrt((_rowsum(kn * kn) + skr) / DK + EPS)
            xh1 = kn * rk
            xh2 = kr * rk
            dkh = dk_ref[hd] * LN2
            d1 = dkh[:, :128]
            d2 = _rope_t(dkh[:, 128:], cos, sin)
            dkng += jnp.concatenate([_colsum(d1 * xh1), _colsum(d2 * xh2)], axis=1)
            dx1 = d1 * kng[:, :128]
            dx2 = d2 * kng[:, 128:]
            dot = (_rowsum(dx1 * xh1) + _rowsum(dx2 * xh2)) / DK
            dkvs.append(jnp.concatenate([rk * (dx1 - xh1 * dot), dv_ref[hd]], axis=1).astype(BF16))
            dkr += rk * (dx2 - xh2 * dot)
        dkng_ref[...] += dkng

        dcqn = jnp.zeros((TB, QL), F32)
        dckvn = jnp.zeros((TB, KVL), F32)
        for hd in range(NH):
            dwuq_ref[hd] += _tn(dqraws[hd], cqn_b)[:DK]
            dcqn += _nn(dqraws[hd], wuq_ref[hd])
            dwukv_ref[hd] += _tn(ckvn_b, dkvs[hd])
            dckvn += _nt(dkvs[hd], wukv_ref[hd])
        dqlg_ref[...] += _colsum(dcqn * cqh)
        dxh = dcqn * qlg
        dcq = rc * (dxh - cqh * jnp.mean(dxh * cqh, axis=-1, keepdims=True))
        dkvlg_ref[...] += _colsum(dckvn * ckvh)
        dxh = dckvn * kvlg
        dckv = r0 * (dxh - ckvh * jnp.mean(dxh * ckvh, axis=-1, keepdims=True))
        dlo_ref[...] = jnp.concatenate([dcq, dckv, dkr], axis=1).astype(BF16)

    row = lambda w: pl.BlockSpec((TB, w), lambda i: (i, 0))
    heads = lambda w: pl.BlockSpec((NH, TB, w), lambda i: (0, i, 0))
    return pl.pallas_call(
        body, name="qkv_bwd", grid=(nb,),
        in_specs=[row(512), pl.BlockSpec((NH, TB, DKP), lambda i: (0, jnp.maximum(i - off, 0), 0)), heads(DKP), heads(DV),
                  pl.BlockSpec((1, 8, 256), lambda i: (jnp.maximum(i - off, 0), 0, 0)), _full((TB, 256)), _full((1, QL)), _full((NH, DKP, QL)), _full((1, KVL)), _full((NH, KVL, 256)),
                  _full((1, DKP)), _full((1, DKP))],
        out_specs=[row(512), _full((NH, DK, QL)), _full((NH, KVL, 256)), _full((1, QL)), _full((1, KVL)),
                   _full((1, DKP)), _full((1, DKP))],
        out_shape=[jax.ShapeDtypeStruct((t_all, 512), BF16), jax.ShapeDtypeStruct((NH, DK, QL), F32),
                   jax.ShapeDtypeStruct((NH, KVL, 256), F32), jax.ShapeDtypeStruct((1, QL), F32),
                   jax.ShapeDtypeStruct((1, KVL), F32), jax.ShapeDtypeStruct((1, DKP), F32), jax.ShapeDtypeStruct((1, DKP), F32)],
        compiler_params=_params(("arbitrary",)),
    )(u, dq, dk, dv, cos, sin, q_lora_g, w_uq_t, kv_lora_g, w_ukv, qn_g, kn_g)


def _in_bwd(ctx, x, modsel, norm_g, dlo, dga, dgp, dpool, dxn, w_in_t):
    s_len, lc = x.shape[0], ctx.shape[0]
    t_all = s_len + lc
    off = lc // TB
    nb = t_all // TB
    nq = s_len // TB
    hb = TB // HALO
    n = TB + 2 * HALO

    def body(ctx_ref, x_ref, mod_ref, ng_ref, dlo_ref, dga_ref, dgp_ref, dp_ref, dpprev_ref, dpnext_ref, dxn_ref, win_ref,
             gx_ref, dwin_ref, dmod_ref, dng_ref):
        i = pl.program_id(0)
        j = i - off

        @pl.when(i == 0)
        def _():
            dwin_ref[...] = jnp.zeros_like(dwin_ref)
            dmod_ref[...] = jnp.zeros_like(dmod_ref)
            dng_ref[...] = jnp.zeros_like(dng_ref)

        latent = i >= off
        dp = dp_ref[...]
        prev = jnp.where(j <= 0, 0.0, dpprev_ref[...])
        nxt = jnp.where(j >= nq - 1, 0.0, dpnext_ref[...])
        win = jnp.concatenate([prev, dp, nxt], axis=0)
        tg = j * TB - HALO + lax.broadcasted_iota(jnp.int32, (n, 1), 0)
        dpin = []
        for g, w in enumerate(POOL_WINDOWS):
            cnt = jnp.maximum(jnp.minimum(tg + w // 2, s_len) - jnp.maximum(tg - w // 2, 0), 1).astype(F32)
            zq = win[:, g * 128:(g + 1) * 128] / cnt
            zq = zq + _shift_rows(zq, 1)
            for step in (1, 2, 4):
                if w >= 4 * step:
                    zq = _shift_rows(zq, -step) + _shift_rows(zq, step)
            dpin.append(zq[HALO:HALO + TB] - dp[:, g * 128:(g + 1) * 128])
        zero = jnp.zeros((TB, 512), BF16)
        du = [dlo_ref[...], jnp.where(latent, dga_ref[...], zero),
              jnp.where(latent, jnp.concatenate(dpin, axis=1).astype(BF16), zero), jnp.where(latent, dgp_ref[...], zero)]

        ng = ng_ref[...]
        xb = jnp.where(i < off, ctx_ref[...], x_ref[...])
        r, xh, xg, h, scale = _modulated(xb, mod_ref, ng)
        hb_ = h.astype(BF16)
        dh = jnp.zeros((TB, D), F32)
        for s, (lo, hi) in enumerate(SEG):
            dwin_ref[lo:hi, :] += _tn(du[s], hb_)
            dh += _nn(du[s], win_ref[lo:hi, :])
        is_lat = latent.astype(F32)
        dsh = _colsum(dh)
        dsc = _colsum(dh * xg)
        dmod_ref[0, 0:1, :] += dsh * (1.0 - is_lat)
        dmod_ref[0, 1:2, :] += dsc * (1.0 - is_lat)
        dmod_ref[1, 0:1, :] += dsh * is_lat
        dmod_ref[1, 1:2, :] += dsc * is_lat
        dxg = dh * (1.0 + scale)
        dng_ref[...] += _colsum(dxg * xh)
        dxh = dxg * ng
        gx_ref[...] = r * (dxh - xh * jnp.mean(dxh * xh, axis=-1, keepdims=True)) + dxn_ref[...]

    row = lambda w: pl.BlockSpec((TB, w), lambda i: (i, 0))
    lat = lambda w: pl.BlockSpec((TB, w), lambda i: (jnp.maximum(i - off, 0), 0))
    last8 = s_len // HALO - 1
    cspec, xspec, mspec = _token_specs(off)
    return pl.pallas_call(
        body, name="in_bwd", grid=(nb,),
        in_specs=[cspec, xspec, mspec, _full((1, D)), row(512), lat(512), lat(512), lat(512),
                  pl.BlockSpec((HALO, 512), lambda i: (jnp.maximum(jnp.maximum(i - off, 0) * hb - 1, 0), 0)),
                  pl.BlockSpec((HALO, 512), lambda i: (jnp.minimum((jnp.maximum(i - off, 0) + 1) * hb, last8), 0)),
                  lat(D), _full((DIN, D))],
        out_specs=[lat(D), _full((DIN, D)), _full((2, 2, D)), _full((1, D))],
        out_shape=[jax.ShapeDtypeStruct((s_len, D), F32), jax.ShapeDtypeStruct((DIN, D), F32),
                   jax.ShapeDtypeStruct((2, 2, D), F32), jax.ShapeDtypeStruct((1, D), F32)],
        compiler_params=_params(("arbitrary",)),
    )(ctx, x, modsel, norm_g, dlo, dga, dgp, dpool, dpool, dpool, dxn, w_in_t)


def _adamw_update(w_ref, g_ref, m_ref, v_ref, d_ref, mo_ref, vo_ref):
    gv = g_ref[...]
    mn = ADAM_B1 * m_ref[...] + (1.0 - ADAM_B1) * gv
    vn = ADAM_B2 * v_ref[...] + (1.0 - ADAM_B2) * (gv * gv)
    m_hat = mn / (1.0 - ADAM_B1 ** ADAM_STEP)
    v_hat = vn / (1.0 - ADAM_B2 ** ADAM_STEP)
    d_ref[...] = -ADAM_LR * (m_hat / (jnp.sqrt(v_hat) + ADAM_EPS) + ADAM_WD * w_ref[...])
    mo_ref[...] = mn
    vo_ref[...] = vn


def _adamw_many(ws, gs, ms, vs):
    n = len(ws)

    def body(*refs):
        for i in range(n):
            _adamw_update(refs[i], refs[n + i], refs[2 * n + i], refs[3 * n + i], refs[4 * n + i], refs[5 * n + i], refs[6 * n + i])

    def spec(w):
        rows, cols = w.shape
        return pl.BlockSpec((rows // 2, cols), lambda i: (i, 0)) if rows % 16 == 0 else _full((rows, cols))

    specs = [spec(w) for w in ws]
    shp = [jax.ShapeDtypeStruct(w.shape, F32) for w in ws]
    out = pl.pallas_call(body, name="adamw_many", grid=(2,), in_specs=specs * 4, out_specs=specs * 3, out_shape=shp * 3,
                         compiler_params=_params(("arbitrary",)))(*ws, *gs, *ms, *vs)
    return out[:n], out[n:2 * n], out[2 * n:]


def _adamw(w, g, m, v, name):
    rows, cols = w.shape
    rb = next(r for r in range(min(rows, 256), 0, -8) if rows % r == 0)

    def body(w_ref, g_ref, m_ref, v_ref, d_ref, mo_ref, vo_ref):
        _adamw_update(w_ref, g_ref, m_ref, v_ref, d_ref, mo_ref, vo_ref)

    spec = pl.BlockSpec((rb, cols), lambda i: (i, 0))
    shp = jax.ShapeDtypeStruct((rows, cols), F32)
    return pl.pallas_call(
        body, name=name, grid=(rows // rb,), in_specs=[spec] * 4, out_specs=[spec] * 3, out_shape=[shp] * 3,
        compiler_params=_params(("arbitrary",)),
    )(w, g, m, v)


class _Links:
    def __init__(self, send_sems, recv_sems):
        self.send_sems, self.recv_sems, self.sends = send_sems, recv_sems, []

    def send(self, src, dst, sem, to):
        cp = pltpu.make_async_remote_copy(src, dst, self.send_sems.at[sem], self.recv_sems.at[sem], device_id=to,
                                          device_id_type=MESH)
        cp.start()
        self.sends.append(cp)

    def arrived(self, dst, sem, frm):
        pltpu.make_async_remote_copy(dst, dst, self.send_sems.at[sem], self.recv_sems.at[sem], device_id=frm,
                                     device_id_type=MESH).wait_recv()

    def drain(self):
        for cp in self.sends:
            cp.wait_send()


def _half(ref, c, axis):
    size = ref.shape[axis - 2] // 2
    win = pl.ds(pl.multiple_of(c * size, 16 if axis == 0 else 128), size)
    idx = (win, slice(None)) if axis == 0 else (slice(None), win)
    return ref.at[(slice(None),) * (len(ref.shape) - 2) + idx]


def _select_rows(slots_ref, n_slots, row=0):
    sub = lax.broadcasted_iota(jnp.int32, (8, 1), 0)
    out = None
    for d in range(n_slots):
        r = jnp.where(sub == d, jnp.broadcast_to(slots_ref[d][row:row + 1, :], (8, slots_ref.shape[-1])), 0.0)
        out = r if out is None else out + r
    return out


def _gather(c, c_ctx, w_mod, b_mod_k, shards, axes, slab_rows):
    nw = len(shards)
    kw = w_mod.shape[1]

    def body(*refs):
        c_ref, cc_ref, wm_ref, b_ref = refs[:4]
        w_refs = refs[4:4 + nw]
        a16_ref, mod_ref = refs[4 + nw:6 + nw]
        g_refs = refs[6 + nw:6 + 2 * nw]
        a_ref, send_sems, recv_sems = refs[6 + 2 * nw:]

        def slab(wi, chip):
            return g_refs[wi].at[chip].at[0:shards[wi].shape[0]]
        x, y, cc = lax.axis_index("x"), lax.axis_index("y"), lax.axis_index("c")
        me = 4 * x + 2 * y + cc
        k = 2 * x + y
        sibling = (x, y, 1 - cc)
        links = _Links(send_sems, recv_sems)
        chips = [_peer(x, y, cc, off + (0,)) for off in CHIPS3]
        chip_a = ((x + 1 - cc) % 2, (y + cc) % 2, cc)
        chip_b = ((x + cc) % 2, (y + 1 - cc) % 2, cc)
        chip_d = (1 - x, 1 - y, cc)
        cv = c_ref[...]
        a_ref[me] = jnp.broadcast_to(cv * _sig(cv), (8, D))
        for j, off in enumerate(PEERS7):
            links.send(a_ref.at[me], a_ref.at[me], j, _peer(x, y, cc, off))
        for wi in range(nw):
            slab(wi, k)[...] = w_refs[wi][...].astype(BF16)
            for j, to in enumerate((chip_a, chip_b)):
                links.send(_half(slab(wi, k), cc, axes[wi]), _half(slab(wi, k), cc, axes[wi]), 10 + wi * 6 + j, to)
            pad = slab_rows[wi] - shards[wi].shape[0]
            if pad:
                for kk in range(4):
                    g_refs[wi][kk, shards[wi].shape[0]:, :] = jnp.zeros((pad, shards[wi].shape[1]), BF16)
        for j, off in enumerate(PEERS7):
            px, py, pc = _peer(x, y, cc, off)
            links.arrived(a_ref.at[4 * px + 2 * py + pc], j, (px, py, pc))
        ccv = cc_ref[...]
        sub = lax.broadcasted_iota(jnp.int32, (8, 1), 0)
        a16 = jnp.concatenate([_select_rows(a_ref, 8), jnp.where(sub == 0, jnp.broadcast_to(ccv * _sig(ccv), (8, D)), 0.0)], axis=0)
        a16_ref[...] = a16
        mod_ref[k] = _dot3(_nn, a16, wm_ref[...]) + b_ref[...]
        for j, to in enumerate(chips):
            links.send(mod_ref.at[k], mod_ref.at[k], 7 + j, to)
        for j, (frm, origin) in enumerate(((chip_a, chip_a), (chip_b, chip_b), (chip_b, chip_d))):
            for wi in range(nw):
                blk = _half(slab(wi, 2 * origin[0] + origin[1]), cc, axes[wi])
                links.arrived(blk, 10 + wi * 6 + j, frm)
                if j == 0:
                    links.send(blk, blk, 10 + wi * 6 + 2, chip_b)
                links.send(blk, blk, 10 + wi * 6 + 3 + j, sibling)
        for j, (px, py, pc) in enumerate(chips):
            links.arrived(mod_ref.at[2 * px + py], 7 + j, (px, py, pc))
        for j, origin in enumerate((chip_b, chip_a, chip_d)):
            for wi in range(nw):
                links.arrived(_half(slab(wi, 2 * origin[0] + origin[1]), 1 - cc, axes[wi]), 10 + wi * 6 + 3 + j, sibling)
        links.drain()

    nsem = 10 + 6 * nw
    return pl.pallas_call(
        body, name="gather", in_specs=[VM] * (4 + nw), out_specs=[VM] * (2 + nw),
        out_shape=[jax.ShapeDtypeStruct((16, D), F32), jax.ShapeDtypeStruct((4, 16, kw), F32)]
        + [jax.ShapeDtypeStruct((4, r, s.shape[1]), BF16) for r, s in zip(slab_rows, shards)],
        scratch_shapes=[pltpu.VMEM((8, 8, D), F32), pltpu.SemaphoreType.DMA((nsem,)), pltpu.SemaphoreType.DMA((nsem,))],
        compiler_params=pltpu.CompilerParams(vmem_limit_bytes=VMEM_LIMIT),
    )(c, c_ctx, w_mod, b_mod_k, *shards)


SMALL_ROW_WIDTHS = (D, QL, KVL, DKP, DKP, 512, 128)
SMALL_OUT_WIDTHS = (D, QL, KVL, DK, DK, 512, 1)


def _reduce(grads, axes, smalls, w_pool_g, dmod8, a16, w_mod, c_ctx):
    nw = len(grads)
    ns = len(smalls)
    kw = w_mod.shape[1]
    halves = []
    for g, ax in zip(grads, axes):
        halves.append((g.shape[1] // 2, g.shape[2]) if ax == 0 else (g.shape[1], g.shape[2] // 2))

    def body(*refs):
        g_refs = refs[:nw]
        small_refs = refs[nw:nw + ns]
        wp_ref, dm_ref, a16_ref, wm_ref, cc_ref = refs[nw + ns:nw + ns + 5]
        o = nw + ns + 5
        r_refs = refs[o:o + nw]
        small_outs = refs[o + nw:o + nw + ns]
        rwp_ref, gw_ref, gb_ref, gc_ref = refs[o + nw + ns:o + nw + ns + 4]
        o = o + nw + ns + 4
        own, sib, part, got = (refs[o + i * nw:o + (i + 1) * nw] for i in range(4))
        smbuf, wps, wpg, dm_all, pc_all, send_sems, recv_sems, local_sems = refs[o + 4 * nw:]
        x, y, cc = lax.axis_index("x"), lax.axis_index("y"), lax.axis_index("c")
        me = 4 * x + 2 * y + cc
        k = 2 * x + y
        sibling = (x, y, 1 - cc)
        links = _Links(send_sems, recv_sems)
        chips = [_peer(x, y, cc, off + (0,)) for off in CHIPS3]
        peers = [_peer(x, y, cc, off) for off in PEERS7]
        big, sm0, wp0, dm0, pc0 = 0, 5 * nw, 5 * nw + 7, 5 * nw + 14, 5 * nw + 21

        locals_ = []
        for wi in range(nw):
            lc = pltpu.make_async_copy(_half(g_refs[wi], cc, axes[wi]), own[wi], local_sems.at[wi])
            lc.start()
            locals_.append(lc)
            links.send(_half(g_refs[wi], 1 - cc, axes[wi]), sib[wi], big + wi * 5, sibling)
        slot = smbuf.at[me]
        slot[...] = jnp.zeros((8, D), F32)
        for r, (ref, w) in enumerate(zip(small_refs, SMALL_ROW_WIDTHS)):
            slot[r:r + 1, 0:w] = jnp.broadcast_to(ref[...], (1, w))
        links.send(wp_ref, wps, wp0, sibling)
        dm_all[me] = dm_ref[...]
        for j, peer in enumerate(peers):
            links.send(dm_all.at[me], dm_all.at[me], dm0 + j, peer)
            links.send(smbuf.at[me], smbuf.at[me], sm0 + j, peer)
        links.arrived(wps, wp0, sibling)
        wpg[k] = (wp_ref[...] + wps[...]).astype(BF16)
        for j, to in enumerate(chips):
            links.send(wpg.at[k], wpg.at[k], wp0 + 1 + j, to)
        for wi in range(nw):
            locals_[wi].wait()
            links.arrived(sib[wi], big + wi * 5, sibling)
            part[wi][...] = (own[wi][...] + sib[wi][...]).astype(BF16)
            got[wi][k] = part[wi][k]
            for j, (px, py, pc) in enumerate(chips):
                links.send(part[wi].at[2 * px + py], got[wi].at[k], big + wi * 5 + 1 + j, (px, py, pc))
        for j, (px, py, pc) in enumerate(peers):
            links.arrived(dm_all.at[4 * px + 2 * py + pc], dm0 + j, (px, py, pc))
        dmc = dm_all[0][1:2, :]
        dml = dm_all[0][0:1, :]
        for d in range(1, 8):
            dmc = dmc + dm_all[d][1:2, :]
            dml = dml + dm_all[d][0:1, :]
        gb_ref[...] = dml + dmc
        sub = lax.broadcasted_iota(jnp.int32, (8, 1), 0)
        b16 = jnp.concatenate([_select_rows(dm_all, 8), jnp.where(sub == 0, jnp.broadcast_to(dmc, (8, 3 * D)), 0.0)], axis=0)
        bk = jnp.zeros((16, kw), F32)
        for kk in range(4):
            bk = bk + jnp.where(k == kk, b16[:, kk * kw:(kk + 1) * kw], 0.0)
        gw_ref[...] = _dot3(_tn, a16_ref[...], bk)
        pc_all[k] = _dot3(_nt, jnp.broadcast_to(bk[8:9, :], (8, kw)), wm_ref[...])
        for j, to in enumerate(chips):
            links.send(pc_all.at[k], pc_all.at[k], pc0 + j, to)
        for j, (px, py, pc) in enumerate(peers):
            links.arrived(smbuf.at[4 * px + 2 * py + pc], sm0 + j, (px, py, pc))
        tot = smbuf[0]
        for d in range(1, 8):
            tot = tot + smbuf[d]
        for r, (ref, w) in enumerate(zip(small_outs, SMALL_OUT_WIDTHS)):
            ref[...] = tot[r:r + 1, 0:w]
        for j, (px, py, pc) in enumerate(chips):
            links.arrived(wpg.at[2 * px + py], wp0 + 1 + j, (px, py, pc))
        wpt = wpg[0].astype(F32)
        for kk in range(1, 4):
            wpt = wpt + wpg[kk].astype(F32)
        rwp_ref[...] = wpt
        for wi in range(nw):
            for j, (px, py, pc) in enumerate(chips):
                links.arrived(got[wi].at[2 * px + py], big + wi * 5 + 1 + j, (px, py, pc))
            total = got[wi][0].astype(F32)
            for kk in range(1, 4):
                total = total + got[wi][kk].astype(F32)
            mine = _half(r_refs[wi], cc, axes[wi])
            mine[...] = total
            links.send(mine, mine, big + wi * 5 + 4, sibling)
        for j, (px, py, pc) in enumerate(chips):
            links.arrived(pc_all.at[2 * px + py], pc0 + j, (px, py, pc))
        ccv = cc_ref[...]
        sg = _sig(ccv)
        gc_ref[...] = (pc_all[0][0:1, :] + pc_all[1][0:1, :] + pc_all[2][0:1, :] + pc_all[3][0:1, :]) * (sg * (1.0 + ccv * (1.0 - sg)))
        for wi in range(nw):
            links.arrived(_half(r_refs[wi], 1 - cc, axes[wi]), big + wi * 5 + 4, sibling)
        links.drain()

    nsem = 5 * nw + 24
    quads = [(4,) + h for h in halves]
    return pl.pallas_call(
        body, name="reduce", in_specs=[ANY] * nw + [VM] * (ns + 5), out_specs=[VM] * (nw + ns + 4),
        out_shape=[jax.ShapeDtypeStruct(g.shape[1:], F32) for g in grads]
        + [jax.ShapeDtypeStruct((1, w), F32) for w in SMALL_OUT_WIDTHS]
        + [jax.ShapeDtypeStruct(w_pool_g.shape, F32), jax.ShapeDtypeStruct((D, kw), F32), jax.ShapeDtypeStruct((1, 3 * D), F32),
           jax.ShapeDtypeStruct((1, D), F32)],
        scratch_shapes=[pltpu.VMEM(q, F32) for q in quads] + [pltpu.VMEM(q, F32) for q in quads]
        + [pltpu.VMEM(q, BF16) for q in quads] + [pltpu.VMEM(q, BF16) for q in quads]
        + [pltpu.VMEM((8, 8, D), F32), pltpu.VMEM(w_pool_g.shape, F32), pltpu.VMEM((4,) + w_pool_g.shape, BF16),
           pltpu.VMEM((8, 8, 3 * D), F32),
           pltpu.VMEM((4, 8, D), F32)]
        + [pltpu.SemaphoreType.DMA((nsem,)), pltpu.SemaphoreType.DMA((nsem,)), pltpu.SemaphoreType.DMA((nw,))],
        compiler_params=pltpu.CompilerParams(vmem_limit_bytes=VMEM_LIMIT),
    )(*grads, *smalls, w_pool_g, dmod8, a16, w_mod, c_ctx)


def _rope_tables(s_len):
    rows = s_len // GRID_W
    per = TB // GRID_W
    n_freq = 16
    inv = ROPE_BASE ** (-jnp.arange(n_freq, dtype=F32) / n_freq)
    ang_r = jnp.arange(rows, dtype=F32)[:, None] * inv
    ang_c = jnp.arange(GRID_W, dtype=F32)[:, None] * inv
    by_row, by_col = [], []
    for fn, pad in ((jnp.cos, 1.0), (jnp.sin, 0.0)):
        r = jnp.concatenate([fn(ang_r), fn(ang_r), jnp.zeros((rows, 96), F32)], axis=1).reshape(rows // per, per, 128)
        by_row.append(jnp.pad(r, ((0, 0), (0, 8 - per), (0, 0))))
        cpart = jnp.concatenate([jnp.zeros((GRID_W, 32), F32), fn(ang_c), fn(ang_c), jnp.full((GRID_W, 64), pad, F32)], axis=1)
        by_col.append(jnp.tile(cpart, (per, 1)))
    return jnp.concatenate(by_row, axis=-1), jnp.concatenate(by_col, axis=-1)


def kernel(x, c, ctx, c_ctx, w_mod, b_mod, norm_g, w_in, q_lora_g, w_uq, kv_lora_g, w_ukv, q_norm_g, k_norm_g, w_pool, pool_scale, w_out, loss_target, m_c_ctx, m_w_mod, m_b_mod, m_norm_g, m_w_in, m_q_lora_g, m_w_uq, m_kv_lora_g, m_w_ukv, m_q_norm_g, m_k_norm_g, m_w_pool, m_pool_scale, m_w_out, v_c_ctx, v_w_mod, v_b_mod, v_norm_g, v_w_in, v_q_lora_g, v_w_uq, v_kv_lora_g, v_w_ukv, v_q_norm_g, v_k_norm_g, v_w_pool, v_pool_scale, v_w_out):
    xi, yi, ci = lax.axis_index("x"), lax.axis_index("y"), lax.axis_index("c")
    me = 4 * xi + 2 * yi + ci
    k = 2 * xi + yi
    s_len = x.shape[1]
    lc = ctx.shape[1]
    kw = w_mod.shape[2]
    weights = dict(c_ctx=c_ctx, w_mod=w_mod, b_mod=b_mod, norm_g=norm_g, w_in=w_in, q_lora_g=q_lora_g, w_uq=w_uq,
                   kv_lora_g=kv_lora_g, w_ukv=w_ukv, q_norm_g=q_norm_g, k_norm_g=k_norm_g, w_pool=w_pool,
                   pool_scale=pool_scale, w_out=w_out)
    m_in = dict(c_ctx=m_c_ctx, w_mod=m_w_mod, b_mod=m_b_mod, norm_g=m_norm_g, w_in=m_w_in, q_lora_g=m_q_lora_g, w_uq=m_w_uq,
                kv_lora_g=m_kv_lora_g, w_ukv=m_w_ukv, q_norm_g=m_q_norm_g, k_norm_g=m_k_norm_g, w_pool=m_w_pool,
                pool_scale=m_pool_scale, w_out=m_w_out)
    v_in = dict(c_ctx=v_c_ctx, w_mod=v_w_mod, b_mod=v_b_mod, norm_g=v_norm_g, w_in=v_w_in, q_lora_g=v_q_lora_g, w_uq=v_w_uq,
                kv_lora_g=v_kv_lora_g, w_ukv=v_w_ukv, q_norm_g=v_q_norm_g, k_norm_g=v_k_norm_g, w_pool=v_w_pool,
                pool_scale=v_pool_scale, w_out=v_w_out)
    order = ["c_ctx", "w_mod", "b_mod", "norm_g", "w_in", "q_lora_g", "w_uq", "kv_lora_g", "w_ukv", "q_norm_g", "k_norm_g",
             "w_pool", "pool_scale", "w_out"]
    transposed = ("w_in", "w_uq")
    as2d = lambda n, a: jnp.transpose(a[0]) if n in transposed else a.reshape(-1, a.shape[-1])
    back = lambda n, a: jnp.transpose(a)[None] if n in transposed else a.reshape(weights[n].shape)

    c_ctx2 = c_ctx.reshape(1, D)
    b_mod_k = lax.dynamic_slice(b_mod, (0, k * kw), (1, kw))
    split = (1, 0, 0, 0)
    a16, mod_all, g_in, g_uq, g_ukv, g_out = _gather(
        c, c_ctx2, w_mod[0], b_mod_k, [as2d("w_in", w_in), as2d("w_uq", w_uq), w_ukv[0], w_out[0]], split,
        (DIN // 4, DKP, KVL, D // 4))
    mod_me = lax.dynamic_index_in_dim(mod_all, me, axis=1, keepdims=False).reshape(3, D)
    mod_c = mod_all[:, 8, :].reshape(3, D)
    modsel = jnp.stack([mod_c, mod_me])
    w_in_t = g_in.reshape(DIN, D)
    w_uq_t = g_uq
    w_out_f = g_out.reshape(D, D)
    qn_g = jnp.pad(q_norm_g, ((0, 0), (0, DKP - DK)))
    kn_g = jnp.pad(k_norm_g, ((0, 0), (0, DKP - DK)))
    w_pool_b = w_pool[0].astype(BF16)
    cos, sin = _rope_tables(s_len)

    u, q, kk, v = _fwd_in(ctx[0], x[0], modsel, norm_g, w_in_t, q_lora_g, w_uq_t, kv_lora_g, g_ukv, qn_g, kn_g, cos, sin)
    attn, lse = _attn_fwd(q, kk, v, s_len)
    (dxn, dattn, dga, dgp, dpool, dw_out, dgate, dps, dw_pool, loss) = _out_stage(
        attn.reshape(s_len // Q_BLOCK, Q_BLOCK, NH * DV), u, x[0], loss_target[0], modsel[1, 2:3, :], w_pool_b, pool_scale,
        w_out_f, lc)
    dattn = dattn.reshape(s_len, NH * DV)
    dq, dk, dv = _attn_bwd(q, kk, v, dattn, attn, lse, s_len)
    dlo, dw_uq_t, dw_ukv, dqlg, dkvlg, dqng, dkng = _qkv_bwd(u, dq, dk, dv, cos, sin, q_lora_g, w_uq_t, kv_lora_g, g_ukv,
                                                            qn_g, kn_g, s_len)
    gx, dw_in_t, dmod, dng = _in_bwd(ctx[0], x[0], modsel, norm_g, dlo, dga, dgp, dpool, dxn, w_in_t)

    dmod_l = jnp.concatenate([dmod[1, 0], dmod[1, 1], dgate[0]]).reshape(1, 3 * D)
    dmod_c = jnp.concatenate([dmod[0, 0], dmod[0, 1], jnp.zeros((D,), F32)]).reshape(1, 3 * D)
    dmod8 = jnp.concatenate([dmod_l, dmod_c, jnp.zeros((6, 3 * D), F32)], axis=0)
    (r_in, r_uq, r_ukv, r_out, g_ng, g_qlg, g_kvlg, g_qng, g_kng, g_ps, loss_all, g_wp, g_w_mod, g_b_mod, g_c_ctx) = _reduce(
        [dw_in_t.reshape(4, DIN // 4, D), dw_uq_t, dw_ukv, dw_out.reshape(4, D // 4, D)], split,
        [dng, dqlg, dkvlg, dqng, dkng, dps, loss], dw_pool, dmod8, a16, w_mod[0], c_ctx2)
    g2d = dict(c_ctx=g_c_ctx, b_mod=g_b_mod, w_mod=g_w_mod, w_in=r_in, w_uq=r_uq, w_ukv=r_ukv, w_out=r_out, norm_g=g_ng,
               q_lora_g=g_qlg, kv_lora_g=g_kvlg, q_norm_g=g_qng, k_norm_g=g_kng, pool_scale=g_ps, w_pool=g_wp.reshape(512, 128))

    d2d, m2d, v2d = {}, {}, {}
    d2d["w_mod"], m2d["w_mod"], v2d["w_mod"] = _adamw(as2d("w_mod", w_mod), g2d["w_mod"], as2d("w_mod", m_w_mod),
                                                      as2d("w_mod", v_w_mod), "adamw_w_mod")
    rest = [n for n in order if n != "w_mod"]
    outs = _adamw_many([as2d(n, weights[n]) for n in rest], [g2d[n] for n in rest], [as2d(n, m_in[n]) for n in rest],
                       [as2d(n, v_in[n]) for n in rest])
    for dst, arrs in zip((d2d, m2d, v2d), outs):
        dst.update(dict(zip(rest, arrs)))

    return (loss_all[0, 0], gx[None], *[back(n, g2d[n]) for n in order], *[back(n, d2d[n]) for n in order],
            *[back(n, m2d[n]) for n in order], *[back(n, v2d[n]) for n in order])
```

```python
import jax
import jax.numpy as jnp
from jax import lax
from jax.experimental import pallas as pl
from jax.experimental.pallas import tpu as pltpu

F32 = jnp.float32
BF16 = jnp.bfloat16
MESH = pl.DeviceIdType.MESH

D = 1024
NH = 4
DK = 192
DKP = 256
DV = 128
QL = 256
KVL = 128
DIN = 1984
U_LO = 448
SEG = ((0, 512), (448, 960), (960, 1472), (1472, 1984))
DU = 2048
POOL_WINDOWS = (2, 4, 8, 16)
HALO = 8
EPS = 1e-6
ROPE_BASE = 10000.0
GRID_W = 64
Q_BLOCK = 128
TB = 256
BWD_QBLOCKS = 1
SCALE = DK ** -0.5
LOG2E = 1.4426950408889634
LN2 = 0.6931471805599453
VMEM_LIMIT = 56 * 1024 * 1024

ADAM_LR = 0.001
ADAM_B1 = 0.9
ADAM_B2 = 0.999
ADAM_EPS = 1e-08
ADAM_WD = 0.01
ADAM_STEP = 10

CHIPS3 = ((1, 0), (0, 1), (1, 1))
PEERS7 = tuple((dx, dy, dc) for dx in (0, 1) for dy in (0, 1) for dc in (0, 1) if (dx, dy, dc) != (0, 0, 0))

VM = pl.BlockSpec(memory_space=pltpu.VMEM)
ANY = pl.BlockSpec(memory_space=pl.ANY)


def _nn(a, b):
    return jnp.dot(a, b, preferred_element_type=F32)


def _nt(a, b):
    return lax.dot_general(a, b, (((1,), (1,)), ((), ())), preferred_element_type=F32)


def _tn(a, b):
    return lax.dot_general(a, b, (((0,), (0,)), ((), ())), preferred_element_type=F32)


def _split3(a):
    a0 = a.astype(BF16)
    r = a - a0.astype(F32)
    a1 = r.astype(BF16)
    a2 = (r - a1.astype(F32)).astype(BF16)
    return a0, a1, a2


def _dot3(dot, a, b):
    sa = _split3(a)
    sb = _split3(b)
    out = None
    for i in range(3):
        for j in range(3 - i):
            t = dot(sa[i], sb[j])
            out = t if out is None else out + t
    return out


def _sig(x):
    return 1.0 / (1.0 + jnp.exp(-x))


def _rot(t):
    src = lax.broadcasted_iota(jnp.int32, (128, 128), 0)
    dst = lax.broadcasted_iota(jnp.int32, (128, 128), 1)
    first = (dst % 32) < 16
    perm = jnp.where(first & (src == dst + 16), -1.0, jnp.where(~first & (src == dst - 16), 1.0, 0.0)).astype(BF16)
    hi = t.astype(BF16)
    lo = (t - hi.astype(F32)).astype(BF16)
    return _nn(hi, perm) + _nn(lo, perm)


def _rope(t, cos, sin):
    return t * cos + _rot(t) * sin


def _rope_t(t, cos, sin):
    return t * cos - _rot(t * sin)


def _rope_block(rows_ref, cols_ref, is_ctx):
    lane = lax.broadcasted_iota(jnp.int32, (TB, 256), 1) % 128
    rows = jnp.concatenate([jnp.broadcast_to(rows_ref[0, r:r + 1, :], (GRID_W, 256)) for r in range(TB // GRID_W)], axis=0)
    cs = jnp.where(lane < 32, rows, cols_ref[...])
    return jnp.where(is_ctx, 1.0, cs[:, :128]), jnp.where(is_ctx, 0.0, cs[:, 128:])


def _shift_rows(z, k):
    n = z.shape[0]
    return pltpu.roll(z, (n - k) % n, 0)


def _colsum(a):
    return jnp.sum(a, axis=0, keepdims=True)


def _rowsum(a):
    return jnp.sum(a, axis=-1, keepdims=True)


def _row_layout(col):
    return jnp.transpose(jnp.broadcast_to(col, (col.shape[0], 128)))[0:8, :]


def _params(sem=None):
    return pltpu.CompilerParams(dimension_semantics=sem, vmem_limit_bytes=VMEM_LIMIT)


def _full(shape):
    nd = len(shape)
    return pl.BlockSpec(shape, lambda *_: (0,) * nd)


def _peer(x, y, c, off):
    dx, dy, dc = off
    return ((x + dx) % 2, (y + dy) % 2, (c + dc) % 2)


def _token_specs(off):
    ctx = pl.BlockSpec((TB, D), lambda i: (jnp.minimum(i, off - 1), 0))
    lat = pl.BlockSpec((TB, D), lambda i: (jnp.maximum(i - off, 0), 0))
    mod = pl.BlockSpec((1, 3, D), lambda i: (jnp.minimum(i // off, 1), 0, 0))
    return ctx, lat, mod


def _modulated(x, mod_ref, ng):
    shift = mod_ref[0, 0:1, :]
    scale = mod_ref[0, 1:2, :]
    r = lax.rsqrt(jnp.mean(x * x, axis=-1, keepdims=True) + EPS)
    xh = x * r
    xg = xh * ng
    return r, xh, xg, xg * (1.0 + scale) + shift, scale


def _fwd_in(ctx, x, modsel, norm_g, w_in_t, q_lora_g, w_uq_t, kv_lora_g, w_ukv, qn_g, kn_g, cos, sin):
    s_len, lc = x.shape[0], ctx.shape[0]
    t_all = s_len + lc
    nb = t_all // TB
    off = lc // TB

    def body(ctx_ref, x_ref, mod_ref, ng_ref, win_ref, qlg_ref, wuq_ref, kvlg_ref, wukv_ref, qng_ref, kng_ref, cos_ref, sin_ref,
             u_ref, q_ref, k_ref, v_ref):
        is_ctx = pl.program_id(0) < off
        xb = jnp.where(is_ctx, ctx_ref[...], x_ref[...])
        _, _, _, h, _ = _modulated(xb, mod_ref, ng_ref[...])
        hb = h.astype(BF16)
        lane = lax.broadcasted_iota(jnp.int32, (TB, 512), 1)
        ulo = jnp.where(lane < U_LO, _nt(hb, win_ref[SEG[0][0]:SEG[0][1], :]), 0.0)
        u_ref[:, 0:512] = ulo
        for j in range(1, 4):
            u_ref[:, j * 512:(j + 1) * 512] = _nt(hb, win_ref[SEG[j][0]:SEG[j][1], :])
        cos, sin = _rope_block(cos_ref, sin_ref, is_ctx)
        cq = ulo[:, 0:QL]
        cqn = (cq * lax.rsqrt(jnp.mean(cq * cq, axis=-1, keepdims=True) + EPS) * qlg_ref[...]).astype(BF16)
        qng = qng_ref[...]
        ckv = ulo[:, QL:QL + KVL]
        ckvn = (ckv * lax.rsqrt(jnp.mean(ckv * ckv, axis=-1, keepdims=True) + EPS) * kvlg_ref[...]).astype(BF16)
        qhs = [_nt(cqn, wuq_ref[hd]) for hd in range(NH)]
        kvs = [_nn(ckvn, wukv_ref[hd]) for hd in range(NH)]
        for hd in range(NH):
            qh = qhs[hd]
            qn = qh * lax.rsqrt(_rowsum(qh * qh) / DK + EPS) * qng
            q_ref[hd] = (jnp.concatenate([qn[:, :128], _rope(qn[:, 128:], cos, sin)], axis=1) * (SCALE * LOG2E)).astype(BF16)
        kr = ulo[:, 384:512]
        skr = _rowsum(kr * kr)
        kng = kng_ref[...]
        kr_roped = _rope(kr * kng[:, 128:], cos, sin)
        for hd in range(NH):
            kv = kvs[hd]
            kn = kv[:, :128]
            rk = lax.rsqrt((_rowsum(kn * kn) + skr) / DK + EPS)
            k_ref[hd] = jnp.concatenate([kn * rk * kng[:, :128], kr_roped * rk], axis=1).astype(BF16)
            v_ref[hd] = kv[:, 128:].astype(BF16)

    row = lambda w: pl.BlockSpec((TB, w), lambda i: (i, 0))
    heads = lambda w: pl.BlockSpec((NH, TB, w), lambda i: (0, i, 0))
    cspec, xspec, mspec = _token_specs(off)
    return pl.pallas_call(
        body, name="fwd_in", grid=(nb,),
        in_specs=[cspec, xspec, mspec, _full((1, D)), _full((DIN, D)), _full((1, QL)), _full((NH, DKP, QL)), _full((1, KVL)),
                  _full((NH, KVL, 256)), _full((1, DKP)), _full((1, DKP)),
                  pl.BlockSpec((1, 8, 256), lambda i: (jnp.maximum(i - off, 0), 0, 0)), _full((TB, 256))],
        out_specs=[row(DU), heads(DKP), heads(DKP), heads(DV)],
        out_shape=[jax.ShapeDtypeStruct((t_all, DU), F32), jax.ShapeDtypeStruct((NH, t_all, DKP), BF16),
                   jax.ShapeDtypeStruct((NH, t_all, DKP), BF16), jax.ShapeDtypeStruct((NH, t_all, DV), BF16)],
        compiler_params=_params(("arbitrary",)),
    )(ctx, x, modsel, norm_g, w_in_t, q_lora_g, w_uq_t, kv_lora_g, w_ukv, qn_g, kn_g, cos, sin)


def _attn_fwd(q, k, v, s_len):
    t_all = q.shape[1]
    off = (t_all - s_len) // TB
    nq = s_len // TB
    nsub = next(n for n in (4, 2, 1) if nq % n == 0)

    def body(*refs):
        q_refs = refs[:nsub]
        k_ref, v_ref, o_ref, lse_ref = refs[nsub:]
        for sb in range(nsub):
            s = _nt(q_refs[sb][0], k_ref[0])
            m = jnp.max(s, axis=-1, keepdims=True)
            e = jnp.exp2(s - m)
            l = _rowsum(e)
            o_ref[sb * TB:(sb + 1) * TB, :] = _nn(e.astype(BF16), v_ref[0]) / l
            lse_ref[0, sb] = _row_layout(m + jnp.log2(l))

    qspec = lambda sb: pl.BlockSpec((1, TB, DKP), lambda h, i: (h, i * nsub + sb + off, 0))
    return pl.pallas_call(
        body, name="attn_fwd", grid=(NH, nq // nsub),
        in_specs=[qspec(sb) for sb in range(nsub)]
        + [pl.BlockSpec((1, t_all, DKP), lambda h, i: (h, 0, 0)), pl.BlockSpec((1, t_all, DV), lambda h, i: (h, 0, 0))],
        out_specs=[pl.BlockSpec((nsub * TB, DV), lambda h, i: (i, h)), pl.BlockSpec((1, nsub, 8, TB), lambda h, i: (h, i, 0, 0))],
        out_shape=[jax.ShapeDtypeStruct((s_len, NH * DV), F32), jax.ShapeDtypeStruct((NH, nq, 8, TB), F32)],
        compiler_params=_params(("arbitrary", "arbitrary")),
    )(*([q] * nsub), k, v)


def _out_stage(attn, u, x, target, gate, w_pool, pool_scale, w_out, lc):
    s_len = x.shape[0]
    t_all = s_len + lc
    off = lc // TB
    nq = s_len // TB
    hb = TB // HALO
    nqb = s_len // Q_BLOCK
    jb = TB // nqb

    def body(attn_ref, ga_ref, pin_ref, pprev_ref, pnext_ref, gp_ref, x_ref, tgt_ref, gate_ref, wp_ref, ps_ref, wo_ref,
             dxn_ref, dattn_ref, dga_ref, dgp_ref, dpool_ref, dwo_ref, dgate_ref, dps_ref, dwp_ref, loss_ref):
        i = pl.program_id(0)

        @pl.when(i == 0)
        def _():
            dwo_ref[...] = jnp.zeros_like(dwo_ref)
            dgate_ref[...] = jnp.zeros_like(dgate_ref)
            dps_ref[...] = jnp.zeros_like(dps_ref)
            dwp_ref[...] = jnp.zeros_like(dwp_ref)
            loss_ref[...] = jnp.zeros_like(loss_ref)

        attn = jnp.concatenate([attn_ref[:, jj, :] for jj in range(jb)], axis=0)
        ga = ga_ref[...]
        gp = gp_ref[...]
        pin = pin_ref[...]
        prev = jnp.where(i == 0, 0.0, pprev_ref[...])
        nxt = jnp.where(i == nq - 1, 0.0, pnext_ref[...])
        win = jnp.concatenate([prev, pin, nxt], axis=0)
        tg = i * TB + lax.broadcasted_iota(jnp.int32, (TB, 1), 0)
        pooled = []
        for g, w in enumerate(POOL_WINDOWS):
            a = win[:, g * 128:(g + 1) * 128]
            p = _shift_rows(a, -1) + a
            for step in (1, 2, 4):
                if w >= 4 * step:
                    p = _shift_rows(p, -step) + _shift_rows(p, step)
            cnt = (jnp.minimum(tg + w // 2, s_len) - jnp.maximum(tg - w // 2, 0)).astype(F32)
            pooled.append(p[HALO:HALO + TB] / cnt - a[HALO:HALO + TB])
        pooled_b = [p.astype(BF16) for p in pooled]
        z = jnp.concatenate([_nn(pooled_b[g], wp_ref[g]) for g in range(4)], axis=1)
        ps = ps_ref[...]
        yp = z * ps
        sga = _sig(ga)
        sila = ga * sga
        sgp = _sig(gp)
        silp = gp * sgp
        br = jnp.concatenate([sila * attn, silp * yp], axis=1).astype(BF16)
        y = _nn(br, wo_ref[...])
        gate = gate_ref[...]
        err = x_ref[...] + gate * y - tgt_ref[...]
        loss_ref[...] += _colsum(_rowsum(err * err)) * (0.5 / D)
        dxn = err * (1.0 / D)
        dxn_ref[...] = dxn
        dgate_ref[...] += _colsum(dxn * y)
        dy = (dxn * gate).astype(BF16)
        dwo_ref[...] += _tn(br, dy)
        dbr = _nt(dy, wo_ref[...])
        dbra = dbr[:, :512]
        dbrp = dbr[:, 512:]
        dattn = dbra * sila
        for jj in range(jb):
            dattn_ref[:, jj, :] = dattn[jj * nqb:(jj + 1) * nqb]
        dga_ref[...] = (dbra * attn * (sga * (1.0 + ga * (1.0 - sga)))).astype(BF16)
        dgp_ref[...] = (dbrp * yp * (sgp * (1.0 + gp * (1.0 - sgp)))).astype(BF16)
        dyp = dbrp * silp
        dps_ref[...] += _colsum(dyp * z)
        dz = (dyp * ps).astype(BF16)
        dpool = []
        for g in range(4):
            dzg = dz[:, g * 128:(g + 1) * 128]
            dwp_ref[g] += _tn(pooled_b[g], dzg)
            dpool.append(_nt(dzg, wp_ref[g]))
        dpool_ref[...] = jnp.concatenate(dpool, axis=1)

    lat = lambda w: pl.BlockSpec((TB, w), lambda i: (i, 0))
    perm = pl.BlockSpec((nqb, jb, 512), lambda i: (0, i, 0))
    ucol = lambda j: pl.BlockSpec((TB, 512), lambda i: (i + off, j))
    last8 = t_all // HALO - 1
    return pl.pallas_call(
        body, name="out_stage", grid=(nq,),
        in_specs=[perm, ucol(1), ucol(2),
                  pl.BlockSpec((HALO, 512), lambda i: ((i + off) * hb - 1, 2)),
                  pl.BlockSpec((HALO, 512), lambda i: (jnp.minimum((i + off + 1) * hb, last8), 2)),
                  ucol(3), lat(D), lat(D), _full((1, D)), _full((4, 128, 128)), _full((1, 512)), _full((D, D))],
        out_specs=[lat(D), perm, lat(512), lat(512), lat(512),
                   _full((D, D)), _full((1, D)), _full((1, 512)), _full((4, 128, 128)), _full((1, 1))],
        out_shape=[jax.ShapeDtypeStruct((s_len, D), F32), jax.ShapeDtypeStruct((nqb, Q_BLOCK, 512), F32),
                   jax.ShapeDtypeStruct((s_len, 512), BF16), jax.ShapeDtypeStruct((s_len, 512), BF16),
                   jax.ShapeDtypeStruct((s_len, 512), F32),
                   jax.ShapeDtypeStruct((D, D), F32), jax.ShapeDtypeStruct((1, D), F32), jax.ShapeDtypeStruct((1, 512), F32),
                   jax.ShapeDtypeStruct((4, 128, 128), F32), jax.ShapeDtypeStruct((1, 1), F32)],
        compiler_params=_params(("arbitrary",)),
    )(attn, u, u, u, u, u, x, target, gate, w_pool, pool_scale, w_out)


def _attn_bwd(q, k, v, dattn, attn, lse, s_len):
    t_all = q.shape[1]
    off = (t_all - s_len) // TB
    nq = s_len // TB
    nch = 4
    chunks = [(c * (t_all // nch), t_all // nch) for c in range(nch)]
    nsub = next(n for n in (BWD_QBLOCKS, 2, 1) if nq % n == 0)
    tq = nsub * TB

    def body(*refs):
        q_refs = refs[:nsub]
        k_ref, v_ref, do_ref, o_ref, lse_ref, dq_ref, dk_ref, dv_ref = refs[nsub:]
        i = pl.program_id(1)

        @pl.when(i == 0)
        def _():
            dk_ref[...] = jnp.zeros_like(dk_ref)
            dv_ref[...] = jnp.zeros_like(dv_ref)

        qb = jnp.concatenate([r[0] for r in q_refs], axis=0)
        delta_r = _row_layout(_rowsum(do_ref[...] * o_ref[...]))[0:1, :]
        do = do_ref[...].astype(BF16)
        lse_r = jnp.concatenate([lse_ref[0, sb][0:1, :] for sb in range(nsub)], axis=1)
        dq = jnp.zeros((tq, DKP), F32)
        for start, size in chunks:
            rows = pl.ds(start, size)
            kc = k_ref[0, rows, :]
            p_t = jnp.exp2(_nt(kc, qb) - lse_r)
            ds_t = (p_t * (_nt(v_ref[0, rows, :], do) - delta_r)).astype(BF16)
            dv_ref[0, rows, :] += _nn(p_t.astype(BF16), do)
            dk_ref[0, rows, :] += _nn(ds_t, qb)
            dq += _tn(ds_t, kc)
        dq_ref[0] = dq * SCALE

    kvspec = lambda w: pl.BlockSpec((1, t_all, w), lambda h, i: (h, 0, 0))
    rowspec = pl.BlockSpec((1, nsub, 8, TB), lambda h, i: (h, i, 0, 0))
    qspec = lambda sb: pl.BlockSpec((1, TB, DKP), lambda h, i: (h, i * nsub + sb + off, 0))
    return pl.pallas_call(
        body, name="attn_bwd", grid=(NH, nq // nsub),
        in_specs=[qspec(sb) for sb in range(nsub)]
        + [kvspec(DKP), kvspec(DV), pl.BlockSpec((tq, DV), lambda h, i: (i, h)), pl.BlockSpec((tq, DV), lambda h, i: (i, h)),
           rowspec],
        out_specs=[pl.BlockSpec((1, tq, DKP), lambda h, i: (h, i, 0)), kvspec(DKP), kvspec(DV)],
        out_shape=[jax.ShapeDtypeStruct((NH, s_len, DKP), F32), jax.ShapeDtypeStruct((NH, t_all, DKP), F32),
                   jax.ShapeDtypeStruct((NH, t_all, DV), F32)],
        compiler_params=_params(("arbitrary", "arbitrary")),
    )(*([q] * nsub), k, v, dattn, attn, lse)


def _qkv_bwd(u, dq, dk, dv, cos, sin, q_lora_g, w_uq_t, kv_lora_g, w_ukv, qn_g, kn_g, s_len):
    t_all = u.shape[0]
    off = (t_all - s_len) // TB
    nb = t_all // TB

    def body(ulo_ref, dq_ref, dk_ref, dv_ref, cos_ref, sin_ref, qlg_ref, wuq_ref, kvlg_ref, wukv_ref, qng_ref, kng_ref,
             dlo_ref, dwuq_ref, dwukv_ref, dqlg_ref, dkvlg_ref, dqng_ref, dkng_ref):
        i = pl.program_id(0)

        @pl.when(i == 0)
        def _():
            for r in (dwuq_ref, dwukv_ref, dqlg_ref, dkvlg_ref, dqng_ref, dkng_ref):
                r[...] = jnp.zeros_like(r)

        latent = i >= off
        ulo = ulo_ref[...]
        cos, sin = _rope_block(cos_ref, sin_ref, pl.program_id(0) < off)
        cq = ulo[:, 0:QL]
        rc = lax.rsqrt(jnp.mean(cq * cq, axis=-1, keepdims=True) + EPS)
        cqh = cq * rc
        qlg = qlg_ref[...]
        cqn_b = (cqh * qlg).astype(BF16)
        qng = qng_ref[...]
        ckv = ulo[:, QL:QL + KVL]
        r0 = lax.rsqrt(jnp.mean(ckv * ckv, axis=-1, keepdims=True) + EPS)
        ckvh = ckv * r0
        kvlg = kvlg_ref[...]
        ckvn_b = (ckvh * kvlg).astype(BF16)
        qhs = [_nt(cqn_b, wuq_ref[hd]) for hd in range(NH)]
        kns = [_nn(ckvn_b, wukv_ref[hd])[:, :128] for hd in range(NH)]
        dqng = jnp.zeros((1, DKP), F32)
        dqraws = []
        for hd in range(NH):
            qh = qhs[hd]
            rq = lax.rsqrt(_rowsum(qh * qh) / DK + EPS)
            xh = qh * rq
            dqh = jnp.where(latent, dq_ref[hd], 0.0)
            dyq = jnp.concatenate([dqh[:, :128], _rope_t(dqh[:, 128:], cos, sin)], axis=1)
            dqng += _colsum(dyq * xh)
            dxh = dyq * qng
            dqraws.append((rq * (dxh - xh * (_rowsum(dxh * xh) / DK))).astype(BF16))
        dqng_ref[...] += dqng

        kr = ulo[:, 384:512]
        skr = _rowsum(kr * kr)
        kng = kng_ref[...]
        dkr = jnp.zeros((TB, 128), F32)
        dkng = jnp.zeros((1, DKP), F32)
        dkvs = []
        for hd in range(NH):
            kn = kns[hd]
            rk = lax.rsqrt((_rowsum(kn * kn) + skr) / DK + EPS)
            xh1 = kn * rk
            xh2 = kr * rk
            dkh = dk_ref[hd] * LN2
            d1 = dkh[:, :128]
            d2 = _rope_t(dkh[:, 128:], cos, sin)
            dkng += jnp.concatenate([_colsum(d1 * xh1), _colsum(d2 * xh2)], axis=1)
            dx1 = d1 * kng[:, :128]
            dx2 = d2 * kng[:, 128:]
            dot = (_rowsum(dx1 * xh1) + _rowsum(dx2 * xh2)) / DK
            dkvs.append(jnp.concatenate([rk * (dx1 - xh1 * dot), dv_ref[hd]], axis=1).astype(BF16))
            dkr += rk * (dx2 - xh2 * dot)
        dkng_ref[...] += dkng

        dcqn = jnp.zeros((TB, QL), F32)
        dckvn = jnp.zeros((TB, KVL), F32)
        for hd in range(NH):
            dwuq_ref[hd] += _tn(dqraws[hd], cqn_b)[:DK]
            dcqn += _nn(dqraws[hd], wuq_ref[hd])
            dwukv_ref[hd] += _tn(ckvn_b, dkvs[hd])
            dckvn += _nt(dkvs[hd], wukv_ref[hd])
        dqlg_ref[...] += _colsum(dcqn * cqh)
        dxh = dcqn * qlg
        dcq = rc * (dxh - cqh * jnp.mean(dxh * cqh, axis=-1, keepdims=True))
        dkvlg_ref[...] += _colsum(dckvn * ckvh)
        dxh = dckvn * kvlg
        dckv = r0 * (dxh - ckvh * jnp.mean(dxh * ckvh, axis=-1, keepdims=True))
        dlo_ref[...] = jnp.concatenate([dcq, dckv, dkr], axis=1).astype(BF16)

    row = lambda w: pl.BlockSpec((TB, w), lambda i: (i, 0))
    heads = lambda w: pl.BlockSpec((NH, TB, w), lambda i: (0, i, 0))
    return pl.pallas_call(
        body, name="qkv_bwd", grid=(nb,),
        in_specs=[row(512), pl.BlockSpec((NH, TB, DKP), lambda i: (0, jnp.maximum(i - off, 0), 0)), heads(DKP), heads(DV),
                  pl.BlockSpec((1, 8, 256), lambda i: (jnp.maximum(i - off, 0), 0, 0)), _full((TB, 256)), _full((1, QL)), _full((NH, DKP, QL)), _full((1, KVL)), _full((NH, KVL, 256)),
                  _full((1, DKP)), _full((1, DKP))],
        out_specs=[row(512), _full((NH, DK, QL)), _full((NH, KVL, 256)), _full((1, QL)), _full((1, KVL)),
                   _full((1, DKP)), _full((1, DKP))],
        out_shape=[jax.ShapeDtypeStruct((t_all, 512), BF16), jax.ShapeDtypeStruct((NH, DK, QL), F32),
                   jax.ShapeDtypeStruct((NH, KVL, 256), F32), jax.ShapeDtypeStruct((1, QL), F32),
                   jax.ShapeDtypeStruct((1, KVL), F32), jax.ShapeDtypeStruct((1, DKP), F32), jax.ShapeDtypeStruct((1, DKP), F32)],
        compiler_params=_params(("arbitrary",)),
    )(u, dq, dk, dv, cos, sin, q_lora_g, w_uq_t, kv_lora_g, w_ukv, qn_g, kn_g)


def _in_bwd(ctx, x, modsel, norm_g, dlo, dga, dgp, dpool, dxn, w_in_t):
    s_len, lc = x.shape[0], ctx.shape[0]
    t_all = s_len + lc
    off = lc // TB
    nb = t_all // TB
    nq = s_len // TB
    hb = TB // HALO
    n = TB + 2 * HALO

    def body(ctx_ref, x_ref, mod_ref, ng_ref, dlo_ref, dga_ref, dgp_ref, dp_ref, dpprev_ref, dpnext_ref, dxn_ref, win_ref,
             gx_ref, dwin_ref, dmod_ref, dng_ref):
        i = pl.program_id(0)
        j = i - off

        @pl.when(i == 0)
        def _():
            dwin_ref[...] = jnp.zeros_like(dwin_ref)
            dmod_ref[...] = jnp.zeros_like(dmod_ref)
            dng_ref[...] = jnp.zeros_like(dng_ref)

        latent = i >= off
        dp = dp_ref[...]
        prev = jnp.where(j <= 0, 0.0, dpprev_ref[...])
        nxt = jnp.where(j >= nq - 1, 0.0, dpnext_ref[...])
        win = jnp.concatenate([prev, dp, nxt], axis=0)
        tg = j * TB - HALO + lax.broadcasted_iota(jnp.int32, (n, 1), 0)
        dpin = []
        for g, w in enumerate(POOL_WINDOWS):
            cnt = jnp.maximum(jnp.minimum(tg + w // 2, s_len) - jnp.maximum(tg - w // 2, 0), 1).astype(F32)
            zq = win[:, g * 128:(g + 1) * 128] / cnt
            zq = zq + _shift_rows(zq, 1)
            for step in (1, 2, 4):
                if w >= 4 * step:
                    zq = _shift_rows(zq, -step) + _shift_rows(zq, step)
            dpin.append(zq[HALO:HALO + TB] - dp[:, g * 128:(g + 1) * 128])
        zero = jnp.zeros((TB, 512), BF16)
        du = [dlo_ref[...], jnp.where(latent, dga_ref[...], zero),
              jnp.where(latent, jnp.concatenate(dpin, axis=1).astype(BF16), zero), jnp.where(latent, dgp_ref[...], zero)]

        ng = ng_ref[...]
        xb = jnp.where(i < off, ctx_ref[...], x_ref[...])
        r, xh, xg, h, scale = _modulated(xb, mod_ref, ng)
        hb_ = h.astype(BF16)
        dh = jnp.zeros((TB, D), F32)
        for s, (lo, hi) in enumerate(SEG):
            dwin_ref[lo:hi, :] += _tn(du[s], hb_)
            dh += _nn(du[s], win_ref[lo:hi, :])
        is_lat = latent.astype(F32)
        dsh = _colsum(dh)
        dsc = _colsum(dh * xg)
        dmod_ref[0, 0:1, :] += dsh * (1.0 - is_lat)
        dmod_ref[0, 1:2, :] += dsc * (1.0 - is_lat)
        dmod_ref[1, 0:1, :] += dsh * is_lat
        dmod_ref[1, 1:2, :] += dsc * is_lat
        dxg = dh * (1.0 + scale)
        dng_ref[...] += _colsum(dxg * xh)
        dxh = dxg * ng
        gx_ref[...] = r * (dxh - xh * jnp.mean(dxh * xh, axis=-1, keepdims=True)) + dxn_ref[...]

    row = lambda w: pl.BlockSpec((TB, w), lambda i: (i, 0))
    lat = lambda w: pl.BlockSpec((TB, w), lambda i: (jnp.maximum(i - off, 0), 0))
    last8 = s_len // HALO - 1
    cspec, xspec, mspec = _token_specs(off)
    return pl.pallas_call(
        body, name="in_bwd", grid=(nb,),
        in_specs=[cspec, xspec, mspec, _full((1, D)), row(512), lat(512), lat(512), lat(512),
                  pl.BlockSpec((HALO, 512), lambda i: (jnp.maximum(jnp.maximum(i - off, 0) * hb - 1, 0), 0)),
                  pl.BlockSpec((HALO, 512), lambda i: (jnp.minimum((jnp.maximum(i - off, 0) + 1) * hb, last8), 0)),
                  lat(D), _full((DIN, D))],
        out_specs=[lat(D), _full((DIN, D)), _full((2, 2, D)), _full((1, D))],
        out_shape=[jax.ShapeDtypeStruct((s_len, D), F32), jax.ShapeDtypeStruct((DIN, D), F32),
                   jax.ShapeDtypeStruct((2, 2, D), F32), jax.ShapeDtypeStruct((1, D), F32)],
        compiler_params=_params(("arbitrary",)),
    )(ctx, x, modsel, norm_g, dlo, dga, dgp, dpool, dpool, dpool, dxn, w_in_t)


def _adamw_update(w_ref, g_ref, m_ref, v_ref, d_ref, mo_ref, vo_ref):
    gv = g_ref[...]
    mn = ADAM_B1 * m_ref[...] + (1.0 - ADAM_B1) * gv
    vn = ADAM_B2 * v_ref[...] + (1.0 - ADAM_B2) * (gv * gv)
    m_hat = mn / (1.0 - ADAM_B1 ** ADAM_STEP)
    v_hat = vn / (1.0 - ADAM_B2 ** ADAM_STEP)
    d_ref[...] = -ADAM_LR * (m_hat / (jnp.sqrt(v_hat) + ADAM_EPS) + ADAM_WD * w_ref[...])
    mo_ref[...] = mn
    vo_ref[...] = vn


def _adamw_many(ws, gs, ms, vs):
    n = len(ws)

    def body(*refs):
        for i in range(n):
            _adamw_update(refs[i], refs[n + i], refs[2 * n + i], refs[3 * n + i], refs[4 * n + i], refs[5 * n + i], refs[6 * n + i])

    def spec(w):
        rows, cols = w.shape
        return pl.BlockSpec((rows // 2, cols), lambda i: (i, 0)) if rows % 16 == 0 else _full((rows, cols))

    specs = [spec(w) for w in ws]
    shp = [jax.ShapeDtypeStruct(w.shape, F32) for w in ws]
    out = pl.pallas_call(body, name="adamw_many", grid=(2,), in_specs=specs * 4, out_specs=specs * 3, out_shape=shp * 3,
                         compiler_params=_params(("arbitrary",)))(*ws, *gs, *ms, *vs)
    return out[:n], out[n:2 * n], out[2 * n:]


def _adamw(w, g, m, v, name):
    rows, cols = w.shape
    rb = next(r for r in range(min(rows, 256), 0, -8) if rows % r == 0)

    def body(w_ref, g_ref, m_ref, v_ref, d_ref, mo_ref, vo_ref):
        _adamw_update(w_ref, g_ref, m_ref, v_ref, d_ref, mo_ref, vo_ref)

    spec = pl.BlockSpec((rb, cols), lambda i: (i, 0))
    shp = jax.ShapeDtypeStruct((rows, cols), F32)
    return pl.pallas_call(
        body, name=name, grid=(rows // rb,), in_specs=[spec] * 4, out_specs=[spec] * 3, out_shape=[shp] * 3,
        compiler_params=_params(("arbitrary",)),
    )(w, g, m, v)


class _Links:
    def __init__(self, send_sems, recv_sems):
        self.send_sems, self.recv_sems, self.sends = send_sems, recv_sems, []

    def send(self, src, dst, sem, to):
        cp = pltpu.make_async_remote_copy(src, dst, self.send_sems.at[sem], self.recv_sems.at[sem], device_id=to,
                                          device_id_type=MESH)
        cp.start()
        self.sends.append(cp)

    def arrived(self, dst, sem, frm):
        pltpu.make_async_remote_copy(dst, dst, self.send_sems.at[sem], self.recv_sems.at[sem], device_id=frm,
                                     device_id_type=MESH).wait_recv()

    def drain(self):
        for cp in self.sends:
            cp.wait_send()


def _half(ref, c, axis):
    size = ref.shape[axis - 2] // 2
    win = pl.ds(pl.multiple_of(c * size, 16 if axis == 0 else 128), size)
    idx = (win, slice(None)) if axis == 0 else (slice(None), win)
    return ref.at[(slice(None),) * (len(ref.shape) - 2) + idx]


def _select_rows(slots_ref, n_slots, row=0):
    sub = lax.broadcasted_iota(jnp.int32, (8, 1), 0)
    out = None
    for d in range(n_slots):
        r = jnp.where(sub == d, jnp.broadcast_to(slots_ref[d][row:row + 1, :], (8, slots_ref.shape[-1])), 0.0)
        out = r if out is None else out + r
    return out


def _gather(c, c_ctx, w_mod, b_mod_k, shards, axes, slab_rows):
    nw = len(shards)
    kw = w_mod.shape[1]

    def body(*refs):
        c_ref, cc_ref, wm_ref, b_ref = refs[:4]
        w_refs = refs[4:4 + nw]
        a16_ref, mod_ref = refs[4 + nw:6 + nw]
        g_refs = refs[6 + nw:6 + 2 * nw]
        a_ref, send_sems, recv_sems = refs[6 + 2 * nw:]

        def slab(wi, chip):
            return g_refs[wi].at[chip].at[0:shards[wi].shape[0]]
        x, y, cc = lax.axis_index("x"), lax.axis_index("y"), lax.axis_index("c")
        me = 4 * x + 2 * y + cc
        k = 2 * x + y
        sibling = (x, y, 1 - cc)
        links = _Links(send_sems, recv_sems)
        chips = [_peer(x, y, cc, off + (0,)) for off in CHIPS3]
        chip_a = ((x + 1 - cc) % 2, (y + cc) % 2, cc)
        chip_b = ((x + cc) % 2, (y + 1 - cc) % 2, cc)
        chip_d = (1 - x, 1 - y, cc)
        cv = c_ref[...]
        a_ref[me] = jnp.broadcast_to(cv * _sig(cv), (8, D))
        for j, off in enumerate(PEERS7):
            links.send(a_ref.at[me], a_ref.at[me], j, _peer(x, y, cc, off))
        for wi in range(nw):
            slab(wi, k)[...] = w_refs[wi][...].astype(BF16)
            for j, to in enumerate((chip_a, chip_b)):
                links.send(_half(slab(wi, k), cc, axes[wi]), _half(slab(wi, k), cc, axes[wi]), 10 + wi * 6 + j, to)
            pad = slab_rows[wi] - shards[wi].shape[0]
            if pad:
                for kk in range(4):
                    g_refs[wi][kk, shards[wi].shape[0]:, :] = jnp.zeros((pad, shards[wi].shape[1]), BF16)
        for j, off in enumerate(PEERS7):
            px, py, pc = _peer(x, y, cc, off)
            links.arrived(a_ref.at[4 * px + 2 * py + pc], j, (px, py, pc))
        ccv = cc_ref[...]
        sub = lax.broadcasted_iota(jnp.int32, (8, 1), 0)
        a16 = jnp.concatenate([_select_rows(a_ref, 8), jnp.where(sub == 0, jnp.broadcast_to(ccv * _sig(ccv), (8, D)), 0.0)], axis=0)
        a16_ref[...] = a16
        mod_ref[k] = _dot3(_nn, a16, wm_ref[...]) + b_ref[...]
        for j, to in enumerate(chips):
            links.send(mod_ref.at[k], mod_ref.at[k], 7 + j, to)
        for j, (frm, origin) in enumerate(((chip_a, chip_a), (chip_b, chip_b), (chip_b, chip_d))):
            for wi in range(nw):
                blk = _half(slab(wi, 2 * origin[0] + origin[1]), cc, axes[wi])
                links.arrived(blk, 10 + wi * 6 + j, frm)
                if j == 0:
                    links.send(blk, blk, 10 + wi * 6 + 2, chip_b)
                links.send(blk, blk, 10 + wi * 6 + 3 + j, sibling)
        for j, (px, py, pc) in enumerate(chips):
            links.arrived(mod_ref.at[2 * px + py], 7 + j, (px, py, pc))
        for j, origin in enumerate((chip_b, chip_a, chip_d)):
            for wi in range(nw):
                links.arrived(_half(slab(wi, 2 * origin[0] + origin[1]), 1 - cc, axes[wi]), 10 + wi * 6 + 3 + j, sibling)
        links.drain()

    nsem = 10 + 6 * nw
    return pl.pallas_call(
        body, name="gather", in_specs=[VM] * (4 + nw), out_specs=[VM] * (2 + nw),
        out_shape=[jax.ShapeDtypeStruct((16, D), F32), jax.ShapeDtypeStruct((4, 16, kw), F32)]
        + [jax.ShapeDtypeStruct((4, r, s.shape[1]), BF16) for r, s in zip(slab_rows, shards)],
        scratch_shapes=[pltpu.VMEM((8, 8, D), F32), pltpu.SemaphoreType.DMA((nsem,)), pltpu.SemaphoreType.DMA((nsem,))],
        compiler_params=pltpu.CompilerParams(vmem_limit_bytes=VMEM_LIMIT),
    )(c, c_ctx, w_mod, b_mod_k, *shards)


SMALL_ROW_WIDTHS = (D, QL, KVL, DKP, DKP, 512, 128)
SMALL_OUT_WIDTHS = (D, QL, KVL, DK, DK, 512, 1)


def _reduce(grads, axes, smalls, w_pool_g, dmod8, a16, w_mod, c_ctx):
    nw = len(grads)
    ns = len(smalls)
    kw = w_mod.shape[1]
    halves = []
    for g, ax in zip(grads, axes):
        halves.append((g.shape[1] // 2, g.shape[2]) if ax == 0 else (g.shape[1], g.shape[2] // 2))

    def body(*refs):
        g_refs = refs[:nw]
        small_refs = refs[nw:nw + ns]
        wp_ref, dm_ref, a16_ref, wm_ref, cc_ref = refs[nw + ns:nw + ns + 5]
        o = nw + ns + 5
        r_refs = refs[o:o + nw]
        small_outs = refs[o + nw:o + nw + ns]
        rwp_ref, gw_ref, gb_ref, gc_ref = refs[o + nw + ns:o + nw + ns + 4]
        o = o + nw + ns + 4
        own, sib, part, got, rel = (refs[o + i * nw:o + (i + 1) * nw] for i in range(5))
        smbuf, wps, wpg, dm_all, pc_all, send_sems, recv_sems, local_sems = refs[o + 5 * nw:]
        x, y, cc = lax.axis_index("x"), lax.axis_index("y"), lax.axis_index("c")
        me = 4 * x + 2 * y + cc
        k = 2 * x + y
        sibling = (x, y, 1 - cc)
        links = _Links(send_sems, recv_sems)
        chips = [_peer(x, y, cc, off + (0,)) for off in CHIPS3]
        peers = [_peer(x, y, cc, off) for off in PEERS7]
        chip_a = ((x + 1 - cc) % 2, (y + cc) % 2, cc)
        chip_b = ((x + cc) % 2, (y + 1 - cc) % 2, cc)
        ka, kb, kd = 2 * chip_a[0] + chip_a[1], 2 * chip_b[0] + chip_b[1], 2 * (1 - x) + (1 - y)
        big, sm0, wp0, dm0, pc0 = 0, 5 * nw, 5 * nw + 7, 5 * nw + 14, 5 * nw + 21

        locals_ = []
        for wi in range(nw):
            lc = pltpu.make_async_copy(_half(g_refs[wi], cc, axes[wi]), own[wi], local_sems.at[wi])
            lc.start()
            locals_.append(lc)
            links.send(_half(g_refs[wi], 1 - cc, axes[wi]), sib[wi], big + wi * 5, sibling)
        slot = smbuf.at[me]
        slot[...] = jnp.zeros((8, D), F32)
        for r, (ref, w) in enumerate(zip(small_refs, SMALL_ROW_WIDTHS)):
            slot[r:r + 1, 0:w] = jnp.broadcast_to(ref[...], (1, w))
        links.send(wp_ref, wps, wp0, sibling)
        dm_all[me] = dm_ref[...]
        for j, peer in enumerate(peers):
            links.send(dm_all.at[me], dm_all.at[me], dm0 + j, peer)
            links.send(smbuf.at[me], smbuf.at[me], sm0 + j, peer)
        links.arrived(wps, wp0, sibling)
        wpg[k] = (wp_ref[...] + wps[...]).astype(BF16)
        for j, to in enumerate(chips):
            links.send(wpg.at[k], wpg.at[k], wp0 + 1 + j, to)
        for wi in range(nw):
            locals_[wi].wait()
            links.arrived(sib[wi], big + wi * 5, sibling)
            part[wi][...] = (own[wi][...] + sib[wi][...]).astype(BF16)
            got[wi][k] = part[wi][k]
            got[wi][kd] = jnp.zeros(halves[wi], BF16)
            links.send(part[wi].at[kd], rel[wi], big + wi * 5 + 1, chip_b)
            links.send(part[wi].at[kb], got[wi].at[k], big + wi * 5 + 2, chip_b)
        for j, (px, py, pc) in enumerate(peers):
            links.arrived(dm_all.at[4 * px + 2 * py + pc], dm0 + j, (px, py, pc))
        dmc = dm_all[0][1:2, :]
        dml = dm_all[0][0:1, :]
        for d in range(1, 8):
            dmc = dmc + dm_all[d][1:2, :]
            dml = dml + dm_all[d][0:1, :]
        gb_ref[...] = dml + dmc
        sub = lax.broadcasted_iota(jnp.int32, (8, 1), 0)
        b16 = jnp.concatenate([_select_rows(dm_all, 8), jnp.where(sub == 0, jnp.broadcast_to(dmc, (8, 3 * D)), 0.0)], axis=0)
        bk = jnp.zeros((16, kw), F32)
        for kk in range(4):
            bk = bk + jnp.where(k == kk, b16[:, kk * kw:(kk + 1) * kw], 0.0)
        gw_ref[...] = _dot3(_tn, a16_ref[...], bk)
        pc_all[k] = _dot3(_nt, jnp.broadcast_to(bk[8:9, :], (8, kw)), wm_ref[...])
        for j, to in enumerate(chips):
            links.send(pc_all.at[k], pc_all.at[k], pc0 + j, to)
        for j, (px, py, pc) in enumerate(peers):
            links.arrived(smbuf.at[4 * px + 2 * py + pc], sm0 + j, (px, py, pc))
        tot = smbuf[0]
        for d in range(1, 8):
            tot = tot + smbuf[d]
        for r, (ref, w) in enumerate(zip(small_outs, SMALL_OUT_WIDTHS)):
            ref[...] = tot[r:r + 1, 0:w]
        for j, (px, py, pc) in enumerate(chips):
            links.arrived(wpg.at[2 * px + py], wp0 + 1 + j, (px, py, pc))
        wpt = wpg[0].astype(F32)
        for kk in range(1, 4):
            wpt = wpt + wpg[kk].astype(F32)
        rwp_ref[...] = wpt
        for wi in range(nw):
            links.arrived(rel[wi], big + wi * 5 + 1, chip_b)
            rel[wi][...] = (part[wi][ka].astype(F32) + rel[wi][...].astype(F32)).astype(BF16)
            links.send(rel[wi], got[wi].at[k], big + wi * 5 + 3, chip_a)
        for wi in range(nw):
            links.arrived(got[wi].at[kb], big + wi * 5 + 2, chip_b)
            links.arrived(got[wi].at[ka], big + wi * 5 + 3, chip_a)
            total = got[wi][0].astype(F32)
            for kk in range(1, 4):
                total = total + got[wi][kk].astype(F32)
            mine = _half(r_refs[wi], cc, axes[wi])
            mine[...] = total
            links.send(mine, mine, big + wi * 5 + 4, sibling)
        for j, (px, py, pc) in enumerate(chips):
            links.arrived(pc_all.at[2 * px + py], pc0 + j, (px, py, pc))
        ccv = cc_ref[...]
        sg = _sig(ccv)
        gc_ref[...] = (pc_all[0][0:1, :] + pc_all[1][0:1, :] + pc_all[2][0:1, :] + pc_all[3][0:1, :]) * (sg * (1.0 + ccv * (1.0 - sg)))
        for wi in range(nw):
            links.arrived(_half(r_refs[wi], 1 - cc, axes[wi]), big + wi * 5 + 4, sibling)
        links.drain()

    nsem = 5 * nw + 24
    quads = [(4,) + h for h in halves]
    return pl.pallas_call(
        body, name="reduce", in_specs=[ANY] * nw + [VM] * (ns + 5), out_specs=[VM] * (nw + ns + 4),
        out_shape=[jax.ShapeDtypeStruct(g.shape[1:], F32) for g in grads]
        + [jax.ShapeDtypeStruct((1, w), F32) for w in SMALL_OUT_WIDTHS]
        + [jax.ShapeDtypeStruct(w_pool_g.shape, F32), jax.ShapeDtypeStruct((D, kw), F32), jax.ShapeDtypeStruct((1, 3 * D), F32),
           jax.ShapeDtypeStruct((1, D), F32)],
        scratch_shapes=[pltpu.VMEM(q, F32) for q in quads] + [pltpu.VMEM(q, F32) for q in quads]
        + [pltpu.VMEM(q, BF16) for q in quads] + [pltpu.VMEM(q, BF16) for q in quads] + [pltpu.VMEM(h, BF16) for h in halves]
        + [pltpu.VMEM((8, 8, D), F32), pltpu.VMEM(w_pool_g.shape, F32), pltpu.VMEM((4,) + w_pool_g.shape, BF16),
           pltpu.VMEM((8, 8, 3 * D), F32),
           pltpu.VMEM((4, 8, D), F32)]
        + [pltpu.SemaphoreType.DMA((nsem,)), pltpu.SemaphoreType.DMA((nsem,)), pltpu.SemaphoreType.DMA((nw,))],
        compiler_params=pltpu.CompilerParams(vmem_limit_bytes=VMEM_LIMIT),
    )(*grads, *smalls, w_pool_g, dmod8, a16, w_mod, c_ctx)


def _rope_tables(s_len):
    rows = s_len // GRID_W
    per = TB // GRID_W
    n_freq = 16
    inv = ROPE_BASE ** (-jnp.arange(n_freq, dtype=F32) / n_freq)
    ang_r = jnp.arange(rows, dtype=F32)[:, None] * inv
    ang_c = jnp.arange(GRID_W, dtype=F32)[:, None] * inv
    by_row, by_col = [], []
    for fn, pad in ((jnp.cos, 1.0), (jnp.sin, 0.0)):
        r = jnp.concatenate([fn(ang_r), fn(ang_r), jnp.zeros((rows, 96), F32)], axis=1).reshape(rows // per, per, 128)
        by_row.append(jnp.pad(r, ((0, 0), (0, 8 - per), (0, 0))))
        cpart = jnp.concatenate([jnp.zeros((GRID_W, 32), F32), fn(ang_c), fn(ang_c), jnp.full((GRID_W, 64), pad, F32)], axis=1)
        by_col.append(jnp.tile(cpart, (per, 1)))
    return jnp.concatenate(by_row, axis=-1), jnp.concatenate(by_col, axis=-1)


def kernel(x, c, ctx, c_ctx, w_mod, b_mod, norm_g, w_in, q_lora_g, w_uq, kv_lora_g, w_ukv, q_norm_g, k_norm_g, w_pool, pool_scale, w_out, loss_target, m_c_ctx, m_w_mod, m_b_mod, m_norm_g, m_w_in, m_q_lora_g, m_w_uq, m_kv_lora_g, m_w_ukv, m_q_norm_g, m_k_norm_g, m_w_pool, m_pool_scale, m_w_out, v_c_ctx, v_w_mod, v_b_mod, v_norm_g, v_w_in, v_q_lora_g, v_w_uq, v_kv_lora_g, v_w_ukv, v_q_norm_g, v_k_norm_g, v_w_pool, v_pool_scale, v_w_out):
    xi, yi, ci = lax.axis_index("x"), lax.axis_index("y"), lax.axis_index("c")
    me = 4 * xi + 2 * yi + ci
    k = 2 * xi + yi
    s_len = x.shape[1]
    lc = ctx.shape[1]
    kw = w_mod.shape[2]
    weights = dict(c_ctx=c_ctx, w_mod=w_mod, b_mod=b_mod, norm_g=norm_g, w_in=w_in, q_lora_g=q_lora_g, w_uq=w_uq,
                   kv_lora_g=kv_lora_g, w_ukv=w_ukv, q_norm_g=q_norm_g, k_norm_g=k_norm_g, w_pool=w_pool,
                   pool_scale=pool_scale, w_out=w_out)
    m_in = dict(c_ctx=m_c_ctx, w_mod=m_w_mod, b_mod=m_b_mod, norm_g=m_norm_g, w_in=m_w_in, q_lora_g=m_q_lora_g, w_uq=m_w_uq,
                kv_lora_g=m_kv_lora_g, w_ukv=m_w_ukv, q_norm_g=m_q_norm_g, k_norm_g=m_k_norm_g, w_pool=m_w_pool,
                pool_scale=m_pool_scale, w_out=m_w_out)
    v_in = dict(c_ctx=v_c_ctx, w_mod=v_w_mod, b_mod=v_b_mod, norm_g=v_norm_g, w_in=v_w_in, q_lora_g=v_q_lora_g, w_uq=v_w_uq,
                kv_lora_g=v_kv_lora_g, w_ukv=v_w_ukv, q_norm_g=v_q_norm_g, k_norm_g=v_k_norm_g, w_pool=v_w_pool,
                pool_scale=v_pool_scale, w_out=v_w_out)
    order = ["c_ctx", "w_mod", "b_mod", "norm_g", "w_in", "q_lora_g", "w_uq", "kv_lora_g", "w_ukv", "q_norm_g", "k_norm_g",
             "w_pool", "pool_scale", "w_out"]
    transposed = ("w_in", "w_uq")
    as2d = lambda n, a: jnp.transpose(a[0]) if n in transposed else a.reshape(-1, a.shape[-1])
    back = lambda n, a: jnp.transpose(a)[None] if n in transposed else a.reshape(weights[n].shape)

    c_ctx2 = c_ctx.reshape(1, D)
    b_mod_k = lax.dynamic_slice(b_mod, (0, k * kw), (1, kw))
    split = (1, 0, 0, 0)
    a16, mod_all, g_in, g_uq, g_ukv, g_out = _gather(
        c, c_ctx2, w_mod[0], b_mod_k, [as2d("w_in", w_in), as2d("w_uq", w_uq), w_ukv[0], w_out[0]], split,
        (DIN // 4, DKP, KVL, D // 4))
    mod_me = lax.dynamic_index_in_dim(mod_all, me, axis=1, keepdims=False).reshape(3, D)
    mod_c = mod_all[:, 8, :].reshape(3, D)
    modsel = jnp.stack([mod_c, mod_me])
    w_in_t = g_in.reshape(DIN, D)
    w_uq_t = g_uq
    w_out_f = g_out.reshape(D, D)
    qn_g = jnp.pad(q_norm_g, ((0, 0), (0, DKP - DK)))
    kn_g = jnp.pad(k_norm_g, ((0, 0), (0, DKP - DK)))
    w_pool_b = w_pool[0].astype(BF16)
    cos, sin = _rope_tables(s_len)

    u, q, kk, v = _fwd_in(ctx[0], x[0], modsel, norm_g, w_in_t, q_lora_g, w_uq_t, kv_lora_g, g_ukv, qn_g, kn_g, cos, sin)
    attn, lse = _attn_fwd(q, kk, v, s_len)
    (dxn, dattn, dga, dgp, dpool, dw_out, dgate, dps, dw_pool, loss) = _out_stage(
        attn.reshape(s_len // Q_BLOCK, Q_BLOCK, NH * DV), u, x[0], loss_target[0], modsel[1, 2:3, :], w_pool_b, pool_scale,
        w_out_f, lc)
    dattn = dattn.reshape(s_len, NH * DV)
    dq, dk, dv = _attn_bwd(q, kk, v, dattn, attn, lse, s_len)
    dlo, dw_uq_t, dw_ukv, dqlg, dkvlg, dqng, dkng = _qkv_bwd(u, dq, dk, dv, cos, sin, q_lora_g, w_uq_t, kv_lora_g, g_ukv,
                                                            qn_g, kn_g, s_len)
    gx, dw_in_t, dmod, dng = _in_bwd(ctx[0], x[0], modsel, norm_g, dlo, dga, dgp, dpool, dxn, w_in_t)

    dmod_l = jnp.concatenate([dmod[1, 0], dmod[1, 1], dgate[0]]).reshape(1, 3 * D)
    dmod_c = jnp.concatenate([dmod[0, 0], dmod[0, 1], jnp.zeros((D,), F32)]).reshape(1, 3 * D)
    dmod8 = jnp.concatenate([dmod_l, dmod_c, jnp.zeros((6, 3 * D), F32)], axis=0)
    (r_in, r_uq, r_ukv, r_out, g_ng, g_qlg, g_kvlg, g_qng, g_kng, g_ps, loss_all, g_wp, g_w_mod, g_b_mod, g_c_ctx) = _reduce(
        [dw_in_t.reshape(4, DIN // 4, D), dw_uq_t, dw_ukv, dw_out.reshape(4, D // 4, D)], split,
        [dng, dqlg, dkvlg, dqng, dkng, dps, loss], dw_pool, dmod8, a16, w_mod[0], c_ctx2)
    g2d = dict(c_ctx=g_c_ctx, b_mod=g_b_mod, w_mod=g_w_mod, w_in=r_in, w_uq=r_uq, w_ukv=r_ukv, w_out=r_out, norm_g=g_ng,
               q_lora_g=g_qlg, kv_lora_g=g_kvlg, q_norm_g=g_qng, k_norm_g=g_kng, pool_scale=g_ps, w_pool=g_wp.reshape(512, 128))

    d2d, m2d, v2d = {}, {}, {}
    d2d["w_mod"], m2d["w_mod"], v2d["w_mod"] = _adamw(as2d("w_mod", w_mod), g2d["w_mod"], as2d("w_mod", m_w_mod),
                                                      as2d("w_mod", v_w_mod), "adamw_w_mod")
    rest = [n for n in order if n != "w_mod"]
    outs = _adamw_many([as2d(n, weights[n]) for n in rest], [g2d[n] for n in rest], [as2d(n, m_in[n]) for n in rest],
                       [as2d(n, v_in[n]) for n in rest])
    for dst, arrs in zip((d2d, m2d, v2d), outs):
        dst.update(dict(zip(rest, arrs)))

    return (loss_all[0, 0], gx[None], *[back(n, g2d[n]) for n in order], *[back(n, d2d[n]) for n in order],
            *[back(n, m2d[n]) for n in order], *[back(n, v2d[n]) for n in order])
```

```python
import jax
import jax.numpy as jnp
from jax import lax
from jax.experimental import pallas as pl
from jax.experimental.pallas import tpu as pltpu

F32 = jnp.float32
BF16 = jnp.bfloat16
MESH = pl.DeviceIdType.MESH

D = 1024
NH = 4
DK = 192
DKP = 256
DV = 128
QL = 256
KVL = 128
DIN = 1984
U_LO = 448
SEG = ((0, 512), (448, 960), (960, 1472), (1472, 1984))
DU = 2048
POOL_WINDOWS = (2, 4, 8, 16)
HALO = 8
EPS = 1e-6
ROPE_BASE = 10000.0
GRID_W = 64
Q_BLOCK = 128
TB = 256
BWD_QBLOCKS = 1
SCALE = DK ** -0.5
LOG2E = 1.4426950408889634
LN2 = 0.6931471805599453
VMEM_LIMIT = 56 * 1024 * 1024

ADAM_LR = 0.001
ADAM_B1 = 0.9
ADAM_B2 = 0.999
ADAM_EPS = 1e-08
ADAM_WD = 0.01
ADAM_STEP = 10

CHIPS3 = ((1, 0), (0, 1), (1, 1))
PEERS7 = tuple((dx, dy, dc) for dx in (0, 1) for dy in (0, 1) for dc in (0, 1) if (dx, dy, dc) != (0, 0, 0))

VM = pl.BlockSpec(memory_space=pltpu.VMEM)
ANY = pl.BlockSpec(memory_space=pl.ANY)


def _nn(a, b):
    return jnp.dot(a, b, preferred_element_type=F32)


def _nt(a, b):
    return lax.dot_general(a, b, (((1,), (1,)), ((), ())), preferred_element_type=F32)


def _tn(a, b):
    return lax.dot_general(a, b, (((0,), (0,)), ((), ())), preferred_element_type=F32)


def _split3(a):
    a0 = a.astype(BF16)
    r = a - a0.astype(F32)
    a1 = r.astype(BF16)
    a2 = (r - a1.astype(F32)).astype(BF16)
    return a0, a1, a2


def _dot3(dot, a, b):
    sa = _split3(a)
    sb = _split3(b)
    out = None
    for i in range(3):
        for j in range(3 - i):
            t = dot(sa[i], sb[j])
            out = t if out is None else out + t
    return out


def _sig(x):
    return 1.0 / (1.0 + jnp.exp(-x))


def _rot(t):
    src = lax.broadcasted_iota(jnp.int32, (128, 128), 0)
    dst = lax.broadcasted_iota(jnp.int32, (128, 128), 1)
    first = (dst % 32) < 16
    perm = jnp.where(first & (src == dst + 16), -1.0, jnp.where(~first & (src == dst - 16), 1.0, 0.0)).astype(BF16)
    hi = t.astype(BF16)
    lo = (t - hi.astype(F32)).astype(BF16)
    return _nn(hi, perm) + _nn(lo, perm)


def _rope(t, cos, sin):
    return t * cos + _rot(t) * sin


def _rope_t(t, cos, sin):
    return t * cos - _rot(t * sin)


def _rope_block(rows_ref, cols_ref, is_ctx):
    lane = lax.broadcasted_iota(jnp.int32, (TB, 256), 1) % 128
    rows = jnp.concatenate([jnp.broadcast_to(rows_ref[0, r:r + 1, :], (GRID_W, 256)) for r in range(TB // GRID_W)], axis=0)
    cs = jnp.where(lane < 32, rows, cols_ref[...])
    return jnp.where(is_ctx, 1.0, cs[:, :128]), jnp.where(is_ctx, 0.0, cs[:, 128:])


def _shift_rows(z, k):
    n = z.shape[0]
    return pltpu.roll(z, (n - k) % n, 0)


def _colsum(a):
    return jnp.sum(a, axis=0, keepdims=True)


def _rowsum(a):
    return jnp.sum(a, axis=-1, keepdims=True)


def _row_layout(col):
    return jnp.transpose(jnp.broadcast_to(col, (col.shape[0], 128)))[0:8, :]


def _params(sem=None):
    return pltpu.CompilerParams(dimension_semantics=sem, vmem_limit_bytes=VMEM_LIMIT)


def _full(shape):
    nd = len(shape)
    return pl.BlockSpec(shape, lambda *_: (0,) * nd)


def _peer(x, y, c, off):
    dx, dy, dc = off
    return ((x + dx) % 2, (y + dy) % 2, (c + dc) % 2)


def _token_specs(off):
    ctx = pl.BlockSpec((TB, D), lambda i: (jnp.minimum(i, off - 1), 0))
    lat = pl.BlockSpec((TB, D), lambda i: (jnp.maximum(i - off, 0), 0))
    mod = pl.BlockSpec((1, 3, D), lambda i: (jnp.minimum(i // off, 1), 0, 0))
    return ctx, lat, mod


def _modulated(x, mod_ref, ng):
    shift = mod_ref[0, 0:1, :]
    scale = mod_ref[0, 1:2, :]
    r = lax.rsqrt(jnp.mean(x * x, axis=-1, keepdims=True) + EPS)
    xh = x * r
    xg = xh * ng
    return r, xh, xg, xg * (1.0 + scale) + shift, scale


def _fwd_in(ctx, x, modsel, norm_g, w_in_t, q_lora_g, w_uq_t, kv_lora_g, w_ukv, qn_g, kn_g, cos, sin):
    s_len, lc = x.shape[0], ctx.shape[0]
    t_all = s_len + lc
    nb = t_all // TB
    off = lc // TB

    def body(ctx_ref, x_ref, mod_ref, ng_ref, win_ref, qlg_ref, wuq_ref, kvlg_ref, wukv_ref, qng_ref, kng_ref, cos_ref, sin_ref,
             u_ref, q_ref, k_ref, v_ref):
        is_ctx = pl.program_id(0) < off
        xb = jnp.where(is_ctx, ctx_ref[...], x_ref[...])
        _, _, _, h, _ = _modulated(xb, mod_ref, ng_ref[...])
        hb = h.astype(BF16)
        lane = lax.broadcasted_iota(jnp.int32, (TB, 512), 1)
        ulo = jnp.where(lane < U_LO, _nt(hb, win_ref[SEG[0][0]:SEG[0][1], :]), 0.0)
        u_ref[:, 0:512] = ulo
        for j in range(1, 4):
            u_ref[:, j * 512:(j + 1) * 512] = _nt(hb, win_ref[SEG[j][0]:SEG[j][1], :])
        cos, sin = _rope_block(cos_ref, sin_ref, is_ctx)
        cq = ulo[:, 0:QL]
        cqn = (cq * lax.rsqrt(jnp.mean(cq * cq, axis=-1, keepdims=True) + EPS) * qlg_ref[...]).astype(BF16)
        qng = qng_ref[...]
        ckv = ulo[:, QL:QL + KVL]
        ckvn = (ckv * lax.rsqrt(jnp.mean(ckv * ckv, axis=-1, keepdims=True) + EPS) * kvlg_ref[...]).astype(BF16)
        qhs = [_nt(cqn, wuq_ref[hd]) for hd in range(NH)]
        kvs = [_nn(ckvn, wukv_ref[hd]) for hd in range(NH)]
        for hd in range(NH):
            qh = qhs[hd]
            qn = qh * lax.rsqrt(_rowsum(qh * qh) / DK + EPS) * qng
            q_ref[hd] = (jnp.concatenate([qn[:, :128], _rope(qn[:, 128:], cos, sin)], axis=1) * (SCALE * LOG2E)).astype(BF16)
        kr = ulo[:, 384:512]
        skr = _rowsum(kr * kr)
        kng = kng_ref[...]
        kr_roped = _rope(kr * kng[:, 128:], cos, sin)
        for hd in range(NH):
            kv = kvs[hd]
            kn = kv[:, :128]
            rk = lax.rsqrt((_rowsum(kn * kn) + skr) / DK + EPS)
            k_ref[hd] = jnp.concatenate([kn * rk * kng[:, :128], kr_roped * rk], axis=1).astype(BF16)
            v_ref[hd] = kv[:, 128:].astype(BF16)

    row = lambda w: pl.BlockSpec((TB, w), lambda i: (i, 0))
    heads = lambda w: pl.BlockSpec((NH, TB, w), lambda i: (0, i, 0))
    cspec, xspec, mspec = _token_specs(off)
    return pl.pallas_call(
        body, name="fwd_in", grid=(nb,),
        in_specs=[cspec, xspec, mspec, _full((1, D)), _full((DIN, D)), _full((1, QL)), _full((NH, DKP, QL)), _full((1, KVL)),
                  _full((NH, KVL, 256)), _full((1, DKP)), _full((1, DKP)),
                  pl.BlockSpec((1, 8, 256), lambda i: (jnp.maximum(i - off, 0), 0, 0)), _full((TB, 256))],
        out_specs=[row(DU), heads(DKP), heads(DKP), heads(DV)],
        out_shape=[jax.ShapeDtypeStruct((t_all, DU), F32), jax.ShapeDtypeStruct((NH, t_all, DKP), BF16),
                   jax.ShapeDtypeStruct((NH, t_all, DKP), BF16), jax.ShapeDtypeStruct((NH, t_all, DV), BF16)],
        compiler_params=_params(("arbitrary",)),
    )(ctx, x, modsel, norm_g, w_in_t, q_lora_g, w_uq_t, kv_lora_g, w_ukv, qn_g, kn_g, cos, sin)


def _attn_fwd(q, k, v, s_len):
    t_all = q.shape[1]
    off = (t_all - s_len) // TB
    nq = s_len // TB
    nsub = next(n for n in (4, 2, 1) if nq % n == 0)

    def body(*refs):
        q_refs = refs[:nsub]
        k_ref, v_ref, o_ref, lse_ref = refs[nsub:]
        for sb in range(nsub):
            s = _nt(q_refs[sb][0], k_ref[0])
            m = jnp.max(s, axis=-1, keepdims=True)
            e = jnp.exp2(s - m)
            l = _rowsum(e)
            o_ref[sb * TB:(sb + 1) * TB, :] = _nn(e.astype(BF16), v_ref[0]) / l
            lse_ref[0, sb] = _row_layout(m + jnp.log2(l))

    qspec = lambda sb: pl.BlockSpec((1, TB, DKP), lambda h, i: (h, i * nsub + sb + off, 0))
    return pl.pallas_call(
        body, name="attn_fwd", grid=(NH, nq // nsub),
        in_specs=[qspec(sb) for sb in range(nsub)]
        + [pl.BlockSpec((1, t_all, DKP), lambda h, i: (h, 0, 0)), pl.BlockSpec((1, t_all, DV), lambda h, i: (h, 0, 0))],
        out_specs=[pl.BlockSpec((nsub * TB, DV), lambda h, i: (i, h)), pl.BlockSpec((1, nsub, 8, TB), lambda h, i: (h, i, 0, 0))],
        out_shape=[jax.ShapeDtypeStruct((s_len, NH * DV), F32), jax.ShapeDtypeStruct((NH, nq, 8, TB), F32)],
        compiler_params=_params(("arbitrary", "arbitrary")),
    )(*([q] * nsub), k, v)


def _out_stage(attn, u, x, target, gate, w_pool, pool_scale, w_out, lc):
    s_len = x.shape[0]
    t_all = s_len + lc
    off = lc // TB
    nq = s_len // TB
    hb = TB // HALO
    nqb = s_len // Q_BLOCK
    jb = TB // nqb

    def body(attn_ref, ga_ref, pin_ref, pprev_ref, pnext_ref, gp_ref, x_ref, tgt_ref, gate_ref, wp_ref, ps_ref, wo_ref,
             dxn_ref, dattn_ref, dga_ref, dgp_ref, dpool_ref, dwo_ref, dgate_ref, dps_ref, dwp_ref, loss_ref):
        i = pl.program_id(0)

        @pl.when(i == 0)
        def _():
            dwo_ref[...] = jnp.zeros_like(dwo_ref)
            dgate_ref[...] = jnp.zeros_like(dgate_ref)
            dps_ref[...] = jnp.zeros_like(dps_ref)
            dwp_ref[...] = jnp.zeros_like(dwp_ref)
            loss_ref[...] = jnp.zeros_like(loss_ref)

        attn = jnp.concatenate([attn_ref[:, jj, :] for jj in range(jb)], axis=0)
        ga = ga_ref[...]
        gp = gp_ref[...]
        pin = pin_ref[...]
        prev = jnp.where(i == 0, 0.0, pprev_ref[...])
        nxt = jnp.where(i == nq - 1, 0.0, pnext_ref[...])
        win = jnp.concatenate([prev, pin, nxt], axis=0)
        tg = i * TB + lax.broadcasted_iota(jnp.int32, (TB, 1), 0)
        pooled = []
        for g, w in enumerate(POOL_WINDOWS):
            a = win[:, g * 128:(g + 1) * 128]
            p = _shift_rows(a, -1) + a
            for step in (1, 2, 4):
                if w >= 4 * step:
                    p = _shift_rows(p, -step) + _shift_rows(p, step)
            cnt = (jnp.minimum(tg + w // 2, s_len) - jnp.maximum(tg - w // 2, 0)).astype(F32)
            pooled.append(p[HALO:HALO + TB] / cnt - a[HALO:HALO + TB])
        pooled_b = [p.astype(BF16) for p in pooled]
        z = jnp.concatenate([_nn(pooled_b[g], wp_ref[g]) for g in range(4)], axis=1)
        ps = ps_ref[...]
        yp = z * ps
        sga = _sig(ga)
        sila = ga * sga
        sgp = _sig(gp)
        silp = gp * sgp
        br = jnp.concatenate([sila * attn, silp * yp], axis=1).astype(BF16)
        y = _nn(br, wo_ref[...])
        gate = gate_ref[...]
        err = x_ref[...] + gate * y - tgt_ref[...]
        loss_ref[...] += _colsum(_rowsum(err * err)) * (0.5 / D)
        dxn = err * (1.0 / D)
        dxn_ref[...] = dxn
        dgate_ref[...] += _colsum(dxn * y)
        dy = (dxn * gate).astype(BF16)
        dwo_ref[...] += _tn(br, dy)
        dbr = _nt(dy, wo_ref[...])
        dbra = dbr[:, :512]
        dbrp = dbr[:, 512:]
        dattn = dbra * sila
        for jj in range(jb):
            dattn_ref[:, jj, :] = dattn[jj * nqb:(jj + 1) * nqb]
        dga_ref[...] = (dbra * attn * (sga * (1.0 + ga * (1.0 - sga)))).astype(BF16)
        dgp_ref[...] = (dbrp * yp * (sgp * (1.0 + gp * (1.0 - sgp)))).astype(BF16)
        dyp = dbrp * silp
        dps_ref[...] += _colsum(dyp * z)
        dz = (dyp * ps).astype(BF16)
        dpool = []
        for g in range(4):
            dzg = dz[:, g * 128:(g + 1) * 128]
            dwp_ref[g] += _tn(pooled_b[g], dzg)
            dpool.append(_nt(dzg, wp_ref[g]))
        dpool_ref[...] = jnp.concatenate(dpool, axis=1)

    lat = lambda w: pl.BlockSpec((TB, w), lambda i: (i, 0))
    perm = pl.BlockSpec((nqb, jb, 512), lambda i: (0, i, 0))
    ucol = lambda j: pl.BlockSpec((TB, 512), lambda i: (i + off, j))
    last8 = t_all // HALO - 1
    return pl.pallas_call(
        body, name="out_stage", grid=(nq,),
        in_specs=[perm, ucol(1), ucol(2),
                  pl.BlockSpec((HALO, 512), lambda i: ((i + off) * hb - 1, 2)),
                  pl.BlockSpec((HALO, 512), lambda i: (jnp.minimum((i + off + 1) * hb, last8), 2)),
                  ucol(3), lat(D), lat(D), _full((1, D)), _full((4, 128, 128)), _full((1, 512)), _full((D, D))],
        out_specs=[lat(D), perm, lat(512), lat(512), lat(512),
                   _full((D, D)), _full((1, D)), _full((1, 512)), _full((4, 128, 128)), _full((1, 1))],
        out_shape=[jax.ShapeDtypeStruct((s_len, D), F32), jax.ShapeDtypeStruct((nqb, Q_BLOCK, 512), F32),
                   jax.ShapeDtypeStruct((s_len, 512), BF16), jax.ShapeDtypeStruct((s_len, 512), BF16),
                   jax.ShapeDtypeStruct((s_len, 512), F32),
                   jax.ShapeDtypeStruct((D, D), F32), jax.ShapeDtypeStruct((1, D), F32), jax.ShapeDtypeStruct((1, 512), F32),
                   jax.ShapeDtypeStruct((4, 128, 128), F32), jax.ShapeDtypeStruct((1, 1), F32)],
        compiler_params=_params(("arbitrary",)),
    )(attn, u, u, u, u, u, x, target, gate, w_pool, pool_scale, w_out)


def _attn_bwd(q, k, v, dattn, attn, lse, s_len):
    t_all = q.shape[1]
    off = (t_all - s_len) // TB
    nq = s_len // TB
    nch = 4
    chunks = [(c * (t_all // nch), t_all // nch) for c in range(nch)]
    nsub = next(n for n in (BWD_QBLOCKS, 2, 1) if nq % n == 0)
    tq = nsub * TB

    def body(*refs):
        q_refs = refs[:nsub]
        k_ref, v_ref, do_ref, o_ref, lse_ref, dq_ref, dk_ref, dv_ref = refs[nsub:]
        i = pl.program_id(1)

        @pl.when(i == 0)
        def _():
            dk_ref[...] = jnp.zeros_like(dk_ref)
            dv_ref[...] = jnp.zeros_like(dv_ref)

        qb = jnp.concatenate([r[0] for r in q_refs], axis=0)
        delta_r = _row_layout(_rowsum(do_ref[...] * o_ref[...]))[0:1, :]
        do = do_ref[...].astype(BF16)
        lse_r = jnp.concatenate([lse_ref[0, sb][0:1, :] for sb in range(nsub)], axis=1)
        dq = jnp.zeros((tq, DKP), F32)
        for start, size in chunks:
            rows = pl.ds(start, size)
            kc = k_ref[0, rows, :]
            p_t = jnp.exp2(_nt(kc, qb) - lse_r)
            ds_t = (p_t * (_nt(v_ref[0, rows, :], do) - delta_r)).astype(BF16)
            dv_ref[0, rows, :] += _nn(p_t.astype(BF16), do)
            dk_ref[0, rows, :] += _nn(ds_t, qb)
            dq += _tn(ds_t, kc)
        dq_ref[0] = dq * SCALE

    kvspec = lambda w: pl.BlockSpec((1, t_all, w), lambda h, i: (h, 0, 0))
    rowspec = pl.BlockSpec((1, nsub, 8, TB), lambda h, i: (h, i, 0, 0))
    qspec = lambda sb: pl.BlockSpec((1, TB, DKP), lambda h, i: (h, i * nsub + sb + off, 0))
    return pl.pallas_call(
        body, name="attn_bwd", grid=(NH, nq // nsub),
        in_specs=[qspec(sb) for sb in range(nsub)]
        + [kvspec(DKP), kvspec(DV), pl.BlockSpec((tq, DV), lambda h, i: (i, h)), pl.BlockSpec((tq, DV), lambda h, i: (i, h)),
           rowspec],
        out_specs=[pl.BlockSpec((1, tq, DKP), lambda h, i: (h, i, 0)), kvspec(DKP), kvspec(DV)],
        out_shape=[jax.ShapeDtypeStruct((NH, s_len, DKP), F32), jax.ShapeDtypeStruct((NH, t_all, DKP), F32),
                   jax.ShapeDtypeStruct((NH, t_all, DV), F32)],
        compiler_params=_params(("arbitrary", "arbitrary")),
    )(*([q] * nsub), k, v, dattn, attn, lse)


def _qkv_bwd(u, dq, dk, dv, cos, sin, q_lora_g, w_uq_t, kv_lora_g, w_ukv, qn_g, kn_g, s_len):
    t_all = u.shape[0]
    off = (t_all - s_len) // TB
    nb = t_all // TB

    def body(ulo_ref, dq_ref, dk_ref, dv_ref, cos_ref, sin_ref, qlg_ref, wuq_ref, kvlg_ref, wukv_ref, qng_ref, kng_ref,
             dlo_ref, dwuq_ref, dwukv_ref, dqlg_ref, dkvlg_ref, dqng_ref, dkng_ref):
        i = pl.program_id(0)

        @pl.when(i == 0)
        def _():
            for r in (dwuq_ref, dwukv_ref, dqlg_ref, dkvlg_ref, dqng_ref, dkng_ref):
                r[...] = jnp.zeros_like(r)

        latent = i >= off
        ulo = ulo_ref[...]
        cos, sin = _rope_block(cos_ref, sin_ref, pl.program_id(0) < off)
        cq = ulo[:, 0:QL]
        rc = lax.rsqrt(jnp.mean(cq * cq, axis=-1, keepdims=True) + EPS)
        cqh = cq * rc
        qlg = qlg_ref[...]
        cqn_b = (cqh * qlg).astype(BF16)
        qng = qng_ref[...]
        ckv = ulo[:, QL:QL + KVL]
        r0 = lax.rsqrt(jnp.mean(ckv * ckv, axis=-1, keepdims=True) + EPS)
        ckvh = ckv * r0
        kvlg = kvlg_ref[...]
        ckvn_b = (ckvh * kvlg).astype(BF16)
        qhs = [_nt(cqn_b, wuq_ref[hd]) for hd in range(NH)]
        kns = [_nn(ckvn_b, wukv_ref[hd])[:, :128] for hd in range(NH)]
        dqng = jnp.zeros((1, DKP), F32)
        dqraws = []
        for hd in range(NH):
            qh = qhs[hd]
            rq = lax.rsqrt(_rowsum(qh * qh) / DK + EPS)
            xh = qh * rq
            dqh = jnp.where(latent, dq_ref[hd], 0.0)
            dyq = jnp.concatenate([dqh[:, :128], _rope_t(dqh[:, 128:], cos, sin)], axis=1)
            dqng += _colsum(dyq * xh)
            dxh = dyq * qng
            dqraws.append((rq * (dxh - xh * (_rowsum(dxh * xh) / DK))).astype(BF16))
        dqng_ref[...] += dqng

        kr = ulo[:, 384:512]
        skr = _rowsum(kr * kr)
        kng = kng_ref[...]
        dkr = jnp.zeros((TB, 128), F32)
        dkng = jnp.zeros((1, DKP), F32)
        dkvs = []
        for hd in range(NH):
            kn = kns[hd]
            rk = lax.rsqrt((_rowsum(kn * kn) + skr) / DK + EPS)
            xh1 = kn * rk
            xh2 = kr * rk
            dkh = dk_ref[hd] * LN2
            d1 = dkh[:, :128]
            d2 = _rope_t(dkh[:, 128:], cos, sin)
            dkng += jnp.concatenate([_colsum(d1 * xh1), _colsum(d2 * xh2)], axis=1)
            dx1 = d1 * kng[:, :128]
            dx2 = d2 * kng[:, 128:]
            dot = (_rowsum(dx1 * xh1) + _rowsum(dx2 * xh2)) / DK
            dkvs.append(jnp.concatenate([rk * (dx1 - xh1 * dot), dv_ref[hd]], axis=1).astype(BF16))
            dkr += rk * (dx2 - xh2 * dot)
        dkng_ref[...] += dkng

        dcqn = jnp.zeros((TB, QL), F32)
        dckvn = jnp.zeros((TB, KVL), F32)
        for hd in range(NH):
            dwuq_ref[hd] += _tn(dqraws[hd], cqn_b)[:DK]
            dcqn += _nn(dqraws[hd], wuq_ref[hd])
            dwukv_ref[hd] += _tn(ckvn_b, dkvs[hd])
            dckvn += _nt(dkvs[hd], wukv_ref[hd])
        dqlg_ref[...] += _colsum(dcqn * cqh)
        dxh = dcqn * qlg
        dcq = rc * (dxh - cqh * jnp.mean(dxh * cqh, axis=-1, keepdims=True))
        dkvlg_ref[...] += _colsum(dckvn * ckvh)
        dxh = dckvn * kvlg
        dckv = r0 * (dxh - ckvh * jnp.mean(dxh * ckvh, axis=-1, keepdims=True))
        dlo_ref[...] = jnp.concatenate([dcq, dckv, dkr], axis=1).astype(BF16)

    row = lambda w: pl.BlockSpec((TB, w), lambda i: (i, 0))
    heads = lambda w: pl.BlockSpec((NH, TB, w), lambda i: (0, i, 0))
    return pl.pallas_call(
        body, name="qkv_bwd", grid=(nb,),
        in_specs=[row(512), pl.BlockSpec((NH, TB, DKP), lambda i: (0, jnp.maximum(i - off, 0), 0)), heads(DKP), heads(DV),
                  pl.BlockSpec((1, 8, 256), lambda i: (jnp.maximum(i - off, 0), 0, 0)), _full((TB, 256)), _full((1, QL)), _full((NH, DKP, QL)), _full((1, KVL)), _full((NH, KVL, 256)),
                  _full((1, DKP)), _full((1, DKP))],
        out_specs=[row(512), _full((NH, DK, QL)), _full((NH, KVL, 256)), _full((1, QL)), _full((1, KVL)),
                   _full((1, DKP)), _full((1, DKP))],
        out_shape=[jax.ShapeDtypeStruct((t_all, 512), BF16), jax.ShapeDtypeStruct((NH, DK, QL), F32),
                   jax.ShapeDtypeStruct((NH, KVL, 256), F32), jax.ShapeDtypeStruct((1, QL), F32),
                   jax.ShapeDtypeStruct((1, KVL), F32), jax.ShapeDtypeStruct((1, DKP), F32), jax.ShapeDtypeStruct((1, DKP), F32)],
        compiler_params=_params(("arbitrary",)),
    )(u, dq, dk, dv, cos, sin, q_lora_g, w_uq_t, kv_lora_g, w_ukv, qn_g, kn_g)


def _in_bwd(ctx, x, modsel, norm_g, dlo, dga, dgp, dpool, dxn, w_in_t):
    s_len, lc = x.shape[0], ctx.shape[0]
    t_all = s_len + lc
    off = lc // TB
    nb = t_all // TB
    nq = s_len // TB
    hb = TB // HALO
    n = TB + 2 * HALO

    def body(ctx_ref, x_ref, mod_ref, ng_ref, dlo_ref, dga_ref, dgp_ref, dp_ref, dpprev_ref, dpnext_ref, dxn_ref, win_ref,
             gx_ref, dwin_ref, dmod_ref, dng_ref):
        i = pl.program_id(0)
        j = i - off

        @pl.when(i == 0)
        def _():
            dwin_ref[...] = jnp.zeros_like(dwin_ref)
            dmod_ref[...] = jnp.zeros_like(dmod_ref)
            dng_ref[...] = jnp.zeros_like(dng_ref)

        latent = i >= off
        dp = dp_ref[...]
        prev = jnp.where(j <= 0, 0.0, dpprev_ref[...])
        nxt = jnp.where(j >= nq - 1, 0.0, dpnext_ref[...])
        win = jnp.concatenate([prev, dp, nxt], axis=0)
        tg = j * TB - HALO + lax.broadcasted_iota(jnp.int32, (n, 1), 0)
        dpin = []
        for g, w in enumerate(POOL_WINDOWS):
            cnt = jnp.maximum(jnp.minimum(tg + w // 2, s_len) - jnp.maximum(tg - w // 2, 0), 1).astype(F32)
            zq = win[:, g * 128:(g + 1) * 128] / cnt
            zq = zq + _shift_rows(zq, 1)
            for step in (1, 2, 4):
                if w >= 4 * step:
                    zq = _shift_rows(zq, -step) + _shift_rows(zq, step)
            dpin.append(zq[HALO:HALO + TB] - dp[:, g * 128:(g + 1) * 128])
        zero = jnp.zeros((TB, 512), BF16)
        du = [dlo_ref[...], jnp.where(latent, dga_ref[...], zero),
              jnp.where(latent, jnp.concatenate(dpin, axis=1).astype(BF16), zero), jnp.where(latent, dgp_ref[...], zero)]

        ng = ng_ref[...]
        xb = jnp.where(i < off, ctx_ref[...], x_ref[...])
        r, xh, xg, h, scale = _modulated(xb, mod_ref, ng)
        hb_ = h.astype(BF16)
        dh = jnp.zeros((TB, D), F32)
        for s, (lo, hi) in enumerate(SEG):
            dwin_ref[lo:hi, :] += _tn(du[s], hb_)
            dh += _nn(du[s], win_ref[lo:hi, :])
        is_lat = latent.astype(F32)
        dsh = _colsum(dh)
        dsc = _colsum(dh * xg)
        dmod_ref[0, 0:1, :] += dsh * (1.0 - is_lat)
        dmod_ref[0, 1:2, :] += dsc * (1.0 - is_lat)
        dmod_ref[1, 0:1, :] += dsh * is_lat
        dmod_ref[1, 1:2, :] += dsc * is_lat
        dxg = dh * (1.0 + scale)
        dng_ref[...] += _colsum(dxg * xh)
        dxh = dxg * ng
        gx_ref[...] = r * (dxh - xh * jnp.mean(dxh * xh, axis=-1, keepdims=True)) + dxn_ref[...]

    row = lambda w: pl.BlockSpec((TB, w), lambda i: (i, 0))
    lat = lambda w: pl.BlockSpec((TB, w), lambda i: (jnp.maximum(i - off, 0), 0))
    last8 = s_len // HALO - 1
    cspec, xspec, mspec = _token_specs(off)
    return pl.pallas_call(
        body, name="in_bwd", grid=(nb,),
        in_specs=[cspec, xspec, mspec, _full((1, D)), row(512), lat(512), lat(512), lat(512),
                  pl.BlockSpec((HALO, 512), lambda i: (jnp.maximum(jnp.maximum(i - off, 0) * hb - 1, 0), 0)),
                  pl.BlockSpec((HALO, 512), lambda i: (jnp.minimum((jnp.maximum(i - off, 0) + 1) * hb, last8), 0)),
                  lat(D), _full((DIN, D))],
        out_specs=[lat(D), _full((DIN, D)), _full((2, 2, D)), _full((1, D))],
        out_shape=[jax.ShapeDtypeStruct((s_len, D), F32), jax.ShapeDtypeStruct((DIN, D), F32),
                   jax.ShapeDtypeStruct((2, 2, D), F32), jax.ShapeDtypeStruct((1, D), F32)],
        compiler_params=_params(("arbitrary",)),
    )(ctx, x, modsel, norm_g, dlo, dga, dgp, dpool, dpool, dpool, dxn, w_in_t)


def _adamw_update(w_ref, g_ref, m_ref, v_ref, d_ref, mo_ref, vo_ref):
    gv = g_ref[...]
    mn = ADAM_B1 * m_ref[...] + (1.0 - ADAM_B1) * gv
    vn = ADAM_B2 * v_ref[...] + (1.0 - ADAM_B2) * (gv * gv)
    m_hat = mn / (1.0 - ADAM_B1 ** ADAM_STEP)
    v_hat = vn / (1.0 - ADAM_B2 ** ADAM_STEP)
    d_ref[...] = -ADAM_LR * (m_hat / (jnp.sqrt(v_hat) + ADAM_EPS) + ADAM_WD * w_ref[...])
    mo_ref[...] = mn
    vo_ref[...] = vn


def _adamw_many(ws, gs, ms, vs):
    n = len(ws)

    def body(*refs):
        for i in range(n):
            _adamw_update(refs[i], refs[n + i], refs[2 * n + i], refs[3 * n + i], refs[4 * n + i], refs[5 * n + i], refs[6 * n + i])

    def spec(w):
        rows, cols = w.shape
        return pl.BlockSpec((rows // 2, cols), lambda i: (i, 0)) if rows % 16 == 0 else _full((rows, cols))

    specs = [spec(w) for w in ws]
    shp = [jax.ShapeDtypeStruct(w.shape, F32) for w in ws]
    out = pl.pallas_call(body, name="adamw_many", grid=(2,), in_specs=specs * 4, out_specs=specs * 3, out_shape=shp * 3,
                         compiler_params=_params(("arbitrary",)))(*ws, *gs, *ms, *vs)
    return out[:n], out[n:2 * n], out[2 * n:]


def _adamw(w, g, m, v, name):
    rows, cols = w.shape
    rb = next(r for r in range(min(rows, 256), 0, -8) if rows % r == 0)

    def body(w_ref, g_ref, m_ref, v_ref, d_ref, mo_ref, vo_ref):
        _adamw_update(w_ref, g_ref, m_ref, v_ref, d_ref, mo_ref, vo_ref)

    spec = pl.BlockSpec((rb, cols), lambda i: (i, 0))
    shp = jax.ShapeDtypeStruct((rows, cols), F32)
    return pl.pallas_call(
        body, name=name, grid=(rows // rb,), in_specs=[spec] * 4, out_specs=[spec] * 3, out_shape=[shp] * 3,
        compiler_params=_params(("arbitrary",)),
    )(w, g, m, v)


class _Links:
    def __init__(self, send_sems, recv_sems):
        self.send_sems, self.recv_sems, self.sends = send_sems, recv_sems, []

    def send(self, src, dst, sem, to):
        cp = pltpu.make_async_remote_copy(src, dst, self.send_sems.at[sem], self.recv_sems.at[sem], device_id=to,
                                          device_id_type=MESH)
        cp.start()
        self.sends.append(cp)

    def arrived(self, dst, sem, frm):
        pltpu.make_async_remote_copy(dst, dst, self.send_sems.at[sem], self.recv_sems.at[sem], device_id=frm,
                                     device_id_type=MESH).wait_recv()

    def drain(self):
        for cp in self.sends:
            cp.wait_send()


def _half(ref, c, axis):
    size = ref.shape[axis - 2] // 2
    win = pl.ds(pl.multiple_of(c * size, 16 if axis == 0 else 128), size)
    idx = (win, slice(None)) if axis == 0 else (slice(None), win)
    return ref.at[(slice(None),) * (len(ref.shape) - 2) + idx]


def _select_rows(slots_ref, n_slots, row=0):
    sub = lax.broadcasted_iota(jnp.int32, (8, 1), 0)
    out = None
    for d in range(n_slots):
        r = jnp.where(sub == d, jnp.broadcast_to(slots_ref[d][row:row + 1, :], (8, slots_ref.shape[-1])), 0.0)
        out = r if out is None else out + r
    return out


def _gather(c, c_ctx, w_mod, b_mod_k, shards, axes, slab_rows):
    nw = len(shards)
    kw = w_mod.shape[1]

    def body(*refs):
        c_ref, cc_ref, wm_ref, b_ref = refs[:4]
        w_refs = refs[4:4 + nw]
        a16_ref, mod_ref = refs[4 + nw:6 + nw]
        g_refs = refs[6 + nw:6 + 2 * nw]
        a_ref, send_sems, recv_sems = refs[6 + 2 * nw:]

        def slab(wi, chip):
            return g_refs[wi].at[chip].at[0:shards[wi].shape[0]]
        x, y, cc = lax.axis_index("x"), lax.axis_index("y"), lax.axis_index("c")
        me = 4 * x + 2 * y + cc
        k = 2 * x + y
        sibling = (x, y, 1 - cc)
        links = _Links(send_sems, recv_sems)
        chips = [_peer(x, y, cc, off + (0,)) for off in CHIPS3]
        chip_a = ((x + 1 - cc) % 2, (y + cc) % 2, cc)
        chip_b = ((x + cc) % 2, (y + 1 - cc) % 2, cc)
        chip_d = (1 - x, 1 - y, cc)
        cv = c_ref[...]
        a_ref[me] = jnp.broadcast_to(cv * _sig(cv), (8, D))
        for j, off in enumerate(PEERS7):
            links.send(a_ref.at[me], a_ref.at[me], j, _peer(x, y, cc, off))
        for wi in range(nw):
            slab(wi, k)[...] = w_refs[wi][...].astype(BF16)
            for j, to in enumerate((chip_a, chip_b)):
                links.send(_half(slab(wi, k), cc, axes[wi]), _half(slab(wi, k), cc, axes[wi]), 10 + wi * 6 + j, to)
            pad = slab_rows[wi] - shards[wi].shape[0]
            if pad:
                for kk in range(4):
                    g_refs[wi][kk, shards[wi].shape[0]:, :] = jnp.zeros((pad, shards[wi].shape[1]), BF16)
        for j, off in enumerate(PEERS7):
            px, py, pc = _peer(x, y, cc, off)
            links.arrived(a_ref.at[4 * px + 2 * py + pc], j, (px, py, pc))
        ccv = cc_ref[...]
        sub = lax.broadcasted_iota(jnp.int32, (8, 1), 0)
        a16 = jnp.concatenate([_select_rows(a_ref, 8), jnp.where(sub == 0, jnp.broadcast_to(ccv * _sig(ccv), (8, D)), 0.0)], axis=0)
        a16_ref[...] = a16
        mod_ref[k] = _dot3(_nn, a16, wm_ref[...]) + b_ref[...]
        for j, to in enumerate(chips):
            links.send(mod_ref.at[k], mod_ref.at[k], 7 + j, to)
        for j, (frm, origin) in enumerate(((chip_a, chip_a), (chip_b, chip_b), (chip_b, chip_d))):
            for wi in range(nw):
                blk = _half(slab(wi, 2 * origin[0] + origin[1]), cc, axes[wi])
                links.arrived(blk, 10 + wi * 6 + j, frm)
                if j == 0:
                    links.send(blk, blk, 10 + wi * 6 + 2, chip_b)
                links.send(blk, blk, 10 + wi * 6 + 3 + j, sibling)
        for j, (px, py, pc) in enumerate(chips):
            links.arrived(mod_ref.at[2 * px + py], 7 + j, (px, py, pc))
        for j, origin in enumerate((chip_b, chip_a, chip_d)):
            for wi in range(nw):
                links.arrived(_half(slab(wi, 2 * origin[0] + origin[1]), 1 - cc, axes[wi]), 10 + wi * 6 + 3 + j, sibling)
        links.drain()

    nsem = 10 + 6 * nw
    return pl.pallas_call(
        body, name="gather", in_specs=[VM] * (4 + nw), out_specs=[VM] * (2 + nw),
        out_shape=[jax.ShapeDtypeStruct((16, D), F32), jax.ShapeDtypeStruct((4, 16, kw), F32)]
        + [jax.ShapeDtypeStruct((4, r, s.shape[1]), BF16) for r, s in zip(slab_rows, shards)],
        scratch_shapes=[pltpu.VMEM((8, 8, D), F32), pltpu.SemaphoreType.DMA((nsem,)), pltpu.SemaphoreType.DMA((nsem,))],
        compiler_params=pltpu.CompilerParams(vmem_limit_bytes=VMEM_LIMIT),
    )(c, c_ctx, w_mod, b_mod_k, *shards)


SMALL_ROW_WIDTHS = (D, QL, KVL, DKP, DKP, 512, 128)
SMALL_OUT_WIDTHS = (D, QL, KVL, DK, DK, 512, 1)


def _reduce(grads, axes, smalls, w_pool_g, dmod8, a16, w_mod, c_ctx):
    nw = len(grads)
    ns = len(smalls)
    kw = w_mod.shape[1]
    halves = []
    for g, ax in zip(grads, axes):
        halves.append((g.shape[1] // 2, g.shape[2]) if ax == 0 else (g.shape[1], g.shape[2] // 2))

    def body(*refs):
        g_refs = refs[:nw]
        small_refs = refs[nw:nw + ns]
        wp_ref, dm_ref, a16_ref, wm_ref, cc_ref = refs[nw + ns:nw + ns + 5]
        o = nw + ns + 5
        r_refs = refs[o:o + nw]
        small_outs = refs[o + nw:o + nw + ns]
        rwp_ref, gw_ref, gb_ref, gc_ref = refs[o + nw + ns:o + nw + ns + 4]
        o = o + nw + ns + 4
        own, sib, oth, part, got, rel = (refs[o + i * nw:o + (i + 1) * nw] for i in range(6))
        smbuf, wps, wpg, dm_all, pc_all, send_sems, recv_sems, local_sems = refs[o + 6 * nw:]
        x, y, cc = lax.axis_index("x"), lax.axis_index("y"), lax.axis_index("c")
        me = 4 * x + 2 * y + cc
        k = 2 * x + y
        sibling = (x, y, 1 - cc)
        links = _Links(send_sems, recv_sems)
        chips = [_peer(x, y, cc, off + (0,)) for off in CHIPS3]
        peers = [_peer(x, y, cc, off) for off in PEERS7]
        chip_a = ((x + 1 - cc) % 2, (y + cc) % 2, cc)
        chip_b = ((x + cc) % 2, (y + 1 - cc) % 2, cc)
        ka, kb, kd = 2 * chip_a[0] + chip_a[1], 2 * chip_b[0] + chip_b[1], 2 * (1 - x) + (1 - y)
        big, sm0, wp0, dm0, pc0 = 0, 5 * nw, 5 * nw + 7, 5 * nw + 14, 5 * nw + 21

        locals_, staged = [], []
        for wi in range(nw):
            st = pltpu.make_async_copy(_half(g_refs[wi], 1 - cc, axes[wi]), oth[wi], local_sems.at[nw + wi])
            st.start()
            staged.append(st)
        for wi in range(nw):
            lc = pltpu.make_async_copy(_half(g_refs[wi], cc, axes[wi]), own[wi], local_sems.at[wi])
            lc.start()
            locals_.append(lc)
        for wi in range(nw):
            staged[wi].wait()
            links.send(oth[wi], sib[wi], big + wi * 5, sibling)
        slot = smbuf.at[me]
        slot[...] = jnp.zeros((8, D), F32)
        for r, (ref, w) in enumerate(zip(small_refs, SMALL_ROW_WIDTHS)):
            slot[r:r + 1, 0:w] = jnp.broadcast_to(ref[...], (1, w))
        links.send(wp_ref, wps, wp0, sibling)
        dm_all[me] = dm_ref[...]
        for j, peer in enumerate(peers):
            links.send(dm_all.at[me], dm_all.at[me], dm0 + j, peer)
            links.send(smbuf.at[me], smbuf.at[me], sm0 + j, peer)
        links.arrived(wps, wp0, sibling)
        wpg[k] = (wp_ref[...] + wps[...]).astype(BF16)
        for j, to in enumerate(chips):
            links.send(wpg.at[k], wpg.at[k], wp0 + 1 + j, to)
        for wi in range(nw):
            locals_[wi].wait()
            links.arrived(sib[wi], big + wi * 5, sibling)
            part[wi][...] = (own[wi][...] + sib[wi][...]).astype(BF16)
            got[wi][k] = part[wi][k]
            got[wi][kd] = jnp.zeros(halves[wi], BF16)
            links.send(part[wi].at[kd], rel[wi], big + wi * 5 + 1, chip_b)
            links.send(part[wi].at[kb], got[wi].at[k], big + wi * 5 + 2, chip_b)
        for j, (px, py, pc) in enumerate(peers):
            links.arrived(dm_all.at[4 * px + 2 * py + pc], dm0 + j, (px, py, pc))
        dmc = dm_all[0][1:2, :]
        dml = dm_all[0][0:1, :]
        for d in range(1, 8):
            dmc = dmc + dm_all[d][1:2, :]
            dml = dml + dm_all[d][0:1, :]
        gb_ref[...] = dml + dmc
        sub = lax.broadcasted_iota(jnp.int32, (8, 1), 0)
        b16 = jnp.concatenate([_select_rows(dm_all, 8), jnp.where(sub == 0, jnp.broadcast_to(dmc, (8, 3 * D)), 0.0)], axis=0)
        bk = jnp.zeros((16, kw), F32)
        for kk in range(4):
            bk = bk + jnp.where(k == kk, b16[:, kk * kw:(kk + 1) * kw], 0.0)
        gw_ref[...] = _dot3(_tn, a16_ref[...], bk)
        pc_all[k] = _dot3(_nt, jnp.broadcast_to(bk[8:9, :], (8, kw)), wm_ref[...])
        for j, to in enumerate(chips):
            links.send(pc_all.at[k], pc_all.at[k], pc0 + j, to)
        for j, (px, py, pc) in enumerate(peers):
            links.arrived(smbuf.at[4 * px + 2 * py + pc], sm0 + j, (px, py, pc))
        tot = smbuf[0]
        for d in range(1, 8):
            tot = tot + smbuf[d]
        for r, (ref, w) in enumerate(zip(small_outs, SMALL_OUT_WIDTHS)):
            ref[...] = tot[r:r + 1, 0:w]
        for j, (px, py, pc) in enumerate(chips):
            links.arrived(wpg.at[2 * px + py], wp0 + 1 + j, (px, py, pc))
        wpt = wpg[0].astype(F32)
        for kk in range(1, 4):
            wpt = wpt + wpg[kk].astype(F32)
        rwp_ref[...] = wpt
        for wi in range(nw):
            links.arrived(rel[wi], big + wi * 5 + 1, chip_b)
            rel[wi][...] = (part[wi][ka].astype(F32) + rel[wi][...].astype(F32)).astype(BF16)
            links.send(rel[wi], got[wi].at[k], big + wi * 5 + 3, chip_a)
        for wi in range(nw):
            links.arrived(got[wi].at[kb], big + wi * 5 + 2, chip_b)
            links.arrived(got[wi].at[ka], big + wi * 5 + 3, chip_a)
            total = got[wi][0].astype(F32)
            for kk in range(1, 4):
                total = total + got[wi][kk].astype(F32)
            mine = _half(r_refs[wi], cc, axes[wi])
            mine[...] = total
            links.send(mine, mine, big + wi * 5 + 4, sibling)
        for j, (px, py, pc) in enumerate(chips):
            links.arrived(pc_all.at[2 * px + py], pc0 + j, (px, py, pc))
        ccv = cc_ref[...]
        sg = _sig(ccv)
        gc_ref[...] = (pc_all[0][0:1, :] + pc_all[1][0:1, :] + pc_all[2][0:1, :] + pc_all[3][0:1, :]) * (sg * (1.0 + ccv * (1.0 - sg)))
        for wi in range(nw):
            links.arrived(_half(r_refs[wi], 1 - cc, axes[wi]), big + wi * 5 + 4, sibling)
        links.drain()

    nsem = 5 * nw + 24
    quads = [(4,) + h for h in halves]
    return pl.pallas_call(
        body, name="reduce", in_specs=[ANY] * nw + [VM] * (ns + 5), out_specs=[VM] * (nw + ns + 4),
        out_shape=[jax.ShapeDtypeStruct(g.shape[1:], F32) for g in grads]
        + [jax.ShapeDtypeStruct((1, w), F32) for w in SMALL_OUT_WIDTHS]
        + [jax.ShapeDtypeStruct(w_pool_g.shape, F32), jax.ShapeDtypeStruct((D, kw), F32), jax.ShapeDtypeStruct((1, 3 * D), F32),
           jax.ShapeDtypeStruct((1, D), F32)],
        scratch_shapes=[pltpu.VMEM(q, F32) for q in quads] + [pltpu.VMEM(q, F32) for q in quads] + [pltpu.VMEM(q, F32) for q in quads]
        + [pltpu.VMEM(q, BF16) for q in quads] + [pltpu.VMEM(q, BF16) for q in quads] + [pltpu.VMEM(h, BF16) for h in halves]
        + [pltpu.VMEM((8, 8, D), F32), pltpu.VMEM(w_pool_g.shape, F32), pltpu.VMEM((4,) + w_pool_g.shape, BF16),
           pltpu.VMEM((8, 8, 3 * D), F32),
           pltpu.VMEM((4, 8, D), F32)]
        + [pltpu.SemaphoreType.DMA((nsem,)), pltpu.SemaphoreType.DMA((nsem,)), pltpu.SemaphoreType.DMA((2 * nw,))],
        compiler_params=pltpu.CompilerParams(vmem_limit_bytes=VMEM_LIMIT),
    )(*grads, *smalls, w_pool_g, dmod8, a16, w_mod, c_ctx)


def _rope_tables(s_len):
    rows = s_len // GRID_W
    per = TB // GRID_W
    n_freq = 16
    inv = ROPE_BASE ** (-jnp.arange(n_freq, dtype=F32) / n_freq)
    ang_r = jnp.arange(rows, dtype=F32)[:, None] * inv
    ang_c = jnp.arange(GRID_W, dtype=F32)[:, None] * inv
    by_row, by_col = [], []
    for fn, pad in ((jnp.cos, 1.0), (jnp.sin, 0.0)):
        r = jnp.concatenate([fn(ang_r), fn(ang_r), jnp.zeros((rows, 96), F32)], axis=1).reshape(rows // per, per, 128)
        by_row.append(jnp.pad(r, ((0, 0), (0, 8 - per), (0, 0))))
        cpart = jnp.concatenate([jnp.zeros((GRID_W, 32), F32), fn(ang_c), fn(ang_c), jnp.full((GRID_W, 64), pad, F32)], axis=1)
        by_col.append(jnp.tile(cpart, (per, 1)))
    return jnp.concatenate(by_row, axis=-1), jnp.concatenate(by_col, axis=-1)


def kernel(x, c, ctx, c_ctx, w_mod, b_mod, norm_g, w_in, q_lora_g, w_uq, kv_lora_g, w_ukv, q_norm_g, k_norm_g, w_pool, pool_scale, w_out, loss_target, m_c_ctx, m_w_mod, m_b_mod, m_norm_g, m_w_in, m_q_lora_g, m_w_uq, m_kv_lora_g, m_w_ukv, m_q_norm_g, m_k_norm_g, m_w_pool, m_pool_scale, m_w_out, v_c_ctx, v_w_mod, v_b_mod, v_norm_g, v_w_in, v_q_lora_g, v_w_uq, v_kv_lora_g, v_w_ukv, v_q_norm_g, v_k_norm_g, v_w_pool, v_pool_scale, v_w_out):
    xi, yi, ci = lax.axis_index("x"), lax.axis_index("y"), lax.axis_index("c")
    me = 4 * xi + 2 * yi + ci
    k = 2 * xi + yi
    s_len = x.shape[1]
    lc = ctx.shape[1]
    kw = w_mod.shape[2]
    weights = dict(c_ctx=c_ctx, w_mod=w_mod, b_mod=b_mod, norm_g=norm_g, w_in=w_in, q_lora_g=q_lora_g, w_uq=w_uq,
                   kv_lora_g=kv_lora_g, w_ukv=w_ukv, q_norm_g=q_norm_g, k_norm_g=k_norm_g, w_pool=w_pool,
                   pool_scale=pool_scale, w_out=w_out)
    m_in = dict(c_ctx=m_c_ctx, w_mod=m_w_mod, b_mod=m_b_mod, norm_g=m_norm_g, w_in=m_w_in, q_lora_g=m_q_lora_g, w_uq=m_w_uq,
                kv_lora_g=m_kv_lora_g, w_ukv=m_w_ukv, q_norm_g=m_q_norm_g, k_norm_g=m_k_norm_g, w_pool=m_w_pool,
                pool_scale=m_pool_scale, w_out=m_w_out)
    v_in = dict(c_ctx=v_c_ctx, w_mod=v_w_mod, b_mod=v_b_mod, norm_g=v_norm_g, w_in=v_w_in, q_lora_g=v_q_lora_g, w_uq=v_w_uq,
                kv_lora_g=v_kv_lora_g, w_ukv=v_w_ukv, q_norm_g=v_q_norm_g, k_norm_g=v_k_norm_g, w_pool=v_w_pool,
                pool_scale=v_pool_scale, w_out=v_w_out)
    order = ["c_ctx", "w_mod", "b_mod", "norm_g", "w_in", "q_lora_g", "w_uq", "kv_lora_g", "w_ukv", "q_norm_g", "k_norm_g",
             "w_pool", "pool_scale", "w_out"]
    transposed = ("w_in", "w_uq")
    as2d = lambda n, a: jnp.transpose(a[0]) if n in transposed else a.reshape(-1, a.shape[-1])
    back = lambda n, a: jnp.transpose(a)[None] if n in transposed else a.reshape(weights[n].shape)

    c_ctx2 = c_ctx.reshape(1, D)
    b_mod_k = lax.dynamic_slice(b_mod, (0, k * kw), (1, kw))
    split = (1, 0, 0, 0)
    a16, mod_all, g_in, g_uq, g_ukv, g_out = _gather(
        c, c_ctx2, w_mod[0], b_mod_k, [as2d("w_in", w_in), as2d("w_uq", w_uq), w_ukv[0], w_out[0]], split,
        (DIN // 4, DKP, KVL, D // 4))
    mod_me = lax.dynamic_index_in_dim(mod_all, me, axis=1, keepdims=False).reshape(3, D)
    mod_c = mod_all[:, 8, :].reshape(3, D)
    modsel = jnp.stack([mod_c, mod_me])
    w_in_t = g_in.reshape(DIN, D)
    w_uq_t = g_uq
    w_out_f = g_out.reshape(D, D)
    qn_g = jnp.pad(q_norm_g, ((0, 0), (0, DKP - DK)))
    kn_g = jnp.pad(k_norm_g, ((0, 0), (0, DKP - DK)))
    w_pool_b = w_pool[0].astype(BF16)
    cos, sin = _rope_tables(s_len)

    u, q, kk, v = _fwd_in(ctx[0], x[0], modsel, norm_g, w_in_t, q_lora_g, w_uq_t, kv_lora_g, g_ukv, qn_g, kn_g, cos, sin)
    attn, lse = _attn_fwd(q, kk, v, s_len)
    (dxn, dattn, dga, dgp, dpool, dw_out, dgate, dps, dw_pool, loss) = _out_stage(
        attn.reshape(s_len // Q_BLOCK, Q_BLOCK, NH * DV), u, x[0], loss_target[0], modsel[1, 2:3, :], w_pool_b, pool_scale,
        w_out_f, lc)
    dattn = dattn.reshape(s_len, NH * DV)
    dq, dk, dv = _attn_bwd(q, kk, v, dattn, attn, lse, s_len)
    dlo, dw_uq_t, dw_ukv, dqlg, dkvlg, dqng, dkng = _qkv_bwd(u, dq, dk, dv, cos, sin, q_lora_g, w_uq_t, kv_lora_g, g_ukv,
                                                            qn_g, kn_g, s_len)
    gx, dw_in_t, dmod, dng = _in_bwd(ctx[0], x[0], modsel, norm_g, dlo, dga, dgp, dpool, dxn, w_in_t)

    dmod_l = jnp.concatenate([dmod[1, 0], dmod[1, 1], dgate[0]]).reshape(1, 3 * D)
    dmod_c = jnp.concatenate([dmod[0, 0], dmod[0, 1], jnp.zeros((D,), F32)]).reshape(1, 3 * D)
    dmod8 = jnp.concatenate([dmod_l, dmod_c, jnp.zeros((6, 3 * D), F32)], axis=0)
    (r_in, r_uq, r_ukv, r_out, g_ng, g_qlg, g_kvlg, g_qng, g_kng, g_ps, loss_all, g_wp, g_w_mod, g_b_mod, g_c_ctx) = _reduce(
        [dw_in_t.reshape(4, DIN // 4, D), dw_uq_t, dw_ukv, dw_out.reshape(4, D // 4, D)], split,
        [dng, dqlg, dkvlg, dqng, dkng, dps, loss], dw_pool, dmod8, a16, w_mod[0], c_ctx2)
    g2d = dict(c_ctx=g_c_ctx, b_mod=g_b_mod, w_mod=g_w_mod, w_in=r_in, w_uq=r_uq, w_ukv=r_ukv, w_out=r_out, norm_g=g_ng,
               q_lora_g=g_qlg, kv_lora_g=g_kvlg, q_norm_g=g_qng, k_norm_g=g_kng, pool_scale=g_ps, w_pool=g_wp.reshape(512, 128))

    d2d, m2d, v2d = {}, {}, {}
    d2d["w_mod"], m2d["w_mod"], v2d["w_mod"] = _adamw(as2d("w_mod", w_mod), g2d["w_mod"], as2d("w_mod", m_w_mod),
                                                      as2d("w_mod", v_w_mod), "adamw_w_mod")
    rest = [n for n in order if n != "w_mod"]
    outs = _adamw_many([as2d(n, weights[n]) for n in rest], [g2d[n] for n in rest], [as2d(n, m_in[n]) for n in rest],
                       [as2d(n, v_in[n]) for n in rest])
    for dst, arrs in zip((d2d, m2d, v2d), outs):
        dst.update(dict(zip(rest, arrs)))

    return (loss_all[0, 0], gx[None], *[back(n, g2d[n]) for n in order], *[back(n, d2d[n]) for n in order],
            *[back(n, m2d[n]) for n in order], *[back(n, v2d[n]) for n in order])
```

```python
import jax
import jax.numpy as jnp
from jax import lax
from jax.experimental import pallas as pl
from jax.experimental.pallas import tpu as pltpu

F32 = jnp.float32
BF16 = jnp.bfloat16
MESH = pl.DeviceIdType.MESH

D = 1024
NH = 4
DK = 192
DKP = 256
DV = 128
QL = 256
KVL = 128
DIN = 1984
U_LO = 448
SEG = ((0, 512), (448, 960), (960, 1472), (1472, 1984))
DU = 2048
POOL_WINDOWS = (2, 4, 8, 16)
HALO = 8
EPS = 1e-6
ROPE_BASE = 10000.0
GRID_W = 64
Q_BLOCK = 128
TB = 256
BWD_QBLOCKS = 1
SCALE = DK ** -0.5
LOG2E = 1.4426950408889634
LN2 = 0.6931471805599453
VMEM_LIMIT = 56 * 1024 * 1024

ADAM_LR = 0.001
ADAM_B1 = 0.9
ADAM_B2 = 0.999
ADAM_EPS = 1e-08
ADAM_WD = 0.01
ADAM_STEP = 10

CHIPS3 = ((1, 0), (0, 1), (1, 1))
PEERS7 = tuple((dx, dy, dc) for dx in (0, 1) for dy in (0, 1) for dc in (0, 1) if (dx, dy, dc) != (0, 0, 0))

VM = pl.BlockSpec(memory_space=pltpu.VMEM)
ANY = pl.BlockSpec(memory_space=pl.ANY)


def _nn(a, b):
    return jnp.dot(a, b, preferred_element_type=F32)


def _nt(a, b):
    return lax.dot_general(a, b, (((1,), (1,)), ((), ())), preferred_element_type=F32)


def _tn(a, b):
    return lax.dot_general(a, b, (((0,), (0,)), ((), ())), preferred_element_type=F32)


def _split3(a):
    a0 = a.astype(BF16)
    r = a - a0.astype(F32)
    a1 = r.astype(BF16)
    a2 = (r - a1.astype(F32)).astype(BF16)
    return a0, a1, a2


def _dot3(dot, a, b):
    sa = _split3(a)
    sb = _split3(b)
    out = None
    for i in range(3):
        for j in range(3 - i):
            t = dot(sa[i], sb[j])
            out = t if out is None else out + t
    return out


def _sig(x):
    return 1.0 / (1.0 + jnp.exp(-x))


def _rot(t):
    src = lax.broadcasted_iota(jnp.int32, (128, 128), 0)
    dst = lax.broadcasted_iota(jnp.int32, (128, 128), 1)
    first = (dst % 32) < 16
    perm = jnp.where(first & (src == dst + 16), -1.0, jnp.where(~first & (src == dst - 16), 1.0, 0.0)).astype(BF16)
    hi = t.astype(BF16)
    lo = (t - hi.astype(F32)).astype(BF16)
    return _nn(hi, perm) + _nn(lo, perm)


def _rope(t, cos, sin):
    return t * cos + _rot(t) * sin


def _rope_t(t, cos, sin):
    return t * cos - _rot(t * sin)


def _rope_block(rows_ref, cols_ref, is_ctx):
    lane = lax.broadcasted_iota(jnp.int32, (TB, 256), 1) % 128
    rows = jnp.concatenate([jnp.broadcast_to(rows_ref[0, r:r + 1, :], (GRID_W, 256)) for r in range(TB // GRID_W)], axis=0)
    cs = jnp.where(lane < 32, rows, cols_ref[...])
    return jnp.where(is_ctx, 1.0, cs[:, :128]), jnp.where(is_ctx, 0.0, cs[:, 128:])


def _shift_rows(z, k):
    n = z.shape[0]
    return pltpu.roll(z, (n - k) % n, 0)


def _colsum(a):
    return jnp.sum(a, axis=0, keepdims=True)


def _rowsum(a):
    return jnp.sum(a, axis=-1, keepdims=True)


def _row_layout(col):
    return jnp.transpose(jnp.broadcast_to(col, (col.shape[0], 128)))[0:8, :]


def _params(sem=None):
    return pltpu.CompilerParams(dimension_semantics=sem, vmem_limit_bytes=VMEM_LIMIT)


def _full(shape):
    nd = len(shape)
    return pl.BlockSpec(shape, lambda *_: (0,) * nd)


def _peer(x, y, c, off):
    dx, dy, dc = off
    return ((x + dx) % 2, (y + dy) % 2, (c + dc) % 2)


def _token_specs(off):
    ctx = pl.BlockSpec((TB, D), lambda i: (jnp.minimum(i, off - 1), 0))
    lat = pl.BlockSpec((TB, D), lambda i: (jnp.maximum(i - off, 0), 0))
    mod = pl.BlockSpec((1, 3, D), lambda i: (jnp.minimum(i // off, 1), 0, 0))
    return ctx, lat, mod


def _modulated(x, mod_ref, ng):
    shift = mod_ref[0, 0:1, :]
    scale = mod_ref[0, 1:2, :]
    r = lax.rsqrt(jnp.mean(x * x, axis=-1, keepdims=True) + EPS)
    xh = x * r
    xg = xh * ng
    return r, xh, xg, xg * (1.0 + scale) + shift, scale


def _fwd_in(ctx, x, modsel, norm_g, w_in_t, q_lora_g, w_uq_t, kv_lora_g, w_ukv, qn_g, kn_g, cos, sin):
    s_len, lc = x.shape[0], ctx.shape[0]
    t_all = s_len + lc
    nb = t_all // TB
    off = lc // TB

    def body(ctx_ref, x_ref, mod_ref, ng_ref, win_ref, qlg_ref, wuq_ref, kvlg_ref, wukv_ref, qng_ref, kng_ref, cos_ref, sin_ref,
             u_ref, q_ref, k_ref, v_ref):
        is_ctx = pl.program_id(0) < off
        xb = jnp.where(is_ctx, ctx_ref[...], x_ref[...])
        _, _, _, h, _ = _modulated(xb, mod_ref, ng_ref[...])
        hb = h.astype(BF16)
        lane = lax.broadcasted_iota(jnp.int32, (TB, 512), 1)
        ulo = jnp.where(lane < U_LO, _nt(hb, win_ref[SEG[0][0]:SEG[0][1], :]), 0.0)
        u_ref[:, 0:512] = ulo
        for j in range(1, 4):
            u_ref[:, j * 512:(j + 1) * 512] = _nt(hb, win_ref[SEG[j][0]:SEG[j][1], :])
        cos, sin = _rope_block(cos_ref, sin_ref, is_ctx)
        cq = ulo[:, 0:QL]
        cqn = (cq * lax.rsqrt(jnp.mean(cq * cq, axis=-1, keepdims=True) + EPS) * qlg_ref[...]).astype(BF16)
        qng = qng_ref[...]
        ckv = ulo[:, QL:QL + KVL]
        ckvn = (ckv * lax.rsqrt(jnp.mean(ckv * ckv, axis=-1, keepdims=True) + EPS) * kvlg_ref[...]).astype(BF16)
        qhs = [_nt(cqn, wuq_ref[hd]) for hd in range(NH)]
        kvs = [_nn(ckvn, wukv_ref[hd]) for hd in range(NH)]
        for hd in range(NH):
            qh = qhs[hd]
            qn = qh * lax.rsqrt(_rowsum(qh * qh) / DK + EPS) * qng
            q_ref[hd] = (jnp.concatenate([qn[:, :128], _rope(qn[:, 128:], cos, sin)], axis=1) * (SCALE * LOG2E)).astype(BF16)
        kr = ulo[:, 384:512]
        skr = _rowsum(kr * kr)
        kng = kng_ref[...]
        kr_roped = _rope(kr * kng[:, 128:], cos, sin)
        for hd in range(NH):
            kv = kvs[hd]
            kn = kv[:, :128]
            rk = lax.rsqrt((_rowsum(kn * kn) + skr) / DK + EPS)
            k_ref[hd] = jnp.concatenate([kn * rk * kng[:, :128], kr_roped * rk], axis=1).astype(BF16)
            v_ref[hd] = kv[:, 128:].astype(BF16)

    row = lambda w: pl.BlockSpec((TB, w), lambda i: (i, 0))
    heads = lambda w: pl.BlockSpec((NH, TB, w), lambda i: (0, i, 0))
    cspec, xspec, mspec = _token_specs(off)
    return pl.pallas_call(
        body, name="fwd_in", grid=(nb,),
        in_specs=[cspec, xspec, mspec, _full((1, D)), _full((DIN, D)), _full((1, QL)), _full((NH, DKP, QL)), _full((1, KVL)),
                  _full((NH, KVL, 256)), _full((1, DKP)), _full((1, DKP)),
                  pl.BlockSpec((1, 8, 256), lambda i: (jnp.maximum(i - off, 0), 0, 0)), _full((TB, 256))],
        out_specs=[row(DU), heads(DKP), heads(DKP), heads(DV)],
        out_shape=[jax.ShapeDtypeStruct((t_all, DU), F32), jax.ShapeDtypeStruct((NH, t_all, DKP), BF16),
                   jax.ShapeDtypeStruct((NH, t_all, DKP), BF16), jax.ShapeDtypeStruct((NH, t_all, DV), BF16)],
        compiler_params=_params(("arbitrary",)),
    )(ctx, x, modsel, norm_g, w_in_t, q_lora_g, w_uq_t, kv_lora_g, w_ukv, qn_g, kn_g, cos, sin)


def _attn_fwd(q, k, v, s_len):
    t_all = q.shape[1]
    off = (t_all - s_len) // TB
    nq = s_len // TB
    nsub = next(n for n in (4, 2, 1) if nq % n == 0)

    def body(*refs):
        q_refs = refs[:nsub]
        k_ref, v_ref, o_ref, lse_ref = refs[nsub:]
        for sb in range(nsub):
            s = _nt(q_refs[sb][0], k_ref[0])
            m = jnp.max(s, axis=-1, keepdims=True)
            e = jnp.exp2(s - m)
            l = _rowsum(e)
            o_ref[sb * TB:(sb + 1) * TB, :] = _nn(e.astype(BF16), v_ref[0]) / l
            lse_ref[0, sb] = _row_layout(m + jnp.log2(l))

    qspec = lambda sb: pl.BlockSpec((1, TB, DKP), lambda h, i: (h, i * nsub + sb + off, 0))
    return pl.pallas_call(
        body, name="attn_fwd", grid=(NH, nq // nsub),
        in_specs=[qspec(sb) for sb in range(nsub)]
        + [pl.BlockSpec((1, t_all, DKP), lambda h, i: (h, 0, 0)), pl.BlockSpec((1, t_all, DV), lambda h, i: (h, 0, 0))],
        out_specs=[pl.BlockSpec((nsub * TB, DV), lambda h, i: (i, h)), pl.BlockSpec((1, nsub, 8, TB), lambda h, i: (h, i, 0, 0))],
        out_shape=[jax.ShapeDtypeStruct((s_len, NH * DV), F32), jax.ShapeDtypeStruct((NH, nq, 8, TB), F32)],
        compiler_params=_params(("arbitrary", "arbitrary")),
    )(*([q] * nsub), k, v)


def _out_stage(attn, u, x, target, gate, w_pool, pool_scale, w_out, lc):
    s_len = x.shape[0]
    t_all = s_len + lc
    off = lc // TB
    nq = s_len // TB
    hb = TB // HALO
    nqb = s_len // Q_BLOCK
    jb = TB // nqb

    def body(attn_ref, ga_ref, pin_ref, pprev_ref, pnext_ref, gp_ref, x_ref, tgt_ref, gate_ref, wp_ref, ps_ref, wo_ref,
             dxn_ref, dattn_ref, dga_ref, dgp_ref, dpool_ref, dwo_ref, dgate_ref, dps_ref, dwp_ref, loss_ref):
        i = pl.program_id(0)

        @pl.when(i == 0)
        def _():
            dwo_ref[...] = jnp.zeros_like(dwo_ref)
            dgate_ref[...] = jnp.zeros_like(dgate_ref)
            dps_ref[...] = jnp.zeros_like(dps_ref)
            dwp_ref[...] = jnp.zeros_like(dwp_ref)
            loss_ref[...] = jnp.zeros_like(loss_ref)

        attn = jnp.concatenate([attn_ref[:, jj, :] for jj in range(jb)], axis=0)
        ga = ga_ref[...]
        gp = gp_ref[...]
        pin = pin_ref[...]
        prev = jnp.where(i == 0, 0.0, pprev_ref[...])
        nxt = jnp.where(i == nq - 1, 0.0, pnext_ref[...])
        win = jnp.concatenate([prev, pin, nxt], axis=0)
        tg = i * TB + lax.broadcasted_iota(jnp.int32, (TB, 1), 0)
        pooled = []
        for g, w in enumerate(POOL_WINDOWS):
            a = win[:, g * 128:(g + 1) * 128]
            p = _shift_rows(a, -1) + a
            for step in (1, 2, 4):
                if w >= 4 * step:
                    p = _shift_rows(p, -step) + _shift_rows(p, step)
            cnt = (jnp.minimum(tg + w // 2, s_len) - jnp.maximum(tg - w // 2, 0)).astype(F32)
            pooled.append(p[HALO:HALO + TB] / cnt - a[HALO:HALO + TB])
        pooled_b = [p.astype(BF16) for p in pooled]
        z = jnp.concatenate([_nn(pooled_b[g], wp_ref[g]) for g in range(4)], axis=1)
        ps = ps_ref[...]
        yp = z * ps
        sga = _sig(ga)
        sila = ga * sga
        sgp = _sig(gp)
        silp = gp * sgp
        br = jnp.concatenate([sila * attn, silp * yp], axis=1).astype(BF16)
        y = _nn(br, wo_ref[...])
        gate = gate_ref[...]
        err = x_ref[...] + gate * y - tgt_ref[...]
        loss_ref[...] += _colsum(_rowsum(err * err)) * (0.5 / D)
        dxn = err * (1.0 / D)
        dxn_ref[...] = dxn
        dgate_ref[...] += _colsum(dxn * y)
        dy = (dxn * gate).astype(BF16)
        dwo_ref[...] += _tn(br, dy)
        dbr = _nt(dy, wo_ref[...])
        dbra = dbr[:, :512]
        dbrp = dbr[:, 512:]
        dattn = dbra * sila
        for jj in range(jb):
            dattn_ref[:, jj, :] = dattn[jj * nqb:(jj + 1) * nqb]
        dga_ref[...] = (dbra * attn * (sga * (1.0 + ga * (1.0 - sga)))).astype(BF16)
        dgp_ref[...] = (dbrp * yp * (sgp * (1.0 + gp * (1.0 - sgp)))).astype(BF16)
        dyp = dbrp * silp
        dps_ref[...] += _colsum(dyp * z)
        dz = (dyp * ps).astype(BF16)
        dpool = []
        for g in range(4):
            dzg = dz[:, g * 128:(g + 1) * 128]
            dwp_ref[g] += _tn(pooled_b[g], dzg)
            dpool.append(_nt(dzg, wp_ref[g]))
        dpool_ref[...] = jnp.concatenate(dpool, axis=1)

    lat = lambda w: pl.BlockSpec((TB, w), lambda i: (i, 0))
    perm = pl.BlockSpec((nqb, jb, 512), lambda i: (0, i, 0))
    ucol = lambda j: pl.BlockSpec((TB, 512), lambda i: (i + off, j))
    last8 = t_all // HALO - 1
    return pl.pallas_call(
        body, name="out_stage", grid=(nq,),
        in_specs=[perm, ucol(1), ucol(2),
                  pl.BlockSpec((HALO, 512), lambda i: ((i + off) * hb - 1, 2)),
                  pl.BlockSpec((HALO, 512), lambda i: (jnp.minimum((i + off + 1) * hb, last8), 2)),
                  ucol(3), lat(D), lat(D), _full((1, D)), _full((4, 128, 128)), _full((1, 512)), _full((D, D))],
        out_specs=[lat(D), perm, lat(512), lat(512), lat(512),
                   _full((D, D)), _full((1, D)), _full((1, 512)), _full((4, 128, 128)), _full((1, 1))],
        out_shape=[jax.ShapeDtypeStruct((s_len, D), F32), jax.ShapeDtypeStruct((nqb, Q_BLOCK, 512), F32),
                   jax.ShapeDtypeStruct((s_len, 512), BF16), jax.ShapeDtypeStruct((s_len, 512), BF16),
                   jax.ShapeDtypeStruct((s_len, 512), F32),
                   jax.ShapeDtypeStruct((D, D), F32), jax.ShapeDtypeStruct((1, D), F32), jax.ShapeDtypeStruct((1, 512), F32),
                   jax.ShapeDtypeStruct((4, 128, 128), F32), jax.ShapeDtypeStruct((1, 1), F32)],
        compiler_params=_params(("arbitrary",)),
    )(attn, u, u, u, u, u, x, target, gate, w_pool, pool_scale, w_out)


def _attn_bwd(q, k, v, dattn, attn, lse, s_len):
    t_all = q.shape[1]
    off = (t_all - s_len) // TB
    nq = s_len // TB
    nch = 4
    chunks = [(c * (t_all // nch), t_all // nch) for c in range(nch)]
    nsub = next(n for n in (BWD_QBLOCKS, 2, 1) if nq % n == 0)
    tq = nsub * TB

    def body(*refs):
        q_refs = refs[:nsub]
        k_ref, v_ref, do_ref, o_ref, lse_ref, dq_ref, dk_ref, dv_ref = refs[nsub:]
        i = pl.program_id(1)

        @pl.when(i == 0)
        def _():
            dk_ref[...] = jnp.zeros_like(dk_ref)
            dv_ref[...] = jnp.zeros_like(dv_ref)

        qb = jnp.concatenate([r[0] for r in q_refs], axis=0)
        delta_r = _row_layout(_rowsum(do_ref[...] * o_ref[...]))[0:1, :]
        do = do_ref[...].astype(BF16)
        lse_r = jnp.concatenate([lse_ref[0, sb][0:1, :] for sb in range(nsub)], axis=1)
        dq = jnp.zeros((tq, DKP), F32)
        for start, size in chunks:
            rows = pl.ds(start, size)
            kc = k_ref[0, rows, :]
            p_t = jnp.exp2(_nt(kc, qb) - lse_r)
            ds_t = (p_t * (_nt(v_ref[0, rows, :], do) - delta_r)).astype(BF16)
            dv_ref[0, rows, :] += _nn(p_t.astype(BF16), do)
            dk_ref[0, rows, :] += _nn(ds_t, qb)
            dq += _tn(ds_t, kc)
        dq_ref[0] = dq * SCALE

    kvspec = lambda w: pl.BlockSpec((1, t_all, w), lambda h, i: (h, 0, 0))
    rowspec = pl.BlockSpec((1, nsub, 8, TB), lambda h, i: (h, i, 0, 0))
    qspec = lambda sb: pl.BlockSpec((1, TB, DKP), lambda h, i: (h, i * nsub + sb + off, 0))
    return pl.pallas_call(
        body, name="attn_bwd", grid=(NH, nq // nsub),
        in_specs=[qspec(sb) for sb in range(nsub)]
        + [kvspec(DKP), kvspec(DV), pl.BlockSpec((tq, DV), lambda h, i: (i, h)), pl.BlockSpec((tq, DV), lambda h, i: (i, h)),
           rowspec],
        out_specs=[pl.BlockSpec((1, tq, DKP), lambda h, i: (h, i, 0)), kvspec(DKP), kvspec(DV)],
        out_shape=[jax.ShapeDtypeStruct((NH, s_len, DKP), F32), jax.ShapeDtypeStruct((NH, t_all, DKP), F32),
                   jax.ShapeDtypeStruct((NH, t_all, DV), F32)],
        compiler_params=_params(("arbitrary", "arbitrary")),
    )(*([q] * nsub), k, v, dattn, attn, lse)


def _qkv_bwd(u, dq, dk, dv, cos, sin, q_lora_g, w_uq_t, kv_lora_g, w_ukv, qn_g, kn_g, s_len):
    t_all = u.shape[0]
    off = (t_all - s_len) // TB
    nb = t_all // TB

    def body(ulo_ref, dq_ref, dk_ref, dv_ref, cos_ref, sin_ref, qlg_ref, wuq_ref, kvlg_ref, wukv_ref, qng_ref, kng_ref,
             dlo_ref, dwuq_ref, dwukv_ref, dqlg_ref, dkvlg_ref, dqng_ref, dkng_ref):
        i = pl.program_id(0)

        @pl.when(i == 0)
        def _():
            for r in (dwuq_ref, dwukv_ref, dqlg_ref, dkvlg_ref, dqng_ref, dkng_ref):
                r[...] = jnp.zeros_like(r)

        latent = i >= off
        ulo = ulo_ref[...]
        cos, sin = _rope_block(cos_ref, sin_ref, pl.program_id(0) < off)
        cq = ulo[:, 0:QL]
        rc = lax.rsqrt(jnp.mean(cq * cq, axis=-1, keepdims=True) + EPS)
        cqh = cq * rc
        qlg = qlg_ref[...]
        cqn_b = (cqh * qlg).astype(BF16)
        qng = qng_ref[...]
        ckv = ulo[:, QL:QL + KVL]
        r0 = lax.rsqrt(jnp.mean(ckv * ckv, axis=-1, keepdims=True) + EPS)
        ckvh = ckv * r0
        kvlg = kvlg_ref[...]
        ckvn_b = (ckvh * kvlg).astype(BF16)
        qhs = [_nt(cqn_b, wuq_ref[hd]) for hd in range(NH)]
        kns = [_nn(ckvn_b, wukv_ref[hd])[:, :128] for hd in range(NH)]
        dqng = jnp.zeros((1, DKP), F32)
        dqraws = []
        for hd in range(NH):
            qh = qhs[hd]
            rq = lax.rsqrt(_rowsum(qh * qh) / DK + EPS)
            xh = qh * rq
            dqh = jnp.where(latent, dq_ref[hd], 0.0)
            dyq = jnp.concatenate([dqh[:, :128], _rope_t(dqh[:, 128:], cos, sin)], axis=1)
            dqng += _colsum(dyq * xh)
            dxh = dyq * qng
            dqraws.append((rq * (dxh - xh * (_rowsum(dxh * xh) / DK))).astype(BF16))
        dqng_ref[...] += dqng

        kr = ulo[:, 384:512]
        skr = _rowsum(kr * kr)
        kng = kng_ref[...]
        dkr = jnp.zeros((TB, 128), F32)
        dkng = jnp.zeros((1, DKP), F32)
        dkvs = []
        for hd in range(NH):
            kn = kns[hd]
            rk = lax.rsqrt((_rowsum(kn * kn) + skr) / DK + EPS)
            xh1 = kn * rk
            xh2 = kr * rk
            dkh = dk_ref[hd] * LN2
            d1 = dkh[:, :128]
            d2 = _rope_t(dkh[:, 128:], cos, sin)
            dkng += jnp.concatenate([_colsum(d1 * xh1), _colsum(d2 * xh2)], axis=1)
            dx1 = d1 * kng[:, :128]
            dx2 = d2 * kng[:, 128:]
            dot = (_rowsum(dx1 * xh1) + _rowsum(dx2 * xh2)) / DK
            dkvs.append(jnp.concatenate([rk * (dx1 - xh1 * dot), dv_ref[hd]], axis=1).astype(BF16))
            dkr += rk * (dx2 - xh2 * dot)
        dkng_ref[...] += dkng

        dcqn = jnp.zeros((TB, QL), F32)
        dckvn = jnp.zeros((TB, KVL), F32)
        for hd in range(NH):
            dwuq_ref[hd] += _tn(dqraws[hd], cqn_b)[:DK]
            dcqn += _nn(dqraws[hd], wuq_ref[hd])
            dwukv_ref[hd] += _tn(ckvn_b, dkvs[hd])
            dckvn += _nt(dkvs[hd], wukv_ref[hd])
        dqlg_ref[...] += _colsum(dcqn * cqh)
        dxh = dcqn * qlg
        dcq = rc * (dxh - cqh * jnp.mean(dxh * cqh, axis=-1, keepdims=True))
        dkvlg_ref[...] += _colsum(dckvn * ckvh)
        dxh = dckvn * kvlg
        dckv = r0 * (dxh - ckvh * jnp.mean(dxh * ckvh, axis=-1, keepdims=True))
        dlo_ref[...] = jnp.concatenate([dcq, dckv, dkr], axis=1).astype(BF16)

    row = lambda w: pl.BlockSpec((TB, w), lambda i: (i, 0))
    heads = lambda w: pl.BlockSpec((NH, TB, w), lambda i: (0, i, 0))
    return pl.pallas_call(
        body, name="qkv_bwd", grid=(nb,),
        in_specs=[row(512), pl.BlockSpec((NH, TB, DKP), lambda i: (0, jnp.maximum(i - off, 0), 0)), heads(DKP), heads(DV),
                  pl.BlockSpec((1, 8, 256), lambda i: (jnp.maximum(i - off, 0), 0, 0)), _full((TB, 256)), _full((1, QL)), _full((NH, DKP, QL)), _full((1, KVL)), _full((NH, KVL, 256)),
                  _full((1, DKP)), _full((1, DKP))],
        out_specs=[row(512), _full((NH, DK, QL)), _full((NH, KVL, 256)), _full((1, QL)), _full((1, KVL)),
                   _full((1, DKP)), _full((1, DKP))],
        out_shape=[jax.ShapeDtypeStruct((t_all, 512), BF16), jax.ShapeDtypeStruct((NH, DK, QL), F32),
                   jax.ShapeDtypeStruct((NH, KVL, 256), F32), jax.ShapeDtypeStruct((1, QL), F32),
                   jax.ShapeDtypeStruct((1, KVL), F32), jax.ShapeDtypeStruct((1, DKP), F32), jax.ShapeDtypeStruct((1, DKP), F32)],
        compiler_params=_params(("arbitrary",)),
    )(u, dq, dk, dv, cos, sin, q_lora_g, w_uq_t, kv_lora_g, w_ukv, qn_g, kn_g)


def _in_bwd(ctx, x, modsel, norm_g, dlo, dga, dgp, dpool, dxn, w_in_t):
    s_len, lc = x.shape[0], ctx.shape[0]
    t_all = s_len + lc
    off = lc // TB
    nb = t_all // TB
    nq = s_len // TB
    hb = TB // HALO
    n = TB + 2 * HALO

    def body(ctx_ref, x_ref, mod_ref, ng_ref, dlo_ref, dga_ref, dgp_ref, dp_ref, dpprev_ref, dpnext_ref, dxn_ref, win_ref,
             gx_ref, dwin_ref, dmod_ref, dng_ref):
        i = pl.program_id(0)
        j = i - off

        @pl.when(i == 0)
        def _():
            dwin_ref[...] = jnp.zeros_like(dwin_ref)
            dmod_ref[...] = jnp.zeros_like(dmod_ref)
            dng_ref[...] = jnp.zeros_like(dng_ref)

        latent = i >= off
        dp = dp_ref[...]
        prev = jnp.where(j <= 0, 0.0, dpprev_ref[...])
        nxt = jnp.where(j >= nq - 1, 0.0, dpnext_ref[...])
        win = jnp.concatenate([prev, dp, nxt], axis=0)
        tg = j * TB - HALO + lax.broadcasted_iota(jnp.int32, (n, 1), 0)
        dpin = []
        for g, w in enumerate(POOL_WINDOWS):
            cnt = jnp.maximum(jnp.minimum(tg + w // 2, s_len) - jnp.maximum(tg - w // 2, 0), 1).astype(F32)
            zq = win[:, g * 128:(g + 1) * 128] / cnt
            zq = zq + _shift_rows(zq, 1)
            for step in (1, 2, 4):
                if w >= 4 * step:
                    zq = _shift_rows(zq, -step) + _shift_rows(zq, step)
            dpin.append(zq[HALO:HALO + TB] - dp[:, g * 128:(g + 1) * 128])
        zero = jnp.zeros((TB, 512), BF16)
        du = [dlo_ref[...], jnp.where(latent, dga_ref[...], zero),
              jnp.where(latent, jnp.concatenate(dpin, axis=1).astype(BF16), zero), jnp.where(latent, dgp_ref[...], zero)]

        ng = ng_ref[...]
        xb = jnp.where(i < off, ctx_ref[...], x_ref[...])
        r, xh, xg, h, scale = _modulated(xb, mod_ref, ng)
        hb_ = h.astype(BF16)
        dh = jnp.zeros((TB, D), F32)
        for s, (lo, hi) in enumerate(SEG):
            dwin_ref[lo:hi, :] += _tn(du[s], hb_)
            dh += _nn(du[s], win_ref[lo:hi, :])
        is_lat = latent.astype(F32)
        dsh = _colsum(dh)
        dsc = _colsum(dh * xg)
        dmod_ref[0, 0:1, :] += dsh * (1.0 - is_lat)
        dmod_ref[0, 1:2, :] += dsc * (1.0 - is_lat)
        dmod_ref[1, 0:1, :] += dsh * is_lat
        dmod_ref[1, 1:2, :] += dsc * is_lat
        dxg = dh * (1.0 + scale)
        dng_ref[...] += _colsum(dxg * xh)
        dxh = dxg * ng
        gx_ref[...] = r * (dxh - xh * jnp.mean(dxh * xh, axis=-1, keepdims=True)) + dxn_ref[...]

    row = lambda w: pl.BlockSpec((TB, w), lambda i: (i, 0))
    lat = lambda w: pl.BlockSpec((TB, w), lambda i: (jnp.maximum(i - off, 0), 0))
    last8 = s_len // HALO - 1
    cspec, xspec, mspec = _token_specs(off)
    return pl.pallas_call(
        body, name="in_bwd", grid=(nb,),
        in_specs=[cspec, xspec, mspec, _full((1, D)), row(512), lat(512), lat(512), lat(512),
                  pl.BlockSpec((HALO, 512), lambda i: (jnp.maximum(jnp.maximum(i - off, 0) * hb - 1, 0), 0)),
                  pl.BlockSpec((HALO, 512), lambda i: (jnp.minimum((jnp.maximum(i - off, 0) + 1) * hb, last8), 0)),
                  lat(D), _full((DIN, D))],
        out_specs=[lat(D), _full((DIN, D)), _full((2, 2, D)), _full((1, D))],
        out_shape=[jax.ShapeDtypeStruct((s_len, D), F32), jax.ShapeDtypeStruct((DIN, D), F32),
                   jax.ShapeDtypeStruct((2, 2, D), F32), jax.ShapeDtypeStruct((1, D), F32)],
        compiler_params=_params(("arbitrary",)),
    )(ctx, x, modsel, norm_g, dlo, dga, dgp, dpool, dpool, dpool, dxn, w_in_t)


def _adamw_update(w_ref, g_ref, m_ref, v_ref, d_ref, mo_ref, vo_ref):
    gv = g_ref[...]
    mn = ADAM_B1 * m_ref[...] + (1.0 - ADAM_B1) * gv
    vn = ADAM_B2 * v_ref[...] + (1.0 - ADAM_B2) * (gv * gv)
    m_hat = mn / (1.0 - ADAM_B1 ** ADAM_STEP)
    v_hat = vn / (1.0 - ADAM_B2 ** ADAM_STEP)
    d_ref[...] = -ADAM_LR * (m_hat / (jnp.sqrt(v_hat) + ADAM_EPS) + ADAM_WD * w_ref[...])
    mo_ref[...] = mn
    vo_ref[...] = vn


def _adamw_many(ws, gs, ms, vs):
    n = len(ws)

    def body(*refs):
        for i in range(n):
            _adamw_update(refs[i], refs[n + i], refs[2 * n + i], refs[3 * n + i], refs[4 * n + i], refs[5 * n + i], refs[6 * n + i])

    def spec(w):
        rows, cols = w.shape
        return pl.BlockSpec((rows // 2, cols), lambda i: (i, 0)) if rows % 16 == 0 else _full((rows, cols))

    specs = [spec(w) for w in ws]
    shp = [jax.ShapeDtypeStruct(w.shape, F32) for w in ws]
    out = pl.pallas_call(body, name="adamw_many", grid=(2,), in_specs=specs * 4, out_specs=specs * 3, out_shape=shp * 3,
                         compiler_params=_params(("arbitrary",)))(*ws, *gs, *ms, *vs)
    return out[:n], out[n:2 * n], out[2 * n:]


class _Links:
    def __init__(self, send_sems, recv_sems):
        self.send_sems, self.recv_sems, self.sends = send_sems, recv_sems, []

    def send(self, src, dst, sem, to):
        cp = pltpu.make_async_remote_copy(src, dst, self.send_sems.at[sem], self.recv_sems.at[sem], device_id=to,
                                          device_id_type=MESH)
        cp.start()
        self.sends.append(cp)

    def arrived(self, dst, sem, frm):
        pltpu.make_async_remote_copy(dst, dst, self.send_sems.at[sem], self.recv_sems.at[sem], device_id=frm,
                                     device_id_type=MESH).wait_recv()

    def drain(self):
        for cp in self.sends:
            cp.wait_send()


def _half(ref, c, axis):
    size = ref.shape[axis - 2] // 2
    win = pl.ds(pl.multiple_of(c * size, 16 if axis == 0 else 128), size)
    idx = (win, slice(None)) if axis == 0 else (slice(None), win)
    return ref.at[(slice(None),) * (len(ref.shape) - 2) + idx]


def _select_rows(slots_ref, n_slots, row=0):
    sub = lax.broadcasted_iota(jnp.int32, (8, 1), 0)
    out = None
    for d in range(n_slots):
        r = jnp.where(sub == d, jnp.broadcast_to(slots_ref[d][row:row + 1, :], (8, slots_ref.shape[-1])), 0.0)
        out = r if out is None else out + r
    return out


def _gather(c, c_ctx, w_mod, b_mod_k, shards, axes, slab_rows):
    nw = len(shards)
    kw = w_mod.shape[1]

    def body(*refs):
        c_ref, cc_ref, wm_ref, b_ref = refs[:4]
        w_refs = refs[4:4 + nw]
        a16_ref, mod_ref = refs[4 + nw:6 + nw]
        g_refs = refs[6 + nw:6 + 2 * nw]
        a_ref, send_sems, recv_sems = refs[6 + 2 * nw:]

        def slab(wi, chip):
            return g_refs[wi].at[chip].at[0:shards[wi].shape[0]]
        x, y, cc = lax.axis_index("x"), lax.axis_index("y"), lax.axis_index("c")
        me = 4 * x + 2 * y + cc
        k = 2 * x + y
        sibling = (x, y, 1 - cc)
        links = _Links(send_sems, recv_sems)
        chips = [_peer(x, y, cc, off + (0,)) for off in CHIPS3]
        chip_a = ((x + 1 - cc) % 2, (y + cc) % 2, cc)
        chip_b = ((x + cc) % 2, (y + 1 - cc) % 2, cc)
        chip_d = (1 - x, 1 - y, cc)
        cv = c_ref[...]
        a_ref[me] = jnp.broadcast_to(cv * _sig(cv), (8, D))
        for j, off in enumerate(PEERS7):
            links.send(a_ref.at[me], a_ref.at[me], j, _peer(x, y, cc, off))
        for wi in range(nw):
            slab(wi, k)[...] = w_refs[wi][...].astype(BF16)
            for j, to in enumerate((chip_a, chip_b)):
                links.send(_half(slab(wi, k), cc, axes[wi]), _half(slab(wi, k), cc, axes[wi]), 10 + wi * 6 + j, to)
            pad = slab_rows[wi] - shards[wi].shape[0]
            if pad:
                for kk in range(4):
                    g_refs[wi][kk, shards[wi].shape[0]:, :] = jnp.zeros((pad, shards[wi].shape[1]), BF16)
        for j, off in enumerate(PEERS7):
            px, py, pc = _peer(x, y, cc, off)
            links.arrived(a_ref.at[4 * px + 2 * py + pc], j, (px, py, pc))
        ccv = cc_ref[...]
        sub = lax.broadcasted_iota(jnp.int32, (8, 1), 0)
        a16 = jnp.concatenate([_select_rows(a_ref, 8), jnp.where(sub == 0, jnp.broadcast_to(ccv * _sig(ccv), (8, D)), 0.0)], axis=0)
        a16_ref[...] = a16
        mod_ref[k] = _dot3(_nn, a16, wm_ref[...]) + b_ref[...]
        for j, to in enumerate(chips):
            links.send(mod_ref.at[k], mod_ref.at[k], 7 + j, to)
        for j, (frm, origin) in enumerate(((chip_a, chip_a), (chip_b, chip_b), (chip_b, chip_d))):
            for wi in range(nw):
                blk = _half(slab(wi, 2 * origin[0] + origin[1]), cc, axes[wi])
                links.arrived(blk, 10 + wi * 6 + j, frm)
                if j == 0:
                    links.send(blk, blk, 10 + wi * 6 + 2, chip_b)
                links.send(blk, blk, 10 + wi * 6 + 3 + j, sibling)
        for j, (px, py, pc) in enumerate(chips):
            links.arrived(mod_ref.at[2 * px + py], 7 + j, (px, py, pc))
        for j, origin in enumerate((chip_b, chip_a, chip_d)):
            for wi in range(nw):
                links.arrived(_half(slab(wi, 2 * origin[0] + origin[1]), 1 - cc, axes[wi]), 10 + wi * 6 + 3 + j, sibling)
        links.drain()

    nsem = 10 + 6 * nw
    return pl.pallas_call(
        body, name="gather", in_specs=[VM] * (4 + nw), out_specs=[VM] * (2 + nw),
        out_shape=[jax.ShapeDtypeStruct((16, D), F32), jax.ShapeDtypeStruct((4, 16, kw), F32)]
        + [jax.ShapeDtypeStruct((4, r, s.shape[1]), BF16) for r, s in zip(slab_rows, shards)],
        scratch_shapes=[pltpu.VMEM((8, 8, D), F32), pltpu.SemaphoreType.DMA((nsem,)), pltpu.SemaphoreType.DMA((nsem,))],
        compiler_params=pltpu.CompilerParams(vmem_limit_bytes=VMEM_LIMIT),
    )(c, c_ctx, w_mod, b_mod_k, *shards)


SMALL_ROW_WIDTHS = (D, QL, KVL, DKP, DKP, 512, 128)
SMALL_OUT_WIDTHS = (D, QL, KVL, DK, DK, 512, 1)


def _reduce(grads, axes, smalls, w_pool_g, dmod8, a16, w_mod, c_ctx, m_w_mod, v_w_mod):
    nw = len(grads)
    ns = len(smalls)
    kw = w_mod.shape[1]
    halves = []
    for g, ax in zip(grads, axes):
        halves.append((g.shape[1] // 2, g.shape[2]) if ax == 0 else (g.shape[1], g.shape[2] // 2))

    def body(*refs):
        g_refs = refs[:nw]
        small_refs = refs[nw:nw + ns]
        wp_ref, dm_ref, a16_ref, wm_ref, cc_ref, mwm_ref, vwm_ref = refs[nw + ns:nw + ns + 7]
        o = nw + ns + 7
        r_refs = refs[o:o + nw]
        small_outs = refs[o + nw:o + nw + ns]
        rwp_ref, gw_ref, gb_ref, gc_ref, dwm_ref, mwm_out, vwm_out = refs[o + nw + ns:o + nw + ns + 7]
        o = o + nw + ns + 7
        own, sib, part, got, rel = (refs[o + i * nw:o + (i + 1) * nw] for i in range(5))
        smbuf, wps, wpg, dm_all, pc_all, m_s, v_s, d_s, send_sems, recv_sems, local_sems, adam_sems = refs[o + 5 * nw:]
        adam_in = [pltpu.make_async_copy(mwm_ref, m_s, adam_sems.at[0]), pltpu.make_async_copy(vwm_ref, v_s, adam_sems.at[1])]
        for cp in adam_in:
            cp.start()
        x, y, cc = lax.axis_index("x"), lax.axis_index("y"), lax.axis_index("c")
        me = 4 * x + 2 * y + cc
        k = 2 * x + y
        sibling = (x, y, 1 - cc)
        links = _Links(send_sems, recv_sems)
        chips = [_peer(x, y, cc, off + (0,)) for off in CHIPS3]
        peers = [_peer(x, y, cc, off) for off in PEERS7]
        chip_a = ((x + 1 - cc) % 2, (y + cc) % 2, cc)
        chip_b = ((x + cc) % 2, (y + 1 - cc) % 2, cc)
        ka, kb, kd = 2 * chip_a[0] + chip_a[1], 2 * chip_b[0] + chip_b[1], 2 * (1 - x) + (1 - y)
        big, sm0, wp0, dm0, pc0 = 0, 5 * nw, 5 * nw + 7, 5 * nw + 14, 5 * nw + 21

        locals_ = []
        for wi in range(nw):
            lc = pltpu.make_async_copy(_half(g_refs[wi], cc, axes[wi]), own[wi], local_sems.at[wi])
            lc.start()
            locals_.append(lc)
            links.send(_half(g_refs[wi], 1 - cc, axes[wi]), sib[wi], big + wi * 5, sibling)
        slot = smbuf.at[me]
        slot[...] = jnp.zeros((8, D), F32)
        for r, (ref, w) in enumerate(zip(small_refs, SMALL_ROW_WIDTHS)):
            slot[r:r + 1, 0:w] = jnp.broadcast_to(ref[...], (1, w))
        links.send(wp_ref, wps, wp0, sibling)
        dm_all[me] = dm_ref[...]
        for j, peer in enumerate(peers):
            links.send(dm_all.at[me], dm_all.at[me], dm0 + j, peer)
            links.send(smbuf.at[me], smbuf.at[me], sm0 + j, peer)
        links.arrived(wps, wp0, sibling)
        wpg[k] = (wp_ref[...] + wps[...]).astype(BF16)
        for j, to in enumerate(chips):
            links.send(wpg.at[k], wpg.at[k], wp0 + 1 + j, to)
        for wi in range(nw):
            locals_[wi].wait()
            links.arrived(sib[wi], big + wi * 5, sibling)
            part[wi][...] = (own[wi][...] + sib[wi][...]).astype(BF16)
            got[wi][k] = part[wi][k]
            got[wi][kd] = jnp.zeros(halves[wi], BF16)
            links.send(part[wi].at[kd], rel[wi], big + wi * 5 + 1, chip_b)
            links.send(part[wi].at[kb], got[wi].at[k], big + wi * 5 + 2, chip_b)
        for j, (px, py, pc) in enumerate(peers):
            links.arrived(dm_all.at[4 * px + 2 * py + pc], dm0 + j, (px, py, pc))
        dmc = dm_all[0][1:2, :]
        dml = dm_all[0][0:1, :]
        for d in range(1, 8):
            dmc = dmc + dm_all[d][1:2, :]
            dml = dml + dm_all[d][0:1, :]
        gb_ref[...] = dml + dmc
        sub = lax.broadcasted_iota(jnp.int32, (8, 1), 0)
        b16 = jnp.concatenate([_select_rows(dm_all, 8), jnp.where(sub == 0, jnp.broadcast_to(dmc, (8, 3 * D)), 0.0)], axis=0)
        bk = jnp.zeros((16, kw), F32)
        for kk in range(4):
            bk = bk + jnp.where(k == kk, b16[:, kk * kw:(kk + 1) * kw], 0.0)
        gw_ref[...] = _dot3(_tn, a16_ref[...], bk)
        for cp in adam_in:
            cp.wait()
        _adamw_update(wm_ref, gw_ref, m_s, v_s, d_s, m_s, v_s)
        adam_out = [pltpu.make_async_copy(d_s, dwm_ref, adam_sems.at[2]), pltpu.make_async_copy(m_s, mwm_out, adam_sems.at[3]),
                    pltpu.make_async_copy(v_s, vwm_out, adam_sems.at[4])]
        for cp in adam_out:
            cp.start()
        pc_all[k] = _dot3(_nt, jnp.broadcast_to(bk[8:9, :], (8, kw)), wm_ref[...])
        for j, to in enumerate(chips):
            links.send(pc_all.at[k], pc_all.at[k], pc0 + j, to)
        for j, (px, py, pc) in enumerate(peers):
            links.arrived(smbuf.at[4 * px + 2 * py + pc], sm0 + j, (px, py, pc))
        tot = smbuf[0]
        for d in range(1, 8):
            tot = tot + smbuf[d]
        for r, (ref, w) in enumerate(zip(small_outs, SMALL_OUT_WIDTHS)):
            ref[...] = tot[r:r + 1, 0:w]
        for j, (px, py, pc) in enumerate(chips):
            links.arrived(wpg.at[2 * px + py], wp0 + 1 + j, (px, py, pc))
        wpt = wpg[0].astype(F32)
        for kk in range(1, 4):
            wpt = wpt + wpg[kk].astype(F32)
        rwp_ref[...] = wpt
        for wi in range(nw):
            links.arrived(rel[wi], big + wi * 5 + 1, chip_b)
            rel[wi][...] = (part[wi][ka].astype(F32) + rel[wi][...].astype(F32)).astype(BF16)
            links.send(rel[wi], got[wi].at[k], big + wi * 5 + 3, chip_a)
        for wi in range(nw):
            links.arrived(got[wi].at[kb], big + wi * 5 + 2, chip_b)
            links.arrived(got[wi].at[ka], big + wi * 5 + 3, chip_a)
            total = got[wi][0].astype(F32)
            for kk in range(1, 4):
                total = total + got[wi][kk].astype(F32)
            mine = _half(r_refs[wi], cc, axes[wi])
            mine[...] = total
            links.send(mine, mine, big + wi * 5 + 4, sibling)
        for j, (px, py, pc) in enumerate(chips):
            links.arrived(pc_all.at[2 * px + py], pc0 + j, (px, py, pc))
        ccv = cc_ref[...]
        sg = _sig(ccv)
        gc_ref[...] = (pc_all[0][0:1, :] + pc_all[1][0:1, :] + pc_all[2][0:1, :] + pc_all[3][0:1, :]) * (sg * (1.0 + ccv * (1.0 - sg)))
        for wi in range(nw):
            links.arrived(_half(r_refs[wi], 1 - cc, axes[wi]), big + wi * 5 + 4, sibling)
        links.drain()
        for cp in adam_out:
            cp.wait()

    nsem = 5 * nw + 24
    quads = [(4,) + h for h in halves]
    wm_shape = jax.ShapeDtypeStruct((D, kw), F32)
    return pl.pallas_call(
        body, name="reduce", in_specs=[ANY] * nw + [VM] * (ns + 5) + [ANY] * 2, out_specs=[VM] * (nw + ns + 4) + [ANY] * 3,
        out_shape=[jax.ShapeDtypeStruct(g.shape[1:], F32) for g in grads]
        + [jax.ShapeDtypeStruct((1, w), F32) for w in SMALL_OUT_WIDTHS]
        + [jax.ShapeDtypeStruct(w_pool_g.shape, F32), wm_shape, jax.ShapeDtypeStruct((1, 3 * D), F32),
           jax.ShapeDtypeStruct((1, D), F32), wm_shape, wm_shape, wm_shape],
        scratch_shapes=[pltpu.VMEM(q, F32) for q in quads] + [pltpu.VMEM(q, F32) for q in quads]
        + [pltpu.VMEM(q, BF16) for q in quads] + [pltpu.VMEM(q, BF16) for q in quads] + [pltpu.VMEM(h, BF16) for h in halves]
        + [pltpu.VMEM((8, 8, D), F32), pltpu.VMEM(w_pool_g.shape, F32), pltpu.VMEM((4,) + w_pool_g.shape, BF16),
           pltpu.VMEM((8, 8, 3 * D), F32),
           pltpu.VMEM((4, 8, D), F32), pltpu.VMEM((D, kw), F32), pltpu.VMEM((D, kw), F32), pltpu.VMEM((D, kw), F32)]
        + [pltpu.SemaphoreType.DMA((nsem,)), pltpu.SemaphoreType.DMA((nsem,)), pltpu.SemaphoreType.DMA((nw,)),
           pltpu.SemaphoreType.DMA((5,))],
        compiler_params=pltpu.CompilerParams(vmem_limit_bytes=VMEM_LIMIT),
    )(*grads, *smalls, w_pool_g, dmod8, a16, w_mod, c_ctx, m_w_mod, v_w_mod)


def _rope_tables(s_len):
    rows = s_len // GRID_W
    per = TB // GRID_W
    n_freq = 16
    inv = ROPE_BASE ** (-jnp.arange(n_freq, dtype=F32) / n_freq)
    ang_r = jnp.arange(rows, dtype=F32)[:, None] * inv
    ang_c = jnp.arange(GRID_W, dtype=F32)[:, None] * inv
    by_row, by_col = [], []
    for fn, pad in ((jnp.cos, 1.0), (jnp.sin, 0.0)):
        r = jnp.concatenate([fn(ang_r), fn(ang_r), jnp.zeros((rows, 96), F32)], axis=1).reshape(rows // per, per, 128)
        by_row.append(jnp.pad(r, ((0, 0), (0, 8 - per), (0, 0))))
        cpart = jnp.concatenate([jnp.zeros((GRID_W, 32), F32), fn(ang_c), fn(ang_c), jnp.full((GRID_W, 64), pad, F32)], axis=1)
        by_col.append(jnp.tile(cpart, (per, 1)))
    return jnp.concatenate(by_row, axis=-1), jnp.concatenate(by_col, axis=-1)


def kernel(x, c, ctx, c_ctx, w_mod, b_mod, norm_g, w_in, q_lora_g, w_uq, kv_lora_g, w_ukv, q_norm_g, k_norm_g, w_pool, pool_scale, w_out, loss_target, m_c_ctx, m_w_mod, m_b_mod, m_norm_g, m_w_in, m_q_lora_g, m_w_uq, m_kv_lora_g, m_w_ukv, m_q_norm_g, m_k_norm_g, m_w_pool, m_pool_scale, m_w_out, v_c_ctx, v_w_mod, v_b_mod, v_norm_g, v_w_in, v_q_lora_g, v_w_uq, v_kv_lora_g, v_w_ukv, v_q_norm_g, v_k_norm_g, v_w_pool, v_pool_scale, v_w_out):
    xi, yi, ci = lax.axis_index("x"), lax.axis_index("y"), lax.axis_index("c")
    me = 4 * xi + 2 * yi + ci
    k = 2 * xi + yi
    s_len = x.shape[1]
    lc = ctx.shape[1]
    kw = w_mod.shape[2]
    weights = dict(c_ctx=c_ctx, w_mod=w_mod, b_mod=b_mod, norm_g=norm_g, w_in=w_in, q_lora_g=q_lora_g, w_uq=w_uq,
                   kv_lora_g=kv_lora_g, w_ukv=w_ukv, q_norm_g=q_norm_g, k_norm_g=k_norm_g, w_pool=w_pool,
                   pool_scale=pool_scale, w_out=w_out)
    m_in = dict(c_ctx=m_c_ctx, w_mod=m_w_mod, b_mod=m_b_mod, norm_g=m_norm_g, w_in=m_w_in, q_lora_g=m_q_lora_g, w_uq=m_w_uq,
                kv_lora_g=m_kv_lora_g, w_ukv=m_w_ukv, q_norm_g=m_q_norm_g, k_norm_g=m_k_norm_g, w_pool=m_w_pool,
                pool_scale=m_pool_scale, w_out=m_w_out)
    v_in = dict(c_ctx=v_c_ctx, w_mod=v_w_mod, b_mod=v_b_mod, norm_g=v_norm_g, w_in=v_w_in, q_lora_g=v_q_lora_g, w_uq=v_w_uq,
                kv_lora_g=v_kv_lora_g, w_ukv=v_w_ukv, q_norm_g=v_q_norm_g, k_norm_g=v_k_norm_g, w_pool=v_w_pool,
                pool_scale=v_pool_scale, w_out=v_w_out)
    order = ["c_ctx", "w_mod", "b_mod", "norm_g", "w_in", "q_lora_g", "w_uq", "kv_lora_g", "w_ukv", "q_norm_g", "k_norm_g",
             "w_pool", "pool_scale", "w_out"]
    transposed = ("w_in", "w_uq")
    as2d = lambda n, a: jnp.transpose(a[0]) if n in transposed else a.reshape(-1, a.shape[-1])
    back = lambda n, a: jnp.transpose(a)[None] if n in transposed else a.reshape(weights[n].shape)

    c_ctx2 = c_ctx.reshape(1, D)
    b_mod_k = lax.dynamic_slice(b_mod, (0, k * kw), (1, kw))
    split = (1, 0, 0, 0)
    a16, mod_all, g_in, g_uq, g_ukv, g_out = _gather(
        c, c_ctx2, w_mod[0], b_mod_k, [as2d("w_in", w_in), as2d("w_uq", w_uq), w_ukv[0], w_out[0]], split,
        (DIN // 4, DKP, KVL, D // 4))
    mod_me = lax.dynamic_index_in_dim(mod_all, me, axis=1, keepdims=False).reshape(3, D)
    mod_c = mod_all[:, 8, :].reshape(3, D)
    modsel = jnp.stack([mod_c, mod_me])
    w_in_t = g_in.reshape(DIN, D)
    w_uq_t = g_uq
    w_out_f = g_out.reshape(D, D)
    qn_g = jnp.pad(q_norm_g, ((0, 0), (0, DKP - DK)))
    kn_g = jnp.pad(k_norm_g, ((0, 0), (0, DKP - DK)))
    w_pool_b = w_pool[0].astype(BF16)
    cos, sin = _rope_tables(s_len)

    u, q, kk, v = _fwd_in(ctx[0], x[0], modsel, norm_g, w_in_t, q_lora_g, w_uq_t, kv_lora_g, g_ukv, qn_g, kn_g, cos, sin)
    attn, lse = _attn_fwd(q, kk, v, s_len)
    (dxn, dattn, dga, dgp, dpool, dw_out, dgate, dps, dw_pool, loss) = _out_stage(
        attn.reshape(s_len // Q_BLOCK, Q_BLOCK, NH * DV), u, x[0], loss_target[0], modsel[1, 2:3, :], w_pool_b, pool_scale,
        w_out_f, lc)
    dattn = dattn.reshape(s_len, NH * DV)
    dq, dk, dv = _attn_bwd(q, kk, v, dattn, attn, lse, s_len)
    dlo, dw_uq_t, dw_ukv, dqlg, dkvlg, dqng, dkng = _qkv_bwd(u, dq, dk, dv, cos, sin, q_lora_g, w_uq_t, kv_lora_g, g_ukv,
                                                            qn_g, kn_g, s_len)
    gx, dw_in_t, dmod, dng = _in_bwd(ctx[0], x[0], modsel, norm_g, dlo, dga, dgp, dpool, dxn, w_in_t)

    dmod_l = jnp.concatenate([dmod[1, 0], dmod[1, 1], dgate[0]]).reshape(1, 3 * D)
    dmod_c = jnp.concatenate([dmod[0, 0], dmod[0, 1], jnp.zeros((D,), F32)]).reshape(1, 3 * D)
    dmod8 = jnp.concatenate([dmod_l, dmod_c, jnp.zeros((6, 3 * D), F32)], axis=0)
    (r_in, r_uq, r_ukv, r_out, g_ng, g_qlg, g_kvlg, g_qng, g_kng, g_ps, loss_all, g_wp, g_w_mod, g_b_mod, g_c_ctx,
     d_w_mod, nm_w_mod, nv_w_mod) = _reduce(
        [dw_in_t.reshape(4, DIN // 4, D), dw_uq_t, dw_ukv, dw_out.reshape(4, D // 4, D)], split,
        [dng, dqlg, dkvlg, dqng, dkng, dps, loss], dw_pool, dmod8, a16, w_mod[0], c_ctx2, m_w_mod[0], v_w_mod[0])
    g2d = dict(c_ctx=g_c_ctx, b_mod=g_b_mod, w_mod=g_w_mod, w_in=r_in, w_uq=r_uq, w_ukv=r_ukv, w_out=r_out, norm_g=g_ng,
               q_lora_g=g_qlg, kv_lora_g=g_kvlg, q_norm_g=g_qng, k_norm_g=g_kng, pool_scale=g_ps, w_pool=g_wp.reshape(512, 128))

    d2d, m2d, v2d = dict(w_mod=d_w_mod), dict(w_mod=nm_w_mod), dict(w_mod=nv_w_mod)
    rest = [n for n in order if n != "w_mod"]
    outs = _adamw_many([as2d(n, weights[n]) for n in rest], [g2d[n] for n in rest], [as2d(n, m_in[n]) for n in rest],
                       [as2d(n, v_in[n]) for n in rest])
    for dst, arrs in zip((d2d, m2d, v2d), outs):
        dst.update(dict(zip(rest, arrs)))

    return (loss_all[0, 0], gx[None], *[back(n, g2d[n]) for n in order], *[back(n, d2d[n]) for n in order],
            *[back(n, m2d[n]) for n in order], *[back(n, v2d[n]) for n in order])
```

```python
import jax
import jax.numpy as jnp
from jax import lax
from jax.experimental import pallas as pl
from jax.experimental.pallas import tpu as pltpu

F32 = jnp.float32
BF16 = jnp.bfloat16
MESH = pl.DeviceIdType.MESH

D = 1024
NH = 4
DK = 192
DKP = 256
DV = 128
QL = 256
KVL = 128
DIN = 1984
U_LO = 448
SEG = ((0, 512), (448, 960), (960, 1472), (1472, 1984))
DU = 2048
POOL_WINDOWS = (2, 4, 8, 16)
HALO = 8
EPS = 1e-6
ROPE_BASE = 10000.0
GRID_W = 64
Q_BLOCK = 128
TB = 256
BWD_QBLOCKS = 1
SCALE = DK ** -0.5
LOG2E = 1.4426950408889634
LN2 = 0.6931471805599453
VMEM_LIMIT = 56 * 1024 * 1024

ADAM_LR = 0.001
ADAM_B1 = 0.9
ADAM_B2 = 0.999
ADAM_EPS = 1e-08
ADAM_WD = 0.01
ADAM_STEP = 10

CHIPS3 = ((1, 0), (0, 1), (1, 1))
PEERS7 = tuple((dx, dy, dc) for dx in (0, 1) for dy in (0, 1) for dc in (0, 1) if (dx, dy, dc) != (0, 0, 0))

VM = pl.BlockSpec(memory_space=pltpu.VMEM)
ANY = pl.BlockSpec(memory_space=pl.ANY)


def _nn(a, b):
    return jnp.dot(a, b, preferred_element_type=F32)


def _nt(a, b):
    return lax.dot_general(a, b, (((1,), (1,)), ((), ())), preferred_element_type=F32)


def _tn(a, b):
    return lax.dot_general(a, b, (((0,), (0,)), ((), ())), preferred_element_type=F32)


def _split3(a):
    a0 = a.astype(BF16)
    r = a - a0.astype(F32)
    a1 = r.astype(BF16)
    a2 = (r - a1.astype(F32)).astype(BF16)
    return a0, a1, a2


def _dot3(dot, a, b):
    sa = _split3(a)
    sb = _split3(b)
    out = None
    for i in range(3):
        for j in range(3 - i):
            t = dot(sa[i], sb[j])
            out = t if out is None else out + t
    return out


def _sig(x):
    return 1.0 / (1.0 + jnp.exp(-x))


def _rot(t):
    src = lax.broadcasted_iota(jnp.int32, (128, 128), 0)
    dst = lax.broadcasted_iota(jnp.int32, (128, 128), 1)
    first = (dst % 32) < 16
    perm = jnp.where(first & (src == dst + 16), -1.0, jnp.where(~first & (src == dst - 16), 1.0, 0.0)).astype(BF16)
    hi = t.astype(BF16)
    lo = (t - hi.astype(F32)).astype(BF16)
    return _nn(hi, perm) + _nn(lo, perm)


def _rope(t, cos, sin):
    return t * cos + _rot(t) * sin


def _rope_t(t, cos, sin):
    return t * cos - _rot(t * sin)


def _rope_block(rows_ref, cols_ref, is_ctx):
    lane = lax.broadcasted_iota(jnp.int32, (TB, 256), 1) % 128
    rows = jnp.concatenate([jnp.broadcast_to(rows_ref[0, r:r + 1, :], (GRID_W, 256)) for r in range(TB // GRID_W)], axis=0)
    cs = jnp.where(lane < 32, rows, cols_ref[...])
    return jnp.where(is_ctx, 1.0, cs[:, :128]), jnp.where(is_ctx, 0.0, cs[:, 128:])


def _shift_rows(z, k):
    n = z.shape[0]
    return pltpu.roll(z, (n - k) % n, 0)


def _colsum(a):
    return jnp.sum(a, axis=0, keepdims=True)


def _rowsum(a):
    return jnp.sum(a, axis=-1, keepdims=True)


def _row_layout(col):
    return jnp.transpose(jnp.broadcast_to(col, (col.shape[0], 128)))[0:8, :]


def _params(sem=None):
    return pltpu.CompilerParams(dimension_semantics=sem, vmem_limit_bytes=VMEM_LIMIT)


def _full(shape):
    nd = len(shape)
    return pl.BlockSpec(shape, lambda *_: (0,) * nd)


def _peer(x, y, c, off):
    dx, dy, dc = off
    return ((x + dx) % 2, (y + dy) % 2, (c + dc) % 2)


def _token_specs(off):
    ctx = pl.BlockSpec((TB, D), lambda i: (jnp.minimum(i, off - 1), 0))
    lat = pl.BlockSpec((TB, D), lambda i: (jnp.maximum(i - off, 0), 0))
    mod = pl.BlockSpec((1, 3, D), lambda i: (jnp.minimum(i // off, 1), 0, 0))
    return ctx, lat, mod


def _modulated(x, mod_ref, ng):
    shift = mod_ref[0, 0:1, :]
    scale = mod_ref[0, 1:2, :]
    r = lax.rsqrt(jnp.mean(x * x, axis=-1, keepdims=True) + EPS)
    xh = x * r
    xg = xh * ng
    return r, xh, xg, xg * (1.0 + scale) + shift, scale


def _fwd_in(ctx, x, modsel, norm_g, w_in_t, q_lora_g, w_uq_t, kv_lora_g, w_ukv, qn_g, kn_g, cos, sin):
    s_len, lc = x.shape[0], ctx.shape[0]
    t_all = s_len + lc
    nb = t_all // TB
    off = lc // TB

    def body(ctx_ref, x_ref, mod_ref, ng_ref, win_ref, qlg_ref, wuq_ref, kvlg_ref, wukv_ref, qng_ref, kng_ref, cos_ref, sin_ref,
             u_ref, q_ref, k_ref, v_ref):
        is_ctx = pl.program_id(0) < off
        xb = jnp.where(is_ctx, ctx_ref[...], x_ref[...])
        _, _, _, h, _ = _modulated(xb, mod_ref, ng_ref[...])
        hb = h.astype(BF16)
        lane = lax.broadcasted_iota(jnp.int32, (TB, 512), 1)
        ulo = jnp.where(lane < U_LO, _nt(hb, win_ref[SEG[0][0]:SEG[0][1], :]), 0.0)
        u_ref[:, 0:512] = ulo
        for j in range(1, 4):
            u_ref[:, j * 512:(j + 1) * 512] = _nt(hb, win_ref[SEG[j][0]:SEG[j][1], :])
        cos, sin = _rope_block(cos_ref, sin_ref, is_ctx)
        cq = ulo[:, 0:QL]
        cqn = (cq * lax.rsqrt(jnp.mean(cq * cq, axis=-1, keepdims=True) + EPS) * qlg_ref[...]).astype(BF16)
        qng = qng_ref[...]
        ckv = ulo[:, QL:QL + KVL]
        ckvn = (ckv * lax.rsqrt(jnp.mean(ckv * ckv, axis=-1, keepdims=True) + EPS) * kvlg_ref[...]).astype(BF16)
        qhs = [_nt(cqn, wuq_ref[hd]) for hd in range(NH)]
        kvs = [_nn(ckvn, wukv_ref[hd]) for hd in range(NH)]
        for hd in range(NH):
            qh = qhs[hd]
            qn = qh * lax.rsqrt(_rowsum(qh * qh) / DK + EPS) * qng
            q_ref[hd] = (jnp.concatenate([qn[:, :128], _rope(qn[:, 128:], cos, sin)], axis=1) * (SCALE * LOG2E)).astype(BF16)
        kr = ulo[:, 384:512]
        skr = _rowsum(kr * kr)
        kng = kng_ref[...]
        kr_roped = _rope(kr * kng[:, 128:], cos, sin)
        for hd in range(NH):
            kv = kvs[hd]
            kn = kv[:, :128]
            rk = lax.rsqrt((_rowsum(kn * kn) + skr) / DK + EPS)
            k_ref[hd] = jnp.concatenate([kn * rk * kng[:, :128], kr_roped * rk], axis=1).astype(BF16)
            v_ref[hd] = kv[:, 128:].astype(BF16)

    row = lambda w: pl.BlockSpec((TB, w), lambda i: (i, 0))
    heads = lambda w: pl.BlockSpec((NH, TB, w), lambda i: (0, i, 0))
    cspec, xspec, mspec = _token_specs(off)
    return pl.pallas_call(
        body, name="fwd_in", grid=(nb,),
        in_specs=[cspec, xspec, mspec, _full((1, D)), _full((DIN, D)), _full((1, QL)), _full((NH, DKP, QL)), _full((1, KVL)),
                  _full((NH, KVL, 256)), _full((1, DKP)), _full((1, DKP)),
                  pl.BlockSpec((1, 8, 256), lambda i: (jnp.maximum(i - off, 0), 0, 0)), _full((TB, 256))],
        out_specs=[row(DU), heads(DKP), heads(DKP), heads(DV)],
        out_shape=[jax.ShapeDtypeStruct((t_all, DU), F32), jax.ShapeDtypeStruct((NH, t_all, DKP), BF16),
                   jax.ShapeDtypeStruct((NH, t_all, DKP), BF16), jax.ShapeDtypeStruct((NH, t_all, DV), BF16)],
        compiler_params=_params(("arbitrary",)),
    )(ctx, x, modsel, norm_g, w_in_t, q_lora_g, w_uq_t, kv_lora_g, w_ukv, qn_g, kn_g, cos, sin)


def _attn_fwd(q, k, v, s_len):
    t_all = q.shape[1]
    off = (t_all - s_len) // TB
    nq = s_len // TB
    nsub = next(n for n in (4, 2, 1) if nq % n == 0)

    def body(*refs):
        q_refs = refs[:nsub]
        k_ref, v_ref, o_ref, lse_ref = refs[nsub:]
        for sb in range(nsub):
            s = _nt(q_refs[sb][0], k_ref[0])
            m = jnp.max(s, axis=-1, keepdims=True)
            e = jnp.exp2(s - m)
            l = _rowsum(e)
            o_ref[sb * TB:(sb + 1) * TB, :] = _nn(e.astype(BF16), v_ref[0]) / l
            lse_ref[0, sb] = _row_layout(m + jnp.log2(l))

    qspec = lambda sb: pl.BlockSpec((1, TB, DKP), lambda h, i: (h, i * nsub + sb + off, 0))
    return pl.pallas_call(
        body, name="attn_fwd", grid=(NH, nq // nsub),
        in_specs=[qspec(sb) for sb in range(nsub)]
        + [pl.BlockSpec((1, t_all, DKP), lambda h, i: (h, 0, 0)), pl.BlockSpec((1, t_all, DV), lambda h, i: (h, 0, 0))],
        out_specs=[pl.BlockSpec((nsub * TB, DV), lambda h, i: (i, h)), pl.BlockSpec((1, nsub, 8, TB), lambda h, i: (h, i, 0, 0))],
        out_shape=[jax.ShapeDtypeStruct((s_len, NH * DV), F32), jax.ShapeDtypeStruct((NH, nq, 8, TB), F32)],
        compiler_params=_params(("arbitrary", "arbitrary")),
    )(*([q] * nsub), k, v)


def _out_stage(attn, u, x, target, gate, w_pool, pool_scale, w_out, lc):
    s_len = x.shape[0]
    t_all = s_len + lc
    off = lc // TB
    nq = s_len // TB
    hb = TB // HALO
    nqb = s_len // Q_BLOCK
    jb = TB // nqb

    def body(attn_ref, ga_ref, pin_ref, pprev_ref, pnext_ref, gp_ref, x_ref, tgt_ref, gate_ref, wp_ref, ps_ref, wo_ref,
             dxn_ref, dattn_ref, dga_ref, dgp_ref, dpool_ref, dwo_ref, dgate_ref, dps_ref, dwp_ref, loss_ref):
        i = pl.program_id(0)

        @pl.when(i == 0)
        def _():
            dwo_ref[...] = jnp.zeros_like(dwo_ref)
            dgate_ref[...] = jnp.zeros_like(dgate_ref)
            dps_ref[...] = jnp.zeros_like(dps_ref)
            dwp_ref[...] = jnp.zeros_like(dwp_ref)
            loss_ref[...] = jnp.zeros_like(loss_ref)

        attn = jnp.concatenate([attn_ref[:, jj, :] for jj in range(jb)], axis=0)
        ga = ga_ref[...]
        gp = gp_ref[...]
        pin = pin_ref[...]
        prev = jnp.where(i == 0, 0.0, pprev_ref[...])
        nxt = jnp.where(i == nq - 1, 0.0, pnext_ref[...])
        win = jnp.concatenate([prev, pin, nxt], axis=0)
        tg = i * TB + lax.broadcasted_iota(jnp.int32, (TB, 1), 0)
        pooled = []
        for g, w in enumerate(POOL_WINDOWS):
            a = win[:, g * 128:(g + 1) * 128]
            p = _shift_rows(a, -1) + a
            for step in (1, 2, 4):
                if w >= 4 * step:
                    p = _shift_rows(p, -step) + _shift_rows(p, step)
            cnt = (jnp.minimum(tg + w // 2, s_len) - jnp.maximum(tg - w // 2, 0)).astype(F32)
            pooled.append(p[HALO:HALO + TB] / cnt - a[HALO:HALO + TB])
        pooled_b = [p.astype(BF16) for p in pooled]
        z = jnp.concatenate([_nn(pooled_b[g], wp_ref[g]) for g in range(4)], axis=1)
        ps = ps_ref[...]
        yp = z * ps
        sga = _sig(ga)
        sila = ga * sga
        sgp = _sig(gp)
        silp = gp * sgp
        br = jnp.concatenate([sila * attn, silp * yp], axis=1).astype(BF16)
        y = _nn(br, wo_ref[...])
        gate = gate_ref[...]
        err = x_ref[...] + gate * y - tgt_ref[...]
        loss_ref[...] += _colsum(_rowsum(err * err)) * (0.5 / D)
        dxn = err * (1.0 / D)
        dxn_ref[...] = dxn
        dgate_ref[...] += _colsum(dxn * y)
        dy = (dxn * gate).astype(BF16)
        dwo_ref[...] += _tn(br, dy)
        dbr = _nt(dy, wo_ref[...])
        dbra = dbr[:, :512]
        dbrp = dbr[:, 512:]
        dattn = dbra * sila
        for jj in range(jb):
            dattn_ref[:, jj, :] = dattn[jj * nqb:(jj + 1) * nqb]
        dga_ref[...] = (dbra * attn * (sga * (1.0 + ga * (1.0 - sga)))).astype(BF16)
        dgp_ref[...] = (dbrp * yp * (sgp * (1.0 + gp * (1.0 - sgp)))).astype(BF16)
        dyp = dbrp * silp
        dps_ref[...] += _colsum(dyp * z)
        dz = (dyp * ps).astype(BF16)
        dpool = []
        for g in range(4):
            dzg = dz[:, g * 128:(g + 1) * 128]
            dwp_ref[g] += _tn(pooled_b[g], dzg)
            dpool.append(_nt(dzg, wp_ref[g]))
        dpool_ref[...] = jnp.concatenate(dpool, axis=1)

    lat = lambda w: pl.BlockSpec((TB, w), lambda i: (i, 0))
    perm = pl.BlockSpec((nqb, jb, 512), lambda i: (0, i, 0))
    ucol = lambda j: pl.BlockSpec((TB, 512), lambda i: (i + off, j))
    last8 = t_all // HALO - 1
    return pl.pallas_call(
        body, name="out_stage", grid=(nq,),
        in_specs=[perm, ucol(1), ucol(2),
                  pl.BlockSpec((HALO, 512), lambda i: ((i + off) * hb - 1, 2)),
                  pl.BlockSpec((HALO, 512), lambda i: (jnp.minimum((i + off + 1) * hb, last8), 2)),
                  ucol(3), lat(D), lat(D), _full((1, D)), _full((4, 128, 128)), _full((1, 512)), _full((D, D))],
        out_specs=[lat(D), perm, lat(512), lat(512), lat(512),
                   _full((D, D)), _full((1, D)), _full((1, 512)), _full((4, 128, 128)), _full((1, 1))],
        out_shape=[jax.ShapeDtypeStruct((s_len, D), F32), jax.ShapeDtypeStruct((nqb, Q_BLOCK, 512), F32),
                   jax.ShapeDtypeStruct((s_len, 512), BF16), jax.ShapeDtypeStruct((s_len, 512), BF16),
                   jax.ShapeDtypeStruct((s_len, 512), F32),
                   jax.ShapeDtypeStruct((D, D), F32), jax.ShapeDtypeStruct((1, D), F32), jax.ShapeDtypeStruct((1, 512), F32),
                   jax.ShapeDtypeStruct((4, 128, 128), F32), jax.ShapeDtypeStruct((1, 1), F32)],
        compiler_params=_params(("arbitrary",)),
    )(attn, u, u, u, u, u, x, target, gate, w_pool, pool_scale, w_out)


def _attn_bwd(q, k, v, dattn, attn, lse, s_len):
    t_all = q.shape[1]
    off = (t_all - s_len) // TB
    nq = s_len // TB
    nch = 4
    chunks = [(c * (t_all // nch), t_all // nch) for c in range(nch)]
    nsub = next(n for n in (BWD_QBLOCKS, 2, 1) if nq % n == 0)
    tq = nsub * TB

    def body(*refs):
        q_refs = refs[:nsub]
        k_ref, v_ref, do_ref, o_ref, lse_ref, dq_ref, dk_ref, dv_ref = refs[nsub:]
        i = pl.program_id(1)

        @pl.when(i == 0)
        def _():
            dk_ref[...] = jnp.zeros_like(dk_ref)
            dv_ref[...] = jnp.zeros_like(dv_ref)

        qb = jnp.concatenate([r[0] for r in q_refs], axis=0)
        delta_r = _row_layout(_rowsum(do_ref[...] * o_ref[...]))[0:1, :]
        do = do_ref[...].astype(BF16)
        lse_r = jnp.concatenate([lse_ref[0, sb][0:1, :] for sb in range(nsub)], axis=1)
        dq = jnp.zeros((tq, DKP), F32)
        for start, size in chunks:
            rows = pl.ds(start, size)
            kc = k_ref[0, rows, :]
            p_t = jnp.exp2(_nt(kc, qb) - lse_r)
            ds_t = (p_t * (_nt(v_ref[0, rows, :], do) - delta_r)).astype(BF16)
            dv_ref[0, rows, :] += _nn(p_t.astype(BF16), do)
            dk_ref[0, rows, :] += _nn(ds_t, qb)
            dq += _tn(ds_t, kc)
        dq_ref[0] = dq * SCALE

    kvspec = lambda w: pl.BlockSpec((1, t_all, w), lambda h, i: (h, 0, 0))
    rowspec = pl.BlockSpec((1, nsub, 8, TB), lambda h, i: (h, i, 0, 0))
    qspec = lambda sb: pl.BlockSpec((1, TB, DKP), lambda h, i: (h, i * nsub + sb + off, 0))
    return pl.pallas_call(
        body, name="attn_bwd", grid=(NH, nq // nsub),
        in_specs=[qspec(sb) for sb in range(nsub)]
        + [kvspec(DKP), kvspec(DV), pl.BlockSpec((tq, DV), lambda h, i: (i, h)), pl.BlockSpec((tq, DV), lambda h, i: (i, h)),
           rowspec],
        out_specs=[pl.BlockSpec((1, tq, DKP), lambda h, i: (h, i, 0)), kvspec(DKP), kvspec(DV)],
        out_shape=[jax.ShapeDtypeStruct((NH, s_len, DKP), F32), jax.ShapeDtypeStruct((NH, t_all, DKP), F32),
                   jax.ShapeDtypeStruct((NH, t_all, DV), F32)],
        compiler_params=_params(("arbitrary", "arbitrary")),
    )(*([q] * nsub), k, v, dattn, attn, lse)


def _qkv_bwd(u, dq, dk, dv, cos, sin, q_lora_g, w_uq_t, kv_lora_g, w_ukv, qn_g, kn_g, s_len):
    t_all = u.shape[0]
    off = (t_all - s_len) // TB
    nb = t_all // TB

    def body(ulo_ref, dq_ref, dk_ref, dv_ref, cos_ref, sin_ref, qlg_ref, wuq_ref, kvlg_ref, wukv_ref, qng_ref, kng_ref,
             dlo_ref, dwuq_ref, dwukv_ref, dqlg_ref, dkvlg_ref, dqng_ref, dkng_ref):
        i = pl.program_id(0)

        @pl.when(i == 0)
        def _():
            for r in (dwuq_ref, dwukv_ref, dqlg_ref, dkvlg_ref, dqng_ref, dkng_ref):
                r[...] = jnp.zeros_like(r)

        latent = i >= off
        ulo = ulo_ref[...]
        cos, sin = _rope_block(cos_ref, sin_ref, pl.program_id(0) < off)
        cq = ulo[:, 0:QL]
        rc = lax.rsqrt(jnp.mean(cq * cq, axis=-1, keepdims=True) + EPS)
        cqh = cq * rc
        qlg = qlg_ref[...]
        cqn_b = (cqh * qlg).astype(BF16)
        qng = qng_ref[...]
        ckv = ulo[:, QL:QL + KVL]
        r0 = lax.rsqrt(jnp.mean(ckv * ckv, axis=-1, keepdims=True) + EPS)
        ckvh = ckv * r0
        kvlg = kvlg_ref[...]
        ckvn_b = (ckvh * kvlg).astype(BF16)
        qhs = [_nt(cqn_b, wuq_ref[hd]) for hd in range(NH)]
        kns = [_nn(ckvn_b, wukv_ref[hd])[:, :128] for hd in range(NH)]
        dqng = jnp.zeros((1, DKP), F32)
        dqraws = []
        for hd in range(NH):
            qh = qhs[hd]
            rq = lax.rsqrt(_rowsum(qh * qh) / DK + EPS)
            xh = qh * rq
            dqh = jnp.where(latent, dq_ref[hd], 0.0)
            dyq = jnp.concatenate([dqh[:, :128], _rope_t(dqh[:, 128:], cos, sin)], axis=1)
            dqng += _colsum(dyq * xh)
            dxh = dyq * qng
            dqraws.append((rq * (dxh - xh * (_rowsum(dxh * xh) / DK))).astype(BF16))
        dqng_ref[...] += dqng

        kr = ulo[:, 384:512]
        skr = _rowsum(kr * kr)
        kng = kng_ref[...]
        dkr = jnp.zeros((TB, 128), F32)
        dkng = jnp.zeros((1, DKP), F32)
        dkvs = []
        for hd in range(NH):
            kn = kns[hd]
            rk = lax.rsqrt((_rowsum(kn * kn) + skr) / DK + EPS)
            xh1 = kn * rk
            xh2 = kr * rk
            dkh = dk_ref[hd] * LN2
            d1 = dkh[:, :128]
            d2 = _rope_t(dkh[:, 128:], cos, sin)
            dkng += jnp.concatenate([_colsum(d1 * xh1), _colsum(d2 * xh2)], axis=1)
            dx1 = d1 * kng[:, :128]
            dx2 = d2 * kng[:, 128:]
            dot = (_rowsum(dx1 * xh1) + _rowsum(dx2 * xh2)) / DK
            dkvs.append(jnp.concatenate([rk * (dx1 - xh1 * dot), dv_ref[hd]], axis=1).astype(BF16))
            dkr += rk * (dx2 - xh2 * dot)
        dkng_ref[...] += dkng

        dcqn = jnp.zeros((TB, QL), F32)
        dckvn = jnp.zeros((TB, KVL), F32)
        for hd in range(NH):
            dwuq_ref[hd] += _tn(dqraws[hd], cqn_b)[:DK]
            dcqn += _nn(dqraws[hd], wuq_ref[hd])
            dwukv_ref[hd] += _tn(ckvn_b, dkvs[hd])
            dckvn += _nt(dkvs[hd], wukv_ref[hd])
        dqlg_ref[...] += _colsum(dcqn * cqh)
        dxh = dcqn * qlg
        dcq = rc * (dxh - cqh * jnp.mean(dxh * cqh, axis=-1, keepdims=True))
        dkvlg_ref[...] += _colsum(dckvn * ckvh)
        dxh = dckvn * kvlg
        dckv = r0 * (dxh - ckvh * jnp.mean(dxh * ckvh, axis=-1, keepdims=True))
        dlo_ref[...] = jnp.concatenate([dcq, dckv, dkr], axis=1).astype(BF16)

    row = lambda w: pl.BlockSpec((TB, w), lambda i: (i, 0))
    heads = lambda w: pl.BlockSpec((NH, TB, w), lambda i: (0, i, 0))
    return pl.pallas_call(
        body, name="qkv_bwd", grid=(nb,),
        in_specs=[row(512), pl.BlockSpec((NH, TB, DKP), lambda i: (0, jnp.maximum(i - off, 0), 0)), heads(DKP), heads(DV),
                  pl.BlockSpec((1, 8, 256), lambda i: (jnp.maximum(i - off, 0), 0, 0)), _full((TB, 256)), _full((1, QL)), _full((NH, DKP, QL)), _full((1, KVL)), _full((NH, KVL, 256)),
                  _full((1, DKP)), _full((1, DKP))],
        out_specs=[row(512), _full((NH, DK, QL)), _full((NH, KVL, 256)), _full((1, QL)), _full((1, KVL)),
                   _full((1, DKP)), _full((1, DKP))],
        out_shape=[jax.ShapeDtypeStruct((t_all, 512), BF16), jax.ShapeDtypeStruct((NH, DK, QL), F32),
                   jax.ShapeDtypeStruct((NH, KVL, 256), F32), jax.ShapeDtypeStruct((1, QL), F32),
                   jax.ShapeDtypeStruct((1, KVL), F32), jax.ShapeDtypeStruct((1, DKP), F32), jax.ShapeDtypeStruct((1, DKP), F32)],
        compiler_params=_params(("arbitrary",)),
    )(u, dq, dk, dv, cos, sin, q_lora_g, w_uq_t, kv_lora_g, w_ukv, qn_g, kn_g)


def _in_bwd(ctx, x, modsel, norm_g, dlo, dga, dgp, dpool, dxn, w_in_t):
    s_len, lc = x.shape[0], ctx.shape[0]
    t_all = s_len + lc
    off = lc // TB
    nb = t_all // TB
    nq = s_len // TB
    hb = TB // HALO
    n = TB + 2 * HALO

    def body(ctx_ref, x_ref, mod_ref, ng_ref, dlo_ref, dga_ref, dgp_ref, dp_ref, dpprev_ref, dpnext_ref, dxn_ref, win_ref,
             gx_ref, dwin_ref, dmod_ref, dng_ref):
        i = pl.program_id(0)
        j = i - off

        @pl.when(i == 0)
        def _():
            dwin_ref[...] = jnp.zeros_like(dwin_ref)
            dmod_ref[...] = jnp.zeros_like(dmod_ref)
            dng_ref[...] = jnp.zeros_like(dng_ref)

        latent = i >= off
        dp = dp_ref[...]
        prev = jnp.where(j <= 0, 0.0, dpprev_ref[...])
        nxt = jnp.where(j >= nq - 1, 0.0, dpnext_ref[...])
        win = jnp.concatenate([prev, dp, nxt], axis=0)
        tg = j * TB - HALO + lax.broadcasted_iota(jnp.int32, (n, 1), 0)
        dpin = []
        for g, w in enumerate(POOL_WINDOWS):
            cnt = jnp.maximum(jnp.minimum(tg + w // 2, s_len) - jnp.maximum(tg - w // 2, 0), 1).astype(F32)
            zq = win[:, g * 128:(g + 1) * 128] / cnt
            zq = zq + _shift_rows(zq, 1)
            for step in (1, 2, 4):
                if w >= 4 * step:
                    zq = _shift_rows(zq, -step) + _shift_rows(zq, step)
            dpin.append(zq[HALO:HALO + TB] - dp[:, g * 128:(g + 1) * 128])
        zero = jnp.zeros((TB, 512), BF16)
        du = [dlo_ref[...], jnp.where(latent, dga_ref[...], zero),
              jnp.where(latent, jnp.concatenate(dpin, axis=1).astype(BF16), zero), jnp.where(latent, dgp_ref[...], zero)]

        ng = ng_ref[...]
        xb = jnp.where(i < off, ctx_ref[...], x_ref[...])
        r, xh, xg, h, scale = _modulated(xb, mod_ref, ng)
        hb_ = h.astype(BF16)
        dh = jnp.zeros((TB, D), F32)
        for s, (lo, hi) in enumerate(SEG):
            dwin_ref[lo:hi, :] += _tn(du[s], hb_)
            dh += _nn(du[s], win_ref[lo:hi, :])
        is_lat = latent.astype(F32)
        dsh = _colsum(dh)
        dsc = _colsum(dh * xg)
        dmod_ref[0, 0:1, :] += dsh * (1.0 - is_lat)
        dmod_ref[0, 1:2, :] += dsc * (1.0 - is_lat)
        dmod_ref[1, 0:1, :] += dsh * is_lat
        dmod_ref[1, 1:2, :] += dsc * is_lat
        dxg = dh * (1.0 + scale)
        dng_ref[...] += _colsum(dxg * xh)
        dxh = dxg * ng
        gx_ref[...] = r * (dxh - xh * jnp.mean(dxh * xh, axis=-1, keepdims=True)) + dxn_ref[...]

    row = lambda w: pl.BlockSpec((TB, w), lambda i: (i, 0))
    lat = lambda w: pl.BlockSpec((TB, w), lambda i: (jnp.maximum(i - off, 0), 0))
    last8 = s_len // HALO - 1
    cspec, xspec, mspec = _token_specs(off)
    return pl.pallas_call(
        body, name="in_bwd", grid=(nb,),
        in_specs=[cspec, xspec, mspec, _full((1, D)), row(512), lat(512), lat(512), lat(512),
                  pl.BlockSpec((HALO, 512), lambda i: (jnp.maximum(jnp.maximum(i - off, 0) * hb - 1, 0), 0)),
                  pl.BlockSpec((HALO, 512), lambda i: (jnp.minimum((jnp.maximum(i - off, 0) + 1) * hb, last8), 0)),
                  lat(D), _full((DIN, D))],
        out_specs=[lat(D), _full((DIN, D)), _full((2, 2, D)), _full((1, D))],
        out_shape=[jax.ShapeDtypeStruct((s_len, D), F32), jax.ShapeDtypeStruct((DIN, D), F32),
                   jax.ShapeDtypeStruct((2, 2, D), F32), jax.ShapeDtypeStruct((1, D), F32)],
        compiler_params=_params(("arbitrary",)),
    )(ctx, x, modsel, norm_g, dlo, dga, dgp, dpool, dpool, dpool, dxn, w_in_t)


def _adamw_update(w_ref, g_ref, m_ref, v_ref, d_ref, mo_ref, vo_ref):
    gv = g_ref[...]
    mn = ADAM_B1 * m_ref[...] + (1.0 - ADAM_B1) * gv
    vn = ADAM_B2 * v_ref[...] + (1.0 - ADAM_B2) * (gv * gv)
    m_hat = mn / (1.0 - ADAM_B1 ** ADAM_STEP)
    v_hat = vn / (1.0 - ADAM_B2 ** ADAM_STEP)
    d_ref[...] = -ADAM_LR * (m_hat / (jnp.sqrt(v_hat) + ADAM_EPS) + ADAM_WD * w_ref[...])
    mo_ref[...] = mn
    vo_ref[...] = vn


def _adamw_many(ws, gs, ms, vs):
    n = len(ws)

    def body(*refs):
        for i in range(n):
            _adamw_update(refs[i], refs[n + i], refs[2 * n + i], refs[3 * n + i], refs[4 * n + i], refs[5 * n + i], refs[6 * n + i])
            refs[7 * n + i][...] = refs[n + i][...]

    def spec(w):
        rows, cols = w.shape
        return pl.BlockSpec((rows // 2, cols), lambda i: (i, 0)) if rows % 16 == 0 else _full((rows, cols))

    specs = [spec(w) for w in ws]
    shp = [jax.ShapeDtypeStruct(w.shape, F32) for w in ws]
    out = pl.pallas_call(body, name="adamw_many", grid=(2,), in_specs=specs * 4, out_specs=specs * 4, out_shape=shp * 4,
                         compiler_params=_params(("arbitrary",)))(*ws, *gs, *ms, *vs)
    return out[:n], out[n:2 * n], out[2 * n:3 * n], out[3 * n:]


class _Links:
    def __init__(self, send_sems, recv_sems):
        self.send_sems, self.recv_sems, self.sends = send_sems, recv_sems, []

    def send(self, src, dst, sem, to):
        cp = pltpu.make_async_remote_copy(src, dst, self.send_sems.at[sem], self.recv_sems.at[sem], device_id=to,
                                          device_id_type=MESH)
        cp.start()
        self.sends.append(cp)

    def arrived(self, dst, sem, frm):
        pltpu.make_async_remote_copy(dst, dst, self.send_sems.at[sem], self.recv_sems.at[sem], device_id=frm,
                                     device_id_type=MESH).wait_recv()

    def drain(self):
        for cp in self.sends:
            cp.wait_send()


def _half(ref, c, axis):
    size = ref.shape[axis - 2] // 2
    win = pl.ds(pl.multiple_of(c * size, 16 if axis == 0 else 128), size)
    idx = (win, slice(None)) if axis == 0 else (slice(None), win)
    return ref.at[(slice(None),) * (len(ref.shape) - 2) + idx]


def _select_rows(slots_ref, n_slots, row=0):
    sub = lax.broadcasted_iota(jnp.int32, (8, 1), 0)
    out = None
    for d in range(n_slots):
        r = jnp.where(sub == d, jnp.broadcast_to(slots_ref[d][row:row + 1, :], (8, slots_ref.shape[-1])), 0.0)
        out = r if out is None else out + r
    return out


def _gather(c, c_ctx, w_mod, b_mod_k, shards, axes, slab_rows):
    nw = len(shards)
    kw = w_mod.shape[1]

    def body(*refs):
        c_ref, cc_ref, wm_ref, b_ref = refs[:4]
        w_refs = refs[4:4 + nw]
        a16_ref, mod_ref = refs[4 + nw:6 + nw]
        g_refs = refs[6 + nw:6 + 2 * nw]
        a_ref, send_sems, recv_sems = refs[6 + 2 * nw:]

        def slab(wi, chip):
            return g_refs[wi].at[chip].at[0:shards[wi].shape[0]]
        x, y, cc = lax.axis_index("x"), lax.axis_index("y"), lax.axis_index("c")
        me = 4 * x + 2 * y + cc
        k = 2 * x + y
        sibling = (x, y, 1 - cc)
        links = _Links(send_sems, recv_sems)
        chips = [_peer(x, y, cc, off + (0,)) for off in CHIPS3]
        chip_a = ((x + 1 - cc) % 2, (y + cc) % 2, cc)
        chip_b = ((x + cc) % 2, (y + 1 - cc) % 2, cc)
        chip_d = (1 - x, 1 - y, cc)
        cv = c_ref[...]
        a_ref[me] = jnp.broadcast_to(cv * _sig(cv), (8, D))
        for j, off in enumerate(PEERS7):
            links.send(a_ref.at[me], a_ref.at[me], j, _peer(x, y, cc, off))
        for wi in range(nw):
            slab(wi, k)[...] = w_refs[wi][...].astype(BF16)
            for j, to in enumerate((chip_a, chip_b)):
                links.send(_half(slab(wi, k), cc, axes[wi]), _half(slab(wi, k), cc, axes[wi]), 10 + wi * 6 + j, to)
            pad = slab_rows[wi] - shards[wi].shape[0]
            if pad:
                for kk in range(4):
                    g_refs[wi][kk, shards[wi].shape[0]:, :] = jnp.zeros((pad, shards[wi].shape[1]), BF16)
        for j, off in enumerate(PEERS7):
            px, py, pc = _peer(x, y, cc, off)
            links.arrived(a_ref.at[4 * px + 2 * py + pc], j, (px, py, pc))
        ccv = cc_ref[...]
        sub = lax.broadcasted_iota(jnp.int32, (8, 1), 0)
        a16 = jnp.concatenate([_select_rows(a_ref, 8), jnp.where(sub == 0, jnp.broadcast_to(ccv * _sig(ccv), (8, D)), 0.0)], axis=0)
        a16_ref[...] = a16
        mod_ref[k] = _dot3(_nn, a16, wm_ref[...]) + b_ref[...]
        for j, to in enumerate(chips):
            links.send(mod_ref.at[k], mod_ref.at[k], 7 + j, to)
        for j, (frm, origin) in enumerate(((chip_a, chip_a), (chip_b, chip_b), (chip_b, chip_d))):
            for wi in range(nw):
                blk = _half(slab(wi, 2 * origin[0] + origin[1]), cc, axes[wi])
                links.arrived(blk, 10 + wi * 6 + j, frm)
                if j == 0:
                    links.send(blk, blk, 10 + wi * 6 + 2, chip_b)
                links.send(blk, blk, 10 + wi * 6 + 3 + j, sibling)
        for j, (px, py, pc) in enumerate(chips):
            links.arrived(mod_ref.at[2 * px + py], 7 + j, (px, py, pc))
        for j, origin in enumerate((chip_b, chip_a, chip_d)):
            for wi in range(nw):
                links.arrived(_half(slab(wi, 2 * origin[0] + origin[1]), 1 - cc, axes[wi]), 10 + wi * 6 + 3 + j, sibling)
        links.drain()

    nsem = 10 + 6 * nw
    return pl.pallas_call(
        body, name="gather", in_specs=[VM] * (4 + nw), out_specs=[VM] * (2 + nw),
        out_shape=[jax.ShapeDtypeStruct((16, D), F32), jax.ShapeDtypeStruct((4, 16, kw), F32)]
        + [jax.ShapeDtypeStruct((4, r, s.shape[1]), BF16) for r, s in zip(slab_rows, shards)],
        scratch_shapes=[pltpu.VMEM((8, 8, D), F32), pltpu.SemaphoreType.DMA((nsem,)), pltpu.SemaphoreType.DMA((nsem,))],
        compiler_params=pltpu.CompilerParams(vmem_limit_bytes=VMEM_LIMIT),
    )(c, c_ctx, w_mod, b_mod_k, *shards)


SMALL_ROW_WIDTHS = (D, QL, KVL, DKP, DKP, 512, 128)
SMALL_OUT_WIDTHS = (D, QL, KVL, DK, DK, 512, 1)


def _reduce(grads, axes, smalls, w_pool_g, dmod8, a16, w_mod, c_ctx, m_w_mod, v_w_mod):
    nw = len(grads)
    ns = len(smalls)
    kw = w_mod.shape[1]
    halves = []
    for g, ax in zip(grads, axes):
        halves.append((g.shape[1] // 2, g.shape[2]) if ax == 0 else (g.shape[1], g.shape[2] // 2))

    def body(*refs):
        g_refs = refs[:nw]
        small_refs = refs[nw:nw + ns]
        wp_ref, dm_ref, a16_ref, wm_ref, cc_ref, mwm_ref, vwm_ref = refs[nw + ns:nw + ns + 7]
        o = nw + ns + 7
        r_refs = refs[o:o + nw]
        small_outs = refs[o + nw:o + nw + ns]
        rwp_ref, gw_ref, gb_ref, gc_ref, dwm_ref, mwm_out, vwm_out = refs[o + nw + ns:o + nw + ns + 7]
        o = o + nw + ns + 7
        own, sib, part, got, rel = (refs[o + i * nw:o + (i + 1) * nw] for i in range(5))
        smbuf, wps, wpg, dm_all, pc_all, m_s, v_s, d_s, send_sems, recv_sems, local_sems, adam_sems = refs[o + 5 * nw:]
        adam_in = [pltpu.make_async_copy(mwm_ref, m_s, adam_sems.at[0]), pltpu.make_async_copy(vwm_ref, v_s, adam_sems.at[1])]
        for cp in adam_in:
            cp.start()
        x, y, cc = lax.axis_index("x"), lax.axis_index("y"), lax.axis_index("c")
        me = 4 * x + 2 * y + cc
        k = 2 * x + y
        sibling = (x, y, 1 - cc)
        links = _Links(send_sems, recv_sems)
        chips = [_peer(x, y, cc, off + (0,)) for off in CHIPS3]
        peers = [_peer(x, y, cc, off) for off in PEERS7]
        chip_a = ((x + 1 - cc) % 2, (y + cc) % 2, cc)
        chip_b = ((x + cc) % 2, (y + 1 - cc) % 2, cc)
        ka, kb, kd = 2 * chip_a[0] + chip_a[1], 2 * chip_b[0] + chip_b[1], 2 * (1 - x) + (1 - y)
        big, sm0, wp0, dm0, pc0 = 0, 5 * nw, 5 * nw + 7, 5 * nw + 14, 5 * nw + 21

        locals_ = []
        for wi in range(nw):
            lc = pltpu.make_async_copy(_half(g_refs[wi], cc, axes[wi]), own[wi], local_sems.at[wi])
            lc.start()
            locals_.append(lc)
            links.send(_half(g_refs[wi], 1 - cc, axes[wi]), sib[wi], big + wi * 5, sibling)
        slot = smbuf.at[me]
        slot[...] = jnp.zeros((8, D), F32)
        for r, (ref, w) in enumerate(zip(small_refs, SMALL_ROW_WIDTHS)):
            slot[r:r + 1, 0:w] = jnp.broadcast_to(ref[...], (1, w))
        links.send(wp_ref, wps, wp0, sibling)
        dm_all[me] = dm_ref[...]
        for j, peer in enumerate(peers):
            links.send(dm_all.at[me], dm_all.at[me], dm0 + j, peer)
            links.send(smbuf.at[me], smbuf.at[me], sm0 + j, peer)
        links.arrived(wps, wp0, sibling)
        wpg[k] = (wp_ref[...] + wps[...]).astype(BF16)
        for j, to in enumerate(chips):
            links.send(wpg.at[k], wpg.at[k], wp0 + 1 + j, to)
        for wi in range(nw):
            locals_[wi].wait()
            links.arrived(sib[wi], big + wi * 5, sibling)
            part[wi][...] = (own[wi][...] + sib[wi][...]).astype(BF16)
            got[wi][k] = part[wi][k]
            got[wi][kd] = jnp.zeros(halves[wi], BF16)
            links.send(part[wi].at[kd], rel[wi], big + wi * 5 + 1, chip_b)
            links.send(part[wi].at[kb], got[wi].at[k], big + wi * 5 + 2, chip_b)
        for j, (px, py, pc) in enumerate(peers):
            links.arrived(dm_all.at[4 * px + 2 * py + pc], dm0 + j, (px, py, pc))
        dmc = dm_all[0][1:2, :]
        dml = dm_all[0][0:1, :]
        for d in range(1, 8):
            dmc = dmc + dm_all[d][1:2, :]
            dml = dml + dm_all[d][0:1, :]
        gb_ref[...] = dml + dmc
        sub = lax.broadcasted_iota(jnp.int32, (8, 1), 0)
        b16 = jnp.concatenate([_select_rows(dm_all, 8), jnp.where(sub == 0, jnp.broadcast_to(dmc, (8, 3 * D)), 0.0)], axis=0)
        bk = jnp.zeros((16, kw), F32)
        for kk in range(4):
            bk = bk + jnp.where(k == kk, b16[:, kk * kw:(kk + 1) * kw], 0.0)
        gw_ref[...] = _dot3(_tn, a16_ref[...], bk)
        for cp in adam_in:
            cp.wait()
        _adamw_update(wm_ref, gw_ref, m_s, v_s, d_s, m_s, v_s)
        adam_out = [pltpu.make_async_copy(d_s, dwm_ref, adam_sems.at[2]), pltpu.make_async_copy(m_s, mwm_out, adam_sems.at[3]),
                    pltpu.make_async_copy(v_s, vwm_out, adam_sems.at[4])]
        for cp in adam_out:
            cp.start()
        pc_all[k] = _dot3(_nt, jnp.broadcast_to(bk[8:9, :], (8, kw)), wm_ref[...])
        for j, to in enumerate(chips):
            links.send(pc_all.at[k], pc_all.at[k], pc0 + j, to)
        for j, (px, py, pc) in enumerate(peers):
            links.arrived(smbuf.at[4 * px + 2 * py + pc], sm0 + j, (px, py, pc))
        tot = smbuf[0]
        for d in range(1, 8):
            tot = tot + smbuf[d]
        for r, (ref, w) in enumerate(zip(small_outs, SMALL_OUT_WIDTHS)):
            ref[...] = tot[r:r + 1, 0:w]
        for j, (px, py, pc) in enumerate(chips):
            links.arrived(wpg.at[2 * px + py], wp0 + 1 + j, (px, py, pc))
        wpt = wpg[0].astype(F32)
        for kk in range(1, 4):
            wpt = wpt + wpg[kk].astype(F32)
        rwp_ref[...] = wpt
        for wi in range(nw):
            links.arrived(rel[wi], big + wi * 5 + 1, chip_b)
            rel[wi][...] = (part[wi][ka].astype(F32) + rel[wi][...].astype(F32)).astype(BF16)
            links.send(rel[wi], got[wi].at[k], big + wi * 5 + 3, chip_a)
        for wi in range(nw):
            links.arrived(got[wi].at[kb], big + wi * 5 + 2, chip_b)
            links.arrived(got[wi].at[ka], big + wi * 5 + 3, chip_a)
            total = got[wi][0].astype(F32)
            for kk in range(1, 4):
                total = total + got[wi][kk].astype(F32)
            mine = _half(r_refs[wi], cc, axes[wi])
            mine[...] = total
            links.send(mine, mine, big + wi * 5 + 4, sibling)
        for j, (px, py, pc) in enumerate(chips):
            links.arrived(pc_all.at[2 * px + py], pc0 + j, (px, py, pc))
        ccv = cc_ref[...]
        sg = _sig(ccv)
        gc_ref[...] = (pc_all[0][0:1, :] + pc_all[1][0:1, :] + pc_all[2][0:1, :] + pc_all[3][0:1, :]) * (sg * (1.0 + ccv * (1.0 - sg)))
        for wi in range(nw):
            links.arrived(_half(r_refs[wi], 1 - cc, axes[wi]), big + wi * 5 + 4, sibling)
        links.drain()
        for cp in adam_out:
            cp.wait()

    nsem = 5 * nw + 24
    quads = [(4,) + h for h in halves]
    wm_shape = jax.ShapeDtypeStruct((D, kw), F32)
    return pl.pallas_call(
        body, name="reduce", in_specs=[ANY] * nw + [VM] * (ns + 5) + [ANY] * 2, out_specs=[VM] * (nw + ns + 4) + [ANY] * 3,
        out_shape=[jax.ShapeDtypeStruct(g.shape[1:], F32) for g in grads]
        + [jax.ShapeDtypeStruct((1, w), F32) for w in SMALL_OUT_WIDTHS]
        + [jax.ShapeDtypeStruct(w_pool_g.shape, F32), wm_shape, jax.ShapeDtypeStruct((1, 3 * D), F32),
           jax.ShapeDtypeStruct((1, D), F32), wm_shape, wm_shape, wm_shape],
        scratch_shapes=[pltpu.VMEM(q, F32) for q in quads] + [pltpu.VMEM(q, F32) for q in quads]
        + [pltpu.VMEM(q, BF16) for q in quads] + [pltpu.VMEM(q, BF16) for q in quads] + [pltpu.VMEM(h, BF16) for h in halves]
        + [pltpu.VMEM((8, 8, D), F32), pltpu.VMEM(w_pool_g.shape, F32), pltpu.VMEM((4,) + w_pool_g.shape, BF16),
           pltpu.VMEM((8, 8, 3 * D), F32),
           pltpu.VMEM((4, 8, D), F32), pltpu.VMEM((D, kw), F32), pltpu.VMEM((D, kw), F32), pltpu.VMEM((D, kw), F32)]
        + [pltpu.SemaphoreType.DMA((nsem,)), pltpu.SemaphoreType.DMA((nsem,)), pltpu.SemaphoreType.DMA((nw,)),
           pltpu.SemaphoreType.DMA((5,))],
        compiler_params=pltpu.CompilerParams(vmem_limit_bytes=VMEM_LIMIT),
    )(*grads, *smalls, w_pool_g, dmod8, a16, w_mod, c_ctx, m_w_mod, v_w_mod)


def _rope_tables(s_len):
    rows = s_len // GRID_W
    per = TB // GRID_W
    n_freq = 16
    inv = ROPE_BASE ** (-jnp.arange(n_freq, dtype=F32) / n_freq)
    ang_r = jnp.arange(rows, dtype=F32)[:, None] * inv
    ang_c = jnp.arange(GRID_W, dtype=F32)[:, None] * inv
    by_row, by_col = [], []
    for fn, pad in ((jnp.cos, 1.0), (jnp.sin, 0.0)):
        r = jnp.concatenate([fn(ang_r), fn(ang_r), jnp.zeros((rows, 96), F32)], axis=1).reshape(rows // per, per, 128)
        by_row.append(jnp.pad(r, ((0, 0), (0, 8 - per), (0, 0))))
        cpart = jnp.concatenate([jnp.zeros((GRID_W, 32), F32), fn(ang_c), fn(ang_c), jnp.full((GRID_W, 64), pad, F32)], axis=1)
        by_col.append(jnp.tile(cpart, (per, 1)))
    return jnp.concatenate(by_row, axis=-1), jnp.concatenate(by_col, axis=-1)


def kernel(x, c, ctx, c_ctx, w_mod, b_mod, norm_g, w_in, q_lora_g, w_uq, kv_lora_g, w_ukv, q_norm_g, k_norm_g, w_pool, pool_scale, w_out, loss_target, m_c_ctx, m_w_mod, m_b_mod, m_norm_g, m_w_in, m_q_lora_g, m_w_uq, m_kv_lora_g, m_w_ukv, m_q_norm_g, m_k_norm_g, m_w_pool, m_pool_scale, m_w_out, v_c_ctx, v_w_mod, v_b_mod, v_norm_g, v_w_in, v_q_lora_g, v_w_uq, v_kv_lora_g, v_w_ukv, v_q_norm_g, v_k_norm_g, v_w_pool, v_pool_scale, v_w_out):
    xi, yi, ci = lax.axis_index("x"), lax.axis_index("y"), lax.axis_index("c")
    me = 4 * xi + 2 * yi + ci
    k = 2 * xi + yi
    s_len = x.shape[1]
    lc = ctx.shape[1]
    kw = w_mod.shape[2]
    weights = dict(c_ctx=c_ctx, w_mod=w_mod, b_mod=b_mod, norm_g=norm_g, w_in=w_in, q_lora_g=q_lora_g, w_uq=w_uq,
                   kv_lora_g=kv_lora_g, w_ukv=w_ukv, q_norm_g=q_norm_g, k_norm_g=k_norm_g, w_pool=w_pool,
                   pool_scale=pool_scale, w_out=w_out)
    m_in = dict(c_ctx=m_c_ctx, w_mod=m_w_mod, b_mod=m_b_mod, norm_g=m_norm_g, w_in=m_w_in, q_lora_g=m_q_lora_g, w_uq=m_w_uq,
                kv_lora_g=m_kv_lora_g, w_ukv=m_w_ukv, q_norm_g=m_q_norm_g, k_norm_g=m_k_norm_g, w_pool=m_w_pool,
                pool_scale=m_pool_scale, w_out=m_w_out)
    v_in = dict(c_ctx=v_c_ctx, w_mod=v_w_mod, b_mod=v_b_mod, norm_g=v_norm_g, w_in=v_w_in, q_lora_g=v_q_lora_g, w_uq=v_w_uq,
                kv_lora_g=v_kv_lora_g, w_ukv=v_w_ukv, q_norm_g=v_q_norm_g, k_norm_g=v_k_norm_g, w_pool=v_w_pool,
                pool_scale=v_pool_scale, w_out=v_w_out)
    order = ["c_ctx", "w_mod", "b_mod", "norm_g", "w_in", "q_lora_g", "w_uq", "kv_lora_g", "w_ukv", "q_norm_g", "k_norm_g",
             "w_pool", "pool_scale", "w_out"]
    transposed = ("w_in", "w_uq")
    as2d = lambda n, a: jnp.transpose(a[0]) if n in transposed else a.reshape(-1, a.shape[-1])
    back = lambda n, a: jnp.transpose(a)[None] if n in transposed else a.reshape(weights[n].shape)

    c_ctx2 = c_ctx.reshape(1, D)
    b_mod_k = lax.dynamic_slice(b_mod, (0, k * kw), (1, kw))
    split = (1, 0, 0, 0)
    a16, mod_all, g_in, g_uq, g_ukv, g_out = _gather(
        c, c_ctx2, w_mod[0], b_mod_k, [as2d("w_in", w_in), as2d("w_uq", w_uq), w_ukv[0], w_out[0]], split,
        (DIN // 4, DKP, KVL, D // 4))
    mod_me = lax.dynamic_index_in_dim(mod_all, me, axis=1, keepdims=False).reshape(3, D)
    mod_c = mod_all[:, 8, :].reshape(3, D)
    modsel = jnp.stack([mod_c, mod_me])
    w_in_t = g_in.reshape(DIN, D)
    w_uq_t = g_uq
    w_out_f = g_out.reshape(D, D)
    qn_g = jnp.pad(q_norm_g, ((0, 0), (0, DKP - DK)))
    kn_g = jnp.pad(k_norm_g, ((0, 0), (0, DKP - DK)))
    w_pool_b = w_pool[0].astype(BF16)
    cos, sin = _rope_tables(s_len)

    u, q, kk, v = _fwd_in(ctx[0], x[0], modsel, norm_g, w_in_t, q_lora_g, w_uq_t, kv_lora_g, g_ukv, qn_g, kn_g, cos, sin)
    attn, lse = _attn_fwd(q, kk, v, s_len)
    (dxn, dattn, dga, dgp, dpool, dw_out, dgate, dps, dw_pool, loss) = _out_stage(
        attn.reshape(s_len // Q_BLOCK, Q_BLOCK, NH * DV), u, x[0], loss_target[0], modsel[1, 2:3, :], w_pool_b, pool_scale,
        w_out_f, lc)
    dattn = dattn.reshape(s_len, NH * DV)
    dq, dk, dv = _attn_bwd(q, kk, v, dattn, attn, lse, s_len)
    dlo, dw_uq_t, dw_ukv, dqlg, dkvlg, dqng, dkng = _qkv_bwd(u, dq, dk, dv, cos, sin, q_lora_g, w_uq_t, kv_lora_g, g_ukv,
                                                            qn_g, kn_g, s_len)
    gx, dw_in_t, dmod, dng = _in_bwd(ctx[0], x[0], modsel, norm_g, dlo, dga, dgp, dpool, dxn, w_in_t)

    dmod_l = jnp.concatenate([dmod[1, 0], dmod[1, 1], dgate[0]]).reshape(1, 3 * D)
    dmod_c = jnp.concatenate([dmod[0, 0], dmod[0, 1], jnp.zeros((D,), F32)]).reshape(1, 3 * D)
    dmod8 = jnp.concatenate([dmod_l, dmod_c, jnp.zeros((6, 3 * D), F32)], axis=0)
    (r_in, r_uq, r_ukv, r_out, g_ng, g_qlg, g_kvlg, g_qng, g_kng, g_ps, loss_all, g_wp, g_w_mod, g_b_mod, g_c_ctx,
     d_w_mod, nm_w_mod, nv_w_mod) = _reduce(
        [dw_in_t.reshape(4, DIN // 4, D), dw_uq_t, dw_ukv, dw_out.reshape(4, D // 4, D)], split,
        [dng, dqlg, dkvlg, dqng, dkng, dps, loss], dw_pool, dmod8, a16, w_mod[0], c_ctx2, m_w_mod[0], v_w_mod[0])
    g2d = dict(c_ctx=g_c_ctx, b_mod=g_b_mod, w_mod=g_w_mod, w_in=r_in, w_uq=r_uq, w_ukv=r_ukv, w_out=r_out, norm_g=g_ng,
               q_lora_g=g_qlg, kv_lora_g=g_kvlg, q_norm_g=g_qng, k_norm_g=g_kng, pool_scale=g_ps, w_pool=g_wp.reshape(512, 128))

    d2d, m2d, v2d = dict(w_mod=d_w_mod), dict(w_mod=nm_w_mod), dict(w_mod=nv_w_mod)
    rest = [n for n in order if n != "w_mod"]
    outs = _adamw_many([as2d(n, weights[n]) for n in rest], [g2d[n] for n in rest], [as2d(n, m_in[n]) for n in rest],
                       [as2d(n, v_in[n]) for n in rest])
    for dst, arrs in zip((d2d, m2d, v2d, g2d), outs):
        dst.update(dict(zip(rest, arrs)))

    return (loss_all[0, 0], gx[None], *[back(n, g2d[n]) for n in order], *[back(n, d2d[n]) for n in order],
            *[back(n, m2d[n]) for n in order], *[back(n, v2d[n]) for n in order])
```

```python
import jax
import jax.numpy as jnp
from jax import lax
from jax.experimental import pallas as pl
from jax.experimental.pallas import tpu as pltpu

F32 = jnp.float32
BF16 = jnp.bfloat16
MESH = pl.DeviceIdType.MESH

D = 1024
NH = 4
DK = 192
DKP = 256
DV = 128
QL = 256
KVL = 128
DIN = 1984
U_LO = 448
SEG = ((0, 512), (448, 960), (960, 1472), (1472, 1984))
DU = 2048
POOL_WINDOWS = (2, 4, 8, 16)
HALO = 8
EPS = 1e-6
ROPE_BASE = 10000.0
GRID_W = 64
Q_BLOCK = 128
TB = 256
BWD_QBLOCKS = 1
SCALE = DK ** -0.5
LOG2E = 1.4426950408889634
LN2 = 0.6931471805599453
VMEM_LIMIT = 56 * 1024 * 1024

ADAM_LR = 0.001
ADAM_B1 = 0.9
ADAM_B2 = 0.999
ADAM_EPS = 1e-08
ADAM_WD = 0.01
ADAM_STEP = 10

CHIPS3 = ((1, 0), (0, 1), (1, 1))
PEERS7 = tuple((dx, dy, dc) for dx in (0, 1) for dy in (0, 1) for dc in (0, 1) if (dx, dy, dc) != (0, 0, 0))

VM = pl.BlockSpec(memory_space=pltpu.VMEM)
ANY = pl.BlockSpec(memory_space=pl.ANY)


def _nn(a, b):
    return jnp.dot(a, b, preferred_element_type=F32)


def _nt(a, b):
    return lax.dot_general(a, b, (((1,), (1,)), ((), ())), preferred_element_type=F32)


def _tn(a, b):
    return lax.dot_general(a, b, (((0,), (0,)), ((), ())), preferred_element_type=F32)


def _split3(a):
    a0 = a.astype(BF16)
    r = a - a0.astype(F32)
    a1 = r.astype(BF16)
    a2 = (r - a1.astype(F32)).astype(BF16)
    return a0, a1, a2


def _dot3(dot, a, b):
    sa = _split3(a)
    sb = _split3(b)
    out = None
    for i in range(3):
        for j in range(3 - i):
            t = dot(sa[i], sb[j])
            out = t if out is None else out + t
    return out


def _sig(x):
    return 1.0 / (1.0 + jnp.exp(-x))


def _rot(t):
    src = lax.broadcasted_iota(jnp.int32, (128, 128), 0)
    dst = lax.broadcasted_iota(jnp.int32, (128, 128), 1)
    first = (dst % 32) < 16
    perm = jnp.where(first & (src == dst + 16), -1.0, jnp.where(~first & (src == dst - 16), 1.0, 0.0)).astype(BF16)
    hi = t.astype(BF16)
    lo = (t - hi.astype(F32)).astype(BF16)
    return _nn(hi, perm) + _nn(lo, perm)


def _rope(t, cos, sin):
    return t * cos + _rot(t) * sin


def _rope_t(t, cos, sin):
    return t * cos - _rot(t * sin)


def _rope_block(rows_ref, cols_ref, is_ctx):
    lane = lax.broadcasted_iota(jnp.int32, (TB, 256), 1) % 128
    rows = jnp.concatenate([jnp.broadcast_to(rows_ref[0, r:r + 1, :], (GRID_W, 256)) for r in range(TB // GRID_W)], axis=0)
    cs = jnp.where(lane < 32, rows, cols_ref[...])
    return jnp.where(is_ctx, 1.0, cs[:, :128]), jnp.where(is_ctx, 0.0, cs[:, 128:])


def _shift_rows(z, k):
    n = z.shape[0]
    return pltpu.roll(z, (n - k) % n, 0)


def _colsum(a):
    return jnp.sum(a, axis=0, keepdims=True)


def _rowsum(a):
    return jnp.sum(a, axis=-1, keepdims=True)


def _row_layout(col):
    return jnp.transpose(jnp.broadcast_to(col, (col.shape[0], 128)))[0:8, :]


def _params(sem=None):
    return pltpu.CompilerParams(dimension_semantics=sem, vmem_limit_bytes=VMEM_LIMIT)


def _full(shape):
    nd = len(shape)
    return pl.BlockSpec(shape, lambda *_: (0,) * nd)


def _peer(x, y, c, off):
    dx, dy, dc = off
    return ((x + dx) % 2, (y + dy) % 2, (c + dc) % 2)


def _token_specs(off):
    ctx = pl.BlockSpec((TB, D), lambda i: (jnp.minimum(i, off - 1), 0))
    lat = pl.BlockSpec((TB, D), lambda i: (jnp.maximum(i - off, 0), 0))
    mod = pl.BlockSpec((1, 3, D), lambda i: (jnp.minimum(i // off, 1), 0, 0))
    return ctx, lat, mod


def _modulated(x, mod_ref, ng):
    shift = mod_ref[0, 0:1, :]
    scale = mod_ref[0, 1:2, :]
    r = lax.rsqrt(jnp.mean(x * x, axis=-1, keepdims=True) + EPS)
    xh = x * r
    xg = xh * ng
    return r, xh, xg, xg * (1.0 + scale) + shift, scale


def _fwd_in(ctx, x, modsel, norm_g, w_in_t, q_lora_g, w_uq_t, kv_lora_g, w_ukv, qn_g, kn_g, cos, sin):
    s_len, lc = x.shape[0], ctx.shape[0]
    t_all = s_len + lc
    nb = t_all // TB
    off = lc // TB

    def body(ctx_ref, x_ref, mod_ref, ng_ref, win_ref, qlg_ref, wuq_ref, kvlg_ref, wukv_ref, qng_ref, kng_ref, cos_ref, sin_ref,
             u_ref, q_ref, k_ref, v_ref):
        is_ctx = pl.program_id(0) < off
        xb = jnp.where(is_ctx, ctx_ref[...], x_ref[...])
        _, _, _, h, _ = _modulated(xb, mod_ref, ng_ref[...])
        hb = h.astype(BF16)
        lane = lax.broadcasted_iota(jnp.int32, (TB, 512), 1)
        ulo = jnp.where(lane < U_LO, _nt(hb, win_ref[SEG[0][0]:SEG[0][1], :]), 0.0)
        u_ref[:, 0:512] = ulo
        for j in range(1, 4):
            u_ref[:, j * 512:(j + 1) * 512] = _nt(hb, win_ref[SEG[j][0]:SEG[j][1], :])
        cos, sin = _rope_block(cos_ref, sin_ref, is_ctx)
        cq = ulo[:, 0:QL]
        cqn = (cq * lax.rsqrt(jnp.mean(cq * cq, axis=-1, keepdims=True) + EPS) * qlg_ref[...]).astype(BF16)
        qng = qng_ref[...]
        ckv = ulo[:, QL:QL + KVL]
        ckvn = (ckv * lax.rsqrt(jnp.mean(ckv * ckv, axis=-1, keepdims=True) + EPS) * kvlg_ref[...]).astype(BF16)
        qhs = [_nt(cqn, wuq_ref[hd]) for hd in range(NH)]
        kvs = [_nn(ckvn, wukv_ref[hd]) for hd in range(NH)]
        for hd in range(NH):
            qh = qhs[hd]
            qn = qh * lax.rsqrt(_rowsum(qh * qh) / DK + EPS) * qng
            q_ref[hd] = (jnp.concatenate([qn[:, :128], _rope(qn[:, 128:], cos, sin)], axis=1) * (SCALE * LOG2E)).astype(BF16)
        kr = ulo[:, 384:512]
        skr = _rowsum(kr * kr)
        kng = kng_ref[...]
        kr_roped = _rope(kr * kng[:, 128:], cos, sin)
        for hd in range(NH):
            kv = kvs[hd]
            kn = kv[:, :128]
            rk = lax.rsqrt((_rowsum(kn * kn) + skr) / DK + EPS)
            k_ref[hd] = jnp.concatenate([kn * rk * kng[:, :128], kr_roped * rk], axis=1).astype(BF16)
            v_ref[hd] = kv[:, 128:].astype(BF16)

    row = lambda w: pl.BlockSpec((TB, w), lambda i: (i, 0))
    heads = lambda w: pl.BlockSpec((NH, TB, w), lambda i: (0, i, 0))
    cspec, xspec, mspec = _token_specs(off)
    return pl.pallas_call(
        body, name="fwd_in", grid=(nb,),
        in_specs=[cspec, xspec, mspec, _full((1, D)), _full((DIN, D)), _full((1, QL)), _full((NH, DKP, QL)), _full((1, KVL)),
                  _full((NH, KVL, 256)), _full((1, DKP)), _full((1, DKP)),
                  pl.BlockSpec((1, 8, 256), lambda i: (jnp.maximum(i - off, 0), 0, 0)), _full((TB, 256))],
        out_specs=[row(DU), heads(DKP), heads(DKP), heads(DV)],
        out_shape=[jax.ShapeDtypeStruct((t_all, DU), F32), jax.ShapeDtypeStruct((NH, t_all, DKP), BF16),
                   jax.ShapeDtypeStruct((NH, t_all, DKP), BF16), jax.ShapeDtypeStruct((NH, t_all, DV), BF16)],
        compiler_params=_params(("arbitrary",)),
    )(ctx, x, modsel, norm_g, w_in_t, q_lora_g, w_uq_t, kv_lora_g, w_ukv, qn_g, kn_g, cos, sin)


def _attn_fwd(q, k, v, s_len):
    t_all = q.shape[1]
    off = (t_all - s_len) // TB
    nq = s_len // TB
    nsub = next(n for n in (4, 2, 1) if nq % n == 0)

    def body(*refs):
        q_refs = refs[:nsub]
        k_ref, v_ref, o_ref, lse_ref = refs[nsub:]
        for sb in range(nsub):
            s = _nt(q_refs[sb][0], k_ref[0])
            m = jnp.max(s, axis=-1, keepdims=True)
            e = jnp.exp2(s - m)
            l = _rowsum(e)
            o_ref[sb * TB:(sb + 1) * TB, :] = _nn(e.astype(BF16), v_ref[0]) / l
            lse_ref[0, sb] = _row_layout(m + jnp.log2(l))

    qspec = lambda sb: pl.BlockSpec((1, TB, DKP), lambda h, i: (h, i * nsub + sb + off, 0))
    return pl.pallas_call(
        body, name="attn_fwd", grid=(NH, nq // nsub),
        in_specs=[qspec(sb) for sb in range(nsub)]
        + [pl.BlockSpec((1, t_all, DKP), lambda h, i: (h, 0, 0)), pl.BlockSpec((1, t_all, DV), lambda h, i: (h, 0, 0))],
        out_specs=[pl.BlockSpec((nsub * TB, DV), lambda h, i: (i, h)), pl.BlockSpec((1, nsub, 8, TB), lambda h, i: (h, i, 0, 0))],
        out_shape=[jax.ShapeDtypeStruct((s_len, NH * DV), F32), jax.ShapeDtypeStruct((NH, nq, 8, TB), F32)],
        compiler_params=_params(("arbitrary", "arbitrary")),
    )(*([q] * nsub), k, v)


def _out_stage(attn, u, x, target, gate, w_pool, pool_scale, w_out, lc):
    s_len = x.shape[0]
    t_all = s_len + lc
    off = lc // TB
    nq = s_len // TB
    hb = TB // HALO
    nqb = s_len // Q_BLOCK
    jb = TB // nqb

    def body(attn_ref, ga_ref, pin_ref, pprev_ref, pnext_ref, gp_ref, x_ref, tgt_ref, gate_ref, wp_ref, ps_ref, wo_ref,
             dxn_ref, dattn_ref, dga_ref, dgp_ref, dpool_ref, dwo_ref, dgate_ref, dps_ref, dwp_ref, loss_ref):
        i = pl.program_id(0)

        @pl.when(i == 0)
        def _():
            dwo_ref[...] = jnp.zeros_like(dwo_ref)
            dgate_ref[...] = jnp.zeros_like(dgate_ref)
            dps_ref[...] = jnp.zeros_like(dps_ref)
            dwp_ref[...] = jnp.zeros_like(dwp_ref)
            loss_ref[...] = jnp.zeros_like(loss_ref)

        attn = jnp.concatenate([attn_ref[:, jj, :] for jj in range(jb)], axis=0)
        ga = ga_ref[...]
        gp = gp_ref[...]
        pin = pin_ref[...]
        prev = jnp.where(i == 0, 0.0, pprev_ref[...])
        nxt = jnp.where(i == nq - 1, 0.0, pnext_ref[...])
        win = jnp.concatenate([prev, pin, nxt], axis=0)
        tg = i * TB + lax.broadcasted_iota(jnp.int32, (TB, 1), 0)
        pooled = []
        for g, w in enumerate(POOL_WINDOWS):
            a = win[:, g * 128:(g + 1) * 128]
            p = _shift_rows(a, -1) + a
            for step in (1, 2, 4):
                if w >= 4 * step:
                    p = _shift_rows(p, -step) + _shift_rows(p, step)
            cnt = (jnp.minimum(tg + w // 2, s_len) - jnp.maximum(tg - w // 2, 0)).astype(F32)
            pooled.append(p[HALO:HALO + TB] / cnt - a[HALO:HALO + TB])
        pooled_b = [p.astype(BF16) for p in pooled]
        z = jnp.concatenate([_nn(pooled_b[g], wp_ref[g]) for g in range(4)], axis=1)
        ps = ps_ref[...]
        yp = z * ps
        sga = _sig(ga)
        sila = ga * sga
        sgp = _sig(gp)
        silp = gp * sgp
        br = jnp.concatenate([sila * attn, silp * yp], axis=1).astype(BF16)
        y = _nn(br, wo_ref[...])
        gate = gate_ref[...]
        err = x_ref[...] + gate * y - tgt_ref[...]
        loss_ref[...] += _colsum(_rowsum(err * err)) * (0.5 / D)
        dxn = err * (1.0 / D)
        dxn_ref[...] = dxn
        dgate_ref[...] += _colsum(dxn * y)
        dy = (dxn * gate).astype(BF16)
        dwo_ref[...] += _tn(br, dy)
        dbr = _nt(dy, wo_ref[...])
        dbra = dbr[:, :512]
        dbrp = dbr[:, 512:]
        dattn = dbra * sila
        for jj in range(jb):
            dattn_ref[:, jj, :] = dattn[jj * nqb:(jj + 1) * nqb]
        dga_ref[...] = (dbra * attn * (sga * (1.0 + ga * (1.0 - sga)))).astype(BF16)
        dgp_ref[...] = (dbrp * yp * (sgp * (1.0 + gp * (1.0 - sgp)))).astype(BF16)
        dyp = dbrp * silp
        dps_ref[...] += _colsum(dyp * z)
        dz = (dyp * ps).astype(BF16)
        dpool = []
        for g in range(4):
            dzg = dz[:, g * 128:(g + 1) * 128]
            dwp_ref[g] += _tn(pooled_b[g], dzg)
            dpool.append(_nt(dzg, wp_ref[g]))
        dpool_ref[...] = jnp.concatenate(dpool, axis=1)

    lat = lambda w: pl.BlockSpec((TB, w), lambda i: (i, 0))
    perm = pl.BlockSpec((nqb, jb, 512), lambda i: (0, i, 0))
    ucol = lambda j: pl.BlockSpec((TB, 512), lambda i: (i + off, j))
    last8 = t_all // HALO - 1
    return pl.pallas_call(
        body, name="out_stage", grid=(nq,),
        in_specs=[perm, ucol(1), ucol(2),
                  pl.BlockSpec((HALO, 512), lambda i: ((i + off) * hb - 1, 2)),
                  pl.BlockSpec((HALO, 512), lambda i: (jnp.minimum((i + off + 1) * hb, last8), 2)),
                  ucol(3), lat(D), lat(D), _full((1, D)), _full((4, 128, 128)), _full((1, 512)), _full((D, D))],
        out_specs=[lat(D), perm, lat(512), lat(512), lat(512),
                   _full((D, D)), _full((1, D)), _full((1, 512)), _full((4, 128, 128)), _full((1, 1))],
        out_shape=[jax.ShapeDtypeStruct((s_len, D), F32), jax.ShapeDtypeStruct((nqb, Q_BLOCK, 512), F32),
                   jax.ShapeDtypeStruct((s_len, 512), BF16), jax.ShapeDtypeStruct((s_len, 512), BF16),
                   jax.ShapeDtypeStruct((s_len, 512), F32),
                   jax.ShapeDtypeStruct((D, D), F32), jax.ShapeDtypeStruct((1, D), F32), jax.ShapeDtypeStruct((1, 512), F32),
                   jax.ShapeDtypeStruct((4, 128, 128), F32), jax.ShapeDtypeStruct((1, 1), F32)],
        compiler_params=_params(("arbitrary",)),
    )(attn, u, u, u, u, u, x, target, gate, w_pool, pool_scale, w_out)


def _attn_bwd(q, k, v, dattn, attn, lse, s_len):
    t_all = q.shape[1]
    off = (t_all - s_len) // TB
    nq = s_len // TB
    nch = 4
    chunks = [(c * (t_all // nch), t_all // nch) for c in range(nch)]
    nsub = next(n for n in (BWD_QBLOCKS, 2, 1) if nq % n == 0)
    tq = nsub * TB

    def body(*refs):
        q_refs = refs[:nsub]
        k_ref, v_ref, do_ref, o_ref, lse_ref, dq_ref, dk_ref, dv_ref = refs[nsub:]
        i = pl.program_id(1)

        @pl.when(i == 0)
        def _():
            dk_ref[...] = jnp.zeros_like(dk_ref)
            dv_ref[...] = jnp.zeros_like(dv_ref)

        qb = jnp.concatenate([r[0] for r in q_refs], axis=0)
        delta_r = _row_layout(_rowsum(do_ref[...] * o_ref[...]))[0:1, :]
        do = do_ref[...].astype(BF16)
        lse_r = jnp.concatenate([lse_ref[0, sb][0:1, :] for sb in range(nsub)], axis=1)
        dq = jnp.zeros((tq, DKP), F32)
        for start, size in chunks:
            rows = pl.ds(start, size)
            kc = k_ref[0, rows, :]
            p_t = jnp.exp2(_nt(kc, qb) - lse_r)
            ds_t = (p_t * (_nt(v_ref[0, rows, :], do) - delta_r)).astype(BF16)
            dv_ref[0, rows, :] += _nn(p_t.astype(BF16), do)
            dk_ref[0, rows, :] += _nn(ds_t, qb)
            dq += _tn(ds_t, kc)
        dq_ref[0] = dq * SCALE

    kvspec = lambda w: pl.BlockSpec((1, t_all, w), lambda h, i: (h, 0, 0))
    rowspec = pl.BlockSpec((1, nsub, 8, TB), lambda h, i: (h, i, 0, 0))
    qspec = lambda sb: pl.BlockSpec((1, TB, DKP), lambda h, i: (h, i * nsub + sb + off, 0))
    return pl.pallas_call(
        body, name="attn_bwd", grid=(NH, nq // nsub),
        in_specs=[qspec(sb) for sb in range(nsub)]
        + [kvspec(DKP), kvspec(DV), pl.BlockSpec((tq, DV), lambda h, i: (i, h)), pl.BlockSpec((tq, DV), lambda h, i: (i, h)),
           rowspec],
        out_specs=[pl.BlockSpec((1, tq, DKP), lambda h, i: (h, i, 0)), kvspec(DKP), kvspec(DV)],
        out_shape=[jax.ShapeDtypeStruct((NH, s_len, DKP), F32), jax.ShapeDtypeStruct((NH, t_all, DKP), F32),
                   jax.ShapeDtypeStruct((NH, t_all, DV), F32)],
        compiler_params=_params(("arbitrary", "arbitrary")),
    )(*([q] * nsub), k, v, dattn, attn, lse)


def _qkv_bwd(u, dq, dk, dv, cos, sin, q_lora_g, w_uq_t, kv_lora_g, w_ukv, qn_g, kn_g, s_len):
    t_all = u.shape[0]
    off = (t_all - s_len) // TB
    nb = t_all // TB

    def body(ulo_ref, dq_ref, dk_ref, dv_ref, cos_ref, sin_ref, qlg_ref, wuq_ref, kvlg_ref, wukv_ref, qng_ref, kng_ref,
             dlo_ref, dwuq_ref, dwukv_ref, dqlg_ref, dkvlg_ref, dqng_ref, dkng_ref):
        i = pl.program_id(0)

        @pl.when(i == 0)
        def _():
            for r in (dwuq_ref, dwukv_ref, dqlg_ref, dkvlg_ref, dqng_ref, dkng_ref):
                r[...] = jnp.zeros_like(r)

        latent = i >= off
        ulo = ulo_ref[...]
        cos, sin = _rope_block(cos_ref, sin_ref, pl.program_id(0) < off)
        cq = ulo[:, 0:QL]
        rc = lax.rsqrt(jnp.mean(cq * cq, axis=-1, keepdims=True) + EPS)
        cqh = cq * rc
        qlg = qlg_ref[...]
        cqn_b = (cqh * qlg).astype(BF16)
        qng = qng_ref[...]
        ckv = ulo[:, QL:QL + KVL]
        r0 = lax.rsqrt(jnp.mean(ckv * ckv, axis=-1, keepdims=True) + EPS)
        ckvh = ckv * r0
        kvlg = kvlg_ref[...]
        ckvn_b = (ckvh * kvlg).astype(BF16)
        qhs = [_nt(cqn_b, wuq_ref[hd]) for hd in range(NH)]
        kns = [_nn(ckvn_b, wukv_ref[hd])[:, :128] for hd in range(NH)]
        dqng = jnp.zeros((1, DKP), F32)
        dqraws = []
        for hd in range(NH):
            qh = qhs[hd]
            rq = lax.rsqrt(_rowsum(qh * qh) / DK + EPS)
            xh = qh * rq
            dqh = jnp.where(latent, dq_ref[hd], 0.0)
            dyq = jnp.concatenate([dqh[:, :128], _rope_t(dqh[:, 128:], cos, sin)], axis=1)
            dqng += _colsum(dyq * xh)
            dxh = dyq * qng
            dqraws.append((rq * (dxh - xh * (_rowsum(dxh * xh) / DK))).astype(BF16))
        dqng_ref[...] += dqng

        kr = ulo[:, 384:512]
        skr = _rowsum(kr * kr)
        kng = kng_ref[...]
        dkr = jnp.zeros((TB, 128), F32)
        dkng = jnp.zeros((1, DKP), F32)
        dkvs = []
        for hd in range(NH):
            kn = kns[hd]
            rk = lax.rsqrt((_rowsum(kn * kn) + skr) / DK + EPS)
            xh1 = kn * rk
            xh2 = kr * rk
            dkh = dk_ref[hd] * LN2
            d1 = dkh[:, :128]
            d2 = _rope_t(dkh[:, 128:], cos, sin)
            dkng += jnp.concatenate([_colsum(d1 * xh1), _colsum(d2 * xh2)], axis=1)
            dx1 = d1 * kng[:, :128]
            dx2 = d2 * kng[:, 128:]
            dot = (_rowsum(dx1 * xh1) + _rowsum(dx2 * xh2)) / DK
            dkvs.append(jnp.concatenate([rk * (dx1 - xh1 * dot), dv_ref[hd]], axis=1).astype(BF16))
            dkr += rk * (dx2 - xh2 * dot)
        dkng_ref[...] += dkng

        dcqn = jnp.zeros((TB, QL), F32)
        dckvn = jnp.zeros((TB, KVL), F32)
        for hd in range(NH):
            dwuq_ref[hd] += _tn(dqraws[hd], cqn_b)[:DK]
            dcqn += _nn(dqraws[hd], wuq_ref[hd])
            dwukv_ref[hd] += _tn(ckvn_b, dkvs[hd])
            dckvn += _nt(dkvs[hd], wukv_ref[hd])
        dqlg_ref[...] += _colsum(dcqn * cqh)
        dxh = dcqn * qlg
        dcq = rc * (dxh - cqh * jnp.mean(dxh * cqh, axis=-1, keepdims=True))
        dkvlg_ref[...] += _colsum(dckvn * ckvh)
        dxh = dckvn * kvlg
        dckv = r0 * (dxh - ckvh * jnp.mean(dxh * ckvh, axis=-1, keepdims=True))
        dlo_ref[...] = jnp.concatenate([dcq, dckv, dkr], axis=1).astype(BF16)

    row = lambda w: pl.BlockSpec((TB, w), lambda i: (i, 0))
    heads = lambda w: pl.BlockSpec((NH, TB, w), lambda i: (0, i, 0))
    return pl.pallas_call(
        body, name="qkv_bwd", grid=(nb,),
        in_specs=[row(512), pl.BlockSpec((NH, TB, DKP), lambda i: (0, jnp.maximum(i - off, 0), 0)), heads(DKP), heads(DV),
                  pl.BlockSpec((1, 8, 256), lambda i: (jnp.maximum(i - off, 0), 0, 0)), _full((TB, 256)), _full((1, QL)), _full((NH, DKP, QL)), _full((1, KVL)), _full((NH, KVL, 256)),
                  _full((1, DKP)), _full((1, DKP))],
        out_specs=[row(512), _full((NH, DK, QL)), _full((NH, KVL, 256)), _full((1, QL)), _full((1, KVL)),
                   _full((1, DKP)), _full((1, DKP))],
        out_shape=[jax.ShapeDtypeStruct((t_all, 512), BF16), jax.ShapeDtypeStruct((NH, DK, QL), F32),
                   jax.ShapeDtypeStruct((NH, KVL, 256), F32), jax.ShapeDtypeStruct((1, QL), F32),
                   jax.ShapeDtypeStruct((1, KVL), F32), jax.ShapeDtypeStruct((1, DKP), F32), jax.ShapeDtypeStruct((1, DKP), F32)],
        compiler_params=_params(("arbitrary",)),
    )(u, dq, dk, dv, cos, sin, q_lora_g, w_uq_t, kv_lora_g, w_ukv, qn_g, kn_g)


def _in_bwd(ctx, x, modsel, norm_g, dlo, dga, dgp, dpool, dxn, w_in_t):
    s_len, lc = x.shape[0], ctx.shape[0]
    t_all = s_len + lc
    off = lc // TB
    nb = t_all // TB
    nq = s_len // TB
    hb = TB // HALO
    n = TB + 2 * HALO

    def body(ctx_ref, x_ref, mod_ref, ng_ref, dlo_ref, dga_ref, dgp_ref, dp_ref, dpprev_ref, dpnext_ref, dxn_ref, win_ref,
             gx_ref, dwin_ref, dmod_ref, dng_ref):
        i = pl.program_id(0)
        j = i - off

        @pl.when(i == 0)
        def _():
            dwin_ref[...] = jnp.zeros_like(dwin_ref)
            dmod_ref[...] = jnp.zeros_like(dmod_ref)
            dng_ref[...] = jnp.zeros_like(dng_ref)

        latent = i >= off
        dp = dp_ref[...]
        prev = jnp.where(j <= 0, 0.0, dpprev_ref[...])
        nxt = jnp.where(j >= nq - 1, 0.0, dpnext_ref[...])
        win = jnp.concatenate([prev, dp, nxt], axis=0)
        tg = j * TB - HALO + lax.broadcasted_iota(jnp.int32, (n, 1), 0)
        dpin = []
        for g, w in enumerate(POOL_WINDOWS):
            cnt = jnp.maximum(jnp.minimum(tg + w // 2, s_len) - jnp.maximum(tg - w // 2, 0), 1).astype(F32)
            zq = win[:, g * 128:(g + 1) * 128] / cnt
            zq = zq + _shift_rows(zq, 1)
            for step in (1, 2, 4):
                if w >= 4 * step:
                    zq = _shift_rows(zq, -step) + _shift_rows(zq, step)
            dpin.append(zq[HALO:HALO + TB] - dp[:, g * 128:(g + 1) * 128])
        zero = jnp.zeros((TB, 512), BF16)
        du = [dlo_ref[...], jnp.where(latent, dga_ref[...], zero),
              jnp.where(latent, jnp.concatenate(dpin, axis=1).astype(BF16), zero), jnp.where(latent, dgp_ref[...], zero)]

        ng = ng_ref[...]
        xb = jnp.where(i < off, ctx_ref[...], x_ref[...])
        r, xh, xg, h, scale = _modulated(xb, mod_ref, ng)
        hb_ = h.astype(BF16)
        dh = jnp.zeros((TB, D), F32)
        for s, (lo, hi) in enumerate(SEG):
            dwin_ref[lo:hi, :] += _tn(du[s], hb_)
            dh += _nn(du[s], win_ref[lo:hi, :])
        is_lat = latent.astype(F32)
        dsh = _colsum(dh)
        dsc = _colsum(dh * xg)
        dmod_ref[0, 0:1, :] += dsh * (1.0 - is_lat)
        dmod_ref[0, 1:2, :] += dsc * (1.0 - is_lat)
        dmod_ref[1, 0:1, :] += dsh * is_lat
        dmod_ref[1, 1:2, :] += dsc * is_lat
        dxg = dh * (1.0 + scale)
        dng_ref[...] += _colsum(dxg * xh)
        dxh = dxg * ng
        gx_ref[...] = r * (dxh - xh * jnp.mean(dxh * xh, axis=-1, keepdims=True)) + dxn_ref[...]

    row = lambda w: pl.BlockSpec((TB, w), lambda i: (i, 0))
    lat = lambda w: pl.BlockSpec((TB, w), lambda i: (jnp.maximum(i - off, 0), 0))
    last8 = s_len // HALO - 1
    cspec, xspec, mspec = _token_specs(off)
    return pl.pallas_call(
        body, name="in_bwd", grid=(nb,),
        in_specs=[cspec, xspec, mspec, _full((1, D)), row(512), lat(512), lat(512), lat(512),
                  pl.BlockSpec((HALO, 512), lambda i: (jnp.maximum(jnp.maximum(i - off, 0) * hb - 1, 0), 0)),
                  pl.BlockSpec((HALO, 512), lambda i: (jnp.minimum((jnp.maximum(i - off, 0) + 1) * hb, last8), 0)),
                  lat(D), _full((DIN, D))],
        out_specs=[lat(D), _full((DIN, D)), _full((2, 2, D)), _full((1, D))],
        out_shape=[jax.ShapeDtypeStruct((s_len, D), F32), jax.ShapeDtypeStruct((DIN, D), F32),
                   jax.ShapeDtypeStruct((2, 2, D), F32), jax.ShapeDtypeStruct((1, D), F32)],
        compiler_params=_params(("arbitrary",)),
    )(ctx, x, modsel, norm_g, dlo, dga, dgp, dpool, dpool, dpool, dxn, w_in_t)


def _adamw_update(w_ref, g_ref, m_ref, v_ref, d_ref, mo_ref, vo_ref):
    gv = g_ref[...]
    mn = ADAM_B1 * m_ref[...] + (1.0 - ADAM_B1) * gv
    vn = ADAM_B2 * v_ref[...] + (1.0 - ADAM_B2) * (gv * gv)
    m_hat = mn / (1.0 - ADAM_B1 ** ADAM_STEP)
    v_hat = vn / (1.0 - ADAM_B2 ** ADAM_STEP)
    d_ref[...] = -ADAM_LR * (m_hat / (jnp.sqrt(v_hat) + ADAM_EPS) + ADAM_WD * w_ref[...])
    mo_ref[...] = mn
    vo_ref[...] = vn


def _adamw_many(ws, gs, ms, vs):
    n = len(ws)
    parts = 4

    def body(*refs):
        for i in range(n):
            _adamw_update(refs[i], refs[n + i], refs[2 * n + i], refs[3 * n + i], refs[4 * n + i], refs[5 * n + i], refs[6 * n + i])
            refs[7 * n + i][...] = refs[n + i][...]

    def spec(w):
        rows, cols = w.shape
        if rows % (8 * parts) == 0:
            return pl.BlockSpec((rows // parts, cols), lambda i: (i, 0))
        if cols % (128 * parts) == 0:
            return pl.BlockSpec((rows, cols // parts), lambda i: (0, i))
        return _full((rows, cols))

    specs = [spec(w) for w in ws]
    shp = [jax.ShapeDtypeStruct(w.shape, F32) for w in ws]
    out = pl.pallas_call(body, name="adamw_many", grid=(parts,), in_specs=specs * 4, out_specs=specs * 4, out_shape=shp * 4,
                         compiler_params=_params(("arbitrary",)))(*ws, *gs, *ms, *vs)
    return out[:n], out[n:2 * n], out[2 * n:3 * n], out[3 * n:]


class _Links:
    def __init__(self, send_sems, recv_sems):
        self.send_sems, self.recv_sems, self.sends = send_sems, recv_sems, []

    def send(self, src, dst, sem, to):
        cp = pltpu.make_async_remote_copy(src, dst, self.send_sems.at[sem], self.recv_sems.at[sem], device_id=to,
                                          device_id_type=MESH)
        cp.start()
        self.sends.append(cp)

    def arrived(self, dst, sem, frm):
        pltpu.make_async_remote_copy(dst, dst, self.send_sems.at[sem], self.recv_sems.at[sem], device_id=frm,
                                     device_id_type=MESH).wait_recv()

    def drain(self):
        for cp in self.sends:
            cp.wait_send()


def _half(ref, c, axis):
    size = ref.shape[axis - 2] // 2
    win = pl.ds(pl.multiple_of(c * size, 16 if axis == 0 else 128), size)
    idx = (win, slice(None)) if axis == 0 else (slice(None), win)
    return ref.at[(slice(None),) * (len(ref.shape) - 2) + idx]


def _select_rows(slots_ref, n_slots, row=0):
    sub = lax.broadcasted_iota(jnp.int32, (8, 1), 0)
    out = None
    for d in range(n_slots):
        r = jnp.where(sub == d, jnp.broadcast_to(slots_ref[d][row:row + 1, :], (8, slots_ref.shape[-1])), 0.0)
        out = r if out is None else out + r
    return out


def _gather(c, c_ctx, w_mod, b_mod_k, shards, axes, slab_rows):
    nw = len(shards)
    kw = w_mod.shape[1]

    def body(*refs):
        c_ref, cc_ref, wm_ref, b_ref = refs[:4]
        w_refs = refs[4:4 + nw]
        a16_ref, mod_ref = refs[4 + nw:6 + nw]
        g_refs = refs[6 + nw:6 + 2 * nw]
        a_ref, send_sems, recv_sems = refs[6 + 2 * nw:]

        def slab(wi, chip):
            return g_refs[wi].at[chip].at[0:shards[wi].shape[0]]
        x, y, cc = lax.axis_index("x"), lax.axis_index("y"), lax.axis_index("c")
        me = 4 * x + 2 * y + cc
        k = 2 * x + y
        sibling = (x, y, 1 - cc)
        links = _Links(send_sems, recv_sems)
        chips = [_peer(x, y, cc, off + (0,)) for off in CHIPS3]
        chip_a = ((x + 1 - cc) % 2, (y + cc) % 2, cc)
        chip_b = ((x + cc) % 2, (y + 1 - cc) % 2, cc)
        chip_d = (1 - x, 1 - y, cc)
        cv = c_ref[...]
        a_ref[me] = jnp.broadcast_to(cv * _sig(cv), (8, D))
        for j, off in enumerate(PEERS7):
            links.send(a_ref.at[me], a_ref.at[me], j, _peer(x, y, cc, off))
        for wi in range(nw):
            slab(wi, k)[...] = w_refs[wi][...].astype(BF16)
            for j, to in enumerate((chip_a, chip_b)):
                links.send(_half(slab(wi, k), cc, axes[wi]), _half(slab(wi, k), cc, axes[wi]), 10 + wi * 6 + j, to)
            pad = slab_rows[wi] - shards[wi].shape[0]
            if pad:
                for kk in range(4):
                    g_refs[wi][kk, shards[wi].shape[0]:, :] = jnp.zeros((pad, shards[wi].shape[1]), BF16)
        for j, off in enumerate(PEERS7):
            px, py, pc = _peer(x, y, cc, off)
            links.arrived(a_ref.at[4 * px + 2 * py + pc], j, (px, py, pc))
        ccv = cc_ref[...]
        sub = lax.broadcasted_iota(jnp.int32, (8, 1), 0)
        a16 = jnp.concatenate([_select_rows(a_ref, 8), jnp.where(sub == 0, jnp.broadcast_to(ccv * _sig(ccv), (8, D)), 0.0)], axis=0)
        a16_ref[...] = a16
        mod_ref[k] = _dot3(_nn, a16, wm_ref[...]) + b_ref[...]
        for j, to in enumerate(chips):
            links.send(mod_ref.at[k], mod_ref.at[k], 7 + j, to)
        for j, (frm, origin) in enumerate(((chip_a, chip_a), (chip_b, chip_b), (chip_b, chip_d))):
            for wi in range(nw):
                blk = _half(slab(wi, 2 * origin[0] + origin[1]), cc, axes[wi])
                links.arrived(blk, 10 + wi * 6 + j, frm)
                if j == 0:
                    links.send(blk, blk, 10 + wi * 6 + 2, chip_b)
                links.send(blk, blk, 10 + wi * 6 + 3 + j, sibling)
        for j, (px, py, pc) in enumerate(chips):
            links.arrived(mod_ref.at[2 * px + py], 7 + j, (px, py, pc))
        for j, origin in enumerate((chip_b, chip_a, chip_d)):
            for wi in range(nw):
                links.arrived(_half(slab(wi, 2 * origin[0] + origin[1]), 1 - cc, axes[wi]), 10 + wi * 6 + 3 + j, sibling)
        links.drain()

    nsem = 10 + 6 * nw
    return pl.pallas_call(
        body, name="gather", in_specs=[VM] * (4 + nw), out_specs=[VM] * (2 + nw),
        out_shape=[jax.ShapeDtypeStruct((16, D), F32), jax.ShapeDtypeStruct((4, 16, kw), F32)]
        + [jax.ShapeDtypeStruct((4, r, s.shape[1]), BF16) for r, s in zip(slab_rows, shards)],
        scratch_shapes=[pltpu.VMEM((8, 8, D), F32), pltpu.SemaphoreType.DMA((nsem,)), pltpu.SemaphoreType.DMA((nsem,))],
        compiler_params=pltpu.CompilerParams(vmem_limit_bytes=VMEM_LIMIT),
    )(c, c_ctx, w_mod, b_mod_k, *shards)


SMALL_ROW_WIDTHS = (D, QL, KVL, DKP, DKP, 512, 128)
SMALL_OUT_WIDTHS = (D, QL, KVL, DK, DK, 512, 1)


def _reduce(grads, axes, smalls, w_pool_g, dmod8, a16, w_mod, c_ctx):
    nw = len(grads)
    ns = len(smalls)
    kw = w_mod.shape[1]
    halves = []
    for g, ax in zip(grads, axes):
        halves.append((g.shape[1] // 2, g.shape[2]) if ax == 0 else (g.shape[1], g.shape[2] // 2))

    def body(*refs):
        g_refs = refs[:nw]
        small_refs = refs[nw:nw + ns]
        wp_ref, dm_ref, a16_ref, wm_ref, cc_ref = refs[nw + ns:nw + ns + 5]
        o = nw + ns + 5
        r_refs = refs[o:o + nw]
        small_outs = refs[o + nw:o + nw + ns]
        rwp_ref, gw_ref, gb_ref, gc_ref = refs[o + nw + ns:o + nw + ns + 4]
        o = o + nw + ns + 4
        own, sib, part, got, rel = (refs[o + i * nw:o + (i + 1) * nw] for i in range(5))
        smbuf, wps, wpg, dm_all, pc_all, send_sems, recv_sems, local_sems = refs[o + 5 * nw:]
        x, y, cc = lax.axis_index("x"), lax.axis_index("y"), lax.axis_index("c")
        me = 4 * x + 2 * y + cc
        k = 2 * x + y
        sibling = (x, y, 1 - cc)
        links = _Links(send_sems, recv_sems)
        chips = [_peer(x, y, cc, off + (0,)) for off in CHIPS3]
        peers = [_peer(x, y, cc, off) for off in PEERS7]
        chip_a = ((x + 1 - cc) % 2, (y + cc) % 2, cc)
        chip_b = ((x + cc) % 2, (y + 1 - cc) % 2, cc)
        ka, kb, kd = 2 * chip_a[0] + chip_a[1], 2 * chip_b[0] + chip_b[1], 2 * (1 - x) + (1 - y)
        big, sm0, wp0, dm0, pc0 = 0, 5 * nw, 5 * nw + 7, 5 * nw + 14, 5 * nw + 21

        locals_ = []
        for wi in range(nw):
            lc = pltpu.make_async_copy(_half(g_refs[wi], cc, axes[wi]), own[wi], local_sems.at[wi])
            lc.start()
            locals_.append(lc)
            links.send(_half(g_refs[wi], 1 - cc, axes[wi]), sib[wi], big + wi * 5, sibling)
        slot = smbuf.at[me]
        slot[...] = jnp.zeros((8, D), F32)
        for r, (ref, w) in enumerate(zip(small_refs, SMALL_ROW_WIDTHS)):
            slot[r:r + 1, 0:w] = jnp.broadcast_to(ref[...], (1, w))
        links.send(wp_ref, wps, wp0, sibling)
        dm_all[me] = dm_ref[...]
        for j, peer in enumerate(peers):
            links.send(dm_all.at[me], dm_all.at[me], dm0 + j, peer)
            links.send(smbuf.at[me], smbuf.at[me], sm0 + j, peer)
        links.arrived(wps, wp0, sibling)
        wpg[k] = (wp_ref[...] + wps[...]).astype(BF16)
        for j, to in enumerate(chips):
            links.send(wpg.at[k], wpg.at[k], wp0 + 1 + j, to)
        for wi in range(nw):
            locals_[wi].wait()
            links.arrived(sib[wi], big + wi * 5, sibling)
            part[wi][...] = (own[wi][...] + sib[wi][...]).astype(BF16)
            got[wi][k] = part[wi][k]
            got[wi][kd] = jnp.zeros(halves[wi], BF16)
            links.send(part[wi].at[kd], rel[wi], big + wi * 5 + 1, chip_b)
            links.send(part[wi].at[kb], got[wi].at[k], big + wi * 5 + 2, chip_b)
        for j, (px, py, pc) in enumerate(peers):
            links.arrived(dm_all.at[4 * px + 2 * py + pc], dm0 + j, (px, py, pc))
        dmc = dm_all[0][1:2, :]
        dml = dm_all[0][0:1, :]
        for d in range(1, 8):
            dmc = dmc + dm_all[d][1:2, :]
            dml = dml + dm_all[d][0:1, :]
        gb_ref[...] = dml + dmc
        sub = lax.broadcasted_iota(jnp.int32, (8, 1), 0)
        b16 = jnp.concatenate([_select_rows(dm_all, 8), jnp.where(sub == 0, jnp.broadcast_to(dmc, (8, 3 * D)), 0.0)], axis=0)
        bk = jnp.zeros((16, kw), F32)
        for kk in range(4):
            bk = bk + jnp.where(k == kk, b16[:, kk * kw:(kk + 1) * kw], 0.0)
        gw_ref[...] = _dot3(_tn, a16_ref[...], bk)
        pc_all[k] = _dot3(_nt, jnp.broadcast_to(bk[8:9, :], (8, kw)), wm_ref[...])
        for j, to in enumerate(chips):
            links.send(pc_all.at[k], pc_all.at[k], pc0 + j, to)
        for j, (px, py, pc) in enumerate(peers):
            links.arrived(smbuf.at[4 * px + 2 * py + pc], sm0 + j, (px, py, pc))
        tot = smbuf[0]
        for d in range(1, 8):
            tot = tot + smbuf[d]
        for r, (ref, w) in enumerate(zip(small_outs, SMALL_OUT_WIDTHS)):
            ref[...] = tot[r:r + 1, 0:w]
        for j, (px, py, pc) in enumerate(chips):
            links.arrived(wpg.at[2 * px + py], wp0 + 1 + j, (px, py, pc))
        wpt = wpg[0].astype(F32)
        for kk in range(1, 4):
            wpt = wpt + wpg[kk].astype(F32)
        rwp_ref[...] = wpt
        for wi in range(nw):
            links.arrived(rel[wi], big + wi * 5 + 1, chip_b)
            rel[wi][...] = (part[wi][ka].astype(F32) + rel[wi][...].astype(F32)).astype(BF16)
            links.send(rel[wi], got[wi].at[k], big + wi * 5 + 3, chip_a)
        for wi in range(nw):
            links.arrived(got[wi].at[kb], big + wi * 5 + 2, chip_b)
            links.arrived(got[wi].at[ka], big + wi * 5 + 3, chip_a)
            total = got[wi][0].astype(F32)
            for kk in range(1, 4):
                total = total + got[wi][kk].astype(F32)
            mine = _half(r_refs[wi], cc, axes[wi])
            mine[...] = total
            links.send(mine, mine, big + wi * 5 + 4, sibling)
        for j, (px, py, pc) in enumerate(chips):
            links.arrived(pc_all.at[2 * px + py], pc0 + j, (px, py, pc))
        ccv = cc_ref[...]
        sg = _sig(ccv)
        gc_ref[...] = (pc_all[0][0:1, :] + pc_all[1][0:1, :] + pc_all[2][0:1, :] + pc_all[3][0:1, :]) * (sg * (1.0 + ccv * (1.0 - sg)))
        for wi in range(nw):
            links.arrived(_half(r_refs[wi], 1 - cc, axes[wi]), big + wi * 5 + 4, sibling)
        links.drain()

    nsem = 5 * nw + 24
    quads = [(4,) + h for h in halves]
    return pl.pallas_call(
        body, name="reduce", in_specs=[ANY] * nw + [VM] * (ns + 5), out_specs=[VM] * (nw + ns + 4),
        out_shape=[jax.ShapeDtypeStruct(g.shape[1:], F32) for g in grads]
        + [jax.ShapeDtypeStruct((1, w), F32) for w in SMALL_OUT_WIDTHS]
        + [jax.ShapeDtypeStruct(w_pool_g.shape, F32), jax.ShapeDtypeStruct((D, kw), F32), jax.ShapeDtypeStruct((1, 3 * D), F32),
           jax.ShapeDtypeStruct((1, D), F32)],
        scratch_shapes=[pltpu.VMEM(q, F32) for q in quads] + [pltpu.VMEM(q, F32) for q in quads]
        + [pltpu.VMEM(q, BF16) for q in quads] + [pltpu.VMEM(q, BF16) for q in quads] + [pltpu.VMEM(h, BF16) for h in halves]
        + [pltpu.VMEM((8, 8, D), F32), pltpu.VMEM(w_pool_g.shape, F32), pltpu.VMEM((4,) + w_pool_g.shape, BF16),
           pltpu.VMEM((8, 8, 3 * D), F32), pltpu.VMEM((4, 8, D), F32)]
        + [pltpu.SemaphoreType.DMA((nsem,)), pltpu.SemaphoreType.DMA((nsem,)), pltpu.SemaphoreType.DMA((nw,))],
        compiler_params=pltpu.CompilerParams(vmem_limit_bytes=VMEM_LIMIT),
    )(*grads, *smalls, w_pool_g, dmod8, a16, w_mod, c_ctx)


def _rope_tables(s_len):
    rows = s_len // GRID_W
    per = TB // GRID_W
    n_freq = 16
    inv = ROPE_BASE ** (-jnp.arange(n_freq, dtype=F32) / n_freq)
    ang_r = jnp.arange(rows, dtype=F32)[:, None] * inv
    ang_c = jnp.arange(GRID_W, dtype=F32)[:, None] * inv
    by_row, by_col = [], []
    for fn, pad in ((jnp.cos, 1.0), (jnp.sin, 0.0)):
        r = jnp.concatenate([fn(ang_r), fn(ang_r), jnp.zeros((rows, 96), F32)], axis=1).reshape(rows // per, per, 128)
        by_row.append(jnp.pad(r, ((0, 0), (0, 8 - per), (0, 0))))
        cpart = jnp.concatenate([jnp.zeros((GRID_W, 32), F32), fn(ang_c), fn(ang_c), jnp.full((GRID_W, 64), pad, F32)], axis=1)
        by_col.append(jnp.tile(cpart, (per, 1)))
    return jnp.concatenate(by_row, axis=-1), jnp.concatenate(by_col, axis=-1)


def kernel(x, c, ctx, c_ctx, w_mod, b_mod, norm_g, w_in, q_lora_g, w_uq, kv_lora_g, w_ukv, q_norm_g, k_norm_g, w_pool, pool_scale, w_out, loss_target, m_c_ctx, m_w_mod, m_b_mod, m_norm_g, m_w_in, m_q_lora_g, m_w_uq, m_kv_lora_g, m_w_ukv, m_q_norm_g, m_k_norm_g, m_w_pool, m_pool_scale, m_w_out, v_c_ctx, v_w_mod, v_b_mod, v_norm_g, v_w_in, v_q_lora_g, v_w_uq, v_kv_lora_g, v_w_ukv, v_q_norm_g, v_k_norm_g, v_w_pool, v_pool_scale, v_w_out):
    xi, yi, ci = lax.axis_index("x"), lax.axis_index("y"), lax.axis_index("c")
    me = 4 * xi + 2 * yi + ci
    k = 2 * xi + yi
    s_len = x.shape[1]
    lc = ctx.shape[1]
    kw = w_mod.shape[2]
    weights = dict(c_ctx=c_ctx, w_mod=w_mod, b_mod=b_mod, norm_g=norm_g, w_in=w_in, q_lora_g=q_lora_g, w_uq=w_uq,
                   kv_lora_g=kv_lora_g, w_ukv=w_ukv, q_norm_g=q_norm_g, k_norm_g=k_norm_g, w_pool=w_pool,
                   pool_scale=pool_scale, w_out=w_out)
    m_in = dict(c_ctx=m_c_ctx, w_mod=m_w_mod, b_mod=m_b_mod, norm_g=m_norm_g, w_in=m_w_in, q_lora_g=m_q_lora_g, w_uq=m_w_uq,
                kv_lora_g=m_kv_lora_g, w_ukv=m_w_ukv, q_norm_g=m_q_norm_g, k_norm_g=m_k_norm_g, w_pool=m_w_pool,
                pool_scale=m_pool_scale, w_out=m_w_out)
    v_in = dict(c_ctx=v_c_ctx, w_mod=v_w_mod, b_mod=v_b_mod, norm_g=v_norm_g, w_in=v_w_in, q_lora_g=v_q_lora_g, w_uq=v_w_uq,
                kv_lora_g=v_kv_lora_g, w_ukv=v_w_ukv, q_norm_g=v_q_norm_g, k_norm_g=v_k_norm_g, w_pool=v_w_pool,
                pool_scale=v_pool_scale, w_out=v_w_out)
    order = ["c_ctx", "w_mod", "b_mod", "norm_g", "w_in", "q_lora_g", "w_uq", "kv_lora_g", "w_ukv", "q_norm_g", "k_norm_g",
             "w_pool", "pool_scale", "w_out"]
    transposed = ("w_in", "w_uq")
    as2d = lambda n, a: jnp.transpose(a[0]) if n in transposed else a.reshape(-1, a.shape[-1])
    back = lambda n, a: jnp.transpose(a)[None] if n in transposed else a.reshape(weights[n].shape)

    c_ctx2 = c_ctx.reshape(1, D)
    b_mod_k = lax.dynamic_slice(b_mod, (0, k * kw), (1, kw))
    split = (1, 0, 0, 0)
    a16, mod_all, g_in, g_uq, g_ukv, g_out = _gather(
        c, c_ctx2, w_mod[0], b_mod_k, [as2d("w_in", w_in), as2d("w_uq", w_uq), w_ukv[0], w_out[0]], split,
        (DIN // 4, DKP, KVL, D // 4))
    mod_me = lax.dynamic_index_in_dim(mod_all, me, axis=1, keepdims=False).reshape(3, D)
    mod_c = mod_all[:, 8, :].reshape(3, D)
    modsel = jnp.stack([mod_c, mod_me])
    w_in_t = g_in.reshape(DIN, D)
    w_uq_t = g_uq
    w_out_f = g_out.reshape(D, D)
    qn_g = jnp.pad(q_norm_g, ((0, 0), (0, DKP - DK)))
    kn_g = jnp.pad(k_norm_g, ((0, 0), (0, DKP - DK)))
    w_pool_b = w_pool[0].astype(BF16)
    cos, sin = _rope_tables(s_len)

    u, q, kk, v = _fwd_in(ctx[0], x[0], modsel, norm_g, w_in_t, q_lora_g, w_uq_t, kv_lora_g, g_ukv, qn_g, kn_g, cos, sin)
    attn, lse = _attn_fwd(q, kk, v, s_len)
    (dxn, dattn, dga, dgp, dpool, dw_out, dgate, dps, dw_pool, loss) = _out_stage(
        attn.reshape(s_len // Q_BLOCK, Q_BLOCK, NH * DV), u, x[0], loss_target[0], modsel[1, 2:3, :], w_pool_b, pool_scale,
        w_out_f, lc)
    dattn = dattn.reshape(s_len, NH * DV)
    dq, dk, dv = _attn_bwd(q, kk, v, dattn, attn, lse, s_len)
    dlo, dw_uq_t, dw_ukv, dqlg, dkvlg, dqng, dkng = _qkv_bwd(u, dq, dk, dv, cos, sin, q_lora_g, w_uq_t, kv_lora_g, g_ukv,
                                                            qn_g, kn_g, s_len)
    gx, dw_in_t, dmod, dng = _in_bwd(ctx[0], x[0], modsel, norm_g, dlo, dga, dgp, dpool, dxn, w_in_t)

    dmod_l = jnp.concatenate([dmod[1, 0], dmod[1, 1], dgate[0]]).reshape(1, 3 * D)
    dmod_c = jnp.concatenate([dmod[0, 0], dmod[0, 1], jnp.zeros((D,), F32)]).reshape(1, 3 * D)
    dmod8 = jnp.concatenate([dmod_l, dmod_c, jnp.zeros((6, 3 * D), F32)], axis=0)
    r_in, r_uq, r_ukv, r_out, g_ng, g_qlg, g_kvlg, g_qng, g_kng, g_ps, loss_all, g_wp, g_w_mod, g_b_mod, g_c_ctx = _reduce(
        [dw_in_t.reshape(4, DIN // 4, D), dw_uq_t, dw_ukv, dw_out.reshape(4, D // 4, D)], split,
        [dng, dqlg, dkvlg, dqng, dkng, dps, loss], dw_pool, dmod8, a16, w_mod[0], c_ctx2)
    g2d = dict(c_ctx=g_c_ctx, b_mod=g_b_mod, w_mod=g_w_mod, w_in=r_in, w_uq=r_uq, w_ukv=r_ukv, w_out=r_out, norm_g=g_ng,
               q_lora_g=g_qlg, kv_lora_g=g_kvlg, q_norm_g=g_qng, k_norm_g=g_kng, pool_scale=g_ps, w_pool=g_wp.reshape(512, 128))

    outs = _adamw_many([as2d(n, weights[n]) for n in order], [g2d[n] for n in order], [as2d(n, m_in[n]) for n in order],
                       [as2d(n, v_in[n]) for n in order])
    d2d, m2d, v2d, g2d = (dict(zip(order, arrs)) for arrs in outs)

    return (loss_all[0, 0], gx[None], *[back(n, g2d[n]) for n in order], *[back(n, d2d[n]) for n in order],
            *[back(n, m2d[n]) for n in order], *[back(n, v2d[n]) for n in order])
```

```python
import jax
import jax.numpy as jnp
from jax import lax
from jax.experimental import pallas as pl
from jax.experimental.pallas import tpu as pltpu

F32 = jnp.float32
BF16 = jnp.bfloat16
MESH = pl.DeviceIdType.MESH

D = 1024
NH = 4
DK = 192
DKP = 256
DV = 128
QL = 256
KVL = 128
DIN = 1984
U_LO = 448
SEG = ((0, 512), (448, 960), (960, 1472), (1472, 1984))
DU = 2048
POOL_WINDOWS = (2, 4, 8, 16)
HALO = 8
EPS = 1e-6
ROPE_BASE = 10000.0
GRID_W = 64
Q_BLOCK = 128
TB = 256
BWD_QBLOCKS = 1
SCALE = DK ** -0.5
LOG2E = 1.4426950408889634
LN2 = 0.6931471805599453
VMEM_LIMIT = 56 * 1024 * 1024

ADAM_LR = 0.001
ADAM_B1 = 0.9
ADAM_B2 = 0.999
ADAM_EPS = 1e-08
ADAM_WD = 0.01
ADAM_STEP = 10

CHIPS3 = ((1, 0), (0, 1), (1, 1))
PEERS7 = tuple((dx, dy, dc) for dx in (0, 1) for dy in (0, 1) for dc in (0, 1) if (dx, dy, dc) != (0, 0, 0))

VM = pl.BlockSpec(memory_space=pltpu.VMEM)
ANY = pl.BlockSpec(memory_space=pl.ANY)


def _nn(a, b):
    return jnp.dot(a, b, preferred_element_type=F32)


def _nt(a, b):
    return lax.dot_general(a, b, (((1,), (1,)), ((), ())), preferred_element_type=F32)


def _tn(a, b):
    return lax.dot_general(a, b, (((0,), (0,)), ((), ())), preferred_element_type=F32)


def _split3(a):
    a0 = a.astype(BF16)
    r = a - a0.astype(F32)
    a1 = r.astype(BF16)
    a2 = (r - a1.astype(F32)).astype(BF16)
    return a0, a1, a2


def _dot3(dot, a, b):
    sa = _split3(a)
    sb = _split3(b)
    out = None
    for i in range(3):
        for j in range(3 - i):
            t = dot(sa[i], sb[j])
            out = t if out is None else out + t
    return out


def _sig(x):
    return 1.0 / (1.0 + jnp.exp(-x))


def _rot(t):
    src = lax.broadcasted_iota(jnp.int32, (128, 128), 0)
    dst = lax.broadcasted_iota(jnp.int32, (128, 128), 1)
    first = (dst % 32) < 16
    perm = jnp.where(first & (src == dst + 16), -1.0, jnp.where(~first & (src == dst - 16), 1.0, 0.0)).astype(BF16)
    hi = t.astype(BF16)
    lo = (t - hi.astype(F32)).astype(BF16)
    return _nn(hi, perm) + _nn(lo, perm)


def _rope(t, cos, sin):
    return t * cos + _rot(t) * sin


def _rope_t(t, cos, sin):
    return t * cos - _rot(t * sin)


def _rope_block(rows_ref, cols_ref, is_ctx):
    lane = lax.broadcasted_iota(jnp.int32, (TB, 256), 1) % 128
    rows = jnp.concatenate([jnp.broadcast_to(rows_ref[0, r:r + 1, :], (GRID_W, 256)) for r in range(TB // GRID_W)], axis=0)
    cs = jnp.where(lane < 32, rows, cols_ref[...])
    return jnp.where(is_ctx, 1.0, cs[:, :128]), jnp.where(is_ctx, 0.0, cs[:, 128:])


def _shift_rows(z, k):
    n = z.shape[0]
    return pltpu.roll(z, (n - k) % n, 0)


def _colsum(a):
    return jnp.sum(a, axis=0, keepdims=True)


def _rowsum(a):
    return jnp.sum(a, axis=-1, keepdims=True)


def _row_layout(col):
    return jnp.transpose(jnp.broadcast_to(col, (col.shape[0], 128)))[0:8, :]


def _params(sem=None):
    return pltpu.CompilerParams(dimension_semantics=sem, vmem_limit_bytes=VMEM_LIMIT)


def _full(shape):
    nd = len(shape)
    return pl.BlockSpec(shape, lambda *_: (0,) * nd)


def _peer(x, y, c, off):
    dx, dy, dc = off
    return ((x + dx) % 2, (y + dy) % 2, (c + dc) % 2)


def _token_specs(off):
    ctx = pl.BlockSpec((TB, D), lambda i: (jnp.minimum(i, off - 1), 0))
    lat = pl.BlockSpec((TB, D), lambda i: (jnp.maximum(i - off, 0), 0))
    mod = pl.BlockSpec((1, 3, D), lambda i: (jnp.minimum(i // off, 1), 0, 0))
    return ctx, lat, mod


def _modulated(x, mod_ref, ng):
    shift = mod_ref[0, 0:1, :]
    scale = mod_ref[0, 1:2, :]
    r = lax.rsqrt(jnp.mean(x * x, axis=-1, keepdims=True) + EPS)
    xh = x * r
    xg = xh * ng
    return r, xh, xg, xg * (1.0 + scale) + shift, scale


def _fwd_in(ctx, x, modsel, norm_g, w_in_t, q_lora_g, w_uq_t, kv_lora_g, w_ukv, qn_g, kn_g, cos, sin):
    s_len, lc = x.shape[0], ctx.shape[0]
    t_all = s_len + lc
    nb = t_all // TB
    off = lc // TB

    def body(ctx_ref, x_ref, mod_ref, ng_ref, win_ref, qlg_ref, wuq_ref, kvlg_ref, wukv_ref, qng_ref, kng_ref, cos_ref, sin_ref,
             u_ref, q_ref, k_ref, v_ref):
        is_ctx = pl.program_id(0) < off
        xb = jnp.where(is_ctx, ctx_ref[...], x_ref[...])
        _, _, _, h, _ = _modulated(xb, mod_ref, ng_ref[...])
        hb = h.astype(BF16)
        lane = lax.broadcasted_iota(jnp.int32, (TB, 512), 1)
        ulo = jnp.where(lane < U_LO, _nt(hb, win_ref[SEG[0][0]:SEG[0][1], :]), 0.0)
        u_ref[:, 0:512] = ulo
        for j in range(1, 4):
            u_ref[:, j * 512:(j + 1) * 512] = _nt(hb, win_ref[SEG[j][0]:SEG[j][1], :])
        cos, sin = _rope_block(cos_ref, sin_ref, is_ctx)
        cq = ulo[:, 0:QL]
        cqn = (cq * lax.rsqrt(jnp.mean(cq * cq, axis=-1, keepdims=True) + EPS) * qlg_ref[...]).astype(BF16)
        qng = qng_ref[...]
        ckv = ulo[:, QL:QL + KVL]
        ckvn = (ckv * lax.rsqrt(jnp.mean(ckv * ckv, axis=-1, keepdims=True) + EPS) * kvlg_ref[...]).astype(BF16)
        qhs = [_nt(cqn, wuq_ref[hd]) for hd in range(NH)]
        kvs = [_nn(ckvn, wukv_ref[hd]) for hd in range(NH)]
        for hd in range(NH):
            qh = qhs[hd]
            qn = qh * lax.rsqrt(_rowsum(qh * qh) / DK + EPS) * qng
            q_ref[hd] = (jnp.concatenate([qn[:, :128], _rope(qn[:, 128:], cos, sin)], axis=1) * (SCALE * LOG2E)).astype(BF16)
        kr = ulo[:, 384:512]
        skr = _rowsum(kr * kr)
        kng = kng_ref[...]
        kr_roped = _rope(kr * kng[:, 128:], cos, sin)
        for hd in range(NH):
            kv = kvs[hd]
            kn = kv[:, :128]
            rk = lax.rsqrt((_rowsum(kn * kn) + skr) / DK + EPS)
            k_ref[hd] = jnp.concatenate([kn * rk * kng[:, :128], kr_roped * rk], axis=1).astype(BF16)
            v_ref[hd] = kv[:, 128:].astype(BF16)

    row = lambda w: pl.BlockSpec((TB, w), lambda i: (i, 0))
    heads = lambda w: pl.BlockSpec((NH, TB, w), lambda i: (0, i, 0))
    cspec, xspec, mspec = _token_specs(off)
    return pl.pallas_call(
        body, name="fwd_in", grid=(nb,),
        in_specs=[cspec, xspec, mspec, _full((1, D)), _full((DIN, D)), _full((1, QL)), _full((NH, DKP, QL)), _full((1, KVL)),
                  _full((NH, KVL, 256)), _full((1, DKP)), _full((1, DKP)),
                  pl.BlockSpec((1, 8, 256), lambda i: (jnp.maximum(i - off, 0), 0, 0)), _full((TB, 256))],
        out_specs=[row(DU), heads(DKP), heads(DKP), heads(DV)],
        out_shape=[jax.ShapeDtypeStruct((t_all, DU), F32), jax.ShapeDtypeStruct((NH, t_all, DKP), BF16),
                   jax.ShapeDtypeStruct((NH, t_all, DKP), BF16), jax.ShapeDtypeStruct((NH, t_all, DV), BF16)],
        compiler_params=_params(("arbitrary",)),
    )(ctx, x, modsel, norm_g, w_in_t, q_lora_g, w_uq_t, kv_lora_g, w_ukv, qn_g, kn_g, cos, sin)


def _attn_fwd(q, k, v, s_len):
    t_all = q.shape[1]
    off = (t_all - s_len) // TB
    nq = s_len // TB
    nsub = next(n for n in (4, 2, 1) if nq % n == 0)

    def body(*refs):
        q_refs = refs[:nsub]
        k_ref, v_ref, o_ref, lse_ref = refs[nsub:]
        for sb in range(nsub):
            s = _nt(q_refs[sb][0], k_ref[0])
            m = jnp.max(s, axis=-1, keepdims=True)
            e = jnp.exp2(s - m)
            l = _rowsum(e)
            o_ref[sb * TB:(sb + 1) * TB, :] = _nn(e.astype(BF16), v_ref[0]) / l
            lse_ref[0, sb] = _row_layout(m + jnp.log2(l))

    qspec = lambda sb: pl.BlockSpec((1, TB, DKP), lambda h, i: (h, i * nsub + sb + off, 0))
    return pl.pallas_call(
        body, name="attn_fwd", grid=(NH, nq // nsub),
        in_specs=[qspec(sb) for sb in range(nsub)]
        + [pl.BlockSpec((1, t_all, DKP), lambda h, i: (h, 0, 0)), pl.BlockSpec((1, t_all, DV), lambda h, i: (h, 0, 0))],
        out_specs=[pl.BlockSpec((nsub * TB, DV), lambda h, i: (i, h)), pl.BlockSpec((1, nsub, 8, TB), lambda h, i: (h, i, 0, 0))],
        out_shape=[jax.ShapeDtypeStruct((s_len, NH * DV), F32), jax.ShapeDtypeStruct((NH, nq, 8, TB), F32)],
        compiler_params=_params(("arbitrary", "arbitrary")),
    )(*([q] * nsub), k, v)


def _out_stage(attn, u, x, target, gate, w_pool, pool_scale, w_out, lc):
    s_len = x.shape[0]
    t_all = s_len + lc
    off = lc // TB
    nq = s_len // TB
    hb = TB // HALO
    nqb = s_len // Q_BLOCK
    jb = TB // nqb

    def body(attn_ref, ga_ref, pin_ref, pprev_ref, pnext_ref, gp_ref, x_ref, tgt_ref, gate_ref, wp_ref, ps_ref, wo_ref,
             dxn_ref, dattn_ref, dga_ref, dgp_ref, dpool_ref, dwo_ref, dgate_ref, dps_ref, dwp_ref, loss_ref):
        i = pl.program_id(0)

        @pl.when(i == 0)
        def _():
            dwo_ref[...] = jnp.zeros_like(dwo_ref)
            dgate_ref[...] = jnp.zeros_like(dgate_ref)
            dps_ref[...] = jnp.zeros_like(dps_ref)
            dwp_ref[...] = jnp.zeros_like(dwp_ref)
            loss_ref[...] = jnp.zeros_like(loss_ref)

        attn = jnp.concatenate([attn_ref[:, jj, :] for jj in range(jb)], axis=0)
        ga = ga_ref[...]
        gp = gp_ref[...]
        pin = pin_ref[...]
        prev = jnp.where(i == 0, 0.0, pprev_ref[...])
        nxt = jnp.where(i == nq - 1, 0.0, pnext_ref[...])
        win = jnp.concatenate([prev, pin, nxt], axis=0)
        tg = i * TB + lax.broadcasted_iota(jnp.int32, (TB, 1), 0)
        pooled = []
        for g, w in enumerate(POOL_WINDOWS):
            a = win[:, g * 128:(g + 1) * 128]
            p = _shift_rows(a, -1) + a
            for step in (1, 2, 4):
                if w >= 4 * step:
                    p = _shift_rows(p, -step) + _shift_rows(p, step)
            cnt = (jnp.minimum(tg + w // 2, s_len) - jnp.maximum(tg - w // 2, 0)).astype(F32)
            pooled.append(p[HALO:HALO + TB] / cnt - a[HALO:HALO + TB])
        pooled_b = [p.astype(BF16) for p in pooled]
        z = jnp.concatenate([_nn(pooled_b[g], wp_ref[g]) for g in range(4)], axis=1)
        ps = ps_ref[...]
        yp = z * ps
        sga = _sig(ga)
        sila = ga * sga
        sgp = _sig(gp)
        silp = gp * sgp
        br = jnp.concatenate([sila * attn, silp * yp], axis=1).astype(BF16)
        y = _nn(br, wo_ref[...])
        gate = gate_ref[...]
        err = x_ref[...] + gate * y - tgt_ref[...]
        loss_ref[...] += _colsum(_rowsum(err * err)) * (0.5 / D)
        dxn = err * (1.0 / D)
        dxn_ref[...] = dxn
        dgate_ref[...] += _colsum(dxn * y)
        dy = (dxn * gate).astype(BF16)
        dwo_ref[...] += _tn(br, dy)
        dbr = _nt(dy, wo_ref[...])
        dbra = dbr[:, :512]
        dbrp = dbr[:, 512:]
        dattn = dbra * sila
        for jj in range(jb):
            dattn_ref[:, jj, :] = dattn[jj * nqb:(jj + 1) * nqb]
        dga_ref[...] = (dbra * attn * (sga * (1.0 + ga * (1.0 - sga)))).astype(BF16)
        dgp_ref[...] = (dbrp * yp * (sgp * (1.0 + gp * (1.0 - sgp)))).astype(BF16)
        dyp = dbrp * silp
        dps_ref[...] += _colsum(dyp * z)
        dz = (dyp * ps).astype(BF16)
        dpool = []
        for g in range(4):
            dzg = dz[:, g * 128:(g + 1) * 128]
            dwp_ref[g] += _tn(pooled_b[g], dzg)
            dpool.append(_nt(dzg, wp_ref[g]))
        dpool_ref[...] = jnp.concatenate(dpool, axis=1)

    lat = lambda w: pl.BlockSpec((TB, w), lambda i: (i, 0))
    perm = pl.BlockSpec((nqb, jb, 512), lambda i: (0, i, 0))
    ucol = lambda j: pl.BlockSpec((TB, 512), lambda i: (i + off, j))
    last8 = t_all // HALO - 1
    return pl.pallas_call(
        body, name="out_stage", grid=(nq,),
        in_specs=[perm, ucol(1), ucol(2),
                  pl.BlockSpec((HALO, 512), lambda i: ((i + off) * hb - 1, 2)),
                  pl.BlockSpec((HALO, 512), lambda i: (jnp.minimum((i + off + 1) * hb, last8), 2)),
                  ucol(3), lat(D), lat(D), _full((1, D)), _full((4, 128, 128)), _full((1, 512)), _full((D, D))],
        out_specs=[lat(D), perm, lat(512), lat(512), lat(512),
                   _full((D, D)), _full((1, D)), _full((1, 512)), _full((4, 128, 128)), _full((1, 1))],
        out_shape=[jax.ShapeDtypeStruct((s_len, D), F32), jax.ShapeDtypeStruct((nqb, Q_BLOCK, 512), F32),
                   jax.ShapeDtypeStruct((s_len, 512), BF16), jax.ShapeDtypeStruct((s_len, 512), BF16),
                   jax.ShapeDtypeStruct((s_len, 512), F32),
                   jax.ShapeDtypeStruct((D, D), F32), jax.ShapeDtypeStruct((1, D), F32), jax.ShapeDtypeStruct((1, 512), F32),
                   jax.ShapeDtypeStruct((4, 128, 128), F32), jax.ShapeDtypeStruct((1, 1), F32)],
        compiler_params=_params(("arbitrary",)),
    )(attn, u, u, u, u, u, x, target, gate, w_pool, pool_scale, w_out)


def _attn_bwd(q, k, v, dattn, attn, lse, s_len):
    t_all = q.shape[1]
    off = (t_all - s_len) // TB
    nq = s_len // TB
    nch = 4
    chunks = [(c * (t_all // nch), t_all // nch) for c in range(nch)]
    nsub = next(n for n in (BWD_QBLOCKS, 2, 1) if nq % n == 0)
    tq = nsub * TB

    def body(*refs):
        q_refs = refs[:nsub]
        k_ref, v_ref, do_ref, o_ref, lse_ref, dq_ref, dk_ref, dv_ref = refs[nsub:]
        i = pl.program_id(1)

        @pl.when(i == 0)
        def _():
            dk_ref[...] = jnp.zeros_like(dk_ref)
            dv_ref[...] = jnp.zeros_like(dv_ref)

        qb = jnp.concatenate([r[0] for r in q_refs], axis=0)
        delta_r = _row_layout(_rowsum(do_ref[...] * o_ref[...]))[0:1, :]
        do = do_ref[...].astype(BF16)
        lse_r = jnp.concatenate([lse_ref[0, sb][0:1, :] for sb in range(nsub)], axis=1)
        dq = jnp.zeros((tq, DKP), F32)
        for start, size in chunks:
            rows = pl.ds(start, size)
            kc = k_ref[0, rows, :]
            p_t = jnp.exp2(_nt(kc, qb) - lse_r)
            ds_t = (p_t * (_nt(v_ref[0, rows, :], do) - delta_r)).astype(BF16)
            dv_ref[0, rows, :] += _nn(p_t.astype(BF16), do)
            dk_ref[0, rows, :] += _nn(ds_t, qb)
            dq += _tn(ds_t, kc)
        dq_ref[0] = dq * SCALE

    kvspec = lambda w: pl.BlockSpec((1, t_all, w), lambda h, i: (h, 0, 0))
    rowspec = pl.BlockSpec((1, nsub, 8, TB), lambda h, i: (h, i, 0, 0))
    qspec = lambda sb: pl.BlockSpec((1, TB, DKP), lambda h, i: (h, i * nsub + sb + off, 0))
    return pl.pallas_call(
        body, name="attn_bwd", grid=(NH, nq // nsub),
        in_specs=[qspec(sb) for sb in range(nsub)]
        + [kvspec(DKP), kvspec(DV), pl.BlockSpec((tq, DV), lambda h, i: (i, h)), pl.BlockSpec((tq, DV), lambda h, i: (i, h)),
           rowspec],
        out_specs=[pl.BlockSpec((1, tq, DKP), lambda h, i: (h, i, 0)), kvspec(DKP), kvspec(DV)],
        out_shape=[jax.ShapeDtypeStruct((NH, s_len, DKP), F32), jax.ShapeDtypeStruct((NH, t_all, DKP), F32),
                   jax.ShapeDtypeStruct((NH, t_all, DV), F32)],
        compiler_params=_params(("arbitrary", "arbitrary")),
    )(*([q] * nsub), k, v, dattn, attn, lse)


def _qkv_bwd(u, dq, dk, dv, cos, sin, q_lora_g, w_uq_t, kv_lora_g, w_ukv, qn_g, kn_g, s_len):
    t_all = u.shape[0]
    off = (t_all - s_len) // TB
    nb = t_all // TB

    def body(ulo_ref, dq_ref, dk_ref, dv_ref, cos_ref, sin_ref, qlg_ref, wuq_ref, kvlg_ref, wukv_ref, qng_ref, kng_ref,
             dlo_ref, dwuq_ref, dwukv_ref, dqlg_ref, dkvlg_ref, dqng_ref, dkng_ref):
        i = pl.program_id(0)

        @pl.when(i == 0)
        def _():
            for r in (dwuq_ref, dwukv_ref, dqlg_ref, dkvlg_ref, dqng_ref, dkng_ref):
                r[...] = jnp.zeros_like(r)

        latent = i >= off
        ulo = ulo_ref[...]
        cos, sin = _rope_block(cos_ref, sin_ref, pl.program_id(0) < off)
        cq = ulo[:, 0:QL]
        rc = lax.rsqrt(jnp.mean(cq * cq, axis=-1, keepdims=True) + EPS)
        cqh = cq * rc
        qlg = qlg_ref[...]
        cqn_b = (cqh * qlg).astype(BF16)
        qng = qng_ref[...]
        ckv = ulo[:, QL:QL + KVL]
        r0 = lax.rsqrt(jnp.mean(ckv * ckv, axis=-1, keepdims=True) + EPS)
        ckvh = ckv * r0
        kvlg = kvlg_ref[...]
        ckvn_b = (ckvh * kvlg).astype(BF16)
        qhs = [_nt(cqn_b, wuq_ref[hd]) for hd in range(NH)]
        kns = [_nn(ckvn_b, wukv_ref[hd])[:, :128] for hd in range(NH)]
        dqng = jnp.zeros((1, DKP), F32)
        dqraws = []
        for hd in range(NH):
            qh = qhs[hd]
            rq = lax.rsqrt(_rowsum(qh * qh) / DK + EPS)
            xh = qh * rq
            dqh = jnp.where(latent, dq_ref[hd], 0.0)
            dyq = jnp.concatenate([dqh[:, :128], _rope_t(dqh[:, 128:], cos, sin)], axis=1)
            dqng += _colsum(dyq * xh)
            dxh = dyq * qng
            dqraws.append((rq * (dxh - xh * (_rowsum(dxh * xh) / DK))).astype(BF16))
        dqng_ref[...] += dqng

        kr = ulo[:, 384:512]
        skr = _rowsum(kr * kr)
        kng = kng_ref[...]
        dkr = jnp.zeros((TB, 128), F32)
        dkng = jnp.zeros((1, DKP), F32)
        dkvs = []
        for hd in range(NH):
            kn = kns[hd]
            rk = lax.rsqrt((_rowsum(kn * kn) + skr) / DK + EPS)
            xh1 = kn * rk
            xh2 = kr * rk
            dkh = dk_ref[hd] * LN2
            d1 = dkh[:, :128]
            d2 = _rope_t(dkh[:, 128:], cos, sin)
            dkng += jnp.concatenate([_colsum(d1 * xh1), _colsum(d2 * xh2)], axis=1)
            dx1 = d1 * kng[:, :128]
            dx2 = d2 * kng[:, 128:]
            dot = (_rowsum(dx1 * xh1) + _rowsum(dx2 * xh2)) / DK
            dkvs.append(jnp.concatenate([rk * (dx1 - xh1 * dot), dv_ref[hd]], axis=1).astype(BF16))
            dkr += rk * (dx2 - xh2 * dot)
        dkng_ref[...] += dkng

        dcqn = jnp.zeros((TB, QL), F32)
        dckvn = jnp.zeros((TB, KVL), F32)
        for hd in range(NH):
            dwuq_ref[hd] += _tn(dqraws[hd], cqn_b)[:DK]
            dcqn += _nn(dqraws[hd], wuq_ref[hd])
            dwukv_ref[hd] += _tn(ckvn_b, dkvs[hd])
            dckvn += _nt(dkvs[hd], wukv_ref[hd])
        dqlg_ref[...] += _colsum(dcqn * cqh)
        dxh = dcqn * qlg
        dcq = rc * (dxh - cqh * jnp.mean(dxh * cqh, axis=-1, keepdims=True))
        dkvlg_ref[...] += _colsum(dckvn * ckvh)
        dxh = dckvn * kvlg
        dckv = r0 * (dxh - ckvh * jnp.mean(dxh * ckvh, axis=-1, keepdims=True))
        dlo_ref[...] = jnp.concatenate([dcq, dckv, dkr], axis=1).astype(BF16)

    row = lambda w: pl.BlockSpec((TB, w), lambda i: (i, 0))
    heads = lambda w: pl.BlockSpec((NH, TB, w), lambda i: (0, i, 0))
    return pl.pallas_call(
        body, name="qkv_bwd", grid=(nb,),
        in_specs=[row(512), pl.BlockSpec((NH, TB, DKP), lambda i: (0, jnp.maximum(i - off, 0), 0)), heads(DKP), heads(DV),
                  pl.BlockSpec((1, 8, 256), lambda i: (jnp.maximum(i - off, 0), 0, 0)), _full((TB, 256)), _full((1, QL)), _full((NH, DKP, QL)), _full((1, KVL)), _full((NH, KVL, 256)),
                  _full((1, DKP)), _full((1, DKP))],
        out_specs=[row(512), _full((NH, DK, QL)), _full((NH, KVL, 256)), _full((1, QL)), _full((1, KVL)),
                   _full((1, DKP)), _full((1, DKP))],
        out_shape=[jax.ShapeDtypeStruct((t_all, 512), BF16), jax.ShapeDtypeStruct((NH, DK, QL), F32),
                   jax.ShapeDtypeStruct((NH, KVL, 256), F32), jax.ShapeDtypeStruct((1, QL), F32),
                   jax.ShapeDtypeStruct((1, KVL), F32), jax.ShapeDtypeStruct((1, DKP), F32), jax.ShapeDtypeStruct((1, DKP), F32)],
        compiler_params=_params(("arbitrary",)),
    )(u, dq, dk, dv, cos, sin, q_lora_g, w_uq_t, kv_lora_g, w_ukv, qn_g, kn_g)


def _in_bwd(ctx, x, modsel, norm_g, dlo, dga, dgp, dpool, dxn, w_in_t):
    s_len, lc = x.shape[0], ctx.shape[0]
    t_all = s_len + lc
    off = lc // TB
    nb = t_all // TB
    nq = s_len // TB
    hb = TB // HALO
    n = TB + 2 * HALO

    def body(ctx_ref, x_ref, mod_ref, ng_ref, dlo_ref, dga_ref, dgp_ref, dp_ref, dpprev_ref, dpnext_ref, dxn_ref, win_ref,
             gx_ref, dwin_ref, dmod_ref, dng_ref):
        i = pl.program_id(0)
        j = i - off

        @pl.when(i == 0)
        def _():
            dwin_ref[...] = jnp.zeros_like(dwin_ref)
            dmod_ref[...] = jnp.zeros_like(dmod_ref)
            dng_ref[...] = jnp.zeros_like(dng_ref)

        latent = i >= off
        dp = dp_ref[...]
        prev = jnp.where(j <= 0, 0.0, dpprev_ref[...])
        nxt = jnp.where(j >= nq - 1, 0.0, dpnext_ref[...])
        win = jnp.concatenate([prev, dp, nxt], axis=0)
        tg = j * TB - HALO + lax.broadcasted_iota(jnp.int32, (n, 1), 0)
        dpin = []
        for g, w in enumerate(POOL_WINDOWS):
            cnt = jnp.maximum(jnp.minimum(tg + w // 2, s_len) - jnp.maximum(tg - w // 2, 0), 1).astype(F32)
            zq = win[:, g * 128:(g + 1) * 128] / cnt
            zq = zq + _shift_rows(zq, 1)
            for step in (1, 2, 4):
                if w >= 4 * step:
                    zq = _shift_rows(zq, -step) + _shift_rows(zq, step)
            dpin.append(zq[HALO:HALO + TB] - dp[:, g * 128:(g + 1) * 128])
        zero = jnp.zeros((TB, 512), BF16)
        du = [dlo_ref[...], jnp.where(latent, dga_ref[...], zero),
              jnp.where(latent, jnp.concatenate(dpin, axis=1).astype(BF16), zero), jnp.where(latent, dgp_ref[...], zero)]

        ng = ng_ref[...]
        xb = jnp.where(i < off, ctx_ref[...], x_ref[...])
        r, xh, xg, h, scale = _modulated(xb, mod_ref, ng)
        hb_ = h.astype(BF16)
        dh = jnp.zeros((TB, D), F32)
        for s, (lo, hi) in enumerate(SEG):
            dwin_ref[lo:hi, :] += _tn(du[s], hb_)
            dh += _nn(du[s], win_ref[lo:hi, :])
        is_lat = latent.astype(F32)
        dsh = _colsum(dh)
        dsc = _colsum(dh * xg)
        dmod_ref[0, 0:1, :] += dsh * (1.0 - is_lat)
        dmod_ref[0, 1:2, :] += dsc * (1.0 - is_lat)
        dmod_ref[1, 0:1, :] += dsh * is_lat
        dmod_ref[1, 1:2, :] += dsc * is_lat
        dxg = dh * (1.0 + scale)
        dng_ref[...] += _colsum(dxg * xh)
        dxh = dxg * ng
        gx_ref[...] = r * (dxh - xh * jnp.mean(dxh * xh, axis=-1, keepdims=True)) + dxn_ref[...]

    row = lambda w: pl.BlockSpec((TB, w), lambda i: (i, 0))
    lat = lambda w: pl.BlockSpec((TB, w), lambda i: (jnp.maximum(i - off, 0), 0))
    last8 = s_len // HALO - 1
    cspec, xspec, mspec = _token_specs(off)
    return pl.pallas_call(
        body, name="in_bwd", grid=(nb,),
        in_specs=[cspec, xspec, mspec, _full((1, D)), row(512), lat(512), lat(512), lat(512),
                  pl.BlockSpec((HALO, 512), lambda i: (jnp.maximum(jnp.maximum(i - off, 0) * hb - 1, 0), 0)),
                  pl.BlockSpec((HALO, 512), lambda i: (jnp.minimum((jnp.maximum(i - off, 0) + 1) * hb, last8), 0)),
                  lat(D), _full((DIN, D))],
        out_specs=[lat(D), _full((DIN, D)), _full((2, 2, D)), _full((1, D))],
        out_shape=[jax.ShapeDtypeStruct((s_len, D), F32), jax.ShapeDtypeStruct((DIN, D), F32),
                   jax.ShapeDtypeStruct((2, 2, D), F32), jax.ShapeDtypeStruct((1, D), F32)],
        compiler_params=_params(("arbitrary",)),
    )(ctx, x, modsel, norm_g, dlo, dga, dgp, dpool, dpool, dpool, dxn, w_in_t)


def _adamw_update(w_ref, g_ref, m_ref, v_ref, d_ref, mo_ref, vo_ref):
    gv = g_ref[...]
    mn = ADAM_B1 * m_ref[...] + (1.0 - ADAM_B1) * gv
    vn = ADAM_B2 * v_ref[...] + (1.0 - ADAM_B2) * (gv * gv)
    m_hat = mn / (1.0 - ADAM_B1 ** ADAM_STEP)
    v_hat = vn / (1.0 - ADAM_B2 ** ADAM_STEP)
    d_ref[...] = -ADAM_LR * (m_hat / (jnp.sqrt(v_hat) + ADAM_EPS) + ADAM_WD * w_ref[...])
    mo_ref[...] = mn
    vo_ref[...] = vn


def _adamw_many(ws, gs, ms, vs):
    n = len(ws)
    parts = 4

    def body(*refs):
        for i in range(n):
            _adamw_update(refs[i], refs[n + i], refs[2 * n + i], refs[3 * n + i], refs[4 * n + i], refs[5 * n + i], refs[6 * n + i])
            refs[7 * n + i][...] = refs[n + i][...]

    def spec(w):
        rows, cols = w.shape
        if rows % (8 * parts) == 0:
            return pl.BlockSpec((rows // parts, cols), lambda i: (i, 0))
        if cols % (128 * parts) == 0:
            return pl.BlockSpec((rows, cols // parts), lambda i: (0, i))
        return _full((rows, cols))

    specs = [spec(w) for w in ws]
    shp = [jax.ShapeDtypeStruct(w.shape, F32) for w in ws]
    out = pl.pallas_call(body, name="adamw_many", grid=(parts,), in_specs=specs * 4, out_specs=specs * 4, out_shape=shp * 4,
                         compiler_params=_params(("arbitrary",)))(*ws, *gs, *ms, *vs)
    return out[:n], out[n:2 * n], out[2 * n:3 * n], out[3 * n:]


class _Links:
    def __init__(self, send_sems, recv_sems):
        self.send_sems, self.recv_sems, self.sends = send_sems, recv_sems, []

    def send(self, src, dst, sem, to):
        cp = pltpu.make_async_remote_copy(src, dst, self.send_sems.at[sem], self.recv_sems.at[sem], device_id=to,
                                          device_id_type=MESH)
        cp.start()
        self.sends.append(cp)

    def arrived(self, dst, sem, frm):
        pltpu.make_async_remote_copy(dst, dst, self.send_sems.at[sem], self.recv_sems.at[sem], device_id=frm,
                                     device_id_type=MESH).wait_recv()

    def drain(self):
        for cp in self.sends:
            cp.wait_send()


def _half(ref, c, axis):
    size = ref.shape[axis - 2] // 2
    win = pl.ds(pl.multiple_of(c * size, 16 if axis == 0 else 128), size)
    idx = (win, slice(None)) if axis == 0 else (slice(None), win)
    return ref.at[(slice(None),) * (len(ref.shape) - 2) + idx]


def _select_rows(slots_ref, n_slots, row=0):
    sub = lax.broadcasted_iota(jnp.int32, (8, 1), 0)
    out = None
    for d in range(n_slots):
        r = jnp.where(sub == d, jnp.broadcast_to(slots_ref[d][row:row + 1, :], (8, slots_ref.shape[-1])), 0.0)
        out = r if out is None else out + r
    return out


def _gather(c, c_ctx, w_mod, b_mod_k, shards, axes, slab_rows):
    nw = len(shards)
    kw = w_mod.shape[1]

    def body(*refs):
        c_ref, cc_ref, wm_ref, b_ref = refs[:4]
        w_refs = refs[4:4 + nw]
        a16_ref, mod_ref = refs[4 + nw:6 + nw]
        g_refs = refs[6 + nw:6 + 2 * nw]
        a_ref, send_sems, recv_sems = refs[6 + 2 * nw:]

        def slab(wi, chip):
            return g_refs[wi].at[chip].at[0:shards[wi].shape[0]]
        x, y, cc = lax.axis_index("x"), lax.axis_index("y"), lax.axis_index("c")
        me = 4 * x + 2 * y + cc
        k = 2 * x + y
        sibling = (x, y, 1 - cc)
        links = _Links(send_sems, recv_sems)
        chips = [_peer(x, y, cc, off + (0,)) for off in CHIPS3]
        chip_a = ((x + 1 - cc) % 2, (y + cc) % 2, cc)
        chip_b = ((x + cc) % 2, (y + 1 - cc) % 2, cc)
        chip_d = (1 - x, 1 - y, cc)
        cv = c_ref[...]
        a_ref[me] = jnp.broadcast_to(cv * _sig(cv), (8, D))
        for j, off in enumerate(PEERS7):
            links.send(a_ref.at[me], a_ref.at[me], j, _peer(x, y, cc, off))
        for wi in range(nw):
            slab(wi, k)[...] = w_refs[wi][...].astype(BF16)
            for j, to in enumerate((chip_a, chip_b)):
                links.send(_half(slab(wi, k), cc, axes[wi]), _half(slab(wi, k), cc, axes[wi]), 10 + wi * 6 + j, to)
            pad = slab_rows[wi] - shards[wi].shape[0]
            if pad:
                for kk in range(4):
                    g_refs[wi][kk, shards[wi].shape[0]:, :] = jnp.zeros((pad, shards[wi].shape[1]), BF16)
        for j, off in enumerate(PEERS7):
            px, py, pc = _peer(x, y, cc, off)
            links.arrived(a_ref.at[4 * px + 2 * py + pc], j, (px, py, pc))
        ccv = cc_ref[...]
        sub = lax.broadcasted_iota(jnp.int32, (8, 1), 0)
        a16 = jnp.concatenate([_select_rows(a_ref, 8), jnp.where(sub == 0, jnp.broadcast_to(ccv * _sig(ccv), (8, D)), 0.0)], axis=0)
        a16_ref[...] = a16
        mod_ref[k] = _dot3(_nn, a16, wm_ref[...]) + b_ref[...]
        for j, to in enumerate(chips):
            links.send(mod_ref.at[k], mod_ref.at[k], 7 + j, to)
        for j, (frm, origin) in enumerate(((chip_a, chip_a), (chip_b, chip_b), (chip_b, chip_d))):
            for wi in range(nw):
                blk = _half(slab(wi, 2 * origin[0] + origin[1]), cc, axes[wi])
                links.arrived(blk, 10 + wi * 6 + j, frm)
                if j == 0:
                    links.send(blk, blk, 10 + wi * 6 + 2, chip_b)
                links.send(blk, blk, 10 + wi * 6 + 3 + j, sibling)
        for j, (px, py, pc) in enumerate(chips):
            links.arrived(mod_ref.at[2 * px + py], 7 + j, (px, py, pc))
        for j, origin in enumerate((chip_b, chip_a, chip_d)):
            for wi in range(nw):
                links.arrived(_half(slab(wi, 2 * origin[0] + origin[1]), 1 - cc, axes[wi]), 10 + wi * 6 + 3 + j, sibling)
        links.drain()

    nsem = 10 + 6 * nw
    return pl.pallas_call(
        body, name="gather", in_specs=[VM] * (4 + nw), out_specs=[VM] * (2 + nw),
        out_shape=[jax.ShapeDtypeStruct((16, D), F32), jax.ShapeDtypeStruct((4, 16, kw), F32)]
        + [jax.ShapeDtypeStruct((4, r, s.shape[1]), BF16) for r, s in zip(slab_rows, shards)],
        scratch_shapes=[pltpu.VMEM((8, 8, D), F32), pltpu.SemaphoreType.DMA((nsem,)), pltpu.SemaphoreType.DMA((nsem,))],
        compiler_params=pltpu.CompilerParams(vmem_limit_bytes=VMEM_LIMIT),
    )(c, c_ctx, w_mod, b_mod_k, *shards)


SMALL_ROW_WIDTHS = (D, QL, KVL, DKP, DKP, 512, 128)
SMALL_OUT_WIDTHS = (D, QL, KVL, DK, DK, 512, 1)


def _reduce(grads, axes, smalls, w_pool_g, dmod8, a16, w_mod, c_ctx):
    nw = len(grads)
    ns = len(smalls)
    kw = w_mod.shape[1]
    halves = []
    for g, ax in zip(grads, axes):
        halves.append((g.shape[1] // 2, g.shape[2]) if ax == 0 else (g.shape[1], g.shape[2] // 2))

    def body(*refs):
        g_refs = refs[:nw]
        small_refs = refs[nw:nw + ns]
        wp_ref, dm_ref, a16_ref, wm_ref, cc_ref = refs[nw + ns:nw + ns + 5]
        o = nw + ns + 5
        r_refs = refs[o:o + nw]
        small_outs = refs[o + nw:o + nw + ns]
        rwp_ref, gw_ref, gb_ref, gc_ref = refs[o + nw + ns:o + nw + ns + 4]
        o = o + nw + ns + 4
        own, sib, part, got, rel = (refs[o + i * nw:o + (i + 1) * nw] for i in range(5))
        smbuf, wps, wpg, dm_all, pc_all, send_sems, recv_sems, local_sems = refs[o + 5 * nw:]
        x, y, cc = lax.axis_index("x"), lax.axis_index("y"), lax.axis_index("c")
        me = 4 * x + 2 * y + cc
        k = 2 * x + y
        sibling = (x, y, 1 - cc)
        links = _Links(send_sems, recv_sems)
        chips = [_peer(x, y, cc, off + (0,)) for off in CHIPS3]
        peers = [_peer(x, y, cc, off) for off in PEERS7]
        chip_a = ((x + 1 - cc) % 2, (y + cc) % 2, cc)
        chip_b = ((x + cc) % 2, (y + 1 - cc) % 2, cc)
        ka, kb, kd = 2 * chip_a[0] + chip_a[1], 2 * chip_b[0] + chip_b[1], 2 * (1 - x) + (1 - y)
        big, sm0, wp0, dm0, pc0 = 0, 5 * nw, 5 * nw + 7, 5 * nw + 14, 5 * nw + 21

        locals_ = []
        for wi in range(nw):
            lc = pltpu.make_async_copy(_half(g_refs[wi], cc, axes[wi]), own[wi], local_sems.at[wi])
            lc.start()
            locals_.append(lc)
            links.send(_half(g_refs[wi], 1 - cc, axes[wi]), sib[wi], big + wi * 5, sibling)
        slot = smbuf.at[me]
        slot[...] = jnp.zeros((8, D), F32)
        for r, (ref, w) in enumerate(zip(small_refs, SMALL_ROW_WIDTHS)):
            slot[r:r + 1, 0:w] = jnp.broadcast_to(ref[...], (1, w))
        links.send(wp_ref, wps, wp0, sibling)
        dm_all[me] = dm_ref[...]
        for j, peer in enumerate(peers):
            links.send(dm_all.at[me], dm_all.at[me], dm0 + j, peer)
            links.send(smbuf.at[me], smbuf.at[me], sm0 + j, peer)
        links.arrived(wps, wp0, sibling)
        wpg[k] = (wp_ref[...] + wps[...]).astype(BF16)
        for j, to in enumerate(chips):
            links.send(wpg.at[k], wpg.at[k], wp0 + 1 + j, to)
        for wi in range(nw):
            locals_[wi].wait()
            links.arrived(sib[wi], big + wi * 5, sibling)
            part[wi][...] = (own[wi][...] + sib[wi][...]).astype(BF16)
            got[wi][k] = part[wi][k]
            got[wi][kd] = jnp.zeros(halves[wi], BF16)
            links.send(part[wi].at[kd], rel[wi], big + wi * 5 + 1, chip_b)
            links.send(part[wi].at[kb], got[wi].at[k], big + wi * 5 + 2, chip_b)
        for j, (px, py, pc) in enumerate(peers):
            links.arrived(dm_all.at[4 * px + 2 * py + pc], dm0 + j, (px, py, pc))
        dm_tot = dm_all[0]
        for d in range(1, 8):
            dm_tot = dm_tot + dm_all[d]
        for kk in range(4):
            gb_ref[:, kk * kw:(kk + 1) * kw] = dm_tot[kk:kk + 1, :] + dm_tot[4 + kk:5 + kk, :]
        top = jnp.zeros((8, kw), F32)
        dmc_k = jnp.zeros((1, kw), F32)
        for kk in range(4):
            top = top + jnp.where(k == kk, _select_rows(dm_all, 8, kk), 0.0)
            dmc_k = dmc_k + jnp.where(k == kk, dm_tot[4 + kk:5 + kk, :], 0.0)
        sub = lax.broadcasted_iota(jnp.int32, (8, 1), 0)
        bk = jnp.concatenate([top, jnp.where(sub == 0, jnp.broadcast_to(dmc_k, (8, kw)), 0.0)], axis=0)
        gw_ref[...] = _dot3(_tn, a16_ref[...], bk)
        pc_all[k] = _dot3(_nt, jnp.broadcast_to(bk[8:9, :], (8, kw)), wm_ref[...])
        for j, to in enumerate(chips):
            links.send(pc_all.at[k], pc_all.at[k], pc0 + j, to)
        for j, (px, py, pc) in enumerate(peers):
            links.arrived(smbuf.at[4 * px + 2 * py + pc], sm0 + j, (px, py, pc))
        tot = smbuf[0]
        for d in range(1, 8):
            tot = tot + smbuf[d]
        for r, (ref, w) in enumerate(zip(small_outs, SMALL_OUT_WIDTHS)):
            ref[...] = tot[r:r + 1, 0:w]
        for j, (px, py, pc) in enumerate(chips):
            links.arrived(wpg.at[2 * px + py], wp0 + 1 + j, (px, py, pc))
        wpt = wpg[0].astype(F32)
        for kk in range(1, 4):
            wpt = wpt + wpg[kk].astype(F32)
        rwp_ref[...] = wpt
        for wi in range(nw):
            links.arrived(rel[wi], big + wi * 5 + 1, chip_b)
            rel[wi][...] = (part[wi][ka].astype(F32) + rel[wi][...].astype(F32)).astype(BF16)
            links.send(rel[wi], got[wi].at[k], big + wi * 5 + 3, chip_a)
        for wi in range(nw):
            links.arrived(got[wi].at[kb], big + wi * 5 + 2, chip_b)
            links.arrived(got[wi].at[ka], big + wi * 5 + 3, chip_a)
            total = got[wi][0].astype(F32)
            for kk in range(1, 4):
                total = total + got[wi][kk].astype(F32)
            mine = _half(r_refs[wi], cc, axes[wi])
            mine[...] = total
            links.send(mine, mine, big + wi * 5 + 4, sibling)
        for j, (px, py, pc) in enumerate(chips):
            links.arrived(pc_all.at[2 * px + py], pc0 + j, (px, py, pc))
        ccv = cc_ref[...]
        sg = _sig(ccv)
        gc_ref[...] = (pc_all[0][0:1, :] + pc_all[1][0:1, :] + pc_all[2][0:1, :] + pc_all[3][0:1, :]) * (sg * (1.0 + ccv * (1.0 - sg)))
        for wi in range(nw):
            links.arrived(_half(r_refs[wi], 1 - cc, axes[wi]), big + wi * 5 + 4, sibling)
        links.drain()

    nsem = 5 * nw + 24
    quads = [(4,) + h for h in halves]
    return pl.pallas_call(
        body, name="reduce", in_specs=[ANY] * nw + [VM] * (ns + 5), out_specs=[VM] * (nw + ns + 4),
        out_shape=[jax.ShapeDtypeStruct(g.shape[1:], F32) for g in grads]
        + [jax.ShapeDtypeStruct((1, w), F32) for w in SMALL_OUT_WIDTHS]
        + [jax.ShapeDtypeStruct(w_pool_g.shape, F32), jax.ShapeDtypeStruct((D, kw), F32), jax.ShapeDtypeStruct((1, 3 * D), F32),
           jax.ShapeDtypeStruct((1, D), F32)],
        scratch_shapes=[pltpu.VMEM(q, F32) for q in quads] + [pltpu.VMEM(q, F32) for q in quads]
        + [pltpu.VMEM(q, BF16) for q in quads] + [pltpu.VMEM(q, BF16) for q in quads] + [pltpu.VMEM(h, BF16) for h in halves]
        + [pltpu.VMEM((8, 8, D), F32), pltpu.VMEM(w_pool_g.shape, F32), pltpu.VMEM((4,) + w_pool_g.shape, BF16),
           pltpu.VMEM((8, 8, kw), F32), pltpu.VMEM((4, 8, D), F32)]
        + [pltpu.SemaphoreType.DMA((nsem,)), pltpu.SemaphoreType.DMA((nsem,)), pltpu.SemaphoreType.DMA((nw,))],
        compiler_params=pltpu.CompilerParams(vmem_limit_bytes=VMEM_LIMIT),
    )(*grads, *smalls, w_pool_g, dmod8, a16, w_mod, c_ctx)


def _rope_tables(s_len):
    rows = s_len // GRID_W
    per = TB // GRID_W
    n_freq = 16
    inv = ROPE_BASE ** (-jnp.arange(n_freq, dtype=F32) / n_freq)
    ang_r = jnp.arange(rows, dtype=F32)[:, None] * inv
    ang_c = jnp.arange(GRID_W, dtype=F32)[:, None] * inv
    by_row, by_col = [], []
    for fn, pad in ((jnp.cos, 1.0), (jnp.sin, 0.0)):
        r = jnp.concatenate([fn(ang_r), fn(ang_r), jnp.zeros((rows, 96), F32)], axis=1).reshape(rows // per, per, 128)
        by_row.append(jnp.pad(r, ((0, 0), (0, 8 - per), (0, 0))))
        cpart = jnp.concatenate([jnp.zeros((GRID_W, 32), F32), fn(ang_c), fn(ang_c), jnp.full((GRID_W, 64), pad, F32)], axis=1)
        by_col.append(jnp.tile(cpart, (per, 1)))
    return jnp.concatenate(by_row, axis=-1), jnp.concatenate(by_col, axis=-1)


def kernel(x, c, ctx, c_ctx, w_mod, b_mod, norm_g, w_in, q_lora_g, w_uq, kv_lora_g, w_ukv, q_norm_g, k_norm_g, w_pool, pool_scale, w_out, loss_target, m_c_ctx, m_w_mod, m_b_mod, m_norm_g, m_w_in, m_q_lora_g, m_w_uq, m_kv_lora_g, m_w_ukv, m_q_norm_g, m_k_norm_g, m_w_pool, m_pool_scale, m_w_out, v_c_ctx, v_w_mod, v_b_mod, v_norm_g, v_w_in, v_q_lora_g, v_w_uq, v_kv_lora_g, v_w_ukv, v_q_norm_g, v_k_norm_g, v_w_pool, v_pool_scale, v_w_out):
    xi, yi, ci = lax.axis_index("x"), lax.axis_index("y"), lax.axis_index("c")
    me = 4 * xi + 2 * yi + ci
    k = 2 * xi + yi
    s_len = x.shape[1]
    lc = ctx.shape[1]
    kw = w_mod.shape[2]
    weights = dict(c_ctx=c_ctx, w_mod=w_mod, b_mod=b_mod, norm_g=norm_g, w_in=w_in, q_lora_g=q_lora_g, w_uq=w_uq,
                   kv_lora_g=kv_lora_g, w_ukv=w_ukv, q_norm_g=q_norm_g, k_norm_g=k_norm_g, w_pool=w_pool,
                   pool_scale=pool_scale, w_out=w_out)
    m_in = dict(c_ctx=m_c_ctx, w_mod=m_w_mod, b_mod=m_b_mod, norm_g=m_norm_g, w_in=m_w_in, q_lora_g=m_q_lora_g, w_uq=m_w_uq,
                kv_lora_g=m_kv_lora_g, w_ukv=m_w_ukv, q_norm_g=m_q_norm_g, k_norm_g=m_k_norm_g, w_pool=m_w_pool,
                pool_scale=m_pool_scale, w_out=m_w_out)
    v_in = dict(c_ctx=v_c_ctx, w_mod=v_w_mod, b_mod=v_b_mod, norm_g=v_norm_g, w_in=v_w_in, q_lora_g=v_q_lora_g, w_uq=v_w_uq,
                kv_lora_g=v_kv_lora_g, w_ukv=v_w_ukv, q_norm_g=v_q_norm_g, k_norm_g=v_k_norm_g, w_pool=v_w_pool,
                pool_scale=v_pool_scale, w_out=v_w_out)
    order = ["c_ctx", "w_mod", "b_mod", "norm_g", "w_in", "q_lora_g", "w_uq", "kv_lora_g", "w_ukv", "q_norm_g", "k_norm_g",
             "w_pool", "pool_scale", "w_out"]
    transposed = ("w_in", "w_uq")
    as2d = lambda n, a: jnp.transpose(a[0]) if n in transposed else a.reshape(-1, a.shape[-1])
    back = lambda n, a: jnp.transpose(a)[None] if n in transposed else a.reshape(weights[n].shape)

    c_ctx2 = c_ctx.reshape(1, D)
    b_mod_k = lax.dynamic_slice(b_mod, (0, k * kw), (1, kw))
    split = (1, 0, 0, 0)
    a16, mod_all, g_in, g_uq, g_ukv, g_out = _gather(
        c, c_ctx2, w_mod[0], b_mod_k, [as2d("w_in", w_in), as2d("w_uq", w_uq), w_ukv[0], w_out[0]], split,
        (DIN // 4, DKP, KVL, D // 4))
    mod_me = lax.dynamic_index_in_dim(mod_all, me, axis=1, keepdims=False).reshape(3, D)
    mod_c = mod_all[:, 8, :].reshape(3, D)
    modsel = jnp.stack([mod_c, mod_me])
    w_in_t = g_in.reshape(DIN, D)
    w_uq_t = g_uq
    w_out_f = g_out.reshape(D, D)
    qn_g = jnp.pad(q_norm_g, ((0, 0), (0, DKP - DK)))
    kn_g = jnp.pad(k_norm_g, ((0, 0), (0, DKP - DK)))
    w_pool_b = w_pool[0].astype(BF16)
    cos, sin = _rope_tables(s_len)

    u, q, kk, v = _fwd_in(ctx[0], x[0], modsel, norm_g, w_in_t, q_lora_g, w_uq_t, kv_lora_g, g_ukv, qn_g, kn_g, cos, sin)
    attn, lse = _attn_fwd(q, kk, v, s_len)
    (dxn, dattn, dga, dgp, dpool, dw_out, dgate, dps, dw_pool, loss) = _out_stage(
        attn.reshape(s_len // Q_BLOCK, Q_BLOCK, NH * DV), u, x[0], loss_target[0], modsel[1, 2:3, :], w_pool_b, pool_scale,
        w_out_f, lc)
    dattn = dattn.reshape(s_len, NH * DV)
    dq, dk, dv = _attn_bwd(q, kk, v, dattn, attn, lse, s_len)
    dlo, dw_uq_t, dw_ukv, dqlg, dkvlg, dqng, dkng = _qkv_bwd(u, dq, dk, dv, cos, sin, q_lora_g, w_uq_t, kv_lora_g, g_ukv,
                                                            qn_g, kn_g, s_len)
    gx, dw_in_t, dmod, dng = _in_bwd(ctx[0], x[0], modsel, norm_g, dlo, dga, dgp, dpool, dxn, w_in_t)

    dmod_l = jnp.concatenate([dmod[1, 0], dmod[1, 1], dgate[0]]).reshape(4, kw)
    dmod_c = jnp.concatenate([dmod[0, 0], dmod[0, 1], jnp.zeros((D,), F32)]).reshape(4, kw)
    dmod8 = jnp.concatenate([dmod_l, dmod_c], axis=0)
    r_in, r_uq, r_ukv, r_out, g_ng, g_qlg, g_kvlg, g_qng, g_kng, g_ps, loss_all, g_wp, g_w_mod, g_b_mod, g_c_ctx = _reduce(
        [dw_in_t.reshape(4, DIN // 4, D), dw_uq_t, dw_ukv, dw_out.reshape(4, D // 4, D)], split,
        [dng, dqlg, dkvlg, dqng, dkng, dps, loss], dw_pool, dmod8, a16, w_mod[0], c_ctx2)
    g2d = dict(c_ctx=g_c_ctx, b_mod=g_b_mod, w_mod=g_w_mod, w_in=r_in, w_uq=r_uq, w_ukv=r_ukv, w_out=r_out, norm_g=g_ng,
               q_lora_g=g_qlg, kv_lora_g=g_kvlg, q_norm_g=g_qng, k_norm_g=g_kng, pool_scale=g_ps, w_pool=g_wp.reshape(512, 128))

    outs = _adamw_many([as2d(n, weights[n]) for n in order], [g2d[n] for n in order], [as2d(n, m_in[n]) for n in order],
                       [as2d(n, v_in[n]) for n in order])
    d2d, m2d, v2d, g2d = (dict(zip(order, arrs)) for arrs in outs)

    return (loss_all[0, 0], gx[None], *[back(n, g2d[n]) for n in order], *[back(n, d2d[n]) for n in order],
            *[back(n, m2d[n]) for n in order], *[back(n, v2d[n]) for n in order])
```

```python
import jax
import jax.numpy as jnp
from jax import lax
from jax.experimental import pallas as pl
from jax.experimental.pallas import tpu as pltpu

F32 = jnp.float32
BF16 = jnp.bfloat16
MESH = pl.DeviceIdType.MESH

D = 1024
NH = 4
DK = 192
DKP = 256
DV = 128
QL = 256
KVL = 128
DIN = 1984
U_LO = 448
SEG = ((0, 512), (448, 960), (960, 1472), (1472, 1984))
DU = 2048
POOL_WINDOWS = (2, 4, 8, 16)
HALO = 8
EPS = 1e-6
ROPE_BASE = 10000.0
GRID_W = 64
Q_BLOCK = 128
TB = 256
BWD_QBLOCKS = 1
SCALE = DK ** -0.5
LOG2E = 1.4426950408889634
LN2 = 0.6931471805599453
VMEM_LIMIT = 56 * 1024 * 1024

ADAM_LR = 0.001
ADAM_B1 = 0.9
ADAM_B2 = 0.999
ADAM_EPS = 1e-08
ADAM_WD = 0.01
ADAM_STEP = 10

CHIPS3 = ((1, 0), (0, 1), (1, 1))
PEERS7 = tuple((dx, dy, dc) for dx in (0, 1) for dy in (0, 1) for dc in (0, 1) if (dx, dy, dc) != (0, 0, 0))

VM = pl.BlockSpec(memory_space=pltpu.VMEM)
ANY = pl.BlockSpec(memory_space=pl.ANY)


def _nn(a, b):
    return jnp.dot(a, b, preferred_element_type=F32)


def _nt(a, b):
    return lax.dot_general(a, b, (((1,), (1,)), ((), ())), preferred_element_type=F32)


def _tn(a, b):
    return lax.dot_general(a, b, (((0,), (0,)), ((), ())), preferred_element_type=F32)


def _split3(a):
    a0 = a.astype(BF16)
    r = a - a0.astype(F32)
    a1 = r.astype(BF16)
    a2 = (r - a1.astype(F32)).astype(BF16)
    return a0, a1, a2


def _dot3(dot, a, b):
    sa = _split3(a)
    sb = _split3(b)
    out = None
    for i in range(3):
        for j in range(3 - i):
            t = dot(sa[i], sb[j])
            out = t if out is None else out + t
    return out


def _sig(x):
    return 1.0 / (1.0 + jnp.exp(-x))


def _rot(t):
    src = lax.broadcasted_iota(jnp.int32, (128, 128), 0)
    dst = lax.broadcasted_iota(jnp.int32, (128, 128), 1)
    first = (dst % 32) < 16
    perm = jnp.where(first & (src == dst + 16), -1.0, jnp.where(~first & (src == dst - 16), 1.0, 0.0)).astype(BF16)
    hi = t.astype(BF16)
    lo = (t - hi.astype(F32)).astype(BF16)
    return _nn(hi, perm) + _nn(lo, perm)


def _rope(t, cos, sin):
    return t * cos + _rot(t) * sin


def _rope_t(t, cos, sin):
    return t * cos - _rot(t * sin)


def _rope_block(rows_ref, cols_ref, is_ctx):
    lane = lax.broadcasted_iota(jnp.int32, (TB, 256), 1) % 128
    rows = jnp.concatenate([jnp.broadcast_to(rows_ref[0, r:r + 1, :], (GRID_W, 256)) for r in range(TB // GRID_W)], axis=0)
    cs = jnp.where(lane < 32, rows, cols_ref[...])
    return jnp.where(is_ctx, 1.0, cs[:, :128]), jnp.where(is_ctx, 0.0, cs[:, 128:])


def _shift_rows(z, k):
    n = z.shape[0]
    return pltpu.roll(z, (n - k) % n, 0)


def _colsum(a):
    return jnp.sum(a, axis=0, keepdims=True)


def _rowsum(a):
    return jnp.sum(a, axis=-1, keepdims=True)


def _row_layout(col):
    return jnp.transpose(jnp.broadcast_to(col, (col.shape[0], 128)))[0:8, :]


def _params(sem=None):
    return pltpu.CompilerParams(dimension_semantics=sem, vmem_limit_bytes=VMEM_LIMIT)


def _full(shape):
    nd = len(shape)
    return pl.BlockSpec(shape, lambda *_: (0,) * nd)


def _peer(x, y, c, off):
    dx, dy, dc = off
    return ((x + dx) % 2, (y + dy) % 2, (c + dc) % 2)


def _token_specs(off):
    ctx = pl.BlockSpec((TB, D), lambda i: (jnp.minimum(i, off - 1), 0))
    lat = pl.BlockSpec((TB, D), lambda i: (jnp.maximum(i - off, 0), 0))
    mod = pl.BlockSpec((1, 3, D), lambda i: (jnp.minimum(i // off, 1), 0, 0))
    return ctx, lat, mod


def _modulated(x, mod_ref, ng):
    shift = mod_ref[0, 0:1, :]
    scale = mod_ref[0, 1:2, :]
    r = lax.rsqrt(jnp.mean(x * x, axis=-1, keepdims=True) + EPS)
    xh = x * r
    xg = xh * ng
    return r, xh, xg, xg * (1.0 + scale) + shift, scale


def _fwd_in(ctx, x, modsel, norm_g, w_in_t, q_lora_g, w_uq_t, kv_lora_g, w_ukv, qn_g, kn_g, cos, sin):
    s_len, lc = x.shape[0], ctx.shape[0]
    t_all = s_len + lc
    nb = t_all // TB
    off = lc // TB

    def body(ctx_ref, x_ref, mod_ref, ng_ref, win_ref, qlg_ref, wuq_ref, kvlg_ref, wukv_ref, qng_ref, kng_ref, cos_ref, sin_ref,
             u_ref, q_ref, k_ref, v_ref):
        is_ctx = pl.program_id(0) < off
        xb = jnp.where(is_ctx, ctx_ref[...], x_ref[...])
        _, _, _, h, _ = _modulated(xb, mod_ref, ng_ref[...])
        hb = h.astype(BF16)
        lane = lax.broadcasted_iota(jnp.int32, (TB, 512), 1)
        ulo = jnp.where(lane < U_LO, _nt(hb, win_ref[SEG[0][0]:SEG[0][1], :]), 0.0)
        u_ref[:, 0:512] = ulo
        for j in range(1, 4):
            u_ref[:, j * 512:(j + 1) * 512] = _nt(hb, win_ref[SEG[j][0]:SEG[j][1], :])
        cos, sin = _rope_block(cos_ref, sin_ref, is_ctx)
        cq = ulo[:, 0:QL]
        cqn = (cq * lax.rsqrt(jnp.mean(cq * cq, axis=-1, keepdims=True) + EPS) * qlg_ref[...]).astype(BF16)
        qng = qng_ref[...]
        ckv = ulo[:, QL:QL + KVL]
        ckvn = (ckv * lax.rsqrt(jnp.mean(ckv * ckv, axis=-1, keepdims=True) + EPS) * kvlg_ref[...]).astype(BF16)
        qhs = [_nt(cqn, wuq_ref[hd]) for hd in range(NH)]
        kvs = [_nn(ckvn, wukv_ref[hd]) for hd in range(NH)]
        for hd in range(NH):
            qh = qhs[hd]
            qn = qh * lax.rsqrt(_rowsum(qh * qh) / DK + EPS) * qng
            q_ref[hd] = (jnp.concatenate([qn[:, :128], _rope(qn[:, 128:], cos, sin)], axis=1) * (SCALE * LOG2E)).astype(BF16)
        kr = ulo[:, 384:512]
        skr = _rowsum(kr * kr)
        kng = kng_ref[...]
        kr_roped = _rope(kr * kng[:, 128:], cos, sin)
        for hd in range(NH):
            kv = kvs[hd]
            kn = kv[:, :128]
            rk = lax.rsqrt((_rowsum(kn * kn) + skr) / DK + EPS)
            k_ref[hd] = jnp.concatenate([kn * rk * kng[:, :128], kr_roped * rk], axis=1).astype(BF16)
            v_ref[hd] = kv[:, 128:].astype(BF16)

    row = lambda w: pl.BlockSpec((TB, w), lambda i: (i, 0))
    heads = lambda w: pl.BlockSpec((NH, TB, w), lambda i: (0, i, 0))
    cspec, xspec, mspec = _token_specs(off)
    return pl.pallas_call(
        body, name="fwd_in", grid=(nb,),
        in_specs=[cspec, xspec, mspec, _full((1, D)), _full((DIN, D)), _full((1, QL)), _full((NH, DKP, QL)), _full((1, KVL)),
                  _full((NH, KVL, 256)), _full((1, DKP)), _full((1, DKP)),
                  pl.BlockSpec((1, 8, 256), lambda i: (jnp.maximum(i - off, 0), 0, 0)), _full((TB, 256))],
        out_specs=[row(DU), heads(DKP), heads(DKP), heads(DV)],
        out_shape=[jax.ShapeDtypeStruct((t_all, DU), F32), jax.ShapeDtypeStruct((NH, t_all, DKP), BF16),
                   jax.ShapeDtypeStruct((NH, t_all, DKP), BF16), jax.ShapeDtypeStruct((NH, t_all, DV), BF16)],
        compiler_params=_params(("arbitrary",)),
    )(ctx, x, modsel, norm_g, w_in_t, q_lora_g, w_uq_t, kv_lora_g, w_ukv, qn_g, kn_g, cos, sin)


def _attn_fwd(q, k, v, s_len):
    t_all = q.shape[1]
    off = (t_all - s_len) // TB
    nq = s_len // TB
    nsub = next(n for n in (4, 2, 1) if nq % n == 0)

    def body(*refs):
        q_refs = refs[:nsub]
        k_ref, v_ref, o_ref, lse_ref = refs[nsub:]
        for sb in range(nsub):
            s = _nt(q_refs[sb][0], k_ref[0])
            m = jnp.max(s, axis=-1, keepdims=True)
            e = jnp.exp2(s - m)
            l = _rowsum(e)
            o_ref[sb * TB:(sb + 1) * TB, :] = _nn(e.astype(BF16), v_ref[0]) / l
            lse_ref[0, sb] = _row_layout(m + jnp.log2(l))

    qspec = lambda sb: pl.BlockSpec((1, TB, DKP), lambda h, i: (h, i * nsub + sb + off, 0))
    return pl.pallas_call(
        body, name="attn_fwd", grid=(NH, nq // nsub),
        in_specs=[qspec(sb) for sb in range(nsub)]
        + [pl.BlockSpec((1, t_all, DKP), lambda h, i: (h, 0, 0)), pl.BlockSpec((1, t_all, DV), lambda h, i: (h, 0, 0))],
        out_specs=[pl.BlockSpec((nsub * TB, DV), lambda h, i: (i, h)), pl.BlockSpec((1, nsub, 8, TB), lambda h, i: (h, i, 0, 0))],
        out_shape=[jax.ShapeDtypeStruct((s_len, NH * DV), F32), jax.ShapeDtypeStruct((NH, nq, 8, TB), F32)],
        compiler_params=_params(("arbitrary", "arbitrary")),
    )(*([q] * nsub), k, v)


def _out_stage(attn, u, x, target, gate, w_pool, pool_scale, w_out, lc):
    s_len = x.shape[0]
    t_all = s_len + lc
    off = lc // TB
    nq = s_len // TB
    hb = TB // HALO
    nqb = s_len // Q_BLOCK
    jb = TB // nqb

    def body(attn_ref, ga_ref, pin_ref, pprev_ref, pnext_ref, gp_ref, x_ref, tgt_ref, gate_ref, wp_ref, ps_ref, wo_ref,
             dxn_ref, dattn_ref, dga_ref, dgp_ref, dpool_ref, dwo_ref, dgate_ref, dps_ref, dwp_ref, loss_ref):
        i = pl.program_id(0)

        @pl.when(i == 0)
        def _():
            dwo_ref[...] = jnp.zeros_like(dwo_ref)
            dgate_ref[...] = jnp.zeros_like(dgate_ref)
            dps_ref[...] = jnp.zeros_like(dps_ref)
            dwp_ref[...] = jnp.zeros_like(dwp_ref)
            loss_ref[...] = jnp.zeros_like(loss_ref)

        attn = jnp.concatenate([attn_ref[:, jj, :] for jj in range(jb)], axis=0)
        ga = ga_ref[...]
        gp = gp_ref[...]
        pin = pin_ref[...]
        prev = jnp.where(i == 0, 0.0, pprev_ref[...])
        nxt = jnp.where(i == nq - 1, 0.0, pnext_ref[...])
        win = jnp.concatenate([prev, pin, nxt], axis=0)
        tg = i * TB + lax.broadcasted_iota(jnp.int32, (TB, 1), 0)
        pooled = []
        for g, w in enumerate(POOL_WINDOWS):
            a = win[:, g * 128:(g + 1) * 128]
            p = _shift_rows(a, -1) + a
            for step in (1, 2, 4):
                if w >= 4 * step:
                    p = _shift_rows(p, -step) + _shift_rows(p, step)
            cnt = (jnp.minimum(tg + w // 2, s_len) - jnp.maximum(tg - w // 2, 0)).astype(F32)
            pooled.append(p[HALO:HALO + TB] / cnt - a[HALO:HALO + TB])
        pooled_b = [p.astype(BF16) for p in pooled]
        z = jnp.concatenate([_nn(pooled_b[g], wp_ref[g]) for g in range(4)], axis=1)
        ps = ps_ref[...]
        yp = z * ps
        sga = _sig(ga)
        sila = ga * sga
        sgp = _sig(gp)
        silp = gp * sgp
        br = jnp.concatenate([sila * attn, silp * yp], axis=1).astype(BF16)
        y = _nn(br, wo_ref[...])
        gate = gate_ref[...]
        err = x_ref[...] + gate * y - tgt_ref[...]
        loss_ref[...] += _colsum(_rowsum(err * err)) * (0.5 / D)
        dxn = err * (1.0 / D)
        dxn_ref[...] = dxn
        dgate_ref[...] += _colsum(dxn * y)
        dy = (dxn * gate).astype(BF16)
        dwo_ref[...] += _tn(br, dy)
        dbr = _nt(dy, wo_ref[...])
        dbra = dbr[:, :512]
        dbrp = dbr[:, 512:]
        dattn = dbra * sila
        for jj in range(jb):
            dattn_ref[:, jj, :] = dattn[jj * nqb:(jj + 1) * nqb]
        dga_ref[...] = (dbra * attn * (sga * (1.0 + ga * (1.0 - sga)))).astype(BF16)
        dgp_ref[...] = (dbrp * yp * (sgp * (1.0 + gp * (1.0 - sgp)))).astype(BF16)
        dyp = dbrp * silp
        dps_ref[...] += _colsum(dyp * z)
        dz = (dyp * ps).astype(BF16)
        dpool = []
        for g in range(4):
            dzg = dz[:, g * 128:(g + 1) * 128]
            dwp_ref[g] += _tn(pooled_b[g], dzg)
            dpool.append(_nt(dzg, wp_ref[g]))
        dpool_ref[...] = jnp.concatenate(dpool, axis=1)

    lat = lambda w: pl.BlockSpec((TB, w), lambda i: (i, 0))
    perm = pl.BlockSpec((nqb, jb, 512), lambda i: (0, i, 0))
    ucol = lambda j: pl.BlockSpec((TB, 512), lambda i: (i + off, j))
    last8 = t_all // HALO - 1
    return pl.pallas_call(
        body, name="out_stage", grid=(nq,),
        in_specs=[perm, ucol(1), ucol(2),
                  pl.BlockSpec((HALO, 512), lambda i: ((i + off) * hb - 1, 2)),
                  pl.BlockSpec((HALO, 512), lambda i: (jnp.minimum((i + off + 1) * hb, last8), 2)),
                  ucol(3), lat(D), lat(D), _full((1, D)), _full((4, 128, 128)), _full((1, 512)), _full((D, D))],
        out_specs=[lat(D), perm, lat(512), lat(512), lat(512),
                   _full((D, D)), _full((1, D)), _full((1, 512)), _full((4, 128, 128)), _full((1, 1))],
        out_shape=[jax.ShapeDtypeStruct((s_len, D), F32), jax.ShapeDtypeStruct((nqb, Q_BLOCK, 512), F32),
                   jax.ShapeDtypeStruct((s_len, 512), BF16), jax.ShapeDtypeStruct((s_len, 512), BF16),
                   jax.ShapeDtypeStruct((s_len, 512), F32),
                   jax.ShapeDtypeStruct((D, D), F32), jax.ShapeDtypeStruct((1, D), F32), jax.ShapeDtypeStruct((1, 512), F32),
                   jax.ShapeDtypeStruct((4, 128, 128), F32), jax.ShapeDtypeStruct((1, 1), F32)],
        compiler_params=_params(("arbitrary",)),
    )(attn, u, u, u, u, u, x, target, gate, w_pool, pool_scale, w_out)


def _attn_bwd(q, k, v, dattn, attn, lse, s_len):
    t_all = q.shape[1]
    off = (t_all - s_len) // TB
    nq = s_len // TB
    nch = 4
    chunks = [(c * (t_all // nch), t_all // nch) for c in range(nch)]
    nsub = next(n for n in (BWD_QBLOCKS, 2, 1) if nq % n == 0)
    tq = nsub * TB

    def body(*refs):
        q_refs = refs[:nsub]
        k_ref, v_ref, do_ref, o_ref, lse_ref, dq_ref, dk_ref, dv_ref = refs[nsub:]
        i = pl.program_id(1)

        @pl.when(i == 0)
        def _():
            dk_ref[...] = jnp.zeros_like(dk_ref)
            dv_ref[...] = jnp.zeros_like(dv_ref)

        qb = jnp.concatenate([r[0] for r in q_refs], axis=0)
        delta_r = _row_layout(_rowsum(do_ref[...] * o_ref[...]))[0:1, :]
        do = do_ref[...].astype(BF16)
        lse_r = jnp.concatenate([lse_ref[0, sb][0:1, :] for sb in range(nsub)], axis=1)
        dq = jnp.zeros((tq, DKP), F32)
        for start, size in chunks:
            rows = pl.ds(start, size)
            kc = k_ref[0, rows, :]
            p_t = jnp.exp2(_nt(kc, qb) - lse_r)
            ds_t = (p_t * (_nt(v_ref[0, rows, :], do) - delta_r)).astype(BF16)
            dv_ref[0, rows, :] += _nn(p_t.astype(BF16), do)
            dk_ref[0, rows, :] += _nn(ds_t, qb)
            dq += _tn(ds_t, kc)
        dq_ref[0] = dq * SCALE

    kvspec = lambda w: pl.BlockSpec((1, t_all, w), lambda h, i: (h, 0, 0))
    rowspec = pl.BlockSpec((1, nsub, 8, TB), lambda h, i: (h, i, 0, 0))
    qspec = lambda sb: pl.BlockSpec((1, TB, DKP), lambda h, i: (h, i * nsub + sb + off, 0))
    return pl.pallas_call(
        body, name="attn_bwd", grid=(NH, nq // nsub),
        in_specs=[qspec(sb) for sb in range(nsub)]
        + [kvspec(DKP), kvspec(DV), pl.BlockSpec((tq, DV), lambda h, i: (i, h)), pl.BlockSpec((tq, DV), lambda h, i: (i, h)),
           rowspec],
        out_specs=[pl.BlockSpec((1, tq, DKP), lambda h, i: (h, i, 0)), kvspec(DKP), kvspec(DV)],
        out_shape=[jax.ShapeDtypeStruct((NH, s_len, DKP), F32), jax.ShapeDtypeStruct((NH, t_all, DKP), F32),
                   jax.ShapeDtypeStruct((NH, t_all, DV), F32)],
        compiler_params=_params(("arbitrary", "arbitrary")),
    )(*([q] * nsub), k, v, dattn, attn, lse)


def _qkv_bwd(u, dq, dk, dv, cos, sin, q_lora_g, w_uq_t, kv_lora_g, w_ukv, qn_g, kn_g, s_len):
    t_all = u.shape[0]
    off = (t_all - s_len) // TB
    nb = t_all // TB

    def body(ulo_ref, dq_ref, dk_ref, dv_ref, cos_ref, sin_ref, qlg_ref, wuq_ref, kvlg_ref, wukv_ref, qng_ref, kng_ref,
             dlo_ref, dwuq_ref, dwukv_ref, dqlg_ref, dkvlg_ref, dqng_ref, dkng_ref):
        i = pl.program_id(0)

        @pl.when(i == 0)
        def _():
            for r in (dwuq_ref, dwukv_ref, dqlg_ref, dkvlg_ref, dqng_ref, dkng_ref):
                r[...] = jnp.zeros_like(r)

        latent = i >= off
        ulo = ulo_ref[...]
        cos, sin = _rope_block(cos_ref, sin_ref, pl.program_id(0) < off)
        cq = ulo[:, 0:QL]
        rc = lax.rsqrt(jnp.mean(cq * cq, axis=-1, keepdims=True) + EPS)
        cqh = cq * rc
        qlg = qlg_ref[...]
        cqn_b = (cqh * qlg).astype(BF16)
        qng = qng_ref[...]
        ckv = ulo[:, QL:QL + KVL]
        r0 = lax.rsqrt(jnp.mean(ckv * ckv, axis=-1, keepdims=True) + EPS)
        ckvh = ckv * r0
        kvlg = kvlg_ref[...]
        ckvn_b = (ckvh * kvlg).astype(BF16)
        qhs = [_nt(cqn_b, wuq_ref[hd]) for hd in range(NH)]
        kns = [_nn(ckvn_b, wukv_ref[hd])[:, :128] for hd in range(NH)]
        dqng = jnp.zeros((1, DKP), F32)
        dqraws = []
        for hd in range(NH):
            qh = qhs[hd]
            rq = lax.rsqrt(_rowsum(qh * qh) / DK + EPS)
            xh = qh * rq
            dqh = jnp.where(latent, dq_ref[hd], 0.0)
            dyq = jnp.concatenate([dqh[:, :128], _rope_t(dqh[:, 128:], cos, sin)], axis=1)
            dqng += _colsum(dyq * xh)
            dxh = dyq * qng
            dqraws.append((rq * (dxh - xh * (_rowsum(dxh * xh) / DK))).astype(BF16))
        dqng_ref[...] += dqng

        kr = ulo[:, 384:512]
        skr = _rowsum(kr * kr)
        kng = kng_ref[...]
        dkr = jnp.zeros((TB, 128), F32)
        dkng = jnp.zeros((1, DKP), F32)
        dkvs = []
        for hd in range(NH):
            kn = kns[hd]
            rk = lax.rsqrt((_rowsum(kn * kn) + skr) / DK + EPS)
            xh1 = kn * rk
            xh2 = kr * rk
            dkh = dk_ref[hd] * LN2
            d1 = dkh[:, :128]
            d2 = _rope_t(dkh[:, 128:], cos, sin)
            dkng += jnp.concatenate([_colsum(d1 * xh1), _colsum(d2 * xh2)], axis=1)
            dx1 = d1 * kng[:, :128]
            dx2 = d2 * kng[:, 128:]
            dot = (_rowsum(dx1 * xh1) + _rowsum(dx2 * xh2)) / DK
            dkvs.append(jnp.concatenate([rk * (dx1 - xh1 * dot), dv_ref[hd]], axis=1).astype(BF16))
            dkr += rk * (dx2 - xh2 * dot)
        dkng_ref[...] += dkng

        dcqn = jnp.zeros((TB, QL), F32)
        dckvn = jnp.zeros((TB, KVL), F32)
        for hd in range(NH):
            dwuq_ref[hd] += _tn(dqraws[hd], cqn_b)[:DK]
            dcqn += _nn(dqraws[hd], wuq_ref[hd])
            dwukv_ref[hd] += _tn(ckvn_b, dkvs[hd])
            dckvn += _nt(dkvs[hd], wukv_ref[hd])
        dqlg_ref[...] += _colsum(dcqn * cqh)
        dxh = dcqn * qlg
        dcq = rc * (dxh - cqh * jnp.mean(dxh * cqh, axis=-1, keepdims=True))
        dkvlg_ref[...] += _colsum(dckvn * ckvh)
        dxh = dckvn * kvlg
        dckv = r0 * (dxh - ckvh * jnp.mean(dxh * ckvh, axis=-1, keepdims=True))
        dlo_ref[...] = jnp.concatenate([dcq, dckv, dkr], axis=1).astype(BF16)

    row = lambda w: pl.BlockSpec((TB, w), lambda i: (i, 0))
    heads = lambda w: pl.BlockSpec((NH, TB, w), lambda i: (0, i, 0))
    return pl.pallas_call(
        body, name="qkv_bwd", grid=(nb,),
        in_specs=[row(512), pl.BlockSpec((NH, TB, DKP), lambda i: (0, jnp.maximum(i - off, 0), 0)), heads(DKP), heads(DV),
                  pl.BlockSpec((1, 8, 256), lambda i: (jnp.maximum(i - off, 0), 0, 0)), _full((TB, 256)), _full((1, QL)), _full((NH, DKP, QL)), _full((1, KVL)), _full((NH, KVL, 256)),
                  _full((1, DKP)), _full((1, DKP))],
        out_specs=[row(512), _full((NH, DK, QL)), _full((NH, KVL, 256)), _full((1, QL)), _full((1, KVL)),
                   _full((1, DKP)), _full((1, DKP))],
        out_shape=[jax.ShapeDtypeStruct((t_all, 512), BF16), jax.ShapeDtypeStruct((NH, DK, QL), F32),
                   jax.ShapeDtypeStruct((NH, KVL, 256), F32), jax.ShapeDtypeStruct((1, QL), F32),
                   jax.ShapeDtypeStruct((1, KVL), F32), jax.ShapeDtypeStruct((1, DKP), F32), jax.ShapeDtypeStruct((1, DKP), F32)],
        compiler_params=_params(("arbitrary",)),
    )(u, dq, dk, dv, cos, sin, q_lora_g, w_uq_t, kv_lora_g, w_ukv, qn_g, kn_g)


def _in_bwd(ctx, x, modsel, norm_g, dlo, dga, dgp, dpool, dxn, w_in_t):
    s_len, lc = x.shape[0], ctx.shape[0]
    t_all = s_len + lc
    off = lc // TB
    nb = t_all // TB
    nq = s_len // TB
    hb = TB // HALO
    n = TB + 2 * HALO

    def body(ctx_ref, x_ref, mod_ref, ng_ref, dlo_ref, dga_ref, dgp_ref, dp_ref, dpprev_ref, dpnext_ref, dxn_ref, win_ref,
             gx_ref, dwin_ref, dmod_ref, dng_ref):
        i = pl.program_id(0)
        j = i - off

        @pl.when(i == 0)
        def _():
            dwin_ref[...] = jnp.zeros_like(dwin_ref)
            dmod_ref[...] = jnp.zeros_like(dmod_ref)
            dng_ref[...] = jnp.zeros_like(dng_ref)

        latent = i >= off
        dp = dp_ref[...]
        prev = jnp.where(j <= 0, 0.0, dpprev_ref[...])
        nxt = jnp.where(j >= nq - 1, 0.0, dpnext_ref[...])
        win = jnp.concatenate([prev, dp, nxt], axis=0)
        tg = j * TB - HALO + lax.broadcasted_iota(jnp.int32, (n, 1), 0)
        dpin = []
        for g, w in enumerate(POOL_WINDOWS):
            cnt = jnp.maximum(jnp.minimum(tg + w // 2, s_len) - jnp.maximum(tg - w // 2, 0), 1).astype(F32)
            zq = win[:, g * 128:(g + 1) * 128] / cnt
            zq = zq + _shift_rows(zq, 1)
            for step in (1, 2, 4):
                if w >= 4 * step:
                    zq = _shift_rows(zq, -step) + _shift_rows(zq, step)
            dpin.append(zq[HALO:HALO + TB] - dp[:, g * 128:(g + 1) * 128])
        zero = jnp.zeros((TB, 512), BF16)
        du = [dlo_ref[...], jnp.where(latent, dga_ref[...], zero),
              jnp.where(latent, jnp.concatenate(dpin, axis=1).astype(BF16), zero), jnp.where(latent, dgp_ref[...], zero)]

        ng = ng_ref[...]
        xb = jnp.where(i < off, ctx_ref[...], x_ref[...])
        r, xh, xg, h, scale = _modulated(xb, mod_ref, ng)
        hb_ = h.astype(BF16)
        dh = jnp.zeros((TB, D), F32)
        for s, (lo, hi) in enumerate(SEG):
            dwin_ref[lo:hi, :] += _tn(du[s], hb_)
            dh += _nn(du[s], win_ref[lo:hi, :])
        is_lat = latent.astype(F32)
        dsh = _colsum(dh)
        dsc = _colsum(dh * xg)
        dmod_ref[0, 0:1, :] += dsh * (1.0 - is_lat)
        dmod_ref[0, 1:2, :] += dsc * (1.0 - is_lat)
        dmod_ref[1, 0:1, :] += dsh * is_lat
        dmod_ref[1, 1:2, :] += dsc * is_lat
        dxg = dh * (1.0 + scale)
        dng_ref[...] += _colsum(dxg * xh)
        dxh = dxg * ng
        gx_ref[...] = r * (dxh - xh * jnp.mean(dxh * xh, axis=-1, keepdims=True)) + dxn_ref[...]

    row = lambda w: pl.BlockSpec((TB, w), lambda i: (i, 0))
    lat = lambda w: pl.BlockSpec((TB, w), lambda i: (jnp.maximum(i - off, 0), 0))
    last8 = s_len // HALO - 1
    cspec, xspec, mspec = _token_specs(off)
    return pl.pallas_call(
        body, name="in_bwd", grid=(nb,),
        in_specs=[cspec, xspec, mspec, _full((1, D)), row(512), lat(512), lat(512), lat(512),
                  pl.BlockSpec((HALO, 512), lambda i: (jnp.maximum(jnp.maximum(i - off, 0) * hb - 1, 0), 0)),
                  pl.BlockSpec((HALO, 512), lambda i: (jnp.minimum((jnp.maximum(i - off, 0) + 1) * hb, last8), 0)),
                  lat(D), _full((DIN, D))],
        out_specs=[lat(D), _full((DIN, D)), _full((2, 2, D)), _full((1, D))],
        out_shape=[jax.ShapeDtypeStruct((s_len, D), F32), jax.ShapeDtypeStruct((DIN, D), F32),
                   jax.ShapeDtypeStruct((2, 2, D), F32), jax.ShapeDtypeStruct((1, D), F32)],
        compiler_params=_params(("arbitrary",)),
    )(ctx, x, modsel, norm_g, dlo, dga, dgp, dpool, dpool, dpool, dxn, w_in_t)


def _adamw_update(w_ref, g_ref, m_ref, v_ref, d_ref, mo_ref, vo_ref):
    gv = g_ref[...]
    mn = ADAM_B1 * m_ref[...] + (1.0 - ADAM_B1) * gv
    vn = ADAM_B2 * v_ref[...] + (1.0 - ADAM_B2) * (gv * gv)
    m_hat = mn / (1.0 - ADAM_B1 ** ADAM_STEP)
    v_hat = vn / (1.0 - ADAM_B2 ** ADAM_STEP)
    d_ref[...] = -ADAM_LR * (m_hat / (jnp.sqrt(v_hat) + ADAM_EPS) + ADAM_WD * w_ref[...])
    mo_ref[...] = mn
    vo_ref[...] = vn


def _adamw_many(ws, gs, ms, vs):
    n = len(ws)
    parts = 4

    def body(*refs):
        for i in range(n):
            _adamw_update(refs[i], refs[n + i], refs[2 * n + i], refs[3 * n + i], refs[4 * n + i], refs[5 * n + i], refs[6 * n + i])
            refs[7 * n + i][...] = refs[n + i][...]

    def spec(w):
        rows, cols = w.shape
        if rows % (8 * parts) == 0:
            return pl.BlockSpec((rows // parts, cols), lambda i: (i, 0))
        if cols % (128 * parts) == 0:
            return pl.BlockSpec((rows, cols // parts), lambda i: (0, i))
        return _full((rows, cols))

    specs = [spec(w) for w in ws]
    shp = [jax.ShapeDtypeStruct(w.shape, F32) for w in ws]
    out = pl.pallas_call(body, name="adamw_many", grid=(parts,), in_specs=specs * 4, out_specs=specs * 4, out_shape=shp * 4,
                         compiler_params=_params(("arbitrary",)))(*ws, *gs, *ms, *vs)
    return out[:n], out[n:2 * n], out[2 * n:3 * n], out[3 * n:]


class _Links:
    def __init__(self, send_sems, recv_sems):
        self.send_sems, self.recv_sems, self.sends = send_sems, recv_sems, []

    def send(self, src, dst, sem, to):
        cp = pltpu.make_async_remote_copy(src, dst, self.send_sems.at[sem], self.recv_sems.at[sem], device_id=to,
                                          device_id_type=MESH)
        cp.start()
        self.sends.append(cp)

    def arrived(self, dst, sem, frm):
        pltpu.make_async_remote_copy(dst, dst, self.send_sems.at[sem], self.recv_sems.at[sem], device_id=frm,
                                     device_id_type=MESH).wait_recv()

    def drain(self):
        for cp in self.sends:
            cp.wait_send()


def _half(ref, c, axis):
    size = ref.shape[axis - 2] // 2
    win = pl.ds(pl.multiple_of(c * size, 16 if axis == 0 else 128), size)
    idx = (win, slice(None)) if axis == 0 else (slice(None), win)
    return ref.at[(slice(None),) * (len(ref.shape) - 2) + idx]


def _select_rows(slots_ref, n_slots, row=0):
    sub = lax.broadcasted_iota(jnp.int32, (8, 1), 0)
    out = None
    for d in range(n_slots):
        r = jnp.where(sub == d, jnp.broadcast_to(slots_ref[d][row:row + 1, :], (8, slots_ref.shape[-1])), 0.0)
        out = r if out is None else out + r
    return out


def _gather(c, c_ctx, w_mod, b_mod_k, shards, axes, slab_rows):
    nw = len(shards)
    kw = w_mod.shape[1]

    def body(*refs):
        c_ref, cc_ref, wm_ref, b_ref = refs[:4]
        w_refs = refs[4:4 + nw]
        a16_ref, mod_ref = refs[4 + nw:6 + nw]
        g_refs = refs[6 + nw:6 + 2 * nw]
        a_ref, send_sems, recv_sems = refs[6 + 2 * nw:]

        def slab(wi, chip):
            return g_refs[wi].at[chip].at[0:shards[wi].shape[0]]
        x, y, cc = lax.axis_index("x"), lax.axis_index("y"), lax.axis_index("c")
        me = 4 * x + 2 * y + cc
        k = 2 * x + y
        sibling = (x, y, 1 - cc)
        links = _Links(send_sems, recv_sems)
        chips = [_peer(x, y, cc, off + (0,)) for off in CHIPS3]
        chip_a = ((x + 1 - cc) % 2, (y + cc) % 2, cc)
        chip_b = ((x + cc) % 2, (y + 1 - cc) % 2, cc)
        chip_d = (1 - x, 1 - y, cc)
        cv = c_ref[...]
        sc = cv * _sig(cv)
        mine = a_ref.at[me]
        for r in range(8):
            mine[r:r + 1, :] = sc[:, r * 128:(r + 1) * 128]
        for j, off in enumerate(PEERS7):
            links.send(a_ref.at[me], a_ref.at[me], j, _peer(x, y, cc, off))
        for wi in range(nw):
            slab(wi, k)[...] = w_refs[wi][...].astype(BF16)
            for j, to in enumerate((chip_a, chip_b)):
                links.send(_half(slab(wi, k), cc, axes[wi]), _half(slab(wi, k), cc, axes[wi]), 10 + wi * 6 + j, to)
            pad = slab_rows[wi] - shards[wi].shape[0]
            if pad:
                for kk in range(4):
                    g_refs[wi][kk, shards[wi].shape[0]:, :] = jnp.zeros((pad, shards[wi].shape[1]), BF16)
        for j, off in enumerate(PEERS7):
            px, py, pc = _peer(x, y, cc, off)
            links.arrived(a_ref.at[4 * px + 2 * py + pc], j, (px, py, pc))
        ccv = cc_ref[...]
        sub = lax.broadcasted_iota(jnp.int32, (8, 1), 0)
        top = jnp.zeros((8, D), F32)
        for d in range(8):
            blk = a_ref[d]
            row = jnp.concatenate([blk[r:r + 1, :] for r in range(8)], axis=1)
            top = top + jnp.where(sub == d, jnp.broadcast_to(row, (8, D)), 0.0)
        a16 = jnp.concatenate([top, jnp.where(sub == 0, jnp.broadcast_to(ccv * _sig(ccv), (8, D)), 0.0)], axis=0)
        a16_ref[...] = a16
        mod_ref[k] = _dot3(_nn, a16, wm_ref[...]) + b_ref[...]
        for j, to in enumerate(chips):
            links.send(mod_ref.at[k], mod_ref.at[k], 7 + j, to)
        for j, (frm, origin) in enumerate(((chip_a, chip_a), (chip_b, chip_b), (chip_b, chip_d))):
            for wi in range(nw):
                blk = _half(slab(wi, 2 * origin[0] + origin[1]), cc, axes[wi])
                links.arrived(blk, 10 + wi * 6 + j, frm)
                if j == 0:
                    links.send(blk, blk, 10 + wi * 6 + 2, chip_b)
                links.send(blk, blk, 10 + wi * 6 + 3 + j, sibling)
        for j, (px, py, pc) in enumerate(chips):
            links.arrived(mod_ref.at[2 * px + py], 7 + j, (px, py, pc))
        for j, origin in enumerate((chip_b, chip_a, chip_d)):
            for wi in range(nw):
                links.arrived(_half(slab(wi, 2 * origin[0] + origin[1]), 1 - cc, axes[wi]), 10 + wi * 6 + 3 + j, sibling)
        links.drain()

    nsem = 10 + 6 * nw
    return pl.pallas_call(
        body, name="gather", in_specs=[VM] * (4 + nw), out_specs=[VM] * (2 + nw),
        out_shape=[jax.ShapeDtypeStruct((16, D), F32), jax.ShapeDtypeStruct((4, 16, kw), F32)]
        + [jax.ShapeDtypeStruct((4, r, s.shape[1]), BF16) for r, s in zip(slab_rows, shards)],
        scratch_shapes=[pltpu.VMEM((8, 8, D // 8), F32), pltpu.SemaphoreType.DMA((nsem,)), pltpu.SemaphoreType.DMA((nsem,))],
        compiler_params=pltpu.CompilerParams(vmem_limit_bytes=VMEM_LIMIT),
    )(c, c_ctx, w_mod, b_mod_k, *shards)


SMALL_ROW_WIDTHS = (D, QL, KVL, DKP, DKP, 512, 128)
SMALL_OUT_WIDTHS = (D, QL, KVL, DK, DK, 512, 1)
SMALL_PACK = 384


def _small_pieces():
    pieces = []
    for i, w in enumerate(SMALL_ROW_WIDTHS):
        for c0 in range(0, w, 128):
            j = len(pieces)
            pieces.append((i, c0, j // (SMALL_PACK // 128), j % (SMALL_PACK // 128) * 128))
    assert len(pieces) <= 8 * (SMALL_PACK // 128)
    return pieces


def _reduce(grads, axes, smalls, w_pool_g, dmod8, a16, w_mod, c_ctx):
    nw = len(grads)
    ns = len(smalls)
    kw = w_mod.shape[1]
    halves = []
    for g, ax in zip(grads, axes):
        halves.append((g.shape[1] // 2, g.shape[2]) if ax == 0 else (g.shape[1], g.shape[2] // 2))

    def body(*refs):
        g_refs = refs[:nw]
        small_refs = refs[nw:nw + ns]
        wp_ref, dm_ref, a16_ref, wm_ref, cc_ref = refs[nw + ns:nw + ns + 5]
        o = nw + ns + 5
        r_refs = refs[o:o + nw]
        small_outs = refs[o + nw:o + nw + ns]
        rwp_ref, gw_ref, gb_ref, gc_ref = refs[o + nw + ns:o + nw + ns + 4]
        o = o + nw + ns + 4
        own, sib, part, got, rel = (refs[o + i * nw:o + (i + 1) * nw] for i in range(5))
        smbuf, wps, wpg, dm_all, pc_all, send_sems, recv_sems, local_sems = refs[o + 5 * nw:]
        x, y, cc = lax.axis_index("x"), lax.axis_index("y"), lax.axis_index("c")
        me = 4 * x + 2 * y + cc
        k = 2 * x + y
        sibling = (x, y, 1 - cc)
        links = _Links(send_sems, recv_sems)
        chips = [_peer(x, y, cc, off + (0,)) for off in CHIPS3]
        peers = [_peer(x, y, cc, off) for off in PEERS7]
        chip_a = ((x + 1 - cc) % 2, (y + cc) % 2, cc)
        chip_b = ((x + cc) % 2, (y + 1 - cc) % 2, cc)
        ka, kb, kd = 2 * chip_a[0] + chip_a[1], 2 * chip_b[0] + chip_b[1], 2 * (1 - x) + (1 - y)
        big, sm0, wp0, dm0, pc0 = 0, 5 * nw, 5 * nw + 7, 5 * nw + 14, 5 * nw + 21

        locals_ = []
        for wi in range(nw):
            lc = pltpu.make_async_copy(_half(g_refs[wi], cc, axes[wi]), own[wi], local_sems.at[wi])
            lc.start()
            locals_.append(lc)
            links.send(_half(g_refs[wi], 1 - cc, axes[wi]), sib[wi], big + wi * 5, sibling)
        slot = smbuf.at[me]
        slot[...] = jnp.zeros((8, SMALL_PACK), F32)
        small_rows = [jnp.broadcast_to(ref[...], (1, w)) for ref, w in zip(small_refs, SMALL_ROW_WIDTHS)]
        for i, c0, row, lane in _small_pieces():
            slot[row:row + 1, lane:lane + 128] = small_rows[i][:, c0:c0 + 128]
        links.send(wp_ref, wps, wp0, sibling)
        dm_all[me] = dm_ref[...]
        for j, peer in enumerate(peers):
            links.send(dm_all.at[me], dm_all.at[me], dm0 + j, peer)
            links.send(smbuf.at[me], smbuf.at[me], sm0 + j, peer)
        links.arrived(wps, wp0, sibling)
        wpg[k] = (wp_ref[...] + wps[...]).astype(BF16)
        for j, to in enumerate(chips):
            links.send(wpg.at[k], wpg.at[k], wp0 + 1 + j, to)
        for wi in range(nw):
            locals_[wi].wait()
            links.arrived(sib[wi], big + wi * 5, sibling)
            part[wi][...] = (own[wi][...] + sib[wi][...]).astype(BF16)
            got[wi][k] = part[wi][k]
            got[wi][kd] = jnp.zeros(halves[wi], BF16)
            links.send(part[wi].at[kd], rel[wi], big + wi * 5 + 1, chip_b)
            links.send(part[wi].at[kb], got[wi].at[k], big + wi * 5 + 2, chip_b)
        for j, (px, py, pc) in enumerate(peers):
            links.arrived(dm_all.at[4 * px + 2 * py + pc], dm0 + j, (px, py, pc))
        dm_tot = dm_all[0]
        for d in range(1, 8):
            dm_tot = dm_tot + dm_all[d]
        for kk in range(4):
            gb_ref[:, kk * kw:(kk + 1) * kw] = dm_tot[kk:kk + 1, :] + dm_tot[4 + kk:5 + kk, :]
        top = jnp.zeros((8, kw), F32)
        dmc_k = jnp.zeros((1, kw), F32)
        for kk in range(4):
            top = top + jnp.where(k == kk, _select_rows(dm_all, 8, kk), 0.0)
            dmc_k = dmc_k + jnp.where(k == kk, dm_tot[4 + kk:5 + kk, :], 0.0)
        sub = lax.broadcasted_iota(jnp.int32, (8, 1), 0)
        bk = jnp.concatenate([top, jnp.where(sub == 0, jnp.broadcast_to(dmc_k, (8, kw)), 0.0)], axis=0)
        gw_ref[...] = _dot3(_tn, a16_ref[...], bk)
        pc_all[k] = _dot3(_nt, jnp.broadcast_to(bk[8:9, :], (8, kw)), wm_ref[...])
        for j, to in enumerate(chips):
            links.send(pc_all.at[k], pc_all.at[k], pc0 + j, to)
        for j, (px, py, pc) in enumerate(peers):
            links.arrived(smbuf.at[4 * px + 2 * py + pc], sm0 + j, (px, py, pc))
        tot = smbuf[0]
        for d in range(1, 8):
            tot = tot + smbuf[d]
        for i, c0, row, lane in _small_pieces():
            n = min(128, SMALL_OUT_WIDTHS[i] - c0)
            if n > 0:
                small_outs[i][:, c0:c0 + n] = tot[row:row + 1, lane:lane + n]
        for j, (px, py, pc) in enumerate(chips):
            links.arrived(wpg.at[2 * px + py], wp0 + 1 + j, (px, py, pc))
        wpt = wpg[0].astype(F32)
        for kk in range(1, 4):
            wpt = wpt + wpg[kk].astype(F32)
        rwp_ref[...] = wpt
        for wi in range(nw):
            links.arrived(rel[wi], big + wi * 5 + 1, chip_b)
            rel[wi][...] = (part[wi][ka].astype(F32) + rel[wi][...].astype(F32)).astype(BF16)
            links.send(rel[wi], got[wi].at[k], big + wi * 5 + 3, chip_a)
        for wi in range(nw):
            links.arrived(got[wi].at[kb], big + wi * 5 + 2, chip_b)
            links.arrived(got[wi].at[ka], big + wi * 5 + 3, chip_a)
            total = got[wi][0].astype(F32)
            for kk in range(1, 4):
                total = total + got[wi][kk].astype(F32)
            mine = _half(r_refs[wi], cc, axes[wi])
            mine[...] = total
            links.send(mine, mine, big + wi * 5 + 4, sibling)
        for j, (px, py, pc) in enumerate(chips):
            links.arrived(pc_all.at[2 * px + py], pc0 + j, (px, py, pc))
        ccv = cc_ref[...]
        sg = _sig(ccv)
        gc_ref[...] = (pc_all[0][0:1, :] + pc_all[1][0:1, :] + pc_all[2][0:1, :] + pc_all[3][0:1, :]) * (sg * (1.0 + ccv * (1.0 - sg)))
        for wi in range(nw):
            links.arrived(_half(r_refs[wi], 1 - cc, axes[wi]), big + wi * 5 + 4, sibling)
        links.drain()

    nsem = 5 * nw + 24
    quads = [(4,) + h for h in halves]
    return pl.pallas_call(
        body, name="reduce", in_specs=[ANY] * nw + [VM] * (ns + 5), out_specs=[VM] * (nw + ns + 4),
        out_shape=[jax.ShapeDtypeStruct(g.shape[1:], F32) for g in grads]
        + [jax.ShapeDtypeStruct((1, w), F32) for w in SMALL_OUT_WIDTHS]
        + [jax.ShapeDtypeStruct(w_pool_g.shape, F32), jax.ShapeDtypeStruct((D, kw), F32), jax.ShapeDtypeStruct((1, 3 * D), F32),
           jax.ShapeDtypeStruct((1, D), F32)],
        scratch_shapes=[pltpu.VMEM(q, F32) for q in quads] + [pltpu.VMEM(q, F32) for q in quads]
        + [pltpu.VMEM(q, BF16) for q in quads] + [pltpu.VMEM(q, BF16) for q in quads] + [pltpu.VMEM(h, BF16) for h in halves]
        + [pltpu.VMEM((8, 8, SMALL_PACK), F32), pltpu.VMEM(w_pool_g.shape, F32), pltpu.VMEM((4,) + w_pool_g.shape, BF16),
           pltpu.VMEM((8, 8, kw), F32), pltpu.VMEM((4, 8, D), F32)]
        + [pltpu.SemaphoreType.DMA((nsem,)), pltpu.SemaphoreType.DMA((nsem,)), pltpu.SemaphoreType.DMA((nw,))],
        compiler_params=pltpu.CompilerParams(vmem_limit_bytes=VMEM_LIMIT),
    )(*grads, *smalls, w_pool_g, dmod8, a16, w_mod, c_ctx)


def _rope_tables(s_len):
    rows = s_len // GRID_W
    per = TB // GRID_W
    n_freq = 16
    inv = ROPE_BASE ** (-jnp.arange(n_freq, dtype=F32) / n_freq)
    ang_r = jnp.arange(rows, dtype=F32)[:, None] * inv
    ang_c = jnp.arange(GRID_W, dtype=F32)[:, None] * inv
    by_row, by_col = [], []
    for fn, pad in ((jnp.cos, 1.0), (jnp.sin, 0.0)):
        r = jnp.concatenate([fn(ang_r), fn(ang_r), jnp.zeros((rows, 96), F32)], axis=1).reshape(rows // per, per, 128)
        by_row.append(jnp.pad(r, ((0, 0), (0, 8 - per), (0, 0))))
        cpart = jnp.concatenate([jnp.zeros((GRID_W, 32), F32), fn(ang_c), fn(ang_c), jnp.full((GRID_W, 64), pad, F32)], axis=1)
        by_col.append(jnp.tile(cpart, (per, 1)))
    return jnp.concatenate(by_row, axis=-1), jnp.concatenate(by_col, axis=-1)


def kernel(x, c, ctx, c_ctx, w_mod, b_mod, norm_g, w_in, q_lora_g, w_uq, kv_lora_g, w_ukv, q_norm_g, k_norm_g, w_pool, pool_scale, w_out, loss_target, m_c_ctx, m_w_mod, m_b_mod, m_norm_g, m_w_in, m_q_lora_g, m_w_uq, m_kv_lora_g, m_w_ukv, m_q_norm_g, m_k_norm_g, m_w_pool, m_pool_scale, m_w_out, v_c_ctx, v_w_mod, v_b_mod, v_norm_g, v_w_in, v_q_lora_g, v_w_uq, v_kv_lora_g, v_w_ukv, v_q_norm_g, v_k_norm_g, v_w_pool, v_pool_scale, v_w_out):
    xi, yi, ci = lax.axis_index("x"), lax.axis_index("y"), lax.axis_index("c")
    me = 4 * xi + 2 * yi + ci
    k = 2 * xi + yi
    s_len = x.shape[1]
    lc = ctx.shape[1]
    kw = w_mod.shape[2]
    weights = dict(c_ctx=c_ctx, w_mod=w_mod, b_mod=b_mod, norm_g=norm_g, w_in=w_in, q_lora_g=q_lora_g, w_uq=w_uq,
                   kv_lora_g=kv_lora_g, w_ukv=w_ukv, q_norm_g=q_norm_g, k_norm_g=k_norm_g, w_pool=w_pool,
                   pool_scale=pool_scale, w_out=w_out)
    m_in = dict(c_ctx=m_c_ctx, w_mod=m_w_mod, b_mod=m_b_mod, norm_g=m_norm_g, w_in=m_w_in, q_lora_g=m_q_lora_g, w_uq=m_w_uq,
                kv_lora_g=m_kv_lora_g, w_ukv=m_w_ukv, q_norm_g=m_q_norm_g, k_norm_g=m_k_norm_g, w_pool=m_w_pool,
                pool_scale=m_pool_scale, w_out=m_w_out)
    v_in = dict(c_ctx=v_c_ctx, w_mod=v_w_mod, b_mod=v_b_mod, norm_g=v_norm_g, w_in=v_w_in, q_lora_g=v_q_lora_g, w_uq=v_w_uq,
                kv_lora_g=v_kv_lora_g, w_ukv=v_w_ukv, q_norm_g=v_q_norm_g, k_norm_g=v_k_norm_g, w_pool=v_w_pool,
                pool_scale=v_pool_scale, w_out=v_w_out)
    order = ["c_ctx", "w_mod", "b_mod", "norm_g", "w_in", "q_lora_g", "w_uq", "kv_lora_g", "w_ukv", "q_norm_g", "k_norm_g",
             "w_pool", "pool_scale", "w_out"]
    transposed = ("w_in", "w_uq")
    as2d = lambda n, a: jnp.transpose(a[0]) if n in transposed else a.reshape(-1, a.shape[-1])
    back = lambda n, a: jnp.transpose(a)[None] if n in transposed else a.reshape(weights[n].shape)

    c_ctx2 = c_ctx.reshape(1, D)
    b_mod_k = lax.dynamic_slice(b_mod, (0, k * kw), (1, kw))
    split = (1, 0, 0, 0)
    a16, mod_all, g_in, g_uq, g_ukv, g_out = _gather(
        c, c_ctx2, w_mod[0], b_mod_k, [as2d("w_in", w_in), as2d("w_uq", w_uq), w_ukv[0], w_out[0]], split,
        (DIN // 4, DKP, KVL, D // 4))
    mod_me = lax.dynamic_index_in_dim(mod_all, me, axis=1, keepdims=False).reshape(3, D)
    mod_c = mod_all[:, 8, :].reshape(3, D)
    modsel = jnp.stack([mod_c, mod_me])
    w_in_t = g_in.reshape(DIN, D)
    w_uq_t = g_uq
    w_out_f = g_out.reshape(D, D)
    qn_g = jnp.pad(q_norm_g, ((0, 0), (0, DKP - DK)))
    kn_g = jnp.pad(k_norm_g, ((0, 0), (0, DKP - DK)))
    w_pool_b = w_pool[0].astype(BF16)
    cos, sin = _rope_tables(s_len)

    u, q, kk, v = _fwd_in(ctx[0], x[0], modsel, norm_g, w_in_t, q_lora_g, w_uq_t, kv_lora_g, g_ukv, qn_g, kn_g, cos, sin)
    attn, lse = _attn_fwd(q, kk, v, s_len)
    (dxn, dattn, dga, dgp, dpool, dw_out, dgate, dps, dw_pool, loss) = _out_stage(
        attn.reshape(s_len // Q_BLOCK, Q_BLOCK, NH * DV), u, x[0], loss_target[0], modsel[1, 2:3, :], w_pool_b, pool_scale,
        w_out_f, lc)
    dattn = dattn.reshape(s_len, NH * DV)
    dq, dk, dv = _attn_bwd(q, kk, v, dattn, attn, lse, s_len)
    dlo, dw_uq_t, dw_ukv, dqlg, dkvlg, dqng, dkng = _qkv_bwd(u, dq, dk, dv, cos, sin, q_lora_g, w_uq_t, kv_lora_g, g_ukv,
                                                            qn_g, kn_g, s_len)
    gx, dw_in_t, dmod, dng = _in_bwd(ctx[0], x[0], modsel, norm_g, dlo, dga, dgp, dpool, dxn, w_in_t)

    dmod_l = jnp.concatenate([dmod[1, 0], dmod[1, 1], dgate[0]]).reshape(4, kw)
    dmod_c = jnp.concatenate([dmod[0, 0], dmod[0, 1], jnp.zeros((D,), F32)]).reshape(4, kw)
    dmod8 = jnp.concatenate([dmod_l, dmod_c], axis=0)
    r_in, r_uq, r_ukv, r_out, g_ng, g_qlg, g_kvlg, g_qng, g_kng, g_ps, loss_all, g_wp, g_w_mod, g_b_mod, g_c_ctx = _reduce(
        [dw_in_t.reshape(4, DIN // 4, D), dw_uq_t, dw_ukv, dw_out.reshape(4, D // 4, D)], split,
        [dng, dqlg, dkvlg, dqng, dkng, dps, loss], dw_pool, dmod8, a16, w_mod[0], c_ctx2)
    g2d = dict(c_ctx=g_c_ctx, b_mod=g_b_mod, w_mod=g_w_mod, w_in=r_in, w_uq=r_uq, w_ukv=r_ukv, w_out=r_out, norm_g=g_ng,
               q_lora_g=g_qlg, kv_lora_g=g_kvlg, q_norm_g=g_qng, k_norm_g=g_kng, pool_scale=g_ps, w_pool=g_wp.reshape(512, 128))

    outs = _adamw_many([as2d(n, weights[n]) for n in order], [g2d[n] for n in order], [as2d(n, m_in[n]) for n in order],
                       [as2d(n, v_in[n]) for n in order])
    d2d, m2d, v2d, g2d = (dict(zip(order, arrs)) for arrs in outs)

    return (loss_all[0, 0], gx[None], *[back(n, g2d[n]) for n in order], *[back(n, d2d[n]) for n in order],
            *[back(n, m2d[n]) for n in order], *[back(n, v2d[n]) for n in order])
```

```python
import jax
import jax.numpy as jnp
from jax import lax
from jax.experimental import pallas as pl
from jax.experimental.pallas import tpu as pltpu

F32 = jnp.float32
BF16 = jnp.bfloat16
MESH = pl.DeviceIdType.MESH

D = 1024
NH = 4
DK = 192
DKP = 256
DV = 128
QL = 256
KVL = 128
DIN = 1984
U_LO = 448
SEG = ((0, 512), (448, 960), (960, 1472), (1472, 1984))
DU = 2048
POOL_WINDOWS = (2, 4, 8, 16)
HALO = 8
EPS = 1e-6
ROPE_BASE = 10000.0
GRID_W = 64
Q_BLOCK = 128
TB = 256
BWD_QBLOCKS = 1
SCALE = DK ** -0.5
LOG2E = 1.4426950408889634
LN2 = 0.6931471805599453
VMEM_LIMIT = 56 * 1024 * 1024

ADAM_LR = 0.001
ADAM_B1 = 0.9
ADAM_B2 = 0.999
ADAM_EPS = 1e-08
ADAM_WD = 0.01
ADAM_STEP = 10

CHIPS3 = ((1, 0), (0, 1), (1, 1))
PEERS7 = tuple((dx, dy, dc) for dx in (0, 1) for dy in (0, 1) for dc in (0, 1) if (dx, dy, dc) != (0, 0, 0))

VM = pl.BlockSpec(memory_space=pltpu.VMEM)
ANY = pl.BlockSpec(memory_space=pl.ANY)


def _nn(a, b):
    return jnp.dot(a, b, preferred_element_type=F32)


def _nt(a, b):
    return lax.dot_general(a, b, (((1,), (1,)), ((), ())), preferred_element_type=F32)


def _tn(a, b):
    return lax.dot_general(a, b, (((0,), (0,)), ((), ())), preferred_element_type=F32)


def _split3(a):
    a0 = a.astype(BF16)
    r = a - a0.astype(F32)
    a1 = r.astype(BF16)
    a2 = (r - a1.astype(F32)).astype(BF16)
    return a0, a1, a2


def _dot3(dot, a, b):
    sa = _split3(a)
    sb = _split3(b)
    out = None
    for i in range(3):
        for j in range(3 - i):
            t = dot(sa[i], sb[j])
            out = t if out is None else out + t
    return out


def _sig(x):
    return 1.0 / (1.0 + jnp.exp(-x))


def _rot(t):
    src = lax.broadcasted_iota(jnp.int32, (128, 128), 0)
    dst = lax.broadcasted_iota(jnp.int32, (128, 128), 1)
    first = (dst % 32) < 16
    perm = jnp.where(first & (src == dst + 16), -1.0, jnp.where(~first & (src == dst - 16), 1.0, 0.0)).astype(BF16)
    hi = t.astype(BF16)
    lo = (t - hi.astype(F32)).astype(BF16)
    return _nn(hi, perm) + _nn(lo, perm)


def _rope(t, cos, sin):
    return t * cos + _rot(t) * sin


def _rope_t(t, cos, sin):
    return t * cos - _rot(t * sin)


def _rope_block(rows_ref, cols_ref, is_ctx):
    lane = lax.broadcasted_iota(jnp.int32, (TB, 256), 1) % 128
    rows = jnp.concatenate([jnp.broadcast_to(rows_ref[0, r:r + 1, :], (GRID_W, 256)) for r in range(TB // GRID_W)], axis=0)
    cs = jnp.where(lane < 32, rows, cols_ref[...])
    return jnp.where(is_ctx, 1.0, cs[:, :128]), jnp.where(is_ctx, 0.0, cs[:, 128:])


def _shift_rows(z, k):
    n = z.shape[0]
    return pltpu.roll(z, (n - k) % n, 0)


def _colsum(a):
    return jnp.sum(a, axis=0, keepdims=True)


def _rowsum(a):
    return jnp.sum(a, axis=-1, keepdims=True)


def _row_layout(col):
    return jnp.transpose(jnp.broadcast_to(col, (col.shape[0], 128)))[0:8, :]


def _params(sem=None):
    return pltpu.CompilerParams(dimension_semantics=sem, vmem_limit_bytes=VMEM_LIMIT)


def _full(shape):
    nd = len(shape)
    return pl.BlockSpec(shape, lambda *_: (0,) * nd)


def _peer(x, y, c, off):
    dx, dy, dc = off
    return ((x + dx) % 2, (y + dy) % 2, (c + dc) % 2)


def _token_specs(off):
    ctx = pl.BlockSpec((TB, D), lambda i: (jnp.minimum(i, off - 1), 0))
    lat = pl.BlockSpec((TB, D), lambda i: (jnp.maximum(i - off, 0), 0))
    mod = pl.BlockSpec((1, 3, D), lambda i: (jnp.minimum(i // off, 1), 0, 0))
    return ctx, lat, mod


def _modulated(x, mod_ref, ng):
    shift = mod_ref[0, 0:1, :]
    scale = mod_ref[0, 1:2, :]
    r = lax.rsqrt(jnp.mean(x * x, axis=-1, keepdims=True) + EPS)
    xh = x * r
    xg = xh * ng
    return r, xh, xg, xg * (1.0 + scale) + shift, scale


def _fwd_in(ctx, x, modsel, norm_g, w_in_t, q_lora_g, w_uq_t, kv_lora_g, w_ukv, qn_g, kn_g, cos, sin):
    s_len, lc = x.shape[0], ctx.shape[0]
    t_all = s_len + lc
    nb = t_all // TB
    off = lc // TB

    def body(ctx_ref, x_ref, mod_ref, ng_ref, win_ref, qlg_ref, wuq_ref, kvlg_ref, wukv_ref, qng_ref, kng_ref, cos_ref, sin_ref,
             u_ref, q_ref, k_ref, v_ref):
        is_ctx = pl.program_id(0) < off
        xb = jnp.where(is_ctx, ctx_ref[...], x_ref[...])
        _, _, _, h, _ = _modulated(xb, mod_ref, ng_ref[...])
        hb = h.astype(BF16)
        lane = lax.broadcasted_iota(jnp.int32, (TB, 512), 1)
        ulo = jnp.where(lane < U_LO, _nt(hb, win_ref[SEG[0][0]:SEG[0][1], :]), 0.0)
        u_ref[:, 0:512] = ulo
        for j in range(1, 4):
            u_ref[:, j * 512:(j + 1) * 512] = _nt(hb, win_ref[SEG[j][0]:SEG[j][1], :])
        cos, sin = _rope_block(cos_ref, sin_ref, is_ctx)
        cq = ulo[:, 0:QL]
        cqn = (cq * lax.rsqrt(jnp.mean(cq * cq, axis=-1, keepdims=True) + EPS) * qlg_ref[...]).astype(BF16)
        qng = qng_ref[...]
        ckv = ulo[:, QL:QL + KVL]
        ckvn = (ckv * lax.rsqrt(jnp.mean(ckv * ckv, axis=-1, keepdims=True) + EPS) * kvlg_ref[...]).astype(BF16)
        qhs = [_nt(cqn, wuq_ref[hd]) for hd in range(NH)]
        kvs = [_nn(ckvn, wukv_ref[hd]) for hd in range(NH)]
        for hd in range(NH):
            qh = qhs[hd]
            qn = qh * lax.rsqrt(_rowsum(qh * qh) / DK + EPS) * qng
            q_ref[hd] = (jnp.concatenate([qn[:, :128], _rope(qn[:, 128:], cos, sin)], axis=1) * (SCALE * LOG2E)).astype(BF16)
        kr = ulo[:, 384:512]
        skr = _rowsum(kr * kr)
        kng = kng_ref[...]
        kr_roped = _rope(kr * kng[:, 128:], cos, sin)
        for hd in range(NH):
            kv = kvs[hd]
            kn = kv[:, :128]
            rk = lax.rsqrt((_rowsum(kn * kn) + skr) / DK + EPS)
            k_ref[hd] = jnp.concatenate([kn * rk * kng[:, :128], kr_roped * rk], axis=1).astype(BF16)
            v_ref[hd] = kv[:, 128:].astype(BF16)

    row = lambda w: pl.BlockSpec((TB, w), lambda i: (i, 0))
    heads = lambda w: pl.BlockSpec((NH, TB, w), lambda i: (0, i, 0))
    cspec, xspec, mspec = _token_specs(off)
    return pl.pallas_call(
        body, name="fwd_in", grid=(nb,),
        in_specs=[cspec, xspec, mspec, _full((1, D)), _full((DIN, D)), _full((1, QL)), _full((NH, DKP, QL)), _full((1, KVL)),
                  _full((NH, KVL, 256)), _full((1, DKP)), _full((1, DKP)),
                  pl.BlockSpec((1, 8, 256), lambda i: (jnp.maximum(i - off, 0), 0, 0)), _full((TB, 256))],
        out_specs=[row(DU), heads(DKP), heads(DKP), heads(DV)],
        out_shape=[jax.ShapeDtypeStruct((t_all, DU), F32), jax.ShapeDtypeStruct((NH, t_all, DKP), BF16),
                   jax.ShapeDtypeStruct((NH, t_all, DKP), BF16), jax.ShapeDtypeStruct((NH, t_all, DV), BF16)],
        compiler_params=_params(("arbitrary",)),
    )(ctx, x, modsel, norm_g, w_in_t, q_lora_g, w_uq_t, kv_lora_g, w_ukv, qn_g, kn_g, cos, sin)


def _attn_fwd(q, k, v, s_len):
    t_all = q.shape[1]
    off = (t_all - s_len) // TB
    nq = s_len // TB
    nsub = next(n for n in (4, 2, 1) if nq % n == 0)

    def body(*refs):
        q_refs = refs[:nsub]
        k_ref, v_ref, o_ref, lse_ref = refs[nsub:]
        for sb in range(nsub):
            s = _nt(q_refs[sb][0], k_ref[0])
            m = jnp.max(s, axis=-1, keepdims=True)
            e = jnp.exp2(s - m)
            l = _rowsum(e)
            o_ref[sb * TB:(sb + 1) * TB, :] = _nn(e.astype(BF16), v_ref[0]) / l
            lse_ref[0, sb] = _row_layout(m + jnp.log2(l))

    qspec = lambda sb: pl.BlockSpec((1, TB, DKP), lambda h, i: (h, i * nsub + sb + off, 0))
    return pl.pallas_call(
        body, name="attn_fwd", grid=(NH, nq // nsub),
        in_specs=[qspec(sb) for sb in range(nsub)]
        + [pl.BlockSpec((1, t_all, DKP), lambda h, i: (h, 0, 0)), pl.BlockSpec((1, t_all, DV), lambda h, i: (h, 0, 0))],
        out_specs=[pl.BlockSpec((nsub * TB, DV), lambda h, i: (i, h)), pl.BlockSpec((1, nsub, 8, TB), lambda h, i: (h, i, 0, 0))],
        out_shape=[jax.ShapeDtypeStruct((s_len, NH * DV), F32), jax.ShapeDtypeStruct((NH, nq, 8, TB), F32)],
        compiler_params=_params(("arbitrary", "arbitrary")),
    )(*([q] * nsub), k, v)


def _out_stage(attn, u, x, target, gate, w_pool, pool_scale, w_out, lc):
    s_len = x.shape[0]
    t_all = s_len + lc
    off = lc // TB
    nq = s_len // TB
    hb = TB // HALO
    nqb = s_len // Q_BLOCK
    jb = TB // nqb

    def body(attn_ref, ga_ref, pin_ref, pprev_ref, pnext_ref, gp_ref, x_ref, tgt_ref, gate_ref, wp_ref, ps_ref, wo_ref,
             dxn_ref, dattn_ref, dga_ref, dgp_ref, dpool_ref, dwo_ref, dgate_ref, dps_ref, dwp_ref, loss_ref):
        i = pl.program_id(0)

        @pl.when(i == 0)
        def _():
            dwo_ref[...] = jnp.zeros_like(dwo_ref)
            dgate_ref[...] = jnp.zeros_like(dgate_ref)
            dps_ref[...] = jnp.zeros_like(dps_ref)
            dwp_ref[...] = jnp.zeros_like(dwp_ref)
            loss_ref[...] = jnp.zeros_like(loss_ref)

        attn = jnp.concatenate([attn_ref[:, jj, :] for jj in range(jb)], axis=0)
        ga = ga_ref[...]
        gp = gp_ref[...]
        pin = pin_ref[...]
        prev = jnp.where(i == 0, 0.0, pprev_ref[...])
        nxt = jnp.where(i == nq - 1, 0.0, pnext_ref[...])
        win = jnp.concatenate([prev, pin, nxt], axis=0)
        tg = i * TB + lax.broadcasted_iota(jnp.int32, (TB, 1), 0)
        pooled = []
        for g, w in enumerate(POOL_WINDOWS):
            a = win[:, g * 128:(g + 1) * 128]
            p = _shift_rows(a, -1) + a
            for step in (1, 2, 4):
                if w >= 4 * step:
                    p = _shift_rows(p, -step) + _shift_rows(p, step)
            cnt = (jnp.minimum(tg + w // 2, s_len) - jnp.maximum(tg - w // 2, 0)).astype(F32)
            pooled.append(p[HALO:HALO + TB] / cnt - a[HALO:HALO + TB])
        pooled_b = [p.astype(BF16) for p in pooled]
        z = jnp.concatenate([_nn(pooled_b[g], wp_ref[g]) for g in range(4)], axis=1)
        ps = ps_ref[...]
        yp = z * ps
        sga = _sig(ga)
        sila = ga * sga
        sgp = _sig(gp)
        silp = gp * sgp
        br = jnp.concatenate([sila * attn, silp * yp], axis=1).astype(BF16)
        y = _nn(br, wo_ref[...])
        gate = gate_ref[...]
        err = x_ref[...] + gate * y - tgt_ref[...]
        loss_ref[...] += _colsum(_rowsum(err * err)) * (0.5 / D)
        dxn = err * (1.0 / D)
        dxn_ref[...] = dxn
        dgate_ref[...] += _colsum(dxn * y)
        dy = (dxn * gate).astype(BF16)
        dwo_ref[...] += _tn(br, dy)
        dbr = _nt(dy, wo_ref[...])
        dbra = dbr[:, :512]
        dbrp = dbr[:, 512:]
        dattn = dbra * sila
        for jj in range(jb):
            dattn_ref[:, jj, :] = dattn[jj * nqb:(jj + 1) * nqb]
        dga_ref[...] = (dbra * attn * (sga * (1.0 + ga * (1.0 - sga)))).astype(BF16)
        dgp_ref[...] = (dbrp * yp * (sgp * (1.0 + gp * (1.0 - sgp)))).astype(BF16)
        dyp = dbrp * silp
        dps_ref[...] += _colsum(dyp * z)
        dz = (dyp * ps).astype(BF16)
        dpool = []
        for g in range(4):
            dzg = dz[:, g * 128:(g + 1) * 128]
            dwp_ref[g] += _tn(pooled_b[g], dzg)
            dpool.append(_nt(dzg, wp_ref[g]))
        dpool_ref[...] = jnp.concatenate(dpool, axis=1)

    lat = lambda w: pl.BlockSpec((TB, w), lambda i: (i, 0))
    perm = pl.BlockSpec((nqb, jb, 512), lambda i: (0, i, 0))
    ucol = lambda j: pl.BlockSpec((TB, 512), lambda i: (i + off, j))
    last8 = t_all // HALO - 1
    return pl.pallas_call(
        body, name="out_stage", grid=(nq,),
        in_specs=[perm, ucol(1), ucol(2),
                  pl.BlockSpec((HALO, 512), lambda i: ((i + off) * hb - 1, 2)),
                  pl.BlockSpec((HALO, 512), lambda i: (jnp.minimum((i + off + 1) * hb, last8), 2)),
                  ucol(3), lat(D), lat(D), _full((1, D)), _full((4, 128, 128)), _full((1, 512)), _full((D, D))],
        out_specs=[lat(D), perm, lat(512), lat(512), lat(512),
                   _full((D, D)), _full((1, D)), _full((1, 512)), _full((4, 128, 128)), _full((1, 1))],
        out_shape=[jax.ShapeDtypeStruct((s_len, D), F32), jax.ShapeDtypeStruct((nqb, Q_BLOCK, 512), F32),
                   jax.ShapeDtypeStruct((s_len, 512), BF16), jax.ShapeDtypeStruct((s_len, 512), BF16),
                   jax.ShapeDtypeStruct((s_len, 512), F32),
                   jax.ShapeDtypeStruct((D, D), F32), jax.ShapeDtypeStruct((1, D), F32), jax.ShapeDtypeStruct((1, 512), F32),
                   jax.ShapeDtypeStruct((4, 128, 128), F32), jax.ShapeDtypeStruct((1, 1), F32)],
        compiler_params=_params(("arbitrary",)),
    )(attn, u, u, u, u, u, x, target, gate, w_pool, pool_scale, w_out)


def _attn_bwd(q, k, v, dattn, attn, lse, s_len):
    t_all = q.shape[1]
    off = (t_all - s_len) // TB
    nq = s_len // TB
    nch = 4
    chunks = [(c * (t_all // nch), t_all // nch) for c in range(nch)]
    nsub = next(n for n in (BWD_QBLOCKS, 2, 1) if nq % n == 0)
    tq = nsub * TB

    def body(*refs):
        q_refs = refs[:nsub]
        k_ref, v_ref, do_ref, o_ref, lse_ref, dq_ref, dk_ref, dv_ref = refs[nsub:]
        i = pl.program_id(1)

        @pl.when(i == 0)
        def _():
            dk_ref[...] = jnp.zeros_like(dk_ref)
            dv_ref[...] = jnp.zeros_like(dv_ref)

        qb = jnp.concatenate([r[0] for r in q_refs], axis=0)
        delta_r = _row_layout(_rowsum(do_ref[...] * o_ref[...]))[0:1, :]
        do = do_ref[...].astype(BF16)
        lse_r = jnp.concatenate([lse_ref[0, sb][0:1, :] for sb in range(nsub)], axis=1)
        dq = jnp.zeros((tq, DKP), F32)
        for start, size in chunks:
            rows = pl.ds(start, size)
            kc = k_ref[0, rows, :]
            p_t = jnp.exp2(_nt(kc, qb) - lse_r)
            ds_t = (p_t * (_nt(v_ref[0, rows, :], do) - delta_r)).astype(BF16)
            dv_ref[0, rows, :] += _nn(p_t.astype(BF16), do)
            dk_ref[0, rows, :] += _nn(ds_t, qb)
            dq += _tn(ds_t, kc)
        dq_ref[0] = dq * SCALE

    kvspec = lambda w: pl.BlockSpec((1, t_all, w), lambda h, i: (h, 0, 0))
    rowspec = pl.BlockSpec((1, nsub, 8, TB), lambda h, i: (h, i, 0, 0))
    qspec = lambda sb: pl.BlockSpec((1, TB, DKP), lambda h, i: (h, i * nsub + sb + off, 0))
    return pl.pallas_call(
        body, name="attn_bwd", grid=(NH, nq // nsub),
        in_specs=[qspec(sb) for sb in range(nsub)]
        + [kvspec(DKP), kvspec(DV), pl.BlockSpec((tq, DV), lambda h, i: (i, h)), pl.BlockSpec((tq, DV), lambda h, i: (i, h)),
           rowspec],
        out_specs=[pl.BlockSpec((1, tq, DKP), lambda h, i: (h, i, 0)), kvspec(DKP), kvspec(DV)],
        out_shape=[jax.ShapeDtypeStruct((NH, s_len, DKP), F32), jax.ShapeDtypeStruct((NH, t_all, DKP), F32),
                   jax.ShapeDtypeStruct((NH, t_all, DV), F32)],
        compiler_params=_params(("arbitrary", "arbitrary")),
    )(*([q] * nsub), k, v, dattn, attn, lse)


def _qkv_bwd(u, dq, dk, dv, cos, sin, q_lora_g, w_uq_t, kv_lora_g, w_ukv, qn_g, kn_g, s_len):
    t_all = u.shape[0]
    off = (t_all - s_len) // TB
    nb = t_all // TB

    def body(ulo_ref, dq_ref, dk_ref, dv_ref, cos_ref, sin_ref, qlg_ref, wuq_ref, kvlg_ref, wukv_ref, qng_ref, kng_ref,
             dlo_ref, dwuq_ref, dwukv_ref, dqlg_ref, dkvlg_ref, dqng_ref, dkng_ref):
        i = pl.program_id(0)

        @pl.when(i == 0)
        def _():
            for r in (dwuq_ref, dwukv_ref, dqlg_ref, dkvlg_ref, dqng_ref, dkng_ref):
                r[...] = jnp.zeros_like(r)

        latent = i >= off
        ulo = ulo_ref[...]
        cos, sin = _rope_block(cos_ref, sin_ref, pl.program_id(0) < off)
        cq = ulo[:, 0:QL]
        rc = lax.rsqrt(jnp.mean(cq * cq, axis=-1, keepdims=True) + EPS)
        cqh = cq * rc
        qlg = qlg_ref[...]
        cqn_b = (cqh * qlg).astype(BF16)
        qng = qng_ref[...]
        ckv = ulo[:, QL:QL + KVL]
        r0 = lax.rsqrt(jnp.mean(ckv * ckv, axis=-1, keepdims=True) + EPS)
        ckvh = ckv * r0
        kvlg = kvlg_ref[...]
        ckvn_b = (ckvh * kvlg).astype(BF16)
        qhs = [_nt(cqn_b, wuq_ref[hd]) for hd in range(NH)]
        kns = [_nn(ckvn_b, wukv_ref[hd])[:, :128] for hd in range(NH)]
        dqng = jnp.zeros((1, DKP), F32)
        dqraws = []
        for hd in range(NH):
            qh = qhs[hd]
            rq = lax.rsqrt(_rowsum(qh * qh) / DK + EPS)
            xh = qh * rq
            dqh = jnp.where(latent, dq_ref[hd], 0.0)
            dyq = jnp.concatenate([dqh[:, :128], _rope_t(dqh[:, 128:], cos, sin)], axis=1)
            dqng += _colsum(dyq * xh)
            dxh = dyq * qng
            dqraws.append((rq * (dxh - xh * (_rowsum(dxh * xh) / DK))).astype(BF16))
        dqng_ref[...] += dqng

        kr = ulo[:, 384:512]
        skr = _rowsum(kr * kr)
        kng = kng_ref[...]
        dkr = jnp.zeros((TB, 128), F32)
        dkng = jnp.zeros((1, DKP), F32)
        dkvs = []
        for hd in range(NH):
            kn = kns[hd]
            rk = lax.rsqrt((_rowsum(kn * kn) + skr) / DK + EPS)
            xh1 = kn * rk
            xh2 = kr * rk
            dkh = dk_ref[hd] * LN2
            d1 = dkh[:, :128]
            d2 = _rope_t(dkh[:, 128:], cos, sin)
            dkng += jnp.concatenate([_colsum(d1 * xh1), _colsum(d2 * xh2)], axis=1)
            dx1 = d1 * kng[:, :128]
            dx2 = d2 * kng[:, 128:]
            dot = (_rowsum(dx1 * xh1) + _rowsum(dx2 * xh2)) / DK
            dkvs.append(jnp.concatenate([rk * (dx1 - xh1 * dot), dv_ref[hd]], axis=1).astype(BF16))
            dkr += rk * (dx2 - xh2 * dot)
        dkng_ref[...] += dkng

        dcqn = jnp.zeros((TB, QL), F32)
        dckvn = jnp.zeros((TB, KVL), F32)
        for hd in range(NH):
            dwuq_ref[hd] += _tn(dqraws[hd], cqn_b)[:DK]
            dcqn += _nn(dqraws[hd], wuq_ref[hd])
            dwukv_ref[hd] += _tn(ckvn_b, dkvs[hd])
            dckvn += _nt(dkvs[hd], wukv_ref[hd])
        dqlg_ref[...] += _colsum(dcqn * cqh)
        dxh = dcqn * qlg
        dcq = rc * (dxh - cqh * jnp.mean(dxh * cqh, axis=-1, keepdims=True))
        dkvlg_ref[...] += _colsum(dckvn * ckvh)
        dxh = dckvn * kvlg
        dckv = r0 * (dxh - ckvh * jnp.mean(dxh * ckvh, axis=-1, keepdims=True))
        dlo_ref[...] = jnp.concatenate([dcq, dckv, dkr], axis=1).astype(BF16)

    row = lambda w: pl.BlockSpec((TB, w), lambda i: (i, 0))
    heads = lambda w: pl.BlockSpec((NH, TB, w), lambda i: (0, i, 0))
    return pl.pallas_call(
        body, name="qkv_bwd", grid=(nb,),
        in_specs=[row(512), pl.BlockSpec((NH, TB, DKP), lambda i: (0, jnp.maximum(i - off, 0), 0)), heads(DKP), heads(DV),
                  pl.BlockSpec((1, 8, 256), lambda i: (jnp.maximum(i - off, 0), 0, 0)), _full((TB, 256)), _full((1, QL)), _full((NH, DKP, QL)), _full((1, KVL)), _full((NH, KVL, 256)),
                  _full((1, DKP)), _full((1, DKP))],
        out_specs=[row(512), _full((NH, DK, QL)), _full((NH, KVL, 256)), _full((1, QL)), _full((1, KVL)),
                   _full((1, DKP)), _full((1, DKP))],
        out_shape=[jax.ShapeDtypeStruct((t_all, 512), BF16), jax.ShapeDtypeStruct((NH, DK, QL), F32),
                   jax.ShapeDtypeStruct((NH, KVL, 256), F32), jax.ShapeDtypeStruct((1, QL), F32),
                   jax.ShapeDtypeStruct((1, KVL), F32), jax.ShapeDtypeStruct((1, DKP), F32), jax.ShapeDtypeStruct((1, DKP), F32)],
        compiler_params=_params(("arbitrary",)),
    )(u, dq, dk, dv, cos, sin, q_lora_g, w_uq_t, kv_lora_g, w_ukv, qn_g, kn_g)


def _in_bwd(ctx, x, modsel, norm_g, dlo, dga, dgp, dpool, dxn, w_in_t):
    s_len, lc = x.shape[0], ctx.shape[0]
    t_all = s_len + lc
    off = lc // TB
    nb = t_all // TB
    nq = s_len // TB
    hb = TB // HALO
    n = TB + 2 * HALO

    def body(ctx_ref, x_ref, mod_ref, ng_ref, dlo_ref, dga_ref, dgp_ref, dp_ref, dpprev_ref, dpnext_ref, dxn_ref, win_ref,
             gx_ref, dwin_ref, dmod_ref, dng_ref):
        i = pl.program_id(0)
        j = i - off

        @pl.when(i == 0)
        def _():
            dwin_ref[...] = jnp.zeros_like(dwin_ref)
            dmod_ref[...] = jnp.zeros_like(dmod_ref)
            dng_ref[...] = jnp.zeros_like(dng_ref)

        latent = i >= off
        dp = dp_ref[...]
        prev = jnp.where(j <= 0, 0.0, dpprev_ref[...])
        nxt = jnp.where(j >= nq - 1, 0.0, dpnext_ref[...])
        win = jnp.concatenate([prev, dp, nxt], axis=0)
        tg = j * TB - HALO + lax.broadcasted_iota(jnp.int32, (n, 1), 0)
        dpin = []
        for g, w in enumerate(POOL_WINDOWS):
            cnt = jnp.maximum(jnp.minimum(tg + w // 2, s_len) - jnp.maximum(tg - w // 2, 0), 1).astype(F32)
            zq = win[:, g * 128:(g + 1) * 128] / cnt
            zq = zq + _shift_rows(zq, 1)
            for step in (1, 2, 4):
                if w >= 4 * step:
                    zq = _shift_rows(zq, -step) + _shift_rows(zq, step)
            dpin.append(zq[HALO:HALO + TB] - dp[:, g * 128:(g + 1) * 128])
        zero = jnp.zeros((TB, 512), BF16)
        du = [dlo_ref[...], jnp.where(latent, dga_ref[...], zero),
              jnp.where(latent, jnp.concatenate(dpin, axis=1).astype(BF16), zero), jnp.where(latent, dgp_ref[...], zero)]

        ng = ng_ref[...]
        xb = jnp.where(i < off, ctx_ref[...], x_ref[...])
        r, xh, xg, h, scale = _modulated(xb, mod_ref, ng)
        hb_ = h.astype(BF16)
        dh = jnp.zeros((TB, D), F32)
        for s, (lo, hi) in enumerate(SEG):
            dwin_ref[lo:hi, :] += _tn(du[s], hb_)
            dh += _nn(du[s], win_ref[lo:hi, :])
        is_lat = latent.astype(F32)
        dsh = _colsum(dh)
        dsc = _colsum(dh * xg)
        dmod_ref[0, 0:1, :] += dsh * (1.0 - is_lat)
        dmod_ref[0, 1:2, :] += dsc * (1.0 - is_lat)
        dmod_ref[1, 0:1, :] += dsh * is_lat
        dmod_ref[1, 1:2, :] += dsc * is_lat
        dxg = dh * (1.0 + scale)
        dng_ref[...] += _colsum(dxg * xh)
        dxh = dxg * ng
        gx_ref[...] = r * (dxh - xh * jnp.mean(dxh * xh, axis=-1, keepdims=True)) + dxn_ref[...]

    row = lambda w: pl.BlockSpec((TB, w), lambda i: (i, 0))
    lat = lambda w: pl.BlockSpec((TB, w), lambda i: (jnp.maximum(i - off, 0), 0))
    last8 = s_len // HALO - 1
    cspec, xspec, mspec = _token_specs(off)
    return pl.pallas_call(
        body, name="in_bwd", grid=(nb,),
        in_specs=[cspec, xspec, mspec, _full((1, D)), row(512), lat(512), lat(512), lat(512),
                  pl.BlockSpec((HALO, 512), lambda i: (jnp.maximum(jnp.maximum(i - off, 0) * hb - 1, 0), 0)),
                  pl.BlockSpec((HALO, 512), lambda i: (jnp.minimum((jnp.maximum(i - off, 0) + 1) * hb, last8), 0)),
                  lat(D), _full((DIN, D))],
        out_specs=[lat(D), _full((DIN, D)), _full((2, 2, D)), _full((1, D))],
        out_shape=[jax.ShapeDtypeStruct((s_len, D), F32), jax.ShapeDtypeStruct((DIN, D), F32),
                   jax.ShapeDtypeStruct((2, 2, D), F32), jax.ShapeDtypeStruct((1, D), F32)],
        compiler_params=_params(("arbitrary",)),
    )(ctx, x, modsel, norm_g, dlo, dga, dgp, dpool, dpool, dpool, dxn, w_in_t)


def _adamw_update(w_ref, g_ref, m_ref, v_ref, d_ref, mo_ref, vo_ref):
    gv = g_ref[...]
    mn = ADAM_B1 * m_ref[...] + (1.0 - ADAM_B1) * gv
    vn = ADAM_B2 * v_ref[...] + (1.0 - ADAM_B2) * (gv * gv)
    m_hat = mn / (1.0 - ADAM_B1 ** ADAM_STEP)
    v_hat = vn / (1.0 - ADAM_B2 ** ADAM_STEP)
    d_ref[...] = -ADAM_LR * (m_hat / (jnp.sqrt(v_hat) + ADAM_EPS) + ADAM_WD * w_ref[...])
    mo_ref[...] = mn
    vo_ref[...] = vn


def _adamw_many(ws, gs, ms, vs):
    n = len(ws)
    parts = 4

    def body(*refs):
        for i in range(n):
            _adamw_update(refs[i], refs[n + i], refs[2 * n + i], refs[3 * n + i], refs[4 * n + i], refs[5 * n + i], refs[6 * n + i])
            refs[7 * n + i][...] = refs[n + i][...]

    def spec(w):
        rows, cols = w.shape
        if rows % (8 * parts) == 0:
            return pl.BlockSpec((rows // parts, cols), lambda i: (i, 0))
        if cols % (128 * parts) == 0:
            return pl.BlockSpec((rows, cols // parts), lambda i: (0, i))
        return _full((rows, cols))

    specs = [spec(w) for w in ws]
    shp = [jax.ShapeDtypeStruct(w.shape, F32) for w in ws]
    out = pl.pallas_call(body, name="adamw_many", grid=(parts,), in_specs=specs * 4, out_specs=specs * 4, out_shape=shp * 4,
                         compiler_params=_params(("arbitrary",)))(*ws, *gs, *ms, *vs)
    return out[:n], out[n:2 * n], out[2 * n:3 * n], out[3 * n:]


class _Links:
    def __init__(self, send_sems, recv_sems):
        self.send_sems, self.recv_sems, self.sends = send_sems, recv_sems, []

    def send(self, src, dst, sem, to):
        cp = pltpu.make_async_remote_copy(src, dst, self.send_sems.at[sem], self.recv_sems.at[sem], device_id=to,
                                          device_id_type=MESH)
        cp.start()
        self.sends.append(cp)

    def arrived(self, dst, sem, frm):
        pltpu.make_async_remote_copy(dst, dst, self.send_sems.at[sem], self.recv_sems.at[sem], device_id=frm,
                                     device_id_type=MESH).wait_recv()

    def drain(self):
        for cp in self.sends:
            cp.wait_send()


def _half(ref, c, axis):
    size = ref.shape[axis - 2] // 2
    win = pl.ds(pl.multiple_of(c * size, 16 if axis == 0 else 128), size)
    idx = (win, slice(None)) if axis == 0 else (slice(None), win)
    return ref.at[(slice(None),) * (len(ref.shape) - 2) + idx]


def _select_rows(slots_ref, n_slots, row=0):
    sub = lax.broadcasted_iota(jnp.int32, (8, 1), 0)
    out = None
    for d in range(n_slots):
        r = jnp.where(sub == d, jnp.broadcast_to(slots_ref[d][row:row + 1, :], (8, slots_ref.shape[-1])), 0.0)
        out = r if out is None else out + r
    return out


def _gather(c, c_ctx, w_mod, b_mod_k, shards, axes, slab_rows):
    nw = len(shards)
    kw = w_mod.shape[1]

    def body(*refs):
        c_ref, cc_ref, wm_ref, b_ref = refs[:4]
        w_refs = refs[4:4 + nw]
        a16_ref, mod_ref = refs[4 + nw:6 + nw]
        g_refs = refs[6 + nw:6 + 2 * nw]
        a_ref, send_sems, recv_sems = refs[6 + 2 * nw:]

        def slab(wi, chip):
            return g_refs[wi].at[chip].at[0:shards[wi].shape[0]]
        x, y, cc = lax.axis_index("x"), lax.axis_index("y"), lax.axis_index("c")
        me = 4 * x + 2 * y + cc
        k = 2 * x + y
        sibling = (x, y, 1 - cc)
        links = _Links(send_sems, recv_sems)
        chips = [_peer(x, y, cc, off + (0,)) for off in CHIPS3]
        chip_a = ((x + 1 - cc) % 2, (y + cc) % 2, cc)
        chip_b = ((x + cc) % 2, (y + 1 - cc) % 2, cc)
        chip_d = (1 - x, 1 - y, cc)
        cv = c_ref[...]
        sc = cv * _sig(cv)
        mine = a_ref.at[me]
        for r in range(8):
            mine[r:r + 1, :] = sc[:, r * 128:(r + 1) * 128]
        for j, off in enumerate(PEERS7):
            links.send(a_ref.at[me], a_ref.at[me], j, _peer(x, y, cc, off))
        for wi in range(nw):
            slab(wi, k)[...] = w_refs[wi][...].astype(BF16)
            for j, to in enumerate((chip_a, chip_b)):
                links.send(_half(slab(wi, k), cc, axes[wi]), _half(slab(wi, k), cc, axes[wi]), 10 + wi * 6 + j, to)
            pad = slab_rows[wi] - shards[wi].shape[0]
            if pad:
                for kk in range(4):
                    g_refs[wi][kk, shards[wi].shape[0]:, :] = jnp.zeros((pad, shards[wi].shape[1]), BF16)
        for j, off in enumerate(PEERS7):
            px, py, pc = _peer(x, y, cc, off)
            links.arrived(a_ref.at[4 * px + 2 * py + pc], j, (px, py, pc))
        ccv = cc_ref[...]
        sub = lax.broadcasted_iota(jnp.int32, (8, 1), 0)
        top = jnp.zeros((8, D), F32)
        for d in range(8):
            blk = a_ref[d]
            row = jnp.concatenate([blk[r:r + 1, :] for r in range(8)], axis=1)
            top = top + jnp.where(sub == d, jnp.broadcast_to(row, (8, D)), 0.0)
        a16 = jnp.concatenate([top, jnp.where(sub == 0, jnp.broadcast_to(ccv * _sig(ccv), (8, D)), 0.0)], axis=0)
        a16_ref[...] = a16
        mod_ref[k] = _dot3(_nn, a16, wm_ref[...]) + b_ref[...]
        for j, to in enumerate(chips):
            links.send(mod_ref.at[k], mod_ref.at[k], 7 + j, to)
        for j, (frm, origin) in enumerate(((chip_a, chip_a), (chip_b, chip_b), (chip_b, chip_d))):
            for wi in range(nw):
                blk = _half(slab(wi, 2 * origin[0] + origin[1]), cc, axes[wi])
                links.arrived(blk, 10 + wi * 6 + j, frm)
                if j == 0:
                    links.send(blk, blk, 10 + wi * 6 + 2, chip_b)
                links.send(blk, blk, 10 + wi * 6 + 3 + j, sibling)
        for j, (px, py, pc) in enumerate(chips):
            links.arrived(mod_ref.at[2 * px + py], 7 + j, (px, py, pc))
        for j, origin in enumerate((chip_b, chip_a, chip_d)):
            for wi in range(nw):
                links.arrived(_half(slab(wi, 2 * origin[0] + origin[1]), 1 - cc, axes[wi]), 10 + wi * 6 + 3 + j, sibling)
        links.drain()

    nsem = 10 + 6 * nw
    return pl.pallas_call(
        body, name="gather", in_specs=[VM] * (4 + nw), out_specs=[VM] * (2 + nw),
        out_shape=[jax.ShapeDtypeStruct((16, D), F32), jax.ShapeDtypeStruct((4, 16, kw), F32)]
        + [jax.ShapeDtypeStruct((4, r, s.shape[1]), BF16) for r, s in zip(slab_rows, shards)],
        scratch_shapes=[pltpu.VMEM((8, 8, D // 8), F32), pltpu.SemaphoreType.DMA((nsem,)), pltpu.SemaphoreType.DMA((nsem,))],
        compiler_params=pltpu.CompilerParams(vmem_limit_bytes=VMEM_LIMIT),
    )(c, c_ctx, w_mod, b_mod_k, *shards)


SMALL_ROW_WIDTHS = (D, QL, KVL, DKP, DKP, 512, 128)
SMALL_OUT_WIDTHS = (D, QL, KVL, DK, DK, 512, 1)
SMALL_PACK = 384


def _small_pieces():
    pieces = []
    for i, w in enumerate(SMALL_ROW_WIDTHS):
        for c0 in range(0, w, 128):
            j = len(pieces)
            pieces.append((i, c0, j // (SMALL_PACK // 128), j % (SMALL_PACK // 128) * 128))
    assert len(pieces) <= 8 * (SMALL_PACK // 128)
    return pieces


def _reduce(grads, axes, smalls, w_pool_g, dmod8, a16, w_mod, c_ctx):
    nw = len(grads)
    ns = len(smalls)
    kw = w_mod.shape[1]
    halves = []
    for g, ax in zip(grads, axes):
        halves.append((g.shape[1] // 2, g.shape[2]) if ax == 0 else (g.shape[1], g.shape[2] // 2))

    def body(*refs):
        g_refs = refs[:nw]
        small_refs = refs[nw:nw + ns]
        wp_ref, dm_ref, a16_ref, wm_ref, cc_ref = refs[nw + ns:nw + ns + 5]
        o = nw + ns + 5
        r_refs = refs[o:o + nw]
        small_outs = refs[o + nw:o + nw + ns]
        rwp_ref, gw_ref, gb_ref, gc_ref = refs[o + nw + ns:o + nw + ns + 4]
        o = o + nw + ns + 4
        own, sib, part, got, rel = (refs[o + i * nw:o + (i + 1) * nw] for i in range(5))
        smbuf, wps, wpg, dm_all, pc_all, send_sems, recv_sems, local_sems = refs[o + 5 * nw:]
        x, y, cc = lax.axis_index("x"), lax.axis_index("y"), lax.axis_index("c")
        me = 4 * x + 2 * y + cc
        k = 2 * x + y
        sibling = (x, y, 1 - cc)
        links = _Links(send_sems, recv_sems)
        chips = [_peer(x, y, cc, off + (0,)) for off in CHIPS3]
        peers = [_peer(x, y, cc, off) for off in PEERS7]
        chip_a = ((x + 1 - cc) % 2, (y + cc) % 2, cc)
        chip_b = ((x + cc) % 2, (y + 1 - cc) % 2, cc)
        ka, kb, kd = 2 * chip_a[0] + chip_a[1], 2 * chip_b[0] + chip_b[1], 2 * (1 - x) + (1 - y)
        big, sm0, wp0, dm0, pc0 = 0, 5 * nw, 5 * nw + 7, 5 * nw + 14, 5 * nw + 21

        locals_ = []
        for wi in range(nw):
            lc = pltpu.make_async_copy(_half(g_refs[wi], cc, axes[wi]), own[wi], local_sems.at[wi])
            lc.start()
            locals_.append(lc)
            links.send(_half(g_refs[wi], 1 - cc, axes[wi]), sib[wi], big + wi * 5, sibling)
        slot = smbuf.at[me]
        slot[...] = jnp.zeros((8, SMALL_PACK), F32)
        small_rows = [jnp.broadcast_to(ref[...], (1, w)) for ref, w in zip(small_refs, SMALL_ROW_WIDTHS)]
        for i, c0, row, lane in _small_pieces():
            slot[row:row + 1, lane:lane + 128] = small_rows[i][:, c0:c0 + 128]
        wp_half = w_pool_g.shape[0] // 2
        wp_mine, wp_other = pl.ds(cc * wp_half, wp_half), pl.ds((1 - cc) * wp_half, wp_half)
        links.send(wp_ref.at[wp_other], wps, wp0, sibling)
        dm_all[me] = dm_ref[...]
        for j, peer in enumerate(peers):
            links.send(dm_all.at[me], dm_all.at[me], dm0 + j, peer)
            links.send(smbuf.at[me], smbuf.at[me], sm0 + j, peer)
        links.arrived(wps, wp0, sibling)
        wpg[k] = (wp_ref[wp_mine] + wps[...]).astype(BF16)
        for j, to in enumerate(chips):
            links.send(wpg.at[k], wpg.at[k], wp0 + 1 + j, to)
        for wi in range(nw):
            locals_[wi].wait()
            links.arrived(sib[wi], big + wi * 5, sibling)
            part[wi][...] = (own[wi][...] + sib[wi][...]).astype(BF16)
            got[wi][k] = part[wi][k]
            got[wi][kd] = jnp.zeros(halves[wi], BF16)
            links.send(part[wi].at[kd], rel[wi], big + wi * 5 + 1, chip_b)
            links.send(part[wi].at[kb], got[wi].at[k], big + wi * 5 + 2, chip_b)
        for j, (px, py, pc) in enumerate(peers):
            links.arrived(dm_all.at[4 * px + 2 * py + pc], dm0 + j, (px, py, pc))
        dm_tot = dm_all[0]
        for d in range(1, 8):
            dm_tot = dm_tot + dm_all[d]
        for kk in range(4):
            gb_ref[:, kk * kw:(kk + 1) * kw] = dm_tot[kk:kk + 1, :] + dm_tot[4 + kk:5 + kk, :]
        top = jnp.zeros((8, kw), F32)
        dmc_k = jnp.zeros((1, kw), F32)
        for kk in range(4):
            top = top + jnp.where(k == kk, _select_rows(dm_all, 8, kk), 0.0)
            dmc_k = dmc_k + jnp.where(k == kk, dm_tot[4 + kk:5 + kk, :], 0.0)
        sub = lax.broadcasted_iota(jnp.int32, (8, 1), 0)
        bk = jnp.concatenate([top, jnp.where(sub == 0, jnp.broadcast_to(dmc_k, (8, kw)), 0.0)], axis=0)
        gw_ref[...] = _dot3(_tn, a16_ref[...], bk)
        pc_all[k] = _dot3(_nt, jnp.broadcast_to(bk[8:9, :], (8, kw)), wm_ref[...])
        for j, to in enumerate(chips):
            links.send(pc_all.at[k], pc_all.at[k], pc0 + j, to)
        for j, (px, py, pc) in enumerate(peers):
            links.arrived(smbuf.at[4 * px + 2 * py + pc], sm0 + j, (px, py, pc))
        tot = smbuf[0]
        for d in range(1, 8):
            tot = tot + smbuf[d]
        for i, c0, row, lane in _small_pieces():
            n = min(128, SMALL_OUT_WIDTHS[i] - c0)
            if n > 0:
                small_outs[i][:, c0:c0 + n] = tot[row:row + 1, lane:lane + n]
        for j, (px, py, pc) in enumerate(chips):
            links.arrived(wpg.at[2 * px + py], wp0 + 1 + j, (px, py, pc))
        wpt = wpg[0].astype(F32)
        for kk in range(1, 4):
            wpt = wpt + wpg[kk].astype(F32)
        rwp_ref[wp_mine] = wpt
        links.send(rwp_ref.at[wp_mine], rwp_ref.at[wp_mine], wp0 + 4, sibling)
        for wi in range(nw):
            links.arrived(rel[wi], big + wi * 5 + 1, chip_b)
            rel[wi][...] = (part[wi][ka].astype(F32) + rel[wi][...].astype(F32)).astype(BF16)
            links.send(rel[wi], got[wi].at[k], big + wi * 5 + 3, chip_a)
        for wi in range(nw):
            links.arrived(got[wi].at[kb], big + wi * 5 + 2, chip_b)
            links.arrived(got[wi].at[ka], big + wi * 5 + 3, chip_a)
            total = got[wi][0].astype(F32)
            for kk in range(1, 4):
                total = total + got[wi][kk].astype(F32)
            mine = _half(r_refs[wi], cc, axes[wi])
            mine[...] = total
            links.send(mine, mine, big + wi * 5 + 4, sibling)
        for j, (px, py, pc) in enumerate(chips):
            links.arrived(pc_all.at[2 * px + py], pc0 + j, (px, py, pc))
        ccv = cc_ref[...]
        sg = _sig(ccv)
        gc_ref[...] = (pc_all[0][0:1, :] + pc_all[1][0:1, :] + pc_all[2][0:1, :] + pc_all[3][0:1, :]) * (sg * (1.0 + ccv * (1.0 - sg)))
        for wi in range(nw):
            links.arrived(_half(r_refs[wi], 1 - cc, axes[wi]), big + wi * 5 + 4, sibling)
        links.arrived(rwp_ref.at[wp_other], wp0 + 4, sibling)
        links.drain()

    nsem = 5 * nw + 24
    quads = [(4,) + h for h in halves]
    wp_half_shape = (w_pool_g.shape[0] // 2,) + w_pool_g.shape[1:]
    return pl.pallas_call(
        body, name="reduce", in_specs=[ANY] * nw + [VM] * (ns + 5), out_specs=[VM] * (nw + ns + 4),
        out_shape=[jax.ShapeDtypeStruct(g.shape[1:], F32) for g in grads]
        + [jax.ShapeDtypeStruct((1, w), F32) for w in SMALL_OUT_WIDTHS]
        + [jax.ShapeDtypeStruct(w_pool_g.shape, F32), jax.ShapeDtypeStruct((D, kw), F32), jax.ShapeDtypeStruct((1, 3 * D), F32),
           jax.ShapeDtypeStruct((1, D), F32)],
        scratch_shapes=[pltpu.VMEM(q, F32) for q in quads] + [pltpu.VMEM(q, F32) for q in quads]
        + [pltpu.VMEM(q, BF16) for q in quads] + [pltpu.VMEM(q, BF16) for q in quads] + [pltpu.VMEM(h, BF16) for h in halves]
        + [pltpu.VMEM((8, 8, SMALL_PACK), F32), pltpu.VMEM(wp_half_shape, F32), pltpu.VMEM((4,) + wp_half_shape, BF16),
           pltpu.VMEM((8, 8, kw), F32), pltpu.VMEM((4, 8, D), F32)]
        + [pltpu.SemaphoreType.DMA((nsem,)), pltpu.SemaphoreType.DMA((nsem,)), pltpu.SemaphoreType.DMA((nw,))],
        compiler_params=pltpu.CompilerParams(vmem_limit_bytes=VMEM_LIMIT),
    )(*grads, *smalls, w_pool_g, dmod8, a16, w_mod, c_ctx)


def _rope_tables(s_len):
    rows = s_len // GRID_W
    per = TB // GRID_W
    n_freq = 16
    inv = ROPE_BASE ** (-jnp.arange(n_freq, dtype=F32) / n_freq)
    ang_r = jnp.arange(rows, dtype=F32)[:, None] * inv
    ang_c = jnp.arange(GRID_W, dtype=F32)[:, None] * inv
    by_row, by_col = [], []
    for fn, pad in ((jnp.cos, 1.0), (jnp.sin, 0.0)):
        r = jnp.concatenate([fn(ang_r), fn(ang_r), jnp.zeros((rows, 96), F32)], axis=1).reshape(rows // per, per, 128)
        by_row.append(jnp.pad(r, ((0, 0), (0, 8 - per), (0, 0))))
        cpart = jnp.concatenate([jnp.zeros((GRID_W, 32), F32), fn(ang_c), fn(ang_c), jnp.full((GRID_W, 64), pad, F32)], axis=1)
        by_col.append(jnp.tile(cpart, (per, 1)))
    return jnp.concatenate(by_row, axis=-1), jnp.concatenate(by_col, axis=-1)


def kernel(x, c, ctx, c_ctx, w_mod, b_mod, norm_g, w_in, q_lora_g, w_uq, kv_lora_g, w_ukv, q_norm_g, k_norm_g, w_pool, pool_scale, w_out, loss_target, m_c_ctx, m_w_mod, m_b_mod, m_norm_g, m_w_in, m_q_lora_g, m_w_uq, m_kv_lora_g, m_w_ukv, m_q_norm_g, m_k_norm_g, m_w_pool, m_pool_scale, m_w_out, v_c_ctx, v_w_mod, v_b_mod, v_norm_g, v_w_in, v_q_lora_g, v_w_uq, v_kv_lora_g, v_w_ukv, v_q_norm_g, v_k_norm_g, v_w_pool, v_pool_scale, v_w_out):
    xi, yi, ci = lax.axis_index("x"), lax.axis_index("y"), lax.axis_index("c")
    me = 4 * xi + 2 * yi + ci
    k = 2 * xi + yi
    s_len = x.shape[1]
    lc = ctx.shape[1]
    kw = w_mod.shape[2]
    weights = dict(c_ctx=c_ctx, w_mod=w_mod, b_mod=b_mod, norm_g=norm_g, w_in=w_in, q_lora_g=q_lora_g, w_uq=w_uq,
                   kv_lora_g=kv_lora_g, w_ukv=w_ukv, q_norm_g=q_norm_g, k_norm_g=k_norm_g, w_pool=w_pool,
                   pool_scale=pool_scale, w_out=w_out)
    m_in = dict(c_ctx=m_c_ctx, w_mod=m_w_mod, b_mod=m_b_mod, norm_g=m_norm_g, w_in=m_w_in, q_lora_g=m_q_lora_g, w_uq=m_w_uq,
                kv_lora_g=m_kv_lora_g, w_ukv=m_w_ukv, q_norm_g=m_q_norm_g, k_norm_g=m_k_norm_g, w_pool=m_w_pool,
                pool_scale=m_pool_scale, w_out=m_w_out)
    v_in = dict(c_ctx=v_c_ctx, w_mod=v_w_mod, b_mod=v_b_mod, norm_g=v_norm_g, w_in=v_w_in, q_lora_g=v_q_lora_g, w_uq=v_w_uq,
                kv_lora_g=v_kv_lora_g, w_ukv=v_w_ukv, q_norm_g=v_q_norm_g, k_norm_g=v_k_norm_g, w_pool=v_w_pool,
                pool_scale=v_pool_scale, w_out=v_w_out)
    order = ["c_ctx", "w_mod", "b_mod", "norm_g", "w_in", "q_lora_g", "w_uq", "kv_lora_g", "w_ukv", "q_norm_g", "k_norm_g",
             "w_pool", "pool_scale", "w_out"]
    transposed = ("w_in", "w_uq")
    as2d = lambda n, a: jnp.transpose(a[0]) if n in transposed else a.reshape(-1, a.shape[-1])
    back = lambda n, a: jnp.transpose(a)[None] if n in transposed else a.reshape(weights[n].shape)

    c_ctx2 = c_ctx.reshape(1, D)
    b_mod_k = lax.dynamic_slice(b_mod, (0, k * kw), (1, kw))
    split = (1, 0, 0, 0)
    a16, mod_all, g_in, g_uq, g_ukv, g_out = _gather(
        c, c_ctx2, w_mod[0], b_mod_k, [as2d("w_in", w_in), as2d("w_uq", w_uq), w_ukv[0], w_out[0]], split,
        (DIN // 4, DKP, KVL, D // 4))
    mod_me = lax.dynamic_index_in_dim(mod_all, me, axis=1, keepdims=False).reshape(3, D)
    mod_c = mod_all[:, 8, :].reshape(3, D)
    modsel = jnp.stack([mod_c, mod_me])
    w_in_t = g_in.reshape(DIN, D)
    w_uq_t = g_uq
    w_out_f = g_out.reshape(D, D)
    qn_g = jnp.pad(q_norm_g, ((0, 0), (0, DKP - DK)))
    kn_g = jnp.pad(k_norm_g, ((0, 0), (0, DKP - DK)))
    w_pool_b = w_pool[0].astype(BF16)
    cos, sin = _rope_tables(s_len)

    u, q, kk, v = _fwd_in(ctx[0], x[0], modsel, norm_g, w_in_t, q_lora_g, w_uq_t, kv_lora_g, g_ukv, qn_g, kn_g, cos, sin)
    attn, lse = _attn_fwd(q, kk, v, s_len)
    (dxn, dattn, dga, dgp, dpool, dw_out, dgate, dps, dw_pool, loss) = _out_stage(
        attn.reshape(s_len // Q_BLOCK, Q_BLOCK, NH * DV), u, x[0], loss_target[0], modsel[1, 2:3, :], w_pool_b, pool_scale,
        w_out_f, lc)
    dattn = dattn.reshape(s_len, NH * DV)
    dq, dk, dv = _attn_bwd(q, kk, v, dattn, attn, lse, s_len)
    dlo, dw_uq_t, dw_ukv, dqlg, dkvlg, dqng, dkng = _qkv_bwd(u, dq, dk, dv, cos, sin, q_lora_g, w_uq_t, kv_lora_g, g_ukv,
                                                            qn_g, kn_g, s_len)
    gx, dw_in_t, dmod, dng = _in_bwd(ctx[0], x[0], modsel, norm_g, dlo, dga, dgp, dpool, dxn, w_in_t)

    dmod_l = jnp.concatenate([dmod[1, 0], dmod[1, 1], dgate[0]]).reshape(4, kw)
    dmod_c = jnp.concatenate([dmod[0, 0], dmod[0, 1], jnp.zeros((D,), F32)]).reshape(4, kw)
    dmod8 = jnp.concatenate([dmod_l, dmod_c], axis=0)
    r_in, r_uq, r_ukv, r_out, g_ng, g_qlg, g_kvlg, g_qng, g_kng, g_ps, loss_all, g_wp, g_w_mod, g_b_mod, g_c_ctx = _reduce(
        [dw_in_t.reshape(4, DIN // 4, D), dw_uq_t, dw_ukv, dw_out.reshape(4, D // 4, D)], split,
        [dng, dqlg, dkvlg, dqng, dkng, dps, loss], dw_pool, dmod8, a16, w_mod[0], c_ctx2)
    g2d = dict(c_ctx=g_c_ctx, b_mod=g_b_mod, w_mod=g_w_mod, w_in=r_in, w_uq=r_uq, w_ukv=r_ukv, w_out=r_out, norm_g=g_ng,
               q_lora_g=g_qlg, kv_lora_g=g_kvlg, q_norm_g=g_qng, k_norm_g=g_kng, pool_scale=g_ps, w_pool=g_wp.reshape(512, 128))

    outs = _adamw_many([as2d(n, weights[n]) for n in order], [g2d[n] for n in order], [as2d(n, m_in[n]) for n in order],
                       [as2d(n, v_in[n]) for n in order])
    d2d, m2d, v2d, g2d = (dict(zip(order, arrs)) for arrs in outs)

    return (loss_all[0, 0], gx[None], *[back(n, g2d[n]) for n in order], *[back(n, d2d[n]) for n in order],
            *[back(n, m2d[n]) for n in order], *[back(n, v2d[n]) for n in order])
```

```python
import jax
import jax.numpy as jnp
from jax import lax
from jax.experimental import pallas as pl
from jax.experimental.pallas import tpu as pltpu

F32 = jnp.float32
BF16 = jnp.bfloat16
MESH = pl.DeviceIdType.MESH

D = 1024
NH = 4
DK = 192
DKP = 256
DV = 128
QL = 256
KVL = 128
DIN = 1984
U_LO = 448
SEG = ((0, 512), (448, 960), (960, 1472), (1472, 1984))
DU = 2048
POOL_WINDOWS = (2, 4, 8, 16)
HALO = 8
EPS = 1e-6
ROPE_BASE = 10000.0
GRID_W = 64
Q_BLOCK = 128
TB = 256
BWD_QBLOCKS = 1
SCALE = DK ** -0.5
LOG2E = 1.4426950408889634
LN2 = 0.6931471805599453
VMEM_LIMIT = 56 * 1024 * 1024

ADAM_LR = 0.001
ADAM_B1 = 0.9
ADAM_B2 = 0.999
ADAM_EPS = 1e-08
ADAM_WD = 0.01
ADAM_STEP = 10

CHIPS3 = ((1, 0), (0, 1), (1, 1))
PEERS7 = tuple((dx, dy, dc) for dx in (0, 1) for dy in (0, 1) for dc in (0, 1) if (dx, dy, dc) != (0, 0, 0))

VM = pl.BlockSpec(memory_space=pltpu.VMEM)
ANY = pl.BlockSpec(memory_space=pl.ANY)


def _nn(a, b):
    return jnp.dot(a, b, preferred_element_type=F32)


def _nt(a, b):
    return lax.dot_general(a, b, (((1,), (1,)), ((), ())), preferred_element_type=F32)


def _tn(a, b):
    return lax.dot_general(a, b, (((0,), (0,)), ((), ())), preferred_element_type=F32)


def _split3(a):
    a0 = a.astype(BF16)
    r = a - a0.astype(F32)
    a1 = r.astype(BF16)
    a2 = (r - a1.astype(F32)).astype(BF16)
    return a0, a1, a2


def _dot3(dot, a, b):
    sa = _split3(a)
    sb = _split3(b)
    out = None
    for i in range(3):
        for j in range(3 - i):
            t = dot(sa[i], sb[j])
            out = t if out is None else out + t
    return out


def _sig(x):
    return 1.0 / (1.0 + jnp.exp(-x))


def _rot(t):
    src = lax.broadcasted_iota(jnp.int32, (128, 128), 0)
    dst = lax.broadcasted_iota(jnp.int32, (128, 128), 1)
    first = (dst % 32) < 16
    perm = jnp.where(first & (src == dst + 16), -1.0, jnp.where(~first & (src == dst - 16), 1.0, 0.0)).astype(BF16)
    hi = t.astype(BF16)
    lo = (t - hi.astype(F32)).astype(BF16)
    return _nn(hi, perm) + _nn(lo, perm)


def _rope(t, cos, sin):
    return t * cos + _rot(t) * sin


def _rope_t(t, cos, sin):
    return t * cos - _rot(t * sin)


def _rope_block(rows_ref, cols_ref, is_ctx):
    lane = lax.broadcasted_iota(jnp.int32, (TB, 256), 1) % 128
    rows = jnp.concatenate([jnp.broadcast_to(rows_ref[0, r:r + 1, :], (GRID_W, 256)) for r in range(TB // GRID_W)], axis=0)
    cs = jnp.where(lane < 32, rows, cols_ref[...])
    return jnp.where(is_ctx, 1.0, cs[:, :128]), jnp.where(is_ctx, 0.0, cs[:, 128:])


def _shift_rows(z, k):
    n = z.shape[0]
    return pltpu.roll(z, (n - k) % n, 0)


def _colsum(a):
    return jnp.sum(a, axis=0, keepdims=True)


def _rowsum(a):
    return jnp.sum(a, axis=-1, keepdims=True)


def _row_layout(col):
    return jnp.transpose(jnp.broadcast_to(col, (col.shape[0], 128)))[0:8, :]


def _params(sem=None):
    return pltpu.CompilerParams(dimension_semantics=sem, vmem_limit_bytes=VMEM_LIMIT)


def _full(shape):
    nd = len(shape)
    return pl.BlockSpec(shape, lambda *_: (0,) * nd)


def _peer(x, y, c, off):
    dx, dy, dc = off
    return ((x + dx) % 2, (y + dy) % 2, (c + dc) % 2)


def _token_specs(off):
    ctx = pl.BlockSpec((TB, D), lambda i: (jnp.minimum(i, off - 1), 0))
    lat = pl.BlockSpec((TB, D), lambda i: (jnp.maximum(i - off, 0), 0))
    mod = pl.BlockSpec((1, 3, D), lambda i: (jnp.minimum(i // off, 1), 0, 0))
    return ctx, lat, mod


def _modulated(x, mod_ref, ng):
    shift = mod_ref[0, 0:1, :]
    scale = mod_ref[0, 1:2, :]
    r = lax.rsqrt(jnp.mean(x * x, axis=-1, keepdims=True) + EPS)
    xh = x * r
    xg = xh * ng
    return r, xh, xg, xg * (1.0 + scale) + shift, scale


def _fwd_in(ctx, x, modsel, norm_g, w_in_t, q_lora_g, w_uq_t, kv_lora_g, w_ukv, qn_g, kn_g, cos, sin):
    s_len, lc = x.shape[0], ctx.shape[0]
    t_all = s_len + lc
    nb = t_all // TB
    off = lc // TB

    def body(ctx_ref, x_ref, mod_ref, ng_ref, win_ref, qlg_ref, wuq_ref, kvlg_ref, wukv_ref, qng_ref, kng_ref, cos_ref, sin_ref,
             u_ref, q_ref, k_ref, v_ref):
        is_ctx = pl.program_id(0) < off
        xb = jnp.where(is_ctx, ctx_ref[...], x_ref[...])
        _, _, _, h, _ = _modulated(xb, mod_ref, ng_ref[...])
        hb = h.astype(BF16)
        lane = lax.broadcasted_iota(jnp.int32, (TB, 512), 1)
        ulo = jnp.where(lane < U_LO, _nt(hb, win_ref[SEG[0][0]:SEG[0][1], :]), 0.0)
        u_ref[:, 0:512] = ulo
        for j in range(1, 4):
            u_ref[:, j * 512:(j + 1) * 512] = _nt(hb, win_ref[SEG[j][0]:SEG[j][1], :])
        cos, sin = _rope_block(cos_ref, sin_ref, is_ctx)
        cq = ulo[:, 0:QL]
        cqn = (cq * lax.rsqrt(jnp.mean(cq * cq, axis=-1, keepdims=True) + EPS) * qlg_ref[...]).astype(BF16)
        qng = qng_ref[...]
        ckv = ulo[:, QL:QL + KVL]
        ckvn = (ckv * lax.rsqrt(jnp.mean(ckv * ckv, axis=-1, keepdims=True) + EPS) * kvlg_ref[...]).astype(BF16)
        qhs = [_nt(cqn, wuq_ref[hd]) for hd in range(NH)]
        kvs = [_nn(ckvn, wukv_ref[hd]) for hd in range(NH)]
        for hd in range(NH):
            qh = qhs[hd]
            qn = qh * lax.rsqrt(_rowsum(qh * qh) / DK + EPS) * qng
            q_ref[hd] = (jnp.concatenate([qn[:, :128], _rope(qn[:, 128:], cos, sin)], axis=1) * (SCALE * LOG2E)).astype(BF16)
        kr = ulo[:, 384:512]
        skr = _rowsum(kr * kr)
        kng = kng_ref[...]
        kr_roped = _rope(kr * kng[:, 128:], cos, sin)
        for hd in range(NH):
            kv = kvs[hd]
            kn = kv[:, :128]
            rk = lax.rsqrt((_rowsum(kn * kn) + skr) / DK + EPS)
            k_ref[hd] = jnp.concatenate([kn * rk * kng[:, :128], kr_roped * rk], axis=1).astype(BF16)
            v_ref[hd] = kv[:, 128:].astype(BF16)

    row = lambda w: pl.BlockSpec((TB, w), lambda i: (i, 0))
    heads = lambda w: pl.BlockSpec((NH, TB, w), lambda i: (0, i, 0))
    cspec, xspec, mspec = _token_specs(off)
    return pl.pallas_call(
        body, name="fwd_in", grid=(nb,),
        in_specs=[cspec, xspec, mspec, _full((1, D)), _full((DIN, D)), _full((1, QL)), _full((NH, DKP, QL)), _full((1, KVL)),
                  _full((NH, KVL, 256)), _full((1, DKP)), _full((1, DKP)),
                  pl.BlockSpec((1, 8, 256), lambda i: (jnp.maximum(i - off, 0), 0, 0)), _full((TB, 256))],
        out_specs=[row(DU), heads(DKP), heads(DKP), heads(DV)],
        out_shape=[jax.ShapeDtypeStruct((t_all, DU), F32), jax.ShapeDtypeStruct((NH, t_all, DKP), BF16),
                   jax.ShapeDtypeStruct((NH, t_all, DKP), BF16), jax.ShapeDtypeStruct((NH, t_all, DV), BF16)],
        compiler_params=_params(("arbitrary",)),
    )(ctx, x, modsel, norm_g, w_in_t, q_lora_g, w_uq_t, kv_lora_g, w_ukv, qn_g, kn_g, cos, sin)


def _attn_fwd(q, k, v, s_len):
    t_all = q.shape[1]
    off = (t_all - s_len) // TB
    nq = s_len // TB
    nsub = next(n for n in (4, 2, 1) if nq % n == 0)

    def body(*refs):
        q_refs = refs[:nsub]
        k_ref, v_ref, o_ref, lse_ref = refs[nsub:]
        for sb in range(nsub):
            s = _nt(q_refs[sb][0], k_ref[0])
            m = jnp.max(s, axis=-1, keepdims=True)
            e = jnp.exp2(s - m)
            l = _rowsum(e)
            o_ref[sb * TB:(sb + 1) * TB, :] = _nn(e.astype(BF16), v_ref[0]) / l
            lse_ref[0, sb] = _row_layout(m + jnp.log2(l))

    qspec = lambda sb: pl.BlockSpec((1, TB, DKP), lambda h, i: (h, i * nsub + sb + off, 0))
    return pl.pallas_call(
        body, name="attn_fwd", grid=(NH, nq // nsub),
        in_specs=[qspec(sb) for sb in range(nsub)]
        + [pl.BlockSpec((1, t_all, DKP), lambda h, i: (h, 0, 0)), pl.BlockSpec((1, t_all, DV), lambda h, i: (h, 0, 0))],
        out_specs=[pl.BlockSpec((nsub * TB, DV), lambda h, i: (i, h)), pl.BlockSpec((1, nsub, 8, TB), lambda h, i: (h, i, 0, 0))],
        out_shape=[jax.ShapeDtypeStruct((s_len, NH * DV), F32), jax.ShapeDtypeStruct((NH, nq, 8, TB), F32)],
        compiler_params=_params(("arbitrary", "arbitrary")),
    )(*([q] * nsub), k, v)


def _out_stage(attn, u, x, target, gate, w_pool, pool_scale, w_out, lc):
    s_len = x.shape[0]
    t_all = s_len + lc
    off = lc // TB
    nq = s_len // TB
    hb = TB // HALO
    nqb = s_len // Q_BLOCK
    jb = TB // nqb

    def body(attn_ref, ga_ref, pin_ref, pprev_ref, pnext_ref, gp_ref, x_ref, tgt_ref, gate_ref, wp_ref, ps_ref, wo_ref,
             dxn_ref, dattn_ref, dga_ref, dgp_ref, dpool_ref, dwo_ref, dgate_ref, dps_ref, dwp_ref, loss_ref):
        i = pl.program_id(0)

        @pl.when(i == 0)
        def _():
            dwo_ref[...] = jnp.zeros_like(dwo_ref)
            dgate_ref[...] = jnp.zeros_like(dgate_ref)
            dps_ref[...] = jnp.zeros_like(dps_ref)
            dwp_ref[...] = jnp.zeros_like(dwp_ref)
            loss_ref[...] = jnp.zeros_like(loss_ref)

        attn = jnp.concatenate([attn_ref[:, jj, :] for jj in range(jb)], axis=0)
        ga = ga_ref[...]
        gp = gp_ref[...]
        pin = pin_ref[...]
        prev = jnp.where(i == 0, 0.0, pprev_ref[...])
        nxt = jnp.where(i == nq - 1, 0.0, pnext_ref[...])
        win = jnp.concatenate([prev, pin, nxt], axis=0)
        tg = i * TB + lax.broadcasted_iota(jnp.int32, (TB, 1), 0)
        pooled = []
        for g, w in enumerate(POOL_WINDOWS):
            a = win[:, g * 128:(g + 1) * 128]
            p = _shift_rows(a, -1) + a
            for step in (1, 2, 4):
                if w >= 4 * step:
                    p = _shift_rows(p, -step) + _shift_rows(p, step)
            cnt = (jnp.minimum(tg + w // 2, s_len) - jnp.maximum(tg - w // 2, 0)).astype(F32)
            pooled.append(p[HALO:HALO + TB] / cnt - a[HALO:HALO + TB])
        pooled_b = [p.astype(BF16) for p in pooled]
        z = jnp.concatenate([_nn(pooled_b[g], wp_ref[g]) for g in range(4)], axis=1)
        ps = ps_ref[...]
        yp = z * ps
        sga = _sig(ga)
        sila = ga * sga
        sgp = _sig(gp)
        silp = gp * sgp
        br = jnp.concatenate([sila * attn, silp * yp], axis=1).astype(BF16)
        y = _nn(br, wo_ref[...])
        gate = gate_ref[...]
        err = x_ref[...] + gate * y - tgt_ref[...]
        loss_ref[...] += _colsum(_rowsum(err * err)) * (0.5 / D)
        dxn = err * (1.0 / D)
        dxn_ref[...] = dxn
        dgate_ref[...] += _colsum(dxn * y)
        dy = (dxn * gate).astype(BF16)
        dwo_ref[...] += _tn(br, dy)
        dbr = _nt(dy, wo_ref[...])
        dbra = dbr[:, :512]
        dbrp = dbr[:, 512:]
        dattn = dbra * sila
        for jj in range(jb):
            dattn_ref[:, jj, :] = dattn[jj * nqb:(jj + 1) * nqb]
        dga_ref[...] = (dbra * attn * (sga * (1.0 + ga * (1.0 - sga)))).astype(BF16)
        dgp_ref[...] = (dbrp * yp * (sgp * (1.0 + gp * (1.0 - sgp)))).astype(BF16)
        dyp = dbrp * silp
        dps_ref[...] += _colsum(dyp * z)
        dz = (dyp * ps).astype(BF16)
        dpool = []
        for g in range(4):
            dzg = dz[:, g * 128:(g + 1) * 128]
            dwp_ref[g] += _tn(pooled_b[g], dzg)
            dpool.append(_nt(dzg, wp_ref[g]))
        dpool_ref[...] = jnp.concatenate(dpool, axis=1)

    lat = lambda w: pl.BlockSpec((TB, w), lambda i: (i, 0))
    perm = pl.BlockSpec((nqb, jb, 512), lambda i: (0, i, 0))
    ucol = lambda j: pl.BlockSpec((TB, 512), lambda i: (i + off, j))
    last8 = t_all // HALO - 1
    return pl.pallas_call(
        body, name="out_stage", grid=(nq,),
        in_specs=[perm, ucol(1), ucol(2),
                  pl.BlockSpec((HALO, 512), lambda i: ((i + off) * hb - 1, 2)),
                  pl.BlockSpec((HALO, 512), lambda i: (jnp.minimum((i + off + 1) * hb, last8), 2)),
                  ucol(3), lat(D), lat(D), _full((1, D)), _full((4, 128, 128)), _full((1, 512)), _full((D, D))],
        out_specs=[lat(D), perm, lat(512), lat(512), lat(512),
                   _full((D, D)), _full((1, D)), _full((1, 512)), _full((4, 128, 128)), _full((1, 1))],
        out_shape=[jax.ShapeDtypeStruct((s_len, D), F32), jax.ShapeDtypeStruct((nqb, Q_BLOCK, 512), F32),
                   jax.ShapeDtypeStruct((s_len, 512), BF16), jax.ShapeDtypeStruct((s_len, 512), BF16),
                   jax.ShapeDtypeStruct((s_len, 512), F32),
                   jax.ShapeDtypeStruct((D, D), F32), jax.ShapeDtypeStruct((1, D), F32), jax.ShapeDtypeStruct((1, 512), F32),
                   jax.ShapeDtypeStruct((4, 128, 128), F32), jax.ShapeDtypeStruct((1, 1), F32)],
        compiler_params=_params(("arbitrary",)),
    )(attn, u, u, u, u, u, x, target, gate, w_pool, pool_scale, w_out)


def _attn_bwd(q, k, v, dattn, attn, lse, s_len):
    t_all = q.shape[1]
    off = (t_all - s_len) // TB
    nq = s_len // TB
    nch = 4
    chunks = [(c * (t_all // nch), t_all // nch) for c in range(nch)]
    nsub = next(n for n in (BWD_QBLOCKS, 2, 1) if nq % n == 0)
    tq = nsub * TB

    def body(*refs):
        q_refs = refs[:nsub]
        k_ref, v_ref, do_ref, o_ref, lse_ref, dq_ref, dk_ref, dv_ref = refs[nsub:]
        i = pl.program_id(1)

        @pl.when(i == 0)
        def _():
            dk_ref[...] = jnp.zeros_like(dk_ref)
            dv_ref[...] = jnp.zeros_like(dv_ref)

        qb = jnp.concatenate([r[0] for r in q_refs], axis=0)
        delta_r = _row_layout(_rowsum(do_ref[...] * o_ref[...]))[0:1, :]
        do = do_ref[...].astype(BF16)
        lse_r = jnp.concatenate([lse_ref[0, sb][0:1, :] for sb in range(nsub)], axis=1)
        dq = jnp.zeros((tq, DKP), F32)
        for start, size in chunks:
            rows = pl.ds(start, size)
            kc = k_ref[0, rows, :]
            p_t = jnp.exp2(_nt(kc, qb) - lse_r)
            ds_t = (p_t * (_nt(v_ref[0, rows, :], do) - delta_r)).astype(BF16)
            dv_ref[0, rows, :] += _nn(p_t.astype(BF16), do)
            dk_ref[0, rows, :] += _nn(ds_t, qb)
            dq += _tn(ds_t, kc)
        dq_ref[0] = dq * SCALE

    kvspec = lambda w: pl.BlockSpec((1, t_all, w), lambda h, i: (h, 0, 0))
    rowspec = pl.BlockSpec((1, nsub, 8, TB), lambda h, i: (h, i, 0, 0))
    qspec = lambda sb: pl.BlockSpec((1, TB, DKP), lambda h, i: (h, i * nsub + sb + off, 0))
    return pl.pallas_call(
        body, name="attn_bwd", grid=(NH, nq // nsub),
        in_specs=[qspec(sb) for sb in range(nsub)]
        + [kvspec(DKP), kvspec(DV), pl.BlockSpec((tq, DV), lambda h, i: (i, h)), pl.BlockSpec((tq, DV), lambda h, i: (i, h)),
           rowspec],
        out_specs=[pl.BlockSpec((1, tq, DKP), lambda h, i: (h, i, 0)), kvspec(DKP), kvspec(DV)],
        out_shape=[jax.ShapeDtypeStruct((NH, s_len, DKP), F32), jax.ShapeDtypeStruct((NH, t_all, DKP), F32),
                   jax.ShapeDtypeStruct((NH, t_all, DV), F32)],
        compiler_params=_params(("arbitrary", "arbitrary")),
    )(*([q] * nsub), k, v, dattn, attn, lse)


def _qkv_bwd(u, dq, dk, dv, cos, sin, q_lora_g, w_uq_t, kv_lora_g, w_ukv, qn_g, kn_g, s_len):
    t_all = u.shape[0]
    off = (t_all - s_len) // TB
    nb = t_all // TB

    def body(ulo_ref, dq_ref, dk_ref, dv_ref, cos_ref, sin_ref, qlg_ref, wuq_ref, kvlg_ref, wukv_ref, qng_ref, kng_ref,
             dlo_ref, dwuq_ref, dwukv_ref, dqlg_ref, dkvlg_ref, dqng_ref, dkng_ref):
        i = pl.program_id(0)

        @pl.when(i == 0)
        def _():
            for r in (dwuq_ref, dwukv_ref, dqlg_ref, dkvlg_ref, dqng_ref, dkng_ref):
                r[...] = jnp.zeros_like(r)

        latent = i >= off
        ulo = ulo_ref[...]
        cos, sin = _rope_block(cos_ref, sin_ref, pl.program_id(0) < off)
        cq = ulo[:, 0:QL]
        rc = lax.rsqrt(jnp.mean(cq * cq, axis=-1, keepdims=True) + EPS)
        cqh = cq * rc
        qlg = qlg_ref[...]
        cqn_b = (cqh * qlg).astype(BF16)
        qng = qng_ref[...]
        ckv = ulo[:, QL:QL + KVL]
        r0 = lax.rsqrt(jnp.mean(ckv * ckv, axis=-1, keepdims=True) + EPS)
        ckvh = ckv * r0
        kvlg = kvlg_ref[...]
        ckvn_b = (ckvh * kvlg).astype(BF16)
        qhs = [_nt(cqn_b, wuq_ref[hd]) for hd in range(NH)]
        kns = [_nn(ckvn_b, wukv_ref[hd])[:, :128] for hd in range(NH)]
        dqng = jnp.zeros((1, DKP), F32)
        dqraws = []
        for hd in range(NH):
            qh = qhs[hd]
            rq = lax.rsqrt(_rowsum(qh * qh) / DK + EPS)
            xh = qh * rq
            dqh = jnp.where(latent, dq_ref[hd], 0.0)
            dyq = jnp.concatenate([dqh[:, :128], _rope_t(dqh[:, 128:], cos, sin)], axis=1)
            dqng += _colsum(dyq * xh)
            dxh = dyq * qng
            dqraws.append((rq * (dxh - xh * (_rowsum(dxh * xh) / DK))).astype(BF16))
        dqng_ref[...] += dqng

        kr = ulo[:, 384:512]
        skr = _rowsum(kr * kr)
        kng = kng_ref[...]
        dkr = jnp.zeros((TB, 128), F32)
        dkng = jnp.zeros((1, DKP), F32)
        dkvs = []
        for hd in range(NH):
            kn = kns[hd]
            rk = lax.rsqrt((_rowsum(kn * kn) + skr) / DK + EPS)
            xh1 = kn * rk
            xh2 = kr * rk
            dkh = dk_ref[hd] * LN2
            d1 = dkh[:, :128]
            d2 = _rope_t(dkh[:, 128:], cos, sin)
            dkng += jnp.concatenate([_colsum(d1 * xh1), _colsum(d2 * xh2)], axis=1)
            dx1 = d1 * kng[:, :128]
            dx2 = d2 * kng[:, 128:]
            dot = (_rowsum(dx1 * xh1) + _rowsum(dx2 * xh2)) / DK
            dkvs.append(jnp.concatenate([rk * (dx1 - xh1 * dot), dv_ref[hd]], axis=1).astype(BF16))
            dkr += rk * (dx2 - xh2 * dot)
        dkng_ref[...] += dkng

        dcqn = jnp.zeros((TB, QL), F32)
        dckvn = jnp.zeros((TB, KVL), F32)
        for hd in range(NH):
            dwuq_ref[hd] += _tn(dqraws[hd], cqn_b)[:DK]
            dcqn += _nn(dqraws[hd], wuq_ref[hd])
            dwukv_ref[hd] += _tn(ckvn_b, dkvs[hd])
            dckvn += _nt(dkvs[hd], wukv_ref[hd])
        dqlg_ref[...] += _colsum(dcqn * cqh)
        dxh = dcqn * qlg
        dcq = rc * (dxh - cqh * jnp.mean(dxh * cqh, axis=-1, keepdims=True))
        dkvlg_ref[...] += _colsum(dckvn * ckvh)
        dxh = dckvn * kvlg
        dckv = r0 * (dxh - ckvh * jnp.mean(dxh * ckvh, axis=-1, keepdims=True))
        dlo_ref[...] = jnp.concatenate([dcq, dckv, dkr], axis=1).astype(BF16)

    row = lambda w: pl.BlockSpec((TB, w), lambda i: (i, 0))
    heads = lambda w: pl.BlockSpec((NH, TB, w), lambda i: (0, i, 0))
    return pl.pallas_call(
        body, name="qkv_bwd", grid=(nb,),
        in_specs=[row(512), pl.BlockSpec((NH, TB, DKP), lambda i: (0, jnp.maximum(i - off, 0), 0)), heads(DKP), heads(DV),
                  pl.BlockSpec((1, 8, 256), lambda i: (jnp.maximum(i - off, 0), 0, 0)), _full((TB, 256)), _full((1, QL)), _full((NH, DKP, QL)), _full((1, KVL)), _full((NH, KVL, 256)),
                  _full((1, DKP)), _full((1, DKP))],
        out_specs=[row(512), _full((NH, DK, QL)), _full((NH, KVL, 256)), _full((1, QL)), _full((1, KVL)),
                   _full((1, DKP)), _full((1, DKP))],
        out_shape=[jax.ShapeDtypeStruct((t_all, 512), BF16), jax.ShapeDtypeStruct((NH, DK, QL), F32),
                   jax.ShapeDtypeStruct((NH, KVL, 256), F32), jax.ShapeDtypeStruct((1, QL), F32),
                   jax.ShapeDtypeStruct((1, KVL), F32), jax.ShapeDtypeStruct((1, DKP), F32), jax.ShapeDtypeStruct((1, DKP), F32)],
        compiler_params=_params(("arbitrary",)),
    )(u, dq, dk, dv, cos, sin, q_lora_g, w_uq_t, kv_lora_g, w_ukv, qn_g, kn_g)


def _in_bwd(ctx, x, modsel, norm_g, dlo, dga, dgp, dpool, dxn, w_in_t):
    s_len, lc = x.shape[0], ctx.shape[0]
    t_all = s_len + lc
    off = lc // TB
    nb = t_all // TB
    nq = s_len // TB
    hb = TB // HALO
    n = TB + 2 * HALO

    def body(ctx_ref, x_ref, mod_ref, ng_ref, dlo_ref, dga_ref, dgp_ref, dp_ref, dpprev_ref, dpnext_ref, dxn_ref, win_ref,
             gx_ref, dwin_ref, dmod_ref, dng_ref):
        i = pl.program_id(0)
        j = i - off

        @pl.when(i == 0)
        def _():
            dwin_ref[...] = jnp.zeros_like(dwin_ref)
            dmod_ref[...] = jnp.zeros_like(dmod_ref)
            dng_ref[...] = jnp.zeros_like(dng_ref)

        latent = i >= off
        dp = dp_ref[...]
        prev = jnp.where(j <= 0, 0.0, dpprev_ref[...])
        nxt = jnp.where(j >= nq - 1, 0.0, dpnext_ref[...])
        win = jnp.concatenate([prev, dp, nxt], axis=0)
        tg = j * TB - HALO + lax.broadcasted_iota(jnp.int32, (n, 1), 0)
        dpin = []
        for g, w in enumerate(POOL_WINDOWS):
            cnt = jnp.maximum(jnp.minimum(tg + w // 2, s_len) - jnp.maximum(tg - w // 2, 0), 1).astype(F32)
            zq = win[:, g * 128:(g + 1) * 128] / cnt
            zq = zq + _shift_rows(zq, 1)
            for step in (1, 2, 4):
                if w >= 4 * step:
                    zq = _shift_rows(zq, -step) + _shift_rows(zq, step)
            dpin.append(zq[HALO:HALO + TB] - dp[:, g * 128:(g + 1) * 128])
        zero = jnp.zeros((TB, 512), BF16)
        du = [dlo_ref[...], jnp.where(latent, dga_ref[...], zero),
              jnp.where(latent, jnp.concatenate(dpin, axis=1).astype(BF16), zero), jnp.where(latent, dgp_ref[...], zero)]

        ng = ng_ref[...]
        xb = jnp.where(i < off, ctx_ref[...], x_ref[...])
        r, xh, xg, h, scale = _modulated(xb, mod_ref, ng)
        hb_ = h.astype(BF16)
        dh = jnp.zeros((TB, D), F32)
        for s, (lo, hi) in enumerate(SEG):
            dwin_ref[lo:hi, :] += _tn(du[s], hb_)
            dh += _nn(du[s], win_ref[lo:hi, :])
        is_lat = latent.astype(F32)
        dsh = _colsum(dh)
        dsc = _colsum(dh * xg)
        dmod_ref[0, 0:1, :] += dsh * (1.0 - is_lat)
        dmod_ref[0, 1:2, :] += dsc * (1.0 - is_lat)
        dmod_ref[1, 0:1, :] += dsh * is_lat
        dmod_ref[1, 1:2, :] += dsc * is_lat
        dxg = dh * (1.0 + scale)
        dng_ref[...] += _colsum(dxg * xh)
        dxh = dxg * ng
        gx_ref[...] = r * (dxh - xh * jnp.mean(dxh * xh, axis=-1, keepdims=True)) + dxn_ref[...]

    row = lambda w: pl.BlockSpec((TB, w), lambda i: (i, 0))
    lat = lambda w: pl.BlockSpec((TB, w), lambda i: (jnp.maximum(i - off, 0), 0))
    last8 = s_len // HALO - 1
    cspec, xspec, mspec = _token_specs(off)
    return pl.pallas_call(
        body, name="in_bwd", grid=(nb,),
        in_specs=[cspec, xspec, mspec, _full((1, D)), row(512), lat(512), lat(512), lat(512),
                  pl.BlockSpec((HALO, 512), lambda i: (jnp.maximum(jnp.maximum(i - off, 0) * hb - 1, 0), 0)),
                  pl.BlockSpec((HALO, 512), lambda i: (jnp.minimum((jnp.maximum(i - off, 0) + 1) * hb, last8), 0)),
                  lat(D), _full((DIN, D))],
        out_specs=[lat(D), _full((DIN, D)), _full((2, 2, D)), _full((1, D))],
        out_shape=[jax.ShapeDtypeStruct((s_len, D), F32), jax.ShapeDtypeStruct((DIN, D), F32),
                   jax.ShapeDtypeStruct((2, 2, D), F32), jax.ShapeDtypeStruct((1, D), F32)],
        compiler_params=_params(("arbitrary",)),
    )(ctx, x, modsel, norm_g, dlo, dga, dgp, dpool, dpool, dpool, dxn, w_in_t)


def _adamw_update(w_ref, g_ref, m_ref, v_ref, d_ref, mo_ref, vo_ref):
    gv = g_ref[...]
    mn = ADAM_B1 * m_ref[...] + (1.0 - ADAM_B1) * gv
    vn = ADAM_B2 * v_ref[...] + (1.0 - ADAM_B2) * (gv * gv)
    m_hat = mn / (1.0 - ADAM_B1 ** ADAM_STEP)
    v_hat = vn / (1.0 - ADAM_B2 ** ADAM_STEP)
    d_ref[...] = -ADAM_LR * (m_hat / (jnp.sqrt(v_hat) + ADAM_EPS) + ADAM_WD * w_ref[...])
    mo_ref[...] = mn
    vo_ref[...] = vn


def _adamw_many(ws, gs, ms, vs):
    n = len(ws)
    parts = 4

    def body(*refs):
        for i in range(n):
            _adamw_update(refs[i], refs[n + i], refs[2 * n + i], refs[3 * n + i], refs[4 * n + i], refs[5 * n + i], refs[6 * n + i])
            refs[7 * n + i][...] = refs[n + i][...]

    def spec(w):
        rows, cols = w.shape
        if rows % (8 * parts) == 0:
            return pl.BlockSpec((rows // parts, cols), lambda i: (i, 0))
        if cols % (128 * parts) == 0:
            return pl.BlockSpec((rows, cols // parts), lambda i: (0, i))
        return _full((rows, cols))

    specs = [spec(w) for w in ws]
    shp = [jax.ShapeDtypeStruct(w.shape, F32) for w in ws]
    out = pl.pallas_call(body, name="adamw_many", grid=(parts,), in_specs=specs * 4, out_specs=specs * 4, out_shape=shp * 4,
                         compiler_params=_params(("arbitrary",)))(*ws, *gs, *ms, *vs)
    return out[:n], out[n:2 * n], out[2 * n:3 * n], out[3 * n:]


class _Links:
    def __init__(self, send_sems, recv_sems):
        self.send_sems, self.recv_sems, self.sends = send_sems, recv_sems, []

    def send(self, src, dst, sem, to):
        cp = pltpu.make_async_remote_copy(src, dst, self.send_sems.at[sem], self.recv_sems.at[sem], device_id=to,
                                          device_id_type=MESH)
        cp.start()
        self.sends.append(cp)

    def arrived(self, dst, sem, frm):
        pltpu.make_async_remote_copy(dst, dst, self.send_sems.at[sem], self.recv_sems.at[sem], device_id=frm,
                                     device_id_type=MESH).wait_recv()

    def drain(self):
        for cp in self.sends:
            cp.wait_send()


def _half(ref, c, axis):
    size = ref.shape[axis - 2] // 2
    win = pl.ds(pl.multiple_of(c * size, 16 if axis == 0 else 128), size)
    idx = (win, slice(None)) if axis == 0 else (slice(None), win)
    return ref.at[(slice(None),) * (len(ref.shape) - 2) + idx]


def _select_rows(slots_ref, n_slots, row=0):
    sub = lax.broadcasted_iota(jnp.int32, (8, 1), 0)
    out = None
    for d in range(n_slots):
        r = jnp.where(sub == d, jnp.broadcast_to(slots_ref[d][row:row + 1, :], (8, slots_ref.shape[-1])), 0.0)
        out = r if out is None else out + r
    return out


def _gather(c, c_ctx, w_mod, b_mod_k, shards, axes, slab_rows):
    nw = len(shards)
    kw = w_mod.shape[1]

    def body(*refs):
        c_ref, cc_ref, wm_hbm, b_ref = refs[:4]
        w_hbm = refs[4:4 + nw]
        a16_ref, mod_ref = refs[4 + nw:6 + nw]
        out_refs = refs[6 + nw:6 + 2 * nw]
        g_refs = refs[6 + 2 * nw:6 + 3 * nw]
        f_refs = refs[6 + 3 * nw:6 + 4 * nw]
        wm_ref, a_ref, send_sems, recv_sems, local_sems = refs[6 + 4 * nw:]
        loads = [pltpu.make_async_copy(w_hbm[wi], f_refs[wi], local_sems.at[wi]) for wi in range(nw)]
        loads.append(pltpu.make_async_copy(wm_hbm, wm_ref, local_sems.at[nw]))
        for cp in loads:
            cp.start()
        stores = []

        def slab(wi, chip, of=g_refs):
            return of[wi].at[chip].at[0:shards[wi].shape[0]]

        def store(src, dst, wi, slot):
            cp = pltpu.make_async_copy(src, dst, local_sems.at[nw + 1 + wi * 8 + slot])
            cp.start()
            stores.append(cp)

        def store_half(wi, chip, half, slot):
            store(_half(slab(wi, chip), half, axes[wi]), _half(slab(wi, chip, out_refs), half, axes[wi]), wi, slot)
        x, y, cc = lax.axis_index("x"), lax.axis_index("y"), lax.axis_index("c")
        me = 4 * x + 2 * y + cc
        k = 2 * x + y
        sibling = (x, y, 1 - cc)
        links = _Links(send_sems, recv_sems)
        chips = [_peer(x, y, cc, off + (0,)) for off in CHIPS3]
        chip_a = ((x + 1 - cc) % 2, (y + cc) % 2, cc)
        chip_b = ((x + cc) % 2, (y + 1 - cc) % 2, cc)
        chip_d = (1 - x, 1 - y, cc)
        cv = c_ref[...]
        sc = cv * _sig(cv)
        mine = a_ref.at[me]
        for r in range(8):
            mine[r:r + 1, :] = sc[:, r * 128:(r + 1) * 128]
        for j, off in enumerate(PEERS7):
            links.send(a_ref.at[me], a_ref.at[me], j, _peer(x, y, cc, off))
        for wi in range(nw):
            loads[wi].wait()
            slab(wi, k)[...] = f_refs[wi][...].astype(BF16)
            for j, to in enumerate((chip_a, chip_b)):
                links.send(_half(slab(wi, k), cc, axes[wi]), _half(slab(wi, k), cc, axes[wi]), 10 + wi * 6 + j, to)
            store(slab(wi, k), slab(wi, k, out_refs), wi, 0)
            rows = shards[wi].shape[0]
            pad = slab_rows[wi] - rows
            if pad:
                for kk in range(4):
                    g_refs[wi][kk, rows:, :] = jnp.zeros((pad, shards[wi].shape[1]), BF16)
                store(g_refs[wi].at[:, pl.ds(rows, pad), :], out_refs[wi].at[:, pl.ds(rows, pad), :], wi, 7)
        for j, off in enumerate(PEERS7):
            px, py, pc = _peer(x, y, cc, off)
            links.arrived(a_ref.at[4 * px + 2 * py + pc], j, (px, py, pc))
        ccv = cc_ref[...]
        sub = lax.broadcasted_iota(jnp.int32, (8, 1), 0)
        top = jnp.zeros((8, D), F32)
        for d in range(8):
            blk = a_ref[d]
            row = jnp.concatenate([blk[r:r + 1, :] for r in range(8)], axis=1)
            top = top + jnp.where(sub == d, jnp.broadcast_to(row, (8, D)), 0.0)
        a16 = jnp.concatenate([top, jnp.where(sub == 0, jnp.broadcast_to(ccv * _sig(ccv), (8, D)), 0.0)], axis=0)
        a16_ref[...] = a16
        loads[nw].wait()
        mod_ref[k] = _dot3(_nn, a16, wm_ref[...]) + b_ref[...]
        for j, to in enumerate(chips):
            links.send(mod_ref.at[k], mod_ref.at[k], 7 + j, to)

        def over_ici(j, frm, origin):
            for wi in range(nw):
                blk = _half(slab(wi, 2 * origin[0] + origin[1]), cc, axes[wi])
                links.arrived(blk, 10 + wi * 6 + j, frm)
                if j == 0:
                    links.send(blk, blk, 10 + wi * 6 + 2, chip_b)
                links.send(blk, blk, 10 + wi * 6 + 3 + j, sibling)
                store_half(wi, 2 * origin[0] + origin[1], cc, 1 + j)

        def from_sibling(j, origin):
            for wi in range(nw):
                links.arrived(_half(slab(wi, 2 * origin[0] + origin[1]), 1 - cc, axes[wi]), 10 + wi * 6 + 3 + j, sibling)
                store_half(wi, 2 * origin[0] + origin[1], 1 - cc, 4 + j)
        over_ici(0, chip_a, chip_a)
        over_ici(1, chip_b, chip_b)
        from_sibling(0, chip_b)
        over_ici(2, chip_b, chip_d)
        from_sibling(1, chip_a)
        from_sibling(2, chip_d)
        for j, (px, py, pc) in enumerate(chips):
            links.arrived(mod_ref.at[2 * px + py], 7 + j, (px, py, pc))
        links.drain()
        for cp in stores:
            cp.wait()

    nsem = 10 + 6 * nw
    gathered = [(4, r, s.shape[1]) for r, s in zip(slab_rows, shards)]
    return pl.pallas_call(
        body, name="gather", in_specs=[VM, VM, ANY, VM] + [ANY] * nw, out_specs=[VM, VM] + [ANY] * nw,
        out_shape=[jax.ShapeDtypeStruct((16, D), F32), jax.ShapeDtypeStruct((4, 16, kw), F32)]
        + [jax.ShapeDtypeStruct(g, BF16) for g in gathered],
        scratch_shapes=[pltpu.VMEM(g, BF16) for g in gathered] + [pltpu.VMEM(s.shape, F32) for s in shards]
        + [pltpu.VMEM(w_mod.shape, F32), pltpu.VMEM((8, 8, D // 8), F32), pltpu.SemaphoreType.DMA((nsem,)),
           pltpu.SemaphoreType.DMA((nsem,)), pltpu.SemaphoreType.DMA((nw + 1 + 8 * nw,))],
        compiler_params=pltpu.CompilerParams(vmem_limit_bytes=VMEM_LIMIT),
    )(c, c_ctx, w_mod, b_mod_k, *shards)


SMALL_ROW_WIDTHS = (D, QL, KVL, DKP, DKP, 512, 128)
SMALL_OUT_WIDTHS = (D, QL, KVL, DK, DK, 512, 1)
SMALL_PACK = 384


def _small_pieces():
    pieces = []
    for i, w in enumerate(SMALL_ROW_WIDTHS):
        for c0 in range(0, w, 128):
            j = len(pieces)
            pieces.append((i, c0, j // (SMALL_PACK // 128), j % (SMALL_PACK // 128) * 128))
    assert len(pieces) <= 8 * (SMALL_PACK // 128)
    return pieces


def _reduce(grads, axes, smalls, w_pool_g, dmod8, a16, w_mod, c_ctx):
    nw = len(grads)
    ns = len(smalls)
    kw = w_mod.shape[1]
    halves = []
    for g, ax in zip(grads, axes):
        halves.append((g.shape[1] // 2, g.shape[2]) if ax == 0 else (g.shape[1], g.shape[2] // 2))

    def body(*refs):
        g_refs = refs[:nw]
        small_refs = refs[nw:nw + ns]
        wp_ref, dm_ref, a16_ref, wm_ref, cc_ref = refs[nw + ns:nw + ns + 5]
        o = nw + ns + 5
        r_refs = refs[o:o + nw]
        small_outs = refs[o + nw:o + nw + ns]
        rwp_ref, gw_ref, gb_ref, gc_ref = refs[o + nw + ns:o + nw + ns + 4]
        o = o + nw + ns + 4
        own, sib, part, got, rel = (refs[o + i * nw:o + (i + 1) * nw] for i in range(5))
        smbuf, wps, wpg, dm_all, pc_all, send_sems, recv_sems, local_sems = refs[o + 5 * nw:]
        x, y, cc = lax.axis_index("x"), lax.axis_index("y"), lax.axis_index("c")
        me = 4 * x + 2 * y + cc
        k = 2 * x + y
        sibling = (x, y, 1 - cc)
        links = _Links(send_sems, recv_sems)
        chips = [_peer(x, y, cc, off + (0,)) for off in CHIPS3]
        peers = [_peer(x, y, cc, off) for off in PEERS7]
        chip_a = ((x + 1 - cc) % 2, (y + cc) % 2, cc)
        chip_b = ((x + cc) % 2, (y + 1 - cc) % 2, cc)
        ka, kb, kd = 2 * chip_a[0] + chip_a[1], 2 * chip_b[0] + chip_b[1], 2 * (1 - x) + (1 - y)
        big, sm0, wp0, dm0, pc0 = 0, 5 * nw, 5 * nw + 7, 5 * nw + 14, 5 * nw + 21

        locals_ = []
        for wi in range(nw):
            lc = pltpu.make_async_copy(_half(g_refs[wi], cc, axes[wi]), own[wi], local_sems.at[wi])
            lc.start()
            locals_.append(lc)
            links.send(_half(g_refs[wi], 1 - cc, axes[wi]), sib[wi], big + wi * 5, sibling)
        slot = smbuf.at[me]
        slot[...] = jnp.zeros((8, SMALL_PACK), F32)
        small_rows = [jnp.broadcast_to(ref[...], (1, w)) for ref, w in zip(small_refs, SMALL_ROW_WIDTHS)]
        for i, c0, row, lane in _small_pieces():
            slot[row:row + 1, lane:lane + 128] = small_rows[i][:, c0:c0 + 128]
        links.send(wp_ref, wps, wp0, sibling)
        dm_all[me] = dm_ref[...]
        for j, peer in enumerate(peers):
            links.send(dm_all.at[me], dm_all.at[me], dm0 + j, peer)
            links.send(smbuf.at[me], smbuf.at[me], sm0 + j, peer)
        links.arrived(wps, wp0, sibling)
        wpg[k] = (wp_ref[...] + wps[...]).astype(BF16)
        for j, to in enumerate(chips):
            links.send(wpg.at[k], wpg.at[k], wp0 + 1 + j, to)
        for wi in range(nw):
            locals_[wi].wait()
            links.arrived(sib[wi], big + wi * 5, sibling)
            part[wi][...] = (own[wi][...] + sib[wi][...]).astype(BF16)
            got[wi][k] = part[wi][k]
            got[wi][kd] = jnp.zeros(halves[wi], BF16)
            links.send(part[wi].at[kd], rel[wi], big + wi * 5 + 1, chip_b)
            links.send(part[wi].at[kb], got[wi].at[k], big + wi * 5 + 2, chip_b)
        for j, (px, py, pc) in enumerate(peers):
            links.arrived(dm_all.at[4 * px + 2 * py + pc], dm0 + j, (px, py, pc))
        dm_tot = dm_all[0]
        for d in range(1, 8):
            dm_tot = dm_tot + dm_all[d]
        for kk in range(4):
            gb_ref[:, kk * kw:(kk + 1) * kw] = dm_tot[kk:kk + 1, :] + dm_tot[4 + kk:5 + kk, :]
        top = jnp.zeros((8, kw), F32)
        dmc_k = jnp.zeros((1, kw), F32)
        for kk in range(4):
            top = top + jnp.where(k == kk, _select_rows(dm_all, 8, kk), 0.0)
            dmc_k = dmc_k + jnp.where(k == kk, dm_tot[4 + kk:5 + kk, :], 0.0)
        sub = lax.broadcasted_iota(jnp.int32, (8, 1), 0)
        bk = jnp.concatenate([top, jnp.where(sub == 0, jnp.broadcast_to(dmc_k, (8, kw)), 0.0)], axis=0)
        gw_ref[...] = _dot3(_tn, a16_ref[...], bk)
        pc_all[k] = _dot3(_nt, jnp.broadcast_to(bk[8:9, :], (8, kw)), wm_ref[...])
        for j, to in enumerate(chips):
            links.send(pc_all.at[k], pc_all.at[k], pc0 + j, to)
        for j, (px, py, pc) in enumerate(peers):
            links.arrived(smbuf.at[4 * px + 2 * py + pc], sm0 + j, (px, py, pc))
        tot = smbuf[0]
        for d in range(1, 8):
            tot = tot + smbuf[d]
        for i, c0, row, lane in _small_pieces():
            n = min(128, SMALL_OUT_WIDTHS[i] - c0)
            if n > 0:
                small_outs[i][:, c0:c0 + n] = tot[row:row + 1, lane:lane + n]
        for j, (px, py, pc) in enumerate(chips):
            links.arrived(wpg.at[2 * px + py], wp0 + 1 + j, (px, py, pc))
        wpt = wpg[0].astype(F32)
        for kk in range(1, 4):
            wpt = wpt + wpg[kk].astype(F32)
        rwp_ref[...] = wpt
        for wi in range(nw):
            links.arrived(rel[wi], big + wi * 5 + 1, chip_b)
            rel[wi][...] = (part[wi][ka].astype(F32) + rel[wi][...].astype(F32)).astype(BF16)
            links.send(rel[wi], got[wi].at[k], big + wi * 5 + 3, chip_a)
        for wi in range(nw):
            links.arrived(got[wi].at[kb], big + wi * 5 + 2, chip_b)
            links.arrived(got[wi].at[ka], big + wi * 5 + 3, chip_a)
            total = got[wi][0].astype(F32)
            for kk in range(1, 4):
                total = total + got[wi][kk].astype(F32)
            mine = _half(r_refs[wi], cc, axes[wi])
            mine[...] = total
            links.send(mine, mine, big + wi * 5 + 4, sibling)
        for j, (px, py, pc) in enumerate(chips):
            links.arrived(pc_all.at[2 * px + py], pc0 + j, (px, py, pc))
        ccv = cc_ref[...]
        sg = _sig(ccv)
        gc_ref[...] = (pc_all[0][0:1, :] + pc_all[1][0:1, :] + pc_all[2][0:1, :] + pc_all[3][0:1, :]) * (sg * (1.0 + ccv * (1.0 - sg)))
        for wi in range(nw):
            links.arrived(_half(r_refs[wi], 1 - cc, axes[wi]), big + wi * 5 + 4, sibling)
        links.drain()

    nsem = 5 * nw + 24
    quads = [(4,) + h for h in halves]
    return pl.pallas_call(
        body, name="reduce", in_specs=[ANY] * nw + [VM] * (ns + 5), out_specs=[VM] * (nw + ns + 4),
        out_shape=[jax.ShapeDtypeStruct(g.shape[1:], F32) for g in grads]
        + [jax.ShapeDtypeStruct((1, w), F32) for w in SMALL_OUT_WIDTHS]
        + [jax.ShapeDtypeStruct(w_pool_g.shape, F32), jax.ShapeDtypeStruct((D, kw), F32), jax.ShapeDtypeStruct((1, 3 * D), F32),
           jax.ShapeDtypeStruct((1, D), F32)],
        scratch_shapes=[pltpu.VMEM(q, F32) for q in quads] + [pltpu.VMEM(q, F32) for q in quads]
        + [pltpu.VMEM(q, BF16) for q in quads] + [pltpu.VMEM(q, BF16) for q in quads] + [pltpu.VMEM(h, BF16) for h in halves]
        + [pltpu.VMEM((8, 8, SMALL_PACK), F32), pltpu.VMEM(w_pool_g.shape, F32), pltpu.VMEM((4,) + w_pool_g.shape, BF16),
           pltpu.VMEM((8, 8, kw), F32), pltpu.VMEM((4, 8, D), F32)]
        + [pltpu.SemaphoreType.DMA((nsem,)), pltpu.SemaphoreType.DMA((nsem,)), pltpu.SemaphoreType.DMA((nw,))],
        compiler_params=pltpu.CompilerParams(vmem_limit_bytes=VMEM_LIMIT),
    )(*grads, *smalls, w_pool_g, dmod8, a16, w_mod, c_ctx)


def _rope_tables(s_len):
    rows = s_len // GRID_W
    per = TB // GRID_W
    n_freq = 16
    inv = ROPE_BASE ** (-jnp.arange(n_freq, dtype=F32) / n_freq)
    ang_r = jnp.arange(rows, dtype=F32)[:, None] * inv
    ang_c = jnp.arange(GRID_W, dtype=F32)[:, None] * inv
    by_row, by_col = [], []
    for fn, pad in ((jnp.cos, 1.0), (jnp.sin, 0.0)):
        r = jnp.concatenate([fn(ang_r), fn(ang_r), jnp.zeros((rows, 96), F32)], axis=1).reshape(rows // per, per, 128)
        by_row.append(jnp.pad(r, ((0, 0), (0, 8 - per), (0, 0))))
        cpart = jnp.concatenate([jnp.zeros((GRID_W, 32), F32), fn(ang_c), fn(ang_c), jnp.full((GRID_W, 64), pad, F32)], axis=1)
        by_col.append(jnp.tile(cpart, (per, 1)))
    return jnp.concatenate(by_row, axis=-1), jnp.concatenate(by_col, axis=-1)


def kernel(x, c, ctx, c_ctx, w_mod, b_mod, norm_g, w_in, q_lora_g, w_uq, kv_lora_g, w_ukv, q_norm_g, k_norm_g, w_pool, pool_scale, w_out, loss_target, m_c_ctx, m_w_mod, m_b_mod, m_norm_g, m_w_in, m_q_lora_g, m_w_uq, m_kv_lora_g, m_w_ukv, m_q_norm_g, m_k_norm_g, m_w_pool, m_pool_scale, m_w_out, v_c_ctx, v_w_mod, v_b_mod, v_norm_g, v_w_in, v_q_lora_g, v_w_uq, v_kv_lora_g, v_w_ukv, v_q_norm_g, v_k_norm_g, v_w_pool, v_pool_scale, v_w_out):
    xi, yi, ci = lax.axis_index("x"), lax.axis_index("y"), lax.axis_index("c")
    me = 4 * xi + 2 * yi + ci
    k = 2 * xi + yi
    s_len = x.shape[1]
    lc = ctx.shape[1]
    kw = w_mod.shape[2]
    weights = dict(c_ctx=c_ctx, w_mod=w_mod, b_mod=b_mod, norm_g=norm_g, w_in=w_in, q_lora_g=q_lora_g, w_uq=w_uq,
                   kv_lora_g=kv_lora_g, w_ukv=w_ukv, q_norm_g=q_norm_g, k_norm_g=k_norm_g, w_pool=w_pool,
                   pool_scale=pool_scale, w_out=w_out)
    m_in = dict(c_ctx=m_c_ctx, w_mod=m_w_mod, b_mod=m_b_mod, norm_g=m_norm_g, w_in=m_w_in, q_lora_g=m_q_lora_g, w_uq=m_w_uq,
                kv_lora_g=m_kv_lora_g, w_ukv=m_w_ukv, q_norm_g=m_q_norm_g, k_norm_g=m_k_norm_g, w_pool=m_w_pool,
                pool_scale=m_pool_scale, w_out=m_w_out)
    v_in = dict(c_ctx=v_c_ctx, w_mod=v_w_mod, b_mod=v_b_mod, norm_g=v_norm_g, w_in=v_w_in, q_lora_g=v_q_lora_g, w_uq=v_w_uq,
                kv_lora_g=v_kv_lora_g, w_ukv=v_w_ukv, q_norm_g=v_q_norm_g, k_norm_g=v_k_norm_g, w_pool=v_w_pool,
                pool_scale=v_pool_scale, w_out=v_w_out)
    order = ["c_ctx", "w_mod", "b_mod", "norm_g", "w_in", "q_lora_g", "w_uq", "kv_lora_g", "w_ukv", "q_norm_g", "k_norm_g",
             "w_pool", "pool_scale", "w_out"]
    transposed = ("w_in", "w_uq")
    as2d = lambda n, a: jnp.transpose(a[0]) if n in transposed else a.reshape(-1, a.shape[-1])
    back = lambda n, a: jnp.transpose(a)[None] if n in transposed else a.reshape(weights[n].shape)

    c_ctx2 = c_ctx.reshape(1, D)
    b_mod_k = lax.dynamic_slice(b_mod, (0, k * kw), (1, kw))
    split = (1, 0, 0, 0)
    a16, mod_all, g_in, g_uq, g_ukv, g_out = _gather(
        c, c_ctx2, w_mod[0], b_mod_k, [as2d("w_in", w_in), as2d("w_uq", w_uq), w_ukv[0], w_out[0]], split,
        (DIN // 4, DKP, KVL, D // 4))
    mod_me = lax.dynamic_index_in_dim(mod_all, me, axis=1, keepdims=False).reshape(3, D)
    mod_c = mod_all[:, 8, :].reshape(3, D)
    modsel = jnp.stack([mod_c, mod_me])
    w_in_t = g_in.reshape(DIN, D)
    w_uq_t = g_uq
    w_out_f = g_out.reshape(D, D)
    qn_g = jnp.pad(q_norm_g, ((0, 0), (0, DKP - DK)))
    kn_g = jnp.pad(k_norm_g, ((0, 0), (0, DKP - DK)))
    w_pool_b = w_pool[0].astype(BF16)
    cos, sin = _rope_tables(s_len)

    u, q, kk, v = _fwd_in(ctx[0], x[0], modsel, norm_g, w_in_t, q_lora_g, w_uq_t, kv_lora_g, g_ukv, qn_g, kn_g, cos, sin)
    attn, lse = _attn_fwd(q, kk, v, s_len)
    (dxn, dattn, dga, dgp, dpool, dw_out, dgate, dps, dw_pool, loss) = _out_stage(
        attn.reshape(s_len // Q_BLOCK, Q_BLOCK, NH * DV), u, x[0], loss_target[0], modsel[1, 2:3, :], w_pool_b, pool_scale,
        w_out_f, lc)
    dattn = dattn.reshape(s_len, NH * DV)
    dq, dk, dv = _attn_bwd(q, kk, v, dattn, attn, lse, s_len)
    dlo, dw_uq_t, dw_ukv, dqlg, dkvlg, dqng, dkng = _qkv_bwd(u, dq, dk, dv, cos, sin, q_lora_g, w_uq_t, kv_lora_g, g_ukv,
                                                            qn_g, kn_g, s_len)
    gx, dw_in_t, dmod, dng = _in_bwd(ctx[0], x[0], modsel, norm_g, dlo, dga, dgp, dpool, dxn, w_in_t)

    dmod_l = jnp.concatenate([dmod[1, 0], dmod[1, 1], dgate[0]]).reshape(4, kw)
    dmod_c = jnp.concatenate([dmod[0, 0], dmod[0, 1], jnp.zeros((D,), F32)]).reshape(4, kw)
    dmod8 = jnp.concatenate([dmod_l, dmod_c], axis=0)
    r_in, r_uq, r_ukv, r_out, g_ng, g_qlg, g_kvlg, g_qng, g_kng, g_ps, loss_all, g_wp, g_w_mod, g_b_mod, g_c_ctx = _reduce(
        [dw_in_t.reshape(4, DIN // 4, D), dw_uq_t, dw_ukv, dw_out.reshape(4, D // 4, D)], split,
        [dng, dqlg, dkvlg, dqng, dkng, dps, loss], dw_pool, dmod8, a16, w_mod[0], c_ctx2)
    g2d = dict(c_ctx=g_c_ctx, b_mod=g_b_mod, w_mod=g_w_mod, w_in=r_in, w_uq=r_uq, w_ukv=r_ukv, w_out=r_out, norm_g=g_ng,
               q_lora_g=g_qlg, kv_lora_g=g_kvlg, q_norm_g=g_qng, k_norm_g=g_kng, pool_scale=g_ps, w_pool=g_wp.reshape(512, 128))

    outs = _adamw_many([as2d(n, weights[n]) for n in order], [g2d[n] for n in order], [as2d(n, m_in[n]) for n in order],
                       [as2d(n, v_in[n]) for n in order])
    d2d, m2d, v2d, g2d = (dict(zip(order, arrs)) for arrs in outs)

    return (loss_all[0, 0], gx[None], *[back(n, g2d[n]) for n in order], *[back(n, d2d[n]) for n in order],
            *[back(n, m2d[n]) for n in order], *[back(n, v2d[n]) for n in order])
```

```python
import jax
import jax.numpy as jnp
from jax import lax
from jax.experimental import pallas as pl
from jax.experimental.pallas import tpu as pltpu

F32 = jnp.float32
BF16 = jnp.bfloat16
MESH = pl.DeviceIdType.MESH

D = 1024
NH = 4
DK = 192
DKP = 256
DV = 128
QL = 256
KVL = 128
DIN = 1984
U_LO = 448
SEG = ((0, 512), (448, 960), (960, 1472), (1472, 1984))
DU = 2048
POOL_WINDOWS = (2, 4, 8, 16)
HALO = 8
EPS = 1e-6
ROPE_BASE = 10000.0
GRID_W = 64
Q_BLOCK = 128
TB = 256
BWD_QBLOCKS = 1
SCALE = DK ** -0.5
LOG2E = 1.4426950408889634
LN2 = 0.6931471805599453
VMEM_LIMIT = 56 * 1024 * 1024

ADAM_LR = 0.001
ADAM_B1 = 0.9
ADAM_B2 = 0.999
ADAM_EPS = 1e-08
ADAM_WD = 0.01
ADAM_STEP = 10

CHIPS3 = ((1, 0), (0, 1), (1, 1))
PEERS7 = tuple((dx, dy, dc) for dx in (0, 1) for dy in (0, 1) for dc in (0, 1) if (dx, dy, dc) != (0, 0, 0))

VM = pl.BlockSpec(memory_space=pltpu.VMEM)
ANY = pl.BlockSpec(memory_space=pl.ANY)


def _nn(a, b):
    return jnp.dot(a, b, preferred_element_type=F32)


def _nt(a, b):
    return lax.dot_general(a, b, (((1,), (1,)), ((), ())), preferred_element_type=F32)


def _tn(a, b):
    return lax.dot_general(a, b, (((0,), (0,)), ((), ())), preferred_element_type=F32)


def _split3(a):
    a0 = a.astype(BF16)
    r = a - a0.astype(F32)
    a1 = r.astype(BF16)
    a2 = (r - a1.astype(F32)).astype(BF16)
    return a0, a1, a2


def _dot3(dot, a, b):
    sa = _split3(a)
    sb = _split3(b)
    out = None
    for i in range(3):
        for j in range(3 - i):
            t = dot(sa[i], sb[j])
            out = t if out is None else out + t
    return out


def _sig(x):
    return 1.0 / (1.0 + jnp.exp(-x))


def _rot(t):
    src = lax.broadcasted_iota(jnp.int32, (128, 128), 0)
    dst = lax.broadcasted_iota(jnp.int32, (128, 128), 1)
    first = (dst % 32) < 16
    perm = jnp.where(first & (src == dst + 16), -1.0, jnp.where(~first & (src == dst - 16), 1.0, 0.0)).astype(BF16)
    hi = t.astype(BF16)
    lo = (t - hi.astype(F32)).astype(BF16)
    return _nn(hi, perm) + _nn(lo, perm)


def _rope(t, cos, sin):
    return t * cos + _rot(t) * sin


def _rope_t(t, cos, sin):
    return t * cos - _rot(t * sin)


def _rope_block(rows_ref, cols_ref, is_ctx):
    lane = lax.broadcasted_iota(jnp.int32, (TB, 256), 1) % 128
    rows = jnp.concatenate([jnp.broadcast_to(rows_ref[0, r:r + 1, :], (GRID_W, 256)) for r in range(TB // GRID_W)], axis=0)
    cs = jnp.where(lane < 32, rows, cols_ref[...])
    return jnp.where(is_ctx, 1.0, cs[:, :128]), jnp.where(is_ctx, 0.0, cs[:, 128:])


def _shift_rows(z, k):
    n = z.shape[0]
    return pltpu.roll(z, (n - k) % n, 0)


def _colsum(a):
    return jnp.sum(a, axis=0, keepdims=True)


def _rowsum(a):
    return jnp.sum(a, axis=-1, keepdims=True)


def _row_layout(col):
    return jnp.transpose(jnp.broadcast_to(col, (col.shape[0], 128)))[0:8, :]


def _params(sem=None):
    return pltpu.CompilerParams(dimension_semantics=sem, vmem_limit_bytes=VMEM_LIMIT)


def _full(shape):
    nd = len(shape)
    return pl.BlockSpec(shape, lambda *_: (0,) * nd)


def _peer(x, y, c, off):
    dx, dy, dc = off
    return ((x + dx) % 2, (y + dy) % 2, (c + dc) % 2)


def _token_specs(off):
    ctx = pl.BlockSpec((TB, D), lambda i: (jnp.minimum(i, off - 1), 0))
    lat = pl.BlockSpec((TB, D), lambda i: (jnp.maximum(i - off, 0), 0))
    mod = pl.BlockSpec((1, 3, D), lambda i: (jnp.minimum(i // off, 1), 0, 0))
    return ctx, lat, mod


def _modulated(x, mod_ref, ng):
    shift = mod_ref[0, 0:1, :]
    scale = mod_ref[0, 1:2, :]
    r = lax.rsqrt(jnp.mean(x * x, axis=-1, keepdims=True) + EPS)
    xh = x * r
    xg = xh * ng
    return r, xh, xg, xg * (1.0 + scale) + shift, scale


def _fwd_in(ctx, x, modsel, norm_g, w_in_t, q_lora_g, w_uq_t, kv_lora_g, w_ukv, qn_g, kn_g, cos, sin):
    s_len, lc = x.shape[0], ctx.shape[0]
    t_all = s_len + lc
    nb = t_all // TB
    off = lc // TB

    def body(ctx_ref, x_ref, mod_ref, ng_ref, win_ref, qlg_ref, wuq_ref, kvlg_ref, wukv_ref, qng_ref, kng_ref, cos_ref, sin_ref,
             u_ref, q_ref, k_ref, v_ref):
        is_ctx = pl.program_id(0) < off
        xb = jnp.where(is_ctx, ctx_ref[...], x_ref[...])
        _, _, _, h, _ = _modulated(xb, mod_ref, ng_ref[...])
        hb = h.astype(BF16)
        lane = lax.broadcasted_iota(jnp.int32, (TB, 512), 1)
        ulo = jnp.where(lane < U_LO, _nt(hb, win_ref[SEG[0][0]:SEG[0][1], :]), 0.0)
        u_ref[:, 0:512] = ulo
        for j in range(1, 4):
            u_ref[:, j * 512:(j + 1) * 512] = _nt(hb, win_ref[SEG[j][0]:SEG[j][1], :])
        cos, sin = _rope_block(cos_ref, sin_ref, is_ctx)
        cq = ulo[:, 0:QL]
        cqn = (cq * lax.rsqrt(jnp.mean(cq * cq, axis=-1, keepdims=True) + EPS) * qlg_ref[...]).astype(BF16)
        qng = qng_ref[...]
        ckv = ulo[:, QL:QL + KVL]
        ckvn = (ckv * lax.rsqrt(jnp.mean(ckv * ckv, axis=-1, keepdims=True) + EPS) * kvlg_ref[...]).astype(BF16)
        qhs = [_nt(cqn, wuq_ref[hd]) for hd in range(NH)]
        kvs = [_nn(ckvn, wukv_ref[hd]) for hd in range(NH)]
        for hd in range(NH):
            qh = qhs[hd]
            qn = qh * lax.rsqrt(_rowsum(qh * qh) / DK + EPS) * qng
            q_ref[hd] = (jnp.concatenate([qn[:, :128], _rope(qn[:, 128:], cos, sin)], axis=1) * (SCALE * LOG2E)).astype(BF16)
        kr = ulo[:, 384:512]
        skr = _rowsum(kr * kr)
        kng = kng_ref[...]
        kr_roped = _rope(kr * kng[:, 128:], cos, sin)
        for hd in range(NH):
            kv = kvs[hd]
            kn = kv[:, :128]
            rk = lax.rsqrt((_rowsum(kn * kn) + skr) / DK + EPS)
            k_ref[hd] = jnp.concatenate([kn * rk * kng[:, :128], kr_roped * rk], axis=1).astype(BF16)
            v_ref[hd] = kv[:, 128:].astype(BF16)

    row = lambda w: pl.BlockSpec((TB, w), lambda i: (i, 0))
    heads = lambda w: pl.BlockSpec((NH, TB, w), lambda i: (0, i, 0))
    cspec, xspec, mspec = _token_specs(off)
    return pl.pallas_call(
        body, name="fwd_in", grid=(nb,),
        in_specs=[cspec, xspec, mspec, _full((1, D)), _full((DIN, D)), _full((1, QL)), _full((NH, DKP, QL)), _full((1, KVL)),
                  _full((NH, KVL, 256)), _full((1, DKP)), _full((1, DKP)),
                  pl.BlockSpec((1, 8, 256), lambda i: (jnp.maximum(i - off, 0), 0, 0)), _full((TB, 256))],
        out_specs=[row(DU), heads(DKP), heads(DKP), heads(DV)],
        out_shape=[jax.ShapeDtypeStruct((t_all, DU), F32), jax.ShapeDtypeStruct((NH, t_all, DKP), BF16),
                   jax.ShapeDtypeStruct((NH, t_all, DKP), BF16), jax.ShapeDtypeStruct((NH, t_all, DV), BF16)],
        compiler_params=_params(("arbitrary",)),
    )(ctx, x, modsel, norm_g, w_in_t, q_lora_g, w_uq_t, kv_lora_g, w_ukv, qn_g, kn_g, cos, sin)


def _attn_fwd(q, k, v, s_len):
    t_all = q.shape[1]
    off = (t_all - s_len) // TB
    nq = s_len // TB
    nsub = next(n for n in (4, 2, 1) if nq % n == 0)

    def body(*refs):
        q_refs = refs[:nsub]
        k_ref, v_ref, o_ref, lse_ref = refs[nsub:]
        for sb in range(nsub):
            s = _nt(q_refs[sb][0], k_ref[0])
            m = jnp.max(s, axis=-1, keepdims=True)
            e = jnp.exp2(s - m)
            l = _rowsum(e)
            o_ref[sb * TB:(sb + 1) * TB, :] = _nn(e.astype(BF16), v_ref[0]) / l
            lse_ref[0, sb] = _row_layout(m + jnp.log2(l))

    qspec = lambda sb: pl.BlockSpec((1, TB, DKP), lambda h, i: (h, i * nsub + sb + off, 0))
    return pl.pallas_call(
        body, name="attn_fwd", grid=(NH, nq // nsub),
        in_specs=[qspec(sb) for sb in range(nsub)]
        + [pl.BlockSpec((1, t_all, DKP), lambda h, i: (h, 0, 0)), pl.BlockSpec((1, t_all, DV), lambda h, i: (h, 0, 0))],
        out_specs=[pl.BlockSpec((nsub * TB, DV), lambda h, i: (i, h)), pl.BlockSpec((1, nsub, 8, TB), lambda h, i: (h, i, 0, 0))],
        out_shape=[jax.ShapeDtypeStruct((s_len, NH * DV), F32), jax.ShapeDtypeStruct((NH, nq, 8, TB), F32)],
        compiler_params=_params(("arbitrary", "arbitrary")),
    )(*([q] * nsub), k, v)


def _out_stage(attn, u, x, target, gate, w_pool, pool_scale, w_out, lc):
    s_len = x.shape[0]
    t_all = s_len + lc
    off = lc // TB
    nq = s_len // TB
    hb = TB // HALO
    nqb = s_len // Q_BLOCK
    jb = TB // nqb

    def body(attn_ref, ga_ref, pin_ref, pprev_ref, pnext_ref, gp_ref, x_ref, tgt_ref, gate_ref, wp_ref, ps_ref, wo_ref,
             dxn_ref, dattn_ref, dga_ref, dgp_ref, dpool_ref, dwo_ref, dgate_ref, dps_ref, dwp_ref, loss_ref):
        i = pl.program_id(0)

        @pl.when(i == 0)
        def _():
            dwo_ref[...] = jnp.zeros_like(dwo_ref)
            dgate_ref[...] = jnp.zeros_like(dgate_ref)
            dps_ref[...] = jnp.zeros_like(dps_ref)
            dwp_ref[...] = jnp.zeros_like(dwp_ref)
            loss_ref[...] = jnp.zeros_like(loss_ref)

        attn = jnp.concatenate([attn_ref[:, jj, :] for jj in range(jb)], axis=0)
        ga = ga_ref[...]
        gp = gp_ref[...]
        pin = pin_ref[...]
        prev = jnp.where(i == 0, 0.0, pprev_ref[...])
        nxt = jnp.where(i == nq - 1, 0.0, pnext_ref[...])
        win = jnp.concatenate([prev, pin, nxt], axis=0)
        tg = i * TB + lax.broadcasted_iota(jnp.int32, (TB, 1), 0)
        pooled = []
        for g, w in enumerate(POOL_WINDOWS):
            a = win[:, g * 128:(g + 1) * 128]
            p = _shift_rows(a, -1) + a
            for step in (1, 2, 4):
                if w >= 4 * step:
                    p = _shift_rows(p, -step) + _shift_rows(p, step)
            cnt = (jnp.minimum(tg + w // 2, s_len) - jnp.maximum(tg - w // 2, 0)).astype(F32)
            pooled.append(p[HALO:HALO + TB] / cnt - a[HALO:HALO + TB])
        pooled_b = [p.astype(BF16) for p in pooled]
        z = jnp.concatenate([_nn(pooled_b[g], wp_ref[g]) for g in range(4)], axis=1)
        ps = ps_ref[...]
        yp = z * ps
        sga = _sig(ga)
        sila = ga * sga
        sgp = _sig(gp)
        silp = gp * sgp
        br = jnp.concatenate([sila * attn, silp * yp], axis=1).astype(BF16)
        y = _nn(br, wo_ref[...])
        gate = gate_ref[...]
        err = x_ref[...] + gate * y - tgt_ref[...]
        loss_ref[...] += _colsum(_rowsum(err * err)) * (0.5 / D)
        dxn = err * (1.0 / D)
        dxn_ref[...] = dxn
        dgate_ref[...] += _colsum(dxn * y)
        dy = (dxn * gate).astype(BF16)
        dwo_ref[...] += _tn(br, dy)
        dbr = _nt(dy, wo_ref[...])
        dbra = dbr[:, :512]
        dbrp = dbr[:, 512:]
        dattn = dbra * sila
        for jj in range(jb):
            dattn_ref[:, jj, :] = dattn[jj * nqb:(jj + 1) * nqb]
        dga_ref[...] = (dbra * attn * (sga * (1.0 + ga * (1.0 - sga)))).astype(BF16)
        dgp_ref[...] = (dbrp * yp * (sgp * (1.0 + gp * (1.0 - sgp)))).astype(BF16)
        dyp = dbrp * silp
        dps_ref[...] += _colsum(dyp * z)
        dz = (dyp * ps).astype(BF16)
        dpool = []
        for g in range(4):
            dzg = dz[:, g * 128:(g + 1) * 128]
            dwp_ref[g] += _tn(pooled_b[g], dzg)
            dpool.append(_nt(dzg, wp_ref[g]))
        dpool_ref[...] = jnp.concatenate(dpool, axis=1)

    lat = lambda w: pl.BlockSpec((TB, w), lambda i: (i, 0))
    perm = pl.BlockSpec((nqb, jb, 512), lambda i: (0, i, 0))
    ucol = lambda j: pl.BlockSpec((TB, 512), lambda i: (i + off, j))
    last8 = t_all // HALO - 1
    return pl.pallas_call(
        body, name="out_stage", grid=(nq,),
        in_specs=[perm, ucol(1), ucol(2),
                  pl.BlockSpec((HALO, 512), lambda i: ((i + off) * hb - 1, 2)),
                  pl.BlockSpec((HALO, 512), lambda i: (jnp.minimum((i + off + 1) * hb, last8), 2)),
                  ucol(3), lat(D), lat(D), _full((1, D)), _full((4, 128, 128)), _full((1, 512)), _full((D, D))],
        out_specs=[lat(D), perm, lat(512), lat(512), lat(512),
                   _full((D, D)), _full((1, D)), _full((1, 512)), _full((4, 128, 128)), _full((1, 1))],
        out_shape=[jax.ShapeDtypeStruct((s_len, D), F32), jax.ShapeDtypeStruct((nqb, Q_BLOCK, 512), F32),
                   jax.ShapeDtypeStruct((s_len, 512), BF16), jax.ShapeDtypeStruct((s_len, 512), BF16),
                   jax.ShapeDtypeStruct((s_len, 512), F32),
                   jax.ShapeDtypeStruct((D, D), F32), jax.ShapeDtypeStruct((1, D), F32), jax.ShapeDtypeStruct((1, 512), F32),
                   jax.ShapeDtypeStruct((4, 128, 128), F32), jax.ShapeDtypeStruct((1, 1), F32)],
        compiler_params=_params(("arbitrary",)),
    )(attn, u, u, u, u, u, x, target, gate, w_pool, pool_scale, w_out)


def _attn_bwd(q, k, v, dattn, attn, lse, s_len):
    t_all = q.shape[1]
    off = (t_all - s_len) // TB
    nq = s_len // TB
    nch = 4
    chunks = [(c * (t_all // nch), t_all // nch) for c in range(nch)]
    nsub = next(n for n in (BWD_QBLOCKS, 2, 1) if nq % n == 0)
    tq = nsub * TB

    def body(*refs):
        q_refs = refs[:nsub]
        k_ref, v_ref, do_ref, o_ref, lse_ref, dq_ref, dk_ref, dv_ref = refs[nsub:]
        i = pl.program_id(1)

        @pl.when(i == 0)
        def _():
            dk_ref[...] = jnp.zeros_like(dk_ref)
            dv_ref[...] = jnp.zeros_like(dv_ref)

        qb = jnp.concatenate([r[0] for r in q_refs], axis=0)
        delta_r = _row_layout(_rowsum(do_ref[...] * o_ref[...]))[0:1, :]
        do = do_ref[...].astype(BF16)
        lse_r = jnp.concatenate([lse_ref[0, sb][0:1, :] for sb in range(nsub)], axis=1)
        dq = jnp.zeros((tq, DKP), F32)
        for start, size in chunks:
            rows = pl.ds(start, size)
            kc = k_ref[0, rows, :]
            p_t = jnp.exp2(_nt(kc, qb) - lse_r)
            ds_t = (p_t * (_nt(v_ref[0, rows, :], do) - delta_r)).astype(BF16)
            dv_ref[0, rows, :] += _nn(p_t.astype(BF16), do)
            dk_ref[0, rows, :] += _nn(ds_t, qb)
            dq += _tn(ds_t, kc)
        dq_ref[0] = dq * SCALE

    kvspec = lambda w: pl.BlockSpec((1, t_all, w), lambda h, i: (h, 0, 0))
    rowspec = pl.BlockSpec((1, nsub, 8, TB), lambda h, i: (h, i, 0, 0))
    qspec = lambda sb: pl.BlockSpec((1, TB, DKP), lambda h, i: (h, i * nsub + sb + off, 0))
    return pl.pallas_call(
        body, name="attn_bwd", grid=(NH, nq // nsub),
        in_specs=[qspec(sb) for sb in range(nsub)]
        + [kvspec(DKP), kvspec(DV), pl.BlockSpec((tq, DV), lambda h, i: (i, h)), pl.BlockSpec((tq, DV), lambda h, i: (i, h)),
           rowspec],
        out_specs=[pl.BlockSpec((1, tq, DKP), lambda h, i: (h, i, 0)), kvspec(DKP), kvspec(DV)],
        out_shape=[jax.ShapeDtypeStruct((NH, s_len, DKP), F32), jax.ShapeDtypeStruct((NH, t_all, DKP), F32),
                   jax.ShapeDtypeStruct((NH, t_all, DV), F32)],
        compiler_params=_params(("arbitrary", "arbitrary")),
    )(*([q] * nsub), k, v, dattn, attn, lse)


def _qkv_bwd(u, dq, dk, dv, cos, sin, q_lora_g, w_uq_t, kv_lora_g, w_ukv, qn_g, kn_g, s_len):
    t_all = u.shape[0]
    off = (t_all - s_len) // TB
    nb = t_all // TB

    def body(ulo_ref, dq_ref, dk_ref, dv_ref, cos_ref, sin_ref, qlg_ref, wuq_ref, kvlg_ref, wukv_ref, qng_ref, kng_ref,
             dlo_ref, dwuq_ref, dwukv_ref, dqlg_ref, dkvlg_ref, dqng_ref, dkng_ref):
        i = pl.program_id(0)

        @pl.when(i == 0)
        def _():
            for r in (dwuq_ref, dwukv_ref, dqlg_ref, dkvlg_ref, dqng_ref, dkng_ref):
                r[...] = jnp.zeros_like(r)

        latent = i >= off
        ulo = ulo_ref[...]
        cos, sin = _rope_block(cos_ref, sin_ref, pl.program_id(0) < off)
        cq = ulo[:, 0:QL]
        rc = lax.rsqrt(jnp.mean(cq * cq, axis=-1, keepdims=True) + EPS)
        cqh = cq * rc
        qlg = qlg_ref[...]
        cqn_b = (cqh * qlg).astype(BF16)
        qng = qng_ref[...]
        ckv = ulo[:, QL:QL + KVL]
        r0 = lax.rsqrt(jnp.mean(ckv * ckv, axis=-1, keepdims=True) + EPS)
        ckvh = ckv * r0
        kvlg = kvlg_ref[...]
        ckvn_b = (ckvh * kvlg).astype(BF16)
        qhs = [_nt(cqn_b, wuq_ref[hd]) for hd in range(NH)]
        kns = [_nn(ckvn_b, wukv_ref[hd])[:, :128] for hd in range(NH)]
        dqng = jnp.zeros((1, DKP), F32)
        dqraws = []
        for hd in range(NH):
            qh = qhs[hd]
            rq = lax.rsqrt(_rowsum(qh * qh) / DK + EPS)
            xh = qh * rq
            dqh = jnp.where(latent, dq_ref[hd], 0.0)
            dyq = jnp.concatenate([dqh[:, :128], _rope_t(dqh[:, 128:], cos, sin)], axis=1)
            dqng += _colsum(dyq * xh)
            dxh = dyq * qng
            dqraws.append((rq * (dxh - xh * (_rowsum(dxh * xh) / DK))).astype(BF16))
        dqng_ref[...] += dqng

        kr = ulo[:, 384:512]
        skr = _rowsum(kr * kr)
        kng = kng_ref[...]
        dkr = jnp.zeros((TB, 128), F32)
        dkng = jnp.zeros((1, DKP), F32)
        dkvs = []
        for hd in range(NH):
            kn = kns[hd]
            rk = lax.rsqrt((_rowsum(kn * kn) + skr) / DK + EPS)
            xh1 = kn * rk
            xh2 = kr * rk
            dkh = dk_ref[hd] * LN2
            d1 = dkh[:, :128]
            d2 = _rope_t(dkh[:, 128:], cos, sin)
            dkng += jnp.concatenate([_colsum(d1 * xh1), _colsum(d2 * xh2)], axis=1)
            dx1 = d1 * kng[:, :128]
            dx2 = d2 * kng[:, 128:]
            dot = (_rowsum(dx1 * xh1) + _rowsum(dx2 * xh2)) / DK
            dkvs.append(jnp.concatenate([rk * (dx1 - xh1 * dot), dv_ref[hd]], axis=1).astype(BF16))
            dkr += rk * (dx2 - xh2 * dot)
        dkng_ref[...] += dkng

        dcqn = jnp.zeros((TB, QL), F32)
        dckvn = jnp.zeros((TB, KVL), F32)
        for hd in range(NH):
            dwuq_ref[hd] += _tn(dqraws[hd], cqn_b)[:DK]
            dcqn += _nn(dqraws[hd], wuq_ref[hd])
            dwukv_ref[hd] += _tn(ckvn_b, dkvs[hd])
            dckvn += _nt(dkvs[hd], wukv_ref[hd])
        dqlg_ref[...] += _colsum(dcqn * cqh)
        dxh = dcqn * qlg
        dcq = rc * (dxh - cqh * jnp.mean(dxh * cqh, axis=-1, keepdims=True))
        dkvlg_ref[...] += _colsum(dckvn * ckvh)
        dxh = dckvn * kvlg
        dckv = r0 * (dxh - ckvh * jnp.mean(dxh * ckvh, axis=-1, keepdims=True))
        dlo_ref[...] = jnp.concatenate([dcq, dckv, dkr], axis=1).astype(BF16)

    row = lambda w: pl.BlockSpec((TB, w), lambda i: (i, 0))
    heads = lambda w: pl.BlockSpec((NH, TB, w), lambda i: (0, i, 0))
    return pl.pallas_call(
        body, name="qkv_bwd", grid=(nb,),
        in_specs=[row(512), pl.BlockSpec((NH, TB, DKP), lambda i: (0, jnp.maximum(i - off, 0), 0)), heads(DKP), heads(DV),
                  pl.BlockSpec((1, 8, 256), lambda i: (jnp.maximum(i - off, 0), 0, 0)), _full((TB, 256)), _full((1, QL)), _full((NH, DKP, QL)), _full((1, KVL)), _full((NH, KVL, 256)),
                  _full((1, DKP)), _full((1, DKP))],
        out_specs=[row(512), _full((NH, DK, QL)), _full((NH, KVL, 256)), _full((1, QL)), _full((1, KVL)),
                   _full((1, DKP)), _full((1, DKP))],
        out_shape=[jax.ShapeDtypeStruct((t_all, 512), BF16), jax.ShapeDtypeStruct((NH, DK, QL), F32),
                   jax.ShapeDtypeStruct((NH, KVL, 256), F32), jax.ShapeDtypeStruct((1, QL), F32),
                   jax.ShapeDtypeStruct((1, KVL), F32), jax.ShapeDtypeStruct((1, DKP), F32), jax.ShapeDtypeStruct((1, DKP), F32)],
        compiler_params=_params(("arbitrary",)),
    )(u, dq, dk, dv, cos, sin, q_lora_g, w_uq_t, kv_lora_g, w_ukv, qn_g, kn_g)


def _in_bwd(ctx, x, modsel, norm_g, dlo, dga, dgp, dpool, dxn, w_in_t):
    s_len, lc = x.shape[0], ctx.shape[0]
    t_all = s_len + lc
    off = lc // TB
    nb = t_all // TB
    nq = s_len // TB
    hb = TB // HALO
    n = TB + 2 * HALO

    def body(ctx_ref, x_ref, mod_ref, ng_ref, dlo_ref, dga_ref, dgp_ref, dp_ref, dpprev_ref, dpnext_ref, dxn_ref, win_ref,
             gx_ref, dwin_ref, dmod_ref, dng_ref):
        i = pl.program_id(0)
        j = i - off

        @pl.when(i == 0)
        def _():
            dwin_ref[...] = jnp.zeros_like(dwin_ref)
            dmod_ref[...] = jnp.zeros_like(dmod_ref)
            dng_ref[...] = jnp.zeros_like(dng_ref)

        latent = i >= off
        dp = dp_ref[...]
        prev = jnp.where(j <= 0, 0.0, dpprev_ref[...])
        nxt = jnp.where(j >= nq - 1, 0.0, dpnext_ref[...])
        win = jnp.concatenate([prev, dp, nxt], axis=0)
        tg = j * TB - HALO + lax.broadcasted_iota(jnp.int32, (n, 1), 0)
        dpin = []
        for g, w in enumerate(POOL_WINDOWS):
            cnt = jnp.maximum(jnp.minimum(tg + w // 2, s_len) - jnp.maximum(tg - w // 2, 0), 1).astype(F32)
            zq = win[:, g * 128:(g + 1) * 128] / cnt
            zq = zq + _shift_rows(zq, 1)
            for step in (1, 2, 4):
                if w >= 4 * step:
                    zq = _shift_rows(zq, -step) + _shift_rows(zq, step)
            dpin.append(zq[HALO:HALO + TB] - dp[:, g * 128:(g + 1) * 128])
        zero = jnp.zeros((TB, 512), BF16)
        du = [dlo_ref[...], jnp.where(latent, dga_ref[...], zero),
              jnp.where(latent, jnp.concatenate(dpin, axis=1).astype(BF16), zero), jnp.where(latent, dgp_ref[...], zero)]

        ng = ng_ref[...]
        xb = jnp.where(i < off, ctx_ref[...], x_ref[...])
        r, xh, xg, h, scale = _modulated(xb, mod_ref, ng)
        hb_ = h.astype(BF16)
        dh = jnp.zeros((TB, D), F32)
        for s, (lo, hi) in enumerate(SEG):
            dwin_ref[lo:hi, :] += _tn(du[s], hb_)
            dh += _nn(du[s], win_ref[lo:hi, :])
        is_lat = latent.astype(F32)
        dsh = _colsum(dh)
        dsc = _colsum(dh * xg)
        dmod_ref[0, 0:1, :] += dsh * (1.0 - is_lat)
        dmod_ref[0, 1:2, :] += dsc * (1.0 - is_lat)
        dmod_ref[1, 0:1, :] += dsh * is_lat
        dmod_ref[1, 1:2, :] += dsc * is_lat
        dxg = dh * (1.0 + scale)
        dng_ref[...] += _colsum(dxg * xh)
        dxh = dxg * ng
        gx_ref[...] = r * (dxh - xh * jnp.mean(dxh * xh, axis=-1, keepdims=True)) + dxn_ref[...]

    row = lambda w: pl.BlockSpec((TB, w), lambda i: (i, 0))
    lat = lambda w: pl.BlockSpec((TB, w), lambda i: (jnp.maximum(i - off, 0), 0))
    last8 = s_len // HALO - 1
    cspec, xspec, mspec = _token_specs(off)
    return pl.pallas_call(
        body, name="in_bwd", grid=(nb,),
        in_specs=[cspec, xspec, mspec, _full((1, D)), row(512), lat(512), lat(512), lat(512),
                  pl.BlockSpec((HALO, 512), lambda i: (jnp.maximum(jnp.maximum(i - off, 0) * hb - 1, 0), 0)),
                  pl.BlockSpec((HALO, 512), lambda i: (jnp.minimum((jnp.maximum(i - off, 0) + 1) * hb, last8), 0)),
                  lat(D), _full((DIN, D))],
        out_specs=[lat(D), _full((DIN, D)), _full((2, 2, D)), _full((1, D))],
        out_shape=[jax.ShapeDtypeStruct((s_len, D), F32), jax.ShapeDtypeStruct((DIN, D), F32),
                   jax.ShapeDtypeStruct((2, 2, D), F32), jax.ShapeDtypeStruct((1, D), F32)],
        compiler_params=_params(("arbitrary",)),
    )(ctx, x, modsel, norm_g, dlo, dga, dgp, dpool, dpool, dpool, dxn, w_in_t)


def _adamw_update(w_ref, g_ref, m_ref, v_ref, d_ref, mo_ref, vo_ref):
    gv = g_ref[...]
    mn = ADAM_B1 * m_ref[...] + (1.0 - ADAM_B1) * gv
    vn = ADAM_B2 * v_ref[...] + (1.0 - ADAM_B2) * (gv * gv)
    m_hat = mn / (1.0 - ADAM_B1 ** ADAM_STEP)
    v_hat = vn / (1.0 - ADAM_B2 ** ADAM_STEP)
    d_ref[...] = -ADAM_LR * (m_hat / (jnp.sqrt(v_hat) + ADAM_EPS) + ADAM_WD * w_ref[...])
    mo_ref[...] = mn
    vo_ref[...] = vn


def _adamw_many(ws, gs, ms, vs):
    n = len(ws)
    parts = 4

    def body(*refs):
        for i in range(n):
            _adamw_update(refs[i], refs[n + i], refs[2 * n + i], refs[3 * n + i], refs[4 * n + i], refs[5 * n + i], refs[6 * n + i])
            refs[7 * n + i][...] = refs[n + i][...]

    def spec(w):
        rows, cols = w.shape
        if rows % (8 * parts) == 0:
            return pl.BlockSpec((rows // parts, cols), lambda i: (i, 0))
        if cols % (128 * parts) == 0:
            return pl.BlockSpec((rows, cols // parts), lambda i: (0, i))
        return _full((rows, cols))

    specs = [spec(w) for w in ws]
    shp = [jax.ShapeDtypeStruct(w.shape, F32) for w in ws]
    out = pl.pallas_call(body, name="adamw_many", grid=(parts,), in_specs=specs * 4, out_specs=specs * 4, out_shape=shp * 4,
                         compiler_params=_params(("arbitrary",)))(*ws, *gs, *ms, *vs)
    return out[:n], out[n:2 * n], out[2 * n:3 * n], out[3 * n:]


class _Links:
    def __init__(self, send_sems, recv_sems):
        self.send_sems, self.recv_sems, self.sends = send_sems, recv_sems, []

    def send(self, src, dst, sem, to):
        cp = pltpu.make_async_remote_copy(src, dst, self.send_sems.at[sem], self.recv_sems.at[sem], device_id=to,
                                          device_id_type=MESH)
        cp.start()
        self.sends.append(cp)

    def arrived(self, dst, sem, frm):
        pltpu.make_async_remote_copy(dst, dst, self.send_sems.at[sem], self.recv_sems.at[sem], device_id=frm,
                                     device_id_type=MESH).wait_recv()

    def drain(self):
        for cp in self.sends:
            cp.wait_send()


def _half(ref, c, axis):
    size = ref.shape[axis - 2] // 2
    win = pl.ds(pl.multiple_of(c * size, 16 if axis == 0 else 128), size)
    idx = (win, slice(None)) if axis == 0 else (slice(None), win)
    return ref.at[(slice(None),) * (len(ref.shape) - 2) + idx]


def _select_rows(slots_ref, n_slots, row=0):
    sub = lax.broadcasted_iota(jnp.int32, (8, 1), 0)
    out = None
    for d in range(n_slots):
        r = jnp.where(sub == d, jnp.broadcast_to(slots_ref[d][row:row + 1, :], (8, slots_ref.shape[-1])), 0.0)
        out = r if out is None else out + r
    return out


def _gather(c, c_ctx, w_mod, b_mod_k, shards, axes, slab_rows):
    nw = len(shards)
    kw = w_mod.shape[1]

    def body(*refs):
        c_ref, cc_ref, wm_hbm, b_ref = refs[:4]
        w_hbm = refs[4:4 + nw]
        a16_ref, mod_ref = refs[4 + nw:6 + nw]
        out_refs = refs[6 + nw:6 + 2 * nw]
        g_refs = refs[6 + 2 * nw:6 + 3 * nw]
        f_refs = refs[6 + 3 * nw:6 + 4 * nw]
        wm_ref, a_ref, send_sems, recv_sems, local_sems = refs[6 + 4 * nw:]
        loads = [pltpu.make_async_copy(w_hbm[wi], f_refs[wi], local_sems.at[wi]) for wi in range(nw)]
        loads.append(pltpu.make_async_copy(wm_hbm, wm_ref, local_sems.at[nw]))
        for cp in loads:
            cp.start()
        stores = []

        def slab(wi, chip, of=g_refs):
            return of[wi].at[chip].at[0:shards[wi].shape[0]]

        def store(src, dst, wi, slot):
            cp = pltpu.make_async_copy(src, dst, local_sems.at[nw + 1 + wi * 8 + slot])
            cp.start()
            stores.append(cp)

        def store_half(wi, chip, half, slot):
            store(_half(slab(wi, chip), half, axes[wi]), _half(slab(wi, chip, out_refs), half, axes[wi]), wi, slot)
        x, y, cc = lax.axis_index("x"), lax.axis_index("y"), lax.axis_index("c")
        me = 4 * x + 2 * y + cc
        k = 2 * x + y
        sibling = (x, y, 1 - cc)
        links = _Links(send_sems, recv_sems)
        chips = [_peer(x, y, cc, off + (0,)) for off in CHIPS3]
        chip_a = ((x + 1 - cc) % 2, (y + cc) % 2, cc)
        chip_b = ((x + cc) % 2, (y + 1 - cc) % 2, cc)
        chip_d = (1 - x, 1 - y, cc)
        cv = c_ref[...]
        sc = cv * _sig(cv)
        mine = a_ref.at[me]
        for r in range(8):
            mine[r:r + 1, :] = sc[:, r * 128:(r + 1) * 128]
        for j, off in enumerate(PEERS7):
            links.send(a_ref.at[me], a_ref.at[me], j, _peer(x, y, cc, off))
        for wi in range(nw):
            loads[wi].wait()
            slab(wi, k)[...] = f_refs[wi][...].astype(BF16)
            for j, to in enumerate((chip_a, chip_b)):
                links.send(_half(slab(wi, k), cc, axes[wi]), _half(slab(wi, k), cc, axes[wi]), 10 + wi * 6 + j, to)
            store(slab(wi, k), slab(wi, k, out_refs), wi, 0)
            rows = shards[wi].shape[0]
            pad = slab_rows[wi] - rows
            if pad:
                for kk in range(4):
                    g_refs[wi][kk, rows:, :] = jnp.zeros((pad, shards[wi].shape[1]), BF16)
                store(g_refs[wi].at[:, pl.ds(rows, pad), :], out_refs[wi].at[:, pl.ds(rows, pad), :], wi, 7)
        for j, off in enumerate(PEERS7):
            px, py, pc = _peer(x, y, cc, off)
            links.arrived(a_ref.at[4 * px + 2 * py + pc], j, (px, py, pc))
        ccv = cc_ref[...]
        sub = lax.broadcasted_iota(jnp.int32, (8, 1), 0)
        top = jnp.zeros((8, D), F32)
        for d in range(8):
            blk = a_ref[d]
            row = jnp.concatenate([blk[r:r + 1, :] for r in range(8)], axis=1)
            top = top + jnp.where(sub == d, jnp.broadcast_to(row, (8, D)), 0.0)
        a16 = jnp.concatenate([top, jnp.where(sub == 0, jnp.broadcast_to(ccv * _sig(ccv), (8, D)), 0.0)], axis=0)
        a16_ref[...] = a16
        loads[nw].wait()
        mod_ref[k] = _dot3(_nn, a16, wm_ref[...]) + b_ref[...]
        for j, to in enumerate(chips):
            links.send(mod_ref.at[k], mod_ref.at[k], 7 + j, to)

        def over_ici(j, frm, origin):
            for wi in range(nw):
                blk = _half(slab(wi, 2 * origin[0] + origin[1]), cc, axes[wi])
                links.arrived(blk, 10 + wi * 6 + j, frm)
                if j == 0:
                    links.send(blk, blk, 10 + wi * 6 + 2, chip_b)
                links.send(blk, blk, 10 + wi * 6 + 3 + j, sibling)
                store_half(wi, 2 * origin[0] + origin[1], cc, 1 + j)

        def from_sibling(j, origin):
            for wi in range(nw):
                links.arrived(_half(slab(wi, 2 * origin[0] + origin[1]), 1 - cc, axes[wi]), 10 + wi * 6 + 3 + j, sibling)
                store_half(wi, 2 * origin[0] + origin[1], 1 - cc, 4 + j)
        over_ici(0, chip_a, chip_a)
        over_ici(1, chip_b, chip_b)
        from_sibling(0, chip_b)
        over_ici(2, chip_b, chip_d)
        from_sibling(1, chip_a)
        from_sibling(2, chip_d)
        for j, (px, py, pc) in enumerate(chips):
            links.arrived(mod_ref.at[2 * px + py], 7 + j, (px, py, pc))
        links.drain()
        for cp in stores:
            cp.wait()

    nsem = 10 + 6 * nw
    gathered = [(4, r, s.shape[1]) for r, s in zip(slab_rows, shards)]
    return pl.pallas_call(
        body, name="gather", in_specs=[VM, VM, ANY, VM] + [ANY] * nw, out_specs=[VM, VM] + [ANY] * nw,
        out_shape=[jax.ShapeDtypeStruct((16, D), F32), jax.ShapeDtypeStruct((4, 16, kw), F32)]
        + [jax.ShapeDtypeStruct(g, BF16) for g in gathered],
        scratch_shapes=[pltpu.VMEM(g, BF16) for g in gathered] + [pltpu.VMEM(s.shape, F32) for s in shards]
        + [pltpu.VMEM(w_mod.shape, F32), pltpu.VMEM((8, 8, D // 8), F32), pltpu.SemaphoreType.DMA((nsem,)),
           pltpu.SemaphoreType.DMA((nsem,)), pltpu.SemaphoreType.DMA((nw + 1 + 8 * nw,))],
        compiler_params=pltpu.CompilerParams(vmem_limit_bytes=VMEM_LIMIT),
    )(c, c_ctx, w_mod, b_mod_k, *shards)


SMALL_ROW_WIDTHS = (D, QL, KVL, DKP, DKP, 512, 128)
SMALL_OUT_WIDTHS = (D, QL, KVL, DK, DK, 512, 1)
SMALL_PACK = 384


def _small_pieces():
    pieces = []
    for i, w in enumerate(SMALL_ROW_WIDTHS):
        for c0 in range(0, w, 128):
            j = len(pieces)
            pieces.append((i, c0, j // (SMALL_PACK // 128), j % (SMALL_PACK // 128) * 128))
    assert len(pieces) <= 8 * (SMALL_PACK // 128)
    return pieces


def _reduce(grads, axes, smalls, w_pool_g, dmod8, a16, w_mod, c_ctx):
    nw = len(grads)
    ns = len(smalls)
    kw = w_mod.shape[1]
    halves = []
    for g, ax in zip(grads, axes):
        halves.append((g.shape[1] // 2, g.shape[2]) if ax == 0 else (g.shape[1], g.shape[2] // 2))

    def body(*refs):
        g_refs = refs[:nw]
        small_refs = refs[nw:nw + ns]
        wp_ref, dm_ref, a16_ref, wm_hbm, cc_ref = refs[nw + ns:nw + ns + 5]
        o = nw + ns + 5
        r_outs = refs[o:o + nw]
        small_outs = refs[o + nw:o + nw + ns]
        rwp_ref, gw_out, gb_ref, gc_ref = refs[o + nw + ns:o + nw + ns + 4]
        o = o + nw + ns + 4
        own, sib, part, got, rel, r_refs = (refs[o + i * nw:o + (i + 1) * nw] for i in range(6))
        smbuf, wps, wpg, dm_all, pc_all, wm_ref, gw_ref, send_sems, recv_sems, local_sems = refs[o + 6 * nw:]
        wm_load = pltpu.make_async_copy(wm_hbm, wm_ref, local_sems.at[nw])
        wm_load.start()
        stores = []

        def store(src, dst, sem):
            cp = pltpu.make_async_copy(src, dst, local_sems.at[nw + 1 + sem])
            cp.start()
            stores.append(cp)
        x, y, cc = lax.axis_index("x"), lax.axis_index("y"), lax.axis_index("c")
        me = 4 * x + 2 * y + cc
        k = 2 * x + y
        sibling = (x, y, 1 - cc)
        links = _Links(send_sems, recv_sems)
        chips = [_peer(x, y, cc, off + (0,)) for off in CHIPS3]
        peers = [_peer(x, y, cc, off) for off in PEERS7]
        chip_a = ((x + 1 - cc) % 2, (y + cc) % 2, cc)
        chip_b = ((x + cc) % 2, (y + 1 - cc) % 2, cc)
        ka, kb, kd = 2 * chip_a[0] + chip_a[1], 2 * chip_b[0] + chip_b[1], 2 * (1 - x) + (1 - y)
        big, sm0, wp0, dm0, pc0 = 0, 5 * nw, 5 * nw + 7, 5 * nw + 14, 5 * nw + 21

        locals_ = []
        for wi in range(nw):
            lc = pltpu.make_async_copy(_half(g_refs[wi], cc, axes[wi]), own[wi], local_sems.at[wi])
            lc.start()
            locals_.append(lc)
            links.send(_half(g_refs[wi], 1 - cc, axes[wi]), sib[wi], big + wi * 5, sibling)
        slot = smbuf.at[me]
        slot[...] = jnp.zeros((8, SMALL_PACK), F32)
        small_rows = [jnp.broadcast_to(ref[...], (1, w)) for ref, w in zip(small_refs, SMALL_ROW_WIDTHS)]
        for i, c0, row, lane in _small_pieces():
            slot[row:row + 1, lane:lane + 128] = small_rows[i][:, c0:c0 + 128]
        links.send(wp_ref, wps, wp0, sibling)
        dm_all[me] = dm_ref[...]
        for j, peer in enumerate(peers):
            links.send(dm_all.at[me], dm_all.at[me], dm0 + j, peer)
            links.send(smbuf.at[me], smbuf.at[me], sm0 + j, peer)
        links.arrived(wps, wp0, sibling)
        wpg[k] = (wp_ref[...] + wps[...]).astype(BF16)
        for j, to in enumerate(chips):
            links.send(wpg.at[k], wpg.at[k], wp0 + 1 + j, to)
        for wi in range(nw):
            locals_[wi].wait()
            links.arrived(sib[wi], big + wi * 5, sibling)
            part[wi][...] = (own[wi][...] + sib[wi][...]).astype(BF16)
            got[wi][k] = part[wi][k]
            got[wi][kd] = jnp.zeros(halves[wi], BF16)
            links.send(part[wi].at[kd], rel[wi], big + wi * 5 + 1, chip_b)
            links.send(part[wi].at[kb], got[wi].at[k], big + wi * 5 + 2, chip_b)
        for j, (px, py, pc) in enumerate(peers):
            links.arrived(dm_all.at[4 * px + 2 * py + pc], dm0 + j, (px, py, pc))
        dm_tot = dm_all[0]
        for d in range(1, 8):
            dm_tot = dm_tot + dm_all[d]
        for kk in range(4):
            gb_ref[:, kk * kw:(kk + 1) * kw] = dm_tot[kk:kk + 1, :] + dm_tot[4 + kk:5 + kk, :]
        top = jnp.zeros((8, kw), F32)
        dmc_k = jnp.zeros((1, kw), F32)
        for kk in range(4):
            top = top + jnp.where(k == kk, _select_rows(dm_all, 8, kk), 0.0)
            dmc_k = dmc_k + jnp.where(k == kk, dm_tot[4 + kk:5 + kk, :], 0.0)
        sub = lax.broadcasted_iota(jnp.int32, (8, 1), 0)
        bk = jnp.concatenate([top, jnp.where(sub == 0, jnp.broadcast_to(dmc_k, (8, kw)), 0.0)], axis=0)
        gw_ref[...] = _dot3(_tn, a16_ref[...], bk)
        store(gw_ref, gw_out, 0)
        wm_load.wait()
        pc_all[k] =_dot3(_nt, jnp.broadcast_to(bk[8:9, :], (8, kw)), wm_ref[...])
        for j, to in enumerate(chips):
            links.send(pc_all.at[k], pc_all.at[k], pc0 + j, to)
        for j, (px, py, pc) in enumerate(peers):
            links.arrived(smbuf.at[4 * px + 2 * py + pc], sm0 + j, (px, py, pc))
        tot = smbuf[0]
        for d in range(1, 8):
            tot = tot + smbuf[d]
        for i, c0, row, lane in _small_pieces():
            n = min(128, SMALL_OUT_WIDTHS[i] - c0)
            if n > 0:
                small_outs[i][:, c0:c0 + n] = tot[row:row + 1, lane:lane + n]
        for j, (px, py, pc) in enumerate(chips):
            links.arrived(wpg.at[2 * px + py], wp0 + 1 + j, (px, py, pc))
        wpt = wpg[0].astype(F32)
        for kk in range(1, 4):
            wpt = wpt + wpg[kk].astype(F32)
        rwp_ref[...] = wpt
        for wi in range(nw):
            links.arrived(rel[wi], big + wi * 5 + 1, chip_b)
            rel[wi][...] = (part[wi][ka].astype(F32) + rel[wi][...].astype(F32)).astype(BF16)
            links.send(rel[wi], got[wi].at[k], big + wi * 5 + 3, chip_a)
        for wi in range(nw):
            links.arrived(got[wi].at[kb], big + wi * 5 + 2, chip_b)
            links.arrived(got[wi].at[ka], big + wi * 5 + 3, chip_a)
            total = got[wi][0].astype(F32)
            for kk in range(1, 4):
                total = total + got[wi][kk].astype(F32)
            mine = _half(r_refs[wi], cc, axes[wi])
            mine[...] = total
            links.send(mine, mine, big + wi * 5 + 4, sibling)
            store(mine, _half(r_outs[wi], cc, axes[wi]), 1 + wi)
        for j, (px, py, pc) in enumerate(chips):
            links.arrived(pc_all.at[2 * px + py], pc0 + j, (px, py, pc))
        ccv = cc_ref[...]
        sg = _sig(ccv)
        gc_ref[...] = (pc_all[0][0:1, :] + pc_all[1][0:1, :] + pc_all[2][0:1, :] + pc_all[3][0:1, :]) * (sg * (1.0 + ccv * (1.0 - sg)))
        for wi in range(nw):
            links.arrived(_half(r_refs[wi], 1 - cc, axes[wi]), big + wi * 5 + 4, sibling)
            store(_half(r_refs[wi], 1 - cc, axes[wi]), _half(r_outs[wi], 1 - cc, axes[wi]), 1 + nw + wi)
        links.drain()
        for cp in stores:
            cp.wait()

    nsem = 5 * nw + 24
    quads = [(4,) + h for h in halves]
    return pl.pallas_call(
        body, name="reduce", in_specs=[ANY] * nw + [VM] * (ns + 3) + [ANY, VM],
        out_specs=[ANY] * nw + [VM] * (ns + 1) + [ANY, VM, VM],
        out_shape=[jax.ShapeDtypeStruct(g.shape[1:], F32) for g in grads]
        + [jax.ShapeDtypeStruct((1, w), F32) for w in SMALL_OUT_WIDTHS]
        + [jax.ShapeDtypeStruct(w_pool_g.shape, F32), jax.ShapeDtypeStruct((D, kw), F32), jax.ShapeDtypeStruct((1, 3 * D), F32),
           jax.ShapeDtypeStruct((1, D), F32)],
        scratch_shapes=[pltpu.VMEM(q, F32) for q in quads] + [pltpu.VMEM(q, F32) for q in quads]
        + [pltpu.VMEM(q, BF16) for q in quads] + [pltpu.VMEM(q, BF16) for q in quads] + [pltpu.VMEM(h, BF16) for h in halves]
        + [pltpu.VMEM(g.shape[1:], F32) for g in grads]
        + [pltpu.VMEM((8, 8, SMALL_PACK), F32), pltpu.VMEM(w_pool_g.shape, F32), pltpu.VMEM((4,) + w_pool_g.shape, BF16),
           pltpu.VMEM((8, 8, kw), F32), pltpu.VMEM((4, 8, D), F32), pltpu.VMEM((D, kw), F32), pltpu.VMEM((D, kw), F32)]
        + [pltpu.SemaphoreType.DMA((nsem,)), pltpu.SemaphoreType.DMA((nsem,)), pltpu.SemaphoreType.DMA((3 * nw + 2,))],
        compiler_params=pltpu.CompilerParams(vmem_limit_bytes=VMEM_LIMIT),
    )(*grads, *smalls, w_pool_g, dmod8, a16, w_mod, c_ctx)


def _rope_tables(s_len):
    rows = s_len // GRID_W
    per = TB // GRID_W
    n_freq = 16
    inv = ROPE_BASE ** (-jnp.arange(n_freq, dtype=F32) / n_freq)
    ang_r = jnp.arange(rows, dtype=F32)[:, None] * inv
    ang_c = jnp.arange(GRID_W, dtype=F32)[:, None] * inv
    by_row, by_col = [], []
    for fn, pad in ((jnp.cos, 1.0), (jnp.sin, 0.0)):
        r = jnp.concatenate([fn(ang_r), fn(ang_r), jnp.zeros((rows, 96), F32)], axis=1).reshape(rows // per, per, 128)
        by_row.append(jnp.pad(r, ((0, 0), (0, 8 - per), (0, 0))))
        cpart = jnp.concatenate([jnp.zeros((GRID_W, 32), F32), fn(ang_c), fn(ang_c), jnp.full((GRID_W, 64), pad, F32)], axis=1)
        by_col.append(jnp.tile(cpart, (per, 1)))
    return jnp.concatenate(by_row, axis=-1), jnp.concatenate(by_col, axis=-1)


def kernel(x, c, ctx, c_ctx, w_mod, b_mod, norm_g, w_in, q_lora_g, w_uq, kv_lora_g, w_ukv, q_norm_g, k_norm_g, w_pool, pool_scale, w_out, loss_target, m_c_ctx, m_w_mod, m_b_mod, m_norm_g, m_w_in, m_q_lora_g, m_w_uq, m_kv_lora_g, m_w_ukv, m_q_norm_g, m_k_norm_g, m_w_pool, m_pool_scale, m_w_out, v_c_ctx, v_w_mod, v_b_mod, v_norm_g, v_w_in, v_q_lora_g, v_w_uq, v_kv_lora_g, v_w_ukv, v_q_norm_g, v_k_norm_g, v_w_pool, v_pool_scale, v_w_out):
    xi, yi, ci = lax.axis_index("x"), lax.axis_index("y"), lax.axis_index("c")
    me = 4 * xi + 2 * yi + ci
    k = 2 * xi + yi
    s_len = x.shape[1]
    lc = ctx.shape[1]
    kw = w_mod.shape[2]
    weights = dict(c_ctx=c_ctx, w_mod=w_mod, b_mod=b_mod, norm_g=norm_g, w_in=w_in, q_lora_g=q_lora_g, w_uq=w_uq,
                   kv_lora_g=kv_lora_g, w_ukv=w_ukv, q_norm_g=q_norm_g, k_norm_g=k_norm_g, w_pool=w_pool,
                   pool_scale=pool_scale, w_out=w_out)
    m_in = dict(c_ctx=m_c_ctx, w_mod=m_w_mod, b_mod=m_b_mod, norm_g=m_norm_g, w_in=m_w_in, q_lora_g=m_q_lora_g, w_uq=m_w_uq,
                kv_lora_g=m_kv_lora_g, w_ukv=m_w_ukv, q_norm_g=m_q_norm_g, k_norm_g=m_k_norm_g, w_pool=m_w_pool,
                pool_scale=m_pool_scale, w_out=m_w_out)
    v_in = dict(c_ctx=v_c_ctx, w_mod=v_w_mod, b_mod=v_b_mod, norm_g=v_norm_g, w_in=v_w_in, q_lora_g=v_q_lora_g, w_uq=v_w_uq,
                kv_lora_g=v_kv_lora_g, w_ukv=v_w_ukv, q_norm_g=v_q_norm_g, k_norm_g=v_k_norm_g, w_pool=v_w_pool,
                pool_scale=v_pool_scale, w_out=v_w_out)
    order = ["c_ctx", "w_mod", "b_mod", "norm_g", "w_in", "q_lora_g", "w_uq", "kv_lora_g", "w_ukv", "q_norm_g", "k_norm_g",
             "w_pool", "pool_scale", "w_out"]
    transposed = ("w_in", "w_uq")
    as2d = lambda n, a: jnp.transpose(a[0]) if n in transposed else a.reshape(-1, a.shape[-1])
    back = lambda n, a: jnp.transpose(a)[None] if n in transposed else a.reshape(weights[n].shape)

    c_ctx2 = c_ctx.reshape(1, D)
    b_mod_k = lax.dynamic_slice(b_mod, (0, k * kw), (1, kw))
    split = (1, 0, 0, 0)
    a16, mod_all, g_in, g_uq, g_ukv, g_out = _gather(
        c, c_ctx2, w_mod[0], b_mod_k, [as2d("w_in", w_in), as2d("w_uq", w_uq), w_ukv[0], w_out[0]], split,
        (DIN // 4, DKP, KVL, D // 4))
    mod_me = lax.dynamic_index_in_dim(mod_all, me, axis=1, keepdims=False).reshape(3, D)
    mod_c = mod_all[:, 8, :].reshape(3, D)
    modsel = jnp.stack([mod_c, mod_me])
    w_in_t = g_in.reshape(DIN, D)
    w_uq_t = g_uq
    w_out_f = g_out.reshape(D, D)
    qn_g = jnp.pad(q_norm_g, ((0, 0), (0, DKP - DK)))
    kn_g = jnp.pad(k_norm_g, ((0, 0), (0, DKP - DK)))
    w_pool_b = w_pool[0].astype(BF16)
    cos, sin = _rope_tables(s_len)

    u, q, kk, v = _fwd_in(ctx[0], x[0], modsel, norm_g, w_in_t, q_lora_g, w_uq_t, kv_lora_g, g_ukv, qn_g, kn_g, cos, sin)
    attn, lse = _attn_fwd(q, kk, v, s_len)
    (dxn, dattn, dga, dgp, dpool, dw_out, dgate, dps, dw_pool, loss) = _out_stage(
        attn.reshape(s_len // Q_BLOCK, Q_BLOCK, NH * DV), u, x[0], loss_target[0], modsel[1, 2:3, :], w_pool_b, pool_scale,
        w_out_f, lc)
    dattn = dattn.reshape(s_len, NH * DV)
    dq, dk, dv = _attn_bwd(q, kk, v, dattn, attn, lse, s_len)
    dlo, dw_uq_t, dw_ukv, dqlg, dkvlg, dqng, dkng = _qkv_bwd(u, dq, dk, dv, cos, sin, q_lora_g, w_uq_t, kv_lora_g, g_ukv,
                                                            qn_g, kn_g, s_len)
    gx, dw_in_t, dmod, dng = _in_bwd(ctx[0], x[0], modsel, norm_g, dlo, dga, dgp, dpool, dxn, w_in_t)

    dmod_l = jnp.concatenate([dmod[1, 0], dmod[1, 1], dgate[0]]).reshape(4, kw)
    dmod_c = jnp.concatenate([dmod[0, 0], dmod[0, 1], jnp.zeros((D,), F32)]).reshape(4, kw)
    dmod8 = jnp.concatenate([dmod_l, dmod_c], axis=0)
    r_in, r_uq, r_ukv, r_out, g_ng, g_qlg, g_kvlg, g_qng, g_kng, g_ps, loss_all, g_wp, g_w_mod, g_b_mod, g_c_ctx = _reduce(
        [dw_in_t.reshape(4, DIN // 4, D), dw_uq_t, dw_ukv, dw_out.reshape(4, D // 4, D)], split,
        [dng, dqlg, dkvlg, dqng, dkng, dps, loss], dw_pool, dmod8, a16, w_mod[0], c_ctx2)
    g2d = dict(c_ctx=g_c_ctx, b_mod=g_b_mod, w_mod=g_w_mod, w_in=r_in, w_uq=r_uq, w_ukv=r_ukv, w_out=r_out, norm_g=g_ng,
               q_lora_g=g_qlg, kv_lora_g=g_kvlg, q_norm_g=g_qng, k_norm_g=g_kng, pool_scale=g_ps, w_pool=g_wp.reshape(512, 128))

    outs = _adamw_many([as2d(n, weights[n]) for n in order], [g2d[n] for n in order], [as2d(n, m_in[n]) for n in order],
                       [as2d(n, v_in[n]) for n in order])
    d2d, m2d, v2d, g2d = (dict(zip(order, arrs)) for arrs in outs)

    return (loss_all[0, 0], gx[None], *[back(n, g2d[n]) for n in order], *[back(n, d2d[n]) for n in order],
            *[back(n, m2d[n]) for n in order], *[back(n, v2d[n]) for n in order])
```

```python
import jax
import jax.numpy as jnp
from jax import lax
from jax.experimental import pallas as pl
from jax.experimental.pallas import tpu as pltpu

F32 = jnp.float32
BF16 = jnp.bfloat16
MESH = pl.DeviceIdType.MESH

D = 1024
NH = 4
DK = 192
DKP = 256
DV = 128
QL = 256
KVL = 128
DIN = 1984
U_LO = 448
SEG = ((0, 512), (448, 960), (960, 1472), (1472, 1984))
DU = 2048
POOL_WINDOWS = (2, 4, 8, 16)
HALO = 8
EPS = 1e-6
ROPE_BASE = 10000.0
GRID_W = 64
Q_BLOCK = 128
TB = 256
BWD_QBLOCKS = 1
SCALE = DK ** -0.5
LOG2E = 1.4426950408889634
LN2 = 0.6931471805599453
VMEM_LIMIT = 56 * 1024 * 1024

ADAM_LR = 0.001
ADAM_B1 = 0.9
ADAM_B2 = 0.999
ADAM_EPS = 1e-08
ADAM_WD = 0.01
ADAM_STEP = 10

CHIPS3 = ((1, 0), (0, 1), (1, 1))
PEERS7 = tuple((dx, dy, dc) for dx in (0, 1) for dy in (0, 1) for dc in (0, 1) if (dx, dy, dc) != (0, 0, 0))

VM = pl.BlockSpec(memory_space=pltpu.VMEM)
ANY = pl.BlockSpec(memory_space=pl.ANY)


def _nn(a, b):
    return jnp.dot(a, b, preferred_element_type=F32)


def _nt(a, b):
    return lax.dot_general(a, b, (((1,), (1,)), ((), ())), preferred_element_type=F32)


def _tn(a, b):
    return lax.dot_general(a, b, (((0,), (0,)), ((), ())), preferred_element_type=F32)


def _split3(a):
    a0 = a.astype(BF16)
    r = a - a0.astype(F32)
    a1 = r.astype(BF16)
    a2 = (r - a1.astype(F32)).astype(BF16)
    return a0, a1, a2


def _dot3(dot, a, b):
    sa = _split3(a)
    sb = _split3(b)
    out = None
    for i in range(3):
        for j in range(3 - i):
            t = dot(sa[i], sb[j])
            out = t if out is None else out + t
    return out


def _sig(x):
    return 1.0 / (1.0 + jnp.exp(-x))


def _rot(t):
    src = lax.broadcasted_iota(jnp.int32, (128, 128), 0)
    dst = lax.broadcasted_iota(jnp.int32, (128, 128), 1)
    first = (dst % 32) < 16
    perm = jnp.where(first & (src == dst + 16), -1.0, jnp.where(~first & (src == dst - 16), 1.0, 0.0)).astype(BF16)
    hi = t.astype(BF16)
    lo = (t - hi.astype(F32)).astype(BF16)
    return _nn(hi, perm) + _nn(lo, perm)


def _rope(t, cos, sin):
    return t * cos + _rot(t) * sin


def _rope_t(t, cos, sin):
    return t * cos - _rot(t * sin)


def _rope_block(rows_ref, cols_ref, is_ctx):
    lane = lax.broadcasted_iota(jnp.int32, (TB, 256), 1) % 128
    rows = jnp.concatenate([jnp.broadcast_to(rows_ref[0, r:r + 1, :], (GRID_W, 256)) for r in range(TB // GRID_W)], axis=0)
    cs = jnp.where(lane < 32, rows, cols_ref[...])
    return jnp.where(is_ctx, 1.0, cs[:, :128]), jnp.where(is_ctx, 0.0, cs[:, 128:])


def _shift_rows(z, k):
    n = z.shape[0]
    return pltpu.roll(z, (n - k) % n, 0)


def _colsum(a):
    return jnp.sum(a, axis=0, keepdims=True)


def _rowsum(a):
    return jnp.sum(a, axis=-1, keepdims=True)


def _row_layout(col):
    return jnp.transpose(jnp.broadcast_to(col, (col.shape[0], 128)))[0:8, :]


def _params(sem=None):
    return pltpu.CompilerParams(dimension_semantics=sem, vmem_limit_bytes=VMEM_LIMIT)


def _full(shape):
    nd = len(shape)
    return pl.BlockSpec(shape, lambda *_: (0,) * nd)


def _peer(x, y, c, off):
    dx, dy, dc = off
    return ((x + dx) % 2, (y + dy) % 2, (c + dc) % 2)


def _token_specs(off):
    ctx = pl.BlockSpec((TB, D), lambda i: (jnp.minimum(i, off - 1), 0))
    lat = pl.BlockSpec((TB, D), lambda i: (jnp.maximum(i - off, 0), 0))
    mod = pl.BlockSpec((1, 3, D), lambda i: (jnp.minimum(i // off, 1), 0, 0))
    return ctx, lat, mod


def _modulated(x, mod_ref, ng):
    shift = mod_ref[0, 0:1, :]
    scale = mod_ref[0, 1:2, :]
    r = lax.rsqrt(jnp.mean(x * x, axis=-1, keepdims=True) + EPS)
    xh = x * r
    xg = xh * ng
    return r, xh, xg, xg * (1.0 + scale) + shift, scale


def _fwd_in(ctx, x, modsel, norm_g, w_in_t, q_lora_g, w_uq_t, kv_lora_g, w_ukv, qn_g, kn_g, cos, sin):
    s_len, lc = x.shape[0], ctx.shape[0]
    t_all = s_len + lc
    nb = t_all // TB
    off = lc // TB

    def body(ctx_ref, x_ref, mod_ref, ng_ref, win_ref, qlg_ref, wuq_ref, kvlg_ref, wukv_ref, qng_ref, kng_ref, cos_ref, sin_ref,
             u_ref, q_ref, k_ref, v_ref):
        is_ctx = pl.program_id(0) < off
        xb = jnp.where(is_ctx, ctx_ref[...], x_ref[...])
        _, _, _, h, _ = _modulated(xb, mod_ref, ng_ref[...])
        hb = h.astype(BF16)
        lane = lax.broadcasted_iota(jnp.int32, (TB, 512), 1)
        ulo = jnp.where(lane < U_LO, _nt(hb, win_ref[SEG[0][0]:SEG[0][1], :]), 0.0)
        u_ref[:, 0:512] = ulo
        for j in range(1, 4):
            u_ref[:, j * 512:(j + 1) * 512] = _nt(hb, win_ref[SEG[j][0]:SEG[j][1], :])
        cos, sin = _rope_block(cos_ref, sin_ref, is_ctx)
        cq = ulo[:, 0:QL]
        cqn = (cq * lax.rsqrt(jnp.mean(cq * cq, axis=-1, keepdims=True) + EPS) * qlg_ref[...]).astype(BF16)
        qng = qng_ref[...]
        ckv = ulo[:, QL:QL + KVL]
        ckvn = (ckv * lax.rsqrt(jnp.mean(ckv * ckv, axis=-1, keepdims=True) + EPS) * kvlg_ref[...]).astype(BF16)
        qhs = [_nt(cqn, wuq_ref[hd]) for hd in range(NH)]
        kvs = [_nn(ckvn, wukv_ref[hd]) for hd in range(NH)]
        for hd in range(NH):
            qh = qhs[hd]
            qn = qh * lax.rsqrt(_rowsum(qh * qh) / DK + EPS) * qng
            q_ref[hd] = (jnp.concatenate([qn[:, :128], _rope(qn[:, 128:], cos, sin)], axis=1) * (SCALE * LOG2E)).astype(BF16)
        kr = ulo[:, 384:512]
        skr = _rowsum(kr * kr)
        kng = kng_ref[...]
        kr_roped = _rope(kr * kng[:, 128:], cos, sin)
        for hd in range(NH):
            kv = kvs[hd]
            kn = kv[:, :128]
            rk = lax.rsqrt((_rowsum(kn * kn) + skr) / DK + EPS)
            k_ref[hd] = jnp.concatenate([kn * rk * kng[:, :128], kr_roped * rk], axis=1).astype(BF16)
            v_ref[hd] = kv[:, 128:].astype(BF16)

    row = lambda w: pl.BlockSpec((TB, w), lambda i: (i, 0))
    heads = lambda w: pl.BlockSpec((NH, TB, w), lambda i: (0, i, 0))
    cspec, xspec, mspec = _token_specs(off)
    return pl.pallas_call(
        body, name="fwd_in", grid=(nb,),
        in_specs=[cspec, xspec, mspec, _full((1, D)), _full((DIN, D)), _full((1, QL)), _full((NH, DKP, QL)), _full((1, KVL)),
                  _full((NH, KVL, 256)), _full((1, DKP)), _full((1, DKP)),
                  pl.BlockSpec((1, 8, 256), lambda i: (jnp.maximum(i - off, 0), 0, 0)), _full((TB, 256))],
        out_specs=[row(DU), heads(DKP), heads(DKP), heads(DV)],
        out_shape=[jax.ShapeDtypeStruct((t_all, DU), F32), jax.ShapeDtypeStruct((NH, t_all, DKP), BF16),
                   jax.ShapeDtypeStruct((NH, t_all, DKP), BF16), jax.ShapeDtypeStruct((NH, t_all, DV), BF16)],
        compiler_params=_params(("arbitrary",)),
    )(ctx, x, modsel, norm_g, w_in_t, q_lora_g, w_uq_t, kv_lora_g, w_ukv, qn_g, kn_g, cos, sin)


def _attn_fwd(q, k, v, s_len):
    t_all = q.shape[1]
    off = (t_all - s_len) // TB
    nq = s_len // TB
    nsub = next(n for n in (4, 2, 1) if nq % n == 0)

    def body(*refs):
        q_refs = refs[:nsub]
        k_ref, v_ref, o_ref, lse_ref = refs[nsub:]
        for sb in range(nsub):
            s = _nt(q_refs[sb][0], k_ref[0])
            m = jnp.max(s, axis=-1, keepdims=True)
            e = jnp.exp2(s - m)
            l = _rowsum(e)
            o_ref[sb * TB:(sb + 1) * TB, :] = _nn(e.astype(BF16), v_ref[0]) / l
            lse_ref[0, sb] = _row_layout(m + jnp.log2(l))

    qspec = lambda sb: pl.BlockSpec((1, TB, DKP), lambda h, i: (h, i * nsub + sb + off, 0))
    return pl.pallas_call(
        body, name="attn_fwd", grid=(NH, nq // nsub),
        in_specs=[qspec(sb) for sb in range(nsub)]
        + [pl.BlockSpec((1, t_all, DKP), lambda h, i: (h, 0, 0)), pl.BlockSpec((1, t_all, DV), lambda h, i: (h, 0, 0))],
        out_specs=[pl.BlockSpec((nsub * TB, DV), lambda h, i: (i, h)), pl.BlockSpec((1, nsub, 8, TB), lambda h, i: (h, i, 0, 0))],
        out_shape=[jax.ShapeDtypeStruct((s_len, NH * DV), F32), jax.ShapeDtypeStruct((NH, nq, 8, TB), F32)],
        compiler_params=_params(("arbitrary", "arbitrary")),
    )(*([q] * nsub), k, v)


def _out_stage(attn, u, x, target, gate, w_pool, pool_scale, w_out, lc):
    s_len = x.shape[0]
    t_all = s_len + lc
    off = lc // TB
    nq = s_len // TB
    hb = TB // HALO
    nqb = s_len // Q_BLOCK
    jb = TB // nqb

    def body(attn_ref, ga_ref, pin_ref, pprev_ref, pnext_ref, gp_ref, x_ref, tgt_ref, gate_ref, wp_ref, ps_ref, wo_ref,
             dxn_ref, dattn_ref, dga_ref, dgp_ref, dpool_ref, dwo_ref, dgate_ref, dps_ref, dwp_ref, loss_ref):
        i = pl.program_id(0)

        @pl.when(i == 0)
        def _():
            dwo_ref[...] = jnp.zeros_like(dwo_ref)
            dgate_ref[...] = jnp.zeros_like(dgate_ref)
            dps_ref[...] = jnp.zeros_like(dps_ref)
            dwp_ref[...] = jnp.zeros_like(dwp_ref)
            loss_ref[...] = jnp.zeros_like(loss_ref)

        attn = jnp.concatenate([attn_ref[:, jj, :] for jj in range(jb)], axis=0)
        ga = ga_ref[...]
        gp = gp_ref[...]
        pin = pin_ref[...]
        prev = jnp.where(i == 0, 0.0, pprev_ref[...])
        nxt = jnp.where(i == nq - 1, 0.0, pnext_ref[...])
        win = jnp.concatenate([prev, pin, nxt], axis=0)
        tg = i * TB + lax.broadcasted_iota(jnp.int32, (TB, 1), 0)
        pooled = []
        for g, w in enumerate(POOL_WINDOWS):
            a = win[:, g * 128:(g + 1) * 128]
            p = _shift_rows(a, -1) + a
            for step in (1, 2, 4):
                if w >= 4 * step:
                    p = _shift_rows(p, -step) + _shift_rows(p, step)
            cnt = (jnp.minimum(tg + w // 2, s_len) - jnp.maximum(tg - w // 2, 0)).astype(F32)
            pooled.append(p[HALO:HALO + TB] / cnt - a[HALO:HALO + TB])
        pooled_b = [p.astype(BF16) for p in pooled]
        z = jnp.concatenate([_nn(pooled_b[g], wp_ref[g]) for g in range(4)], axis=1)
        ps = ps_ref[...]
        yp = z * ps
        sga = _sig(ga)
        sila = ga * sga
        sgp = _sig(gp)
        silp = gp * sgp
        br = jnp.concatenate([sila * attn, silp * yp], axis=1).astype(BF16)
        y = _nn(br, wo_ref[...])
        gate = gate_ref[...]
        err = x_ref[...] + gate * y - tgt_ref[...]
        loss_ref[...] += _colsum(_rowsum(err * err)) * (0.5 / D)
        dxn = err * (1.0 / D)
        dxn_ref[...] = dxn
        dgate_ref[...] += _colsum(dxn * y)
        dy = (dxn * gate).astype(BF16)
        dwo_ref[...] += _tn(br, dy)
        dbr = _nt(dy, wo_ref[...])
        dbra = dbr[:, :512]
        dbrp = dbr[:, 512:]
        dattn = dbra * sila
        for jj in range(jb):
            dattn_ref[:, jj, :] = dattn[jj * nqb:(jj + 1) * nqb]
        dga_ref[...] = (dbra * attn * (sga * (1.0 + ga * (1.0 - sga)))).astype(BF16)
        dgp_ref[...] = (dbrp * yp * (sgp * (1.0 + gp * (1.0 - sgp)))).astype(BF16)
        dyp = dbrp * silp
        dps_ref[...] += _colsum(dyp * z)
        dz = (dyp * ps).astype(BF16)
        dpool = []
        for g in range(4):
            dzg = dz[:, g * 128:(g + 1) * 128]
            dwp_ref[g] += _tn(pooled_b[g], dzg)
            dpool.append(_nt(dzg, wp_ref[g]))
        dpool_ref[...] = jnp.concatenate(dpool, axis=1)

    lat = lambda w: pl.BlockSpec((TB, w), lambda i: (i, 0))
    perm = pl.BlockSpec((nqb, jb, 512), lambda i: (0, i, 0))
    ucol = lambda j: pl.BlockSpec((TB, 512), lambda i: (i + off, j))
    last8 = t_all // HALO - 1
    return pl.pallas_call(
        body, name="out_stage", grid=(nq,),
        in_specs=[perm, ucol(1), ucol(2),
                  pl.BlockSpec((HALO, 512), lambda i: ((i + off) * hb - 1, 2)),
                  pl.BlockSpec((HALO, 512), lambda i: (jnp.minimum((i + off + 1) * hb, last8), 2)),
                  ucol(3), lat(D), lat(D), _full((1, D)), _full((4, 128, 128)), _full((1, 512)), _full((D, D))],
        out_specs=[lat(D), perm, lat(512), lat(512), lat(512),
                   _full((D, D)), _full((1, D)), _full((1, 512)), _full((4, 128, 128)), _full((1, 1))],
        out_shape=[jax.ShapeDtypeStruct((s_len, D), F32), jax.ShapeDtypeStruct((nqb, Q_BLOCK, 512), F32),
                   jax.ShapeDtypeStruct((s_len, 512), BF16), jax.ShapeDtypeStruct((s_len, 512), BF16),
                   jax.ShapeDtypeStruct((s_len, 512), F32),
                   jax.ShapeDtypeStruct((D, D), F32), jax.ShapeDtypeStruct((1, D), F32), jax.ShapeDtypeStruct((1, 512), F32),
                   jax.ShapeDtypeStruct((4, 128, 128), F32), jax.ShapeDtypeStruct((1, 1), F32)],
        compiler_params=_params(("arbitrary",)),
    )(attn, u, u, u, u, u, x, target, gate, w_pool, pool_scale, w_out)


def _attn_bwd(q, k, v, dattn, attn, lse, s_len):
    t_all = q.shape[1]
    off = (t_all - s_len) // TB
    nq = s_len // TB
    nch = 4
    chunks = [(c * (t_all // nch), t_all // nch) for c in range(nch)]
    nsub = next(n for n in (BWD_QBLOCKS, 2, 1) if nq % n == 0)
    tq = nsub * TB

    def body(*refs):
        q_refs = refs[:nsub]
        k_ref, v_ref, do_ref, o_ref, lse_ref, dq_ref, dk_ref, dv_ref = refs[nsub:]
        i = pl.program_id(1)

        @pl.when(i == 0)
        def _():
            dk_ref[...] = jnp.zeros_like(dk_ref)
            dv_ref[...] = jnp.zeros_like(dv_ref)

        qb = jnp.concatenate([r[0] for r in q_refs], axis=0)
        delta_r = _row_layout(_rowsum(do_ref[...] * o_ref[...]))[0:1, :]
        do = do_ref[...].astype(BF16)
        lse_r = jnp.concatenate([lse_ref[0, sb][0:1, :] for sb in range(nsub)], axis=1)
        dq = jnp.zeros((tq, DKP), F32)
        for start, size in chunks:
            rows = pl.ds(start, size)
            kc = k_ref[0, rows, :]
            p_t = jnp.exp2(_nt(kc, qb) - lse_r)
            ds_t = (p_t * (_nt(v_ref[0, rows, :], do) - delta_r)).astype(BF16)
            dv_ref[0, rows, :] += _nn(p_t.astype(BF16), do)
            dk_ref[0, rows, :] += _nn(ds_t, qb)
            dq += _tn(ds_t, kc)
        dq_ref[0] = dq * SCALE

    kvspec = lambda w: pl.BlockSpec((1, t_all, w), lambda h, i: (h, 0, 0))
    rowspec = pl.BlockSpec((1, nsub, 8, TB), lambda h, i: (h, i, 0, 0))
    qspec = lambda sb: pl.BlockSpec((1, TB, DKP), lambda h, i: (h, i * nsub + sb + off, 0))
    return pl.pallas_call(
        body, name="attn_bwd", grid=(NH, nq // nsub),
        in_specs=[qspec(sb) for sb in range(nsub)]
        + [kvspec(DKP), kvspec(DV), pl.BlockSpec((tq, DV), lambda h, i: (i, h)), pl.BlockSpec((tq, DV), lambda h, i: (i, h)),
           rowspec],
        out_specs=[pl.BlockSpec((1, tq, DKP), lambda h, i: (h, i, 0)), kvspec(DKP), kvspec(DV)],
        out_shape=[jax.ShapeDtypeStruct((NH, s_len, DKP), F32), jax.ShapeDtypeStruct((NH, t_all, DKP), F32),
                   jax.ShapeDtypeStruct((NH, t_all, DV), F32)],
        compiler_params=_params(("arbitrary", "arbitrary")),
    )(*([q] * nsub), k, v, dattn, attn, lse)


def _qkv_bwd(u, dq, dk, dv, cos, sin, q_lora_g, w_uq_t, kv_lora_g, w_ukv, qn_g, kn_g, s_len):
    t_all = u.shape[0]
    off = (t_all - s_len) // TB
    nb = t_all // TB

    def body(ulo_ref, dq_ref, dk_ref, dv_ref, cos_ref, sin_ref, qlg_ref, wuq_ref, kvlg_ref, wukv_ref, qng_ref, kng_ref,
             dlo_ref, dwuq_ref, dwukv_ref, dqlg_ref, dkvlg_ref, dqng_ref, dkng_ref):
        i = pl.program_id(0)

        @pl.when(i == 0)
        def _():
            for r in (dwuq_ref, dwukv_ref, dqlg_ref, dkvlg_ref, dqng_ref, dkng_ref):
                r[...] = jnp.zeros_like(r)

        latent = i >= off
        ulo = ulo_ref[...]
        cos, sin = _rope_block(cos_ref, sin_ref, pl.program_id(0) < off)
        cq = ulo[:, 0:QL]
        rc = lax.rsqrt(jnp.mean(cq * cq, axis=-1, keepdims=True) + EPS)
        cqh = cq * rc
        qlg = qlg_ref[...]
        cqn_b = (cqh * qlg).astype(BF16)
        qng = qng_ref[...]
        ckv = ulo[:, QL:QL + KVL]
        r0 = lax.rsqrt(jnp.mean(ckv * ckv, axis=-1, keepdims=True) + EPS)
        ckvh = ckv * r0
        kvlg = kvlg_ref[...]
        ckvn_b = (ckvh * kvlg).astype(BF16)
        qhs = [_nt(cqn_b, wuq_ref[hd]) for hd in range(NH)]
        kns = [_nn(ckvn_b, wukv_ref[hd])[:, :128] for hd in range(NH)]
        dqng = jnp.zeros((1, DKP), F32)
        dqraws = []
        for hd in range(NH):
            qh = qhs[hd]
            rq = lax.rsqrt(_rowsum(qh * qh) / DK + EPS)
            xh = qh * rq
            dqh = jnp.where(latent, dq_ref[hd], 0.0)
            dyq = jnp.concatenate([dqh[:, :128], _rope_t(dqh[:, 128:], cos, sin)], axis=1)
            dqng += _colsum(dyq * xh)
            dxh = dyq * qng
            dqraws.append((rq * (dxh - xh * (_rowsum(dxh * xh) / DK))).astype(BF16))
        dqng_ref[...] += dqng

        kr = ulo[:, 384:512]
        skr = _rowsum(kr * kr)
        kng = kng_ref[...]
        dkr = jnp.zeros((TB, 128), F32)
        dkng = jnp.zeros((1, DKP), F32)
        dkvs = []
        for hd in range(NH):
            kn = kns[hd]
            rk = lax.rsqrt((_rowsum(kn * kn) + skr) / DK + EPS)
            xh1 = kn * rk
            xh2 = kr * rk
            dkh = dk_ref[hd] * LN2
            d1 = dkh[:, :128]
            d2 = _rope_t(dkh[:, 128:], cos, sin)
            dkng += jnp.concatenate([_colsum(d1 * xh1), _colsum(d2 * xh2)], axis=1)
            dx1 = d1 * kng[:, :128]
            dx2 = d2 * kng[:, 128:]
            dot = (_rowsum(dx1 * xh1) + _rowsum(dx2 * xh2)) / DK
            dkvs.append(jnp.concatenate([rk * (dx1 - xh1 * dot), dv_ref[hd]], axis=1).astype(BF16))
            dkr += rk * (dx2 - xh2 * dot)
        dkng_ref[...] += dkng

        dcqn = jnp.zeros((TB, QL), F32)
        dckvn = jnp.zeros((TB, KVL), F32)
        for hd in range(NH):
            dwuq_ref[hd] += _tn(dqraws[hd], cqn_b)[:DK]
            dcqn += _nn(dqraws[hd], wuq_ref[hd])
            dwukv_ref[hd] += _tn(ckvn_b, dkvs[hd])
            dckvn += _nt(dkvs[hd], wukv_ref[hd])
        dqlg_ref[...] += _colsum(dcqn * cqh)
        dxh = dcqn * qlg
        dcq = rc * (dxh - cqh * jnp.mean(dxh * cqh, axis=-1, keepdims=True))
        dkvlg_ref[...] += _colsum(dckvn * ckvh)
        dxh = dckvn * kvlg
        dckv = r0 * (dxh - ckvh * jnp.mean(dxh * ckvh, axis=-1, keepdims=True))
        dlo_ref[...] = jnp.concatenate([dcq, dckv, dkr], axis=1).astype(BF16)

    row = lambda w: pl.BlockSpec((TB, w), lambda i: (i, 0))
    heads = lambda w: pl.BlockSpec((NH, TB, w), lambda i: (0, i, 0))
    return pl.pallas_call(
        body, name="qkv_bwd", grid=(nb,),
        in_specs=[row(512), pl.BlockSpec((NH, TB, DKP), lambda i: (0, jnp.maximum(i - off, 0), 0)), heads(DKP), heads(DV),
                  pl.BlockSpec((1, 8, 256), lambda i: (jnp.maximum(i - off, 0), 0, 0)), _full((TB, 256)), _full((1, QL)), _full((NH, DKP, QL)), _full((1, KVL)), _full((NH, KVL, 256)),
                  _full((1, DKP)), _full((1, DKP))],
        out_specs=[row(512), _full((NH, DK, QL)), _full((NH, KVL, 256)), _full((1, QL)), _full((1, KVL)),
                   _full((1, DKP)), _full((1, DKP))],
        out_shape=[jax.ShapeDtypeStruct((t_all, 512), BF16), jax.ShapeDtypeStruct((NH, DK, QL), F32),
                   jax.ShapeDtypeStruct((NH, KVL, 256), F32), jax.ShapeDtypeStruct((1, QL), F32),
                   jax.ShapeDtypeStruct((1, KVL), F32), jax.ShapeDtypeStruct((1, DKP), F32), jax.ShapeDtypeStruct((1, DKP), F32)],
        compiler_params=_params(("arbitrary",)),
    )(u, dq, dk, dv, cos, sin, q_lora_g, w_uq_t, kv_lora_g, w_ukv, qn_g, kn_g)


def _in_bwd(ctx, x, modsel, norm_g, dlo, dga, dgp, dpool, dxn, w_in_t):
    s_len, lc = x.shape[0], ctx.shape[0]
    t_all = s_len + lc
    off = lc // TB
    nb = t_all // TB
    nq = s_len // TB
    hb = TB // HALO
    n = TB + 2 * HALO

    def body(ctx_ref, x_ref, mod_ref, ng_ref, dlo_ref, dga_ref, dgp_ref, dp_ref, dpprev_ref, dpnext_ref, dxn_ref, win_ref,
             gx_ref, dwin_ref, dmod_ref, dng_ref):
        i = pl.program_id(0)
        j = i - off

        @pl.when(i == 0)
        def _():
            dwin_ref[...] = jnp.zeros_like(dwin_ref)
            dmod_ref[...] = jnp.zeros_like(dmod_ref)
            dng_ref[...] = jnp.zeros_like(dng_ref)

        latent = i >= off
        dp = dp_ref[...]
        prev = jnp.where(j <= 0, 0.0, dpprev_ref[...])
        nxt = jnp.where(j >= nq - 1, 0.0, dpnext_ref[...])
        win = jnp.concatenate([prev, dp, nxt], axis=0)
        tg = j * TB - HALO + lax.broadcasted_iota(jnp.int32, (n, 1), 0)
        dpin = []
        for g, w in enumerate(POOL_WINDOWS):
            cnt = jnp.maximum(jnp.minimum(tg + w // 2, s_len) - jnp.maximum(tg - w // 2, 0), 1).astype(F32)
            zq = win[:, g * 128:(g + 1) * 128] / cnt
            zq = zq + _shift_rows(zq, 1)
            for step in (1, 2, 4):
                if w >= 4 * step:
                    zq = _shift_rows(zq, -step) + _shift_rows(zq, step)
            dpin.append(zq[HALO:HALO + TB] - dp[:, g * 128:(g + 1) * 128])
        zero = jnp.zeros((TB, 512), BF16)
        du = [dlo_ref[...], jnp.where(latent, dga_ref[...], zero),
              jnp.where(latent, jnp.concatenate(dpin, axis=1).astype(BF16), zero), jnp.where(latent, dgp_ref[...], zero)]

        ng = ng_ref[...]
        xb = jnp.where(i < off, ctx_ref[...], x_ref[...])
        r, xh, xg, h, scale = _modulated(xb, mod_ref, ng)
        hb_ = h.astype(BF16)
        dh = jnp.zeros((TB, D), F32)
        for s, (lo, hi) in enumerate(SEG):
            dwin_ref[lo:hi, :] += _tn(du[s], hb_)
            dh += _nn(du[s], win_ref[lo:hi, :])
        is_lat = latent.astype(F32)
        dsh = _colsum(dh)
        dsc = _colsum(dh * xg)
        dmod_ref[0, 0:1, :] += dsh * (1.0 - is_lat)
        dmod_ref[0, 1:2, :] += dsc * (1.0 - is_lat)
        dmod_ref[1, 0:1, :] += dsh * is_lat
        dmod_ref[1, 1:2, :] += dsc * is_lat
        dxg = dh * (1.0 + scale)
        dng_ref[...] += _colsum(dxg * xh)
        dxh = dxg * ng
        gx_ref[...] = r * (dxh - xh * jnp.mean(dxh * xh, axis=-1, keepdims=True)) + dxn_ref[...]

    row = lambda w: pl.BlockSpec((TB, w), lambda i: (i, 0))
    lat = lambda w: pl.BlockSpec((TB, w), lambda i: (jnp.maximum(i - off, 0), 0))
    last8 = s_len // HALO - 1
    cspec, xspec, mspec = _token_specs(off)
    return pl.pallas_call(
        body, name="in_bwd", grid=(nb,),
        in_specs=[cspec, xspec, mspec, _full((1, D)), row(512), lat(512), lat(512), lat(512),
                  pl.BlockSpec((HALO, 512), lambda i: (jnp.maximum(jnp.maximum(i - off, 0) * hb - 1, 0), 0)),
                  pl.BlockSpec((HALO, 512), lambda i: (jnp.minimum((jnp.maximum(i - off, 0) + 1) * hb, last8), 0)),
                  lat(D), _full((DIN, D))],
        out_specs=[lat(D), _full((DIN, D)), _full((2, 2, D)), _full((1, D))],
        out_shape=[jax.ShapeDtypeStruct((s_len, D), F32), jax.ShapeDtypeStruct((DIN, D), F32),
                   jax.ShapeDtypeStruct((2, 2, D), F32), jax.ShapeDtypeStruct((1, D), F32)],
        compiler_params=_params(("arbitrary",)),
    )(ctx, x, modsel, norm_g, dlo, dga, dgp, dpool, dpool, dpool, dxn, w_in_t)


def _adamw_update(w_ref, g_ref, m_ref, v_ref, d_ref, mo_ref, vo_ref):
    gv = g_ref[...]
    mn = ADAM_B1 * m_ref[...] + (1.0 - ADAM_B1) * gv
    vn = ADAM_B2 * v_ref[...] + (1.0 - ADAM_B2) * (gv * gv)
    m_hat = mn / (1.0 - ADAM_B1 ** ADAM_STEP)
    v_hat = vn / (1.0 - ADAM_B2 ** ADAM_STEP)
    d_ref[...] = -ADAM_LR * (m_hat / (jnp.sqrt(v_hat) + ADAM_EPS) + ADAM_WD * w_ref[...])
    mo_ref[...] = mn
    vo_ref[...] = vn


def _adamw_many(ws, gs, ms, vs):
    n = len(ws)
    parts = 4

    def body(*refs):
        for i in range(n):
            _adamw_update(refs[i], refs[n + i], refs[2 * n + i], refs[3 * n + i], refs[4 * n + i], refs[5 * n + i], refs[6 * n + i])
            refs[7 * n + i][...] = refs[n + i][...]

    def spec(w):
        rows, cols = w.shape
        if rows % (8 * parts) == 0:
            return pl.BlockSpec((rows // parts, cols), lambda i: (i, 0))
        if cols % (128 * parts) == 0:
            return pl.BlockSpec((rows, cols // parts), lambda i: (0, i))
        return _full((rows, cols))

    specs = [spec(w) for w in ws]
    shp = [jax.ShapeDtypeStruct(w.shape, F32) for w in ws]
    out = pl.pallas_call(body, name="adamw_many", grid=(parts,), in_specs=specs * 4, out_specs=specs * 4, out_shape=shp * 4,
                         compiler_params=_params(("arbitrary",)))(*ws, *gs, *ms, *vs)
    return out[:n], out[n:2 * n], out[2 * n:3 * n], out[3 * n:]


class _Links:
    def __init__(self, send_sems, recv_sems):
        self.send_sems, self.recv_sems, self.sends = send_sems, recv_sems, []

    def send(self, src, dst, sem, to):
        cp = pltpu.make_async_remote_copy(src, dst, self.send_sems.at[sem], self.recv_sems.at[sem], device_id=to,
                                          device_id_type=MESH)
        cp.start()
        self.sends.append(cp)

    def arrived(self, dst, sem, frm):
        pltpu.make_async_remote_copy(dst, dst, self.send_sems.at[sem], self.recv_sems.at[sem], device_id=frm,
                                     device_id_type=MESH).wait_recv()

    def drain(self):
        for cp in self.sends:
            cp.wait_send()


def _half(ref, c, axis):
    size = ref.shape[axis - 2] // 2
    win = pl.ds(pl.multiple_of(c * size, 16 if axis == 0 else 128), size)
    idx = (win, slice(None)) if axis == 0 else (slice(None), win)
    return ref.at[(slice(None),) * (len(ref.shape) - 2) + idx]


def _select_rows(slots_ref, n_slots, row=0):
    sub = lax.broadcasted_iota(jnp.int32, (8, 1), 0)
    out = None
    for d in range(n_slots):
        r = jnp.where(sub == d, jnp.broadcast_to(slots_ref[d][row:row + 1, :], (8, slots_ref.shape[-1])), 0.0)
        out = r if out is None else out + r
    return out


def _gather(c, c_ctx, w_mod, b_mod_k, shards, axes, slab_rows):
    nw = len(shards)
    kw = w_mod.shape[1]

    def body(*refs):
        c_ref, cc_ref, wm_hbm, b_ref = refs[:4]
        w_hbm = refs[4:4 + nw]
        a16_ref, mod_ref = refs[4 + nw:6 + nw]
        out_refs = refs[6 + nw:6 + 2 * nw]
        g_refs = refs[6 + 2 * nw:6 + 3 * nw]
        f_refs = refs[6 + 3 * nw:6 + 4 * nw]
        wm_ref, a_ref, send_sems, recv_sems, local_sems = refs[6 + 4 * nw:]
        loads = [pltpu.make_async_copy(w_hbm[wi], f_refs[wi], local_sems.at[wi]) for wi in range(nw)]
        loads.append(pltpu.make_async_copy(wm_hbm, wm_ref, local_sems.at[nw]))
        for cp in loads:
            cp.start()
        stores = []

        def slab(wi, chip, of=g_refs):
            return of[wi].at[chip].at[0:shards[wi].shape[0]]

        def store(src, dst, wi, slot):
            cp = pltpu.make_async_copy(src, dst, local_sems.at[nw + 1 + wi * 8 + slot])
            cp.start()
            stores.append(cp)

        def store_half(wi, chip, half, slot):
            store(_half(slab(wi, chip), half, axes[wi]), _half(slab(wi, chip, out_refs), half, axes[wi]), wi, slot)
        x, y, cc = lax.axis_index("x"), lax.axis_index("y"), lax.axis_index("c")
        me = 4 * x + 2 * y + cc
        k = 2 * x + y
        sibling = (x, y, 1 - cc)
        links = _Links(send_sems, recv_sems)
        chips = [_peer(x, y, cc, off + (0,)) for off in CHIPS3]
        chip_a = ((x + 1 - cc) % 2, (y + cc) % 2, cc)
        chip_b = ((x + cc) % 2, (y + 1 - cc) % 2, cc)
        chip_d = (1 - x, 1 - y, cc)
        cv = c_ref[...]
        sc = cv * _sig(cv)
        mine = a_ref.at[me]
        for r in range(8):
            mine[r:r + 1, :] = sc[:, r * 128:(r + 1) * 128]
        for j, off in enumerate(PEERS7):
            links.send(a_ref.at[me], a_ref.at[me], j, _peer(x, y, cc, off))
        for wi in range(nw):
            loads[wi].wait()
            slab(wi, k)[...] = f_refs[wi][...].astype(BF16)
            for j, to in enumerate((chip_a, chip_b)):
                links.send(_half(slab(wi, k), cc, axes[wi]), _half(slab(wi, k), cc, axes[wi]), 10 + wi * 6 + j, to)
            store(slab(wi, k), slab(wi, k, out_refs), wi, 0)
            rows = shards[wi].shape[0]
            pad = slab_rows[wi] - rows
            if pad:
                for kk in range(4):
                    g_refs[wi][kk, rows:, :] = jnp.zeros((pad, shards[wi].shape[1]), BF16)
                store(g_refs[wi].at[:, pl.ds(rows, pad), :], out_refs[wi].at[:, pl.ds(rows, pad), :], wi, 7)
        for j, off in enumerate(PEERS7):
            px, py, pc = _peer(x, y, cc, off)
            links.arrived(a_ref.at[4 * px + 2 * py + pc], j, (px, py, pc))
        ccv = cc_ref[...]
        sub = lax.broadcasted_iota(jnp.int32, (8, 1), 0)
        top = jnp.zeros((8, D), F32)
        for d in range(8):
            blk = a_ref[d]
            row = jnp.concatenate([blk[r:r + 1, :] for r in range(8)], axis=1)
            top = top + jnp.where(sub == d, jnp.broadcast_to(row, (8, D)), 0.0)
        a16 = jnp.concatenate([top, jnp.where(sub == 0, jnp.broadcast_to(ccv * _sig(ccv), (8, D)), 0.0)], axis=0)
        a16_ref[...] = a16
        loads[nw].wait()
        mod_ref[k] = _dot3(_nn, a16, wm_ref[...]) + b_ref[...]
        for j, to in enumerate(chips):
            links.send(mod_ref.at[k], mod_ref.at[k], 7 + j, to)

        def over_ici(j, frm, origin):
            for wi in range(nw):
                blk = _half(slab(wi, 2 * origin[0] + origin[1]), cc, axes[wi])
                links.arrived(blk, 10 + wi * 6 + j, frm)
                if j == 0:
                    links.send(blk, blk, 10 + wi * 6 + 2, chip_b)
                links.send(blk, blk, 10 + wi * 6 + 3 + j, sibling)
                store_half(wi, 2 * origin[0] + origin[1], cc, 1 + j)

        def from_sibling(j, origin):
            for wi in range(nw):
                links.arrived(_half(slab(wi, 2 * origin[0] + origin[1]), 1 - cc, axes[wi]), 10 + wi * 6 + 3 + j, sibling)
                store_half(wi, 2 * origin[0] + origin[1], 1 - cc, 4 + j)
        over_ici(0, chip_a, chip_a)
        over_ici(1, chip_b, chip_b)
        from_sibling(0, chip_b)
        over_ici(2, chip_b, chip_d)
        from_sibling(1, chip_a)
        from_sibling(2, chip_d)
        for j, (px, py, pc) in enumerate(chips):
            links.arrived(mod_ref.at[2 * px + py], 7 + j, (px, py, pc))
        links.drain()
        for cp in stores:
            cp.wait()

    nsem = 10 + 6 * nw
    gathered = [(4, r, s.shape[1]) for r, s in zip(slab_rows, shards)]
    return pl.pallas_call(
        body, name="gather", in_specs=[VM, VM, ANY, VM] + [ANY] * nw, out_specs=[VM, VM] + [ANY] * nw,
        out_shape=[jax.ShapeDtypeStruct((16, D), F32), jax.ShapeDtypeStruct((4, 16, kw), F32)]
        + [jax.ShapeDtypeStruct(g, BF16) for g in gathered],
        scratch_shapes=[pltpu.VMEM(g, BF16) for g in gathered] + [pltpu.VMEM(s.shape, F32) for s in shards]
        + [pltpu.VMEM(w_mod.shape, F32), pltpu.VMEM((8, 8, D // 8), F32), pltpu.SemaphoreType.DMA((nsem,)),
           pltpu.SemaphoreType.DMA((nsem,)), pltpu.SemaphoreType.DMA((nw + 1 + 8 * nw,))],
        compiler_params=pltpu.CompilerParams(vmem_limit_bytes=VMEM_LIMIT),
    )(c, c_ctx, w_mod, b_mod_k, *shards)


SMALL_ROW_WIDTHS = (D, QL, KVL, DKP, DKP, 512, 128)
SMALL_OUT_WIDTHS = (D, QL, KVL, DK, DK, 512, 1)
SMALL_PACK = 384


def _small_pieces():
    pieces = []
    for i, w in enumerate(SMALL_ROW_WIDTHS):
        for c0 in range(0, w, 128):
            j = len(pieces)
            pieces.append((i, c0, j // (SMALL_PACK // 128), j % (SMALL_PACK // 128) * 128))
    assert len(pieces) <= 8 * (SMALL_PACK // 128)
    return pieces


def _reduce(grads, axes, smalls, w_pool_g, dmod8, a16, w_mod, c_ctx):
    nw = len(grads)
    ns = len(smalls)
    kw = w_mod.shape[1]
    halves = []
    for g, ax in zip(grads, axes):
        halves.append((g.shape[1] // 2, g.shape[2]) if ax == 0 else (g.shape[1], g.shape[2] // 2))

    def body(*refs):
        g_refs = refs[:nw]
        small_refs = refs[nw:nw + ns]
        wp_ref, dm_ref, a16_ref, wm_hbm, cc_ref = refs[nw + ns:nw + ns + 5]
        o = nw + ns + 5
        r_outs = refs[o:o + nw]
        small_outs = refs[o + nw:o + nw + ns]
        rwp_ref, gw_out, gb_ref, gc_ref = refs[o + nw + ns:o + nw + ns + 4]
        o = o + nw + ns + 4
        own, sib, part, got, rel, r_refs = (refs[o + i * nw:o + (i + 1) * nw] for i in range(6))
        smbuf, wps, wpg, dm_all, pc_all, wm_ref, gw_ref, send_sems, recv_sems, local_sems = refs[o + 6 * nw:]
        wm_load = pltpu.make_async_copy(wm_hbm, wm_ref, local_sems.at[nw])
        wm_load.start()
        stores = []

        def store(src, dst, sem):
            cp = pltpu.make_async_copy(src, dst, local_sems.at[nw + 1 + sem])
            cp.start()
            stores.append(cp)
        x, y, cc = lax.axis_index("x"), lax.axis_index("y"), lax.axis_index("c")
        me = 4 * x + 2 * y + cc
        k = 2 * x + y
        sibling = (x, y, 1 - cc)
        links = _Links(send_sems, recv_sems)
        chips = [_peer(x, y, cc, off + (0,)) for off in CHIPS3]
        peers = [_peer(x, y, cc, off) for off in PEERS7]
        chip_a = ((x + 1 - cc) % 2, (y + cc) % 2, cc)
        chip_b = ((x + cc) % 2, (y + 1 - cc) % 2, cc)
        ka, kb, kd = 2 * chip_a[0] + chip_a[1], 2 * chip_b[0] + chip_b[1], 2 * (1 - x) + (1 - y)
        big, sm0, wp0, dm0, pc0 = 0, 5 * nw, 5 * nw + 7, 5 * nw + 14, 5 * nw + 21

        locals_ = []
        for wi in range(nw):
            lc = pltpu.make_async_copy(_half(g_refs[wi], cc, axes[wi]), own[wi], local_sems.at[wi])
            lc.start()
            locals_.append(lc)
            links.send(_half(g_refs[wi], 1 - cc, axes[wi]), sib[wi], big + wi * 5, sibling)
        slot = smbuf.at[me]
        slot[...] = jnp.zeros((8, SMALL_PACK), F32)
        small_rows = [jnp.broadcast_to(ref[...], (1, w)) for ref, w in zip(small_refs, SMALL_ROW_WIDTHS)]
        for i, c0, row, lane in _small_pieces():
            slot[row:row + 1, lane:lane + 128] = small_rows[i][:, c0:c0 + 128]
        links.send(wp_ref, wps, wp0, sibling)
        dm_all[me] = dm_ref[...]
        for j, peer in enumerate(peers):
            links.send(dm_all.at[me], dm_all.at[me], dm0 + j, peer)
            links.send(smbuf.at[me], smbuf.at[me], sm0 + j, peer)
        links.arrived(wps, wp0, sibling)
        wpg[k] = (wp_ref[...] + wps[...]).astype(BF16)
        for j, to in enumerate(chips):
            links.send(wpg.at[k], wpg.at[k], wp0 + 1 + j, to)
        for wi in range(nw):
            locals_[wi].wait()
            links.arrived(sib[wi], big + wi * 5, sibling)
            part[wi][...] = (own[wi][...] + sib[wi][...]).astype(BF16)
            got[wi][k] = part[wi][k]
            got[wi][kd] = jnp.zeros(halves[wi], BF16)
            links.send(part[wi].at[kd], rel[wi], big + wi * 5 + 1, chip_b)
            links.send(part[wi].at[kb], got[wi].at[k], big + wi * 5 + 2, chip_b)
        for j, (px, py, pc) in enumerate(peers):
            links.arrived(dm_all.at[4 * px + 2 * py + pc], dm0 + j, (px, py, pc))
        dm_tot = dm_all[0]
        for d in range(1, 8):
            dm_tot = dm_tot + dm_all[d]
        for kk in range(4):
            gb_ref[:, kk * kw:(kk + 1) * kw] = dm_tot[kk:kk + 1, :] + dm_tot[4 + kk:5 + kk, :]
        top = jnp.zeros((8, kw), F32)
        dmc_k = jnp.zeros((1, kw), F32)
        for kk in range(4):
            top = top + jnp.where(k == kk, _select_rows(dm_all, 8, kk), 0.0)
            dmc_k = dmc_k + jnp.where(k == kk, dm_tot[4 + kk:5 + kk, :], 0.0)
        sub = lax.broadcasted_iota(jnp.int32, (8, 1), 0)
        bk = jnp.concatenate([top, jnp.where(sub == 0, jnp.broadcast_to(dmc_k, (8, kw)), 0.0)], axis=0)
        gw_ref[...] = _dot3(_tn, a16_ref[...], bk)
        store(gw_ref, gw_out, 0)
        wm_load.wait()
        pc_all[k] =_dot3(_nt, jnp.broadcast_to(bk[8:9, :], (8, kw)), wm_ref[...])
        for j, to in enumerate(chips):
            links.send(pc_all.at[k], pc_all.at[k], pc0 + j, to)
        for j, (px, py, pc) in enumerate(peers):
            links.arrived(smbuf.at[4 * px + 2 * py + pc], sm0 + j, (px, py, pc))
        tot = smbuf[0]
        for d in range(1, 8):
            tot = tot + smbuf[d]
        for i, c0, row, lane in _small_pieces():
            n = min(128, SMALL_OUT_WIDTHS[i] - c0)
            if n > 0:
                small_outs[i][:, c0:c0 + n] = tot[row:row + 1, lane:lane + n]
        for j, (px, py, pc) in enumerate(chips):
            links.arrived(wpg.at[2 * px + py], wp0 + 1 + j, (px, py, pc))
        wpt = wpg[0].astype(F32)
        for kk in range(1, 4):
            wpt = wpt + wpg[kk].astype(F32)
        rwp_ref[...] = wpt
        for wi in range(nw):
            links.arrived(rel[wi], big + wi * 5 + 1, chip_b)
            rel[wi][...] = (part[wi][ka].astype(F32) + rel[wi][...].astype(F32)).astype(BF16)
            links.send(rel[wi], got[wi].at[k], big + wi * 5 + 3, chip_a)
        for wi in range(nw):
            links.arrived(got[wi].at[kb], big + wi * 5 + 2, chip_b)
            links.arrived(got[wi].at[ka], big + wi * 5 + 3, chip_a)
            total = got[wi][0].astype(F32)
            for kk in range(1, 4):
                total = total + got[wi][kk].astype(F32)
            mine = _half(r_refs[wi], cc, axes[wi])
            mine[...] = total
            links.send(mine, mine, big + wi * 5 + 4, sibling)
            store(mine, _half(r_outs[wi], cc, axes[wi]), 1 + wi)
        for j, (px, py, pc) in enumerate(chips):
            links.arrived(pc_all.at[2 * px + py], pc0 + j, (px, py, pc))
        ccv = cc_ref[...]
        sg = _sig(ccv)
        gc_ref[...] = (pc_all[0][0:1, :] + pc_all[1][0:1, :] + pc_all[2][0:1, :] + pc_all[3][0:1, :]) * (sg * (1.0 + ccv * (1.0 - sg)))
        for wi in range(nw):
            links.arrived(_half(r_refs[wi], 1 - cc, axes[wi]), big + wi * 5 + 4, sibling)
            store(_half(r_refs[wi], 1 - cc, axes[wi]), _half(r_outs[wi], 1 - cc, axes[wi]), 1 + nw + wi)
        links.drain()
        for cp in stores:
            cp.wait()

    nsem = 5 * nw + 24
    quads = [(4,) + h for h in halves]
    return pl.pallas_call(
        body, name="reduce", in_specs=[ANY] * nw + [VM] * (ns + 3) + [ANY, VM],
        out_specs=[ANY] * nw + [VM] * (ns + 1) + [ANY, VM, VM],
        out_shape=[jax.ShapeDtypeStruct(g.shape[1:], F32) for g in grads]
        + [jax.ShapeDtypeStruct((1, w), F32) for w in SMALL_OUT_WIDTHS]
        + [jax.ShapeDtypeStruct(w_pool_g.shape, F32), jax.ShapeDtypeStruct((D, kw), F32), jax.ShapeDtypeStruct((1, 3 * D), F32),
           jax.ShapeDtypeStruct((1, D), F32)],
        scratch_shapes=[pltpu.VMEM(q, F32) for q in quads] + [pltpu.VMEM(q, F32) for q in quads]
        + [pltpu.VMEM(q, BF16) for q in quads] + [pltpu.VMEM(q, BF16) for q in quads] + [pltpu.VMEM(h, BF16) for h in halves]
        + [pltpu.VMEM(g.shape[1:], F32) for g in grads]
        + [pltpu.VMEM((8, 8, SMALL_PACK), F32), pltpu.VMEM(w_pool_g.shape, F32), pltpu.VMEM((4,) + w_pool_g.shape, BF16),
           pltpu.VMEM((8, 8, kw), F32), pltpu.VMEM((4, 8, D), F32), pltpu.VMEM((D, kw), F32), pltpu.VMEM((D, kw), F32)]
        + [pltpu.SemaphoreType.DMA((nsem,)), pltpu.SemaphoreType.DMA((nsem,)), pltpu.SemaphoreType.DMA((3 * nw + 2,))],
        compiler_params=pltpu.CompilerParams(vmem_limit_bytes=VMEM_LIMIT),
    )(*grads, *smalls, w_pool_g, dmod8, a16, w_mod, c_ctx)


def _rope_tables(s_len):
    rows = s_len // GRID_W
    per = TB // GRID_W
    n_freq = 16
    inv = ROPE_BASE ** (-jnp.arange(n_freq, dtype=F32) / n_freq)
    ang_r = jnp.arange(rows, dtype=F32)[:, None] * inv
    ang_c = jnp.arange(GRID_W, dtype=F32)[:, None] * inv
    by_row, by_col = [], []
    for fn, pad in ((jnp.cos, 1.0), (jnp.sin, 0.0)):
        r = jnp.concatenate([fn(ang_r), fn(ang_r), jnp.zeros((rows, 96), F32)], axis=1).reshape(rows // per, per, 128)
        by_row.append(jnp.pad(r, ((0, 0), (0, 8 - per), (0, 0))))
        cpart = jnp.concatenate([jnp.zeros((GRID_W, 32), F32), fn(ang_c), fn(ang_c), jnp.full((GRID_W, 64), pad, F32)], axis=1)
        by_col.append(jnp.tile(cpart, (per, 1)))
    return jnp.concatenate(by_row, axis=-1), jnp.concatenate(by_col, axis=-1)


def kernel(x, c, ctx, c_ctx, w_mod, b_mod, norm_g, w_in, q_lora_g, w_uq, kv_lora_g, w_ukv, q_norm_g, k_norm_g, w_pool, pool_scale, w_out, loss_target, m_c_ctx, m_w_mod, m_b_mod, m_norm_g, m_w_in, m_q_lora_g, m_w_uq, m_kv_lora_g, m_w_ukv, m_q_norm_g, m_k_norm_g, m_w_pool, m_pool_scale, m_w_out, v_c_ctx, v_w_mod, v_b_mod, v_norm_g, v_w_in, v_q_lora_g, v_w_uq, v_kv_lora_g, v_w_ukv, v_q_norm_g, v_k_norm_g, v_w_pool, v_pool_scale, v_w_out):
    xi, yi, ci = lax.axis_index("x"), lax.axis_index("y"), lax.axis_index("c")
    me = 4 * xi + 2 * yi + ci
    k = 2 * xi + yi
    s_len = x.shape[1]
    lc = ctx.shape[1]
    kw = w_mod.shape[2]
    weights = dict(c_ctx=c_ctx, w_mod=w_mod, b_mod=b_mod, norm_g=norm_g, w_in=w_in, q_lora_g=q_lora_g, w_uq=w_uq,
                   kv_lora_g=kv_lora_g, w_ukv=w_ukv, q_norm_g=q_norm_g, k_norm_g=k_norm_g, w_pool=w_pool,
                   pool_scale=pool_scale, w_out=w_out)
    m_in = dict(c_ctx=m_c_ctx, w_mod=m_w_mod, b_mod=m_b_mod, norm_g=m_norm_g, w_in=m_w_in, q_lora_g=m_q_lora_g, w_uq=m_w_uq,
                kv_lora_g=m_kv_lora_g, w_ukv=m_w_ukv, q_norm_g=m_q_norm_g, k_norm_g=m_k_norm_g, w_pool=m_w_pool,
                pool_scale=m_pool_scale, w_out=m_w_out)
    v_in = dict(c_ctx=v_c_ctx, w_mod=v_w_mod, b_mod=v_b_mod, norm_g=v_norm_g, w_in=v_w_in, q_lora_g=v_q_lora_g, w_uq=v_w_uq,
                kv_lora_g=v_kv_lora_g, w_ukv=v_w_ukv, q_norm_g=v_q_norm_g, k_norm_g=v_k_norm_g, w_pool=v_w_pool,
                pool_scale=v_pool_scale, w_out=v_w_out)
    order = ["c_ctx", "w_mod", "b_mod", "norm_g", "w_in", "q_lora_g", "w_uq", "kv_lora_g", "w_ukv", "q_norm_g", "k_norm_g",
             "w_pool", "pool_scale", "w_out"]
    transposed = ("w_in", "w_uq")
    as2d = lambda n, a: jnp.transpose(a[0]) if n in transposed else a.reshape(-1, a.shape[-1])
    back = lambda n, a: jnp.transpose(a)[None] if n in transposed else a.reshape(weights[n].shape)

    c_ctx2 = c_ctx.reshape(1, D)
    b_mod_k = lax.dynamic_slice(b_mod, (0, k * kw), (1, kw))
    split = (0, 1, 0, 0)
    a16, mod_all, g_out, g_in, g_uq, g_ukv = _gather(
        c, c_ctx2, w_mod[0], b_mod_k, [w_out[0], as2d("w_in", w_in), as2d("w_uq", w_uq), w_ukv[0]], split,
        (D // 4, DIN // 4, DKP, KVL))
    mod_me = lax.dynamic_index_in_dim(mod_all, me, axis=1, keepdims=False).reshape(3, D)
    mod_c = mod_all[:, 8, :].reshape(3, D)
    modsel = jnp.stack([mod_c, mod_me])
    w_in_t = g_in.reshape(DIN, D)
    w_uq_t = g_uq
    w_out_f = g_out.reshape(D, D)
    qn_g = jnp.pad(q_norm_g, ((0, 0), (0, DKP - DK)))
    kn_g = jnp.pad(k_norm_g, ((0, 0), (0, DKP - DK)))
    w_pool_b = w_pool[0].astype(BF16)
    cos, sin = _rope_tables(s_len)

    u, q, kk, v = _fwd_in(ctx[0], x[0], modsel, norm_g, w_in_t, q_lora_g, w_uq_t, kv_lora_g, g_ukv, qn_g, kn_g, cos, sin)
    attn, lse = _attn_fwd(q, kk, v, s_len)
    (dxn, dattn, dga, dgp, dpool, dw_out, dgate, dps, dw_pool, loss) = _out_stage(
        attn.reshape(s_len // Q_BLOCK, Q_BLOCK, NH * DV), u, x[0], loss_target[0], modsel[1, 2:3, :], w_pool_b, pool_scale,
        w_out_f, lc)
    dattn = dattn.reshape(s_len, NH * DV)
    dq, dk, dv = _attn_bwd(q, kk, v, dattn, attn, lse, s_len)
    dlo, dw_uq_t, dw_ukv, dqlg, dkvlg, dqng, dkng = _qkv_bwd(u, dq, dk, dv, cos, sin, q_lora_g, w_uq_t, kv_lora_g, g_ukv,
                                                            qn_g, kn_g, s_len)
    gx, dw_in_t, dmod, dng = _in_bwd(ctx[0], x[0], modsel, norm_g, dlo, dga, dgp, dpool, dxn, w_in_t)

    dmod_l = jnp.concatenate([dmod[1, 0], dmod[1, 1], dgate[0]]).reshape(4, kw)
    dmod_c = jnp.concatenate([dmod[0, 0], dmod[0, 1], jnp.zeros((D,), F32)]).reshape(4, kw)
    dmod8 = jnp.concatenate([dmod_l, dmod_c], axis=0)
    r_out, r_in, r_uq, r_ukv, g_ng, g_qlg, g_kvlg, g_qng, g_kng, g_ps, loss_all, g_wp, g_w_mod, g_b_mod, g_c_ctx = _reduce(
        [dw_out.reshape(4, D // 4, D), dw_in_t.reshape(4, DIN // 4, D), dw_uq_t, dw_ukv], split,
        [dng, dqlg, dkvlg, dqng, dkng, dps, loss], dw_pool, dmod8, a16, w_mod[0], c_ctx2)
    g2d = dict(c_ctx=g_c_ctx, b_mod=g_b_mod, w_mod=g_w_mod, w_in=r_in, w_uq=r_uq, w_ukv=r_ukv, w_out=r_out, norm_g=g_ng,
               q_lora_g=g_qlg, kv_lora_g=g_kvlg, q_norm_g=g_qng, k_norm_g=g_kng, pool_scale=g_ps, w_pool=g_wp.reshape(512, 128))

    outs = _adamw_many([as2d(n, weights[n]) for n in order], [g2d[n] for n in order], [as2d(n, m_in[n]) for n in order],
                       [as2d(n, v_in[n]) for n in order])
    d2d, m2d, v2d, g2d = (dict(zip(order, arrs)) for arrs in outs)

    return (loss_all[0, 0], gx[None], *[back(n, g2d[n]) for n in order], *[back(n, d2d[n]) for n in order],
            *[back(n, m2d[n]) for n in order], *[back(n, v2d[n]) for n in order])
```

```python
import jax
import jax.numpy as jnp
import numpy as np
from jax import lax
from jax.experimental import pallas as pl
from jax.experimental.pallas import tpu as pltpu

F32 = jnp.float32
BF16 = jnp.bfloat16
MESH = pl.DeviceIdType.MESH

D = 1024
NH = 4
DK = 192
DKP = 256
DV = 128
QL = 256
KVL = 128
DIN = 1984
U_LO = 448
SEG = ((0, 512), (448, 960), (960, 1472), (1472, 1984))
DU = 2048
POOL_WINDOWS = (2, 4, 8, 16)
HALO = 8
EPS = 1e-6
ROPE_BASE = 10000.0
GRID_W = 64
Q_BLOCK = 128
TB = 256
BWD_QBLOCKS = 1
SCALE = DK ** -0.5
LOG2E = 1.4426950408889634
LN2 = 0.6931471805599453
VMEM_LIMIT = 56 * 1024 * 1024

ADAM_LR = 0.001
ADAM_B1 = 0.9
ADAM_B2 = 0.999
ADAM_EPS = 1e-08
ADAM_WD = 0.01
ADAM_STEP = 10

CHIPS3 = ((1, 0), (0, 1), (1, 1))
PEERS7 = tuple((dx, dy, dc) for dx in (0, 1) for dy in (0, 1) for dc in (0, 1) if (dx, dy, dc) != (0, 0, 0))

VM = pl.BlockSpec(memory_space=pltpu.VMEM)
ANY = pl.BlockSpec(memory_space=pl.ANY)


def _nn(a, b):
    return jnp.dot(a, b, preferred_element_type=F32)


def _nt(a, b):
    return lax.dot_general(a, b, (((1,), (1,)), ((), ())), preferred_element_type=F32)


def _tn(a, b):
    return lax.dot_general(a, b, (((0,), (0,)), ((), ())), preferred_element_type=F32)


def _split3(a):
    a0 = a.astype(BF16)
    r = a - a0.astype(F32)
    a1 = r.astype(BF16)
    a2 = (r - a1.astype(F32)).astype(BF16)
    return a0, a1, a2


def _dot3(dot, a, b):
    sa = _split3(a)
    sb = _split3(b)
    out = None
    for i in range(3):
        for j in range(3 - i):
            t = dot(sa[i], sb[j])
            out = t if out is None else out + t
    return out


def _sig(x):
    return 1.0 / (1.0 + jnp.exp(-x))


def _rot(t):
    src = lax.broadcasted_iota(jnp.int32, (128, 128), 0)
    dst = lax.broadcasted_iota(jnp.int32, (128, 128), 1)
    first = (dst % 32) < 16
    perm = jnp.where(first & (src == dst + 16), -1.0, jnp.where(~first & (src == dst - 16), 1.0, 0.0)).astype(BF16)
    hi = t.astype(BF16)
    lo = (t - hi.astype(F32)).astype(BF16)
    return _nn(hi, perm) + _nn(lo, perm)


def _rope(t, cos, sin):
    return t * cos + _rot(t) * sin


def _rope_t(t, cos, sin):
    return t * cos - _rot(t * sin)


def _rope_block(rows_ref, cols_ref, is_ctx):
    lane = lax.broadcasted_iota(jnp.int32, (TB, 256), 1) % 128
    rows = jnp.concatenate([jnp.broadcast_to(rows_ref[0, r:r + 1, :], (GRID_W, 256)) for r in range(TB // GRID_W)], axis=0)
    cs = jnp.where(lane < 32, rows, cols_ref[...])
    return jnp.where(is_ctx, 1.0, cs[:, :128]), jnp.where(is_ctx, 0.0, cs[:, 128:])


def _shift_rows(z, k):
    n = z.shape[0]
    return pltpu.roll(z, (n - k) % n, 0)


def _colsum(a):
    return jnp.sum(a, axis=0, keepdims=True)


def _rowsum(a):
    return jnp.sum(a, axis=-1, keepdims=True)


def _row_layout(col):
    return jnp.transpose(jnp.broadcast_to(col, (col.shape[0], 128)))[0:8, :]


def _params(sem=None):
    return pltpu.CompilerParams(dimension_semantics=sem, vmem_limit_bytes=VMEM_LIMIT)


def _full(shape):
    nd = len(shape)
    return pl.BlockSpec(shape, lambda *_: (0,) * nd)


def _peer(x, y, c, off):
    dx, dy, dc = off
    return ((x + dx) % 2, (y + dy) % 2, (c + dc) % 2)


def _token_specs(off):
    ctx = pl.BlockSpec((TB, D), lambda i: (jnp.minimum(i, off - 1), 0))
    lat = pl.BlockSpec((TB, D), lambda i: (jnp.maximum(i - off, 0), 0))
    mod = pl.BlockSpec((1, 3, D), lambda i: (jnp.minimum(i // off, 1), 0, 0))
    return ctx, lat, mod


def _modulated(x, mod_ref, ng):
    shift = mod_ref[0, 0:1, :]
    scale = mod_ref[0, 1:2, :]
    r = lax.rsqrt(jnp.mean(x * x, axis=-1, keepdims=True) + EPS)
    xh = x * r
    xg = xh * ng
    return r, xh, xg, xg * (1.0 + scale) + shift, scale


def _fwd_in(ctx, x, modsel, norm_g, w_in_t, q_lora_g, w_uq_t, kv_lora_g, w_ukv, qn_g, kn_g, cos, sin):
    s_len, lc = x.shape[0], ctx.shape[0]
    t_all = s_len + lc
    nb = t_all // TB
    off = lc // TB

    def body(ctx_ref, x_ref, mod_ref, ng_ref, win_ref, qlg_ref, wuq_ref, kvlg_ref, wukv_ref, qng_ref, kng_ref, cos_ref, sin_ref,
             u_ref, q_ref, k_ref, v_ref):
        is_ctx = pl.program_id(0) < off
        xb = jnp.where(is_ctx, ctx_ref[...], x_ref[...])
        _, _, _, h, _ = _modulated(xb, mod_ref, ng_ref[...])
        hb = h.astype(BF16)
        lane = lax.broadcasted_iota(jnp.int32, (TB, 512), 1)
        ulo = jnp.where(lane < U_LO, _nt(hb, win_ref[SEG[0][0]:SEG[0][1], :]), 0.0)
        u_ref[:, 0:512] = ulo
        for j in range(1, 4):
            u_ref[:, j * 512:(j + 1) * 512] = _nt(hb, win_ref[SEG[j][0]:SEG[j][1], :])
        cos, sin = _rope_block(cos_ref, sin_ref, is_ctx)
        cq = ulo[:, 0:QL]
        cqn = (cq * lax.rsqrt(jnp.mean(cq * cq, axis=-1, keepdims=True) + EPS) * qlg_ref[...]).astype(BF16)
        qng = qng_ref[...]
        ckv = ulo[:, QL:QL + KVL]
        ckvn = (ckv * lax.rsqrt(jnp.mean(ckv * ckv, axis=-1, keepdims=True) + EPS) * kvlg_ref[...]).astype(BF16)
        qhs = [_nt(cqn, wuq_ref[hd]) for hd in range(NH)]
        kvs = [_nn(ckvn, wukv_ref[hd]) for hd in range(NH)]
        for hd in range(NH):
            qh = qhs[hd]
            qn = qh * lax.rsqrt(_rowsum(qh * qh) / DK + EPS) * qng
            q_ref[hd] = (jnp.concatenate([qn[:, :128], _rope(qn[:, 128:], cos, sin)], axis=1) * (SCALE * LOG2E)).astype(BF16)
        kr = ulo[:, 384:512]
        skr = _rowsum(kr * kr)
        kng = kng_ref[...]
        kr_roped = _rope(kr * kng[:, 128:], cos, sin)
        for hd in range(NH):
            kv = kvs[hd]
            kn = kv[:, :128]
            rk = lax.rsqrt((_rowsum(kn * kn) + skr) / DK + EPS)
            k_ref[hd] = jnp.concatenate([kn * rk * kng[:, :128], kr_roped * rk], axis=1).astype(BF16)
            v_ref[hd] = kv[:, 128:].astype(BF16)

    row = lambda w: pl.BlockSpec((TB, w), lambda i: (i, 0))
    heads = lambda w: pl.BlockSpec((NH, TB, w), lambda i: (0, i, 0))
    cspec, xspec, mspec = _token_specs(off)
    return pl.pallas_call(
        body, name="fwd_in", grid=(nb,),
        in_specs=[cspec, xspec, mspec, _full((1, D)), _full((DIN, D)), _full((1, QL)), _full((NH, DKP, QL)), _full((1, KVL)),
                  _full((NH, KVL, 256)), _full((1, DKP)), _full((1, DKP)),
                  pl.BlockSpec((1, 8, 256), lambda i: (jnp.maximum(i - off, 0), 0, 0)), _full((TB, 256))],
        out_specs=[row(DU), heads(DKP), heads(DKP), heads(DV)],
        out_shape=[jax.ShapeDtypeStruct((t_all, DU), F32), jax.ShapeDtypeStruct((NH, t_all, DKP), BF16),
                   jax.ShapeDtypeStruct((NH, t_all, DKP), BF16), jax.ShapeDtypeStruct((NH, t_all, DV), BF16)],
        compiler_params=_params(("arbitrary",)),
    )(ctx, x, modsel, norm_g, w_in_t, q_lora_g, w_uq_t, kv_lora_g, w_ukv, qn_g, kn_g, cos, sin)


def _attn_fwd(q, k, v, s_len):
    t_all = q.shape[1]
    off = (t_all - s_len) // TB
    nq = s_len // TB
    nsub = next(n for n in (4, 2, 1) if nq % n == 0)

    def body(*refs):
        q_refs = refs[:nsub]
        k_ref, v_ref, o_ref, lse_ref = refs[nsub:]
        for sb in range(nsub):
            s = _nt(q_refs[sb][0], k_ref[0])
            m = jnp.max(s, axis=-1, keepdims=True)
            e = jnp.exp2(s - m)
            l = _rowsum(e)
            o_ref[sb * TB:(sb + 1) * TB, :] = _nn(e.astype(BF16), v_ref[0]) / l
            lse_ref[0, sb] = _row_layout(m + jnp.log2(l))

    qspec = lambda sb: pl.BlockSpec((1, TB, DKP), lambda h, i: (h, i * nsub + sb + off, 0))
    return pl.pallas_call(
        body, name="attn_fwd", grid=(NH, nq // nsub),
        in_specs=[qspec(sb) for sb in range(nsub)]
        + [pl.BlockSpec((1, t_all, DKP), lambda h, i: (h, 0, 0)), pl.BlockSpec((1, t_all, DV), lambda h, i: (h, 0, 0))],
        out_specs=[pl.BlockSpec((nsub * TB, DV), lambda h, i: (i, h)), pl.BlockSpec((1, nsub, 8, TB), lambda h, i: (h, i, 0, 0))],
        out_shape=[jax.ShapeDtypeStruct((s_len, NH * DV), F32), jax.ShapeDtypeStruct((NH, nq, 8, TB), F32)],
        compiler_params=_params(("arbitrary", "arbitrary")),
    )(*([q] * nsub), k, v)


def _out_stage(attn, u, x, target, modsel, w_pool, pool_scale, w_out, lc):
    s_len = x.shape[0]
    t_all = s_len + lc
    off = lc // TB
    nq = s_len // TB
    hb = TB // HALO
    nqb = s_len // Q_BLOCK
    jb = TB // nqb

    def body(attn_ref, ga_ref, pin_ref, pprev_ref, pnext_ref, gp_ref, x_ref, tgt_ref, gate_ref, wp_ref, ps_ref, wo_ref,
             dxn_ref, dattn_ref, dga_ref, dgp_ref, dpool_ref, dwo_ref, dgate_ref, dps_ref, dwp_ref, loss_ref):
        i = pl.program_id(0)

        @pl.when(i == 0)
        def _():
            dwo_ref[...] = jnp.zeros_like(dwo_ref)
            dgate_ref[...] = jnp.zeros_like(dgate_ref)
            dps_ref[...] = jnp.zeros_like(dps_ref)
            dwp_ref[...] = jnp.zeros_like(dwp_ref)
            loss_ref[...] = jnp.zeros_like(loss_ref)

        attn = jnp.concatenate([attn_ref[:, jj, :] for jj in range(jb)], axis=0)
        ga = ga_ref[...]
        gp = gp_ref[...]
        pin = pin_ref[...]
        prev = jnp.where(i == 0, 0.0, pprev_ref[...])
        nxt = jnp.where(i == nq - 1, 0.0, pnext_ref[...])
        win = jnp.concatenate([prev, pin, nxt], axis=0)
        tg = i * TB + lax.broadcasted_iota(jnp.int32, (TB, 1), 0)
        pooled = []
        for g, w in enumerate(POOL_WINDOWS):
            a = win[:, g * 128:(g + 1) * 128]
            p = _shift_rows(a, -1) + a
            for step in (1, 2, 4):
                if w >= 4 * step:
                    p = _shift_rows(p, -step) + _shift_rows(p, step)
            cnt = (jnp.minimum(tg + w // 2, s_len) - jnp.maximum(tg - w // 2, 0)).astype(F32)
            pooled.append(p[HALO:HALO + TB] / cnt - a[HALO:HALO + TB])
        pooled_b = [p.astype(BF16) for p in pooled]
        z = jnp.concatenate([_nn(pooled_b[g], wp_ref[g]) for g in range(4)], axis=1)
        ps = ps_ref[...]
        yp = z * ps
        sga = _sig(ga)
        sila = ga * sga
        sgp = _sig(gp)
        silp = gp * sgp
        br = jnp.concatenate([sila * attn, silp * yp], axis=1).astype(BF16)
        y = _nn(br, wo_ref[...])
        gate = gate_ref[0, 2:3, :]
        err = x_ref[...] + gate * y - tgt_ref[...]
        loss_ref[...] += _colsum(_rowsum(err * err)) * (0.5 / D)
        dxn = err * (1.0 / D)
        dxn_ref[...] = dxn
        dgate_ref[...] += _colsum(dxn * y)
        dy = (dxn * gate).astype(BF16)
        dwo_ref[...] += _tn(br, dy)
        dbr = _nt(dy, wo_ref[...])
        dbra = dbr[:, :512]
        dbrp = dbr[:, 512:]
        dattn = dbra * sila
        for jj in range(jb):
            dattn_ref[:, jj, :] = dattn[jj * nqb:(jj + 1) * nqb]
        dga_ref[...] = (dbra * attn * (sga * (1.0 + ga * (1.0 - sga)))).astype(BF16)
        dgp_ref[...] = (dbrp * yp * (sgp * (1.0 + gp * (1.0 - sgp)))).astype(BF16)
        dyp = dbrp * silp
        dps_ref[...] += _colsum(dyp * z)
        dz = (dyp * ps).astype(BF16)
        dpool = []
        for g in range(4):
            dzg = dz[:, g * 128:(g + 1) * 128]
            dwp_ref[g] += _tn(pooled_b[g], dzg)
            dpool.append(_nt(dzg, wp_ref[g]))
        dpool_ref[...] = jnp.concatenate(dpool, axis=1)

    lat = lambda w: pl.BlockSpec((TB, w), lambda i: (i, 0))
    perm = pl.BlockSpec((nqb, jb, 512), lambda i: (0, i, 0))
    ucol = lambda j: pl.BlockSpec((TB, 512), lambda i: (i + off, j))
    last8 = t_all // HALO - 1
    return pl.pallas_call(
        body, name="out_stage", grid=(nq,),
        in_specs=[perm, ucol(1), ucol(2),
                  pl.BlockSpec((HALO, 512), lambda i: ((i + off) * hb - 1, 2)),
                  pl.BlockSpec((HALO, 512), lambda i: (jnp.minimum((i + off + 1) * hb, last8), 2)),
                  ucol(3), lat(D), lat(D), pl.BlockSpec((1, 3, D), lambda i: (1, 0, 0)), _full((4, 128, 128)), _full((1, 512)),
                  _full((D, D))],
        out_specs=[lat(D), perm, lat(512), lat(512), lat(512),
                   _full((D, D)), _full((1, D)), _full((1, 512)), _full((4, 128, 128)), _full((1, 1))],
        out_shape=[jax.ShapeDtypeStruct((s_len, D), F32), jax.ShapeDtypeStruct((nqb, Q_BLOCK, 512), F32),
                   jax.ShapeDtypeStruct((s_len, 512), BF16), jax.ShapeDtypeStruct((s_len, 512), BF16),
                   jax.ShapeDtypeStruct((s_len, 512), F32),
                   jax.ShapeDtypeStruct((D, D), F32), jax.ShapeDtypeStruct((1, D), F32), jax.ShapeDtypeStruct((1, 512), F32),
                   jax.ShapeDtypeStruct((4, 128, 128), F32), jax.ShapeDtypeStruct((1, 1), F32)],
        compiler_params=_params(("arbitrary",)),
    )(attn, u, u, u, u, u, x, target, modsel, w_pool, pool_scale, w_out)


def _attn_bwd(q, k, v, dattn, attn, lse, s_len):
    t_all = q.shape[1]
    off = (t_all - s_len) // TB
    nq = s_len // TB
    nch = 4
    chunks = [(c * (t_all // nch), t_all // nch) for c in range(nch)]
    nsub = next(n for n in (BWD_QBLOCKS, 2, 1) if nq % n == 0)
    tq = nsub * TB

    def body(*refs):
        q_refs = refs[:nsub]
        k_ref, v_ref, do_ref, o_ref, lse_ref, dq_ref, dk_ref, dv_ref = refs[nsub:]
        i = pl.program_id(1)

        @pl.when(i == 0)
        def _():
            dk_ref[...] = jnp.zeros_like(dk_ref)
            dv_ref[...] = jnp.zeros_like(dv_ref)

        qb = jnp.concatenate([r[0] for r in q_refs], axis=0)
        delta_r = _row_layout(_rowsum(do_ref[...] * o_ref[...]))[0:1, :]
        do = do_ref[...].astype(BF16)
        lse_r = jnp.concatenate([lse_ref[0, sb][0:1, :] for sb in range(nsub)], axis=1)
        dq = jnp.zeros((tq, DKP), F32)
        for start, size in chunks:
            rows = pl.ds(start, size)
            kc = k_ref[0, rows, :]
            p_t = jnp.exp2(_nt(kc, qb) - lse_r)
            ds_t = (p_t * (_nt(v_ref[0, rows, :], do) - delta_r)).astype(BF16)
            dv_ref[0, rows, :] += _nn(p_t.astype(BF16), do)
            dk_ref[0, rows, :] += _nn(ds_t, qb)
            dq += _tn(ds_t, kc)
        dq_ref[0] = dq * SCALE

    kvspec = lambda w: pl.BlockSpec((1, t_all, w), lambda h, i: (h, 0, 0))
    rowspec = pl.BlockSpec((1, nsub, 8, TB), lambda h, i: (h, i, 0, 0))
    qspec = lambda sb: pl.BlockSpec((1, TB, DKP), lambda h, i: (h, i * nsub + sb + off, 0))
    return pl.pallas_call(
        body, name="attn_bwd", grid=(NH, nq // nsub),
        in_specs=[qspec(sb) for sb in range(nsub)]
        + [kvspec(DKP), kvspec(DV), pl.BlockSpec((tq, DV), lambda h, i: (i, h)), pl.BlockSpec((tq, DV), lambda h, i: (i, h)),
           rowspec],
        out_specs=[pl.BlockSpec((1, tq, DKP), lambda h, i: (h, i, 0)), kvspec(DKP), kvspec(DV)],
        out_shape=[jax.ShapeDtypeStruct((NH, s_len, DKP), F32), jax.ShapeDtypeStruct((NH, t_all, DKP), F32),
                   jax.ShapeDtypeStruct((NH, t_all, DV), F32)],
        compiler_params=_params(("arbitrary", "arbitrary")),
    )(*([q] * nsub), k, v, dattn, attn, lse)


def _qkv_bwd(u, dq, dk, dv, cos, sin, q_lora_g, w_uq_t, kv_lora_g, w_ukv, qn_g, kn_g, s_len):
    t_all = u.shape[0]
    off = (t_all - s_len) // TB
    nb = t_all // TB

    def body(ulo_ref, dq_ref, dk_ref, dv_ref, cos_ref, sin_ref, qlg_ref, wuq_ref, kvlg_ref, wukv_ref, qng_ref, kng_ref,
             dlo_ref, dwuq_ref, dwukv_ref, dqlg_ref, dkvlg_ref, dqng_ref, dkng_ref):
        i = pl.program_id(0)

        @pl.when(i == 0)
        def _():
            for r in (dwuq_ref, dwukv_ref, dqlg_ref, dkvlg_ref, dqng_ref, dkng_ref):
                r[...] = jnp.zeros_like(r)

        latent = i >= off
        ulo = ulo_ref[...]
        cos, sin = _rope_block(cos_ref, sin_ref, pl.program_id(0) < off)
        cq = ulo[:, 0:QL]
        rc = lax.rsqrt(jnp.mean(cq * cq, axis=-1, keepdims=True) + EPS)
        cqh = cq * rc
        qlg = qlg_ref[...]
        cqn_b = (cqh * qlg).astype(BF16)
        qng = qng_ref[...]
        ckv = ulo[:, QL:QL + KVL]
        r0 = lax.rsqrt(jnp.mean(ckv * ckv, axis=-1, keepdims=True) + EPS)
        ckvh = ckv * r0
        kvlg = kvlg_ref[...]
        ckvn_b = (ckvh * kvlg).astype(BF16)
        qhs = [_nt(cqn_b, wuq_ref[hd]) for hd in range(NH)]
        kns = [_nn(ckvn_b, wukv_ref[hd])[:, :128] for hd in range(NH)]
        dqng = jnp.zeros((1, DKP), F32)
        dqraws = []
        for hd in range(NH):
            qh = qhs[hd]
            rq = lax.rsqrt(_rowsum(qh * qh) / DK + EPS)
            xh = qh * rq
            dqh = jnp.where(latent, dq_ref[hd], 0.0)
            dyq = jnp.concatenate([dqh[:, :128], _rope_t(dqh[:, 128:], cos, sin)], axis=1)
            dqng += _colsum(dyq * xh)
            dxh = dyq * qng
            dqraws.append((rq * (dxh - xh * (_rowsum(dxh * xh) / DK))).astype(BF16))
        dqng_ref[...] += dqng

        kr = ulo[:, 384:512]
        skr = _rowsum(kr * kr)
        kng = kng_ref[...]
        dkr = jnp.zeros((TB, 128), F32)
        dkng = jnp.zeros((1, DKP), F32)
        dkvs = []
        for hd in range(NH):
            kn = kns[hd]
            rk = lax.rsqrt((_rowsum(kn * kn) + skr) / DK + EPS)
            xh1 = kn * rk
            xh2 = kr * rk
            dkh = dk_ref[hd] * LN2
            d1 = dkh[:, :128]
            d2 = _rope_t(dkh[:, 128:], cos, sin)
            dkng += jnp.concatenate([_colsum(d1 * xh1), _colsum(d2 * xh2)], axis=1)
            dx1 = d1 * kng[:, :128]
            dx2 = d2 * kng[:, 128:]
            dot = (_rowsum(dx1 * xh1) + _rowsum(dx2 * xh2)) / DK
            dkvs.append(jnp.concatenate([rk * (dx1 - xh1 * dot), dv_ref[hd]], axis=1).astype(BF16))
            dkr += rk * (dx2 - xh2 * dot)
        dkng_ref[...] += dkng

        dcqn = jnp.zeros((TB, QL), F32)
        dckvn = jnp.zeros((TB, KVL), F32)
        for hd in range(NH):
            dwuq_ref[hd] += _tn(dqraws[hd], cqn_b)[:DK]
            dcqn += _nn(dqraws[hd], wuq_ref[hd])
            dwukv_ref[hd] += _tn(ckvn_b, dkvs[hd])
            dckvn += _nt(dkvs[hd], wukv_ref[hd])
        dqlg_ref[...] += _colsum(dcqn * cqh)
        dxh = dcqn * qlg
        dcq = rc * (dxh - cqh * jnp.mean(dxh * cqh, axis=-1, keepdims=True))
        dkvlg_ref[...] += _colsum(dckvn * ckvh)
        dxh = dckvn * kvlg
        dckv = r0 * (dxh - ckvh * jnp.mean(dxh * ckvh, axis=-1, keepdims=True))
        dlo_ref[...] = jnp.concatenate([dcq, dckv, dkr], axis=1).astype(BF16)

    row = lambda w: pl.BlockSpec((TB, w), lambda i: (i, 0))
    heads = lambda w: pl.BlockSpec((NH, TB, w), lambda i: (0, i, 0))
    return pl.pallas_call(
        body, name="qkv_bwd", grid=(nb,),
        in_specs=[row(512), pl.BlockSpec((NH, TB, DKP), lambda i: (0, jnp.maximum(i - off, 0), 0)), heads(DKP), heads(DV),
                  pl.BlockSpec((1, 8, 256), lambda i: (jnp.maximum(i - off, 0), 0, 0)), _full((TB, 256)), _full((1, QL)), _full((NH, DKP, QL)), _full((1, KVL)), _full((NH, KVL, 256)),
                  _full((1, DKP)), _full((1, DKP))],
        out_specs=[row(512), _full((NH, DK, QL)), _full((NH, KVL, 256)), _full((1, QL)), _full((1, KVL)),
                   _full((1, DKP)), _full((1, DKP))],
        out_shape=[jax.ShapeDtypeStruct((t_all, 512), BF16), jax.ShapeDtypeStruct((NH, DK, QL), F32),
                   jax.ShapeDtypeStruct((NH, KVL, 256), F32), jax.ShapeDtypeStruct((1, QL), F32),
                   jax.ShapeDtypeStruct((1, KVL), F32), jax.ShapeDtypeStruct((1, DKP), F32), jax.ShapeDtypeStruct((1, DKP), F32)],
        compiler_params=_params(("arbitrary",)),
    )(u, dq, dk, dv, cos, sin, q_lora_g, w_uq_t, kv_lora_g, w_ukv, qn_g, kn_g)


def _in_bwd(ctx, x, modsel, norm_g, dlo, dga, dgp, dpool, dxn, w_in_t):
    s_len, lc = x.shape[0], ctx.shape[0]
    t_all = s_len + lc
    off = lc // TB
    nb = t_all // TB
    nq = s_len // TB
    hb = TB // HALO
    n = TB + 2 * HALO

    def body(ctx_ref, x_ref, mod_ref, ng_ref, dlo_ref, dga_ref, dgp_ref, dp_ref, dpprev_ref, dpnext_ref, dxn_ref, win_ref,
             gx_ref, dwin_ref, dmod_ref, dng_ref):
        i = pl.program_id(0)
        j = i - off

        @pl.when(i == 0)
        def _():
            dwin_ref[...] = jnp.zeros_like(dwin_ref)
            dmod_ref[...] = jnp.zeros_like(dmod_ref)
            dng_ref[...] = jnp.zeros_like(dng_ref)

        latent = i >= off
        dp = dp_ref[...]
        prev = jnp.where(j <= 0, 0.0, dpprev_ref[...])
        nxt = jnp.where(j >= nq - 1, 0.0, dpnext_ref[...])
        win = jnp.concatenate([prev, dp, nxt], axis=0)
        tg = j * TB - HALO + lax.broadcasted_iota(jnp.int32, (n, 1), 0)
        dpin = []
        for g, w in enumerate(POOL_WINDOWS):
            cnt = jnp.maximum(jnp.minimum(tg + w // 2, s_len) - jnp.maximum(tg - w // 2, 0), 1).astype(F32)
            zq = win[:, g * 128:(g + 1) * 128] / cnt
            zq = zq + _shift_rows(zq, 1)
            for step in (1, 2, 4):
                if w >= 4 * step:
                    zq = _shift_rows(zq, -step) + _shift_rows(zq, step)
            dpin.append(zq[HALO:HALO + TB] - dp[:, g * 128:(g + 1) * 128])
        zero = jnp.zeros((TB, 512), BF16)
        du = [dlo_ref[...], jnp.where(latent, dga_ref[...], zero),
              jnp.where(latent, jnp.concatenate(dpin, axis=1).astype(BF16), zero), jnp.where(latent, dgp_ref[...], zero)]

        ng = ng_ref[...]
        xb = jnp.where(i < off, ctx_ref[...], x_ref[...])
        r, xh, xg, h, scale = _modulated(xb, mod_ref, ng)
        hb_ = h.astype(BF16)
        dh = jnp.zeros((TB, D), F32)
        for s, (lo, hi) in enumerate(SEG):
            dwin_ref[lo:hi, :] += _tn(du[s], hb_)
            dh += _nn(du[s], win_ref[lo:hi, :])
        is_lat = latent.astype(F32)
        dsh = _colsum(dh)
        dsc = _colsum(dh * xg)
        dmod_ref[0, 0:1, :] += dsh * (1.0 - is_lat)
        dmod_ref[0, 1:2, :] += dsc * (1.0 - is_lat)
        dmod_ref[1, 0:1, :] += dsh * is_lat
        dmod_ref[1, 1:2, :] += dsc * is_lat
        dxg = dh * (1.0 + scale)
        dng_ref[...] += _colsum(dxg * xh)
        dxh = dxg * ng
        gx_ref[...] = r * (dxh - xh * jnp.mean(dxh * xh, axis=-1, keepdims=True)) + dxn_ref[...]

    row = lambda w: pl.BlockSpec((TB, w), lambda i: (i, 0))
    lat = lambda w: pl.BlockSpec((TB, w), lambda i: (jnp.maximum(i - off, 0), 0))
    last8 = s_len // HALO - 1
    cspec, xspec, mspec = _token_specs(off)
    return pl.pallas_call(
        body, name="in_bwd", grid=(nb,),
        in_specs=[cspec, xspec, mspec, _full((1, D)), row(512), lat(512), lat(512), lat(512),
                  pl.BlockSpec((HALO, 512), lambda i: (jnp.maximum(jnp.maximum(i - off, 0) * hb - 1, 0), 0)),
                  pl.BlockSpec((HALO, 512), lambda i: (jnp.minimum((jnp.maximum(i - off, 0) + 1) * hb, last8), 0)),
                  lat(D), _full((DIN, D))],
        out_specs=[lat(D), _full((DIN, D)), _full((2, 2, D)), _full((1, D))],
        out_shape=[jax.ShapeDtypeStruct((s_len, D), F32), jax.ShapeDtypeStruct((DIN, D), F32),
                   jax.ShapeDtypeStruct((2, 2, D), F32), jax.ShapeDtypeStruct((1, D), F32)],
        compiler_params=_params(("arbitrary",)),
    )(ctx, x, modsel, norm_g, dlo, dga, dgp, dpool, dpool, dpool, dxn, w_in_t)


def _adamw_update(w_ref, g_ref, m_ref, v_ref, d_ref, mo_ref, vo_ref):
    gv = g_ref[...]
    mn = ADAM_B1 * m_ref[...] + (1.0 - ADAM_B1) * gv
    vn = ADAM_B2 * v_ref[...] + (1.0 - ADAM_B2) * (gv * gv)
    m_hat = mn / (1.0 - ADAM_B1 ** ADAM_STEP)
    v_hat = vn / (1.0 - ADAM_B2 ** ADAM_STEP)
    d_ref[...] = -ADAM_LR * (m_hat / (jnp.sqrt(v_hat) + ADAM_EPS) + ADAM_WD * w_ref[...])
    mo_ref[...] = mn
    vo_ref[...] = vn


def _adamw_many(ws, gs, ms, vs):
    n = len(ws)
    parts = 4

    def body(*refs):
        for i in range(n):
            _adamw_update(refs[i], refs[n + i], refs[2 * n + i], refs[3 * n + i], refs[4 * n + i], refs[5 * n + i], refs[6 * n + i])
            refs[7 * n + i][...] = refs[n + i][...]

    def spec(w):
        rows, cols = w.shape
        if rows % (8 * parts) == 0:
            return pl.BlockSpec((rows // parts, cols), lambda i: (i, 0))
        if cols % (128 * parts) == 0:
            return pl.BlockSpec((rows, cols // parts), lambda i: (0, i))
        return _full((rows, cols))

    specs = [spec(w) for w in ws]
    shp = [jax.ShapeDtypeStruct(w.shape, F32) for w in ws]
    out = pl.pallas_call(body, name="adamw_many", grid=(parts,), in_specs=specs * 4, out_specs=specs * 4, out_shape=shp * 4,
                         compiler_params=_params(("arbitrary",)))(*ws, *gs, *ms, *vs)
    return out[:n], out[n:2 * n], out[2 * n:3 * n], out[3 * n:]


class _Links:
    def __init__(self, send_sems, recv_sems):
        self.send_sems, self.recv_sems, self.sends = send_sems, recv_sems, []

    def send(self, src, dst, sem, to):
        cp = pltpu.make_async_remote_copy(src, dst, self.send_sems.at[sem], self.recv_sems.at[sem], device_id=to,
                                          device_id_type=MESH)
        cp.start()
        self.sends.append(cp)

    def arrived(self, dst, sem, frm):
        pltpu.make_async_remote_copy(dst, dst, self.send_sems.at[sem], self.recv_sems.at[sem], device_id=frm,
                                     device_id_type=MESH).wait_recv()

    def drain(self):
        for cp in self.sends:
            cp.wait_send()


def _half(ref, c, axis):
    size = ref.shape[axis - 2] // 2
    win = pl.ds(pl.multiple_of(c * size, 16 if axis == 0 else 128), size)
    idx = (win, slice(None)) if axis == 0 else (slice(None), win)
    return ref.at[(slice(None),) * (len(ref.shape) - 2) + idx]


def _select_rows(slots_ref, n_slots, row=0):
    sub = lax.broadcasted_iota(jnp.int32, (8, 1), 0)
    out = None
    for d in range(n_slots):
        r = jnp.where(sub == d, jnp.broadcast_to(slots_ref[d][row:row + 1, :], (8, slots_ref.shape[-1])), 0.0)
        out = r if out is None else out + r
    return out


def _gather(c, c_ctx, w_mod, b_mod_k, shards, axes, slab_rows):
    nw = len(shards)
    kw = w_mod.shape[1]

    def body(*refs):
        c_ref, cc_ref, wm_hbm, b_ref = refs[:4]
        w_hbm = refs[4:4 + nw]
        a16_ref, mod_ref = refs[4 + nw:6 + nw]
        out_refs = refs[6 + nw:6 + 2 * nw]
        g_refs = refs[6 + 2 * nw:6 + 3 * nw]
        f_refs = refs[6 + 3 * nw:6 + 4 * nw]
        wm_ref, a_ref, send_sems, recv_sems, local_sems = refs[6 + 4 * nw:]
        loads = [pltpu.make_async_copy(w_hbm[wi], f_refs[wi], local_sems.at[wi]) for wi in range(nw)]
        loads.append(pltpu.make_async_copy(wm_hbm, wm_ref, local_sems.at[nw]))
        for cp in loads:
            cp.start()
        stores = []

        def slab(wi, chip, of=g_refs):
            return of[wi].at[chip].at[0:shards[wi].shape[0]]

        def store(src, dst, wi, slot):
            cp = pltpu.make_async_copy(src, dst, local_sems.at[nw + 1 + wi * 8 + slot])
            cp.start()
            stores.append(cp)

        def store_half(wi, chip, half, slot):
            store(_half(slab(wi, chip), half, axes[wi]), _half(slab(wi, chip, out_refs), half, axes[wi]), wi, slot)
        x, y, cc = lax.axis_index("x"), lax.axis_index("y"), lax.axis_index("c")
        me = 4 * x + 2 * y + cc
        k = 2 * x + y
        sibling = (x, y, 1 - cc)
        links = _Links(send_sems, recv_sems)
        chips = [_peer(x, y, cc, off + (0,)) for off in CHIPS3]
        chip_a = ((x + 1 - cc) % 2, (y + cc) % 2, cc)
        chip_b = ((x + cc) % 2, (y + 1 - cc) % 2, cc)
        chip_d = (1 - x, 1 - y, cc)
        cv = c_ref[...]
        sc = cv * _sig(cv)
        mine = a_ref.at[me]
        for r in range(8):
            mine[r:r + 1, :] = sc[:, r * 128:(r + 1) * 128]
        for j, off in enumerate(PEERS7):
            links.send(a_ref.at[me], a_ref.at[me], j, _peer(x, y, cc, off))
        for wi in range(nw):
            loads[wi].wait()
            slab(wi, k)[...] = f_refs[wi][...].astype(BF16)
            for j, to in enumerate((chip_a, chip_b)):
                links.send(_half(slab(wi, k), cc, axes[wi]), _half(slab(wi, k), cc, axes[wi]), 10 + wi * 6 + j, to)
            store(slab(wi, k), slab(wi, k, out_refs), wi, 0)
            rows = shards[wi].shape[0]
            pad = slab_rows[wi] - rows
            if pad:
                for kk in range(4):
                    g_refs[wi][kk, rows:, :] = jnp.zeros((pad, shards[wi].shape[1]), BF16)
                store(g_refs[wi].at[:, pl.ds(rows, pad), :], out_refs[wi].at[:, pl.ds(rows, pad), :], wi, 7)
        for j, off in enumerate(PEERS7):
            px, py, pc = _peer(x, y, cc, off)
            links.arrived(a_ref.at[4 * px + 2 * py + pc], j, (px, py, pc))
        ccv = cc_ref[...]
        sub = lax.broadcasted_iota(jnp.int32, (8, 1), 0)
        top = jnp.zeros((8, D), F32)
        for d in range(8):
            blk = a_ref[d]
            row = jnp.concatenate([blk[r:r + 1, :] for r in range(8)], axis=1)
            top = top + jnp.where(sub == d, jnp.broadcast_to(row, (8, D)), 0.0)
        a16 = jnp.concatenate([top, jnp.where(sub == 0, jnp.broadcast_to(ccv * _sig(ccv), (8, D)), 0.0)], axis=0)
        a16_ref[...] = a16
        loads[nw].wait()
        mod_ref[k] = _dot3(_nn, a16, wm_ref[...]) + b_ref[...]
        for j, to in enumerate(chips):
            links.send(mod_ref.at[k], mod_ref.at[k], 7 + j, to)

        def over_ici(j, frm, origin):
            for wi in range(nw):
                blk = _half(slab(wi, 2 * origin[0] + origin[1]), cc, axes[wi])
                links.arrived(blk, 10 + wi * 6 + j, frm)
                if j == 0:
                    links.send(blk, blk, 10 + wi * 6 + 2, chip_b)
                links.send(blk, blk, 10 + wi * 6 + 3 + j, sibling)
                store_half(wi, 2 * origin[0] + origin[1], cc, 1 + j)

        def from_sibling(j, origin):
            for wi in range(nw):
                links.arrived(_half(slab(wi, 2 * origin[0] + origin[1]), 1 - cc, axes[wi]), 10 + wi * 6 + 3 + j, sibling)
                store_half(wi, 2 * origin[0] + origin[1], 1 - cc, 4 + j)
        over_ici(0, chip_a, chip_a)
        over_ici(1, chip_b, chip_b)
        from_sibling(0, chip_b)
        over_ici(2, chip_b, chip_d)
        from_sibling(1, chip_a)
        from_sibling(2, chip_d)
        for j, (px, py, pc) in enumerate(chips):
            links.arrived(mod_ref.at[2 * px + py], 7 + j, (px, py, pc))
        links.drain()
        for cp in stores:
            cp.wait()

    nsem = 10 + 6 * nw
    gathered = [(4, r, s.shape[1]) for r, s in zip(slab_rows, shards)]
    return pl.pallas_call(
        body, name="gather", in_specs=[VM, VM, ANY, VM] + [ANY] * nw, out_specs=[VM, VM] + [ANY] * nw,
        out_shape=[jax.ShapeDtypeStruct((16, D), F32), jax.ShapeDtypeStruct((4, 16, kw), F32)]
        + [jax.ShapeDtypeStruct(g, BF16) for g in gathered],
        scratch_shapes=[pltpu.VMEM(g, BF16) for g in gathered] + [pltpu.VMEM(s.shape, F32) for s in shards]
        + [pltpu.VMEM(w_mod.shape, F32), pltpu.VMEM((8, 8, D // 8), F32), pltpu.SemaphoreType.DMA((nsem,)),
           pltpu.SemaphoreType.DMA((nsem,)), pltpu.SemaphoreType.DMA((nw + 1 + 8 * nw,))],
        compiler_params=pltpu.CompilerParams(vmem_limit_bytes=VMEM_LIMIT),
    )(c, c_ctx, w_mod, b_mod_k, *shards)


SMALL_ROW_WIDTHS = (D, QL, KVL, DKP, DKP, 512, 128)
SMALL_OUT_WIDTHS = (D, QL, KVL, DK, DK, 512, 1)
SMALL_PACK = 384


def _small_pieces():
    pieces = []
    for i, w in enumerate(SMALL_ROW_WIDTHS):
        for c0 in range(0, w, 128):
            j = len(pieces)
            pieces.append((i, c0, j // (SMALL_PACK // 128), j % (SMALL_PACK // 128) * 128))
    assert len(pieces) <= 8 * (SMALL_PACK // 128)
    return pieces


def _reduce(grads, axes, smalls, w_pool_g, dmod8, a16, w_mod, c_ctx):
    nw = len(grads)
    ns = len(smalls)
    kw = w_mod.shape[1]
    halves = []
    for g, ax in zip(grads, axes):
        halves.append((g.shape[1] // 2, g.shape[2]) if ax == 0 else (g.shape[1], g.shape[2] // 2))

    def body(*refs):
        g_refs = refs[:nw]
        small_refs = refs[nw:nw + ns]
        wp_ref, dm_ref, a16_ref, wm_hbm, cc_ref = refs[nw + ns:nw + ns + 5]
        o = nw + ns + 5
        r_outs = refs[o:o + nw]
        small_outs = refs[o + nw:o + nw + ns]
        rwp_ref, gw_out, gb_ref, gc_ref = refs[o + nw + ns:o + nw + ns + 4]
        o = o + nw + ns + 4
        own, sib, part, got, rel, r_refs = (refs[o + i * nw:o + (i + 1) * nw] for i in range(6))
        smbuf, wps, wpg, dm_all, pc_all, wm_ref, gw_ref, send_sems, recv_sems, local_sems = refs[o + 6 * nw:]
        wm_load = pltpu.make_async_copy(wm_hbm, wm_ref, local_sems.at[nw])
        wm_load.start()
        stores = []

        def store(src, dst, sem):
            cp = pltpu.make_async_copy(src, dst, local_sems.at[nw + 1 + sem])
            cp.start()
            stores.append(cp)
        x, y, cc = lax.axis_index("x"), lax.axis_index("y"), lax.axis_index("c")
        me = 4 * x + 2 * y + cc
        k = 2 * x + y
        sibling = (x, y, 1 - cc)
        links = _Links(send_sems, recv_sems)
        chips = [_peer(x, y, cc, off + (0,)) for off in CHIPS3]
        peers = [_peer(x, y, cc, off) for off in PEERS7]
        chip_a = ((x + 1 - cc) % 2, (y + cc) % 2, cc)
        chip_b = ((x + cc) % 2, (y + 1 - cc) % 2, cc)
        ka, kb, kd = 2 * chip_a[0] + chip_a[1], 2 * chip_b[0] + chip_b[1], 2 * (1 - x) + (1 - y)
        big, sm0, wp0, dm0, pc0 = 0, 5 * nw, 5 * nw + 7, 5 * nw + 14, 5 * nw + 21

        locals_ = []
        for wi in range(nw):
            lc = pltpu.make_async_copy(_half(g_refs[wi], cc, axes[wi]), own[wi], local_sems.at[wi])
            lc.start()
            locals_.append(lc)
            links.send(_half(g_refs[wi], 1 - cc, axes[wi]), sib[wi], big + wi * 5, sibling)
        slot = smbuf.at[me]
        slot[...] = jnp.zeros((8, SMALL_PACK), F32)
        small_rows = [jnp.broadcast_to(ref[...], (1, w)) for ref, w in zip(small_refs, SMALL_ROW_WIDTHS)]
        for i, c0, row, lane in _small_pieces():
            slot[row:row + 1, lane:lane + 128] = small_rows[i][:, c0:c0 + 128]
        links.send(wp_ref, wps, wp0, sibling)
        dm_all[me] = dm_ref[...]
        for j, peer in enumerate(peers):
            links.send(dm_all.at[me], dm_all.at[me], dm0 + j, peer)
            links.send(smbuf.at[me], smbuf.at[me], sm0 + j, peer)
        links.arrived(wps, wp0, sibling)
        wpg[k] = (wp_ref[...] + wps[...]).astype(BF16)
        for j, to in enumerate(chips):
            links.send(wpg.at[k], wpg.at[k], wp0 + 1 + j, to)
        for wi in range(nw):
            locals_[wi].wait()
            links.arrived(sib[wi], big + wi * 5, sibling)
            part[wi][...] = (own[wi][...] + sib[wi][...]).astype(BF16)
            got[wi][k] = part[wi][k]
            got[wi][kd] = jnp.zeros(halves[wi], BF16)
            links.send(part[wi].at[kd], rel[wi], big + wi * 5 + 1, chip_b)
            links.send(part[wi].at[kb], got[wi].at[k], big + wi * 5 + 2, chip_b)
        for j, (px, py, pc) in enumerate(peers):
            links.arrived(dm_all.at[4 * px + 2 * py + pc], dm0 + j, (px, py, pc))
        dm_tot = dm_all[0]
        for d in range(1, 8):
            dm_tot = dm_tot + dm_all[d]
        for kk in range(4):
            gb_ref[:, kk * kw:(kk + 1) * kw] = dm_tot[kk:kk + 1, :] + dm_tot[4 + kk:5 + kk, :]
        top = jnp.zeros((8, kw), F32)
        dmc_k = jnp.zeros((1, kw), F32)
        for kk in range(4):
            top = top + jnp.where(k == kk, _select_rows(dm_all, 8, kk), 0.0)
            dmc_k = dmc_k + jnp.where(k == kk, dm_tot[4 + kk:5 + kk, :], 0.0)
        sub = lax.broadcasted_iota(jnp.int32, (8, 1), 0)
        bk = jnp.concatenate([top, jnp.where(sub == 0, jnp.broadcast_to(dmc_k, (8, kw)), 0.0)], axis=0)
        gw_ref[...] = _dot3(_tn, a16_ref[...], bk)
        store(gw_ref, gw_out, 0)
        wm_load.wait()
        pc_all[k] =_dot3(_nt, jnp.broadcast_to(bk[8:9, :], (8, kw)), wm_ref[...])
        for j, to in enumerate(chips):
            links.send(pc_all.at[k], pc_all.at[k], pc0 + j, to)
        for j, (px, py, pc) in enumerate(peers):
            links.arrived(smbuf.at[4 * px + 2 * py + pc], sm0 + j, (px, py, pc))
        tot = smbuf[0]
        for d in range(1, 8):
            tot = tot + smbuf[d]
        for i, c0, row, lane in _small_pieces():
            n = min(128, SMALL_OUT_WIDTHS[i] - c0)
            if n > 0:
                small_outs[i][:, c0:c0 + n] = tot[row:row + 1, lane:lane + n]
        for j, (px, py, pc) in enumerate(chips):
            links.arrived(wpg.at[2 * px + py], wp0 + 1 + j, (px, py, pc))
        wpt = wpg[0].astype(F32)
        for kk in range(1, 4):
            wpt = wpt + wpg[kk].astype(F32)
        rwp_ref[...] = wpt
        for wi in range(nw):
            links.arrived(rel[wi], big + wi * 5 + 1, chip_b)
            rel[wi][...] = (part[wi][ka].astype(F32) + rel[wi][...].astype(F32)).astype(BF16)
            links.send(rel[wi], got[wi].at[k], big + wi * 5 + 3, chip_a)
        for wi in range(nw):
            links.arrived(got[wi].at[kb], big + wi * 5 + 2, chip_b)
            links.arrived(got[wi].at[ka], big + wi * 5 + 3, chip_a)
            total = got[wi][0].astype(F32)
            for kk in range(1, 4):
                total = total + got[wi][kk].astype(F32)
            mine = _half(r_refs[wi], cc, axes[wi])
            mine[...] = total
            links.send(mine, mine, big + wi * 5 + 4, sibling)
            store(mine, _half(r_outs[wi], cc, axes[wi]), 1 + wi)
        for j, (px, py, pc) in enumerate(chips):
            links.arrived(pc_all.at[2 * px + py], pc0 + j, (px, py, pc))
        ccv = cc_ref[...]
        sg = _sig(ccv)
        gc_ref[...] = (pc_all[0][0:1, :] + pc_all[1][0:1, :] + pc_all[2][0:1, :] + pc_all[3][0:1, :]) * (sg * (1.0 + ccv * (1.0 - sg)))
        for wi in range(nw):
            links.arrived(_half(r_refs[wi], 1 - cc, axes[wi]), big + wi * 5 + 4, sibling)
            store(_half(r_refs[wi], 1 - cc, axes[wi]), _half(r_outs[wi], 1 - cc, axes[wi]), 1 + nw + wi)
        links.drain()
        for cp in stores:
            cp.wait()

    nsem = 5 * nw + 24
    quads = [(4,) + h for h in halves]
    return pl.pallas_call(
        body, name="reduce", in_specs=[ANY] * nw + [VM] * (ns + 3) + [ANY, VM],
        out_specs=[ANY] * nw + [VM] * (ns + 1) + [ANY, VM, VM],
        out_shape=[jax.ShapeDtypeStruct(g.shape[1:], F32) for g in grads]
        + [jax.ShapeDtypeStruct((1, w), F32) for w in SMALL_OUT_WIDTHS]
        + [jax.ShapeDtypeStruct(w_pool_g.shape, F32), jax.ShapeDtypeStruct((D, kw), F32), jax.ShapeDtypeStruct((1, 3 * D), F32),
           jax.ShapeDtypeStruct((1, D), F32)],
        scratch_shapes=[pltpu.VMEM(q, F32) for q in quads] + [pltpu.VMEM(q, F32) for q in quads]
        + [pltpu.VMEM(q, BF16) for q in quads] + [pltpu.VMEM(q, BF16) for q in quads] + [pltpu.VMEM(h, BF16) for h in halves]
        + [pltpu.VMEM(g.shape[1:], F32) for g in grads]
        + [pltpu.VMEM((8, 8, SMALL_PACK), F32), pltpu.VMEM(w_pool_g.shape, F32), pltpu.VMEM((4,) + w_pool_g.shape, BF16),
           pltpu.VMEM((8, 8, kw), F32), pltpu.VMEM((4, 8, D), F32), pltpu.VMEM((D, kw), F32), pltpu.VMEM((D, kw), F32)]
        + [pltpu.SemaphoreType.DMA((nsem,)), pltpu.SemaphoreType.DMA((nsem,)), pltpu.SemaphoreType.DMA((3 * nw + 2,))],
        compiler_params=pltpu.CompilerParams(vmem_limit_bytes=VMEM_LIMIT),
    )(*grads, *smalls, w_pool_g, dmod8, a16, w_mod, c_ctx)


def _rope_tables(s_len):
    rows = s_len // GRID_W
    per = TB // GRID_W
    n_freq = 16
    f32 = np.float32
    inv = f32(ROPE_BASE) ** (-np.arange(n_freq, dtype=f32) / f32(n_freq))
    ang_r = np.arange(rows, dtype=f32)[:, None] * inv
    ang_c = np.arange(GRID_W, dtype=f32)[:, None] * inv
    by_row, by_col = [], []
    for fn, pad in ((np.cos, 1.0), (np.sin, 0.0)):
        r = np.concatenate([fn(ang_r), fn(ang_r), np.zeros((rows, 96), f32)], axis=1).reshape(rows // per, per, 128)
        by_row.append(np.pad(r, ((0, 0), (0, 8 - per), (0, 0))))
        cpart = np.concatenate([np.zeros((GRID_W, 32), f32), fn(ang_c), fn(ang_c), np.full((GRID_W, 64), pad, f32)], axis=1)
        by_col.append(np.tile(cpart, (per, 1)))
    return jnp.asarray(np.concatenate(by_row, axis=-1), F32), jnp.asarray(np.concatenate(by_col, axis=-1), F32)


def kernel(x, c, ctx, c_ctx, w_mod, b_mod, norm_g, w_in, q_lora_g, w_uq, kv_lora_g, w_ukv, q_norm_g, k_norm_g, w_pool, pool_scale, w_out, loss_target, m_c_ctx, m_w_mod, m_b_mod, m_norm_g, m_w_in, m_q_lora_g, m_w_uq, m_kv_lora_g, m_w_ukv, m_q_norm_g, m_k_norm_g, m_w_pool, m_pool_scale, m_w_out, v_c_ctx, v_w_mod, v_b_mod, v_norm_g, v_w_in, v_q_lora_g, v_w_uq, v_kv_lora_g, v_w_ukv, v_q_norm_g, v_k_norm_g, v_w_pool, v_pool_scale, v_w_out):
    xi, yi, ci = lax.axis_index("x"), lax.axis_index("y"), lax.axis_index("c")
    me = 4 * xi + 2 * yi + ci
    k = 2 * xi + yi
    s_len = x.shape[1]
    lc = ctx.shape[1]
    kw = w_mod.shape[2]
    weights = dict(c_ctx=c_ctx, w_mod=w_mod, b_mod=b_mod, norm_g=norm_g, w_in=w_in, q_lora_g=q_lora_g, w_uq=w_uq,
                   kv_lora_g=kv_lora_g, w_ukv=w_ukv, q_norm_g=q_norm_g, k_norm_g=k_norm_g, w_pool=w_pool,
                   pool_scale=pool_scale, w_out=w_out)
    m_in = dict(c_ctx=m_c_ctx, w_mod=m_w_mod, b_mod=m_b_mod, norm_g=m_norm_g, w_in=m_w_in, q_lora_g=m_q_lora_g, w_uq=m_w_uq,
                kv_lora_g=m_kv_lora_g, w_ukv=m_w_ukv, q_norm_g=m_q_norm_g, k_norm_g=m_k_norm_g, w_pool=m_w_pool,
                pool_scale=m_pool_scale, w_out=m_w_out)
    v_in = dict(c_ctx=v_c_ctx, w_mod=v_w_mod, b_mod=v_b_mod, norm_g=v_norm_g, w_in=v_w_in, q_lora_g=v_q_lora_g, w_uq=v_w_uq,
                kv_lora_g=v_kv_lora_g, w_ukv=v_w_ukv, q_norm_g=v_q_norm_g, k_norm_g=v_k_norm_g, w_pool=v_w_pool,
                pool_scale=v_pool_scale, w_out=v_w_out)
    order = ["c_ctx", "w_mod", "b_mod", "norm_g", "w_in", "q_lora_g", "w_uq", "kv_lora_g", "w_ukv", "q_norm_g", "k_norm_g",
             "w_pool", "pool_scale", "w_out"]
    transposed = ("w_in", "w_uq")
    as2d = lambda n, a: jnp.transpose(a[0]) if n in transposed else a.reshape(-1, a.shape[-1])
    back = lambda n, a: jnp.transpose(a)[None] if n in transposed else a.reshape(weights[n].shape)

    c_ctx2 = c_ctx.reshape(1, D)
    b_mod_k = lax.dynamic_slice(b_mod, (0, k * kw), (1, kw))
    split = (0, 1, 0, 0)
    a16, mod_all, g_out, g_in, g_uq, g_ukv = _gather(
        c, c_ctx2, w_mod[0], b_mod_k, [w_out[0], as2d("w_in", w_in), as2d("w_uq", w_uq), w_ukv[0]], split,
        (D // 4, DIN // 4, DKP, KVL))
    mod_me = lax.dynamic_index_in_dim(mod_all, me, axis=1, keepdims=False).reshape(3, D)
    mod_c = mod_all[:, 8, :].reshape(3, D)
    modsel = jnp.stack([mod_c, mod_me])
    w_in_t = g_in.reshape(DIN, D)
    w_uq_t = g_uq
    w_out_f = g_out.reshape(D, D)
    qn_g = jnp.pad(q_norm_g, ((0, 0), (0, DKP - DK)))
    kn_g = jnp.pad(k_norm_g, ((0, 0), (0, DKP - DK)))
    w_pool_b = w_pool[0].astype(BF16)
    cos, sin = _rope_tables(s_len)

    u, q, kk, v = _fwd_in(ctx[0], x[0], modsel, norm_g, w_in_t, q_lora_g, w_uq_t, kv_lora_g, g_ukv, qn_g, kn_g, cos, sin)
    attn, lse = _attn_fwd(q, kk, v, s_len)
    (dxn, dattn, dga, dgp, dpool, dw_out, dgate, dps, dw_pool, loss) = _out_stage(
        attn.reshape(s_len // Q_BLOCK, Q_BLOCK, NH * DV), u, x[0], loss_target[0], modsel, w_pool_b, pool_scale,
        w_out_f, lc)
    dattn = dattn.reshape(s_len, NH * DV)
    dq, dk, dv = _attn_bwd(q, kk, v, dattn, attn, lse, s_len)
    dlo, dw_uq_t, dw_ukv, dqlg, dkvlg, dqng, dkng = _qkv_bwd(u, dq, dk, dv, cos, sin, q_lora_g, w_uq_t, kv_lora_g, g_ukv,
                                                            qn_g, kn_g, s_len)
    gx, dw_in_t, dmod, dng = _in_bwd(ctx[0], x[0], modsel, norm_g, dlo, dga, dgp, dpool, dxn, w_in_t)

    dmod_l = jnp.concatenate([dmod[1, 0], dmod[1, 1], dgate[0]]).reshape(4, kw)
    dmod_c = jnp.concatenate([dmod[0, 0], dmod[0, 1], jnp.zeros((D,), F32)]).reshape(4, kw)
    dmod8 = jnp.concatenate([dmod_l, dmod_c], axis=0)
    r_out, r_in, r_uq, r_ukv, g_ng, g_qlg, g_kvlg, g_qng, g_kng, g_ps, loss_all, g_wp, g_w_mod, g_b_mod, g_c_ctx = _reduce(
        [dw_out.reshape(4, D // 4, D), dw_in_t.reshape(4, DIN // 4, D), dw_uq_t, dw_ukv], split,
        [dng, dqlg, dkvlg, dqng, dkng, dps, loss], dw_pool, dmod8, a16, w_mod[0], c_ctx2)
    g2d = dict(c_ctx=g_c_ctx, b_mod=g_b_mod, w_mod=g_w_mod, w_in=r_in, w_uq=r_uq, w_ukv=r_ukv, w_out=r_out, norm_g=g_ng,
               q_lora_g=g_qlg, kv_lora_g=g_kvlg, q_norm_g=g_qng, k_norm_g=g_kng, pool_scale=g_ps, w_pool=g_wp.reshape(512, 128))

    outs = _adamw_many([as2d(n, weights[n]) for n in order], [g2d[n] for n in order], [as2d(n, m_in[n]) for n in order],
                       [as2d(n, v_in[n]) for n in order])
    d2d, m2d, v2d, g2d = (dict(zip(order, arrs)) for arrs in outs)

    return (loss_all[0, 0], gx[None], *[back(n, g2d[n]) for n in order], *[back(n, d2d[n]) for n in order],
            *[back(n, m2d[n]) for n in order], *[back(n, v2d[n]) for n in order])
```

```python
import jax
import jax.numpy as jnp
import numpy as np
from jax import lax
from jax.experimental import pallas as pl
from jax.experimental.pallas import tpu as pltpu

F32 = jnp.float32
BF16 = jnp.bfloat16
MESH = pl.DeviceIdType.MESH

D = 1024
NH = 4
DK = 192
DKP = 256
DV = 128
QL = 256
KVL = 128
DIN = 1984
U_LO = 448
SEG = ((0, 512), (448, 960), (960, 1472), (1472, 1984))
DU = 2048
POOL_WINDOWS = (2, 4, 8, 16)
HALO = 8
EPS = 1e-6
ROPE_BASE = 10000.0
GRID_W = 64
Q_BLOCK = 128
TB = 256
BWD_QBLOCKS = 1
SCALE = DK ** -0.5
LOG2E = 1.4426950408889634
LN2 = 0.6931471805599453
VMEM_LIMIT = 56 * 1024 * 1024

ADAM_LR = 0.001
ADAM_B1 = 0.9
ADAM_B2 = 0.999
ADAM_EPS = 1e-08
ADAM_WD = 0.01
ADAM_STEP = 10

CHIPS3 = ((1, 0), (0, 1), (1, 1))
PEERS7 = tuple((dx, dy, dc) for dx in (0, 1) for dy in (0, 1) for dc in (0, 1) if (dx, dy, dc) != (0, 0, 0))

VM = pl.BlockSpec(memory_space=pltpu.VMEM)
ANY = pl.BlockSpec(memory_space=pl.ANY)


def _nn(a, b):
    return jnp.dot(a, b, preferred_element_type=F32)


def _nt(a, b):
    return lax.dot_general(a, b, (((1,), (1,)), ((), ())), preferred_element_type=F32)


def _tn(a, b):
    return lax.dot_general(a, b, (((0,), (0,)), ((), ())), preferred_element_type=F32)


def _split3(a):
    a0 = a.astype(BF16)
    r = a - a0.astype(F32)
    a1 = r.astype(BF16)
    a2 = (r - a1.astype(F32)).astype(BF16)
    return a0, a1, a2


def _dot3(dot, a, b):
    sa = _split3(a)
    sb = _split3(b)
    out = None
    for i in range(3):
        for j in range(3 - i):
            t = dot(sa[i], sb[j])
            out = t if out is None else out + t
    return out


def _sig(x):
    return 1.0 / (1.0 + jnp.exp(-x))


def _rot(t):
    src = lax.broadcasted_iota(jnp.int32, (128, 128), 0)
    dst = lax.broadcasted_iota(jnp.int32, (128, 128), 1)
    first = (dst % 32) < 16
    perm = jnp.where(first & (src == dst + 16), -1.0, jnp.where(~first & (src == dst - 16), 1.0, 0.0)).astype(BF16)
    hi = t.astype(BF16)
    lo = (t - hi.astype(F32)).astype(BF16)
    return _nn(hi, perm) + _nn(lo, perm)


def _rope(t, cos, sin):
    return t * cos + _rot(t) * sin


def _rope_t(t, cos, sin):
    return t * cos - _rot(t * sin)


def _rope_block(rows_ref, cols_ref, is_ctx):
    lane = lax.broadcasted_iota(jnp.int32, (TB, 256), 1) % 128
    rows = jnp.concatenate([jnp.broadcast_to(rows_ref[0, r:r + 1, :], (GRID_W, 256)) for r in range(TB // GRID_W)], axis=0)
    cs = jnp.where(lane < 32, rows, cols_ref[...])
    return jnp.where(is_ctx, 1.0, cs[:, :128]), jnp.where(is_ctx, 0.0, cs[:, 128:])


def _shift_rows(z, k):
    n = z.shape[0]
    return pltpu.roll(z, (n - k) % n, 0)


def _colsum(a):
    return jnp.sum(a, axis=0, keepdims=True)


def _rowsum(a):
    return jnp.sum(a, axis=-1, keepdims=True)


def _row_layout(col):
    return jnp.transpose(jnp.broadcast_to(col, (col.shape[0], 128)))[0:8, :]


def _params(sem=None):
    return pltpu.CompilerParams(dimension_semantics=sem, vmem_limit_bytes=VMEM_LIMIT)


def _full(shape):
    nd = len(shape)
    return pl.BlockSpec(shape, lambda *_: (0,) * nd)


def _peer(x, y, c, off):
    dx, dy, dc = off
    return ((x + dx) % 2, (y + dy) % 2, (c + dc) % 2)


def _token_specs(off):
    ctx = pl.BlockSpec((TB, D), lambda i: (jnp.minimum(i, off - 1), 0))
    lat = pl.BlockSpec((TB, D), lambda i: (jnp.maximum(i - off, 0), 0))
    mod = pl.BlockSpec((1, 3, D), lambda i: (jnp.minimum(i // off, 1), 0, 0))
    return ctx, lat, mod


def _modulated(x, mod_ref, ng):
    shift = mod_ref[0, 0:1, :]
    scale = mod_ref[0, 1:2, :]
    r = lax.rsqrt(jnp.mean(x * x, axis=-1, keepdims=True) + EPS)
    xh = x * r
    xg = xh * ng
    return r, xh, xg, xg * (1.0 + scale) + shift, scale


def _fwd_in(ctx, x, modsel, norm_g, w_in_t, q_lora_g, w_uq_t, kv_lora_g, w_ukv, qn_g, kn_g, cos, sin):
    s_len, lc = x.shape[0], ctx.shape[0]
    t_all = s_len + lc
    nb = t_all // TB
    off = lc // TB

    def body(ctx_ref, x_ref, mod_ref, ng_ref, win_ref, qlg_ref, wuq_ref, kvlg_ref, wukv_ref, qng_ref, kng_ref, cos_ref, sin_ref,
             u_ref, q_ref, k_ref, v_ref):
        is_ctx = pl.program_id(0) < off
        xb = jnp.where(is_ctx, ctx_ref[...], x_ref[...])
        _, _, _, h, _ = _modulated(xb, mod_ref, ng_ref[...])
        hb = h.astype(BF16)
        lane = lax.broadcasted_iota(jnp.int32, (TB, 512), 1)
        ulo = jnp.where(lane < U_LO, _nt(hb, win_ref[SEG[0][0]:SEG[0][1], :]), 0.0)
        u_ref[:, 0:512] = ulo
        for j in range(1, 4):
            u_ref[:, j * 512:(j + 1) * 512] = _nt(hb, win_ref[SEG[j][0]:SEG[j][1], :])
        cos, sin = _rope_block(cos_ref, sin_ref, is_ctx)
        cq = ulo[:, 0:QL]
        cqn = (cq * lax.rsqrt(jnp.mean(cq * cq, axis=-1, keepdims=True) + EPS) * qlg_ref[...]).astype(BF16)
        qng = qng_ref[...]
        ckv = ulo[:, QL:QL + KVL]
        ckvn = (ckv * lax.rsqrt(jnp.mean(ckv * ckv, axis=-1, keepdims=True) + EPS) * kvlg_ref[...]).astype(BF16)
        qhs = [_nt(cqn, wuq_ref[hd]) for hd in range(NH)]
        kvs = [_nn(ckvn, wukv_ref[hd]) for hd in range(NH)]
        for hd in range(NH):
            qh = qhs[hd]
            qn = qh * lax.rsqrt(_rowsum(qh * qh) / DK + EPS) * qng
            q_ref[hd] = (jnp.concatenate([qn[:, :128], _rope(qn[:, 128:], cos, sin)], axis=1) * (SCALE * LOG2E)).astype(BF16)
        kr = ulo[:, 384:512]
        skr = _rowsum(kr * kr)
        kng = kng_ref[...]
        kr_roped = _rope(kr * kng[:, 128:], cos, sin)
        for hd in range(NH):
            kv = kvs[hd]
            kn = kv[:, :128]
            rk = lax.rsqrt((_rowsum(kn * kn) + skr) / DK + EPS)
            k_ref[hd] = jnp.concatenate([kn * rk * kng[:, :128], kr_roped * rk], axis=1).astype(BF16)
            v_ref[hd] = kv[:, 128:].astype(BF16)

    row = lambda w: pl.BlockSpec((TB, w), lambda i: (i, 0))
    heads = lambda w: pl.BlockSpec((NH, TB, w), lambda i: (0, i, 0))
    cspec, xspec, mspec = _token_specs(off)
    return pl.pallas_call(
        body, name="fwd_in", grid=(nb,),
        in_specs=[cspec, xspec, mspec, _full((1, D)), _full((DIN, D)), _full((1, QL)), _full((NH, DKP, QL)), _full((1, KVL)),
                  _full((NH, KVL, 256)), _full((1, DKP)), _full((1, DKP)),
                  pl.BlockSpec((1, 8, 256), lambda i: (jnp.maximum(i - off, 0), 0, 0)), _full((TB, 256))],
        out_specs=[row(DU), heads(DKP), heads(DKP), heads(DV)],
        out_shape=[jax.ShapeDtypeStruct((t_all, DU), F32), jax.ShapeDtypeStruct((NH, t_all, DKP), BF16),
                   jax.ShapeDtypeStruct((NH, t_all, DKP), BF16), jax.ShapeDtypeStruct((NH, t_all, DV), BF16)],
        compiler_params=_params(("arbitrary",)),
    )(ctx, x, modsel, norm_g, w_in_t, q_lora_g, w_uq_t, kv_lora_g, w_ukv, qn_g, kn_g, cos, sin)


def _attn_fwd(q, k, v, s_len):
    t_all = q.shape[1]
    off = (t_all - s_len) // TB
    nq = s_len // TB
    nsub = next(n for n in (4, 2, 1) if nq % n == 0)

    def body(*refs):
        q_refs = refs[:nsub]
        k_ref, v_ref, o_ref, lse_ref = refs[nsub:]
        for sb in range(nsub):
            s = _nt(q_refs[sb][0], k_ref[0])
            m = jnp.max(s, axis=-1, keepdims=True)
            e = jnp.exp2(s - m)
            l = _rowsum(e)
            o_ref[sb * TB:(sb + 1) * TB, :] = _nn(e.astype(BF16), v_ref[0]) / l
            lse_ref[0, sb] = _row_layout(m + jnp.log2(l))

    qspec = lambda sb: pl.BlockSpec((1, TB, DKP), lambda h, i: (h, i * nsub + sb + off, 0))
    return pl.pallas_call(
        body, name="attn_fwd", grid=(NH, nq // nsub),
        in_specs=[qspec(sb) for sb in range(nsub)]
        + [pl.BlockSpec((1, t_all, DKP), lambda h, i: (h, 0, 0)), pl.BlockSpec((1, t_all, DV), lambda h, i: (h, 0, 0))],
        out_specs=[pl.BlockSpec((nsub * TB, DV), lambda h, i: (i, h)), pl.BlockSpec((1, nsub, 8, TB), lambda h, i: (h, i, 0, 0))],
        out_shape=[jax.ShapeDtypeStruct((s_len, NH * DV), F32), jax.ShapeDtypeStruct((NH, nq, 8, TB), F32)],
        compiler_params=_params(("arbitrary", "arbitrary")),
    )(*([q] * nsub), k, v)


def _out_stage(attn, u, x, target, modsel, w_pool, pool_scale, w_out, lc):
    s_len = x.shape[0]
    t_all = s_len + lc
    off = lc // TB
    nq = s_len // TB
    hb = TB // HALO
    nqb = s_len // Q_BLOCK
    jb = TB // nqb

    def body(attn_ref, ga_ref, pin_ref, pprev_ref, pnext_ref, gp_ref, x_ref, tgt_ref, gate_ref, wp_ref, ps_ref, wo_ref,
             dxn_ref, dattn_ref, dga_ref, dgp_ref, dpool_ref, dwo_ref, dgate_ref, dps_ref, dwp_ref, loss_ref):
        i = pl.program_id(0)

        @pl.when(i == 0)
        def _():
            dwo_ref[...] = jnp.zeros_like(dwo_ref)
            dgate_ref[...] = jnp.zeros_like(dgate_ref)
            dps_ref[...] = jnp.zeros_like(dps_ref)
            dwp_ref[...] = jnp.zeros_like(dwp_ref)
            loss_ref[...] = jnp.zeros_like(loss_ref)

        attn = jnp.concatenate([attn_ref[:, jj, :] for jj in range(jb)], axis=0)
        ga = ga_ref[...]
        gp = gp_ref[...]
        pin = pin_ref[...]
        prev = jnp.where(i == 0, 0.0, pprev_ref[...])
        nxt = jnp.where(i == nq - 1, 0.0, pnext_ref[...])
        win = jnp.concatenate([prev, pin, nxt], axis=0)
        tg = i * TB + lax.broadcasted_iota(jnp.int32, (TB, 1), 0)
        pooled = []
        for g, w in enumerate(POOL_WINDOWS):
            a = win[:, g * 128:(g + 1) * 128]
            p = _shift_rows(a, -1) + a
            for step in (1, 2, 4):
                if w >= 4 * step:
                    p = _shift_rows(p, -step) + _shift_rows(p, step)
            cnt = (jnp.minimum(tg + w // 2, s_len) - jnp.maximum(tg - w // 2, 0)).astype(F32)
            pooled.append(p[HALO:HALO + TB] / cnt - a[HALO:HALO + TB])
        pooled_b = [p.astype(BF16) for p in pooled]
        wp = [wp_ref[g].astype(BF16) for g in range(4)]
        z = jnp.concatenate([_nn(pooled_b[g], wp[g]) for g in range(4)], axis=1)
        ps = ps_ref[...]
        yp = z * ps
        sga = _sig(ga)
        sila = ga * sga
        sgp = _sig(gp)
        silp = gp * sgp
        br = jnp.concatenate([sila * attn, silp * yp], axis=1).astype(BF16)
        y = _nn(br, wo_ref[...])
        gate = gate_ref[0, 2:3, :]
        err = x_ref[...] + gate * y - tgt_ref[...]
        loss_ref[...] += _colsum(_rowsum(err * err)) * (0.5 / D)
        dxn = err * (1.0 / D)
        dxn_ref[...] = dxn
        dgate_ref[...] += _colsum(dxn * y)
        dy = (dxn * gate).astype(BF16)
        dwo_ref[...] += _tn(br, dy)
        dbr = _nt(dy, wo_ref[...])
        dbra = dbr[:, :512]
        dbrp = dbr[:, 512:]
        dattn = dbra * sila
        for jj in range(jb):
            dattn_ref[:, jj, :] = dattn[jj * nqb:(jj + 1) * nqb]
        dga_ref[...] = (dbra * attn * (sga * (1.0 + ga * (1.0 - sga)))).astype(BF16)
        dgp_ref[...] = (dbrp * yp * (sgp * (1.0 + gp * (1.0 - sgp)))).astype(BF16)
        dyp = dbrp * silp
        dps_ref[...] += _colsum(dyp * z)
        dz = (dyp * ps).astype(BF16)
        dpool = []
        for g in range(4):
            dzg = dz[:, g * 128:(g + 1) * 128]
            dwp_ref[g] += _tn(pooled_b[g], dzg)
            dpool.append(_nt(dzg, wp[g]))
        dpool_ref[...] = jnp.concatenate(dpool, axis=1)

    lat = lambda w: pl.BlockSpec((TB, w), lambda i: (i, 0))
    perm = pl.BlockSpec((nqb, jb, 512), lambda i: (0, i, 0))
    ucol = lambda j: pl.BlockSpec((TB, 512), lambda i: (i + off, j))
    last8 = t_all // HALO - 1
    return pl.pallas_call(
        body, name="out_stage", grid=(nq,),
        in_specs=[perm, ucol(1), ucol(2),
                  pl.BlockSpec((HALO, 512), lambda i: ((i + off) * hb - 1, 2)),
                  pl.BlockSpec((HALO, 512), lambda i: (jnp.minimum((i + off + 1) * hb, last8), 2)),
                  ucol(3), lat(D), lat(D), pl.BlockSpec((1, 3, D), lambda i: (1, 0, 0)), _full((4, 128, 128)), _full((1, 512)),
                  _full((D, D))],
        out_specs=[lat(D), perm, lat(512), lat(512), lat(512),
                   _full((D, D)), _full((1, D)), _full((1, 512)), _full((4, 128, 128)), _full((1, 1))],
        out_shape=[jax.ShapeDtypeStruct((s_len, D), F32), jax.ShapeDtypeStruct((nqb, Q_BLOCK, 512), F32),
                   jax.ShapeDtypeStruct((s_len, 512), BF16), jax.ShapeDtypeStruct((s_len, 512), BF16),
                   jax.ShapeDtypeStruct((s_len, 512), F32),
                   jax.ShapeDtypeStruct((D, D), F32), jax.ShapeDtypeStruct((1, D), F32), jax.ShapeDtypeStruct((1, 512), F32),
                   jax.ShapeDtypeStruct((4, 128, 128), F32), jax.ShapeDtypeStruct((1, 1), F32)],
        compiler_params=_params(("arbitrary",)),
    )(attn, u, u, u, u, u, x, target, modsel, w_pool, pool_scale, w_out)


def _attn_bwd(q, k, v, dattn, attn, lse, s_len):
    t_all = q.shape[1]
    off = (t_all - s_len) // TB
    nq = s_len // TB
    nch = 4
    chunks = [(c * (t_all // nch), t_all // nch) for c in range(nch)]
    nsub = next(n for n in (BWD_QBLOCKS, 2, 1) if nq % n == 0)
    tq = nsub * TB

    def body(*refs):
        q_refs = refs[:nsub]
        k_ref, v_ref, do_ref, o_ref, lse_ref, dq_ref, dk_ref, dv_ref = refs[nsub:]
        i = pl.program_id(1)

        @pl.when(i == 0)
        def _():
            dk_ref[...] = jnp.zeros_like(dk_ref)
            dv_ref[...] = jnp.zeros_like(dv_ref)

        qb = jnp.concatenate([r[0] for r in q_refs], axis=0)
        delta_r = _row_layout(_rowsum(do_ref[...] * o_ref[...]))[0:1, :]
        do = do_ref[...].astype(BF16)
        lse_r = jnp.concatenate([lse_ref[0, sb][0:1, :] for sb in range(nsub)], axis=1)
        dq = jnp.zeros((tq, DKP), F32)
        for start, size in chunks:
            rows = pl.ds(start, size)
            kc = k_ref[0, rows, :]
            p_t = jnp.exp2(_nt(kc, qb) - lse_r)
            ds_t = (p_t * (_nt(v_ref[0, rows, :], do) - delta_r)).astype(BF16)
            dv_ref[0, rows, :] += _nn(p_t.astype(BF16), do)
            dk_ref[0, rows, :] += _nn(ds_t, qb)
            dq += _tn(ds_t, kc)
        dq_ref[0] = dq * SCALE

    kvspec = lambda w: pl.BlockSpec((1, t_all, w), lambda h, i: (h, 0, 0))
    rowspec = pl.BlockSpec((1, nsub, 8, TB), lambda h, i: (h, i, 0, 0))
    qspec = lambda sb: pl.BlockSpec((1, TB, DKP), lambda h, i: (h, i * nsub + sb + off, 0))
    return pl.pallas_call(
        body, name="attn_bwd", grid=(NH, nq // nsub),
        in_specs=[qspec(sb) for sb in range(nsub)]
        + [kvspec(DKP), kvspec(DV), pl.BlockSpec((tq, DV), lambda h, i: (i, h)), pl.BlockSpec((tq, DV), lambda h, i: (i, h)),
           rowspec],
        out_specs=[pl.BlockSpec((1, tq, DKP), lambda h, i: (h, i, 0)), kvspec(DKP), kvspec(DV)],
        out_shape=[jax.ShapeDtypeStruct((NH, s_len, DKP), F32), jax.ShapeDtypeStruct((NH, t_all, DKP), F32),
                   jax.ShapeDtypeStruct((NH, t_all, DV), F32)],
        compiler_params=_params(("arbitrary", "arbitrary")),
    )(*([q] * nsub), k, v, dattn, attn, lse)


def _qkv_bwd(u, dq, dk, dv, cos, sin, q_lora_g, w_uq_t, kv_lora_g, w_ukv, qn_g, kn_g, s_len):
    t_all = u.shape[0]
    off = (t_all - s_len) // TB
    nb = t_all // TB

    def body(ulo_ref, dq_ref, dk_ref, dv_ref, cos_ref, sin_ref, qlg_ref, wuq_ref, kvlg_ref, wukv_ref, qng_ref, kng_ref,
             dlo_ref, dwuq_ref, dwukv_ref, dqlg_ref, dkvlg_ref, dqng_ref, dkng_ref):
        i = pl.program_id(0)

        @pl.when(i == 0)
        def _():
            for r in (dwuq_ref, dwukv_ref, dqlg_ref, dkvlg_ref, dqng_ref, dkng_ref):
                r[...] = jnp.zeros_like(r)

        latent = i >= off
        ulo = ulo_ref[...]
        cos, sin = _rope_block(cos_ref, sin_ref, pl.program_id(0) < off)
        cq = ulo[:, 0:QL]
        rc = lax.rsqrt(jnp.mean(cq * cq, axis=-1, keepdims=True) + EPS)
        cqh = cq * rc
        qlg = qlg_ref[...]
        cqn_b = (cqh * qlg).astype(BF16)
        qng = qng_ref[...]
        ckv = ulo[:, QL:QL + KVL]
        r0 = lax.rsqrt(jnp.mean(ckv * ckv, axis=-1, keepdims=True) + EPS)
        ckvh = ckv * r0
        kvlg = kvlg_ref[...]
        ckvn_b = (ckvh * kvlg).astype(BF16)
        qhs = [_nt(cqn_b, wuq_ref[hd]) for hd in range(NH)]
        kns = [_nn(ckvn_b, wukv_ref[hd])[:, :128] for hd in range(NH)]
        dqng = jnp.zeros((1, DKP), F32)
        dqraws = []
        for hd in range(NH):
            qh = qhs[hd]
            rq = lax.rsqrt(_rowsum(qh * qh) / DK + EPS)
            xh = qh * rq
            dqh = jnp.where(latent, dq_ref[hd], 0.0)
            dyq = jnp.concatenate([dqh[:, :128], _rope_t(dqh[:, 128:], cos, sin)], axis=1)
            dqng += _colsum(dyq * xh)
            dxh = dyq * qng
            dqraws.append((rq * (dxh - xh * (_rowsum(dxh * xh) / DK))).astype(BF16))
        dqng_ref[...] += dqng

        kr = ulo[:, 384:512]
        skr = _rowsum(kr * kr)
        kng = kng_ref[...]
        dkr = jnp.zeros((TB, 128), F32)
        dkng = jnp.zeros((1, DKP), F32)
        dkvs = []
        for hd in range(NH):
            kn = kns[hd]
            rk = lax.rsqrt((_rowsum(kn * kn) + skr) / DK + EPS)
            xh1 = kn * rk
            xh2 = kr * rk
            dkh = dk_ref[hd] * LN2
            d1 = dkh[:, :128]
            d2 = _rope_t(dkh[:, 128:], cos, sin)
            dkng += jnp.concatenate([_colsum(d1 * xh1), _colsum(d2 * xh2)], axis=1)
            dx1 = d1 * kng[:, :128]
            dx2 = d2 * kng[:, 128:]
            dot = (_rowsum(dx1 * xh1) + _rowsum(dx2 * xh2)) / DK
            dkvs.append(jnp.concatenate([rk * (dx1 - xh1 * dot), dv_ref[hd]], axis=1).astype(BF16))
            dkr += rk * (dx2 - xh2 * dot)
        dkng_ref[...] += dkng

        dcqn = jnp.zeros((TB, QL), F32)
        dckvn = jnp.zeros((TB, KVL), F32)
        for hd in range(NH):
            dwuq_ref[hd] += _tn(dqraws[hd], cqn_b)[:DK]
            dcqn += _nn(dqraws[hd], wuq_ref[hd])
            dwukv_ref[hd] += _tn(ckvn_b, dkvs[hd])
            dckvn += _nt(dkvs[hd], wukv_ref[hd])
        dqlg_ref[...] += _colsum(dcqn * cqh)
        dxh = dcqn * qlg
        dcq = rc * (dxh - cqh * jnp.mean(dxh * cqh, axis=-1, keepdims=True))
        dkvlg_ref[...] += _colsum(dckvn * ckvh)
        dxh = dckvn * kvlg
        dckv = r0 * (dxh - ckvh * jnp.mean(dxh * ckvh, axis=-1, keepdims=True))
        dlo_ref[...] = jnp.concatenate([dcq, dckv, dkr], axis=1).astype(BF16)

    row = lambda w: pl.BlockSpec((TB, w), lambda i: (i, 0))
    heads = lambda w: pl.BlockSpec((NH, TB, w), lambda i: (0, i, 0))
    return pl.pallas_call(
        body, name="qkv_bwd", grid=(nb,),
        in_specs=[row(512), pl.BlockSpec((NH, TB, DKP), lambda i: (0, jnp.maximum(i - off, 0), 0)), heads(DKP), heads(DV),
                  pl.BlockSpec((1, 8, 256), lambda i: (jnp.maximum(i - off, 0), 0, 0)), _full((TB, 256)), _full((1, QL)), _full((NH, DKP, QL)), _full((1, KVL)), _full((NH, KVL, 256)),
                  _full((1, DKP)), _full((1, DKP))],
        out_specs=[row(512), _full((NH, DK, QL)), _full((NH, KVL, 256)), _full((1, QL)), _full((1, KVL)),
                   _full((1, DKP)), _full((1, DKP))],
        out_shape=[jax.ShapeDtypeStruct((t_all, 512), BF16), jax.ShapeDtypeStruct((NH, DK, QL), F32),
                   jax.ShapeDtypeStruct((NH, KVL, 256), F32), jax.ShapeDtypeStruct((1, QL), F32),
                   jax.ShapeDtypeStruct((1, KVL), F32), jax.ShapeDtypeStruct((1, DKP), F32), jax.ShapeDtypeStruct((1, DKP), F32)],
        compiler_params=_params(("arbitrary",)),
    )(u, dq, dk, dv, cos, sin, q_lora_g, w_uq_t, kv_lora_g, w_ukv, qn_g, kn_g)


def _in_bwd(ctx, x, modsel, norm_g, dlo, dga, dgp, dpool, dxn, w_in_t):
    s_len, lc = x.shape[0], ctx.shape[0]
    t_all = s_len + lc
    off = lc // TB
    nb = t_all // TB
    nq = s_len // TB
    hb = TB // HALO
    n = TB + 2 * HALO

    def body(ctx_ref, x_ref, mod_ref, ng_ref, dlo_ref, dga_ref, dgp_ref, dp_ref, dpprev_ref, dpnext_ref, dxn_ref, win_ref,
             gx_ref, dwin_ref, dmod_ref, dng_ref):
        i = pl.program_id(0)
        j = i - off

        @pl.when(i == 0)
        def _():
            dwin_ref[...] = jnp.zeros_like(dwin_ref)
            dmod_ref[...] = jnp.zeros_like(dmod_ref)
            dng_ref[...] = jnp.zeros_like(dng_ref)

        latent = i >= off
        dp = dp_ref[...]
        prev = jnp.where(j <= 0, 0.0, dpprev_ref[...])
        nxt = jnp.where(j >= nq - 1, 0.0, dpnext_ref[...])
        win = jnp.concatenate([prev, dp, nxt], axis=0)
        tg = j * TB - HALO + lax.broadcasted_iota(jnp.int32, (n, 1), 0)
        dpin = []
        for g, w in enumerate(POOL_WINDOWS):
            cnt = jnp.maximum(jnp.minimum(tg + w // 2, s_len) - jnp.maximum(tg - w // 2, 0), 1).astype(F32)
            zq = win[:, g * 128:(g + 1) * 128] / cnt
            zq = zq + _shift_rows(zq, 1)
            for step in (1, 2, 4):
                if w >= 4 * step:
                    zq = _shift_rows(zq, -step) + _shift_rows(zq, step)
            dpin.append(zq[HALO:HALO + TB] - dp[:, g * 128:(g + 1) * 128])
        zero = jnp.zeros((TB, 512), BF16)
        du = [dlo_ref[...], jnp.where(latent, dga_ref[...], zero),
              jnp.where(latent, jnp.concatenate(dpin, axis=1).astype(BF16), zero), jnp.where(latent, dgp_ref[...], zero)]

        ng = ng_ref[...]
        xb = jnp.where(i < off, ctx_ref[...], x_ref[...])
        r, xh, xg, h, scale = _modulated(xb, mod_ref, ng)
        hb_ = h.astype(BF16)
        dh = jnp.zeros((TB, D), F32)
        for s, (lo, hi) in enumerate(SEG):
            dwin_ref[lo:hi, :] += _tn(du[s], hb_)
            dh += _nn(du[s], win_ref[lo:hi, :])
        is_lat = latent.astype(F32)
        dsh = _colsum(dh)
        dsc = _colsum(dh * xg)
        dmod_ref[0, 0:1, :] += dsh * (1.0 - is_lat)
        dmod_ref[0, 1:2, :] += dsc * (1.0 - is_lat)
        dmod_ref[1, 0:1, :] += dsh * is_lat
        dmod_ref[1, 1:2, :] += dsc * is_lat
        dxg = dh * (1.0 + scale)
        dng_ref[...] += _colsum(dxg * xh)
        dxh = dxg * ng
        gx_ref[...] = r * (dxh - xh * jnp.mean(dxh * xh, axis=-1, keepdims=True)) + dxn_ref[...]

    row = lambda w: pl.BlockSpec((TB, w), lambda i: (i, 0))
    lat = lambda w: pl.BlockSpec((TB, w), lambda i: (jnp.maximum(i - off, 0), 0))
    last8 = s_len // HALO - 1
    cspec, xspec, mspec = _token_specs(off)
    return pl.pallas_call(
        body, name="in_bwd", grid=(nb,),
        in_specs=[cspec, xspec, mspec, _full((1, D)), row(512), lat(512), lat(512), lat(512),
                  pl.BlockSpec((HALO, 512), lambda i: (jnp.maximum(jnp.maximum(i - off, 0) * hb - 1, 0), 0)),
                  pl.BlockSpec((HALO, 512), lambda i: (jnp.minimum((jnp.maximum(i - off, 0) + 1) * hb, last8), 0)),
                  lat(D), _full((DIN, D))],
        out_specs=[lat(D), _full((DIN, D)), _full((2, 2, D)), _full((1, D))],
        out_shape=[jax.ShapeDtypeStruct((s_len, D), F32), jax.ShapeDtypeStruct((DIN, D), F32),
                   jax.ShapeDtypeStruct((2, 2, D), F32), jax.ShapeDtypeStruct((1, D), F32)],
        compiler_params=_params(("arbitrary",)),
    )(ctx, x, modsel, norm_g, dlo, dga, dgp, dpool, dpool, dpool, dxn, w_in_t)


def _adamw_update(w_ref, g_ref, m_ref, v_ref, d_ref, mo_ref, vo_ref):
    gv = g_ref[...]
    mn = ADAM_B1 * m_ref[...] + (1.0 - ADAM_B1) * gv
    vn = ADAM_B2 * v_ref[...] + (1.0 - ADAM_B2) * (gv * gv)
    m_hat = mn / (1.0 - ADAM_B1 ** ADAM_STEP)
    v_hat = vn / (1.0 - ADAM_B2 ** ADAM_STEP)
    d_ref[...] = -ADAM_LR * (m_hat / (jnp.sqrt(v_hat) + ADAM_EPS) + ADAM_WD * w_ref[...])
    mo_ref[...] = mn
    vo_ref[...] = vn


def _adamw_many(ws, gs, ms, vs):
    n = len(ws)
    parts = 4

    def body(*refs):
        for i in range(n):
            _adamw_update(refs[i], refs[n + i], refs[2 * n + i], refs[3 * n + i], refs[4 * n + i], refs[5 * n + i], refs[6 * n + i])
            refs[7 * n + i][...] = refs[n + i][...]

    def spec(w):
        rows, cols = w.shape
        if rows % (8 * parts) == 0:
            return pl.BlockSpec((rows // parts, cols), lambda i: (i, 0))
        if cols % (128 * parts) == 0:
            return pl.BlockSpec((rows, cols // parts), lambda i: (0, i))
        return _full((rows, cols))

    specs = [spec(w) for w in ws]
    shp = [jax.ShapeDtypeStruct(w.shape, F32) for w in ws]
    out = pl.pallas_call(body, name="adamw_many", grid=(parts,), in_specs=specs * 4, out_specs=specs * 4, out_shape=shp * 4,
                         compiler_params=_params(("arbitrary",)))(*ws, *gs, *ms, *vs)
    return out[:n], out[n:2 * n], out[2 * n:3 * n], out[3 * n:]


class _Links:
    def __init__(self, send_sems, recv_sems):
        self.send_sems, self.recv_sems, self.sends = send_sems, recv_sems, []

    def send(self, src, dst, sem, to):
        cp = pltpu.make_async_remote_copy(src, dst, self.send_sems.at[sem], self.recv_sems.at[sem], device_id=to,
                                          device_id_type=MESH)
        cp.start()
        self.sends.append(cp)

    def arrived(self, dst, sem, frm):
        pltpu.make_async_remote_copy(dst, dst, self.send_sems.at[sem], self.recv_sems.at[sem], device_id=frm,
                                     device_id_type=MESH).wait_recv()

    def drain(self):
        for cp in self.sends:
            cp.wait_send()


def _half(ref, c, axis):
    size = ref.shape[axis - 2] // 2
    win = pl.ds(pl.multiple_of(c * size, 16 if axis == 0 else 128), size)
    idx = (win, slice(None)) if axis == 0 else (slice(None), win)
    return ref.at[(slice(None),) * (len(ref.shape) - 2) + idx]


def _select_rows(slots_ref, n_slots, row=0):
    sub = lax.broadcasted_iota(jnp.int32, (8, 1), 0)
    out = None
    for d in range(n_slots):
        r = jnp.where(sub == d, jnp.broadcast_to(slots_ref[d][row:row + 1, :], (8, slots_ref.shape[-1])), 0.0)
        out = r if out is None else out + r
    return out


def _gather(c, c_ctx, w_mod, b_mod_k, shards, axes, slab_rows):
    nw = len(shards)
    kw = w_mod.shape[1]

    def body(*refs):
        c_ref, cc_ref, wm_hbm, b_ref = refs[:4]
        w_hbm = refs[4:4 + nw]
        a16_ref, modsel_ref = refs[4 + nw:6 + nw]
        out_refs = refs[6 + nw:6 + 2 * nw]
        g_refs = refs[6 + 2 * nw:6 + 3 * nw]
        f_refs = refs[6 + 3 * nw:6 + 4 * nw]
        wm_ref, a_ref, mod_ref, send_sems, recv_sems, local_sems = refs[6 + 4 * nw:]
        loads = [pltpu.make_async_copy(w_hbm[wi], f_refs[wi], local_sems.at[wi]) for wi in range(nw)]
        loads.append(pltpu.make_async_copy(wm_hbm, wm_ref, local_sems.at[nw]))
        for cp in loads:
            cp.start()
        stores = []

        def slab(wi, chip, of=g_refs):
            return of[wi].at[chip].at[0:shards[wi].shape[0]]

        def store(src, dst, wi, slot):
            cp = pltpu.make_async_copy(src, dst, local_sems.at[nw + 1 + wi * 8 + slot])
            cp.start()
            stores.append(cp)

        def store_half(wi, chip, half, slot):
            store(_half(slab(wi, chip), half, axes[wi]), _half(slab(wi, chip, out_refs), half, axes[wi]), wi, slot)
        x, y, cc = lax.axis_index("x"), lax.axis_index("y"), lax.axis_index("c")
        me = 4 * x + 2 * y + cc
        k = 2 * x + y
        sibling = (x, y, 1 - cc)
        links = _Links(send_sems, recv_sems)
        chips = [_peer(x, y, cc, off + (0,)) for off in CHIPS3]
        chip_a = ((x + 1 - cc) % 2, (y + cc) % 2, cc)
        chip_b = ((x + cc) % 2, (y + 1 - cc) % 2, cc)
        chip_d = (1 - x, 1 - y, cc)
        cv = c_ref[...]
        sc = cv * _sig(cv)
        mine = a_ref.at[me]
        for r in range(8):
            mine[r:r + 1, :] = sc[:, r * 128:(r + 1) * 128]
        for j, off in enumerate(PEERS7):
            links.send(a_ref.at[me], a_ref.at[me], j, _peer(x, y, cc, off))
        for wi in range(nw):
            loads[wi].wait()
            slab(wi, k)[...] = f_refs[wi][...].astype(BF16)
            for j, to in enumerate((chip_a, chip_b)):
                links.send(_half(slab(wi, k), cc, axes[wi]), _half(slab(wi, k), cc, axes[wi]), 10 + wi * 6 + j, to)
            store(slab(wi, k), slab(wi, k, out_refs), wi, 0)
            rows = shards[wi].shape[0]
            pad = slab_rows[wi] - rows
            if pad:
                for kk in range(4):
                    g_refs[wi][kk, rows:, :] = jnp.zeros((pad, shards[wi].shape[1]), BF16)
                store(g_refs[wi].at[:, pl.ds(rows, pad), :], out_refs[wi].at[:, pl.ds(rows, pad), :], wi, 7)
        for j, off in enumerate(PEERS7):
            px, py, pc = _peer(x, y, cc, off)
            links.arrived(a_ref.at[4 * px + 2 * py + pc], j, (px, py, pc))
        ccv = cc_ref[...]
        sub = lax.broadcasted_iota(jnp.int32, (8, 1), 0)
        top = jnp.zeros((8, D), F32)
        for d in range(8):
            blk = a_ref[d]
            row = jnp.concatenate([blk[r:r + 1, :] for r in range(8)], axis=1)
            top = top + jnp.where(sub == d, jnp.broadcast_to(row, (8, D)), 0.0)
        a16 = jnp.concatenate([top, jnp.where(sub == 0, jnp.broadcast_to(ccv * _sig(ccv), (8, D)), 0.0)], axis=0)
        a16_ref[...] = a16
        loads[nw].wait()
        mod_ref[k] = _dot3(_nn, a16, wm_ref[...]) + b_ref[...]
        for j, to in enumerate(chips):
            links.send(mod_ref.at[k], mod_ref.at[k], 7 + j, to)

        def over_ici(j, frm, origin):
            for wi in range(nw):
                blk = _half(slab(wi, 2 * origin[0] + origin[1]), cc, axes[wi])
                links.arrived(blk, 10 + wi * 6 + j, frm)
                if j == 0:
                    links.send(blk, blk, 10 + wi * 6 + 2, chip_b)
                links.send(blk, blk, 10 + wi * 6 + 3 + j, sibling)
                store_half(wi, 2 * origin[0] + origin[1], cc, 1 + j)

        def from_sibling(j, origin):
            for wi in range(nw):
                links.arrived(_half(slab(wi, 2 * origin[0] + origin[1]), 1 - cc, axes[wi]), 10 + wi * 6 + 3 + j, sibling)
                store_half(wi, 2 * origin[0] + origin[1], 1 - cc, 4 + j)
        over_ici(0, chip_a, chip_a)
        over_ici(1, chip_b, chip_b)
        from_sibling(0, chip_b)
        over_ici(2, chip_b, chip_d)
        from_sibling(1, chip_a)
        from_sibling(2, chip_d)
        for j, (px, py, pc) in enumerate(chips):
            links.arrived(mod_ref.at[2 * px + py], 7 + j, (px, py, pc))
        for sel, row in ((0, 8), (1, me)):
            for kk in range(4):
                piece = mod_ref[kk, pl.ds(row, 1), :]
                lo = kk * kw
                while lo < (kk + 1) * kw:
                    r, col = divmod(lo, D)
                    n = min((kk + 1) * kw - lo, D - col)
                    modsel_ref[sel, r:r + 1, col:col + n] = piece[:, lo - kk * kw:lo - kk * kw + n]
                    lo += n
        links.drain()
        for cp in stores:
            cp.wait()

    nsem = 10 + 6 * nw
    gathered = [(4, r, s.shape[1]) for r, s in zip(slab_rows, shards)]
    return pl.pallas_call(
        body, name="gather", in_specs=[VM, VM, ANY, VM] + [ANY] * nw, out_specs=[VM, VM] + [ANY] * nw,
        out_shape=[jax.ShapeDtypeStruct((16, D), F32), jax.ShapeDtypeStruct((2, 3, D), F32)]
        + [jax.ShapeDtypeStruct(g, BF16) for g in gathered],
        scratch_shapes=[pltpu.VMEM(g, BF16) for g in gathered] + [pltpu.VMEM(s.shape, F32) for s in shards]
        + [pltpu.VMEM(w_mod.shape, F32), pltpu.VMEM((8, 8, D // 8), F32), pltpu.VMEM((4, 16, kw), F32),
           pltpu.SemaphoreType.DMA((nsem,)),
           pltpu.SemaphoreType.DMA((nsem,)), pltpu.SemaphoreType.DMA((nw + 1 + 8 * nw,))],
        compiler_params=pltpu.CompilerParams(vmem_limit_bytes=VMEM_LIMIT),
    )(c, c_ctx, w_mod, b_mod_k, *shards)


SMALL_ROW_WIDTHS = (D, QL, KVL, DKP, DKP, 512, 128)
SMALL_OUT_WIDTHS = (D, QL, KVL, DK, DK, 512, 1)
SMALL_PACK = 384


def _small_pieces():
    pieces = []
    for i, w in enumerate(SMALL_ROW_WIDTHS):
        for c0 in range(0, w, 128):
            j = len(pieces)
            pieces.append((i, c0, j // (SMALL_PACK // 128), j % (SMALL_PACK // 128) * 128))
    assert len(pieces) <= 8 * (SMALL_PACK // 128)
    return pieces


def _reduce(grads, axes, smalls, w_pool_g, dmod8, a16, w_mod, c_ctx):
    nw = len(grads)
    ns = len(smalls)
    kw = w_mod.shape[1]
    halves = []
    for g, ax in zip(grads, axes):
        halves.append((g.shape[1] // 2, g.shape[2]) if ax == 0 else (g.shape[1], g.shape[2] // 2))

    def body(*refs):
        g_refs = refs[:nw]
        small_refs = refs[nw:nw + ns]
        wp_ref, dm_ref, a16_ref, wm_hbm, cc_ref = refs[nw + ns:nw + ns + 5]
        o = nw + ns + 5
        r_outs = refs[o:o + nw]
        small_outs = refs[o + nw:o + nw + ns]
        rwp_ref, gw_out, gb_ref, gc_ref = refs[o + nw + ns:o + nw + ns + 4]
        o = o + nw + ns + 4
        own, sib, part, got, rel, r_refs = (refs[o + i * nw:o + (i + 1) * nw] for i in range(6))
        smbuf, wps, wpg, dm_all, pc_all, wm_ref, gw_ref, send_sems, recv_sems, local_sems = refs[o + 6 * nw:]
        wm_load = pltpu.make_async_copy(wm_hbm, wm_ref, local_sems.at[nw])
        wm_load.start()
        stores = []

        def store(src, dst, sem):
            cp = pltpu.make_async_copy(src, dst, local_sems.at[nw + 1 + sem])
            cp.start()
            stores.append(cp)
        x, y, cc = lax.axis_index("x"), lax.axis_index("y"), lax.axis_index("c")
        me = 4 * x + 2 * y + cc
        k = 2 * x + y
        sibling = (x, y, 1 - cc)
        links = _Links(send_sems, recv_sems)
        chips = [_peer(x, y, cc, off + (0,)) for off in CHIPS3]
        peers = [_peer(x, y, cc, off) for off in PEERS7]
        chip_a = ((x + 1 - cc) % 2, (y + cc) % 2, cc)
        chip_b = ((x + cc) % 2, (y + 1 - cc) % 2, cc)
        ka, kb, kd = 2 * chip_a[0] + chip_a[1], 2 * chip_b[0] + chip_b[1], 2 * (1 - x) + (1 - y)
        big, sm0, wp0, dm0, pc0 = 0, 5 * nw, 5 * nw + 7, 5 * nw + 14, 5 * nw + 21

        locals_ = []
        for wi in range(nw):
            lc = pltpu.make_async_copy(_half(g_refs[wi], cc, axes[wi]), own[wi], local_sems.at[wi])
            lc.start()
            locals_.append(lc)
            links.send(_half(g_refs[wi], 1 - cc, axes[wi]), sib[wi], big + wi * 5, sibling)
        slot = smbuf.at[me]
        slot[...] = jnp.zeros((8, SMALL_PACK), F32)
        small_rows = [jnp.broadcast_to(ref[...], (1, w)) for ref, w in zip(small_refs, SMALL_ROW_WIDTHS)]
        for i, c0, row, lane in _small_pieces():
            slot[row:row + 1, lane:lane + 128] = small_rows[i][:, c0:c0 + 128]
        links.send(wp_ref, wps, wp0, sibling)
        dm_all[me] = dm_ref[...]
        for j, peer in enumerate(peers):
            links.send(dm_all.at[me], dm_all.at[me], dm0 + j, peer)
            links.send(smbuf.at[me], smbuf.at[me], sm0 + j, peer)
        links.arrived(wps, wp0, sibling)
        wpg[k] = (wp_ref[...] + wps[...]).astype(BF16)
        for j, to in enumerate(chips):
            links.send(wpg.at[k], wpg.at[k], wp0 + 1 + j, to)
        for wi in range(nw):
            locals_[wi].wait()
            links.arrived(sib[wi], big + wi * 5, sibling)
            part[wi][...] = (own[wi][...] + sib[wi][...]).astype(BF16)
            got[wi][k] = part[wi][k]
            got[wi][kd] = jnp.zeros(halves[wi], BF16)
            links.send(part[wi].at[kd], rel[wi], big + wi * 5 + 1, chip_b)
            links.send(part[wi].at[kb], got[wi].at[k], big + wi * 5 + 2, chip_b)
        for j, (px, py, pc) in enumerate(peers):
            links.arrived(dm_all.at[4 * px + 2 * py + pc], dm0 + j, (px, py, pc))
        dm_tot = dm_all[0]
        for d in range(1, 8):
            dm_tot = dm_tot + dm_all[d]
        for kk in range(4):
            gb_ref[:, kk * kw:(kk + 1) * kw] = dm_tot[kk:kk + 1, :] + dm_tot[4 + kk:5 + kk, :]
        top = jnp.zeros((8, kw), F32)
        dmc_k = jnp.zeros((1, kw), F32)
        for kk in range(4):
            top = top + jnp.where(k == kk, _select_rows(dm_all, 8, kk), 0.0)
            dmc_k = dmc_k + jnp.where(k == kk, dm_tot[4 + kk:5 + kk, :], 0.0)
        sub = lax.broadcasted_iota(jnp.int32, (8, 1), 0)
        bk = jnp.concatenate([top, jnp.where(sub == 0, jnp.broadcast_to(dmc_k, (8, kw)), 0.0)], axis=0)
        gw_ref[...] = _dot3(_tn, a16_ref[...], bk)
        store(gw_ref, gw_out, 0)
        wm_load.wait()
        pc_all[k] =_dot3(_nt, jnp.broadcast_to(bk[8:9, :], (8, kw)), wm_ref[...])
        for j, to in enumerate(chips):
            links.send(pc_all.at[k], pc_all.at[k], pc0 + j, to)
        for j, (px, py, pc) in enumerate(peers):
            links.arrived(smbuf.at[4 * px + 2 * py + pc], sm0 + j, (px, py, pc))
        tot = smbuf[0]
        for d in range(1, 8):
            tot = tot + smbuf[d]
        for i, c0, row, lane in _small_pieces():
            n = min(128, SMALL_OUT_WIDTHS[i] - c0)
            if n > 0:
                small_outs[i][:, c0:c0 + n] = tot[row:row + 1, lane:lane + n]
        for j, (px, py, pc) in enumerate(chips):
            links.arrived(wpg.at[2 * px + py], wp0 + 1 + j, (px, py, pc))
        wpt = wpg[0].astype(F32)
        for kk in range(1, 4):
            wpt = wpt + wpg[kk].astype(F32)
        rwp_ref[...] = wpt
        for wi in range(nw):
            links.arrived(rel[wi], big + wi * 5 + 1, chip_b)
            rel[wi][...] = (part[wi][ka].astype(F32) + rel[wi][...].astype(F32)).astype(BF16)
            links.send(rel[wi], got[wi].at[k], big + wi * 5 + 3, chip_a)
        for wi in range(nw):
            links.arrived(got[wi].at[kb], big + wi * 5 + 2, chip_b)
            links.arrived(got[wi].at[ka], big + wi * 5 + 3, chip_a)
            total = got[wi][0].astype(F32)
            for kk in range(1, 4):
                total = total + got[wi][kk].astype(F32)
            mine = _half(r_refs[wi], cc, axes[wi])
            mine[...] = total
            links.send(mine, mine, big + wi * 5 + 4, sibling)
            store(mine, _half(r_outs[wi], cc, axes[wi]), 1 + wi)
        for j, (px, py, pc) in enumerate(chips):
            links.arrived(pc_all.at[2 * px + py], pc0 + j, (px, py, pc))
        ccv = cc_ref[...]
        sg = _sig(ccv)
        gc_ref[...] = (pc_all[0][0:1, :] + pc_all[1][0:1, :] + pc_all[2][0:1, :] + pc_all[3][0:1, :]) * (sg * (1.0 + ccv * (1.0 - sg)))
        for wi in range(nw):
            links.arrived(_half(r_refs[wi], 1 - cc, axes[wi]), big + wi * 5 + 4, sibling)
            store(_half(r_refs[wi], 1 - cc, axes[wi]), _half(r_outs[wi], 1 - cc, axes[wi]), 1 + nw + wi)
        links.drain()
        for cp in stores:
            cp.wait()

    nsem = 5 * nw + 24
    quads = [(4,) + h for h in halves]
    return pl.pallas_call(
        body, name="reduce", in_specs=[ANY] * nw + [VM] * (ns + 3) + [ANY, VM],
        out_specs=[ANY] * nw + [VM] * (ns + 1) + [ANY, VM, VM],
        out_shape=[jax.ShapeDtypeStruct(g.shape[1:], F32) for g in grads]
        + [jax.ShapeDtypeStruct((1, w), F32) for w in SMALL_OUT_WIDTHS]
        + [jax.ShapeDtypeStruct(w_pool_g.shape, F32), jax.ShapeDtypeStruct((D, kw), F32), jax.ShapeDtypeStruct((1, 3 * D), F32),
           jax.ShapeDtypeStruct((1, D), F32)],
        scratch_shapes=[pltpu.VMEM(q, F32) for q in quads] + [pltpu.VMEM(q, F32) for q in quads]
        + [pltpu.VMEM(q, BF16) for q in quads] + [pltpu.VMEM(q, BF16) for q in quads] + [pltpu.VMEM(h, BF16) for h in halves]
        + [pltpu.VMEM(g.shape[1:], F32) for g in grads]
        + [pltpu.VMEM((8, 8, SMALL_PACK), F32), pltpu.VMEM(w_pool_g.shape, F32), pltpu.VMEM((4,) + w_pool_g.shape, BF16),
           pltpu.VMEM((8, 8, kw), F32), pltpu.VMEM((4, 8, D), F32), pltpu.VMEM((D, kw), F32), pltpu.VMEM((D, kw), F32)]
        + [pltpu.SemaphoreType.DMA((nsem,)), pltpu.SemaphoreType.DMA((nsem,)), pltpu.SemaphoreType.DMA((3 * nw + 2,))],
        compiler_params=pltpu.CompilerParams(vmem_limit_bytes=VMEM_LIMIT),
    )(*grads, *smalls, w_pool_g, dmod8, a16, w_mod, c_ctx)


def _rope_tables(s_len):
    rows = s_len // GRID_W
    per = TB // GRID_W
    n_freq = 16
    f32 = np.float32
    inv = f32(ROPE_BASE) ** (-np.arange(n_freq, dtype=f32) / f32(n_freq))
    ang_r = np.arange(rows, dtype=f32)[:, None] * inv
    ang_c = np.arange(GRID_W, dtype=f32)[:, None] * inv
    by_row, by_col = [], []
    for fn, pad in ((np.cos, 1.0), (np.sin, 0.0)):
        r = np.concatenate([fn(ang_r), fn(ang_r), np.zeros((rows, 96), f32)], axis=1).reshape(rows // per, per, 128)
        by_row.append(np.pad(r, ((0, 0), (0, 8 - per), (0, 0))))
        cpart = np.concatenate([np.zeros((GRID_W, 32), f32), fn(ang_c), fn(ang_c), np.full((GRID_W, 64), pad, f32)], axis=1)
        by_col.append(np.tile(cpart, (per, 1)))
    return jnp.asarray(np.concatenate(by_row, axis=-1), F32), jnp.asarray(np.concatenate(by_col, axis=-1), F32)


def kernel(x, c, ctx, c_ctx, w_mod, b_mod, norm_g, w_in, q_lora_g, w_uq, kv_lora_g, w_ukv, q_norm_g, k_norm_g, w_pool, pool_scale, w_out, loss_target, m_c_ctx, m_w_mod, m_b_mod, m_norm_g, m_w_in, m_q_lora_g, m_w_uq, m_kv_lora_g, m_w_ukv, m_q_norm_g, m_k_norm_g, m_w_pool, m_pool_scale, m_w_out, v_c_ctx, v_w_mod, v_b_mod, v_norm_g, v_w_in, v_q_lora_g, v_w_uq, v_kv_lora_g, v_w_ukv, v_q_norm_g, v_k_norm_g, v_w_pool, v_pool_scale, v_w_out):
    xi, yi, ci = lax.axis_index("x"), lax.axis_index("y"), lax.axis_index("c")
    me = 4 * xi + 2 * yi + ci
    k = 2 * xi + yi
    s_len = x.shape[1]
    lc = ctx.shape[1]
    kw = w_mod.shape[2]
    weights = dict(c_ctx=c_ctx, w_mod=w_mod, b_mod=b_mod, norm_g=norm_g, w_in=w_in, q_lora_g=q_lora_g, w_uq=w_uq,
                   kv_lora_g=kv_lora_g, w_ukv=w_ukv, q_norm_g=q_norm_g, k_norm_g=k_norm_g, w_pool=w_pool,
                   pool_scale=pool_scale, w_out=w_out)
    m_in = dict(c_ctx=m_c_ctx, w_mod=m_w_mod, b_mod=m_b_mod, norm_g=m_norm_g, w_in=m_w_in, q_lora_g=m_q_lora_g, w_uq=m_w_uq,
                kv_lora_g=m_kv_lora_g, w_ukv=m_w_ukv, q_norm_g=m_q_norm_g, k_norm_g=m_k_norm_g, w_pool=m_w_pool,
                pool_scale=m_pool_scale, w_out=m_w_out)
    v_in = dict(c_ctx=v_c_ctx, w_mod=v_w_mod, b_mod=v_b_mod, norm_g=v_norm_g, w_in=v_w_in, q_lora_g=v_q_lora_g, w_uq=v_w_uq,
                kv_lora_g=v_kv_lora_g, w_ukv=v_w_ukv, q_norm_g=v_q_norm_g, k_norm_g=v_k_norm_g, w_pool=v_w_pool,
                pool_scale=v_pool_scale, w_out=v_w_out)
    order = ["c_ctx", "w_mod", "b_mod", "norm_g", "w_in", "q_lora_g", "w_uq", "kv_lora_g", "w_ukv", "q_norm_g", "k_norm_g",
             "w_pool", "pool_scale", "w_out"]
    transposed = ("w_in", "w_uq")
    as2d = lambda n, a: jnp.transpose(a[0]) if n in transposed else a.reshape(-1, a.shape[-1])
    back = lambda n, a: jnp.transpose(a)[None] if n in transposed else a.reshape(weights[n].shape)

    c_ctx2 = c_ctx.reshape(1, D)
    b_mod_k = lax.dynamic_slice(b_mod, (0, k * kw), (1, kw))
    split = (0, 1, 0, 0)
    a16, modsel, g_out, g_in, g_uq, g_ukv = _gather(
        c, c_ctx2, w_mod[0], b_mod_k, [w_out[0], as2d("w_in", w_in), as2d("w_uq", w_uq), w_ukv[0]], split,
        (D // 4, DIN // 4, DKP, KVL))
    w_in_t = g_in.reshape(DIN, D)
    w_uq_t = g_uq
    w_out_f = g_out.reshape(D, D)
    qn_g = jnp.pad(q_norm_g, ((0, 0), (0, DKP - DK)))
    kn_g = jnp.pad(k_norm_g, ((0, 0), (0, DKP - DK)))
    cos, sin = _rope_tables(s_len)

    u, q, kk, v = _fwd_in(ctx[0], x[0], modsel, norm_g, w_in_t, q_lora_g, w_uq_t, kv_lora_g, g_ukv, qn_g, kn_g, cos, sin)
    attn, lse = _attn_fwd(q, kk, v, s_len)
    (dxn, dattn, dga, dgp, dpool, dw_out, dgate, dps, dw_pool, loss) = _out_stage(
        attn.reshape(s_len // Q_BLOCK, Q_BLOCK, NH * DV), u, x[0], loss_target[0], modsel, w_pool[0], pool_scale,
        w_out_f, lc)
    dattn = dattn.reshape(s_len, NH * DV)
    dq, dk, dv = _attn_bwd(q, kk, v, dattn, attn, lse, s_len)
    dlo, dw_uq_t, dw_ukv, dqlg, dkvlg, dqng, dkng = _qkv_bwd(u, dq, dk, dv, cos, sin, q_lora_g, w_uq_t, kv_lora_g, g_ukv,
                                                            qn_g, kn_g, s_len)
    gx, dw_in_t, dmod, dng = _in_bwd(ctx[0], x[0], modsel, norm_g, dlo, dga, dgp, dpool, dxn, w_in_t)

    dmod_l = jnp.concatenate([dmod[1, 0], dmod[1, 1], dgate[0]]).reshape(4, kw)
    dmod_c = jnp.concatenate([dmod[0, 0], dmod[0, 1], jnp.zeros((D,), F32)]).reshape(4, kw)
    dmod8 = jnp.concatenate([dmod_l, dmod_c], axis=0)
    r_out, r_in, r_uq, r_ukv, g_ng, g_qlg, g_kvlg, g_qng, g_kng, g_ps, loss_all, g_wp, g_w_mod, g_b_mod, g_c_ctx = _reduce(
        [dw_out.reshape(4, D // 4, D), dw_in_t.reshape(4, DIN // 4, D), dw_uq_t, dw_ukv], split,
        [dng, dqlg, dkvlg, dqng, dkng, dps, loss], dw_pool, dmod8, a16, w_mod[0], c_ctx2)
    g2d = dict(c_ctx=g_c_ctx, b_mod=g_b_mod, w_mod=g_w_mod, w_in=r_in, w_uq=r_uq, w_ukv=r_ukv, w_out=r_out, norm_g=g_ng,
               q_lora_g=g_qlg, kv_lora_g=g_kvlg, q_norm_g=g_qng, k_norm_g=g_kng, pool_scale=g_ps, w_pool=g_wp.reshape(512, 128))

    outs = _adamw_many([as2d(n, weights[n]) for n in order], [g2d[n] for n in order], [as2d(n, m_in[n]) for n in order],
                       [as2d(n, v_in[n]) for n in order])
    d2d, m2d, v2d, g2d = (dict(zip(order, arrs)) for arrs in outs)

    return (loss_all[0, 0], gx[None], *[back(n, g2d[n]) for n in order], *[back(n, d2d[n]) for n in order],
            *[back(n, m2d[n]) for n in order], *[back(n, v2d[n]) for n in order])
```

```python
import jax
import jax.numpy as jnp
import numpy as np
from jax import lax
from jax.experimental import pallas as pl
from jax.experimental.pallas import tpu as pltpu

F32 = jnp.float32
BF16 = jnp.bfloat16
MESH = pl.DeviceIdType.MESH

D = 1024
NH = 4
DK = 192
DKP = 256
DV = 128
QL = 256
KVL = 128
DIN = 1984
U_LO = 448
SEG = ((0, 512), (448, 960), (960, 1472), (1472, 1984))
DU = 2048
POOL_WINDOWS = (2, 4, 8, 16)
HALO = 8
EPS = 1e-6
ROPE_BASE = 10000.0
GRID_W = 64
Q_BLOCK = 128
TB = 256
BWD_QBLOCKS = 1
SCALE = DK ** -0.5
LOG2E = 1.4426950408889634
LN2 = 0.6931471805599453
VMEM_LIMIT = 56 * 1024 * 1024

ADAM_LR = 0.001
ADAM_B1 = 0.9
ADAM_B2 = 0.999
ADAM_EPS = 1e-08
ADAM_WD = 0.01
ADAM_STEP = 10

CHIPS3 = ((1, 0), (0, 1), (1, 1))
PEERS7 = tuple((dx, dy, dc) for dx in (0, 1) for dy in (0, 1) for dc in (0, 1) if (dx, dy, dc) != (0, 0, 0))

VM = pl.BlockSpec(memory_space=pltpu.VMEM)
ANY = pl.BlockSpec(memory_space=pl.ANY)


def _nn(a, b):
    return jnp.dot(a, b, preferred_element_type=F32)


def _nt(a, b):
    return lax.dot_general(a, b, (((1,), (1,)), ((), ())), preferred_element_type=F32)


def _tn(a, b):
    return lax.dot_general(a, b, (((0,), (0,)), ((), ())), preferred_element_type=F32)


def _split3(a):
    a0 = a.astype(BF16)
    r = a - a0.astype(F32)
    a1 = r.astype(BF16)
    a2 = (r - a1.astype(F32)).astype(BF16)
    return a0, a1, a2


def _dot3(dot, a, b):
    sa = _split3(a)
    sb = _split3(b)
    out = None
    for i in range(3):
        for j in range(3 - i):
            t = dot(sa[i], sb[j])
            out = t if out is None else out + t
    return out


def _sig(x):
    return 1.0 / (1.0 + jnp.exp(-x))


def _rot(t):
    src = lax.broadcasted_iota(jnp.int32, (128, 128), 0)
    dst = lax.broadcasted_iota(jnp.int32, (128, 128), 1)
    first = (dst % 32) < 16
    perm = jnp.where(first & (src == dst + 16), -1.0, jnp.where(~first & (src == dst - 16), 1.0, 0.0)).astype(BF16)
    hi = t.astype(BF16)
    lo = (t - hi.astype(F32)).astype(BF16)
    return _nn(hi, perm) + _nn(lo, perm)


def _rope(t, cos, sin):
    return t * cos + _rot(t) * sin


def _rope_t(t, cos, sin):
    return t * cos - _rot(t * sin)


def _rope_block(rows_ref, cols_ref, is_ctx):
    lane = lax.broadcasted_iota(jnp.int32, (TB, 256), 1) % 128
    rows = jnp.concatenate([jnp.broadcast_to(rows_ref[0, r:r + 1, :], (GRID_W, 256)) for r in range(TB // GRID_W)], axis=0)
    cs = jnp.where(lane < 32, rows, cols_ref[...])
    return jnp.where(is_ctx, 1.0, cs[:, :128]), jnp.where(is_ctx, 0.0, cs[:, 128:])


def _shift_rows(z, k):
    n = z.shape[0]
    return pltpu.roll(z, (n - k) % n, 0)


def _colsum(a):
    return jnp.sum(a, axis=0, keepdims=True)


def _rowsum(a):
    return jnp.sum(a, axis=-1, keepdims=True)


def _row_layout(col):
    return jnp.transpose(jnp.broadcast_to(col, (col.shape[0], 128)))[0:8, :]


def _params(sem=None):
    return pltpu.CompilerParams(dimension_semantics=sem, vmem_limit_bytes=VMEM_LIMIT)


def _full(shape):
    nd = len(shape)
    return pl.BlockSpec(shape, lambda *_: (0,) * nd)


def _peer(x, y, c, off):
    dx, dy, dc = off
    return ((x + dx) % 2, (y + dy) % 2, (c + dc) % 2)


def _token_specs(off):
    ctx = pl.BlockSpec((TB, D), lambda i: (jnp.minimum(i, off - 1), 0))
    lat = pl.BlockSpec((TB, D), lambda i: (jnp.maximum(i - off, 0), 0))
    mod = pl.BlockSpec((1, 3, D), lambda i: (jnp.minimum(i // off, 1), 0, 0))
    return ctx, lat, mod


def _modulated(x, mod_ref, ng):
    shift = mod_ref[0, 0:1, :]
    scale = mod_ref[0, 1:2, :]
    r = lax.rsqrt(jnp.mean(x * x, axis=-1, keepdims=True) + EPS)
    xh = x * r
    xg = xh * ng
    return r, xh, xg, xg * (1.0 + scale) + shift, scale


def _fwd_in(ctx, x, modsel, norm_g, w_in_t, q_lora_g, w_uq_t, kv_lora_g, w_ukv, qn_g, kn_g, cos, sin):
    s_len, lc = x.shape[0], ctx.shape[0]
    t_all = s_len + lc
    nb = t_all // TB
    off = lc // TB

    def body(ctx_ref, x_ref, mod_ref, ng_ref, win_ref, qlg_ref, wuq_ref, kvlg_ref, wukv_ref, qng_ref, kng_ref, cos_ref, sin_ref,
             u_ref, q_ref, k_ref, v_ref):
        is_ctx = pl.program_id(0) < off
        xb = jnp.where(is_ctx, ctx_ref[...], x_ref[...])
        _, _, _, h, _ = _modulated(xb, mod_ref, ng_ref[...])
        hb = h.astype(BF16)
        lane = lax.broadcasted_iota(jnp.int32, (TB, 512), 1)
        ulo = jnp.where(lane < U_LO, _nt(hb, win_ref[SEG[0][0]:SEG[0][1], :]), 0.0)
        u_ref[:, 0:512] = ulo
        for j in range(1, 4):
            u_ref[:, j * 512:(j + 1) * 512] = _nt(hb, win_ref[SEG[j][0]:SEG[j][1], :])
        cos, sin = _rope_block(cos_ref, sin_ref, is_ctx)
        cq = ulo[:, 0:QL]
        cqn = (cq * lax.rsqrt(jnp.mean(cq * cq, axis=-1, keepdims=True) + EPS) * qlg_ref[...]).astype(BF16)
        qng = qng_ref[...]
        ckv = ulo[:, QL:QL + KVL]
        ckvn = (ckv * lax.rsqrt(jnp.mean(ckv * ckv, axis=-1, keepdims=True) + EPS) * kvlg_ref[...]).astype(BF16)
        qhs = [_nt(cqn, wuq_ref[hd]) for hd in range(NH)]
        kvs = [_nn(ckvn, wukv_ref[hd]) for hd in range(NH)]
        for hd in range(NH):
            qh = qhs[hd]
            qn = qh * lax.rsqrt(_rowsum(qh * qh) / DK + EPS) * qng
            q_ref[hd] = (jnp.concatenate([qn[:, :128], _rope(qn[:, 128:], cos, sin)], axis=1) * (SCALE * LOG2E)).astype(BF16)
        kr = ulo[:, 384:512]
        skr = _rowsum(kr * kr)
        kng = kng_ref[...]
        kr_roped = _rope(kr * kng[:, 128:], cos, sin)
        for hd in range(NH):
            kv = kvs[hd]
            kn = kv[:, :128]
            rk = lax.rsqrt((_rowsum(kn * kn) + skr) / DK + EPS)
            k_ref[hd] = jnp.concatenate([kn * rk * kng[:, :128], kr_roped * rk], axis=1).astype(BF16)
            v_ref[hd] = kv[:, 128:].astype(BF16)

    row = lambda w: pl.BlockSpec((TB, w), lambda i: (i, 0))
    heads = lambda w: pl.BlockSpec((NH, TB, w), lambda i: (0, i, 0))
    cspec, xspec, mspec = _token_specs(off)
    return pl.pallas_call(
        body, name="fwd_in", grid=(nb,),
        in_specs=[cspec, xspec, mspec, _full((1, D)), _full((DIN, D)), _full((1, QL)), _full((NH, DKP, QL)), _full((1, KVL)),
                  _full((NH, KVL, 256)), _full((1, DKP)), _full((1, DKP)),
                  pl.BlockSpec((1, 8, 256), lambda i: (jnp.maximum(i - off, 0), 0, 0)), _full((TB, 256))],
        out_specs=[row(DU), heads(DKP), heads(DKP), heads(DV)],
        out_shape=[jax.ShapeDtypeStruct((t_all, DU), F32), jax.ShapeDtypeStruct((NH, t_all, DKP), BF16),
                   jax.ShapeDtypeStruct((NH, t_all, DKP), BF16), jax.ShapeDtypeStruct((NH, t_all, DV), BF16)],
        compiler_params=_params(("arbitrary",)),
    )(ctx, x, modsel, norm_g, w_in_t, q_lora_g, w_uq_t, kv_lora_g, w_ukv, qn_g, kn_g, cos, sin)


def _attn_fwd(q, k, v, s_len):
    t_all = q.shape[1]
    off = (t_all - s_len) // TB
    nq = s_len // TB
    nsub = next(n for n in (4, 2, 1) if nq % n == 0)

    def body(*refs):
        q_refs = refs[:nsub]
        k_ref, v_ref, o_ref, lse_ref = refs[nsub:]
        for sb in range(nsub):
            s = _nt(q_refs[sb][0], k_ref[0])
            m = jnp.max(s, axis=-1, keepdims=True)
            e = jnp.exp2(s - m)
            l = _rowsum(e)
            o_ref[sb * TB:(sb + 1) * TB, :] = _nn(e.astype(BF16), v_ref[0]) / l
            lse_ref[0, sb] = _row_layout(m + jnp.log2(l))

    qspec = lambda sb: pl.BlockSpec((1, TB, DKP), lambda h, i: (h, i * nsub + sb + off, 0))
    return pl.pallas_call(
        body, name="attn_fwd", grid=(NH, nq // nsub),
        in_specs=[qspec(sb) for sb in range(nsub)]
        + [pl.BlockSpec((1, t_all, DKP), lambda h, i: (h, 0, 0)), pl.BlockSpec((1, t_all, DV), lambda h, i: (h, 0, 0))],
        out_specs=[pl.BlockSpec((nsub * TB, DV), lambda h, i: (i, h)), pl.BlockSpec((1, nsub, 8, TB), lambda h, i: (h, i, 0, 0))],
        out_shape=[jax.ShapeDtypeStruct((s_len, NH * DV), F32), jax.ShapeDtypeStruct((NH, nq, 8, TB), F32)],
        compiler_params=_params(("arbitrary", "arbitrary")),
    )(*([q] * nsub), k, v)


def _out_stage(attn, u, x, target, modsel, w_pool, pool_scale, w_out, lc):
    s_len = x.shape[0]
    t_all = s_len + lc
    off = lc // TB
    nq = s_len // TB
    hb = TB // HALO
    nqb = s_len // Q_BLOCK
    jb = TB // nqb

    def body(attn_ref, ga_ref, pin_ref, pprev_ref, pnext_ref, gp_ref, x_ref, tgt_ref, gate_ref, wp_ref, ps_ref, wo_ref,
             dxn_ref, dattn_ref, dga_ref, dgp_ref, dpool_ref, dwo_ref, dgate_ref, dps_ref, dwp_ref, loss_ref):
        i = pl.program_id(0)

        @pl.when(i == 0)
        def _():
            dwo_ref[...] = jnp.zeros_like(dwo_ref)
            dgate_ref[...] = jnp.zeros_like(dgate_ref)
            dps_ref[...] = jnp.zeros_like(dps_ref)
            dwp_ref[...] = jnp.zeros_like(dwp_ref)
            loss_ref[...] = jnp.zeros_like(loss_ref)

        attn = jnp.concatenate([attn_ref[:, jj, :] for jj in range(jb)], axis=0)
        ga = ga_ref[...]
        gp = gp_ref[...]
        pin = pin_ref[...]
        prev = jnp.where(i == 0, 0.0, pprev_ref[...])
        nxt = jnp.where(i == nq - 1, 0.0, pnext_ref[...])
        win = jnp.concatenate([prev, pin, nxt], axis=0)
        tg = i * TB + lax.broadcasted_iota(jnp.int32, (TB, 1), 0)
        pooled = []
        for g, w in enumerate(POOL_WINDOWS):
            a = win[:, g * 128:(g + 1) * 128]
            p = _shift_rows(a, -1) + a
            for step in (1, 2, 4):
                if w >= 4 * step:
                    p = _shift_rows(p, -step) + _shift_rows(p, step)
            cnt = (jnp.minimum(tg + w // 2, s_len) - jnp.maximum(tg - w // 2, 0)).astype(F32)
            pooled.append(p[HALO:HALO + TB] / cnt - a[HALO:HALO + TB])
        pooled_b = [p.astype(BF16) for p in pooled]
        wp = [wp_ref[g].astype(BF16) for g in range(4)]
        z = jnp.concatenate([_nn(pooled_b[g], wp[g]) for g in range(4)], axis=1)
        ps = ps_ref[...]
        yp = z * ps
        sga = _sig(ga)
        sila = ga * sga
        sgp = _sig(gp)
        silp = gp * sgp
        br = jnp.concatenate([sila * attn, silp * yp], axis=1).astype(BF16)
        y = _nn(br, wo_ref[...])
        gate = gate_ref[0, 2:3, :]
        err = x_ref[...] + gate * y - tgt_ref[...]
        loss_ref[...] += _colsum(_rowsum(err * err)) * (0.5 / D)
        dxn = err * (1.0 / D)
        dxn_ref[...] = dxn
        dgate_ref[...] += _colsum(dxn * y)
        dy = (dxn * gate).astype(BF16)
        dwo_ref[...] += _tn(br, dy)
        dbr = _nt(dy, wo_ref[...])
        dbra = dbr[:, :512]
        dbrp = dbr[:, 512:]
        dattn = dbra * sila
        for jj in range(jb):
            dattn_ref[:, jj, :] = dattn[jj * nqb:(jj + 1) * nqb]
        dga_ref[...] = (dbra * attn * (sga * (1.0 + ga * (1.0 - sga)))).astype(BF16)
        dgp_ref[...] = (dbrp * yp * (sgp * (1.0 + gp * (1.0 - sgp)))).astype(BF16)
        dyp = dbrp * silp
        dps_ref[...] += _colsum(dyp * z)
        dz = (dyp * ps).astype(BF16)
        dpool = []
        for g in range(4):
            dzg = dz[:, g * 128:(g + 1) * 128]
            dwp_ref[g] += _tn(pooled_b[g], dzg)
            dpool.append(_nt(dzg, wp[g]))
        dpool_ref[...] = jnp.concatenate(dpool, axis=1)

    lat = lambda w: pl.BlockSpec((TB, w), lambda i: (i, 0))
    perm = pl.BlockSpec((nqb, jb, 512), lambda i: (0, i, 0))
    ucol = lambda j: pl.BlockSpec((TB, 512), lambda i: (i + off, j))
    last8 = t_all // HALO - 1
    return pl.pallas_call(
        body, name="out_stage", grid=(nq,),
        in_specs=[perm, ucol(1), ucol(2),
                  pl.BlockSpec((HALO, 512), lambda i: ((i + off) * hb - 1, 2)),
                  pl.BlockSpec((HALO, 512), lambda i: (jnp.minimum((i + off + 1) * hb, last8), 2)),
                  ucol(3), lat(D), lat(D), pl.BlockSpec((1, 3, D), lambda i: (1, 0, 0)), _full((4, 128, 128)), _full((1, 512)),
                  _full((D, D))],
        out_specs=[lat(D), perm, lat(512), lat(512), lat(512),
                   _full((D, D)), _full((1, D)), _full((1, 512)), _full((4, 128, 128)), _full((1, 1))],
        out_shape=[jax.ShapeDtypeStruct((s_len, D), F32), jax.ShapeDtypeStruct((nqb, Q_BLOCK, 512), F32),
                   jax.ShapeDtypeStruct((s_len, 512), BF16), jax.ShapeDtypeStruct((s_len, 512), BF16),
                   jax.ShapeDtypeStruct((s_len, 512), F32),
                   jax.ShapeDtypeStruct((D, D), F32), jax.ShapeDtypeStruct((1, D), F32), jax.ShapeDtypeStruct((1, 512), F32),
                   jax.ShapeDtypeStruct((4, 128, 128), F32), jax.ShapeDtypeStruct((1, 1), F32)],
        compiler_params=_params(("arbitrary",)),
    )(attn, u, u, u, u, u, x, target, modsel, w_pool, pool_scale, w_out)


def _attn_bwd(q, k, v, dattn, attn, lse, s_len):
    t_all = q.shape[1]
    off = (t_all - s_len) // TB
    nq = s_len // TB
    nch = 4
    chunks = [(c * (t_all // nch), t_all // nch) for c in range(nch)]
    nsub = next(n for n in (BWD_QBLOCKS, 2, 1) if nq % n == 0)
    tq = nsub * TB

    def body(*refs):
        q_refs = refs[:nsub]
        k_ref, v_ref, do_ref, o_ref, lse_ref, dq_ref, dk_ref, dv_ref = refs[nsub:]
        i = pl.program_id(1)

        @pl.when(i == 0)
        def _():
            dk_ref[...] = jnp.zeros_like(dk_ref)
            dv_ref[...] = jnp.zeros_like(dv_ref)

        qb = jnp.concatenate([r[0] for r in q_refs], axis=0)
        delta_r = _row_layout(_rowsum(do_ref[...] * o_ref[...]))[0:1, :]
        do = do_ref[...].astype(BF16)
        lse_r = jnp.concatenate([lse_ref[0, sb][0:1, :] for sb in range(nsub)], axis=1)
        dq = jnp.zeros((tq, DKP), F32)
        for start, size in chunks:
            rows = pl.ds(start, size)
            kc = k_ref[0, rows, :]
            p_t = jnp.exp2(_nt(kc, qb) - lse_r)
            ds_t = (p_t * (_nt(v_ref[0, rows, :], do) - delta_r)).astype(BF16)
            dv_ref[0, rows, :] += _nn(p_t.astype(BF16), do)
            dk_ref[0, rows, :] += _nn(ds_t, qb)
            dq += _tn(ds_t, kc)
        dq_ref[0] = dq * SCALE

    kvspec = lambda w: pl.BlockSpec((1, t_all, w), lambda h, i: (h, 0, 0))
    rowspec = pl.BlockSpec((1, nsub, 8, TB), lambda h, i: (h, i, 0, 0))
    qspec = lambda sb: pl.BlockSpec((1, TB, DKP), lambda h, i: (h, i * nsub + sb + off, 0))
    return pl.pallas_call(
        body, name="attn_bwd", grid=(NH, nq // nsub),
        in_specs=[qspec(sb) for sb in range(nsub)]
        + [kvspec(DKP), kvspec(DV), pl.BlockSpec((tq, DV), lambda h, i: (i, h)), pl.BlockSpec((tq, DV), lambda h, i: (i, h)),
           rowspec],
        out_specs=[pl.BlockSpec((1, tq, DKP), lambda h, i: (h, i, 0)), kvspec(DKP), kvspec(DV)],
        out_shape=[jax.ShapeDtypeStruct((NH, s_len, DKP), F32), jax.ShapeDtypeStruct((NH, t_all, DKP), F32),
                   jax.ShapeDtypeStruct((NH, t_all, DV), F32)],
        compiler_params=_params(("arbitrary", "arbitrary")),
    )(*([q] * nsub), k, v, dattn, attn, lse)


def _qkv_bwd(u, dq, dk, dv, cos, sin, q_lora_g, w_uq_t, kv_lora_g, w_ukv, qn_g, kn_g, s_len):
    t_all = u.shape[0]
    off = (t_all - s_len) // TB
    nb = t_all // TB

    def body(ulo_ref, dq_ref, dk_ref, dv_ref, cos_ref, sin_ref, qlg_ref, wuq_ref, kvlg_ref, wukv_ref, qng_ref, kng_ref,
             dlo_ref, dwuq_ref, dwukv_ref, dqlg_ref, dkvlg_ref, dqng_ref, dkng_ref):
        i = pl.program_id(0)

        @pl.when(i == 0)
        def _():
            for r in (dwuq_ref, dwukv_ref, dqlg_ref, dkvlg_ref, dqng_ref, dkng_ref):
                r[...] = jnp.zeros_like(r)

        latent = i >= off
        ulo = ulo_ref[...]
        cos, sin = _rope_block(cos_ref, sin_ref, pl.program_id(0) < off)
        cq = ulo[:, 0:QL]
        rc = lax.rsqrt(jnp.mean(cq * cq, axis=-1, keepdims=True) + EPS)
        cqh = cq * rc
        qlg = qlg_ref[...]
        cqn_b = (cqh * qlg).astype(BF16)
        qng = qng_ref[...]
        ckv = ulo[:, QL:QL + KVL]
        r0 = lax.rsqrt(jnp.mean(ckv * ckv, axis=-1, keepdims=True) + EPS)
        ckvh = ckv * r0
        kvlg = kvlg_ref[...]
        ckvn_b = (ckvh * kvlg).astype(BF16)
        qhs = [_nt(cqn_b, wuq_ref[hd]) for hd in range(NH)]
        kns = [_nn(ckvn_b, wukv_ref[hd])[:, :128] for hd in range(NH)]
        dqng = jnp.zeros((1, DKP), F32)
        dqraws = []
        for hd in range(NH):
            qh = qhs[hd]
            rq = lax.rsqrt(_rowsum(qh * qh) / DK + EPS)
            xh = qh * rq
            dqh = jnp.where(latent, dq_ref[hd], 0.0)
            dyq = jnp.concatenate([dqh[:, :128], _rope_t(dqh[:, 128:], cos, sin)], axis=1)
            dqng += _colsum(dyq * xh)
            dxh = dyq * qng
            dqraws.append((rq * (dxh - xh * (_rowsum(dxh * xh) / DK))).astype(BF16))
        dqng_ref[...] += dqng

        kr = ulo[:, 384:512]
        skr = _rowsum(kr * kr)
        kng = kng_ref[...]
        dkr = jnp.zeros((TB, 128), F32)
        dkng = jnp.zeros((1, DKP), F32)
        dkvs = []
        for hd in range(NH):
            kn = kns[hd]
            rk = lax.rsqrt((_rowsum(kn * kn) + skr) / DK + EPS)
            xh1 = kn * rk
            xh2 = kr * rk
            dkh = dk_ref[hd] * LN2
            d1 = dkh[:, :128]
            d2 = _rope_t(dkh[:, 128:], cos, sin)
            dkng += jnp.concatenate([_colsum(d1 * xh1), _colsum(d2 * xh2)], axis=1)
            dx1 = d1 * kng[:, :128]
            dx2 = d2 * kng[:, 128:]
            dot = (_rowsum(dx1 * xh1) + _rowsum(dx2 * xh2)) / DK
            dkvs.append(jnp.concatenate([rk * (dx1 - xh1 * dot), dv_ref[hd]], axis=1).astype(BF16))
            dkr += rk * (dx2 - xh2 * dot)
        dkng_ref[...] += dkng

        dcqn = jnp.zeros((TB, QL), F32)
        dckvn = jnp.zeros((TB, KVL), F32)
        for hd in range(NH):
            dwuq_ref[hd] += _tn(dqraws[hd], cqn_b)[:DK]
            dcqn += _nn(dqraws[hd], wuq_ref[hd])
            dwukv_ref[hd] += _tn(ckvn_b, dkvs[hd])
            dckvn += _nt(dkvs[hd], wukv_ref[hd])
        dqlg_ref[...] += _colsum(dcqn * cqh)
        dxh = dcqn * qlg
        dcq = rc * (dxh - cqh * jnp.mean(dxh * cqh, axis=-1, keepdims=True))
        dkvlg_ref[...] += _colsum(dckvn * ckvh)
        dxh = dckvn * kvlg
        dckv = r0 * (dxh - ckvh * jnp.mean(dxh * ckvh, axis=-1, keepdims=True))
        dlo_ref[...] = jnp.concatenate([dcq, dckv, dkr], axis=1).astype(BF16)

    row = lambda w: pl.BlockSpec((TB, w), lambda i: (i, 0))
    heads = lambda w: pl.BlockSpec((NH, TB, w), lambda i: (0, i, 0))
    return pl.pallas_call(
        body, name="qkv_bwd", grid=(nb,),
        in_specs=[row(512), pl.BlockSpec((NH, TB, DKP), lambda i: (0, jnp.maximum(i - off, 0), 0)), heads(DKP), heads(DV),
                  pl.BlockSpec((1, 8, 256), lambda i: (jnp.maximum(i - off, 0), 0, 0)), _full((TB, 256)), _full((1, QL)), _full((NH, DKP, QL)), _full((1, KVL)), _full((NH, KVL, 256)),
                  _full((1, DKP)), _full((1, DKP))],
        out_specs=[row(512), _full((NH, DK, QL)), _full((NH, KVL, 256)), _full((1, QL)), _full((1, KVL)),
                   _full((1, DKP)), _full((1, DKP))],
        out_shape=[jax.ShapeDtypeStruct((t_all, 512), BF16), jax.ShapeDtypeStruct((NH, DK, QL), F32),
                   jax.ShapeDtypeStruct((NH, KVL, 256), F32), jax.ShapeDtypeStruct((1, QL), F32),
                   jax.ShapeDtypeStruct((1, KVL), F32), jax.ShapeDtypeStruct((1, DKP), F32), jax.ShapeDtypeStruct((1, DKP), F32)],
        compiler_params=_params(("arbitrary",)),
    )(u, dq, dk, dv, cos, sin, q_lora_g, w_uq_t, kv_lora_g, w_ukv, qn_g, kn_g)


def _in_bwd(ctx, x, modsel, norm_g, dlo, dga, dgp, dpool, dxn, w_in_t):
    s_len, lc = x.shape[0], ctx.shape[0]
    t_all = s_len + lc
    off = lc // TB
    nb = t_all // TB
    nq = s_len // TB
    hb = TB // HALO
    n = TB + 2 * HALO

    def body(ctx_ref, x_ref, mod_ref, ng_ref, dlo_ref, dga_ref, dgp_ref, dp_ref, dpprev_ref, dpnext_ref, dxn_ref, win_ref,
             gx_ref, dwin_ref, dmod_ref, dng_ref):
        i = pl.program_id(0)
        j = i - off

        @pl.when(i == 0)
        def _():
            dwin_ref[...] = jnp.zeros_like(dwin_ref)
            dmod_ref[...] = jnp.zeros_like(dmod_ref)
            dng_ref[...] = jnp.zeros_like(dng_ref)

        latent = i >= off
        dp = dp_ref[...]
        prev = jnp.where(j <= 0, 0.0, dpprev_ref[...])
        nxt = jnp.where(j >= nq - 1, 0.0, dpnext_ref[...])
        win = jnp.concatenate([prev, dp, nxt], axis=0)
        tg = j * TB - HALO + lax.broadcasted_iota(jnp.int32, (n, 1), 0)
        dpin = []
        for g, w in enumerate(POOL_WINDOWS):
            cnt = jnp.maximum(jnp.minimum(tg + w // 2, s_len) - jnp.maximum(tg - w // 2, 0), 1).astype(F32)
            zq = win[:, g * 128:(g + 1) * 128] / cnt
            zq = zq + _shift_rows(zq, 1)
            for step in (1, 2, 4):
                if w >= 4 * step:
                    zq = _shift_rows(zq, -step) + _shift_rows(zq, step)
            dpin.append(zq[HALO:HALO + TB] - dp[:, g * 128:(g + 1) * 128])
        zero = jnp.zeros((TB, 512), BF16)
        du = [dlo_ref[...], jnp.where(latent, dga_ref[...], zero),
              jnp.where(latent, jnp.concatenate(dpin, axis=1).astype(BF16), zero), jnp.where(latent, dgp_ref[...], zero)]

        ng = ng_ref[...]
        xb = jnp.where(i < off, ctx_ref[...], x_ref[...])
        r, xh, xg, h, scale = _modulated(xb, mod_ref, ng)
        hb_ = h.astype(BF16)
        dh = jnp.zeros((TB, D), F32)
        for s, (lo, hi) in enumerate(SEG):
            dwin_ref[lo:hi, :] += _tn(du[s], hb_)
            dh += _nn(du[s], win_ref[lo:hi, :])
        is_lat = latent.astype(F32)
        dsh = _colsum(dh)
        dsc = _colsum(dh * xg)
        dmod_ref[0, 0:1, :] += dsh * (1.0 - is_lat)
        dmod_ref[0, 1:2, :] += dsc * (1.0 - is_lat)
        dmod_ref[1, 0:1, :] += dsh * is_lat
        dmod_ref[1, 1:2, :] += dsc * is_lat
        dxg = dh * (1.0 + scale)
        dng_ref[...] += _colsum(dxg * xh)
        dxh = dxg * ng
        gx_ref[...] = r * (dxh - xh * jnp.mean(dxh * xh, axis=-1, keepdims=True)) + dxn_ref[...]

    row = lambda w: pl.BlockSpec((TB, w), lambda i: (i, 0))
    lat = lambda w: pl.BlockSpec((TB, w), lambda i: (jnp.maximum(i - off, 0), 0))
    last8 = s_len // HALO - 1
    cspec, xspec, mspec = _token_specs(off)
    return pl.pallas_call(
        body, name="in_bwd", grid=(nb,),
        in_specs=[cspec, xspec, mspec, _full((1, D)), row(512), lat(512), lat(512), lat(512),
                  pl.BlockSpec((HALO, 512), lambda i: (jnp.maximum(jnp.maximum(i - off, 0) * hb - 1, 0), 0)),
                  pl.BlockSpec((HALO, 512), lambda i: (jnp.minimum((jnp.maximum(i - off, 0) + 1) * hb, last8), 0)),
                  lat(D), _full((DIN, D))],
        out_specs=[lat(D), _full((DIN, D)), _full((2, 2, D)), _full((1, D))],
        out_shape=[jax.ShapeDtypeStruct((s_len, D), F32), jax.ShapeDtypeStruct((DIN, D), F32),
                   jax.ShapeDtypeStruct((2, 2, D), F32), jax.ShapeDtypeStruct((1, D), F32)],
        compiler_params=_params(("arbitrary",)),
    )(ctx, x, modsel, norm_g, dlo, dga, dgp, dpool, dpool, dpool, dxn, w_in_t)


def _adamw_update(w_ref, g_ref, m_ref, v_ref, d_ref, mo_ref, vo_ref):
    gv = g_ref[...]
    mn = ADAM_B1 * m_ref[...] + (1.0 - ADAM_B1) * gv
    vn = ADAM_B2 * v_ref[...] + (1.0 - ADAM_B2) * (gv * gv)
    m_hat = mn / (1.0 - ADAM_B1 ** ADAM_STEP)
    v_hat = vn / (1.0 - ADAM_B2 ** ADAM_STEP)
    d_ref[...] = -ADAM_LR * (m_hat / (jnp.sqrt(v_hat) + ADAM_EPS) + ADAM_WD * w_ref[...])
    mo_ref[...] = mn
    vo_ref[...] = vn


def _adamw_many(ws, gs, ms, vs):
    n = len(ws)
    parts = 4

    def body(*refs):
        for i in range(n):
            _adamw_update(refs[i], refs[n + i], refs[2 * n + i], refs[3 * n + i], refs[4 * n + i], refs[5 * n + i], refs[6 * n + i])
            refs[7 * n + i][...] = refs[n + i][...]

    def spec(w):
        rows, cols = w.shape
        if rows % (8 * parts) == 0:
            return pl.BlockSpec((rows // parts, cols), lambda i: (i, 0))
        if cols % (128 * parts) == 0:
            return pl.BlockSpec((rows, cols // parts), lambda i: (0, i))
        return _full((rows, cols))

    specs = [spec(w) for w in ws]
    shp = [jax.ShapeDtypeStruct(w.shape, F32) for w in ws]
    out = pl.pallas_call(body, name="adamw_many", grid=(parts,), in_specs=specs * 4, out_specs=specs * 4, out_shape=shp * 4,
                         compiler_params=_params(("arbitrary",)))(*ws, *gs, *ms, *vs)
    return out[:n], out[n:2 * n], out[2 * n:3 * n], out[3 * n:]


class _Links:
    def __init__(self, send_sems, recv_sems):
        self.send_sems, self.recv_sems, self.sends = send_sems, recv_sems, []

    def send(self, src, dst, sem, to):
        cp = pltpu.make_async_remote_copy(src, dst, self.send_sems.at[sem], self.recv_sems.at[sem], device_id=to,
                                          device_id_type=MESH)
        cp.start()
        self.sends.append(cp)

    def arrived(self, dst, sem, frm):
        pltpu.make_async_remote_copy(dst, dst, self.send_sems.at[sem], self.recv_sems.at[sem], device_id=frm,
                                     device_id_type=MESH).wait_recv()

    def drain(self):
        for cp in self.sends:
            cp.wait_send()


def _half(ref, c, axis):
    size = ref.shape[axis - 2] // 2
    win = pl.ds(pl.multiple_of(c * size, 16 if axis == 0 else 128), size)
    idx = (win, slice(None)) if axis == 0 else (slice(None), win)
    return ref.at[(slice(None),) * (len(ref.shape) - 2) + idx]


def _select_rows(slots_ref, n_slots, row=0):
    sub = lax.broadcasted_iota(jnp.int32, (8, 1), 0)
    out = None
    for d in range(n_slots):
        r = jnp.where(sub == d, jnp.broadcast_to(slots_ref[d][row:row + 1, :], (8, slots_ref.shape[-1])), 0.0)
        out = r if out is None else out + r
    return out


def _gather(c, c_ctx, w_mod, b_mod, shards, axes, slab_rows):
    nw = len(shards)
    kw = w_mod.shape[1]

    def body(*refs):
        c_ref, cc_ref, wm_hbm, b_ref = refs[:4]
        w_hbm = refs[4:4 + nw]
        a16_ref, modsel_ref = refs[4 + nw:6 + nw]
        out_refs = refs[6 + nw:6 + 2 * nw]
        g_refs = refs[6 + 2 * nw:6 + 3 * nw]
        f_refs = refs[6 + 3 * nw:6 + 4 * nw]
        wm_ref, a_ref, mod_ref, send_sems, recv_sems, local_sems = refs[6 + 4 * nw:]
        loads = [pltpu.make_async_copy(w_hbm[wi], f_refs[wi], local_sems.at[wi]) for wi in range(nw)]
        loads.append(pltpu.make_async_copy(wm_hbm, wm_ref, local_sems.at[nw]))
        for cp in loads:
            cp.start()
        stores = []

        def slab(wi, chip, of=g_refs):
            return of[wi].at[chip].at[0:shards[wi].shape[0]]

        def store(src, dst, wi, slot):
            cp = pltpu.make_async_copy(src, dst, local_sems.at[nw + 1 + wi * 8 + slot])
            cp.start()
            stores.append(cp)

        def store_half(wi, chip, half, slot):
            store(_half(slab(wi, chip), half, axes[wi]), _half(slab(wi, chip, out_refs), half, axes[wi]), wi, slot)
        x, y, cc = lax.axis_index("x"), lax.axis_index("y"), lax.axis_index("c")
        me = 4 * x + 2 * y + cc
        k = 2 * x + y
        sibling = (x, y, 1 - cc)
        links = _Links(send_sems, recv_sems)
        chips = [_peer(x, y, cc, off + (0,)) for off in CHIPS3]
        chip_a = ((x + 1 - cc) % 2, (y + cc) % 2, cc)
        chip_b = ((x + cc) % 2, (y + 1 - cc) % 2, cc)
        chip_d = (1 - x, 1 - y, cc)
        cv = c_ref[...]
        sc = cv * _sig(cv)
        mine = a_ref.at[me]
        for r in range(8):
            mine[r:r + 1, :] = sc[:, r * 128:(r + 1) * 128]
        for j, off in enumerate(PEERS7):
            links.send(a_ref.at[me], a_ref.at[me], j, _peer(x, y, cc, off))
        for wi in range(nw):
            loads[wi].wait()
            slab(wi, k)[...] = f_refs[wi][...].astype(BF16)
            for j, to in enumerate((chip_a, chip_b)):
                links.send(_half(slab(wi, k), cc, axes[wi]), _half(slab(wi, k), cc, axes[wi]), 10 + wi * 6 + j, to)
            store(slab(wi, k), slab(wi, k, out_refs), wi, 0)
            rows = shards[wi].shape[0]
            pad = slab_rows[wi] - rows
            if pad:
                for kk in range(4):
                    g_refs[wi][kk, rows:, :] = jnp.zeros((pad, shards[wi].shape[1]), BF16)
                store(g_refs[wi].at[:, pl.ds(rows, pad), :], out_refs[wi].at[:, pl.ds(rows, pad), :], wi, 7)
        for j, off in enumerate(PEERS7):
            px, py, pc = _peer(x, y, cc, off)
            links.arrived(a_ref.at[4 * px + 2 * py + pc], j, (px, py, pc))
        ccv = cc_ref[...]
        sub = lax.broadcasted_iota(jnp.int32, (8, 1), 0)
        top = jnp.zeros((8, D), F32)
        for d in range(8):
            blk = a_ref[d]
            row = jnp.concatenate([blk[r:r + 1, :] for r in range(8)], axis=1)
            top = top + jnp.where(sub == d, jnp.broadcast_to(row, (8, D)), 0.0)
        a16 = jnp.concatenate([top, jnp.where(sub == 0, jnp.broadcast_to(ccv * _sig(ccv), (8, D)), 0.0)], axis=0)
        a16_ref[...] = a16
        loads[nw].wait()
        b_k = jnp.zeros((1, kw), F32)
        for kk in range(4):
            b_k = b_k + jnp.where(k == kk, b_ref[:, kk * kw:(kk + 1) * kw], 0.0)
        mod_ref[k] = _dot3(_nn, a16, wm_ref[...]) + b_k
        for j, to in enumerate(chips):
            links.send(mod_ref.at[k], mod_ref.at[k], 7 + j, to)

        def over_ici(j, frm, origin):
            for wi in range(nw):
                blk = _half(slab(wi, 2 * origin[0] + origin[1]), cc, axes[wi])
                links.arrived(blk, 10 + wi * 6 + j, frm)
                if j == 0:
                    links.send(blk, blk, 10 + wi * 6 + 2, chip_b)
                links.send(blk, blk, 10 + wi * 6 + 3 + j, sibling)
                store_half(wi, 2 * origin[0] + origin[1], cc, 1 + j)

        def from_sibling(j, origin):
            for wi in range(nw):
                links.arrived(_half(slab(wi, 2 * origin[0] + origin[1]), 1 - cc, axes[wi]), 10 + wi * 6 + 3 + j, sibling)
                store_half(wi, 2 * origin[0] + origin[1], 1 - cc, 4 + j)
        over_ici(0, chip_a, chip_a)
        over_ici(1, chip_b, chip_b)
        from_sibling(0, chip_b)
        over_ici(2, chip_b, chip_d)
        from_sibling(1, chip_a)
        from_sibling(2, chip_d)
        for j, (px, py, pc) in enumerate(chips):
            links.arrived(mod_ref.at[2 * px + py], 7 + j, (px, py, pc))
        for sel, row in ((0, 8), (1, me)):
            for kk in range(4):
                piece = mod_ref[kk, pl.ds(row, 1), :]
                lo = kk * kw
                while lo < (kk + 1) * kw:
                    r, col = divmod(lo, D)
                    n = min((kk + 1) * kw - lo, D - col)
                    modsel_ref[sel, r:r + 1, col:col + n] = piece[:, lo - kk * kw:lo - kk * kw + n]
                    lo += n
        links.drain()
        for cp in stores:
            cp.wait()

    nsem = 10 + 6 * nw
    gathered = [(4, r, s.shape[1]) for r, s in zip(slab_rows, shards)]
    return pl.pallas_call(
        body, name="gather", in_specs=[VM, VM, ANY, VM] + [ANY] * nw, out_specs=[VM, VM] + [ANY] * nw,
        out_shape=[jax.ShapeDtypeStruct((16, D), F32), jax.ShapeDtypeStruct((2, 3, D), F32)]
        + [jax.ShapeDtypeStruct(g, BF16) for g in gathered],
        scratch_shapes=[pltpu.VMEM(g, BF16) for g in gathered] + [pltpu.VMEM(s.shape, F32) for s in shards]
        + [pltpu.VMEM(w_mod.shape, F32), pltpu.VMEM((8, 8, D // 8), F32), pltpu.VMEM((4, 16, kw), F32),
           pltpu.SemaphoreType.DMA((nsem,)),
           pltpu.SemaphoreType.DMA((nsem,)), pltpu.SemaphoreType.DMA((nw + 1 + 8 * nw,))],
        compiler_params=pltpu.CompilerParams(vmem_limit_bytes=VMEM_LIMIT),
    )(c, c_ctx, w_mod, b_mod, *shards)


SMALL_ROW_WIDTHS = (D, QL, KVL, DKP, DKP, 512, 128)
SMALL_OUT_WIDTHS = (D, QL, KVL, DK, DK, 512, 1)
SMALL_PACK = 384


def _small_pieces():
    pieces = []
    for i, w in enumerate(SMALL_ROW_WIDTHS):
        for c0 in range(0, w, 128):
            j = len(pieces)
            pieces.append((i, c0, j // (SMALL_PACK // 128), j % (SMALL_PACK // 128) * 128))
    assert len(pieces) <= 8 * (SMALL_PACK // 128)
    return pieces


def _reduce(grads, axes, smalls, w_pool_g, dmod, dgate, a16, w_mod, c_ctx):
    nw = len(grads)
    ns = len(smalls)
    kw = w_mod.shape[1]
    halves = []
    for g, ax in zip(grads, axes):
        halves.append((g.shape[1] // 2, g.shape[2]) if ax == 0 else (g.shape[1], g.shape[2] // 2))

    def body(*refs):
        g_refs = refs[:nw]
        small_refs = refs[nw:nw + ns]
        wp_ref, dmod_ref, dgate_ref, a16_ref, wm_hbm, cc_ref = refs[nw + ns:nw + ns + 6]
        o = nw + ns + 6
        r_outs = refs[o:o + nw]
        small_outs = refs[o + nw:o + nw + ns]
        rwp_ref, gw_out, gb_ref, gc_ref = refs[o + nw + ns:o + nw + ns + 4]
        o = o + nw + ns + 4
        own, sib, part, got, rel, r_refs = (refs[o + i * nw:o + (i + 1) * nw] for i in range(6))
        smbuf, wps, wpg, dm_all, pc_all, wm_ref, gw_ref, send_sems, recv_sems, local_sems = refs[o + 6 * nw:]
        wm_load = pltpu.make_async_copy(wm_hbm, wm_ref, local_sems.at[nw])
        wm_load.start()
        stores = []

        def store(src, dst, sem):
            cp = pltpu.make_async_copy(src, dst, local_sems.at[nw + 1 + sem])
            cp.start()
            stores.append(cp)
        x, y, cc = lax.axis_index("x"), lax.axis_index("y"), lax.axis_index("c")
        me = 4 * x + 2 * y + cc
        k = 2 * x + y
        sibling = (x, y, 1 - cc)
        links = _Links(send_sems, recv_sems)
        chips = [_peer(x, y, cc, off + (0,)) for off in CHIPS3]
        peers = [_peer(x, y, cc, off) for off in PEERS7]
        chip_a = ((x + 1 - cc) % 2, (y + cc) % 2, cc)
        chip_b = ((x + cc) % 2, (y + 1 - cc) % 2, cc)
        ka, kb, kd = 2 * chip_a[0] + chip_a[1], 2 * chip_b[0] + chip_b[1], 2 * (1 - x) + (1 - y)
        big, sm0, wp0, dm0, pc0 = 0, 5 * nw, 5 * nw + 7, 5 * nw + 14, 5 * nw + 21

        locals_ = []
        for wi in range(nw):
            lc = pltpu.make_async_copy(_half(g_refs[wi], cc, axes[wi]), own[wi], local_sems.at[wi])
            lc.start()
            locals_.append(lc)
            links.send(_half(g_refs[wi], 1 - cc, axes[wi]), sib[wi], big + wi * 5, sibling)
        slot = smbuf.at[me]
        slot[...] = jnp.zeros((8, SMALL_PACK), F32)
        small_rows = [jnp.broadcast_to(ref[...], (1, w)) for ref, w in zip(small_refs, SMALL_ROW_WIDTHS)]
        for i, c0, row, lane in _small_pieces():
            slot[row:row + 1, lane:lane + 128] = small_rows[i][:, c0:c0 + 128]
        links.send(wp_ref, wps, wp0, sibling)
        dm_mine = dm_all.at[me]
        dm_mine[...] = jnp.zeros((8, kw), F32)
        dm_rows = ((0, (dmod_ref[1, 0:1, :], dmod_ref[1, 1:2, :], dgate_ref[...])), (4, (dmod_ref[0, 0:1, :], dmod_ref[0, 1:2, :])))
        for base, rows in dm_rows:
            for r, v in enumerate(rows):
                lo = r * D
                while lo < (r + 1) * D:
                    kk, col = divmod(lo, kw)
                    n = min((r + 1) * D - lo, kw - col)
                    dm_mine[base + kk:base + kk + 1, col:col + n] = v[:, lo - r * D:lo - r * D + n]
                    lo += n
        for j, peer in enumerate(peers):
            links.send(dm_all.at[me], dm_all.at[me], dm0 + j, peer)
            links.send(smbuf.at[me], smbuf.at[me], sm0 + j, peer)
        links.arrived(wps, wp0, sibling)
        wpg[k] = (wp_ref[...] + wps[...]).astype(BF16)
        for j, to in enumerate(chips):
            links.send(wpg.at[k], wpg.at[k], wp0 + 1 + j, to)
        for wi in range(nw):
            locals_[wi].wait()
            links.arrived(sib[wi], big + wi * 5, sibling)
            part[wi][...] = (own[wi][...] + sib[wi][...]).astype(BF16)
            got[wi][k] = part[wi][k]
            got[wi][kd] = jnp.zeros(halves[wi], BF16)
            links.send(part[wi].at[kd], rel[wi], big + wi * 5 + 1, chip_b)
            links.send(part[wi].at[kb], got[wi].at[k], big + wi * 5 + 2, chip_b)
        for j, (px, py, pc) in enumerate(peers):
            links.arrived(dm_all.at[4 * px + 2 * py + pc], dm0 + j, (px, py, pc))
        dm_tot = dm_all[0]
        for d in range(1, 8):
            dm_tot = dm_tot + dm_all[d]
        for kk in range(4):
            gb_ref[:, kk * kw:(kk + 1) * kw] = dm_tot[kk:kk + 1, :] + dm_tot[4 + kk:5 + kk, :]
        top = jnp.zeros((8, kw), F32)
        dmc_k = jnp.zeros((1, kw), F32)
        for kk in range(4):
            top = top + jnp.where(k == kk, _select_rows(dm_all, 8, kk), 0.0)
            dmc_k = dmc_k + jnp.where(k == kk, dm_tot[4 + kk:5 + kk, :], 0.0)
        sub = lax.broadcasted_iota(jnp.int32, (8, 1), 0)
        bk = jnp.concatenate([top, jnp.where(sub == 0, jnp.broadcast_to(dmc_k, (8, kw)), 0.0)], axis=0)
        gw_ref[...] = _dot3(_tn, a16_ref[...], bk)
        store(gw_ref, gw_out, 0)
        wm_load.wait()
        pc_all[k] =_dot3(_nt, jnp.broadcast_to(bk[8:9, :], (8, kw)), wm_ref[...])
        for j, to in enumerate(chips):
            links.send(pc_all.at[k], pc_all.at[k], pc0 + j, to)
        for j, (px, py, pc) in enumerate(peers):
            links.arrived(smbuf.at[4 * px + 2 * py + pc], sm0 + j, (px, py, pc))
        tot = smbuf[0]
        for d in range(1, 8):
            tot = tot + smbuf[d]
        for i, c0, row, lane in _small_pieces():
            n = min(128, SMALL_OUT_WIDTHS[i] - c0)
            if n > 0:
                small_outs[i][:, c0:c0 + n] = tot[row:row + 1, lane:lane + n]
        for j, (px, py, pc) in enumerate(chips):
            links.arrived(wpg.at[2 * px + py], wp0 + 1 + j, (px, py, pc))
        wpt = wpg[0].astype(F32)
        for kk in range(1, 4):
            wpt = wpt + wpg[kk].astype(F32)
        rwp_ref[...] = wpt
        for wi in range(nw):
            links.arrived(rel[wi], big + wi * 5 + 1, chip_b)
            rel[wi][...] = (part[wi][ka].astype(F32) + rel[wi][...].astype(F32)).astype(BF16)
            links.send(rel[wi], got[wi].at[k], big + wi * 5 + 3, chip_a)
        for wi in range(nw):
            links.arrived(got[wi].at[kb], big + wi * 5 + 2, chip_b)
            links.arrived(got[wi].at[ka], big + wi * 5 + 3, chip_a)
            total = got[wi][0].astype(F32)
            for kk in range(1, 4):
                total = total + got[wi][kk].astype(F32)
            mine = _half(r_refs[wi], cc, axes[wi])
            mine[...] = total
            links.send(mine, mine, big + wi * 5 + 4, sibling)
            store(mine, _half(r_outs[wi], cc, axes[wi]), 1 + wi)
        for j, (px, py, pc) in enumerate(chips):
            links.arrived(pc_all.at[2 * px + py], pc0 + j, (px, py, pc))
        ccv = cc_ref[...]
        sg = _sig(ccv)
        gc_ref[...] = (pc_all[0][0:1, :] + pc_all[1][0:1, :] + pc_all[2][0:1, :] + pc_all[3][0:1, :]) * (sg * (1.0 + ccv * (1.0 - sg)))
        for wi in range(nw):
            links.arrived(_half(r_refs[wi], 1 - cc, axes[wi]), big + wi * 5 + 4, sibling)
            store(_half(r_refs[wi], 1 - cc, axes[wi]), _half(r_outs[wi], 1 - cc, axes[wi]), 1 + nw + wi)
        links.drain()
        for cp in stores:
            cp.wait()

    nsem = 5 * nw + 24
    quads = [(4,) + h for h in halves]
    return pl.pallas_call(
        body, name="reduce", in_specs=[ANY] * nw + [VM] * (ns + 4) + [ANY, VM],
        out_specs=[ANY] * nw + [VM] * (ns + 1) + [ANY, VM, VM],
        out_shape=[jax.ShapeDtypeStruct(g.shape[1:], F32) for g in grads]
        + [jax.ShapeDtypeStruct((1, w), F32) for w in SMALL_OUT_WIDTHS]
        + [jax.ShapeDtypeStruct(w_pool_g.shape, F32), jax.ShapeDtypeStruct((D, kw), F32), jax.ShapeDtypeStruct((1, 3 * D), F32),
           jax.ShapeDtypeStruct((1, D), F32)],
        scratch_shapes=[pltpu.VMEM(q, F32) for q in quads] + [pltpu.VMEM(q, F32) for q in quads]
        + [pltpu.VMEM(q, BF16) for q in quads] + [pltpu.VMEM(q, BF16) for q in quads] + [pltpu.VMEM(h, BF16) for h in halves]
        + [pltpu.VMEM(g.shape[1:], F32) for g in grads]
        + [pltpu.VMEM((8, 8, SMALL_PACK), F32), pltpu.VMEM(w_pool_g.shape, F32), pltpu.VMEM((4,) + w_pool_g.shape, BF16),
           pltpu.VMEM((8, 8, kw), F32), pltpu.VMEM((4, 8, D), F32), pltpu.VMEM((D, kw), F32), pltpu.VMEM((D, kw), F32)]
        + [pltpu.SemaphoreType.DMA((nsem,)), pltpu.SemaphoreType.DMA((nsem,)), pltpu.SemaphoreType.DMA((3 * nw + 2,))],
        compiler_params=pltpu.CompilerParams(vmem_limit_bytes=VMEM_LIMIT),
    )(*grads, *smalls, w_pool_g, dmod, dgate, a16, w_mod, c_ctx)


def _rope_tables(s_len):
    rows = s_len // GRID_W
    per = TB // GRID_W
    n_freq = 16
    f32 = np.float32
    inv = f32(ROPE_BASE) ** (-np.arange(n_freq, dtype=f32) / f32(n_freq))
    ang_r = np.arange(rows, dtype=f32)[:, None] * inv
    ang_c = np.arange(GRID_W, dtype=f32)[:, None] * inv
    by_row, by_col = [], []
    for fn, pad in ((np.cos, 1.0), (np.sin, 0.0)):
        r = np.concatenate([fn(ang_r), fn(ang_r), np.zeros((rows, 96), f32)], axis=1).reshape(rows // per, per, 128)
        by_row.append(np.pad(r, ((0, 0), (0, 8 - per), (0, 0))))
        cpart = np.concatenate([np.zeros((GRID_W, 32), f32), fn(ang_c), fn(ang_c), np.full((GRID_W, 64), pad, f32)], axis=1)
        by_col.append(np.tile(cpart, (per, 1)))
    return jnp.asarray(np.concatenate(by_row, axis=-1), F32), jnp.asarray(np.concatenate(by_col, axis=-1), F32)


def kernel(x, c, ctx, c_ctx, w_mod, b_mod, norm_g, w_in, q_lora_g, w_uq, kv_lora_g, w_ukv, q_norm_g, k_norm_g, w_pool, pool_scale, w_out, loss_target, m_c_ctx, m_w_mod, m_b_mod, m_norm_g, m_w_in, m_q_lora_g, m_w_uq, m_kv_lora_g, m_w_ukv, m_q_norm_g, m_k_norm_g, m_w_pool, m_pool_scale, m_w_out, v_c_ctx, v_w_mod, v_b_mod, v_norm_g, v_w_in, v_q_lora_g, v_w_uq, v_kv_lora_g, v_w_ukv, v_q_norm_g, v_k_norm_g, v_w_pool, v_pool_scale, v_w_out):
    s_len = x.shape[1]
    lc = ctx.shape[1]
    weights = dict(c_ctx=c_ctx, w_mod=w_mod, b_mod=b_mod, norm_g=norm_g, w_in=w_in, q_lora_g=q_lora_g, w_uq=w_uq,
                   kv_lora_g=kv_lora_g, w_ukv=w_ukv, q_norm_g=q_norm_g, k_norm_g=k_norm_g, w_pool=w_pool,
                   pool_scale=pool_scale, w_out=w_out)
    m_in = dict(c_ctx=m_c_ctx, w_mod=m_w_mod, b_mod=m_b_mod, norm_g=m_norm_g, w_in=m_w_in, q_lora_g=m_q_lora_g, w_uq=m_w_uq,
                kv_lora_g=m_kv_lora_g, w_ukv=m_w_ukv, q_norm_g=m_q_norm_g, k_norm_g=m_k_norm_g, w_pool=m_w_pool,
                pool_scale=m_pool_scale, w_out=m_w_out)
    v_in = dict(c_ctx=v_c_ctx, w_mod=v_w_mod, b_mod=v_b_mod, norm_g=v_norm_g, w_in=v_w_in, q_lora_g=v_q_lora_g, w_uq=v_w_uq,
                kv_lora_g=v_kv_lora_g, w_ukv=v_w_ukv, q_norm_g=v_q_norm_g, k_norm_g=v_k_norm_g, w_pool=v_w_pool,
                pool_scale=v_pool_scale, w_out=v_w_out)
    order = ["c_ctx", "w_mod", "b_mod", "norm_g", "w_in", "q_lora_g", "w_uq", "kv_lora_g", "w_ukv", "q_norm_g", "k_norm_g",
             "w_pool", "pool_scale", "w_out"]
    transposed = ("w_in", "w_uq")
    as2d = lambda n, a: jnp.transpose(a[0]) if n in transposed else a.reshape(-1, a.shape[-1])
    back = lambda n, a: jnp.transpose(a)[None] if n in transposed else a.reshape(weights[n].shape)

    c_ctx2 = c_ctx.reshape(1, D)
    split = (0, 1, 0, 0)
    a16, modsel, g_out, g_in, g_uq, g_ukv = _gather(
        c, c_ctx2, w_mod[0], b_mod, [w_out[0], as2d("w_in", w_in), as2d("w_uq", w_uq), w_ukv[0]], split,
        (D // 4, DIN // 4, DKP, KVL))
    w_in_t = g_in.reshape(DIN, D)
    w_uq_t = g_uq
    w_out_f = g_out.reshape(D, D)
    qn_g = jnp.pad(q_norm_g, ((0, 0), (0, DKP - DK)))
    kn_g = jnp.pad(k_norm_g, ((0, 0), (0, DKP - DK)))
    cos, sin = _rope_tables(s_len)

    u, q, kk, v = _fwd_in(ctx[0], x[0], modsel, norm_g, w_in_t, q_lora_g, w_uq_t, kv_lora_g, g_ukv, qn_g, kn_g, cos, sin)
    attn, lse = _attn_fwd(q, kk, v, s_len)
    (dxn, dattn, dga, dgp, dpool, dw_out, dgate, dps, dw_pool, loss) = _out_stage(
        attn.reshape(s_len // Q_BLOCK, Q_BLOCK, NH * DV), u, x[0], loss_target[0], modsel, w_pool[0], pool_scale,
        w_out_f, lc)
    dattn = dattn.reshape(s_len, NH * DV)
    dq, dk, dv = _attn_bwd(q, kk, v, dattn, attn, lse, s_len)
    dlo, dw_uq_t, dw_ukv, dqlg, dkvlg, dqng, dkng = _qkv_bwd(u, dq, dk, dv, cos, sin, q_lora_g, w_uq_t, kv_lora_g, g_ukv,
                                                            qn_g, kn_g, s_len)
    gx, dw_in_t, dmod, dng = _in_bwd(ctx[0], x[0], modsel, norm_g, dlo, dga, dgp, dpool, dxn, w_in_t)

    r_out, r_in, r_uq, r_ukv, g_ng, g_qlg, g_kvlg, g_qng, g_kng, g_ps, loss_all, g_wp, g_w_mod, g_b_mod, g_c_ctx = _reduce(
        [dw_out.reshape(4, D // 4, D), dw_in_t.reshape(4, DIN // 4, D), dw_uq_t, dw_ukv], split,
        [dng, dqlg, dkvlg, dqng, dkng, dps, loss], dw_pool, dmod, dgate, a16, w_mod[0], c_ctx2)
    g2d = dict(c_ctx=g_c_ctx, b_mod=g_b_mod, w_mod=g_w_mod, w_in=r_in, w_uq=r_uq, w_ukv=r_ukv, w_out=r_out, norm_g=g_ng,
               q_lora_g=g_qlg, kv_lora_g=g_kvlg, q_norm_g=g_qng, k_norm_g=g_kng, pool_scale=g_ps, w_pool=g_wp.reshape(512, 128))

    outs = _adamw_many([as2d(n, weights[n]) for n in order], [g2d[n] for n in order], [as2d(n, m_in[n]) for n in order],
                       [as2d(n, v_in[n]) for n in order])
    d2d, m2d, v2d, g2d = (dict(zip(order, arrs)) for arrs in outs)

    return (loss_all[0, 0], gx[None], *[back(n, g2d[n]) for n in order], *[back(n, d2d[n]) for n in order],
            *[back(n, m2d[n]) for n in order], *[back(n, v2d[n]) for n in order])
```

```python
import jax
import jax.numpy as jnp
import numpy as np
from jax import lax
from jax.experimental import pallas as pl
from jax.experimental.pallas import tpu as pltpu

F32 = jnp.float32
BF16 = jnp.bfloat16
MESH = pl.DeviceIdType.MESH

D = 1024
NH = 4
DK = 192
DKP = 256
DV = 128
QL = 256
KVL = 128
DIN = 1984
U_LO = 448
SEG = ((0, 512), (448, 960), (960, 1472), (1472, 1984))
DU = 2048
POOL_WINDOWS = (2, 4, 8, 16)
HALO = 8
EPS = 1e-6
ROPE_BASE = 10000.0
GRID_W = 64
Q_BLOCK = 128
TB = 256
BWD_QBLOCKS = 1
SCALE = DK ** -0.5
LOG2E = 1.4426950408889634
LN2 = 0.6931471805599453
VMEM_LIMIT = 56 * 1024 * 1024

ADAM_LR = 0.001
ADAM_B1 = 0.9
ADAM_B2 = 0.999
ADAM_EPS = 1e-08
ADAM_WD = 0.01
ADAM_STEP = 10

CHIPS3 = ((1, 0), (0, 1), (1, 1))
PEERS7 = tuple((dx, dy, dc) for dx in (0, 1) for dy in (0, 1) for dc in (0, 1) if (dx, dy, dc) != (0, 0, 0))

VM = pl.BlockSpec(memory_space=pltpu.VMEM)
ANY = pl.BlockSpec(memory_space=pl.ANY)


def _nn(a, b):
    return jnp.dot(a, b, preferred_element_type=F32)


def _nt(a, b):
    return lax.dot_general(a, b, (((1,), (1,)), ((), ())), preferred_element_type=F32)


def _tn(a, b):
    return lax.dot_general(a, b, (((0,), (0,)), ((), ())), preferred_element_type=F32)


def _split3(a):
    a0 = a.astype(BF16)
    r = a - a0.astype(F32)
    a1 = r.astype(BF16)
    a2 = (r - a1.astype(F32)).astype(BF16)
    return a0, a1, a2


def _dot3(dot, a, b):
    sa = _split3(a)
    sb = _split3(b)
    out = None
    for i in range(3):
        for j in range(3 - i):
            t = dot(sa[i], sb[j])
            out = t if out is None else out + t
    return out


def _sig(x):
    return 1.0 / (1.0 + jnp.exp(-x))


def _rot(t):
    src = lax.broadcasted_iota(jnp.int32, (128, 128), 0)
    dst = lax.broadcasted_iota(jnp.int32, (128, 128), 1)
    first = (dst % 32) < 16
    perm = jnp.where(first & (src == dst + 16), -1.0, jnp.where(~first & (src == dst - 16), 1.0, 0.0)).astype(BF16)
    hi = t.astype(BF16)
    lo = (t - hi.astype(F32)).astype(BF16)
    return _nn(hi, perm) + _nn(lo, perm)


def _rope(t, cos, sin):
    return t * cos + _rot(t) * sin


def _rope_t(t, cos, sin):
    return t * cos - _rot(t * sin)


def _rope_block(rows_ref, cols_ref, is_ctx):
    lane = lax.broadcasted_iota(jnp.int32, (TB, 256), 1) % 128
    rows = jnp.concatenate([jnp.broadcast_to(rows_ref[0, r:r + 1, :], (GRID_W, 256)) for r in range(TB // GRID_W)], axis=0)
    cs = jnp.where(lane < 32, rows, cols_ref[...])
    return jnp.where(is_ctx, 1.0, cs[:, :128]), jnp.where(is_ctx, 0.0, cs[:, 128:])


def _shift_rows(z, k):
    n = z.shape[0]
    return pltpu.roll(z, (n - k) % n, 0)


def _colsum(a):
    return jnp.sum(a, axis=0, keepdims=True)


def _rowsum(a):
    return jnp.sum(a, axis=-1, keepdims=True)


def _row_layout(col):
    return jnp.transpose(jnp.broadcast_to(col, (col.shape[0], 128)))[0:8, :]


def _params(sem=None):
    return pltpu.CompilerParams(dimension_semantics=sem, vmem_limit_bytes=VMEM_LIMIT)


def _full(shape):
    nd = len(shape)
    return pl.BlockSpec(shape, lambda *_: (0,) * nd)


def _peer(x, y, c, off):
    dx, dy, dc = off
    return ((x + dx) % 2, (y + dy) % 2, (c + dc) % 2)


def _token_specs(off):
    ctx = pl.BlockSpec((TB, D), lambda i: (jnp.minimum(i, off - 1), 0))
    lat = pl.BlockSpec((TB, D), lambda i: (jnp.maximum(i - off, 0), 0))
    mod = pl.BlockSpec((1, 3, D), lambda i: (jnp.minimum(i // off, 1), 0, 0))
    return ctx, lat, mod


def _head_gains(qng_ref, kng_ref, pad_ref):
    pad_ref[...] = jnp.zeros((2, DKP), F32)
    pad_ref[0:1, 0:DK] = qng_ref[...]
    pad_ref[1:2, 0:DK] = kng_ref[...]
    return pad_ref[0:1, :], pad_ref[1:2, :]


def _modulated(x, mod_ref, ng):
    shift = mod_ref[0, 0:1, :]
    scale = mod_ref[0, 1:2, :]
    r = lax.rsqrt(jnp.mean(x * x, axis=-1, keepdims=True) + EPS)
    xh = x * r
    xg = xh * ng
    return r, xh, xg, xg * (1.0 + scale) + shift, scale


def _fwd_in(ctx, x, modsel, norm_g, w_in_t, q_lora_g, w_uq_t, kv_lora_g, w_ukv, qn_g, kn_g, cos, sin):
    s_len, lc = x.shape[0], ctx.shape[0]
    t_all = s_len + lc
    nb = t_all // TB
    off = lc // TB

    def body(ctx_ref, x_ref, mod_ref, ng_ref, win_ref, qlg_ref, wuq_ref, kvlg_ref, wukv_ref, qng_ref, kng_ref, cos_ref, sin_ref,
             u_ref, q_ref, k_ref, v_ref, pad_ref):
        is_ctx = pl.program_id(0) < off
        qng, kng = _head_gains(qng_ref, kng_ref, pad_ref)
        xb = jnp.where(is_ctx, ctx_ref[...], x_ref[...])
        _, _, _, h, _ = _modulated(xb, mod_ref, ng_ref[...])
        hb = h.astype(BF16)
        lane = lax.broadcasted_iota(jnp.int32, (TB, 512), 1)
        ulo = jnp.where(lane < U_LO, _nt(hb, win_ref[SEG[0][0]:SEG[0][1], :]), 0.0)
        u_ref[:, 0:512] = ulo
        for j in range(1, 4):
            u_ref[:, j * 512:(j + 1) * 512] = _nt(hb, win_ref[SEG[j][0]:SEG[j][1], :])
        cos, sin = _rope_block(cos_ref, sin_ref, is_ctx)
        cq = ulo[:, 0:QL]
        cqn = (cq * lax.rsqrt(jnp.mean(cq * cq, axis=-1, keepdims=True) + EPS) * qlg_ref[...]).astype(BF16)
        ckv = ulo[:, QL:QL + KVL]
        ckvn = (ckv * lax.rsqrt(jnp.mean(ckv * ckv, axis=-1, keepdims=True) + EPS) * kvlg_ref[...]).astype(BF16)
        qhs = [_nt(cqn, wuq_ref[hd]) for hd in range(NH)]
        kvs = [_nn(ckvn, wukv_ref[hd]) for hd in range(NH)]
        for hd in range(NH):
            qh = qhs[hd]
            qn = qh * lax.rsqrt(_rowsum(qh * qh) / DK + EPS) * qng
            q_ref[hd] = (jnp.concatenate([qn[:, :128], _rope(qn[:, 128:], cos, sin)], axis=1) * (SCALE * LOG2E)).astype(BF16)
        kr = ulo[:, 384:512]
        skr = _rowsum(kr * kr)
        kr_roped =_rope(kr * kng[:, 128:], cos, sin)
        for hd in range(NH):
            kv = kvs[hd]
            kn = kv[:, :128]
            rk = lax.rsqrt((_rowsum(kn * kn) + skr) / DK + EPS)
            k_ref[hd] = jnp.concatenate([kn * rk * kng[:, :128], kr_roped * rk], axis=1).astype(BF16)
            v_ref[hd] = kv[:, 128:].astype(BF16)

    row = lambda w: pl.BlockSpec((TB, w), lambda i: (i, 0))
    heads = lambda w: pl.BlockSpec((NH, TB, w), lambda i: (0, i, 0))
    cspec, xspec, mspec = _token_specs(off)
    return pl.pallas_call(
        body, name="fwd_in", grid=(nb,),
        in_specs=[cspec, xspec, mspec, _full((1, D)), _full((DIN, D)), _full((1, QL)), _full((NH, DKP, QL)), _full((1, KVL)),
                  _full((NH, KVL, 256)), _full((1, DK)), _full((1, DK)),
                  pl.BlockSpec((1, 8, 256), lambda i: (jnp.maximum(i - off, 0), 0, 0)), _full((TB, 256))],
        out_specs=[row(DU), heads(DKP), heads(DKP), heads(DV)],
        out_shape=[jax.ShapeDtypeStruct((t_all, DU), F32), jax.ShapeDtypeStruct((NH, t_all, DKP), BF16),
                   jax.ShapeDtypeStruct((NH, t_all, DKP), BF16), jax.ShapeDtypeStruct((NH, t_all, DV), BF16)],
        scratch_shapes=[pltpu.VMEM((2, DKP), F32)],
        compiler_params=_params(("arbitrary",)),
    )(ctx, x, modsel, norm_g, w_in_t, q_lora_g, w_uq_t, kv_lora_g, w_ukv, qn_g, kn_g, cos, sin)


def _attn_fwd(q, k, v, s_len):
    t_all = q.shape[1]
    off = (t_all - s_len) // TB
    nq = s_len // TB
    nsub = next(n for n in (4, 2, 1) if nq % n == 0)

    def body(*refs):
        q_refs = refs[:nsub]
        k_ref, v_ref, o_ref, lse_ref = refs[nsub:]
        for sb in range(nsub):
            s = _nt(q_refs[sb][0], k_ref[0])
            m = jnp.max(s, axis=-1, keepdims=True)
            e = jnp.exp2(s - m)
            l = _rowsum(e)
            o_ref[sb * TB:(sb + 1) * TB, :] = _nn(e.astype(BF16), v_ref[0]) / l
            lse_ref[0, sb] = _row_layout(m + jnp.log2(l))

    qspec = lambda sb: pl.BlockSpec((1, TB, DKP), lambda h, i: (h, i * nsub + sb + off, 0))
    return pl.pallas_call(
        body, name="attn_fwd", grid=(NH, nq // nsub),
        in_specs=[qspec(sb) for sb in range(nsub)]
        + [pl.BlockSpec((1, t_all, DKP), lambda h, i: (h, 0, 0)), pl.BlockSpec((1, t_all, DV), lambda h, i: (h, 0, 0))],
        out_specs=[pl.BlockSpec((nsub * TB, DV), lambda h, i: (i, h)), pl.BlockSpec((1, nsub, 8, TB), lambda h, i: (h, i, 0, 0))],
        out_shape=[jax.ShapeDtypeStruct((s_len, NH * DV), F32), jax.ShapeDtypeStruct((NH, nq, 8, TB), F32)],
        compiler_params=_params(("arbitrary", "arbitrary")),
    )(*([q] * nsub), k, v)


def _out_stage(attn, u, x, target, modsel, w_pool, pool_scale, w_out, lc):
    s_len = x.shape[0]
    t_all = s_len + lc
    off = lc // TB
    nq = s_len // TB
    hb = TB // HALO
    nqb = s_len // Q_BLOCK
    jb = TB // nqb

    def body(attn_ref, ga_ref, pin_ref, pprev_ref, pnext_ref, gp_ref, x_ref, tgt_ref, gate_ref, wp_ref, ps_ref, wo_ref,
             dxn_ref, dattn_ref, dga_ref, dgp_ref, dpool_ref, dwo_ref, dgate_ref, dps_ref, dwp_ref, loss_ref):
        i = pl.program_id(0)

        @pl.when(i == 0)
        def _():
            dwo_ref[...] = jnp.zeros_like(dwo_ref)
            dgate_ref[...] = jnp.zeros_like(dgate_ref)
            dps_ref[...] = jnp.zeros_like(dps_ref)
            dwp_ref[...] = jnp.zeros_like(dwp_ref)
            loss_ref[...] = jnp.zeros_like(loss_ref)

        attn = jnp.concatenate([attn_ref[:, jj, :] for jj in range(jb)], axis=0)
        ga = ga_ref[...]
        gp = gp_ref[...]
        pin = pin_ref[...]
        prev = jnp.where(i == 0, 0.0, pprev_ref[...])
        nxt = jnp.where(i == nq - 1, 0.0, pnext_ref[...])
        win = jnp.concatenate([prev, pin, nxt], axis=0)
        tg = i * TB + lax.broadcasted_iota(jnp.int32, (TB, 1), 0)
        pooled = []
        for g, w in enumerate(POOL_WINDOWS):
            a = win[:, g * 128:(g + 1) * 128]
            p = _shift_rows(a, -1) + a
            for step in (1, 2, 4):
                if w >= 4 * step:
                    p = _shift_rows(p, -step) + _shift_rows(p, step)
            cnt = (jnp.minimum(tg + w // 2, s_len) - jnp.maximum(tg - w // 2, 0)).astype(F32)
            pooled.append(p[HALO:HALO + TB] / cnt - a[HALO:HALO + TB])
        pooled_b = [p.astype(BF16) for p in pooled]
        wp = [wp_ref[g].astype(BF16) for g in range(4)]
        z = jnp.concatenate([_nn(pooled_b[g], wp[g]) for g in range(4)], axis=1)
        ps = ps_ref[...]
        yp = z * ps
        sga = _sig(ga)
        sila = ga * sga
        sgp = _sig(gp)
        silp = gp * sgp
        br = jnp.concatenate([sila * attn, silp * yp], axis=1).astype(BF16)
        y = _nn(br, wo_ref[...])
        gate = gate_ref[0, 2:3, :]
        err = x_ref[...] + gate * y - tgt_ref[...]
        loss_ref[...] += _colsum(_rowsum(err * err)) * (0.5 / D)
        dxn = err * (1.0 / D)
        dxn_ref[...] = dxn
        dgate_ref[...] += _colsum(dxn * y)
        dy = (dxn * gate).astype(BF16)
        dwo_ref[...] += _tn(br, dy)
        dbr = _nt(dy, wo_ref[...])
        dbra = dbr[:, :512]
        dbrp = dbr[:, 512:]
        dattn = dbra * sila
        for jj in range(jb):
            dattn_ref[:, jj, :] = dattn[jj * nqb:(jj + 1) * nqb]
        dga_ref[...] = (dbra * attn * (sga * (1.0 + ga * (1.0 - sga)))).astype(BF16)
        dgp_ref[...] = (dbrp * yp * (sgp * (1.0 + gp * (1.0 - sgp)))).astype(BF16)
        dyp = dbrp * silp
        dps_ref[...] += _colsum(dyp * z)
        dz = (dyp * ps).astype(BF16)
        dpool = []
        for g in range(4):
            dzg = dz[:, g * 128:(g + 1) * 128]
            dwp_ref[g] += _tn(pooled_b[g], dzg)
            dpool.append(_nt(dzg, wp[g]))
        dpool_ref[...] = jnp.concatenate(dpool, axis=1)

    lat = lambda w: pl.BlockSpec((TB, w), lambda i: (i, 0))
    perm = pl.BlockSpec((nqb, jb, 512), lambda i: (0, i, 0))
    ucol = lambda j: pl.BlockSpec((TB, 512), lambda i: (i + off, j))
    last8 = t_all // HALO - 1
    return pl.pallas_call(
        body, name="out_stage", grid=(nq,),
        in_specs=[perm, ucol(1), ucol(2),
                  pl.BlockSpec((HALO, 512), lambda i: ((i + off) * hb - 1, 2)),
                  pl.BlockSpec((HALO, 512), lambda i: (jnp.minimum((i + off + 1) * hb, last8), 2)),
                  ucol(3), lat(D), lat(D), pl.BlockSpec((1, 3, D), lambda i: (1, 0, 0)), _full((4, 128, 128)), _full((1, 512)),
                  _full((D, D))],
        out_specs=[lat(D), perm, lat(512), lat(512), lat(512),
                   _full((D, D)), _full((1, D)), _full((1, 512)), _full((4, 128, 128)), _full((1, 1))],
        out_shape=[jax.ShapeDtypeStruct((s_len, D), F32), jax.ShapeDtypeStruct((nqb, Q_BLOCK, 512), F32),
                   jax.ShapeDtypeStruct((s_len, 512), BF16), jax.ShapeDtypeStruct((s_len, 512), BF16),
                   jax.ShapeDtypeStruct((s_len, 512), F32),
                   jax.ShapeDtypeStruct((D, D), F32), jax.ShapeDtypeStruct((1, D), F32), jax.ShapeDtypeStruct((1, 512), F32),
                   jax.ShapeDtypeStruct((4, 128, 128), F32), jax.ShapeDtypeStruct((1, 1), F32)],
        compiler_params=_params(("arbitrary",)),
    )(attn, u, u, u, u, u, x, target, modsel, w_pool, pool_scale, w_out)


def _attn_bwd(q, k, v, dattn, attn, lse, s_len):
    t_all = q.shape[1]
    off = (t_all - s_len) // TB
    nq = s_len // TB
    nch = 4
    chunks = [(c * (t_all // nch), t_all // nch) for c in range(nch)]
    nsub = next(n for n in (BWD_QBLOCKS, 2, 1) if nq % n == 0)
    tq = nsub * TB

    def body(*refs):
        q_refs = refs[:nsub]
        k_ref, v_ref, do_ref, o_ref, lse_ref, dq_ref, dk_ref, dv_ref = refs[nsub:]
        i = pl.program_id(1)

        @pl.when(i == 0)
        def _():
            dk_ref[...] = jnp.zeros_like(dk_ref)
            dv_ref[...] = jnp.zeros_like(dv_ref)

        qb = jnp.concatenate([r[0] for r in q_refs], axis=0)
        delta_r = _row_layout(_rowsum(do_ref[...] * o_ref[...]))[0:1, :]
        do = do_ref[...].astype(BF16)
        lse_r = jnp.concatenate([lse_ref[0, sb][0:1, :] for sb in range(nsub)], axis=1)
        dq = jnp.zeros((tq, DKP), F32)
        for start, size in chunks:
            rows = pl.ds(start, size)
            kc = k_ref[0, rows, :]
            p_t = jnp.exp2(_nt(kc, qb) - lse_r)
            ds_t = (p_t * (_nt(v_ref[0, rows, :], do) - delta_r)).astype(BF16)
            dv_ref[0, rows, :] += _nn(p_t.astype(BF16), do)
            dk_ref[0, rows, :] += _nn(ds_t, qb)
            dq += _tn(ds_t, kc)
        dq_ref[0] = dq * SCALE

    kvspec = lambda w: pl.BlockSpec((1, t_all, w), lambda h, i: (h, 0, 0))
    rowspec = pl.BlockSpec((1, nsub, 8, TB), lambda h, i: (h, i, 0, 0))
    qspec = lambda sb: pl.BlockSpec((1, TB, DKP), lambda h, i: (h, i * nsub + sb + off, 0))
    return pl.pallas_call(
        body, name="attn_bwd", grid=(NH, nq // nsub),
        in_specs=[qspec(sb) for sb in range(nsub)]
        + [kvspec(DKP), kvspec(DV), pl.BlockSpec((tq, DV), lambda h, i: (i, h)), pl.BlockSpec((tq, DV), lambda h, i: (i, h)),
           rowspec],
        out_specs=[pl.BlockSpec((1, tq, DKP), lambda h, i: (h, i, 0)), kvspec(DKP), kvspec(DV)],
        out_shape=[jax.ShapeDtypeStruct((NH, s_len, DKP), F32), jax.ShapeDtypeStruct((NH, t_all, DKP), F32),
                   jax.ShapeDtypeStruct((NH, t_all, DV), F32)],
        compiler_params=_params(("arbitrary", "arbitrary")),
    )(*([q] * nsub), k, v, dattn, attn, lse)


def _qkv_bwd(u, dq, dk, dv, cos, sin, q_lora_g, w_uq_t, kv_lora_g, w_ukv, qn_g, kn_g, s_len):
    t_all = u.shape[0]
    off = (t_all - s_len) // TB
    nb = t_all // TB

    def body(ulo_ref, dq_ref, dk_ref, dv_ref, cos_ref, sin_ref, qlg_ref, wuq_ref, kvlg_ref, wukv_ref, qng_ref, kng_ref,
             dlo_ref, dwuq_ref, dwukv_ref, dqlg_ref, dkvlg_ref, dqng_ref, dkng_ref, pad_ref):
        i = pl.program_id(0)
        qng, kng = _head_gains(qng_ref, kng_ref, pad_ref)

        @pl.when(i == 0)
        def _():
            for r in (dwuq_ref, dwukv_ref, dqlg_ref, dkvlg_ref, dqng_ref, dkng_ref):
                r[...] = jnp.zeros_like(r)

        latent = i >= off
        ulo = ulo_ref[...]
        cos, sin = _rope_block(cos_ref, sin_ref, pl.program_id(0) < off)
        cq = ulo[:, 0:QL]
        rc = lax.rsqrt(jnp.mean(cq * cq, axis=-1, keepdims=True) + EPS)
        cqh = cq * rc
        qlg = qlg_ref[...]
        cqn_b = (cqh * qlg).astype(BF16)
        ckv = ulo[:, QL:QL + KVL]
        r0 = lax.rsqrt(jnp.mean(ckv * ckv, axis=-1, keepdims=True) + EPS)
        ckvh = ckv * r0
        kvlg = kvlg_ref[...]
        ckvn_b = (ckvh * kvlg).astype(BF16)
        qhs = [_nt(cqn_b, wuq_ref[hd]) for hd in range(NH)]
        kns = [_nn(ckvn_b, wukv_ref[hd])[:, :128] for hd in range(NH)]
        dqng = jnp.zeros((1, DKP), F32)
        dqraws = []
        for hd in range(NH):
            qh = qhs[hd]
            rq = lax.rsqrt(_rowsum(qh * qh) / DK + EPS)
            xh = qh * rq
            dqh = jnp.where(latent, dq_ref[hd], 0.0)
            dyq = jnp.concatenate([dqh[:, :128], _rope_t(dqh[:, 128:], cos, sin)], axis=1)
            dqng += _colsum(dyq * xh)
            dxh = dyq * qng
            dqraws.append((rq * (dxh - xh * (_rowsum(dxh * xh) / DK))).astype(BF16))
        dqng_ref[...] += dqng

        kr = ulo[:, 384:512]
        skr = _rowsum(kr * kr)
        dkr =jnp.zeros((TB, 128), F32)
        dkng = jnp.zeros((1, DKP), F32)
        dkvs = []
        for hd in range(NH):
            kn = kns[hd]
            rk = lax.rsqrt((_rowsum(kn * kn) + skr) / DK + EPS)
            xh1 = kn * rk
            xh2 = kr * rk
            dkh = dk_ref[hd] * LN2
            d1 = dkh[:, :128]
            d2 = _rope_t(dkh[:, 128:], cos, sin)
            dkng += jnp.concatenate([_colsum(d1 * xh1), _colsum(d2 * xh2)], axis=1)
            dx1 = d1 * kng[:, :128]
            dx2 = d2 * kng[:, 128:]
            dot = (_rowsum(dx1 * xh1) + _rowsum(dx2 * xh2)) / DK
            dkvs.append(jnp.concatenate([rk * (dx1 - xh1 * dot), dv_ref[hd]], axis=1).astype(BF16))
            dkr += rk * (dx2 - xh2 * dot)
        dkng_ref[...] += dkng

        dcqn = jnp.zeros((TB, QL), F32)
        dckvn = jnp.zeros((TB, KVL), F32)
        for hd in range(NH):
            dwuq_ref[hd] += _tn(dqraws[hd], cqn_b)[:DK]
            dcqn += _nn(dqraws[hd], wuq_ref[hd])
            dwukv_ref[hd] += _tn(ckvn_b, dkvs[hd])
            dckvn += _nt(dkvs[hd], wukv_ref[hd])
        dqlg_ref[...] += _colsum(dcqn * cqh)
        dxh = dcqn * qlg
        dcq = rc * (dxh - cqh * jnp.mean(dxh * cqh, axis=-1, keepdims=True))
        dkvlg_ref[...] += _colsum(dckvn * ckvh)
        dxh = dckvn * kvlg
        dckv = r0 * (dxh - ckvh * jnp.mean(dxh * ckvh, axis=-1, keepdims=True))
        dlo_ref[...] = jnp.concatenate([dcq, dckv, dkr], axis=1).astype(BF16)

    row = lambda w: pl.BlockSpec((TB, w), lambda i: (i, 0))
    heads = lambda w: pl.BlockSpec((NH, TB, w), lambda i: (0, i, 0))
    return pl.pallas_call(
        body, name="qkv_bwd", grid=(nb,),
        in_specs=[row(512), pl.BlockSpec((NH, TB, DKP), lambda i: (0, jnp.maximum(i - off, 0), 0)), heads(DKP), heads(DV),
                  pl.BlockSpec((1, 8, 256), lambda i: (jnp.maximum(i - off, 0), 0, 0)), _full((TB, 256)), _full((1, QL)), _full((NH, DKP, QL)), _full((1, KVL)), _full((NH, KVL, 256)),
                  _full((1, DK)), _full((1, DK))],
        out_specs=[row(512), _full((NH, DK, QL)), _full((NH, KVL, 256)), _full((1, QL)), _full((1, KVL)),
                   _full((1, DKP)), _full((1, DKP))],
        out_shape=[jax.ShapeDtypeStruct((t_all, 512), BF16), jax.ShapeDtypeStruct((NH, DK, QL), F32),
                   jax.ShapeDtypeStruct((NH, KVL, 256), F32), jax.ShapeDtypeStruct((1, QL), F32),
                   jax.ShapeDtypeStruct((1, KVL), F32), jax.ShapeDtypeStruct((1, DKP), F32), jax.ShapeDtypeStruct((1, DKP), F32)],
        scratch_shapes=[pltpu.VMEM((2, DKP), F32)],
        compiler_params=_params(("arbitrary",)),
    )(u, dq, dk, dv, cos, sin, q_lora_g, w_uq_t, kv_lora_g, w_ukv, qn_g, kn_g)


def _in_bwd(ctx, x, modsel, norm_g, dlo, dga, dgp, dpool, dxn, w_in_t):
    s_len, lc = x.shape[0], ctx.shape[0]
    t_all = s_len + lc
    off = lc // TB
    nb = t_all // TB
    nq = s_len // TB
    hb = TB // HALO
    n = TB + 2 * HALO

    def body(ctx_ref, x_ref, mod_ref, ng_ref, dlo_ref, dga_ref, dgp_ref, dp_ref, dpprev_ref, dpnext_ref, dxn_ref, win_ref,
             gx_ref, dwin_ref, dmod_ref, dng_ref):
        i = pl.program_id(0)
        j = i - off

        @pl.when(i == 0)
        def _():
            dwin_ref[...] = jnp.zeros_like(dwin_ref)
            dmod_ref[...] = jnp.zeros_like(dmod_ref)
            dng_ref[...] = jnp.zeros_like(dng_ref)

        latent = i >= off
        dp = dp_ref[...]
        prev = jnp.where(j <= 0, 0.0, dpprev_ref[...])
        nxt = jnp.where(j >= nq - 1, 0.0, dpnext_ref[...])
        win = jnp.concatenate([prev, dp, nxt], axis=0)
        tg = j * TB - HALO + lax.broadcasted_iota(jnp.int32, (n, 1), 0)
        dpin = []
        for g, w in enumerate(POOL_WINDOWS):
            cnt = jnp.maximum(jnp.minimum(tg + w // 2, s_len) - jnp.maximum(tg - w // 2, 0), 1).astype(F32)
            zq = win[:, g * 128:(g + 1) * 128] / cnt
            zq = zq + _shift_rows(zq, 1)
            for step in (1, 2, 4):
                if w >= 4 * step:
                    zq = _shift_rows(zq, -step) + _shift_rows(zq, step)
            dpin.append(zq[HALO:HALO + TB] - dp[:, g * 128:(g + 1) * 128])
        zero = jnp.zeros((TB, 512), BF16)
        du = [dlo_ref[...], jnp.where(latent, dga_ref[...], zero),
              jnp.where(latent, jnp.concatenate(dpin, axis=1).astype(BF16), zero), jnp.where(latent, dgp_ref[...], zero)]

        ng = ng_ref[...]
        xb = jnp.where(i < off, ctx_ref[...], x_ref[...])
        r, xh, xg, h, scale = _modulated(xb, mod_ref, ng)
        hb_ = h.astype(BF16)
        dh = jnp.zeros((TB, D), F32)
        for s, (lo, hi) in enumerate(SEG):
            dwin_ref[lo:hi, :] += _tn(du[s], hb_)
            dh += _nn(du[s], win_ref[lo:hi, :])
        is_lat = latent.astype(F32)
        dsh = _colsum(dh)
        dsc = _colsum(dh * xg)
        dmod_ref[0, 0:1, :] += dsh * (1.0 - is_lat)
        dmod_ref[0, 1:2, :] += dsc * (1.0 - is_lat)
        dmod_ref[1, 0:1, :] += dsh * is_lat
        dmod_ref[1, 1:2, :] += dsc * is_lat
        dxg = dh * (1.0 + scale)
        dng_ref[...] += _colsum(dxg * xh)
        dxh = dxg * ng
        gx_ref[...] = r * (dxh - xh * jnp.mean(dxh * xh, axis=-1, keepdims=True)) + dxn_ref[...]

    row = lambda w: pl.BlockSpec((TB, w), lambda i: (i, 0))
    lat = lambda w: pl.BlockSpec((TB, w), lambda i: (jnp.maximum(i - off, 0), 0))
    last8 = s_len // HALO - 1
    cspec, xspec, mspec = _token_specs(off)
    return pl.pallas_call(
        body, name="in_bwd", grid=(nb,),
        in_specs=[cspec, xspec, mspec, _full((1, D)), row(512), lat(512), lat(512), lat(512),
                  pl.BlockSpec((HALO, 512), lambda i: (jnp.maximum(jnp.maximum(i - off, 0) * hb - 1, 0), 0)),
                  pl.BlockSpec((HALO, 512), lambda i: (jnp.minimum((jnp.maximum(i - off, 0) + 1) * hb, last8), 0)),
                  lat(D), _full((DIN, D))],
        out_specs=[lat(D), _full((DIN, D)), _full((2, 2, D)), _full((1, D))],
        out_shape=[jax.ShapeDtypeStruct((s_len, D), F32), jax.ShapeDtypeStruct((DIN, D), F32),
                   jax.ShapeDtypeStruct((2, 2, D), F32), jax.ShapeDtypeStruct((1, D), F32)],
        compiler_params=_params(("arbitrary",)),
    )(ctx, x, modsel, norm_g, dlo, dga, dgp, dpool, dpool, dpool, dxn, w_in_t)


def _adamw_update(w_ref, g_ref, m_ref, v_ref, d_ref, mo_ref, vo_ref):
    gv = g_ref[...]
    mn = ADAM_B1 * m_ref[...] + (1.0 - ADAM_B1) * gv
    vn = ADAM_B2 * v_ref[...] + (1.0 - ADAM_B2) * (gv * gv)
    m_hat = mn / (1.0 - ADAM_B1 ** ADAM_STEP)
    v_hat = vn / (1.0 - ADAM_B2 ** ADAM_STEP)
    d_ref[...] = -ADAM_LR * (m_hat / (jnp.sqrt(v_hat) + ADAM_EPS) + ADAM_WD * w_ref[...])
    mo_ref[...] = mn
    vo_ref[...] = vn


def _adamw_many(ws, gs, ms, vs):
    n = len(ws)
    parts = 4

    def body(*refs):
        for i in range(n):
            _adamw_update(refs[i], refs[n + i], refs[2 * n + i], refs[3 * n + i], refs[4 * n + i], refs[5 * n + i], refs[6 * n + i])
            refs[7 * n + i][...] = refs[n + i][...]

    def spec(w):
        rows, cols = w.shape
        if rows % (8 * parts) == 0:
            return pl.BlockSpec((rows // parts, cols), lambda i: (i, 0))
        if cols % (128 * parts) == 0:
            return pl.BlockSpec((rows, cols // parts), lambda i: (0, i))
        return _full((rows, cols))

    specs = [spec(w) for w in ws]
    shp = [jax.ShapeDtypeStruct(w.shape, F32) for w in ws]
    out = pl.pallas_call(body, name="adamw_many", grid=(parts,), in_specs=specs * 4, out_specs=specs * 4, out_shape=shp * 4,
                         compiler_params=_params(("arbitrary",)))(*ws, *gs, *ms, *vs)
    return out[:n], out[n:2 * n], out[2 * n:3 * n], out[3 * n:]


class _Links:
    def __init__(self, send_sems, recv_sems):
        self.send_sems, self.recv_sems, self.sends = send_sems, recv_sems, []

    def send(self, src, dst, sem, to):
        cp = pltpu.make_async_remote_copy(src, dst, self.send_sems.at[sem], self.recv_sems.at[sem], device_id=to,
                                          device_id_type=MESH)
        cp.start()
        self.sends.append(cp)

    def arrived(self, dst, sem, frm):
        pltpu.make_async_remote_copy(dst, dst, self.send_sems.at[sem], self.recv_sems.at[sem], device_id=frm,
                                     device_id_type=MESH).wait_recv()

    def drain(self):
        for cp in self.sends:
            cp.wait_send()


def _half(ref, c, axis):
    size = ref.shape[axis - 2] // 2
    win = pl.ds(pl.multiple_of(c * size, 16 if axis == 0 else 128), size)
    idx = (win, slice(None)) if axis == 0 else (slice(None), win)
    return ref.at[(slice(None),) * (len(ref.shape) - 2) + idx]


def _select_rows(slots_ref, n_slots, row=0):
    sub = lax.broadcasted_iota(jnp.int32, (8, 1), 0)
    out = None
    for d in range(n_slots):
        r = jnp.where(sub == d, jnp.broadcast_to(slots_ref[d][row:row + 1, :], (8, slots_ref.shape[-1])), 0.0)
        out = r if out is None else out + r
    return out


def _gather(c, c_ctx, w_mod, b_mod, shards, axes, slab_rows):
    nw = len(shards)
    kw = w_mod.shape[1]

    def body(*refs):
        c_ref, cc_ref, wm_hbm, b_ref = refs[:4]
        w_hbm = refs[4:4 + nw]
        a16_ref, modsel_ref = refs[4 + nw:6 + nw]
        out_refs = refs[6 + nw:6 + 2 * nw]
        g_refs = refs[6 + 2 * nw:6 + 3 * nw]
        f_refs = refs[6 + 3 * nw:6 + 4 * nw]
        wm_ref, a_ref, mod_ref, send_sems, recv_sems, local_sems = refs[6 + 4 * nw:]
        loads = [pltpu.make_async_copy(w_hbm[wi], f_refs[wi], local_sems.at[wi]) for wi in range(nw)]
        loads.append(pltpu.make_async_copy(wm_hbm, wm_ref, local_sems.at[nw]))
        for cp in loads:
            cp.start()
        stores = []

        def slab(wi, chip, of=g_refs):
            return of[wi].at[chip].at[0:shards[wi].shape[0]]

        def store(src, dst, wi, slot):
            cp = pltpu.make_async_copy(src, dst, local_sems.at[nw + 1 + wi * 8 + slot])
            cp.start()
            stores.append(cp)

        def store_half(wi, chip, half, slot):
            store(_half(slab(wi, chip), half, axes[wi]), _half(slab(wi, chip, out_refs), half, axes[wi]), wi, slot)
        x, y, cc = lax.axis_index("x"), lax.axis_index("y"), lax.axis_index("c")
        me = 4 * x + 2 * y + cc
        k = 2 * x + y
        sibling = (x, y, 1 - cc)
        links = _Links(send_sems, recv_sems)
        chips = [_peer(x, y, cc, off + (0,)) for off in CHIPS3]
        chip_a = ((x + 1 - cc) % 2, (y + cc) % 2, cc)
        chip_b = ((x + cc) % 2, (y + 1 - cc) % 2, cc)
        chip_d = (1 - x, 1 - y, cc)
        cv = c_ref[...]
        sc = cv * _sig(cv)
        mine = a_ref.at[me]
        for r in range(8):
            mine[r:r + 1, :] = sc[:, r * 128:(r + 1) * 128]
        for j, off in enumerate(PEERS7):
            links.send(a_ref.at[me], a_ref.at[me], j, _peer(x, y, cc, off))
        for wi in range(nw):
            loads[wi].wait()
            slab(wi, k)[...] = f_refs[wi][...].astype(BF16)
            for j, to in enumerate((chip_a, chip_b)):
                links.send(_half(slab(wi, k), cc, axes[wi]), _half(slab(wi, k), cc, axes[wi]), 10 + wi * 6 + j, to)
            store(slab(wi, k), slab(wi, k, out_refs), wi, 0)
            rows = shards[wi].shape[0]
            pad = slab_rows[wi] - rows
            if pad:
                for kk in range(4):
                    g_refs[wi][kk, rows:, :] = jnp.zeros((pad, shards[wi].shape[1]), BF16)
                store(g_refs[wi].at[:, pl.ds(rows, pad), :], out_refs[wi].at[:, pl.ds(rows, pad), :], wi, 7)
        for j, off in enumerate(PEERS7):
            px, py, pc = _peer(x, y, cc, off)
            links.arrived(a_ref.at[4 * px + 2 * py + pc], j, (px, py, pc))
        ccv = cc_ref[...]
        sub = lax.broadcasted_iota(jnp.int32, (8, 1), 0)
        top = jnp.zeros((8, D), F32)
        for d in range(8):
            blk = a_ref[d]
            row = jnp.concatenate([blk[r:r + 1, :] for r in range(8)], axis=1)
            top = top + jnp.where(sub == d, jnp.broadcast_to(row, (8, D)), 0.0)
        a16 = jnp.concatenate([top, jnp.where(sub == 0, jnp.broadcast_to(ccv * _sig(ccv), (8, D)), 0.0)], axis=0)
        a16_ref[...] = a16
        loads[nw].wait()
        b_k = jnp.zeros((1, kw), F32)
        for kk in range(4):
            b_k = b_k + jnp.where(k == kk, b_ref[:, kk * kw:(kk + 1) * kw], 0.0)
        mod_ref[k] = _dot3(_nn, a16, wm_ref[...]) + b_k
        for j, to in enumerate(chips):
            links.send(mod_ref.at[k], mod_ref.at[k], 7 + j, to)

        def over_ici(j, frm, origin):
            for wi in range(nw):
                blk = _half(slab(wi, 2 * origin[0] + origin[1]), cc, axes[wi])
                links.arrived(blk, 10 + wi * 6 + j, frm)
                if j == 0:
                    links.send(blk, blk, 10 + wi * 6 + 2, chip_b)
                links.send(blk, blk, 10 + wi * 6 + 3 + j, sibling)
                store_half(wi, 2 * origin[0] + origin[1], cc, 1 + j)

        def from_sibling(j, origin):
            for wi in range(nw):
                links.arrived(_half(slab(wi, 2 * origin[0] + origin[1]), 1 - cc, axes[wi]), 10 + wi * 6 + 3 + j, sibling)
                store_half(wi, 2 * origin[0] + origin[1], 1 - cc, 4 + j)
        over_ici(0, chip_a, chip_a)
        over_ici(1, chip_b, chip_b)
        from_sibling(0, chip_b)
        over_ici(2, chip_b, chip_d)
        from_sibling(1, chip_a)
        from_sibling(2, chip_d)
        for j, (px, py, pc) in enumerate(chips):
            links.arrived(mod_ref.at[2 * px + py], 7 + j, (px, py, pc))
        for sel, row in ((0, 8), (1, me)):
            for kk in range(4):
                piece = mod_ref[kk, pl.ds(row, 1), :]
                lo = kk * kw
                while lo < (kk + 1) * kw:
                    r, col = divmod(lo, D)
                    n = min((kk + 1) * kw - lo, D - col)
                    modsel_ref[sel, r:r + 1, col:col + n] = piece[:, lo - kk * kw:lo - kk * kw + n]
                    lo += n
        links.drain()
        for cp in stores:
            cp.wait()

    nsem = 10 + 6 * nw
    gathered = [(4, r, s.shape[1]) for r, s in zip(slab_rows, shards)]
    return pl.pallas_call(
        body, name="gather", in_specs=[VM, VM, ANY, VM] + [ANY] * nw, out_specs=[VM, VM] + [ANY] * nw,
        out_shape=[jax.ShapeDtypeStruct((16, D), F32), jax.ShapeDtypeStruct((2, 3, D), F32)]
        + [jax.ShapeDtypeStruct(g, BF16) for g in gathered],
        scratch_shapes=[pltpu.VMEM(g, BF16) for g in gathered] + [pltpu.VMEM(s.shape, F32) for s in shards]
        + [pltpu.VMEM(w_mod.shape, F32), pltpu.VMEM((8, 8, D // 8), F32), pltpu.VMEM((4, 16, kw), F32),
           pltpu.SemaphoreType.DMA((nsem,)),
           pltpu.SemaphoreType.DMA((nsem,)), pltpu.SemaphoreType.DMA((nw + 1 + 8 * nw,))],
        compiler_params=pltpu.CompilerParams(vmem_limit_bytes=VMEM_LIMIT),
    )(c, c_ctx, w_mod, b_mod, *shards)


SMALL_ROW_WIDTHS = (D, QL, KVL, DKP, DKP, 512, 128)
SMALL_OUT_WIDTHS = (D, QL, KVL, DK, DK, 512, 1)
SMALL_PACK = 384


def _small_pieces():
    pieces = []
    for i, w in enumerate(SMALL_ROW_WIDTHS):
        for c0 in range(0, w, 128):
            j = len(pieces)
            pieces.append((i, c0, j // (SMALL_PACK // 128), j % (SMALL_PACK // 128) * 128))
    assert len(pieces) <= 8 * (SMALL_PACK // 128)
    return pieces


def _reduce(grads, axes, smalls, w_pool_g, dmod, dgate, a16, w_mod, c_ctx):
    nw = len(grads)
    ns = len(smalls)
    kw = w_mod.shape[1]
    halves = []
    for g, ax in zip(grads, axes):
        halves.append((g.shape[1] // 2, g.shape[2]) if ax == 0 else (g.shape[1], g.shape[2] // 2))

    def body(*refs):
        g_refs = refs[:nw]
        small_refs = refs[nw:nw + ns]
        wp_ref, dmod_ref, dgate_ref, a16_ref, wm_hbm, cc_ref = refs[nw + ns:nw + ns + 6]
        o = nw + ns + 6
        r_outs = refs[o:o + nw]
        small_outs = refs[o + nw:o + nw + ns]
        rwp_ref, gw_out, gb_ref, gc_ref = refs[o + nw + ns:o + nw + ns + 4]
        o = o + nw + ns + 4
        own, sib, part, got, rel, r_refs = (refs[o + i * nw:o + (i + 1) * nw] for i in range(6))
        smbuf, wps, wpg, dm_all, pc_all, wm_ref, gw_ref, send_sems, recv_sems, local_sems = refs[o + 6 * nw:]
        wm_load = pltpu.make_async_copy(wm_hbm, wm_ref, local_sems.at[nw])
        wm_load.start()
        stores = []

        def store(src, dst, sem):
            cp = pltpu.make_async_copy(src, dst, local_sems.at[nw + 1 + sem])
            cp.start()
            stores.append(cp)
        x, y, cc = lax.axis_index("x"), lax.axis_index("y"), lax.axis_index("c")
        me = 4 * x + 2 * y + cc
        k = 2 * x + y
        sibling = (x, y, 1 - cc)
        links = _Links(send_sems, recv_sems)
        chips = [_peer(x, y, cc, off + (0,)) for off in CHIPS3]
        peers = [_peer(x, y, cc, off) for off in PEERS7]
        chip_a = ((x + 1 - cc) % 2, (y + cc) % 2, cc)
        chip_b = ((x + cc) % 2, (y + 1 - cc) % 2, cc)
        ka, kb, kd = 2 * chip_a[0] + chip_a[1], 2 * chip_b[0] + chip_b[1], 2 * (1 - x) + (1 - y)
        big, sm0, wp0, dm0, pc0 = 0, 5 * nw, 5 * nw + 7, 5 * nw + 14, 5 * nw + 21

        locals_ = []
        for wi in range(nw):
            lc = pltpu.make_async_copy(_half(g_refs[wi], cc, axes[wi]), own[wi], local_sems.at[wi])
            lc.start()
            locals_.append(lc)
            links.send(_half(g_refs[wi], 1 - cc, axes[wi]), sib[wi], big + wi * 5, sibling)
        slot = smbuf.at[me]
        slot[...] = jnp.zeros((8, SMALL_PACK), F32)
        small_rows = [jnp.broadcast_to(ref[...], (1, w)) for ref, w in zip(small_refs, SMALL_ROW_WIDTHS)]
        for i, c0, row, lane in _small_pieces():
            slot[row:row + 1, lane:lane + 128] = small_rows[i][:, c0:c0 + 128]
        links.send(wp_ref, wps, wp0, sibling)
        dm_mine = dm_all.at[me]
        dm_mine[...] = jnp.zeros((8, kw), F32)
        dm_rows = ((0, (dmod_ref[1, 0:1, :], dmod_ref[1, 1:2, :], dgate_ref[...])), (4, (dmod_ref[0, 0:1, :], dmod_ref[0, 1:2, :])))
        for base, rows in dm_rows:
            for r, v in enumerate(rows):
                lo = r * D
                while lo < (r + 1) * D:
                    kk, col = divmod(lo, kw)
                    n = min((r + 1) * D - lo, kw - col)
                    dm_mine[base + kk:base + kk + 1, col:col + n] = v[:, lo - r * D:lo - r * D + n]
                    lo += n
        for j, peer in enumerate(peers):
            links.send(dm_all.at[me], dm_all.at[me], dm0 + j, peer)
            links.send(smbuf.at[me], smbuf.at[me], sm0 + j, peer)
        links.arrived(wps, wp0, sibling)
        wpg[k] = (wp_ref[...] + wps[...]).astype(BF16)
        for j, to in enumerate(chips):
            links.send(wpg.at[k], wpg.at[k], wp0 + 1 + j, to)
        for wi in range(nw):
            locals_[wi].wait()
            links.arrived(sib[wi], big + wi * 5, sibling)
            part[wi][...] = (own[wi][...] + sib[wi][...]).astype(BF16)
            got[wi][k] = part[wi][k]
            got[wi][kd] = jnp.zeros(halves[wi], BF16)
            links.send(part[wi].at[kd], rel[wi], big + wi * 5 + 1, chip_b)
            links.send(part[wi].at[kb], got[wi].at[k], big + wi * 5 + 2, chip_b)
        for j, (px, py, pc) in enumerate(peers):
            links.arrived(dm_all.at[4 * px + 2 * py + pc], dm0 + j, (px, py, pc))
        dm_tot = dm_all[0]
        for d in range(1, 8):
            dm_tot = dm_tot + dm_all[d]
        for kk in range(4):
            gb_ref[:, kk * kw:(kk + 1) * kw] = dm_tot[kk:kk + 1, :] + dm_tot[4 + kk:5 + kk, :]
        top = jnp.zeros((8, kw), F32)
        dmc_k = jnp.zeros((1, kw), F32)
        for kk in range(4):
            top = top + jnp.where(k == kk, _select_rows(dm_all, 8, kk), 0.0)
            dmc_k = dmc_k + jnp.where(k == kk, dm_tot[4 + kk:5 + kk, :], 0.0)
        sub = lax.broadcasted_iota(jnp.int32, (8, 1), 0)
        bk = jnp.concatenate([top, jnp.where(sub == 0, jnp.broadcast_to(dmc_k, (8, kw)), 0.0)], axis=0)
        gw_ref[...] = _dot3(_tn, a16_ref[...], bk)
        store(gw_ref, gw_out, 0)
        wm_load.wait()
        pc_all[k] =_dot3(_nt, jnp.broadcast_to(bk[8:9, :], (8, kw)), wm_ref[...])
        for j, to in enumerate(chips):
            links.send(pc_all.at[k], pc_all.at[k], pc0 + j, to)
        for j, (px, py, pc) in enumerate(peers):
            links.arrived(smbuf.at[4 * px + 2 * py + pc], sm0 + j, (px, py, pc))
        tot = smbuf[0]
        for d in range(1, 8):
            tot = tot + smbuf[d]
        for i, c0, row, lane in _small_pieces():
            n = min(128, SMALL_OUT_WIDTHS[i] - c0)
            if n > 0:
                small_outs[i][:, c0:c0 + n] = tot[row:row + 1, lane:lane + n]
        for j, (px, py, pc) in enumerate(chips):
            links.arrived(wpg.at[2 * px + py], wp0 + 1 + j, (px, py, pc))
        wpt = wpg[0].astype(F32)
        for kk in range(1, 4):
            wpt = wpt + wpg[kk].astype(F32)
        rwp_ref[...] = wpt
        for wi in range(nw):
            links.arrived(rel[wi], big + wi * 5 + 1, chip_b)
            rel[wi][...] = (part[wi][ka].astype(F32) + rel[wi][...].astype(F32)).astype(BF16)
            links.send(rel[wi], got[wi].at[k], big + wi * 5 + 3, chip_a)
        for wi in range(nw):
            links.arrived(got[wi].at[kb], big + wi * 5 + 2, chip_b)
            links.arrived(got[wi].at[ka], big + wi * 5 + 3, chip_a)
            total = got[wi][0].astype(F32)
            for kk in range(1, 4):
                total = total + got[wi][kk].astype(F32)
            mine = _half(r_refs[wi], cc, axes[wi])
            mine[...] = total
            links.send(mine, mine, big + wi * 5 + 4, sibling)
            store(mine, _half(r_outs[wi], cc, axes[wi]), 1 + wi)
        for j, (px, py, pc) in enumerate(chips):
            links.arrived(pc_all.at[2 * px + py], pc0 + j, (px, py, pc))
        ccv = cc_ref[...]
        sg = _sig(ccv)
        gc_ref[...] = (pc_all[0][0:1, :] + pc_all[1][0:1, :] + pc_all[2][0:1, :] + pc_all[3][0:1, :]) * (sg * (1.0 + ccv * (1.0 - sg)))
        for wi in range(nw):
            links.arrived(_half(r_refs[wi], 1 - cc, axes[wi]), big + wi * 5 + 4, sibling)
            store(_half(r_refs[wi], 1 - cc, axes[wi]), _half(r_outs[wi], 1 - cc, axes[wi]), 1 + nw + wi)
        links.drain()
        for cp in stores:
            cp.wait()

    nsem = 5 * nw + 24
    quads = [(4,) + h for h in halves]
    return pl.pallas_call(
        body, name="reduce", in_specs=[ANY] * nw + [VM] * (ns + 4) + [ANY, VM],
        out_specs=[ANY] * nw + [VM] * (ns + 1) + [ANY, VM, VM],
        out_shape=[jax.ShapeDtypeStruct(g.shape[1:], F32) for g in grads]
        + [jax.ShapeDtypeStruct((1, w), F32) for w in SMALL_OUT_WIDTHS]
        + [jax.ShapeDtypeStruct(w_pool_g.shape, F32), jax.ShapeDtypeStruct((D, kw), F32), jax.ShapeDtypeStruct((1, 3 * D), F32),
           jax.ShapeDtypeStruct((1, D), F32)],
        scratch_shapes=[pltpu.VMEM(q, F32) for q in quads] + [pltpu.VMEM(q, F32) for q in quads]
        + [pltpu.VMEM(q, BF16) for q in quads] + [pltpu.VMEM(q, BF16) for q in quads] + [pltpu.VMEM(h, BF16) for h in halves]
        + [pltpu.VMEM(g.shape[1:], F32) for g in grads]
        + [pltpu.VMEM((8, 8, SMALL_PACK), F32), pltpu.VMEM(w_pool_g.shape, F32), pltpu.VMEM((4,) + w_pool_g.shape, BF16),
           pltpu.VMEM((8, 8, kw), F32), pltpu.VMEM((4, 8, D), F32), pltpu.VMEM((D, kw), F32), pltpu.VMEM((D, kw), F32)]
        + [pltpu.SemaphoreType.DMA((nsem,)), pltpu.SemaphoreType.DMA((nsem,)), pltpu.SemaphoreType.DMA((3 * nw + 2,))],
        compiler_params=pltpu.CompilerParams(vmem_limit_bytes=VMEM_LIMIT),
    )(*grads, *smalls, w_pool_g, dmod, dgate, a16, w_mod, c_ctx)


def _rope_tables(s_len):
    rows = s_len // GRID_W
    per = TB // GRID_W
    n_freq = 16
    f32 = np.float32
    inv = f32(ROPE_BASE) ** (-np.arange(n_freq, dtype=f32) / f32(n_freq))
    ang_r = np.arange(rows, dtype=f32)[:, None] * inv
    ang_c = np.arange(GRID_W, dtype=f32)[:, None] * inv
    by_row, by_col = [], []
    for fn, pad in ((np.cos, 1.0), (np.sin, 0.0)):
        r = np.concatenate([fn(ang_r), fn(ang_r), np.zeros((rows, 96), f32)], axis=1).reshape(rows // per, per, 128)
        by_row.append(np.pad(r, ((0, 0), (0, 8 - per), (0, 0))))
        cpart = np.concatenate([np.zeros((GRID_W, 32), f32), fn(ang_c), fn(ang_c), np.full((GRID_W, 64), pad, f32)], axis=1)
        by_col.append(np.tile(cpart, (per, 1)))
    return jnp.asarray(np.concatenate(by_row, axis=-1), F32), jnp.asarray(np.concatenate(by_col, axis=-1), F32)


def kernel(x, c, ctx, c_ctx, w_mod, b_mod, norm_g, w_in, q_lora_g, w_uq, kv_lora_g, w_ukv, q_norm_g, k_norm_g, w_pool, pool_scale, w_out, loss_target, m_c_ctx, m_w_mod, m_b_mod, m_norm_g, m_w_in, m_q_lora_g, m_w_uq, m_kv_lora_g, m_w_ukv, m_q_norm_g, m_k_norm_g, m_w_pool, m_pool_scale, m_w_out, v_c_ctx, v_w_mod, v_b_mod, v_norm_g, v_w_in, v_q_lora_g, v_w_uq, v_kv_lora_g, v_w_ukv, v_q_norm_g, v_k_norm_g, v_w_pool, v_pool_scale, v_w_out):
    s_len = x.shape[1]
    lc = ctx.shape[1]
    weights = dict(c_ctx=c_ctx, w_mod=w_mod, b_mod=b_mod, norm_g=norm_g, w_in=w_in, q_lora_g=q_lora_g, w_uq=w_uq,
                   kv_lora_g=kv_lora_g, w_ukv=w_ukv, q_norm_g=q_norm_g, k_norm_g=k_norm_g, w_pool=w_pool,
                   pool_scale=pool_scale, w_out=w_out)
    m_in = dict(c_ctx=m_c_ctx, w_mod=m_w_mod, b_mod=m_b_mod, norm_g=m_norm_g, w_in=m_w_in, q_lora_g=m_q_lora_g, w_uq=m_w_uq,
                kv_lora_g=m_kv_lora_g, w_ukv=m_w_ukv, q_norm_g=m_q_norm_g, k_norm_g=m_k_norm_g, w_pool=m_w_pool,
                pool_scale=m_pool_scale, w_out=m_w_out)
    v_in = dict(c_ctx=v_c_ctx, w_mod=v_w_mod, b_mod=v_b_mod, norm_g=v_norm_g, w_in=v_w_in, q_lora_g=v_q_lora_g, w_uq=v_w_uq,
                kv_lora_g=v_kv_lora_g, w_ukv=v_w_ukv, q_norm_g=v_q_norm_g, k_norm_g=v_k_norm_g, w_pool=v_w_pool,
                pool_scale=v_pool_scale, w_out=v_w_out)
    order = ["c_ctx", "w_mod", "b_mod", "norm_g", "w_in", "q_lora_g", "w_uq", "kv_lora_g", "w_ukv", "q_norm_g", "k_norm_g",
             "w_pool", "pool_scale", "w_out"]
    transposed = ("w_in", "w_uq")
    as2d = lambda n, a: jnp.transpose(a[0]) if n in transposed else a.reshape(-1, a.shape[-1])
    back = lambda n, a: jnp.transpose(a)[None] if n in transposed else a.reshape(weights[n].shape)

    c_ctx2 = c_ctx.reshape(1, D)
    split = (0, 1, 0, 0)
    a16, modsel, g_out, g_in, g_uq, g_ukv = _gather(
        c, c_ctx2, w_mod[0], b_mod, [w_out[0], as2d("w_in", w_in), as2d("w_uq", w_uq), w_ukv[0]], split,
        (D // 4, DIN // 4, DKP, KVL))
    w_in_t = g_in.reshape(DIN, D)
    w_uq_t = g_uq
    w_out_f = g_out.reshape(D, D)
    cos, sin = _rope_tables(s_len)

    u, q, kk, v = _fwd_in(ctx[0], x[0], modsel, norm_g, w_in_t, q_lora_g, w_uq_t, kv_lora_g, g_ukv, q_norm_g, k_norm_g, cos, sin)
    attn, lse = _attn_fwd(q, kk, v, s_len)
    (dxn, dattn, dga, dgp, dpool, dw_out, dgate, dps, dw_pool, loss) = _out_stage(
        attn.reshape(s_len // Q_BLOCK, Q_BLOCK, NH * DV), u, x[0], loss_target[0], modsel, w_pool[0], pool_scale,
        w_out_f, lc)
    dattn = dattn.reshape(s_len, NH * DV)
    dq, dk, dv = _attn_bwd(q, kk, v, dattn, attn, lse, s_len)
    dlo, dw_uq_t, dw_ukv, dqlg, dkvlg, dqng, dkng = _qkv_bwd(u, dq, dk, dv, cos, sin, q_lora_g, w_uq_t, kv_lora_g, g_ukv,
                                                            q_norm_g, k_norm_g, s_len)
    gx, dw_in_t, dmod, dng = _in_bwd(ctx[0], x[0], modsel, norm_g, dlo, dga, dgp, dpool, dxn, w_in_t)

    r_out, r_in, r_uq, r_ukv, g_ng, g_qlg, g_kvlg, g_qng, g_kng, g_ps, loss_all, g_wp, g_w_mod, g_b_mod, g_c_ctx = _reduce(
        [dw_out.reshape(4, D // 4, D), dw_in_t.reshape(4, DIN // 4, D), dw_uq_t, dw_ukv], split,
        [dng, dqlg, dkvlg, dqng, dkng, dps, loss], dw_pool, dmod, dgate, a16, w_mod[0], c_ctx2)
    g2d = dict(c_ctx=g_c_ctx, b_mod=g_b_mod, w_mod=g_w_mod, w_in=r_in, w_uq=r_uq, w_ukv=r_ukv, w_out=r_out, norm_g=g_ng,
               q_lora_g=g_qlg, kv_lora_g=g_kvlg, q_norm_g=g_qng, k_norm_g=g_kng, pool_scale=g_ps, w_pool=g_wp.reshape(512, 128))

    outs = _adamw_many([as2d(n, weights[n]) for n in order], [g2d[n] for n in order], [as2d(n, m_in[n]) for n in order],
                       [as2d(n, v_in[n]) for n in order])
    d2d, m2d, v2d, g2d = (dict(zip(order, arrs)) for arrs in outs)

    return (loss_all[0, 0], gx[None], *[back(n, g2d[n]) for n in order], *[back(n, d2d[n]) for n in order],
            *[back(n, m2d[n]) for n in order], *[back(n, v2d[n]) for n in order])
```

```python
import jax
import jax.numpy as jnp
import numpy as np
from jax import lax
from jax.experimental import pallas as pl
from jax.experimental.pallas import tpu as pltpu

F32 = jnp.float32
BF16 = jnp.bfloat16
MESH = pl.DeviceIdType.MESH

D = 1024
NH = 4
DK = 192
DKP = 256
DV = 128
QL = 256
KVL = 128
DIN = 1984
U_LO = 448
SEG = ((0, 512), (448, 960), (960, 1472), (1472, 1984))
DU = 2048
POOL_WINDOWS = (2, 4, 8, 16)
HALO = 8
EPS = 1e-6
ROPE_BASE = 10000.0
GRID_W = 64
Q_BLOCK = 128
TB = 256
BWD_QBLOCKS = 1
SCALE = DK ** -0.5
LOG2E = 1.4426950408889634
LN2 = 0.6931471805599453
VMEM_LIMIT = 56 * 1024 * 1024

ADAM_LR = 0.001
ADAM_B1 = 0.9
ADAM_B2 = 0.999
ADAM_EPS = 1e-08
ADAM_WD = 0.01
ADAM_STEP = 10

CHIPS3 = ((1, 0), (0, 1), (1, 1))
PEERS7 = tuple((dx, dy, dc) for dx in (0, 1) for dy in (0, 1) for dc in (0, 1) if (dx, dy, dc) != (0, 0, 0))

VM = pl.BlockSpec(memory_space=pltpu.VMEM)
ANY = pl.BlockSpec(memory_space=pl.ANY)


def _nn(a, b):
    return jnp.dot(a, b, preferred_element_type=F32)


def _nt(a, b):
    return lax.dot_general(a, b, (((1,), (1,)), ((), ())), preferred_element_type=F32)


def _tn(a, b):
    return lax.dot_general(a, b, (((0,), (0,)), ((), ())), preferred_element_type=F32)


def _split3(a):
    a0 = a.astype(BF16)
    r = a - a0.astype(F32)
    a1 = r.astype(BF16)
    a2 = (r - a1.astype(F32)).astype(BF16)
    return a0, a1, a2


def _dot3(dot, a, b):
    sa = _split3(a)
    sb = _split3(b)
    out = None
    for i in range(3):
        for j in range(3 - i):
            t = dot(sa[i], sb[j])
            out = t if out is None else out + t
    return out


def _sig(x):
    return 1.0 / (1.0 + jnp.exp(-x))


def _rot(t):
    src = lax.broadcasted_iota(jnp.int32, (128, 128), 0)
    dst = lax.broadcasted_iota(jnp.int32, (128, 128), 1)
    first = (dst % 32) < 16
    perm = jnp.where(first & (src == dst + 16), -1.0, jnp.where(~first & (src == dst - 16), 1.0, 0.0)).astype(BF16)
    hi = t.astype(BF16)
    lo = (t - hi.astype(F32)).astype(BF16)
    return _nn(hi, perm) + _nn(lo, perm)


def _rope(t, cos, sin):
    return t * cos + _rot(t) * sin


def _rope_t(t, cos, sin):
    return t * cos - _rot(t * sin)


def _rope_block(rows_ref, cols_ref, is_ctx):
    lane = lax.broadcasted_iota(jnp.int32, (TB, 256), 1) % 128
    rows = jnp.concatenate([jnp.broadcast_to(rows_ref[0, r:r + 1, :], (GRID_W, 256)) for r in range(TB // GRID_W)], axis=0)
    cs = jnp.where(lane < 32, rows, cols_ref[...])
    return jnp.where(is_ctx, 1.0, cs[:, :128]), jnp.where(is_ctx, 0.0, cs[:, 128:])


def _shift_rows(z, k):
    n = z.shape[0]
    return pltpu.roll(z, (n - k) % n, 0)


def _colsum(a):
    return jnp.sum(a, axis=0, keepdims=True)


def _rowsum(a):
    return jnp.sum(a, axis=-1, keepdims=True)


def _row_layout(col):
    return jnp.transpose(jnp.broadcast_to(col, (col.shape[0], 128)))[0:8, :]


def _params(sem=None):
    return pltpu.CompilerParams(dimension_semantics=sem, vmem_limit_bytes=VMEM_LIMIT)


def _full(shape):
    nd = len(shape)
    return pl.BlockSpec(shape, lambda *_: (0,) * nd)


def _peer(x, y, c, off):
    dx, dy, dc = off
    return ((x + dx) % 2, (y + dy) % 2, (c + dc) % 2)


def _token_specs(off):
    ctx = pl.BlockSpec((TB, D), lambda i: (jnp.minimum(i, off - 1), 0))
    lat = pl.BlockSpec((TB, D), lambda i: (jnp.maximum(i - off, 0), 0))
    mod = pl.BlockSpec((1, 3, D), lambda i: (jnp.minimum(i // off, 1), 0, 0))
    return ctx, lat, mod


def _head_gains(qng_ref, kng_ref, pad_ref):
    pad_ref[...] = jnp.zeros((2, DKP), F32)
    pad_ref[0:1, 0:DK] = qng_ref[...]
    pad_ref[1:2, 0:DK] = kng_ref[...]
    return pad_ref[0:1, :], pad_ref[1:2, :]


def _modulated(x, mod_ref, ng):
    shift = mod_ref[0, 0:1, :]
    scale = mod_ref[0, 1:2, :]
    r = lax.rsqrt(jnp.mean(x * x, axis=-1, keepdims=True) + EPS)
    xh = x * r
    xg = xh * ng
    return r, xh, xg, xg * (1.0 + scale) + shift, scale


def _fwd_in(ctx, x, modsel, norm_g, w_in_t, q_lora_g, w_uq_t, kv_lora_g, w_ukv, qn_g, kn_g, cos, sin):
    s_len, lc = x.shape[0], ctx.shape[0]
    t_all = s_len + lc
    nb = t_all // TB
    off = lc // TB

    def body(ctx_ref, x_ref, mod_ref, ng_ref, win_ref, qlg_ref, wuq_ref, kvlg_ref, wukv_ref, qng_ref, kng_ref, cos_ref, sin_ref,
             u_ref, q_ref, k_ref, v_ref, pad_ref):
        is_ctx = pl.program_id(0) < off
        qng, kng = _head_gains(qng_ref, kng_ref, pad_ref)
        xb = jnp.where(is_ctx, ctx_ref[...], x_ref[...])
        _, _, _, h, _ = _modulated(xb, mod_ref, ng_ref[...])
        hb = h.astype(BF16)
        lane = lax.broadcasted_iota(jnp.int32, (TB, 512), 1)
        ulo = jnp.where(lane < U_LO, _nt(hb, win_ref[SEG[0][0]:SEG[0][1], :]), 0.0)
        u_ref[:, 0:512] = ulo
        for j in range(1, 4):
            u_ref[:, j * 512:(j + 1) * 512] = _nt(hb, win_ref[SEG[j][0]:SEG[j][1], :])
        cos, sin = _rope_block(cos_ref, sin_ref, is_ctx)
        cq = ulo[:, 0:QL]
        cqn = (cq * lax.rsqrt(jnp.mean(cq * cq, axis=-1, keepdims=True) + EPS) * qlg_ref[...]).astype(BF16)
        ckv = ulo[:, QL:QL + KVL]
        ckvn = (ckv * lax.rsqrt(jnp.mean(ckv * ckv, axis=-1, keepdims=True) + EPS) * kvlg_ref[...]).astype(BF16)
        qhs = [_nt(cqn, wuq_ref[hd]) for hd in range(NH)]
        kvs = [_nn(ckvn, wukv_ref[hd]) for hd in range(NH)]
        for hd in range(NH):
            qh = qhs[hd]
            qn = qh * lax.rsqrt(_rowsum(qh * qh) / DK + EPS) * qng
            q_ref[hd] = (jnp.concatenate([qn[:, :128], _rope(qn[:, 128:], cos, sin)], axis=1) * (SCALE * LOG2E)).astype(BF16)
        kr = ulo[:, 384:512]
        skr = _rowsum(kr * kr)
        kr_roped =_rope(kr * kng[:, 128:], cos, sin)
        for hd in range(NH):
            kv = kvs[hd]
            kn = kv[:, :128]
            rk = lax.rsqrt((_rowsum(kn * kn) + skr) / DK + EPS)
            k_ref[hd] = jnp.concatenate([kn * rk * kng[:, :128], kr_roped * rk], axis=1).astype(BF16)
            v_ref[hd] = kv[:, 128:].astype(BF16)

    row = lambda w: pl.BlockSpec((TB, w), lambda i: (i, 0))
    heads = lambda w: pl.BlockSpec((NH, TB, w), lambda i: (0, i, 0))
    cspec, xspec, mspec = _token_specs(off)
    return pl.pallas_call(
        body, name="fwd_in", grid=(nb,),
        in_specs=[cspec, xspec, mspec, _full((1, D)), _full((DIN, D)), _full((1, QL)), _full((NH, DKP, QL)), _full((1, KVL)),
                  _full((NH, KVL, 256)), _full((1, DK)), _full((1, DK)),
                  pl.BlockSpec((1, 8, 256), lambda i: (jnp.maximum(i - off, 0), 0, 0)), _full((TB, 256))],
        out_specs=[row(DU), heads(DKP), heads(DKP), heads(DV)],
        out_shape=[jax.ShapeDtypeStruct((t_all, DU), F32), jax.ShapeDtypeStruct((NH, t_all, DKP), BF16),
                   jax.ShapeDtypeStruct((NH, t_all, DKP), BF16), jax.ShapeDtypeStruct((NH, t_all, DV), BF16)],
        scratch_shapes=[pltpu.VMEM((2, DKP), F32)],
        compiler_params=_params(("arbitrary",)),
    )(ctx, x, modsel, norm_g, w_in_t, q_lora_g, w_uq_t, kv_lora_g, w_ukv, qn_g, kn_g, cos, sin)


def _attn_fwd(q, k, v, s_len):
    t_all = q.shape[1]
    off = (t_all - s_len) // TB
    nq = s_len // TB
    nsub = next(n for n in (4, 2, 1) if nq % n == 0)

    def body(*refs):
        q_refs = refs[:nsub]
        k_ref, v_ref, o_ref, lse_ref = refs[nsub:]
        for sb in range(nsub):
            s = _nt(q_refs[sb][0], k_ref[0])
            m = jnp.max(s, axis=-1, keepdims=True)
            e = jnp.exp2(s - m)
            l = _rowsum(e)
            o_ref[sb * TB:(sb + 1) * TB, :] = _nn(e.astype(BF16), v_ref[0]) / l
            lse_ref[0, sb] = _row_layout(m + jnp.log2(l))

    qspec = lambda sb: pl.BlockSpec((1, TB, DKP), lambda h, i: (h, i * nsub + sb + off, 0))
    return pl.pallas_call(
        body, name="attn_fwd", grid=(NH, nq // nsub),
        in_specs=[qspec(sb) for sb in range(nsub)]
        + [pl.BlockSpec((1, t_all, DKP), lambda h, i: (h, 0, 0)), pl.BlockSpec((1, t_all, DV), lambda h, i: (h, 0, 0))],
        out_specs=[pl.BlockSpec((nsub * TB, DV), lambda h, i: (i, h)), pl.BlockSpec((1, nsub, 8, TB), lambda h, i: (h, i, 0, 0))],
        out_shape=[jax.ShapeDtypeStruct((s_len, NH * DV), F32), jax.ShapeDtypeStruct((NH, nq, 8, TB), F32)],
        compiler_params=_params(("arbitrary", "arbitrary")),
    )(*([q] * nsub), k, v)


def _out_stage(attn, u, x, target, modsel, w_pool, pool_scale, w_out, lc):
    s_len = x.shape[0]
    t_all = s_len + lc
    off = lc // TB
    nq = s_len // TB
    hb = TB // HALO
    nqb = s_len // Q_BLOCK
    jb = TB // nqb

    def body(attn_ref, ga_ref, pin_ref, pprev_ref, pnext_ref, gp_ref, x_ref, tgt_ref, gate_ref, wp_ref, ps_ref, wo_ref,
             dxn_ref, dattn_ref, dga_ref, dgp_ref, dpool_ref, dwo_ref, dgate_ref, dps_ref, dwp_ref, loss_ref):
        i = pl.program_id(0)

        @pl.when(i == 0)
        def _():
            dwo_ref[...] = jnp.zeros_like(dwo_ref)
            dgate_ref[...] = jnp.zeros_like(dgate_ref)
            dps_ref[...] = jnp.zeros_like(dps_ref)
            dwp_ref[...] = jnp.zeros_like(dwp_ref)
            loss_ref[...] = jnp.zeros_like(loss_ref)

        attn = jnp.concatenate([attn_ref[:, jj, :] for jj in range(jb)], axis=0)
        ga = ga_ref[...]
        gp = gp_ref[...]
        pin = pin_ref[...]
        prev = jnp.where(i == 0, 0.0, pprev_ref[...])
        nxt = jnp.where(i == nq - 1, 0.0, pnext_ref[...])
        win = jnp.concatenate([prev, pin, nxt], axis=0)
        tg = i * TB + lax.broadcasted_iota(jnp.int32, (TB, 1), 0)
        pooled = []
        for g, w in enumerate(POOL_WINDOWS):
            a = win[:, g * 128:(g + 1) * 128]
            p = _shift_rows(a, -1) + a
            for step in (1, 2, 4):
                if w >= 4 * step:
                    p = _shift_rows(p, -step) + _shift_rows(p, step)
            cnt = (jnp.minimum(tg + w // 2, s_len) - jnp.maximum(tg - w // 2, 0)).astype(F32)
            pooled.append(p[HALO:HALO + TB] / cnt - a[HALO:HALO + TB])
        pooled_b = [p.astype(BF16) for p in pooled]
        wp = [wp_ref[g].astype(BF16) for g in range(4)]
        z = jnp.concatenate([_nn(pooled_b[g], wp[g]) for g in range(4)], axis=1)
        ps = ps_ref[...]
        yp = z * ps
        sga = _sig(ga)
        sila = ga * sga
        sgp = _sig(gp)
        silp = gp * sgp
        br = jnp.concatenate([sila * attn, silp * yp], axis=1).astype(BF16)
        y = _nn(br, wo_ref[...])
        gate = gate_ref[0, 2:3, :]
        err = x_ref[...] + gate * y - tgt_ref[...]
        loss_ref[...] += _colsum(_rowsum(err * err)) * (0.5 / D)
        dxn = err * (1.0 / D)
        dxn_ref[...] = dxn
        dgate_ref[...] += _colsum(dxn * y)
        dy = (dxn * gate).astype(BF16)
        dwo_ref[...] += _tn(br, dy)
        dbr = _nt(dy, wo_ref[...])
        dbra = dbr[:, :512]
        dbrp = dbr[:, 512:]
        dattn = dbra * sila
        for jj in range(jb):
            dattn_ref[:, jj, :] = dattn[jj * nqb:(jj + 1) * nqb]
        dga_ref[...] = (dbra * attn * (sga * (1.0 + ga * (1.0 - sga)))).astype(BF16)
        dgp_ref[...] = (dbrp * yp * (sgp * (1.0 + gp * (1.0 - sgp)))).astype(BF16)
        dyp = dbrp * silp
        dps_ref[...] += _colsum(dyp * z)
        dz = (dyp * ps).astype(BF16)
        dpool = []
        for g in range(4):
            dzg = dz[:, g * 128:(g + 1) * 128]
            dwp_ref[g] += _tn(pooled_b[g], dzg)
            dpool.append(_nt(dzg, wp[g]))
        dpool_ref[...] = jnp.concatenate(dpool, axis=1)

    lat = lambda w: pl.BlockSpec((TB, w), lambda i: (i, 0))
    perm = pl.BlockSpec((nqb, jb, 512), lambda i: (0, i, 0))
    ucol = lambda j: pl.BlockSpec((TB, 512), lambda i: (i + off, j))
    last8 = t_all // HALO - 1
    return pl.pallas_call(
        body, name="out_stage", grid=(nq,),
        in_specs=[perm, ucol(1), ucol(2),
                  pl.BlockSpec((HALO, 512), lambda i: ((i + off) * hb - 1, 2)),
                  pl.BlockSpec((HALO, 512), lambda i: (jnp.minimum((i + off + 1) * hb, last8), 2)),
                  ucol(3), lat(D), lat(D), pl.BlockSpec((1, 3, D), lambda i: (1, 0, 0)), _full((4, 128, 128)), _full((1, 512)),
                  _full((D, D))],
        out_specs=[lat(D), perm, lat(512), lat(512), lat(512),
                   _full((D, D)), _full((1, D)), _full((1, 512)), _full((4, 128, 128)), _full((1, 1))],
        out_shape=[jax.ShapeDtypeStruct((s_len, D), F32), jax.ShapeDtypeStruct((nqb, Q_BLOCK, 512), F32),
                   jax.ShapeDtypeStruct((s_len, 512), BF16), jax.ShapeDtypeStruct((s_len, 512), BF16),
                   jax.ShapeDtypeStruct((s_len, 512), F32),
                   jax.ShapeDtypeStruct((D, D), F32), jax.ShapeDtypeStruct((1, D), F32), jax.ShapeDtypeStruct((1, 512), F32),
                   jax.ShapeDtypeStruct((4, 128, 128), F32), jax.ShapeDtypeStruct((1, 1), F32)],
        compiler_params=_params(("arbitrary",)),
    )(attn, u, u, u, u, u, x, target, modsel, w_pool, pool_scale, w_out)


def _attn_bwd(q, k, v, dattn, attn, lse, s_len):
    t_all = q.shape[1]
    off = (t_all - s_len) // TB
    nq = s_len // TB
    nch = 4
    chunks = [(c * (t_all // nch), t_all // nch) for c in range(nch)]
    nsub = next(n for n in (BWD_QBLOCKS, 2, 1) if nq % n == 0)
    tq = nsub * TB

    def body(*refs):
        q_refs = refs[:nsub]
        k_ref, v_ref, do_ref, o_ref, lse_ref, dq_ref, dk_ref, dv_ref = refs[nsub:]
        i = pl.program_id(1)

        @pl.when(i == 0)
        def _():
            dk_ref[...] = jnp.zeros_like(dk_ref)
            dv_ref[...] = jnp.zeros_like(dv_ref)

        qb = jnp.concatenate([r[0] for r in q_refs], axis=0)
        delta_r = _row_layout(_rowsum(do_ref[...] * o_ref[...]))[0:1, :]
        do = do_ref[...].astype(BF16)
        lse_r = jnp.concatenate([lse_ref[0, sb][0:1, :] for sb in range(nsub)], axis=1)
        dq = jnp.zeros((tq, DKP), F32)
        for start, size in chunks:
            rows = pl.ds(start, size)
            kc = k_ref[0, rows, :]
            p_t = jnp.exp2(_nt(kc, qb) - lse_r)
            ds_t = (p_t * (_nt(v_ref[0, rows, :], do) - delta_r)).astype(BF16)
            dv_ref[0, rows, :] += _nn(p_t.astype(BF16), do)
            dk_ref[0, rows, :] += _nn(ds_t, qb)
            dq += _tn(ds_t, kc)
        dq_ref[0] = dq * SCALE

    kvspec = lambda w: pl.BlockSpec((1, t_all, w), lambda h, i: (h, 0, 0))
    rowspec = pl.BlockSpec((1, nsub, 8, TB), lambda h, i: (h, i, 0, 0))
    qspec = lambda sb: pl.BlockSpec((1, TB, DKP), lambda h, i: (h, i * nsub + sb + off, 0))
    return pl.pallas_call(
        body, name="attn_bwd", grid=(NH, nq // nsub),
        in_specs=[qspec(sb) for sb in range(nsub)]
        + [kvspec(DKP), kvspec(DV), pl.BlockSpec((tq, DV), lambda h, i: (i, h)), pl.BlockSpec((tq, DV), lambda h, i: (i, h)),
           rowspec],
        out_specs=[pl.BlockSpec((1, tq, DKP), lambda h, i: (h, i, 0)), kvspec(DKP), kvspec(DV)],
        out_shape=[jax.ShapeDtypeStruct((NH, s_len, DKP), F32), jax.ShapeDtypeStruct((NH, t_all, DKP), F32),
                   jax.ShapeDtypeStruct((NH, t_all, DV), F32)],
        compiler_params=_params(("arbitrary", "arbitrary")),
    )(*([q] * nsub), k, v, dattn, attn, lse)


def _qkv_bwd(u, dq, dk, dv, cos, sin, q_lora_g, w_uq_t, kv_lora_g, w_ukv, qn_g, kn_g, s_len):
    t_all = u.shape[0]
    off = (t_all - s_len) // TB
    nb = t_all // TB

    def body(ulo_ref, dq_ref, dk_ref, dv_ref, cos_ref, sin_ref, qlg_ref, wuq_ref, kvlg_ref, wukv_ref, qng_ref, kng_ref,
             dlo_ref, dwuq_ref, dwukv_ref, dqlg_ref, dkvlg_ref, dqng_ref, dkng_ref, pad_ref):
        i = pl.program_id(0)
        qng, kng = _head_gains(qng_ref, kng_ref, pad_ref)

        @pl.when(i == 0)
        def _():
            for r in (dwuq_ref, dwukv_ref, dqlg_ref, dkvlg_ref, dqng_ref, dkng_ref):
                r[...] = jnp.zeros_like(r)

        latent = i >= off
        ulo = ulo_ref[...]
        cos, sin = _rope_block(cos_ref, sin_ref, pl.program_id(0) < off)
        cq = ulo[:, 0:QL]
        rc = lax.rsqrt(jnp.mean(cq * cq, axis=-1, keepdims=True) + EPS)
        cqh = cq * rc
        qlg = qlg_ref[...]
        cqn_b = (cqh * qlg).astype(BF16)
        ckv = ulo[:, QL:QL + KVL]
        r0 = lax.rsqrt(jnp.mean(ckv * ckv, axis=-1, keepdims=True) + EPS)
        ckvh = ckv * r0
        kvlg = kvlg_ref[...]
        ckvn_b = (ckvh * kvlg).astype(BF16)
        qhs = [_nt(cqn_b, wuq_ref[hd]) for hd in range(NH)]
        kns = [_nn(ckvn_b, wukv_ref[hd])[:, :128] for hd in range(NH)]
        dqng = jnp.zeros((1, DKP), F32)
        dqraws = []
        for hd in range(NH):
            qh = qhs[hd]
            rq = lax.rsqrt(_rowsum(qh * qh) / DK + EPS)
            xh = qh * rq
            dqh = jnp.where(latent, dq_ref[hd], 0.0)
            dyq = jnp.concatenate([dqh[:, :128], _rope_t(dqh[:, 128:], cos, sin)], axis=1)
            dqng += _colsum(dyq * xh)
            dxh = dyq * qng
            dqraws.append((rq * (dxh - xh * (_rowsum(dxh * xh) / DK))).astype(BF16))
        dqng_ref[...] += dqng

        kr = ulo[:, 384:512]
        skr = _rowsum(kr * kr)
        dkr =jnp.zeros((TB, 128), F32)
        dkng = jnp.zeros((1, DKP), F32)
        dkvs = []
        for hd in range(NH):
            kn = kns[hd]
            rk = lax.rsqrt((_rowsum(kn * kn) + skr) / DK + EPS)
            xh1 = kn * rk
            xh2 = kr * rk
            dkh = dk_ref[hd] * LN2
            d1 = dkh[:, :128]
            d2 = _rope_t(dkh[:, 128:], cos, sin)
            dkng += jnp.concatenate([_colsum(d1 * xh1), _colsum(d2 * xh2)], axis=1)
            dx1 = d1 * kng[:, :128]
            dx2 = d2 * kng[:, 128:]
            dot = (_rowsum(dx1 * xh1) + _rowsum(dx2 * xh2)) / DK
            dkvs.append(jnp.concatenate([rk * (dx1 - xh1 * dot), dv_ref[hd]], axis=1).astype(BF16))
            dkr += rk * (dx2 - xh2 * dot)
        dkng_ref[...] += dkng

        dcqn = jnp.zeros((TB, QL), F32)
        dckvn = jnp.zeros((TB, KVL), F32)
        for hd in range(NH):
            dwuq_ref[hd] += _tn(dqraws[hd], cqn_b)[:DK]
            dcqn += _nn(dqraws[hd], wuq_ref[hd])
            dwukv_ref[hd] += _tn(ckvn_b, dkvs[hd])
            dckvn += _nt(dkvs[hd], wukv_ref[hd])
        dqlg_ref[...] += _colsum(dcqn * cqh)
        dxh = dcqn * qlg
        dcq = rc * (dxh - cqh * jnp.mean(dxh * cqh, axis=-1, keepdims=True))
        dkvlg_ref[...] += _colsum(dckvn * ckvh)
        dxh = dckvn * kvlg
        dckv = r0 * (dxh - ckvh * jnp.mean(dxh * ckvh, axis=-1, keepdims=True))
        dlo_ref[...] = jnp.concatenate([dcq, dckv, dkr], axis=1).astype(BF16)

    row = lambda w: pl.BlockSpec((TB, w), lambda i: (i, 0))
    heads = lambda w: pl.BlockSpec((NH, TB, w), lambda i: (0, i, 0))
    return pl.pallas_call(
        body, name="qkv_bwd", grid=(nb,),
        in_specs=[row(512), pl.BlockSpec((NH, TB, DKP), lambda i: (0, jnp.maximum(i - off, 0), 0)), heads(DKP), heads(DV),
                  pl.BlockSpec((1, 8, 256), lambda i: (jnp.maximum(i - off, 0), 0, 0)), _full((TB, 256)), _full((1, QL)), _full((NH, DKP, QL)), _full((1, KVL)), _full((NH, KVL, 256)),
                  _full((1, DK)), _full((1, DK))],
        out_specs=[row(512), _full((NH, DK, QL)), _full((NH, KVL, 256)), _full((1, QL)), _full((1, KVL)),
                   _full((1, DKP)), _full((1, DKP))],
        out_shape=[jax.ShapeDtypeStruct((t_all, 512), BF16), jax.ShapeDtypeStruct((NH, DK, QL), F32),
                   jax.ShapeDtypeStruct((NH, KVL, 256), F32), jax.ShapeDtypeStruct((1, QL), F32),
                   jax.ShapeDtypeStruct((1, KVL), F32), jax.ShapeDtypeStruct((1, DKP), F32), jax.ShapeDtypeStruct((1, DKP), F32)],
        scratch_shapes=[pltpu.VMEM((2, DKP), F32)],
        compiler_params=_params(("arbitrary",)),
    )(u, dq, dk, dv, cos, sin, q_lora_g, w_uq_t, kv_lora_g, w_ukv, qn_g, kn_g)


def _in_bwd(ctx, x, modsel, norm_g, dlo, dga, dgp, dpool, dxn, w_in_t):
    s_len, lc = x.shape[0], ctx.shape[0]
    t_all = s_len + lc
    off = lc // TB
    nb = t_all // TB
    nq = s_len // TB
    hb = TB // HALO
    n = TB + 2 * HALO

    def body(ctx_ref, x_ref, mod_ref, ng_ref, dlo_ref, dga_ref, dgp_ref, dp_ref, dpprev_ref, dpnext_ref, dxn_ref, win_ref,
             gx_ref, dwin_ref, dmod_ref, dng_ref):
        i = pl.program_id(0)
        j = i - off

        @pl.when(i == 0)
        def _():
            dwin_ref[...] = jnp.zeros_like(dwin_ref)
            dmod_ref[...] = jnp.zeros_like(dmod_ref)
            dng_ref[...] = jnp.zeros_like(dng_ref)

        latent = i >= off
        dp = dp_ref[...]
        prev = jnp.where(j <= 0, 0.0, dpprev_ref[...])
        nxt = jnp.where(j >= nq - 1, 0.0, dpnext_ref[...])
        win = jnp.concatenate([prev, dp, nxt], axis=0)
        tg = j * TB - HALO + lax.broadcasted_iota(jnp.int32, (n, 1), 0)
        dpin = []
        for g, w in enumerate(POOL_WINDOWS):
            cnt = jnp.maximum(jnp.minimum(tg + w // 2, s_len) - jnp.maximum(tg - w // 2, 0), 1).astype(F32)
            zq = win[:, g * 128:(g + 1) * 128] / cnt
            zq = zq + _shift_rows(zq, 1)
            for step in (1, 2, 4):
                if w >= 4 * step:
                    zq = _shift_rows(zq, -step) + _shift_rows(zq, step)
            dpin.append(zq[HALO:HALO + TB] - dp[:, g * 128:(g + 1) * 128])
        zero = jnp.zeros((TB, 512), BF16)
        du = [dlo_ref[...], jnp.where(latent, dga_ref[...], zero),
              jnp.where(latent, jnp.concatenate(dpin, axis=1).astype(BF16), zero), jnp.where(latent, dgp_ref[...], zero)]

        ng = ng_ref[...]
        xb = jnp.where(i < off, ctx_ref[...], x_ref[...])
        r, xh, xg, h, scale = _modulated(xb, mod_ref, ng)
        hb_ = h.astype(BF16)
        dh = jnp.zeros((TB, D), F32)
        for s, (lo, hi) in enumerate(SEG):
            dwin_ref[lo:hi, :] += _tn(du[s], hb_)
            dh += _nn(du[s], win_ref[lo:hi, :])
        is_lat = latent.astype(F32)
        dsh = _colsum(dh)
        dsc = _colsum(dh * xg)
        dmod_ref[0, 0:1, :] += dsh * (1.0 - is_lat)
        dmod_ref[0, 1:2, :] += dsc * (1.0 - is_lat)
        dmod_ref[1, 0:1, :] += dsh * is_lat
        dmod_ref[1, 1:2, :] += dsc * is_lat
        dxg = dh * (1.0 + scale)
        dng_ref[...] += _colsum(dxg * xh)
        dxh = dxg * ng
        gx_ref[...] = r * (dxh - xh * jnp.mean(dxh * xh, axis=-1, keepdims=True)) + dxn_ref[...]

    row = lambda w: pl.BlockSpec((TB, w), lambda i: (i, 0))
    lat = lambda w: pl.BlockSpec((TB, w), lambda i: (jnp.maximum(i - off, 0), 0))
    last8 = s_len // HALO - 1
    cspec, xspec, mspec = _token_specs(off)
    return pl.pallas_call(
        body, name="in_bwd", grid=(nb,),
        in_specs=[cspec, xspec, mspec, _full((1, D)), row(512), lat(512), lat(512), lat(512),
                  pl.BlockSpec((HALO, 512), lambda i: (jnp.maximum(jnp.maximum(i - off, 0) * hb - 1, 0), 0)),
                  pl.BlockSpec((HALO, 512), lambda i: (jnp.minimum((jnp.maximum(i - off, 0) + 1) * hb, last8), 0)),
                  lat(D), _full((DIN, D))],
        out_specs=[lat(D), _full((DIN, D)), _full((2, 2, D)), _full((1, D))],
        out_shape=[jax.ShapeDtypeStruct((s_len, D), F32), jax.ShapeDtypeStruct((DIN, D), F32),
                   jax.ShapeDtypeStruct((2, 2, D), F32), jax.ShapeDtypeStruct((1, D), F32)],
        compiler_params=_params(("arbitrary",)),
    )(ctx, x, modsel, norm_g, dlo, dga, dgp, dpool, dpool, dpool, dxn, w_in_t)


def _adamw_update(w_ref, g_ref, m_ref, v_ref, d_ref, mo_ref, vo_ref):
    gv = g_ref[...]
    mn = ADAM_B1 * m_ref[...] + (1.0 - ADAM_B1) * gv
    vn = ADAM_B2 * v_ref[...] + (1.0 - ADAM_B2) * (gv * gv)
    m_hat = mn / (1.0 - ADAM_B1 ** ADAM_STEP)
    v_hat = vn / (1.0 - ADAM_B2 ** ADAM_STEP)
    d_ref[...] = -ADAM_LR * (m_hat / (jnp.sqrt(v_hat) + ADAM_EPS) + ADAM_WD * w_ref[...])
    mo_ref[...] = mn
    vo_ref[...] = vn


def _adamw_many(ws, gs, ms, vs):
    n = len(ws)
    parts = 8

    def body(*refs):
        for i in range(n):
            _adamw_update(refs[i], refs[n + i], refs[2 * n + i], refs[3 * n + i], refs[4 * n + i], refs[5 * n + i], refs[6 * n + i])
            refs[7 * n + i][...] = refs[n + i][...]

    def spec(w):
        rows, cols = w.shape
        if rows % (8 * parts) == 0:
            return pl.BlockSpec((rows // parts, cols), lambda i: (i, 0))
        if cols % (128 * parts) == 0:
            return pl.BlockSpec((rows, cols // parts), lambda i: (0, i))
        return _full((rows, cols))

    specs = [spec(w) for w in ws]
    shp = [jax.ShapeDtypeStruct(w.shape, F32) for w in ws]
    out = pl.pallas_call(body, name="adamw_many", grid=(parts,), in_specs=specs * 4, out_specs=specs * 4, out_shape=shp * 4,
                         compiler_params=_params(("arbitrary",)))(*ws, *gs, *ms, *vs)
    return out[:n], out[n:2 * n], out[2 * n:3 * n], out[3 * n:]


class _Links:
    def __init__(self, send_sems, recv_sems):
        self.send_sems, self.recv_sems, self.sends = send_sems, recv_sems, []

    def send(self, src, dst, sem, to):
        cp = pltpu.make_async_remote_copy(src, dst, self.send_sems.at[sem], self.recv_sems.at[sem], device_id=to,
                                          device_id_type=MESH)
        cp.start()
        self.sends.append(cp)

    def arrived(self, dst, sem, frm):
        pltpu.make_async_remote_copy(dst, dst, self.send_sems.at[sem], self.recv_sems.at[sem], device_id=frm,
                                     device_id_type=MESH).wait_recv()

    def drain(self):
        for cp in self.sends:
            cp.wait_send()


def _half(ref, c, axis):
    size = ref.shape[axis - 2] // 2
    win = pl.ds(pl.multiple_of(c * size, 16 if axis == 0 else 128), size)
    idx = (win, slice(None)) if axis == 0 else (slice(None), win)
    return ref.at[(slice(None),) * (len(ref.shape) - 2) + idx]


def _select_rows(slots_ref, n_slots, row=0):
    sub = lax.broadcasted_iota(jnp.int32, (8, 1), 0)
    out = None
    for d in range(n_slots):
        r = jnp.where(sub == d, jnp.broadcast_to(slots_ref[d][row:row + 1, :], (8, slots_ref.shape[-1])), 0.0)
        out = r if out is None else out + r
    return out


def _gather(c, c_ctx, w_mod, b_mod, shards, axes, slab_rows):
    nw = len(shards)
    kw = w_mod.shape[1]

    def body(*refs):
        c_ref, cc_ref, wm_hbm, b_ref = refs[:4]
        w_hbm = refs[4:4 + nw]
        a16_ref, modsel_ref = refs[4 + nw:6 + nw]
        out_refs = refs[6 + nw:6 + 2 * nw]
        g_refs = refs[6 + 2 * nw:6 + 3 * nw]
        f_refs = refs[6 + 3 * nw:6 + 4 * nw]
        wm_ref, a_ref, mod_ref, send_sems, recv_sems, local_sems = refs[6 + 4 * nw:]
        loads = [pltpu.make_async_copy(w_hbm[wi], f_refs[wi], local_sems.at[wi]) for wi in range(nw)]
        loads.append(pltpu.make_async_copy(wm_hbm, wm_ref, local_sems.at[nw]))
        for cp in loads:
            cp.start()
        stores = []

        def slab(wi, chip, of=g_refs):
            return of[wi].at[chip].at[0:shards[wi].shape[0]]

        def store(src, dst, wi, slot):
            cp = pltpu.make_async_copy(src, dst, local_sems.at[nw + 1 + wi * 8 + slot])
            cp.start()
            stores.append(cp)

        def store_half(wi, chip, half, slot):
            store(_half(slab(wi, chip), half, axes[wi]), _half(slab(wi, chip, out_refs), half, axes[wi]), wi, slot)
        x, y, cc = lax.axis_index("x"), lax.axis_index("y"), lax.axis_index("c")
        me = 4 * x + 2 * y + cc
        k = 2 * x + y
        sibling = (x, y, 1 - cc)
        links = _Links(send_sems, recv_sems)
        chips = [_peer(x, y, cc, off + (0,)) for off in CHIPS3]
        chip_a = ((x + 1 - cc) % 2, (y + cc) % 2, cc)
        chip_b = ((x + cc) % 2, (y + 1 - cc) % 2, cc)
        chip_d = (1 - x, 1 - y, cc)
        cv = c_ref[...]
        sc = cv * _sig(cv)
        mine = a_ref.at[me]
        for r in range(8):
            mine[r:r + 1, :] = sc[:, r * 128:(r + 1) * 128]
        for j, off in enumerate(PEERS7):
            links.send(a_ref.at[me], a_ref.at[me], j, _peer(x, y, cc, off))
        for wi in range(nw):
            loads[wi].wait()
            slab(wi, k)[...] = f_refs[wi][...].astype(BF16)
            for j, to in enumerate((chip_a, chip_b)):
                links.send(_half(slab(wi, k), cc, axes[wi]), _half(slab(wi, k), cc, axes[wi]), 10 + wi * 6 + j, to)
            store(slab(wi, k), slab(wi, k, out_refs), wi, 0)
            rows = shards[wi].shape[0]
            pad = slab_rows[wi] - rows
            if pad:
                for kk in range(4):
                    g_refs[wi][kk, rows:, :] = jnp.zeros((pad, shards[wi].shape[1]), BF16)
                store(g_refs[wi].at[:, pl.ds(rows, pad), :], out_refs[wi].at[:, pl.ds(rows, pad), :], wi, 7)
        for j, off in enumerate(PEERS7):
            px, py, pc = _peer(x, y, cc, off)
            links.arrived(a_ref.at[4 * px + 2 * py + pc], j, (px, py, pc))
        ccv = cc_ref[...]
        sub = lax.broadcasted_iota(jnp.int32, (8, 1), 0)
        top = jnp.zeros((8, D), F32)
        for d in range(8):
            blk = a_ref[d]
            row = jnp.concatenate([blk[r:r + 1, :] for r in range(8)], axis=1)
            top = top + jnp.where(sub == d, jnp.broadcast_to(row, (8, D)), 0.0)
        a16 = jnp.concatenate([top, jnp.where(sub == 0, jnp.broadcast_to(ccv * _sig(ccv), (8, D)), 0.0)], axis=0)
        a16_ref[...] = a16
        loads[nw].wait()
        b_k = jnp.zeros((1, kw), F32)
        for kk in range(4):
            b_k = b_k + jnp.where(k == kk, b_ref[:, kk * kw:(kk + 1) * kw], 0.0)
        mod_ref[k] = _dot3(_nn, a16, wm_ref[...]) + b_k
        for j, to in enumerate(chips):
            links.send(mod_ref.at[k], mod_ref.at[k], 7 + j, to)

        def over_ici(j, frm, origin):
            for wi in range(nw):
                blk = _half(slab(wi, 2 * origin[0] + origin[1]), cc, axes[wi])
                links.arrived(blk, 10 + wi * 6 + j, frm)
                if j == 0:
                    links.send(blk, blk, 10 + wi * 6 + 2, chip_b)
                links.send(blk, blk, 10 + wi * 6 + 3 + j, sibling)
                store_half(wi, 2 * origin[0] + origin[1], cc, 1 + j)

        def from_sibling(j, origin):
            for wi in range(nw):
                links.arrived(_half(slab(wi, 2 * origin[0] + origin[1]), 1 - cc, axes[wi]), 10 + wi * 6 + 3 + j, sibling)
                store_half(wi, 2 * origin[0] + origin[1], 1 - cc, 4 + j)
        over_ici(0, chip_a, chip_a)
        over_ici(1, chip_b, chip_b)
        from_sibling(0, chip_b)
        over_ici(2, chip_b, chip_d)
        from_sibling(1, chip_a)
        from_sibling(2, chip_d)
        for j, (px, py, pc) in enumerate(chips):
            links.arrived(mod_ref.at[2 * px + py], 7 + j, (px, py, pc))
        for sel, row in ((0, 8), (1, me)):
            for kk in range(4):
                piece = mod_ref[kk, pl.ds(row, 1), :]
                lo = kk * kw
                while lo < (kk + 1) * kw:
                    r, col = divmod(lo, D)
                    n = min((kk + 1) * kw - lo, D - col)
                    modsel_ref[sel, r:r + 1, col:col + n] = piece[:, lo - kk * kw:lo - kk * kw + n]
                    lo += n
        links.drain()
        for cp in stores:
            cp.wait()

    nsem = 10 + 6 * nw
    gathered = [(4, r, s.shape[1]) for r, s in zip(slab_rows, shards)]
    return pl.pallas_call(
        body, name="gather", in_specs=[VM, VM, ANY, VM] + [ANY] * nw, out_specs=[VM, VM] + [ANY] * nw,
        out_shape=[jax.ShapeDtypeStruct((16, D), F32), jax.ShapeDtypeStruct((2, 3, D), F32)]
        + [jax.ShapeDtypeStruct(g, BF16) for g in gathered],
        scratch_shapes=[pltpu.VMEM(g, BF16) for g in gathered] + [pltpu.VMEM(s.shape, F32) for s in shards]
        + [pltpu.VMEM(w_mod.shape, F32), pltpu.VMEM((8, 8, D // 8), F32), pltpu.VMEM((4, 16, kw), F32),
           pltpu.SemaphoreType.DMA((nsem,)),
           pltpu.SemaphoreType.DMA((nsem,)), pltpu.SemaphoreType.DMA((nw + 1 + 8 * nw,))],
        compiler_params=pltpu.CompilerParams(vmem_limit_bytes=VMEM_LIMIT),
    )(c, c_ctx, w_mod, b_mod, *shards)


SMALL_ROW_WIDTHS = (D, QL, KVL, DKP, DKP, 512, 128)
SMALL_OUT_WIDTHS = (D, QL, KVL, DK, DK, 512, 1)
SMALL_PACK = 384


def _small_pieces():
    pieces = []
    for i, w in enumerate(SMALL_ROW_WIDTHS):
        for c0 in range(0, w, 128):
            j = len(pieces)
            pieces.append((i, c0, j // (SMALL_PACK // 128), j % (SMALL_PACK // 128) * 128))
    assert len(pieces) <= 8 * (SMALL_PACK // 128)
    return pieces


def _reduce(grads, axes, smalls, w_pool_g, dmod, dgate, a16, w_mod, c_ctx):
    nw = len(grads)
    ns = len(smalls)
    kw = w_mod.shape[1]
    halves = []
    for g, ax in zip(grads, axes):
        halves.append((g.shape[1] // 2, g.shape[2]) if ax == 0 else (g.shape[1], g.shape[2] // 2))

    def body(*refs):
        g_refs = refs[:nw]
        small_refs = refs[nw:nw + ns]
        wp_ref, dmod_ref, dgate_ref, a16_ref, wm_hbm, cc_ref = refs[nw + ns:nw + ns + 6]
        o = nw + ns + 6
        r_outs = refs[o:o + nw]
        small_outs = refs[o + nw:o + nw + ns]
        rwp_ref, gw_out, gb_ref, gc_ref = refs[o + nw + ns:o + nw + ns + 4]
        o = o + nw + ns + 4
        own, sib, part, got, rel, r_refs = (refs[o + i * nw:o + (i + 1) * nw] for i in range(6))
        smbuf, wps, wpg, dm_all, pc_all, wm_ref, gw_ref, send_sems, recv_sems, local_sems = refs[o + 6 * nw:]
        wm_load = pltpu.make_async_copy(wm_hbm, wm_ref, local_sems.at[nw])
        wm_load.start()
        stores = []

        def store(src, dst, sem):
            cp = pltpu.make_async_copy(src, dst, local_sems.at[nw + 1 + sem])
            cp.start()
            stores.append(cp)
        x, y, cc = lax.axis_index("x"), lax.axis_index("y"), lax.axis_index("c")
        me = 4 * x + 2 * y + cc
        k = 2 * x + y
        sibling = (x, y, 1 - cc)
        links = _Links(send_sems, recv_sems)
        chips = [_peer(x, y, cc, off + (0,)) for off in CHIPS3]
        peers = [_peer(x, y, cc, off) for off in PEERS7]
        chip_a = ((x + 1 - cc) % 2, (y + cc) % 2, cc)
        chip_b = ((x + cc) % 2, (y + 1 - cc) % 2, cc)
        ka, kb, kd = 2 * chip_a[0] + chip_a[1], 2 * chip_b[0] + chip_b[1], 2 * (1 - x) + (1 - y)
        big, sm0, wp0, dm0, pc0 = 0, 5 * nw, 5 * nw + 7, 5 * nw + 14, 5 * nw + 21

        locals_ = []
        for wi in range(nw):
            lc = pltpu.make_async_copy(_half(g_refs[wi], cc, axes[wi]), own[wi], local_sems.at[wi])
            lc.start()
            locals_.append(lc)
            links.send(_half(g_refs[wi], 1 - cc, axes[wi]), sib[wi], big + wi * 5, sibling)
        slot = smbuf.at[me]
        slot[...] = jnp.zeros((8, SMALL_PACK), F32)
        small_rows = [jnp.broadcast_to(ref[...], (1, w)) for ref, w in zip(small_refs, SMALL_ROW_WIDTHS)]
        for i, c0, row, lane in _small_pieces():
            slot[row:row + 1, lane:lane + 128] = small_rows[i][:, c0:c0 + 128]
        links.send(wp_ref, wps, wp0, sibling)
        dm_mine = dm_all.at[me]
        dm_mine[...] = jnp.zeros((8, kw), F32)
        dm_rows = ((0, (dmod_ref[1, 0:1, :], dmod_ref[1, 1:2, :], dgate_ref[...])), (4, (dmod_ref[0, 0:1, :], dmod_ref[0, 1:2, :])))
        for base, rows in dm_rows:
            for r, v in enumerate(rows):
                lo = r * D
                while lo < (r + 1) * D:
                    kk, col = divmod(lo, kw)
                    n = min((r + 1) * D - lo, kw - col)
                    dm_mine[base + kk:base + kk + 1, col:col + n] = v[:, lo - r * D:lo - r * D + n]
                    lo += n
        for j, peer in enumerate(peers):
            links.send(dm_all.at[me], dm_all.at[me], dm0 + j, peer)
            links.send(smbuf.at[me], smbuf.at[me], sm0 + j, peer)
        links.arrived(wps, wp0, sibling)
        wpg[k] = (wp_ref[...] + wps[...]).astype(BF16)
        for j, to in enumerate(chips):
            links.send(wpg.at[k], wpg.at[k], wp0 + 1 + j, to)
        for wi in range(nw):
            locals_[wi].wait()
            links.arrived(sib[wi], big + wi * 5, sibling)
            part[wi][...] = (own[wi][...] + sib[wi][...]).astype(BF16)
            got[wi][k] = part[wi][k]
            got[wi][kd] = jnp.zeros(halves[wi], BF16)
            links.send(part[wi].at[kd], rel[wi], big + wi * 5 + 1, chip_b)
            links.send(part[wi].at[kb], got[wi].at[k], big + wi * 5 + 2, chip_b)
        for j, (px, py, pc) in enumerate(peers):
            links.arrived(dm_all.at[4 * px + 2 * py + pc], dm0 + j, (px, py, pc))
        dm_tot = dm_all[0]
        for d in range(1, 8):
            dm_tot = dm_tot + dm_all[d]
        for kk in range(4):
            gb_ref[:, kk * kw:(kk + 1) * kw] = dm_tot[kk:kk + 1, :] + dm_tot[4 + kk:5 + kk, :]
        top = jnp.zeros((8, kw), F32)
        dmc_k = jnp.zeros((1, kw), F32)
        for kk in range(4):
            top = top + jnp.where(k == kk, _select_rows(dm_all, 8, kk), 0.0)
            dmc_k = dmc_k + jnp.where(k == kk, dm_tot[4 + kk:5 + kk, :], 0.0)
        sub = lax.broadcasted_iota(jnp.int32, (8, 1), 0)
        bk = jnp.concatenate([top, jnp.where(sub == 0, jnp.broadcast_to(dmc_k, (8, kw)), 0.0)], axis=0)
        gw_ref[...] = _dot3(_tn, a16_ref[...], bk)
        store(gw_ref, gw_out, 0)
        wm_load.wait()
        pc_all[k] =_dot3(_nt, jnp.broadcast_to(bk[8:9, :], (8, kw)), wm_ref[...])
        for j, to in enumerate(chips):
            links.send(pc_all.at[k], pc_all.at[k], pc0 + j, to)
        for j, (px, py, pc) in enumerate(peers):
            links.arrived(smbuf.at[4 * px + 2 * py + pc], sm0 + j, (px, py, pc))
        tot = smbuf[0]
        for d in range(1, 8):
            tot = tot + smbuf[d]
        for i, c0, row, lane in _small_pieces():
            n = min(128, SMALL_OUT_WIDTHS[i] - c0)
            if n > 0:
                small_outs[i][:, c0:c0 + n] = tot[row:row + 1, lane:lane + n]
        for j, (px, py, pc) in enumerate(chips):
            links.arrived(wpg.at[2 * px + py], wp0 + 1 + j, (px, py, pc))
        wpt = wpg[0].astype(F32)
        for kk in range(1, 4):
            wpt = wpt + wpg[kk].astype(F32)
        rwp_ref[...] = wpt
        for wi in range(nw):
            links.arrived(rel[wi], big + wi * 5 + 1, chip_b)
            rel[wi][...] = (part[wi][ka].astype(F32) + rel[wi][...].astype(F32)).astype(BF16)
            links.send(rel[wi], got[wi].at[k], big + wi * 5 + 3, chip_a)
        for wi in range(nw):
            links.arrived(got[wi].at[kb], big + wi * 5 + 2, chip_b)
            links.arrived(got[wi].at[ka], big + wi * 5 + 3, chip_a)
            total = got[wi][0].astype(F32)
            for kk in range(1, 4):
                total = total + got[wi][kk].astype(F32)
            mine = _half(r_refs[wi], cc, axes[wi])
            mine[...] = total
            links.send(mine, mine, big + wi * 5 + 4, sibling)
            store(mine, _half(r_outs[wi], cc, axes[wi]), 1 + wi)
        for j, (px, py, pc) in enumerate(chips):
            links.arrived(pc_all.at[2 * px + py], pc0 + j, (px, py, pc))
        ccv = cc_ref[...]
        sg = _sig(ccv)
        gc_ref[...] = (pc_all[0][0:1, :] + pc_all[1][0:1, :] + pc_all[2][0:1, :] + pc_all[3][0:1, :]) * (sg * (1.0 + ccv * (1.0 - sg)))
        for wi in range(nw):
            links.arrived(_half(r_refs[wi], 1 - cc, axes[wi]), big + wi * 5 + 4, sibling)
            store(_half(r_refs[wi], 1 - cc, axes[wi]), _half(r_outs[wi], 1 - cc, axes[wi]), 1 + nw + wi)
        links.drain()
        for cp in stores:
            cp.wait()

    nsem = 5 * nw + 24
    quads = [(4,) + h for h in halves]
    return pl.pallas_call(
        body, name="reduce", in_specs=[ANY] * nw + [VM] * (ns + 4) + [ANY, VM],
        out_specs=[ANY] * nw + [VM] * (ns + 1) + [ANY, VM, VM],
        out_shape=[jax.ShapeDtypeStruct(g.shape[1:], F32) for g in grads]
        + [jax.ShapeDtypeStruct((1, w), F32) for w in SMALL_OUT_WIDTHS]
        + [jax.ShapeDtypeStruct(w_pool_g.shape, F32), jax.ShapeDtypeStruct((D, kw), F32), jax.ShapeDtypeStruct((1, 3 * D), F32),
           jax.ShapeDtypeStruct((1, D), F32)],
        scratch_shapes=[pltpu.VMEM(q, F32) for q in quads] + [pltpu.VMEM(q, F32) for q in quads]
        + [pltpu.VMEM(q, BF16) for q in quads] + [pltpu.VMEM(q, BF16) for q in quads] + [pltpu.VMEM(h, BF16) for h in halves]
        + [pltpu.VMEM(g.shape[1:], F32) for g in grads]
        + [pltpu.VMEM((8, 8, SMALL_PACK), F32), pltpu.VMEM(w_pool_g.shape, F32), pltpu.VMEM((4,) + w_pool_g.shape, BF16),
           pltpu.VMEM((8, 8, kw), F32), pltpu.VMEM((4, 8, D), F32), pltpu.VMEM((D, kw), F32), pltpu.VMEM((D, kw), F32)]
        + [pltpu.SemaphoreType.DMA((nsem,)), pltpu.SemaphoreType.DMA((nsem,)), pltpu.SemaphoreType.DMA((3 * nw + 2,))],
        compiler_params=pltpu.CompilerParams(vmem_limit_bytes=VMEM_LIMIT),
    )(*grads, *smalls, w_pool_g, dmod, dgate, a16, w_mod, c_ctx)


def _rope_tables(s_len):
    rows = s_len // GRID_W
    per = TB // GRID_W
    n_freq = 16
    f32 = np.float32
    inv = f32(ROPE_BASE) ** (-np.arange(n_freq, dtype=f32) / f32(n_freq))
    ang_r = np.arange(rows, dtype=f32)[:, None] * inv
    ang_c = np.arange(GRID_W, dtype=f32)[:, None] * inv
    by_row, by_col = [], []
    for fn, pad in ((np.cos, 1.0), (np.sin, 0.0)):
        r = np.concatenate([fn(ang_r), fn(ang_r), np.zeros((rows, 96), f32)], axis=1).reshape(rows // per, per, 128)
        by_row.append(np.pad(r, ((0, 0), (0, 8 - per), (0, 0))))
        cpart = np.concatenate([np.zeros((GRID_W, 32), f32), fn(ang_c), fn(ang_c), np.full((GRID_W, 64), pad, f32)], axis=1)
        by_col.append(np.tile(cpart, (per, 1)))
    return jnp.asarray(np.concatenate(by_row, axis=-1), F32), jnp.asarray(np.concatenate(by_col, axis=-1), F32)


def kernel(x, c, ctx, c_ctx, w_mod, b_mod, norm_g, w_in, q_lora_g, w_uq, kv_lora_g, w_ukv, q_norm_g, k_norm_g, w_pool, pool_scale, w_out, loss_target, m_c_ctx, m_w_mod, m_b_mod, m_norm_g, m_w_in, m_q_lora_g, m_w_uq, m_kv_lora_g, m_w_ukv, m_q_norm_g, m_k_norm_g, m_w_pool, m_pool_scale, m_w_out, v_c_ctx, v_w_mod, v_b_mod, v_norm_g, v_w_in, v_q_lora_g, v_w_uq, v_kv_lora_g, v_w_ukv, v_q_norm_g, v_k_norm_g, v_w_pool, v_pool_scale, v_w_out):
    s_len = x.shape[1]
    lc = ctx.shape[1]
    weights = dict(c_ctx=c_ctx, w_mod=w_mod, b_mod=b_mod, norm_g=norm_g, w_in=w_in, q_lora_g=q_lora_g, w_uq=w_uq,
                   kv_lora_g=kv_lora_g, w_ukv=w_ukv, q_norm_g=q_norm_g, k_norm_g=k_norm_g, w_pool=w_pool,
                   pool_scale=pool_scale, w_out=w_out)
    m_in = dict(c_ctx=m_c_ctx, w_mod=m_w_mod, b_mod=m_b_mod, norm_g=m_norm_g, w_in=m_w_in, q_lora_g=m_q_lora_g, w_uq=m_w_uq,
                kv_lora_g=m_kv_lora_g, w_ukv=m_w_ukv, q_norm_g=m_q_norm_g, k_norm_g=m_k_norm_g, w_pool=m_w_pool,
                pool_scale=m_pool_scale, w_out=m_w_out)
    v_in = dict(c_ctx=v_c_ctx, w_mod=v_w_mod, b_mod=v_b_mod, norm_g=v_norm_g, w_in=v_w_in, q_lora_g=v_q_lora_g, w_uq=v_w_uq,
                kv_lora_g=v_kv_lora_g, w_ukv=v_w_ukv, q_norm_g=v_q_norm_g, k_norm_g=v_k_norm_g, w_pool=v_w_pool,
                pool_scale=v_pool_scale, w_out=v_w_out)
    order = ["c_ctx", "w_mod", "b_mod", "norm_g", "w_in", "q_lora_g", "w_uq", "kv_lora_g", "w_ukv", "q_norm_g", "k_norm_g",
             "w_pool", "pool_scale", "w_out"]
    transposed = ("w_in", "w_uq")
    as2d = lambda n, a: jnp.transpose(a[0]) if n in transposed else a.reshape(-1, a.shape[-1])
    back = lambda n, a: jnp.transpose(a)[None] if n in transposed else a.reshape(weights[n].shape)

    c_ctx2 = c_ctx.reshape(1, D)
    split = (0, 1, 0, 0)
    a16, modsel, g_out, g_in, g_uq, g_ukv = _gather(
        c, c_ctx2, w_mod[0], b_mod, [w_out[0], as2d("w_in", w_in), as2d("w_uq", w_uq), w_ukv[0]], split,
        (D // 4, DIN // 4, DKP, KVL))
    w_in_t = g_in.reshape(DIN, D)
    w_uq_t = g_uq
    w_out_f = g_out.reshape(D, D)
    cos, sin = _rope_tables(s_len)

    u, q, kk, v = _fwd_in(ctx[0], x[0], modsel, norm_g, w_in_t, q_lora_g, w_uq_t, kv_lora_g, g_ukv, q_norm_g, k_norm_g, cos, sin)
    attn, lse = _attn_fwd(q, kk, v, s_len)
    (dxn, dattn, dga, dgp, dpool, dw_out, dgate, dps, dw_pool, loss) = _out_stage(
        attn.reshape(s_len // Q_BLOCK, Q_BLOCK, NH * DV), u, x[0], loss_target[0], modsel, w_pool[0], pool_scale,
        w_out_f, lc)
    dattn = dattn.reshape(s_len, NH * DV)
    dq, dk, dv = _attn_bwd(q, kk, v, dattn, attn, lse, s_len)
    dlo, dw_uq_t, dw_ukv, dqlg, dkvlg, dqng, dkng = _qkv_bwd(u, dq, dk, dv, cos, sin, q_lora_g, w_uq_t, kv_lora_g, g_ukv,
                                                            q_norm_g, k_norm_g, s_len)
    gx, dw_in_t, dmod, dng = _in_bwd(ctx[0], x[0], modsel, norm_g, dlo, dga, dgp, dpool, dxn, w_in_t)

    r_out, r_in, r_uq, r_ukv, g_ng, g_qlg, g_kvlg, g_qng, g_kng, g_ps, loss_all, g_wp, g_w_mod, g_b_mod, g_c_ctx = _reduce(
        [dw_out.reshape(4, D // 4, D), dw_in_t.reshape(4, DIN // 4, D), dw_uq_t, dw_ukv], split,
        [dng, dqlg, dkvlg, dqng, dkng, dps, loss], dw_pool, dmod, dgate, a16, w_mod[0], c_ctx2)
    g2d = dict(c_ctx=g_c_ctx, b_mod=g_b_mod, w_mod=g_w_mod, w_in=r_in, w_uq=r_uq, w_ukv=r_ukv, w_out=r_out, norm_g=g_ng,
               q_lora_g=g_qlg, kv_lora_g=g_kvlg, q_norm_g=g_qng, k_norm_g=g_kng, pool_scale=g_ps, w_pool=g_wp.reshape(512, 128))

    outs = _adamw_many([as2d(n, weights[n]) for n in order], [g2d[n] for n in order], [as2d(n, m_in[n]) for n in order],
                       [as2d(n, v_in[n]) for n in order])
    d2d, m2d, v2d, g2d = (dict(zip(order, arrs)) for arrs in outs)

    return (loss_all[0, 0], gx[None], *[back(n, g2d[n]) for n in order], *[back(n, d2d[n]) for n in order],
            *[back(n, m2d[n]) for n in order], *[back(n, v2d[n]) for n in order])
```

```python
import jax
import jax.numpy as jnp
import numpy as np
from jax import lax
from jax.experimental import pallas as pl
from jax.experimental.pallas import tpu as pltpu

F32 = jnp.float32
BF16 = jnp.bfloat16
MESH = pl.DeviceIdType.MESH

D = 1024
NH = 4
DK = 192
DKP = 256
DV = 128
QL = 256
KVL = 128
DIN = 1984
U_LO = 448
SEG = ((0, 512), (448, 960), (960, 1472), (1472, 1984))
DU = 2048
POOL_WINDOWS = (2, 4, 8, 16)
HALO = 8
EPS = 1e-6
ROPE_BASE = 10000.0
GRID_W = 64
Q_BLOCK = 128
TB = 256
BWD_QBLOCKS = 1
SCALE = DK ** -0.5
LOG2E = 1.4426950408889634
LN2 = 0.6931471805599453
VMEM_LIMIT = 56 * 1024 * 1024

ADAM_LR = 0.001
ADAM_B1 = 0.9
ADAM_B2 = 0.999
ADAM_EPS = 1e-08
ADAM_WD = 0.01
ADAM_STEP = 10

CHIPS3 = ((1, 0), (0, 1), (1, 1))
PEERS7 = tuple((dx, dy, dc) for dx in (0, 1) for dy in (0, 1) for dc in (0, 1) if (dx, dy, dc) != (0, 0, 0))

VM = pl.BlockSpec(memory_space=pltpu.VMEM)
ANY = pl.BlockSpec(memory_space=pl.ANY)


def _nn(a, b):
    return jnp.dot(a, b, preferred_element_type=F32)


def _nt(a, b):
    return lax.dot_general(a, b, (((1,), (1,)), ((), ())), preferred_element_type=F32)


def _tn(a, b):
    return lax.dot_general(a, b, (((0,), (0,)), ((), ())), preferred_element_type=F32)


def _split3(a):
    a0 = a.astype(BF16)
    r = a - a0.astype(F32)
    a1 = r.astype(BF16)
    a2 = (r - a1.astype(F32)).astype(BF16)
    return a0, a1, a2


def _dot3(dot, a, b):
    sa = _split3(a)
    sb = _split3(b)
    out = None
    for i in range(3):
        for j in range(3 - i):
            t = dot(sa[i], sb[j])
            out = t if out is None else out + t
    return out


def _sig(x):
    return 1.0 / (1.0 + jnp.exp(-x))


def _rot(t):
    src = lax.broadcasted_iota(jnp.int32, (128, 128), 0)
    dst = lax.broadcasted_iota(jnp.int32, (128, 128), 1)
    first = (dst % 32) < 16
    perm = jnp.where(first & (src == dst + 16), -1.0, jnp.where(~first & (src == dst - 16), 1.0, 0.0)).astype(BF16)
    hi = t.astype(BF16)
    lo = (t - hi.astype(F32)).astype(BF16)
    return _nn(hi, perm) + _nn(lo, perm)


def _rope(t, cos, sin):
    return t * cos + _rot(t) * sin


def _rope_t(t, cos, sin):
    return t * cos - _rot(t * sin)


def _rope_block(rows_ref, cols_ref, is_ctx):
    lane = lax.broadcasted_iota(jnp.int32, (TB, 256), 1) % 128
    rows = jnp.concatenate([jnp.broadcast_to(rows_ref[0, r:r + 1, :], (GRID_W, 256)) for r in range(TB // GRID_W)], axis=0)
    cs = jnp.where(lane < 32, rows, cols_ref[...])
    return jnp.where(is_ctx, 1.0, cs[:, :128]), jnp.where(is_ctx, 0.0, cs[:, 128:])


def _shift_rows(z, k):
    n = z.shape[0]
    return pltpu.roll(z, (n - k) % n, 0)


def _colsum(a):
    return jnp.sum(a, axis=0, keepdims=True)


def _rowsum(a):
    return jnp.sum(a, axis=-1, keepdims=True)


def _row_layout(col):
    return jnp.transpose(jnp.broadcast_to(col, (col.shape[0], 128)))[0:8, :]


def _params(sem=None):
    return pltpu.CompilerParams(dimension_semantics=sem, vmem_limit_bytes=VMEM_LIMIT)


def _full(shape):
    nd = len(shape)
    return pl.BlockSpec(shape, lambda *_: (0,) * nd)


def _peer(x, y, c, off):
    dx, dy, dc = off
    return ((x + dx) % 2, (y + dy) % 2, (c + dc) % 2)


def _token_specs(off):
    ctx = pl.BlockSpec((TB, D), lambda i: (jnp.minimum(i, off - 1), 0))
    lat = pl.BlockSpec((TB, D), lambda i: (jnp.maximum(i - off, 0), 0))
    mod = pl.BlockSpec((1, 3, D), lambda i: (jnp.minimum(i // off, 1), 0, 0))
    return ctx, lat, mod


def _head_gains(qng_ref, kng_ref, pad_ref):
    pad_ref[...] = jnp.zeros((2, DKP), F32)
    pad_ref[0:1, 0:DK] = qng_ref[...]
    pad_ref[1:2, 0:DK] = kng_ref[...]
    return pad_ref[0:1, :], pad_ref[1:2, :]


def _modulated(x, mod_ref, ng):
    shift = mod_ref[0, 0:1, :]
    scale = mod_ref[0, 1:2, :]
    r = lax.rsqrt(jnp.mean(x * x, axis=-1, keepdims=True) + EPS)
    xh = x * r
    xg = xh * ng
    return r, xh, xg, xg * (1.0 + scale) + shift, scale


def _fwd_in(ctx, x, modsel, norm_g, w_in_t, q_lora_g, w_uq_t, kv_lora_g, w_ukv, qn_g, kn_g, cos, sin):
    s_len, lc = x.shape[0], ctx.shape[0]
    t_all = s_len + lc
    nb = t_all // TB
    off = lc // TB

    def body(ctx_ref, x_ref, mod_ref, ng_ref, win_ref, qlg_ref, wuq_ref, kvlg_ref, wukv_ref, qng_ref, kng_ref, cos_ref, sin_ref,
             u_ref, q_ref, k_ref, v_ref, pad_ref):
        is_ctx = pl.program_id(0) < off
        qng, kng = _head_gains(qng_ref, kng_ref, pad_ref)
        xb = jnp.where(is_ctx, ctx_ref[...], x_ref[...])
        _, _, _, h, _ = _modulated(xb, mod_ref, ng_ref[...])
        hb = h.astype(BF16)
        lane = lax.broadcasted_iota(jnp.int32, (TB, 512), 1)
        ulo = jnp.where(lane < U_LO, _nt(hb, win_ref[SEG[0][0]:SEG[0][1], :]), 0.0)
        u_ref[:, 0:512] = ulo
        for j in range(1, 4):
            u_ref[:, j * 512:(j + 1) * 512] = _nt(hb, win_ref[SEG[j][0]:SEG[j][1], :])
        cos, sin = _rope_block(cos_ref, sin_ref, is_ctx)
        cq = ulo[:, 0:QL]
        cqn = (cq * lax.rsqrt(jnp.mean(cq * cq, axis=-1, keepdims=True) + EPS) * qlg_ref[...]).astype(BF16)
        ckv = ulo[:, QL:QL + KVL]
        ckvn = (ckv * lax.rsqrt(jnp.mean(ckv * ckv, axis=-1, keepdims=True) + EPS) * kvlg_ref[...]).astype(BF16)
        qhs = [_nt(cqn, wuq_ref[hd]) for hd in range(NH)]
        kvs = [_nn(ckvn, wukv_ref[hd]) for hd in range(NH)]
        for hd in range(NH):
            qh = qhs[hd]
            qn = qh * lax.rsqrt(_rowsum(qh * qh) / DK + EPS) * qng
            q_ref[hd] = (jnp.concatenate([qn[:, :128], _rope(qn[:, 128:], cos, sin)], axis=1) * (SCALE * LOG2E)).astype(BF16)
        kr = ulo[:, 384:512]
        skr = _rowsum(kr * kr)
        kr_roped =_rope(kr * kng[:, 128:], cos, sin)
        for hd in range(NH):
            kv = kvs[hd]
            kn = kv[:, :128]
            rk = lax.rsqrt((_rowsum(kn * kn) + skr) / DK + EPS)
            k_ref[hd] = jnp.concatenate([kn * rk * kng[:, :128], kr_roped * rk], axis=1).astype(BF16)
            v_ref[hd] = kv[:, 128:].astype(BF16)

    row = lambda w: pl.BlockSpec((TB, w), lambda i: (i, 0))
    heads = lambda w: pl.BlockSpec((NH, TB, w), lambda i: (0, i, 0))
    cspec, xspec, mspec = _token_specs(off)
    return pl.pallas_call(
        body, name="fwd_in", grid=(nb,),
        in_specs=[cspec, xspec, mspec, _full((1, D)), _full((DIN, D)), _full((1, QL)), _full((NH, DKP, QL)), _full((1, KVL)),
                  _full((NH, KVL, 256)), _full((1, DK)), _full((1, DK)),
                  pl.BlockSpec((1, 8, 256), lambda i: (jnp.maximum(i - off, 0), 0, 0)), _full((TB, 256))],
        out_specs=[row(DU), heads(DKP), heads(DKP), heads(DV)],
        out_shape=[jax.ShapeDtypeStruct((t_all, DU), F32), jax.ShapeDtypeStruct((NH, t_all, DKP), BF16),
                   jax.ShapeDtypeStruct((NH, t_all, DKP), BF16), jax.ShapeDtypeStruct((NH, t_all, DV), BF16)],
        scratch_shapes=[pltpu.VMEM((2, DKP), F32)],
        compiler_params=_params(("arbitrary",)),
    )(ctx, x, modsel, norm_g, w_in_t, q_lora_g, w_uq_t, kv_lora_g, w_ukv, qn_g, kn_g, cos, sin)


def _attn_fwd(q, k, v, s_len):
    t_all = q.shape[1]
    off = (t_all - s_len) // TB
    nq = s_len // TB
    nsub = next(n for n in (4, 2, 1) if nq % n == 0)

    def body(*refs):
        q_refs = refs[:nsub]
        k_ref, v_ref, o_ref, lse_ref = refs[nsub:]
        for sb in range(nsub):
            s = _nt(q_refs[sb][0], k_ref[0])
            m = jnp.max(s, axis=-1, keepdims=True)
            e = jnp.exp2(s - m)
            l = _rowsum(e)
            o_ref[sb * TB:(sb + 1) * TB, :] = _nn(e.astype(BF16), v_ref[0]) / l
            lse_ref[0, sb] = _row_layout(m + jnp.log2(l))

    qspec = lambda sb: pl.BlockSpec((1, TB, DKP), lambda h, i: (h, i * nsub + sb + off, 0))
    return pl.pallas_call(
        body, name="attn_fwd", grid=(NH, nq // nsub),
        in_specs=[qspec(sb) for sb in range(nsub)]
        + [pl.BlockSpec((1, t_all, DKP), lambda h, i: (h, 0, 0)), pl.BlockSpec((1, t_all, DV), lambda h, i: (h, 0, 0))],
        out_specs=[pl.BlockSpec((nsub * TB, DV), lambda h, i: (i, h)), pl.BlockSpec((1, nsub, 8, TB), lambda h, i: (h, i, 0, 0))],
        out_shape=[jax.ShapeDtypeStruct((s_len, NH * DV), F32), jax.ShapeDtypeStruct((NH, nq, 8, TB), F32)],
        compiler_params=_params(("arbitrary", "arbitrary")),
    )(*([q] * nsub), k, v)


def _out_stage(attn, u, x, target, modsel, w_pool, pool_scale, w_out, lc):
    s_len = x.shape[0]
    t_all = s_len + lc
    off = lc // TB
    nq = s_len // TB
    hb = TB // HALO
    nqb = s_len // Q_BLOCK
    jb = TB // nqb

    def body(attn_ref, ga_ref, pin_ref, pprev_ref, pnext_ref, gp_ref, x_ref, tgt_ref, gate_ref, wp_ref, ps_ref, wo_ref,
             dxn_ref, dattn_ref, dga_ref, dgp_ref, dpool_ref, dwo_ref, dgate_ref, dps_ref, dwp_ref, loss_ref):
        i = pl.program_id(0)

        @pl.when(i == 0)
        def _():
            dwo_ref[...] = jnp.zeros_like(dwo_ref)
            dgate_ref[...] = jnp.zeros_like(dgate_ref)
            dps_ref[...] = jnp.zeros_like(dps_ref)
            dwp_ref[...] = jnp.zeros_like(dwp_ref)
            loss_ref[...] = jnp.zeros_like(loss_ref)

        attn = jnp.concatenate([attn_ref[:, jj, :] for jj in range(jb)], axis=0)
        ga = ga_ref[...]
        gp = gp_ref[...]
        pin = pin_ref[...]
        prev = jnp.where(i == 0, 0.0, pprev_ref[...])
        nxt = jnp.where(i == nq - 1, 0.0, pnext_ref[...])
        win = jnp.concatenate([prev, pin, nxt], axis=0)
        tg = i * TB + lax.broadcasted_iota(jnp.int32, (TB, 1), 0)
        pooled = []
        for g, w in enumerate(POOL_WINDOWS):
            a = win[:, g * 128:(g + 1) * 128]
            p = _shift_rows(a, -1) + a
            for step in (1, 2, 4):
                if w >= 4 * step:
                    p = _shift_rows(p, -step) + _shift_rows(p, step)
            cnt = (jnp.minimum(tg + w // 2, s_len) - jnp.maximum(tg - w // 2, 0)).astype(F32)
            pooled.append(p[HALO:HALO + TB] / cnt - a[HALO:HALO + TB])
        pooled_b = [p.astype(BF16) for p in pooled]
        wp = [wp_ref[g].astype(BF16) for g in range(4)]
        z = jnp.concatenate([_nn(pooled_b[g], wp[g]) for g in range(4)], axis=1)
        ps = ps_ref[...]
        yp = z * ps
        sga = _sig(ga)
        sila = ga * sga
        sgp = _sig(gp)
        silp = gp * sgp
        br = jnp.concatenate([sila * attn, silp * yp], axis=1).astype(BF16)
        y = _nn(br, wo_ref[...])
        gate = gate_ref[0, 2:3, :]
        err = x_ref[...] + gate * y - tgt_ref[...]
        loss_ref[...] += _colsum(_rowsum(err * err)) * (0.5 / D)
        dxn = err * (1.0 / D)
        dxn_ref[...] = dxn
        dgate_ref[...] += _colsum(dxn * y)
        dy = (dxn * gate).astype(BF16)
        dwo_ref[...] += _tn(br, dy)
        dbr = _nt(dy, wo_ref[...])
        dbra = dbr[:, :512]
        dbrp = dbr[:, 512:]
        dattn = dbra * sila
        for jj in range(jb):
            dattn_ref[:, jj, :] = dattn[jj * nqb:(jj + 1) * nqb]
        dga_ref[...] = (dbra * attn * (sga * (1.0 + ga * (1.0 - sga)))).astype(BF16)
        dgp_ref[...] = (dbrp * yp * (sgp * (1.0 + gp * (1.0 - sgp)))).astype(BF16)
        dyp = dbrp * silp
        dps_ref[...] += _colsum(dyp * z)
        dz = (dyp * ps).astype(BF16)
        dpool = []
        for g in range(4):
            dzg = dz[:, g * 128:(g + 1) * 128]
            dwp_ref[g] += _tn(pooled_b[g], dzg)
            dpool.append(_nt(dzg, wp[g]))
        dpool_ref[...] = jnp.concatenate(dpool, axis=1)

    lat = lambda w: pl.BlockSpec((TB, w), lambda i: (i, 0))
    perm = pl.BlockSpec((nqb, jb, 512), lambda i: (0, i, 0))
    ucol = lambda j: pl.BlockSpec((TB, 512), lambda i: (i + off, j))
    last8 = t_all // HALO - 1
    return pl.pallas_call(
        body, name="out_stage", grid=(nq,),
        in_specs=[perm, ucol(1), ucol(2),
                  pl.BlockSpec((HALO, 512), lambda i: ((i + off) * hb - 1, 2)),
                  pl.BlockSpec((HALO, 512), lambda i: (jnp.minimum((i + off + 1) * hb, last8), 2)),
                  ucol(3), lat(D), lat(D), pl.BlockSpec((1, 3, D), lambda i: (1, 0, 0)), _full((4, 128, 128)), _full((1, 512)),
                  _full((D, D))],
        out_specs=[lat(D), perm, lat(512), lat(512), lat(512),
                   _full((D, D)), _full((1, D)), _full((1, 512)), _full((4, 128, 128)), _full((1, 1))],
        out_shape=[jax.ShapeDtypeStruct((s_len, D), F32), jax.ShapeDtypeStruct((nqb, Q_BLOCK, 512), F32),
                   jax.ShapeDtypeStruct((s_len, 512), BF16), jax.ShapeDtypeStruct((s_len, 512), BF16),
                   jax.ShapeDtypeStruct((s_len, 512), F32),
                   jax.ShapeDtypeStruct((D, D), F32), jax.ShapeDtypeStruct((1, D), F32), jax.ShapeDtypeStruct((1, 512), F32),
                   jax.ShapeDtypeStruct((4, 128, 128), F32), jax.ShapeDtypeStruct((1, 1), F32)],
        compiler_params=_params(("arbitrary",)),
    )(attn, u, u, u, u, u, x, target, modsel, w_pool, pool_scale, w_out)


def _attn_bwd(q, k, v, dattn, attn, lse, s_len):
    t_all = q.shape[1]
    off = (t_all - s_len) // TB
    nq = s_len // TB
    nch = 4
    chunks = [(c * (t_all // nch), t_all // nch) for c in range(nch)]
    nsub = next(n for n in (BWD_QBLOCKS, 2, 1) if nq % n == 0)
    tq = nsub * TB

    def body(*refs):
        q_refs = refs[:nsub]
        k_ref, v_ref, do_ref, o_ref, lse_ref, dq_ref, dk_ref, dv_ref = refs[nsub:]
        i = pl.program_id(1)

        @pl.when(i == 0)
        def _():
            dk_ref[...] = jnp.zeros_like(dk_ref)
            dv_ref[...] = jnp.zeros_like(dv_ref)

        qb = jnp.concatenate([r[0] for r in q_refs], axis=0)
        delta_r = _row_layout(_rowsum(do_ref[...] * o_ref[...]))[0:1, :]
        do = do_ref[...].astype(BF16)
        lse_r = jnp.concatenate([lse_ref[0, sb][0:1, :] for sb in range(nsub)], axis=1)
        dq = jnp.zeros((tq, DKP), F32)
        for start, size in chunks:
            rows = pl.ds(start, size)
            kc = k_ref[0, rows, :]
            p_t = jnp.exp2(_nt(kc, qb) - lse_r)
            ds_t = (p_t * (_nt(v_ref[0, rows, :], do) - delta_r)).astype(BF16)
            dv_ref[0, rows, :] += _nn(p_t.astype(BF16), do)
            dk_ref[0, rows, :] += _nn(ds_t, qb)
            dq += _tn(ds_t, kc)
        dq_ref[0] = dq * SCALE

    kvspec = lambda w: pl.BlockSpec((1, t_all, w), lambda h, i: (h, 0, 0))
    rowspec = pl.BlockSpec((1, nsub, 8, TB), lambda h, i: (h, i, 0, 0))
    qspec = lambda sb: pl.BlockSpec((1, TB, DKP), lambda h, i: (h, i * nsub + sb + off, 0))
    return pl.pallas_call(
        body, name="attn_bwd", grid=(NH, nq // nsub),
        in_specs=[qspec(sb) for sb in range(nsub)]
        + [kvspec(DKP), kvspec(DV), pl.BlockSpec((tq, DV), lambda h, i: (i, h)), pl.BlockSpec((tq, DV), lambda h, i: (i, h)),
           rowspec],
        out_specs=[pl.BlockSpec((1, tq, DKP), lambda h, i: (h, i, 0)), kvspec(DKP), kvspec(DV)],
        out_shape=[jax.ShapeDtypeStruct((NH, s_len, DKP), F32), jax.ShapeDtypeStruct((NH, t_all, DKP), F32),
                   jax.ShapeDtypeStruct((NH, t_all, DV), F32)],
        compiler_params=_params(("arbitrary", "arbitrary")),
    )(*([q] * nsub), k, v, dattn, attn, lse)


def _qkv_bwd(u, dq, dk, dv, cos, sin, q_lora_g, w_uq_t, kv_lora_g, w_ukv, qn_g, kn_g, s_len):
    t_all = u.shape[0]
    off = (t_all - s_len) // TB
    nb = t_all // TB

    def body(ulo_ref, dq_ref, dk_ref, dv_ref, cos_ref, sin_ref, qlg_ref, wuq_ref, kvlg_ref, wukv_ref, qng_ref, kng_ref,
             dlo_ref, dwuq_ref, dwukv_ref, dqlg_ref, dkvlg_ref, dqng_ref, dkng_ref, pad_ref):
        i = pl.program_id(0)
        qng, kng = _head_gains(qng_ref, kng_ref, pad_ref)

        @pl.when(i == 0)
        def _():
            for r in (dwuq_ref, dwukv_ref, dqlg_ref, dkvlg_ref, dqng_ref, dkng_ref):
                r[...] = jnp.zeros_like(r)

        latent = i >= off
        ulo = ulo_ref[...]
        cos, sin = _rope_block(cos_ref, sin_ref, pl.program_id(0) < off)
        cq = ulo[:, 0:QL]
        rc = lax.rsqrt(jnp.mean(cq * cq, axis=-1, keepdims=True) + EPS)
        cqh = cq * rc
        qlg = qlg_ref[...]
        cqn_b = (cqh * qlg).astype(BF16)
        ckv = ulo[:, QL:QL + KVL]
        r0 = lax.rsqrt(jnp.mean(ckv * ckv, axis=-1, keepdims=True) + EPS)
        ckvh = ckv * r0
        kvlg = kvlg_ref[...]
        ckvn_b = (ckvh * kvlg).astype(BF16)
        qhs = [_nt(cqn_b, wuq_ref[hd]) for hd in range(NH)]
        kns = [_nn(ckvn_b, wukv_ref[hd])[:, :128] for hd in range(NH)]
        dqng = jnp.zeros((1, DKP), F32)
        dqraws = []
        for hd in range(NH):
            qh = qhs[hd]
            rq = lax.rsqrt(_rowsum(qh * qh) / DK + EPS)
            xh = qh * rq
            dqh = jnp.where(latent, dq_ref[hd], 0.0)
            dyq = jnp.concatenate([dqh[:, :128], _rope_t(dqh[:, 128:], cos, sin)], axis=1)
            dqng += _colsum(dyq * xh)
            dxh = dyq * qng
            dqraws.append((rq * (dxh - xh * (_rowsum(dxh * xh) / DK))).astype(BF16))
        dqng_ref[...] += dqng

        kr = ulo[:, 384:512]
        skr = _rowsum(kr * kr)
        dkr =jnp.zeros((TB, 128), F32)
        dkng = jnp.zeros((1, DKP), F32)
        dkvs = []
        for hd in range(NH):
            kn = kns[hd]
            rk = lax.rsqrt((_rowsum(kn * kn) + skr) / DK + EPS)
            xh1 = kn * rk
            xh2 = kr * rk
            dkh = dk_ref[hd] * LN2
            d1 = dkh[:, :128]
            d2 = _rope_t(dkh[:, 128:], cos, sin)
            dkng += jnp.concatenate([_colsum(d1 * xh1), _colsum(d2 * xh2)], axis=1)
            dx1 = d1 * kng[:, :128]
            dx2 = d2 * kng[:, 128:]
            dot = (_rowsum(dx1 * xh1) + _rowsum(dx2 * xh2)) / DK
            dkvs.append(jnp.concatenate([rk * (dx1 - xh1 * dot), dv_ref[hd]], axis=1).astype(BF16))
            dkr += rk * (dx2 - xh2 * dot)
        dkng_ref[...] += dkng

        dcqn = jnp.zeros((TB, QL), F32)
        dckvn = jnp.zeros((TB, KVL), F32)
        for hd in range(NH):
            dwuq_ref[hd] += _tn(dqraws[hd], cqn_b)[:DK]
            dcqn += _nn(dqraws[hd], wuq_ref[hd])
            dwukv_ref[hd] += _tn(ckvn_b, dkvs[hd])
            dckvn += _nt(dkvs[hd], wukv_ref[hd])
        dqlg_ref[...] += _colsum(dcqn * cqh)
        dxh = dcqn * qlg
        dcq = rc * (dxh - cqh * jnp.mean(dxh * cqh, axis=-1, keepdims=True))
        dkvlg_ref[...] += _colsum(dckvn * ckvh)
        dxh = dckvn * kvlg
        dckv = r0 * (dxh - ckvh * jnp.mean(dxh * ckvh, axis=-1, keepdims=True))
        dlo_ref[...] = jnp.concatenate([dcq, dckv, dkr], axis=1).astype(BF16)

    row = lambda w: pl.BlockSpec((TB, w), lambda i: (i, 0))
    heads = lambda w: pl.BlockSpec((NH, TB, w), lambda i: (0, i, 0))
    return pl.pallas_call(
        body, name="qkv_bwd", grid=(nb,),
        in_specs=[row(512), pl.BlockSpec((NH, TB, DKP), lambda i: (0, jnp.maximum(i - off, 0), 0)), heads(DKP), heads(DV),
                  pl.BlockSpec((1, 8, 256), lambda i: (jnp.maximum(i - off, 0), 0, 0)), _full((TB, 256)), _full((1, QL)), _full((NH, DKP, QL)), _full((1, KVL)), _full((NH, KVL, 256)),
                  _full((1, DK)), _full((1, DK))],
        out_specs=[row(512), _full((NH, DK, QL)), _full((NH, KVL, 256)), _full((1, QL)), _full((1, KVL)),
                   _full((1, DKP)), _full((1, DKP))],
        out_shape=[jax.ShapeDtypeStruct((t_all, 512), BF16), jax.ShapeDtypeStruct((NH, DK, QL), F32),
                   jax.ShapeDtypeStruct((NH, KVL, 256), F32), jax.ShapeDtypeStruct((1, QL), F32),
                   jax.ShapeDtypeStruct((1, KVL), F32), jax.ShapeDtypeStruct((1, DKP), F32), jax.ShapeDtypeStruct((1, DKP), F32)],
        scratch_shapes=[pltpu.VMEM((2, DKP), F32)],
        compiler_params=_params(("arbitrary",)),
    )(u, dq, dk, dv, cos, sin, q_lora_g, w_uq_t, kv_lora_g, w_ukv, qn_g, kn_g)


def _in_bwd(ctx, x, modsel, norm_g, dlo, dga, dgp, dpool, dxn, w_in_t):
    s_len, lc = x.shape[0], ctx.shape[0]
    t_all = s_len + lc
    off = lc // TB
    nb = t_all // TB
    nq = s_len // TB
    hb = TB // HALO
    n = TB + 2 * HALO

    def body(ctx_ref, x_ref, mod_ref, ng_ref, dlo_ref, dga_ref, dgp_ref, dp_ref, dpprev_ref, dpnext_ref, dxn_ref, win_ref,
             gx_ref, dwin_ref, dmod_ref, dng_ref):
        i = pl.program_id(0)
        j = i - off

        @pl.when(i == 0)
        def _():
            dwin_ref[...] = jnp.zeros_like(dwin_ref)
            dmod_ref[...] = jnp.zeros_like(dmod_ref)
            dng_ref[...] = jnp.zeros_like(dng_ref)

        latent = i >= off
        dp = dp_ref[...]
        prev = jnp.where(j <= 0, 0.0, dpprev_ref[...])
        nxt = jnp.where(j >= nq - 1, 0.0, dpnext_ref[...])
        win = jnp.concatenate([prev, dp, nxt], axis=0)
        tg = j * TB - HALO + lax.broadcasted_iota(jnp.int32, (n, 1), 0)
        dpin = []
        for g, w in enumerate(POOL_WINDOWS):
            cnt = jnp.maximum(jnp.minimum(tg + w // 2, s_len) - jnp.maximum(tg - w // 2, 0), 1).astype(F32)
            zq = win[:, g * 128:(g + 1) * 128] / cnt
            zq = zq + _shift_rows(zq, 1)
            for step in (1, 2, 4):
                if w >= 4 * step:
                    zq = _shift_rows(zq, -step) + _shift_rows(zq, step)
            dpin.append(zq[HALO:HALO + TB] - dp[:, g * 128:(g + 1) * 128])
        zero = jnp.zeros((TB, 512), BF16)
        du = [dlo_ref[...], jnp.where(latent, dga_ref[...], zero),
              jnp.where(latent, jnp.concatenate(dpin, axis=1).astype(BF16), zero), jnp.where(latent, dgp_ref[...], zero)]

        ng = ng_ref[...]
        xb = jnp.where(i < off, ctx_ref[...], x_ref[...])
        r, xh, xg, h, scale = _modulated(xb, mod_ref, ng)
        hb_ = h.astype(BF16)
        dh = jnp.zeros((TB, D), F32)
        for s, (lo, hi) in enumerate(SEG):
            dwin_ref[lo:hi, :] += _tn(du[s], hb_)
            dh += _nn(du[s], win_ref[lo:hi, :])
        is_lat = latent.astype(F32)
        dsh = _colsum(dh)
        dsc = _colsum(dh * xg)
        dmod_ref[0, 0:1, :] += dsh * (1.0 - is_lat)
        dmod_ref[0, 1:2, :] += dsc * (1.0 - is_lat)
        dmod_ref[1, 0:1, :] += dsh * is_lat
        dmod_ref[1, 1:2, :] += dsc * is_lat
        dxg = dh * (1.0 + scale)
        dng_ref[...] += _colsum(dxg * xh)
        dxh = dxg * ng
        gx_ref[...] = r * (dxh - xh * jnp.mean(dxh * xh, axis=-1, keepdims=True)) + dxn_ref[...]

    row = lambda w: pl.BlockSpec((TB, w), lambda i: (i, 0))
    lat = lambda w: pl.BlockSpec((TB, w), lambda i: (jnp.maximum(i - off, 0), 0))
    last8 = s_len // HALO - 1
    cspec, xspec, mspec = _token_specs(off)
    return pl.pallas_call(
        body, name="in_bwd", grid=(nb,),
        in_specs=[cspec, xspec, mspec, _full((1, D)), row(512), lat(512), lat(512), lat(512),
                  pl.BlockSpec((HALO, 512), lambda i: (jnp.maximum(jnp.maximum(i - off, 0) * hb - 1, 0), 0)),
                  pl.BlockSpec((HALO, 512), lambda i: (jnp.minimum((jnp.maximum(i - off, 0) + 1) * hb, last8), 0)),
                  lat(D), _full((DIN, D))],
        out_specs=[lat(D), _full((DIN, D)), _full((2, 2, D)), _full((1, D))],
        out_shape=[jax.ShapeDtypeStruct((s_len, D), F32), jax.ShapeDtypeStruct((DIN, D), F32),
                   jax.ShapeDtypeStruct((2, 2, D), F32), jax.ShapeDtypeStruct((1, D), F32)],
        compiler_params=_params(("arbitrary",)),
    )(ctx, x, modsel, norm_g, dlo, dga, dgp, dpool, dpool, dpool, dxn, w_in_t)


def _adamw_update(w_ref, g_ref, m_ref, v_ref, d_ref, mo_ref, vo_ref):
    gv = g_ref[...]
    mn = ADAM_B1 * m_ref[...] + (1.0 - ADAM_B1) * gv
    vn = ADAM_B2 * v_ref[...] + (1.0 - ADAM_B2) * (gv * gv)
    m_hat = mn / (1.0 - ADAM_B1 ** ADAM_STEP)
    v_hat = vn / (1.0 - ADAM_B2 ** ADAM_STEP)
    d_ref[...] = -ADAM_LR * (m_hat / (jnp.sqrt(v_hat) + ADAM_EPS) + ADAM_WD * w_ref[...])
    mo_ref[...] = mn
    vo_ref[...] = vn


def _adamw_many(ws, gs, ms, vs):
    n = len(ws)
    parts = 4

    def body(*refs):
        for i in range(n):
            _adamw_update(refs[i], refs[n + i], refs[2 * n + i], refs[3 * n + i], refs[4 * n + i], refs[5 * n + i], refs[6 * n + i])
            refs[7 * n + i][...] = refs[n + i][...]

    def spec(w):
        rows, cols = w.shape
        if rows % (8 * parts) == 0:
            return pl.BlockSpec((rows // parts, cols), lambda i: (i, 0))
        if cols % (128 * parts) == 0:
            return pl.BlockSpec((rows, cols // parts), lambda i: (0, i))
        return _full((rows, cols))

    specs = [spec(w) for w in ws]
    shp = [jax.ShapeDtypeStruct(w.shape, F32) for w in ws]
    out = pl.pallas_call(body, name="adamw_many", grid=(parts,), in_specs=specs * 4, out_specs=specs * 4, out_shape=shp * 4,
                         compiler_params=_params(("arbitrary",)))(*ws, *gs, *ms, *vs)
    return out[:n], out[n:2 * n], out[2 * n:3 * n], out[3 * n:]


class _Links:
    def __init__(self, send_sems, recv_sems):
        self.send_sems, self.recv_sems, self.sends = send_sems, recv_sems, []

    def send(self, src, dst, sem, to):
        cp = pltpu.make_async_remote_copy(src, dst, self.send_sems.at[sem], self.recv_sems.at[sem], device_id=to,
                                          device_id_type=MESH)
        cp.start()
        self.sends.append(cp)

    def arrived(self, dst, sem, frm):
        pltpu.make_async_remote_copy(dst, dst, self.send_sems.at[sem], self.recv_sems.at[sem], device_id=frm,
                                     device_id_type=MESH).wait_recv()

    def drain(self):
        for cp in self.sends:
            cp.wait_send()


def _half(ref, c, axis):
    size = ref.shape[axis - 2] // 2
    win = pl.ds(pl.multiple_of(c * size, 16 if axis == 0 else 128), size)
    idx = (win, slice(None)) if axis == 0 else (slice(None), win)
    return ref.at[(slice(None),) * (len(ref.shape) - 2) + idx]


def _select_rows(slots_ref, n_slots, row=0):
    sub = lax.broadcasted_iota(jnp.int32, (8, 1), 0)
    out = None
    for d in range(n_slots):
        r = jnp.where(sub == d, jnp.broadcast_to(slots_ref[d][row:row + 1, :], (8, slots_ref.shape[-1])), 0.0)
        out = r if out is None else out + r
    return out


def _gather(c, c_ctx, w_mod, b_mod, shards, axes, slab_rows):
    nw = len(shards)
    kw = w_mod.shape[1]

    def body(*refs):
        c_ref, cc_ref, wm_hbm, b_ref = refs[:4]
        w_hbm = refs[4:4 + nw]
        a16_ref, modsel_ref = refs[4 + nw:6 + nw]
        out_refs = refs[6 + nw:6 + 2 * nw]
        g_refs = refs[6 + 2 * nw:6 + 3 * nw]
        f_refs = refs[6 + 3 * nw:6 + 4 * nw]
        wm_ref, a_ref, mod_ref, send_sems, recv_sems, local_sems = refs[6 + 4 * nw:]
        loads = [pltpu.make_async_copy(w_hbm[wi], f_refs[wi], local_sems.at[wi]) for wi in range(nw)]
        loads.append(pltpu.make_async_copy(wm_hbm, wm_ref, local_sems.at[nw]))
        for cp in loads:
            cp.start()
        stores = []

        def slab(wi, chip, of=g_refs):
            return of[wi].at[chip].at[0:shards[wi].shape[0]]

        def store(src, dst, wi, slot):
            cp = pltpu.make_async_copy(src, dst, local_sems.at[nw + 1 + wi * 8 + slot])
            cp.start()
            stores.append(cp)

        def store_half(wi, chip, half, slot):
            store(_half(slab(wi, chip), half, axes[wi]), _half(slab(wi, chip, out_refs), half, axes[wi]), wi, slot)
        x, y, cc = lax.axis_index("x"), lax.axis_index("y"), lax.axis_index("c")
        me = 4 * x + 2 * y + cc
        k = 2 * x + y
        sibling = (x, y, 1 - cc)
        links = _Links(send_sems, recv_sems)
        chips = [_peer(x, y, cc, off + (0,)) for off in CHIPS3]
        chip_a = ((x + 1 - cc) % 2, (y + cc) % 2, cc)
        chip_b = ((x + cc) % 2, (y + 1 - cc) % 2, cc)
        chip_d = (1 - x, 1 - y, cc)
        cv = c_ref[...]
        sc = cv * _sig(cv)
        mine = a_ref.at[me]
        for r in range(8):
            mine[r:r + 1, :] = sc[:, r * 128:(r + 1) * 128]
        for j, off in enumerate(PEERS7):
            links.send(a_ref.at[me], a_ref.at[me], j, _peer(x, y, cc, off))
        for wi in range(nw):
            loads[wi].wait()
            slab(wi, k)[...] = f_refs[wi][...].astype(BF16)
            for j, to in enumerate((chip_a, chip_b)):
                links.send(_half(slab(wi, k), cc, axes[wi]), _half(slab(wi, k), cc, axes[wi]), 10 + wi * 6 + j, to)
            store(slab(wi, k), slab(wi, k, out_refs), wi, 0)
            rows = shards[wi].shape[0]
            pad = slab_rows[wi] - rows
            if pad:
                for kk in range(4):
                    g_refs[wi][kk, rows:, :] = jnp.zeros((pad, shards[wi].shape[1]), BF16)
                store(g_refs[wi].at[:, pl.ds(rows, pad), :], out_refs[wi].at[:, pl.ds(rows, pad), :], wi, 7)
        for j, off in enumerate(PEERS7):
            px, py, pc = _peer(x, y, cc, off)
            links.arrived(a_ref.at[4 * px + 2 * py + pc], j, (px, py, pc))
        ccv = cc_ref[...]
        sub = lax.broadcasted_iota(jnp.int32, (8, 1), 0)
        top = jnp.zeros((8, D), F32)
        for d in range(8):
            blk = a_ref[d]
            row = jnp.concatenate([blk[r:r + 1, :] for r in range(8)], axis=1)
            top = top + jnp.where(sub == d, jnp.broadcast_to(row, (8, D)), 0.0)
        a16 = jnp.concatenate([top, jnp.where(sub == 0, jnp.broadcast_to(ccv * _sig(ccv), (8, D)), 0.0)], axis=0)
        a16_ref[...] = a16
        loads[nw].wait()
        b_k = jnp.zeros((1, kw), F32)
        for kk in range(4):
            b_k = b_k + jnp.where(k == kk, b_ref[:, kk * kw:(kk + 1) * kw], 0.0)
        mod_ref[k] = _dot3(_nn, a16, wm_ref[...]) + b_k
        for j, to in enumerate(chips):
            links.send(mod_ref.at[k], mod_ref.at[k], 7 + j, to)

        def over_ici(j, frm, origin):
            for wi in range(nw):
                blk = _half(slab(wi, 2 * origin[0] + origin[1]), cc, axes[wi])
                links.arrived(blk, 10 + wi * 6 + j, frm)
                if j == 0:
                    links.send(blk, blk, 10 + wi * 6 + 2, chip_b)
                links.send(blk, blk, 10 + wi * 6 + 3 + j, sibling)
                store_half(wi, 2 * origin[0] + origin[1], cc, 1 + j)

        def from_sibling(j, origin):
            for wi in range(nw):
                links.arrived(_half(slab(wi, 2 * origin[0] + origin[1]), 1 - cc, axes[wi]), 10 + wi * 6 + 3 + j, sibling)
                store_half(wi, 2 * origin[0] + origin[1], 1 - cc, 4 + j)
        over_ici(0, chip_a, chip_a)
        over_ici(1, chip_b, chip_b)
        from_sibling(0, chip_b)
        over_ici(2, chip_b, chip_d)
        from_sibling(1, chip_a)
        from_sibling(2, chip_d)
        for j, (px, py, pc) in enumerate(chips):
            links.arrived(mod_ref.at[2 * px + py], 7 + j, (px, py, pc))
        for sel, row in ((0, 8), (1, me)):
            for kk in range(4):
                piece = mod_ref[kk, pl.ds(row, 1), :]
                lo = kk * kw
                while lo < (kk + 1) * kw:
                    r, col = divmod(lo, D)
                    n = min((kk + 1) * kw - lo, D - col)
                    modsel_ref[sel, r:r + 1, col:col + n] = piece[:, lo - kk * kw:lo - kk * kw + n]
                    lo += n
        links.drain()
        for cp in stores:
            cp.wait()

    nsem = 10 + 6 * nw
    gathered = [(4, r, s.shape[1]) for r, s in zip(slab_rows, shards)]
    return pl.pallas_call(
        body, name="gather", in_specs=[VM, VM, ANY, VM] + [ANY] * nw, out_specs=[VM, VM] + [ANY] * nw,
        out_shape=[jax.ShapeDtypeStruct((16, D), F32), jax.ShapeDtypeStruct((2, 3, D), F32)]
        + [jax.ShapeDtypeStruct(g, BF16) for g in gathered],
        scratch_shapes=[pltpu.VMEM(g, BF16) for g in gathered] + [pltpu.VMEM(s.shape, F32) for s in shards]
        + [pltpu.VMEM(w_mod.shape, F32), pltpu.VMEM((8, 8, D // 8), F32), pltpu.VMEM((4, 16, kw), F32),
           pltpu.SemaphoreType.DMA((nsem,)),
           pltpu.SemaphoreType.DMA((nsem,)), pltpu.SemaphoreType.DMA((nw + 1 + 8 * nw,))],
        compiler_params=pltpu.CompilerParams(vmem_limit_bytes=VMEM_LIMIT),
    )(c, c_ctx, w_mod, b_mod, *shards)


SMALL_ROW_WIDTHS = (D, QL, KVL, DKP, DKP, 512, 128)
SMALL_OUT_WIDTHS = (D, QL, KVL, DK, DK, 512, 1)
SMALL_PACK = 384


def _small_pieces():
    pieces = []
    for i, w in enumerate(SMALL_ROW_WIDTHS):
        for c0 in range(0, w, 128):
            j = len(pieces)
            pieces.append((i, c0, j // (SMALL_PACK // 128), j % (SMALL_PACK // 128) * 128))
    assert len(pieces) <= 8 * (SMALL_PACK // 128)
    return pieces


def _reduce(grads, axes, smalls, w_pool_g, dmod, dgate, a16, w_mod, c_ctx):
    nw = len(grads)
    ns = len(smalls)
    kw = w_mod.shape[1]
    halves = []
    for g, ax in zip(grads, axes):
        halves.append((g.shape[1] // 2, g.shape[2]) if ax == 0 else (g.shape[1], g.shape[2] // 2))

    def body(*refs):
        g_refs = refs[:nw]
        small_refs = refs[nw:nw + ns]
        wp_ref, dmod_ref, dgate_ref, a16_ref, wm_hbm, cc_ref = refs[nw + ns:nw + ns + 6]
        o = nw + ns + 6
        r_outs = refs[o:o + nw]
        small_outs = refs[o + nw:o + nw + ns]
        rwp_ref, gw_out, gb_ref, gc_ref = refs[o + nw + ns:o + nw + ns + 4]
        o = o + nw + ns + 4
        own, sib, part, got, rel, r_refs = (refs[o + i * nw:o + (i + 1) * nw] for i in range(6))
        smbuf, wps, wpg, dm_all, pc_all, wm_ref, gw_ref, send_sems, recv_sems, local_sems = refs[o + 6 * nw:]
        wm_load = pltpu.make_async_copy(wm_hbm, wm_ref, local_sems.at[nw])
        wm_load.start()
        stores = []

        def store(src, dst, sem):
            cp = pltpu.make_async_copy(src, dst, local_sems.at[nw + 1 + sem])
            cp.start()
            stores.append(cp)
        x, y, cc = lax.axis_index("x"), lax.axis_index("y"), lax.axis_index("c")
        me = 4 * x + 2 * y + cc
        k = 2 * x + y
        sibling = (x, y, 1 - cc)
        links = _Links(send_sems, recv_sems)
        chips = [_peer(x, y, cc, off + (0,)) for off in CHIPS3]
        peers = [_peer(x, y, cc, off) for off in PEERS7]
        chip_a = ((x + 1 - cc) % 2, (y + cc) % 2, cc)
        chip_b = ((x + cc) % 2, (y + 1 - cc) % 2, cc)
        ka, kb, kd = 2 * chip_a[0] + chip_a[1], 2 * chip_b[0] + chip_b[1], 2 * (1 - x) + (1 - y)
        big, sm0, wp0, dm0, pc0 = 0, 5 * nw, 5 * nw + 7, 5 * nw + 14, 5 * nw + 21

        locals_ = []
        for wi in range(nw):
            lc = pltpu.make_async_copy(_half(g_refs[wi], cc, axes[wi]), own[wi], local_sems.at[wi])
            lc.start()
            locals_.append(lc)
            links.send(_half(g_refs[wi], 1 - cc, axes[wi]), sib[wi], big + wi * 5, sibling)
        slot = smbuf.at[me]
        slot[...] = jnp.zeros((8, SMALL_PACK), F32)
        small_rows = [jnp.broadcast_to(ref[...], (1, w)) for ref, w in zip(small_refs, SMALL_ROW_WIDTHS)]
        for i, c0, row, lane in _small_pieces():
            slot[row:row + 1, lane:lane + 128] = small_rows[i][:, c0:c0 + 128]
        links.send(wp_ref, wps, wp0, sibling)
        dm_mine = dm_all.at[me]
        dm_mine[...] = jnp.zeros((8, kw), F32)
        dm_rows = ((0, (dmod_ref[1, 0:1, :], dmod_ref[1, 1:2, :], dgate_ref[...])), (4, (dmod_ref[0, 0:1, :], dmod_ref[0, 1:2, :])))
        for base, rows in dm_rows:
            for r, v in enumerate(rows):
                lo = r * D
                while lo < (r + 1) * D:
                    kk, col = divmod(lo, kw)
                    n = min((r + 1) * D - lo, kw - col)
                    dm_mine[base + kk:base + kk + 1, col:col + n] = v[:, lo - r * D:lo - r * D + n]
                    lo += n
        for j, peer in enumerate(peers):
            links.send(dm_all.at[me], dm_all.at[me], dm0 + j, peer)
            links.send(smbuf.at[me], smbuf.at[me], sm0 + j, peer)
        links.arrived(wps, wp0, sibling)
        wpg[k] = (wp_ref[...] + wps[...]).astype(BF16)
        for j, to in enumerate(chips):
            links.send(wpg.at[k], wpg.at[k], wp0 + 1 + j, to)
        for wi in range(nw):
            locals_[wi].wait()
            links.arrived(sib[wi], big + wi * 5, sibling)
            part[wi][...] = (own[wi][...] + sib[wi][...]).astype(BF16)
            got[wi][k] = part[wi][k]
            got[wi][kd] = jnp.zeros(halves[wi], BF16)
            links.send(part[wi].at[kd], rel[wi], big + wi * 5 + 1, chip_b)
            links.send(part[wi].at[kb], got[wi].at[k], big + wi * 5 + 2, chip_b)
        for wi in range(nw):
            links.arrived(rel[wi], big + wi * 5 + 1, chip_b)
            rel[wi][...] = (part[wi][ka].astype(F32) + rel[wi][...].astype(F32)).astype(BF16)
            links.send(rel[wi], got[wi].at[k], big + wi * 5 + 3, chip_a)
        for j, (px, py, pc) in enumerate(peers):
            links.arrived(dm_all.at[4 * px + 2 * py + pc], dm0 + j, (px, py, pc))
        dm_tot = dm_all[0]
        for d in range(1, 8):
            dm_tot = dm_tot + dm_all[d]
        for kk in range(4):
            gb_ref[:, kk * kw:(kk + 1) * kw] = dm_tot[kk:kk + 1, :] + dm_tot[4 + kk:5 + kk, :]
        top = jnp.zeros((8, kw), F32)
        dmc_k = jnp.zeros((1, kw), F32)
        for kk in range(4):
            top = top + jnp.where(k == kk, _select_rows(dm_all, 8, kk), 0.0)
            dmc_k = dmc_k + jnp.where(k == kk, dm_tot[4 + kk:5 + kk, :], 0.0)
        sub = lax.broadcasted_iota(jnp.int32, (8, 1), 0)
        bk = jnp.concatenate([top, jnp.where(sub == 0, jnp.broadcast_to(dmc_k, (8, kw)), 0.0)], axis=0)
        gw_ref[...] = _dot3(_tn, a16_ref[...], bk)
        store(gw_ref, gw_out, 0)
        wm_load.wait()
        pc_all[k] =_dot3(_nt, jnp.broadcast_to(bk[8:9, :], (8, kw)), wm_ref[...])
        for j, to in enumerate(chips):
            links.send(pc_all.at[k], pc_all.at[k], pc0 + j, to)
        for j, (px, py, pc) in enumerate(peers):
            links.arrived(smbuf.at[4 * px + 2 * py + pc], sm0 + j, (px, py, pc))
        tot = smbuf[0]
        for d in range(1, 8):
            tot = tot + smbuf[d]
        for i, c0, row, lane in _small_pieces():
            n = min(128, SMALL_OUT_WIDTHS[i] - c0)
            if n > 0:
                small_outs[i][:, c0:c0 + n] = tot[row:row + 1, lane:lane + n]
        for j, (px, py, pc) in enumerate(chips):
            links.arrived(wpg.at[2 * px + py], wp0 + 1 + j, (px, py, pc))
        wpt = wpg[0].astype(F32)
        for kk in range(1, 4):
            wpt = wpt + wpg[kk].astype(F32)
        rwp_ref[...] = wpt
        for wi in range(nw):
            links.arrived(got[wi].at[kb], big + wi * 5 + 2, chip_b)
            links.arrived(got[wi].at[ka], big + wi * 5 + 3, chip_a)
            total = got[wi][0].astype(F32)
            for kk in range(1, 4):
                total = total + got[wi][kk].astype(F32)
            mine = _half(r_refs[wi], cc, axes[wi])
            mine[...] = total
            links.send(mine, mine, big + wi * 5 + 4, sibling)
            store(mine, _half(r_outs[wi], cc, axes[wi]), 1 + wi)
        for j, (px, py, pc) in enumerate(chips):
            links.arrived(pc_all.at[2 * px + py], pc0 + j, (px, py, pc))
        ccv = cc_ref[...]
        sg = _sig(ccv)
        gc_ref[...] = (pc_all[0][0:1, :] + pc_all[1][0:1, :] + pc_all[2][0:1, :] + pc_all[3][0:1, :]) * (sg * (1.0 + ccv * (1.0 - sg)))
        for wi in range(nw):
            links.arrived(_half(r_refs[wi], 1 - cc, axes[wi]), big + wi * 5 + 4, sibling)
            store(_half(r_refs[wi], 1 - cc, axes[wi]), _half(r_outs[wi], 1 - cc, axes[wi]), 1 + nw + wi)
        links.drain()
        for cp in stores:
            cp.wait()

    nsem = 5 * nw + 24
    quads = [(4,) + h for h in halves]
    return pl.pallas_call(
        body, name="reduce", in_specs=[ANY] * nw + [VM] * (ns + 4) + [ANY, VM],
        out_specs=[ANY] * nw + [VM] * (ns + 1) + [ANY, VM, VM],
        out_shape=[jax.ShapeDtypeStruct(g.shape[1:], F32) for g in grads]
        + [jax.ShapeDtypeStruct((1, w), F32) for w in SMALL_OUT_WIDTHS]
        + [jax.ShapeDtypeStruct(w_pool_g.shape, F32), jax.ShapeDtypeStruct((D, kw), F32), jax.ShapeDtypeStruct((1, 3 * D), F32),
           jax.ShapeDtypeStruct((1, D), F32)],
        scratch_shapes=[pltpu.VMEM(q, F32) for q in quads] + [pltpu.VMEM(q, F32) for q in quads]
        + [pltpu.VMEM(q, BF16) for q in quads] + [pltpu.VMEM(q, BF16) for q in quads] + [pltpu.VMEM(h, BF16) for h in halves]
        + [pltpu.VMEM(g.shape[1:], F32) for g in grads]
        + [pltpu.VMEM((8, 8, SMALL_PACK), F32), pltpu.VMEM(w_pool_g.shape, F32), pltpu.VMEM((4,) + w_pool_g.shape, BF16),
           pltpu.VMEM((8, 8, kw), F32), pltpu.VMEM((4, 8, D), F32), pltpu.VMEM((D, kw), F32), pltpu.VMEM((D, kw), F32)]
        + [pltpu.SemaphoreType.DMA((nsem,)), pltpu.SemaphoreType.DMA((nsem,)), pltpu.SemaphoreType.DMA((3 * nw + 2,))],
        compiler_params=pltpu.CompilerParams(vmem_limit_bytes=VMEM_LIMIT),
    )(*grads, *smalls, w_pool_g, dmod, dgate, a16, w_mod, c_ctx)


def _rope_tables(s_len):
    rows = s_len // GRID_W
    per = TB // GRID_W
    n_freq = 16
    f32 = np.float32
    inv = f32(ROPE_BASE) ** (-np.arange(n_freq, dtype=f32) / f32(n_freq))
    ang_r = np.arange(rows, dtype=f32)[:, None] * inv
    ang_c = np.arange(GRID_W, dtype=f32)[:, None] * inv
    by_row, by_col = [], []
    for fn, pad in ((np.cos, 1.0), (np.sin, 0.0)):
        r = np.concatenate([fn(ang_r), fn(ang_r), np.zeros((rows, 96), f32)], axis=1).reshape(rows // per, per, 128)
        by_row.append(np.pad(r, ((0, 0), (0, 8 - per), (0, 0))))
        cpart = np.concatenate([np.zeros((GRID_W, 32), f32), fn(ang_c), fn(ang_c), np.full((GRID_W, 64), pad, f32)], axis=1)
        by_col.append(np.tile(cpart, (per, 1)))
    return jnp.asarray(np.concatenate(by_row, axis=-1), F32), jnp.asarray(np.concatenate(by_col, axis=-1), F32)


def kernel(x, c, ctx, c_ctx, w_mod, b_mod, norm_g, w_in, q_lora_g, w_uq, kv_lora_g, w_ukv, q_norm_g, k_norm_g, w_pool, pool_scale, w_out, loss_target, m_c_ctx, m_w_mod, m_b_mod, m_norm_g, m_w_in, m_q_lora_g, m_w_uq, m_kv_lora_g, m_w_ukv, m_q_norm_g, m_k_norm_g, m_w_pool, m_pool_scale, m_w_out, v_c_ctx, v_w_mod, v_b_mod, v_norm_g, v_w_in, v_q_lora_g, v_w_uq, v_kv_lora_g, v_w_ukv, v_q_norm_g, v_k_norm_g, v_w_pool, v_pool_scale, v_w_out):
    s_len = x.shape[1]
    lc = ctx.shape[1]
    weights = dict(c_ctx=c_ctx, w_mod=w_mod, b_mod=b_mod, norm_g=norm_g, w_in=w_in, q_lora_g=q_lora_g, w_uq=w_uq,
                   kv_lora_g=kv_lora_g, w_ukv=w_ukv, q_norm_g=q_norm_g, k_norm_g=k_norm_g, w_pool=w_pool,
                   pool_scale=pool_scale, w_out=w_out)
    m_in = dict(c_ctx=m_c_ctx, w_mod=m_w_mod, b_mod=m_b_mod, norm_g=m_norm_g, w_in=m_w_in, q_lora_g=m_q_lora_g, w_uq=m_w_uq,
                kv_lora_g=m_kv_lora_g, w_ukv=m_w_ukv, q_norm_g=m_q_norm_g, k_norm_g=m_k_norm_g, w_pool=m_w_pool,
                pool_scale=m_pool_scale, w_out=m_w_out)
    v_in = dict(c_ctx=v_c_ctx, w_mod=v_w_mod, b_mod=v_b_mod, norm_g=v_norm_g, w_in=v_w_in, q_lora_g=v_q_lora_g, w_uq=v_w_uq,
                kv_lora_g=v_kv_lora_g, w_ukv=v_w_ukv, q_norm_g=v_q_norm_g, k_norm_g=v_k_norm_g, w_pool=v_w_pool,
                pool_scale=v_pool_scale, w_out=v_w_out)
    order = ["c_ctx", "w_mod", "b_mod", "norm_g", "w_in", "q_lora_g", "w_uq", "kv_lora_g", "w_ukv", "q_norm_g", "k_norm_g",
             "w_pool", "pool_scale", "w_out"]
    transposed = ("w_in", "w_uq")
    as2d = lambda n, a: jnp.transpose(a[0]) if n in transposed else a.reshape(-1, a.shape[-1])
    back = lambda n, a: jnp.transpose(a)[None] if n in transposed else a.reshape(weights[n].shape)

    c_ctx2 = c_ctx.reshape(1, D)
    split = (0, 1, 0, 0)
    a16, modsel, g_out, g_in, g_uq, g_ukv = _gather(
        c, c_ctx2, w_mod[0], b_mod, [w_out[0], as2d("w_in", w_in), as2d("w_uq", w_uq), w_ukv[0]], split,
        (D // 4, DIN // 4, DKP, KVL))
    w_in_t = g_in.reshape(DIN, D)
    w_uq_t = g_uq
    w_out_f = g_out.reshape(D, D)
    cos, sin = _rope_tables(s_len)

    u, q, kk, v = _fwd_in(ctx[0], x[0], modsel, norm_g, w_in_t, q_lora_g, w_uq_t, kv_lora_g, g_ukv, q_norm_g, k_norm_g, cos, sin)
    attn, lse = _attn_fwd(q, kk, v, s_len)
    (dxn, dattn, dga, dgp, dpool, dw_out, dgate, dps, dw_pool, loss) = _out_stage(
        attn.reshape(s_len // Q_BLOCK, Q_BLOCK, NH * DV), u, x[0], loss_target[0], modsel, w_pool[0], pool_scale,
        w_out_f, lc)
    dattn = dattn.reshape(s_len, NH * DV)
    dq, dk, dv = _attn_bwd(q, kk, v, dattn, attn, lse, s_len)
    dlo, dw_uq_t, dw_ukv, dqlg, dkvlg, dqng, dkng = _qkv_bwd(u, dq, dk, dv, cos, sin, q_lora_g, w_uq_t, kv_lora_g, g_ukv,
                                                            q_norm_g, k_norm_g, s_len)
    gx, dw_in_t, dmod, dng = _in_bwd(ctx[0], x[0], modsel, norm_g, dlo, dga, dgp, dpool, dxn, w_in_t)

    r_out, r_in, r_uq, r_ukv, g_ng, g_qlg, g_kvlg, g_qng, g_kng, g_ps, loss_all, g_wp, g_w_mod, g_b_mod, g_c_ctx = _reduce(
        [dw_out.reshape(4, D // 4, D), dw_in_t.reshape(4, DIN // 4, D), dw_uq_t, dw_ukv], split,
        [dng, dqlg, dkvlg, dqng, dkng, dps, loss], dw_pool, dmod, dgate, a16, w_mod[0], c_ctx2)
    g2d = dict(c_ctx=g_c_ctx, b_mod=g_b_mod, w_mod=g_w_mod, w_in=r_in, w_uq=r_uq, w_ukv=r_ukv, w_out=r_out, norm_g=g_ng,
               q_lora_g=g_qlg, kv_lora_g=g_kvlg, q_norm_g=g_qng, k_norm_g=g_kng, pool_scale=g_ps, w_pool=g_wp.reshape(512, 128))

    outs = _adamw_many([as2d(n, weights[n]) for n in order], [g2d[n] for n in order], [as2d(n, m_in[n]) for n in order],
                       [as2d(n, v_in[n]) for n in order])
    d2d, m2d, v2d, g2d = (dict(zip(order, arrs)) for arrs in outs)

    return (loss_all[0, 0], gx[None], *[back(n, g2d[n]) for n in order], *[back(n, d2d[n]) for n in order],
            *[back(n, m2d[n]) for n in order], *[back(n, v2d[n]) for n in order])
```

```python
import jax
import jax.numpy as jnp
import numpy as np
from jax import lax
from jax.experimental import pallas as pl
from jax.experimental.pallas import tpu as pltpu

F32 = jnp.float32
BF16 = jnp.bfloat16
MESH = pl.DeviceIdType.MESH

D = 1024
NH = 4
DK = 192
DKP = 256
DV = 128
QL = 256
KVL = 128
DIN = 1984
U_LO = 448
SEG = ((0, 512), (448, 960), (960, 1472), (1472, 1984))
DU = 2048
POOL_WINDOWS = (2, 4, 8, 16)
HALO = 8
EPS = 1e-6
ROPE_BASE = 10000.0
GRID_W = 64
Q_BLOCK = 128
TB = 256
BWD_QBLOCKS = 1
SCALE = DK ** -0.5
LOG2E = 1.4426950408889634
LN2 = 0.6931471805599453
VMEM_LIMIT = 56 * 1024 * 1024

ADAM_LR = 0.001
ADAM_B1 = 0.9
ADAM_B2 = 0.999
ADAM_EPS = 1e-08
ADAM_WD = 0.01
ADAM_STEP = 10

CHIPS3 = ((1, 0), (0, 1), (1, 1))
PEERS7 = tuple((dx, dy, dc) for dx in (0, 1) for dy in (0, 1) for dc in (0, 1) if (dx, dy, dc) != (0, 0, 0))

VM = pl.BlockSpec(memory_space=pltpu.VMEM)
ANY = pl.BlockSpec(memory_space=pl.ANY)


def _nn(a, b):
    return jnp.dot(a, b, preferred_element_type=F32)


def _nt(a, b):
    return lax.dot_general(a, b, (((1,), (1,)), ((), ())), preferred_element_type=F32)


def _tn(a, b):
    return lax.dot_general(a, b, (((0,), (0,)), ((), ())), preferred_element_type=F32)


def _split3(a):
    a0 = a.astype(BF16)
    r = a - a0.astype(F32)
    a1 = r.astype(BF16)
    a2 = (r - a1.astype(F32)).astype(BF16)
    return a0, a1, a2


def _dot3(dot, a, b):
    sa = _split3(a)
    sb = _split3(b)
    out = None
    for i in range(3):
        for j in range(3 - i):
            t = dot(sa[i], sb[j])
            out = t if out is None else out + t
    return out


def _sig(x):
    return 1.0 / (1.0 + jnp.exp(-x))


def _rot(t):
    src = lax.broadcasted_iota(jnp.int32, (128, 128), 0)
    dst = lax.broadcasted_iota(jnp.int32, (128, 128), 1)
    first = (dst % 32) < 16
    perm = jnp.where(first & (src == dst + 16), -1.0, jnp.where(~first & (src == dst - 16), 1.0, 0.0)).astype(BF16)
    hi = t.astype(BF16)
    lo = (t - hi.astype(F32)).astype(BF16)
    return _nn(hi, perm) + _nn(lo, perm)


def _rope(t, cos, sin):
    return t * cos + _rot(t) * sin


def _rope_t(t, cos, sin):
    return t * cos - _rot(t * sin)


def _rope_block(rows_ref, cols_ref, is_ctx):
    lane = lax.broadcasted_iota(jnp.int32, (TB, 256), 1) % 128
    rows = jnp.concatenate([jnp.broadcast_to(rows_ref[0, r:r + 1, :], (GRID_W, 256)) for r in range(TB // GRID_W)], axis=0)
    cs = jnp.where(lane < 32, rows, cols_ref[...])
    return jnp.where(is_ctx, 1.0, cs[:, :128]), jnp.where(is_ctx, 0.0, cs[:, 128:])


def _shift_rows(z, k):
    n = z.shape[0]
    return pltpu.roll(z, (n - k) % n, 0)


def _colsum(a):
    return jnp.sum(a, axis=0, keepdims=True)


def _rowsum(a):
    return jnp.sum(a, axis=-1, keepdims=True)


def _row_layout(col):
    return jnp.transpose(jnp.broadcast_to(col, (col.shape[0], 128)))[0:8, :]


def _params(sem=None):
    return pltpu.CompilerParams(dimension_semantics=sem, vmem_limit_bytes=VMEM_LIMIT)


def _full(shape):
    nd = len(shape)
    return pl.BlockSpec(shape, lambda *_: (0,) * nd)


def _peer(x, y, c, off):
    dx, dy, dc = off
    return ((x + dx) % 2, (y + dy) % 2, (c + dc) % 2)


def _token_specs(off):
    ctx = pl.BlockSpec((TB, D), lambda i: (jnp.minimum(i, off - 1), 0))
    lat = pl.BlockSpec((TB, D), lambda i: (jnp.maximum(i - off, 0), 0))
    mod = pl.BlockSpec((1, 3, D), lambda i: (jnp.minimum(i // off, 1), 0, 0))
    return ctx, lat, mod


def _head_gains(qng_ref, kng_ref, pad_ref):
    pad_ref[...] = jnp.zeros((2, DKP), F32)
    pad_ref[0:1, 0:DK] = qng_ref[...]
    pad_ref[1:2, 0:DK] = kng_ref[...]
    return pad_ref[0:1, :], pad_ref[1:2, :]


def _modulated(x, mod_ref, ng):
    shift = mod_ref[0, 0:1, :]
    scale = mod_ref[0, 1:2, :]
    r = lax.rsqrt(jnp.mean(x * x, axis=-1, keepdims=True) + EPS)
    xh = x * r
    xg = xh * ng
    return r, xh, xg, xg * (1.0 + scale) + shift, scale


def _fwd_in(ctx, x, modsel, norm_g, w_in_t, q_lora_g, w_uq_t, kv_lora_g, w_ukv, qn_g, kn_g, cos, sin):
    s_len, lc = x.shape[0], ctx.shape[0]
    t_all = s_len + lc
    nb = t_all // TB
    off = lc // TB

    def body(ctx_ref, x_ref, mod_ref, ng_ref, win_ref, qlg_ref, wuq_ref, kvlg_ref, wukv_ref, qng_ref, kng_ref, cos_ref, sin_ref,
             u_ref, q_ref, k_ref, v_ref, pad_ref):
        is_ctx = pl.program_id(0) < off
        qng, kng = _head_gains(qng_ref, kng_ref, pad_ref)
        xb = jnp.where(is_ctx, ctx_ref[...], x_ref[...])
        _, _, _, h, _ = _modulated(xb, mod_ref, ng_ref[...])
        hb = h.astype(BF16)
        lane = lax.broadcasted_iota(jnp.int32, (TB, 512), 1)
        ulo = jnp.where(lane < U_LO, _nt(hb, win_ref[SEG[0][0]:SEG[0][1], :]), 0.0)
        u_ref[:, 0:512] = ulo
        for j in range(1, 4):
            u_ref[:, j * 512:(j + 1) * 512] = _nt(hb, win_ref[SEG[j][0]:SEG[j][1], :])
        cos, sin = _rope_block(cos_ref, sin_ref, is_ctx)
        cq = ulo[:, 0:QL]
        cqn = (cq * lax.rsqrt(jnp.mean(cq * cq, axis=-1, keepdims=True) + EPS) * qlg_ref[...]).astype(BF16)
        ckv = ulo[:, QL:QL + KVL]
        ckvn = (ckv * lax.rsqrt(jnp.mean(ckv * ckv, axis=-1, keepdims=True) + EPS) * kvlg_ref[...]).astype(BF16)
        qhs = [_nt(cqn, wuq_ref[hd]) for hd in range(NH)]
        kvs = [_nn(ckvn, wukv_ref[hd]) for hd in range(NH)]
        for hd in range(NH):
            qh = qhs[hd]
            qn = qh * lax.rsqrt(_rowsum(qh * qh) / DK + EPS) * qng
            q_ref[hd] = (jnp.concatenate([qn[:, :128], _rope(qn[:, 128:], cos, sin)], axis=1) * (SCALE * LOG2E)).astype(BF16)
        kr = ulo[:, 384:512]
        skr = _rowsum(kr * kr)
        kr_roped =_rope(kr * kng[:, 128:], cos, sin)
        for hd in range(NH):
            kv = kvs[hd]
            kn = kv[:, :128]
            rk = lax.rsqrt((_rowsum(kn * kn) + skr) / DK + EPS)
            k_ref[hd] = jnp.concatenate([kn * rk * kng[:, :128], kr_roped * rk], axis=1).astype(BF16)
            v_ref[hd] = kv[:, 128:].astype(BF16)

    row = lambda w: pl.BlockSpec((TB, w), lambda i: (i, 0))
    heads = lambda w: pl.BlockSpec((NH, TB, w), lambda i: (0, i, 0))
    cspec, xspec, mspec = _token_specs(off)
    return pl.pallas_call(
        body, name="fwd_in", grid=(nb,),
        in_specs=[cspec, xspec, mspec, _full((1, D)), _full((DIN, D)), _full((1, QL)), _full((NH, DKP, QL)), _full((1, KVL)),
                  _full((NH, KVL, 256)), _full((1, DK)), _full((1, DK)),
                  pl.BlockSpec((1, 8, 256), lambda i: (jnp.maximum(i - off, 0), 0, 0)), _full((TB, 256))],
        out_specs=[row(DU), heads(DKP), heads(DKP), heads(DV)],
        out_shape=[jax.ShapeDtypeStruct((t_all, DU), F32), jax.ShapeDtypeStruct((NH, t_all, DKP), BF16),
                   jax.ShapeDtypeStruct((NH, t_all, DKP), BF16), jax.ShapeDtypeStruct((NH, t_all, DV), BF16)],
        scratch_shapes=[pltpu.VMEM((2, DKP), F32)],
        compiler_params=_params(("arbitrary",)),
    )(ctx, x, modsel, norm_g, w_in_t, q_lora_g, w_uq_t, kv_lora_g, w_ukv, qn_g, kn_g, cos, sin)


def _attn_fwd(q, k, v, s_len):
    t_all = q.shape[1]
    off = (t_all - s_len) // TB
    nq = s_len // TB
    nsub = next(n for n in (4, 2, 1) if nq % n == 0)

    def body(*refs):
        q_refs = refs[:nsub]
        k_ref, v_ref, o_ref, lse_ref = refs[nsub:]
        for sb in range(nsub):
            s = _nt(q_refs[sb][0], k_ref[0])
            m = jnp.max(s, axis=-1, keepdims=True)
            e = jnp.exp2(s - m)
            l = _rowsum(e)
            o_ref[sb * TB:(sb + 1) * TB, :] = _nn(e.astype(BF16), v_ref[0]) / l
            lse_ref[0, sb] = _row_layout(m + jnp.log2(l))

    qspec = lambda sb: pl.BlockSpec((1, TB, DKP), lambda h, i: (h, i * nsub + sb + off, 0))
    return pl.pallas_call(
        body, name="attn_fwd", grid=(NH, nq // nsub),
        in_specs=[qspec(sb) for sb in range(nsub)]
        + [pl.BlockSpec((1, t_all, DKP), lambda h, i: (h, 0, 0)), pl.BlockSpec((1, t_all, DV), lambda h, i: (h, 0, 0))],
        out_specs=[pl.BlockSpec((nsub * TB, DV), lambda h, i: (i, h)), pl.BlockSpec((1, nsub, 8, TB), lambda h, i: (h, i, 0, 0))],
        out_shape=[jax.ShapeDtypeStruct((s_len, NH * DV), F32), jax.ShapeDtypeStruct((NH, nq, 8, TB), F32)],
        compiler_params=_params(("arbitrary", "arbitrary")),
    )(*([q] * nsub), k, v)


def _out_stage(attn, u, x, target, modsel, w_pool, pool_scale, w_out, lc):
    s_len = x.shape[0]
    t_all = s_len + lc
    off = lc // TB
    nq = s_len // TB
    hb = TB // HALO
    nqb = s_len // Q_BLOCK
    jb = TB // nqb

    def body(attn_ref, ga_ref, pin_ref, pprev_ref, pnext_ref, gp_ref, x_ref, tgt_ref, gate_ref, wp_ref, ps_ref, wo_ref,
             dxn_ref, dattn_ref, dga_ref, dgp_ref, dpool_ref, dwo_ref, dgate_ref, dps_ref, dwp_ref, loss_ref):
        i = pl.program_id(0)

        @pl.when(i == 0)
        def _():
            dwo_ref[...] = jnp.zeros_like(dwo_ref)
            dgate_ref[...] = jnp.zeros_like(dgate_ref)
            dps_ref[...] = jnp.zeros_like(dps_ref)
            dwp_ref[...] = jnp.zeros_like(dwp_ref)
            loss_ref[...] = jnp.zeros_like(loss_ref)

        attn = jnp.concatenate([attn_ref[:, jj, :] for jj in range(jb)], axis=0)
        ga = ga_ref[...]
        gp = gp_ref[...]
        pin = pin_ref[...]
        prev = jnp.where(i == 0, 0.0, pprev_ref[...])
        nxt = jnp.where(i == nq - 1, 0.0, pnext_ref[...])
        win = jnp.concatenate([prev, pin, nxt], axis=0)
        tg = i * TB + lax.broadcasted_iota(jnp.int32, (TB, 1), 0)
        pooled = []
        for g, w in enumerate(POOL_WINDOWS):
            a = win[:, g * 128:(g + 1) * 128]
            p = _shift_rows(a, -1) + a
            for step in (1, 2, 4):
                if w >= 4 * step:
                    p = _shift_rows(p, -step) + _shift_rows(p, step)
            cnt = (jnp.minimum(tg + w // 2, s_len) - jnp.maximum(tg - w // 2, 0)).astype(F32)
            pooled.append(p[HALO:HALO + TB] / cnt - a[HALO:HALO + TB])
        pooled_b = [p.astype(BF16) for p in pooled]
        wp = [wp_ref[g].astype(BF16) for g in range(4)]
        z = jnp.concatenate([_nn(pooled_b[g], wp[g]) for g in range(4)], axis=1)
        ps = ps_ref[...]
        yp = z * ps
        sga = _sig(ga)
        sila = ga * sga
        sgp = _sig(gp)
        silp = gp * sgp
        br = jnp.concatenate([sila * attn, silp * yp], axis=1).astype(BF16)
        y = _nn(br, wo_ref[...])
        gate = gate_ref[0, 2:3, :]
        err = x_ref[...] + gate * y - tgt_ref[...]
        loss_ref[...] += _colsum(_rowsum(err * err)) * (0.5 / D)
        dxn = err * (1.0 / D)
        dxn_ref[...] = dxn
        dgate_ref[...] += _colsum(dxn * y)
        dy = (dxn * gate).astype(BF16)
        dwo_ref[...] += _tn(br, dy)
        dbr = _nt(dy, wo_ref[...])
        dbra = dbr[:, :512]
        dbrp = dbr[:, 512:]
        dattn = dbra * sila
        for jj in range(jb):
            dattn_ref[:, jj, :] = dattn[jj * nqb:(jj + 1) * nqb]
        dga_ref[...] = (dbra * attn * (sga * (1.0 + ga * (1.0 - sga)))).astype(BF16)
        dgp_ref[...] = (dbrp * yp * (sgp * (1.0 + gp * (1.0 - sgp)))).astype(BF16)
        dyp = dbrp * silp
        dps_ref[...] += _colsum(dyp * z)
        dz = (dyp * ps).astype(BF16)
        dpool = []
        for g in range(4):
            dzg = dz[:, g * 128:(g + 1) * 128]
            dwp_ref[g] += _tn(pooled_b[g], dzg)
            dpool.append(_nt(dzg, wp[g]))
        dpool_ref[...] = jnp.concatenate(dpool, axis=1)

    lat = lambda w: pl.BlockSpec((TB, w), lambda i: (i, 0))
    perm = pl.BlockSpec((nqb, jb, 512), lambda i: (0, i, 0))
    ucol = lambda j: pl.BlockSpec((TB, 512), lambda i: (i + off, j))
    last8 = t_all // HALO - 1
    return pl.pallas_call(
        body, name="out_stage", grid=(nq,),
        in_specs=[perm, ucol(1), ucol(2),
                  pl.BlockSpec((HALO, 512), lambda i: ((i + off) * hb - 1, 2)),
                  pl.BlockSpec((HALO, 512), lambda i: (jnp.minimum((i + off + 1) * hb, last8), 2)),
                  ucol(3), lat(D), lat(D), pl.BlockSpec((1, 3, D), lambda i: (1, 0, 0)), _full((4, 128, 128)), _full((1, 512)),
                  _full((D, D))],
        out_specs=[lat(D), perm, lat(512), lat(512), lat(512),
                   _full((D, D)), _full((1, D)), _full((1, 512)), _full((4, 128, 128)), _full((1, 1))],
        out_shape=[jax.ShapeDtypeStruct((s_len, D), F32), jax.ShapeDtypeStruct((nqb, Q_BLOCK, 512), F32),
                   jax.ShapeDtypeStruct((s_len, 512), BF16), jax.ShapeDtypeStruct((s_len, 512), BF16),
                   jax.ShapeDtypeStruct((s_len, 512), F32),
                   jax.ShapeDtypeStruct((D, D), F32), jax.ShapeDtypeStruct((1, D), F32), jax.ShapeDtypeStruct((1, 512), F32),
                   jax.ShapeDtypeStruct((4, 128, 128), F32), jax.ShapeDtypeStruct((1, 1), F32)],
        compiler_params=_params(("arbitrary",)),
    )(attn, u, u, u, u, u, x, target, modsel, w_pool, pool_scale, w_out)


def _attn_bwd(q, k, v, dattn, attn, lse, s_len):
    t_all = q.shape[1]
    off = (t_all - s_len) // TB
    nq = s_len // TB
    nch = 4
    chunks = [(c * (t_all // nch), t_all // nch) for c in range(nch)]
    nsub = next(n for n in (BWD_QBLOCKS, 2, 1) if nq % n == 0)
    tq = nsub * TB

    def body(*refs):
        q_refs = refs[:nsub]
        k_ref, v_ref, do_ref, o_ref, lse_ref, dq_ref, dk_ref, dv_ref = refs[nsub:]
        i = pl.program_id(1)

        @pl.when(i == 0)
        def _():
            dk_ref[...] = jnp.zeros_like(dk_ref)
            dv_ref[...] = jnp.zeros_like(dv_ref)

        qb = jnp.concatenate([r[0] for r in q_refs], axis=0)
        delta_r = _row_layout(_rowsum(do_ref[...] * o_ref[...]))[0:1, :]
        do = do_ref[...].astype(BF16)
        lse_r = jnp.concatenate([lse_ref[0, sb][0:1, :] for sb in range(nsub)], axis=1)
        dq = jnp.zeros((tq, DKP), F32)
        for start, size in chunks:
            rows = pl.ds(start, size)
            kc = k_ref[0, rows, :]
            p_t = jnp.exp2(_nt(kc, qb) - lse_r)
            ds_t = (p_t * (_nt(v_ref[0, rows, :], do) - delta_r)).astype(BF16)
            dv_ref[0, rows, :] += _nn(p_t.astype(BF16), do)
            dk_ref[0, rows, :] += _nn(ds_t, qb)
            dq += _tn(ds_t, kc)
        dq_ref[0] = dq * SCALE

    kvspec = lambda w: pl.BlockSpec((1, t_all, w), lambda h, i: (h, 0, 0))
    rowspec = pl.BlockSpec((1, nsub, 8, TB), lambda h, i: (h, i, 0, 0))
    qspec = lambda sb: pl.BlockSpec((1, TB, DKP), lambda h, i: (h, i * nsub + sb + off, 0))
    return pl.pallas_call(
        body, name="attn_bwd", grid=(NH, nq // nsub),
        in_specs=[qspec(sb) for sb in range(nsub)]
        + [kvspec(DKP), kvspec(DV), pl.BlockSpec((tq, DV), lambda h, i: (i, h)), pl.BlockSpec((tq, DV), lambda h, i: (i, h)),
           rowspec],
        out_specs=[pl.BlockSpec((1, tq, DKP), lambda h, i: (h, i, 0)), kvspec(DKP), kvspec(DV)],
        out_shape=[jax.ShapeDtypeStruct((NH, s_len, DKP), F32), jax.ShapeDtypeStruct((NH, t_all, DKP), F32),
                   jax.ShapeDtypeStruct((NH, t_all, DV), F32)],
        compiler_params=_params(("arbitrary", "arbitrary")),
    )(*([q] * nsub), k, v, dattn, attn, lse)


def _qkv_bwd(u, dq, dk, dv, cos, sin, q_lora_g, w_uq_t, kv_lora_g, w_ukv, qn_g, kn_g, s_len):
    t_all = u.shape[0]
    off = (t_all - s_len) // TB
    nb = t_all // TB

    def body(ulo_ref, dq_ref, dk_ref, dv_ref, cos_ref, sin_ref, qlg_ref, wuq_ref, kvlg_ref, wukv_ref, qng_ref, kng_ref,
             dlo_ref, dwuq_ref, dwukv_ref, dqlg_ref, dkvlg_ref, dqng_ref, dkng_ref, pad_ref):
        i = pl.program_id(0)
        qng, kng = _head_gains(qng_ref, kng_ref, pad_ref)

        @pl.when(i == 0)
        def _():
            for r in (dwuq_ref, dwukv_ref, dqlg_ref, dkvlg_ref, dqng_ref, dkng_ref):
                r[...] = jnp.zeros_like(r)

        latent = i >= off
        ulo = ulo_ref[...]
        cos, sin = _rope_block(cos_ref, sin_ref, pl.program_id(0) < off)
        cq = ulo[:, 0:QL]
        rc = lax.rsqrt(jnp.mean(cq * cq, axis=-1, keepdims=True) + EPS)
        cqh = cq * rc
        qlg = qlg_ref[...]
        cqn_b = (cqh * qlg).astype(BF16)
        ckv = ulo[:, QL:QL + KVL]
        r0 = lax.rsqrt(jnp.mean(ckv * ckv, axis=-1, keepdims=True) + EPS)
        ckvh = ckv * r0
        kvlg = kvlg_ref[...]
        ckvn_b = (ckvh * kvlg).astype(BF16)
        qhs = [_nt(cqn_b, wuq_ref[hd]) for hd in range(NH)]
        kns = [_nn(ckvn_b, wukv_ref[hd])[:, :128] for hd in range(NH)]
        dqng = jnp.zeros((1, DKP), F32)
        dqraws = []
        for hd in range(NH):
            qh = qhs[hd]
            rq = lax.rsqrt(_rowsum(qh * qh) / DK + EPS)
            xh = qh * rq
            dqh = jnp.where(latent, dq_ref[hd], 0.0)
            dyq = jnp.concatenate([dqh[:, :128], _rope_t(dqh[:, 128:], cos, sin)], axis=1)
            dqng += _colsum(dyq * xh)
            dxh = dyq * qng
            dqraws.append((rq * (dxh - xh * (_rowsum(dxh * xh) / DK))).astype(BF16))
        dqng_ref[...] += dqng

        kr = ulo[:, 384:512]
        skr = _rowsum(kr * kr)
        dkr =jnp.zeros((TB, 128), F32)
        dkng = jnp.zeros((1, DKP), F32)
        dkvs = []
        for hd in range(NH):
            kn = kns[hd]
            rk = lax.rsqrt((_rowsum(kn * kn) + skr) / DK + EPS)
            xh1 = kn * rk
            xh2 = kr * rk
            dkh = dk_ref[hd] * LN2
            d1 = dkh[:, :128]
            d2 = _rope_t(dkh[:, 128:], cos, sin)
            dkng += jnp.concatenate([_colsum(d1 * xh1), _colsum(d2 * xh2)], axis=1)
            dx1 = d1 * kng[:, :128]
            dx2 = d2 * kng[:, 128:]
            dot = (_rowsum(dx1 * xh1) + _rowsum(dx2 * xh2)) / DK
            dkvs.append(jnp.concatenate([rk * (dx1 - xh1 * dot), dv_ref[hd]], axis=1).astype(BF16))
            dkr += rk * (dx2 - xh2 * dot)
        dkng_ref[...] += dkng

        dcqn = jnp.zeros((TB, QL), F32)
        dckvn = jnp.zeros((TB, KVL), F32)
        for hd in range(NH):
            dwuq_ref[hd] += _tn(dqraws[hd], cqn_b)[:DK]
            dcqn += _nn(dqraws[hd], wuq_ref[hd])
            dwukv_ref[hd] += _tn(ckvn_b, dkvs[hd])
            dckvn += _nt(dkvs[hd], wukv_ref[hd])
        dqlg_ref[...] += _colsum(dcqn * cqh)
        dxh = dcqn * qlg
        dcq = rc * (dxh - cqh * jnp.mean(dxh * cqh, axis=-1, keepdims=True))
        dkvlg_ref[...] += _colsum(dckvn * ckvh)
        dxh = dckvn * kvlg
        dckv = r0 * (dxh - ckvh * jnp.mean(dxh * ckvh, axis=-1, keepdims=True))
        dlo_ref[...] = jnp.concatenate([dcq, dckv, dkr], axis=1).astype(BF16)

    row = lambda w: pl.BlockSpec((TB, w), lambda i: (i, 0))
    heads = lambda w: pl.BlockSpec((NH, TB, w), lambda i: (0, i, 0))
    return pl.pallas_call(
        body, name="qkv_bwd", grid=(nb,),
        in_specs=[row(512), pl.BlockSpec((NH, TB, DKP), lambda i: (0, jnp.maximum(i - off, 0), 0)), heads(DKP), heads(DV),
                  pl.BlockSpec((1, 8, 256), lambda i: (jnp.maximum(i - off, 0), 0, 0)), _full((TB, 256)), _full((1, QL)), _full((NH, DKP, QL)), _full((1, KVL)), _full((NH, KVL, 256)),
                  _full((1, DK)), _full((1, DK))],
        out_specs=[row(512), _full((NH, DK, QL)), _full((NH, KVL, 256)), _full((1, QL)), _full((1, KVL)),
                   _full((1, DKP)), _full((1, DKP))],
        out_shape=[jax.ShapeDtypeStruct((t_all, 512), BF16), jax.ShapeDtypeStruct((NH, DK, QL), F32),
                   jax.ShapeDtypeStruct((NH, KVL, 256), F32), jax.ShapeDtypeStruct((1, QL), F32),
                   jax.ShapeDtypeStruct((1, KVL), F32), jax.ShapeDtypeStruct((1, DKP), F32), jax.ShapeDtypeStruct((1, DKP), F32)],
        scratch_shapes=[pltpu.VMEM((2, DKP), F32)],
        compiler_params=_params(("arbitrary",)),
    )(u, dq, dk, dv, cos, sin, q_lora_g, w_uq_t, kv_lora_g, w_ukv, qn_g, kn_g)


def _in_bwd(ctx, x, modsel, norm_g, dlo, dga, dgp, dpool, dxn, w_in_t):
    s_len, lc = x.shape[0], ctx.shape[0]
    t_all = s_len + lc
    off = lc // TB
    nb = t_all // TB
    nq = s_len // TB
    hb = TB // HALO
    n = TB + 2 * HALO

    def body(ctx_ref, x_ref, mod_ref, ng_ref, dlo_ref, dga_ref, dgp_ref, dp_ref, dpprev_ref, dpnext_ref, dxn_ref, win_ref,
             gx_ref, dwin_ref, dmod_ref, dng_ref):
        i = pl.program_id(0)
        j = i - off

        @pl.when(i == 0)
        def _():
            dwin_ref[...] = jnp.zeros_like(dwin_ref)
            dmod_ref[...] = jnp.zeros_like(dmod_ref)
            dng_ref[...] = jnp.zeros_like(dng_ref)

        latent = i >= off
        dp = dp_ref[...]
        prev = jnp.where(j <= 0, 0.0, dpprev_ref[...])
        nxt = jnp.where(j >= nq - 1, 0.0, dpnext_ref[...])
        win = jnp.concatenate([prev, dp, nxt], axis=0)
        tg = j * TB - HALO + lax.broadcasted_iota(jnp.int32, (n, 1), 0)
        dpin = []
        for g, w in enumerate(POOL_WINDOWS):
            cnt = jnp.maximum(jnp.minimum(tg + w // 2, s_len) - jnp.maximum(tg - w // 2, 0), 1).astype(F32)
            zq = win[:, g * 128:(g + 1) * 128] / cnt
            zq = zq + _shift_rows(zq, 1)
            for step in (1, 2, 4):
                if w >= 4 * step:
                    zq = _shift_rows(zq, -step) + _shift_rows(zq, step)
            dpin.append(zq[HALO:HALO + TB] - dp[:, g * 128:(g + 1) * 128])
        zero = jnp.zeros((TB, 512), BF16)
        du = [dlo_ref[...], jnp.where(latent, dga_ref[...], zero),
              jnp.where(latent, jnp.concatenate(dpin, axis=1).astype(BF16), zero), jnp.where(latent, dgp_ref[...], zero)]

        ng = ng_ref[...]
        xb = jnp.where(i < off, ctx_ref[...], x_ref[...])
        r, xh, xg, h, scale = _modulated(xb, mod_ref, ng)
        hb_ = h.astype(BF16)
        dh = jnp.zeros((TB, D), F32)
        for s, (lo, hi) in enumerate(SEG):
            dwin_ref[lo:hi, :] += _tn(du[s], hb_)
            dh += _nn(du[s], win_ref[lo:hi, :])
        is_lat = latent.astype(F32)
        dsh = _colsum(dh)
        dsc = _colsum(dh * xg)
        dmod_ref[0, 0:1, :] += dsh * (1.0 - is_lat)
        dmod_ref[0, 1:2, :] += dsc * (1.0 - is_lat)
        dmod_ref[1, 0:1, :] += dsh * is_lat
        dmod_ref[1, 1:2, :] += dsc * is_lat
        dxg = dh * (1.0 + scale)
        dng_ref[...] += _colsum(dxg * xh)
        dxh = dxg * ng
        gx_ref[...] = r * (dxh - xh * jnp.mean(dxh * xh, axis=-1, keepdims=True)) + dxn_ref[...]

    row = lambda w: pl.BlockSpec((TB, w), lambda i: (i, 0))
    lat = lambda w: pl.BlockSpec((TB, w), lambda i: (jnp.maximum(i - off, 0), 0))
    last8 = s_len // HALO - 1
    cspec, xspec, mspec = _token_specs(off)
    return pl.pallas_call(
        body, name="in_bwd", grid=(nb,),
        in_specs=[cspec, xspec, mspec, _full((1, D)), row(512), lat(512), lat(512), lat(512),
                  pl.BlockSpec((HALO, 512), lambda i: (jnp.maximum(jnp.maximum(i - off, 0) * hb - 1, 0), 0)),
                  pl.BlockSpec((HALO, 512), lambda i: (jnp.minimum((jnp.maximum(i - off, 0) + 1) * hb, last8), 0)),
                  lat(D), _full((DIN, D))],
        out_specs=[lat(D), _full((DIN, D)), _full((2, 2, D)), _full((1, D))],
        out_shape=[jax.ShapeDtypeStruct((s_len, D), F32), jax.ShapeDtypeStruct((DIN, D), F32),
                   jax.ShapeDtypeStruct((2, 2, D), F32), jax.ShapeDtypeStruct((1, D), F32)],
        compiler_params=_params(("arbitrary",)),
    )(ctx, x, modsel, norm_g, dlo, dga, dgp, dpool, dpool, dpool, dxn, w_in_t)


def _adamw_update(w_ref, g_ref, m_ref, v_ref, d_ref, mo_ref, vo_ref):
    gv = g_ref[...]
    mn = ADAM_B1 * m_ref[...] + (1.0 - ADAM_B1) * gv
    vn = ADAM_B2 * v_ref[...] + (1.0 - ADAM_B2) * (gv * gv)
    m_hat = mn / (1.0 - ADAM_B1 ** ADAM_STEP)
    v_hat = vn / (1.0 - ADAM_B2 ** ADAM_STEP)
    d_ref[...] = -ADAM_LR * (m_hat / (jnp.sqrt(v_hat) + ADAM_EPS) + ADAM_WD * w_ref[...])
    mo_ref[...] = mn
    vo_ref[...] = vn


def _adamw_many(ws, gs, ms, vs):
    n = len(ws)
    parts = 4

    def body(*refs):
        for i in range(n):
            _adamw_update(refs[i], refs[n + i], refs[2 * n + i], refs[3 * n + i], refs[4 * n + i], refs[5 * n + i], refs[6 * n + i])
            refs[7 * n + i][...] = refs[n + i][...]

    def spec(w):
        rows, cols = w.shape
        if rows % (8 * parts) == 0:
            return pl.BlockSpec((rows // parts, cols), lambda i: (i, 0))
        if cols % (128 * parts) == 0:
            return pl.BlockSpec((rows, cols // parts), lambda i: (0, i))
        return _full((rows, cols))

    specs = [spec(w) for w in ws]
    shp = [jax.ShapeDtypeStruct(w.shape, F32) for w in ws]
    out = pl.pallas_call(body, name="adamw_many", grid=(parts,), in_specs=specs * 4, out_specs=specs * 4, out_shape=shp * 4,
                         compiler_params=_params(("arbitrary",)))(*ws, *gs, *ms, *vs)
    return out[:n], out[n:2 * n], out[2 * n:3 * n], out[3 * n:]


class _Links:
    def __init__(self, send_sems, recv_sems):
        self.send_sems, self.recv_sems, self.sends = send_sems, recv_sems, []

    def send(self, src, dst, sem, to):
        cp = pltpu.make_async_remote_copy(src, dst, self.send_sems.at[sem], self.recv_sems.at[sem], device_id=to,
                                          device_id_type=MESH)
        cp.start()
        self.sends.append(cp)

    def arrived(self, dst, sem, frm):
        pltpu.make_async_remote_copy(dst, dst, self.send_sems.at[sem], self.recv_sems.at[sem], device_id=frm,
                                     device_id_type=MESH).wait_recv()

    def drain(self):
        for cp in self.sends:
            cp.wait_send()


def _half(ref, c, axis):
    size = ref.shape[axis - 2] // 2
    win = pl.ds(pl.multiple_of(c * size, 16 if axis == 0 else 128), size)
    idx = (win, slice(None)) if axis == 0 else (slice(None), win)
    return ref.at[(slice(None),) * (len(ref.shape) - 2) + idx]


def _select_rows(slots_ref, n_slots, row=0):
    sub = lax.broadcasted_iota(jnp.int32, (8, 1), 0)
    out = None
    for d in range(n_slots):
        r = jnp.where(sub == d, jnp.broadcast_to(slots_ref[d][row:row + 1, :], (8, slots_ref.shape[-1])), 0.0)
        out = r if out is None else out + r
    return out


def _gather(c, c_ctx, w_mod, b_mod, shards, axes, slab_rows):
    nw = len(shards)
    kw = w_mod.shape[1]

    def body(*refs):
        c_ref, cc_ref, wm_hbm, b_ref = refs[:4]
        w_hbm = refs[4:4 + nw]
        a16_ref, modsel_ref = refs[4 + nw:6 + nw]
        out_refs = refs[6 + nw:6 + 2 * nw]
        g_refs = refs[6 + 2 * nw:6 + 3 * nw]
        f_refs = refs[6 + 3 * nw:6 + 4 * nw]
        wm_ref, a_ref, mod_ref, send_sems, recv_sems, local_sems = refs[6 + 4 * nw:]
        loads = [pltpu.make_async_copy(w_hbm[wi], f_refs[wi], local_sems.at[wi]) for wi in range(nw)]
        loads.append(pltpu.make_async_copy(wm_hbm, wm_ref, local_sems.at[nw]))
        for cp in loads:
            cp.start()
        stores = []

        def slab(wi, chip, of=g_refs):
            return of[wi].at[chip].at[0:shards[wi].shape[0]]

        def store(src, dst, wi, slot):
            cp = pltpu.make_async_copy(src, dst, local_sems.at[nw + 1 + wi * 8 + slot])
            cp.start()
            stores.append(cp)

        def store_half(wi, chip, half, slot):
            store(_half(slab(wi, chip), half, axes[wi]), _half(slab(wi, chip, out_refs), half, axes[wi]), wi, slot)
        x, y, cc = lax.axis_index("x"), lax.axis_index("y"), lax.axis_index("c")
        me = 4 * x + 2 * y + cc
        k = 2 * x + y
        sibling = (x, y, 1 - cc)
        links = _Links(send_sems, recv_sems)
        chips = [_peer(x, y, cc, off + (0,)) for off in CHIPS3]
        chip_a = ((x + 1 - cc) % 2, (y + cc) % 2, cc)
        chip_b = ((x + cc) % 2, (y + 1 - cc) % 2, cc)
        chip_d = (1 - x, 1 - y, cc)
        cv = c_ref[...]
        sc = cv * _sig(cv)
        mine = a_ref.at[me]
        for r in range(8):
            mine[r:r + 1, :] = sc[:, r * 128:(r + 1) * 128]
        for j, off in enumerate(PEERS7):
            links.send(a_ref.at[me], a_ref.at[me], j, _peer(x, y, cc, off))
        for wi in range(nw):
            loads[wi].wait()
            slab(wi, k)[...] = f_refs[wi][...].astype(BF16)
            for j, to in enumerate((chip_a, chip_b)):
                links.send(_half(slab(wi, k), cc, axes[wi]), _half(slab(wi, k), cc, axes[wi]), 10 + wi * 6 + j, to)
            store(slab(wi, k), slab(wi, k, out_refs), wi, 0)
            rows = shards[wi].shape[0]
            pad = slab_rows[wi] - rows
            if pad:
                for kk in range(4):
                    g_refs[wi][kk, rows:, :] = jnp.zeros((pad, shards[wi].shape[1]), BF16)
                store(g_refs[wi].at[:, pl.ds(rows, pad), :], out_refs[wi].at[:, pl.ds(rows, pad), :], wi, 7)
        for j, off in enumerate(PEERS7):
            px, py, pc = _peer(x, y, cc, off)
            links.arrived(a_ref.at[4 * px + 2 * py + pc], j, (px, py, pc))
        ccv = cc_ref[...]
        sub = lax.broadcasted_iota(jnp.int32, (8, 1), 0)
        top = jnp.zeros((8, D), F32)
        for d in range(8):
            blk = a_ref[d]
            row = jnp.concatenate([blk[r:r + 1, :] for r in range(8)], axis=1)
            top = top + jnp.where(sub == d, jnp.broadcast_to(row, (8, D)), 0.0)
        a16 = jnp.concatenate([top, jnp.where(sub == 0, jnp.broadcast_to(ccv * _sig(ccv), (8, D)), 0.0)], axis=0)
        a16_ref[...] = a16
        loads[nw].wait()
        b_k = jnp.zeros((1, kw), F32)
        for kk in range(4):
            b_k = b_k + jnp.where(k == kk, b_ref[:, kk * kw:(kk + 1) * kw], 0.0)
        mod_ref[k] = _dot3(_nn, a16, wm_ref[...]) + b_k
        for j, to in enumerate(chips):
            links.send(mod_ref.at[k], mod_ref.at[k], 7 + j, to)

        def over_ici(j, frm, origin):
            for wi in range(nw):
                blk = _half(slab(wi, 2 * origin[0] + origin[1]), cc, axes[wi])
                links.arrived(blk, 10 + wi * 6 + j, frm)
                if j == 0:
                    links.send(blk, blk, 10 + wi * 6 + 2, chip_b)
                links.send(blk, blk, 10 + wi * 6 + 3 + j, sibling)
                store_half(wi, 2 * origin[0] + origin[1], cc, 1 + j)

        def from_sibling(j, origin):
            for wi in range(nw):
                links.arrived(_half(slab(wi, 2 * origin[0] + origin[1]), 1 - cc, axes[wi]), 10 + wi * 6 + 3 + j, sibling)
                store_half(wi, 2 * origin[0] + origin[1], 1 - cc, 4 + j)
        over_ici(0, chip_a, chip_a)
        over_ici(1, chip_b, chip_b)
        from_sibling(0, chip_b)
        over_ici(2, chip_b, chip_d)
        from_sibling(1, chip_a)
        from_sibling(2, chip_d)
        for j, (px, py, pc) in enumerate(chips):
            links.arrived(mod_ref.at[2 * px + py], 7 + j, (px, py, pc))
        for sel, row in ((0, 8), (1, me)):
            for kk in range(4):
                piece = mod_ref[kk, pl.ds(row, 1), :]
                lo = kk * kw
                while lo < (kk + 1) * kw:
                    r, col = divmod(lo, D)
                    n = min((kk + 1) * kw - lo, D - col)
                    modsel_ref[sel, r:r + 1, col:col + n] = piece[:, lo - kk * kw:lo - kk * kw + n]
                    lo += n
        links.drain()
        for cp in stores:
            cp.wait()

    nsem = 10 + 6 * nw
    gathered = [(4, r, s.shape[1]) for r, s in zip(slab_rows, shards)]
    return pl.pallas_call(
        body, name="gather", in_specs=[VM, VM, ANY, VM] + [ANY] * nw, out_specs=[VM, VM] + [ANY] * nw,
        out_shape=[jax.ShapeDtypeStruct((16, D), F32), jax.ShapeDtypeStruct((2, 3, D), F32)]
        + [jax.ShapeDtypeStruct(g, BF16) for g in gathered],
        scratch_shapes=[pltpu.VMEM(g, BF16) for g in gathered] + [pltpu.VMEM(s.shape, F32) for s in shards]
        + [pltpu.VMEM(w_mod.shape, F32), pltpu.VMEM((8, 8, D // 8), F32), pltpu.VMEM((4, 16, kw), F32),
           pltpu.SemaphoreType.DMA((nsem,)),
           pltpu.SemaphoreType.DMA((nsem,)), pltpu.SemaphoreType.DMA((nw + 1 + 8 * nw,))],
        compiler_params=pltpu.CompilerParams(vmem_limit_bytes=VMEM_LIMIT),
    )(c, c_ctx, w_mod, b_mod, *shards)


SMALL_ROW_WIDTHS = (D, QL, KVL, DKP, DKP, 512, 128)
SMALL_OUT_WIDTHS = (D, QL, KVL, DK, DK, 512, 1)
SMALL_PACK = 384


def _small_pieces():
    pieces = []
    for i, w in enumerate(SMALL_ROW_WIDTHS):
        for c0 in range(0, w, 128):
            j = len(pieces)
            pieces.append((i, c0, j // (SMALL_PACK // 128), j % (SMALL_PACK // 128) * 128))
    assert len(pieces) <= 8 * (SMALL_PACK // 128)
    return pieces


def _reduce(grads, axes, smalls, w_pool_g, dmod, dgate, a16, w_mod, c_ctx):
    nw = len(grads)
    ns = len(smalls)
    kw = w_mod.shape[1]
    halves = []
    for g, ax in zip(grads, axes):
        halves.append((g.shape[1] // 2, g.shape[2]) if ax == 0 else (g.shape[1], g.shape[2] // 2))

    def body(*refs):
        g_refs = refs[:nw]
        small_refs = refs[nw:nw + ns]
        wp_ref, dmod_ref, dgate_ref, a16_ref, wm_hbm, cc_ref = refs[nw + ns:nw + ns + 6]
        o = nw + ns + 6
        r_outs = refs[o:o + nw]
        small_outs = refs[o + nw:o + nw + ns]
        rwp_ref, gw_out, gb_ref, gc_ref = refs[o + nw + ns:o + nw + ns + 4]
        o = o + nw + ns + 4
        own, sib, part, got, rel, r_refs = (refs[o + i * nw:o + (i + 1) * nw] for i in range(6))
        smbuf, wps, wpg, dm_all, pc_all, wm_ref, gw_ref, send_sems, recv_sems, local_sems = refs[o + 6 * nw:]
        wm_load = pltpu.make_async_copy(wm_hbm, wm_ref, local_sems.at[nw])
        wm_load.start()
        stores = []

        def store(src, dst, sem):
            cp = pltpu.make_async_copy(src, dst, local_sems.at[nw + 1 + sem])
            cp.start()
            stores.append(cp)
        x, y, cc = lax.axis_index("x"), lax.axis_index("y"), lax.axis_index("c")
        me = 4 * x + 2 * y + cc
        k = 2 * x + y
        sibling = (x, y, 1 - cc)
        links = _Links(send_sems, recv_sems)
        chips = [_peer(x, y, cc, off + (0,)) for off in CHIPS3]
        peers = [_peer(x, y, cc, off) for off in PEERS7]
        chip_a = ((x + 1 - cc) % 2, (y + cc) % 2, cc)
        chip_b = ((x + cc) % 2, (y + 1 - cc) % 2, cc)
        ka, kb, kd = 2 * chip_a[0] + chip_a[1], 2 * chip_b[0] + chip_b[1], 2 * (1 - x) + (1 - y)
        big, sm0, wp0, dm0, pc0 = 0, 5 * nw, 5 * nw + 7, 5 * nw + 14, 5 * nw + 21

        locals_ = []
        for wi in range(nw):
            lc = pltpu.make_async_copy(_half(g_refs[wi], cc, axes[wi]), own[wi], local_sems.at[wi])
            lc.start()
            locals_.append(lc)
            links.send(_half(g_refs[wi], 1 - cc, axes[wi]), sib[wi], big + wi * 5, sibling)
        slot = smbuf.at[me]
        slot[...] = jnp.zeros((8, SMALL_PACK), F32)
        small_rows = [jnp.broadcast_to(ref[...], (1, w)) for ref, w in zip(small_refs, SMALL_ROW_WIDTHS)]
        for i, c0, row, lane in _small_pieces():
            slot[row:row + 1, lane:lane + 128] = small_rows[i][:, c0:c0 + 128]
        links.send(wp_ref, wps, wp0, sibling)
        dm_mine = dm_all.at[me]
        dm_mine[...] = jnp.zeros((8, kw), F32)
        dm_rows = ((0, (dmod_ref[1, 0:1, :], dmod_ref[1, 1:2, :], dgate_ref[...])), (4, (dmod_ref[0, 0:1, :], dmod_ref[0, 1:2, :])))
        for base, rows in dm_rows:
            for r, v in enumerate(rows):
                lo = r * D
                while lo < (r + 1) * D:
                    kk, col = divmod(lo, kw)
                    n = min((r + 1) * D - lo, kw - col)
                    dm_mine[base + kk:base + kk + 1, col:col + n] = v[:, lo - r * D:lo - r * D + n]
                    lo += n
        for j, peer in enumerate(peers):
            links.send(dm_all.at[me], dm_all.at[me], dm0 + j, peer)
            links.send(smbuf.at[me], smbuf.at[me], sm0 + j, peer)
        links.arrived(wps, wp0, sibling)
        wpg[k] = (wp_ref[...] + wps[...]).astype(BF16)
        for j, to in enumerate(chips):
            links.send(wpg.at[k], wpg.at[k], wp0 + 1 + j, to)
        for wi in range(nw):
            locals_[wi].wait()
            links.arrived(sib[wi], big + wi * 5, sibling)
            for slab_k in (kd, kb):
                part[wi][slab_k] = (own[wi][slab_k] + sib[wi][slab_k]).astype(BF16)
            links.send(part[wi].at[kd], rel[wi], big + wi * 5 + 1, chip_b)
            links.send(part[wi].at[kb], got[wi].at[k], big + wi * 5 + 2, chip_b)
            for slab_k in (ka, k):
                part[wi][slab_k] = (own[wi][slab_k] + sib[wi][slab_k]).astype(BF16)
            got[wi][k] = part[wi][k]
            got[wi][kd] = jnp.zeros(halves[wi], BF16)
        for j, (px, py, pc) in enumerate(peers):
            links.arrived(dm_all.at[4 * px + 2 * py + pc], dm0 + j, (px, py, pc))
        dm_tot = dm_all[0]
        for d in range(1, 8):
            dm_tot = dm_tot + dm_all[d]
        for kk in range(4):
            gb_ref[:, kk * kw:(kk + 1) * kw] = dm_tot[kk:kk + 1, :] + dm_tot[4 + kk:5 + kk, :]
        top = jnp.zeros((8, kw), F32)
        dmc_k = jnp.zeros((1, kw), F32)
        for kk in range(4):
            top = top + jnp.where(k == kk, _select_rows(dm_all, 8, kk), 0.0)
            dmc_k = dmc_k + jnp.where(k == kk, dm_tot[4 + kk:5 + kk, :], 0.0)
        sub = lax.broadcasted_iota(jnp.int32, (8, 1), 0)
        bk = jnp.concatenate([top, jnp.where(sub == 0, jnp.broadcast_to(dmc_k, (8, kw)), 0.0)], axis=0)
        gw_ref[...] = _dot3(_tn, a16_ref[...], bk)
        store(gw_ref, gw_out, 0)
        wm_load.wait()
        pc_all[k] =_dot3(_nt, jnp.broadcast_to(bk[8:9, :], (8, kw)), wm_ref[...])
        for j, to in enumerate(chips):
            links.send(pc_all.at[k], pc_all.at[k], pc0 + j, to)
        for j, (px, py, pc) in enumerate(peers):
            links.arrived(smbuf.at[4 * px + 2 * py + pc], sm0 + j, (px, py, pc))
        tot = smbuf[0]
        for d in range(1, 8):
            tot = tot + smbuf[d]
        for i, c0, row, lane in _small_pieces():
            n = min(128, SMALL_OUT_WIDTHS[i] - c0)
            if n > 0:
                small_outs[i][:, c0:c0 + n] = tot[row:row + 1, lane:lane + n]
        for j, (px, py, pc) in enumerate(chips):
            links.arrived(wpg.at[2 * px + py], wp0 + 1 + j, (px, py, pc))
        wpt = wpg[0].astype(F32)
        for kk in range(1, 4):
            wpt = wpt + wpg[kk].astype(F32)
        rwp_ref[...] = wpt
        for wi in range(nw):
            links.arrived(rel[wi], big + wi * 5 + 1, chip_b)
            rel[wi][...] = (part[wi][ka].astype(F32) + rel[wi][...].astype(F32)).astype(BF16)
            links.send(rel[wi], got[wi].at[k], big + wi * 5 + 3, chip_a)
        for wi in range(nw):
            links.arrived(got[wi].at[kb], big + wi * 5 + 2, chip_b)
            links.arrived(got[wi].at[ka], big + wi * 5 + 3, chip_a)
            total = got[wi][0].astype(F32)
            for kk in range(1, 4):
                total = total + got[wi][kk].astype(F32)
            mine = _half(r_refs[wi], cc, axes[wi])
            mine[...] = total
            links.send(mine, mine, big + wi * 5 + 4, sibling)
            store(mine, _half(r_outs[wi], cc, axes[wi]), 1 + wi)
        for j, (px, py, pc) in enumerate(chips):
            links.arrived(pc_all.at[2 * px + py], pc0 + j, (px, py, pc))
        ccv = cc_ref[...]
        sg = _sig(ccv)
        gc_ref[...] = (pc_all[0][0:1, :] + pc_all[1][0:1, :] + pc_all[2][0:1, :] + pc_all[3][0:1, :]) * (sg * (1.0 + ccv * (1.0 - sg)))
        for wi in range(nw):
            links.arrived(_half(r_refs[wi], 1 - cc, axes[wi]), big + wi * 5 + 4, sibling)
            store(_half(r_refs[wi], 1 - cc, axes[wi]), _half(r_outs[wi], 1 - cc, axes[wi]), 1 + nw + wi)
        links.drain()
        for cp in stores:
            cp.wait()

    nsem = 5 * nw + 24
    quads = [(4,) + h for h in halves]
    return pl.pallas_call(
        body, name="reduce", in_specs=[ANY] * nw + [VM] * (ns + 4) + [ANY, VM],
        out_specs=[ANY] * nw + [VM] * (ns + 1) + [ANY, VM, VM],
        out_shape=[jax.ShapeDtypeStruct(g.shape[1:], F32) for g in grads]
        + [jax.ShapeDtypeStruct((1, w), F32) for w in SMALL_OUT_WIDTHS]
        + [jax.ShapeDtypeStruct(w_pool_g.shape, F32), jax.ShapeDtypeStruct((D, kw), F32), jax.ShapeDtypeStruct((1, 3 * D), F32),
           jax.ShapeDtypeStruct((1, D), F32)],
        scratch_shapes=[pltpu.VMEM(q, F32) for q in quads] + [pltpu.VMEM(q, F32) for q in quads]
        + [pltpu.VMEM(q, BF16) for q in quads] + [pltpu.VMEM(q, BF16) for q in quads] + [pltpu.VMEM(h, BF16) for h in halves]
        + [pltpu.VMEM(g.shape[1:], F32) for g in grads]
        + [pltpu.VMEM((8, 8, SMALL_PACK), F32), pltpu.VMEM(w_pool_g.shape, F32), pltpu.VMEM((4,) + w_pool_g.shape, BF16),
           pltpu.VMEM((8, 8, kw), F32), pltpu.VMEM((4, 8, D), F32), pltpu.VMEM((D, kw), F32), pltpu.VMEM((D, kw), F32)]
        + [pltpu.SemaphoreType.DMA((nsem,)), pltpu.SemaphoreType.DMA((nsem,)), pltpu.SemaphoreType.DMA((3 * nw + 2,))],
        compiler_params=pltpu.CompilerParams(vmem_limit_bytes=VMEM_LIMIT),
    )(*grads, *smalls, w_pool_g, dmod, dgate, a16, w_mod, c_ctx)


def _rope_tables(s_len):
    rows = s_len // GRID_W
    per = TB // GRID_W
    n_freq = 16
    f32 = np.float32
    inv = f32(ROPE_BASE) ** (-np.arange(n_freq, dtype=f32) / f32(n_freq))
    ang_r = np.arange(rows, dtype=f32)[:, None] * inv
    ang_c = np.arange(GRID_W, dtype=f32)[:, None] * inv
    by_row, by_col = [], []
    for fn, pad in ((np.cos, 1.0), (np.sin, 0.0)):
        r = np.concatenate([fn(ang_r), fn(ang_r), np.zeros((rows, 96), f32)], axis=1).reshape(rows // per, per, 128)
        by_row.append(np.pad(r, ((0, 0), (0, 8 - per), (0, 0))))
        cpart = np.concatenate([np.zeros((GRID_W, 32), f32), fn(ang_c), fn(ang_c), np.full((GRID_W, 64), pad, f32)], axis=1)
        by_col.append(np.tile(cpart, (per, 1)))
    return jnp.asarray(np.concatenate(by_row, axis=-1), F32), jnp.asarray(np.concatenate(by_col, axis=-1), F32)


def kernel(x, c, ctx, c_ctx, w_mod, b_mod, norm_g, w_in, q_lora_g, w_uq, kv_lora_g, w_ukv, q_norm_g, k_norm_g, w_pool, pool_scale, w_out, loss_target, m_c_ctx, m_w_mod, m_b_mod, m_norm_g, m_w_in, m_q_lora_g, m_w_uq, m_kv_lora_g, m_w_ukv, m_q_norm_g, m_k_norm_g, m_w_pool, m_pool_scale, m_w_out, v_c_ctx, v_w_mod, v_b_mod, v_norm_g, v_w_in, v_q_lora_g, v_w_uq, v_kv_lora_g, v_w_ukv, v_q_norm_g, v_k_norm_g, v_w_pool, v_pool_scale, v_w_out):
    s_len = x.shape[1]
    lc = ctx.shape[1]
    weights = dict(c_ctx=c_ctx, w_mod=w_mod, b_mod=b_mod, norm_g=norm_g, w_in=w_in, q_lora_g=q_lora_g, w_uq=w_uq,
                   kv_lora_g=kv_lora_g, w_ukv=w_ukv, q_norm_g=q_norm_g, k_norm_g=k_norm_g, w_pool=w_pool,
                   pool_scale=pool_scale, w_out=w_out)
    m_in = dict(c_ctx=m_c_ctx, w_mod=m_w_mod, b_mod=m_b_mod, norm_g=m_norm_g, w_in=m_w_in, q_lora_g=m_q_lora_g, w_uq=m_w_uq,
                kv_lora_g=m_kv_lora_g, w_ukv=m_w_ukv, q_norm_g=m_q_norm_g, k_norm_g=m_k_norm_g, w_pool=m_w_pool,
                pool_scale=m_pool_scale, w_out=m_w_out)
    v_in = dict(c_ctx=v_c_ctx, w_mod=v_w_mod, b_mod=v_b_mod, norm_g=v_norm_g, w_in=v_w_in, q_lora_g=v_q_lora_g, w_uq=v_w_uq,
                kv_lora_g=v_kv_lora_g, w_ukv=v_w_ukv, q_norm_g=v_q_norm_g, k_norm_g=v_k_norm_g, w_pool=v_w_pool,
                pool_scale=v_pool_scale, w_out=v_w_out)
    order = ["c_ctx", "w_mod", "b_mod", "norm_g", "w_in", "q_lora_g", "w_uq", "kv_lora_g", "w_ukv", "q_norm_g", "k_norm_g",
             "w_pool", "pool_scale", "w_out"]
    transposed = ("w_in", "w_uq")
    as2d = lambda n, a: jnp.transpose(a[0]) if n in transposed else a.reshape(-1, a.shape[-1])
    back = lambda n, a: jnp.transpose(a)[None] if n in transposed else a.reshape(weights[n].shape)

    c_ctx2 = c_ctx.reshape(1, D)
    split = (0, 1, 0, 0)
    a16, modsel, g_out, g_in, g_uq, g_ukv = _gather(
        c, c_ctx2, w_mod[0], b_mod, [w_out[0], as2d("w_in", w_in), as2d("w_uq", w_uq), w_ukv[0]], split,
        (D // 4, DIN // 4, DKP, KVL))
    w_in_t = g_in.reshape(DIN, D)
    w_uq_t = g_uq
    w_out_f = g_out.reshape(D, D)
    cos, sin = _rope_tables(s_len)

    u, q, kk, v = _fwd_in(ctx[0], x[0], modsel, norm_g, w_in_t, q_lora_g, w_uq_t, kv_lora_g, g_ukv, q_norm_g, k_norm_g, cos, sin)
    attn, lse = _attn_fwd(q, kk, v, s_len)
    (dxn, dattn, dga, dgp, dpool, dw_out, dgate, dps, dw_pool, loss) = _out_stage(
        attn.reshape(s_len // Q_BLOCK, Q_BLOCK, NH * DV), u, x[0], loss_target[0], modsel, w_pool[0], pool_scale,
        w_out_f, lc)
    dattn = dattn.reshape(s_len, NH * DV)
    dq, dk, dv = _attn_bwd(q, kk, v, dattn, attn, lse, s_len)
    dlo, dw_uq_t, dw_ukv, dqlg, dkvlg, dqng, dkng = _qkv_bwd(u, dq, dk, dv, cos, sin, q_lora_g, w_uq_t, kv_lora_g, g_ukv,
                                                            q_norm_g, k_norm_g, s_len)
    gx, dw_in_t, dmod, dng = _in_bwd(ctx[0], x[0], modsel, norm_g, dlo, dga, dgp, dpool, dxn, w_in_t)

    r_out, r_in, r_uq, r_ukv, g_ng, g_qlg, g_kvlg, g_qng, g_kng, g_ps, loss_all, g_wp, g_w_mod, g_b_mod, g_c_ctx = _reduce(
        [dw_out.reshape(4, D // 4, D), dw_in_t.reshape(4, DIN // 4, D), dw_uq_t, dw_ukv], split,
        [dng, dqlg, dkvlg, dqng, dkng, dps, loss], dw_pool, dmod, dgate, a16, w_mod[0], c_ctx2)
    g2d = dict(c_ctx=g_c_ctx, b_mod=g_b_mod, w_mod=g_w_mod, w_in=r_in, w_uq=r_uq, w_ukv=r_ukv, w_out=r_out, norm_g=g_ng,
               q_lora_g=g_qlg, kv_lora_g=g_kvlg, q_norm_g=g_qng, k_norm_g=g_kng, pool_scale=g_ps, w_pool=g_wp.reshape(512, 128))

    outs = _adamw_many([as2d(n, weights[n]) for n in order], [g2d[n] for n in order], [as2d(n, m_in[n]) for n in order],
                       [as2d(n, v_in[n]) for n in order])
    d2d, m2d, v2d, g2d = (dict(zip(order, arrs)) for arrs in outs)

    return (loss_all[0, 0], gx[None], *[back(n, g2d[n]) for n in order], *[back(n, d2d[n]) for n in order],
            *[back(n, m2d[n]) for n in order], *[back(n, v2d[n]) for n in order])
```

```python
import jax
import jax.numpy as jnp
import numpy as np
from jax import lax
from jax.experimental import pallas as pl
from jax.experimental.pallas import tpu as pltpu

F32 = jnp.float32
BF16 = jnp.bfloat16
MESH = pl.DeviceIdType.MESH

D = 1024
NH = 4
DK = 192
DKP = 256
DV = 128
QL = 256
KVL = 128
DIN = 1984
U_LO = 448
SEG = ((0, 512), (448, 960), (960, 1472), (1472, 1984))
DU = 2048
POOL_WINDOWS = (2, 4, 8, 16)
HALO = 8
EPS = 1e-6
ROPE_BASE = 10000.0
GRID_W = 64
Q_BLOCK = 128
TB = 256
BWD_QBLOCKS = 1
SCALE = DK ** -0.5
LOG2E = 1.4426950408889634
LN2 = 0.6931471805599453
VMEM_LIMIT = 56 * 1024 * 1024

ADAM_LR = 0.001
ADAM_B1 = 0.9
ADAM_B2 = 0.999
ADAM_EPS = 1e-08
ADAM_WD = 0.01
ADAM_STEP = 10

CHIPS3 = ((1, 0), (0, 1), (1, 1))
PEERS7 = tuple((dx, dy, dc) for dx in (0, 1) for dy in (0, 1) for dc in (0, 1) if (dx, dy, dc) != (0, 0, 0))

VM = pl.BlockSpec(memory_space=pltpu.VMEM)
ANY = pl.BlockSpec(memory_space=pl.ANY)


def _nn(a, b):
    return jnp.dot(a, b, preferred_element_type=F32)


def _nt(a, b):
    return lax.dot_general(a, b, (((1,), (1,)), ((), ())), preferred_element_type=F32)


def _tn(a, b):
    return lax.dot_general(a, b, (((0,), (0,)), ((), ())), preferred_element_type=F32)


def _split3(a):
    a0 = a.astype(BF16)
    r = a - a0.astype(F32)
    a1 = r.astype(BF16)
    a2 = (r - a1.astype(F32)).astype(BF16)
    return a0, a1, a2


def _dot3(dot, a, b):
    sa = _split3(a)
    sb = _split3(b)
    out = None
    for i in range(3):
        for j in range(3 - i):
            t = dot(sa[i], sb[j])
            out = t if out is None else out + t
    return out


def _sig(x):
    return 1.0 / (1.0 + jnp.exp(-x))


def _rot(t):
    src = lax.broadcasted_iota(jnp.int32, (128, 128), 0)
    dst = lax.broadcasted_iota(jnp.int32, (128, 128), 1)
    first = (dst % 32) < 16
    perm = jnp.where(first & (src == dst + 16), -1.0, jnp.where(~first & (src == dst - 16), 1.0, 0.0)).astype(BF16)
    hi = t.astype(BF16)
    lo = (t - hi.astype(F32)).astype(BF16)
    return _nn(hi, perm) + _nn(lo, perm)


def _rope(t, cos, sin):
    return t * cos + _rot(t) * sin


def _rope_t(t, cos, sin):
    return t * cos - _rot(t * sin)


def _rope_block(rows_ref, cols_ref, is_ctx):
    lane = lax.broadcasted_iota(jnp.int32, (TB, 256), 1) % 128
    rows = jnp.concatenate([jnp.broadcast_to(rows_ref[0, r:r + 1, :], (GRID_W, 256)) for r in range(TB // GRID_W)], axis=0)
    cs = jnp.where(lane < 32, rows, cols_ref[...])
    return jnp.where(is_ctx, 1.0, cs[:, :128]), jnp.where(is_ctx, 0.0, cs[:, 128:])


def _shift_rows(z, k):
    n = z.shape[0]
    return pltpu.roll(z, (n - k) % n, 0)


def _colsum(a):
    return jnp.sum(a, axis=0, keepdims=True)


def _rowsum(a):
    return jnp.sum(a, axis=-1, keepdims=True)


def _row_layout(col):
    return jnp.transpose(jnp.broadcast_to(col, (col.shape[0], 128)))[0:8, :]


def _params(sem=None):
    return pltpu.CompilerParams(dimension_semantics=sem, vmem_limit_bytes=VMEM_LIMIT)


def _full(shape):
    nd = len(shape)
    return pl.BlockSpec(shape, lambda *_: (0,) * nd)


def _peer(x, y, c, off):
    dx, dy, dc = off
    return ((x + dx) % 2, (y + dy) % 2, (c + dc) % 2)


def _token_specs(off):
    ctx = pl.BlockSpec((TB, D), lambda i: (jnp.minimum(i, off - 1), 0))
    lat = pl.BlockSpec((TB, D), lambda i: (jnp.maximum(i - off, 0), 0))
    mod = pl.BlockSpec((1, 3, D), lambda i: (jnp.minimum(i // off, 1), 0, 0))
    return ctx, lat, mod


def _head_gains(qng_ref, kng_ref, pad_ref):
    pad_ref[...] = jnp.zeros((2, DKP), F32)
    pad_ref[0:1, 0:DK] = qng_ref[...]
    pad_ref[1:2, 0:DK] = kng_ref[...]
    return pad_ref[0:1, :], pad_ref[1:2, :]


def _modulated(x, mod_ref, ng):
    shift = mod_ref[0, 0:1, :]
    scale = mod_ref[0, 1:2, :]
    r = lax.rsqrt(jnp.mean(x * x, axis=-1, keepdims=True) + EPS)
    xh = x * r
    xg = xh * ng
    return r, xh, xg, xg * (1.0 + scale) + shift, scale


def _fwd_in(ctx, x, modsel, norm_g, w_in_t, q_lora_g, w_uq_t, kv_lora_g, w_ukv, qn_g, kn_g, cos, sin):
    s_len, lc = x.shape[0], ctx.shape[0]
    t_all = s_len + lc
    nb = t_all // TB
    off = lc // TB

    def body(ctx_ref, x_ref, mod_ref, ng_ref, win_ref, qlg_ref, wuq_ref, kvlg_ref, wukv_ref, qng_ref, kng_ref, cos_ref, sin_ref,
             u_ref, q_ref, k_ref, v_ref, pad_ref):
        is_ctx = pl.program_id(0) < off
        qng, kng = _head_gains(qng_ref, kng_ref, pad_ref)
        xb = jnp.where(is_ctx, ctx_ref[...], x_ref[...])
        _, _, _, h, _ = _modulated(xb, mod_ref, ng_ref[...])
        hb = h.astype(BF16)
        lane = lax.broadcasted_iota(jnp.int32, (TB, 512), 1)
        ulo = jnp.where(lane < U_LO, _nt(hb, win_ref[SEG[0][0]:SEG[0][1], :]), 0.0)
        u_ref[:, 0:512] = ulo
        for j in range(1, 4):
            u_ref[:, j * 512:(j + 1) * 512] = _nt(hb, win_ref[SEG[j][0]:SEG[j][1], :])
        cos, sin = _rope_block(cos_ref, sin_ref, is_ctx)
        cq = ulo[:, 0:QL]
        cqn = (cq * lax.rsqrt(jnp.mean(cq * cq, axis=-1, keepdims=True) + EPS) * qlg_ref[...]).astype(BF16)
        ckv = ulo[:, QL:QL + KVL]
        ckvn = (ckv * lax.rsqrt(jnp.mean(ckv * ckv, axis=-1, keepdims=True) + EPS) * kvlg_ref[...]).astype(BF16)
        qhs = [_nt(cqn, wuq_ref[hd]) for hd in range(NH)]
        kvs = [_nn(ckvn, wukv_ref[hd]) for hd in range(NH)]
        for hd in range(NH):
            qh = qhs[hd]
            qn = qh * lax.rsqrt(_rowsum(qh * qh) / DK + EPS) * qng
            q_ref[hd] = (jnp.concatenate([qn[:, :128], _rope(qn[:, 128:], cos, sin)], axis=1) * (SCALE * LOG2E)).astype(BF16)
        kr = ulo[:, 384:512]
        skr = _rowsum(kr * kr)
        kr_roped =_rope(kr * kng[:, 128:], cos, sin)
        for hd in range(NH):
            kv = kvs[hd]
            kn = kv[:, :128]
            rk = lax.rsqrt((_rowsum(kn * kn) + skr) / DK + EPS)
            k_ref[hd] = jnp.concatenate([kn * rk * kng[:, :128], kr_roped * rk], axis=1).astype(BF16)
            v_ref[hd] = kv[:, 128:].astype(BF16)

    row = lambda w: pl.BlockSpec((TB, w), lambda i: (i, 0))
    heads = lambda w: pl.BlockSpec((NH, TB, w), lambda i: (0, i, 0))
    cspec, xspec, mspec = _token_specs(off)
    return pl.pallas_call(
        body, name="fwd_in", grid=(nb,),
        in_specs=[cspec, xspec, mspec, _full((1, D)), _full((DIN, D)), _full((1, QL)), _full((NH, DKP, QL)), _full((1, KVL)),
                  _full((NH, KVL, 256)), _full((1, DK)), _full((1, DK)),
                  pl.BlockSpec((1, 8, 256), lambda i: (jnp.maximum(i - off, 0), 0, 0)), _full((TB, 256))],
        out_specs=[row(DU), heads(DKP), heads(DKP), heads(DV)],
        out_shape=[jax.ShapeDtypeStruct((t_all, DU), F32), jax.ShapeDtypeStruct((NH, t_all, DKP), BF16),
                   jax.ShapeDtypeStruct((NH, t_all, DKP), BF16), jax.ShapeDtypeStruct((NH, t_all, DV), BF16)],
        scratch_shapes=[pltpu.VMEM((2, DKP), F32)],
        compiler_params=_params(("arbitrary",)),
    )(ctx, x, modsel, norm_g, w_in_t, q_lora_g, w_uq_t, kv_lora_g, w_ukv, qn_g, kn_g, cos, sin)


def _attn_fwd(q, k, v, s_len):
    t_all = q.shape[1]
    off = (t_all - s_len) // TB
    nq = s_len // TB
    nsub = next(n for n in (4, 2, 1) if nq % n == 0)

    def body(*refs):
        q_refs = refs[:nsub]
        k_ref, v_ref, o_ref, lse_ref = refs[nsub:]
        for sb in range(nsub):
            s = _nt(q_refs[sb][0], k_ref[0])
            m = jnp.max(s, axis=-1, keepdims=True)
            e = jnp.exp2(s - m)
            l = _rowsum(e)
            o_ref[sb * TB:(sb + 1) * TB, :] = _nn(e.astype(BF16), v_ref[0]) / l
            lse_ref[0, sb] = _row_layout(m + jnp.log2(l))

    qspec = lambda sb: pl.BlockSpec((1, TB, DKP), lambda h, i: (h, i * nsub + sb + off, 0))
    return pl.pallas_call(
        body, name="attn_fwd", grid=(NH, nq // nsub),
        in_specs=[qspec(sb) for sb in range(nsub)]
        + [pl.BlockSpec((1, t_all, DKP), lambda h, i: (h, 0, 0)), pl.BlockSpec((1, t_all, DV), lambda h, i: (h, 0, 0))],
        out_specs=[pl.BlockSpec((nsub * TB, DV), lambda h, i: (i, h)), pl.BlockSpec((1, nsub, 8, TB), lambda h, i: (h, i, 0, 0))],
        out_shape=[jax.ShapeDtypeStruct((s_len, NH * DV), F32), jax.ShapeDtypeStruct((NH, nq, 8, TB), F32)],
        compiler_params=_params(("arbitrary", "arbitrary")),
    )(*([q] * nsub), k, v)


def _out_stage(attn, u, x, target, modsel, w_pool, pool_scale, w_out, lc):
    s_len = x.shape[0]
    t_all = s_len + lc
    off = lc // TB
    nq = s_len // TB
    hb = TB // HALO
    nqb = s_len // Q_BLOCK
    jb = TB // nqb

    def body(attn_ref, ga_ref, pin_ref, pprev_ref, pnext_ref, gp_ref, x_ref, tgt_ref, gate_ref, wp_ref, ps_ref, wo_ref,
             dxn_ref, dattn_ref, dga_ref, dgp_ref, dpool_ref, dwo_ref, dgate_ref, dps_ref, dwp_ref, loss_ref):
        i = pl.program_id(0)

        @pl.when(i == 0)
        def _():
            dwo_ref[...] = jnp.zeros_like(dwo_ref)
            dgate_ref[...] = jnp.zeros_like(dgate_ref)
            dps_ref[...] = jnp.zeros_like(dps_ref)
            dwp_ref[...] = jnp.zeros_like(dwp_ref)
            loss_ref[...] = jnp.zeros_like(loss_ref)

        attn = jnp.concatenate([attn_ref[:, jj, :] for jj in range(jb)], axis=0)
        ga = ga_ref[...]
        gp = gp_ref[...]
        pin = pin_ref[...]
        prev = jnp.where(i == 0, 0.0, pprev_ref[...])
        nxt = jnp.where(i == nq - 1, 0.0, pnext_ref[...])
        win = jnp.concatenate([prev, pin, nxt], axis=0)
        tg = i * TB + lax.broadcasted_iota(jnp.int32, (TB, 1), 0)
        pooled = []
        for g, w in enumerate(POOL_WINDOWS):
            a = win[:, g * 128:(g + 1) * 128]
            p = _shift_rows(a, -1) + a
            for step in (1, 2, 4):
                if w >= 4 * step:
                    p = _shift_rows(p, -step) + _shift_rows(p, step)
            cnt = (jnp.minimum(tg + w // 2, s_len) - jnp.maximum(tg - w // 2, 0)).astype(F32)
            pooled.append(p[HALO:HALO + TB] / cnt - a[HALO:HALO + TB])
        pooled_b = [p.astype(BF16) for p in pooled]
        wp = [wp_ref[g].astype(BF16) for g in range(4)]
        z = jnp.concatenate([_nn(pooled_b[g], wp[g]) for g in range(4)], axis=1)
        ps = ps_ref[...]
        yp = z * ps
        sga = _sig(ga)
        sila = ga * sga
        sgp = _sig(gp)
        silp = gp * sgp
        br = jnp.concatenate([sila * attn, silp * yp], axis=1).astype(BF16)
        y = _nn(br, wo_ref[...])
        gate = gate_ref[0, 2:3, :]
        err = x_ref[...] + gate * y - tgt_ref[...]
        loss_ref[...] += _colsum(_rowsum(err * err)) * (0.5 / D)
        dxn = err * (1.0 / D)
        dxn_ref[...] = dxn
        dgate_ref[...] += _colsum(dxn * y)
        dy = (dxn * gate).astype(BF16)
        dwo_ref[...] += _tn(br, dy)
        dbr = _nt(dy, wo_ref[...])
        dbra = dbr[:, :512]
        dbrp = dbr[:, 512:]
        dattn = dbra * sila
        for jj in range(jb):
            dattn_ref[:, jj, :] = dattn[jj * nqb:(jj + 1) * nqb]
        dga_ref[...] = (dbra * attn * (sga * (1.0 + ga * (1.0 - sga)))).astype(BF16)
        dgp_ref[...] = (dbrp * yp * (sgp * (1.0 + gp * (1.0 - sgp)))).astype(BF16)
        dyp = dbrp * silp
        dps_ref[...] += _colsum(dyp * z)
        dz = (dyp * ps).astype(BF16)
        dpool = []
        for g in range(4):
            dzg = dz[:, g * 128:(g + 1) * 128]
            dwp_ref[g] += _tn(pooled_b[g], dzg)
            dpool.append(_nt(dzg, wp[g]))
        dpool_ref[...] = jnp.concatenate(dpool, axis=1)

    lat = lambda w: pl.BlockSpec((TB, w), lambda i: (i, 0))
    perm = pl.BlockSpec((nqb, jb, 512), lambda i: (0, i, 0))
    ucol = lambda j: pl.BlockSpec((TB, 512), lambda i: (i + off, j))
    last8 = t_all // HALO - 1
    return pl.pallas_call(
        body, name="out_stage", grid=(nq,),
        in_specs=[perm, ucol(1), ucol(2),
                  pl.BlockSpec((HALO, 512), lambda i: ((i + off) * hb - 1, 2)),
                  pl.BlockSpec((HALO, 512), lambda i: (jnp.minimum((i + off + 1) * hb, last8), 2)),
                  ucol(3), lat(D), lat(D), pl.BlockSpec((1, 3, D), lambda i: (1, 0, 0)), _full((4, 128, 128)), _full((1, 512)),
                  _full((D, D))],
        out_specs=[lat(D), perm, lat(512), lat(512), lat(512),
                   _full((D, D)), _full((1, D)), _full((1, 512)), _full((4, 128, 128)), _full((1, 1))],
        out_shape=[jax.ShapeDtypeStruct((s_len, D), F32), jax.ShapeDtypeStruct((nqb, Q_BLOCK, 512), F32),
                   jax.ShapeDtypeStruct((s_len, 512), BF16), jax.ShapeDtypeStruct((s_len, 512), BF16),
                   jax.ShapeDtypeStruct((s_len, 512), F32),
                   jax.ShapeDtypeStruct((D, D), F32), jax.ShapeDtypeStruct((1, D), F32), jax.ShapeDtypeStruct((1, 512), F32),
                   jax.ShapeDtypeStruct((4, 128, 128), F32), jax.ShapeDtypeStruct((1, 1), F32)],
        compiler_params=_params(("arbitrary",)),
    )(attn, u, u, u, u, u, x, target, modsel, w_pool, pool_scale, w_out)


def _attn_bwd(q, k, v, dattn, attn, lse, s_len):
    t_all = q.shape[1]
    off = (t_all - s_len) // TB
    nq = s_len // TB
    nch = 4
    chunks = [(c * (t_all // nch), t_all // nch) for c in range(nch)]
    nsub = next(n for n in (BWD_QBLOCKS, 2, 1) if nq % n == 0)
    tq = nsub * TB

    def body(*refs):
        q_refs = refs[:nsub]
        k_ref, v_ref, do_ref, o_ref, lse_ref, dq_ref, dk_ref, dv_ref = refs[nsub:]
        i = pl.program_id(1)

        @pl.when(i == 0)
        def _():
            dk_ref[...] = jnp.zeros_like(dk_ref)
            dv_ref[...] = jnp.zeros_like(dv_ref)

        qb = jnp.concatenate([r[0] for r in q_refs], axis=0)
        delta_r = _row_layout(_rowsum(do_ref[...] * o_ref[...]))[0:1, :]
        do = do_ref[...].astype(BF16)
        lse_r = jnp.concatenate([lse_ref[0, sb][0:1, :] for sb in range(nsub)], axis=1)
        dq = jnp.zeros((tq, DKP), F32)
        for start, size in chunks:
            rows = pl.ds(start, size)
            kc = k_ref[0, rows, :]
            p_t = jnp.exp2(_nt(kc, qb) - lse_r)
            ds_t = (p_t * (_nt(v_ref[0, rows, :], do) - delta_r)).astype(BF16)
            dv_ref[0, rows, :] += _nn(p_t.astype(BF16), do)
            dk_ref[0, rows, :] += _nn(ds_t, qb)
            dq += _tn(ds_t, kc)
        dq_ref[0] = dq * SCALE

    kvspec = lambda w: pl.BlockSpec((1, t_all, w), lambda h, i: (h, 0, 0))
    rowspec = pl.BlockSpec((1, nsub, 8, TB), lambda h, i: (h, i, 0, 0))
    qspec = lambda sb: pl.BlockSpec((1, TB, DKP), lambda h, i: (h, i * nsub + sb + off, 0))
    return pl.pallas_call(
        body, name="attn_bwd", grid=(NH, nq // nsub),
        in_specs=[qspec(sb) for sb in range(nsub)]
        + [kvspec(DKP), kvspec(DV), pl.BlockSpec((tq, DV), lambda h, i: (i, h)), pl.BlockSpec((tq, DV), lambda h, i: (i, h)),
           rowspec],
        out_specs=[pl.BlockSpec((1, tq, DKP), lambda h, i: (h, i, 0)), kvspec(DKP), kvspec(DV)],
        out_shape=[jax.ShapeDtypeStruct((NH, s_len, DKP), F32), jax.ShapeDtypeStruct((NH, t_all, DKP), F32),
                   jax.ShapeDtypeStruct((NH, t_all, DV), F32)],
        compiler_params=_params(("arbitrary", "arbitrary")),
    )(*([q] * nsub), k, v, dattn, attn, lse)


def _qkv_bwd(u, dq, dk, dv, cos, sin, q_lora_g, w_uq_t, kv_lora_g, w_ukv, qn_g, kn_g, s_len):
    t_all = u.shape[0]
    off = (t_all - s_len) // TB
    nb = t_all // TB

    def body(ulo_ref, dq_ref, dk_ref, dv_ref, cos_ref, sin_ref, qlg_ref, wuq_ref, kvlg_ref, wukv_ref, qng_ref, kng_ref,
             dlo_ref, dwuq_ref, dwukv_ref, dqlg_ref, dkvlg_ref, dqng_ref, dkng_ref, pad_ref):
        i = pl.program_id(0)
        qng, kng = _head_gains(qng_ref, kng_ref, pad_ref)

        @pl.when(i == 0)
        def _():
            for r in (dwuq_ref, dwukv_ref, dqlg_ref, dkvlg_ref, dqng_ref, dkng_ref):
                r[...] = jnp.zeros_like(r)

        latent = i >= off
        ulo = ulo_ref[...]
        cos, sin = _rope_block(cos_ref, sin_ref, pl.program_id(0) < off)
        cq = ulo[:, 0:QL]
        rc = lax.rsqrt(jnp.mean(cq * cq, axis=-1, keepdims=True) + EPS)
        cqh = cq * rc
        qlg = qlg_ref[...]
        cqn_b = (cqh * qlg).astype(BF16)
        ckv = ulo[:, QL:QL + KVL]
        r0 = lax.rsqrt(jnp.mean(ckv * ckv, axis=-1, keepdims=True) + EPS)
        ckvh = ckv * r0
        kvlg = kvlg_ref[...]
        ckvn_b = (ckvh * kvlg).astype(BF16)
        qhs = [_nt(cqn_b, wuq_ref[hd]) for hd in range(NH)]
        kns = [_nn(ckvn_b, wukv_ref[hd])[:, :128] for hd in range(NH)]
        dqng = jnp.zeros((1, DKP), F32)
        dqraws = []
        for hd in range(NH):
            qh = qhs[hd]
            rq = lax.rsqrt(_rowsum(qh * qh) / DK + EPS)
            xh = qh * rq
            dqh = jnp.where(latent, dq_ref[hd], 0.0)
            dyq = jnp.concatenate([dqh[:, :128], _rope_t(dqh[:, 128:], cos, sin)], axis=1)
            dqng += _colsum(dyq * xh)
            dxh = dyq * qng
            dqraws.append((rq * (dxh - xh * (_rowsum(dxh * xh) / DK))).astype(BF16))
        dqng_ref[...] += dqng

        kr = ulo[:, 384:512]
        skr = _rowsum(kr * kr)
        dkr =jnp.zeros((TB, 128), F32)
        dkng = jnp.zeros((1, DKP), F32)
        dkvs = []
        for hd in range(NH):
            kn = kns[hd]
            rk = lax.rsqrt((_rowsum(kn * kn) + skr) / DK + EPS)
            xh1 = kn * rk
            xh2 = kr * rk
            dkh = dk_ref[hd] * LN2
            d1 = dkh[:, :128]
            d2 = _rope_t(dkh[:, 128:], cos, sin)
            dkng += jnp.concatenate([_colsum(d1 * xh1), _colsum(d2 * xh2)], axis=1)
            dx1 = d1 * kng[:, :128]
            dx2 = d2 * kng[:, 128:]
            dot = (_rowsum(dx1 * xh1) + _rowsum(dx2 * xh2)) / DK
            dkvs.append(jnp.concatenate([rk * (dx1 - xh1 * dot), dv_ref[hd]], axis=1).astype(BF16))
            dkr += rk * (dx2 - xh2 * dot)
        dkng_ref[...] += dkng

        dcqn = jnp.zeros((TB, QL), F32)
        dckvn = jnp.zeros((TB, KVL), F32)
        for hd in range(NH):
            dwuq_ref[hd] += _tn(dqraws[hd], cqn_b)[:DK]
            dcqn += _nn(dqraws[hd], wuq_ref[hd])
            dwukv_ref[hd] += _tn(ckvn_b, dkvs[hd])
            dckvn += _nt(dkvs[hd], wukv_ref[hd])
        dqlg_ref[...] += _colsum(dcqn * cqh)
        dxh = dcqn * qlg
        dcq = rc * (dxh - cqh * jnp.mean(dxh * cqh, axis=-1, keepdims=True))
        dkvlg_ref[...] += _colsum(dckvn * ckvh)
        dxh = dckvn * kvlg
        dckv = r0 * (dxh - ckvh * jnp.mean(dxh * ckvh, axis=-1, keepdims=True))
        dlo_ref[...] = jnp.concatenate([dcq, dckv, dkr], axis=1).astype(BF16)

    row = lambda w: pl.BlockSpec((TB, w), lambda i: (i, 0))
    heads = lambda w: pl.BlockSpec((NH, TB, w), lambda i: (0, i, 0))
    return pl.pallas_call(
        body, name="qkv_bwd", grid=(nb,),
        in_specs=[row(512), pl.BlockSpec((NH, TB, DKP), lambda i: (0, jnp.maximum(i - off, 0), 0)), heads(DKP), heads(DV),
                  pl.BlockSpec((1, 8, 256), lambda i: (jnp.maximum(i - off, 0), 0, 0)), _full((TB, 256)), _full((1, QL)), _full((NH, DKP, QL)), _full((1, KVL)), _full((NH, KVL, 256)),
                  _full((1, DK)), _full((1, DK))],
        out_specs=[row(512), _full((NH, DK, QL)), _full((NH, KVL, 256)), _full((1, QL)), _full((1, KVL)),
                   _full((1, DKP)), _full((1, DKP))],
        out_shape=[jax.ShapeDtypeStruct((t_all, 512), BF16), jax.ShapeDtypeStruct((NH, DK, QL), F32),
                   jax.ShapeDtypeStruct((NH, KVL, 256), F32), jax.ShapeDtypeStruct((1, QL), F32),
                   jax.ShapeDtypeStruct((1, KVL), F32), jax.ShapeDtypeStruct((1, DKP), F32), jax.ShapeDtypeStruct((1, DKP), F32)],
        scratch_shapes=[pltpu.VMEM((2, DKP), F32)],
        compiler_params=_params(("arbitrary",)),
    )(u, dq, dk, dv, cos, sin, q_lora_g, w_uq_t, kv_lora_g, w_ukv, qn_g, kn_g)


def _in_bwd(ctx, x, modsel, norm_g, dlo, dga, dgp, dpool, dxn, w_in_t):
    s_len, lc = x.shape[0], ctx.shape[0]
    t_all = s_len + lc
    off = lc // TB
    nb = t_all // TB
    nq = s_len // TB
    hb = TB // HALO
    n = TB + 2 * HALO

    def body(ctx_ref, x_ref, mod_ref, ng_ref, dlo_ref, dga_ref, dgp_ref, dp_ref, dpprev_ref, dpnext_ref, dxn_ref, win_ref,
             gx_ref, dwin_ref, dmod_ref, dng_ref):
        i = pl.program_id(0)
        j = i - off

        @pl.when(i == 0)
        def _():
            dwin_ref[...] = jnp.zeros_like(dwin_ref)
            dmod_ref[...] = jnp.zeros_like(dmod_ref)
            dng_ref[...] = jnp.zeros_like(dng_ref)

        latent = i >= off
        dp = dp_ref[...]
        prev = jnp.where(j <= 0, 0.0, dpprev_ref[...])
        nxt = jnp.where(j >= nq - 1, 0.0, dpnext_ref[...])
        win = jnp.concatenate([prev, dp, nxt], axis=0)
        tg = j * TB - HALO + lax.broadcasted_iota(jnp.int32, (n, 1), 0)
        dpin = []
        for g, w in enumerate(POOL_WINDOWS):
            cnt = jnp.maximum(jnp.minimum(tg + w // 2, s_len) - jnp.maximum(tg - w // 2, 0), 1).astype(F32)
            zq = win[:, g * 128:(g + 1) * 128] / cnt
            zq = zq + _shift_rows(zq, 1)
            for step in (1, 2, 4):
                if w >= 4 * step:
                    zq = _shift_rows(zq, -step) + _shift_rows(zq, step)
            dpin.append(zq[HALO:HALO + TB] - dp[:, g * 128:(g + 1) * 128])
        zero = jnp.zeros((TB, 512), BF16)
        du = [dlo_ref[...], jnp.where(latent, dga_ref[...], zero),
              jnp.where(latent, jnp.concatenate(dpin, axis=1).astype(BF16), zero), jnp.where(latent, dgp_ref[...], zero)]

        ng = ng_ref[...]
        xb = jnp.where(i < off, ctx_ref[...], x_ref[...])
        r, xh, xg, h, scale = _modulated(xb, mod_ref, ng)
        hb_ = h.astype(BF16)
        dh = jnp.zeros((TB, D), F32)
        for s, (lo, hi) in enumerate(SEG):
            dwin_ref[lo:hi, :] += _tn(du[s], hb_)
            dh += _nn(du[s], win_ref[lo:hi, :])
        is_lat = latent.astype(F32)
        dsh = _colsum(dh)
        dsc = _colsum(dh * xg)
        dmod_ref[0, 0:1, :] += dsh * (1.0 - is_lat)
        dmod_ref[0, 1:2, :] += dsc * (1.0 - is_lat)
        dmod_ref[1, 0:1, :] += dsh * is_lat
        dmod_ref[1, 1:2, :] += dsc * is_lat
        dxg = dh * (1.0 + scale)
        dng_ref[...] += _colsum(dxg * xh)
        dxh = dxg * ng
        gx_ref[...] = r * (dxh - xh * jnp.mean(dxh * xh, axis=-1, keepdims=True)) + dxn_ref[...]

    row = lambda w: pl.BlockSpec((TB, w), lambda i: (i, 0))
    lat = lambda w: pl.BlockSpec((TB, w), lambda i: (jnp.maximum(i - off, 0), 0))
    last8 = s_len // HALO - 1
    cspec, xspec, mspec = _token_specs(off)
    return pl.pallas_call(
        body, name="in_bwd", grid=(nb,),
        in_specs=[cspec, xspec, mspec, _full((1, D)), row(512), lat(512), lat(512), lat(512),
                  pl.BlockSpec((HALO, 512), lambda i: (jnp.maximum(jnp.maximum(i - off, 0) * hb - 1, 0), 0)),
                  pl.BlockSpec((HALO, 512), lambda i: (jnp.minimum((jnp.maximum(i - off, 0) + 1) * hb, last8), 0)),
                  lat(D), _full((DIN, D))],
        out_specs=[lat(D), _full((DIN, D)), _full((2, 2, D)), _full((1, D))],
        out_shape=[jax.ShapeDtypeStruct((s_len, D), F32), jax.ShapeDtypeStruct((DIN, D), F32),
                   jax.ShapeDtypeStruct((2, 2, D), F32), jax.ShapeDtypeStruct((1, D), F32)],
        compiler_params=_params(("arbitrary",)),
    )(ctx, x, modsel, norm_g, dlo, dga, dgp, dpool, dpool, dpool, dxn, w_in_t)


def _adamw_update(w_ref, g_ref, m_ref, v_ref, d_ref, mo_ref, vo_ref):
    gv = g_ref[...]
    mn = ADAM_B1 * m_ref[...] + (1.0 - ADAM_B1) * gv
    vn = ADAM_B2 * v_ref[...] + (1.0 - ADAM_B2) * (gv * gv)
    m_hat = mn / (1.0 - ADAM_B1 ** ADAM_STEP)
    v_hat = vn / (1.0 - ADAM_B2 ** ADAM_STEP)
    d_ref[...] = -ADAM_LR * (m_hat / (jnp.sqrt(v_hat) + ADAM_EPS) + ADAM_WD * w_ref[...])
    mo_ref[...] = mn
    vo_ref[...] = vn


def _adamw_many(ws, gs, ms, vs):
    n = len(ws)
    parts = 4

    def body(*refs):
        for i in range(n):
            _adamw_update(refs[i], refs[n + i], refs[2 * n + i], refs[3 * n + i], refs[4 * n + i], refs[5 * n + i], refs[6 * n + i])
            refs[7 * n + i][...] = refs[n + i][...]

    def spec(w):
        rows, cols = w.shape
        if rows % (8 * parts) == 0:
            return pl.BlockSpec((rows // parts, cols), lambda i: (i, 0))
        if cols % (128 * parts) == 0:
            return pl.BlockSpec((rows, cols // parts), lambda i: (0, i))
        return _full((rows, cols))

    specs = [spec(w) for w in ws]
    shp = [jax.ShapeDtypeStruct(w.shape, F32) for w in ws]
    out = pl.pallas_call(body, name="adamw_many", grid=(parts,), in_specs=specs * 4, out_specs=specs * 4, out_shape=shp * 4,
                         compiler_params=_params(("arbitrary",)))(*ws, *gs, *ms, *vs)
    return out[:n], out[n:2 * n], out[2 * n:3 * n], out[3 * n:]


LOCAL_DMA = 1


class _Links:
    def __init__(self, send_sems, recv_sems):
        self.send_sems, self.recv_sems, self.sends = send_sems, recv_sems, []

    def send(self, src, dst, sem, to):
        cp = pltpu.make_async_remote_copy(src, dst, self.send_sems.at[sem], self.recv_sems.at[sem], device_id=to,
                                          device_id_type=MESH)
        cp.start()
        self.sends.append(cp)

    def arrived(self, dst, sem, frm):
        pltpu.make_async_remote_copy(dst, dst, self.send_sems.at[sem], self.recv_sems.at[sem], device_id=frm,
                                     device_id_type=MESH).wait_recv()

    def drain(self):
        for cp in self.sends:
            cp.wait_send()


def _half(ref, c, axis):
    size = ref.shape[axis - 2] // 2
    win = pl.ds(pl.multiple_of(c * size, 16 if axis == 0 else 128), size)
    idx = (win, slice(None)) if axis == 0 else (slice(None), win)
    return ref.at[(slice(None),) * (len(ref.shape) - 2) + idx]


def _select_rows(slots_ref, n_slots, row=0):
    sub = lax.broadcasted_iota(jnp.int32, (8, 1), 0)
    out = None
    for d in range(n_slots):
        r = jnp.where(sub == d, jnp.broadcast_to(slots_ref[d][row:row + 1, :], (8, slots_ref.shape[-1])), 0.0)
        out = r if out is None else out + r
    return out


def _gather(c, c_ctx, w_mod, b_mod, shards, axes, slab_rows):
    nw = len(shards)
    kw = w_mod.shape[1]

    def body(*refs):
        c_ref, cc_ref, wm_hbm, b_ref = refs[:4]
        w_hbm = refs[4:4 + nw]
        a16_ref, modsel_ref = refs[4 + nw:6 + nw]
        out_refs = refs[6 + nw:6 + 2 * nw]
        g_refs = refs[6 + 2 * nw:6 + 3 * nw]
        f_refs = refs[6 + 3 * nw:6 + 4 * nw]
        wm_ref, a_ref, mod_ref, send_sems, recv_sems, local_sems = refs[6 + 4 * nw:]
        loads = [pltpu.make_async_copy(w_hbm[wi], f_refs[wi], local_sems.at[wi]) for wi in range(nw)]
        loads.append(pltpu.make_async_copy(wm_hbm, wm_ref, local_sems.at[nw]))
        for cp in loads:
            cp.start(priority=LOCAL_DMA)
        stores = []

        def slab(wi, chip, of=g_refs):
            return of[wi].at[chip].at[0:shards[wi].shape[0]]

        def store(src, dst, wi, slot):
            cp = pltpu.make_async_copy(src, dst, local_sems.at[nw + 1 + wi * 8 + slot])
            cp.start(priority=LOCAL_DMA)
            stores.append(cp)

        def store_half(wi, chip, half, slot):
            store(_half(slab(wi, chip), half, axes[wi]), _half(slab(wi, chip, out_refs), half, axes[wi]), wi, slot)
        x, y, cc = lax.axis_index("x"), lax.axis_index("y"), lax.axis_index("c")
        me = 4 * x + 2 * y + cc
        k = 2 * x + y
        sibling = (x, y, 1 - cc)
        links = _Links(send_sems, recv_sems)
        chips = [_peer(x, y, cc, off + (0,)) for off in CHIPS3]
        chip_a = ((x + 1 - cc) % 2, (y + cc) % 2, cc)
        chip_b = ((x + cc) % 2, (y + 1 - cc) % 2, cc)
        chip_d = (1 - x, 1 - y, cc)
        cv = c_ref[...]
        sc = cv * _sig(cv)
        mine = a_ref.at[me]
        for r in range(8):
            mine[r:r + 1, :] = sc[:, r * 128:(r + 1) * 128]
        for j, off in enumerate(PEERS7):
            links.send(a_ref.at[me], a_ref.at[me], j, _peer(x, y, cc, off))
        for wi in range(nw):
            loads[wi].wait()
            slab(wi, k)[...] = f_refs[wi][...].astype(BF16)
            for j, to in enumerate((chip_a, chip_b)):
                links.send(_half(slab(wi, k), cc, axes[wi]), _half(slab(wi, k), cc, axes[wi]), 10 + wi * 6 + j, to)
            store(slab(wi, k), slab(wi, k, out_refs), wi, 0)
            rows = shards[wi].shape[0]
            pad = slab_rows[wi] - rows
            if pad:
                for kk in range(4):
                    g_refs[wi][kk, rows:, :] = jnp.zeros((pad, shards[wi].shape[1]), BF16)
                store(g_refs[wi].at[:, pl.ds(rows, pad), :], out_refs[wi].at[:, pl.ds(rows, pad), :], wi, 7)
        for j, off in enumerate(PEERS7):
            px, py, pc = _peer(x, y, cc, off)
            links.arrived(a_ref.at[4 * px + 2 * py + pc], j, (px, py, pc))
        ccv = cc_ref[...]
        sub = lax.broadcasted_iota(jnp.int32, (8, 1), 0)
        top = jnp.zeros((8, D), F32)
        for d in range(8):
            blk = a_ref[d]
            row = jnp.concatenate([blk[r:r + 1, :] for r in range(8)], axis=1)
            top = top + jnp.where(sub == d, jnp.broadcast_to(row, (8, D)), 0.0)
        a16 = jnp.concatenate([top, jnp.where(sub == 0, jnp.broadcast_to(ccv * _sig(ccv), (8, D)), 0.0)], axis=0)
        a16_ref[...] = a16
        loads[nw].wait()
        b_k = jnp.zeros((1, kw), F32)
        for kk in range(4):
            b_k = b_k + jnp.where(k == kk, b_ref[:, kk * kw:(kk + 1) * kw], 0.0)
        mod_ref[k] = _dot3(_nn, a16, wm_ref[...]) + b_k
        for j, to in enumerate(chips):
            links.send(mod_ref.at[k], mod_ref.at[k], 7 + j, to)

        def over_ici(j, frm, origin):
            for wi in range(nw):
                blk = _half(slab(wi, 2 * origin[0] + origin[1]), cc, axes[wi])
                links.arrived(blk, 10 + wi * 6 + j, frm)
                if j == 0:
                    links.send(blk, blk, 10 + wi * 6 + 2, chip_b)
                links.send(blk, blk, 10 + wi * 6 + 3 + j, sibling)
                store_half(wi, 2 * origin[0] + origin[1], cc, 1 + j)

        def from_sibling(j, origin):
            for wi in range(nw):
                links.arrived(_half(slab(wi, 2 * origin[0] + origin[1]), 1 - cc, axes[wi]), 10 + wi * 6 + 3 + j, sibling)
                store_half(wi, 2 * origin[0] + origin[1], 1 - cc, 4 + j)
        over_ici(0, chip_a, chip_a)
        over_ici(1, chip_b, chip_b)
        from_sibling(0, chip_b)
        over_ici(2, chip_b, chip_d)
        from_sibling(1, chip_a)
        from_sibling(2, chip_d)
        for j, (px, py, pc) in enumerate(chips):
            links.arrived(mod_ref.at[2 * px + py], 7 + j, (px, py, pc))
        for sel, row in ((0, 8), (1, me)):
            for kk in range(4):
                piece = mod_ref[kk, pl.ds(row, 1), :]
                lo = kk * kw
                while lo < (kk + 1) * kw:
                    r, col = divmod(lo, D)
                    n = min((kk + 1) * kw - lo, D - col)
                    modsel_ref[sel, r:r + 1, col:col + n] = piece[:, lo - kk * kw:lo - kk * kw + n]
                    lo += n
        links.drain()
        for cp in stores:
            cp.wait()

    nsem = 10 + 6 * nw
    gathered = [(4, r, s.shape[1]) for r, s in zip(slab_rows, shards)]
    return pl.pallas_call(
        body, name="gather", in_specs=[VM, VM, ANY, VM] + [ANY] * nw, out_specs=[VM, VM] + [ANY] * nw,
        out_shape=[jax.ShapeDtypeStruct((16, D), F32), jax.ShapeDtypeStruct((2, 3, D), F32)]
        + [jax.ShapeDtypeStruct(g, BF16) for g in gathered],
        scratch_shapes=[pltpu.VMEM(g, BF16) for g in gathered] + [pltpu.VMEM(s.shape, F32) for s in shards]
        + [pltpu.VMEM(w_mod.shape, F32), pltpu.VMEM((8, 8, D // 8), F32), pltpu.VMEM((4, 16, kw), F32),
           pltpu.SemaphoreType.DMA((nsem,)),
           pltpu.SemaphoreType.DMA((nsem,)), pltpu.SemaphoreType.DMA((nw + 1 + 8 * nw,))],
        compiler_params=pltpu.CompilerParams(vmem_limit_bytes=VMEM_LIMIT),
    )(c, c_ctx, w_mod, b_mod, *shards)


SMALL_ROW_WIDTHS = (D, QL, KVL, DKP, DKP, 512, 128)
SMALL_OUT_WIDTHS = (D, QL, KVL, DK, DK, 512, 1)
SMALL_PACK = 384


def _small_pieces():
    pieces = []
    for i, w in enumerate(SMALL_ROW_WIDTHS):
        for c0 in range(0, w, 128):
            j = len(pieces)
            pieces.append((i, c0, j // (SMALL_PACK // 128), j % (SMALL_PACK // 128) * 128))
    assert len(pieces) <= 8 * (SMALL_PACK // 128)
    return pieces


def _reduce(grads, axes, smalls, w_pool_g, dmod, dgate, a16, w_mod, c_ctx):
    nw = len(grads)
    ns = len(smalls)
    kw = w_mod.shape[1]
    halves = []
    for g, ax in zip(grads, axes):
        halves.append((g.shape[1] // 2, g.shape[2]) if ax == 0 else (g.shape[1], g.shape[2] // 2))

    def body(*refs):
        g_refs = refs[:nw]
        small_refs = refs[nw:nw + ns]
        wp_ref, dmod_ref, dgate_ref, a16_ref, wm_hbm, cc_ref = refs[nw + ns:nw + ns + 6]
        o = nw + ns + 6
        r_outs = refs[o:o + nw]
        small_outs = refs[o + nw:o + nw + ns]
        rwp_ref, gw_out, gb_ref, gc_ref = refs[o + nw + ns:o + nw + ns + 4]
        o = o + nw + ns + 4
        own, sib, part, got, rel, r_refs = (refs[o + i * nw:o + (i + 1) * nw] for i in range(6))
        smbuf, wps, wpg, dm_all, pc_all, wm_ref, gw_ref, send_sems, recv_sems, local_sems = refs[o + 6 * nw:]
        wm_load = pltpu.make_async_copy(wm_hbm, wm_ref, local_sems.at[nw])
        wm_load.start(priority=LOCAL_DMA)
        stores = []

        def store(src, dst, sem):
            cp = pltpu.make_async_copy(src, dst, local_sems.at[nw + 1 + sem])
            cp.start(priority=LOCAL_DMA)
            stores.append(cp)
        x, y, cc = lax.axis_index("x"), lax.axis_index("y"), lax.axis_index("c")
        me = 4 * x + 2 * y + cc
        k = 2 * x + y
        sibling = (x, y, 1 - cc)
        links = _Links(send_sems, recv_sems)
        chips = [_peer(x, y, cc, off + (0,)) for off in CHIPS3]
        peers = [_peer(x, y, cc, off) for off in PEERS7]
        chip_a = ((x + 1 - cc) % 2, (y + cc) % 2, cc)
        chip_b = ((x + cc) % 2, (y + 1 - cc) % 2, cc)
        ka, kb, kd = 2 * chip_a[0] + chip_a[1], 2 * chip_b[0] + chip_b[1], 2 * (1 - x) + (1 - y)
        big, sm0, wp0, dm0, pc0 = 0, 5 * nw, 5 * nw + 7, 5 * nw + 14, 5 * nw + 21

        locals_ = []
        for wi in range(nw):
            lc = pltpu.make_async_copy(_half(g_refs[wi], cc, axes[wi]), own[wi], local_sems.at[wi])
            lc.start(priority=LOCAL_DMA)
            locals_.append(lc)
            links.send(_half(g_refs[wi], 1 - cc, axes[wi]), sib[wi], big + wi * 5, sibling)
        slot = smbuf.at[me]
        slot[...] = jnp.zeros((8, SMALL_PACK), F32)
        small_rows = [jnp.broadcast_to(ref[...], (1, w)) for ref, w in zip(small_refs, SMALL_ROW_WIDTHS)]
        for i, c0, row, lane in _small_pieces():
            slot[row:row + 1, lane:lane + 128] = small_rows[i][:, c0:c0 + 128]
        links.send(wp_ref, wps, wp0, sibling)
        dm_mine = dm_all.at[me]
        dm_mine[...] = jnp.zeros((8, kw), F32)
        dm_rows = ((0, (dmod_ref[1, 0:1, :], dmod_ref[1, 1:2, :], dgate_ref[...])), (4, (dmod_ref[0, 0:1, :], dmod_ref[0, 1:2, :])))
        for base, rows in dm_rows:
            for r, v in enumerate(rows):
                lo = r * D
                while lo < (r + 1) * D:
                    kk, col = divmod(lo, kw)
                    n = min((r + 1) * D - lo, kw - col)
                    dm_mine[base + kk:base + kk + 1, col:col + n] = v[:, lo - r * D:lo - r * D + n]
                    lo += n
        for j, peer in enumerate(peers):
            links.send(dm_all.at[me], dm_all.at[me], dm0 + j, peer)
            links.send(smbuf.at[me], smbuf.at[me], sm0 + j, peer)
        links.arrived(wps, wp0, sibling)
        wpg[k] = (wp_ref[...] + wps[...]).astype(BF16)
        for j, to in enumerate(chips):
            links.send(wpg.at[k], wpg.at[k], wp0 + 1 + j, to)
        for wi in range(nw):
            locals_[wi].wait()
            links.arrived(sib[wi], big + wi * 5, sibling)
            for slab_k in (kd, kb):
                part[wi][slab_k] = (own[wi][slab_k] + sib[wi][slab_k]).astype(BF16)
            links.send(part[wi].at[kd], rel[wi], big + wi * 5 + 1, chip_b)
            links.send(part[wi].at[kb], got[wi].at[k], big + wi * 5 + 2, chip_b)
            for slab_k in (ka, k):
                part[wi][slab_k] = (own[wi][slab_k] + sib[wi][slab_k]).astype(BF16)
            got[wi][k] = part[wi][k]
            got[wi][kd] = jnp.zeros(halves[wi], BF16)
        for j, (px, py, pc) in enumerate(peers):
            links.arrived(dm_all.at[4 * px + 2 * py + pc], dm0 + j, (px, py, pc))
        dm_tot = dm_all[0]
        for d in range(1, 8):
            dm_tot = dm_tot + dm_all[d]
        for kk in range(4):
            gb_ref[:, kk * kw:(kk + 1) * kw] = dm_tot[kk:kk + 1, :] + dm_tot[4 + kk:5 + kk, :]
        top = jnp.zeros((8, kw), F32)
        dmc_k = jnp.zeros((1, kw), F32)
        for kk in range(4):
            top = top + jnp.where(k == kk, _select_rows(dm_all, 8, kk), 0.0)
            dmc_k = dmc_k + jnp.where(k == kk, dm_tot[4 + kk:5 + kk, :], 0.0)
        sub = lax.broadcasted_iota(jnp.int32, (8, 1), 0)
        bk = jnp.concatenate([top, jnp.where(sub == 0, jnp.broadcast_to(dmc_k, (8, kw)), 0.0)], axis=0)
        gw_ref[...] = _dot3(_tn, a16_ref[...], bk)
        store(gw_ref, gw_out, 0)
        wm_load.wait()
        pc_all[k] =_dot3(_nt, jnp.broadcast_to(bk[8:9, :], (8, kw)), wm_ref[...])
        for j, to in enumerate(chips):
            links.send(pc_all.at[k], pc_all.at[k], pc0 + j, to)
        for j, (px, py, pc) in enumerate(peers):
            links.arrived(smbuf.at[4 * px + 2 * py + pc], sm0 + j, (px, py, pc))
        tot = smbuf[0]
        for d in range(1, 8):
            tot = tot + smbuf[d]
        for i, c0, row, lane in _small_pieces():
            n = min(128, SMALL_OUT_WIDTHS[i] - c0)
            if n > 0:
                small_outs[i][:, c0:c0 + n] = tot[row:row + 1, lane:lane + n]
        for j, (px, py, pc) in enumerate(chips):
            links.arrived(wpg.at[2 * px + py], wp0 + 1 + j, (px, py, pc))
        wpt = wpg[0].astype(F32)
        for kk in range(1, 4):
            wpt = wpt + wpg[kk].astype(F32)
        rwp_ref[...] = wpt
        for wi in range(nw):
            links.arrived(rel[wi], big + wi * 5 + 1, chip_b)
            rel[wi][...] = (part[wi][ka].astype(F32) + rel[wi][...].astype(F32)).astype(BF16)
            links.send(rel[wi], got[wi].at[k], big + wi * 5 + 3, chip_a)
        for wi in range(nw):
            links.arrived(got[wi].at[kb], big + wi * 5 + 2, chip_b)
            links.arrived(got[wi].at[ka], big + wi * 5 + 3, chip_a)
            total = got[wi][0].astype(F32)
            for kk in range(1, 4):
                total = total + got[wi][kk].astype(F32)
            mine = _half(r_refs[wi], cc, axes[wi])
            mine[...] = total
            links.send(mine, mine, big + wi * 5 + 4, sibling)
            store(mine, _half(r_outs[wi], cc, axes[wi]), 1 + wi)
        for j, (px, py, pc) in enumerate(chips):
            links.arrived(pc_all.at[2 * px + py], pc0 + j, (px, py, pc))
        ccv = cc_ref[...]
        sg = _sig(ccv)
        gc_ref[...] = (pc_all[0][0:1, :] + pc_all[1][0:1, :] + pc_all[2][0:1, :] + pc_all[3][0:1, :]) * (sg * (1.0 + ccv * (1.0 - sg)))
        for wi in range(nw):
            links.arrived(_half(r_refs[wi], 1 - cc, axes[wi]), big + wi * 5 + 4, sibling)
            store(_half(r_refs[wi], 1 - cc, axes[wi]), _half(r_outs[wi], 1 - cc, axes[wi]), 1 + nw + wi)
        links.drain()
        for cp in stores:
            cp.wait()

    nsem = 5 * nw + 24
    quads = [(4,) + h for h in halves]
    return pl.pallas_call(
        body, name="reduce", in_specs=[ANY] * nw + [VM] * (ns + 4) + [ANY, VM],
        out_specs=[ANY] * nw + [VM] * (ns + 1) + [ANY, VM, VM],
        out_shape=[jax.ShapeDtypeStruct(g.shape[1:], F32) for g in grads]
        + [jax.ShapeDtypeStruct((1, w), F32) for w in SMALL_OUT_WIDTHS]
        + [jax.ShapeDtypeStruct(w_pool_g.shape, F32), jax.ShapeDtypeStruct((D, kw), F32), jax.ShapeDtypeStruct((1, 3 * D), F32),
           jax.ShapeDtypeStruct((1, D), F32)],
        scratch_shapes=[pltpu.VMEM(q, F32) for q in quads] + [pltpu.VMEM(q, F32) for q in quads]
        + [pltpu.VMEM(q, BF16) for q in quads] + [pltpu.VMEM(q, BF16) for q in quads] + [pltpu.VMEM(h, BF16) for h in halves]
        + [pltpu.VMEM(g.shape[1:], F32) for g in grads]
        + [pltpu.VMEM((8, 8, SMALL_PACK), F32), pltpu.VMEM(w_pool_g.shape, F32), pltpu.VMEM((4,) + w_pool_g.shape, BF16),
           pltpu.VMEM((8, 8, kw), F32), pltpu.VMEM((4, 8, D), F32), pltpu.VMEM((D, kw), F32), pltpu.VMEM((D, kw), F32)]
        + [pltpu.SemaphoreType.DMA((nsem,)), pltpu.SemaphoreType.DMA((nsem,)), pltpu.SemaphoreType.DMA((3 * nw + 2,))],
        compiler_params=pltpu.CompilerParams(vmem_limit_bytes=VMEM_LIMIT),
    )(*grads, *smalls, w_pool_g, dmod, dgate, a16, w_mod, c_ctx)


def _rope_tables(s_len):
    rows = s_len // GRID_W
    per = TB // GRID_W
    n_freq = 16
    f32 = np.float32
    inv = f32(ROPE_BASE) ** (-np.arange(n_freq, dtype=f32) / f32(n_freq))
    ang_r = np.arange(rows, dtype=f32)[:, None] * inv
    ang_c = np.arange(GRID_W, dtype=f32)[:, None] * inv
    by_row, by_col = [], []
    for fn, pad in ((np.cos, 1.0), (np.sin, 0.0)):
        r = np.concatenate([fn(ang_r), fn(ang_r), np.zeros((rows, 96), f32)], axis=1).reshape(rows // per, per, 128)
        by_row.append(np.pad(r, ((0, 0), (0, 8 - per), (0, 0))))
        cpart = np.concatenate([np.zeros((GRID_W, 32), f32), fn(ang_c), fn(ang_c), np.full((GRID_W, 64), pad, f32)], axis=1)
        by_col.append(np.tile(cpart, (per, 1)))
    return jnp.asarray(np.concatenate(by_row, axis=-1), F32), jnp.asarray(np.concatenate(by_col, axis=-1), F32)


def kernel(x, c, ctx, c_ctx, w_mod, b_mod, norm_g, w_in, q_lora_g, w_uq, kv_lora_g, w_ukv, q_norm_g, k_norm_g, w_pool, pool_scale, w_out, loss_target, m_c_ctx, m_w_mod, m_b_mod, m_norm_g, m_w_in, m_q_lora_g, m_w_uq, m_kv_lora_g, m_w_ukv, m_q_norm_g, m_k_norm_g, m_w_pool, m_pool_scale, m_w_out, v_c_ctx, v_w_mod, v_b_mod, v_norm_g, v_w_in, v_q_lora_g, v_w_uq, v_kv_lora_g, v_w_ukv, v_q_norm_g, v_k_norm_g, v_w_pool, v_pool_scale, v_w_out):
    s_len = x.shape[1]
    lc = ctx.shape[1]
    weights = dict(c_ctx=c_ctx, w_mod=w_mod, b_mod=b_mod, norm_g=norm_g, w_in=w_in, q_lora_g=q_lora_g, w_uq=w_uq,
                   kv_lora_g=kv_lora_g, w_ukv=w_ukv, q_norm_g=q_norm_g, k_norm_g=k_norm_g, w_pool=w_pool,
                   pool_scale=pool_scale, w_out=w_out)
    m_in = dict(c_ctx=m_c_ctx, w_mod=m_w_mod, b_mod=m_b_mod, norm_g=m_norm_g, w_in=m_w_in, q_lora_g=m_q_lora_g, w_uq=m_w_uq,
                kv_lora_g=m_kv_lora_g, w_ukv=m_w_ukv, q_norm_g=m_q_norm_g, k_norm_g=m_k_norm_g, w_pool=m_w_pool,
                pool_scale=m_pool_scale, w_out=m_w_out)
    v_in = dict(c_ctx=v_c_ctx, w_mod=v_w_mod, b_mod=v_b_mod, norm_g=v_norm_g, w_in=v_w_in, q_lora_g=v_q_lora_g, w_uq=v_w_uq,
                kv_lora_g=v_kv_lora_g, w_ukv=v_w_ukv, q_norm_g=v_q_norm_g, k_norm_g=v_k_norm_g, w_pool=v_w_pool,
                pool_scale=v_pool_scale, w_out=v_w_out)
    order = ["c_ctx", "w_mod", "b_mod", "norm_g", "w_in", "q_lora_g", "w_uq", "kv_lora_g", "w_ukv", "q_norm_g", "k_norm_g",
             "w_pool", "pool_scale", "w_out"]
    transposed = ("w_in", "w_uq")
    as2d = lambda n, a: jnp.transpose(a[0]) if n in transposed else a.reshape(-1, a.shape[-1])
    back = lambda n, a: jnp.transpose(a)[None] if n in transposed else a.reshape(weights[n].shape)

    c_ctx2 = c_ctx.reshape(1, D)
    split = (0, 1, 0, 0)
    a16, modsel, g_out, g_in, g_uq, g_ukv = _gather(
        c, c_ctx2, w_mod[0], b_mod, [w_out[0], as2d("w_in", w_in), as2d("w_uq", w_uq), w_ukv[0]], split,
        (D // 4, DIN // 4, DKP, KVL))
    w_in_t = g_in.reshape(DIN, D)
    w_uq_t = g_uq
    w_out_f = g_out.reshape(D, D)
    cos, sin = _rope_tables(s_len)

    u, q, kk, v = _fwd_in(ctx[0], x[0], modsel, norm_g, w_in_t, q_lora_g, w_uq_t, kv_lora_g, g_ukv, q_norm_g, k_norm_g, cos, sin)
    attn, lse = _attn_fwd(q, kk, v, s_len)
    (dxn, dattn, dga, dgp, dpool, dw_out, dgate, dps, dw_pool, loss) = _out_stage(
        attn.reshape(s_len // Q_BLOCK, Q_BLOCK, NH * DV), u, x[0], loss_target[0], modsel, w_pool[0], pool_scale,
        w_out_f, lc)
    dattn = dattn.reshape(s_len, NH * DV)
    dq, dk, dv = _attn_bwd(q, kk, v, dattn, attn, lse, s_len)
    dlo, dw_uq_t, dw_ukv, dqlg, dkvlg, dqng, dkng = _qkv_bwd(u, dq, dk, dv, cos, sin, q_lora_g, w_uq_t, kv_lora_g, g_ukv,
                                                            q_norm_g, k_norm_g, s_len)
    gx, dw_in_t, dmod, dng = _in_bwd(ctx[0], x[0], modsel, norm_g, dlo, dga, dgp, dpool, dxn, w_in_t)

    r_out, r_in, r_uq, r_ukv, g_ng, g_qlg, g_kvlg, g_qng, g_kng, g_ps, loss_all, g_wp, g_w_mod, g_b_mod, g_c_ctx = _reduce(
        [dw_out.reshape(4, D // 4, D), dw_in_t.reshape(4, DIN // 4, D), dw_uq_t, dw_ukv], split,
        [dng, dqlg, dkvlg, dqng, dkng, dps, loss], dw_pool, dmod, dgate, a16, w_mod[0], c_ctx2)
    g2d = dict(c_ctx=g_c_ctx, b_mod=g_b_mod, w_mod=g_w_mod, w_in=r_in, w_uq=r_uq, w_ukv=r_ukv, w_out=r_out, norm_g=g_ng,
               q_lora_g=g_qlg, kv_lora_g=g_kvlg, q_norm_g=g_qng, k_norm_g=g_kng, pool_scale=g_ps, w_pool=g_wp.reshape(512, 128))

    outs = _adamw_many([as2d(n, weights[n]) for n in order], [g2d[n] for n in order], [as2d(n, m_in[n]) for n in order],
                       [as2d(n, v_in[n]) for n in order])
    d2d, m2d, v2d, g2d = (dict(zip(order, arrs)) for arrs in outs)

    return (loss_all[0, 0], gx[None], *[back(n, g2d[n]) for n in order], *[back(n, d2d[n]) for n in order],
            *[back(n, m2d[n]) for n in order], *[back(n, v2d[n]) for n in order])
```

```python
import jax
import jax.numpy as jnp
import numpy as np
from jax import lax
from jax.experimental import pallas as pl
from jax.experimental.pallas import tpu as pltpu

F32 = jnp.float32
BF16 = jnp.bfloat16
MESH = pl.DeviceIdType.MESH

D = 1024
NH = 4
DK = 192
DKP = 256
DV = 128
QL = 256
KVL = 128
DIN = 1984
U_LO = 448
SEG = ((0, 512), (448, 960), (960, 1472), (1472, 1984))
DU = 2048
POOL_WINDOWS = (2, 4, 8, 16)
HALO = 8
EPS = 1e-6
ROPE_BASE = 10000.0
GRID_W = 64
Q_BLOCK = 128
TB = 256
BWD_QBLOCKS = 1
SCALE = DK ** -0.5
LOG2E = 1.4426950408889634
LN2 = 0.6931471805599453
VMEM_LIMIT = 56 * 1024 * 1024

ADAM_LR = 0.001
ADAM_B1 = 0.9
ADAM_B2 = 0.999
ADAM_EPS = 1e-08
ADAM_WD = 0.01
ADAM_STEP = 10

CHIPS3 = ((1, 0), (0, 1), (1, 1))
PEERS7 = tuple((dx, dy, dc) for dx in (0, 1) for dy in (0, 1) for dc in (0, 1) if (dx, dy, dc) != (0, 0, 0))

VM = pl.BlockSpec(memory_space=pltpu.VMEM)
ANY = pl.BlockSpec(memory_space=pl.ANY)


def _nn(a, b):
    return jnp.dot(a, b, preferred_element_type=F32)


def _nt(a, b):
    return lax.dot_general(a, b, (((1,), (1,)), ((), ())), preferred_element_type=F32)


def _tn(a, b):
    return lax.dot_general(a, b, (((0,), (0,)), ((), ())), preferred_element_type=F32)


def _split3(a):
    a0 = a.astype(BF16)
    r = a - a0.astype(F32)
    a1 = r.astype(BF16)
    a2 = (r - a1.astype(F32)).astype(BF16)
    return a0, a1, a2


def _dot3(dot, a, b):
    sa = _split3(a)
    sb = _split3(b)
    out = None
    for i in range(3):
        for j in range(3 - i):
            t = dot(sa[i], sb[j])
            out = t if out is None else out + t
    return out


def _sig(x):
    return 1.0 / (1.0 + jnp.exp(-x))


def _rot(t):
    src = lax.broadcasted_iota(jnp.int32, (128, 128), 0)
    dst = lax.broadcasted_iota(jnp.int32, (128, 128), 1)
    first = (dst % 32) < 16
    perm = jnp.where(first & (src == dst + 16), -1.0, jnp.where(~first & (src == dst - 16), 1.0, 0.0)).astype(BF16)
    hi = t.astype(BF16)
    lo = (t - hi.astype(F32)).astype(BF16)
    return _nn(hi, perm) + _nn(lo, perm)


def _rope(t, cos, sin):
    return t * cos + _rot(t) * sin


def _rope_t(t, cos, sin):
    return t * cos - _rot(t * sin)


def _rope_block(rows_ref, cols_ref, is_ctx):
    lane = lax.broadcasted_iota(jnp.int32, (TB, 256), 1) % 128
    rows = jnp.concatenate([jnp.broadcast_to(rows_ref[0, r:r + 1, :], (GRID_W, 256)) for r in range(TB // GRID_W)], axis=0)
    cs = jnp.where(lane < 32, rows, cols_ref[...])
    return jnp.where(is_ctx, 1.0, cs[:, :128]), jnp.where(is_ctx, 0.0, cs[:, 128:])


def _shift_rows(z, k):
    n = z.shape[0]
    return pltpu.roll(z, (n - k) % n, 0)


def _colsum(a):
    return jnp.sum(a, axis=0, keepdims=True)


def _rowsum(a):
    return jnp.sum(a, axis=-1, keepdims=True)


def _row_layout(col):
    return jnp.transpose(jnp.broadcast_to(col, (col.shape[0], 128)))[0:8, :]


def _params(sem=None):
    return pltpu.CompilerParams(dimension_semantics=sem, vmem_limit_bytes=VMEM_LIMIT)


def _full(shape):
    nd = len(shape)
    return pl.BlockSpec(shape, lambda *_: (0,) * nd)


def _peer(x, y, c, off):
    dx, dy, dc = off
    return ((x + dx) % 2, (y + dy) % 2, (c + dc) % 2)


def _token_specs(off):
    ctx = pl.BlockSpec((TB, D), lambda i: (jnp.minimum(i, off - 1), 0))
    lat = pl.BlockSpec((TB, D), lambda i: (jnp.maximum(i - off, 0), 0))
    mod = pl.BlockSpec((1, 3, D), lambda i: (jnp.minimum(i // off, 1), 0, 0))
    return ctx, lat, mod


def _head_gains(qng_ref, kng_ref, pad_ref):
    pad_ref[...] = jnp.zeros((2, DKP), F32)
    pad_ref[0:1, 0:DK] = qng_ref[...]
    pad_ref[1:2, 0:DK] = kng_ref[...]
    return pad_ref[0:1, :], pad_ref[1:2, :]


def _modulated(x, mod_ref, ng):
    shift = mod_ref[0, 0:1, :]
    scale = mod_ref[0, 1:2, :]
    r = lax.rsqrt(jnp.mean(x * x, axis=-1, keepdims=True) + EPS)
    xh = x * r
    xg = xh * ng
    return r, xh, xg, xg * (1.0 + scale) + shift, scale


def _fwd_in(ctx, x, modsel, norm_g, w_in_t, q_lora_g, w_uq_t, kv_lora_g, w_ukv, qn_g, kn_g, cos, sin):
    s_len, lc = x.shape[0], ctx.shape[0]
    t_all = s_len + lc
    nb = t_all // TB
    off = lc // TB

    def body(ctx_ref, x_ref, mod_ref, ng_ref, win_ref, qlg_ref, wuq_ref, kvlg_ref, wukv_ref, qng_ref, kng_ref, cos_ref, sin_ref,
             u_ref, q_ref, k_ref, v_ref, pad_ref):
        is_ctx = pl.program_id(0) < off
        qng, kng = _head_gains(qng_ref, kng_ref, pad_ref)
        xb = jnp.where(is_ctx, ctx_ref[...], x_ref[...])
        _, _, _, h, _ = _modulated(xb, mod_ref, ng_ref[...])
        hb = h.astype(BF16)
        lane = lax.broadcasted_iota(jnp.int32, (TB, 512), 1)
        ulo = jnp.where(lane < U_LO, _nt(hb, win_ref[SEG[0][0]:SEG[0][1], :]), 0.0)
        u_ref[:, 0:512] = ulo
        for j in range(1, 4):
            u_ref[:, j * 512:(j + 1) * 512] = _nt(hb, win_ref[SEG[j][0]:SEG[j][1], :])
        cos, sin = _rope_block(cos_ref, sin_ref, is_ctx)
        cq = ulo[:, 0:QL]
        cqn = (cq * lax.rsqrt(jnp.mean(cq * cq, axis=-1, keepdims=True) + EPS) * qlg_ref[...]).astype(BF16)
        ckv = ulo[:, QL:QL + KVL]
        ckvn = (ckv * lax.rsqrt(jnp.mean(ckv * ckv, axis=-1, keepdims=True) + EPS) * kvlg_ref[...]).astype(BF16)
        qhs = [_nt(cqn, wuq_ref[hd]) for hd in range(NH)]
        kvs = [_nn(ckvn, wukv_ref[hd]) for hd in range(NH)]
        for hd in range(NH):
            qh = qhs[hd]
            qn = qh * lax.rsqrt(_rowsum(qh * qh) / DK + EPS) * qng
            q_ref[hd] = (jnp.concatenate([qn[:, :128], _rope(qn[:, 128:], cos, sin)], axis=1) * (SCALE * LOG2E)).astype(BF16)
        kr = ulo[:, 384:512]
        skr = _rowsum(kr * kr)
        kr_roped =_rope(kr * kng[:, 128:], cos, sin)
        for hd in range(NH):
            kv = kvs[hd]
            kn = kv[:, :128]
            rk = lax.rsqrt((_rowsum(kn * kn) + skr) / DK + EPS)
            k_ref[hd] = jnp.concatenate([kn * rk * kng[:, :128], kr_roped * rk], axis=1).astype(BF16)
            v_ref[hd] = kv[:, 128:].astype(BF16)

    row = lambda w: pl.BlockSpec((TB, w), lambda i: (i, 0))
    heads = lambda w: pl.BlockSpec((NH, TB, w), lambda i: (0, i, 0))
    cspec, xspec, mspec = _token_specs(off)
    return pl.pallas_call(
        body, name="fwd_in", grid=(nb,),
        in_specs=[cspec, xspec, mspec, _full((1, D)), _full((DIN, D)), _full((1, QL)), _full((NH, DKP, QL)), _full((1, KVL)),
                  _full((NH, KVL, 256)), _full((1, DK)), _full((1, DK)),
                  pl.BlockSpec((1, 8, 256), lambda i: (jnp.maximum(i - off, 0), 0, 0)), _full((TB, 256))],
        out_specs=[row(DU), heads(DKP), heads(DKP), heads(DV)],
        out_shape=[jax.ShapeDtypeStruct((t_all, DU), F32), jax.ShapeDtypeStruct((NH, t_all, DKP), BF16),
                   jax.ShapeDtypeStruct((NH, t_all, DKP), BF16), jax.ShapeDtypeStruct((NH, t_all, DV), BF16)],
        scratch_shapes=[pltpu.VMEM((2, DKP), F32)],
        compiler_params=_params(("arbitrary",)),
    )(ctx, x, modsel, norm_g, w_in_t, q_lora_g, w_uq_t, kv_lora_g, w_ukv, qn_g, kn_g, cos, sin)


def _attn_fwd(q, k, v, s_len):
    t_all = q.shape[1]
    off = (t_all - s_len) // TB
    nq = s_len // TB
    nsub = next(n for n in (4, 2, 1) if nq % n == 0)

    def body(*refs):
        q_refs = refs[:nsub]
        k_ref, v_ref, o_ref, lse_ref = refs[nsub:]
        for sb in range(nsub):
            s = _nt(q_refs[sb][0], k_ref[0])
            m = jnp.max(s, axis=-1, keepdims=True)
            e = jnp.exp2(s - m)
            l = _rowsum(e)
            o_ref[sb * TB:(sb + 1) * TB, :] = _nn(e.astype(BF16), v_ref[0]) / l
            lse_ref[0, sb] = _row_layout(m + jnp.log2(l))

    qspec = lambda sb: pl.BlockSpec((1, TB, DKP), lambda h, i: (h, i * nsub + sb + off, 0))
    return pl.pallas_call(
        body, name="attn_fwd", grid=(NH, nq // nsub),
        in_specs=[qspec(sb) for sb in range(nsub)]
        + [pl.BlockSpec((1, t_all, DKP), lambda h, i: (h, 0, 0)), pl.BlockSpec((1, t_all, DV), lambda h, i: (h, 0, 0))],
        out_specs=[pl.BlockSpec((nsub * TB, DV), lambda h, i: (i, h)), pl.BlockSpec((1, nsub, 8, TB), lambda h, i: (h, i, 0, 0))],
        out_shape=[jax.ShapeDtypeStruct((s_len, NH * DV), F32), jax.ShapeDtypeStruct((NH, nq, 8, TB), F32)],
        compiler_params=_params(("arbitrary", "arbitrary")),
    )(*([q] * nsub), k, v)


def _out_stage(attn, u, x, target, modsel, w_pool, pool_scale, w_out, lc):
    s_len = x.shape[0]
    t_all = s_len + lc
    off = lc // TB
    nq = s_len // TB
    hb = TB // HALO
    nqb = s_len // Q_BLOCK
    jb = TB // nqb

    def body(attn_ref, ga_ref, pin_ref, pprev_ref, pnext_ref, gp_ref, x_ref, tgt_ref, gate_ref, wp_ref, ps_ref, wo_ref,
             dxn_ref, dattn_ref, dga_ref, dgp_ref, dpool_ref, dwo_ref, dgate_ref, dps_ref, dwp_ref, loss_ref):
        i = pl.program_id(0)

        @pl.when(i == 0)
        def _():
            dwo_ref[...] = jnp.zeros_like(dwo_ref)
            dgate_ref[...] = jnp.zeros_like(dgate_ref)
            dps_ref[...] = jnp.zeros_like(dps_ref)
            dwp_ref[...] = jnp.zeros_like(dwp_ref)
            loss_ref[...] = jnp.zeros_like(loss_ref)

        attn = jnp.concatenate([attn_ref[:, jj, :] for jj in range(jb)], axis=0)
        ga = ga_ref[...]
        gp = gp_ref[...]
        pin = pin_ref[...]
        prev = jnp.where(i == 0, 0.0, pprev_ref[...])
        nxt = jnp.where(i == nq - 1, 0.0, pnext_ref[...])
        win = jnp.concatenate([prev, pin, nxt], axis=0)
        tg = i * TB + lax.broadcasted_iota(jnp.int32, (TB, 1), 0)
        pooled = []
        for g, w in enumerate(POOL_WINDOWS):
            a = win[:, g * 128:(g + 1) * 128]
            p = _shift_rows(a, -1) + a
            for step in (1, 2, 4):
                if w >= 4 * step:
                    p = _shift_rows(p, -step) + _shift_rows(p, step)
            cnt = (jnp.minimum(tg + w // 2, s_len) - jnp.maximum(tg - w // 2, 0)).astype(F32)
            pooled.append(p[HALO:HALO + TB] / cnt - a[HALO:HALO + TB])
        pooled_b = [p.astype(BF16) for p in pooled]
        wp = [wp_ref[g].astype(BF16) for g in range(4)]
        z = jnp.concatenate([_nn(pooled_b[g], wp[g]) for g in range(4)], axis=1)
        ps = ps_ref[...]
        yp = z * ps
        sga = _sig(ga)
        sila = ga * sga
        sgp = _sig(gp)
        silp = gp * sgp
        br = jnp.concatenate([sila * attn, silp * yp], axis=1).astype(BF16)
        y = _nn(br, wo_ref[...])
        gate = gate_ref[0, 2:3, :]
        err = x_ref[...] + gate * y - tgt_ref[...]
        loss_ref[...] += _colsum(_rowsum(err * err)) * (0.5 / D)
        dxn = err * (1.0 / D)
        dxn_ref[...] = dxn
        dgate_ref[...] += _colsum(dxn * y)
        dy = (dxn * gate).astype(BF16)
        dwo_ref[...] += _tn(br, dy)
        dbr = _nt(dy, wo_ref[...])
        dbra = dbr[:, :512]
        dbrp = dbr[:, 512:]
        dattn = dbra * sila
        for jj in range(jb):
            dattn_ref[:, jj, :] = dattn[jj * nqb:(jj + 1) * nqb]
        dga_ref[...] = (dbra * attn * (sga * (1.0 + ga * (1.0 - sga)))).astype(BF16)
        dgp_ref[...] = (dbrp * yp * (sgp * (1.0 + gp * (1.0 - sgp)))).astype(BF16)
        dyp = dbrp * silp
        dps_ref[...] += _colsum(dyp * z)
        dz = (dyp * ps).astype(BF16)
        dpool = []
        for g in range(4):
            dzg = dz[:, g * 128:(g + 1) * 128]
            dwp_ref[g] += _tn(pooled_b[g], dzg)
            dpool.append(_nt(dzg, wp[g]))
        dpool_ref[...] = jnp.concatenate(dpool, axis=1)

    lat = lambda w: pl.BlockSpec((TB, w), lambda i: (i, 0))
    perm = pl.BlockSpec((nqb, jb, 512), lambda i: (0, i, 0))
    ucol = lambda j: pl.BlockSpec((TB, 512), lambda i: (i + off, j))
    last8 = t_all // HALO - 1
    return pl.pallas_call(
        body, name="out_stage", grid=(nq,),
        in_specs=[perm, ucol(1), ucol(2),
                  pl.BlockSpec((HALO, 512), lambda i: ((i + off) * hb - 1, 2)),
                  pl.BlockSpec((HALO, 512), lambda i: (jnp.minimum((i + off + 1) * hb, last8), 2)),
                  ucol(3), lat(D), lat(D), pl.BlockSpec((1, 3, D), lambda i: (1, 0, 0)), _full((4, 128, 128)), _full((1, 512)),
                  _full((D, D))],
        out_specs=[lat(D), perm, lat(512), lat(512), lat(512),
                   _full((D, D)), _full((1, D)), _full((1, 512)), _full((4, 128, 128)), _full((1, 1))],
        out_shape=[jax.ShapeDtypeStruct((s_len, D), F32), jax.ShapeDtypeStruct((nqb, Q_BLOCK, 512), F32),
                   jax.ShapeDtypeStruct((s_len, 512), BF16), jax.ShapeDtypeStruct((s_len, 512), BF16),
                   jax.ShapeDtypeStruct((s_len, 512), F32),
                   jax.ShapeDtypeStruct((D, D), F32), jax.ShapeDtypeStruct((1, D), F32), jax.ShapeDtypeStruct((1, 512), F32),
                   jax.ShapeDtypeStruct((4, 128, 128), F32), jax.ShapeDtypeStruct((1, 1), F32)],
        compiler_params=_params(("arbitrary",)),
    )(attn, u, u, u, u, u, x, target, modsel, w_pool, pool_scale, w_out)


def _attn_bwd(q, k, v, dattn, attn, lse, s_len):
    t_all = q.shape[1]
    off = (t_all - s_len) // TB
    nq = s_len // TB
    nch = 4
    chunks = [(c * (t_all // nch), t_all // nch) for c in range(nch)]
    nsub = next(n for n in (BWD_QBLOCKS, 2, 1) if nq % n == 0)
    tq = nsub * TB

    def body(*refs):
        q_refs = refs[:nsub]
        k_ref, v_ref, do_ref, o_ref, lse_ref, dq_ref, dk_ref, dv_ref = refs[nsub:]
        i = pl.program_id(1)

        @pl.when(i == 0)
        def _():
            dk_ref[...] = jnp.zeros_like(dk_ref)
            dv_ref[...] = jnp.zeros_like(dv_ref)

        qb = jnp.concatenate([r[0] for r in q_refs], axis=0)
        delta_r = _row_layout(_rowsum(do_ref[...] * o_ref[...]))[0:1, :]
        do = do_ref[...].astype(BF16)
        lse_r = jnp.concatenate([lse_ref[0, sb][0:1, :] for sb in range(nsub)], axis=1)
        dq = jnp.zeros((tq, DKP), F32)
        for start, size in chunks:
            rows = pl.ds(start, size)
            kc = k_ref[0, rows, :]
            p_t = jnp.exp2(_nt(kc, qb) - lse_r)
            ds_t = (p_t * (_nt(v_ref[0, rows, :], do) - delta_r)).astype(BF16)
            dv_ref[0, rows, :] += _nn(p_t.astype(BF16), do)
            dk_ref[0, rows, :] += _nn(ds_t, qb)
            dq += _tn(ds_t, kc)
        dq_ref[0] = dq * SCALE

    kvspec = lambda w: pl.BlockSpec((1, t_all, w), lambda h, i: (h, 0, 0))
    rowspec = pl.BlockSpec((1, nsub, 8, TB), lambda h, i: (h, i, 0, 0))
    qspec = lambda sb: pl.BlockSpec((1, TB, DKP), lambda h, i: (h, i * nsub + sb + off, 0))
    return pl.pallas_call(
        body, name="attn_bwd", grid=(NH, nq // nsub),
        in_specs=[qspec(sb) for sb in range(nsub)]
        + [kvspec(DKP), kvspec(DV), pl.BlockSpec((tq, DV), lambda h, i: (i, h)), pl.BlockSpec((tq, DV), lambda h, i: (i, h)),
           rowspec],
        out_specs=[pl.BlockSpec((1, tq, DKP), lambda h, i: (h, i, 0)), kvspec(DKP), kvspec(DV)],
        out_shape=[jax.ShapeDtypeStruct((NH, s_len, DKP), F32), jax.ShapeDtypeStruct((NH, t_all, DKP), F32),
                   jax.ShapeDtypeStruct((NH, t_all, DV), F32)],
        compiler_params=_params(("arbitrary", "arbitrary")),
    )(*([q] * nsub), k, v, dattn, attn, lse)


def _qkv_bwd(u, dq, dk, dv, cos, sin, q_lora_g, w_uq_t, kv_lora_g, w_ukv, qn_g, kn_g, s_len):
    t_all = u.shape[0]
    off = (t_all - s_len) // TB
    nb = t_all // TB

    def body(ulo_ref, dq_ref, dk_ref, dv_ref, cos_ref, sin_ref, qlg_ref, wuq_ref, kvlg_ref, wukv_ref, qng_ref, kng_ref,
             dlo_ref, dwuq_ref, dwukv_ref, dqlg_ref, dkvlg_ref, dqng_ref, dkng_ref, pad_ref):
        i = pl.program_id(0)
        qng, kng = _head_gains(qng_ref, kng_ref, pad_ref)

        @pl.when(i == 0)
        def _():
            for r in (dwuq_ref, dwukv_ref, dqlg_ref, dkvlg_ref, dqng_ref, dkng_ref):
                r[...] = jnp.zeros_like(r)

        latent = i >= off
        ulo = ulo_ref[...]
        cos, sin = _rope_block(cos_ref, sin_ref, pl.program_id(0) < off)
        cq = ulo[:, 0:QL]
        rc = lax.rsqrt(jnp.mean(cq * cq, axis=-1, keepdims=True) + EPS)
        cqh = cq * rc
        qlg = qlg_ref[...]
        cqn_b = (cqh * qlg).astype(BF16)
        ckv = ulo[:, QL:QL + KVL]
        r0 = lax.rsqrt(jnp.mean(ckv * ckv, axis=-1, keepdims=True) + EPS)
        ckvh = ckv * r0
        kvlg = kvlg_ref[...]
        ckvn_b = (ckvh * kvlg).astype(BF16)
        qhs = [_nt(cqn_b, wuq_ref[hd]) for hd in range(NH)]
        kns = [_nn(ckvn_b, wukv_ref[hd])[:, :128] for hd in range(NH)]
        dqng = jnp.zeros((1, DKP), F32)
        dqraws = []
        for hd in range(NH):
            qh = qhs[hd]
            rq = lax.rsqrt(_rowsum(qh * qh) / DK + EPS)
            xh = qh * rq
            dqh = jnp.where(latent, dq_ref[hd], 0.0)
            dyq = jnp.concatenate([dqh[:, :128], _rope_t(dqh[:, 128:], cos, sin)], axis=1)
            dqng += _colsum(dyq * xh)
            dxh = dyq * qng
            dqraws.append((rq * (dxh - xh * (_rowsum(dxh * xh) / DK))).astype(BF16))
        dqng_ref[...] += dqng

        kr = ulo[:, 384:512]
        skr = _rowsum(kr * kr)
        dkr =jnp.zeros((TB, 128), F32)
        dkng = jnp.zeros((1, DKP), F32)
        dkvs = []
        for hd in range(NH):
            kn = kns[hd]
            rk = lax.rsqrt((_rowsum(kn * kn) + skr) / DK + EPS)
            xh1 = kn * rk
            xh2 = kr * rk
            dkh = dk_ref[hd] * LN2
            d1 = dkh[:, :128]
            d2 = _rope_t(dkh[:, 128:], cos, sin)
            dkng += jnp.concatenate([_colsum(d1 * xh1), _colsum(d2 * xh2)], axis=1)
            dx1 = d1 * kng[:, :128]
            dx2 = d2 * kng[:, 128:]
            dot = (_rowsum(dx1 * xh1) + _rowsum(dx2 * xh2)) / DK
            dkvs.append(jnp.concatenate([rk * (dx1 - xh1 * dot), dv_ref[hd]], axis=1).astype(BF16))
            dkr += rk * (dx2 - xh2 * dot)
        dkng_ref[...] += dkng

        dcqn = jnp.zeros((TB, QL), F32)
        dckvn = jnp.zeros((TB, KVL), F32)
        for hd in range(NH):
            dwuq_ref[hd] += _tn(dqraws[hd], cqn_b)[:DK]
            dcqn += _nn(dqraws[hd], wuq_ref[hd])
            dwukv_ref[hd] += _tn(ckvn_b, dkvs[hd])
            dckvn += _nt(dkvs[hd], wukv_ref[hd])
        dqlg_ref[...] += _colsum(dcqn * cqh)
        dxh = dcqn * qlg
        dcq = rc * (dxh - cqh * jnp.mean(dxh * cqh, axis=-1, keepdims=True))
        dkvlg_ref[...] += _colsum(dckvn * ckvh)
        dxh = dckvn * kvlg
        dckv = r0 * (dxh - ckvh * jnp.mean(dxh * ckvh, axis=-1, keepdims=True))
        dlo_ref[...] = jnp.concatenate([dcq, dckv, dkr], axis=1).astype(BF16)

    row = lambda w: pl.BlockSpec((TB, w), lambda i: (i, 0))
    heads = lambda w: pl.BlockSpec((NH, TB, w), lambda i: (0, i, 0))
    return pl.pallas_call(
        body, name="qkv_bwd", grid=(nb,),
        in_specs=[row(512), pl.BlockSpec((NH, TB, DKP), lambda i: (0, jnp.maximum(i - off, 0), 0)), heads(DKP), heads(DV),
                  pl.BlockSpec((1, 8, 256), lambda i: (jnp.maximum(i - off, 0), 0, 0)), _full((TB, 256)), _full((1, QL)), _full((NH, DKP, QL)), _full((1, KVL)), _full((NH, KVL, 256)),
                  _full((1, DK)), _full((1, DK))],
        out_specs=[row(512), _full((NH, DK, QL)), _full((NH, KVL, 256)), _full((1, QL)), _full((1, KVL)),
                   _full((1, DKP)), _full((1, DKP))],
        out_shape=[jax.ShapeDtypeStruct((t_all, 512), BF16), jax.ShapeDtypeStruct((NH, DK, QL), F32),
                   jax.ShapeDtypeStruct((NH, KVL, 256), F32), jax.ShapeDtypeStruct((1, QL), F32),
                   jax.ShapeDtypeStruct((1, KVL), F32), jax.ShapeDtypeStruct((1, DKP), F32), jax.ShapeDtypeStruct((1, DKP), F32)],
        scratch_shapes=[pltpu.VMEM((2, DKP), F32)],
        compiler_params=_params(("arbitrary",)),
    )(u, dq, dk, dv, cos, sin, q_lora_g, w_uq_t, kv_lora_g, w_ukv, qn_g, kn_g)


def _in_bwd(ctx, x, modsel, norm_g, dlo, dga, dgp, dpool, dxn, w_in_t):
    s_len, lc = x.shape[0], ctx.shape[0]
    t_all = s_len + lc
    off = lc // TB
    nb = t_all // TB
    nq = s_len // TB
    hb = TB // HALO
    n = TB + 2 * HALO

    def body(ctx_ref, x_ref, mod_ref, ng_ref, dlo_ref, dga_ref, dgp_ref, dp_ref, dpprev_ref, dpnext_ref, dxn_ref, win_ref,
             gx_ref, dwin_ref, dmod_ref, dng_ref):
        i = pl.program_id(0)
        j = i - off

        @pl.when(i == 0)
        def _():
            dwin_ref[...] = jnp.zeros_like(dwin_ref)
            dmod_ref[...] = jnp.zeros_like(dmod_ref)
            dng_ref[...] = jnp.zeros_like(dng_ref)

        latent = i >= off
        dp = dp_ref[...]
        prev = jnp.where(j <= 0, 0.0, dpprev_ref[...])
        nxt = jnp.where(j >= nq - 1, 0.0, dpnext_ref[...])
        win = jnp.concatenate([prev, dp, nxt], axis=0)
        tg = j * TB - HALO + lax.broadcasted_iota(jnp.int32, (n, 1), 0)
        dpin = []
        for g, w in enumerate(POOL_WINDOWS):
            cnt = jnp.maximum(jnp.minimum(tg + w // 2, s_len) - jnp.maximum(tg - w // 2, 0), 1).astype(F32)
            zq = win[:, g * 128:(g + 1) * 128] / cnt
            zq = zq + _shift_rows(zq, 1)
            for step in (1, 2, 4):
                if w >= 4 * step:
                    zq = _shift_rows(zq, -step) + _shift_rows(zq, step)
            dpin.append(zq[HALO:HALO + TB] - dp[:, g * 128:(g + 1) * 128])
        zero = jnp.zeros((TB, 512), BF16)
        du = [dlo_ref[...], jnp.where(latent, dga_ref[...], zero),
              jnp.where(latent, jnp.concatenate(dpin, axis=1).astype(BF16), zero), jnp.where(latent, dgp_ref[...], zero)]

        ng = ng_ref[...]
        xb = jnp.where(i < off, ctx_ref[...], x_ref[...])
        r, xh, xg, h, scale = _modulated(xb, mod_ref, ng)
        hb_ = h.astype(BF16)
        dh = jnp.zeros((TB, D), F32)
        for s, (lo, hi) in enumerate(SEG):
            dwin_ref[lo:hi, :] += _tn(du[s], hb_)
            dh += _nn(du[s], win_ref[lo:hi, :])
        is_lat = latent.astype(F32)
        dsh = _colsum(dh)
        dsc = _colsum(dh * xg)
        dmod_ref[0, 0:1, :] += dsh * (1.0 - is_lat)
        dmod_ref[0, 1:2, :] += dsc * (1.0 - is_lat)
        dmod_ref[1, 0:1, :] += dsh * is_lat
        dmod_ref[1, 1:2, :] += dsc * is_lat
        dxg = dh * (1.0 + scale)
        dng_ref[...] += _colsum(dxg * xh)
        dxh = dxg * ng
        gx_ref[...] = r * (dxh - xh * jnp.mean(dxh * xh, axis=-1, keepdims=True)) + dxn_ref[...]

    row = lambda w: pl.BlockSpec((TB, w), lambda i: (i, 0))
    lat = lambda w: pl.BlockSpec((TB, w), lambda i: (jnp.maximum(i - off, 0), 0))
    last8 = s_len // HALO - 1
    cspec, xspec, mspec = _token_specs(off)
    return pl.pallas_call(
        body, name="in_bwd", grid=(nb,),
        in_specs=[cspec, xspec, mspec, _full((1, D)), row(512), lat(512), lat(512), lat(512),
                  pl.BlockSpec((HALO, 512), lambda i: (jnp.maximum(jnp.maximum(i - off, 0) * hb - 1, 0), 0)),
                  pl.BlockSpec((HALO, 512), lambda i: (jnp.minimum((jnp.maximum(i - off, 0) + 1) * hb, last8), 0)),
                  lat(D), _full((DIN, D))],
        out_specs=[lat(D), _full((DIN, D)), _full((2, 2, D)), _full((1, D))],
        out_shape=[jax.ShapeDtypeStruct((s_len, D), F32), jax.ShapeDtypeStruct((DIN, D), F32),
                   jax.ShapeDtypeStruct((2, 2, D), F32), jax.ShapeDtypeStruct((1, D), F32)],
        compiler_params=_params(("arbitrary",)),
    )(ctx, x, modsel, norm_g, dlo, dga, dgp, dpool, dpool, dpool, dxn, w_in_t)


def _adamw_update(w_ref, g_ref, m_ref, v_ref, d_ref, mo_ref, vo_ref):
    gv = g_ref[...]
    mn = ADAM_B1 * m_ref[...] + (1.0 - ADAM_B1) * gv
    vn = ADAM_B2 * v_ref[...] + (1.0 - ADAM_B2) * (gv * gv)
    m_hat = mn / (1.0 - ADAM_B1 ** ADAM_STEP)
    v_hat = vn / (1.0 - ADAM_B2 ** ADAM_STEP)
    d_ref[...] = -ADAM_LR * (m_hat / (jnp.sqrt(v_hat) + ADAM_EPS) + ADAM_WD * w_ref[...])
    mo_ref[...] = mn
    vo_ref[...] = vn


def _adamw_many(ws, gs, ms, vs):
    n = len(ws)
    parts = 4

    def body(*refs):
        for i in range(n):
            _adamw_update(refs[i], refs[n + i], refs[2 * n + i], refs[3 * n + i], refs[4 * n + i], refs[5 * n + i], refs[6 * n + i])
            refs[7 * n + i][...] = refs[n + i][...]

    def spec(w):
        rows, cols = w.shape
        if rows % (8 * parts) == 0:
            return pl.BlockSpec((rows // parts, cols), lambda i: (i, 0))
        if cols % (128 * parts) == 0:
            return pl.BlockSpec((rows, cols // parts), lambda i: (0, i))
        return _full((rows, cols))

    specs = [spec(w) for w in ws]
    shp = [jax.ShapeDtypeStruct(w.shape, F32) for w in ws]
    out = pl.pallas_call(body, name="adamw_many", grid=(parts,), in_specs=specs * 4, out_specs=specs * 4, out_shape=shp * 4,
                         compiler_params=_params(("arbitrary",)))(*ws, *gs, *ms, *vs)
    return out[:n], out[n:2 * n], out[2 * n:3 * n], out[3 * n:]


LOCAL_DMA = 1


class _Links:
    def __init__(self, send_sems, recv_sems):
        self.send_sems, self.recv_sems, self.sends = send_sems, recv_sems, []

    def send(self, src, dst, sem, to):
        cp = pltpu.make_async_remote_copy(src, dst, self.send_sems.at[sem], self.recv_sems.at[sem], device_id=to,
                                          device_id_type=MESH)
        cp.start()
        self.sends.append(cp)

    def arrived(self, dst, sem, frm):
        pltpu.make_async_remote_copy(dst, dst, self.send_sems.at[sem], self.recv_sems.at[sem], device_id=frm,
                                     device_id_type=MESH).wait_recv()

    def drain(self):
        for cp in self.sends:
            cp.wait_send()


def _half(ref, c, axis):
    size = ref.shape[axis - 2] // 2
    win = pl.ds(pl.multiple_of(c * size, 16 if axis == 0 else 128), size)
    idx = (win, slice(None)) if axis == 0 else (slice(None), win)
    return ref.at[(slice(None),) * (len(ref.shape) - 2) + idx]


def _select_rows(slots_ref, n_slots, row=0):
    sub = lax.broadcasted_iota(jnp.int32, (8, 1), 0)
    out = None
    for d in range(n_slots):
        r = jnp.where(sub == d, jnp.broadcast_to(slots_ref[d][row:row + 1, :], (8, slots_ref.shape[-1])), 0.0)
        out = r if out is None else out + r
    return out


def _gather(c, c_ctx, w_mod, b_mod, shards, axes, slab_rows):
    nw = len(shards)
    kw = w_mod.shape[1]

    def body(*refs):
        c_ref, cc_ref, wm_hbm, b_ref = refs[:4]
        w_hbm = refs[4:4 + nw]
        a16_ref, modsel_ref = refs[4 + nw:6 + nw]
        out_refs = refs[6 + nw:6 + 2 * nw]
        g_refs = refs[6 + 2 * nw:6 + 3 * nw]
        f_refs = refs[6 + 3 * nw:6 + 4 * nw]
        wm_ref, a_ref, mod_ref, send_sems, recv_sems, local_sems = refs[6 + 4 * nw:]
        loads = [pltpu.make_async_copy(w_hbm[wi], f_refs[wi], local_sems.at[wi]) for wi in range(nw)]
        loads.append(pltpu.make_async_copy(wm_hbm, wm_ref, local_sems.at[nw]))
        for cp in loads:
            cp.start(priority=LOCAL_DMA)
        stores = []

        def slab(wi, chip, of=g_refs):
            return of[wi].at[chip].at[0:shards[wi].shape[0]]

        def store(src, dst, wi, slot):
            cp = pltpu.make_async_copy(src, dst, local_sems.at[nw + 1 + wi * 8 + slot])
            cp.start(priority=LOCAL_DMA)
            stores.append(cp)

        def store_half(wi, chip, half, slot):
            store(_half(slab(wi, chip), half, axes[wi]), _half(slab(wi, chip, out_refs), half, axes[wi]), wi, slot)
        x, y, cc = lax.axis_index("x"), lax.axis_index("y"), lax.axis_index("c")
        me = 4 * x + 2 * y + cc
        k = 2 * x + y
        sibling = (x, y, 1 - cc)
        links = _Links(send_sems, recv_sems)
        chips = [_peer(x, y, cc, off + (0,)) for off in CHIPS3]
        chip_a = ((x + 1 - cc) % 2, (y + cc) % 2, cc)
        chip_b = ((x + cc) % 2, (y + 1 - cc) % 2, cc)
        chip_d = (1 - x, 1 - y, cc)
        cv = c_ref[...]
        sc = cv * _sig(cv)
        mine = a_ref.at[me]
        for r in range(8):
            mine[r:r + 1, :] = sc[:, r * 128:(r + 1) * 128]
        for j, off in enumerate(PEERS7):
            links.send(a_ref.at[me], a_ref.at[me], j, _peer(x, y, cc, off))
        for wi in range(nw):
            loads[wi].wait()
            _half(slab(wi, k), cc, axes[wi])[...] = _half(f_refs[wi], cc, axes[wi])[...].astype(BF16)
            for j, to in enumerate((chip_a, chip_b)):
                links.send(_half(slab(wi, k), cc, axes[wi]), _half(slab(wi, k), cc, axes[wi]), 10 + wi * 6 + j, to)
            _half(slab(wi, k), 1 - cc, axes[wi])[...] = _half(f_refs[wi], 1 - cc, axes[wi])[...].astype(BF16)
            store(slab(wi, k), slab(wi, k, out_refs), wi, 0)
            rows = shards[wi].shape[0]
            pad = slab_rows[wi] - rows
            if pad:
                for kk in range(4):
                    g_refs[wi][kk, rows:, :] = jnp.zeros((pad, shards[wi].shape[1]), BF16)
                store(g_refs[wi].at[:, pl.ds(rows, pad), :], out_refs[wi].at[:, pl.ds(rows, pad), :], wi, 7)
        for j, off in enumerate(PEERS7):
            px, py, pc = _peer(x, y, cc, off)
            links.arrived(a_ref.at[4 * px + 2 * py + pc], j, (px, py, pc))
        ccv = cc_ref[...]
        sub = lax.broadcasted_iota(jnp.int32, (8, 1), 0)
        top = jnp.zeros((8, D), F32)
        for d in range(8):
            blk = a_ref[d]
            row = jnp.concatenate([blk[r:r + 1, :] for r in range(8)], axis=1)
            top = top + jnp.where(sub == d, jnp.broadcast_to(row, (8, D)), 0.0)
        a16 = jnp.concatenate([top, jnp.where(sub == 0, jnp.broadcast_to(ccv * _sig(ccv), (8, D)), 0.0)], axis=0)
        a16_ref[...] = a16
        loads[nw].wait()
        b_k = jnp.zeros((1, kw), F32)
        for kk in range(4):
            b_k = b_k + jnp.where(k == kk, b_ref[:, kk * kw:(kk + 1) * kw], 0.0)
        mod_ref[k] = _dot3(_nn, a16, wm_ref[...]) + b_k
        for j, to in enumerate(chips):
            links.send(mod_ref.at[k], mod_ref.at[k], 7 + j, to)

        def over_ici(j, frm, origin):
            for wi in range(nw):
                blk = _half(slab(wi, 2 * origin[0] + origin[1]), cc, axes[wi])
                links.arrived(blk, 10 + wi * 6 + j, frm)
                if j == 0:
                    links.send(blk, blk, 10 + wi * 6 + 2, chip_b)
                links.send(blk, blk, 10 + wi * 6 + 3 + j, sibling)
                store_half(wi, 2 * origin[0] + origin[1], cc, 1 + j)

        def from_sibling(j, origin):
            for wi in range(nw):
                links.arrived(_half(slab(wi, 2 * origin[0] + origin[1]), 1 - cc, axes[wi]), 10 + wi * 6 + 3 + j, sibling)
                store_half(wi, 2 * origin[0] + origin[1], 1 - cc, 4 + j)
        over_ici(0, chip_a, chip_a)
        over_ici(1, chip_b, chip_b)
        from_sibling(0, chip_b)
        over_ici(2, chip_b, chip_d)
        from_sibling(1, chip_a)
        from_sibling(2, chip_d)
        for j, (px, py, pc) in enumerate(chips):
            links.arrived(mod_ref.at[2 * px + py], 7 + j, (px, py, pc))
        for sel, row in ((0, 8), (1, me)):
            for kk in range(4):
                piece = mod_ref[kk, pl.ds(row, 1), :]
                lo = kk * kw
                while lo < (kk + 1) * kw:
                    r, col = divmod(lo, D)
                    n = min((kk + 1) * kw - lo, D - col)
                    modsel_ref[sel, r:r + 1, col:col + n] = piece[:, lo - kk * kw:lo - kk * kw + n]
                    lo += n
        links.drain()
        for cp in stores:
            cp.wait()

    nsem = 10 + 6 * nw
    gathered = [(4, r, s.shape[1]) for r, s in zip(slab_rows, shards)]
    return pl.pallas_call(
        body, name="gather", in_specs=[VM, VM, ANY, VM] + [ANY] * nw, out_specs=[VM, VM] + [ANY] * nw,
        out_shape=[jax.ShapeDtypeStruct((16, D), F32), jax.ShapeDtypeStruct((2, 3, D), F32)]
        + [jax.ShapeDtypeStruct(g, BF16) for g in gathered],
        scratch_shapes=[pltpu.VMEM(g, BF16) for g in gathered] + [pltpu.VMEM(s.shape, F32) for s in shards]
        + [pltpu.VMEM(w_mod.shape, F32), pltpu.VMEM((8, 8, D // 8), F32), pltpu.VMEM((4, 16, kw), F32),
           pltpu.SemaphoreType.DMA((nsem,)),
           pltpu.SemaphoreType.DMA((nsem,)), pltpu.SemaphoreType.DMA((nw + 1 + 8 * nw,))],
        compiler_params=pltpu.CompilerParams(vmem_limit_bytes=VMEM_LIMIT),
    )(c, c_ctx, w_mod, b_mod, *shards)


SMALL_ROW_WIDTHS = (D, QL, KVL, DKP, DKP, 512, 128)
SMALL_OUT_WIDTHS = (D, QL, KVL, DK, DK, 512, 1)
SMALL_PACK = 384


def _small_pieces():
    pieces = []
    for i, w in enumerate(SMALL_ROW_WIDTHS):
        for c0 in range(0, w, 128):
            j = len(pieces)
            pieces.append((i, c0, j // (SMALL_PACK // 128), j % (SMALL_PACK // 128) * 128))
    assert len(pieces) <= 8 * (SMALL_PACK // 128)
    return pieces


def _reduce(grads, axes, smalls, w_pool_g, dmod, dgate, a16, w_mod, c_ctx):
    nw = len(grads)
    ns = len(smalls)
    kw = w_mod.shape[1]
    halves = []
    for g, ax in zip(grads, axes):
        halves.append((g.shape[1] // 2, g.shape[2]) if ax == 0 else (g.shape[1], g.shape[2] // 2))

    def body(*refs):
        g_refs = refs[:nw]
        small_refs = refs[nw:nw + ns]
        wp_ref, dmod_ref, dgate_ref, a16_ref, wm_hbm, cc_ref = refs[nw + ns:nw + ns + 6]
        o = nw + ns + 6
        r_outs = refs[o:o + nw]
        small_outs = refs[o + nw:o + nw + ns]
        rwp_ref, gw_out, gb_ref, gc_ref = refs[o + nw + ns:o + nw + ns + 4]
        o = o + nw + ns + 4
        own, sib, part, got, rel, r_refs = (refs[o + i * nw:o + (i + 1) * nw] for i in range(6))
        smbuf, wps, wpg, dm_all, pc_all, wm_ref, gw_ref, send_sems, recv_sems, local_sems = refs[o + 6 * nw:]
        wm_load = pltpu.make_async_copy(wm_hbm, wm_ref, local_sems.at[nw])
        wm_load.start(priority=LOCAL_DMA)
        stores = []

        def store(src, dst, sem):
            cp = pltpu.make_async_copy(src, dst, local_sems.at[nw + 1 + sem])
            cp.start(priority=LOCAL_DMA)
            stores.append(cp)
        x, y, cc = lax.axis_index("x"), lax.axis_index("y"), lax.axis_index("c")
        me = 4 * x + 2 * y + cc
        k = 2 * x + y
        sibling = (x, y, 1 - cc)
        links = _Links(send_sems, recv_sems)
        chips = [_peer(x, y, cc, off + (0,)) for off in CHIPS3]
        peers = [_peer(x, y, cc, off) for off in PEERS7]
        chip_a = ((x + 1 - cc) % 2, (y + cc) % 2, cc)
        chip_b = ((x + cc) % 2, (y + 1 - cc) % 2, cc)
        ka, kb, kd = 2 * chip_a[0] + chip_a[1], 2 * chip_b[0] + chip_b[1], 2 * (1 - x) + (1 - y)
        big, sm0, wp0, dm0, pc0 = 0, 5 * nw, 5 * nw + 7, 5 * nw + 14, 5 * nw + 21

        locals_ = []
        for wi in range(nw):
            lc = pltpu.make_async_copy(_half(g_refs[wi], cc, axes[wi]), own[wi], local_sems.at[wi])
            lc.start(priority=LOCAL_DMA)
            locals_.append(lc)
            links.send(_half(g_refs[wi], 1 - cc, axes[wi]), sib[wi], big + wi * 5, sibling)
        slot = smbuf.at[me]
        slot[...] = jnp.zeros((8, SMALL_PACK), F32)
        small_rows = [jnp.broadcast_to(ref[...], (1, w)) for ref, w in zip(small_refs, SMALL_ROW_WIDTHS)]
        for i, c0, row, lane in _small_pieces():
            slot[row:row + 1, lane:lane + 128] = small_rows[i][:, c0:c0 + 128]
        links.send(wp_ref, wps, wp0, sibling)
        dm_mine = dm_all.at[me]
        dm_mine[...] = jnp.zeros((8, kw), F32)
        dm_rows = ((0, (dmod_ref[1, 0:1, :], dmod_ref[1, 1:2, :], dgate_ref[...])), (4, (dmod_ref[0, 0:1, :], dmod_ref[0, 1:2, :])))
        for base, rows in dm_rows:
            for r, v in enumerate(rows):
                lo = r * D
                while lo < (r + 1) * D:
                    kk, col = divmod(lo, kw)
                    n = min((r + 1) * D - lo, kw - col)
                    dm_mine[base + kk:base + kk + 1, col:col + n] = v[:, lo - r * D:lo - r * D + n]
                    lo += n
        for j, peer in enumerate(peers):
            links.send(dm_all.at[me], dm_all.at[me], dm0 + j, peer)
            links.send(smbuf.at[me], smbuf.at[me], sm0 + j, peer)
        links.arrived(wps, wp0, sibling)
        wpg[k] = (wp_ref[...] + wps[...]).astype(BF16)
        for j, to in enumerate(chips):
            links.send(wpg.at[k], wpg.at[k], wp0 + 1 + j, to)
        for wi in range(nw):
            locals_[wi].wait()
            links.arrived(sib[wi], big + wi * 5, sibling)
            for slab_k in (kd, kb):
                part[wi][slab_k] = (own[wi][slab_k] + sib[wi][slab_k]).astype(BF16)
            links.send(part[wi].at[kd], rel[wi], big + wi * 5 + 1, chip_b)
            links.send(part[wi].at[kb], got[wi].at[k], big + wi * 5 + 2, chip_b)
            for slab_k in (ka, k):
                part[wi][slab_k] = (own[wi][slab_k] + sib[wi][slab_k]).astype(BF16)
            got[wi][k] = part[wi][k]
            got[wi][kd] = jnp.zeros(halves[wi], BF16)
        for j, (px, py, pc) in enumerate(peers):
            links.arrived(dm_all.at[4 * px + 2 * py + pc], dm0 + j, (px, py, pc))
        dm_tot = dm_all[0]
        for d in range(1, 8):
            dm_tot = dm_tot + dm_all[d]
        for kk in range(4):
            gb_ref[:, kk * kw:(kk + 1) * kw] = dm_tot[kk:kk + 1, :] + dm_tot[4 + kk:5 + kk, :]
        top = jnp.zeros((8, kw), F32)
        dmc_k = jnp.zeros((1, kw), F32)
        for kk in range(4):
            top = top + jnp.where(k == kk, _select_rows(dm_all, 8, kk), 0.0)
            dmc_k = dmc_k + jnp.where(k == kk, dm_tot[4 + kk:5 + kk, :], 0.0)
        sub = lax.broadcasted_iota(jnp.int32, (8, 1), 0)
        bk = jnp.concatenate([top, jnp.where(sub == 0, jnp.broadcast_to(dmc_k, (8, kw)), 0.0)], axis=0)
        gw_ref[...] = _dot3(_tn, a16_ref[...], bk)
        store(gw_ref, gw_out, 0)
        wm_load.wait()
        pc_all[k] =_dot3(_nt, jnp.broadcast_to(bk[8:9, :], (8, kw)), wm_ref[...])
        for j, to in enumerate(chips):
            links.send(pc_all.at[k], pc_all.at[k], pc0 + j, to)
        for j, (px, py, pc) in enumerate(peers):
            links.arrived(smbuf.at[4 * px + 2 * py + pc], sm0 + j, (px, py, pc))
        tot = smbuf[0]
        for d in range(1, 8):
            tot = tot + smbuf[d]
        for i, c0, row, lane in _small_pieces():
            n = min(128, SMALL_OUT_WIDTHS[i] - c0)
            if n > 0:
                small_outs[i][:, c0:c0 + n] = tot[row:row + 1, lane:lane + n]
        for j, (px, py, pc) in enumerate(chips):
            links.arrived(wpg.at[2 * px + py], wp0 + 1 + j, (px, py, pc))
        wpt = wpg[0].astype(F32)
        for kk in range(1, 4):
            wpt = wpt + wpg[kk].astype(F32)
        rwp_ref[...] = wpt
        for wi in range(nw):
            links.arrived(rel[wi], big + wi * 5 + 1, chip_b)
            rel[wi][...] = (part[wi][ka].astype(F32) + rel[wi][...].astype(F32)).astype(BF16)
            links.send(rel[wi], got[wi].at[k], big + wi * 5 + 3, chip_a)
        for wi in range(nw):
            links.arrived(got[wi].at[kb], big + wi * 5 + 2, chip_b)
            links.arrived(got[wi].at[ka], big + wi * 5 + 3, chip_a)
            total = got[wi][0].astype(F32)
            for kk in range(1, 4):
                total = total + got[wi][kk].astype(F32)
            mine = _half(r_refs[wi], cc, axes[wi])
            mine[...] = total
            links.send(mine, mine, big + wi * 5 + 4, sibling)
            store(mine, _half(r_outs[wi], cc, axes[wi]), 1 + wi)
        for j, (px, py, pc) in enumerate(chips):
            links.arrived(pc_all.at[2 * px + py], pc0 + j, (px, py, pc))
        ccv = cc_ref[...]
        sg = _sig(ccv)
        gc_ref[...] = (pc_all[0][0:1, :] + pc_all[1][0:1, :] + pc_all[2][0:1, :] + pc_all[3][0:1, :]) * (sg * (1.0 + ccv * (1.0 - sg)))
        for wi in range(nw):
            links.arrived(_half(r_refs[wi], 1 - cc, axes[wi]), big + wi * 5 + 4, sibling)
            store(_half(r_refs[wi], 1 - cc, axes[wi]), _half(r_outs[wi], 1 - cc, axes[wi]), 1 + nw + wi)
        links.drain()
        for cp in stores:
            cp.wait()

    nsem = 5 * nw + 24
    quads = [(4,) + h for h in halves]
    return pl.pallas_call(
        body, name="reduce", in_specs=[ANY] * nw + [VM] * (ns + 4) + [ANY, VM],
        out_specs=[ANY] * nw + [VM] * (ns + 1) + [ANY, VM, VM],
        out_shape=[jax.ShapeDtypeStruct(g.shape[1:], F32) for g in grads]
        + [jax.ShapeDtypeStruct((1, w), F32) for w in SMALL_OUT_WIDTHS]
        + [jax.ShapeDtypeStruct(w_pool_g.shape, F32), jax.ShapeDtypeStruct((D, kw), F32), jax.ShapeDtypeStruct((1, 3 * D), F32),
           jax.ShapeDtypeStruct((1, D), F32)],
        scratch_shapes=[pltpu.VMEM(q, F32) for q in quads] + [pltpu.VMEM(q, F32) for q in quads]
        + [pltpu.VMEM(q, BF16) for q in quads] + [pltpu.VMEM(q, BF16) for q in quads] + [pltpu.VMEM(h, BF16) for h in halves]
        + [pltpu.VMEM(g.shape[1:], F32) for g in grads]
        + [pltpu.VMEM((8, 8, SMALL_PACK), F32), pltpu.VMEM(w_pool_g.shape, F32), pltpu.VMEM((4,) + w_pool_g.shape, BF16),
           pltpu.VMEM((8, 8, kw), F32), pltpu.VMEM((4, 8, D), F32), pltpu.VMEM((D, kw), F32), pltpu.VMEM((D, kw), F32)]
        + [pltpu.SemaphoreType.DMA((nsem,)), pltpu.SemaphoreType.DMA((nsem,)), pltpu.SemaphoreType.DMA((3 * nw + 2,))],
        compiler_params=pltpu.CompilerParams(vmem_limit_bytes=VMEM_LIMIT),
    )(*grads, *smalls, w_pool_g, dmod, dgate, a16, w_mod, c_ctx)


def _rope_tables(s_len):
    rows = s_len // GRID_W
    per = TB // GRID_W
    n_freq = 16
    f32 = np.float32
    inv = f32(ROPE_BASE) ** (-np.arange(n_freq, dtype=f32) / f32(n_freq))
    ang_r = np.arange(rows, dtype=f32)[:, None] * inv
    ang_c = np.arange(GRID_W, dtype=f32)[:, None] * inv
    by_row, by_col = [], []
    for fn, pad in ((np.cos, 1.0), (np.sin, 0.0)):
        r = np.concatenate([fn(ang_r), fn(ang_r), np.zeros((rows, 96), f32)], axis=1).reshape(rows // per, per, 128)
        by_row.append(np.pad(r, ((0, 0), (0, 8 - per), (0, 0))))
        cpart = np.concatenate([np.zeros((GRID_W, 32), f32), fn(ang_c), fn(ang_c), np.full((GRID_W, 64), pad, f32)], axis=1)
        by_col.append(np.tile(cpart, (per, 1)))
    return jnp.asarray(np.concatenate(by_row, axis=-1), F32), jnp.asarray(np.concatenate(by_col, axis=-1), F32)


def kernel(x, c, ctx, c_ctx, w_mod, b_mod, norm_g, w_in, q_lora_g, w_uq, kv_lora_g, w_ukv, q_norm_g, k_norm_g, w_pool, pool_scale, w_out, loss_target, m_c_ctx, m_w_mod, m_b_mod, m_norm_g, m_w_in, m_q_lora_g, m_w_uq, m_kv_lora_g, m_w_ukv, m_q_norm_g, m_k_norm_g, m_w_pool, m_pool_scale, m_w_out, v_c_ctx, v_w_mod, v_b_mod, v_norm_g, v_w_in, v_q_lora_g, v_w_uq, v_kv_lora_g, v_w_ukv, v_q_norm_g, v_k_norm_g, v_w_pool, v_pool_scale, v_w_out):
    s_len = x.shape[1]
    lc = ctx.shape[1]
    weights = dict(c_ctx=c_ctx, w_mod=w_mod, b_mod=b_mod, norm_g=norm_g, w_in=w_in, q_lora_g=q_lora_g, w_uq=w_uq,
                   kv_lora_g=kv_lora_g, w_ukv=w_ukv, q_norm_g=q_norm_g, k_norm_g=k_norm_g, w_pool=w_pool,
                   pool_scale=pool_scale, w_out=w_out)
    m_in = dict(c_ctx=m_c_ctx, w_mod=m_w_mod, b_mod=m_b_mod, norm_g=m_norm_g, w_in=m_w_in, q_lora_g=m_q_lora_g, w_uq=m_w_uq,
                kv_lora_g=m_kv_lora_g, w_ukv=m_w_ukv, q_norm_g=m_q_norm_g, k_norm_g=m_k_norm_g, w_pool=m_w_pool,
                pool_scale=m_pool_scale, w_out=m_w_out)
    v_in = dict(c_ctx=v_c_ctx, w_mod=v_w_mod, b_mod=v_b_mod, norm_g=v_norm_g, w_in=v_w_in, q_lora_g=v_q_lora_g, w_uq=v_w_uq,
                kv_lora_g=v_kv_lora_g, w_ukv=v_w_ukv, q_norm_g=v_q_norm_g, k_norm_g=v_k_norm_g, w_pool=v_w_pool,
                pool_scale=v_pool_scale, w_out=v_w_out)
    order = ["c_ctx", "w_mod", "b_mod", "norm_g", "w_in", "q_lora_g", "w_uq", "kv_lora_g", "w_ukv", "q_norm_g", "k_norm_g",
             "w_pool", "pool_scale", "w_out"]
    transposed = ("w_in", "w_uq")
    as2d = lambda n, a: jnp.transpose(a[0]) if n in transposed else a.reshape(-1, a.shape[-1])
    back = lambda n, a: jnp.transpose(a)[None] if n in transposed else a.reshape(weights[n].shape)

    c_ctx2 = c_ctx.reshape(1, D)
    split = (0, 1, 0, 0)
    a16, modsel, g_out, g_in, g_uq, g_ukv = _gather(
        c, c_ctx2, w_mod[0], b_mod, [w_out[0], as2d("w_in", w_in), as2d("w_uq", w_uq), w_ukv[0]], split,
        (D // 4, DIN // 4, DKP, KVL))
    w_in_t = g_in.reshape(DIN, D)
    w_uq_t = g_uq
    w_out_f = g_out.reshape(D, D)
    cos, sin = _rope_tables(s_len)

    u, q, kk, v = _fwd_in(ctx[0], x[0], modsel, norm_g, w_in_t, q_lora_g, w_uq_t, kv_lora_g, g_ukv, q_norm_g, k_norm_g, cos, sin)
    attn, lse = _attn_fwd(q, kk, v, s_len)
    (dxn, dattn, dga, dgp, dpool, dw_out, dgate, dps, dw_pool, loss) = _out_stage(
        attn.reshape(s_len // Q_BLOCK, Q_BLOCK, NH * DV), u, x[0], loss_target[0], modsel, w_pool[0], pool_scale,
        w_out_f, lc)
    dattn = dattn.reshape(s_len, NH * DV)
    dq, dk, dv = _attn_bwd(q, kk, v, dattn, attn, lse, s_len)
    dlo, dw_uq_t, dw_ukv, dqlg, dkvlg, dqng, dkng = _qkv_bwd(u, dq, dk, dv, cos, sin, q_lora_g, w_uq_t, kv_lora_g, g_ukv,
                                                            q_norm_g, k_norm_g, s_len)
    gx, dw_in_t, dmod, dng = _in_bwd(ctx[0], x[0], modsel, norm_g, dlo, dga, dgp, dpool, dxn, w_in_t)

    r_out, r_in, r_uq, r_ukv, g_ng, g_qlg, g_kvlg, g_qng, g_kng, g_ps, loss_all, g_wp, g_w_mod, g_b_mod, g_c_ctx = _reduce(
        [dw_out.reshape(4, D // 4, D), dw_in_t.reshape(4, DIN // 4, D), dw_uq_t, dw_ukv], split,
        [dng, dqlg, dkvlg, dqng, dkng, dps, loss], dw_pool, dmod, dgate, a16, w_mod[0], c_ctx2)
    g2d = dict(c_ctx=g_c_ctx, b_mod=g_b_mod, w_mod=g_w_mod, w_in=r_in, w_uq=r_uq, w_ukv=r_ukv, w_out=r_out, norm_g=g_ng,
               q_lora_g=g_qlg, kv_lora_g=g_kvlg, q_norm_g=g_qng, k_norm_g=g_kng, pool_scale=g_ps, w_pool=g_wp.reshape(512, 128))

    outs = _adamw_many([as2d(n, weights[n]) for n in order], [g2d[n] for n in order], [as2d(n, m_in[n]) for n in order],
                       [as2d(n, v_in[n]) for n in order])
    d2d, m2d, v2d, g2d = (dict(zip(order, arrs)) for arrs in outs)

    return (loss_all[0, 0], gx[None], *[back(n, g2d[n]) for n in order], *[back(n, d2d[n]) for n in order],
            *[back(n, m2d[n]) for n in order], *[back(n, v2d[n]) for n in order])
```
